```python
import jax, jax.numpy as jnp
from jax import lax
import numpy as np

D_MODEL = 1024
BATCH = 8
SEQ = 8192
DEPTH = 1

N_META = 16
RET_HEADS = 4
RET_QK_DIM = D_MODEL // RET_HEADS
RET_WIDTH = 2 * D_MODEL
RET_V_DIM = RET_WIDTH // RET_HEADS
RET_CHUNK = 64
GLA_HEADS = 4
GLA_K_DIM = (D_MODEL // 2) // GLA_HEADS
GLA_WIDTH = D_MODEL
GLA_V_DIM = GLA_WIDTH // GLA_HEADS
GLA_GATE_RANK = 16
GLA_GATE_TAU = 16.0
GLA_CHUNK = 16
ROPE_BASE = 10000.0
EPS = 1e-6
IN_SIZES = (RET_HEADS * RET_QK_DIM, RET_HEADS * RET_QK_DIM, RET_WIDTH, RET_WIDTH,
            GLA_HEADS * GLA_K_DIM, GLA_HEADS * GLA_K_DIM, GLA_WIDTH, GLA_WIDTH,
            GLA_GATE_RANK, D_MODEL, D_MODEL)
IN_COLS = sum(IN_SIZES)

kernel_name = "hybrid_retention_gla_gated_merge"


def rms_norm(x, gain):
    xf = x.astype(jnp.float32)
    y = xf * lax.rsqrt(jnp.mean(xf * xf, axis=-1, keepdims=True) + EPS) * gain.astype(jnp.float32)
    return y.astype(x.dtype)


def head_group_norm(o, gain):
    of = o.astype(jnp.float32)
    mu = jnp.mean(of, axis=-1, keepdims=True)
    var = jnp.mean(jnp.square(of - mu), axis=-1, keepdims=True)
    return ((of - mu) * lax.rsqrt(var + EPS) * gain.astype(jnp.float32)).astype(o.dtype)


def head_rms_norm(o, gain):
    of = o.astype(jnp.float32)
    return (of * lax.rsqrt(jnp.mean(of * of, axis=-1, keepdims=True) + EPS) * gain.astype(jnp.float32)).astype(o.dtype)


def rope(t, pos):
    half = t.shape[-1] // 2
    inv = ROPE_BASE ** (-jnp.arange(half, dtype=jnp.float32) / half)
    ang = pos[:, None] * inv[None, :]
    cos = jnp.cos(ang)[None, :, None, :]
    sin = jnp.sin(ang)[None, :, None, :]
    t1 = t[..., :half].astype(jnp.float32)
    t2 = t[..., half:].astype(jnp.float32)
    return jnp.concatenate([t1 * cos - t2 * sin, t2 * cos + t1 * sin], axis=-1).astype(t.dtype)


def to_chunks(t, c):
    pad = (-N_META) % c
    t = jnp.pad(t, ((0, 0), (pad, 0), (0, 0), (0, 0)))
    b, lp, h, d = t.shape
    return t.reshape(b, lp // c, c, h, d)


def from_chunks(t, c):
    b, n, _, h, d = t.shape
    pad = (-N_META) % c
    return t.reshape(b, n * c, h, d)[:, pad:]


def retention_chunked(q, k, v):
    c = RET_CHUNK
    qc, kc, vc = to_chunks(q, c), to_chunks(k, c), to_chunks(v, c)
    log_gamma = jnp.log1p(-(2.0 ** (-5.0 - jnp.arange(RET_HEADS, dtype=jnp.float32))))
    idx = jnp.arange(c, dtype=jnp.float32)
    rel = idx[:, None] - idx[None, :]
    decay = jnp.where(rel[None] >= 0, jnp.exp(jnp.maximum(rel, 0.0)[None] * log_gamma[:, None, None]), 0.0)
    scores = jnp.einsum('bnihd,bnjhd->bnhij', qc, kc) * decay[None, None]
    intra = jnp.einsum('bnhij,bnjhe->bnihe', scores, vc)
    xi = jnp.exp((idx[:, None] + 1.0) * log_gamma[None, :])
    zeta = jnp.exp((c - 1.0 - idx[:, None]) * log_gamma[None, :])
    gamma_c = jnp.exp(c * log_gamma)

    def step(state, xs):
        q_n, k_n, v_n = xs
        inter = jnp.einsum('bihd,bhde->bihe', q_n, state) * xi[None, :, :, None]
        state = state * gamma_c[None, :, None, None] + jnp.einsum('bjhd,bjhe->bhde', k_n * zeta[None, :, :, None], v_n)
        return state, inter

    bsz = q.shape[0]
    state0 = jnp.zeros((bsz, RET_HEADS, RET_QK_DIM, RET_V_DIM), jnp.float32)
    xs = (jnp.moveaxis(qc, 1, 0), jnp.moveaxis(kc, 1, 0), jnp.moveaxis(vc, 1, 0))
    _, inter = lax.scan(step, state0, xs)
    out = intra + jnp.moveaxis(inter, 0, 1)
    return from_chunks(out, c).astype(v.dtype)


def gla_chunked(q, k, v, log_a):
    c = GLA_CHUNK
    qc, kc, vc = to_chunks(q, c), to_chunks(k, c), to_chunks(v, c)
    ac = to_chunks(log_a.astype(jnp.float32), c)
    b = jnp.cumsum(ac, axis=2)
    b_last = b[:, :, -1]
    q_dec = qc * jnp.exp(b)
    k_inv = kc * jnp.exp(-b)
    k_end = kc * jnp.exp(b_last[:, :, None] - b)
    mask = jnp.tril(jnp.ones((c, c), dtype=bool))
    scores = jnp.where(mask, jnp.einsum('bnihd,bnjhd->bnhij', q_dec, k_inv), 0.0)
    intra = jnp.einsum('bnhij,bnjhe->bnihe', scores, vc)

    def step(state, xs):
        q_n, k_n, v_n, a_n = xs
        inter = jnp.einsum('bihd,bhde->bihe', q_n, state)
        state = state * jnp.exp(a_n)[..., None] + jnp.einsum('bjhd,bjhe->bhde', k_n, v_n)
        return state, inter

    bsz = q.shape[0]
    state0 = jnp.zeros((bsz, GLA_HEADS, GLA_K_DIM, GLA_V_DIM), jnp.float32)
    xs = (jnp.moveaxis(q_dec, 1, 0), jnp.moveaxis(k_end, 1, 0), jnp.moveaxis(vc, 1, 0), jnp.moveaxis(b_last, 1, 0))
    _, inter = lax.scan(step, state0, xs)
    out = intra + jnp.moveaxis(inter, 0, 1)
    return from_chunks(out, c).astype(v.dtype)


def hybrid_layer(h, norm_gain, w_in, w_gate_up, b_gate, ret_norm_gain, gla_norm_gain,
                 w_branch_ret, w_branch_gla, w_out):
    bsz, length, _ = h.shape
    u = rms_norm(h, norm_gain)
    proj = u @ w_in
    points = [int(p) for p in np.cumsum(IN_SIZES)[:-1]]
    (rq, rk, rv, rg, gq, gk, gv, gg, glr, m_ret, m_gla) = jnp.split(proj, points, axis=-1)

    pos = jnp.arange(length, dtype=jnp.float32)
    rq = rope(rq.reshape(bsz, length, RET_HEADS, RET_QK_DIM), pos)
    rk = rope(rk.reshape(bsz, length, RET_HEADS, RET_QK_DIM), pos) * (RET_QK_DIM ** -0.5)
    rv = rv.reshape(bsz, length, RET_HEADS, RET_V_DIM)
    o_ret = retention_chunked(rq, rk, rv)
    o_ret = head_group_norm(o_ret, ret_norm_gain.reshape(RET_HEADS, RET_V_DIM)).reshape(bsz, length, RET_WIDTH)
    o_ret = o_ret * jax.nn.silu(rg)

    gq = gq.reshape(bsz, length, GLA_HEADS, GLA_K_DIM) * (GLA_K_DIM ** -0.5)
    gk = gk.reshape(bsz, length, GLA_HEADS, GLA_K_DIM)
    gv = gv.reshape(bsz, length, GLA_HEADS, GLA_V_DIM)
    log_a = jax.nn.log_sigmoid((glr @ w_gate_up + b_gate).astype(jnp.float32)) / GLA_GATE_TAU
    log_a = log_a.reshape(bsz, length, GLA_HEADS, GLA_K_DIM)
    o_gla = gla_chunked(gq, gk, gv, log_a)
    o_gla = head_rms_norm(o_gla, gla_norm_gain.reshape(GLA_HEADS, GLA_V_DIM)).reshape(bsz, length, GLA_WIDTH)
    o_gla = o_gla * jax.nn.silu(gg)

    merged = jax.nn.sigmoid(m_ret) * (o_ret @ w_branch_ret) + jax.nn.sigmoid(m_gla) * (o_gla @ w_branch_gla)
    return h + merged @ w_out


def _fwd_setup_inputs(seed: int = 0) -> dict:
    key = jax.random.key(seed)
    ks = jax.random.split(key, 12)
    f = jnp.float32
    gk_dim = GLA_HEADS * GLA_K_DIM
    return {
        "x": jax.random.normal(ks[0], (BATCH, SEQ, D_MODEL), f),
        "meta_tokens": jax.random.normal(ks[1], (N_META, D_MODEL), f),
        "norm_gain": 1.0 + 0.02 * jax.random.normal(ks[2], (DEPTH, D_MODEL), f),
        "w_in": jax.random.normal(ks[3], (DEPTH, D_MODEL, IN_COLS), f) * D_MODEL ** -0.5,
        "w_gate_up": jax.random.normal(ks[4], (DEPTH, GLA_GATE_RANK, gk_dim), f) * GLA_GATE_RANK ** -0.5,
        "b_gate": 0.01 * jax.random.normal(ks[5], (DEPTH, gk_dim), f),
        "ret_norm_gain": 1.0 + 0.02 * jax.random.normal(ks[6], (DEPTH, RET_WIDTH), f),
        "gla_norm_gain": 1.0 + 0.02 * jax.random.normal(ks[7], (DEPTH, GLA_WIDTH), f),
        "w_branch_ret": jax.random.normal(ks[8], (DEPTH, RET_WIDTH, D_MODEL), f) * RET_WIDTH ** -0.5,
        "w_branch_gla": jax.random.normal(ks[9], (DEPTH, GLA_WIDTH, D_MODEL), f) * GLA_WIDTH ** -0.5,
        "w_out": jax.random.normal(ks[10], (DEPTH, D_MODEL, D_MODEL), f) * D_MODEL ** -0.5,
        "final_norm_gain": 1.0 + 0.02 * jax.random.normal(ks[11], (D_MODEL,), f),
    }


def _fwd_reference(x, meta_tokens, norm_gain, w_in, w_gate_up, b_gate, ret_norm_gain, gla_norm_gain,
              w_branch_ret, w_branch_gla, w_out, final_norm_gain):
    bsz = x.shape[0]
    meta = jnp.broadcast_to(meta_tokens.astype(x.dtype)[None], (bsz, N_META, D_MODEL))
    h = jnp.concatenate([meta, x], axis=1)
    for layer in range(DEPTH):
        h = hybrid_layer(h, norm_gain[layer], w_in[layer], w_gate_up[layer], b_gate[layer],
                         ret_norm_gain[layer], gla_norm_gain[layer], w_branch_ret[layer],
                         w_branch_gla[layer], w_out[layer])
    h = rms_norm(h, final_norm_gain)
    return h[:, N_META:]


import jax as _jax
import jax.numpy as _jnp

TWIN_FORMAT = 'train_step'
FWD_PARAMS = ['x', 'meta_tokens', 'norm_gain', 'w_in', 'w_gate_up', 'b_gate', 'ret_norm_gain', 'gla_norm_gain', 'w_branch_ret', 'w_branch_gla', 'w_out', 'final_norm_gain']
TWIN_WEIGHTS = ['meta_tokens', 'norm_gain', 'w_in', 'w_gate_up', 'b_gate', 'ret_norm_gain', 'gla_norm_gain', 'w_branch_ret', 'w_branch_gla', 'w_out', 'final_norm_gain']
TWIN_DIFF_INPUT = 'x'
TWIN_INPUTS = ['x', 'meta_tokens', 'norm_gain', 'w_in', 'w_gate_up', 'b_gate', 'ret_norm_gain', 'gla_norm_gain', 'w_branch_ret', 'w_branch_gla', 'w_out', 'final_norm_gain', 'loss_target', 'm_meta_tokens', 'm_norm_gain', 'm_w_in', 'm_w_gate_up', 'm_b_gate', 'm_ret_norm_gain', 'm_gla_norm_gain', 'm_w_branch_ret', 'm_w_branch_gla', 'm_w_out', 'm_final_norm_gain', 'v_meta_tokens', 'v_norm_gain', 'v_w_in', 'v_w_gate_up', 'v_b_gate', 'v_ret_norm_gain', 'v_gla_norm_gain', 'v_w_branch_ret', 'v_w_branch_gla', 'v_w_out', 'v_final_norm_gain']
TWIN_OUTPUTS = ['loss', 'grad_x', 'grad_meta_tokens', 'grad_norm_gain', 'grad_w_in', 'grad_w_gate_up', 'grad_b_gate', 'grad_ret_norm_gain', 'grad_gla_norm_gain', 'grad_w_branch_ret', 'grad_w_branch_gla', 'grad_w_out', 'grad_final_norm_gain', 'delta_meta_tokens', 'delta_norm_gain', 'delta_w_in', 'delta_w_gate_up', 'delta_b_gate', 'delta_ret_norm_gain', 'delta_gla_norm_gain', 'delta_w_branch_ret', 'delta_w_branch_gla', 'delta_w_out', 'delta_final_norm_gain', 'new_m_meta_tokens', 'new_m_norm_gain', 'new_m_w_in', 'new_m_w_gate_up', 'new_m_b_gate', 'new_m_ret_norm_gain', 'new_m_gla_norm_gain', 'new_m_w_branch_ret', 'new_m_w_branch_gla', 'new_m_w_out', 'new_m_final_norm_gain', 'new_v_meta_tokens', 'new_v_norm_gain', 'new_v_w_in', 'new_v_w_gate_up', 'new_v_b_gate', 'new_v_ret_norm_gain', 'new_v_gla_norm_gain', 'new_v_w_branch_ret', 'new_v_w_branch_gla', 'new_v_w_out', 'new_v_final_norm_gain']
TWIN_LEAF_KINDS = {'loss': 'loss', 'grad_x': 'grad_x', 'grad_meta_tokens': 'grad_w', 'grad_norm_gain': 'grad_w', 'grad_w_in': 'grad_w', 'grad_w_gate_up': 'grad_w', 'grad_b_gate': 'grad_w', 'grad_ret_norm_gain': 'grad_w', 'grad_gla_norm_gain': 'grad_w', 'grad_w_branch_ret': 'grad_w', 'grad_w_branch_gla': 'grad_w', 'grad_w_out': 'grad_w', 'grad_final_norm_gain': 'grad_w', 'delta_meta_tokens': 'delta_w', 'delta_norm_gain': 'delta_w', 'delta_w_in': 'delta_w', 'delta_w_gate_up': 'delta_w', 'delta_b_gate': 'delta_w', 'delta_ret_norm_gain': 'delta_w', 'delta_gla_norm_gain': 'delta_w', 'delta_w_branch_ret': 'delta_w', 'delta_w_branch_gla': 'delta_w', 'delta_w_out': 'delta_w', 'delta_final_norm_gain': 'delta_w', 'new_m_meta_tokens': 'new_m', 'new_m_norm_gain': 'new_m', 'new_m_w_in': 'new_m', 'new_m_w_gate_up': 'new_m', 'new_m_b_gate': 'new_m', 'new_m_ret_norm_gain': 'new_m', 'new_m_gla_norm_gain': 'new_m', 'new_m_w_branch_ret': 'new_m', 'new_m_w_branch_gla': 'new_m', 'new_m_w_out': 'new_m', 'new_m_final_norm_gain': 'new_m', 'new_v_meta_tokens': 'new_v', 'new_v_norm_gain': 'new_v', 'new_v_w_in': 'new_v', 'new_v_w_gate_up': 'new_v', 'new_v_b_gate': 'new_v', 'new_v_ret_norm_gain': 'new_v', 'new_v_gla_norm_gain': 'new_v', 'new_v_w_branch_ret': 'new_v', 'new_v_w_branch_gla': 'new_v', 'new_v_w_out': 'new_v', 'new_v_final_norm_gain': 'new_v'}


def _forward(args):
    return _fwd_reference(*[args[k] for k in FWD_PARAMS])


def _output_shape():
    out = _jax.eval_shape(lambda: _forward(_fwd_setup_inputs(0)))
    return out.shape, out.dtype

N_MICROBATCH = 1
ADAM_LR = 0.001
ADAM_B1 = 0.9
ADAM_B2 = 0.999
ADAM_EPS = 1e-08
ADAM_WD = 0.01
ADAM_STEP = 10
PER_EXAMPLE_BATCH_AXIS = {'x': 0, 'loss_target': 0}
SHARED_INPUTS = []
_WEIGHT_DTYPES = {'meta_tokens': _jnp.float32, 'norm_gain': _jnp.float32, 'w_in': _jnp.float32, 'w_gate_up': _jnp.float32, 'b_gate': _jnp.float32, 'ret_norm_gain': _jnp.float32, 'gla_norm_gain': _jnp.float32, 'w_branch_ret': _jnp.float32, 'w_branch_gla': _jnp.float32, 'w_out': _jnp.float32, 'final_norm_gain': _jnp.float32}
MOMENT_SCALE = {'meta_tokens': 9.294789e-03, 'norm_gain': 2.488442e-01, 'w_in': 6.508535e-02, 'w_gate_up': 1.194586e-02, 'b_gate': 5.483859e-02, 'ret_norm_gain': 5.350222e-02, 'gla_norm_gain': 7.502294e-02, 'w_branch_ret': 7.356786e-02, 'w_branch_gla': 7.408790e-02, 'w_out': 1.046991e-01, 'final_norm_gain': 6.394252e+01}


def _to_microbatches(a, axis):
    t = _jnp.moveaxis(a, axis, 0)
    t = t.reshape((N_MICROBATCH, t.shape[0] // N_MICROBATCH) + t.shape[1:])
    return _jnp.moveaxis(t, 1, axis + 1)


def setup_inputs(seed: int = 0) -> dict:
    inp = _fwd_setup_inputs(seed)
    key = _jax.random.fold_in(_jax.random.key(seed), 7919)
    shape, _ = _output_shape()
    out = dict(inp)
    out["loss_target"] = _jax.random.normal(_jax.random.fold_in(key, 0), shape, _jnp.float32)
    for i, name in enumerate(TWIN_WEIGHTS):
        w = inp[name].astype(_jnp.float32)
        if MOMENT_SCALE is None:
            s = _jnp.sqrt(_jnp.mean(_jnp.square(w)) + 1e-30)
        else:
            s = MOMENT_SCALE[name]
        km, kv = _jax.random.split(_jax.random.fold_in(key, i + 1))
        out[name] = w
        out["m_" + name] = s * _jax.random.normal(km, w.shape, _jnp.float32)
        out["v_" + name] = (s * s) * _jax.random.uniform(kv, w.shape, _jnp.float32, 0.5, 1.5)
    if N_MICROBATCH > 1:
        for name, axis in PER_EXAMPLE_BATCH_AXIS.items():
            out[name] = _to_microbatches(out[name], axis)
    return {'x': out['x'], 'meta_tokens': out['meta_tokens'], 'norm_gain': out['norm_gain'], 'w_in': out['w_in'], 'w_gate_up': out['w_gate_up'], 'b_gate': out['b_gate'], 'ret_norm_gain': out['ret_norm_gain'], 'gla_norm_gain': out['gla_norm_gain'], 'w_branch_ret': out['w_branch_ret'], 'w_branch_gla': out['w_branch_gla'], 'w_out': out['w_out'], 'final_norm_gain': out['final_norm_gain'], 'loss_target': out['loss_target'], 'm_meta_tokens': out['m_meta_tokens'], 'm_norm_gain': out['m_norm_gain'], 'm_w_in': out['m_w_in'], 'm_w_gate_up': out['m_w_gate_up'], 'm_b_gate': out['m_b_gate'], 'm_ret_norm_gain': out['m_ret_norm_gain'], 'm_gla_norm_gain': out['m_gla_norm_gain'], 'm_w_branch_ret': out['m_w_branch_ret'], 'm_w_branch_gla': out['m_w_branch_gla'], 'm_w_out': out['m_w_out'], 'm_final_norm_gain': out['m_final_norm_gain'], 'v_meta_tokens': out['v_meta_tokens'], 'v_norm_gain': out['v_norm_gain'], 'v_w_in': out['v_w_in'], 'v_w_gate_up': out['v_w_gate_up'], 'v_b_gate': out['v_b_gate'], 'v_ret_norm_gain': out['v_ret_norm_gain'], 'v_gla_norm_gain': out['v_gla_norm_gain'], 'v_w_branch_ret': out['v_w_branch_ret'], 'v_w_branch_gla': out['v_w_branch_gla'], 'v_w_out': out['v_w_out'], 'v_final_norm_gain': out['v_final_norm_gain']}


def _loss(weights, diff, rest, loss_target):
    with _jax.named_scope("forward"):
        args = {**rest, TWIN_DIFF_INPUT: diff, **{k: w.astype(_WEIGHT_DTYPES[k]) for k, w in weights.items()}}
        y = _forward(args)
    with _jax.named_scope("loss_head"):
        err = _jnp.square(y.astype(_jnp.float32) - loss_target)
        return 0.5 * _jnp.sum(_jnp.mean(err, axis=-1)) if err.ndim else 0.5 * err


def _adamw(w, g, m, v):
    m = ADAM_B1 * m + (1.0 - ADAM_B1) * g
    v = ADAM_B2 * v + (1.0 - ADAM_B2) * _jnp.square(g)
    m_hat = m / (1.0 - ADAM_B1 ** ADAM_STEP)
    v_hat = v / (1.0 - ADAM_B2 ** ADAM_STEP)
    delta = -ADAM_LR * (m_hat / (_jnp.sqrt(v_hat) + ADAM_EPS) + ADAM_WD * w)
    return delta, m, v


def reference(x, meta_tokens, norm_gain, w_in, w_gate_up, b_gate, ret_norm_gain, gla_norm_gain, w_branch_ret, w_branch_gla, w_out, final_norm_gain, loss_target, m_meta_tokens, m_norm_gain, m_w_in, m_w_gate_up, m_b_gate, m_ret_norm_gain, m_gla_norm_gain, m_w_branch_ret, m_w_branch_gla, m_w_out, m_final_norm_gain, v_meta_tokens, v_norm_gain, v_w_in, v_w_gate_up, v_b_gate, v_ret_norm_gain, v_gla_norm_gain, v_w_branch_ret, v_w_branch_gla, v_w_out, v_final_norm_gain):
    given = dict(x=x, meta_tokens=meta_tokens, norm_gain=norm_gain, w_in=w_in, w_gate_up=w_gate_up, b_gate=b_gate, ret_norm_gain=ret_norm_gain, gla_norm_gain=gla_norm_gain, w_branch_ret=w_branch_ret, w_branch_gla=w_branch_gla, w_out=w_out, final_norm_gain=final_norm_gain, loss_target=loss_target, m_meta_tokens=m_meta_tokens, m_norm_gain=m_norm_gain, m_w_in=m_w_in, m_w_gate_up=m_w_gate_up, m_b_gate=m_b_gate, m_ret_norm_gain=m_ret_norm_gain, m_gla_norm_gain=m_gla_norm_gain, m_w_branch_ret=m_w_branch_ret, m_w_branch_gla=m_w_branch_gla, m_w_out=m_w_out, m_final_norm_gain=m_final_norm_gain, v_meta_tokens=v_meta_tokens, v_norm_gain=v_norm_gain, v_w_in=v_w_in, v_w_gate_up=v_w_gate_up, v_b_gate=v_b_gate, v_ret_norm_gain=v_ret_norm_gain, v_gla_norm_gain=v_gla_norm_gain, v_w_branch_ret=v_w_branch_ret, v_w_branch_gla=v_w_branch_gla, v_w_out=v_w_out, v_final_norm_gain=v_final_norm_gain)
    weights = {n: given[n] for n in TWIN_WEIGHTS}
    shared = {n: given[n] for n in SHARED_INPUTS}
    per_example = {n: given[n] for n in ['x']}
    grad_fn = _jax.value_and_grad(_loss, argnums=(0, 1))

    def one_microbatch(ex, loss_target):
        ex = dict(ex)
        diff = ex.pop(TWIN_DIFF_INPUT)
        return grad_fn(weights, diff, {**shared, **ex}, loss_target)

    if N_MICROBATCH == 1:
        loss, (grad_w, grad_x) = one_microbatch(per_example, given["loss_target"])
    else:
        def body(carry, xs):
            loss_sum, grad_sum = carry
            l_k, (gw_k, gx_k) = one_microbatch(xs[0], xs[1])
            with _jax.named_scope("update"):
                return (loss_sum + l_k, _jax.tree.map(_jnp.add, grad_sum, gw_k)), gx_k

        init = (_jnp.zeros((), _jnp.float32), _jax.tree.map(_jnp.zeros_like, weights))
        (loss, grad_w), grad_x = _jax.lax.scan(body, init, (per_example, given["loss_target"]))
    with _jax.named_scope("update"):
        delta_w, new_m, new_v = {}, {}, {}
        for n in TWIN_WEIGHTS:
            delta_w[n], new_m[n], new_v[n] = _adamw(weights[n], grad_w[n], given["m_" + n], given["v_" + n])
    return (loss, grad_x, *[grad_w[n] for n in TWIN_WEIGHTS], *[delta_w[n] for n in TWIN_WEIGHTS],
            *[new_m[n] for n in TWIN_WEIGHTS], *[new_v[n] for n in TWIN_WEIGHTS])
```

```python
import functools
import math

import numpy as np
import jax
import jax.numpy as jnp
from jax import lax
from jax.experimental import pallas as pl
from jax.experimental.pallas import tpu as pltpu

F32 = jnp.float32
BF16 = jnp.bfloat16

D_MODEL = 1024
N_META = 16
EPS = 1e-6
ROPE_BASE = 10000.0
RET_HEADS, RET_QK, RET_V = 4, 256, 512
RET_W = RET_HEADS * RET_V
GLA_HEADS, GLA_K, GLA_V = 4, 128, 256
GLA_W = GLA_HEADS * GLA_V
GLA_KW = GLA_HEADS * GLA_K
GATE_RANK = 16
GATE_TAU = 16.0
GLA_SUB = 16

TM = 256
T0 = TM
PADF = T0 - N_META
GC = 128
TK = 768

W_R = 6144
W_G = 3088
W_GP = 3200
W_M = 2048
IN_COLS = W_R + W_G + W_M

ADAM_LR, ADAM_B1, ADAM_B2, ADAM_EPS, ADAM_WD, ADAM_STEP = 0.001, 0.9, 0.999, 1e-08, 0.01, 10

VMEM_LIMIT = 56 * 1024 * 1024

NN = ((1,), (0,))
NT = ((1,), (1,))
TN = ((0,), (0,))


def _dot(a, b, dims):
    return lax.dot_general(a, b, (dims, ((), ())), preferred_element_type=F32)


def _cparams(n_axes):
    return pltpu.CompilerParams(dimension_semantics=("arbitrary",) * n_axes, vmem_limit_bytes=VMEM_LIMIT)


def _sigmoid(x):
    return 1.0 / (1.0 + jnp.exp(-x))


def _split3(x):
    hi = x.astype(BF16)
    r1 = x - hi.astype(F32)
    mid = r1.astype(BF16)
    lo = (r1 - mid.astype(F32)).astype(BF16)
    return hi, mid, lo


def _exact_pm(p, x):
    hi, mid, lo = _split3(x)
    return _dot(p, hi, NN) + _dot(p, mid, NN) + _dot(p, lo, NN)


def _rms_call(h0, gain):
    tp = h0.shape[0]

    def body(h_ref, g_ref, u_ref):
        h = h_ref[...]
        r = lax.rsqrt(jnp.mean(h * h, axis=-1, keepdims=True) + EPS)
        u_ref[...] = (h * r * g_ref[...]).astype(BF16)

    return pl.pallas_call(
        body, name="rms_in", grid=(tp // TM,),
        in_specs=[pl.BlockSpec((TM, D_MODEL), lambda i: (i, 0)), pl.BlockSpec((1, D_MODEL), lambda i: (0, 0))],
        out_specs=pl.BlockSpec((TM, D_MODEL), lambda i: (i, 0)),
        out_shape=jax.ShapeDtypeStruct((tp, D_MODEL), BF16),
        compiler_params=_cparams(1),
    )(h0, gain)


def _mm_nn(name, a, b, out_dtype, tn, col0, ncols, epilogue=None, extras=(), extra_specs=()):
    m, k = a.shape
    nj, j0 = ncols // tn, col0 // tn

    def body(a_ref, b_ref, *rest):
        *ex, o_ref = rest
        acc = _dot(a_ref[...], b_ref[...], NN)
        if epilogue is None:
            o_ref[...] = acc.astype(out_dtype)
        else:
            epilogue(acc, o_ref, *ex)

    return pl.pallas_call(
        body, name=name, grid=(nj, m // TM),
        in_specs=[pl.BlockSpec((TM, k), lambda j, i: (i, 0)), pl.BlockSpec((k, tn), lambda j, i: (0, j0 + j))]
        + list(extra_specs),
        out_specs=pl.BlockSpec((TM, tn), lambda j, i: (i, j)),
        out_shape=jax.ShapeDtypeStruct((m, ncols), out_dtype),
        compiler_params=_cparams(2),
    )(a, b, *extras)


def _rope_epilogue(acc, o_ref, cos_ref, sin_ref):
    scale = jnp.where(pl.program_id(0) == 1, RET_QK ** -0.5, 1.0).astype(F32)
    cos, sin = cos_ref[...], sin_ref[...]
    half = RET_QK // 2
    for h in range(RET_HEADS):
        t1 = acc[:, h * RET_QK:h * RET_QK + half]
        t2 = acc[:, h * RET_QK + half:(h + 1) * RET_QK]
        o_ref[:, h * RET_QK:h * RET_QK + half] = ((t1 * cos - t2 * sin) * scale).astype(BF16)
        o_ref[:, h * RET_QK + half:(h + 1) * RET_QK] = ((t2 * cos + t1 * sin) * scale).astype(BF16)


def _gqk_epilogue(acc, o_ref):
    o_ref[:, :GLA_KW] = acc[:, :GLA_KW] * (GLA_K ** -0.5)
    o_ref[:, GLA_KW:] = acc[:, GLA_KW:]


def _mm_nt_acc(name, a, w, tk, acc_in=None, epilogue=None, extras=(), extra_specs=(), extra_out_shapes=(),
               extra_out_specs=()):
    m, k = a.shape
    n = w.shape[0]
    nk = k // tk
    has_acc = acc_in is not None

    def body(*refs):
        a_ref, w_ref = refs[0], refs[1]
        pos = 2
        acc_ref = None
        if has_acc:
            acc_ref = refs[pos]
            pos += 1
        n_ex = len(extras)
        ex = refs[pos:pos + n_ex]
        outs = refs[pos + n_ex:-1]
        scr = refs[-1]
        i, kk = pl.program_id(0), pl.program_id(1)

        @pl.when(kk == 0)
        def _():
            scr[...] = acc_ref[...] if has_acc else jnp.zeros_like(scr)

        scr[...] += _dot(a_ref[...], w_ref[...], NT)

        @pl.when(kk == nk - 1)
        def _():
            if epilogue is None:
                outs[0][...] = scr[...]
            else:
                epilogue(scr[...], outs, i, *ex)

    in_specs = [pl.BlockSpec((TM, tk), lambda i, kk: (i, kk)), pl.BlockSpec((n, tk), lambda i, kk: (0, kk))]
    args = [a, w]
    if has_acc:
        in_specs.append(pl.BlockSpec((TM, n), lambda i, kk: (i, 0)))
        args.append(acc_in)
    in_specs += list(extra_specs)
    args += list(extras)
    if epilogue is None:
        out_shape = [jax.ShapeDtypeStruct((m, n), F32)]
        out_specs = [pl.BlockSpec((TM, n), lambda i, kk: (i, 0))]
    else:
        out_shape, out_specs = list(extra_out_shapes), list(extra_out_specs)
    return pl.pallas_call(
        body, name=name, grid=(m // TM, nk), in_specs=in_specs, out_specs=out_specs, out_shape=out_shape,
        scratch_shapes=[pltpu.VMEM((TM, n), F32)], compiler_params=_cparams(2),
    )(*args)


def _rms_bwd_epilogue(du, outs, i, h_ref, g_ref, dh1_ref):
    dh0_ref, dg_ref = outs
    h = h_ref[...]
    r = lax.rsqrt(jnp.mean(h * h, axis=-1, keepdims=True) + EPS)
    xh = h * r
    dxh = du * g_ref[...]
    dh0_ref[...] = dh1_ref[...] + r * (dxh - xh * jnp.mean(dxh * xh, axis=-1, keepdims=True))

    @pl.when(i == 0)
    def _():
        dg_ref[...] = jnp.zeros_like(dg_ref)

    dg_ref[...] += jnp.sum(du * xh, axis=0, keepdims=True)


def _mm_tn(name, a, b, bn):
    t, m = a.shape
    n = b.shape[1]

    def body(a_ref, b_ref, o_ref):
        @pl.when(pl.program_id(1) == 0)
        def _():
            o_ref[...] = jnp.zeros_like(o_ref)

        o_ref[...] += _dot(a_ref[...], b_ref[...], TN)

    return pl.pallas_call(
        body, name=name, grid=(n // bn, t // TK),
        in_specs=[pl.BlockSpec((TK, m), lambda j, kk: (kk, 0)), pl.BlockSpec((TK, bn), lambda j, kk: (kk, j))],
        out_specs=pl.BlockSpec((m, bn), lambda j, kk: (0, j)),
        out_shape=jax.ShapeDtypeStruct((m, n), F32),
        compiler_params=_cparams(2),
    )(a, b)


def _ret_consts(lg):
    c = TM
    ii = lax.broadcasted_iota(jnp.int32, (c, c), 0)
    jj = lax.broadcasted_iota(jnp.int32, (c, c), 1)
    rel = (ii - jj).astype(F32)
    dm = jnp.where(rel >= 0, jnp.exp(jnp.maximum(rel, 0.0) * lg), 0.0)
    idx = lax.broadcasted_iota(jnp.int32, (c, 1), 0).astype(F32)
    xi = jnp.exp((idx + 1.0) * lg)
    zeta = jnp.exp((c - 1.0 - idx) * lg)
    gc = jnp.exp(jnp.full((1, 1), c, F32) * lg)
    return dm, xi, zeta, gc


def _ret_fwd_call(rqk, rv, rg, gain, lgam):
    tp = rqk.shape[0]
    nc = tp // TM

    def body(lg_ref, qk_ref, v_ref, rg_ref, g_ref, o_ref, a_ref, st_ref, s_scr):
        @pl.when(pl.program_id(0) == 0)
        def _():
            s_scr[...] = jnp.zeros_like(s_scr)

        for h in range(RET_HEADS):
            dm, xi, zeta, gc = _ret_consts(lg_ref[h])
            q = qk_ref[:, h * RET_QK:(h + 1) * RET_QK]
            k = qk_ref[:, D_MODEL + h * RET_QK:D_MODEL + (h + 1) * RET_QK]
            v = v_ref[:, h * RET_V:(h + 1) * RET_V]
            sb = s_scr[h].astype(BF16)
            st_ref[0, h] = sb
            s = _dot(q, k, NT) * dm
            o = _dot(s.astype(BF16), v, NN) + xi * _dot(q, sb, NN)
            kz = (k.astype(F32) * zeta).astype(BF16)
            s_scr[h] = gc * s_scr[h] + _dot(kz, v, TN)
            o_ref[:, h * RET_V:(h + 1) * RET_V] = o
            mu = jnp.mean(o, axis=-1, keepdims=True)
            xc = o - mu
            xh = xc * lax.rsqrt(jnp.mean(xc * xc, axis=-1, keepdims=True) + EPS)
            g = rg_ref[:, h * RET_V:(h + 1) * RET_V]
            a_ref[:, h * RET_V:(h + 1) * RET_V] = (
                xh * g_ref[:, h * RET_V:(h + 1) * RET_V] * (g * _sigmoid(g))).astype(BF16)

    return pl.pallas_call(
        body, name="ret_fwd", grid=(nc,),
        in_specs=[pl.BlockSpec(memory_space=pltpu.SMEM),
                  pl.BlockSpec((TM, 2 * D_MODEL), lambda n: (n, 0)),
                  pl.BlockSpec((TM, RET_W), lambda n: (n, 0)),
                  pl.BlockSpec((TM, RET_W), lambda n: (n, 0)),
                  pl.BlockSpec((1, RET_W), lambda n: (0, 0))],
        out_specs=[pl.BlockSpec((TM, RET_W), lambda n: (n, 0)),
                   pl.BlockSpec((TM, RET_W), lambda n: (n, 0)),
                   pl.BlockSpec((1, RET_HEADS, RET_QK, RET_V), lambda n: (n, 0, 0, 0))],
        out_shape=[jax.ShapeDtypeStruct((tp, RET_W), F32), jax.ShapeDtypeStruct((tp, RET_W), BF16),
                   jax.ShapeDtypeStruct((nc, RET_HEADS, RET_QK, RET_V), BF16)],
        scratch_shapes=[pltpu.VMEM((RET_HEADS, RET_QK, RET_V), F32)],
        compiler_params=_cparams(1),
    )(lgam, rqk, rv, rg, gain)


def _ret_bwd_call(rqk, rv, rg, o_ret, da, states, gain, lgam, cos, sin):
    tp = rqk.shape[0]
    nc = tp // TM
    half = RET_QK // 2

    def body(lg_ref, qk_ref, v_ref, rg_ref, o_ref, da_ref, st_ref, g_ref, cos_ref, sin_ref, dp_ref, dg_ref, ds_scr):
        @pl.when(pl.program_id(0) == 0)
        def _():
            ds_scr[...] = jnp.zeros_like(ds_scr)
            dg_ref[...] = jnp.zeros_like(dg_ref)

        cos, sin = cos_ref[...], sin_ref[...]
        for h in range(RET_HEADS):
            hs = slice(h * RET_V, (h + 1) * RET_V)
            dm, xi, zeta, gc = _ret_consts(lg_ref[h])
            o = o_ref[:, hs]
            mu = jnp.mean(o, axis=-1, keepdims=True)
            xc = o - mu
            rstd = lax.rsqrt(jnp.mean(xc * xc, axis=-1, keepdims=True) + EPS)
            xh = xc * rstd
            gain_h = g_ref[:, hs]
            g = rg_ref[:, hs]
            sg = _sigmoid(g)
            silu = g * sg
            dah = da_ref[:, hs]
            dp_ref[:, 4 * D_MODEL + h * RET_V:4 * D_MODEL + (h + 1) * RET_V] = (
                dah * (xh * gain_h) * (sg * (1.0 + g * (1.0 - sg)))).astype(BF16)
            dn = dah * silu
            dg_ref[:, hs] += jnp.sum(dn * xh, axis=0, keepdims=True)
            dxh = dn * gain_h
            do = rstd * (dxh - jnp.mean(dxh, axis=-1, keepdims=True)
                         - xh * jnp.mean(dxh * xh, axis=-1, keepdims=True))
            dob = do.astype(BF16)
            q = qk_ref[:, h * RET_QK:(h + 1) * RET_QK]
            k = qk_ref[:, D_MODEL + h * RET_QK:D_MODEL + (h + 1) * RET_QK]
            v = v_ref[:, hs]
            sp = st_ref[0, h]
            ds = ds_scr[h]
            dsb = ds.astype(BF16)
            s = (_dot(q, k, NT) * dm).astype(BF16)
            dsc = (_dot(dob, v, NT) * dm).astype(BF16)
            dq = _dot(dsc, k, NN) + xi * _dot(dob, sp, NT)
            dk = _dot(dsc, q, TN) + zeta * _dot(v, dsb, NT)
            kz = (k.astype(F32) * zeta).astype(BF16)
            dv = _dot(s, dob, TN) + _dot(kz, dsb, NN)
            qx = (q.astype(F32) * xi).astype(BF16)
            ds_scr[h] = gc * ds + _dot(qx, dob, TN)
            dp_ref[:, 2 * D_MODEL + h * RET_V:2 * D_MODEL + (h + 1) * RET_V] = dv.astype(BF16)
            dk = dk * (RET_QK ** -0.5)
            for base, t in ((0, dq), (D_MODEL, dk)):
                t1, t2 = t[:, :half], t[:, half:]
                dp_ref[:, base + h * RET_QK:base + h * RET_QK + half] = (t1 * cos + t2 * sin).astype(BF16)
                dp_ref[:, base + h * RET_QK + half:base + (h + 1) * RET_QK] = (t2 * cos - t1 * sin).astype(BF16)

    rev = lambda n: (nc - 1 - n, 0)
    return pl.pallas_call(
        body, name="ret_bwd", grid=(nc,),
        in_specs=[pl.BlockSpec(memory_space=pltpu.SMEM),
                  pl.BlockSpec((TM, 2 * D_MODEL), rev),
                  pl.BlockSpec((TM, RET_W), rev),
                  pl.BlockSpec((TM, RET_W), rev),
                  pl.BlockSpec((TM, RET_W), rev),
                  pl.BlockSpec((TM, RET_W), rev),
                  pl.BlockSpec((1, RET_HEADS, RET_QK, RET_V), lambda n: (nc - 1 - n, 0, 0, 0)),
                  pl.BlockSpec((1, RET_W), lambda n: (0, 0)),
                  pl.BlockSpec((TM, half), rev),
                  pl.BlockSpec((TM, half), rev)],
        out_specs=[pl.BlockSpec((TM, W_R), rev), pl.BlockSpec((1, RET_W), lambda n: (0, 0))],
        out_shape=[jax.ShapeDtypeStruct((tp, W_R), BF16), jax.ShapeDtypeStruct((1, RET_W), F32)],
        scratch_shapes=[pltpu.VMEM((RET_HEADS, RET_QK, RET_V), F32)],
        compiler_params=_cparams(1),
    )(lgam, rqk, rv, rg, o_ret, da, states, gain, cos, sin)


GLA_LEVELS = tuple(GC >> (s + 1) for s in range(int(math.log2(GC // GLA_SUB))))
NLEV = len(GLA_LEVELS)
NBLK = 2 * NLEV + 3
P_ROWS = NBLK * GC + 8


def _gla_p_matrix():
    c = GC
    i = np.arange(c)[:, None]
    r = np.arange(c)[None, :]
    blocks = []
    for m in GLA_LEVELS:
        second = (i & m) != 0
        ref = (i // (2 * m)) * 2 * m + m - 1
        blocks.append(second & (r > ref) & (r <= i))
    for m in GLA_LEVELS:
        second = (i & m) != 0
        ref = (i // (2 * m)) * 2 * m + m - 1
        blocks.append((~second) & (r > i) & (r <= ref))
    blocks.append((r >= (i // GLA_SUB) * GLA_SUB) & (r <= i))
    blocks.append(r <= i)
    blocks.append(r > i)
    blocks.append(np.ones((8, c), bool))
    return np.concatenate([b.astype(np.float32) for b in blocks], axis=0)


def _gla_masks():
    ii = lax.broadcasted_iota(jnp.int32, (GC, GC), 0)
    jj = lax.broadcasted_iota(jnp.int32, (GC, GC), 1)
    masks = []
    for m in GLA_LEVELS:
        sh = int(math.log2(2 * m))
        masks.append(((ii >> sh) == (jj >> sh)) & ((ii & m) != 0) & ((jj & m) == 0))
    sh = int(math.log2(GLA_SUB))
    md = ((ii >> sh) == (jj >> sh)) & (jj <= ii)
    row = lax.broadcasted_iota(jnp.int32, (GC, 1), 0)
    second = [(row & m) != 0 for m in GLA_LEVELS]
    return masks, md, second


def _gla_log_decay(glr_ref, wg_ref, bg_ref):
    z = _dot(glr_ref[...].astype(BF16), wg_ref[...], NN) + bg_ref[...]
    la = (jnp.minimum(z, 0.0) - jnp.log1p(jnp.exp(-jnp.abs(z)))) * (1.0 / GATE_TAU)
    return z, la


def _gla_factors(e, h, second):
    cs = slice(h * GLA_K, (h + 1) * GLA_K)
    blk = lambda b: e[b * GC:(b + 1) * GC, cs]
    fq = [jnp.where(second[l], jnp.exp(blk(l)), 0.0) for l in range(NLEV)]
    fk = [jnp.where(second[l], 0.0, jnp.exp(blk(NLEV + l))) for l in range(NLEV)]
    ed = jnp.exp(blk(2 * NLEV))
    edi = jnp.exp(-blk(2 * NLEV))
    eb = jnp.exp(blk(2 * NLEV + 1))
    ee = jnp.exp(blk(2 * NLEV + 2))
    ebl = jnp.exp(e[NBLK * GC:NBLK * GC + 1, cs])
    return fq, fk, ed, edi, eb, ee, ebl


def _gla_scores(q, k, fq, fk, ed, edi, masks, md):
    qt = [(q * f).astype(BF16) for f in fq]
    kt = [(k * f).astype(BF16) for f in fk]
    qd = (q * ed).astype(BF16)
    kd = (k * edi).astype(BF16)
    a = jnp.where(md, _dot(qd, kd, NT), 0.0)
    for l in range(NLEV):
        a = a + jnp.where(masks[l], _dot(qt[l], kt[l], NT), 0.0)
    return a, qt, kt, qd, kd


def _gla_fwd_call(gqk, gv, glr, gg, wg, bg, gain, pmat):
    tp = gqk.shape[0]
    nc = tp // GC

    def body(qk_ref, v_ref, glr_ref, gg_ref, wg_ref, bg_ref, g_ref, p_ref, o_ref, a_ref, st_ref, s_scr):
        @pl.when(pl.program_id(0) == 0)
        def _():
            s_scr[...] = jnp.zeros_like(s_scr)

        _, la = _gla_log_decay(glr_ref, wg_ref, bg_ref)
        e = _exact_pm(p_ref[...], la)
        masks, md, second = _gla_masks()
        for h in range(GLA_HEADS):
            q = qk_ref[:, h * GLA_K:(h + 1) * GLA_K]
            k = qk_ref[:, GLA_KW + h * GLA_K:GLA_KW + (h + 1) * GLA_K]
            vs = slice(h * GLA_V, (h + 1) * GLA_V)
            v = v_ref[:, vs]
            fq, fk, ed, edi, eb, ee, ebl = _gla_factors(e, h, second)
            a, *_ = _gla_scores(q, k, fq, fk, ed, edi, masks, md)
            sb = s_scr[h].astype(BF16)
            st_ref[0, h] = sb
            o = _dot(a.astype(BF16), v, NN) + _dot((q * eb).astype(BF16), sb, NT)
            s_scr[h] = s_scr[h] * ebl + _dot(v, (k * ee).astype(BF16), TN)
            o_ref[:, vs] = o
            xh = o * lax.rsqrt(jnp.mean(o * o, axis=-1, keepdims=True) + EPS)
            g = gg_ref[:, vs]
            a_ref[:, vs] = (xh * g_ref[:, vs] * (g * _sigmoid(g))).astype(BF16)

    return pl.pallas_call(
        body, name="gla_fwd", grid=(nc,),
        in_specs=[pl.BlockSpec((GC, 2 * GLA_KW), lambda n: (n, 0)),
                  pl.BlockSpec((GC, GLA_W), lambda n: (n, 0)),
                  pl.BlockSpec((GC, 128), lambda n: (n, 0)),
                  pl.BlockSpec((GC, GLA_W), lambda n: (n, 0)),
                  pl.BlockSpec((128, GLA_KW), lambda n: (0, 0)),
                  pl.BlockSpec((1, GLA_KW), lambda n: (0, 0)),
                  pl.BlockSpec((1, GLA_W), lambda n: (0, 0)),
                  pl.BlockSpec((P_ROWS, GC), lambda n: (0, 0))],
        out_specs=[pl.BlockSpec((GC, GLA_W), lambda n: (n, 0)),
                   pl.BlockSpec((GC, GLA_W), lambda n: (n, 0)),
                   pl.BlockSpec((1, GLA_HEADS, GLA_V, GLA_K), lambda n: (n, 0, 0, 0))],
        out_shape=[jax.ShapeDtypeStruct((tp, GLA_W), F32), jax.ShapeDtypeStruct((tp, GLA_W), BF16),
                   jax.ShapeDtypeStruct((nc, GLA_HEADS, GLA_V, GLA_K), BF16)],
        scratch_shapes=[pltpu.VMEM((GLA_HEADS, GLA_V, GLA_K), F32)],
        compiler_params=_cparams(1),
    )(gqk, gv, glr, gg, wg, bg, gain, pmat)


def _gla_bwd_call(gqk, gv, glr, gg, o_gla, da, states, wg, bg, gain, pmat, pmat_t):
    tp = gqk.shape[0]
    nc = tp // GC
    o_gv, o_gg, o_lr = 2 * GLA_KW, 2 * GLA_KW + GLA_W, 2 * GLA_KW + 2 * GLA_W

    def body(qk_ref, v_ref, glr_ref, gg_ref, o_ref, da_ref, st_ref, wg_ref, bg_ref, g_ref, p_ref, pt_ref,
             dp_ref, dwg_ref, dbg_ref, dg_ref, ds_scr, de_scr):
        n = pl.program_id(0)

        @pl.when(n == 0)
        def _():
            ds_scr[...] = jnp.zeros_like(ds_scr)
            dwg_ref[...] = jnp.zeros_like(dwg_ref)
            dbg_ref[...] = jnp.zeros_like(dbg_ref)
            dg_ref[...] = jnp.zeros_like(dg_ref)
            de_scr[...] = jnp.zeros_like(de_scr)

        z, la = _gla_log_decay(glr_ref, wg_ref, bg_ref)
        e = _exact_pm(p_ref[...], la)
        masks, md, second = _gla_masks()
        for h in range(GLA_HEADS):
            cs = slice(h * GLA_K, (h + 1) * GLA_K)
            vs = slice(h * GLA_V, (h + 1) * GLA_V)
            o = o_ref[:, vs]
            rstd = lax.rsqrt(jnp.mean(o * o, axis=-1, keepdims=True) + EPS)
            xh = o * rstd
            gain_h = g_ref[:, vs]
            g = gg_ref[:, vs]
            sg = _sigmoid(g)
            dah = da_ref[:, vs]
            dp_ref[:, o_gg + h * GLA_V:o_gg + (h + 1) * GLA_V] = (
                dah * (xh * gain_h) * (sg * (1.0 + g * (1.0 - sg)))).astype(BF16)
            dn = dah * (g * sg)
            dg_ref[:, vs] += jnp.sum(dn * xh, axis=0, keepdims=True)
            dxh = dn * gain_h
            do = rstd * (dxh - xh * jnp.mean(dxh * xh, axis=-1, keepdims=True))
            dob = do.astype(BF16)
            q = qk_ref[:, cs]
            k = qk_ref[:, GLA_KW + h * GLA_K:GLA_KW + (h + 1) * GLA_K]
            v = v_ref[:, vs]
            fq, fk, ed, edi, eb, ee, ebl = _gla_factors(e, h, second)
            a, qt, kt, qd, kd = _gla_scores(q, k, fq, fk, ed, edi, masks, md)
            sp = st_ref[0, h]
            ds = ds_scr[h]
            dsb = ds.astype(BF16)
            q_in = q * eb
            k_end = k * ee
            da_s = _dot(dob, v, NT)
            dv = _dot(a.astype(BF16), dob, TN) + _dot(k_end.astype(BF16), dsb, NT)
            dq_in = _dot(dob, sp, NN)
            dk_end = _dot(v, dsb, NN)
            dbl = jnp.sum(sp.astype(F32) * ds, axis=0, keepdims=True) * ebl
            ds_scr[h] = ds * ebl + _dot(dob, q_in.astype(BF16), TN)
            dq = dq_in * eb
            dk = dk_end * ee
            de_scr[(2 * NLEV + 1) * GC:(2 * NLEV + 2) * GC, cs] = dq_in * q_in
            de_scr[(2 * NLEV + 2) * GC:(2 * NLEV + 3) * GC, cs] = dk_end * k_end
            de_scr[NBLK * GC:NBLK * GC + 1, cs] = dbl
            for l in range(NLEV):
                dal = jnp.where(masks[l], da_s, 0.0).astype(BF16)
                dqt = _dot(dal, kt[l], NN)
                dkt = _dot(dal, qt[l], TN)
                dq = dq + dqt * fq[l]
                dk = dk + dkt * fk[l]
                de_scr[l * GC:(l + 1) * GC, cs] = dqt * (q * fq[l])
                de_scr[(NLEV + l) * GC:(NLEV + l + 1) * GC, cs] = dkt * (k * fk[l])
            dad = jnp.where(md, da_s, 0.0).astype(BF16)
            dqd = _dot(dad, kd, NN)
            dkd = _dot(dad, qd, TN)
            dq = dq + dqd * ed
            dk = dk + dkd * edi
            de_scr[2 * NLEV * GC:(2 * NLEV + 1) * GC, cs] = dqd * (q * ed) - dkd * (k * edi)
            dp_ref[:, cs] = (dq * (GLA_K ** -0.5)).astype(BF16)
            dp_ref[:, GLA_KW + h * GLA_K:GLA_KW + (h + 1) * GLA_K] = dk.astype(BF16)
            dp_ref[:, o_gv + h * GLA_V:o_gv + (h + 1) * GLA_V] = dv.astype(BF16)
        dla = _exact_pm(pt_ref[...], de_scr[...])
        row = (nc - 1 - n) * GC + lax.broadcasted_iota(jnp.int32, (GC, 1), 0)
        dz = jnp.where(row >= PADF, dla * (1.0 / GATE_TAU) * _sigmoid(-z), 0.0)
        dzb = dz.astype(BF16)
        dp_ref[:, o_lr:] = _dot(dzb, wg_ref[...], NT).astype(BF16)
        dwg_ref[...] += _dot(glr_ref[...].astype(BF16), dzb, TN)
        dbg_ref[...] += jnp.sum(dz, axis=0, keepdims=True)

    rev = lambda n: (nc - 1 - n, 0)
    const = lambda n: (0, 0)
    return pl.pallas_call(
        body, name="gla_bwd", grid=(nc,),
        in_specs=[pl.BlockSpec((GC, 2 * GLA_KW), rev),
                  pl.BlockSpec((GC, GLA_W), rev),
                  pl.BlockSpec((GC, 128), rev),
                  pl.BlockSpec((GC, GLA_W), rev),
                  pl.BlockSpec((GC, GLA_W), rev),
                  pl.BlockSpec((GC, GLA_W), rev),
                  pl.BlockSpec((1, GLA_HEADS, GLA_V, GLA_K), lambda n: (nc - 1 - n, 0, 0, 0)),
                  pl.BlockSpec((128, GLA_KW), const),
                  pl.BlockSpec((1, GLA_KW), const),
                  pl.BlockSpec((1, GLA_W), const),
                  pl.BlockSpec((P_ROWS, GC), const),
                  pl.BlockSpec((GC, P_ROWS), const)],
        out_specs=[pl.BlockSpec((GC, W_GP), rev), pl.BlockSpec((128, GLA_KW), const),
                   pl.BlockSpec((1, GLA_KW), const), pl.BlockSpec((1, GLA_W), const)],
        out_shape=[jax.ShapeDtypeStruct((tp, W_GP), BF16), jax.ShapeDtypeStruct((128, GLA_KW), F32),
                   jax.ShapeDtypeStruct((1, GLA_KW), F32), jax.ShapeDtypeStruct((1, GLA_W), F32)],
        scratch_shapes=[pltpu.VMEM((GLA_HEADS, GLA_V, GLA_K), F32), pltpu.VMEM((P_ROWS, GLA_KW), F32)],
        compiler_params=_cparams(1),
    )(gqk, gv, glr, gg, o_gla, da, states, wg, bg, gain, pmat, pmat_t)


def _mid_call(a_ret, a_gla, mg, h0, tgt, wbr, wbg, wout, gf):
    tp = h0.shape[0]
    nt = tp // TM

    def body(ar_ref, ag_ref, mg_ref, h_ref, t_ref, wbr_ref, wbg_ref, wo_ref, gf_ref,
             dh1_ref, dar_ref, dag_ref, dm_ref, mb_ref, dh1b_ref, dprb_ref, dpgb_ref, loss_ref, dgf_ref):
        i = pl.program_id(0)

        @pl.when(i == 0)
        def _():
            loss_ref[...] = jnp.zeros_like(loss_ref)
            dgf_ref[...] = jnp.zeros_like(dgf_ref)

        ar, ag = ar_ref[...], ag_ref[...]
        pr = _dot(ar, wbr_ref[...], NN)
        pg = _dot(ag, wbg_ref[...], NN)
        sr = _sigmoid(mg_ref[:, :D_MODEL])
        sg = _sigmoid(mg_ref[:, D_MODEL:])
        merged = (sr * pr + sg * pg).astype(BF16)
        mb_ref[...] = merged
        h1 = h_ref[...] + _dot(merged, wo_ref[...], NN)
        r1 = lax.rsqrt(jnp.mean(h1 * h1, axis=-1, keepdims=True) + EPS)
        xh = h1 * r1
        gfv = gf_ref[...]
        live = jnp.where(i > 0, 1.0, 0.0).astype(F32)
        err = (xh * gfv - t_ref[...]) * live
        loss_ref[...] += jnp.full(loss_ref.shape, 0.5 / D_MODEL, F32) * jnp.sum(err * err)
        dy = err * (1.0 / D_MODEL)
        dgf_ref[...] += jnp.sum(dy * xh, axis=0, keepdims=True)
        dxh = dy * gfv
        dh1 = r1 * (dxh - xh * jnp.mean(dxh * xh, axis=-1, keepdims=True))
        dh1_ref[...] = dh1
        dh1b = dh1.astype(BF16)
        dh1b_ref[...] = dh1b
        dmerged = _dot(dh1b, wo_ref[...], NT)
        dm_ref[:, :D_MODEL] = (dmerged * pr * sr * (1.0 - sr)).astype(BF16)
        dm_ref[:, D_MODEL:] = (dmerged * pg * sg * (1.0 - sg)).astype(BF16)
        dpr = (dmerged * sr).astype(BF16)
        dpg = (dmerged * sg).astype(BF16)
        dprb_ref[...] = dpr
        dpgb_ref[...] = dpg
        dar_ref[...] = _dot(dpr, wbr_ref[...], NT)
        dag_ref[...] = _dot(dpg, wbg_ref[...], NT)

    tile = lambda w: pl.BlockSpec((TM, w), lambda i: (i, 0))
    const = lambda r, w: pl.BlockSpec((r, w), lambda i: (0, 0))
    return pl.pallas_call(
        body, name="merge_out_loss", grid=(nt,),
        in_specs=[tile(RET_W), tile(GLA_W), tile(W_M), tile(D_MODEL),
                  pl.BlockSpec((TM, D_MODEL), lambda i: (jnp.maximum(i - 1, 0), 0)),
                  const(RET_W, D_MODEL), const(GLA_W, D_MODEL), const(D_MODEL, D_MODEL), const(1, D_MODEL)],
        out_specs=[tile(D_MODEL), tile(RET_W), tile(GLA_W), tile(W_M), tile(D_MODEL), tile(D_MODEL), tile(D_MODEL),
                   tile(D_MODEL), const(1, 128), const(1, D_MODEL)],
        out_shape=[jax.ShapeDtypeStruct((tp, D_MODEL), F32), jax.ShapeDtypeStruct((tp, RET_W), F32),
                   jax.ShapeDtypeStruct((tp, GLA_W), F32), jax.ShapeDtypeStruct((tp, W_M), BF16),
                   jax.ShapeDtypeStruct((tp, D_MODEL), BF16), jax.ShapeDtypeStruct((tp, D_MODEL), BF16),
                   jax.ShapeDtypeStruct((tp, D_MODEL), BF16), jax.ShapeDtypeStruct((tp, D_MODEL), BF16),
                   jax.ShapeDtypeStruct((1, 128), F32), jax.ShapeDtypeStruct((1, D_MODEL), F32)],
        compiler_params=_cparams(1),
    )(a_ret, a_gla, mg, h0, tgt, wbr, wbg, wout, gf)


def _local_step(x2d, tgt2d, meta, norm_gain, w_in_bf, w_gate_up, b_gate, ret_gain, gla_gain, wbr, wbg, wout,
                final_gain):
    seq = x2d.shape[0]
    tp = T0 + seq
    h0 = jnp.concatenate([jnp.zeros((PADF, D_MODEL), F32), meta, x2d], axis=0)
    w_r = w_in_bf[:, :W_R]
    w_g = jnp.pad(w_in_bf[:, W_R:W_R + W_G], ((0, 0), (0, W_GP - W_G)))
    w_m = w_in_bf[:, W_R + W_G:]
    wg_pad = jnp.pad(w_gate_up, ((0, 128 - GATE_RANK), (0, 0))).astype(BF16)

    pos = jnp.arange(tp, dtype=F32) - PADF
    half = RET_QK // 2
    inv = ROPE_BASE ** (-jnp.arange(half, dtype=F32) / half)
    ang = pos[:, None] * inv[None, :]
    cos, sin = jnp.cos(ang), jnp.sin(ang)
    lgam = jnp.log1p(-(2.0 ** (-5.0 - jnp.arange(RET_HEADS, dtype=F32))))
    pm_np = _gla_p_matrix()
    pmat = jnp.asarray(pm_np, BF16)
    pmat_t = jnp.asarray(pm_np.T.copy(), BF16)

    u = _rms_call(h0, norm_gain)
    tab = pl.BlockSpec((TM, half), lambda j, i: (i, 0))
    rqk = _mm_nn("proj_rqk", u, w_r, BF16, D_MODEL, 0, 2 * D_MODEL, _rope_epilogue, (cos, sin), (tab, tab))
    rv = _mm_nn("proj_rv", u, w_r, BF16, D_MODEL, 2 * D_MODEL, RET_W)
    rg = _mm_nn("proj_rg", u, w_r, F32, D_MODEL, 4 * D_MODEL, RET_W)
    gqk = _mm_nn("proj_gqk", u, w_g, F32, 2 * GLA_KW, 0, 2 * GLA_KW, _gqk_epilogue)
    gv = _mm_nn("proj_gv", u, w_g, BF16, GLA_W, 2 * GLA_KW, GLA_W)
    gg = _mm_nn("proj_gg", u, w_g, F32, GLA_W, 2 * GLA_KW + GLA_W, GLA_W)
    glr = _mm_nn("proj_glr", u, w_g, F32, 128, 2 * GLA_KW + 2 * GLA_W, 128)
    mg = _mm_nn("proj_mg", u, w_m, F32, D_MODEL, 0, W_M)

    o_ret, a_ret, st_ret = _ret_fwd_call(rqk, rv, rg, ret_gain, lgam)
    o_gla, a_gla, st_gla = _gla_fwd_call(gqk, gv, glr, gg, wg_pad, b_gate, gla_gain, pmat)

    gf = final_gain.reshape(1, D_MODEL)
    (dh1, da_ret, da_gla, dm, merged_b, dh1_b, dpr_b, dpg_b, loss, dgf) = _mid_call(
        a_ret, a_gla, mg, h0, tgt2d, wbr, wbg, wout, gf)

    d_g, dwg, dbg, dgla_gain = _gla_bwd_call(gqk, gv, glr, gg, o_gla, da_gla, st_gla, wg_pad, b_gate, gla_gain,
                                             pmat, pmat_t)
    d_r, dret_gain = _ret_bwd_call(rqk, rv, rg, o_ret, da_ret, st_ret, ret_gain, lgam, cos, sin)

    du = _mm_nt_acc("du_m", dm, w_m, W_M)[0]
    du = _mm_nt_acc("du_g", d_g, w_g, W_GP, acc_in=du)[0]
    tile = pl.BlockSpec((TM, D_MODEL), lambda i, kk: (i, 0))
    row = pl.BlockSpec((1, D_MODEL), lambda i, kk: (0, 0))
    dh0, dnorm_gain = _mm_nt_acc(
        "du_r", d_r, w_r, 2 * D_MODEL, acc_in=du, epilogue=_rms_bwd_epilogue, extras=(h0, norm_gain, dh1),
        extra_specs=(tile, row, tile),
        extra_out_shapes=(jax.ShapeDtypeStruct((tp, D_MODEL), F32), jax.ShapeDtypeStruct((1, D_MODEL), F32)),
        extra_out_specs=(tile, row))

    dw_in = jnp.concatenate([_mm_tn("dw_r", u, d_r, D_MODEL), _mm_tn("dw_g", u, d_g, 640)[:, :W_G],
                             _mm_tn("dw_m", u, dm, D_MODEL)], axis=1)
    dwout = _mm_tn("dw_out", merged_b, dh1_b, D_MODEL)
    dwbr = _mm_tn("dw_br", a_ret, dpr_b, D_MODEL)
    dwbg = _mm_tn("dw_bg", a_gla, dpg_b, D_MODEL)

    return dict(loss=loss[0, 0], dx=dh0[T0:], dmeta=dh0[PADF:T0], norm_gain=dnorm_gain, w_in=dw_in,
                w_gate_up=dwg[:GATE_RANK], b_gate=dbg, ret_norm_gain=dret_gain, gla_norm_gain=dgla_gain,
                w_branch_ret=dwbr, w_branch_gla=dwbg, w_out=dwout, final_norm_gain=dgf.reshape(D_MODEL))


MESH = pl.DeviceIdType.MESH
ANY = pl.BlockSpec(memory_space=pl.ANY)


def _place():
    return lax.axis_index("x"), lax.axis_index("y"), lax.axis_index("c")


def _gather8_call(name, parts):
    n = len(parts)

    def body(*refs):
        x_refs, out_refs = refs[:n], refs[n:2 * n]
        send_sems, recv_sems, local_sems = refs[2 * n:]
        x, y, c = _place()
        me, sibling = (x, y, c), (x, y, 1 - c)
        chips = [(1 - x, y), (x, 1 - y), (1 - x, 1 - y)]

        def slot(t, px, py, pc):
            return out_refs[t].at[4 * px + 2 * py + pc]

        def copy(t, k, block, to, src=None):
            return pltpu.make_async_remote_copy(
                src_ref=slot(t, *block) if src is None else src, dst_ref=slot(t, *block),
                send_sem=send_sems.at[7 * t + k], recv_sem=recv_sems.at[7 * t + k], device_id=to, device_id_type=MESH)

        mine = [pltpu.make_async_copy(x_refs[t], slot(t, *me), local_sems.at[t]) for t in range(n)]
        for cp in mine:
            cp.start()
        first = []
        for t in range(n):
            first.append(copy(t, 0, me, sibling, src=x_refs[t]))
            first += [copy(t, 1 + j, me, (*chip, c), src=x_refs[t]) for j, chip in enumerate(chips)]
        for cp in first:
            cp.start()
        passed = []
        for j, chip in enumerate(chips):
            for t in range(n):
                copy(t, 1 + j, (*chip, c), me).wait_recv()
                fwd = copy(t, 4 + j, (*chip, c), sibling)
                fwd.start()
                passed.append(fwd)
        for t in range(n):
            copy(t, 0, sibling, me).wait_recv()
            for j, chip in enumerate(chips):
                copy(t, 4 + j, (*chip, 1 - c), me).wait_recv()
        for cp in first + passed:
            cp.wait_send()
        for cp in mine:
            cp.wait()

    return pl.pallas_call(
        body, name=name,
        out_shape=[jax.ShapeDtypeStruct((8,) + p.shape, p.dtype) for p in parts],
        in_specs=[ANY] * n, out_specs=[ANY] * n,
        scratch_shapes=[pltpu.SemaphoreType.DMA((7 * n,)), pltpu.SemaphoreType.DMA((7 * n,)),
                        pltpu.SemaphoreType.DMA((n,))],
    )(*parts)


def _swap_halves_call(name, gs):
    n = len(gs)

    def body(*refs):
        g_refs, b_refs = refs[:n], refs[n:2 * n]
        send_sems, recv_sems = refs[2 * n:]
        x, y, c = _place()
        copies = [pltpu.make_async_remote_copy(
            src_ref=g_refs[t].at[1 - c], dst_ref=b_refs[t], send_sem=send_sems.at[t], recv_sem=recv_sems.at[t],
            device_id=(x, y, 1 - c), device_id_type=MESH) for t in range(n)]
        for cp in copies:
            cp.start()
        for cp in copies:
            cp.wait()

    return pl.pallas_call(
        body, name=name,
        out_shape=[jax.ShapeDtypeStruct(g.shape[1:], g.dtype) for g in gs],
        in_specs=[ANY] * n, out_specs=[ANY] * n,
        scratch_shapes=[pltpu.SemaphoreType.DMA((n,)), pltpu.SemaphoreType.DMA((n,))],
    )(*gs)


def _exchange_shards_call(name, ss):
    n = len(ss)

    def body(*refs):
        s_refs, b_refs = refs[:n], refs[n:2 * n]
        send_sems, recv_sems = refs[2 * n:]
        x, y, c = _place()
        chips = [(1 - x, y), (x, 1 - y), (1 - x, 1 - y)]
        copies = [pltpu.make_async_remote_copy(
            src_ref=s_refs[t].at[2 * chip[0] + chip[1]], dst_ref=b_refs[t].at[j], send_sem=send_sems.at[3 * t + j],
            recv_sem=recv_sems.at[3 * t + j], device_id=(*chip, c), device_id_type=MESH)
            for t in range(n) for j, chip in enumerate(chips)]
        for cp in copies:
            cp.start()
        for cp in copies:
            cp.wait()

    return pl.pallas_call(
        body, name=name,
        out_shape=[jax.ShapeDtypeStruct((3,) + s.shape[1:], s.dtype) for s in ss],
        in_specs=[ANY] * n, out_specs=[ANY] * n,
        scratch_shapes=[pltpu.SemaphoreType.DMA((3 * n,)), pltpu.SemaphoreType.DMA((3 * n,))],
    )(*ss)


def _join_halves_call(name, ts):
    n = len(ts)

    def body(*refs):
        t_refs, o_refs = refs[:n], refs[n:2 * n]
        send_sems, recv_sems, local_sems = refs[2 * n:]
        x, y, c = _place()
        mine = [pltpu.make_async_copy(t_refs[t], o_refs[t].at[c], local_sems.at[t]) for t in range(n)]
        for cp in mine:
            cp.start()
        copies = [pltpu.make_async_remote_copy(
            src_ref=t_refs[t], dst_ref=o_refs[t].at[c], send_sem=send_sems.at[t], recv_sem=recv_sems.at[t],
            device_id=(x, y, 1 - c), device_id_type=MESH) for t in range(n)]
        for cp in copies:
            cp.start()
        for t in range(n):
            copies[t].wait_send()
            pltpu.make_async_remote_copy(
                src_ref=t_refs[t], dst_ref=o_refs[t].at[1 - c], send_sem=send_sems.at[t], recv_sem=recv_sems.at[t],
                device_id=(x, y, 1 - c), device_id_type=MESH).wait_recv()
        for cp in mine:
            cp.wait()

    return pl.pallas_call(
        body, name=name,
        out_shape=[jax.ShapeDtypeStruct((2,) + t.shape, t.dtype) for t in ts],
        in_specs=[ANY] * n, out_specs=[ANY] * n,
        scratch_shapes=[pltpu.SemaphoreType.DMA((n,)), pltpu.SemaphoreType.DMA((n,)), pltpu.SemaphoreType.DMA((n,))],
    )(*ts)


def _row_block(rows, cols, budget):
    best = 8
    for rb in range(8, rows + 1, 8):
        if rows % rb == 0 and rb * cols * 4 <= budget:
            best = rb
    return best


def _add_half_call(name, g, b, cidx):
    _, _, r, cc = g.shape
    rb = _row_block(r, cc, 2 * 1024 * 1024)

    def body(c_ref, g_ref, b_ref, o_ref):
        o_ref[...] = g_ref[...] + b_ref[...]

    return pl.pallas_call(
        body, name=name,
        grid_spec=pltpu.PrefetchScalarGridSpec(
            num_scalar_prefetch=1, grid=(4, r // rb),
            in_specs=[pl.BlockSpec((None, None, rb, cc), lambda k, i, c_ref: (c_ref[0], k, i, 0)),
                      pl.BlockSpec((None, rb, cc), lambda k, i, c_ref: (k, i, 0))],
            out_specs=pl.BlockSpec((None, rb, cc), lambda k, i, c_ref: (k, i, 0))),
        out_shape=jax.ShapeDtypeStruct(b.shape, F32),
        compiler_params=_cparams(2),
    )(cidx, g, b)


def _add_chips_call(name, s, b, kidx):
    _, r, cc = s.shape
    rb = _row_block(r, cc, 2 * 1024 * 1024)

    def body(k_ref, s_ref, b0_ref, b1_ref, b2_ref, o_ref):
        o_ref[...] = ((s_ref[...] + b0_ref[...]) + b1_ref[...]) + b2_ref[...]

    def peer(j):
        return pl.BlockSpec((None, rb, cc), lambda i, k_ref: (j, i, 0))

    return pl.pallas_call(
        body, name=name,
        grid_spec=pltpu.PrefetchScalarGridSpec(
            num_scalar_prefetch=1, grid=(r // rb,),
            in_specs=[pl.BlockSpec((None, rb, cc), lambda i, k_ref: (k_ref[0], i, 0)), peer(0), peer(1), peer(2)],
            out_specs=pl.BlockSpec((rb, cc), lambda i, k_ref: (i, 0))),
        out_shape=jax.ShapeDtypeStruct((r, cc), F32),
        compiler_params=_cparams(1),
    )(kidx, s, b, b, b)


def _sum8_call(name, g):
    def body(g_ref, o_ref):
        acc = g_ref[0]
        for d in range(1, 8):
            acc = acc + g_ref[d]
        o_ref[...] = acc

    return pl.pallas_call(body, name=name, out_shape=jax.ShapeDtypeStruct(g.shape[1:], F32))(g)


def _adamw_call(name, w, g, m, v):
    r, cc = w.shape
    rb = _row_block(r, cc, 1024 * 1024) if r % 8 == 0 else r

    def body(w_ref, g_ref, m_ref, v_ref, d_ref, m2_ref, v2_ref):
        gv = g_ref[...]
        m2 = ADAM_B1 * m_ref[...] + (1.0 - ADAM_B1) * gv
        v2 = ADAM_B2 * v_ref[...] + (1.0 - ADAM_B2) * (gv * gv)
        m_hat = m2 / (1.0 - ADAM_B1 ** ADAM_STEP)
        v_hat = v2 / (1.0 - ADAM_B2 ** ADAM_STEP)
        d_ref[...] = -ADAM_LR * (m_hat / (jnp.sqrt(v_hat) + ADAM_EPS) + ADAM_WD * w_ref[...])
        m2_ref[...] = m2
        v2_ref[...] = v2

    spec = pl.BlockSpec((rb, cc), lambda i: (i, 0))
    return pl.pallas_call(
        body, name=name, grid=(r // rb,), in_specs=[spec] * 4, out_specs=[spec] * 3,
        out_shape=[jax.ShapeDtypeStruct((r, cc), F32)] * 3, compiler_params=_cparams(1),
    )(w, g, m, v)


SMALL = (("norm_gain", D_MODEL), ("b_gate", GLA_KW), ("ret_norm_gain", RET_W), ("gla_norm_gain", GLA_W),
         ("final_norm_gain", D_MODEL), ("w_gate_up", GATE_RANK * GLA_KW), ("meta_tokens", N_META * D_MODEL))


def _pack_rows(vecs, rows):
    flat = jnp.concatenate([v.reshape(-1) for v in vecs])
    return jnp.pad(flat, (0, rows * 128 - flat.shape[0])).reshape(rows, 128)


def kernel(x, meta_tokens, norm_gain, w_in, w_gate_up, b_gate, ret_norm_gain, gla_norm_gain, w_branch_ret, w_branch_gla, w_out, final_norm_gain, loss_target, m_meta_tokens, m_norm_gain, m_w_in, m_w_gate_up, m_b_gate, m_ret_norm_gain, m_gla_norm_gain, m_w_branch_ret, m_w_branch_gla, m_w_out, m_final_norm_gain, v_meta_tokens, v_norm_gain, v_w_in, v_w_gate_up, v_b_gate, v_ret_norm_gain, v_gla_norm_gain, v_w_branch_ret, v_w_branch_gla, v_w_out, v_final_norm_gain):
    xi, yi, ci = _place()
    kme = 2 * xi + yi
    cidx = jnp.reshape(ci, (1,)).astype(jnp.int32)
    kidx = jnp.reshape(kme, (1,)).astype(jnp.int32)
    sw_in = w_in.shape[2]

    def my_half(a, dtype):
        r, cc = a.shape
        return lax.dynamic_index_in_dim(a.reshape(2, r // 2, cc), ci, 0, keepdims=False).astype(dtype)

    g_in, g_br, g_bg, g_out, g_meta, g_wg = _gather8_call(
        "gather_weights",
        [my_half(w_in[0], BF16), my_half(w_branch_ret[0], BF16), my_half(w_branch_gla[0], BF16),
         my_half(w_out[0], BF16), my_half(meta_tokens, F32), my_half(w_gate_up[0], F32)])
    w_in_bf = g_in.reshape(4, 2, D_MODEL // 2, sw_in).transpose(1, 2, 0, 3).reshape(D_MODEL, 4 * sw_in)
    wbr = g_br.reshape(RET_W, D_MODEL)
    wbg = g_bg.reshape(GLA_W, D_MODEL)
    wout = g_out.reshape(D_MODEL, D_MODEL)
    meta = g_meta.reshape(4, 2, N_META // 2, D_MODEL // 4).transpose(1, 2, 0, 3).reshape(N_META, D_MODEL)
    wg_full = g_wg.reshape(4, 2, GATE_RANK // 2, GLA_KW // 4).transpose(1, 2, 0, 3).reshape(GATE_RANK, GLA_KW)

    loc = _local_step(x[0], loss_target[0], meta, norm_gain, w_in_bf, wg_full, b_gate, ret_norm_gain, gla_norm_gain,
                      wbr, wbg, wout, final_norm_gain)
    loss = lax.psum(loc["loss"], ("x", "y", "c"))

    g2 = [loc["w_in"].reshape(2, D_MODEL // 2, 4, sw_in).transpose(0, 2, 1, 3),
          loc["w_branch_ret"].reshape(4, 2, RET_W // 8, D_MODEL).transpose(1, 0, 2, 3),
          loc["w_branch_gla"].reshape(4, 2, GLA_W // 8, D_MODEL).transpose(1, 0, 2, 3),
          loc["w_out"].reshape(4, 2, D_MODEL // 8, D_MODEL).transpose(1, 0, 2, 3)]
    names = ("w_in", "w_branch_ret", "w_branch_gla", "w_out")
    from_sib = _swap_halves_call("swap_halves", g2)
    chip_sum = [_add_half_call("add_half_" + nm, g, b, cidx) for nm, g, b in zip(names, g2, from_sib)]
    from_chips = _exchange_shards_call("exchange_shards", chip_sum)
    mine = [_add_chips_call("add_chips_" + nm, s, b, kidx) for nm, s, b in zip(names, chip_sum, from_chips)]
    full = _join_halves_call("join_halves", mine)
    big_w = dict(w_in=w_in[0], w_branch_ret=w_branch_ret[0], w_branch_gla=w_branch_gla[0], w_out=w_out[0])
    big_m = dict(w_in=m_w_in[0], w_branch_ret=m_w_branch_ret[0], w_branch_gla=m_w_branch_gla[0], w_out=m_w_out[0])
    big_v = dict(w_in=v_w_in[0], w_branch_ret=v_w_branch_ret[0], w_branch_gla=v_w_branch_gla[0], w_out=v_w_out[0])
    grads, deltas, new_m, new_v = {}, {}, {}, {}
    for nm, f in zip(names, full):
        shape = big_w[nm].shape
        g = f.reshape(shape)
        d, m2, v2 = _adamw_call("adamw_" + nm, big_w[nm], g, big_m[nm], big_v[nm])
        grads[nm], deltas[nm], new_m[nm], new_v[nm] = (a.reshape((1,) + shape) for a in (g, d, m2, v2))

    small_g = dict(loc)
    small_g["meta_tokens"] = loc["dmeta"]
    n_small = sum(sz for _, sz in SMALL)
    rows = -(-n_small // 128 // 8) * 8
    (g_small,) = _gather8_call("gather_small_grads", [_pack_rows([small_g[nm] for nm, _ in SMALL], rows)])
    tot = _sum8_call("sum_small_grads", g_small).reshape(-1)
    off = 0
    sg = {}
    for nm, sz in SMALL:
        sg[nm] = tot[off:off + sz]
        off += sz
    sg["w_gate_up"] = lax.dynamic_slice_in_dim(sg["w_gate_up"].reshape(GATE_RANK, GLA_KW), kme * (GLA_KW // 4),
                                               GLA_KW // 4, axis=1)
    sg["meta_tokens"] = lax.dynamic_slice_in_dim(sg["meta_tokens"].reshape(N_META, D_MODEL), kme * (D_MODEL // 4),
                                                 D_MODEL // 4, axis=1)
    small_w = dict(norm_gain=norm_gain, b_gate=b_gate, ret_norm_gain=ret_norm_gain, gla_norm_gain=gla_norm_gain,
                   final_norm_gain=final_norm_gain, w_gate_up=w_gate_up, meta_tokens=meta_tokens)
    small_m = dict(norm_gain=m_norm_gain, b_gate=m_b_gate, ret_norm_gain=m_ret_norm_gain,
                   gla_norm_gain=m_gla_norm_gain, final_norm_gain=m_final_norm_gain, w_gate_up=m_w_gate_up,
                   meta_tokens=m_meta_tokens)
    small_v = dict(norm_gain=v_norm_gain, b_gate=v_b_gate, ret_norm_gain=v_ret_norm_gain,
                   gla_norm_gain=v_gla_norm_gain, final_norm_gain=v_final_norm_gain, w_gate_up=v_w_gate_up,
                   meta_tokens=v_meta_tokens)
    order = [nm for nm, _ in SMALL]
    sizes = [small_w[nm].size for nm in order]
    prow = -(-sum(sizes) // 128 // 8) * 8
    pk = lambda d: _pack_rows([d[nm] for nm in order], prow)
    d_s, m_s, v_s = _adamw_call("adamw_small", pk(small_w), pk(sg), pk(small_m), pk(small_v))
    off = 0
    for nm, sz in zip(order, sizes):
        shape = small_w[nm].shape
        grads[nm] = sg[nm].reshape(shape)
        deltas[nm], new_m[nm], new_v[nm] = (a.reshape(-1)[off:off + sz].reshape(shape) for a in (d_s, m_s, v_s))
        off += sz

    out_order = ("meta_tokens", "norm_gain", "w_in", "w_gate_up", "b_gate", "ret_norm_gain", "gla_norm_gain",
                 "w_branch_ret", "w_branch_gla", "w_out", "final_norm_gain")
    dx = loc["dx"].reshape(x.shape)
    return (loss, dx, *[grads[nm] for nm in out_order], *[deltas[nm] for nm in out_order],
            *[new_m[nm] for nm in out_order], *[new_v[nm] for nm in out_order])
```

```python
import functools
import math

import numpy as np
import jax
import jax.numpy as jnp
from jax import lax
from jax.experimental import pallas as pl
from jax.experimental.pallas import tpu as pltpu

F32 = jnp.float32
BF16 = jnp.bfloat16

D_MODEL = 1024
N_META = 16
EPS = 1e-6
ROPE_BASE = 10000.0
RET_HEADS, RET_QK, RET_V = 4, 256, 512
RET_W = RET_HEADS * RET_V
GLA_HEADS, GLA_K, GLA_V = 4, 128, 256
GLA_W = GLA_HEADS * GLA_V
GLA_KW = GLA_HEADS * GLA_K
GATE_RANK = 16
GATE_TAU = 16.0
GLA_SUB = 16

TM = 256
T0 = TM
PADF = T0 - N_META
GC = 128
TK = 768

W_R = 6144
W_G = 3088
W_GP = 3200
W_M = 2048
IN_COLS = W_R + W_G + W_M

ADAM_LR, ADAM_B1, ADAM_B2, ADAM_EPS, ADAM_WD, ADAM_STEP = 0.001, 0.9, 0.999, 1e-08, 0.01, 10

VMEM_LIMIT = 56 * 1024 * 1024

NN = ((1,), (0,))
NT = ((1,), (1,))
TN = ((0,), (0,))


def _dot(a, b, dims):
    return lax.dot_general(a, b, (dims, ((), ())), preferred_element_type=F32)


def _cparams(n_axes):
    return pltpu.CompilerParams(dimension_semantics=("arbitrary",) * n_axes, vmem_limit_bytes=VMEM_LIMIT)


def _sigmoid(x):
    return 1.0 / (1.0 + jnp.exp(-x))


def _split3(x):
    hi = x.astype(BF16)
    r1 = x - hi.astype(F32)
    mid = r1.astype(BF16)
    lo = (r1 - mid.astype(F32)).astype(BF16)
    return hi, mid, lo


def _exact_pm(p, x):
    hi, mid, lo = _split3(x)
    return _dot(p, hi, NN) + _dot(p, mid, NN) + _dot(p, lo, NN)


def _rms_call(h0, gain):
    tp = h0.shape[0]

    def body(h_ref, g_ref, u_ref):
        h = h_ref[...]
        r = lax.rsqrt(jnp.mean(h * h, axis=-1, keepdims=True) + EPS)
        u_ref[...] = (h * r * g_ref[...]).astype(BF16)

    return pl.pallas_call(
        body, name="rms_in", grid=(tp // TM,),
        in_specs=[pl.BlockSpec((TM, D_MODEL), lambda i: (i, 0)), pl.BlockSpec((1, D_MODEL), lambda i: (0, 0))],
        out_specs=pl.BlockSpec((TM, D_MODEL), lambda i: (i, 0)),
        out_shape=jax.ShapeDtypeStruct((tp, D_MODEL), BF16),
        compiler_params=_cparams(1),
    )(h0, gain)


def _mm_nn(name, a, b, out_dtype, tn, col0, ncols, epilogue=None, extras=(), extra_specs=()):
    m, k = a.shape
    nj, j0 = ncols // tn, col0 // tn

    def body(a_ref, b_ref, *rest):
        *ex, o_ref = rest
        acc = _dot(a_ref[...], b_ref[...], NN)
        if epilogue is None:
            o_ref[...] = acc.astype(out_dtype)
        else:
            epilogue(acc, o_ref, *ex)

    return pl.pallas_call(
        body, name=name, grid=(nj, m // TM),
        in_specs=[pl.BlockSpec((TM, k), lambda j, i: (i, 0)), pl.BlockSpec((k, tn), lambda j, i: (0, j0 + j))]
        + list(extra_specs),
        out_specs=pl.BlockSpec((TM, tn), lambda j, i: (i, j)),
        out_shape=jax.ShapeDtypeStruct((m, ncols), out_dtype),
        compiler_params=_cparams(2),
    )(a, b, *extras)


def _rope_epilogue(acc, o_ref, cos_ref, sin_ref):
    scale = jnp.where(pl.program_id(0) == 1, RET_QK ** -0.5, 1.0).astype(F32)
    cos, sin = cos_ref[...], sin_ref[...]
    half = RET_QK // 2
    for h in range(RET_HEADS):
        t1 = acc[:, h * RET_QK:h * RET_QK + half]
        t2 = acc[:, h * RET_QK + half:(h + 1) * RET_QK]
        o_ref[:, h * RET_QK:h * RET_QK + half] = ((t1 * cos - t2 * sin) * scale).astype(BF16)
        o_ref[:, h * RET_QK + half:(h + 1) * RET_QK] = ((t2 * cos + t1 * sin) * scale).astype(BF16)


def _gqk_epilogue(acc, o_ref):
    o_ref[:, :GLA_KW] = acc[:, :GLA_KW] * (GLA_K ** -0.5)
    o_ref[:, GLA_KW:] = acc[:, GLA_KW:]


def _exchange_copies(s_refs, b_refs, send_sems, recv_sems):
    x, y, c = _place()
    chips = [(1 - x, y), (x, 1 - y), (1 - x, 1 - y)]
    return [pltpu.make_async_remote_copy(
        src_ref=s_refs[t].at[2 * chip[0] + chip[1]], dst_ref=b_refs[t].at[j], send_sem=send_sems.at[3 * t + j],
        recv_sem=recv_sems.at[3 * t + j], device_id=(*chip, c), device_id_type=MESH)
        for t in range(len(s_refs)) for j, chip in enumerate(chips)]


def _exchange_shapes(ss):
    return ([jax.ShapeDtypeStruct((3,) + s.shape[1:], s.dtype) for s in ss],
            [pltpu.SemaphoreType.DMA((3 * len(ss),)), pltpu.SemaphoreType.DMA((3 * len(ss),))])


def _mm_nt_acc(name, a, w, tk, acc_in=None, epilogue=None, extras=(), extra_specs=(), extra_out_shapes=(),
               extra_out_specs=(), exchange=()):
    m, k = a.shape
    n = w.shape[0]
    nk, ni = k // tk, m // TM
    has_acc = acc_in is not None
    n_xc = len(exchange)

    def body(*refs):
        a_ref, w_ref = refs[0], refs[1]
        pos = 2
        acc_ref = None
        if has_acc:
            acc_ref = refs[pos]
            pos += 1
        ex = refs[pos:pos + len(extras)]
        pos += len(extras)
        xc_src = refs[pos:pos + n_xc]
        pos += n_xc
        n_scr = 3 if n_xc else 1
        outs = refs[pos:len(refs) - n_scr - n_xc]
        xc_dst = refs[len(refs) - n_scr - n_xc:len(refs) - n_scr]
        scr = refs[len(refs) - n_scr]
        i, kk = pl.program_id(0), pl.program_id(1)
        if n_xc:
            copies = _exchange_copies(xc_src, xc_dst, refs[-2], refs[-1])

            @pl.when((i == 0) & (kk == 0))
            def _():
                for cp in copies:
                    cp.start()

        @pl.when(kk == 0)
        def _():
            scr[...] = acc_ref[...] if has_acc else jnp.zeros_like(scr)

        scr[...] += _dot(a_ref[...], w_ref[...], NT)

        @pl.when(kk == nk - 1)
        def _():
            if epilogue is None:
                outs[0][...] = scr[...]
            else:
                epilogue(scr[...], outs, i, *ex)

        if n_xc:
            @pl.when((i == ni - 1) & (kk == nk - 1))
            def _():
                for cp in copies:
                    cp.wait()

    in_specs = [pl.BlockSpec((TM, tk), lambda i, kk: (i, kk)), pl.BlockSpec((n, tk), lambda i, kk: (0, kk))]
    args = [a, w]
    if has_acc:
        in_specs.append(pl.BlockSpec((TM, n), lambda i, kk: (i, 0)))
        args.append(acc_in)
    in_specs += list(extra_specs) + [ANY] * n_xc
    args += list(extras) + list(exchange)
    if epilogue is None:
        out_shape = [jax.ShapeDtypeStruct((m, n), F32)]
        out_specs = [pl.BlockSpec((TM, n), lambda i, kk: (i, 0))]
    else:
        out_shape, out_specs = list(extra_out_shapes), list(extra_out_specs)
    scratch = [pltpu.VMEM((TM, n), F32)]
    if n_xc:
        xc_shapes, xc_sems = _exchange_shapes(exchange)
        out_shape += xc_shapes
        out_specs += [ANY] * n_xc
        scratch += xc_sems
    return pl.pallas_call(
        body, name=name, grid=(ni, nk), in_specs=in_specs, out_specs=out_specs, out_shape=out_shape,
        scratch_shapes=scratch, compiler_params=_cparams(2),
    )(*args)


def _rms_bwd_epilogue(du, outs, i, h_ref, g_ref, dh1_ref):
    dx_ref, dmeta_ref, dg_ref = outs
    h = h_ref[...]
    r = lax.rsqrt(jnp.mean(h * h, axis=-1, keepdims=True) + EPS)
    xh = h * r
    dxh = du * g_ref[...]
    dh0 = dh1_ref[...] + r * (dxh - xh * jnp.mean(dxh * xh, axis=-1, keepdims=True))

    @pl.when(i == 0)
    def _():
        dg_ref[...] = jnp.zeros_like(dg_ref)
        dmeta_ref[...] = dh0[PADF:, :]

    @pl.when(i > 0)
    def _():
        dx_ref[...] = dh0

    dg_ref[...] += jnp.sum(du * xh, axis=0, keepdims=True)


def _mm_tn(name, a, b, bn):
    t, m = a.shape
    n = b.shape[1]

    def body(a_ref, b_ref, o_ref):
        @pl.when(pl.program_id(1) == 0)
        def _():
            o_ref[...] = jnp.zeros_like(o_ref)

        o_ref[...] += _dot(a_ref[...], b_ref[...], TN)

    return pl.pallas_call(
        body, name=name, grid=(n // bn, t // TK),
        in_specs=[pl.BlockSpec((TK, m), lambda j, kk: (kk, 0)), pl.BlockSpec((TK, bn), lambda j, kk: (kk, j))],
        out_specs=pl.BlockSpec((m, bn), lambda j, kk: (0, j)),
        out_shape=jax.ShapeDtypeStruct((m, n), F32),
        compiler_params=_cparams(2),
    )(a, b)


def _ret_consts(lg):
    c = TM
    ii = lax.broadcasted_iota(jnp.int32, (c, c), 0)
    jj = lax.broadcasted_iota(jnp.int32, (c, c), 1)
    rel = (ii - jj).astype(F32)
    dm = jnp.where(rel >= 0, jnp.exp(jnp.maximum(rel, 0.0) * lg), 0.0)
    idx = lax.broadcasted_iota(jnp.int32, (c, 1), 0).astype(F32)
    xi = jnp.exp((idx + 1.0) * lg)
    zeta = jnp.exp((c - 1.0 - idx) * lg)
    gc = jnp.exp(jnp.full((1, 1), c, F32) * lg)
    return dm, xi, zeta, gc


def _ret_fwd_call(rqk, rv, rg, gain, lgam):
    tp = rqk.shape[0]
    nc = tp // TM

    def body(lg_ref, qk_ref, v_ref, rg_ref, g_ref, o_ref, a_ref, st_ref, s_scr):
        @pl.when(pl.program_id(0) == 0)
        def _():
            s_scr[...] = jnp.zeros_like(s_scr)

        for h in range(RET_HEADS):
            dm, xi, zeta, gc = _ret_consts(lg_ref[h])
            q = qk_ref[:, h * RET_QK:(h + 1) * RET_QK]
            k = qk_ref[:, D_MODEL + h * RET_QK:D_MODEL + (h + 1) * RET_QK]
            v = v_ref[:, h * RET_V:(h + 1) * RET_V]
            sb = s_scr[h].astype(BF16)
            st_ref[0, h] = sb
            s = _dot(q, k, NT) * dm
            o = _dot(s.astype(BF16), v, NN) + xi * _dot(q, sb, NN)
            kz = (k.astype(F32) * zeta).astype(BF16)
            s_scr[h] = gc * s_scr[h] + _dot(kz, v, TN)
            o_ref[:, h * RET_V:(h + 1) * RET_V] = o
            mu = jnp.mean(o, axis=-1, keepdims=True)
            xc = o - mu
            xh = xc * lax.rsqrt(jnp.mean(xc * xc, axis=-1, keepdims=True) + EPS)
            g = rg_ref[:, h * RET_V:(h + 1) * RET_V]
            a_ref[:, h * RET_V:(h + 1) * RET_V] = (
                xh * g_ref[:, h * RET_V:(h + 1) * RET_V] * (g * _sigmoid(g))).astype(BF16)

    return pl.pallas_call(
        body, name="ret_fwd", grid=(nc,),
        in_specs=[pl.BlockSpec(memory_space=pltpu.SMEM),
                  pl.BlockSpec((TM, 2 * D_MODEL), lambda n: (n, 0)),
                  pl.BlockSpec((TM, RET_W), lambda n: (n, 0)),
                  pl.BlockSpec((TM, RET_W), lambda n: (n, 0)),
                  pl.BlockSpec((1, RET_W), lambda n: (0, 0))],
        out_specs=[pl.BlockSpec((TM, RET_W), lambda n: (n, 0)),
                   pl.BlockSpec((TM, RET_W), lambda n: (n, 0)),
                   pl.BlockSpec((1, RET_HEADS, RET_QK, RET_V), lambda n: (n, 0, 0, 0))],
        out_shape=[jax.ShapeDtypeStruct((tp, RET_W), F32), jax.ShapeDtypeStruct((tp, RET_W), BF16),
                   jax.ShapeDtypeStruct((nc, RET_HEADS, RET_QK, RET_V), BF16)],
        scratch_shapes=[pltpu.VMEM((RET_HEADS, RET_QK, RET_V), F32)],
        compiler_params=_cparams(1),
    )(lgam, rqk, rv, rg, gain)


def _ret_bwd_call(rqk, rv, rg, o_ret, da, states, gain, lgam, cos, sin):
    tp = rqk.shape[0]
    nc = tp // TM
    half = RET_QK // 2

    def body(lg_ref, qk_ref, v_ref, rg_ref, o_ref, da_ref, st_ref, g_ref, cos_ref, sin_ref, dp_ref, dg_ref, ds_scr):
        @pl.when(pl.program_id(0) == 0)
        def _():
            ds_scr[...] = jnp.zeros_like(ds_scr)
            dg_ref[...] = jnp.zeros_like(dg_ref)

        cos, sin = cos_ref[...], sin_ref[...]
        for h in range(RET_HEADS):
            hs = slice(h * RET_V, (h + 1) * RET_V)
            dm, xi, zeta, gc = _ret_consts(lg_ref[h])
            o = o_ref[:, hs]
            mu = jnp.mean(o, axis=-1, keepdims=True)
            xc = o - mu
            rstd = lax.rsqrt(jnp.mean(xc * xc, axis=-1, keepdims=True) + EPS)
            xh = xc * rstd
            gain_h = g_ref[:, hs]
            g = rg_ref[:, hs]
            sg = _sigmoid(g)
            silu = g * sg
            dah = da_ref[:, hs]
            dp_ref[:, 4 * D_MODEL + h * RET_V:4 * D_MODEL + (h + 1) * RET_V] = (
                dah * (xh * gain_h) * (sg * (1.0 + g * (1.0 - sg)))).astype(BF16)
            dn = dah * silu
            dg_ref[:, hs] += jnp.sum(dn * xh, axis=0, keepdims=True)
            dxh = dn * gain_h
            do = rstd * (dxh - jnp.mean(dxh, axis=-1, keepdims=True)
                         - xh * jnp.mean(dxh * xh, axis=-1, keepdims=True))
            dob = do.astype(BF16)
            q = qk_ref[:, h * RET_QK:(h + 1) * RET_QK]
            k = qk_ref[:, D_MODEL + h * RET_QK:D_MODEL + (h + 1) * RET_QK]
            v = v_ref[:, hs]
            sp = st_ref[0, h]
            ds = ds_scr[h]
            dsb = ds.astype(BF16)
            s = (_dot(q, k, NT) * dm).astype(BF16)
            dsc = (_dot(dob, v, NT) * dm).astype(BF16)
            dq = _dot(dsc, k, NN) + xi * _dot(dob, sp, NT)
            dk = _dot(dsc, q, TN) + zeta * _dot(v, dsb, NT)
            kz = (k.astype(F32) * zeta).astype(BF16)
            dv = _dot(s, dob, TN) + _dot(kz, dsb, NN)
            qx = (q.astype(F32) * xi).astype(BF16)
            ds_scr[h] = gc * ds + _dot(qx, dob, TN)
            dp_ref[:, 2 * D_MODEL + h * RET_V:2 * D_MODEL + (h + 1) * RET_V] = dv.astype(BF16)
            dk = dk * (RET_QK ** -0.5)
            for base, t in ((0, dq), (D_MODEL, dk)):
                t1, t2 = t[:, :half], t[:, half:]
                dp_ref[:, base + h * RET_QK:base + h * RET_QK + half] = (t1 * cos + t2 * sin).astype(BF16)
                dp_ref[:, base + h * RET_QK + half:base + (h + 1) * RET_QK] = (t2 * cos - t1 * sin).astype(BF16)

    rev = lambda n: (nc - 1 - n, 0)
    return pl.pallas_call(
        body, name="ret_bwd", grid=(nc,),
        in_specs=[pl.BlockSpec(memory_space=pltpu.SMEM),
                  pl.BlockSpec((TM, 2 * D_MODEL), rev),
                  pl.BlockSpec((TM, RET_W), rev),
                  pl.BlockSpec((TM, RET_W), rev),
                  pl.BlockSpec((TM, RET_W), rev),
                  pl.BlockSpec((TM, RET_W), rev),
                  pl.BlockSpec((1, RET_HEADS, RET_QK, RET_V), lambda n: (nc - 1 - n, 0, 0, 0)),
                  pl.BlockSpec((1, RET_W), lambda n: (0, 0)),
                  pl.BlockSpec((TM, half), rev),
                  pl.BlockSpec((TM, half), rev)],
        out_specs=[pl.BlockSpec((TM, W_R), rev), pl.BlockSpec((1, RET_W), lambda n: (0, 0))],
        out_shape=[jax.ShapeDtypeStruct((tp, W_R), BF16), jax.ShapeDtypeStruct((1, RET_W), F32)],
        scratch_shapes=[pltpu.VMEM((RET_HEADS, RET_QK, RET_V), F32)],
        compiler_params=_cparams(1),
    )(lgam, rqk, rv, rg, o_ret, da, states, gain, cos, sin)


GLA_LEVELS = tuple(GC >> (s + 1) for s in range(int(math.log2(GC // GLA_SUB))))
NLEV = len(GLA_LEVELS)
NBLK = 2 * NLEV + 3
P_ROWS = NBLK * GC + 8


def _gla_p_matrix():
    c = GC
    i = np.arange(c)[:, None]
    r = np.arange(c)[None, :]
    blocks = []
    for m in GLA_LEVELS:
        second = (i & m) != 0
        ref = (i // (2 * m)) * 2 * m + m - 1
        blocks.append(second & (r > ref) & (r <= i))
    for m in GLA_LEVELS:
        second = (i & m) != 0
        ref = (i // (2 * m)) * 2 * m + m - 1
        blocks.append((~second) & (r > i) & (r <= ref))
    blocks.append((r >= (i // GLA_SUB) * GLA_SUB) & (r <= i))
    blocks.append(r <= i)
    blocks.append(r > i)
    blocks.append(np.ones((8, c), bool))
    return np.concatenate([b.astype(np.float32) for b in blocks], axis=0)


def _gla_masks():
    ii = lax.broadcasted_iota(jnp.int32, (GC, GC), 0)
    jj = lax.broadcasted_iota(jnp.int32, (GC, GC), 1)
    masks = []
    for m in GLA_LEVELS:
        sh = int(math.log2(2 * m))
        masks.append(((ii >> sh) == (jj >> sh)) & ((ii & m) != 0) & ((jj & m) == 0))
    sh = int(math.log2(GLA_SUB))
    md = ((ii >> sh) == (jj >> sh)) & (jj <= ii)
    row = lax.broadcasted_iota(jnp.int32, (GC, 1), 0)
    second = [(row & m) != 0 for m in GLA_LEVELS]
    return masks, md, second


def _gla_log_decay(glr_ref, wg_ref, bg_ref):
    z = _dot(glr_ref[...].astype(BF16), wg_ref[...], NN) + bg_ref[...]
    la = (jnp.minimum(z, 0.0) - jnp.log1p(jnp.exp(-jnp.abs(z)))) * (1.0 / GATE_TAU)
    return z, la


def _gla_factors(e, h, second):
    cs = slice(h * GLA_K, (h + 1) * GLA_K)
    blk = lambda b: e[b * GC:(b + 1) * GC, cs]
    fq = [jnp.where(second[l], jnp.exp(blk(l)), 0.0) for l in range(NLEV)]
    fk = [jnp.where(second[l], 0.0, jnp.exp(blk(NLEV + l))) for l in range(NLEV)]
    ed = jnp.exp(blk(2 * NLEV))
    edi = jnp.exp(-blk(2 * NLEV))
    eb = jnp.exp(blk(2 * NLEV + 1))
    ee = jnp.exp(blk(2 * NLEV + 2))
    ebl = jnp.exp(e[NBLK * GC:NBLK * GC + 1, cs])
    return fq, fk, ed, edi, eb, ee, ebl


def _gla_scores(q, k, fq, fk, ed, edi, masks, md):
    qt = [(q * f).astype(BF16) for f in fq]
    kt = [(k * f).astype(BF16) for f in fk]
    qd = (q * ed).astype(BF16)
    kd = (k * edi).astype(BF16)
    a = jnp.where(md, _dot(qd, kd, NT), 0.0)
    for l in range(NLEV):
        a = a + jnp.where(masks[l], _dot(qt[l], kt[l], NT), 0.0)
    return a, qt, kt, qd, kd


def _gla_fwd_call(gqk, gv, glr, gg, wg, bg, gain, pmat):
    tp = gqk.shape[0]
    nc = tp // GC

    def body(qk_ref, v_ref, glr_ref, gg_ref, wg_ref, bg_ref, g_ref, p_ref, o_ref, a_ref, st_ref, s_scr):
        @pl.when(pl.program_id(0) == 0)
        def _():
            s_scr[...] = jnp.zeros_like(s_scr)

        _, la = _gla_log_decay(glr_ref, wg_ref, bg_ref)
        e = _exact_pm(p_ref[...], la)
        masks, md, second = _gla_masks()
        for h in range(GLA_HEADS):
            q = qk_ref[:, h * GLA_K:(h + 1) * GLA_K]
            k = qk_ref[:, GLA_KW + h * GLA_K:GLA_KW + (h + 1) * GLA_K]
            vs = slice(h * GLA_V, (h + 1) * GLA_V)
            v = v_ref[:, vs]
            fq, fk, ed, edi, eb, ee, ebl = _gla_factors(e, h, second)
            a, *_ = _gla_scores(q, k, fq, fk, ed, edi, masks, md)
            sb = s_scr[h].astype(BF16)
            st_ref[0, h] = sb
            o = _dot(a.astype(BF16), v, NN) + _dot((q * eb).astype(BF16), sb, NT)
            s_scr[h] = s_scr[h] * ebl + _dot(v, (k * ee).astype(BF16), TN)
            o_ref[:, vs] = o
            xh = o * lax.rsqrt(jnp.mean(o * o, axis=-1, keepdims=True) + EPS)
            g = gg_ref[:, vs]
            a_ref[:, vs] = (xh * g_ref[:, vs] * (g * _sigmoid(g))).astype(BF16)

    return pl.pallas_call(
        body, name="gla_fwd", grid=(nc,),
        in_specs=[pl.BlockSpec((GC, 2 * GLA_KW), lambda n: (n, 0)),
                  pl.BlockSpec((GC, GLA_W), lambda n: (n, 0)),
                  pl.BlockSpec((GC, 128), lambda n: (n, 0)),
                  pl.BlockSpec((GC, GLA_W), lambda n: (n, 0)),
                  pl.BlockSpec((128, GLA_KW), lambda n: (0, 0)),
                  pl.BlockSpec((1, GLA_KW), lambda n: (0, 0)),
                  pl.BlockSpec((1, GLA_W), lambda n: (0, 0)),
                  pl.BlockSpec((P_ROWS, GC), lambda n: (0, 0))],
        out_specs=[pl.BlockSpec((GC, GLA_W), lambda n: (n, 0)),
                   pl.BlockSpec((GC, GLA_W), lambda n: (n, 0)),
                   pl.BlockSpec((1, GLA_HEADS, GLA_V, GLA_K), lambda n: (n, 0, 0, 0))],
        out_shape=[jax.ShapeDtypeStruct((tp, GLA_W), F32), jax.ShapeDtypeStruct((tp, GLA_W), BF16),
                   jax.ShapeDtypeStruct((nc, GLA_HEADS, GLA_V, GLA_K), BF16)],
        scratch_shapes=[pltpu.VMEM((GLA_HEADS, GLA_V, GLA_K), F32)],
        compiler_params=_cparams(1),
    )(gqk, gv, glr, gg, wg, bg, gain, pmat)


def _gla_bwd_call(gqk, gv, glr, gg, o_gla, da, states, wg, bg, gain, pmat, pmat_t, exchange=()):
    tp = gqk.shape[0]
    nc = tp // GC
    o_gv, o_gg, o_lr = 2 * GLA_KW, 2 * GLA_KW + GLA_W, 2 * GLA_KW + 2 * GLA_W
    n_xc = len(exchange)

    def body(qk_ref, v_ref, glr_ref, gg_ref, o_ref, da_ref, st_ref, wg_ref, bg_ref, g_ref, p_ref, pt_ref, *rest):
        xc_src = rest[:n_xc]
        dp_ref, dwg_ref, dbg_ref, dg_ref = rest[n_xc:n_xc + 4]
        xc_dst = rest[n_xc + 4:2 * n_xc + 4]
        ds_scr, de_scr = rest[2 * n_xc + 4:2 * n_xc + 6]
        n = pl.program_id(0)
        if n_xc:
            copies = _exchange_copies(xc_src, xc_dst, rest[-2], rest[-1])

            @pl.when(n == 0)
            def _():
                for cp in copies:
                    cp.start()

            @pl.when(n == nc - 1)
            def _():
                for cp in copies:
                    cp.wait()

        @pl.when(n == 0)
        def _():
            ds_scr[...] = jnp.zeros_like(ds_scr)
            dwg_ref[...] = jnp.zeros_like(dwg_ref)
            dbg_ref[...] = jnp.zeros_like(dbg_ref)
            dg_ref[...] = jnp.zeros_like(dg_ref)
            de_scr[...] = jnp.zeros_like(de_scr)

        z, la = _gla_log_decay(glr_ref, wg_ref, bg_ref)
        e = _exact_pm(p_ref[...], la)
        masks, md, second = _gla_masks()
        for h in range(GLA_HEADS):
            cs = slice(h * GLA_K, (h + 1) * GLA_K)
            vs = slice(h * GLA_V, (h + 1) * GLA_V)
            o = o_ref[:, vs]
            rstd = lax.rsqrt(jnp.mean(o * o, axis=-1, keepdims=True) + EPS)
            xh = o * rstd
            gain_h = g_ref[:, vs]
            g = gg_ref[:, vs]
            sg = _sigmoid(g)
            dah = da_ref[:, vs]
            dp_ref[:, o_gg + h * GLA_V:o_gg + (h + 1) * GLA_V] = (
                dah * (xh * gain_h) * (sg * (1.0 + g * (1.0 - sg)))).astype(BF16)
            dn = dah * (g * sg)
            dg_ref[:, vs] += jnp.sum(dn * xh, axis=0, keepdims=True)
            dxh = dn * gain_h
            do = rstd * (dxh - xh * jnp.mean(dxh * xh, axis=-1, keepdims=True))
            dob = do.astype(BF16)
            q = qk_ref[:, cs]
            k = qk_ref[:, GLA_KW + h * GLA_K:GLA_KW + (h + 1) * GLA_K]
            v = v_ref[:, vs]
            fq, fk, ed, edi, eb, ee, ebl = _gla_factors(e, h, second)
            a, qt, kt, qd, kd = _gla_scores(q, k, fq, fk, ed, edi, masks, md)
            sp = st_ref[0, h]
            ds = ds_scr[h]
            dsb = ds.astype(BF16)
            q_in = q * eb
            k_end = k * ee
            da_s = _dot(dob, v, NT)
            dv = _dot(a.astype(BF16), dob, TN) + _dot(k_end.astype(BF16), dsb, NT)
            dq_in = _dot(dob, sp, NN)
            dk_end = _dot(v, dsb, NN)
            dbl = jnp.sum(sp.astype(F32) * ds, axis=0, keepdims=True) * ebl
            ds_scr[h] = ds * ebl + _dot(dob, q_in.astype(BF16), TN)
            dq = dq_in * eb
            dk = dk_end * ee
            de_scr[(2 * NLEV + 1) * GC:(2 * NLEV + 2) * GC, cs] = dq_in * q_in
            de_scr[(2 * NLEV + 2) * GC:(2 * NLEV + 3) * GC, cs] = dk_end * k_end
            de_scr[NBLK * GC:NBLK * GC + 1, cs] = dbl
            for l in range(NLEV):
                dal = jnp.where(masks[l], da_s, 0.0).astype(BF16)
                dqt = _dot(dal, kt[l], NN)
                dkt = _dot(dal, qt[l], TN)
                dq = dq + dqt * fq[l]
                dk = dk + dkt * fk[l]
                de_scr[l * GC:(l + 1) * GC, cs] = dqt * (q * fq[l])
                de_scr[(NLEV + l) * GC:(NLEV + l + 1) * GC, cs] = dkt * (k * fk[l])
            dad = jnp.where(md, da_s, 0.0).astype(BF16)
            dqd = _dot(dad, kd, NN)
            dkd = _dot(dad, qd, TN)
            dq = dq + dqd * ed
            dk = dk + dkd * edi
            de_scr[2 * NLEV * GC:(2 * NLEV + 1) * GC, cs] = dqd * (q * ed) - dkd * (k * edi)
            dp_ref[:, cs] = (dq * (GLA_K ** -0.5)).astype(BF16)
            dp_ref[:, GLA_KW + h * GLA_K:GLA_KW + (h + 1) * GLA_K] = dk.astype(BF16)
            dp_ref[:, o_gv + h * GLA_V:o_gv + (h + 1) * GLA_V] = dv.astype(BF16)
        dla = _exact_pm(pt_ref[...], de_scr[...])
        row = (nc - 1 - n) * GC + lax.broadcasted_iota(jnp.int32, (GC, 1), 0)
        dz = jnp.where(row >= PADF, dla * (1.0 / GATE_TAU) * _sigmoid(-z), 0.0)
        dzb = dz.astype(BF16)
        dp_ref[:, o_lr:] = _dot(dzb, wg_ref[...], NT).astype(BF16)
        dwg_ref[...] += _dot(glr_ref[...].astype(BF16), dzb, TN)
        dbg_ref[...] += jnp.sum(dz, axis=0, keepdims=True)

    rev = lambda n: (nc - 1 - n, 0)
    const = lambda n: (0, 0)
    xc_shapes, xc_sems = _exchange_shapes(exchange) if n_xc else ([], [])
    return pl.pallas_call(
        body, name="gla_bwd", grid=(nc,),
        in_specs=[pl.BlockSpec((GC, 2 * GLA_KW), rev),
                  pl.BlockSpec((GC, GLA_W), rev),
                  pl.BlockSpec((GC, 128), rev),
                  pl.BlockSpec((GC, GLA_W), rev),
                  pl.BlockSpec((GC, GLA_W), rev),
                  pl.BlockSpec((GC, GLA_W), rev),
                  pl.BlockSpec((1, GLA_HEADS, GLA_V, GLA_K), lambda n: (nc - 1 - n, 0, 0, 0)),
                  pl.BlockSpec((128, GLA_KW), const),
                  pl.BlockSpec((1, GLA_KW), const),
                  pl.BlockSpec((1, GLA_W), const),
                  pl.BlockSpec((P_ROWS, GC), const),
                  pl.BlockSpec((GC, P_ROWS), const)] + [ANY] * n_xc,
        out_specs=[pl.BlockSpec((GC, W_GP), rev), pl.BlockSpec((128, GLA_KW), const),
                   pl.BlockSpec((1, GLA_KW), const), pl.BlockSpec((1, GLA_W), const)] + [ANY] * n_xc,
        out_shape=[jax.ShapeDtypeStruct((tp, W_GP), BF16), jax.ShapeDtypeStruct((128, GLA_KW), F32),
                   jax.ShapeDtypeStruct((1, GLA_KW), F32), jax.ShapeDtypeStruct((1, GLA_W), F32)] + xc_shapes,
        scratch_shapes=[pltpu.VMEM((GLA_HEADS, GLA_V, GLA_K), F32), pltpu.VMEM((P_ROWS, GLA_KW), F32)] + xc_sems,
        compiler_params=_cparams(1),
    )(gqk, gv, glr, gg, o_gla, da, states, wg, bg, gain, pmat, pmat_t, *exchange)


def _mid_call(a_ret, a_gla, mg, h0, tgt, wbr, wbg, wout, gf):
    tp = h0.shape[0]
    nt = tp // TM

    def body(ar_ref, ag_ref, mg_ref, h_ref, t_ref, wbr_ref, wbg_ref, wo_ref, gf_ref,
             dh1_ref, dar_ref, dag_ref, dm_ref, mb_ref, dh1b_ref, dprb_ref, dpgb_ref, loss_ref, dgf_ref):
        i = pl.program_id(0)

        @pl.when(i == 0)
        def _():
            loss_ref[...] = jnp.zeros_like(loss_ref)
            dgf_ref[...] = jnp.zeros_like(dgf_ref)

        ar, ag = ar_ref[...], ag_ref[...]
        pr = _dot(ar, wbr_ref[...], NN)
        pg = _dot(ag, wbg_ref[...], NN)
        sr = _sigmoid(mg_ref[:, :D_MODEL])
        sg = _sigmoid(mg_ref[:, D_MODEL:])
        merged = (sr * pr + sg * pg).astype(BF16)
        mb_ref[...] = merged
        h1 = h_ref[...] + _dot(merged, wo_ref[...], NN)
        r1 = lax.rsqrt(jnp.mean(h1 * h1, axis=-1, keepdims=True) + EPS)
        xh = h1 * r1
        gfv = gf_ref[...]
        live = jnp.where(i > 0, 1.0, 0.0).astype(F32)
        err = (xh * gfv - t_ref[...]) * live
        loss_ref[...] += jnp.full(loss_ref.shape, 0.5 / D_MODEL, F32) * jnp.sum(err * err)
        dy = err * (1.0 / D_MODEL)
        dgf_ref[...] += jnp.sum(dy * xh, axis=0, keepdims=True)
        dxh = dy * gfv
        dh1 = r1 * (dxh - xh * jnp.mean(dxh * xh, axis=-1, keepdims=True))
        dh1_ref[...] = dh1
        dh1b = dh1.astype(BF16)
        dh1b_ref[...] = dh1b
        dmerged = _dot(dh1b, wo_ref[...], NT)
        dm_ref[:, :D_MODEL] = (dmerged * pr * sr * (1.0 - sr)).astype(BF16)
        dm_ref[:, D_MODEL:] = (dmerged * pg * sg * (1.0 - sg)).astype(BF16)
        dpr = (dmerged * sr).astype(BF16)
        dpg = (dmerged * sg).astype(BF16)
        dprb_ref[...] = dpr
        dpgb_ref[...] = dpg
        dar_ref[...] = _dot(dpr, wbr_ref[...], NT)
        dag_ref[...] = _dot(dpg, wbg_ref[...], NT)

    tile = lambda w: pl.BlockSpec((TM, w), lambda i: (i, 0))
    const = lambda r, w: pl.BlockSpec((r, w), lambda i: (0, 0))
    return pl.pallas_call(
        body, name="merge_out_loss", grid=(nt,),
        in_specs=[tile(RET_W), tile(GLA_W), tile(W_M), tile(D_MODEL),
                  pl.BlockSpec((TM, D_MODEL), lambda i: (jnp.maximum(i - 1, 0), 0)),
                  const(RET_W, D_MODEL), const(GLA_W, D_MODEL), const(D_MODEL, D_MODEL), const(1, D_MODEL)],
        out_specs=[tile(D_MODEL), tile(RET_W), tile(GLA_W), tile(W_M), tile(D_MODEL), tile(D_MODEL), tile(D_MODEL),
                   tile(D_MODEL), const(1, 128), const(1, D_MODEL)],
        out_shape=[jax.ShapeDtypeStruct((tp, D_MODEL), F32), jax.ShapeDtypeStruct((tp, RET_W), F32),
                   jax.ShapeDtypeStruct((tp, GLA_W), F32), jax.ShapeDtypeStruct((tp, W_M), BF16),
                   jax.ShapeDtypeStruct((tp, D_MODEL), BF16), jax.ShapeDtypeStruct((tp, D_MODEL), BF16),
                   jax.ShapeDtypeStruct((tp, D_MODEL), BF16), jax.ShapeDtypeStruct((tp, D_MODEL), BF16),
                   jax.ShapeDtypeStruct((1, 128), F32), jax.ShapeDtypeStruct((1, D_MODEL), F32)],
        compiler_params=_cparams(1),
    )(a_ret, a_gla, mg, h0, tgt, wbr, wbg, wout, gf)


def _device_step(x2d, tgt2d, meta, norm_gain, w_in_bf, w_gate_up, b_gate, ret_gain, gla_gain, wbr, wbg, wout,
                 final_gain, ck):
    seq = x2d.shape[0]
    tp = T0 + seq
    h0 = jnp.concatenate([jnp.zeros((PADF, D_MODEL), F32), meta, x2d], axis=0)
    w_r = w_in_bf[:, :W_R]
    w_g = jnp.pad(w_in_bf[:, W_R:W_R + W_G], ((0, 0), (0, W_GP - W_G)))
    w_m = w_in_bf[:, W_R + W_G:]
    wg_pad = jnp.pad(w_gate_up, ((0, 128 - GATE_RANK), (0, 0))).astype(BF16)

    pos = jnp.arange(tp, dtype=F32) - PADF
    half = RET_QK // 2
    inv = ROPE_BASE ** (-jnp.arange(half, dtype=F32) / half)
    ang = pos[:, None] * inv[None, :]
    cos, sin = jnp.cos(ang), jnp.sin(ang)
    lgam = jnp.log1p(-(2.0 ** (-5.0 - jnp.arange(RET_HEADS, dtype=F32))))
    pm_np = _gla_p_matrix()
    pmat = jnp.asarray(pm_np, BF16)
    pmat_t = jnp.asarray(pm_np.T.copy(), BF16)

    u = _rms_call(h0, norm_gain)
    tab = pl.BlockSpec((TM, half), lambda j, i: (i, 0))
    rqk = _mm_nn("proj_rqk", u, w_r, BF16, D_MODEL, 0, 2 * D_MODEL, _rope_epilogue, (cos, sin), (tab, tab))
    rv = _mm_nn("proj_rv", u, w_r, BF16, D_MODEL, 2 * D_MODEL, RET_W)
    rg = _mm_nn("proj_rg", u, w_r, F32, D_MODEL, 4 * D_MODEL, RET_W)
    gqk = _mm_nn("proj_gqk", u, w_g, F32, 2 * GLA_KW, 0, 2 * GLA_KW, _gqk_epilogue)
    gv = _mm_nn("proj_gv", u, w_g, BF16, GLA_W, 2 * GLA_KW, GLA_W)
    gg = _mm_nn("proj_gg", u, w_g, F32, GLA_W, 2 * GLA_KW + GLA_W, GLA_W)
    glr = _mm_nn("proj_glr", u, w_g, F32, 128, 2 * GLA_KW + 2 * GLA_W, 128)
    mg = _mm_nn("proj_mg", u, w_m, F32, D_MODEL, 0, W_M)

    o_ret, a_ret, st_ret = _ret_fwd_call(rqk, rv, rg, ret_gain, lgam)
    o_gla, a_gla, st_gla = _gla_fwd_call(gqk, gv, glr, gg, wg_pad, b_gate, gla_gain, pmat)

    gf = final_gain.reshape(1, D_MODEL)
    (dh1, da_ret, da_gla, dm, merged_b, dh1_b, dpr_b, dpg_b, loss, dgf) = _mid_call(
        a_ret, a_gla, mg, h0, tgt2d, wbr, wbg, wout, gf)

    names_b = ("w_branch_ret", "w_branch_gla", "w_out")
    g2_b = [_mm_tn("dw_br", a_ret, dpr_b, D_MODEL).reshape(4, 2, RET_W // 8, D_MODEL).transpose(1, 0, 2, 3),
            _mm_tn("dw_bg", a_gla, dpg_b, D_MODEL).reshape(4, 2, GLA_W // 8, D_MODEL).transpose(1, 0, 2, 3),
            _mm_tn("dw_out", merged_b, dh1_b, D_MODEL).reshape(4, 2, D_MODEL // 8, D_MODEL).transpose(1, 0, 2, 3)]
    sib_b = _swap_halves_call("swap_halves_branch", g2_b)
    sum_b = [_add_half_call("add_half_" + nm, g, b, ck) for nm, g, b in zip(names_b, g2_b, sib_b)]
    d_g, dwg, dbg, dgla_gain, *chips_b = _gla_bwd_call(gqk, gv, glr, gg, o_gla, da_gla, st_gla, wg_pad, b_gate,
                                                       gla_gain, pmat, pmat_t, exchange=sum_b)
    mine = [_add_chips_call("add_chips_" + nm, g, b, p, ck) for nm, g, b, p in zip(names_b, g2_b, sib_b, chips_b)]

    d_r, dret_gain = _ret_bwd_call(rqk, rv, rg, o_ret, da_ret, st_ret, ret_gain, lgam, cos, sin)

    dw_in = jnp.concatenate([_mm_tn("dw_r", u, d_r, D_MODEL), _mm_tn("dw_g", u, d_g, 640)[:, :W_G],
                             _mm_tn("dw_m", u, dm, D_MODEL)], axis=1)
    g2_in = dw_in.reshape(2, D_MODEL // 2, 4, IN_COLS // 4).transpose(0, 2, 1, 3)
    (sib_in,) = _swap_halves_call("swap_halves_in", [g2_in])
    sum_in = _add_half_call("add_half_w_in", g2_in, sib_in, ck)

    du = _mm_nt_acc("du_m", dm, w_m, W_M)[0]
    du = _mm_nt_acc("du_g", d_g, w_g, W_GP, acc_in=du)[0]
    tile = pl.BlockSpec((TM, D_MODEL), lambda i, kk: (i, 0))
    row = pl.BlockSpec((1, D_MODEL), lambda i, kk: (0, 0))
    dx, dmeta, dnorm_gain, chips_in = _mm_nt_acc(
        "du_r", d_r, w_r, 2 * D_MODEL, acc_in=du, epilogue=_rms_bwd_epilogue, extras=(h0, norm_gain, dh1),
        extra_specs=(tile, row, tile),
        extra_out_shapes=(jax.ShapeDtypeStruct((seq, D_MODEL), F32), jax.ShapeDtypeStruct((N_META, D_MODEL), F32),
                          jax.ShapeDtypeStruct((1, D_MODEL), F32)),
        extra_out_specs=(pl.BlockSpec((TM, D_MODEL), lambda i, kk: (jnp.maximum(i - 1, 0), 0)),
                         pl.BlockSpec((N_META, D_MODEL), lambda i, kk: (0, 0)), row),
        exchange=[sum_in])
    mine = [_add_chips_call("add_chips_w_in", g2_in, sib_in, chips_in, ck)] + mine
    full = _join_halves_call("join_halves", mine)

    return dict(loss=loss[0, 0], dx=dx, dmeta=dmeta, norm_gain=dnorm_gain, w_gate_up=dwg[:GATE_RANK], b_gate=dbg,
                ret_norm_gain=dret_gain, gla_norm_gain=dgla_gain, final_norm_gain=dgf.reshape(D_MODEL),
                w_in=full[0], w_branch_ret=full[1], w_branch_gla=full[2], w_out=full[3])


MESH = pl.DeviceIdType.MESH
ANY = pl.BlockSpec(memory_space=pl.ANY)


def _place():
    return lax.axis_index("x"), lax.axis_index("y"), lax.axis_index("c")


def _gather8_call(name, parts):
    n = len(parts)

    def body(*refs):
        x_refs, out_refs = refs[:n], refs[n:2 * n]
        send_sems, recv_sems, local_sems = refs[2 * n:]
        x, y, c = _place()
        me, sibling = (x, y, c), (x, y, 1 - c)
        chips = [(1 - x, y), (x, 1 - y), (1 - x, 1 - y)]

        def slot(t, px, py, pc):
            return out_refs[t].at[4 * px + 2 * py + pc]

        def copy(t, k, block, to, src=None):
            return pltpu.make_async_remote_copy(
                src_ref=slot(t, *block) if src is None else src, dst_ref=slot(t, *block),
                send_sem=send_sems.at[7 * t + k], recv_sem=recv_sems.at[7 * t + k], device_id=to, device_id_type=MESH)

        mine = [pltpu.make_async_copy(x_refs[t], slot(t, *me), local_sems.at[t]) for t in range(n)]
        for cp in mine:
            cp.start()
        first = []
        for t in range(n):
            first.append(copy(t, 0, me, sibling, src=x_refs[t]))
            first += [copy(t, 1 + j, me, (*chip, c), src=x_refs[t]) for j, chip in enumerate(chips)]
        for cp in first:
            cp.start()
        passed = []
        for j, chip in enumerate(chips):
            for t in range(n):
                copy(t, 1 + j, (*chip, c), me).wait_recv()
                fwd = copy(t, 4 + j, (*chip, c), sibling)
                fwd.start()
                passed.append(fwd)
        for t in range(n):
            copy(t, 0, sibling, me).wait_recv()
            for j, chip in enumerate(chips):
                copy(t, 4 + j, (*chip, 1 - c), me).wait_recv()
        for cp in first + passed:
            cp.wait_send()
        for cp in mine:
            cp.wait()

    return pl.pallas_call(
        body, name=name,
        out_shape=[jax.ShapeDtypeStruct((8,) + p.shape, p.dtype) for p in parts],
        in_specs=[ANY] * n, out_specs=[ANY] * n,
        scratch_shapes=[pltpu.SemaphoreType.DMA((7 * n,)), pltpu.SemaphoreType.DMA((7 * n,)),
                        pltpu.SemaphoreType.DMA((n,))],
    )(*parts)


def _swap_halves_call(name, gs):
    n = len(gs)

    def body(*refs):
        g_refs, b_refs = refs[:n], refs[n:2 * n]
        send_sems, recv_sems = refs[2 * n:]
        x, y, c = _place()
        copies = [pltpu.make_async_remote_copy(
            src_ref=g_refs[t].at[1 - c], dst_ref=b_refs[t], send_sem=send_sems.at[t], recv_sem=recv_sems.at[t],
            device_id=(x, y, 1 - c), device_id_type=MESH) for t in range(n)]
        for cp in copies:
            cp.start()
        for cp in copies:
            cp.wait()

    return pl.pallas_call(
        body, name=name,
        out_shape=[jax.ShapeDtypeStruct(g.shape[1:], g.dtype) for g in gs],
        in_specs=[ANY] * n, out_specs=[ANY] * n,
        scratch_shapes=[pltpu.SemaphoreType.DMA((n,)), pltpu.SemaphoreType.DMA((n,))],
    )(*gs)


def _join_halves_call(name, ts):
    n = len(ts)

    def body(*refs):
        t_refs, o_refs = refs[:n], refs[n:2 * n]
        send_sems, recv_sems, local_sems = refs[2 * n:]
        x, y, c = _place()
        mine = [pltpu.make_async_copy(t_refs[t], o_refs[t].at[c], local_sems.at[t]) for t in range(n)]
        for cp in mine:
            cp.start()
        copies = [pltpu.make_async_remote_copy(
            src_ref=t_refs[t], dst_ref=o_refs[t].at[c], send_sem=send_sems.at[t], recv_sem=recv_sems.at[t],
            device_id=(x, y, 1 - c), device_id_type=MESH) for t in range(n)]
        for cp in copies:
            cp.start()
        for t in range(n):
            copies[t].wait_send()
            pltpu.make_async_remote_copy(
                src_ref=t_refs[t], dst_ref=o_refs[t].at[1 - c], send_sem=send_sems.at[t], recv_sem=recv_sems.at[t],
                device_id=(x, y, 1 - c), device_id_type=MESH).wait_recv()
        for cp in mine:
            cp.wait()

    return pl.pallas_call(
        body, name=name,
        out_shape=[jax.ShapeDtypeStruct((2,) + t.shape, t.dtype) for t in ts],
        in_specs=[ANY] * n, out_specs=[ANY] * n,
        scratch_shapes=[pltpu.SemaphoreType.DMA((n,)), pltpu.SemaphoreType.DMA((n,)), pltpu.SemaphoreType.DMA((n,))],
    )(*ts)


def _row_block(rows, cols, budget):
    best = 8
    for rb in range(8, rows + 1, 8):
        if rows % rb == 0 and rb * cols * 4 <= budget:
            best = rb
    return best


def _add_half_call(name, g, b, ck):
    _, _, r, cc = g.shape
    rb = _row_block(r, cc, 2 * 1024 * 1024)

    def body(ck_ref, g_ref, b_ref, o_ref):
        o_ref[...] = (g_ref[...] + b_ref[...]).astype(BF16)

    return pl.pallas_call(
        body, name=name,
        grid_spec=pltpu.PrefetchScalarGridSpec(
            num_scalar_prefetch=1, grid=(4, r // rb),
            in_specs=[pl.BlockSpec((None, None, rb, cc), lambda k, i, ck_ref: (ck_ref[0], k, i, 0)),
                      pl.BlockSpec((None, rb, cc), lambda k, i, ck_ref: (k, i, 0))],
            out_specs=pl.BlockSpec((None, rb, cc), lambda k, i, ck_ref: (k, i, 0))),
        out_shape=jax.ShapeDtypeStruct(b.shape, BF16),
        compiler_params=_cparams(2),
    )(ck, g, b)


def _add_chips_call(name, g, b, p, ck):
    _, _, r, cc = g.shape
    rb = _row_block(r, cc, 2 * 1024 * 1024)

    def body(ck_ref, g_ref, b_ref, p0_ref, p1_ref, p2_ref, o_ref):
        own = g_ref[...] + b_ref[...]
        o_ref[...] = ((own + p0_ref[...].astype(F32)) + p1_ref[...].astype(F32)) + p2_ref[...].astype(F32)

    def peer(j):
        return pl.BlockSpec((None, rb, cc), lambda i, ck_ref: (j, i, 0))

    return pl.pallas_call(
        body, name=name,
        grid_spec=pltpu.PrefetchScalarGridSpec(
            num_scalar_prefetch=1, grid=(r // rb,),
            in_specs=[pl.BlockSpec((None, None, rb, cc), lambda i, ck_ref: (ck_ref[0], ck_ref[1], i, 0)),
                      pl.BlockSpec((None, rb, cc), lambda i, ck_ref: (ck_ref[1], i, 0)),
                      peer(0), peer(1), peer(2)],
            out_specs=pl.BlockSpec((rb, cc), lambda i, ck_ref: (i, 0))),
        out_shape=jax.ShapeDtypeStruct((r, cc), F32),
        compiler_params=_cparams(1),
    )(ck, g, b, p, p, p)


def _sum8_call(name, g):
    def body(g_ref, o_ref):
        acc = g_ref[0]
        for d in range(1, 8):
            acc = acc + g_ref[d]
        o_ref[...] = acc

    return pl.pallas_call(body, name=name, out_shape=jax.ShapeDtypeStruct(g.shape[1:], F32))(g)


def _adamw_call(name, w, g, m, v):
    r, cc = w.shape
    rb = _row_block(r, cc, 1024 * 1024) if r % 8 == 0 else r

    def body(w_ref, g_ref, m_ref, v_ref, d_ref, m2_ref, v2_ref):
        gv = g_ref[...]
        m2 = ADAM_B1 * m_ref[...] + (1.0 - ADAM_B1) * gv
        v2 = ADAM_B2 * v_ref[...] + (1.0 - ADAM_B2) * (gv * gv)
        m_hat = m2 / (1.0 - ADAM_B1 ** ADAM_STEP)
        v_hat = v2 / (1.0 - ADAM_B2 ** ADAM_STEP)
        d_ref[...] = -ADAM_LR * (m_hat / (jnp.sqrt(v_hat) + ADAM_EPS) + ADAM_WD * w_ref[...])
        m2_ref[...] = m2
        v2_ref[...] = v2

    spec = pl.BlockSpec((rb, cc), lambda i: (i, 0))
    return pl.pallas_call(
        body, name=name, grid=(r // rb,), in_specs=[spec] * 4, out_specs=[spec] * 3,
        out_shape=[jax.ShapeDtypeStruct((r, cc), F32)] * 3, compiler_params=_cparams(1),
    )(w, g, m, v)


SMALL = (("norm_gain", D_MODEL), ("b_gate", GLA_KW), ("ret_norm_gain", RET_W), ("gla_norm_gain", GLA_W),
         ("final_norm_gain", D_MODEL), ("w_gate_up", GATE_RANK * GLA_KW), ("meta_tokens", N_META * D_MODEL))


def _pack_rows(vecs, rows):
    flat = jnp.concatenate([v.reshape(-1) for v in vecs])
    return jnp.pad(flat, (0, rows * 128 - flat.shape[0])).reshape(rows, 128)


def kernel(x, meta_tokens, norm_gain, w_in, w_gate_up, b_gate, ret_norm_gain, gla_norm_gain, w_branch_ret, w_branch_gla, w_out, final_norm_gain, loss_target, m_meta_tokens, m_norm_gain, m_w_in, m_w_gate_up, m_b_gate, m_ret_norm_gain, m_gla_norm_gain, m_w_branch_ret, m_w_branch_gla, m_w_out, m_final_norm_gain, v_meta_tokens, v_norm_gain, v_w_in, v_w_gate_up, v_b_gate, v_ret_norm_gain, v_gla_norm_gain, v_w_branch_ret, v_w_branch_gla, v_w_out, v_final_norm_gain):
    xi, yi, ci = _place()
    kme = 2 * xi + yi
    ck = jnp.stack([ci, kme]).astype(jnp.int32)
    sw_in = w_in.shape[2]

    def my_half(a, dtype):
        r, cc = a.shape
        return lax.dynamic_index_in_dim(a.reshape(2, r // 2, cc), ci, 0, keepdims=False).astype(dtype)

    g_in, g_br, g_bg, g_out, g_meta, g_wg = _gather8_call(
        "gather_weights",
        [my_half(w_in[0], BF16), my_half(w_branch_ret[0], BF16), my_half(w_branch_gla[0], BF16),
         my_half(w_out[0], BF16), my_half(meta_tokens, F32), my_half(w_gate_up[0], F32)])
    w_in_bf = g_in.reshape(4, 2, D_MODEL // 2, sw_in).transpose(1, 2, 0, 3).reshape(D_MODEL, 4 * sw_in)
    wbr = g_br.reshape(RET_W, D_MODEL)
    wbg = g_bg.reshape(GLA_W, D_MODEL)
    wout = g_out.reshape(D_MODEL, D_MODEL)
    meta = g_meta.reshape(4, 2, N_META // 2, D_MODEL // 4).transpose(1, 2, 0, 3).reshape(N_META, D_MODEL)
    wg_full = g_wg.reshape(4, 2, GATE_RANK // 2, GLA_KW // 4).transpose(1, 2, 0, 3).reshape(GATE_RANK, GLA_KW)

    loc = _device_step(x[0], loss_target[0], meta, norm_gain, w_in_bf, wg_full, b_gate, ret_norm_gain, gla_norm_gain,
                       wbr, wbg, wout, final_norm_gain, ck)
    loss = lax.psum(loc["loss"], ("x", "y", "c"))
    names = ("w_in", "w_branch_ret", "w_branch_gla", "w_out")
    full = [loc[nm] for nm in names]
    big_w = dict(w_in=w_in[0], w_branch_ret=w_branch_ret[0], w_branch_gla=w_branch_gla[0], w_out=w_out[0])
    big_m = dict(w_in=m_w_in[0], w_branch_ret=m_w_branch_ret[0], w_branch_gla=m_w_branch_gla[0], w_out=m_w_out[0])
    big_v = dict(w_in=v_w_in[0], w_branch_ret=v_w_branch_ret[0], w_branch_gla=v_w_branch_gla[0], w_out=v_w_out[0])
    grads, deltas, new_m, new_v = {}, {}, {}, {}
    for nm, f in zip(names, full):
        shape = big_w[nm].shape
        g = f.reshape(shape)
        d, m2, v2 = _adamw_call("adamw_" + nm, big_w[nm], g, big_m[nm], big_v[nm])
        grads[nm], deltas[nm], new_m[nm], new_v[nm] = (a.reshape((1,) + shape) for a in (g, d, m2, v2))

    small_g = dict(loc)
    small_g["meta_tokens"] = loc["dmeta"]
    n_small = sum(sz for _, sz in SMALL)
    rows = -(-n_small // 128 // 8) * 8
    (g_small,) = _gather8_call("gather_small_grads", [_pack_rows([small_g[nm] for nm, _ in SMALL], rows)])
    tot = _sum8_call("sum_small_grads", g_small).reshape(-1)
    off = 0
    sg = {}
    for nm, sz in SMALL:
        sg[nm] = tot[off:off + sz]
        off += sz
    sg["w_gate_up"] = lax.dynamic_slice_in_dim(sg["w_gate_up"].reshape(GATE_RANK, GLA_KW), kme * (GLA_KW // 4),
                                               GLA_KW // 4, axis=1)
    sg["meta_tokens"] = lax.dynamic_slice_in_dim(sg["meta_tokens"].reshape(N_META, D_MODEL), kme * (D_MODEL // 4),
                                                 D_MODEL // 4, axis=1)
    small_w = dict(norm_gain=norm_gain, b_gate=b_gate, ret_norm_gain=ret_norm_gain, gla_norm_gain=gla_norm_gain,
                   final_norm_gain=final_norm_gain, w_gate_up=w_gate_up, meta_tokens=meta_tokens)
    small_m = dict(norm_gain=m_norm_gain, b_gate=m_b_gate, ret_norm_gain=m_ret_norm_gain,
                   gla_norm_gain=m_gla_norm_gain, final_norm_gain=m_final_norm_gain, w_gate_up=m_w_gate_up,
                   meta_tokens=m_meta_tokens)
    small_v = dict(norm_gain=v_norm_gain, b_gate=v_b_gate, ret_norm_gain=v_ret_norm_gain,
                   gla_norm_gain=v_gla_norm_gain, final_norm_gain=v_final_norm_gain, w_gate_up=v_w_gate_up,
                   meta_tokens=v_meta_tokens)
    order = [nm for nm, _ in SMALL]
    sizes = [small_w[nm].size for nm in order]
    prow = -(-sum(sizes) // 128 // 8) * 8
    pk = lambda d: _pack_rows([d[nm] for nm in order], prow)
    d_s, m_s, v_s = _adamw_call("adamw_small", pk(small_w), pk(sg), pk(small_m), pk(small_v))
    off = 0
    for nm, sz in zip(order, sizes):
        shape = small_w[nm].shape
        grads[nm] = sg[nm].reshape(shape)
        deltas[nm], new_m[nm], new_v[nm] = (a.reshape(-1)[off:off + sz].reshape(shape) for a in (d_s, m_s, v_s))
        off += sz

    out_order = ("meta_tokens", "norm_gain", "w_in", "w_gate_up", "b_gate", "ret_norm_gain", "gla_norm_gain",
                 "w_branch_ret", "w_branch_gla", "w_out", "final_norm_gain")
    dx = loc["dx"].reshape(x.shape)
    return (loss, dx, *[grads[nm] for nm in out_order], *[deltas[nm] for nm in out_order],
            *[new_m[nm] for nm in out_order], *[new_v[nm] for nm in out_order])
```

```python
import functools
import math

import numpy as np
import jax
import jax.numpy as jnp
from jax import lax
from jax.experimental import pallas as pl
from jax.experimental.pallas import tpu as pltpu

F32 = jnp.float32
BF16 = jnp.bfloat16

D_MODEL = 1024
N_META = 16
EPS = 1e-6
ROPE_BASE = 10000.0
RET_HEADS, RET_QK, RET_V = 4, 256, 512
RET_W = RET_HEADS * RET_V
GLA_HEADS, GLA_K, GLA_V = 4, 128, 256
GLA_W = GLA_HEADS * GLA_V
GLA_KW = GLA_HEADS * GLA_K
GATE_RANK = 16
GATE_TAU = 16.0
GLA_SUB = 16

TM = 256
T0 = TM
PADF = T0 - N_META
GC = 128
TB = 768
TK = 768

W_R = 6144
W_G = 3088
W_GP = 3200
W_M = 2048
IN_COLS = W_R + W_G + W_M

ADAM_LR, ADAM_B1, ADAM_B2, ADAM_EPS, ADAM_WD, ADAM_STEP = 0.001, 0.9, 0.999, 1e-08, 0.01, 10

VMEM_LIMIT = 56 * 1024 * 1024

NN = ((1,), (0,))
NT = ((1,), (1,))
TN = ((0,), (0,))


def _dot(a, b, dims):
    return lax.dot_general(a, b, (dims, ((), ())), preferred_element_type=F32)


def _cparams(n_axes):
    return pltpu.CompilerParams(dimension_semantics=("arbitrary",) * n_axes, vmem_limit_bytes=VMEM_LIMIT)


def _sigmoid(x):
    return 1.0 / (1.0 + jnp.exp(-x))


def _split3(x):
    hi = x.astype(BF16)
    r1 = x - hi.astype(F32)
    mid = r1.astype(BF16)
    lo = (r1 - mid.astype(F32)).astype(BF16)
    return hi, mid, lo


def _exact_pm(p, x):
    hi, mid, lo = _split3(x)
    return _dot(p, hi, NN) + _dot(p, mid, NN) + _dot(p, lo, NN)


def _rms_call(h0, gain):
    tp = h0.shape[0]

    def body(h_ref, g_ref, u_ref):
        h = h_ref[...]
        r = lax.rsqrt(jnp.mean(h * h, axis=-1, keepdims=True) + EPS)
        u_ref[...] = (h * r * g_ref[...]).astype(BF16)

    return pl.pallas_call(
        body, name="rms_in", grid=(tp // TM,),
        in_specs=[pl.BlockSpec((TM, D_MODEL), lambda i: (i, 0)), pl.BlockSpec((1, D_MODEL), lambda i: (0, 0))],
        out_specs=pl.BlockSpec((TM, D_MODEL), lambda i: (i, 0)),
        out_shape=jax.ShapeDtypeStruct((tp, D_MODEL), BF16),
        compiler_params=_cparams(1),
    )(h0, gain)


def _mm_nn(name, a, b, out_dtype, tn, col0, ncols, epilogue=None, extras=(), extra_specs=()):
    m, k = a.shape
    nj, j0 = ncols // tn, col0 // tn

    def body(a_ref, b_ref, *rest):
        *ex, o_ref = rest
        acc = _dot(a_ref[...], b_ref[...], NN)
        if epilogue is None:
            o_ref[...] = acc.astype(out_dtype)
        else:
            epilogue(acc, o_ref, *ex)

    return pl.pallas_call(
        body, name=name, grid=(nj, m // TB),
        in_specs=[pl.BlockSpec((TB, k), lambda j, i: (i, 0)), pl.BlockSpec((k, tn), lambda j, i: (0, j0 + j))]
        + list(extra_specs),
        out_specs=pl.BlockSpec((TB, tn), lambda j, i: (i, j)),
        out_shape=jax.ShapeDtypeStruct((m, ncols), out_dtype),
        compiler_params=_cparams(2),
    )(a, b, *extras)


def _rope_epilogue(acc, o_ref, cos_ref, sin_ref):
    scale = jnp.where(pl.program_id(0) == 1, RET_QK ** -0.5, 1.0).astype(F32)
    cos, sin = cos_ref[...], sin_ref[...]
    half = RET_QK // 2
    for h in range(RET_HEADS):
        t1 = acc[:, h * RET_QK:h * RET_QK + half]
        t2 = acc[:, h * RET_QK + half:(h + 1) * RET_QK]
        o_ref[:, h * RET_QK:h * RET_QK + half] = ((t1 * cos - t2 * sin) * scale).astype(BF16)
        o_ref[:, h * RET_QK + half:(h + 1) * RET_QK] = ((t2 * cos + t1 * sin) * scale).astype(BF16)


def _gqk_epilogue(acc, o_ref):
    o_ref[:, :GLA_KW] = acc[:, :GLA_KW] * (GLA_K ** -0.5)
    o_ref[:, GLA_KW:] = acc[:, GLA_KW:]


def _exchange_copies(s_refs, b_refs, send_sems, recv_sems):
    x, y, c = _place()
    chips = [(1 - x, y), (x, 1 - y), (1 - x, 1 - y)]
    return [pltpu.make_async_remote_copy(
        src_ref=s_refs[t].at[2 * chip[0] + chip[1]], dst_ref=b_refs[t].at[j], send_sem=send_sems.at[3 * t + j],
        recv_sem=recv_sems.at[3 * t + j], device_id=(*chip, c), device_id_type=MESH)
        for t in range(len(s_refs)) for j, chip in enumerate(chips)]


def _exchange_shapes(ss):
    return ([jax.ShapeDtypeStruct((3,) + s.shape[1:], s.dtype) for s in ss],
            [pltpu.SemaphoreType.DMA((3 * len(ss),)), pltpu.SemaphoreType.DMA((3 * len(ss),))])


def _mm_nt_acc(name, a, w, tk, acc_in=None, epilogue=None, extras=(), extra_specs=(), extra_out_shapes=(),
               extra_out_specs=(), exchange=()):
    m, k = a.shape
    n = w.shape[0]
    nk, ni = k // tk, m // TB
    has_acc = acc_in is not None
    n_xc = len(exchange)

    def body(*refs):
        a_ref, w_ref = refs[0], refs[1]
        pos = 2
        acc_ref = None
        if has_acc:
            acc_ref = refs[pos]
            pos += 1
        ex = refs[pos:pos + len(extras)]
        pos += len(extras)
        xc_src = refs[pos:pos + n_xc]
        pos += n_xc
        n_scr = 3 if n_xc else 1
        outs = refs[pos:len(refs) - n_scr - n_xc]
        xc_dst = refs[len(refs) - n_scr - n_xc:len(refs) - n_scr]
        scr = refs[len(refs) - n_scr]
        i, kk = pl.program_id(0), pl.program_id(1)
        if n_xc:
            copies = _exchange_copies(xc_src, xc_dst, refs[-2], refs[-1])

            @pl.when((i == 0) & (kk == 0))
            def _():
                for cp in copies:
                    cp.start()

        @pl.when(kk == 0)
        def _():
            scr[...] = acc_ref[...] if has_acc else jnp.zeros_like(scr)

        scr[...] += _dot(a_ref[...], w_ref[...], NT)

        @pl.when(kk == nk - 1)
        def _():
            if epilogue is None:
                outs[0][...] = scr[...]
            else:
                epilogue(scr[...], outs, i, *ex)

        if n_xc:
            @pl.when((i == ni - 1) & (kk == nk - 1))
            def _():
                for cp in copies:
                    cp.wait()

    in_specs = [pl.BlockSpec((TB, tk), lambda i, kk: (i, kk)), pl.BlockSpec((n, tk), lambda i, kk: (0, kk))]
    args = [a, w]
    if has_acc:
        in_specs.append(pl.BlockSpec((TB, n), lambda i, kk: (i, 0)))
        args.append(acc_in)
    in_specs += list(extra_specs) + [ANY] * n_xc
    args += list(extras) + list(exchange)
    if epilogue is None:
        out_shape = [jax.ShapeDtypeStruct((m, n), F32)]
        out_specs = [pl.BlockSpec((TB, n), lambda i, kk: (i, 0))]
    else:
        out_shape, out_specs = list(extra_out_shapes), list(extra_out_specs)
    scratch = [pltpu.VMEM((TB, n), F32)]
    if n_xc:
        xc_shapes, xc_sems = _exchange_shapes(exchange)
        out_shape += xc_shapes
        out_specs += [ANY] * n_xc
        scratch += xc_sems
    return pl.pallas_call(
        body, name=name, grid=(ni, nk), in_specs=in_specs, out_specs=out_specs, out_shape=out_shape,
        scratch_shapes=scratch, compiler_params=_cparams(2),
    )(*args)


def _rms_bwd_epilogue(du, outs, i, h_ref, g_ref, dh1_ref):
    dh0_ref, dg_ref = outs
    h = h_ref[...]
    r = lax.rsqrt(jnp.mean(h * h, axis=-1, keepdims=True) + EPS)
    xh = h * r
    dxh = du * g_ref[...]
    dh0_ref[...] = dh1_ref[...] + r * (dxh - xh * jnp.mean(dxh * xh, axis=-1, keepdims=True))

    @pl.when(i == 0)
    def _():
        dg_ref[...] = jnp.zeros_like(dg_ref)

    dg_ref[...] += jnp.sum(du * xh, axis=0, keepdims=True)


def _mm_tn(name, a, b, bn):
    t, m = a.shape
    n = b.shape[1]

    def body(a_ref, b_ref, o_ref):
        @pl.when(pl.program_id(1) == 0)
        def _():
            o_ref[...] = jnp.zeros_like(o_ref)

        o_ref[...] += _dot(a_ref[...], b_ref[...], TN)

    return pl.pallas_call(
        body, name=name, grid=(n // bn, t // TK),
        in_specs=[pl.BlockSpec((TK, m), lambda j, kk: (kk, 0)), pl.BlockSpec((TK, bn), lambda j, kk: (kk, j))],
        out_specs=pl.BlockSpec((m, bn), lambda j, kk: (0, j)),
        out_shape=jax.ShapeDtypeStruct((m, n), F32),
        compiler_params=_cparams(2),
    )(a, b)


def _ret_consts(lg):
    c = TM
    ii = lax.broadcasted_iota(jnp.int32, (c, c), 0)
    jj = lax.broadcasted_iota(jnp.int32, (c, c), 1)
    rel = (ii - jj).astype(F32)
    dm = jnp.where(rel >= 0, jnp.exp(jnp.maximum(rel, 0.0) * lg), 0.0)
    idx = lax.broadcasted_iota(jnp.int32, (c, 1), 0).astype(F32)
    xi = jnp.exp((idx + 1.0) * lg)
    zeta = jnp.exp((c - 1.0 - idx) * lg)
    gc = jnp.exp(jnp.full((1, 1), c, F32) * lg)
    return dm, xi, zeta, gc


def _ret_fwd_call(rqk, rv, rg, gain, lgam):
    tp = rqk.shape[0]
    nc = tp // TM

    def body(lg_ref, qk_ref, v_ref, rg_ref, g_ref, o_ref, a_ref, st_ref, s_scr):
        @pl.when(pl.program_id(0) == 0)
        def _():
            s_scr[...] = jnp.zeros_like(s_scr)

        for h in range(RET_HEADS):
            dm, xi, zeta, gc = _ret_consts(lg_ref[h])
            q = qk_ref[:, h * RET_QK:(h + 1) * RET_QK]
            k = qk_ref[:, D_MODEL + h * RET_QK:D_MODEL + (h + 1) * RET_QK]
            v = v_ref[:, h * RET_V:(h + 1) * RET_V]
            sb = s_scr[h].astype(BF16)
            st_ref[0, h] = sb
            s = _dot(q, k, NT) * dm
            o = _dot(s.astype(BF16), v, NN) + xi * _dot(q, sb, NN)
            kz = (k.astype(F32) * zeta).astype(BF16)
            s_scr[h] = gc * s_scr[h] + _dot(kz, v, TN)
            o_ref[:, h * RET_V:(h + 1) * RET_V] = o
            mu = jnp.mean(o, axis=-1, keepdims=True)
            xc = o - mu
            xh = xc * lax.rsqrt(jnp.mean(xc * xc, axis=-1, keepdims=True) + EPS)
            g = rg_ref[:, h * RET_V:(h + 1) * RET_V]
            a_ref[:, h * RET_V:(h + 1) * RET_V] = (
                xh * g_ref[:, h * RET_V:(h + 1) * RET_V] * (g * _sigmoid(g))).astype(BF16)

    return pl.pallas_call(
        body, name="ret_fwd", grid=(nc,),
        in_specs=[pl.BlockSpec(memory_space=pltpu.SMEM),
                  pl.BlockSpec((TM, 2 * D_MODEL), lambda n: (n, 0)),
                  pl.BlockSpec((TM, RET_W), lambda n: (n, 0)),
                  pl.BlockSpec((TM, RET_W), lambda n: (n, 0)),
                  pl.BlockSpec((1, RET_W), lambda n: (0, 0))],
        out_specs=[pl.BlockSpec((TM, RET_W), lambda n: (n, 0)),
                   pl.BlockSpec((TM, RET_W), lambda n: (n, 0)),
                   pl.BlockSpec((1, RET_HEADS, RET_QK, RET_V), lambda n: (n, 0, 0, 0))],
        out_shape=[jax.ShapeDtypeStruct((tp, RET_W), F32), jax.ShapeDtypeStruct((tp, RET_W), BF16),
                   jax.ShapeDtypeStruct((nc, RET_HEADS, RET_QK, RET_V), BF16)],
        scratch_shapes=[pltpu.VMEM((RET_HEADS, RET_QK, RET_V), F32)],
        compiler_params=_cparams(1),
    )(lgam, rqk, rv, rg, gain)


def _ret_bwd_call(rqk, rv, rg, o_ret, da, states, gain, lgam, cos, sin):
    tp = rqk.shape[0]
    nc = tp // TM
    half = RET_QK // 2

    def body(lg_ref, qk_ref, v_ref, rg_ref, o_ref, da_ref, st_ref, g_ref, cos_ref, sin_ref, dp_ref, dg_ref, ds_scr):
        @pl.when(pl.program_id(0) == 0)
        def _():
            ds_scr[...] = jnp.zeros_like(ds_scr)
            dg_ref[...] = jnp.zeros_like(dg_ref)

        cos, sin = cos_ref[...], sin_ref[...]
        for h in range(RET_HEADS):
            hs = slice(h * RET_V, (h + 1) * RET_V)
            dm, xi, zeta, gc = _ret_consts(lg_ref[h])
            o = o_ref[:, hs]
            mu = jnp.mean(o, axis=-1, keepdims=True)
            xc = o - mu
            rstd = lax.rsqrt(jnp.mean(xc * xc, axis=-1, keepdims=True) + EPS)
            xh = xc * rstd
            gain_h = g_ref[:, hs]
            g = rg_ref[:, hs]
            sg = _sigmoid(g)
            silu = g * sg
            dah = da_ref[:, hs]
            dp_ref[:, 4 * D_MODEL + h * RET_V:4 * D_MODEL + (h + 1) * RET_V] = (
                dah * (xh * gain_h) * (sg * (1.0 + g * (1.0 - sg)))).astype(BF16)
            dn = dah * silu
            dg_ref[:, hs] += jnp.sum(dn * xh, axis=0, keepdims=True)
            dxh = dn * gain_h
            do = rstd * (dxh - jnp.mean(dxh, axis=-1, keepdims=True)
                         - xh * jnp.mean(dxh * xh, axis=-1, keepdims=True))
            dob = do.astype(BF16)
            q = qk_ref[:, h * RET_QK:(h + 1) * RET_QK]
            k = qk_ref[:, D_MODEL + h * RET_QK:D_MODEL + (h + 1) * RET_QK]
            v = v_ref[:, hs]
            sp = st_ref[0, h]
            ds = ds_scr[h]
            dsb = ds.astype(BF16)
            s = (_dot(q, k, NT) * dm).astype(BF16)
            dsc = (_dot(dob, v, NT) * dm).astype(BF16)
            dq = _dot(dsc, k, NN) + xi * _dot(dob, sp, NT)
            dk = _dot(dsc, q, TN) + zeta * _dot(v, dsb, NT)
            kz = (k.astype(F32) * zeta).astype(BF16)
            dv = _dot(s, dob, TN) + _dot(kz, dsb, NN)
            qx = (q.astype(F32) * xi).astype(BF16)
            ds_scr[h] = gc * ds + _dot(qx, dob, TN)
            dp_ref[:, 2 * D_MODEL + h * RET_V:2 * D_MODEL + (h + 1) * RET_V] = dv.astype(BF16)
            dk = dk * (RET_QK ** -0.5)
            for base, t in ((0, dq), (D_MODEL, dk)):
                t1, t2 = t[:, :half], t[:, half:]
                dp_ref[:, base + h * RET_QK:base + h * RET_QK + half] = (t1 * cos + t2 * sin).astype(BF16)
                dp_ref[:, base + h * RET_QK + half:base + (h + 1) * RET_QK] = (t2 * cos - t1 * sin).astype(BF16)

    rev = lambda n: (nc - 1 - n, 0)
    return pl.pallas_call(
        body, name="ret_bwd", grid=(nc,),
        in_specs=[pl.BlockSpec(memory_space=pltpu.SMEM),
                  pl.BlockSpec((TM, 2 * D_MODEL), rev),
                  pl.BlockSpec((TM, RET_W), rev),
                  pl.BlockSpec((TM, RET_W), rev),
                  pl.BlockSpec((TM, RET_W), rev),
                  pl.BlockSpec((TM, RET_W), rev),
                  pl.BlockSpec((1, RET_HEADS, RET_QK, RET_V), lambda n: (nc - 1 - n, 0, 0, 0)),
                  pl.BlockSpec((1, RET_W), lambda n: (0, 0)),
                  pl.BlockSpec((TM, half), rev),
                  pl.BlockSpec((TM, half), rev)],
        out_specs=[pl.BlockSpec((TM, W_R), rev), pl.BlockSpec((1, RET_W), lambda n: (0, 0))],
        out_shape=[jax.ShapeDtypeStruct((tp, W_R), BF16), jax.ShapeDtypeStruct((1, RET_W), F32)],
        scratch_shapes=[pltpu.VMEM((RET_HEADS, RET_QK, RET_V), F32)],
        compiler_params=_cparams(1),
    )(lgam, rqk, rv, rg, o_ret, da, states, gain, cos, sin)


GLA_LEVELS = tuple(GC >> (s + 1) for s in range(int(math.log2(GC // GLA_SUB))))
NLEV = len(GLA_LEVELS)
NBLK = 2 * NLEV + 3
P_ROWS = NBLK * GC + 8


def _gla_p_matrix():
    c = GC
    i = np.arange(c)[:, None]
    r = np.arange(c)[None, :]
    blocks = []
    for m in GLA_LEVELS:
        second = (i & m) != 0
        ref = (i // (2 * m)) * 2 * m + m - 1
        blocks.append(second & (r > ref) & (r <= i))
    for m in GLA_LEVELS:
        second = (i & m) != 0
        ref = (i // (2 * m)) * 2 * m + m - 1
        blocks.append((~second) & (r > i) & (r <= ref))
    blocks.append((r >= (i // GLA_SUB) * GLA_SUB) & (r <= i))
    blocks.append(r <= i)
    blocks.append(r > i)
    blocks.append(np.ones((8, c), bool))
    return np.concatenate([b.astype(np.float32) for b in blocks], axis=0)


def _gla_masks():
    ii = lax.broadcasted_iota(jnp.int32, (GC, GC), 0)
    jj = lax.broadcasted_iota(jnp.int32, (GC, GC), 1)
    masks = []
    for m in GLA_LEVELS:
        sh = int(math.log2(2 * m))
        masks.append(((ii >> sh) == (jj >> sh)) & ((ii & m) != 0) & ((jj & m) == 0))
    sh = int(math.log2(GLA_SUB))
    md = ((ii >> sh) == (jj >> sh)) & (jj <= ii)
    row = lax.broadcasted_iota(jnp.int32, (GC, 1), 0)
    second = [(row & m) != 0 for m in GLA_LEVELS]
    return masks, md, second


def _gla_log_decay(glr_ref, wg_ref, bg_ref):
    z = _dot(glr_ref[...].astype(BF16), wg_ref[...], NN) + bg_ref[...]
    la = (jnp.minimum(z, 0.0) - jnp.log1p(jnp.exp(-jnp.abs(z)))) * (1.0 / GATE_TAU)
    return z, la


def _gla_factors(e, h, second):
    cs = slice(h * GLA_K, (h + 1) * GLA_K)
    blk = lambda b: e[b * GC:(b + 1) * GC, cs]
    fq = [jnp.where(second[l], jnp.exp(blk(l)), 0.0) for l in range(NLEV)]
    fk = [jnp.where(second[l], 0.0, jnp.exp(blk(NLEV + l))) for l in range(NLEV)]
    ed = jnp.exp(blk(2 * NLEV))
    edi = jnp.exp(-blk(2 * NLEV))
    eb = jnp.exp(blk(2 * NLEV + 1))
    ee = jnp.exp(blk(2 * NLEV + 2))
    ebl = jnp.exp(e[NBLK * GC:NBLK * GC + 1, cs])
    return fq, fk, ed, edi, eb, ee, ebl


def _gla_scores(q, k, fq, fk, ed, edi, masks, md):
    qt = [(q * f).astype(BF16) for f in fq]
    kt = [(k * f).astype(BF16) for f in fk]
    qd = (q * ed).astype(BF16)
    kd = (k * edi).astype(BF16)
    a = jnp.where(md, _dot(qd, kd, NT), 0.0)
    for l in range(NLEV):
        a = a + jnp.where(masks[l], _dot(qt[l], kt[l], NT), 0.0)
    return a, qt, kt, qd, kd


def _gla_fwd_call(gqk, gv, glr, gg, wg, bg, gain, pmat):
    tp = gqk.shape[0]
    nc = tp // GC

    def body(qk_ref, v_ref, glr_ref, gg_ref, wg_ref, bg_ref, g_ref, p_ref, o_ref, a_ref, st_ref, s_scr):
        @pl.when(pl.program_id(0) == 0)
        def _():
            s_scr[...] = jnp.zeros_like(s_scr)

        _, la = _gla_log_decay(glr_ref, wg_ref, bg_ref)
        e = _exact_pm(p_ref[...], la)
        masks, md, second = _gla_masks()
        for h in range(GLA_HEADS):
            q = qk_ref[:, h * GLA_K:(h + 1) * GLA_K]
            k = qk_ref[:, GLA_KW + h * GLA_K:GLA_KW + (h + 1) * GLA_K]
            vs = slice(h * GLA_V, (h + 1) * GLA_V)
            v = v_ref[:, vs]
            fq, fk, ed, edi, eb, ee, ebl = _gla_factors(e, h, second)
            a, *_ = _gla_scores(q, k, fq, fk, ed, edi, masks, md)
            sb = s_scr[h].astype(BF16)
            st_ref[0, h] = sb
            o = _dot(a.astype(BF16), v, NN) + _dot((q * eb).astype(BF16), sb, NT)
            s_scr[h] = s_scr[h] * ebl + _dot(v, (k * ee).astype(BF16), TN)
            o_ref[:, vs] = o
            xh = o * lax.rsqrt(jnp.mean(o * o, axis=-1, keepdims=True) + EPS)
            g = gg_ref[:, vs]
            a_ref[:, vs] = (xh * g_ref[:, vs] * (g * _sigmoid(g))).astype(BF16)

    return pl.pallas_call(
        body, name="gla_fwd", grid=(nc,),
        in_specs=[pl.BlockSpec((GC, 2 * GLA_KW), lambda n: (n, 0)),
                  pl.BlockSpec((GC, GLA_W), lambda n: (n, 0)),
                  pl.BlockSpec((GC, 128), lambda n: (n, 0)),
                  pl.BlockSpec((GC, GLA_W), lambda n: (n, 0)),
                  pl.BlockSpec((128, GLA_KW), lambda n: (0, 0)),
                  pl.BlockSpec((1, GLA_KW), lambda n: (0, 0)),
                  pl.BlockSpec((1, GLA_W), lambda n: (0, 0)),
                  pl.BlockSpec((P_ROWS, GC), lambda n: (0, 0))],
        out_specs=[pl.BlockSpec((GC, GLA_W), lambda n: (n, 0)),
                   pl.BlockSpec((GC, GLA_W), lambda n: (n, 0)),
                   pl.BlockSpec((1, GLA_HEADS, GLA_V, GLA_K), lambda n: (n, 0, 0, 0))],
        out_shape=[jax.ShapeDtypeStruct((tp, GLA_W), F32), jax.ShapeDtypeStruct((tp, GLA_W), BF16),
                   jax.ShapeDtypeStruct((nc, GLA_HEADS, GLA_V, GLA_K), BF16)],
        scratch_shapes=[pltpu.VMEM((GLA_HEADS, GLA_V, GLA_K), F32)],
        compiler_params=_cparams(1),
    )(gqk, gv, glr, gg, wg, bg, gain, pmat)


def _gla_bwd_call(gqk, gv, glr, gg, o_gla, da, states, wg, bg, gain, pmat, pmat_t, exchange=()):
    tp = gqk.shape[0]
    nc = tp // GC
    o_gv, o_gg, o_lr = 2 * GLA_KW, 2 * GLA_KW + GLA_W, 2 * GLA_KW + 2 * GLA_W
    n_xc = len(exchange)

    def body(qk_ref, v_ref, glr_ref, gg_ref, o_ref, da_ref, st_ref, wg_ref, bg_ref, g_ref, p_ref, pt_ref, *rest):
        xc_src = rest[:n_xc]
        dp_ref, dwg_ref, dbg_ref, dg_ref = rest[n_xc:n_xc + 4]
        xc_dst = rest[n_xc + 4:2 * n_xc + 4]
        ds_scr, de_scr = rest[2 * n_xc + 4:2 * n_xc + 6]
        n = pl.program_id(0)
        if n_xc:
            copies = _exchange_copies(xc_src, xc_dst, rest[-2], rest[-1])

            @pl.when(n == 0)
            def _():
                for cp in copies:
                    cp.start()

            @pl.when(n == nc - 1)
            def _():
                for cp in copies:
                    cp.wait()

        @pl.when(n == 0)
        def _():
            ds_scr[...] = jnp.zeros_like(ds_scr)
            dwg_ref[...] = jnp.zeros_like(dwg_ref)
            dbg_ref[...] = jnp.zeros_like(dbg_ref)
            dg_ref[...] = jnp.zeros_like(dg_ref)
            de_scr[...] = jnp.zeros_like(de_scr)

        z, la = _gla_log_decay(glr_ref, wg_ref, bg_ref)
        e = _exact_pm(p_ref[...], la)
        masks, md, second = _gla_masks()
        for h in range(GLA_HEADS):
            cs = slice(h * GLA_K, (h + 1) * GLA_K)
            vs = slice(h * GLA_V, (h + 1) * GLA_V)
            o = o_ref[:, vs]
            rstd = lax.rsqrt(jnp.mean(o * o, axis=-1, keepdims=True) + EPS)
            xh = o * rstd
            gain_h = g_ref[:, vs]
            g = gg_ref[:, vs]
            sg = _sigmoid(g)
            dah = da_ref[:, vs]
            dp_ref[:, o_gg + h * GLA_V:o_gg + (h + 1) * GLA_V] = (
                dah * (xh * gain_h) * (sg * (1.0 + g * (1.0 - sg)))).astype(BF16)
            dn = dah * (g * sg)
            dg_ref[:, vs] += jnp.sum(dn * xh, axis=0, keepdims=True)
            dxh = dn * gain_h
            do = rstd * (dxh - xh * jnp.mean(dxh * xh, axis=-1, keepdims=True))
            dob = do.astype(BF16)
            q = qk_ref[:, cs]
            k = qk_ref[:, GLA_KW + h * GLA_K:GLA_KW + (h + 1) * GLA_K]
            v = v_ref[:, vs]
            fq, fk, ed, edi, eb, ee, ebl = _gla_factors(e, h, second)
            a, qt, kt, qd, kd = _gla_scores(q, k, fq, fk, ed, edi, masks, md)
            sp = st_ref[0, h]
            ds = ds_scr[h]
            dsb = ds.astype(BF16)
            q_in = q * eb
            k_end = k * ee
            da_s = _dot(dob, v, NT)
            dv = _dot(a.astype(BF16), dob, TN) + _dot(k_end.astype(BF16), dsb, NT)
            dq_in = _dot(dob, sp, NN)
            dk_end = _dot(v, dsb, NN)
            dbl = jnp.sum(sp.astype(F32) * ds, axis=0, keepdims=True) * ebl
            ds_scr[h] = ds * ebl + _dot(dob, q_in.astype(BF16), TN)
            dq = dq_in * eb
            dk = dk_end * ee
            de_scr[(2 * NLEV + 1) * GC:(2 * NLEV + 2) * GC, cs] = dq_in * q_in
            de_scr[(2 * NLEV + 2) * GC:(2 * NLEV + 3) * GC, cs] = dk_end * k_end
            de_scr[NBLK * GC:NBLK * GC + 1, cs] = dbl
            for l in range(NLEV):
                dal = jnp.where(masks[l], da_s, 0.0).astype(BF16)
                dqt = _dot(dal, kt[l], NN)
                dkt = _dot(dal, qt[l], TN)
                dq = dq + dqt * fq[l]
                dk = dk + dkt * fk[l]
                de_scr[l * GC:(l + 1) * GC, cs] = dqt * (q * fq[l])
                de_scr[(NLEV + l) * GC:(NLEV + l + 1) * GC, cs] = dkt * (k * fk[l])
            dad = jnp.where(md, da_s, 0.0).astype(BF16)
            dqd = _dot(dad, kd, NN)
            dkd = _dot(dad, qd, TN)
            dq = dq + dqd * ed
            dk = dk + dkd * edi
            de_scr[2 * NLEV * GC:(2 * NLEV + 1) * GC, cs] = dqd * (q * ed) - dkd * (k * edi)
            dp_ref[:, cs] = (dq * (GLA_K ** -0.5)).astype(BF16)
            dp_ref[:, GLA_KW + h * GLA_K:GLA_KW + (h + 1) * GLA_K] = dk.astype(BF16)
            dp_ref[:, o_gv + h * GLA_V:o_gv + (h + 1) * GLA_V] = dv.astype(BF16)
        dla = _exact_pm(pt_ref[...], de_scr[...])
        row = (nc - 1 - n) * GC + lax.broadcasted_iota(jnp.int32, (GC, 1), 0)
        dz = jnp.where(row >= PADF, dla * (1.0 / GATE_TAU) * _sigmoid(-z), 0.0)
        dzb = dz.astype(BF16)
        dp_ref[:, o_lr:] = _dot(dzb, wg_ref[...], NT).astype(BF16)
        dwg_ref[...] += _dot(glr_ref[...].astype(BF16), dzb, TN)
        dbg_ref[...] += jnp.sum(dz, axis=0, keepdims=True)

    rev = lambda n: (nc - 1 - n, 0)
    const = lambda n: (0, 0)
    xc_shapes, xc_sems = _exchange_shapes(exchange) if n_xc else ([], [])
    return pl.pallas_call(
        body, name="gla_bwd", grid=(nc,),
        in_specs=[pl.BlockSpec((GC, 2 * GLA_KW), rev),
                  pl.BlockSpec((GC, GLA_W), rev),
                  pl.BlockSpec((GC, 128), rev),
                  pl.BlockSpec((GC, GLA_W), rev),
                  pl.BlockSpec((GC, GLA_W), rev),
                  pl.BlockSpec((GC, GLA_W), rev),
                  pl.BlockSpec((1, GLA_HEADS, GLA_V, GLA_K), lambda n: (nc - 1 - n, 0, 0, 0)),
                  pl.BlockSpec((128, GLA_KW), const),
                  pl.BlockSpec((1, GLA_KW), const),
                  pl.BlockSpec((1, GLA_W), const),
                  pl.BlockSpec((P_ROWS, GC), const),
                  pl.BlockSpec((GC, P_ROWS), const)] + [ANY] * n_xc,
        out_specs=[pl.BlockSpec((GC, W_GP), rev), pl.BlockSpec((128, GLA_KW), const),
                   pl.BlockSpec((1, GLA_KW), const), pl.BlockSpec((1, GLA_W), const)] + [ANY] * n_xc,
        out_shape=[jax.ShapeDtypeStruct((tp, W_GP), BF16), jax.ShapeDtypeStruct((128, GLA_KW), F32),
                   jax.ShapeDtypeStruct((1, GLA_KW), F32), jax.ShapeDtypeStruct((1, GLA_W), F32)] + xc_shapes,
        scratch_shapes=[pltpu.VMEM((GLA_HEADS, GLA_V, GLA_K), F32), pltpu.VMEM((P_ROWS, GLA_KW), F32)] + xc_sems,
        compiler_params=_cparams(1),
    )(gqk, gv, glr, gg, o_gla, da, states, wg, bg, gain, pmat, pmat_t, *exchange)


def _mid_call(a_ret, a_gla, mg, h0, tgt, wbr, wbg, wout, gf):
    tp = h0.shape[0]
    nt = tp // TM

    def body(ar_ref, ag_ref, mg_ref, h_ref, t_ref, wbr_ref, wbg_ref, wo_ref, gf_ref,
             dh1_ref, dar_ref, dag_ref, dm_ref, mb_ref, dh1b_ref, dprb_ref, dpgb_ref, loss_ref, dgf_ref):
        i = pl.program_id(0)

        @pl.when(i == 0)
        def _():
            loss_ref[...] = jnp.zeros_like(loss_ref)
            dgf_ref[...] = jnp.zeros_like(dgf_ref)

        ar, ag = ar_ref[...], ag_ref[...]
        pr = _dot(ar, wbr_ref[...], NN)
        pg = _dot(ag, wbg_ref[...], NN)
        sr = _sigmoid(mg_ref[:, :D_MODEL])
        sg = _sigmoid(mg_ref[:, D_MODEL:])
        merged = (sr * pr + sg * pg).astype(BF16)
        mb_ref[...] = merged
        h1 = h_ref[...] + _dot(merged, wo_ref[...], NN)
        r1 = lax.rsqrt(jnp.mean(h1 * h1, axis=-1, keepdims=True) + EPS)
        xh = h1 * r1
        gfv = gf_ref[...]
        live = jnp.where(i > 0, 1.0, 0.0).astype(F32)
        err = (xh * gfv - t_ref[...]) * live
        loss_ref[...] += jnp.full(loss_ref.shape, 0.5 / D_MODEL, F32) * jnp.sum(err * err)
        dy = err * (1.0 / D_MODEL)
        dgf_ref[...] += jnp.sum(dy * xh, axis=0, keepdims=True)
        dxh = dy * gfv
        dh1 = r1 * (dxh - xh * jnp.mean(dxh * xh, axis=-1, keepdims=True))
        dh1_ref[...] = dh1
        dh1b = dh1.astype(BF16)
        dh1b_ref[...] = dh1b
        dmerged = _dot(dh1b, wo_ref[...], NT)
        dm_ref[:, :D_MODEL] = (dmerged * pr * sr * (1.0 - sr)).astype(BF16)
        dm_ref[:, D_MODEL:] = (dmerged * pg * sg * (1.0 - sg)).astype(BF16)
        dpr = (dmerged * sr).astype(BF16)
        dpg = (dmerged * sg).astype(BF16)
        dprb_ref[...] = dpr
        dpgb_ref[...] = dpg
        dar_ref[...] = _dot(dpr, wbr_ref[...], NT)
        dag_ref[...] = _dot(dpg, wbg_ref[...], NT)

    tile = lambda w: pl.BlockSpec((TM, w), lambda i: (i, 0))
    const = lambda r, w: pl.BlockSpec((r, w), lambda i: (0, 0))
    return pl.pallas_call(
        body, name="merge_out_loss", grid=(nt,),
        in_specs=[tile(RET_W), tile(GLA_W), tile(W_M), tile(D_MODEL),
                  pl.BlockSpec((TM, D_MODEL), lambda i: (jnp.maximum(i - 1, 0), 0)),
                  const(RET_W, D_MODEL), const(GLA_W, D_MODEL), const(D_MODEL, D_MODEL), const(1, D_MODEL)],
        out_specs=[tile(D_MODEL), tile(RET_W), tile(GLA_W), tile(W_M), tile(D_MODEL), tile(D_MODEL), tile(D_MODEL),
                   tile(D_MODEL), const(1, 128), const(1, D_MODEL)],
        out_shape=[jax.ShapeDtypeStruct((tp, D_MODEL), F32), jax.ShapeDtypeStruct((tp, RET_W), F32),
                   jax.ShapeDtypeStruct((tp, GLA_W), F32), jax.ShapeDtypeStruct((tp, W_M), BF16),
                   jax.ShapeDtypeStruct((tp, D_MODEL), BF16), jax.ShapeDtypeStruct((tp, D_MODEL), BF16),
                   jax.ShapeDtypeStruct((tp, D_MODEL), BF16), jax.ShapeDtypeStruct((tp, D_MODEL), BF16),
                   jax.ShapeDtypeStruct((1, 128), F32), jax.ShapeDtypeStruct((1, D_MODEL), F32)],
        compiler_params=_cparams(1),
    )(a_ret, a_gla, mg, h0, tgt, wbr, wbg, wout, gf)


def _device_step(x2d, tgt2d, meta, norm_gain, w_in_bf, w_gate_up, b_gate, ret_gain, gla_gain, wbr, wbg, wout,
                 final_gain, ck):
    seq = x2d.shape[0]
    tp = T0 + seq
    h0 = jnp.concatenate([jnp.zeros((PADF, D_MODEL), F32), meta, x2d], axis=0)
    w_r = w_in_bf[:, :W_R]
    w_g = jnp.pad(w_in_bf[:, W_R:W_R + W_G], ((0, 0), (0, W_GP - W_G)))
    w_m = w_in_bf[:, W_R + W_G:]
    wg_pad = jnp.pad(w_gate_up, ((0, 128 - GATE_RANK), (0, 0))).astype(BF16)

    pos = jnp.arange(tp, dtype=F32) - PADF
    half = RET_QK // 2
    inv = ROPE_BASE ** (-jnp.arange(half, dtype=F32) / half)
    ang = pos[:, None] * inv[None, :]
    cos, sin = jnp.cos(ang), jnp.sin(ang)
    lgam = jnp.log1p(-(2.0 ** (-5.0 - jnp.arange(RET_HEADS, dtype=F32))))
    pm_np = _gla_p_matrix()
    pmat = jnp.asarray(pm_np, BF16)
    pmat_t = jnp.asarray(pm_np.T.copy(), BF16)

    u = _rms_call(h0, norm_gain)
    tab = pl.BlockSpec((TB, half), lambda j, i: (i, 0))
    rqk = _mm_nn("proj_rqk", u, w_r, BF16, D_MODEL, 0, 2 * D_MODEL, _rope_epilogue, (cos, sin), (tab, tab))
    rv = _mm_nn("proj_rv", u, w_r, BF16, D_MODEL, 2 * D_MODEL, RET_W)
    rg = _mm_nn("proj_rg", u, w_r, F32, D_MODEL, 4 * D_MODEL, RET_W)
    gqk = _mm_nn("proj_gqk", u, w_g, F32, 2 * GLA_KW, 0, 2 * GLA_KW, _gqk_epilogue)
    gv = _mm_nn("proj_gv", u, w_g, BF16, GLA_W, 2 * GLA_KW, GLA_W)
    gg = _mm_nn("proj_gg", u, w_g, F32, GLA_W, 2 * GLA_KW + GLA_W, GLA_W)
    glr = _mm_nn("proj_glr", u, w_g, F32, 128, 2 * GLA_KW + 2 * GLA_W, 128)
    mg = _mm_nn("proj_mg", u, w_m, F32, D_MODEL, 0, W_M)

    o_ret, a_ret, st_ret = _ret_fwd_call(rqk, rv, rg, ret_gain, lgam)
    o_gla, a_gla, st_gla = _gla_fwd_call(gqk, gv, glr, gg, wg_pad, b_gate, gla_gain, pmat)

    gf = final_gain.reshape(1, D_MODEL)
    (dh1, da_ret, da_gla, dm, merged_b, dh1_b, dpr_b, dpg_b, loss, dgf) = _mid_call(
        a_ret, a_gla, mg, h0, tgt2d, wbr, wbg, wout, gf)

    names_b = ("w_branch_ret", "w_branch_gla", "w_out")
    g2_b = [_mm_tn("dw_br", a_ret, dpr_b, D_MODEL).reshape(4, 2, RET_W // 8, D_MODEL).transpose(1, 0, 2, 3),
            _mm_tn("dw_bg", a_gla, dpg_b, D_MODEL).reshape(4, 2, GLA_W // 8, D_MODEL).transpose(1, 0, 2, 3),
            _mm_tn("dw_out", merged_b, dh1_b, D_MODEL).reshape(4, 2, D_MODEL // 8, D_MODEL).transpose(1, 0, 2, 3)]
    sib_b = _swap_halves_call("swap_halves_branch", g2_b)
    sum_b = [_add_half_call("add_half_" + nm, g, b, ck) for nm, g, b in zip(names_b, g2_b, sib_b)]
    d_g, dwg, dbg, dgla_gain, *chips_b = _gla_bwd_call(gqk, gv, glr, gg, o_gla, da_gla, st_gla, wg_pad, b_gate,
                                                       gla_gain, pmat, pmat_t, exchange=sum_b)
    mine = [_add_chips_call("add_chips_" + nm, g, b, p, ck) for nm, g, b, p in zip(names_b, g2_b, sib_b, chips_b)]

    d_r, dret_gain = _ret_bwd_call(rqk, rv, rg, o_ret, da_ret, st_ret, ret_gain, lgam, cos, sin)

    dw_in = jnp.concatenate([_mm_tn("dw_r", u, d_r, D_MODEL), _mm_tn("dw_g", u, d_g, 640)[:, :W_G],
                             _mm_tn("dw_m", u, dm, D_MODEL)], axis=1)
    g2_in = dw_in.reshape(2, D_MODEL // 2, 4, IN_COLS // 4).transpose(0, 2, 1, 3)
    (sib_in,) = _swap_halves_call("swap_halves_in", [g2_in])
    sum_in = _add_half_call("add_half_w_in", g2_in, sib_in, ck)

    du = _mm_nt_acc("du_m", dm, w_m, D_MODEL)[0]
    du = _mm_nt_acc("du_g", d_g, w_g, W_GP // 5, acc_in=du)[0]
    tile = pl.BlockSpec((TB, D_MODEL), lambda i, kk: (i, 0))
    row = pl.BlockSpec((1, D_MODEL), lambda i, kk: (0, 0))
    dh0, dnorm_gain, chips_in = _mm_nt_acc(
        "du_r", d_r, w_r, D_MODEL, acc_in=du, epilogue=_rms_bwd_epilogue, extras=(h0, norm_gain, dh1),
        extra_specs=(tile, row, tile),
        extra_out_shapes=(jax.ShapeDtypeStruct((tp, D_MODEL), F32), jax.ShapeDtypeStruct((1, D_MODEL), F32)),
        extra_out_specs=(tile, row), exchange=[sum_in])
    mine = [_add_chips_call("add_chips_w_in", g2_in, sib_in, chips_in, ck)] + mine
    full = _join_halves_call("join_halves", mine)

    return dict(loss=loss[0, 0], dx=dh0[T0:], dmeta=dh0[PADF:T0], norm_gain=dnorm_gain, w_gate_up=dwg[:GATE_RANK], b_gate=dbg,
                ret_norm_gain=dret_gain, gla_norm_gain=dgla_gain, final_norm_gain=dgf.reshape(D_MODEL),
                w_in=full[0], w_branch_ret=full[1], w_branch_gla=full[2], w_out=full[3])


MESH = pl.DeviceIdType.MESH
ANY = pl.BlockSpec(memory_space=pl.ANY)


def _place():
    return lax.axis_index("x"), lax.axis_index("y"), lax.axis_index("c")


def _gather8_call(name, parts):
    n = len(parts)

    def body(*refs):
        x_refs, out_refs = refs[:n], refs[n:2 * n]
        send_sems, recv_sems, local_sems = refs[2 * n:]
        x, y, c = _place()
        me, sibling = (x, y, c), (x, y, 1 - c)
        chips = [(1 - x, y), (x, 1 - y), (1 - x, 1 - y)]

        def slot(t, px, py, pc):
            return out_refs[t].at[4 * px + 2 * py + pc]

        def copy(t, k, block, to, src=None):
            return pltpu.make_async_remote_copy(
                src_ref=slot(t, *block) if src is None else src, dst_ref=slot(t, *block),
                send_sem=send_sems.at[7 * t + k], recv_sem=recv_sems.at[7 * t + k], device_id=to, device_id_type=MESH)

        mine = [pltpu.make_async_copy(x_refs[t], slot(t, *me), local_sems.at[t]) for t in range(n)]
        for cp in mine:
            cp.start()
        first = []
        for t in range(n):
            first.append(copy(t, 0, me, sibling, src=x_refs[t]))
            first += [copy(t, 1 + j, me, (*chip, c), src=x_refs[t]) for j, chip in enumerate(chips)]
        for cp in first:
            cp.start()
        passed = []
        for j, chip in enumerate(chips):
            for t in range(n):
                copy(t, 1 + j, (*chip, c), me).wait_recv()
                fwd = copy(t, 4 + j, (*chip, c), sibling)
                fwd.start()
                passed.append(fwd)
        for t in range(n):
            copy(t, 0, sibling, me).wait_recv()
            for j, chip in enumerate(chips):
                copy(t, 4 + j, (*chip, 1 - c), me).wait_recv()
        for cp in first + passed:
            cp.wait_send()
        for cp in mine:
            cp.wait()

    return pl.pallas_call(
        body, name=name,
        out_shape=[jax.ShapeDtypeStruct((8,) + p.shape, p.dtype) for p in parts],
        in_specs=[ANY] * n, out_specs=[ANY] * n,
        scratch_shapes=[pltpu.SemaphoreType.DMA((7 * n,)), pltpu.SemaphoreType.DMA((7 * n,)),
                        pltpu.SemaphoreType.DMA((n,))],
    )(*parts)


def _swap_halves_call(name, gs):
    n = len(gs)

    def body(*refs):
        g_refs, b_refs = refs[:n], refs[n:2 * n]
        send_sems, recv_sems = refs[2 * n:]
        x, y, c = _place()
        copies = [pltpu.make_async_remote_copy(
            src_ref=g_refs[t].at[1 - c], dst_ref=b_refs[t], send_sem=send_sems.at[t], recv_sem=recv_sems.at[t],
            device_id=(x, y, 1 - c), device_id_type=MESH) for t in range(n)]
        for cp in copies:
            cp.start()
        for cp in copies:
            cp.wait()

    return pl.pallas_call(
        body, name=name,
        out_shape=[jax.ShapeDtypeStruct(g.shape[1:], g.dtype) for g in gs],
        in_specs=[ANY] * n, out_specs=[ANY] * n,
        scratch_shapes=[pltpu.SemaphoreType.DMA((n,)), pltpu.SemaphoreType.DMA((n,))],
    )(*gs)


def _join_halves_call(name, ts):
    n = len(ts)

    def body(*refs):
        o_refs = refs[n:2 * n]
        send_sems, recv_sems = refs[2 * n:]
        x, y, c = _place()
        copies = [pltpu.make_async_remote_copy(
            src_ref=o_refs[t].at[c], dst_ref=o_refs[t].at[c], send_sem=send_sems.at[t], recv_sem=recv_sems.at[t],
            device_id=(x, y, 1 - c), device_id_type=MESH) for t in range(n)]
        for cp in copies:
            cp.start()
        for t in range(n):
            copies[t].wait_send()
            pltpu.make_async_remote_copy(
                src_ref=o_refs[t].at[c], dst_ref=o_refs[t].at[1 - c], send_sem=send_sems.at[t],
                recv_sem=recv_sems.at[t], device_id=(x, y, 1 - c), device_id_type=MESH).wait_recv()

    return pl.pallas_call(
        body, name=name,
        out_shape=[jax.ShapeDtypeStruct(t.shape, t.dtype) for t in ts],
        in_specs=[ANY] * n, out_specs=[ANY] * n, input_output_aliases={t: t for t in range(n)},
        scratch_shapes=[pltpu.SemaphoreType.DMA((n,)), pltpu.SemaphoreType.DMA((n,))],
    )(*ts)


def _row_block(rows, cols, budget):
    best = 8
    for rb in range(8, rows + 1, 8):
        if rows % rb == 0 and rb * cols * 4 <= budget:
            best = rb
    return best


def _add_half_call(name, g, b, ck):
    _, _, r, cc = g.shape
    rb = _row_block(r, cc, 2 * 1024 * 1024)

    def body(ck_ref, g_ref, b_ref, o_ref):
        o_ref[...] = (g_ref[...] + b_ref[...]).astype(BF16)

    return pl.pallas_call(
        body, name=name,
        grid_spec=pltpu.PrefetchScalarGridSpec(
            num_scalar_prefetch=1, grid=(4, r // rb),
            in_specs=[pl.BlockSpec((None, None, rb, cc), lambda k, i, ck_ref: (ck_ref[0], k, i, 0)),
                      pl.BlockSpec((None, rb, cc), lambda k, i, ck_ref: (k, i, 0))],
            out_specs=pl.BlockSpec((None, rb, cc), lambda k, i, ck_ref: (k, i, 0))),
        out_shape=jax.ShapeDtypeStruct(b.shape, BF16),
        compiler_params=_cparams(2),
    )(ck, g, b)


def _add_chips_call(name, g, b, p, ck):
    _, _, r, cc = g.shape
    rb = _row_block(r, cc, 2 * 1024 * 1024)

    def body(ck_ref, g_ref, b_ref, p0_ref, p1_ref, p2_ref, o_ref):
        own = g_ref[...] + b_ref[...]
        o_ref[...] = ((own + p0_ref[...].astype(F32)) + p1_ref[...].astype(F32)) + p2_ref[...].astype(F32)

    def peer(j):
        return pl.BlockSpec((None, rb, cc), lambda i, ck_ref: (j, i, 0))

    return pl.pallas_call(
        body, name=name,
        grid_spec=pltpu.PrefetchScalarGridSpec(
            num_scalar_prefetch=1, grid=(r // rb,),
            in_specs=[pl.BlockSpec((None, None, rb, cc), lambda i, ck_ref: (ck_ref[0], ck_ref[1], i, 0)),
                      pl.BlockSpec((None, rb, cc), lambda i, ck_ref: (ck_ref[1], i, 0)),
                      peer(0), peer(1), peer(2)],
            out_specs=pl.BlockSpec((None, rb, cc), lambda i, ck_ref: (ck_ref[0], i, 0))),
        out_shape=jax.ShapeDtypeStruct((2, r, cc), F32),
        compiler_params=_cparams(1),
    )(ck, g, b, p, p, p)


def _sum8_call(name, g):
    def body(g_ref, o_ref):
        acc = g_ref[0]
        for d in range(1, 8):
            acc = acc + g_ref[d]
        o_ref[...] = acc

    return pl.pallas_call(body, name=name, out_shape=jax.ShapeDtypeStruct(g.shape[1:], F32))(g)


def _adamw_call(name, w, g, m, v):
    r, cc = w.shape
    rb = _row_block(r, cc, 1024 * 1024) if r % 8 == 0 else r

    def body(w_ref, g_ref, m_ref, v_ref, d_ref, m2_ref, v2_ref):
        gv = g_ref[...]
        m2 = ADAM_B1 * m_ref[...] + (1.0 - ADAM_B1) * gv
        v2 = ADAM_B2 * v_ref[...] + (1.0 - ADAM_B2) * (gv * gv)
        m_hat = m2 / (1.0 - ADAM_B1 ** ADAM_STEP)
        v_hat = v2 / (1.0 - ADAM_B2 ** ADAM_STEP)
        d_ref[...] = -ADAM_LR * (m_hat / (jnp.sqrt(v_hat) + ADAM_EPS) + ADAM_WD * w_ref[...])
        m2_ref[...] = m2
        v2_ref[...] = v2

    spec = pl.BlockSpec((rb, cc), lambda i: (i, 0))
    return pl.pallas_call(
        body, name=name, grid=(r // rb,), in_specs=[spec] * 4, out_specs=[spec] * 3,
        out_shape=[jax.ShapeDtypeStruct((r, cc), F32)] * 3, compiler_params=_cparams(1),
    )(w, g, m, v)


SMALL = (("norm_gain", D_MODEL), ("b_gate", GLA_KW), ("ret_norm_gain", RET_W), ("gla_norm_gain", GLA_W),
         ("final_norm_gain", D_MODEL), ("w_gate_up", GATE_RANK * GLA_KW), ("meta_tokens", N_META * D_MODEL))


def _pack_rows(vecs, rows):
    flat = jnp.concatenate([v.reshape(-1) for v in vecs])
    return jnp.pad(flat, (0, rows * 128 - flat.shape[0])).reshape(rows, 128)


def kernel(x, meta_tokens, norm_gain, w_in, w_gate_up, b_gate, ret_norm_gain, gla_norm_gain, w_branch_ret, w_branch_gla, w_out, final_norm_gain, loss_target, m_meta_tokens, m_norm_gain, m_w_in, m_w_gate_up, m_b_gate, m_ret_norm_gain, m_gla_norm_gain, m_w_branch_ret, m_w_branch_gla, m_w_out, m_final_norm_gain, v_meta_tokens, v_norm_gain, v_w_in, v_w_gate_up, v_b_gate, v_ret_norm_gain, v_gla_norm_gain, v_w_branch_ret, v_w_branch_gla, v_w_out, v_final_norm_gain):
    xi, yi, ci = _place()
    kme = 2 * xi + yi
    ck = jnp.stack([ci, kme]).astype(jnp.int32)
    sw_in = w_in.shape[2]

    def my_half(a, dtype):
        r, cc = a.shape
        return lax.dynamic_index_in_dim(a.reshape(2, r // 2, cc), ci, 0, keepdims=False).astype(dtype)

    g_in, g_br, g_bg, g_out, g_meta, g_wg = _gather8_call(
        "gather_weights",
        [my_half(w_in[0], BF16), my_half(w_branch_ret[0], BF16), my_half(w_branch_gla[0], BF16),
         my_half(w_out[0], BF16), my_half(meta_tokens, F32), my_half(w_gate_up[0], F32)])
    w_in_bf = g_in.reshape(4, 2, D_MODEL // 2, sw_in).transpose(1, 2, 0, 3).reshape(D_MODEL, 4 * sw_in)
    wbr = g_br.reshape(RET_W, D_MODEL)
    wbg = g_bg.reshape(GLA_W, D_MODEL)
    wout = g_out.reshape(D_MODEL, D_MODEL)
    meta = g_meta.reshape(4, 2, N_META // 2, D_MODEL // 4).transpose(1, 2, 0, 3).reshape(N_META, D_MODEL)
    wg_full = g_wg.reshape(4, 2, GATE_RANK // 2, GLA_KW // 4).transpose(1, 2, 0, 3).reshape(GATE_RANK, GLA_KW)

    loc = _device_step(x[0], loss_target[0], meta, norm_gain, w_in_bf, wg_full, b_gate, ret_norm_gain, gla_norm_gain,
                       wbr, wbg, wout, final_norm_gain, ck)
    loss = lax.psum(loc["loss"], ("x", "y", "c"))
    names = ("w_in", "w_branch_ret", "w_branch_gla", "w_out")
    full = [loc[nm] for nm in names]
    big_w = dict(w_in=w_in[0], w_branch_ret=w_branch_ret[0], w_branch_gla=w_branch_gla[0], w_out=w_out[0])
    big_m = dict(w_in=m_w_in[0], w_branch_ret=m_w_branch_ret[0], w_branch_gla=m_w_branch_gla[0], w_out=m_w_out[0])
    big_v = dict(w_in=v_w_in[0], w_branch_ret=v_w_branch_ret[0], w_branch_gla=v_w_branch_gla[0], w_out=v_w_out[0])
    grads, deltas, new_m, new_v = {}, {}, {}, {}
    for nm, f in zip(names, full):
        shape = big_w[nm].shape
        g = f.reshape(shape)
        d, m2, v2 = _adamw_call("adamw_" + nm, big_w[nm], g, big_m[nm], big_v[nm])
        grads[nm], deltas[nm], new_m[nm], new_v[nm] = (a.reshape((1,) + shape) for a in (g, d, m2, v2))

    small_g = dict(loc)
    small_g["meta_tokens"] = loc["dmeta"]
    n_small = sum(sz for _, sz in SMALL)
    rows = -(-n_small // 128 // 8) * 8
    (g_small,) = _gather8_call("gather_small_grads", [_pack_rows([small_g[nm] for nm, _ in SMALL], rows)])
    tot = _sum8_call("sum_small_grads", g_small).reshape(-1)
    off = 0
    sg = {}
    for nm, sz in SMALL:
        sg[nm] = tot[off:off + sz]
        off += sz
    sg["w_gate_up"] = lax.dynamic_slice_in_dim(sg["w_gate_up"].reshape(GATE_RANK, GLA_KW), kme * (GLA_KW // 4),
                                               GLA_KW // 4, axis=1)
    sg["meta_tokens"] = lax.dynamic_slice_in_dim(sg["meta_tokens"].reshape(N_META, D_MODEL), kme * (D_MODEL // 4),
                                                 D_MODEL // 4, axis=1)
    small_w = dict(norm_gain=norm_gain, b_gate=b_gate, ret_norm_gain=ret_norm_gain, gla_norm_gain=gla_norm_gain,
                   final_norm_gain=final_norm_gain, w_gate_up=w_gate_up, meta_tokens=meta_tokens)
    small_m = dict(norm_gain=m_norm_gain, b_gate=m_b_gate, ret_norm_gain=m_ret_norm_gain,
                   gla_norm_gain=m_gla_norm_gain, final_norm_gain=m_final_norm_gain, w_gate_up=m_w_gate_up,
                   meta_tokens=m_meta_tokens)
    small_v = dict(norm_gain=v_norm_gain, b_gate=v_b_gate, ret_norm_gain=v_ret_norm_gain,
                   gla_norm_gain=v_gla_norm_gain, final_norm_gain=v_final_norm_gain, w_gate_up=v_w_gate_up,
                   meta_tokens=v_meta_tokens)
    order = [nm for nm, _ in SMALL]
    sizes = [small_w[nm].size for nm in order]
    prow = -(-sum(sizes) // 128 // 8) * 8
    pk = lambda d: _pack_rows([d[nm] for nm in order], prow)
    d_s, m_s, v_s = _adamw_call("adamw_small", pk(small_w), pk(sg), pk(small_m), pk(small_v))
    off = 0
    for nm, sz in zip(order, sizes):
        shape = small_w[nm].shape
        grads[nm] = sg[nm].reshape(shape)
        deltas[nm], new_m[nm], new_v[nm] = (a.reshape(-1)[off:off + sz].reshape(shape) for a in (d_s, m_s, v_s))
        off += sz

    out_order = ("meta_tokens", "norm_gain", "w_in", "w_gate_up", "b_gate", "ret_norm_gain", "gla_norm_gain",
                 "w_branch_ret", "w_branch_gla", "w_out", "final_norm_gain")
    dx = loc["dx"].reshape(x.shape)
    return (loss, dx, *[grads[nm] for nm in out_order], *[deltas[nm] for nm in out_order],
            *[new_m[nm] for nm in out_order], *[new_v[nm] for nm in out_order])
```

```python
import functools
import math

import numpy as np
import jax
import jax.numpy as jnp
from jax import lax
from jax.experimental import pallas as pl
from jax.experimental.pallas import tpu as pltpu

F32 = jnp.float32
BF16 = jnp.bfloat16

D_MODEL = 1024
N_META = 16
EPS = 1e-6
ROPE_BASE = 10000.0
RET_HEADS, RET_QK, RET_V = 4, 256, 512
RET_W = RET_HEADS * RET_V
GLA_HEADS, GLA_K, GLA_V = 4, 128, 256
GLA_W = GLA_HEADS * GLA_V
GLA_KW = GLA_HEADS * GLA_K
GATE_RANK = 16
GATE_TAU = 16.0
GLA_SUB = 16

TM = 256
T0 = TM
PADF = T0 - N_META
GC = 128
TB = 768
TK = 768

W_R = 6144
W_G = 3088
W_GP = 3200
W_M = 2048
IN_COLS = W_R + W_G + W_M

ADAM_LR, ADAM_B1, ADAM_B2, ADAM_EPS, ADAM_WD, ADAM_STEP = 0.001, 0.9, 0.999, 1e-08, 0.01, 10

VMEM_LIMIT = 56 * 1024 * 1024

NN = ((1,), (0,))
NT = ((1,), (1,))
TN = ((0,), (0,))


def _dot(a, b, dims):
    return lax.dot_general(a, b, (dims, ((), ())), preferred_element_type=F32)


def _cparams(n_axes):
    return pltpu.CompilerParams(dimension_semantics=("arbitrary",) * n_axes, vmem_limit_bytes=VMEM_LIMIT)


def _sigmoid(x):
    return 1.0 / (1.0 + jnp.exp(-x))


def _split3(x):
    hi = x.astype(BF16)
    r1 = x - hi.astype(F32)
    mid = r1.astype(BF16)
    lo = (r1 - mid.astype(F32)).astype(BF16)
    return hi, mid, lo


def _exact_pm(p, x):
    hi, mid, lo = _split3(x)
    return _dot(p, hi, NN) + _dot(p, mid, NN) + _dot(p, lo, NN)


def _rms_call(x2d, head, gain):
    tp = T0 + x2d.shape[0]

    def body(x_ref, hd_ref, g_ref, h_ref, u_ref):
        h = jnp.where(pl.program_id(0) == 0, hd_ref[...], x_ref[...])
        h_ref[...] = h
        r = lax.rsqrt(jnp.mean(h * h, axis=-1, keepdims=True) + EPS)
        u_ref[...] = (h * r * g_ref[...]).astype(BF16)

    tile = pl.BlockSpec((TM, D_MODEL), lambda i: (i, 0))
    return pl.pallas_call(
        body, name="rms_in", grid=(tp // TM,),
        in_specs=[pl.BlockSpec((TM, D_MODEL), lambda i: (jnp.maximum(i - 1, 0), 0)),
                  pl.BlockSpec((T0, D_MODEL), lambda i: (0, 0)), pl.BlockSpec((1, D_MODEL), lambda i: (0, 0))],
        out_specs=[tile, tile],
        out_shape=[jax.ShapeDtypeStruct((tp, D_MODEL), F32), jax.ShapeDtypeStruct((tp, D_MODEL), BF16)],
        compiler_params=_cparams(1),
    )(x2d, head, gain)


def _mm_nn(name, a, b, out_dtype, tn, col0, ncols, epilogue=None, extras=(), extra_specs=()):
    m, k = a.shape
    nj, j0 = ncols // tn, col0 // tn

    def body(a_ref, b_ref, *rest):
        *ex, o_ref = rest
        acc = _dot(a_ref[...], b_ref[...], NN)
        if epilogue is None:
            o_ref[...] = acc.astype(out_dtype)
        else:
            epilogue(acc, o_ref, *ex)

    return pl.pallas_call(
        body, name=name, grid=(nj, m // TB),
        in_specs=[pl.BlockSpec((TB, k), lambda j, i: (i, 0)), pl.BlockSpec((k, tn), lambda j, i: (0, j0 + j))]
        + list(extra_specs),
        out_specs=pl.BlockSpec((TB, tn), lambda j, i: (i, j)),
        out_shape=jax.ShapeDtypeStruct((m, ncols), out_dtype),
        compiler_params=_cparams(2),
    )(a, b, *extras)


def _rope_epilogue(acc, o_ref, cos_ref, sin_ref):
    scale = jnp.where(pl.program_id(0) == 1, RET_QK ** -0.5, 1.0).astype(F32)
    cos, sin = cos_ref[...], sin_ref[...]
    half = RET_QK // 2
    for h in range(RET_HEADS):
        t1 = acc[:, h * RET_QK:h * RET_QK + half]
        t2 = acc[:, h * RET_QK + half:(h + 1) * RET_QK]
        o_ref[:, h * RET_QK:h * RET_QK + half] = ((t1 * cos - t2 * sin) * scale).astype(BF16)
        o_ref[:, h * RET_QK + half:(h + 1) * RET_QK] = ((t2 * cos + t1 * sin) * scale).astype(BF16)


def _gqk_epilogue(acc, o_ref):
    o_ref[:, :GLA_KW] = acc[:, :GLA_KW] * (GLA_K ** -0.5)
    o_ref[:, GLA_KW:] = acc[:, GLA_KW:]


def _exchange_copies(s_refs, b_refs, send_sems, recv_sems):
    x, y, c = _place()
    chips = [(1 - x, y), (x, 1 - y), (1 - x, 1 - y)]
    return [pltpu.make_async_remote_copy(
        src_ref=s_refs[t].at[2 * chip[0] + chip[1]], dst_ref=b_refs[t].at[j], send_sem=send_sems.at[3 * t + j],
        recv_sem=recv_sems.at[3 * t + j], device_id=(*chip, c), device_id_type=MESH)
        for t in range(len(s_refs)) for j, chip in enumerate(chips)]


def _exchange_shapes(ss):
    return ([jax.ShapeDtypeStruct((3,) + s.shape[1:], s.dtype) for s in ss],
            [pltpu.SemaphoreType.DMA((3 * len(ss),)), pltpu.SemaphoreType.DMA((3 * len(ss),))])


def _mm_nt_acc(name, a, w, tk, acc_in=None, epilogue=None, extras=(), extra_specs=(), extra_out_shapes=(),
               extra_out_specs=(), exchange=()):
    m, k = a.shape
    n = w.shape[0]
    nk, ni = k // tk, m // TB
    has_acc = acc_in is not None
    n_xc = len(exchange)

    def body(*refs):
        a_ref, w_ref = refs[0], refs[1]
        pos = 2
        acc_ref = None
        if has_acc:
            acc_ref = refs[pos]
            pos += 1
        ex = refs[pos:pos + len(extras)]
        pos += len(extras)
        xc_src = refs[pos:pos + n_xc]
        pos += n_xc
        n_scr = 3 if n_xc else 1
        outs = refs[pos:len(refs) - n_scr - n_xc]
        xc_dst = refs[len(refs) - n_scr - n_xc:len(refs) - n_scr]
        scr = refs[len(refs) - n_scr]
        i, kk = pl.program_id(0), pl.program_id(1)
        if n_xc:
            copies = _exchange_copies(xc_src, xc_dst, refs[-2], refs[-1])

            @pl.when((i == 0) & (kk == 0))
            def _():
                for cp in copies:
                    cp.start()

        @pl.when(kk == 0)
        def _():
            scr[...] = acc_ref[...] if has_acc else jnp.zeros_like(scr)

        scr[...] += _dot(a_ref[...], w_ref[...], NT)

        @pl.when(kk == nk - 1)
        def _():
            if epilogue is None:
                outs[0][...] = scr[...]
            else:
                epilogue(scr[...], outs, i, *ex)

        if n_xc:
            @pl.when((i == ni - 1) & (kk == nk - 1))
            def _():
                for cp in copies:
                    cp.wait()

    in_specs = [pl.BlockSpec((TB, tk), lambda i, kk: (i, kk)), pl.BlockSpec((n, tk), lambda i, kk: (0, kk))]
    args = [a, w]
    if has_acc:
        in_specs.append(pl.BlockSpec((TB, n), lambda i, kk: (i, 0)))
        args.append(acc_in)
    in_specs += list(extra_specs) + [ANY] * n_xc
    args += list(extras) + list(exchange)
    if epilogue is None:
        out_shape = [jax.ShapeDtypeStruct((m, n), F32)]
        out_specs = [pl.BlockSpec((TB, n), lambda i, kk: (i, 0))]
    else:
        out_shape, out_specs = list(extra_out_shapes), list(extra_out_specs)
    scratch = [pltpu.VMEM((TB, n), F32)]
    if n_xc:
        xc_shapes, xc_sems = _exchange_shapes(exchange)
        out_shape += xc_shapes
        out_specs += [ANY] * n_xc
        scratch += xc_sems
    return pl.pallas_call(
        body, name=name, grid=(ni, nk), in_specs=in_specs, out_specs=out_specs, out_shape=out_shape,
        scratch_shapes=scratch, compiler_params=_cparams(2),
    )(*args)


def _rms_bwd_epilogue(du, outs, i, h_ref, g_ref, dh1_ref):
    dh0_ref, dg_ref = outs
    h = h_ref[...]
    r = lax.rsqrt(jnp.mean(h * h, axis=-1, keepdims=True) + EPS)
    xh = h * r
    dxh = du * g_ref[...]
    dh0_ref[...] = dh1_ref[...] + r * (dxh - xh * jnp.mean(dxh * xh, axis=-1, keepdims=True))

    @pl.when(i == 0)
    def _():
        dg_ref[...] = jnp.zeros_like(dg_ref)

    dg_ref[...] += jnp.sum(du * xh, axis=0, keepdims=True)


def _mm_tn(name, a, b, bn):
    t, m = a.shape
    n = b.shape[1]

    def body(a_ref, b_ref, o_ref):
        @pl.when(pl.program_id(1) == 0)
        def _():
            o_ref[...] = jnp.zeros_like(o_ref)

        o_ref[...] += _dot(a_ref[...], b_ref[...], TN)

    return pl.pallas_call(
        body, name=name, grid=(n // bn, t // TK),
        in_specs=[pl.BlockSpec((TK, m), lambda j, kk: (kk, 0)), pl.BlockSpec((TK, bn), lambda j, kk: (kk, j))],
        out_specs=pl.BlockSpec((m, bn), lambda j, kk: (0, j)),
        out_shape=jax.ShapeDtypeStruct((m, n), F32),
        compiler_params=_cparams(2),
    )(a, b)


def _ret_fill_decay(lg_ref, dm_scr):
    c = TM
    ii = lax.broadcasted_iota(jnp.int32, (c, c), 0)
    jj = lax.broadcasted_iota(jnp.int32, (c, c), 1)
    rel = (ii - jj).astype(F32)
    for h in range(RET_HEADS):
        dm_scr[h] = jnp.where(rel >= 0, jnp.exp(jnp.maximum(rel, 0.0) * lg_ref[h]), 0.0)


def _ret_consts(lg, dm_ref):
    c = TM
    idx = lax.broadcasted_iota(jnp.int32, (c, 1), 0).astype(F32)
    xi = jnp.exp((idx + 1.0) * lg)
    zeta = jnp.exp((c - 1.0 - idx) * lg)
    gc = jnp.exp(jnp.full((1, 1), c, F32) * lg)
    return dm_ref[...], xi, zeta, gc


def _ret_fwd_call(rqk, rv, rg, gain, lgam):
    tp = rqk.shape[0]
    nc = tp // TM

    def body(lg_ref, qk_ref, v_ref, rg_ref, g_ref, o_ref, a_ref, st_ref, s_scr, dm_scr):
        @pl.when(pl.program_id(0) == 0)
        def _():
            s_scr[...] = jnp.zeros_like(s_scr)
            _ret_fill_decay(lg_ref, dm_scr)

        for h in range(RET_HEADS):
            dm, xi, zeta, gc = _ret_consts(lg_ref[h], dm_scr.at[h])
            q = qk_ref[:, h * RET_QK:(h + 1) * RET_QK]
            k = qk_ref[:, D_MODEL + h * RET_QK:D_MODEL + (h + 1) * RET_QK]
            v = v_ref[:, h * RET_V:(h + 1) * RET_V]
            sb = s_scr[h].astype(BF16)
            st_ref[0, h] = sb
            s = _dot(q, k, NT) * dm
            o = _dot(s.astype(BF16), v, NN) + xi * _dot(q, sb, NN)
            kz = (k.astype(F32) * zeta).astype(BF16)
            s_scr[h] = gc * s_scr[h] + _dot(kz, v, TN)
            o_ref[:, h * RET_V:(h + 1) * RET_V] = o
            mu = jnp.mean(o, axis=-1, keepdims=True)
            xc = o - mu
            xh = xc * lax.rsqrt(jnp.mean(xc * xc, axis=-1, keepdims=True) + EPS)
            g = rg_ref[:, h * RET_V:(h + 1) * RET_V]
            a_ref[:, h * RET_V:(h + 1) * RET_V] = (
                xh * g_ref[:, h * RET_V:(h + 1) * RET_V] * (g * _sigmoid(g))).astype(BF16)

    return pl.pallas_call(
        body, name="ret_fwd", grid=(nc,),
        in_specs=[pl.BlockSpec(memory_space=pltpu.SMEM),
                  pl.BlockSpec((TM, 2 * D_MODEL), lambda n: (n, 0)),
                  pl.BlockSpec((TM, RET_W), lambda n: (n, 0)),
                  pl.BlockSpec((TM, RET_W), lambda n: (n, 0)),
                  pl.BlockSpec((1, RET_W), lambda n: (0, 0))],
        out_specs=[pl.BlockSpec((TM, RET_W), lambda n: (n, 0)),
                   pl.BlockSpec((TM, RET_W), lambda n: (n, 0)),
                   pl.BlockSpec((1, RET_HEADS, RET_QK, RET_V), lambda n: (n, 0, 0, 0))],
        out_shape=[jax.ShapeDtypeStruct((tp, RET_W), F32), jax.ShapeDtypeStruct((tp, RET_W), BF16),
                   jax.ShapeDtypeStruct((nc, RET_HEADS, RET_QK, RET_V), BF16)],
        scratch_shapes=[pltpu.VMEM((RET_HEADS, RET_QK, RET_V), F32), pltpu.VMEM((RET_HEADS, TM, TM), F32)],
        compiler_params=_cparams(1),
    )(lgam, rqk, rv, rg, gain)


def _ret_bwd_call(rqk, rv, rg, o_ret, da, states, gain, lgam, cos, sin):
    tp = rqk.shape[0]
    nc = tp // TM
    half = RET_QK // 2

    def body(lg_ref, qk_ref, v_ref, rg_ref, o_ref, da_ref, st_ref, g_ref, cos_ref, sin_ref, dp_ref, dg_ref, ds_scr,
             dm_scr):
        @pl.when(pl.program_id(0) == 0)
        def _():
            ds_scr[...] = jnp.zeros_like(ds_scr)
            dg_ref[...] = jnp.zeros_like(dg_ref)
            _ret_fill_decay(lg_ref, dm_scr)

        cos, sin = cos_ref[...], sin_ref[...]
        for h in range(RET_HEADS):
            hs = slice(h * RET_V, (h + 1) * RET_V)
            dm, xi, zeta, gc = _ret_consts(lg_ref[h], dm_scr.at[h])
            o = o_ref[:, hs]
            mu = jnp.mean(o, axis=-1, keepdims=True)
            xc = o - mu
            rstd = lax.rsqrt(jnp.mean(xc * xc, axis=-1, keepdims=True) + EPS)
            xh = xc * rstd
            gain_h = g_ref[:, hs]
            g = rg_ref[:, hs]
            sg = _sigmoid(g)
            silu = g * sg
            dah = da_ref[:, hs]
            dp_ref[:, 4 * D_MODEL + h * RET_V:4 * D_MODEL + (h + 1) * RET_V] = (
                dah * (xh * gain_h) * (sg * (1.0 + g * (1.0 - sg)))).astype(BF16)
            dn = dah * silu
            dg_ref[:, hs] += jnp.sum(dn * xh, axis=0, keepdims=True)
            dxh = dn * gain_h
            do = rstd * (dxh - jnp.mean(dxh, axis=-1, keepdims=True)
                         - xh * jnp.mean(dxh * xh, axis=-1, keepdims=True))
            dob = do.astype(BF16)
            q = qk_ref[:, h * RET_QK:(h + 1) * RET_QK]
            k = qk_ref[:, D_MODEL + h * RET_QK:D_MODEL + (h + 1) * RET_QK]
            v = v_ref[:, hs]
            sp = st_ref[0, h]
            ds = ds_scr[h]
            dsb = ds.astype(BF16)
            s = (_dot(q, k, NT) * dm).astype(BF16)
            dsc = (_dot(dob, v, NT) * dm).astype(BF16)
            dq = _dot(dsc, k, NN) + xi * _dot(dob, sp, NT)
            dk = _dot(dsc, q, TN) + zeta * _dot(v, dsb, NT)
            kz = (k.astype(F32) * zeta).astype(BF16)
            dv = _dot(s, dob, TN) + _dot(kz, dsb, NN)
            qx = (q.astype(F32) * xi).astype(BF16)
            ds_scr[h] = gc * ds + _dot(qx, dob, TN)
            dp_ref[:, 2 * D_MODEL + h * RET_V:2 * D_MODEL + (h + 1) * RET_V] = dv.astype(BF16)
            dk = dk * (RET_QK ** -0.5)
            for base, t in ((0, dq), (D_MODEL, dk)):
                t1, t2 = t[:, :half], t[:, half:]
                dp_ref[:, base + h * RET_QK:base + h * RET_QK + half] = (t1 * cos + t2 * sin).astype(BF16)
                dp_ref[:, base + h * RET_QK + half:base + (h + 1) * RET_QK] = (t2 * cos - t1 * sin).astype(BF16)

    rev = lambda n: (nc - 1 - n, 0)
    return pl.pallas_call(
        body, name="ret_bwd", grid=(nc,),
        in_specs=[pl.BlockSpec(memory_space=pltpu.SMEM),
                  pl.BlockSpec((TM, 2 * D_MODEL), rev),
                  pl.BlockSpec((TM, RET_W), rev),
                  pl.BlockSpec((TM, RET_W), rev),
                  pl.BlockSpec((TM, RET_W), rev),
                  pl.BlockSpec((TM, RET_W), rev),
                  pl.BlockSpec((1, RET_HEADS, RET_QK, RET_V), lambda n: (nc - 1 - n, 0, 0, 0)),
                  pl.BlockSpec((1, RET_W), lambda n: (0, 0)),
                  pl.BlockSpec((TM, half), rev),
                  pl.BlockSpec((TM, half), rev)],
        out_specs=[pl.BlockSpec((TM, W_R), rev), pl.BlockSpec((1, RET_W), lambda n: (0, 0))],
        out_shape=[jax.ShapeDtypeStruct((tp, W_R), BF16), jax.ShapeDtypeStruct((1, RET_W), F32)],
        scratch_shapes=[pltpu.VMEM((RET_HEADS, RET_QK, RET_V), F32), pltpu.VMEM((RET_HEADS, TM, TM), F32)],
        compiler_params=_cparams(1),
    )(lgam, rqk, rv, rg, o_ret, da, states, gain, cos, sin)


GLA_LEVELS = tuple(GC >> (s + 1) for s in range(int(math.log2(GC // GLA_SUB))))
NLEV = len(GLA_LEVELS)
NBLK = 2 * NLEV + 3
P_ROWS = NBLK * GC + 8


def _gla_p_matrix():
    c = GC
    i = np.arange(c)[:, None]
    r = np.arange(c)[None, :]
    blocks = []
    for m in GLA_LEVELS:
        second = (i & m) != 0
        ref = (i // (2 * m)) * 2 * m + m - 1
        blocks.append(second & (r > ref) & (r <= i))
    for m in GLA_LEVELS:
        second = (i & m) != 0
        ref = (i // (2 * m)) * 2 * m + m - 1
        blocks.append((~second) & (r > i) & (r <= ref))
    blocks.append((r >= (i // GLA_SUB) * GLA_SUB) & (r <= i))
    blocks.append(r <= i)
    blocks.append(r > i)
    blocks.append(np.ones((8, c), bool))
    return np.concatenate([b.astype(np.float32) for b in blocks], axis=0)


def _gla_masks():
    ii = lax.broadcasted_iota(jnp.int32, (GC, GC), 0)
    jj = lax.broadcasted_iota(jnp.int32, (GC, GC), 1)
    masks = []
    for m in GLA_LEVELS:
        sh = int(math.log2(2 * m))
        masks.append(((ii >> sh) == (jj >> sh)) & ((ii & m) != 0) & ((jj & m) == 0))
    sh = int(math.log2(GLA_SUB))
    md = ((ii >> sh) == (jj >> sh)) & (jj <= ii)
    row = lax.broadcasted_iota(jnp.int32, (GC, 1), 0)
    second = [(row & m) != 0 for m in GLA_LEVELS]
    return masks, md, second


def _gla_log_decay(glr_ref, wg_ref, bg_ref):
    z = _dot(glr_ref[...].astype(BF16), wg_ref[...], NN) + bg_ref[...]
    la = (jnp.minimum(z, 0.0) - jnp.log1p(jnp.exp(-jnp.abs(z)))) * (1.0 / GATE_TAU)
    return z, la


def _gla_factors(e, h, second):
    cs = slice(h * GLA_K, (h + 1) * GLA_K)
    blk = lambda b: e[b * GC:(b + 1) * GC, cs]
    fq = [jnp.where(second[l], jnp.exp(blk(l)), 0.0) for l in range(NLEV)]
    fk = [jnp.where(second[l], 0.0, jnp.exp(blk(NLEV + l))) for l in range(NLEV)]
    ed = jnp.exp(blk(2 * NLEV))
    edi = jnp.exp(-blk(2 * NLEV))
    eb = jnp.exp(blk(2 * NLEV + 1))
    ee = jnp.exp(blk(2 * NLEV + 2))
    ebl = jnp.exp(e[NBLK * GC:NBLK * GC + 1, cs])
    return fq, fk, ed, edi, eb, ee, ebl


def _gla_scores(q, k, fq, fk, ed, edi, masks, md):
    qt = [(q * f).astype(BF16) for f in fq]
    kt = [(k * f).astype(BF16) for f in fk]
    qd = (q * ed).astype(BF16)
    kd = (k * edi).astype(BF16)
    a = jnp.where(md, _dot(qd, kd, NT), 0.0)
    for l in range(NLEV):
        a = a + jnp.where(masks[l], _dot(qt[l], kt[l], NT), 0.0)
    return a, qt, kt, qd, kd


def _gla_fwd_call(gqk, gv, glr, gg, wg, bg, gain, pmat):
    tp = gqk.shape[0]
    nc = tp // GC

    def body(qk_ref, v_ref, glr_ref, gg_ref, wg_ref, bg_ref, g_ref, p_ref, o_ref, a_ref, st_ref, s_scr):
        @pl.when(pl.program_id(0) == 0)
        def _():
            s_scr[...] = jnp.zeros_like(s_scr)

        _, la = _gla_log_decay(glr_ref, wg_ref, bg_ref)
        e = _exact_pm(p_ref[...], la)
        masks, md, second = _gla_masks()
        for h in range(GLA_HEADS):
            q = qk_ref[:, h * GLA_K:(h + 1) * GLA_K]
            k = qk_ref[:, GLA_KW + h * GLA_K:GLA_KW + (h + 1) * GLA_K]
            vs = slice(h * GLA_V, (h + 1) * GLA_V)
            v = v_ref[:, vs]
            fq, fk, ed, edi, eb, ee, ebl = _gla_factors(e, h, second)
            a, *_ = _gla_scores(q, k, fq, fk, ed, edi, masks, md)
            sb = s_scr[h].astype(BF16)
            st_ref[0, h] = sb
            o = _dot(a.astype(BF16), v, NN) + _dot((q * eb).astype(BF16), sb, NT)
            s_scr[h] = s_scr[h] * ebl + _dot(v, (k * ee).astype(BF16), TN)
            o_ref[:, vs] = o
            xh = o * lax.rsqrt(jnp.mean(o * o, axis=-1, keepdims=True) + EPS)
            g = gg_ref[:, vs]
            a_ref[:, vs] = (xh * g_ref[:, vs] * (g * _sigmoid(g))).astype(BF16)

    return pl.pallas_call(
        body, name="gla_fwd", grid=(nc,),
        in_specs=[pl.BlockSpec((GC, 2 * GLA_KW), lambda n: (n, 0)),
                  pl.BlockSpec((GC, GLA_W), lambda n: (n, 0)),
                  pl.BlockSpec((GC, 128), lambda n: (n, 0)),
                  pl.BlockSpec((GC, GLA_W), lambda n: (n, 0)),
                  pl.BlockSpec((128, GLA_KW), lambda n: (0, 0)),
                  pl.BlockSpec((1, GLA_KW), lambda n: (0, 0)),
                  pl.BlockSpec((1, GLA_W), lambda n: (0, 0)),
                  pl.BlockSpec((P_ROWS, GC), lambda n: (0, 0))],
        out_specs=[pl.BlockSpec((GC, GLA_W), lambda n: (n, 0)),
                   pl.BlockSpec((GC, GLA_W), lambda n: (n, 0)),
                   pl.BlockSpec((1, GLA_HEADS, GLA_V, GLA_K), lambda n: (n, 0, 0, 0))],
        out_shape=[jax.ShapeDtypeStruct((tp, GLA_W), F32), jax.ShapeDtypeStruct((tp, GLA_W), BF16),
                   jax.ShapeDtypeStruct((nc, GLA_HEADS, GLA_V, GLA_K), BF16)],
        scratch_shapes=[pltpu.VMEM((GLA_HEADS, GLA_V, GLA_K), F32)],
        compiler_params=_cparams(1),
    )(gqk, gv, glr, gg, wg, bg, gain, pmat)


def _gla_bwd_call(gqk, gv, glr, gg, o_gla, da, states, wg, bg, gain, pmat, pmat_t, exchange=()):
    tp = gqk.shape[0]
    nc = tp // GC
    o_gv, o_gg, o_lr = 2 * GLA_KW, 2 * GLA_KW + GLA_W, 2 * GLA_KW + 2 * GLA_W
    n_xc = len(exchange)

    def body(qk_ref, v_ref, glr_ref, gg_ref, o_ref, da_ref, st_ref, wg_ref, bg_ref, g_ref, p_ref, pt_ref, *rest):
        xc_src = rest[:n_xc]
        dp_ref, dwg_ref, dbg_ref, dg_ref = rest[n_xc:n_xc + 4]
        xc_dst = rest[n_xc + 4:2 * n_xc + 4]
        ds_scr, de_scr = rest[2 * n_xc + 4:2 * n_xc + 6]
        n = pl.program_id(0)
        if n_xc:
            copies = _exchange_copies(xc_src, xc_dst, rest[-2], rest[-1])

            @pl.when(n == 0)
            def _():
                for cp in copies:
                    cp.start()

            @pl.when(n == nc - 1)
            def _():
                for cp in copies:
                    cp.wait()

        @pl.when(n == 0)
        def _():
            ds_scr[...] = jnp.zeros_like(ds_scr)
            dwg_ref[...] = jnp.zeros_like(dwg_ref)
            dbg_ref[...] = jnp.zeros_like(dbg_ref)
            dg_ref[...] = jnp.zeros_like(dg_ref)
            de_scr[...] = jnp.zeros_like(de_scr)

        z, la = _gla_log_decay(glr_ref, wg_ref, bg_ref)
        e = _exact_pm(p_ref[...], la)
        masks, md, second = _gla_masks()
        for h in range(GLA_HEADS):
            cs = slice(h * GLA_K, (h + 1) * GLA_K)
            vs = slice(h * GLA_V, (h + 1) * GLA_V)
            o = o_ref[:, vs]
            rstd = lax.rsqrt(jnp.mean(o * o, axis=-1, keepdims=True) + EPS)
            xh = o * rstd
            gain_h = g_ref[:, vs]
            g = gg_ref[:, vs]
            sg = _sigmoid(g)
            dah = da_ref[:, vs]
            dp_ref[:, o_gg + h * GLA_V:o_gg + (h + 1) * GLA_V] = (
                dah * (xh * gain_h) * (sg * (1.0 + g * (1.0 - sg)))).astype(BF16)
            dn = dah * (g * sg)
            dg_ref[:, vs] += jnp.sum(dn * xh, axis=0, keepdims=True)
            dxh = dn * gain_h
            do = rstd * (dxh - xh * jnp.mean(dxh * xh, axis=-1, keepdims=True))
            dob = do.astype(BF16)
            q = qk_ref[:, cs]
            k = qk_ref[:, GLA_KW + h * GLA_K:GLA_KW + (h + 1) * GLA_K]
            v = v_ref[:, vs]
            fq, fk, ed, edi, eb, ee, ebl = _gla_factors(e, h, second)
            a, qt, kt, qd, kd = _gla_scores(q, k, fq, fk, ed, edi, masks, md)
            sp = st_ref[0, h]
            ds = ds_scr[h]
            dsb = ds.astype(BF16)
            q_in = q * eb
            k_end = k * ee
            da_s = _dot(dob, v, NT)
            dv = _dot(a.astype(BF16), dob, TN) + _dot(k_end.astype(BF16), dsb, NT)
            dq_in = _dot(dob, sp, NN)
            dk_end = _dot(v, dsb, NN)
            dbl = jnp.sum(sp.astype(F32) * ds, axis=0, keepdims=True) * ebl
            ds_scr[h] = ds * ebl + _dot(dob, q_in.astype(BF16), TN)
            dq = dq_in * eb
            dk = dk_end * ee
            de_scr[(2 * NLEV + 1) * GC:(2 * NLEV + 2) * GC, cs] = dq_in * q_in
            de_scr[(2 * NLEV + 2) * GC:(2 * NLEV + 3) * GC, cs] = dk_end * k_end
            de_scr[NBLK * GC:NBLK * GC + 1, cs] = dbl
            for l in range(NLEV):
                dal = jnp.where(masks[l], da_s, 0.0).astype(BF16)
                dqt = _dot(dal, kt[l], NN)
                dkt = _dot(dal, qt[l], TN)
                dq = dq + dqt * fq[l]
                dk = dk + dkt * fk[l]
                de_scr[l * GC:(l + 1) * GC, cs] = dqt * (q * fq[l])
                de_scr[(NLEV + l) * GC:(NLEV + l + 1) * GC, cs] = dkt * (k * fk[l])
            dad = jnp.where(md, da_s, 0.0).astype(BF16)
            dqd = _dot(dad, kd, NN)
            dkd = _dot(dad, qd, TN)
            dq = dq + dqd * ed
            dk = dk + dkd * edi
            de_scr[2 * NLEV * GC:(2 * NLEV + 1) * GC, cs] = dqd * (q * ed) - dkd * (k * edi)
            dp_ref[:, cs] = (dq * (GLA_K ** -0.5)).astype(BF16)
            dp_ref[:, GLA_KW + h * GLA_K:GLA_KW + (h + 1) * GLA_K] = dk.astype(BF16)
            dp_ref[:, o_gv + h * GLA_V:o_gv + (h + 1) * GLA_V] = dv.astype(BF16)
        dla = _exact_pm(pt_ref[...], de_scr[...])
        row = (nc - 1 - n) * GC + lax.broadcasted_iota(jnp.int32, (GC, 1), 0)
        dz = jnp.where(row >= PADF, dla * (1.0 / GATE_TAU) * _sigmoid(-z), 0.0)
        dzb = dz.astype(BF16)
        dp_ref[:, o_lr:] = _dot(dzb, wg_ref[...], NT).astype(BF16)
        dwg_ref[...] += _dot(glr_ref[...].astype(BF16), dzb, TN)
        dbg_ref[...] += jnp.sum(dz, axis=0, keepdims=True)

    rev = lambda n: (nc - 1 - n, 0)
    const = lambda n: (0, 0)
    xc_shapes, xc_sems = _exchange_shapes(exchange) if n_xc else ([], [])
    return pl.pallas_call(
        body, name="gla_bwd", grid=(nc,),
        in_specs=[pl.BlockSpec((GC, 2 * GLA_KW), rev),
                  pl.BlockSpec((GC, GLA_W), rev),
                  pl.BlockSpec((GC, 128), rev),
                  pl.BlockSpec((GC, GLA_W), rev),
                  pl.BlockSpec((GC, GLA_W), rev),
                  pl.BlockSpec((GC, GLA_W), rev),
                  pl.BlockSpec((1, GLA_HEADS, GLA_V, GLA_K), lambda n: (nc - 1 - n, 0, 0, 0)),
                  pl.BlockSpec((128, GLA_KW), const),
                  pl.BlockSpec((1, GLA_KW), const),
                  pl.BlockSpec((1, GLA_W), const),
                  pl.BlockSpec((P_ROWS, GC), const),
                  pl.BlockSpec((GC, P_ROWS), const)] + [ANY] * n_xc,
        out_specs=[pl.BlockSpec((GC, W_GP), rev), pl.BlockSpec((128, GLA_KW), const),
                   pl.BlockSpec((1, GLA_KW), const), pl.BlockSpec((1, GLA_W), const)] + [ANY] * n_xc,
        out_shape=[jax.ShapeDtypeStruct((tp, W_GP), BF16), jax.ShapeDtypeStruct((128, GLA_KW), F32),
                   jax.ShapeDtypeStruct((1, GLA_KW), F32), jax.ShapeDtypeStruct((1, GLA_W), F32)] + xc_shapes,
        scratch_shapes=[pltpu.VMEM((GLA_HEADS, GLA_V, GLA_K), F32), pltpu.VMEM((P_ROWS, GLA_KW), F32)] + xc_sems,
        compiler_params=_cparams(1),
    )(gqk, gv, glr, gg, o_gla, da, states, wg, bg, gain, pmat, pmat_t, *exchange)


def _mid_call(a_ret, a_gla, mg, h0, tgt, wbr, wbg, wout, gf):
    tp = h0.shape[0]
    nt = tp // TM

    def body(ar_ref, ag_ref, mg_ref, h_ref, t_ref, wbr_ref, wbg_ref, wo_ref, gf_ref,
             dh1_ref, dar_ref, dag_ref, dm_ref, mb_ref, dh1b_ref, dprb_ref, dpgb_ref, loss_ref, dgf_ref):
        i = pl.program_id(0)

        @pl.when(i == 0)
        def _():
            loss_ref[...] = jnp.zeros_like(loss_ref)
            dgf_ref[...] = jnp.zeros_like(dgf_ref)

        ar, ag = ar_ref[...], ag_ref[...]
        pr = _dot(ar, wbr_ref[...], NN)
        pg = _dot(ag, wbg_ref[...], NN)
        sr = _sigmoid(mg_ref[:, :D_MODEL])
        sg = _sigmoid(mg_ref[:, D_MODEL:])
        merged = (sr * pr + sg * pg).astype(BF16)
        mb_ref[...] = merged
        h1 = h_ref[...] + _dot(merged, wo_ref[...], NN)
        r1 = lax.rsqrt(jnp.mean(h1 * h1, axis=-1, keepdims=True) + EPS)
        xh = h1 * r1
        gfv = gf_ref[...]
        live = jnp.where(i > 0, 1.0, 0.0).astype(F32)
        err = (xh * gfv - t_ref[...]) * live
        loss_ref[...] += jnp.full(loss_ref.shape, 0.5 / D_MODEL, F32) * jnp.sum(err * err)
        dy = err * (1.0 / D_MODEL)
        dgf_ref[...] += jnp.sum(dy * xh, axis=0, keepdims=True)
        dxh = dy * gfv
        dh1 = r1 * (dxh - xh * jnp.mean(dxh * xh, axis=-1, keepdims=True))
        dh1_ref[...] = dh1
        dh1b = dh1.astype(BF16)
        dh1b_ref[...] = dh1b
        dmerged = _dot(dh1b, wo_ref[...], NT)
        dm_ref[:, :D_MODEL] = (dmerged * pr * sr * (1.0 - sr)).astype(BF16)
        dm_ref[:, D_MODEL:] = (dmerged * pg * sg * (1.0 - sg)).astype(BF16)
        dpr = (dmerged * sr).astype(BF16)
        dpg = (dmerged * sg).astype(BF16)
        dprb_ref[...] = dpr
        dpgb_ref[...] = dpg
        dar_ref[...] = _dot(dpr, wbr_ref[...], NT)
        dag_ref[...] = _dot(dpg, wbg_ref[...], NT)

    tile = lambda w: pl.BlockSpec((TM, w), lambda i: (i, 0))
    const = lambda r, w: pl.BlockSpec((r, w), lambda i: (0, 0))
    return pl.pallas_call(
        body, name="merge_out_loss", grid=(nt,),
        in_specs=[tile(RET_W), tile(GLA_W), tile(W_M), tile(D_MODEL),
                  pl.BlockSpec((TM, D_MODEL), lambda i: (jnp.maximum(i - 1, 0), 0)),
                  const(RET_W, D_MODEL), const(GLA_W, D_MODEL), const(D_MODEL, D_MODEL), const(1, D_MODEL)],
        out_specs=[tile(D_MODEL), tile(RET_W), tile(GLA_W), tile(W_M), tile(D_MODEL), tile(D_MODEL), tile(D_MODEL),
                   tile(D_MODEL), const(1, 128), const(1, D_MODEL)],
        out_shape=[jax.ShapeDtypeStruct((tp, D_MODEL), F32), jax.ShapeDtypeStruct((tp, RET_W), F32),
                   jax.ShapeDtypeStruct((tp, GLA_W), F32), jax.ShapeDtypeStruct((tp, W_M), BF16),
                   jax.ShapeDtypeStruct((tp, D_MODEL), BF16), jax.ShapeDtypeStruct((tp, D_MODEL), BF16),
                   jax.ShapeDtypeStruct((tp, D_MODEL), BF16), jax.ShapeDtypeStruct((tp, D_MODEL), BF16),
                   jax.ShapeDtypeStruct((1, 128), F32), jax.ShapeDtypeStruct((1, D_MODEL), F32)],
        compiler_params=_cparams(1),
    )(a_ret, a_gla, mg, h0, tgt, wbr, wbg, wout, gf)


def _device_step(x2d, tgt2d, meta, norm_gain, w_in_bf, w_gate_up, b_gate, ret_gain, gla_gain, wbr, wbg, wout,
                 final_gain, ck):
    seq = x2d.shape[0]
    tp = T0 + seq
    head = jnp.concatenate([jnp.zeros((PADF, D_MODEL), F32), meta], axis=0)
    w_r = w_in_bf
    w_g = jnp.pad(w_in_bf[:, W_R:W_R + W_G], ((0, 0), (0, W_GP - W_G)))
    w_m = w_in_bf[:, W_R + W_G:]
    wg_pad = jnp.pad(w_gate_up, ((0, 128 - GATE_RANK), (0, 0))).astype(BF16)

    pos = jnp.arange(tp, dtype=F32) - PADF
    half = RET_QK // 2
    inv = ROPE_BASE ** (-jnp.arange(half, dtype=F32) / half)
    ang = pos[:, None] * inv[None, :]
    cos, sin = jnp.cos(ang), jnp.sin(ang)
    lgam = jnp.log1p(-(2.0 ** (-5.0 - jnp.arange(RET_HEADS, dtype=F32))))
    pm_np = _gla_p_matrix()
    pmat = jnp.asarray(pm_np, BF16)
    pmat_t = jnp.asarray(pm_np.T.copy(), BF16)

    h0, u = _rms_call(x2d, head, norm_gain)
    tab = pl.BlockSpec((TB, half), lambda j, i: (i, 0))
    rqk = _mm_nn("proj_rqk", u, w_r, BF16, D_MODEL, 0, 2 * D_MODEL, _rope_epilogue, (cos, sin), (tab, tab))
    rv = _mm_nn("proj_rv", u, w_r, BF16, D_MODEL, 2 * D_MODEL, RET_W)
    rg = _mm_nn("proj_rg", u, w_r, F32, D_MODEL, 4 * D_MODEL, RET_W)
    gqk = _mm_nn("proj_gqk", u, w_g, F32, 2 * GLA_KW, 0, 2 * GLA_KW, _gqk_epilogue)
    gv = _mm_nn("proj_gv", u, w_g, BF16, GLA_W, 2 * GLA_KW, GLA_W)
    gg = _mm_nn("proj_gg", u, w_g, F32, GLA_W, 2 * GLA_KW + GLA_W, GLA_W)
    glr = _mm_nn("proj_glr", u, w_g, F32, 128, 2 * GLA_KW + 2 * GLA_W, 128)
    mg = _mm_nn("proj_mg", u, w_m, F32, D_MODEL, 0, W_M)

    o_ret, a_ret, st_ret = _ret_fwd_call(rqk, rv, rg, ret_gain, lgam)
    o_gla, a_gla, st_gla = _gla_fwd_call(gqk, gv, glr, gg, wg_pad, b_gate, gla_gain, pmat)

    gf = final_gain.reshape(1, D_MODEL)
    (dh1, da_ret, da_gla, dm, merged_b, dh1_b, dpr_b, dpg_b, loss, dgf) = _mid_call(
        a_ret, a_gla, mg, h0, tgt2d, wbr, wbg, wout, gf)

    names_b = ("w_branch_ret", "w_branch_gla", "w_out")
    g2_b = [_mm_tn("dw_br", a_ret, dpr_b, D_MODEL).reshape(4, 2, RET_W // 8, D_MODEL).transpose(1, 0, 2, 3),
            _mm_tn("dw_bg", a_gla, dpg_b, D_MODEL).reshape(4, 2, GLA_W // 8, D_MODEL).transpose(1, 0, 2, 3),
            _mm_tn("dw_out", merged_b, dh1_b, D_MODEL).reshape(4, 2, D_MODEL // 8, D_MODEL).transpose(1, 0, 2, 3)]
    sib_b = _swap_halves_call("swap_halves_branch", g2_b)
    sum_b = [_add_half_call("add_half_" + nm, g, b, ck) for nm, g, b in zip(names_b, g2_b, sib_b)]
    d_g, dwg, dbg, dgla_gain, *chips_b = _gla_bwd_call(gqk, gv, glr, gg, o_gla, da_gla, st_gla, wg_pad, b_gate,
                                                       gla_gain, pmat, pmat_t, exchange=sum_b)
    mine = [_add_chips_call("add_chips_" + nm, g, b, p, ck) for nm, g, b, p in zip(names_b, g2_b, sib_b, chips_b)]

    d_r, dret_gain = _ret_bwd_call(rqk, rv, rg, o_ret, da_ret, st_ret, ret_gain, lgam, cos, sin)

    dw_in = jnp.concatenate([_mm_tn("dw_r", u, d_r, D_MODEL), _mm_tn("dw_g", u, d_g, 640)[:, :W_G],
                             _mm_tn("dw_m", u, dm, D_MODEL)], axis=1)
    g2_in = dw_in.reshape(2, D_MODEL // 2, 4, IN_COLS // 4).transpose(0, 2, 1, 3)
    (sib_in,) = _swap_halves_call("swap_halves_in", [g2_in])
    sum_in = _add_half_call("add_half_w_in", g2_in, sib_in, ck)

    du = _mm_nt_acc("du_m", dm, w_m, W_M)[0]
    du = _mm_nt_acc("du_g", d_g, w_g, W_GP, acc_in=du)[0]
    tile = pl.BlockSpec((TB, D_MODEL), lambda i, kk: (i, 0))
    row = pl.BlockSpec((1, D_MODEL), lambda i, kk: (0, 0))
    dh0, dnorm_gain, chips_in = _mm_nt_acc(
        "du_r", d_r, w_r, 2 * D_MODEL, acc_in=du, epilogue=_rms_bwd_epilogue, extras=(h0, norm_gain, dh1),
        extra_specs=(tile, row, tile),
        extra_out_shapes=(jax.ShapeDtypeStruct((tp, D_MODEL), F32), jax.ShapeDtypeStruct((1, D_MODEL), F32)),
        extra_out_specs=(tile, row), exchange=[sum_in])
    mine = [_add_chips_call("add_chips_w_in", g2_in, sib_in, chips_in, ck)] + mine
    full = _join_halves_call("join_halves", mine)

    return dict(loss=loss[0, 0], dx=dh0[T0:], dmeta=dh0[PADF:T0], norm_gain=dnorm_gain, w_gate_up=dwg[:GATE_RANK], b_gate=dbg,
                ret_norm_gain=dret_gain, gla_norm_gain=dgla_gain, final_norm_gain=dgf.reshape(D_MODEL),
                w_in=full[0], w_branch_ret=full[1], w_branch_gla=full[2], w_out=full[3])


MESH = pl.DeviceIdType.MESH
ANY = pl.BlockSpec(memory_space=pl.ANY)


def _place():
    return lax.axis_index("x"), lax.axis_index("y"), lax.axis_index("c")


def _gather8_call(name, parts):
    n = len(parts)

    def body(*refs):
        x_refs, out_refs = refs[:n], refs[n:2 * n]
        send_sems, recv_sems, local_sems = refs[2 * n:]
        x, y, c = _place()
        me, sibling = (x, y, c), (x, y, 1 - c)
        chips = [(1 - x, y), (x, 1 - y), (1 - x, 1 - y)]

        def slot(t, px, py, pc):
            return out_refs[t].at[4 * px + 2 * py + pc]

        def copy(t, k, block, to, src=None):
            return pltpu.make_async_remote_copy(
                src_ref=slot(t, *block) if src is None else src, dst_ref=slot(t, *block),
                send_sem=send_sems.at[7 * t + k], recv_sem=recv_sems.at[7 * t + k], device_id=to, device_id_type=MESH)

        mine = [pltpu.make_async_copy(x_refs[t], slot(t, *me), local_sems.at[t]) for t in range(n)]
        for cp in mine:
            cp.start()
        first = []
        for t in range(n):
            first.append(copy(t, 0, me, sibling, src=x_refs[t]))
            first += [copy(t, 1 + j, me, (*chip, c), src=x_refs[t]) for j, chip in enumerate(chips)]
        for cp in first:
            cp.start()
        passed = []
        for j, chip in enumerate(chips):
            for t in range(n):
                copy(t, 1 + j, (*chip, c), me).wait_recv()
                fwd = copy(t, 4 + j, (*chip, c), sibling)
                fwd.start()
                passed.append(fwd)
        for t in range(n):
            copy(t, 0, sibling, me).wait_recv()
            for j, chip in enumerate(chips):
                copy(t, 4 + j, (*chip, 1 - c), me).wait_recv()
        for cp in first + passed:
            cp.wait_send()
        for cp in mine:
            cp.wait()

    return pl.pallas_call(
        body, name=name,
        out_shape=[jax.ShapeDtypeStruct((8,) + p.shape, p.dtype) for p in parts],
        in_specs=[ANY] * n, out_specs=[ANY] * n,
        scratch_shapes=[pltpu.SemaphoreType.DMA((7 * n,)), pltpu.SemaphoreType.DMA((7 * n,)),
                        pltpu.SemaphoreType.DMA((n,))],
    )(*parts)


def _swap_halves_call(name, gs):
    n = len(gs)

    def body(*refs):
        g_refs, b_refs = refs[:n], refs[n:2 * n]
        send_sems, recv_sems = refs[2 * n:]
        x, y, c = _place()
        copies = [pltpu.make_async_remote_copy(
            src_ref=g_refs[t].at[1 - c], dst_ref=b_refs[t], send_sem=send_sems.at[t], recv_sem=recv_sems.at[t],
            device_id=(x, y, 1 - c), device_id_type=MESH) for t in range(n)]
        for cp in copies:
            cp.start()
        for cp in copies:
            cp.wait()

    return pl.pallas_call(
        body, name=name,
        out_shape=[jax.ShapeDtypeStruct(g.shape[1:], g.dtype) for g in gs],
        in_specs=[ANY] * n, out_specs=[ANY] * n,
        scratch_shapes=[pltpu.SemaphoreType.DMA((n,)), pltpu.SemaphoreType.DMA((n,))],
    )(*gs)


def _join_halves_call(name, ts):
    n = len(ts)

    def body(*refs):
        o_refs = refs[n:2 * n]
        send_sems, recv_sems = refs[2 * n:]
        x, y, c = _place()
        copies = [pltpu.make_async_remote_copy(
            src_ref=o_refs[t].at[c], dst_ref=o_refs[t].at[c], send_sem=send_sems.at[t], recv_sem=recv_sems.at[t],
            device_id=(x, y, 1 - c), device_id_type=MESH) for t in range(n)]
        for cp in copies:
            cp.start()
        for t in range(n):
            copies[t].wait_send()
            pltpu.make_async_remote_copy(
                src_ref=o_refs[t].at[c], dst_ref=o_refs[t].at[1 - c], send_sem=send_sems.at[t],
                recv_sem=recv_sems.at[t], device_id=(x, y, 1 - c), device_id_type=MESH).wait_recv()

    return pl.pallas_call(
        body, name=name,
        out_shape=[jax.ShapeDtypeStruct(t.shape, t.dtype) for t in ts],
        in_specs=[ANY] * n, out_specs=[ANY] * n, input_output_aliases={t: t for t in range(n)},
        scratch_shapes=[pltpu.SemaphoreType.DMA((n,)), pltpu.SemaphoreType.DMA((n,))],
    )(*ts)


def _row_block(rows, cols, budget):
    best = 8
    for rb in range(8, rows + 1, 8):
        if rows % rb == 0 and rb * cols * 4 <= budget:
            best = rb
    return best


def _add_half_call(name, g, b, ck):
    _, _, r, cc = g.shape
    rb = _row_block(r, cc, 2 * 1024 * 1024)

    def body(ck_ref, g_ref, b_ref, o_ref):
        o_ref[...] = (g_ref[...] + b_ref[...]).astype(BF16)

    return pl.pallas_call(
        body, name=name,
        grid_spec=pltpu.PrefetchScalarGridSpec(
            num_scalar_prefetch=1, grid=(4, r // rb),
            in_specs=[pl.BlockSpec((None, None, rb, cc), lambda k, i, ck_ref: (ck_ref[0], k, i, 0)),
                      pl.BlockSpec((None, rb, cc), lambda k, i, ck_ref: (k, i, 0))],
            out_specs=pl.BlockSpec((None, rb, cc), lambda k, i, ck_ref: (k, i, 0))),
        out_shape=jax.ShapeDtypeStruct(b.shape, BF16),
        compiler_params=_cparams(2),
    )(ck, g, b)


def _add_chips_call(name, g, b, p, ck):
    _, _, r, cc = g.shape
    rb = _row_block(r, cc, 2 * 1024 * 1024)

    def body(ck_ref, g_ref, b_ref, p0_ref, p1_ref, p2_ref, o_ref):
        own = g_ref[...] + b_ref[...]
        o_ref[...] = ((own + p0_ref[...].astype(F32)) + p1_ref[...].astype(F32)) + p2_ref[...].astype(F32)

    def peer(j):
        return pl.BlockSpec((None, rb, cc), lambda i, ck_ref: (j, i, 0))

    return pl.pallas_call(
        body, name=name,
        grid_spec=pltpu.PrefetchScalarGridSpec(
            num_scalar_prefetch=1, grid=(r // rb,),
            in_specs=[pl.BlockSpec((None, None, rb, cc), lambda i, ck_ref: (ck_ref[0], ck_ref[1], i, 0)),
                      pl.BlockSpec((None, rb, cc), lambda i, ck_ref: (ck_ref[1], i, 0)),
                      peer(0), peer(1), peer(2)],
            out_specs=pl.BlockSpec((None, rb, cc), lambda i, ck_ref: (ck_ref[0], i, 0))),
        out_shape=jax.ShapeDtypeStruct((2, r, cc), F32),
        compiler_params=_cparams(1),
    )(ck, g, b, p, p, p)


def _sum8_call(name, g):
    def body(g_ref, o_ref):
        acc = g_ref[0]
        for d in range(1, 8):
            acc = acc + g_ref[d]
        o_ref[...] = acc

    return pl.pallas_call(body, name=name, out_shape=jax.ShapeDtypeStruct(g.shape[1:], F32))(g)


def _adamw_call(name, w, g, m, v):
    r, cc = w.shape
    if r % 8 == 0 or r * cc * 4 <= 1024 * 1024:
        rb = _row_block(r, cc, 1024 * 1024) if r % 8 == 0 else r
        grid, spec = (r // rb,), pl.BlockSpec((rb, cc), lambda i: (i, 0))
    else:
        grid, spec = (cc // 128,), pl.BlockSpec((r, 128), lambda i: (0, i))

    def body(w_ref, g_ref, m_ref, v_ref, d_ref, m2_ref, v2_ref):
        gv = g_ref[...]
        m2 = ADAM_B1 * m_ref[...] + (1.0 - ADAM_B1) * gv
        v2 = ADAM_B2 * v_ref[...] + (1.0 - ADAM_B2) * (gv * gv)
        m_hat = m2 / (1.0 - ADAM_B1 ** ADAM_STEP)
        v_hat = v2 / (1.0 - ADAM_B2 ** ADAM_STEP)
        d_ref[...] = -ADAM_LR * (m_hat / (jnp.sqrt(v_hat) + ADAM_EPS) + ADAM_WD * w_ref[...])
        m2_ref[...] = m2
        v2_ref[...] = v2

    return pl.pallas_call(
        body, name=name, grid=grid, in_specs=[spec] * 4, out_specs=[spec] * 3,
        out_shape=[jax.ShapeDtypeStruct((r, cc), F32)] * 3, compiler_params=_cparams(1),
    )(w, g, m, v)


SMALL = (("norm_gain", D_MODEL), ("b_gate", GLA_KW), ("ret_norm_gain", RET_W), ("gla_norm_gain", GLA_W),
         ("final_norm_gain", D_MODEL), ("w_gate_up", GATE_RANK * GLA_KW), ("meta_tokens", N_META * D_MODEL))


def _pack_rows(vecs, rows):
    flat = jnp.concatenate([v.reshape(-1) for v in vecs])
    return jnp.pad(flat, (0, rows * 128 - flat.shape[0])).reshape(rows, 128)


def kernel(x, meta_tokens, norm_gain, w_in, w_gate_up, b_gate, ret_norm_gain, gla_norm_gain, w_branch_ret, w_branch_gla, w_out, final_norm_gain, loss_target, m_meta_tokens, m_norm_gain, m_w_in, m_w_gate_up, m_b_gate, m_ret_norm_gain, m_gla_norm_gain, m_w_branch_ret, m_w_branch_gla, m_w_out, m_final_norm_gain, v_meta_tokens, v_norm_gain, v_w_in, v_w_gate_up, v_b_gate, v_ret_norm_gain, v_gla_norm_gain, v_w_branch_ret, v_w_branch_gla, v_w_out, v_final_norm_gain):
    xi, yi, ci = _place()
    kme = 2 * xi + yi
    ck = jnp.stack([ci, kme]).astype(jnp.int32)
    sw_in = w_in.shape[2]

    def my_half(a, dtype):
        r, cc = a.shape
        return lax.dynamic_index_in_dim(a.reshape(2, r // 2, cc), ci, 0, keepdims=False).astype(dtype)

    g_in, g_br, g_bg, g_out, g_meta, g_wg = _gather8_call(
        "gather_weights",
        [my_half(w_in[0], BF16), my_half(w_branch_ret[0], BF16), my_half(w_branch_gla[0], BF16),
         my_half(w_out[0], BF16), my_half(meta_tokens, F32), my_half(w_gate_up[0], F32)])
    w_in_bf = g_in.reshape(4, 2, D_MODEL // 2, sw_in).transpose(1, 2, 0, 3).reshape(D_MODEL, 4 * sw_in)
    wbr = g_br.reshape(RET_W, D_MODEL)
    wbg = g_bg.reshape(GLA_W, D_MODEL)
    wout = g_out.reshape(D_MODEL, D_MODEL)
    meta = g_meta.reshape(4, 2, N_META // 2, D_MODEL // 4).transpose(1, 2, 0, 3).reshape(N_META, D_MODEL)
    wg_full = g_wg.reshape(4, 2, GATE_RANK // 2, GLA_KW // 4).transpose(1, 2, 0, 3).reshape(GATE_RANK, GLA_KW)

    loc = _device_step(x[0], loss_target[0], meta, norm_gain, w_in_bf, wg_full, b_gate, ret_norm_gain, gla_norm_gain,
                       wbr, wbg, wout, final_norm_gain, ck)
    loss = lax.psum(loc["loss"], ("x", "y", "c"))
    names = ("w_in", "w_branch_ret", "w_branch_gla", "w_out")
    full = [loc[nm] for nm in names]
    big_w = dict(w_in=w_in[0], w_branch_ret=w_branch_ret[0], w_branch_gla=w_branch_gla[0], w_out=w_out[0])
    big_m = dict(w_in=m_w_in[0], w_branch_ret=m_w_branch_ret[0], w_branch_gla=m_w_branch_gla[0], w_out=m_w_out[0])
    big_v = dict(w_in=v_w_in[0], w_branch_ret=v_w_branch_ret[0], w_branch_gla=v_w_branch_gla[0], w_out=v_w_out[0])
    grads, deltas, new_m, new_v = {}, {}, {}, {}
    for nm, f in zip(names, full):
        shape = big_w[nm].shape
        g = f.reshape(shape)
        if nm == "w_in":
            d, m2, v2 = (a.T for a in _adamw_call("adamw_" + nm, big_w[nm].T, g.T, big_m[nm].T, big_v[nm].T))
        else:
            d, m2, v2 = _adamw_call("adamw_" + nm, big_w[nm], g, big_m[nm], big_v[nm])
        grads[nm], deltas[nm], new_m[nm], new_v[nm] = (a.reshape((1,) + shape) for a in (g, d, m2, v2))

    small_g = dict(loc)
    small_g["meta_tokens"] = loc["dmeta"]
    n_small = sum(sz for _, sz in SMALL)
    rows = -(-n_small // 128 // 8) * 8
    (g_small,) = _gather8_call("gather_small_grads", [_pack_rows([small_g[nm] for nm, _ in SMALL], rows)])
    tot = _sum8_call("sum_small_grads", g_small).reshape(-1)
    off = 0
    sg = {}
    for nm, sz in SMALL:
        sg[nm] = tot[off:off + sz]
        off += sz
    sg["w_gate_up"] = lax.dynamic_slice_in_dim(sg["w_gate_up"].reshape(GATE_RANK, GLA_KW), kme * (GLA_KW // 4),
                                               GLA_KW // 4, axis=1)
    sg["meta_tokens"] = lax.dynamic_slice_in_dim(sg["meta_tokens"].reshape(N_META, D_MODEL), kme * (D_MODEL // 4),
                                                 D_MODEL // 4, axis=1)
    small_w = dict(norm_gain=norm_gain, b_gate=b_gate, ret_norm_gain=ret_norm_gain, gla_norm_gain=gla_norm_gain,
                   final_norm_gain=final_norm_gain, w_gate_up=w_gate_up, meta_tokens=meta_tokens)
    small_m = dict(norm_gain=m_norm_gain, b_gate=m_b_gate, ret_norm_gain=m_ret_norm_gain,
                   gla_norm_gain=m_gla_norm_gain, final_norm_gain=m_final_norm_gain, w_gate_up=m_w_gate_up,
                   meta_tokens=m_meta_tokens)
    small_v = dict(norm_gain=v_norm_gain, b_gate=v_b_gate, ret_norm_gain=v_ret_norm_gain,
                   gla_norm_gain=v_gla_norm_gain, final_norm_gain=v_final_norm_gain, w_gate_up=v_w_gate_up,
                   meta_tokens=v_meta_tokens)
    order = [nm for nm, _ in SMALL]
    sizes = [small_w[nm].size for nm in order]
    prow = -(-sum(sizes) // 128 // 8) * 8
    pk = lambda d: _pack_rows([d[nm] for nm in order], prow)
    d_s, m_s, v_s = _adamw_call("adamw_small", pk(small_w), pk(sg), pk(small_m), pk(small_v))
    off = 0
    for nm, sz in zip(order, sizes):
        shape = small_w[nm].shape
        grads[nm] = sg[nm].reshape(shape)
        deltas[nm], new_m[nm], new_v[nm] = (a.reshape(-1)[off:off + sz].reshape(shape) for a in (d_s, m_s, v_s))
        off += sz

    out_order = ("meta_tokens", "norm_gain", "w_in", "w_gate_up", "b_gate", "ret_norm_gain", "gla_norm_gain",
                 "w_branch_ret", "w_branch_gla", "w_out", "final_norm_gain")
    dx = loc["dx"].reshape(x.shape)
    return (loss, dx, *[grads[nm] for nm in out_order], *[deltas[nm] for nm in out_order],
            *[new_m[nm] for nm in out_order], *[new_v[nm] for nm in out_order])
```

```python
import functools
import math

import numpy as np
import jax
import jax.numpy as jnp
from jax import lax
from jax.experimental import pallas as pl
from jax.experimental.pallas import tpu as pltpu

F32 = jnp.float32
BF16 = jnp.bfloat16

D_MODEL = 1024
N_META = 16
EPS = 1e-6
ROPE_BASE = 10000.0
RET_HEADS, RET_QK, RET_V = 4, 256, 512
RET_W = RET_HEADS * RET_V
GLA_HEADS, GLA_K, GLA_V = 4, 128, 256
GLA_W = GLA_HEADS * GLA_V
GLA_KW = GLA_HEADS * GLA_K
GATE_RANK = 16
GATE_TAU = 16.0
GLA_SUB = 16

TM = 256
T0 = TM
PADF = T0 - N_META
GC = 128
TB = 768
TK = 768

W_R = 6144
W_G = 3088
W_GP = 3200
W_M = 2048
IN_COLS = W_R + W_G + W_M

ADAM_LR, ADAM_B1, ADAM_B2, ADAM_EPS, ADAM_WD, ADAM_STEP = 0.001, 0.9, 0.999, 1e-08, 0.01, 10

VMEM_LIMIT = 56 * 1024 * 1024

NN = ((1,), (0,))
NT = ((1,), (1,))
TN = ((0,), (0,))


def _dot(a, b, dims):
    return lax.dot_general(a, b, (dims, ((), ())), preferred_element_type=F32)


def _cparams(n_axes):
    return pltpu.CompilerParams(dimension_semantics=("arbitrary",) * n_axes, vmem_limit_bytes=VMEM_LIMIT)


def _sigmoid(x):
    return 1.0 / (1.0 + jnp.exp(-x))


def _split3(x):
    hi = x.astype(BF16)
    r1 = x - hi.astype(F32)
    mid = r1.astype(BF16)
    lo = (r1 - mid.astype(F32)).astype(BF16)
    return hi, mid, lo


def _exact_pm(p, x):
    hi, mid, lo = _split3(x)
    return _dot(p, hi, NN) + _dot(p, mid, NN) + _dot(p, lo, NN)


def _rms_call(x2d, head, gain):
    tp = T0 + x2d.shape[0]

    def body(x_ref, hd_ref, g_ref, h_ref, u_ref):
        h = jnp.where(pl.program_id(0) == 0, hd_ref[...], x_ref[...])
        h_ref[...] = h
        r = lax.rsqrt(jnp.mean(h * h, axis=-1, keepdims=True) + EPS)
        u_ref[...] = (h * r * g_ref[...]).astype(BF16)

    tile = pl.BlockSpec((TM, D_MODEL), lambda i: (i, 0))
    return pl.pallas_call(
        body, name="rms_in", grid=(tp // TM,),
        in_specs=[pl.BlockSpec((TM, D_MODEL), lambda i: (jnp.maximum(i - 1, 0), 0)),
                  pl.BlockSpec((T0, D_MODEL), lambda i: (0, 0)), pl.BlockSpec((1, D_MODEL), lambda i: (0, 0))],
        out_specs=[tile, tile],
        out_shape=[jax.ShapeDtypeStruct((tp, D_MODEL), F32), jax.ShapeDtypeStruct((tp, D_MODEL), BF16)],
        compiler_params=_cparams(1),
    )(x2d, head, gain)


def _mm_nn(name, a, b, out_dtype, tn, col0, ncols, epilogue=None, extras=(), extra_specs=()):
    m, k = a.shape
    nj, j0 = ncols // tn, col0 // tn

    def body(a_ref, b_ref, *rest):
        *ex, o_ref = rest
        acc = _dot(a_ref[...], b_ref[...], NN)
        if epilogue is None:
            o_ref[...] = acc.astype(out_dtype)
        else:
            epilogue(acc, o_ref, *ex)

    return pl.pallas_call(
        body, name=name, grid=(nj, m // TB),
        in_specs=[pl.BlockSpec((TB, k), lambda j, i: (i, 0)), pl.BlockSpec((k, tn), lambda j, i: (0, j0 + j))]
        + list(extra_specs),
        out_specs=pl.BlockSpec((TB, tn), lambda j, i: (i, j)),
        out_shape=jax.ShapeDtypeStruct((m, ncols), out_dtype),
        compiler_params=_cparams(2),
    )(a, b, *extras)


def _rope_epilogue(acc, o_ref, cos_ref, sin_ref):
    scale = jnp.where(pl.program_id(0) == 1, RET_QK ** -0.5, 1.0).astype(F32)
    cos, sin = cos_ref[...], sin_ref[...]
    half = RET_QK // 2
    for h in range(RET_HEADS):
        t1 = acc[:, h * RET_QK:h * RET_QK + half]
        t2 = acc[:, h * RET_QK + half:(h + 1) * RET_QK]
        o_ref[:, h * RET_QK:h * RET_QK + half] = ((t1 * cos - t2 * sin) * scale).astype(BF16)
        o_ref[:, h * RET_QK + half:(h + 1) * RET_QK] = ((t2 * cos + t1 * sin) * scale).astype(BF16)


def _gqk_epilogue(acc, o_ref):
    o_ref[:, :GLA_KW] = acc[:, :GLA_KW] * (GLA_K ** -0.5)
    o_ref[:, GLA_KW:] = acc[:, GLA_KW:]


def _exchange_copies(s_refs, b_refs, send_sems, recv_sems):
    x, y, c = _place()
    chips = [(1 - x, y), (x, 1 - y), (1 - x, 1 - y)]
    return [pltpu.make_async_remote_copy(
        src_ref=s_refs[t].at[2 * chip[0] + chip[1]], dst_ref=b_refs[t].at[j], send_sem=send_sems.at[3 * t + j],
        recv_sem=recv_sems.at[3 * t + j], device_id=(*chip, c), device_id_type=MESH)
        for t in range(len(s_refs)) for j, chip in enumerate(chips)]


def _exchange_shapes(ss):
    return ([jax.ShapeDtypeStruct((3,) + s.shape[1:], s.dtype) for s in ss],
            [pltpu.SemaphoreType.DMA((3 * len(ss),)), pltpu.SemaphoreType.DMA((3 * len(ss),))])


def _mm_nt_acc(name, a, w, tk, acc_in=None, epilogue=None, extras=(), extra_specs=(), extra_out_shapes=(),
               extra_out_specs=(), exchange=()):
    m, k = a.shape
    n = w.shape[0]
    nk, ni = k // tk, m // TB
    has_acc = acc_in is not None
    n_xc = len(exchange)

    def body(*refs):
        a_ref, w_ref = refs[0], refs[1]
        pos = 2
        acc_ref = None
        if has_acc:
            acc_ref = refs[pos]
            pos += 1
        ex = refs[pos:pos + len(extras)]
        pos += len(extras)
        xc_src = refs[pos:pos + n_xc]
        pos += n_xc
        n_scr = 3 if n_xc else 1
        outs = refs[pos:len(refs) - n_scr - n_xc]
        xc_dst = refs[len(refs) - n_scr - n_xc:len(refs) - n_scr]
        scr = refs[len(refs) - n_scr]
        i, kk = pl.program_id(0), pl.program_id(1)
        if n_xc:
            copies = _exchange_copies(xc_src, xc_dst, refs[-2], refs[-1])

            @pl.when((i == 0) & (kk == 0))
            def _():
                for cp in copies:
                    cp.start()

        @pl.when(kk == 0)
        def _():
            scr[...] = acc_ref[...] if has_acc else jnp.zeros_like(scr)

        scr[...] += _dot(a_ref[...], w_ref[...], NT)

        @pl.when(kk == nk - 1)
        def _():
            if epilogue is None:
                outs[0][...] = scr[...]
            else:
                epilogue(scr[...], outs, i, *ex)

        if n_xc:
            @pl.when((i == ni - 1) & (kk == nk - 1))
            def _():
                for cp in copies:
                    cp.wait()

    in_specs = [pl.BlockSpec((TB, tk), lambda i, kk: (i, kk)), pl.BlockSpec((n, tk), lambda i, kk: (0, kk))]
    args = [a, w]
    if has_acc:
        in_specs.append(pl.BlockSpec((TB, n), lambda i, kk: (i, 0)))
        args.append(acc_in)
    in_specs += list(extra_specs) + [ANY] * n_xc
    args += list(extras) + list(exchange)
    if epilogue is None:
        out_shape = [jax.ShapeDtypeStruct((m, n), F32)]
        out_specs = [pl.BlockSpec((TB, n), lambda i, kk: (i, 0))]
    else:
        out_shape, out_specs = list(extra_out_shapes), list(extra_out_specs)
    scratch = [pltpu.VMEM((TB, n), F32)]
    if n_xc:
        xc_shapes, xc_sems = _exchange_shapes(exchange)
        out_shape += xc_shapes
        out_specs += [ANY] * n_xc
        scratch += xc_sems
    return pl.pallas_call(
        body, name=name, grid=(ni, nk), in_specs=in_specs, out_specs=out_specs, out_shape=out_shape,
        scratch_shapes=scratch, compiler_params=_cparams(2),
    )(*args)


def _rms_bwd_epilogue(du, outs, i, h_ref, g_ref, dh1_ref):
    dh0_ref, dg_ref = outs
    h = h_ref[...]
    r = lax.rsqrt(jnp.mean(h * h, axis=-1, keepdims=True) + EPS)
    xh = h * r
    dxh = du * g_ref[...]
    dh0_ref[...] = dh1_ref[...] + r * (dxh - xh * jnp.mean(dxh * xh, axis=-1, keepdims=True))

    @pl.when(i == 0)
    def _():
        dg_ref[...] = jnp.zeros_like(dg_ref)

    dg_ref[...] += jnp.sum(du * xh, axis=0, keepdims=True)


def _mm_tn(name, a, b, bn):
    t, m = a.shape
    n = b.shape[1]

    def body(a_ref, b_ref, o_ref):
        @pl.when(pl.program_id(1) == 0)
        def _():
            o_ref[...] = jnp.zeros_like(o_ref)

        o_ref[...] += _dot(a_ref[...], b_ref[...], TN)

    return pl.pallas_call(
        body, name=name, grid=(n // bn, t // TK),
        in_specs=[pl.BlockSpec((TK, m), lambda j, kk: (kk, 0)), pl.BlockSpec((TK, bn), lambda j, kk: (kk, j))],
        out_specs=pl.BlockSpec((m, bn), lambda j, kk: (0, j)),
        out_shape=jax.ShapeDtypeStruct((m, n), F32),
        compiler_params=_cparams(2),
    )(a, b)


def _ret_fill_decay(lg_ref, dm_scr):
    c = TM
    ii = lax.broadcasted_iota(jnp.int32, (c, c), 0)
    jj = lax.broadcasted_iota(jnp.int32, (c, c), 1)
    rel = (ii - jj).astype(F32)
    for h in range(RET_HEADS):
        dm_scr[h] = jnp.where(rel >= 0, jnp.exp(jnp.maximum(rel, 0.0) * lg_ref[h]), 0.0)


def _ret_consts(lg, dm_ref):
    c = TM
    idx = lax.broadcasted_iota(jnp.int32, (c, 1), 0).astype(F32)
    xi = jnp.exp((idx + 1.0) * lg)
    zeta = jnp.exp((c - 1.0 - idx) * lg)
    gc = jnp.exp(jnp.full((1, 1), c, F32) * lg)
    return dm_ref[...], xi, zeta, gc


def _ret_fwd_call(rqk, rv, rg, gain, lgam):
    tp = rqk.shape[0]
    nc = tp // TM

    def body(lg_ref, qk_ref, v_ref, rg_ref, g_ref, o_ref, a_ref, st_ref, s_scr, dm_scr):
        @pl.when(pl.program_id(0) == 0)
        def _():
            s_scr[...] = jnp.zeros_like(s_scr)
            _ret_fill_decay(lg_ref, dm_scr)

        for h in range(RET_HEADS):
            dm, xi, zeta, gc = _ret_consts(lg_ref[h], dm_scr.at[h])
            q = qk_ref[:, h * RET_QK:(h + 1) * RET_QK]
            k = qk_ref[:, D_MODEL + h * RET_QK:D_MODEL + (h + 1) * RET_QK]
            v = v_ref[:, h * RET_V:(h + 1) * RET_V]
            sb = s_scr[h].astype(BF16)
            st_ref[0, h] = sb
            s = _dot(q, k, NT) * dm
            o = _dot(s.astype(BF16), v, NN) + xi * _dot(q, sb, NN)
            kz = (k.astype(F32) * zeta).astype(BF16)
            s_scr[h] = gc * s_scr[h] + _dot(kz, v, TN)
            o_ref[:, h * RET_V:(h + 1) * RET_V] = o
            mu = jnp.mean(o, axis=-1, keepdims=True)
            xc = o - mu
            xh = xc * lax.rsqrt(jnp.mean(xc * xc, axis=-1, keepdims=True) + EPS)
            g = rg_ref[:, h * RET_V:(h + 1) * RET_V]
            a_ref[:, h * RET_V:(h + 1) * RET_V] = (
                xh * g_ref[:, h * RET_V:(h + 1) * RET_V] * (g * _sigmoid(g))).astype(BF16)

    return pl.pallas_call(
        body, name="ret_fwd", grid=(nc,),
        in_specs=[pl.BlockSpec(memory_space=pltpu.SMEM),
                  pl.BlockSpec((TM, 2 * D_MODEL), lambda n: (n, 0)),
                  pl.BlockSpec((TM, RET_W), lambda n: (n, 0)),
                  pl.BlockSpec((TM, RET_W), lambda n: (n, 0)),
                  pl.BlockSpec((1, RET_W), lambda n: (0, 0))],
        out_specs=[pl.BlockSpec((TM, RET_W), lambda n: (n, 0)),
                   pl.BlockSpec((TM, RET_W), lambda n: (n, 0)),
                   pl.BlockSpec((1, RET_HEADS, RET_QK, RET_V), lambda n: (n, 0, 0, 0))],
        out_shape=[jax.ShapeDtypeStruct((tp, RET_W), F32), jax.ShapeDtypeStruct((tp, RET_W), BF16),
                   jax.ShapeDtypeStruct((nc, RET_HEADS, RET_QK, RET_V), BF16)],
        scratch_shapes=[pltpu.VMEM((RET_HEADS, RET_QK, RET_V), F32), pltpu.VMEM((RET_HEADS, TM, TM), F32)],
        compiler_params=_cparams(1),
    )(lgam, rqk, rv, rg, gain)


def _ret_bwd_call(rqk, rv, rg, o_ret, da, states, gain, lgam, cos, sin):
    tp = rqk.shape[0]
    nc = tp // TM
    half = RET_QK // 2

    def body(lg_ref, qk_ref, v_ref, rg_ref, o_ref, da_ref, st_ref, g_ref, cos_ref, sin_ref, dp_ref, dg_ref, ds_scr,
             dm_scr):
        @pl.when(pl.program_id(0) == 0)
        def _():
            ds_scr[...] = jnp.zeros_like(ds_scr)
            dg_ref[...] = jnp.zeros_like(dg_ref)
            _ret_fill_decay(lg_ref, dm_scr)

        cos, sin = cos_ref[...], sin_ref[...]
        for h in range(RET_HEADS):
            hs = slice(h * RET_V, (h + 1) * RET_V)
            dm, xi, zeta, gc = _ret_consts(lg_ref[h], dm_scr.at[h])
            o = o_ref[:, hs]
            mu = jnp.mean(o, axis=-1, keepdims=True)
            xc = o - mu
            rstd = lax.rsqrt(jnp.mean(xc * xc, axis=-1, keepdims=True) + EPS)
            xh = xc * rstd
            gain_h = g_ref[:, hs]
            g = rg_ref[:, hs]
            sg = _sigmoid(g)
            silu = g * sg
            dah = da_ref[:, hs]
            dp_ref[:, 4 * D_MODEL + h * RET_V:4 * D_MODEL + (h + 1) * RET_V] = (
                dah * (xh * gain_h) * (sg * (1.0 + g * (1.0 - sg)))).astype(BF16)
            dn = dah * silu
            dg_ref[:, hs] += jnp.sum(dn * xh, axis=0, keepdims=True)
            dxh = dn * gain_h
            do = rstd * (dxh - jnp.mean(dxh, axis=-1, keepdims=True)
                         - xh * jnp.mean(dxh * xh, axis=-1, keepdims=True))
            dob = do.astype(BF16)
            q = qk_ref[:, h * RET_QK:(h + 1) * RET_QK]
            k = qk_ref[:, D_MODEL + h * RET_QK:D_MODEL + (h + 1) * RET_QK]
            v = v_ref[:, hs]
            sp = st_ref[0, h]
            ds = ds_scr[h]
            dsb = ds.astype(BF16)
            s = (_dot(q, k, NT) * dm).astype(BF16)
            dsc = (_dot(dob, v, NT) * dm).astype(BF16)
            dq = _dot(dsc, k, NN) + xi * _dot(dob, sp, NT)
            dk = _dot(dsc, q, TN) + zeta * _dot(v, dsb, NT)
            kz = (k.astype(F32) * zeta).astype(BF16)
            dv = _dot(s, dob, TN) + _dot(kz, dsb, NN)
            qx = (q.astype(F32) * xi).astype(BF16)
            ds_scr[h] = gc * ds + _dot(qx, dob, TN)
            dp_ref[:, 2 * D_MODEL + h * RET_V:2 * D_MODEL + (h + 1) * RET_V] = dv.astype(BF16)
            dk = dk * (RET_QK ** -0.5)
            for base, t in ((0, dq), (D_MODEL, dk)):
                t1, t2 = t[:, :half], t[:, half:]
                dp_ref[:, base + h * RET_QK:base + h * RET_QK + half] = (t1 * cos + t2 * sin).astype(BF16)
                dp_ref[:, base + h * RET_QK + half:base + (h + 1) * RET_QK] = (t2 * cos - t1 * sin).astype(BF16)

    rev = lambda n: (nc - 1 - n, 0)
    return pl.pallas_call(
        body, name="ret_bwd", grid=(nc,),
        in_specs=[pl.BlockSpec(memory_space=pltpu.SMEM),
                  pl.BlockSpec((TM, 2 * D_MODEL), rev),
                  pl.BlockSpec((TM, RET_W), rev),
                  pl.BlockSpec((TM, RET_W), rev),
                  pl.BlockSpec((TM, RET_W), rev),
                  pl.BlockSpec((TM, RET_W), rev),
                  pl.BlockSpec((1, RET_HEADS, RET_QK, RET_V), lambda n: (nc - 1 - n, 0, 0, 0)),
                  pl.BlockSpec((1, RET_W), lambda n: (0, 0)),
                  pl.BlockSpec((TM, half), rev),
                  pl.BlockSpec((TM, half), rev)],
        out_specs=[pl.BlockSpec((TM, W_R), rev), pl.BlockSpec((1, RET_W), lambda n: (0, 0))],
        out_shape=[jax.ShapeDtypeStruct((tp, W_R), BF16), jax.ShapeDtypeStruct((1, RET_W), F32)],
        scratch_shapes=[pltpu.VMEM((RET_HEADS, RET_QK, RET_V), F32), pltpu.VMEM((RET_HEADS, TM, TM), F32)],
        compiler_params=_cparams(1),
    )(lgam, rqk, rv, rg, o_ret, da, states, gain, cos, sin)


GLA_LEVELS = tuple(GC >> (s + 1) for s in range(int(math.log2(GC // GLA_SUB))))
NLEV = len(GLA_LEVELS)


def _gla_tril():
    return np.tril(np.ones((GC, GC), np.float32))


def _gla_masks():
    ii = lax.broadcasted_iota(jnp.int32, (GC, GC), 0)
    jj = lax.broadcasted_iota(jnp.int32, (GC, GC), 1)
    masks = []
    for m in GLA_LEVELS:
        sh = int(math.log2(2 * m))
        masks.append(((ii >> sh) == (jj >> sh)) & ((ii & m) != 0) & ((jj & m) == 0))
    sh = int(math.log2(GLA_SUB))
    md = ((ii >> sh) == (jj >> sh)) & (jj <= ii)
    row = lax.broadcasted_iota(jnp.int32, (GC, 1), 0)
    second = [(row & m) != 0 for m in GLA_LEVELS]
    return masks, md, second


def _gla_log_decay(glr_ref, wg_ref, bg_ref):
    z = _dot(glr_ref[...].astype(BF16), wg_ref[...], NN) + bg_ref[...]
    la = (jnp.minimum(z, 0.0) - jnp.log1p(jnp.exp(-jnp.abs(z)))) * (1.0 / GATE_TAU)
    return z, la


def _gla_row_steps(b_ref, cs, rows, size):
    parts = [jnp.zeros((size, GLA_K), F32) if r is None else jnp.broadcast_to(b_ref[r:r + 1, cs], (size, GLA_K))
             for r in rows]
    return parts[0] if len(parts) == 1 else jnp.concatenate(parts, axis=0)


def _gla_factors(b_ref, h, second):
    cs = slice(h * GLA_K, (h + 1) * GLA_K)
    b = b_ref[:, cs]
    fq, fk = [], []
    for l, m in enumerate(GLA_LEVELS):
        d = b - _gla_row_steps(b_ref, cs, [s + m - 1 for s in range(0, GC, 2 * m)], 2 * m)
        f = jnp.exp(jnp.where(second[l], d, -d))
        fq.append(jnp.where(second[l], f, 0.0))
        fk.append(jnp.where(second[l], 0.0, f))
    dd = b - _gla_row_steps(b_ref, cs, [None] + [s - 1 for s in range(GLA_SUB, GC, GLA_SUB)], GLA_SUB)
    ed = jnp.exp(dd)
    edi = jnp.exp(-dd)
    eb = jnp.exp(b)
    bl = b_ref[GC - 1:GC, cs]
    ee = jnp.exp(bl - b)
    ebl = jnp.exp(bl)
    return fq, fk, ed, edi, eb, ee, ebl


def _gla_scores(q, k, fq, fk, ed, edi, masks, md):
    qt = [(q * f).astype(BF16) for f in fq]
    kt = [(k * f).astype(BF16) for f in fk]
    qd = (q * ed).astype(BF16)
    kd = (k * edi).astype(BF16)
    a = jnp.where(md, _dot(qd, kd, NT), 0.0)
    for l in range(NLEV):
        a = a + jnp.where(masks[l], _dot(qt[l], kt[l], NT), 0.0)
    return a, qt, kt, qd, kd


def _gla_fwd_call(gqk, gv, glr, gg, wg, bg, gain, pmat):
    tp = gqk.shape[0]
    nc = tp // GC

    def body(qk_ref, v_ref, glr_ref, gg_ref, wg_ref, bg_ref, g_ref, p_ref, o_ref, a_ref, st_ref, s_scr, b_scr):
        @pl.when(pl.program_id(0) == 0)
        def _():
            s_scr[...] = jnp.zeros_like(s_scr)

        _, la = _gla_log_decay(glr_ref, wg_ref, bg_ref)
        b_scr[...] = _exact_pm(p_ref[...], la)
        masks, md, second = _gla_masks()
        for h in range(GLA_HEADS):
            q = qk_ref[:, h * GLA_K:(h + 1) * GLA_K]
            k = qk_ref[:, GLA_KW + h * GLA_K:GLA_KW + (h + 1) * GLA_K]
            vs = slice(h * GLA_V, (h + 1) * GLA_V)
            v = v_ref[:, vs]
            fq, fk, ed, edi, eb, ee, ebl = _gla_factors(b_scr, h, second)
            a, *_ = _gla_scores(q, k, fq, fk, ed, edi, masks, md)
            sb = s_scr[h].astype(BF16)
            st_ref[0, h] = sb
            o = _dot(a.astype(BF16), v, NN) + _dot((q * eb).astype(BF16), sb, NT)
            s_scr[h] = s_scr[h] * ebl + _dot(v, (k * ee).astype(BF16), TN)
            o_ref[:, vs] = o
            xh = o * lax.rsqrt(jnp.mean(o * o, axis=-1, keepdims=True) + EPS)
            g = gg_ref[:, vs]
            a_ref[:, vs] = (xh * g_ref[:, vs] * (g * _sigmoid(g))).astype(BF16)

    return pl.pallas_call(
        body, name="gla_fwd", grid=(nc,),
        in_specs=[pl.BlockSpec((GC, 2 * GLA_KW), lambda n: (n, 0)),
                  pl.BlockSpec((GC, GLA_W), lambda n: (n, 0)),
                  pl.BlockSpec((GC, 128), lambda n: (n, 0)),
                  pl.BlockSpec((GC, GLA_W), lambda n: (n, 0)),
                  pl.BlockSpec((128, GLA_KW), lambda n: (0, 0)),
                  pl.BlockSpec((1, GLA_KW), lambda n: (0, 0)),
                  pl.BlockSpec((1, GLA_W), lambda n: (0, 0)),
                  pl.BlockSpec((GC, GC), lambda n: (0, 0))],
        out_specs=[pl.BlockSpec((GC, GLA_W), lambda n: (n, 0)),
                   pl.BlockSpec((GC, GLA_W), lambda n: (n, 0)),
                   pl.BlockSpec((1, GLA_HEADS, GLA_V, GLA_K), lambda n: (n, 0, 0, 0))],
        out_shape=[jax.ShapeDtypeStruct((tp, GLA_W), F32), jax.ShapeDtypeStruct((tp, GLA_W), BF16),
                   jax.ShapeDtypeStruct((nc, GLA_HEADS, GLA_V, GLA_K), BF16)],
        scratch_shapes=[pltpu.VMEM((GLA_HEADS, GLA_V, GLA_K), F32), pltpu.VMEM((GC, GLA_KW), F32)],
        compiler_params=_cparams(1),
    )(gqk, gv, glr, gg, wg, bg, gain, pmat)


def _gla_bwd_call(gqk, gv, glr, gg, o_gla, da, states, wg, bg, gain, pmat, pmat_t, exchange=()):
    tp = gqk.shape[0]
    nc = tp // GC
    o_gv, o_gg, o_lr = 2 * GLA_KW, 2 * GLA_KW + GLA_W, 2 * GLA_KW + 2 * GLA_W
    n_xc = len(exchange)

    def body(qk_ref, v_ref, glr_ref, gg_ref, o_ref, da_ref, st_ref, wg_ref, bg_ref, g_ref, p_ref, pt_ref, *rest):
        xc_src = rest[:n_xc]
        dp_ref, dwg_ref, dbg_ref, dg_ref = rest[n_xc:n_xc + 4]
        xc_dst = rest[n_xc + 4:2 * n_xc + 4]
        ds_scr, b_scr, db_scr = rest[2 * n_xc + 4:2 * n_xc + 7]
        n = pl.program_id(0)
        if n_xc:
            copies = _exchange_copies(xc_src, xc_dst, rest[-2], rest[-1])

            @pl.when(n == 0)
            def _():
                for cp in copies:
                    cp.start()

            @pl.when(n == nc - 1)
            def _():
                for cp in copies:
                    cp.wait()

        @pl.when(n == 0)
        def _():
            ds_scr[...] = jnp.zeros_like(ds_scr)
            dwg_ref[...] = jnp.zeros_like(dwg_ref)
            dbg_ref[...] = jnp.zeros_like(dbg_ref)
            dg_ref[...] = jnp.zeros_like(dg_ref)

        z, la = _gla_log_decay(glr_ref, wg_ref, bg_ref)
        b_scr[...] = _exact_pm(p_ref[...], la)
        masks, md, second = _gla_masks()
        for h in range(GLA_HEADS):
            cs = slice(h * GLA_K, (h + 1) * GLA_K)
            vs = slice(h * GLA_V, (h + 1) * GLA_V)
            o = o_ref[:, vs]
            rstd = lax.rsqrt(jnp.mean(o * o, axis=-1, keepdims=True) + EPS)
            xh = o * rstd
            gain_h = g_ref[:, vs]
            g = gg_ref[:, vs]
            sg = _sigmoid(g)
            dah = da_ref[:, vs]
            dp_ref[:, o_gg + h * GLA_V:o_gg + (h + 1) * GLA_V] = (
                dah * (xh * gain_h) * (sg * (1.0 + g * (1.0 - sg)))).astype(BF16)
            dn = dah * (g * sg)
            dg_ref[:, vs] += jnp.sum(dn * xh, axis=0, keepdims=True)
            dxh = dn * gain_h
            do = rstd * (dxh - xh * jnp.mean(dxh * xh, axis=-1, keepdims=True))
            dob = do.astype(BF16)
            q = qk_ref[:, cs]
            k = qk_ref[:, GLA_KW + h * GLA_K:GLA_KW + (h + 1) * GLA_K]
            v = v_ref[:, vs]
            fq, fk, ed, edi, eb, ee, ebl = _gla_factors(b_scr, h, second)
            a, qt, kt, qd, kd = _gla_scores(q, k, fq, fk, ed, edi, masks, md)
            sp = st_ref[0, h]
            ds = ds_scr[h]
            dsb = ds.astype(BF16)
            q_in = q * eb
            k_end = k * ee
            da_s = _dot(dob, v, NT)
            dv = _dot(a.astype(BF16), dob, TN) + _dot(k_end.astype(BF16), dsb, NT)
            dq_in = _dot(dob, sp, NN)
            dk_end = _dot(v, dsb, NN)
            dbl = jnp.sum(sp.astype(F32) * ds, axis=0, keepdims=True) * ebl
            ds_scr[h] = ds * ebl + _dot(dob, q_in.astype(BF16), TN)
            dq = dq_in * eb
            dk = dk_end * ee
            de_end = dk_end * k_end
            db = dq_in * q_in - de_end
            placed = [(GC - 1, jnp.sum(de_end, axis=0, keepdims=True) + dbl)]
            for l, m in enumerate(GLA_LEVELS):
                dal = jnp.where(masks[l], da_s, 0.0).astype(BF16)
                dqt = _dot(dal, kt[l], NN)
                dkt = _dot(dal, qt[l], TN)
                dq = dq + dqt * fq[l]
                dk = dk + dkt * fk[l]
                gl = dqt * (q * fq[l]) - dkt * (k * fk[l])
                db = db + gl
                placed += [(s + m - 1, -jnp.sum(gl[s:s + 2 * m], axis=0, keepdims=True)) for s in range(0, GC, 2 * m)]
            dad = jnp.where(md, da_s, 0.0).astype(BF16)
            dqd = _dot(dad, kd, NN)
            dkd = _dot(dad, qd, TN)
            dq = dq + dqd * ed
            dk = dk + dkd * edi
            gd = dqd * (q * ed) - dkd * (k * edi)
            db = db + gd
            placed += [(s - 1, -jnp.sum(gd[s:s + GLA_SUB], axis=0, keepdims=True)) for s in range(GLA_SUB, GC, GLA_SUB)]
            db_scr[:, cs] = db
            for r, val in placed:
                db_scr[r:r + 1, cs] += val
            dp_ref[:, cs] = (dq * (GLA_K ** -0.5)).astype(BF16)
            dp_ref[:, GLA_KW + h * GLA_K:GLA_KW + (h + 1) * GLA_K] = dk.astype(BF16)
            dp_ref[:, o_gv + h * GLA_V:o_gv + (h + 1) * GLA_V] = dv.astype(BF16)
        dla = _exact_pm(pt_ref[...], db_scr[...])
        row = (nc - 1 - n) * GC + lax.broadcasted_iota(jnp.int32, (GC, 1), 0)
        dz = jnp.where(row >= PADF, dla * (1.0 / GATE_TAU) * _sigmoid(-z), 0.0)
        dzb = dz.astype(BF16)
        dp_ref[:, o_lr:] = _dot(dzb, wg_ref[...], NT).astype(BF16)
        dwg_ref[...] += _dot(glr_ref[...].astype(BF16), dzb, TN)
        dbg_ref[...] += jnp.sum(dz, axis=0, keepdims=True)

    rev = lambda n: (nc - 1 - n, 0)
    const = lambda n: (0, 0)
    xc_shapes, xc_sems = _exchange_shapes(exchange) if n_xc else ([], [])
    return pl.pallas_call(
        body, name="gla_bwd", grid=(nc,),
        in_specs=[pl.BlockSpec((GC, 2 * GLA_KW), rev),
                  pl.BlockSpec((GC, GLA_W), rev),
                  pl.BlockSpec((GC, 128), rev),
                  pl.BlockSpec((GC, GLA_W), rev),
                  pl.BlockSpec((GC, GLA_W), rev),
                  pl.BlockSpec((GC, GLA_W), rev),
                  pl.BlockSpec((1, GLA_HEADS, GLA_V, GLA_K), lambda n: (nc - 1 - n, 0, 0, 0)),
                  pl.BlockSpec((128, GLA_KW), const),
                  pl.BlockSpec((1, GLA_KW), const),
                  pl.BlockSpec((1, GLA_W), const),
                  pl.BlockSpec((GC, GC), const),
                  pl.BlockSpec((GC, GC), const)] + [ANY] * n_xc,
        out_specs=[pl.BlockSpec((GC, W_GP), rev), pl.BlockSpec((128, GLA_KW), const),
                   pl.BlockSpec((1, GLA_KW), const), pl.BlockSpec((1, GLA_W), const)] + [ANY] * n_xc,
        out_shape=[jax.ShapeDtypeStruct((tp, W_GP), BF16), jax.ShapeDtypeStruct((128, GLA_KW), F32),
                   jax.ShapeDtypeStruct((1, GLA_KW), F32), jax.ShapeDtypeStruct((1, GLA_W), F32)] + xc_shapes,
        scratch_shapes=[pltpu.VMEM((GLA_HEADS, GLA_V, GLA_K), F32), pltpu.VMEM((GC, GLA_KW), F32),
                        pltpu.VMEM((GC, GLA_KW), F32)] + xc_sems,
        compiler_params=_cparams(1),
    )(gqk, gv, glr, gg, o_gla, da, states, wg, bg, gain, pmat, pmat_t, *exchange)


def _mid_call(a_ret, a_gla, mg, h0, tgt, wbr, wbg, wout, gf):
    tp = h0.shape[0]
    nt = tp // TM

    def body(ar_ref, ag_ref, mg_ref, h_ref, t_ref, wbr_ref, wbg_ref, wo_ref, gf_ref,
             dh1_ref, dar_ref, dag_ref, dm_ref, mb_ref, dh1b_ref, dprb_ref, dpgb_ref, loss_ref, dgf_ref):
        i = pl.program_id(0)

        @pl.when(i == 0)
        def _():
            loss_ref[...] = jnp.zeros_like(loss_ref)
            dgf_ref[...] = jnp.zeros_like(dgf_ref)

        ar, ag = ar_ref[...], ag_ref[...]
        pr = _dot(ar, wbr_ref[...], NN)
        pg = _dot(ag, wbg_ref[...], NN)
        sr = _sigmoid(mg_ref[:, :D_MODEL])
        sg = _sigmoid(mg_ref[:, D_MODEL:])
        merged = (sr * pr + sg * pg).astype(BF16)
        mb_ref[...] = merged
        h1 = h_ref[...] + _dot(merged, wo_ref[...], NN)
        r1 = lax.rsqrt(jnp.mean(h1 * h1, axis=-1, keepdims=True) + EPS)
        xh = h1 * r1
        gfv = gf_ref[...]
        live = jnp.where(i > 0, 1.0, 0.0).astype(F32)
        err = (xh * gfv - t_ref[...]) * live
        loss_ref[...] += jnp.full(loss_ref.shape, 0.5 / D_MODEL, F32) * jnp.sum(err * err)
        dy = err * (1.0 / D_MODEL)
        dgf_ref[...] += jnp.sum(dy * xh, axis=0, keepdims=True)
        dxh = dy * gfv
        dh1 = r1 * (dxh - xh * jnp.mean(dxh * xh, axis=-1, keepdims=True))
        dh1_ref[...] = dh1
        dh1b = dh1.astype(BF16)
        dh1b_ref[...] = dh1b
        dmerged = _dot(dh1b, wo_ref[...], NT)
        dm_ref[:, :D_MODEL] = (dmerged * pr * sr * (1.0 - sr)).astype(BF16)
        dm_ref[:, D_MODEL:] = (dmerged * pg * sg * (1.0 - sg)).astype(BF16)
        dpr = (dmerged * sr).astype(BF16)
        dpg = (dmerged * sg).astype(BF16)
        dprb_ref[...] = dpr
        dpgb_ref[...] = dpg
        dar_ref[...] = _dot(dpr, wbr_ref[...], NT)
        dag_ref[...] = _dot(dpg, wbg_ref[...], NT)

    tile = lambda w: pl.BlockSpec((TM, w), lambda i: (i, 0))
    const = lambda r, w: pl.BlockSpec((r, w), lambda i: (0, 0))
    return pl.pallas_call(
        body, name="merge_out_loss", grid=(nt,),
        in_specs=[tile(RET_W), tile(GLA_W), tile(W_M), tile(D_MODEL),
                  pl.BlockSpec((TM, D_MODEL), lambda i: (jnp.maximum(i - 1, 0), 0)),
                  const(RET_W, D_MODEL), const(GLA_W, D_MODEL), const(D_MODEL, D_MODEL), const(1, D_MODEL)],
        out_specs=[tile(D_MODEL), tile(RET_W), tile(GLA_W), tile(W_M), tile(D_MODEL), tile(D_MODEL), tile(D_MODEL),
                   tile(D_MODEL), const(1, 128), const(1, D_MODEL)],
        out_shape=[jax.ShapeDtypeStruct((tp, D_MODEL), F32), jax.ShapeDtypeStruct((tp, RET_W), F32),
                   jax.ShapeDtypeStruct((tp, GLA_W), F32), jax.ShapeDtypeStruct((tp, W_M), BF16),
                   jax.ShapeDtypeStruct((tp, D_MODEL), BF16), jax.ShapeDtypeStruct((tp, D_MODEL), BF16),
                   jax.ShapeDtypeStruct((tp, D_MODEL), BF16), jax.ShapeDtypeStruct((tp, D_MODEL), BF16),
                   jax.ShapeDtypeStruct((1, 128), F32), jax.ShapeDtypeStruct((1, D_MODEL), F32)],
        compiler_params=_cparams(1),
    )(a_ret, a_gla, mg, h0, tgt, wbr, wbg, wout, gf)


def _device_step(x2d, tgt2d, meta, norm_gain, w_in_bf, w_gate_up, b_gate, ret_gain, gla_gain, wbr, wbg, wout,
                 final_gain, ck):
    seq = x2d.shape[0]
    tp = T0 + seq
    head = jnp.concatenate([jnp.zeros((PADF, D_MODEL), F32), meta], axis=0)
    w_r = w_in_bf
    w_g = jnp.pad(w_in_bf[:, W_R:W_R + W_G], ((0, 0), (0, W_GP - W_G)))
    w_m = w_in_bf[:, W_R + W_G:]
    wg_pad = jnp.pad(w_gate_up, ((0, 128 - GATE_RANK), (0, 0))).astype(BF16)

    pos = jnp.arange(tp, dtype=F32) - PADF
    half = RET_QK // 2
    inv = ROPE_BASE ** (-jnp.arange(half, dtype=F32) / half)
    ang = pos[:, None] * inv[None, :]
    cos, sin = jnp.cos(ang), jnp.sin(ang)
    lgam = jnp.log1p(-(2.0 ** (-5.0 - jnp.arange(RET_HEADS, dtype=F32))))
    pmat = jnp.asarray(_gla_tril(), BF16)
    pmat_t = jnp.asarray(_gla_tril().T.copy(), BF16)

    h0, u = _rms_call(x2d, head, norm_gain)
    tab = pl.BlockSpec((TB, half), lambda j, i: (i, 0))
    rqk = _mm_nn("proj_rqk", u, w_r, BF16, D_MODEL, 0, 2 * D_MODEL, _rope_epilogue, (cos, sin), (tab, tab))
    rv = _mm_nn("proj_rv", u, w_r, BF16, D_MODEL, 2 * D_MODEL, RET_W)
    rg = _mm_nn("proj_rg", u, w_r, F32, D_MODEL, 4 * D_MODEL, RET_W)
    gqk = _mm_nn("proj_gqk", u, w_g, F32, 2 * GLA_KW, 0, 2 * GLA_KW, _gqk_epilogue)
    gv = _mm_nn("proj_gv", u, w_g, BF16, GLA_W, 2 * GLA_KW, GLA_W)
    gg = _mm_nn("proj_gg", u, w_g, F32, GLA_W, 2 * GLA_KW + GLA_W, GLA_W)
    glr = _mm_nn("proj_glr", u, w_g, F32, 128, 2 * GLA_KW + 2 * GLA_W, 128)
    mg = _mm_nn("proj_mg", u, w_m, F32, D_MODEL, 0, W_M)

    o_ret, a_ret, st_ret = _ret_fwd_call(rqk, rv, rg, ret_gain, lgam)
    o_gla, a_gla, st_gla = _gla_fwd_call(gqk, gv, glr, gg, wg_pad, b_gate, gla_gain, pmat)

    gf = final_gain.reshape(1, D_MODEL)
    (dh1, da_ret, da_gla, dm, merged_b, dh1_b, dpr_b, dpg_b, loss, dgf) = _mid_call(
        a_ret, a_gla, mg, h0, tgt2d, wbr, wbg, wout, gf)

    names_b = ("w_branch_ret", "w_branch_gla", "w_out")
    g2_b = [_mm_tn("dw_br", a_ret, dpr_b, D_MODEL).reshape(4, 2, RET_W // 8, D_MODEL).transpose(1, 0, 2, 3),
            _mm_tn("dw_bg", a_gla, dpg_b, D_MODEL).reshape(4, 2, GLA_W // 8, D_MODEL).transpose(1, 0, 2, 3),
            _mm_tn("dw_out", merged_b, dh1_b, D_MODEL).reshape(4, 2, D_MODEL // 8, D_MODEL).transpose(1, 0, 2, 3)]
    sib_b = _swap_halves_call("swap_halves_branch", g2_b)
    sum_b = [_add_half_call("add_half_" + nm, g, b, ck) for nm, g, b in zip(names_b, g2_b, sib_b)]
    d_g, dwg, dbg, dgla_gain, *chips_b = _gla_bwd_call(gqk, gv, glr, gg, o_gla, da_gla, st_gla, wg_pad, b_gate,
                                                       gla_gain, pmat, pmat_t, exchange=sum_b)
    mine = [_add_chips_call("add_chips_" + nm, g, b, p, ck) for nm, g, b, p in zip(names_b, g2_b, sib_b, chips_b)]

    d_r, dret_gain = _ret_bwd_call(rqk, rv, rg, o_ret, da_ret, st_ret, ret_gain, lgam, cos, sin)

    dw_in = jnp.concatenate([_mm_tn("dw_r", u, d_r, D_MODEL), _mm_tn("dw_g", u, d_g, 640)[:, :W_G],
                             _mm_tn("dw_m", u, dm, D_MODEL)], axis=1)
    g2_in = dw_in.reshape(2, D_MODEL // 2, 4, IN_COLS // 4).transpose(0, 2, 1, 3)
    (sib_in,) = _swap_halves_call("swap_halves_in", [g2_in])
    sum_in = _add_half_call("add_half_w_in", g2_in, sib_in, ck)

    du = _mm_nt_acc("du_m", dm, w_m, W_M)[0]
    du = _mm_nt_acc("du_g", d_g, w_g, W_GP, acc_in=du)[0]
    tile = pl.BlockSpec((TB, D_MODEL), lambda i, kk: (i, 0))
    row = pl.BlockSpec((1, D_MODEL), lambda i, kk: (0, 0))
    dh0, dnorm_gain, chips_in = _mm_nt_acc(
        "du_r", d_r, w_r, 2 * D_MODEL, acc_in=du, epilogue=_rms_bwd_epilogue, extras=(h0, norm_gain, dh1),
        extra_specs=(tile, row, tile),
        extra_out_shapes=(jax.ShapeDtypeStruct((tp, D_MODEL), F32), jax.ShapeDtypeStruct((1, D_MODEL), F32)),
        extra_out_specs=(tile, row), exchange=[sum_in])
    mine = [_add_chips_call("add_chips_w_in", g2_in, sib_in, chips_in, ck)] + mine
    full = _join_halves_call("join_halves", mine)

    return dict(loss=loss[0, 0], dx=dh0[T0:], dmeta=dh0[PADF:T0], norm_gain=dnorm_gain, w_gate_up=dwg[:GATE_RANK], b_gate=dbg,
                ret_norm_gain=dret_gain, gla_norm_gain=dgla_gain, final_norm_gain=dgf.reshape(D_MODEL),
                w_in=full[0], w_branch_ret=full[1], w_branch_gla=full[2], w_out=full[3])


MESH = pl.DeviceIdType.MESH
ANY = pl.BlockSpec(memory_space=pl.ANY)


def _place():
    return lax.axis_index("x"), lax.axis_index("y"), lax.axis_index("c")


def _gather8_call(name, parts):
    n = len(parts)

    def body(*refs):
        x_refs, out_refs = refs[:n], refs[n:2 * n]
        send_sems, recv_sems, local_sems = refs[2 * n:]
        x, y, c = _place()
        me, sibling = (x, y, c), (x, y, 1 - c)
        chips = [(1 - x, y), (x, 1 - y), (1 - x, 1 - y)]

        def slot(t, px, py, pc):
            return out_refs[t].at[4 * px + 2 * py + pc]

        def copy(t, k, block, to, src=None):
            return pltpu.make_async_remote_copy(
                src_ref=slot(t, *block) if src is None else src, dst_ref=slot(t, *block),
                send_sem=send_sems.at[7 * t + k], recv_sem=recv_sems.at[7 * t + k], device_id=to, device_id_type=MESH)

        mine = [pltpu.make_async_copy(x_refs[t], slot(t, *me), local_sems.at[t]) for t in range(n)]
        for cp in mine:
            cp.start()
        first = []
        for t in range(n):
            first.append(copy(t, 0, me, sibling, src=x_refs[t]))
            first += [copy(t, 1 + j, me, (*chip, c), src=x_refs[t]) for j, chip in enumerate(chips)]
        for cp in first:
            cp.start()
        passed = []
        for j, chip in enumerate(chips):
            for t in range(n):
                copy(t, 1 + j, (*chip, c), me).wait_recv()
                fwd = copy(t, 4 + j, (*chip, c), sibling)
                fwd.start()
                passed.append(fwd)
        for t in range(n):
            copy(t, 0, sibling, me).wait_recv()
            for j, chip in enumerate(chips):
                copy(t, 4 + j, (*chip, 1 - c), me).wait_recv()
        for cp in first + passed:
            cp.wait_send()
        for cp in mine:
            cp.wait()

    return pl.pallas_call(
        body, name=name,
        out_shape=[jax.ShapeDtypeStruct((8,) + p.shape, p.dtype) for p in parts],
        in_specs=[ANY] * n, out_specs=[ANY] * n,
        scratch_shapes=[pltpu.SemaphoreType.DMA((7 * n,)), pltpu.SemaphoreType.DMA((7 * n,)),
                        pltpu.SemaphoreType.DMA((n,))],
    )(*parts)


def _swap_halves_call(name, gs):
    n = len(gs)

    def body(*refs):
        g_refs, b_refs = refs[:n], refs[n:2 * n]
        send_sems, recv_sems = refs[2 * n:]
        x, y, c = _place()
        copies = [pltpu.make_async_remote_copy(
            src_ref=g_refs[t].at[1 - c], dst_ref=b_refs[t], send_sem=send_sems.at[t], recv_sem=recv_sems.at[t],
            device_id=(x, y, 1 - c), device_id_type=MESH) for t in range(n)]
        for cp in copies:
            cp.start()
        for cp in copies:
            cp.wait()

    return pl.pallas_call(
        body, name=name,
        out_shape=[jax.ShapeDtypeStruct(g.shape[1:], g.dtype) for g in gs],
        in_specs=[ANY] * n, out_specs=[ANY] * n,
        scratch_shapes=[pltpu.SemaphoreType.DMA((n,)), pltpu.SemaphoreType.DMA((n,))],
    )(*gs)


def _join_halves_call(name, ts):
    n = len(ts)

    def body(*refs):
        o_refs = refs[n:2 * n]
        send_sems, recv_sems = refs[2 * n:]
        x, y, c = _place()
        copies = [pltpu.make_async_remote_copy(
            src_ref=o_refs[t].at[c], dst_ref=o_refs[t].at[c], send_sem=send_sems.at[t], recv_sem=recv_sems.at[t],
            device_id=(x, y, 1 - c), device_id_type=MESH) for t in range(n)]
        for cp in copies:
            cp.start()
        for t in range(n):
            copies[t].wait_send()
            pltpu.make_async_remote_copy(
                src_ref=o_refs[t].at[c], dst_ref=o_refs[t].at[1 - c], send_sem=send_sems.at[t],
                recv_sem=recv_sems.at[t], device_id=(x, y, 1 - c), device_id_type=MESH).wait_recv()

    return pl.pallas_call(
        body, name=name,
        out_shape=[jax.ShapeDtypeStruct(t.shape, t.dtype) for t in ts],
        in_specs=[ANY] * n, out_specs=[ANY] * n, input_output_aliases={t: t for t in range(n)},
        scratch_shapes=[pltpu.SemaphoreType.DMA((n,)), pltpu.SemaphoreType.DMA((n,))],
    )(*ts)


def _row_block(rows, cols, budget):
    best = 8
    for rb in range(8, rows + 1, 8):
        if rows % rb == 0 and rb * cols * 4 <= budget:
            best = rb
    return best


def _add_half_call(name, g, b, ck):
    _, _, r, cc = g.shape
    rb = _row_block(r, cc, 2 * 1024 * 1024)

    def body(ck_ref, g_ref, b_ref, o_ref):
        o_ref[...] = (g_ref[...] + b_ref[...]).astype(BF16)

    return pl.pallas_call(
        body, name=name,
        grid_spec=pltpu.PrefetchScalarGridSpec(
            num_scalar_prefetch=1, grid=(4, r // rb),
            in_specs=[pl.BlockSpec((None, None, rb, cc), lambda k, i, ck_ref: (ck_ref[0], k, i, 0)),
                      pl.BlockSpec((None, rb, cc), lambda k, i, ck_ref: (k, i, 0))],
            out_specs=pl.BlockSpec((None, rb, cc), lambda k, i, ck_ref: (k, i, 0))),
        out_shape=jax.ShapeDtypeStruct(b.shape, BF16),
        compiler_params=_cparams(2),
    )(ck, g, b)


def _add_chips_call(name, g, b, p, ck):
    _, _, r, cc = g.shape
    rb = _row_block(r, cc, 2 * 1024 * 1024)

    def body(ck_ref, g_ref, b_ref, p0_ref, p1_ref, p2_ref, o_ref):
        own = g_ref[...] + b_ref[...]
        o_ref[...] = ((own + p0_ref[...].astype(F32)) + p1_ref[...].astype(F32)) + p2_ref[...].astype(F32)

    def peer(j):
        return pl.BlockSpec((None, rb, cc), lambda i, ck_ref: (j, i, 0))

    return pl.pallas_call(
        body, name=name,
        grid_spec=pltpu.PrefetchScalarGridSpec(
            num_scalar_prefetch=1, grid=(r // rb,),
            in_specs=[pl.BlockSpec((None, None, rb, cc), lambda i, ck_ref: (ck_ref[0], ck_ref[1], i, 0)),
                      pl.BlockSpec((None, rb, cc), lambda i, ck_ref: (ck_ref[1], i, 0)),
                      peer(0), peer(1), peer(2)],
            out_specs=pl.BlockSpec((None, rb, cc), lambda i, ck_ref: (ck_ref[0], i, 0))),
        out_shape=jax.ShapeDtypeStruct((2, r, cc), F32),
        compiler_params=_cparams(1),
    )(ck, g, b, p, p, p)


def _sum8_call(name, g):
    def body(g_ref, o_ref):
        acc = g_ref[0]
        for d in range(1, 8):
            acc = acc + g_ref[d]
        o_ref[...] = acc

    return pl.pallas_call(body, name=name, out_shape=jax.ShapeDtypeStruct(g.shape[1:], F32))(g)


def _adamw_call(name, w, g, m, v):
    r, cc = w.shape
    if r % 8 == 0 or r * cc * 4 <= 1024 * 1024:
        rb = _row_block(r, cc, 1024 * 1024) if r % 8 == 0 else r
        grid, spec = (r // rb,), pl.BlockSpec((rb, cc), lambda i: (i, 0))
    else:
        grid, spec = (cc // 128,), pl.BlockSpec((r, 128), lambda i: (0, i))

    def body(w_ref, g_ref, m_ref, v_ref, d_ref, m2_ref, v2_ref):
        gv = g_ref[...]
        m2 = ADAM_B1 * m_ref[...] + (1.0 - ADAM_B1) * gv
        v2 = ADAM_B2 * v_ref[...] + (1.0 - ADAM_B2) * (gv * gv)
        m_hat = m2 / (1.0 - ADAM_B1 ** ADAM_STEP)
        v_hat = v2 / (1.0 - ADAM_B2 ** ADAM_STEP)
        d_ref[...] = -ADAM_LR * (m_hat / (jnp.sqrt(v_hat) + ADAM_EPS) + ADAM_WD * w_ref[...])
        m2_ref[...] = m2
        v2_ref[...] = v2

    return pl.pallas_call(
        body, name=name, grid=grid, in_specs=[spec] * 4, out_specs=[spec] * 3,
        out_shape=[jax.ShapeDtypeStruct((r, cc), F32)] * 3, compiler_params=_cparams(1),
    )(w, g, m, v)


SMALL = (("norm_gain", D_MODEL), ("b_gate", GLA_KW), ("ret_norm_gain", RET_W), ("gla_norm_gain", GLA_W),
         ("final_norm_gain", D_MODEL), ("w_gate_up", GATE_RANK * GLA_KW), ("meta_tokens", N_META * D_MODEL))


def _pack_rows(vecs, rows):
    flat = jnp.concatenate([v.reshape(-1) for v in vecs])
    return jnp.pad(flat, (0, rows * 128 - flat.shape[0])).reshape(rows, 128)


def kernel(x, meta_tokens, norm_gain, w_in, w_gate_up, b_gate, ret_norm_gain, gla_norm_gain, w_branch_ret, w_branch_gla, w_out, final_norm_gain, loss_target, m_meta_tokens, m_norm_gain, m_w_in, m_w_gate_up, m_b_gate, m_ret_norm_gain, m_gla_norm_gain, m_w_branch_ret, m_w_branch_gla, m_w_out, m_final_norm_gain, v_meta_tokens, v_norm_gain, v_w_in, v_w_gate_up, v_b_gate, v_ret_norm_gain, v_gla_norm_gain, v_w_branch_ret, v_w_branch_gla, v_w_out, v_final_norm_gain):
    xi, yi, ci = _place()
    kme = 2 * xi + yi
    ck = jnp.stack([ci, kme]).astype(jnp.int32)
    sw_in = w_in.shape[2]

    def my_half(a, dtype):
        r, cc = a.shape
        return lax.dynamic_index_in_dim(a.reshape(2, r // 2, cc), ci, 0, keepdims=False).astype(dtype)

    g_in, g_br, g_bg, g_out, g_meta, g_wg = _gather8_call(
        "gather_weights",
        [my_half(w_in[0], BF16), my_half(w_branch_ret[0], BF16), my_half(w_branch_gla[0], BF16),
         my_half(w_out[0], BF16), my_half(meta_tokens, F32), my_half(w_gate_up[0], F32)])
    w_in_bf = g_in.reshape(4, 2, D_MODEL // 2, sw_in).transpose(1, 2, 0, 3).reshape(D_MODEL, 4 * sw_in)
    wbr = g_br.reshape(RET_W, D_MODEL)
    wbg = g_bg.reshape(GLA_W, D_MODEL)
    wout = g_out.reshape(D_MODEL, D_MODEL)
    meta = g_meta.reshape(4, 2, N_META // 2, D_MODEL // 4).transpose(1, 2, 0, 3).reshape(N_META, D_MODEL)
    wg_full = g_wg.reshape(4, 2, GATE_RANK // 2, GLA_KW // 4).transpose(1, 2, 0, 3).reshape(GATE_RANK, GLA_KW)

    loc = _device_step(x[0], loss_target[0], meta, norm_gain, w_in_bf, wg_full, b_gate, ret_norm_gain, gla_norm_gain,
                       wbr, wbg, wout, final_norm_gain, ck)
    loss = lax.psum(loc["loss"], ("x", "y", "c"))
    names = ("w_in", "w_branch_ret", "w_branch_gla", "w_out")
    full = [loc[nm] for nm in names]
    big_w = dict(w_in=w_in[0], w_branch_ret=w_branch_ret[0], w_branch_gla=w_branch_gla[0], w_out=w_out[0])
    big_m = dict(w_in=m_w_in[0], w_branch_ret=m_w_branch_ret[0], w_branch_gla=m_w_branch_gla[0], w_out=m_w_out[0])
    big_v = dict(w_in=v_w_in[0], w_branch_ret=v_w_branch_ret[0], w_branch_gla=v_w_branch_gla[0], w_out=v_w_out[0])
    grads, deltas, new_m, new_v = {}, {}, {}, {}
    for nm, f in zip(names, full):
        shape = big_w[nm].shape
        g = f.reshape(shape)
        if nm == "w_in":
            d, m2, v2 = (a.T for a in _adamw_call("adamw_" + nm, big_w[nm].T, g.T, big_m[nm].T, big_v[nm].T))
        else:
            d, m2, v2 = _adamw_call("adamw_" + nm, big_w[nm], g, big_m[nm], big_v[nm])
        grads[nm], deltas[nm], new_m[nm], new_v[nm] = (a.reshape((1,) + shape) for a in (g, d, m2, v2))

    small_g = dict(loc)
    small_g["meta_tokens"] = loc["dmeta"]
    n_small = sum(sz for _, sz in SMALL)
    rows = -(-n_small // 128 // 8) * 8
    (g_small,) = _gather8_call("gather_small_grads", [_pack_rows([small_g[nm] for nm, _ in SMALL], rows)])
    tot = _sum8_call("sum_small_grads", g_small).reshape(-1)
    off = 0
    sg = {}
    for nm, sz in SMALL:
        sg[nm] = tot[off:off + sz]
        off += sz
    sg["w_gate_up"] = lax.dynamic_slice_in_dim(sg["w_gate_up"].reshape(GATE_RANK, GLA_KW), kme * (GLA_KW // 4),
                                               GLA_KW // 4, axis=1)
    sg["meta_tokens"] = lax.dynamic_slice_in_dim(sg["meta_tokens"].reshape(N_META, D_MODEL), kme * (D_MODEL // 4),
                                                 D_MODEL // 4, axis=1)
    small_w = dict(norm_gain=norm_gain, b_gate=b_gate, ret_norm_gain=ret_norm_gain, gla_norm_gain=gla_norm_gain,
                   final_norm_gain=final_norm_gain, w_gate_up=w_gate_up, meta_tokens=meta_tokens)
    small_m = dict(norm_gain=m_norm_gain, b_gate=m_b_gate, ret_norm_gain=m_ret_norm_gain,
                   gla_norm_gain=m_gla_norm_gain, final_norm_gain=m_final_norm_gain, w_gate_up=m_w_gate_up,
                   meta_tokens=m_meta_tokens)
    small_v = dict(norm_gain=v_norm_gain, b_gate=v_b_gate, ret_norm_gain=v_ret_norm_gain,
                   gla_norm_gain=v_gla_norm_gain, final_norm_gain=v_final_norm_gain, w_gate_up=v_w_gate_up,
                   meta_tokens=v_meta_tokens)
    order = [nm for nm, _ in SMALL]
    sizes = [small_w[nm].size for nm in order]
    prow = -(-sum(sizes) // 128 // 8) * 8
    pk = lambda d: _pack_rows([d[nm] for nm in order], prow)
    d_s, m_s, v_s = _adamw_call("adamw_small", pk(small_w), pk(sg), pk(small_m), pk(small_v))
    off = 0
    for nm, sz in zip(order, sizes):
        shape = small_w[nm].shape
        grads[nm] = sg[nm].reshape(shape)
        deltas[nm], new_m[nm], new_v[nm] = (a.reshape(-1)[off:off + sz].reshape(shape) for a in (d_s, m_s, v_s))
        off += sz

    out_order = ("meta_tokens", "norm_gain", "w_in", "w_gate_up", "b_gate", "ret_norm_gain", "gla_norm_gain",
                 "w_branch_ret", "w_branch_gla", "w_out", "final_norm_gain")
    dx = loc["dx"].reshape(x.shape)
    return (loss, dx, *[grads[nm] for nm in out_order], *[deltas[nm] for nm in out_order],
            *[new_m[nm] for nm in out_order], *[new_v[nm] for nm in out_order])
```

```python
import math
from typing import Callable, NamedTuple

import numpy as np
import jax
import jax.numpy as jnp
from jax import lax
from jax.experimental import pallas as pl
from jax.experimental.pallas import tpu as pltpu

F32 = jnp.float32
BF16 = jnp.bfloat16

D_MODEL = 1024
N_META = 16
EPS = 1e-6
ROPE_BASE = 10000.0
RET_HEADS, RET_QK, RET_V = 4, 256, 512
RET_W = RET_HEADS * RET_V
GLA_HEADS, GLA_K, GLA_V = 4, 128, 256
GLA_W = GLA_HEADS * GLA_V
GLA_KW = GLA_HEADS * GLA_K
GATE_RANK = 16
GATE_TAU = 16.0
GLA_SUB = 16

TM = 256
T0 = TM
PADF = T0 - N_META
GC = 128
TB = 768
TK = 768

W_R = 6144
W_G = 3088
W_GP = 3200
W_M = 2048
IN_COLS = W_R + W_G + W_M
WIN_STEP = (IN_COLS // 4) // 128 * 128
WIN_W = -(-(3 * (IN_COLS // 4 - WIN_STEP) + IN_COLS // 4) // 128) * 128
IN_PAD = 3 * WIN_STEP + WIN_W

ADAM_LR, ADAM_B1, ADAM_B2, ADAM_EPS, ADAM_WD, ADAM_STEP = 0.001, 0.9, 0.999, 1e-08, 0.01, 10

VMEM_LIMIT = 56 * 1024 * 1024

NN = ((1,), (0,))
NT = ((1,), (1,))
TN = ((0,), (0,))


def _dot(a, b, dims):
    return lax.dot_general(a, b, (dims, ((), ())), preferred_element_type=F32)


def _cparams(n_axes):
    return pltpu.CompilerParams(dimension_semantics=("arbitrary",) * n_axes, vmem_limit_bytes=VMEM_LIMIT)


def _sigmoid(x):
    return 1.0 / (1.0 + jnp.exp(-x))


def _split3(x):
    hi = x.astype(BF16)
    r1 = x - hi.astype(F32)
    mid = r1.astype(BF16)
    lo = (r1 - mid.astype(F32)).astype(BF16)
    return hi, mid, lo


def _exact_pm(p, x):
    hi, mid, lo = _split3(x)
    return _dot(p, hi, NN) + _dot(p, mid, NN) + _dot(p, lo, NN)


def _rms_call(x2d, head, gain):
    tp = T0 + x2d.shape[0]

    def body(x_ref, hd_ref, g_ref, h_ref, u_ref):
        h = jnp.where(pl.program_id(0) == 0, hd_ref[...], x_ref[...])
        h_ref[...] = h
        r = lax.rsqrt(jnp.mean(h * h, axis=-1, keepdims=True) + EPS)
        u_ref[...] = (h * r * g_ref[...]).astype(BF16)

    tile = pl.BlockSpec((TM, D_MODEL), lambda i: (i, 0))
    return pl.pallas_call(
        body, name="rms_in", grid=(tp // TM,),
        in_specs=[pl.BlockSpec((TM, D_MODEL), lambda i: (jnp.maximum(i - 1, 0), 0)),
                  pl.BlockSpec((T0, D_MODEL), lambda i: (0, 0)), pl.BlockSpec((1, D_MODEL), lambda i: (0, 0))],
        out_specs=[tile, tile],
        out_shape=[jax.ShapeDtypeStruct((tp, D_MODEL), F32), jax.ShapeDtypeStruct((tp, D_MODEL), BF16)],
        compiler_params=_cparams(1),
    )(x2d, head, gain)


def _mm_nn(name, a, b, out_dtype, tn, col0, ncols, epilogue=None, extras=(), extra_specs=()):
    m, k = a.shape
    nj, j0 = ncols // tn, col0 // tn

    def body(a_ref, b_ref, *rest):
        *ex, o_ref = rest
        acc = _dot(a_ref[...], b_ref[...], NN)
        if epilogue is None:
            o_ref[...] = acc.astype(out_dtype)
        else:
            epilogue(acc, o_ref, *ex)

    return pl.pallas_call(
        body, name=name, grid=(nj, m // TB),
        in_specs=[pl.BlockSpec((TB, k), lambda j, i: (i, 0)), pl.BlockSpec((k, tn), lambda j, i: (0, j0 + j))]
        + list(extra_specs),
        out_specs=pl.BlockSpec((TB, tn), lambda j, i: (i, j)),
        out_shape=jax.ShapeDtypeStruct((m, ncols), out_dtype),
        compiler_params=_cparams(2),
    )(a, b, *extras)


def _rope_epilogue(acc, o_ref, cos_ref, sin_ref):
    scale = jnp.where(pl.program_id(0) == 1, RET_QK ** -0.5, 1.0).astype(F32)
    cos, sin = cos_ref[...], sin_ref[...]
    half = RET_QK // 2
    for h in range(RET_HEADS):
        t1 = acc[:, h * RET_QK:h * RET_QK + half]
        t2 = acc[:, h * RET_QK + half:(h + 1) * RET_QK]
        o_ref[:, h * RET_QK:h * RET_QK + half] = ((t1 * cos - t2 * sin) * scale).astype(BF16)
        o_ref[:, h * RET_QK + half:(h + 1) * RET_QK] = ((t2 * cos + t1 * sin) * scale).astype(BF16)


def _gqk_epilogue(acc, o_ref):
    o_ref[:, :GLA_KW] = acc[:, :GLA_KW] * (GLA_K ** -0.5)
    o_ref[:, GLA_KW:] = acc[:, GLA_KW:]


class _Comm(NamedTuple):
    srcs: tuple
    out_shapes: tuple
    n_sems: int
    make: Callable


def _comm_sems(comm):
    return [pltpu.SemaphoreType.DMA((comm.n_sems,)), pltpu.SemaphoreType.DMA((comm.n_sems,))]


def _other_chips(x, y):
    return [(1 - x, y), (x, 1 - y), (1 - x, 1 - y)]


def _exchange_plan(ss):
    def make(s_refs, b_refs, send_sems, recv_sems):
        x, y, c = _place()
        return [pltpu.make_async_remote_copy(
            src_ref=s_refs[t].at[2 * chip[0] + chip[1]], dst_ref=b_refs[t].at[j], send_sem=send_sems.at[3 * t + j],
            recv_sem=recv_sems.at[3 * t + j], device_id=(*chip, c), device_id_type=MESH)
            for t in range(len(s_refs)) for j, chip in enumerate(_other_chips(x, y))]

    return _Comm(tuple(ss), tuple(jax.ShapeDtypeStruct((3,) + s.shape[1:], s.dtype) for s in ss), 3 * len(ss), make)


def _exchange_window_plan(s):
    def make(s_refs, b_refs, send_sems, recv_sems):
        x, y, c = _place()
        return [pltpu.make_async_remote_copy(
            src_ref=s_refs[0].at[:, pl.ds(pl.multiple_of((2 * chip[0] + chip[1]) * WIN_STEP, 128), WIN_W)],
            dst_ref=b_refs[0].at[j], send_sem=send_sems.at[j], recv_sem=recv_sems.at[j], device_id=(*chip, c),
            device_id_type=MESH) for j, chip in enumerate(_other_chips(x, y))]

    return _Comm((s,), (jax.ShapeDtypeStruct((3, s.shape[0], WIN_W), s.dtype),), 3, make)


def _swap_plan(gs):
    def make(g_refs, b_refs, send_sems, recv_sems):
        x, y, c = _place()
        return [pltpu.make_async_remote_copy(
            src_ref=g_refs[t].at[1 - c], dst_ref=b_refs[t], send_sem=send_sems.at[t], recv_sem=recv_sems.at[t],
            device_id=(x, y, 1 - c), device_id_type=MESH) for t in range(len(g_refs))]

    return _Comm(tuple(gs), tuple(jax.ShapeDtypeStruct(g.shape[1:], g.dtype) for g in gs), len(gs), make)


def _spread_plan(parts):
    def make(p_refs, o_refs, send_sems, recv_sems):
        x, y, c = _place()
        copies = []
        for t in range(len(p_refs)):
            mine = o_refs[t].at[4 * x + 2 * y + c]
            copies.append(pltpu.make_async_copy(p_refs[t], mine, send_sems.at[7 * len(p_refs) + t]))
            for r in range(1, 8):
                peer = (1 - x if r & 4 else x, 1 - y if r & 2 else y, 1 - c if r & 1 else c)
                copies.append(pltpu.make_async_remote_copy(
                    src_ref=p_refs[t], dst_ref=mine, send_sem=send_sems.at[7 * t + r - 1],
                    recv_sem=recv_sems.at[7 * t + r - 1], device_id=peer, device_id_type=MESH))
        return copies

    return _Comm(tuple(parts), tuple(jax.ShapeDtypeStruct((8,) + p.shape, p.dtype) for p in parts), 8 * len(parts),
                 make)


def _mm_nt_acc(name, a, w, tk, acc_in=None, epilogue=None, extras=(), extra_specs=(), extra_out_shapes=(),
               extra_out_specs=(), comm=None):
    m, k = a.shape
    n = w.shape[0]
    nk, ni = k // tk, m // TB
    has_acc = acc_in is not None
    n_xc = len(comm.srcs) if comm else 0

    def body(*refs):
        a_ref, w_ref = refs[0], refs[1]
        pos = 2
        acc_ref = None
        if has_acc:
            acc_ref = refs[pos]
            pos += 1
        ex = refs[pos:pos + len(extras)]
        pos += len(extras)
        xc_src = refs[pos:pos + n_xc]
        pos += n_xc
        n_scr = 3 if n_xc else 1
        outs = refs[pos:len(refs) - n_scr - n_xc]
        xc_dst = refs[len(refs) - n_scr - n_xc:len(refs) - n_scr]
        scr = refs[len(refs) - n_scr]
        i, kk = pl.program_id(0), pl.program_id(1)
        if n_xc:
            copies = comm.make(xc_src, xc_dst, refs[-2], refs[-1])

            @pl.when((i == 0) & (kk == 0))
            def _():
                for cp in copies:
                    cp.start()

        @pl.when(kk == 0)
        def _():
            scr[...] = acc_ref[...] if has_acc else jnp.zeros_like(scr)

        scr[...] += _dot(a_ref[...], w_ref[...], NT)

        @pl.when(kk == nk - 1)
        def _():
            if epilogue is None:
                outs[0][...] = scr[...]
            else:
                epilogue(scr[...], outs, i, *ex)

        if n_xc:
            @pl.when((i == ni - 1) & (kk == nk - 1))
            def _():
                for cp in copies:
                    cp.wait()

    in_specs = [pl.BlockSpec((TB, tk), lambda i, kk: (i, kk)), pl.BlockSpec((n, tk), lambda i, kk: (0, kk))]
    args = [a, w]
    if has_acc:
        in_specs.append(pl.BlockSpec((TB, n), lambda i, kk: (i, 0)))
        args.append(acc_in)
    in_specs += list(extra_specs) + [ANY] * n_xc
    args += list(extras) + (list(comm.srcs) if comm else [])
    if epilogue is None:
        out_shape = [jax.ShapeDtypeStruct((m, n), F32)]
        out_specs = [pl.BlockSpec((TB, n), lambda i, kk: (i, 0))]
    else:
        out_shape, out_specs = list(extra_out_shapes), list(extra_out_specs)
    scratch = [pltpu.VMEM((TB, n), F32)]
    if n_xc:
        out_shape += list(comm.out_shapes)
        out_specs += [ANY] * n_xc
        scratch += _comm_sems(comm)
    return pl.pallas_call(
        body, name=name, grid=(ni, nk), in_specs=in_specs, out_specs=out_specs, out_shape=out_shape,
        scratch_shapes=scratch, compiler_params=_cparams(2),
    )(*args)


def _rms_bwd_epilogue(du, outs, i, h_ref, g_ref, dh1_ref):
    dh0_ref, dg_ref = outs
    h = h_ref[...]
    r = lax.rsqrt(jnp.mean(h * h, axis=-1, keepdims=True) + EPS)
    xh = h * r
    dxh = du * g_ref[...]
    dh0_ref[...] = dh1_ref[...] + r * (dxh - xh * jnp.mean(dxh * xh, axis=-1, keepdims=True))

    @pl.when(i == 0)
    def _():
        dg_ref[...] = jnp.zeros_like(dg_ref)

    dg_ref[...] += jnp.sum(du * xh, axis=0, keepdims=True)


def _mm_tn(name, a, b, bn, ncols=None, bcol0=0, into=None, col0=0, out_cols=None):
    t, m = a.shape
    n = ncols or b.shape[1]
    j0, bj0 = col0 // bn, bcol0 // bn

    def body(a_ref, b_ref, *rest):
        o_ref = rest[-1]

        @pl.when(pl.program_id(1) == 0)
        def _():
            o_ref[...] = jnp.zeros_like(o_ref)

        o_ref[...] += _dot(a_ref[...], b_ref[...], TN)

    in_specs = [pl.BlockSpec((TK, m), lambda j, kk: (kk, 0)), pl.BlockSpec((TK, bn), lambda j, kk: (kk, bj0 + j))]
    args = [a, b]
    aliases = {}
    if into is not None:
        in_specs.append(ANY)
        args.append(into)
        aliases = {2: 0}
        out_cols = into.shape[1]
    return pl.pallas_call(
        body, name=name, grid=(n // bn, t // TK), in_specs=in_specs,
        out_specs=pl.BlockSpec((m, bn), lambda j, kk: (0, j0 + j)),
        out_shape=jax.ShapeDtypeStruct((m, out_cols or n), F32), input_output_aliases=aliases,
        compiler_params=_cparams(2),
    )(*args)


def _ret_fill_decay(lg_ref, dm_scr):
    c = TM
    ii = lax.broadcasted_iota(jnp.int32, (c, c), 0)
    jj = lax.broadcasted_iota(jnp.int32, (c, c), 1)
    rel = (ii - jj).astype(F32)
    for h in range(RET_HEADS):
        dm_scr[h] = jnp.where(rel >= 0, jnp.exp(jnp.maximum(rel, 0.0) * lg_ref[h]), 0.0)


def _ret_consts(lg, dm_ref):
    c = TM
    idx = lax.broadcasted_iota(jnp.int32, (c, 1), 0).astype(F32)
    xi = jnp.exp((idx + 1.0) * lg)
    zeta = jnp.exp((c - 1.0 - idx) * lg)
    gc = jnp.exp(jnp.full((1, 1), c, F32) * lg)
    return dm_ref[...], xi, zeta, gc


def _ret_fwd_call(rqk, rv, rg, gain, lgam):
    tp = rqk.shape[0]
    nc = tp // TM

    def body(lg_ref, qk_ref, v_ref, rg_ref, g_ref, o_ref, a_ref, st_ref, s_scr, dm_scr):
        @pl.when(pl.program_id(0) == 0)
        def _():
            s_scr[...] = jnp.zeros_like(s_scr)
            _ret_fill_decay(lg_ref, dm_scr)

        for h in range(RET_HEADS):
            dm, xi, zeta, gc = _ret_consts(lg_ref[h], dm_scr.at[h])
            q = qk_ref[:, h * RET_QK:(h + 1) * RET_QK]
            k = qk_ref[:, D_MODEL + h * RET_QK:D_MODEL + (h + 1) * RET_QK]
            v = v_ref[:, h * RET_V:(h + 1) * RET_V]
            sb = s_scr[h].astype(BF16)
            st_ref[0, h] = sb
            s = _dot(q, k, NT) * dm
            o = _dot(s.astype(BF16), v, NN) + xi * _dot(q, sb, NN)
            kz = (k.astype(F32) * zeta).astype(BF16)
            s_scr[h] = gc * s_scr[h] + _dot(kz, v, TN)
            o_ref[:, h * RET_V:(h + 1) * RET_V] = o
            mu = jnp.mean(o, axis=-1, keepdims=True)
            xc = o - mu
            xh = xc * lax.rsqrt(jnp.mean(xc * xc, axis=-1, keepdims=True) + EPS)
            g = rg_ref[:, h * RET_V:(h + 1) * RET_V]
            a_ref[:, h * RET_V:(h + 1) * RET_V] = (
                xh * g_ref[:, h * RET_V:(h + 1) * RET_V] * (g * _sigmoid(g))).astype(BF16)

    return pl.pallas_call(
        body, name="ret_fwd", grid=(nc,),
        in_specs=[pl.BlockSpec(memory_space=pltpu.SMEM),
                  pl.BlockSpec((TM, 2 * D_MODEL), lambda n: (n, 0)),
                  pl.BlockSpec((TM, RET_W), lambda n: (n, 0)),
                  pl.BlockSpec((TM, RET_W), lambda n: (n, 0)),
                  pl.BlockSpec((1, RET_W), lambda n: (0, 0))],
        out_specs=[pl.BlockSpec((TM, RET_W), lambda n: (n, 0)),
                   pl.BlockSpec((TM, RET_W), lambda n: (n, 0)),
                   pl.BlockSpec((1, RET_HEADS, RET_QK, RET_V), lambda n: (n, 0, 0, 0))],
        out_shape=[jax.ShapeDtypeStruct((tp, RET_W), F32), jax.ShapeDtypeStruct((tp, RET_W), BF16),
                   jax.ShapeDtypeStruct((nc, RET_HEADS, RET_QK, RET_V), BF16)],
        scratch_shapes=[pltpu.VMEM((RET_HEADS, RET_QK, RET_V), F32), pltpu.VMEM((RET_HEADS, TM, TM), F32)],
        compiler_params=_cparams(1),
    )(lgam, rqk, rv, rg, gain)


def _ret_bwd_call(rqk, rv, rg, o_ret, da, states, gain, lgam, cos, sin):
    tp = rqk.shape[0]
    nc = tp // TM
    half = RET_QK // 2

    def body(lg_ref, qk_ref, v_ref, rg_ref, o_ref, da_ref, st_ref, g_ref, cos_ref, sin_ref, dp_ref, dg_ref, ds_scr,
             dm_scr):
        @pl.when(pl.program_id(0) == 0)
        def _():
            ds_scr[...] = jnp.zeros_like(ds_scr)
            dg_ref[...] = jnp.zeros_like(dg_ref)
            _ret_fill_decay(lg_ref, dm_scr)

        cos, sin = cos_ref[...], sin_ref[...]
        for h in range(RET_HEADS):
            hs = slice(h * RET_V, (h + 1) * RET_V)
            dm, xi, zeta, gc = _ret_consts(lg_ref[h], dm_scr.at[h])
            o = o_ref[:, hs]
            mu = jnp.mean(o, axis=-1, keepdims=True)
            xc = o - mu
            rstd = lax.rsqrt(jnp.mean(xc * xc, axis=-1, keepdims=True) + EPS)
            xh = xc * rstd
            gain_h = g_ref[:, hs]
            g = rg_ref[:, hs]
            sg = _sigmoid(g)
            silu = g * sg
            dah = da_ref[:, hs]
            dp_ref[:, 4 * D_MODEL + h * RET_V:4 * D_MODEL + (h + 1) * RET_V] = (
                dah * (xh * gain_h) * (sg * (1.0 + g * (1.0 - sg)))).astype(BF16)
            dn = dah * silu
            dg_ref[:, hs] += jnp.sum(dn * xh, axis=0, keepdims=True)
            dxh = dn * gain_h
            do = rstd * (dxh - jnp.mean(dxh, axis=-1, keepdims=True)
                         - xh * jnp.mean(dxh * xh, axis=-1, keepdims=True))
            dob = do.astype(BF16)
            q = qk_ref[:, h * RET_QK:(h + 1) * RET_QK]
            k = qk_ref[:, D_MODEL + h * RET_QK:D_MODEL + (h + 1) * RET_QK]
            v = v_ref[:, hs]
            sp = st_ref[0, h]
            ds = ds_scr[h]
            dsb = ds.astype(BF16)
            s = (_dot(q, k, NT) * dm).astype(BF16)
            dsc = (_dot(dob, v, NT) * dm).astype(BF16)
            dq = _dot(dsc, k, NN) + xi * _dot(dob, sp, NT)
            dk = _dot(dsc, q, TN) + zeta * _dot(v, dsb, NT)
            kz = (k.astype(F32) * zeta).astype(BF16)
            dv = _dot(s, dob, TN) + _dot(kz, dsb, NN)
            qx = (q.astype(F32) * xi).astype(BF16)
            ds_scr[h] = gc * ds + _dot(qx, dob, TN)
            dp_ref[:, 2 * D_MODEL + h * RET_V:2 * D_MODEL + (h + 1) * RET_V] = dv.astype(BF16)
            dk = dk * (RET_QK ** -0.5)
            for base, t in ((0, dq), (D_MODEL, dk)):
                t1, t2 = t[:, :half], t[:, half:]
                dp_ref[:, base + h * RET_QK:base + h * RET_QK + half] = (t1 * cos + t2 * sin).astype(BF16)
                dp_ref[:, base + h * RET_QK + half:base + (h + 1) * RET_QK] = (t2 * cos - t1 * sin).astype(BF16)

    rev = lambda n: (nc - 1 - n, 0)
    return pl.pallas_call(
        body, name="ret_bwd", grid=(nc,),
        in_specs=[pl.BlockSpec(memory_space=pltpu.SMEM),
                  pl.BlockSpec((TM, 2 * D_MODEL), rev),
                  pl.BlockSpec((TM, RET_W), rev),
                  pl.BlockSpec((TM, RET_W), rev),
                  pl.BlockSpec((TM, RET_W), rev),
                  pl.BlockSpec((TM, RET_W), rev),
                  pl.BlockSpec((1, RET_HEADS, RET_QK, RET_V), lambda n: (nc - 1 - n, 0, 0, 0)),
                  pl.BlockSpec((1, RET_W), lambda n: (0, 0)),
                  pl.BlockSpec((TM, half), rev),
                  pl.BlockSpec((TM, half), rev)],
        out_specs=[pl.BlockSpec((TM, W_R), rev), pl.BlockSpec((1, RET_W), lambda n: (0, 0))],
        out_shape=[jax.ShapeDtypeStruct((tp, W_R), BF16), jax.ShapeDtypeStruct((1, RET_W), F32)],
        scratch_shapes=[pltpu.VMEM((RET_HEADS, RET_QK, RET_V), F32), pltpu.VMEM((RET_HEADS, TM, TM), F32)],
        compiler_params=_cparams(1),
    )(lgam, rqk, rv, rg, o_ret, da, states, gain, cos, sin)


GLA_LEVELS = tuple(GC >> (s + 1) for s in range(int(math.log2(GC // GLA_SUB))))
NLEV = len(GLA_LEVELS)


def _gla_tril():
    return np.tril(np.ones((GC, GC), np.float32))


def _gla_masks():
    ii = lax.broadcasted_iota(jnp.int32, (GC, GC), 0)
    jj = lax.broadcasted_iota(jnp.int32, (GC, GC), 1)
    masks = []
    for m in GLA_LEVELS:
        sh = int(math.log2(2 * m))
        masks.append(((ii >> sh) == (jj >> sh)) & ((ii & m) != 0) & ((jj & m) == 0))
    sh = int(math.log2(GLA_SUB))
    md = ((ii >> sh) == (jj >> sh)) & (jj <= ii)
    row = lax.broadcasted_iota(jnp.int32, (GC, 1), 0)
    second = [(row & m) != 0 for m in GLA_LEVELS]
    return masks, md, second


def _gla_log_decay(glr_ref, wg_ref, bg_ref):
    z = _dot(glr_ref[...].astype(BF16), wg_ref[...], NN) + bg_ref[...]
    la = (jnp.minimum(z, 0.0) - jnp.log1p(jnp.exp(-jnp.abs(z)))) * (1.0 / GATE_TAU)
    return z, la


def _gla_row_steps(b_ref, cs, rows, size):
    parts = [jnp.zeros((size, GLA_K), F32) if r is None else jnp.broadcast_to(b_ref[r:r + 1, cs], (size, GLA_K))
             for r in rows]
    return parts[0] if len(parts) == 1 else jnp.concatenate(parts, axis=0)


def _gla_factors(b_ref, h, second):
    cs = slice(h * GLA_K, (h + 1) * GLA_K)
    b = b_ref[:, cs]
    fq, fk = [], []
    for l, m in enumerate(GLA_LEVELS):
        d = b - _gla_row_steps(b_ref, cs, [s + m - 1 for s in range(0, GC, 2 * m)], 2 * m)
        f = jnp.exp(jnp.where(second[l], d, -d))
        fq.append(jnp.where(second[l], f, 0.0))
        fk.append(jnp.where(second[l], 0.0, f))
    dd = b - _gla_row_steps(b_ref, cs, [None] + [s - 1 for s in range(GLA_SUB, GC, GLA_SUB)], GLA_SUB)
    ed = jnp.exp(dd)
    edi = jnp.exp(-dd)
    eb = jnp.exp(b)
    bl = b_ref[GC - 1:GC, cs]
    ee = jnp.exp(bl - b)
    ebl = jnp.exp(bl)
    return fq, fk, ed, edi, eb, ee, ebl


def _gla_scores(q, k, fq, fk, ed, edi, masks, md):
    qt = [(q * f).astype(BF16) for f in fq]
    kt = [(k * f).astype(BF16) for f in fk]
    qd = (q * ed).astype(BF16)
    kd = (k * edi).astype(BF16)
    a = jnp.where(md, _dot(qd, kd, NT), 0.0)
    for l in range(NLEV):
        a = a + jnp.where(masks[l], _dot(qt[l], kt[l], NT), 0.0)
    return a, qt, kt, qd, kd


def _gla_fwd_call(gqk, gv, glr, gg, wg, bg, gain, pmat, comm=None):
    tp = gqk.shape[0]
    nc = tp // GC
    n_xc = len(comm.srcs) if comm else 0

    def body(qk_ref, v_ref, glr_ref, gg_ref, wg_ref, bg_ref, g_ref, p_ref, *rest):
        xc_src = rest[:n_xc]
        o_ref, a_ref, st_ref = rest[n_xc:n_xc + 3]
        xc_dst = rest[n_xc + 3:2 * n_xc + 3]
        s_scr, b_scr = rest[2 * n_xc + 3:2 * n_xc + 5]
        n = pl.program_id(0)
        if n_xc:
            copies = comm.make(xc_src, xc_dst, rest[-2], rest[-1])

            @pl.when(n == 0)
            def _():
                for cp in copies:
                    cp.start()

            @pl.when(n == nc - 1)
            def _():
                for cp in copies:
                    cp.wait()

        @pl.when(n == 0)
        def _():
            s_scr[...] = jnp.zeros_like(s_scr)

        _, la = _gla_log_decay(glr_ref, wg_ref, bg_ref)
        b_scr[...] = _exact_pm(p_ref[...], la)
        masks, md, second = _gla_masks()
        for h in range(GLA_HEADS):
            q = qk_ref[:, h * GLA_K:(h + 1) * GLA_K]
            k = qk_ref[:, GLA_KW + h * GLA_K:GLA_KW + (h + 1) * GLA_K]
            vs = slice(h * GLA_V, (h + 1) * GLA_V)
            v = v_ref[:, vs]
            fq, fk, ed, edi, eb, ee, ebl = _gla_factors(b_scr, h, second)
            a, *_ = _gla_scores(q, k, fq, fk, ed, edi, masks, md)
            sb = s_scr[h].astype(BF16)
            st_ref[0, h] = sb
            o = _dot(a.astype(BF16), v, NN) + _dot((q * eb).astype(BF16), sb, NT)
            s_scr[h] = s_scr[h] * ebl + _dot(v, (k * ee).astype(BF16), TN)
            o_ref[:, vs] = o
            xh = o * lax.rsqrt(jnp.mean(o * o, axis=-1, keepdims=True) + EPS)
            g = gg_ref[:, vs]
            a_ref[:, vs] = (xh * g_ref[:, vs] * (g * _sigmoid(g))).astype(BF16)

    return pl.pallas_call(
        body, name="gla_fwd", grid=(nc,),
        in_specs=[pl.BlockSpec((GC, 2 * GLA_KW), lambda n: (n, 0)),
                  pl.BlockSpec((GC, GLA_W), lambda n: (n, 0)),
                  pl.BlockSpec((GC, 128), lambda n: (n, 0)),
                  pl.BlockSpec((GC, GLA_W), lambda n: (n, 0)),
                  pl.BlockSpec((128, GLA_KW), lambda n: (0, 0)),
                  pl.BlockSpec((1, GLA_KW), lambda n: (0, 0)),
                  pl.BlockSpec((1, GLA_W), lambda n: (0, 0)),
                  pl.BlockSpec((GC, GC), lambda n: (0, 0))] + [ANY] * n_xc,
        out_specs=[pl.BlockSpec((GC, GLA_W), lambda n: (n, 0)),
                   pl.BlockSpec((GC, GLA_W), lambda n: (n, 0)),
                   pl.BlockSpec((1, GLA_HEADS, GLA_V, GLA_K), lambda n: (n, 0, 0, 0))] + [ANY] * n_xc,
        out_shape=[jax.ShapeDtypeStruct((tp, GLA_W), F32), jax.ShapeDtypeStruct((tp, GLA_W), BF16),
                   jax.ShapeDtypeStruct((nc, GLA_HEADS, GLA_V, GLA_K), BF16)] + (list(comm.out_shapes) if comm else []),
        scratch_shapes=[pltpu.VMEM((GLA_HEADS, GLA_V, GLA_K), F32), pltpu.VMEM((GC, GLA_KW), F32)]
        + (_comm_sems(comm) if comm else []),
        compiler_params=_cparams(1),
    )(gqk, gv, glr, gg, wg, bg, gain, pmat, *(comm.srcs if comm else ()))


def _gla_bwd_call(gqk, gv, glr, gg, o_gla, da, states, wg, bg, gain, pmat, pmat_t, comm=None):
    tp = gqk.shape[0]
    nc = tp // GC
    o_gv, o_gg, o_lr = 2 * GLA_KW, 2 * GLA_KW + GLA_W, 2 * GLA_KW + 2 * GLA_W
    n_xc = len(comm.srcs) if comm else 0

    def body(qk_ref, v_ref, glr_ref, gg_ref, o_ref, da_ref, st_ref, wg_ref, bg_ref, g_ref, p_ref, pt_ref, *rest):
        xc_src = rest[:n_xc]
        dp_ref, dwg_ref, dbg_ref, dg_ref = rest[n_xc:n_xc + 4]
        xc_dst = rest[n_xc + 4:2 * n_xc + 4]
        ds_scr, b_scr, db_scr = rest[2 * n_xc + 4:2 * n_xc + 7]
        n = pl.program_id(0)
        if n_xc:
            copies = comm.make(xc_src, xc_dst, rest[-2], rest[-1])

            @pl.when(n == 0)
            def _():
                for cp in copies:
                    cp.start()

            @pl.when(n == nc - 1)
            def _():
                for cp in copies:
                    cp.wait()

        @pl.when(n == 0)
        def _():
            ds_scr[...] = jnp.zeros_like(ds_scr)
            dwg_ref[...] = jnp.zeros_like(dwg_ref)
            dbg_ref[...] = jnp.zeros_like(dbg_ref)
            dg_ref[...] = jnp.zeros_like(dg_ref)

        z, la = _gla_log_decay(glr_ref, wg_ref, bg_ref)
        b_scr[...] = _exact_pm(p_ref[...], la)
        masks, md, second = _gla_masks()
        for h in range(GLA_HEADS):
            cs = slice(h * GLA_K, (h + 1) * GLA_K)
            vs = slice(h * GLA_V, (h + 1) * GLA_V)
            o = o_ref[:, vs]
            rstd = lax.rsqrt(jnp.mean(o * o, axis=-1, keepdims=True) + EPS)
            xh = o * rstd
            gain_h = g_ref[:, vs]
            g = gg_ref[:, vs]
            sg = _sigmoid(g)
            dah = da_ref[:, vs]
            dp_ref[:, o_gg + h * GLA_V:o_gg + (h + 1) * GLA_V] = (
                dah * (xh * gain_h) * (sg * (1.0 + g * (1.0 - sg)))).astype(BF16)
            dn = dah * (g * sg)
            dg_ref[:, vs] += jnp.sum(dn * xh, axis=0, keepdims=True)
            dxh = dn * gain_h
            do = rstd * (dxh - xh * jnp.mean(dxh * xh, axis=-1, keepdims=True))
            dob = do.astype(BF16)
            q = qk_ref[:, cs]
            k = qk_ref[:, GLA_KW + h * GLA_K:GLA_KW + (h + 1) * GLA_K]
            v = v_ref[:, vs]
            fq, fk, ed, edi, eb, ee, ebl = _gla_factors(b_scr, h, second)
            a, qt, kt, qd, kd = _gla_scores(q, k, fq, fk, ed, edi, masks, md)
            sp = st_ref[0, h]
            ds = ds_scr[h]
            dsb = ds.astype(BF16)
            q_in = q * eb
            k_end = k * ee
            da_s = _dot(dob, v, NT)
            dv = _dot(a.astype(BF16), dob, TN) + _dot(k_end.astype(BF16), dsb, NT)
            dq_in = _dot(dob, sp, NN)
            dk_end = _dot(v, dsb, NN)
            dbl = jnp.sum(sp.astype(F32) * ds, axis=0, keepdims=True) * ebl
            ds_scr[h] = ds * ebl + _dot(dob, q_in.astype(BF16), TN)
            dq = dq_in * eb
            dk = dk_end * ee
            de_end = dk_end * k_end
            db = dq_in * q_in - de_end
            placed = [(GC - 1, jnp.sum(de_end, axis=0, keepdims=True) + dbl)]
            for l, m in enumerate(GLA_LEVELS):
                dal = jnp.where(masks[l], da_s, 0.0).astype(BF16)
                dqt = _dot(dal, kt[l], NN)
                dkt = _dot(dal, qt[l], TN)
                dq = dq + dqt * fq[l]
                dk = dk + dkt * fk[l]
                gl = dqt * (q * fq[l]) - dkt * (k * fk[l])
                db = db + gl
                placed += [(s + m - 1, -jnp.sum(gl[s:s + 2 * m], axis=0, keepdims=True)) for s in range(0, GC, 2 * m)]
            dad = jnp.where(md, da_s, 0.0).astype(BF16)
            dqd = _dot(dad, kd, NN)
            dkd = _dot(dad, qd, TN)
            dq = dq + dqd * ed
            dk = dk + dkd * edi
            gd = dqd * (q * ed) - dkd * (k * edi)
            db = db + gd
            placed += [(s - 1, -jnp.sum(gd[s:s + GLA_SUB], axis=0, keepdims=True)) for s in range(GLA_SUB, GC, GLA_SUB)]
            db_scr[:, cs] = db
            for r, val in placed:
                db_scr[r:r + 1, cs] += val
            dp_ref[:, cs] = (dq * (GLA_K ** -0.5)).astype(BF16)
            dp_ref[:, GLA_KW + h * GLA_K:GLA_KW + (h + 1) * GLA_K] = dk.astype(BF16)
            dp_ref[:, o_gv + h * GLA_V:o_gv + (h + 1) * GLA_V] = dv.astype(BF16)
        dla = _exact_pm(pt_ref[...], db_scr[...])
        row = (nc - 1 - n) * GC + lax.broadcasted_iota(jnp.int32, (GC, 1), 0)
        dz = jnp.where(row >= PADF, dla * (1.0 / GATE_TAU) * _sigmoid(-z), 0.0)
        dzb = dz.astype(BF16)
        dp_ref[:, o_lr:] = _dot(dzb, wg_ref[...], NT).astype(BF16)
        dwg_ref[...] += _dot(glr_ref[...].astype(BF16), dzb, TN)
        dbg_ref[...] += jnp.sum(dz, axis=0, keepdims=True)

    rev = lambda n: (nc - 1 - n, 0)
    const = lambda n: (0, 0)
    xc_shapes, xc_sems = (list(comm.out_shapes), _comm_sems(comm)) if n_xc else ([], [])
    return pl.pallas_call(
        body, name="gla_bwd", grid=(nc,),
        in_specs=[pl.BlockSpec((GC, 2 * GLA_KW), rev),
                  pl.BlockSpec((GC, GLA_W), rev),
                  pl.BlockSpec((GC, 128), rev),
                  pl.BlockSpec((GC, GLA_W), rev),
                  pl.BlockSpec((GC, GLA_W), rev),
                  pl.BlockSpec((GC, GLA_W), rev),
                  pl.BlockSpec((1, GLA_HEADS, GLA_V, GLA_K), lambda n: (nc - 1 - n, 0, 0, 0)),
                  pl.BlockSpec((128, GLA_KW), const),
                  pl.BlockSpec((1, GLA_KW), const),
                  pl.BlockSpec((1, GLA_W), const),
                  pl.BlockSpec((GC, GC), const),
                  pl.BlockSpec((GC, GC), const)] + [ANY] * n_xc,
        out_specs=[pl.BlockSpec((GC, W_GP), rev), pl.BlockSpec((128, GLA_KW), const),
                   pl.BlockSpec((1, GLA_KW), const), pl.BlockSpec((1, GLA_W), const)] + [ANY] * n_xc,
        out_shape=[jax.ShapeDtypeStruct((tp, W_GP), BF16), jax.ShapeDtypeStruct((128, GLA_KW), F32),
                   jax.ShapeDtypeStruct((1, GLA_KW), F32), jax.ShapeDtypeStruct((1, GLA_W), F32)] + xc_shapes,
        scratch_shapes=[pltpu.VMEM((GLA_HEADS, GLA_V, GLA_K), F32), pltpu.VMEM((GC, GLA_KW), F32),
                        pltpu.VMEM((GC, GLA_KW), F32)] + xc_sems,
        compiler_params=_cparams(1),
    )(gqk, gv, glr, gg, o_gla, da, states, wg, bg, gain, pmat, pmat_t, *(comm.srcs if comm else ()))


def _mid_call(a_ret, a_gla, mg, h0, tgt, wbr, wbg, wout, gf):
    tp = h0.shape[0]
    nt = tp // TM

    def body(ar_ref, ag_ref, mg_ref, h_ref, t_ref, wbr_ref, wbg_ref, wo_ref, gf_ref,
             dh1_ref, dar_ref, dag_ref, dm_ref, mb_ref, dh1b_ref, dprb_ref, dpgb_ref, loss_ref, dgf_ref):
        i = pl.program_id(0)

        @pl.when(i == 0)
        def _():
            loss_ref[...] = jnp.zeros_like(loss_ref)
            dgf_ref[...] = jnp.zeros_like(dgf_ref)

        ar, ag = ar_ref[...], ag_ref[...]
        pr = _dot(ar, wbr_ref[...], NN)
        pg = _dot(ag, wbg_ref[...], NN)
        sr = _sigmoid(mg_ref[:, :D_MODEL])
        sg = _sigmoid(mg_ref[:, D_MODEL:])
        merged = (sr * pr + sg * pg).astype(BF16)
        mb_ref[...] = merged
        h1 = h_ref[...] + _dot(merged, wo_ref[...], NN)
        r1 = lax.rsqrt(jnp.mean(h1 * h1, axis=-1, keepdims=True) + EPS)
        xh = h1 * r1
        gfv = gf_ref[...]
        live = jnp.where(i > 0, 1.0, 0.0).astype(F32)
        err = (xh * gfv - t_ref[...]) * live
        loss_ref[...] += jnp.full(loss_ref.shape, 0.5 / D_MODEL, F32) * jnp.sum(err * err)
        dy = err * (1.0 / D_MODEL)
        dgf_ref[...] += jnp.sum(dy * xh, axis=0, keepdims=True)
        dxh = dy * gfv
        dh1 = r1 * (dxh - xh * jnp.mean(dxh * xh, axis=-1, keepdims=True))
        dh1_ref[...] = dh1
        dh1b = dh1.astype(BF16)
        dh1b_ref[...] = dh1b
        dmerged = _dot(dh1b, wo_ref[...], NT)
        dm_ref[:, :D_MODEL] = (dmerged * pr * sr * (1.0 - sr)).astype(BF16)
        dm_ref[:, D_MODEL:] = (dmerged * pg * sg * (1.0 - sg)).astype(BF16)
        dpr = (dmerged * sr).astype(BF16)
        dpg = (dmerged * sg).astype(BF16)
        dprb_ref[...] = dpr
        dpgb_ref[...] = dpg
        dar_ref[...] = _dot(dpr, wbr_ref[...], NT)
        dag_ref[...] = _dot(dpg, wbg_ref[...], NT)

    tile = lambda w: pl.BlockSpec((TM, w), lambda i: (i, 0))
    const = lambda r, w: pl.BlockSpec((r, w), lambda i: (0, 0))
    return pl.pallas_call(
        body, name="merge_out_loss", grid=(nt,),
        in_specs=[tile(RET_W), tile(GLA_W), tile(W_M), tile(D_MODEL),
                  pl.BlockSpec((TM, D_MODEL), lambda i: (jnp.maximum(i - 1, 0), 0)),
                  const(RET_W, D_MODEL), const(GLA_W, D_MODEL), const(D_MODEL, D_MODEL), const(1, D_MODEL)],
        out_specs=[tile(D_MODEL), tile(RET_W), tile(GLA_W), tile(W_M), tile(D_MODEL), tile(D_MODEL), tile(D_MODEL),
                   tile(D_MODEL), const(1, 128), const(1, D_MODEL)],
        out_shape=[jax.ShapeDtypeStruct((tp, D_MODEL), F32), jax.ShapeDtypeStruct((tp, RET_W), F32),
                   jax.ShapeDtypeStruct((tp, GLA_W), F32), jax.ShapeDtypeStruct((tp, W_M), BF16),
                   jax.ShapeDtypeStruct((tp, D_MODEL), BF16), jax.ShapeDtypeStruct((tp, D_MODEL), BF16),
                   jax.ShapeDtypeStruct((tp, D_MODEL), BF16), jax.ShapeDtypeStruct((tp, D_MODEL), BF16),
                   jax.ShapeDtypeStruct((1, 128), F32), jax.ShapeDtypeStruct((1, D_MODEL), F32)],
        compiler_params=_cparams(1),
    )(a_ret, a_gla, mg, h0, tgt, wbr, wbg, wout, gf)


def _device_step(x2d, tgt2d, meta, norm_gain, w_in_bf, w_gate_up, b_gate, ret_gain, gla_gain, branch_parts,
                 final_gain, ck):
    seq = x2d.shape[0]
    tp = T0 + seq
    head = jnp.concatenate([jnp.zeros((PADF, D_MODEL), F32), meta], axis=0)
    w_r = w_in_bf
    w_g = jnp.pad(w_in_bf[:, W_R:W_R + W_G], ((0, 0), (0, W_GP - W_G)))
    w_m = w_in_bf[:, W_R + W_G:]
    wg_pad = jnp.pad(w_gate_up, ((0, 128 - GATE_RANK), (0, 0))).astype(BF16)

    pos = jnp.arange(tp, dtype=F32) - PADF
    half = RET_QK // 2
    inv = ROPE_BASE ** (-jnp.arange(half, dtype=F32) / half)
    ang = pos[:, None] * inv[None, :]
    cos, sin = jnp.cos(ang), jnp.sin(ang)
    lgam = jnp.log1p(-(2.0 ** (-5.0 - jnp.arange(RET_HEADS, dtype=F32))))
    pmat = jnp.asarray(_gla_tril(), BF16)
    pmat_t = jnp.asarray(_gla_tril().T.copy(), BF16)

    h0, u = _rms_call(x2d, head, norm_gain)
    tab = pl.BlockSpec((TB, half), lambda j, i: (i, 0))
    rqk = _mm_nn("proj_rqk", u, w_r, BF16, D_MODEL, 0, 2 * D_MODEL, _rope_epilogue, (cos, sin), (tab, tab))
    rv = _mm_nn("proj_rv", u, w_r, BF16, D_MODEL, 2 * D_MODEL, RET_W)
    rg = _mm_nn("proj_rg", u, w_r, F32, D_MODEL, 4 * D_MODEL, RET_W)
    gqk = _mm_nn("proj_gqk", u, w_g, F32, 2 * GLA_KW, 0, 2 * GLA_KW, _gqk_epilogue)
    gv = _mm_nn("proj_gv", u, w_g, BF16, GLA_W, 2 * GLA_KW, GLA_W)
    gg = _mm_nn("proj_gg", u, w_g, F32, GLA_W, 2 * GLA_KW + GLA_W, GLA_W)
    glr = _mm_nn("proj_glr", u, w_g, F32, 128, 2 * GLA_KW + 2 * GLA_W, 128)
    mg = _mm_nn("proj_mg", u, w_m, F32, D_MODEL, 0, W_M)

    o_ret, a_ret, st_ret = _ret_fwd_call(rqk, rv, rg, ret_gain, lgam)
    o_gla, a_gla, st_gla, g_br, g_bg, g_out = _gla_fwd_call(gqk, gv, glr, gg, wg_pad, b_gate, gla_gain, pmat,
                                                            comm=_spread_plan(branch_parts))
    wbr = g_br.reshape(RET_W, D_MODEL)
    wbg = g_bg.reshape(GLA_W, D_MODEL)
    wout = g_out.reshape(D_MODEL, D_MODEL)

    gf = final_gain.reshape(1, D_MODEL)
    (dh1, da_ret, da_gla, dm, merged_b, dh1_b, dpr_b, dpg_b, loss, dgf) = _mid_call(
        a_ret, a_gla, mg, h0, tgt2d, wbr, wbg, wout, gf)

    names_b = ("w_branch_ret", "w_branch_gla", "w_out")
    g2_b = [_mm_tn("dw_br", a_ret, dpr_b, D_MODEL).reshape(4, 2, RET_W // 8, D_MODEL).transpose(1, 0, 2, 3),
            _mm_tn("dw_bg", a_gla, dpg_b, D_MODEL).reshape(4, 2, GLA_W // 8, D_MODEL).transpose(1, 0, 2, 3),
            _mm_tn("dw_out", merged_b, dh1_b, D_MODEL).reshape(4, 2, D_MODEL // 8, D_MODEL).transpose(1, 0, 2, 3)]
    sib_b = _swap_halves_call("swap_halves_branch", g2_b)
    sum_b = [_add_half_call("add_half_" + nm, g, b, ck) for nm, g, b in zip(names_b, g2_b, sib_b)]
    d_g, dwg, dbg, dgla_gain, *chips_b = _gla_bwd_call(gqk, gv, glr, gg, o_gla, da_gla, st_gla, wg_pad, b_gate,
                                                       gla_gain, pmat, pmat_t, comm=_exchange_plan(sum_b))
    mine = [_add_chips_call("add_chips_" + nm, g, b, p, ck) for nm, g, b, p in zip(names_b, g2_b, sib_b, chips_b)]

    d_r, dret_gain = _ret_bwd_call(rqk, rv, rg, o_ret, da_ret, st_ret, ret_gain, lgam, cos, sin)

    dwp = _mm_tn("dw_r", u, d_r, 2 * D_MODEL, out_cols=IN_PAD)
    dwp = _mm_tn("dw_g", u, d_g, D_MODEL, ncols=W_GP - 128, into=dwp, col0=W_R)
    dwp = _mm_tn("dw_glr", u, d_g, 128, ncols=128, bcol0=W_GP - 128, into=dwp, col0=W_R + W_GP - 128)
    dw_m = jnp.pad(_mm_tn("dw_m", u, dm, 2 * D_MODEL), ((0, 0), (0, IN_PAD - IN_COLS)))
    g2_in = lax.dynamic_update_slice(dwp, dw_m, (0, W_R + W_G)).reshape(2, D_MODEL // 2, IN_PAD)

    du, sib_in = _mm_nt_acc("du_m", dm, w_m, W_M, comm=_swap_plan([g2_in]))
    sum_in = _add_rows_call("add_half_w_in", g2_in, sib_in, ck)
    du = _mm_nt_acc("du_g", d_g, w_g, W_GP, acc_in=du)[0]
    tile = pl.BlockSpec((TB, D_MODEL), lambda i, kk: (i, 0))
    row = pl.BlockSpec((1, D_MODEL), lambda i, kk: (0, 0))
    dh0, dnorm_gain, chips_in = _mm_nt_acc(
        "du_r", d_r, w_r, 2 * D_MODEL, acc_in=du, epilogue=_rms_bwd_epilogue, extras=(h0, norm_gain, dh1),
        extra_specs=(tile, row, tile),
        extra_out_shapes=(jax.ShapeDtypeStruct((tp, D_MODEL), F32), jax.ShapeDtypeStruct((1, D_MODEL), F32)),
        extra_out_specs=(tile, row), comm=_exchange_window_plan(sum_in))
    mine = [_add_window_call("add_chips_w_in", g2_in, sib_in, chips_in, ck)] + mine
    full = _join_halves_call("join_halves", mine)

    return dict(loss=loss[0, 0], dx=dh0[T0:], dmeta=dh0[PADF:T0], norm_gain=dnorm_gain, w_gate_up=dwg[:GATE_RANK], b_gate=dbg,
                ret_norm_gain=dret_gain, gla_norm_gain=dgla_gain, final_norm_gain=dgf.reshape(D_MODEL),
                w_in=full[0], w_branch_ret=full[1], w_branch_gla=full[2], w_out=full[3])


MESH = pl.DeviceIdType.MESH
ANY = pl.BlockSpec(memory_space=pl.ANY)


def _place():
    return lax.axis_index("x"), lax.axis_index("y"), lax.axis_index("c")


def _gather8_call(name, parts):
    n = len(parts)

    def body(*refs):
        x_refs, out_refs = refs[:n], refs[n:2 * n]
        send_sems, recv_sems, local_sems = refs[2 * n:]
        x, y, c = _place()
        me, sibling = (x, y, c), (x, y, 1 - c)
        chips = [(1 - x, y), (x, 1 - y), (1 - x, 1 - y)]

        def slot(t, px, py, pc):
            return out_refs[t].at[4 * px + 2 * py + pc]

        def copy(t, k, block, to, src=None):
            return pltpu.make_async_remote_copy(
                src_ref=slot(t, *block) if src is None else src, dst_ref=slot(t, *block),
                send_sem=send_sems.at[7 * t + k], recv_sem=recv_sems.at[7 * t + k], device_id=to, device_id_type=MESH)

        mine = [pltpu.make_async_copy(x_refs[t], slot(t, *me), local_sems.at[t]) for t in range(n)]
        for cp in mine:
            cp.start()
        first = []
        for t in range(n):
            first.append(copy(t, 0, me, sibling, src=x_refs[t]))
            first += [copy(t, 1 + j, me, (*chip, c), src=x_refs[t]) for j, chip in enumerate(chips)]
        for cp in first:
            cp.start()
        passed = []
        for j, chip in enumerate(chips):
            for t in range(n):
                copy(t, 1 + j, (*chip, c), me).wait_recv()
                fwd = copy(t, 4 + j, (*chip, c), sibling)
                fwd.start()
                passed.append(fwd)
        for t in range(n):
            copy(t, 0, sibling, me).wait_recv()
            for j, chip in enumerate(chips):
                copy(t, 4 + j, (*chip, 1 - c), me).wait_recv()
        for cp in first + passed:
            cp.wait_send()
        for cp in mine:
            cp.wait()

    return pl.pallas_call(
        body, name=name,
        out_shape=[jax.ShapeDtypeStruct((8,) + p.shape, p.dtype) for p in parts],
        in_specs=[ANY] * n, out_specs=[ANY] * n,
        scratch_shapes=[pltpu.SemaphoreType.DMA((7 * n,)), pltpu.SemaphoreType.DMA((7 * n,)),
                        pltpu.SemaphoreType.DMA((n,))],
    )(*parts)


def _swap_halves_call(name, gs):
    n = len(gs)

    def body(*refs):
        g_refs, b_refs = refs[:n], refs[n:2 * n]
        send_sems, recv_sems = refs[2 * n:]
        x, y, c = _place()
        copies = [pltpu.make_async_remote_copy(
            src_ref=g_refs[t].at[1 - c], dst_ref=b_refs[t], send_sem=send_sems.at[t], recv_sem=recv_sems.at[t],
            device_id=(x, y, 1 - c), device_id_type=MESH) for t in range(n)]
        for cp in copies:
            cp.start()
        for cp in copies:
            cp.wait()

    return pl.pallas_call(
        body, name=name,
        out_shape=[jax.ShapeDtypeStruct(g.shape[1:], g.dtype) for g in gs],
        in_specs=[ANY] * n, out_specs=[ANY] * n,
        scratch_shapes=[pltpu.SemaphoreType.DMA((n,)), pltpu.SemaphoreType.DMA((n,))],
    )(*gs)


def _join_halves_call(name, ts):
    n = len(ts)

    def body(*refs):
        o_refs = refs[n:2 * n]
        send_sems, recv_sems = refs[2 * n:]
        x, y, c = _place()
        copies = [pltpu.make_async_remote_copy(
            src_ref=o_refs[t].at[c], dst_ref=o_refs[t].at[c], send_sem=send_sems.at[t], recv_sem=recv_sems.at[t],
            device_id=(x, y, 1 - c), device_id_type=MESH) for t in range(n)]
        for cp in copies:
            cp.start()
        for t in range(n):
            copies[t].wait_send()
            pltpu.make_async_remote_copy(
                src_ref=o_refs[t].at[c], dst_ref=o_refs[t].at[1 - c], send_sem=send_sems.at[t],
                recv_sem=recv_sems.at[t], device_id=(x, y, 1 - c), device_id_type=MESH).wait_recv()

    return pl.pallas_call(
        body, name=name,
        out_shape=[jax.ShapeDtypeStruct(t.shape, t.dtype) for t in ts],
        in_specs=[ANY] * n, out_specs=[ANY] * n, input_output_aliases={t: t for t in range(n)},
        scratch_shapes=[pltpu.SemaphoreType.DMA((n,)), pltpu.SemaphoreType.DMA((n,))],
    )(*ts)


def _row_block(rows, cols, budget):
    best = 8
    for rb in range(8, rows + 1, 8):
        if rows % rb == 0 and rb * cols * 4 <= budget:
            best = rb
    return best


def _add_half_call(name, g, b, ck):
    _, _, r, cc = g.shape
    rb = _row_block(r, cc, 2 * 1024 * 1024)

    def body(ck_ref, g_ref, b_ref, o_ref):
        o_ref[...] = (g_ref[...] + b_ref[...]).astype(BF16)

    return pl.pallas_call(
        body, name=name,
        grid_spec=pltpu.PrefetchScalarGridSpec(
            num_scalar_prefetch=1, grid=(4, r // rb),
            in_specs=[pl.BlockSpec((None, None, rb, cc), lambda k, i, ck_ref: (ck_ref[0], k, i, 0)),
                      pl.BlockSpec((None, rb, cc), lambda k, i, ck_ref: (k, i, 0))],
            out_specs=pl.BlockSpec((None, rb, cc), lambda k, i, ck_ref: (k, i, 0))),
        out_shape=jax.ShapeDtypeStruct(b.shape, BF16),
        compiler_params=_cparams(2),
    )(ck, g, b)


def _add_rows_call(name, g, b, ck):
    _, r, cc = g.shape
    rb = _row_block(r, cc, 2 * 1024 * 1024)

    def body(ck_ref, g_ref, b_ref, o_ref):
        o_ref[...] = (g_ref[...] + b_ref[...]).astype(BF16)

    return pl.pallas_call(
        body, name=name,
        grid_spec=pltpu.PrefetchScalarGridSpec(
            num_scalar_prefetch=1, grid=(r // rb,),
            in_specs=[pl.BlockSpec((None, rb, cc), lambda i, ck_ref: (ck_ref[0], i, 0)),
                      pl.BlockSpec((rb, cc), lambda i, ck_ref: (i, 0))],
            out_specs=pl.BlockSpec((rb, cc), lambda i, ck_ref: (i, 0))),
        out_shape=jax.ShapeDtypeStruct((r, cc), BF16),
        compiler_params=_cparams(1),
    )(ck, g, b)


def _add_window_call(name, g, b, p, ck):
    _, r, _ = g.shape
    nb, step = WIN_W // 128, WIN_STEP // 128

    def body(ck_ref, g_ref, b_ref, p0_ref, p1_ref, p2_ref, o_ref):
        own = g_ref[...] + b_ref[...]
        o_ref[...] = ((own + p0_ref[...].astype(F32)) + p1_ref[...].astype(F32)) + p2_ref[...].astype(F32)

    def peer(j):
        return pl.BlockSpec((None, r, 128), lambda i, ck_ref: (j, 0, i))

    return pl.pallas_call(
        body, name=name,
        grid_spec=pltpu.PrefetchScalarGridSpec(
            num_scalar_prefetch=1, grid=(nb,),
            in_specs=[pl.BlockSpec((None, r, 128), lambda i, ck_ref: (ck_ref[0], 0, step * ck_ref[1] + i)),
                      pl.BlockSpec((r, 128), lambda i, ck_ref: (0, step * ck_ref[1] + i)),
                      peer(0), peer(1), peer(2)],
            out_specs=pl.BlockSpec((None, r, 128), lambda i, ck_ref: (ck_ref[0], 0, i))),
        out_shape=jax.ShapeDtypeStruct((2, r, WIN_W), F32),
        compiler_params=_cparams(1),
    )(ck, g, b, p, p, p)


def _add_chips_call(name, g, b, p, ck):
    _, _, r, cc = g.shape
    rb = _row_block(r, cc, 2 * 1024 * 1024)

    def body(ck_ref, g_ref, b_ref, p0_ref, p1_ref, p2_ref, o_ref):
        own = g_ref[...] + b_ref[...]
        o_ref[...] = ((own + p0_ref[...].astype(F32)) + p1_ref[...].astype(F32)) + p2_ref[...].astype(F32)

    def peer(j):
        return pl.BlockSpec((None, rb, cc), lambda i, ck_ref: (j, i, 0))

    return pl.pallas_call(
        body, name=name,
        grid_spec=pltpu.PrefetchScalarGridSpec(
            num_scalar_prefetch=1, grid=(r // rb,),
            in_specs=[pl.BlockSpec((None, None, rb, cc), lambda i, ck_ref: (ck_ref[0], ck_ref[1], i, 0)),
                      pl.BlockSpec((None, rb, cc), lambda i, ck_ref: (ck_ref[1], i, 0)),
                      peer(0), peer(1), peer(2)],
            out_specs=pl.BlockSpec((None, rb, cc), lambda i, ck_ref: (ck_ref[0], i, 0))),
        out_shape=jax.ShapeDtypeStruct((2, r, cc), F32),
        compiler_params=_cparams(1),
    )(ck, g, b, p, p, p)


def _sum8_call(name, g):
    def body(g_ref, o_ref):
        acc = g_ref[0]
        for d in range(1, 8):
            acc = acc + g_ref[d]
        o_ref[...] = acc

    return pl.pallas_call(body, name=name, out_shape=jax.ShapeDtypeStruct(g.shape[1:], F32))(g)


def _adamw_call(name, w, g, m, v):
    r, cc = w.shape
    if r % 8 == 0 or r * cc * 4 <= 1024 * 1024:
        rb = _row_block(r, cc, 1024 * 1024) if r % 8 == 0 else r
        grid, spec = (r // rb,), pl.BlockSpec((rb, cc), lambda i: (i, 0))
    else:
        grid, spec = (cc // 128,), pl.BlockSpec((r, 128), lambda i: (0, i))

    def body(w_ref, g_ref, m_ref, v_ref, d_ref, m2_ref, v2_ref):
        gv = g_ref[...]
        m2 = ADAM_B1 * m_ref[...] + (1.0 - ADAM_B1) * gv
        v2 = ADAM_B2 * v_ref[...] + (1.0 - ADAM_B2) * (gv * gv)
        m_hat = m2 / (1.0 - ADAM_B1 ** ADAM_STEP)
        v_hat = v2 / (1.0 - ADAM_B2 ** ADAM_STEP)
        d_ref[...] = -ADAM_LR * (m_hat / (jnp.sqrt(v_hat) + ADAM_EPS) + ADAM_WD * w_ref[...])
        m2_ref[...] = m2
        v2_ref[...] = v2

    return pl.pallas_call(
        body, name=name, grid=grid, in_specs=[spec] * 4, out_specs=[spec] * 3,
        out_shape=[jax.ShapeDtypeStruct((r, cc), F32)] * 3, compiler_params=_cparams(1),
    )(w, g, m, v)


SMALL = (("norm_gain", D_MODEL), ("b_gate", GLA_KW), ("ret_norm_gain", RET_W), ("gla_norm_gain", GLA_W),
         ("final_norm_gain", D_MODEL), ("w_gate_up", GATE_RANK * GLA_KW), ("meta_tokens", N_META * D_MODEL))


def _pack_rows(vecs, rows):
    flat = jnp.concatenate([v.reshape(-1) for v in vecs])
    return jnp.pad(flat, (0, rows * 128 - flat.shape[0])).reshape(rows, 128)


def kernel(x, meta_tokens, norm_gain, w_in, w_gate_up, b_gate, ret_norm_gain, gla_norm_gain, w_branch_ret, w_branch_gla, w_out, final_norm_gain, loss_target, m_meta_tokens, m_norm_gain, m_w_in, m_w_gate_up, m_b_gate, m_ret_norm_gain, m_gla_norm_gain, m_w_branch_ret, m_w_branch_gla, m_w_out, m_final_norm_gain, v_meta_tokens, v_norm_gain, v_w_in, v_w_gate_up, v_b_gate, v_ret_norm_gain, v_gla_norm_gain, v_w_branch_ret, v_w_branch_gla, v_w_out, v_final_norm_gain):
    xi, yi, ci = _place()
    kme = 2 * xi + yi
    ck = jnp.stack([ci, kme]).astype(jnp.int32)
    sw_in = w_in.shape[2]

    def my_half(a, dtype):
        r, cc = a.shape
        return lax.dynamic_index_in_dim(a.reshape(2, r // 2, cc), ci, 0, keepdims=False).astype(dtype)

    g_in, g_meta, g_wg = _gather8_call(
        "gather_weights", [my_half(w_in[0], BF16), my_half(meta_tokens, F32), my_half(w_gate_up[0], F32)])
    branch_parts = [my_half(w_branch_ret[0], BF16), my_half(w_branch_gla[0], BF16), my_half(w_out[0], BF16)]
    w_in_bf = g_in.reshape(4, 2, D_MODEL // 2, sw_in).transpose(1, 2, 0, 3).reshape(D_MODEL, 4 * sw_in)
    meta = g_meta.reshape(4, 2, N_META // 2, D_MODEL // 4).transpose(1, 2, 0, 3).reshape(N_META, D_MODEL)
    wg_full = g_wg.reshape(4, 2, GATE_RANK // 2, GLA_KW // 4).transpose(1, 2, 0, 3).reshape(GATE_RANK, GLA_KW)

    loc = _device_step(x[0], loss_target[0], meta, norm_gain, w_in_bf, wg_full, b_gate, ret_norm_gain, gla_norm_gain,
                       branch_parts, final_norm_gain, ck)
    loss = lax.psum(loc["loss"], ("x", "y", "c"))
    names = ("w_in", "w_branch_ret", "w_branch_gla", "w_out")
    full = [loc[nm] for nm in names]
    big_w = dict(w_in=w_in[0], w_branch_ret=w_branch_ret[0], w_branch_gla=w_branch_gla[0], w_out=w_out[0])
    big_m = dict(w_in=m_w_in[0], w_branch_ret=m_w_branch_ret[0], w_branch_gla=m_w_branch_gla[0], w_out=m_w_out[0])
    big_v = dict(w_in=v_w_in[0], w_branch_ret=v_w_branch_ret[0], w_branch_gla=v_w_branch_gla[0], w_out=v_w_out[0])
    grads, deltas, new_m, new_v = {}, {}, {}, {}
    for nm, f in zip(names, full):
        shape = big_w[nm].shape
        if nm == "w_in":
            f = lax.dynamic_slice_in_dim(f, (sw_in - WIN_STEP) * kme, sw_in, axis=2)
        g = f.reshape(shape)
        if nm == "w_in":
            d, m2, v2 = (a.T for a in _adamw_call("adamw_" + nm, big_w[nm].T, g.T, big_m[nm].T, big_v[nm].T))
        else:
            d, m2, v2 = _adamw_call("adamw_" + nm, big_w[nm], g, big_m[nm], big_v[nm])
        grads[nm], deltas[nm], new_m[nm], new_v[nm] = (a.reshape((1,) + shape) for a in (g, d, m2, v2))

    small_g = dict(loc)
    small_g["meta_tokens"] = loc["dmeta"]
    n_small = sum(sz for _, sz in SMALL)
    rows = -(-n_small // 128 // 8) * 8
    (g_small,) = _gather8_call("gather_small_grads", [_pack_rows([small_g[nm] for nm, _ in SMALL], rows)])
    tot = _sum8_call("sum_small_grads", g_small).reshape(-1)
    off = 0
    sg = {}
    for nm, sz in SMALL:
        sg[nm] = tot[off:off + sz]
        off += sz
    sg["w_gate_up"] = lax.dynamic_slice_in_dim(sg["w_gate_up"].reshape(GATE_RANK, GLA_KW), kme * (GLA_KW // 4),
                                               GLA_KW // 4, axis=1)
    sg["meta_tokens"] = lax.dynamic_slice_in_dim(sg["meta_tokens"].reshape(N_META, D_MODEL), kme * (D_MODEL // 4),
                                                 D_MODEL // 4, axis=1)
    small_w = dict(norm_gain=norm_gain, b_gate=b_gate, ret_norm_gain=ret_norm_gain, gla_norm_gain=gla_norm_gain,
                   final_norm_gain=final_norm_gain, w_gate_up=w_gate_up, meta_tokens=meta_tokens)
    small_m = dict(norm_gain=m_norm_gain, b_gate=m_b_gate, ret_norm_gain=m_ret_norm_gain,
                   gla_norm_gain=m_gla_norm_gain, final_norm_gain=m_final_norm_gain, w_gate_up=m_w_gate_up,
                   meta_tokens=m_meta_tokens)
    small_v = dict(norm_gain=v_norm_gain, b_gate=v_b_gate, ret_norm_gain=v_ret_norm_gain,
                   gla_norm_gain=v_gla_norm_gain, final_norm_gain=v_final_norm_gain, w_gate_up=v_w_gate_up,
                   meta_tokens=v_meta_tokens)
    order = [nm for nm, _ in SMALL]
    sizes = [small_w[nm].size for nm in order]
    prow = -(-sum(sizes) // 128 // 8) * 8
    pk = lambda d: _pack_rows([d[nm] for nm in order], prow)
    d_s, m_s, v_s = _adamw_call("adamw_small", pk(small_w), pk(sg), pk(small_m), pk(small_v))
    off = 0
    for nm, sz in zip(order, sizes):
        shape = small_w[nm].shape
        grads[nm] = sg[nm].reshape(shape)
        deltas[nm], new_m[nm], new_v[nm] = (a.reshape(-1)[off:off + sz].reshape(shape) for a in (d_s, m_s, v_s))
        off += sz

    out_order = ("meta_tokens", "norm_gain", "w_in", "w_gate_up", "b_gate", "ret_norm_gain", "gla_norm_gain",
                 "w_branch_ret", "w_branch_gla", "w_out", "final_norm_gain")
    dx = loc["dx"].reshape(x.shape)
    return (loss, dx, *[grads[nm] for nm in out_order], *[deltas[nm] for nm in out_order],
            *[new_m[nm] for nm in out_order], *[new_v[nm] for nm in out_order])
```

```python
import math
from typing import Callable, NamedTuple

import numpy as np
import jax
import jax.numpy as jnp
from jax import lax
from jax.experimental import pallas as pl
from jax.experimental.pallas import tpu as pltpu

F32 = jnp.float32
BF16 = jnp.bfloat16

D_MODEL = 1024
N_META = 16
EPS = 1e-6
ROPE_BASE = 10000.0
RET_HEADS, RET_QK, RET_V = 4, 256, 512
RET_W = RET_HEADS * RET_V
GLA_HEADS, GLA_K, GLA_V = 4, 128, 256
GLA_W = GLA_HEADS * GLA_V
GLA_KW = GLA_HEADS * GLA_K
GATE_RANK = 16
GATE_TAU = 16.0
GLA_SUB = 16

TM = 256
T0 = TM
PADF = T0 - N_META
GC = 128
TB = 768
TK = 768

W_R = 6144
W_G = 3088
W_GP = 3200
W_M = 2048
IN_COLS = W_R + W_G + W_M
WIN_STEP = (IN_COLS // 4) // 128 * 128
WIN_W = -(-(3 * (IN_COLS // 4 - WIN_STEP) + IN_COLS // 4) // 128) * 128
IN_PAD = 3 * WIN_STEP + WIN_W

ADAM_LR, ADAM_B1, ADAM_B2, ADAM_EPS, ADAM_WD, ADAM_STEP = 0.001, 0.9, 0.999, 1e-08, 0.01, 10

VMEM_LIMIT = 56 * 1024 * 1024

NN = ((1,), (0,))
NT = ((1,), (1,))
TN = ((0,), (0,))


def _dot(a, b, dims):
    return lax.dot_general(a, b, (dims, ((), ())), preferred_element_type=F32)


def _cparams(n_axes):
    return pltpu.CompilerParams(dimension_semantics=("arbitrary",) * n_axes, vmem_limit_bytes=VMEM_LIMIT)


def _sigmoid(x):
    return 0.5 * jnp.tanh(0.5 * x) + 0.5


def _split3(x):
    hi = x.astype(BF16)
    r1 = x - hi.astype(F32)
    mid = r1.astype(BF16)
    lo = (r1 - mid.astype(F32)).astype(BF16)
    return hi, mid, lo


def _exact_pm(p, x):
    hi, mid, lo = _split3(x)
    return _dot(p, hi, NN) + _dot(p, mid, NN) + _dot(p, lo, NN)


def _rms_call(x2d, head, gain):
    tp = T0 + x2d.shape[0]

    def body(x_ref, hd_ref, g_ref, h_ref, u_ref):
        h = jnp.where(pl.program_id(0) == 0, hd_ref[...], x_ref[...])
        h_ref[...] = h
        r = lax.rsqrt(jnp.mean(h * h, axis=-1, keepdims=True) + EPS)
        u_ref[...] = (h * r * g_ref[...]).astype(BF16)

    tile = pl.BlockSpec((TM, D_MODEL), lambda i: (i, 0))
    return pl.pallas_call(
        body, name="rms_in", grid=(tp // TM,),
        in_specs=[pl.BlockSpec((TM, D_MODEL), lambda i: (jnp.maximum(i - 1, 0), 0)),
                  pl.BlockSpec((T0, D_MODEL), lambda i: (0, 0)), pl.BlockSpec((1, D_MODEL), lambda i: (0, 0))],
        out_specs=[tile, tile],
        out_shape=[jax.ShapeDtypeStruct((tp, D_MODEL), F32), jax.ShapeDtypeStruct((tp, D_MODEL), BF16)],
        compiler_params=_cparams(1),
    )(x2d, head, gain)


PROJ_ROWS_MAX = 1408


def _proj_rows(m):
    return max(r for r in range(16, PROJ_ROWS_MAX + 1, 16) if m % r == 0)


def _mm_nn(name, a, b, out_dtype, tn, col0, ncols, epilogue=None, extras=(), extra_specs=()):
    m, k = a.shape
    nj, j0 = ncols // tn, col0 // tn
    tb = _proj_rows(m)

    def body(a_ref, b_ref, *rest):
        *ex, o_ref = rest
        acc = _dot(a_ref[...], b_ref[...], NN)
        if epilogue is None:
            o_ref[...] = acc.astype(out_dtype)
        else:
            epilogue(acc, o_ref, *ex)

    return pl.pallas_call(
        body, name=name, grid=(nj, m // tb),
        in_specs=[pl.BlockSpec((tb, k), lambda j, i: (i, 0)), pl.BlockSpec((k, tn), lambda j, i: (0, j0 + j))]
        + list(extra_specs),
        out_specs=pl.BlockSpec((tb, tn), lambda j, i: (i, j)),
        out_shape=jax.ShapeDtypeStruct((m, ncols), out_dtype),
        compiler_params=_cparams(2),
    )(a, b, *extras)


def _rope_epilogue(acc, o_ref, cos_ref, sin_ref):
    scale = jnp.where(pl.program_id(0) == 1, RET_QK ** -0.5, 1.0).astype(F32)
    cos, sin = cos_ref[...], sin_ref[...]
    half = RET_QK // 2
    for h in range(RET_HEADS):
        t1 = acc[:, h * RET_QK:h * RET_QK + half]
        t2 = acc[:, h * RET_QK + half:(h + 1) * RET_QK]
        o_ref[:, h * RET_QK:h * RET_QK + half] = ((t1 * cos - t2 * sin) * scale).astype(BF16)
        o_ref[:, h * RET_QK + half:(h + 1) * RET_QK] = ((t2 * cos + t1 * sin) * scale).astype(BF16)


def _gqk_epilogue(acc, o_ref):
    o_ref[:, :GLA_KW] = acc[:, :GLA_KW] * (GLA_K ** -0.5)
    o_ref[:, GLA_KW:] = acc[:, GLA_KW:]


class _Comm(NamedTuple):
    srcs: tuple
    out_shapes: tuple
    n_sems: int
    make: Callable


def _comm_sems(comm):
    return [pltpu.SemaphoreType.DMA((comm.n_sems,)), pltpu.SemaphoreType.DMA((comm.n_sems,))]


def _other_chips(x, y):
    return [(1 - x, y), (x, 1 - y), (1 - x, 1 - y)]


def _exchange_plan(ss):
    def make(s_refs, b_refs, send_sems, recv_sems):
        x, y, c = _place()
        return [pltpu.make_async_remote_copy(
            src_ref=s_refs[t].at[2 * chip[0] + chip[1]], dst_ref=b_refs[t].at[j], send_sem=send_sems.at[3 * t + j],
            recv_sem=recv_sems.at[3 * t + j], device_id=(*chip, c), device_id_type=MESH)
            for t in range(len(s_refs)) for j, chip in enumerate(_other_chips(x, y))]

    return _Comm(tuple(ss), tuple(jax.ShapeDtypeStruct((3,) + s.shape[1:], s.dtype) for s in ss), 3 * len(ss), make)


def _exchange_window_plan(s):
    def make(s_refs, b_refs, send_sems, recv_sems):
        x, y, c = _place()
        return [pltpu.make_async_remote_copy(
            src_ref=s_refs[0].at[:, pl.ds(pl.multiple_of((2 * chip[0] + chip[1]) * WIN_STEP, 128), WIN_W)],
            dst_ref=b_refs[0].at[j], send_sem=send_sems.at[j], recv_sem=recv_sems.at[j], device_id=(*chip, c),
            device_id_type=MESH) for j, chip in enumerate(_other_chips(x, y))]

    return _Comm((s,), (jax.ShapeDtypeStruct((3, s.shape[0], WIN_W), s.dtype),), 3, make)


def _swap_plan(gs):
    def make(g_refs, b_refs, send_sems, recv_sems):
        x, y, c = _place()
        return [pltpu.make_async_remote_copy(
            src_ref=g_refs[t].at[1 - c], dst_ref=b_refs[t], send_sem=send_sems.at[t], recv_sem=recv_sems.at[t],
            device_id=(x, y, 1 - c), device_id_type=MESH) for t in range(len(g_refs))]

    return _Comm(tuple(gs), tuple(jax.ShapeDtypeStruct(g.shape[1:], g.dtype) for g in gs), len(gs), make)


def _spread_plan(parts):
    def make(p_refs, o_refs, send_sems, recv_sems):
        x, y, c = _place()
        copies = []
        for t in range(len(p_refs)):
            mine = o_refs[t].at[4 * x + 2 * y + c]
            copies.append(pltpu.make_async_copy(p_refs[t], mine, send_sems.at[7 * len(p_refs) + t]))
            for r in range(1, 8):
                peer = (1 - x if r & 4 else x, 1 - y if r & 2 else y, 1 - c if r & 1 else c)
                copies.append(pltpu.make_async_remote_copy(
                    src_ref=p_refs[t], dst_ref=mine, send_sem=send_sems.at[7 * t + r - 1],
                    recv_sem=recv_sems.at[7 * t + r - 1], device_id=peer, device_id_type=MESH))
        return copies

    return _Comm(tuple(parts), tuple(jax.ShapeDtypeStruct((8,) + p.shape, p.dtype) for p in parts), 8 * len(parts),
                 make)


def _mm_nt_acc(name, a, w, tk, acc_in=None, epilogue=None, extras=(), extra_specs=(), extra_out_shapes=(),
               extra_out_specs=(), comm=None):
    m, k = a.shape
    n = w.shape[0]
    nk, ni = k // tk, m // TB
    has_acc = acc_in is not None
    n_xc = len(comm.srcs) if comm else 0

    def body(*refs):
        a_ref, w_ref = refs[0], refs[1]
        pos = 2
        acc_ref = None
        if has_acc:
            acc_ref = refs[pos]
            pos += 1
        ex = refs[pos:pos + len(extras)]
        pos += len(extras)
        xc_src = refs[pos:pos + n_xc]
        pos += n_xc
        n_scr = 3 if n_xc else 1
        outs = refs[pos:len(refs) - n_scr - n_xc]
        xc_dst = refs[len(refs) - n_scr - n_xc:len(refs) - n_scr]
        scr = refs[len(refs) - n_scr]
        i, kk = pl.program_id(0), pl.program_id(1)
        if n_xc:
            copies = comm.make(xc_src, xc_dst, refs[-2], refs[-1])

            @pl.when((i == 0) & (kk == 0))
            def _():
                for cp in copies:
                    cp.start()

        @pl.when(kk == 0)
        def _():
            scr[...] = acc_ref[...] if has_acc else jnp.zeros_like(scr)

        scr[...] += _dot(a_ref[...], w_ref[...], NT)

        @pl.when(kk == nk - 1)
        def _():
            if epilogue is None:
                outs[0][...] = scr[...]
            else:
                epilogue(scr[...], outs, i, *ex)

        if n_xc:
            @pl.when((i == ni - 1) & (kk == nk - 1))
            def _():
                for cp in copies:
                    cp.wait()

    in_specs = [pl.BlockSpec((TB, tk), lambda i, kk: (i, kk)), pl.BlockSpec((n, tk), lambda i, kk: (0, kk))]
    args = [a, w]
    if has_acc:
        in_specs.append(pl.BlockSpec((TB, n), lambda i, kk: (i, 0)))
        args.append(acc_in)
    in_specs += list(extra_specs) + [ANY] * n_xc
    args += list(extras) + (list(comm.srcs) if comm else [])
    if epilogue is None:
        out_shape = [jax.ShapeDtypeStruct((m, n), F32)]
        out_specs = [pl.BlockSpec((TB, n), lambda i, kk: (i, 0))]
    else:
        out_shape, out_specs = list(extra_out_shapes), list(extra_out_specs)
    scratch = [pltpu.VMEM((TB, n), F32)]
    if n_xc:
        out_shape += list(comm.out_shapes)
        out_specs += [ANY] * n_xc
        scratch += _comm_sems(comm)
    return pl.pallas_call(
        body, name=name, grid=(ni, nk), in_specs=in_specs, out_specs=out_specs, out_shape=out_shape,
        scratch_shapes=scratch, compiler_params=_cparams(2),
    )(*args)


def _rms_bwd_epilogue(du, outs, i, h_ref, g_ref, dh1_ref):
    dh0_ref, dg_ref = outs
    h = h_ref[...]
    r = lax.rsqrt(jnp.mean(h * h, axis=-1, keepdims=True) + EPS)
    xh = h * r
    dxh = du * g_ref[...]
    dh0_ref[...] = dh1_ref[...] + r * (dxh - xh * jnp.mean(dxh * xh, axis=-1, keepdims=True))

    @pl.when(i == 0)
    def _():
        dg_ref[...] = jnp.zeros_like(dg_ref)

    dg_ref[...] += jnp.sum(du * xh, axis=0, keepdims=True)


def _mm_tn(name, a, b, bn, ncols=None, bcol0=0, into=None, col0=0, out_cols=None):
    t, m = a.shape
    n = ncols or b.shape[1]
    j0, bj0 = col0 // bn, bcol0 // bn

    def body(a_ref, b_ref, *rest):
        o_ref = rest[-1]

        @pl.when(pl.program_id(1) == 0)
        def _():
            o_ref[...] = jnp.zeros_like(o_ref)

        o_ref[...] += _dot(a_ref[...], b_ref[...], TN)

    in_specs = [pl.BlockSpec((TK, m), lambda j, kk: (kk, 0)), pl.BlockSpec((TK, bn), lambda j, kk: (kk, bj0 + j))]
    args = [a, b]
    aliases = {}
    if into is not None:
        in_specs.append(ANY)
        args.append(into)
        aliases = {2: 0}
        out_cols = into.shape[1]
    return pl.pallas_call(
        body, name=name, grid=(n // bn, t // TK), in_specs=in_specs,
        out_specs=pl.BlockSpec((m, bn), lambda j, kk: (0, j0 + j)),
        out_shape=jax.ShapeDtypeStruct((m, out_cols or n), F32), input_output_aliases=aliases,
        compiler_params=_cparams(2),
    )(*args)


def _place_merge_cols_call(dwp, dw_m):
    c0 = W_R + W_GP - 128
    tail = IN_PAD - c0
    rows = 256

    def body(m_ref, p_ref, o_ref, buf, low, sem):
        get = pltpu.make_async_copy(o_ref.at[:, pl.ds(c0, 128)], low, sem)
        get.start()
        get.wait()
        for r in range(0, D_MODEL, rows):
            buf[r:r + rows, :] = jnp.concatenate(
                [low[r:r + rows, :GATE_RANK], m_ref[r:r + rows, :],
                 jnp.zeros((rows, tail - GATE_RANK - W_M), F32)], axis=1)
        put = pltpu.make_async_copy(buf, o_ref.at[:, pl.ds(c0, tail)], sem)
        put.start()
        put.wait()

    return pl.pallas_call(
        body, name="place_merge_cols",
        in_specs=[pl.BlockSpec(memory_space=pltpu.VMEM), ANY], out_specs=ANY,
        out_shape=jax.ShapeDtypeStruct(dwp.shape, F32), input_output_aliases={1: 0},
        scratch_shapes=[pltpu.VMEM((D_MODEL, tail), F32), pltpu.VMEM((D_MODEL, 128), F32), pltpu.SemaphoreType.DMA],
        compiler_params=pltpu.CompilerParams(vmem_limit_bytes=VMEM_LIMIT),
    )(dw_m, dwp)


def _ret_fill_decay(lg_ref, dm_scr):
    c = TM
    ii = lax.broadcasted_iota(jnp.int32, (c, c), 0)
    jj = lax.broadcasted_iota(jnp.int32, (c, c), 1)
    rel = (ii - jj).astype(F32)
    for h in range(RET_HEADS):
        dm_scr[h] = jnp.where(rel >= 0, jnp.exp(jnp.maximum(rel, 0.0) * lg_ref[h]), 0.0)


def _ret_consts(lg, dm_ref):
    c = TM
    idx = lax.broadcasted_iota(jnp.int32, (c, 1), 0).astype(F32)
    xi = jnp.exp((idx + 1.0) * lg)
    zeta = jnp.exp((c - 1.0 - idx) * lg)
    gc = jnp.exp(jnp.full((1, 1), c, F32) * lg)
    return dm_ref[...], xi, zeta, gc


def _ret_fwd_call(rqk, rv, rg, gain, lgam):
    tp = rqk.shape[0]
    nc = tp // TM

    def body(lg_ref, qk_ref, v_ref, rg_ref, g_ref, o_ref, a_ref, st_ref, s_scr, dm_scr):
        @pl.when(pl.program_id(0) == 0)
        def _():
            s_scr[...] = jnp.zeros_like(s_scr)
            _ret_fill_decay(lg_ref, dm_scr)

        for h in range(RET_HEADS):
            dm, xi, zeta, gc = _ret_consts(lg_ref[h], dm_scr.at[h])
            q = qk_ref[:, h * RET_QK:(h + 1) * RET_QK]
            k = qk_ref[:, D_MODEL + h * RET_QK:D_MODEL + (h + 1) * RET_QK]
            v = v_ref[:, h * RET_V:(h + 1) * RET_V]
            sb = s_scr[h].astype(BF16)
            st_ref[0, h] = sb
            s = _dot(q, k, NT) * dm
            o = _dot(s.astype(BF16), v, NN) + xi * _dot(q, sb, NN)
            kz = (k.astype(F32) * zeta).astype(BF16)
            s_scr[h] = gc * s_scr[h] + _dot(kz, v, TN)
            o_ref[:, h * RET_V:(h + 1) * RET_V] = o
            mu = jnp.mean(o, axis=-1, keepdims=True)
            xc = o - mu
            xh = xc * lax.rsqrt(jnp.mean(xc * xc, axis=-1, keepdims=True) + EPS)
            g = rg_ref[:, h * RET_V:(h + 1) * RET_V]
            a_ref[:, h * RET_V:(h + 1) * RET_V] = (
                xh * g_ref[:, h * RET_V:(h + 1) * RET_V] * (g * _sigmoid(g))).astype(BF16)

    return pl.pallas_call(
        body, name="ret_fwd", grid=(nc,),
        in_specs=[pl.BlockSpec(memory_space=pltpu.SMEM),
                  pl.BlockSpec((TM, 2 * D_MODEL), lambda n: (n, 0)),
                  pl.BlockSpec((TM, RET_W), lambda n: (n, 0)),
                  pl.BlockSpec((TM, RET_W), lambda n: (n, 0)),
                  pl.BlockSpec((1, RET_W), lambda n: (0, 0))],
        out_specs=[pl.BlockSpec((TM, RET_W), lambda n: (n, 0)),
                   pl.BlockSpec((TM, RET_W), lambda n: (n, 0)),
                   pl.BlockSpec((1, RET_HEADS, RET_QK, RET_V), lambda n: (n, 0, 0, 0))],
        out_shape=[jax.ShapeDtypeStruct((tp, RET_W), F32), jax.ShapeDtypeStruct((tp, RET_W), BF16),
                   jax.ShapeDtypeStruct((nc, RET_HEADS, RET_QK, RET_V), BF16)],
        scratch_shapes=[pltpu.VMEM((RET_HEADS, RET_QK, RET_V), F32), pltpu.VMEM((RET_HEADS, TM, TM), F32)],
        compiler_params=_cparams(1),
    )(lgam, rqk, rv, rg, gain)


def _ret_bwd_call(rqk, rv, rg, o_ret, da, states, gain, lgam, cos, sin):
    tp = rqk.shape[0]
    nc = tp // TM
    half = RET_QK // 2

    def body(lg_ref, qk_ref, v_ref, rg_ref, o_ref, da_ref, st_ref, g_ref, cos_ref, sin_ref, dp_ref, dg_ref, ds_scr,
             dm_scr):
        @pl.when(pl.program_id(0) == 0)
        def _():
            ds_scr[...] = jnp.zeros_like(ds_scr)
            dg_ref[...] = jnp.zeros_like(dg_ref)
            _ret_fill_decay(lg_ref, dm_scr)

        cos, sin = cos_ref[...], sin_ref[...]
        for h in range(RET_HEADS):
            hs = slice(h * RET_V, (h + 1) * RET_V)
            dm, xi, zeta, gc = _ret_consts(lg_ref[h], dm_scr.at[h])
            o = o_ref[:, hs]
            mu = jnp.mean(o, axis=-1, keepdims=True)
            xc = o - mu
            rstd = lax.rsqrt(jnp.mean(xc * xc, axis=-1, keepdims=True) + EPS)
            xh = xc * rstd
            gain_h = g_ref[:, hs]
            g = rg_ref[:, hs]
            sg = _sigmoid(g)
            silu = g * sg
            dah = da_ref[:, hs]
            dp_ref[:, 4 * D_MODEL + h * RET_V:4 * D_MODEL + (h + 1) * RET_V] = (
                dah * (xh * gain_h) * (sg * (1.0 + g * (1.0 - sg)))).astype(BF16)
            dn = dah * silu
            dg_ref[:, hs] += jnp.sum(dn * xh, axis=0, keepdims=True)
            dxh = dn * gain_h
            do = rstd * (dxh - jnp.mean(dxh, axis=-1, keepdims=True)
                         - xh * jnp.mean(dxh * xh, axis=-1, keepdims=True))
            dob = do.astype(BF16)
            q = qk_ref[:, h * RET_QK:(h + 1) * RET_QK]
            k = qk_ref[:, D_MODEL + h * RET_QK:D_MODEL + (h + 1) * RET_QK]
            v = v_ref[:, hs]
            sp = st_ref[0, h]
            ds = ds_scr[h]
            dsb = ds.astype(BF16)
            s = (_dot(q, k, NT) * dm).astype(BF16)
            dsc = (_dot(dob, v, NT) * dm).astype(BF16)
            dq = _dot(dsc, k, NN) + xi * _dot(dob, sp, NT)
            dk = _dot(dsc, q, TN) + zeta * _dot(v, dsb, NT)
            kz = (k.astype(F32) * zeta).astype(BF16)
            dv = _dot(s, dob, TN) + _dot(kz, dsb, NN)
            qx = (q.astype(F32) * xi).astype(BF16)
            ds_scr[h] = gc * ds + _dot(qx, dob, TN)
            dp_ref[:, 2 * D_MODEL + h * RET_V:2 * D_MODEL + (h + 1) * RET_V] = dv.astype(BF16)
            dk = dk * (RET_QK ** -0.5)
            for base, t in ((0, dq), (D_MODEL, dk)):
                t1, t2 = t[:, :half], t[:, half:]
                dp_ref[:, base + h * RET_QK:base + h * RET_QK + half] = (t1 * cos + t2 * sin).astype(BF16)
                dp_ref[:, base + h * RET_QK + half:base + (h + 1) * RET_QK] = (t2 * cos - t1 * sin).astype(BF16)

    rev = lambda n: (nc - 1 - n, 0)
    return pl.pallas_call(
        body, name="ret_bwd", grid=(nc,),
        in_specs=[pl.BlockSpec(memory_space=pltpu.SMEM),
                  pl.BlockSpec((TM, 2 * D_MODEL), rev),
                  pl.BlockSpec((TM, RET_W), rev),
                  pl.BlockSpec((TM, RET_W), rev),
                  pl.BlockSpec((TM, RET_W), rev),
                  pl.BlockSpec((TM, RET_W), rev),
                  pl.BlockSpec((1, RET_HEADS, RET_QK, RET_V), lambda n: (nc - 1 - n, 0, 0, 0)),
                  pl.BlockSpec((1, RET_W), lambda n: (0, 0)),
                  pl.BlockSpec((TM, half), rev),
                  pl.BlockSpec((TM, half), rev)],
        out_specs=[pl.BlockSpec((TM, W_R), rev), pl.BlockSpec((1, RET_W), lambda n: (0, 0))],
        out_shape=[jax.ShapeDtypeStruct((tp, W_R), BF16), jax.ShapeDtypeStruct((1, RET_W), F32)],
        scratch_shapes=[pltpu.VMEM((RET_HEADS, RET_QK, RET_V), F32), pltpu.VMEM((RET_HEADS, TM, TM), F32)],
        compiler_params=_cparams(1),
    )(lgam, rqk, rv, rg, o_ret, da, states, gain, cos, sin)


GLA_LEVELS = tuple(GC >> (s + 1) for s in range(int(math.log2(GC // GLA_SUB))))
NLEV = len(GLA_LEVELS)


def _gla_tril():
    return np.tril(np.ones((GC, GC), np.float32))


def _gla_masks():
    ii = lax.broadcasted_iota(jnp.int32, (GC, GC), 0)
    jj = lax.broadcasted_iota(jnp.int32, (GC, GC), 1)
    masks = []
    for m in GLA_LEVELS:
        sh = int(math.log2(2 * m))
        masks.append(((ii >> sh) == (jj >> sh)) & ((ii & m) != 0) & ((jj & m) == 0))
    sh = int(math.log2(GLA_SUB))
    md = ((ii >> sh) == (jj >> sh)) & (jj <= ii)
    row = lax.broadcasted_iota(jnp.int32, (GC, 1), 0)
    second = [(row & m) != 0 for m in GLA_LEVELS]
    return masks, md, second


def _gla_log_decay(glr_ref, wg_ref, bg_ref):
    z = _dot(glr_ref[...].astype(BF16), wg_ref[...], NN) + bg_ref[...]
    la = (jnp.minimum(z, 0.0) - jnp.log1p(jnp.exp(-jnp.abs(z)))) * (1.0 / GATE_TAU)
    return z, la


def _gla_row_steps(b_ref, cs, rows, size):
    parts = [jnp.zeros((size, GLA_K), F32) if r is None else jnp.broadcast_to(b_ref[r:r + 1, cs], (size, GLA_K))
             for r in rows]
    return parts[0] if len(parts) == 1 else jnp.concatenate(parts, axis=0)


def _gla_factors(b_ref, h, second):
    cs = slice(h * GLA_K, (h + 1) * GLA_K)
    b = b_ref[:, cs]
    fq, fk = [], []
    for l, m in enumerate(GLA_LEVELS):
        d = b - _gla_row_steps(b_ref, cs, [s + m - 1 for s in range(0, GC, 2 * m)], 2 * m)
        f = jnp.exp(jnp.where(second[l], d, -d))
        fq.append(jnp.where(second[l], f, 0.0))
        fk.append(jnp.where(second[l], 0.0, f))
    dd = b - _gla_row_steps(b_ref, cs, [None] + [s - 1 for s in range(GLA_SUB, GC, GLA_SUB)], GLA_SUB)
    ed = jnp.exp(dd)
    edi = jnp.exp(-dd)
    eb = jnp.exp(b)
    bl = b_ref[GC - 1:GC, cs]
    ee = jnp.exp(bl - b)
    ebl = jnp.exp(bl)
    return fq, fk, ed, edi, eb, ee, ebl


def _gla_scores(q, k, fq, fk, ed, edi, masks, md):
    qt = [(q * f).astype(BF16) for f in fq]
    kt = [(k * f).astype(BF16) for f in fk]
    qd = (q * ed).astype(BF16)
    kd = (k * edi).astype(BF16)
    a = jnp.where(md, _dot(qd, kd, NT), 0.0)
    for l in range(NLEV):
        a = a + jnp.where(masks[l], _dot(qt[l], kt[l], NT), 0.0)
    return a, qt, kt, qd, kd


def _gla_fwd_call(gqk, gv, glr, gg, wg, bg, gain, pmat, comm=None):
    tp = gqk.shape[0]
    nc = tp // GC
    n_xc = len(comm.srcs) if comm else 0

    def body(qk_ref, v_ref, glr_ref, gg_ref, wg_ref, bg_ref, g_ref, p_ref, *rest):
        xc_src = rest[:n_xc]
        o_ref, a_ref, st_ref = rest[n_xc:n_xc + 3]
        xc_dst = rest[n_xc + 3:2 * n_xc + 3]
        s_scr, b_scr = rest[2 * n_xc + 3:2 * n_xc + 5]
        n = pl.program_id(0)
        if n_xc:
            copies = comm.make(xc_src, xc_dst, rest[-2], rest[-1])

            @pl.when(n == 0)
            def _():
                for cp in copies:
                    cp.start()

            @pl.when(n == nc - 1)
            def _():
                for cp in copies:
                    cp.wait()

        @pl.when(n == 0)
        def _():
            s_scr[...] = jnp.zeros_like(s_scr)

        _, la = _gla_log_decay(glr_ref, wg_ref, bg_ref)
        b_scr[...] = _exact_pm(p_ref[...], la)
        masks, md, second = _gla_masks()
        for h in range(GLA_HEADS):
            q = qk_ref[:, h * GLA_K:(h + 1) * GLA_K]
            k = qk_ref[:, GLA_KW + h * GLA_K:GLA_KW + (h + 1) * GLA_K]
            vs = slice(h * GLA_V, (h + 1) * GLA_V)
            v = v_ref[:, vs]
            fq, fk, ed, edi, eb, ee, ebl = _gla_factors(b_scr, h, second)
            a, *_ = _gla_scores(q, k, fq, fk, ed, edi, masks, md)
            sb = s_scr[h].astype(BF16)
            st_ref[0, h] = sb
            o = _dot(a.astype(BF16), v, NN) + _dot((q * eb).astype(BF16), sb, NT)
            s_scr[h] = s_scr[h] * ebl + _dot(v, (k * ee).astype(BF16), TN)
            o_ref[:, vs] = o
            xh = o * lax.rsqrt(jnp.mean(o * o, axis=-1, keepdims=True) + EPS)
            g = gg_ref[:, vs]
            a_ref[:, vs] = (xh * g_ref[:, vs] * (g * _sigmoid(g))).astype(BF16)

    return pl.pallas_call(
        body, name="gla_fwd", grid=(nc,),
        in_specs=[pl.BlockSpec((GC, 2 * GLA_KW), lambda n: (n, 0)),
                  pl.BlockSpec((GC, GLA_W), lambda n: (n, 0)),
                  pl.BlockSpec((GC, 128), lambda n: (n, 0)),
                  pl.BlockSpec((GC, GLA_W), lambda n: (n, 0)),
                  pl.BlockSpec((128, GLA_KW), lambda n: (0, 0)),
                  pl.BlockSpec((1, GLA_KW), lambda n: (0, 0)),
                  pl.BlockSpec((1, GLA_W), lambda n: (0, 0)),
                  pl.BlockSpec((GC, GC), lambda n: (0, 0))] + [ANY] * n_xc,
        out_specs=[pl.BlockSpec((GC, GLA_W), lambda n: (n, 0)),
                   pl.BlockSpec((GC, GLA_W), lambda n: (n, 0)),
                   pl.BlockSpec((1, GLA_HEADS, GLA_V, GLA_K), lambda n: (n, 0, 0, 0))] + [ANY] * n_xc,
        out_shape=[jax.ShapeDtypeStruct((tp, GLA_W), F32), jax.ShapeDtypeStruct((tp, GLA_W), BF16),
                   jax.ShapeDtypeStruct((nc, GLA_HEADS, GLA_V, GLA_K), BF16)] + (list(comm.out_shapes) if comm else []),
        scratch_shapes=[pltpu.VMEM((GLA_HEADS, GLA_V, GLA_K), F32), pltpu.VMEM((GC, GLA_KW), F32)]
        + (_comm_sems(comm) if comm else []),
        compiler_params=_cparams(1),
    )(gqk, gv, glr, gg, wg, bg, gain, pmat, *(comm.srcs if comm else ()))


def _gla_bwd_call(gqk, gv, glr, gg, o_gla, da, states, wg, bg, gain, pmat, pmat_t, comm=None):
    tp = gqk.shape[0]
    nc = tp // GC
    o_gv, o_gg, o_lr = 2 * GLA_KW, 2 * GLA_KW + GLA_W, 2 * GLA_KW + 2 * GLA_W
    n_xc = len(comm.srcs) if comm else 0

    def body(qk_ref, v_ref, glr_ref, gg_ref, o_ref, da_ref, st_ref, wg_ref, bg_ref, g_ref, p_ref, pt_ref, *rest):
        xc_src = rest[:n_xc]
        dp_ref, dwg_ref, dbg_ref, dg_ref = rest[n_xc:n_xc + 4]
        xc_dst = rest[n_xc + 4:2 * n_xc + 4]
        ds_scr, b_scr, db_scr = rest[2 * n_xc + 4:2 * n_xc + 7]
        n = pl.program_id(0)
        if n_xc:
            copies = comm.make(xc_src, xc_dst, rest[-2], rest[-1])

            @pl.when(n == 0)
            def _():
                for cp in copies:
                    cp.start()

            @pl.when(n == nc - 1)
            def _():
                for cp in copies:
                    cp.wait()

        @pl.when(n == 0)
        def _():
            ds_scr[...] = jnp.zeros_like(ds_scr)
            dwg_ref[...] = jnp.zeros_like(dwg_ref)
            dbg_ref[...] = jnp.zeros_like(dbg_ref)
            dg_ref[...] = jnp.zeros_like(dg_ref)

        z, la = _gla_log_decay(glr_ref, wg_ref, bg_ref)
        b_scr[...] = _exact_pm(p_ref[...], la)
        masks, md, second = _gla_masks()
        for h in range(GLA_HEADS):
            cs = slice(h * GLA_K, (h + 1) * GLA_K)
            vs = slice(h * GLA_V, (h + 1) * GLA_V)
            o = o_ref[:, vs]
            rstd = lax.rsqrt(jnp.mean(o * o, axis=-1, keepdims=True) + EPS)
            xh = o * rstd
            gain_h = g_ref[:, vs]
            g = gg_ref[:, vs]
            sg = _sigmoid(g)
            dah = da_ref[:, vs]
            dp_ref[:, o_gg + h * GLA_V:o_gg + (h + 1) * GLA_V] = (
                dah * (xh * gain_h) * (sg * (1.0 + g * (1.0 - sg)))).astype(BF16)
            dn = dah * (g * sg)
            dg_ref[:, vs] += jnp.sum(dn * xh, axis=0, keepdims=True)
            dxh = dn * gain_h
            do = rstd * (dxh - xh * jnp.mean(dxh * xh, axis=-1, keepdims=True))
            dob = do.astype(BF16)
            q = qk_ref[:, cs]
            k = qk_ref[:, GLA_KW + h * GLA_K:GLA_KW + (h + 1) * GLA_K]
            v = v_ref[:, vs]
            fq, fk, ed, edi, eb, ee, ebl = _gla_factors(b_scr, h, second)
            a, qt, kt, qd, kd = _gla_scores(q, k, fq, fk, ed, edi, masks, md)
            sp = st_ref[0, h]
            ds = ds_scr[h]
            dsb = ds.astype(BF16)
            q_in = q * eb
            k_end = k * ee
            da_s = _dot(dob, v, NT)
            dv = _dot(a.astype(BF16), dob, TN) + _dot(k_end.astype(BF16), dsb, NT)
            dq_in = _dot(dob, sp, NN)
            dk_end = _dot(v, dsb, NN)
            dbl = jnp.sum(sp.astype(F32) * ds, axis=0, keepdims=True) * ebl
            ds_scr[h] = ds * ebl + _dot(dob, q_in.astype(BF16), TN)
            dq = dq_in * eb
            dk = dk_end * ee
            de_end = dk_end * k_end
            db = dq_in * q_in - de_end
            placed = [(GC - 1, jnp.sum(de_end, axis=0, keepdims=True) + dbl)]
            for l, m in enumerate(GLA_LEVELS):
                dal = jnp.where(masks[l], da_s, 0.0).astype(BF16)
                dqt = _dot(dal, kt[l], NN)
                dkt = _dot(dal, qt[l], TN)
                dq = dq + dqt * fq[l]
                dk = dk + dkt * fk[l]
                gl = dqt * (q * fq[l]) - dkt * (k * fk[l])
                db = db + gl
                placed += [(s + m - 1, -jnp.sum(gl[s:s + 2 * m], axis=0, keepdims=True)) for s in range(0, GC, 2 * m)]
            dad = jnp.where(md, da_s, 0.0).astype(BF16)
            dqd = _dot(dad, kd, NN)
            dkd = _dot(dad, qd, TN)
            dq = dq + dqd * ed
            dk = dk + dkd * edi
            gd = dqd * (q * ed) - dkd * (k * edi)
            db = db + gd
            placed += [(s - 1, -jnp.sum(gd[s:s + GLA_SUB], axis=0, keepdims=True)) for s in range(GLA_SUB, GC, GLA_SUB)]
            db_scr[:, cs] = db
            for r, val in placed:
                db_scr[r:r + 1, cs] += val
            dp_ref[:, cs] = (dq * (GLA_K ** -0.5)).astype(BF16)
            dp_ref[:, GLA_KW + h * GLA_K:GLA_KW + (h + 1) * GLA_K] = dk.astype(BF16)
            dp_ref[:, o_gv + h * GLA_V:o_gv + (h + 1) * GLA_V] = dv.astype(BF16)
        dla = _exact_pm(pt_ref[...], db_scr[...])
        row = (nc - 1 - n) * GC + lax.broadcasted_iota(jnp.int32, (GC, 1), 0)
        dz = jnp.where(row >= PADF, dla * (1.0 / GATE_TAU) * _sigmoid(-z), 0.0)
        dzb = dz.astype(BF16)
        dp_ref[:, o_lr:] = _dot(dzb, wg_ref[...], NT).astype(BF16)
        dwg_ref[...] += _dot(glr_ref[...].astype(BF16), dzb, TN)
        dbg_ref[...] += jnp.sum(dz, axis=0, keepdims=True)

    rev = lambda n: (nc - 1 - n, 0)
    const = lambda n: (0, 0)
    xc_shapes, xc_sems = (list(comm.out_shapes), _comm_sems(comm)) if n_xc else ([], [])
    return pl.pallas_call(
        body, name="gla_bwd", grid=(nc,),
        in_specs=[pl.BlockSpec((GC, 2 * GLA_KW), rev),
                  pl.BlockSpec((GC, GLA_W), rev),
                  pl.BlockSpec((GC, 128), rev),
                  pl.BlockSpec((GC, GLA_W), rev),
                  pl.BlockSpec((GC, GLA_W), rev),
                  pl.BlockSpec((GC, GLA_W), rev),
                  pl.BlockSpec((1, GLA_HEADS, GLA_V, GLA_K), lambda n: (nc - 1 - n, 0, 0, 0)),
                  pl.BlockSpec((128, GLA_KW), const),
                  pl.BlockSpec((1, GLA_KW), const),
                  pl.BlockSpec((1, GLA_W), const),
                  pl.BlockSpec((GC, GC), const),
                  pl.BlockSpec((GC, GC), const)] + [ANY] * n_xc,
        out_specs=[pl.BlockSpec((GC, W_GP), rev), pl.BlockSpec((128, GLA_KW), const),
                   pl.BlockSpec((1, GLA_KW), const), pl.BlockSpec((1, GLA_W), const)] + [ANY] * n_xc,
        out_shape=[jax.ShapeDtypeStruct((tp, W_GP), BF16), jax.ShapeDtypeStruct((128, GLA_KW), F32),
                   jax.ShapeDtypeStruct((1, GLA_KW), F32), jax.ShapeDtypeStruct((1, GLA_W), F32)] + xc_shapes,
        scratch_shapes=[pltpu.VMEM((GLA_HEADS, GLA_V, GLA_K), F32), pltpu.VMEM((GC, GLA_KW), F32),
                        pltpu.VMEM((GC, GLA_KW), F32)] + xc_sems,
        compiler_params=_cparams(1),
    )(gqk, gv, glr, gg, o_gla, da, states, wg, bg, gain, pmat, pmat_t, *(comm.srcs if comm else ()))


def _mid_call(a_ret, a_gla, mg, h0, tgt, wbr, wbg, wout, gf):
    tp = h0.shape[0]
    nt = tp // TM

    def body(ar_ref, ag_ref, mg_ref, h_ref, t_ref, wbr_ref, wbg_ref, wo_ref, gf_ref,
             dh1_ref, dar_ref, dag_ref, dm_ref, mb_ref, dh1b_ref, dprb_ref, dpgb_ref, loss_ref, dgf_ref):
        i = pl.program_id(0)

        @pl.when(i == 0)
        def _():
            loss_ref[...] = jnp.zeros_like(loss_ref)
            dgf_ref[...] = jnp.zeros_like(dgf_ref)

        ar, ag = ar_ref[...], ag_ref[...]
        pr = _dot(ar, wbr_ref[...], NN)
        pg = _dot(ag, wbg_ref[...], NN)
        sr = _sigmoid(mg_ref[:, :D_MODEL])
        sg = _sigmoid(mg_ref[:, D_MODEL:])
        merged = (sr * pr + sg * pg).astype(BF16)
        mb_ref[...] = merged
        h1 = h_ref[...] + _dot(merged, wo_ref[...], NN)
        r1 = lax.rsqrt(jnp.mean(h1 * h1, axis=-1, keepdims=True) + EPS)
        xh = h1 * r1
        gfv = gf_ref[...]
        live = jnp.where(i > 0, 1.0, 0.0).astype(F32)
        err = (xh * gfv - t_ref[...]) * live
        loss_ref[...] += jnp.full(loss_ref.shape, 0.5 / D_MODEL, F32) * jnp.sum(err * err)
        dy = err * (1.0 / D_MODEL)
        dgf_ref[...] += jnp.sum(dy * xh, axis=0, keepdims=True)
        dxh = dy * gfv
        dh1 = r1 * (dxh - xh * jnp.mean(dxh * xh, axis=-1, keepdims=True))
        dh1_ref[...] = dh1
        dh1b = dh1.astype(BF16)
        dh1b_ref[...] = dh1b
        dmerged = _dot(dh1b, wo_ref[...], NT)
        dm_ref[:, :D_MODEL] = (dmerged * pr * sr * (1.0 - sr)).astype(BF16)
        dm_ref[:, D_MODEL:] = (dmerged * pg * sg * (1.0 - sg)).astype(BF16)
        dpr = (dmerged * sr).astype(BF16)
        dpg = (dmerged * sg).astype(BF16)
        dprb_ref[...] = dpr
        dpgb_ref[...] = dpg
        dar_ref[...] = _dot(dpr, wbr_ref[...], NT)
        dag_ref[...] = _dot(dpg, wbg_ref[...], NT)

    tile = lambda w: pl.BlockSpec((TM, w), lambda i: (i, 0))
    const = lambda r, w: pl.BlockSpec((r, w), lambda i: (0, 0))
    return pl.pallas_call(
        body, name="merge_out_loss", grid=(nt,),
        in_specs=[tile(RET_W), tile(GLA_W), tile(W_M), tile(D_MODEL),
                  pl.BlockSpec((TM, D_MODEL), lambda i: (jnp.maximum(i - 1, 0), 0)),
                  const(RET_W, D_MODEL), const(GLA_W, D_MODEL), const(D_MODEL, D_MODEL), const(1, D_MODEL)],
        out_specs=[tile(D_MODEL), tile(RET_W), tile(GLA_W), tile(W_M), tile(D_MODEL), tile(D_MODEL), tile(D_MODEL),
                   tile(D_MODEL), const(1, 128), const(1, D_MODEL)],
        out_shape=[jax.ShapeDtypeStruct((tp, D_MODEL), F32), jax.ShapeDtypeStruct((tp, RET_W), F32),
                   jax.ShapeDtypeStruct((tp, GLA_W), F32), jax.ShapeDtypeStruct((tp, W_M), BF16),
                   jax.ShapeDtypeStruct((tp, D_MODEL), BF16), jax.ShapeDtypeStruct((tp, D_MODEL), BF16),
                   jax.ShapeDtypeStruct((tp, D_MODEL), BF16), jax.ShapeDtypeStruct((tp, D_MODEL), BF16),
                   jax.ShapeDtypeStruct((1, 128), F32), jax.ShapeDtypeStruct((1, D_MODEL), F32)],
        compiler_params=_cparams(1),
    )(a_ret, a_gla, mg, h0, tgt, wbr, wbg, wout, gf)


def _device_step(x2d, tgt2d, meta, norm_gain, w_in_bf, w_gate_up, b_gate, ret_gain, gla_gain, branch_parts,
                 final_gain, ck):
    seq = x2d.shape[0]
    tp = T0 + seq
    head = jnp.concatenate([jnp.zeros((PADF, D_MODEL), F32), meta], axis=0)
    w_r = w_in_bf
    w_g = jnp.pad(w_in_bf[:, W_R:W_R + W_G], ((0, 0), (0, W_GP - W_G)))
    w_m = w_in_bf[:, W_R + W_G:]
    wg_pad = jnp.pad(w_gate_up, ((0, 128 - GATE_RANK), (0, 0))).astype(BF16)

    pos = jnp.arange(tp, dtype=F32) - PADF
    half = RET_QK // 2
    inv = ROPE_BASE ** (-jnp.arange(half, dtype=F32) / half)
    ang = pos[:, None] * inv[None, :]
    cos, sin = jnp.cos(ang), jnp.sin(ang)
    lgam = jnp.log1p(-(2.0 ** (-5.0 - jnp.arange(RET_HEADS, dtype=F32))))
    pmat = jnp.asarray(_gla_tril(), BF16)
    pmat_t = jnp.asarray(_gla_tril().T.copy(), BF16)

    h0, u = _rms_call(x2d, head, norm_gain)
    tab = pl.BlockSpec((_proj_rows(tp), half), lambda j, i: (i, 0))
    rqk = _mm_nn("proj_rqk", u, w_r, BF16, D_MODEL, 0, 2 * D_MODEL, _rope_epilogue, (cos, sin), (tab, tab))
    rv = _mm_nn("proj_rv", u, w_r, BF16, D_MODEL, 2 * D_MODEL, RET_W)
    rg = _mm_nn("proj_rg", u, w_r, F32, D_MODEL, 4 * D_MODEL, RET_W)
    gqk = _mm_nn("proj_gqk", u, w_g, F32, 2 * GLA_KW, 0, 2 * GLA_KW, _gqk_epilogue)
    gv = _mm_nn("proj_gv", u, w_g, BF16, GLA_W, 2 * GLA_KW, GLA_W)
    gg = _mm_nn("proj_gg", u, w_g, F32, GLA_W, 2 * GLA_KW + GLA_W, GLA_W)
    glr = _mm_nn("proj_glr", u, w_g, F32, 128, 2 * GLA_KW + 2 * GLA_W, 128)
    mg = _mm_nn("proj_mg", u, w_m, F32, D_MODEL, 0, W_M)

    o_ret, a_ret, st_ret = _ret_fwd_call(rqk, rv, rg, ret_gain, lgam)
    o_gla, a_gla, st_gla, g_br, g_bg, g_out = _gla_fwd_call(gqk, gv, glr, gg, wg_pad, b_gate, gla_gain, pmat,
                                                            comm=_spread_plan(branch_parts))
    wbr = g_br.reshape(RET_W, D_MODEL)
    wbg = g_bg.reshape(GLA_W, D_MODEL)
    wout = g_out.reshape(D_MODEL, D_MODEL)

    gf = final_gain.reshape(1, D_MODEL)
    (dh1, da_ret, da_gla, dm, merged_b, dh1_b, dpr_b, dpg_b, loss, dgf) = _mid_call(
        a_ret, a_gla, mg, h0, tgt2d, wbr, wbg, wout, gf)

    names_b = ("w_branch_ret", "w_branch_gla", "w_out")
    g2_b = [_mm_tn("dw_br", a_ret, dpr_b, D_MODEL).reshape(4, 2, RET_W // 8, D_MODEL).transpose(1, 0, 2, 3),
            _mm_tn("dw_bg", a_gla, dpg_b, D_MODEL).reshape(4, 2, GLA_W // 8, D_MODEL).transpose(1, 0, 2, 3),
            _mm_tn("dw_out", merged_b, dh1_b, D_MODEL).reshape(4, 2, D_MODEL // 8, D_MODEL).transpose(1, 0, 2, 3)]
    sib_b = _swap_halves_call("swap_halves_branch", g2_b)
    sum_b = [_add_half_call("add_half_" + nm, g, b, ck) for nm, g, b in zip(names_b, g2_b, sib_b)]
    d_g, dwg, dbg, dgla_gain, *chips_b = _gla_bwd_call(gqk, gv, glr, gg, o_gla, da_gla, st_gla, wg_pad, b_gate,
                                                       gla_gain, pmat, pmat_t, comm=_exchange_plan(sum_b))
    mine = [_add_chips_call("add_chips_" + nm, g, b, p, ck) for nm, g, b, p in zip(names_b, g2_b, sib_b, chips_b)]

    d_r, dret_gain = _ret_bwd_call(rqk, rv, rg, o_ret, da_ret, st_ret, ret_gain, lgam, cos, sin)

    dwp = _mm_tn("dw_r", u, d_r, 2 * D_MODEL, out_cols=IN_PAD)
    dwp = _mm_tn("dw_g", u, d_g, D_MODEL, ncols=W_GP - 128, into=dwp, col0=W_R)
    dwp = _mm_tn("dw_glr", u, d_g, 128, ncols=128, bcol0=W_GP - 128, into=dwp, col0=W_R + W_GP - 128)
    g2_in = _place_merge_cols_call(dwp, _mm_tn("dw_m", u, dm, 2 * D_MODEL)).reshape(2, D_MODEL // 2, IN_PAD)

    du, sib_in = _mm_nt_acc("du_m", dm, w_m, W_M, comm=_swap_plan([g2_in]))
    sum_in = _add_rows_call("add_half_w_in", g2_in, sib_in, ck)
    du = _mm_nt_acc("du_g", d_g, w_g, W_GP, acc_in=du)[0]
    tile = pl.BlockSpec((TB, D_MODEL), lambda i, kk: (i, 0))
    row = pl.BlockSpec((1, D_MODEL), lambda i, kk: (0, 0))
    dh0, dnorm_gain, chips_in = _mm_nt_acc(
        "du_r", d_r, w_r, 2 * D_MODEL, acc_in=du, epilogue=_rms_bwd_epilogue, extras=(h0, norm_gain, dh1),
        extra_specs=(tile, row, tile),
        extra_out_shapes=(jax.ShapeDtypeStruct((tp, D_MODEL), F32), jax.ShapeDtypeStruct((1, D_MODEL), F32)),
        extra_out_specs=(tile, row), comm=_exchange_window_plan(sum_in))
    mine = [_add_window_call("add_chips_w_in", g2_in, sib_in, chips_in, ck)] + mine
    full = _join_halves_call("join_halves", mine)

    return dict(loss=loss[0, 0], dx=dh0[T0:], dmeta=dh0[PADF:T0], norm_gain=dnorm_gain, w_gate_up=dwg[:GATE_RANK], b_gate=dbg,
                ret_norm_gain=dret_gain, gla_norm_gain=dgla_gain, final_norm_gain=dgf.reshape(D_MODEL),
                w_in=full[0], w_branch_ret=full[1], w_branch_gla=full[2], w_out=full[3])


MESH = pl.DeviceIdType.MESH
ANY = pl.BlockSpec(memory_space=pl.ANY)


def _place():
    return lax.axis_index("x"), lax.axis_index("y"), lax.axis_index("c")


def _gather8_call(name, parts):
    n = len(parts)

    def body(*refs):
        x_refs, out_refs = refs[:n], refs[n:2 * n]
        send_sems, recv_sems, local_sems = refs[2 * n:]
        x, y, c = _place()
        me, sibling = (x, y, c), (x, y, 1 - c)
        chips = [(1 - x, y), (x, 1 - y), (1 - x, 1 - y)]

        def slot(t, px, py, pc):
            return out_refs[t].at[4 * px + 2 * py + pc]

        def copy(t, k, block, to, src=None):
            return pltpu.make_async_remote_copy(
                src_ref=slot(t, *block) if src is None else src, dst_ref=slot(t, *block),
                send_sem=send_sems.at[7 * t + k], recv_sem=recv_sems.at[7 * t + k], device_id=to, device_id_type=MESH)

        mine = [pltpu.make_async_copy(x_refs[t], slot(t, *me), local_sems.at[t]) for t in range(n)]
        for cp in mine:
            cp.start()
        first = []
        for t in range(n):
            first.append(copy(t, 0, me, sibling, src=x_refs[t]))
            first += [copy(t, 1 + j, me, (*chip, c), src=x_refs[t]) for j, chip in enumerate(chips)]
        for cp in first:
            cp.start()
        passed = []
        for j, chip in enumerate(chips):
            for t in range(n):
                copy(t, 1 + j, (*chip, c), me).wait_recv()
                fwd = copy(t, 4 + j, (*chip, c), sibling)
                fwd.start()
                passed.append(fwd)
        for t in range(n):
            copy(t, 0, sibling, me).wait_recv()
            for j, chip in enumerate(chips):
                copy(t, 4 + j, (*chip, 1 - c), me).wait_recv()
        for cp in first + passed:
            cp.wait_send()
        for cp in mine:
            cp.wait()

    return pl.pallas_call(
        body, name=name,
        out_shape=[jax.ShapeDtypeStruct((8,) + p.shape, p.dtype) for p in parts],
        in_specs=[ANY] * n, out_specs=[ANY] * n,
        scratch_shapes=[pltpu.SemaphoreType.DMA((7 * n,)), pltpu.SemaphoreType.DMA((7 * n,)),
                        pltpu.SemaphoreType.DMA((n,))],
    )(*parts)


def _swap_halves_call(name, gs):
    n = len(gs)

    def body(*refs):
        g_refs, b_refs = refs[:n], refs[n:2 * n]
        send_sems, recv_sems = refs[2 * n:]
        x, y, c = _place()
        copies = [pltpu.make_async_remote_copy(
            src_ref=g_refs[t].at[1 - c], dst_ref=b_refs[t], send_sem=send_sems.at[t], recv_sem=recv_sems.at[t],
            device_id=(x, y, 1 - c), device_id_type=MESH) for t in range(n)]
        for cp in copies:
            cp.start()
        for cp in copies:
            cp.wait()

    return pl.pallas_call(
        body, name=name,
        out_shape=[jax.ShapeDtypeStruct(g.shape[1:], g.dtype) for g in gs],
        in_specs=[ANY] * n, out_specs=[ANY] * n,
        scratch_shapes=[pltpu.SemaphoreType.DMA((n,)), pltpu.SemaphoreType.DMA((n,))],
    )(*gs)


def _join_halves_call(name, ts):
    n = len(ts)

    def body(*refs):
        o_refs = refs[n:2 * n]
        send_sems, recv_sems = refs[2 * n:]
        x, y, c = _place()
        copies = [pltpu.make_async_remote_copy(
            src_ref=o_refs[t].at[c], dst_ref=o_refs[t].at[c], send_sem=send_sems.at[t], recv_sem=recv_sems.at[t],
            device_id=(x, y, 1 - c), device_id_type=MESH) for t in range(n)]
        for cp in copies:
            cp.start()
        for t in range(n):
            copies[t].wait_send()
            pltpu.make_async_remote_copy(
                src_ref=o_refs[t].at[c], dst_ref=o_refs[t].at[1 - c], send_sem=send_sems.at[t],
                recv_sem=recv_sems.at[t], device_id=(x, y, 1 - c), device_id_type=MESH).wait_recv()

    return pl.pallas_call(
        body, name=name,
        out_shape=[jax.ShapeDtypeStruct(t.shape, t.dtype) for t in ts],
        in_specs=[ANY] * n, out_specs=[ANY] * n, input_output_aliases={t: t for t in range(n)},
        scratch_shapes=[pltpu.SemaphoreType.DMA((n,)), pltpu.SemaphoreType.DMA((n,))],
    )(*ts)


def _row_block(rows, cols, budget):
    best = 8
    for rb in range(8, rows + 1, 8):
        if rows % rb == 0 and rb * cols * 4 <= budget:
            best = rb
    return best


def _add_half_call(name, g, b, ck):
    _, _, r, cc = g.shape
    rb = _row_block(r, cc, 2 * 1024 * 1024)

    def body(ck_ref, g_ref, b_ref, o_ref):
        o_ref[...] = (g_ref[...] + b_ref[...]).astype(BF16)

    return pl.pallas_call(
        body, name=name,
        grid_spec=pltpu.PrefetchScalarGridSpec(
            num_scalar_prefetch=1, grid=(4, r // rb),
            in_specs=[pl.BlockSpec((None, None, rb, cc), lambda k, i, ck_ref: (ck_ref[0], k, i, 0)),
                      pl.BlockSpec((None, rb, cc), lambda k, i, ck_ref: (k, i, 0))],
            out_specs=pl.BlockSpec((None, rb, cc), lambda k, i, ck_ref: (k, i, 0))),
        out_shape=jax.ShapeDtypeStruct(b.shape, BF16),
        compiler_params=_cparams(2),
    )(ck, g, b)


def _add_rows_call(name, g, b, ck):
    _, r, cc = g.shape
    rb = _row_block(r, cc, 2 * 1024 * 1024)

    def body(ck_ref, g_ref, b_ref, o_ref):
        o_ref[...] = (g_ref[...] + b_ref[...]).astype(BF16)

    return pl.pallas_call(
        body, name=name,
        grid_spec=pltpu.PrefetchScalarGridSpec(
            num_scalar_prefetch=1, grid=(r // rb,),
            in_specs=[pl.BlockSpec((None, rb, cc), lambda i, ck_ref: (ck_ref[0], i, 0)),
                      pl.BlockSpec((rb, cc), lambda i, ck_ref: (i, 0))],
            out_specs=pl.BlockSpec((rb, cc), lambda i, ck_ref: (i, 0))),
        out_shape=jax.ShapeDtypeStruct((r, cc), BF16),
        compiler_params=_cparams(1),
    )(ck, g, b)


def _add_window_call(name, g, b, p, ck):
    _, r, _ = g.shape
    nb, step = WIN_W // 128, WIN_STEP // 128

    def body(ck_ref, g_ref, b_ref, p0_ref, p1_ref, p2_ref, o_ref):
        own = g_ref[...] + b_ref[...]
        o_ref[...] = ((own + p0_ref[...].astype(F32)) + p1_ref[...].astype(F32)) + p2_ref[...].astype(F32)

    def peer(j):
        return pl.BlockSpec((None, r, 128), lambda i, ck_ref: (j, 0, i))

    return pl.pallas_call(
        body, name=name,
        grid_spec=pltpu.PrefetchScalarGridSpec(
            num_scalar_prefetch=1, grid=(nb,),
            in_specs=[pl.BlockSpec((None, r, 128), lambda i, ck_ref: (ck_ref[0], 0, step * ck_ref[1] + i)),
                      pl.BlockSpec((r, 128), lambda i, ck_ref: (0, step * ck_ref[1] + i)),
                      peer(0), peer(1), peer(2)],
            out_specs=pl.BlockSpec((None, r, 128), lambda i, ck_ref: (ck_ref[0], 0, i))),
        out_shape=jax.ShapeDtypeStruct((2, r, WIN_W), F32),
        compiler_params=_cparams(1),
    )(ck, g, b, p, p, p)


def _add_chips_call(name, g, b, p, ck):
    _, _, r, cc = g.shape
    rb = _row_block(r, cc, 2 * 1024 * 1024)

    def body(ck_ref, g_ref, b_ref, p0_ref, p1_ref, p2_ref, o_ref):
        own = g_ref[...] + b_ref[...]
        o_ref[...] = ((own + p0_ref[...].astype(F32)) + p1_ref[...].astype(F32)) + p2_ref[...].astype(F32)

    def peer(j):
        return pl.BlockSpec((None, rb, cc), lambda i, ck_ref: (j, i, 0))

    return pl.pallas_call(
        body, name=name,
        grid_spec=pltpu.PrefetchScalarGridSpec(
            num_scalar_prefetch=1, grid=(r // rb,),
            in_specs=[pl.BlockSpec((None, None, rb, cc), lambda i, ck_ref: (ck_ref[0], ck_ref[1], i, 0)),
                      pl.BlockSpec((None, rb, cc), lambda i, ck_ref: (ck_ref[1], i, 0)),
                      peer(0), peer(1), peer(2)],
            out_specs=pl.BlockSpec((None, rb, cc), lambda i, ck_ref: (ck_ref[0], i, 0))),
        out_shape=jax.ShapeDtypeStruct((2, r, cc), F32),
        compiler_params=_cparams(1),
    )(ck, g, b, p, p, p)


def _sum8_call(name, g):
    def body(g_ref, o_ref):
        acc = g_ref[0]
        for d in range(1, 8):
            acc = acc + g_ref[d]
        o_ref[...] = acc

    return pl.pallas_call(body, name=name, out_shape=jax.ShapeDtypeStruct(g.shape[1:], F32))(g)


def _adamw_call(name, w, g, m, v):
    r, cc = w.shape
    if r % 8 == 0 or r * cc * 4 <= 1024 * 1024:
        rb = _row_block(r, cc, 1024 * 1024) if r % 8 == 0 else r
        grid, spec = (r // rb,), pl.BlockSpec((rb, cc), lambda i: (i, 0))
    else:
        grid, spec = (cc // 128,), pl.BlockSpec((r, 128), lambda i: (0, i))

    def body(w_ref, g_ref, m_ref, v_ref, d_ref, m2_ref, v2_ref):
        gv = g_ref[...]
        m2 = ADAM_B1 * m_ref[...] + (1.0 - ADAM_B1) * gv
        v2 = ADAM_B2 * v_ref[...] + (1.0 - ADAM_B2) * (gv * gv)
        m_hat = m2 / (1.0 - ADAM_B1 ** ADAM_STEP)
        v_hat = v2 / (1.0 - ADAM_B2 ** ADAM_STEP)
        d_ref[...] = -ADAM_LR * (m_hat / (jnp.sqrt(v_hat) + ADAM_EPS) + ADAM_WD * w_ref[...])
        m2_ref[...] = m2
        v2_ref[...] = v2

    return pl.pallas_call(
        body, name=name, grid=grid, in_specs=[spec] * 4, out_specs=[spec] * 3,
        out_shape=[jax.ShapeDtypeStruct((r, cc), F32)] * 3, compiler_params=_cparams(1),
    )(w, g, m, v)


SMALL = (("norm_gain", D_MODEL), ("b_gate", GLA_KW), ("ret_norm_gain", RET_W), ("gla_norm_gain", GLA_W),
         ("final_norm_gain", D_MODEL), ("w_gate_up", GATE_RANK * GLA_KW), ("meta_tokens", N_META * D_MODEL))


def _pack_rows(vecs, rows):
    flat = jnp.concatenate([v.reshape(-1) for v in vecs])
    return jnp.pad(flat, (0, rows * 128 - flat.shape[0])).reshape(rows, 128)


def kernel(x, meta_tokens, norm_gain, w_in, w_gate_up, b_gate, ret_norm_gain, gla_norm_gain, w_branch_ret, w_branch_gla, w_out, final_norm_gain, loss_target, m_meta_tokens, m_norm_gain, m_w_in, m_w_gate_up, m_b_gate, m_ret_norm_gain, m_gla_norm_gain, m_w_branch_ret, m_w_branch_gla, m_w_out, m_final_norm_gain, v_meta_tokens, v_norm_gain, v_w_in, v_w_gate_up, v_b_gate, v_ret_norm_gain, v_gla_norm_gain, v_w_branch_ret, v_w_branch_gla, v_w_out, v_final_norm_gain):
    xi, yi, ci = _place()
    kme = 2 * xi + yi
    ck = jnp.stack([ci, kme]).astype(jnp.int32)
    sw_in = w_in.shape[2]

    def my_half(a, dtype):
        r, cc = a.shape
        return lax.dynamic_index_in_dim(a.reshape(2, r // 2, cc), ci, 0, keepdims=False).astype(dtype)

    g_in, g_meta, g_wg = _gather8_call(
        "gather_weights", [my_half(w_in[0], BF16), my_half(meta_tokens, F32), my_half(w_gate_up[0], F32)])
    branch_parts = [my_half(w_branch_ret[0], BF16), my_half(w_branch_gla[0], BF16), my_half(w_out[0], BF16)]
    w_in_bf = g_in.reshape(4, 2, D_MODEL // 2, sw_in).transpose(1, 2, 0, 3).reshape(D_MODEL, 4 * sw_in)
    meta = g_meta.reshape(4, 2, N_META // 2, D_MODEL // 4).transpose(1, 2, 0, 3).reshape(N_META, D_MODEL)
    wg_full = g_wg.reshape(4, 2, GATE_RANK // 2, GLA_KW // 4).transpose(1, 2, 0, 3).reshape(GATE_RANK, GLA_KW)

    loc = _device_step(x[0], loss_target[0], meta, norm_gain, w_in_bf, wg_full, b_gate, ret_norm_gain, gla_norm_gain,
                       branch_parts, final_norm_gain, ck)
    loss = lax.psum(loc["loss"], ("x", "y", "c"))
    names = ("w_in", "w_branch_ret", "w_branch_gla", "w_out")
    full = [loc[nm] for nm in names]
    big_w = dict(w_in=w_in[0], w_branch_ret=w_branch_ret[0], w_branch_gla=w_branch_gla[0], w_out=w_out[0])
    big_m = dict(w_in=m_w_in[0], w_branch_ret=m_w_branch_ret[0], w_branch_gla=m_w_branch_gla[0], w_out=m_w_out[0])
    big_v = dict(w_in=v_w_in[0], w_branch_ret=v_w_branch_ret[0], w_branch_gla=v_w_branch_gla[0], w_out=v_w_out[0])
    grads, deltas, new_m, new_v = {}, {}, {}, {}
    for nm, f in zip(names, full):
        shape = big_w[nm].shape
        if nm == "w_in":
            f = lax.dynamic_slice_in_dim(f, (sw_in - WIN_STEP) * kme, sw_in, axis=2)
        g = f.reshape(shape)
        if nm == "w_in":
            d, m2, v2 = (a.T for a in _adamw_call("adamw_" + nm, big_w[nm].T, g.T, big_m[nm].T, big_v[nm].T))
        else:
            d, m2, v2 = _adamw_call("adamw_" + nm, big_w[nm], g, big_m[nm], big_v[nm])
        grads[nm], deltas[nm], new_m[nm], new_v[nm] = (a.reshape((1,) + shape) for a in (g, d, m2, v2))

    small_g = dict(loc)
    small_g["meta_tokens"] = loc["dmeta"]
    n_small = sum(sz for _, sz in SMALL)
    rows = -(-n_small // 128 // 8) * 8
    (g_small,) = _gather8_call("gather_small_grads", [_pack_rows([small_g[nm] for nm, _ in SMALL], rows)])
    tot = _sum8_call("sum_small_grads", g_small).reshape(-1)
    off = 0
    sg = {}
    for nm, sz in SMALL:
        sg[nm] = tot[off:off + sz]
        off += sz
    sg["w_gate_up"] = lax.dynamic_slice_in_dim(sg["w_gate_up"].reshape(GATE_RANK, GLA_KW), kme * (GLA_KW // 4),
                                               GLA_KW // 4, axis=1)
    sg["meta_tokens"] = lax.dynamic_slice_in_dim(sg["meta_tokens"].reshape(N_META, D_MODEL), kme * (D_MODEL // 4),
                                                 D_MODEL // 4, axis=1)
    small_w = dict(norm_gain=norm_gain, b_gate=b_gate, ret_norm_gain=ret_norm_gain, gla_norm_gain=gla_norm_gain,
                   final_norm_gain=final_norm_gain, w_gate_up=w_gate_up, meta_tokens=meta_tokens)
    small_m = dict(norm_gain=m_norm_gain, b_gate=m_b_gate, ret_norm_gain=m_ret_norm_gain,
                   gla_norm_gain=m_gla_norm_gain, final_norm_gain=m_final_norm_gain, w_gate_up=m_w_gate_up,
                   meta_tokens=m_meta_tokens)
    small_v = dict(norm_gain=v_norm_gain, b_gate=v_b_gate, ret_norm_gain=v_ret_norm_gain,
                   gla_norm_gain=v_gla_norm_gain, final_norm_gain=v_final_norm_gain, w_gate_up=v_w_gate_up,
                   meta_tokens=v_meta_tokens)
    order = [nm for nm, _ in SMALL]
    sizes = [small_w[nm].size for nm in order]
    prow = -(-sum(sizes) // 128 // 8) * 8
    pk = lambda d: _pack_rows([d[nm] for nm in order], prow)
    d_s, m_s, v_s = _adamw_call("adamw_small", pk(small_w), pk(sg), pk(small_m), pk(small_v))
    off = 0
    for nm, sz in zip(order, sizes):
        shape = small_w[nm].shape
        grads[nm] = sg[nm].reshape(shape)
        deltas[nm], new_m[nm], new_v[nm] = (a.reshape(-1)[off:off + sz].reshape(shape) for a in (d_s, m_s, v_s))
        off += sz

    out_order = ("meta_tokens", "norm_gain", "w_in", "w_gate_up", "b_gate", "ret_norm_gain", "gla_norm_gain",
                 "w_branch_ret", "w_branch_gla", "w_out", "final_norm_gain")
    dx = loc["dx"].reshape(x.shape)
    return (loss, dx, *[grads[nm] for nm in out_order], *[deltas[nm] for nm in out_order],
            *[new_m[nm] for nm in out_order], *[new_v[nm] for nm in out_order])
```

```python
import math
from typing import Callable, NamedTuple

import numpy as np
import jax
import jax.numpy as jnp
from jax import lax
from jax.experimental import pallas as pl
from jax.experimental.pallas import tpu as pltpu

F32 = jnp.float32
BF16 = jnp.bfloat16

D_MODEL = 1024
N_META = 16
EPS = 1e-6
ROPE_BASE = 10000.0
RET_HEADS, RET_QK, RET_V = 4, 256, 512
RET_W = RET_HEADS * RET_V
GLA_HEADS, GLA_K, GLA_V = 4, 128, 256
GLA_W = GLA_HEADS * GLA_V
GLA_KW = GLA_HEADS * GLA_K
GATE_RANK = 16
GATE_TAU = 16.0
GLA_SUB = 16

TM = 256
T0 = TM
PADF = T0 - N_META
GC = 128
TB = 768
TK = 768

W_R = 6144
W_G = 3088
W_GP = 3200
W_M = 2048
IN_COLS = W_R + W_G + W_M
WIN_STEP = (IN_COLS // 4) // 128 * 128
WIN_W = -(-(3 * (IN_COLS // 4 - WIN_STEP) + IN_COLS // 4) // 128) * 128
IN_PAD = 3 * WIN_STEP + WIN_W

ADAM_LR, ADAM_B1, ADAM_B2, ADAM_EPS, ADAM_WD, ADAM_STEP = 0.001, 0.9, 0.999, 1e-08, 0.01, 10

VMEM_LIMIT = 56 * 1024 * 1024

NN = ((1,), (0,))
NT = ((1,), (1,))
TN = ((0,), (0,))


def _dot(a, b, dims):
    return lax.dot_general(a, b, (dims, ((), ())), preferred_element_type=F32)


def _cparams(n_axes):
    return pltpu.CompilerParams(dimension_semantics=("arbitrary",) * n_axes, vmem_limit_bytes=VMEM_LIMIT)


def _sigmoid(x):
    return 0.5 * jnp.tanh(0.5 * x) + 0.5


def _split3(x):
    hi = x.astype(BF16)
    r1 = x - hi.astype(F32)
    mid = r1.astype(BF16)
    lo = (r1 - mid.astype(F32)).astype(BF16)
    return hi, mid, lo


def _exact_pm(p, x):
    hi, mid, lo = _split3(x)
    return _dot(p, hi, NN) + _dot(p, mid, NN) + _dot(p, lo, NN)


def _rms_call(x2d, head, gain):
    tp = T0 + x2d.shape[0]

    def body(x_ref, hd_ref, g_ref, h_ref, u_ref):
        h = jnp.where(pl.program_id(0) == 0, hd_ref[...], x_ref[...])
        h_ref[...] = h
        r = lax.rsqrt(jnp.mean(h * h, axis=-1, keepdims=True) + EPS)
        u_ref[...] = (h * r * g_ref[...]).astype(BF16)

    tile = pl.BlockSpec((TM, D_MODEL), lambda i: (i, 0))
    return pl.pallas_call(
        body, name="rms_in", grid=(tp // TM,),
        in_specs=[pl.BlockSpec((TM, D_MODEL), lambda i: (jnp.maximum(i - 1, 0), 0)),
                  pl.BlockSpec((T0, D_MODEL), lambda i: (0, 0)), pl.BlockSpec((1, D_MODEL), lambda i: (0, 0))],
        out_specs=[tile, tile],
        out_shape=[jax.ShapeDtypeStruct((tp, D_MODEL), F32), jax.ShapeDtypeStruct((tp, D_MODEL), BF16)],
        compiler_params=_cparams(1),
    )(x2d, head, gain)


PROJ_ROWS_MAX = 1408


def _proj_rows(m):
    return max(r for r in range(16, PROJ_ROWS_MAX + 1, 16) if m % r == 0)


def _mm_nn(name, a, b, out_dtype, tn, col0, ncols, epilogue=None, extras=(), extra_specs=()):
    m, k = a.shape
    nj, j0 = ncols // tn, col0 // tn
    tb = _proj_rows(m)

    def body(a_ref, b_ref, *rest):
        *ex, o_ref = rest
        acc = _dot(a_ref[...], b_ref[...], NN)
        if epilogue is None:
            o_ref[...] = acc.astype(out_dtype)
        else:
            epilogue(acc, o_ref, *ex)

    return pl.pallas_call(
        body, name=name, grid=(nj, m // tb),
        in_specs=[pl.BlockSpec((tb, k), lambda j, i: (i, 0)), pl.BlockSpec((k, tn), lambda j, i: (0, j0 + j))]
        + list(extra_specs),
        out_specs=pl.BlockSpec((tb, tn), lambda j, i: (i, j)),
        out_shape=jax.ShapeDtypeStruct((m, ncols), out_dtype),
        compiler_params=_cparams(2),
    )(a, b, *extras)


def _rope_epilogue(acc, o_ref, cos_ref, sin_ref):
    scale = jnp.where(pl.program_id(0) == 1, RET_QK ** -0.5, 1.0).astype(F32)
    cos, sin = cos_ref[...], sin_ref[...]
    half = RET_QK // 2
    for h in range(RET_HEADS):
        t1 = acc[:, h * RET_QK:h * RET_QK + half]
        t2 = acc[:, h * RET_QK + half:(h + 1) * RET_QK]
        o_ref[:, h * RET_QK:h * RET_QK + half] = ((t1 * cos - t2 * sin) * scale).astype(BF16)
        o_ref[:, h * RET_QK + half:(h + 1) * RET_QK] = ((t2 * cos + t1 * sin) * scale).astype(BF16)


def _gqk_epilogue(acc, o_ref):
    o_ref[:, :GLA_KW] = acc[:, :GLA_KW] * (GLA_K ** -0.5)
    o_ref[:, GLA_KW:] = acc[:, GLA_KW:]


class _Comm(NamedTuple):
    srcs: tuple
    out_shapes: tuple
    n_sems: int
    make: Callable


def _comm_sems(comm):
    return [pltpu.SemaphoreType.DMA((comm.n_sems,)), pltpu.SemaphoreType.DMA((comm.n_sems,))]


def _other_chips(x, y):
    return [(1 - x, y), (x, 1 - y), (1 - x, 1 - y)]


def _exchange_plan(ss):
    def make(s_refs, b_refs, send_sems, recv_sems):
        x, y, c = _place()
        return [pltpu.make_async_remote_copy(
            src_ref=s_refs[t].at[2 * chip[0] + chip[1]], dst_ref=b_refs[t].at[j], send_sem=send_sems.at[3 * t + j],
            recv_sem=recv_sems.at[3 * t + j], device_id=(*chip, c), device_id_type=MESH)
            for t in range(len(s_refs)) for j, chip in enumerate(_other_chips(x, y))]

    return _Comm(tuple(ss), tuple(jax.ShapeDtypeStruct((3,) + s.shape[1:], s.dtype) for s in ss), 3 * len(ss), make)


def _exchange_window_plan(s):
    def make(s_refs, b_refs, send_sems, recv_sems):
        x, y, c = _place()
        return [pltpu.make_async_remote_copy(
            src_ref=s_refs[0].at[:, pl.ds(pl.multiple_of((2 * chip[0] + chip[1]) * WIN_STEP, 128), WIN_W)],
            dst_ref=b_refs[0].at[j], send_sem=send_sems.at[j], recv_sem=recv_sems.at[j], device_id=(*chip, c),
            device_id_type=MESH) for j, chip in enumerate(_other_chips(x, y))]

    return _Comm((s,), (jax.ShapeDtypeStruct((3, s.shape[0], WIN_W), s.dtype),), 3, make)


def _swap_plan(gs):
    def make(g_refs, b_refs, send_sems, recv_sems):
        x, y, c = _place()
        return [pltpu.make_async_remote_copy(
            src_ref=g_refs[t].at[1 - c], dst_ref=b_refs[t], send_sem=send_sems.at[t], recv_sem=recv_sems.at[t],
            device_id=(x, y, 1 - c), device_id_type=MESH) for t in range(len(g_refs))]

    return _Comm(tuple(gs), tuple(jax.ShapeDtypeStruct(g.shape[1:], g.dtype) for g in gs), len(gs), make)


def _spread_plan(parts):
    def make(p_refs, o_refs, send_sems, recv_sems):
        x, y, c = _place()
        copies = []
        for t in range(len(p_refs)):
            mine = o_refs[t].at[4 * x + 2 * y + c]
            copies.append(pltpu.make_async_copy(p_refs[t], mine, send_sems.at[7 * len(p_refs) + t]))
            for r in range(1, 8):
                peer = (1 - x if r & 4 else x, 1 - y if r & 2 else y, 1 - c if r & 1 else c)
                copies.append(pltpu.make_async_remote_copy(
                    src_ref=p_refs[t], dst_ref=mine, send_sem=send_sems.at[7 * t + r - 1],
                    recv_sem=recv_sems.at[7 * t + r - 1], device_id=peer, device_id_type=MESH))
        return copies

    return _Comm(tuple(parts), tuple(jax.ShapeDtypeStruct((8,) + p.shape, p.dtype) for p in parts), 8 * len(parts),
                 make)


def _mm_nt_acc(name, a, w, tk, acc_in=None, epilogue=None, extras=(), extra_specs=(), extra_out_shapes=(),
               extra_out_specs=(), extra_scratch=(), comm=None):
    m, k = a.shape
    n = w.shape[0]
    nk, ni = k // tk, m // TB
    has_acc = acc_in is not None
    n_xc = len(comm.srcs) if comm else 0
    n_es = len(extra_scratch)

    def body(*refs):
        a_ref, w_ref = refs[0], refs[1]
        pos = 2
        acc_ref = None
        if has_acc:
            acc_ref = refs[pos]
            pos += 1
        ex = refs[pos:pos + len(extras)]
        pos += len(extras)
        xc_src = refs[pos:pos + n_xc]
        pos += n_xc
        n_scr = 1 + n_es + (2 if n_xc else 0)
        outs = refs[pos:len(refs) - n_scr - n_xc]
        xc_dst = refs[len(refs) - n_scr - n_xc:len(refs) - n_scr]
        scr = refs[len(refs) - n_scr]
        es = refs[len(refs) - n_scr + 1:len(refs) - n_scr + 1 + n_es]
        i, kk = pl.program_id(0), pl.program_id(1)
        if n_xc:
            copies = comm.make(xc_src, xc_dst, refs[-2], refs[-1])

            @pl.when((i == 0) & (kk == 0))
            def _():
                for cp in copies:
                    cp.start()

        @pl.when(kk == 0)
        def _():
            scr[...] = acc_ref[...] if has_acc else jnp.zeros_like(scr)

        scr[...] += _dot(a_ref[...], w_ref[...], NT)

        @pl.when(kk == nk - 1)
        def _():
            if epilogue is None:
                outs[0][...] = scr[...]
            else:
                epilogue(scr[...], outs, i, ni, *ex, *es)

        if n_xc:
            @pl.when((i == ni - 1) & (kk == nk - 1))
            def _():
                for cp in copies:
                    cp.wait()

    in_specs = [pl.BlockSpec((TB, tk), lambda i, kk: (i, kk)), pl.BlockSpec((n, tk), lambda i, kk: (0, kk))]
    args = [a, w]
    if has_acc:
        in_specs.append(pl.BlockSpec((TB, n), lambda i, kk: (i, 0)))
        args.append(acc_in)
    in_specs += list(extra_specs) + [ANY] * n_xc
    args += list(extras) + (list(comm.srcs) if comm else [])
    if epilogue is None:
        out_shape = [jax.ShapeDtypeStruct((m, n), F32)]
        out_specs = [pl.BlockSpec((TB, n), lambda i, kk: (i, 0))]
    else:
        out_shape, out_specs = list(extra_out_shapes), list(extra_out_specs)
    scratch = [pltpu.VMEM((TB, n), F32)] + list(extra_scratch)
    if n_xc:
        out_shape += list(comm.out_shapes)
        out_specs += [ANY] * n_xc
        scratch += _comm_sems(comm)
    return pl.pallas_call(
        body, name=name, grid=(ni, nk), in_specs=in_specs, out_specs=out_specs, out_shape=out_shape,
        scratch_shapes=scratch, compiler_params=_cparams(2),
    )(*args)


def _rms_bwd_epilogue(du, outs, i, ni, h_ref, g_ref, dh1_ref, obuf, sems):
    dx_ref, dmeta_ref, dg_ref = outs
    h = h_ref[...]
    r = lax.rsqrt(jnp.mean(h * h, axis=-1, keepdims=True) + EPS)
    xh = h * r
    dxh = du * g_ref[...]
    dh0 = dh1_ref[...] + r * (dxh - xh * jnp.mean(dxh * xh, axis=-1, keepdims=True))

    def put(slot, tile):
        return pltpu.make_async_copy(obuf.at[slot], dx_ref.at[pl.ds(pl.multiple_of(tile * TB - T0, 8), TB)],
                                     sems.at[slot])

    @pl.when(i == 0)
    def _():
        dg_ref[...] = jnp.zeros_like(dg_ref)
        dmeta_ref[...] = dh0[PADF:T0, :]
        obuf[0] = dh0
        first = pltpu.make_async_copy(obuf.at[0, pl.ds(T0, TB - T0)], dx_ref.at[pl.ds(0, TB - T0)], sems.at[0])
        first.start()
        first.wait()

    @pl.when(i >= 1)
    def _():
        slot = i % 2

        @pl.when(i >= 3)
        def _():
            put(slot, i - 2).wait()

        obuf[slot] = dh0
        put(slot, i).start()

    dg_ref[...] += jnp.sum(du * xh, axis=0, keepdims=True)

    @pl.when(i == ni - 1)
    def _():
        for tile in (ni - 2, ni - 1):
            if tile >= 1:
                put(tile % 2, tile).wait()


def _mm_tn(name, a, b, bn, ncols=None, bcol0=0, into=None, col0=0, out_cols=None):
    t, m = a.shape
    n = ncols or b.shape[1]
    j0, bj0 = col0 // bn, bcol0 // bn

    def body(a_ref, b_ref, *rest):
        o_ref = rest[-1]

        @pl.when(pl.program_id(1) == 0)
        def _():
            o_ref[...] = jnp.zeros_like(o_ref)

        o_ref[...] += _dot(a_ref[...], b_ref[...], TN)

    in_specs = [pl.BlockSpec((TK, m), lambda j, kk: (kk, 0)), pl.BlockSpec((TK, bn), lambda j, kk: (kk, bj0 + j))]
    args = [a, b]
    aliases = {}
    if into is not None:
        in_specs.append(ANY)
        args.append(into)
        aliases = {2: 0}
        out_cols = into.shape[1]
    return pl.pallas_call(
        body, name=name, grid=(n // bn, t // TK), in_specs=in_specs,
        out_specs=pl.BlockSpec((m, bn), lambda j, kk: (0, j0 + j)),
        out_shape=jax.ShapeDtypeStruct((m, out_cols or n), F32), input_output_aliases=aliases,
        compiler_params=_cparams(2),
    )(*args)


def _place_merge_cols_call(dwp, dw_m):
    c0 = W_R + W_GP - 128
    tail = IN_PAD - c0
    rows = 256

    def body(m_ref, p_ref, o_ref, buf, low, sem):
        get = pltpu.make_async_copy(o_ref.at[:, pl.ds(c0, 128)], low, sem)
        get.start()
        get.wait()
        for r in range(0, D_MODEL, rows):
            buf[r:r + rows, :] = jnp.concatenate(
                [low[r:r + rows, :GATE_RANK], m_ref[r:r + rows, :],
                 jnp.zeros((rows, tail - GATE_RANK - W_M), F32)], axis=1)
        put = pltpu.make_async_copy(buf, o_ref.at[:, pl.ds(c0, tail)], sem)
        put.start()
        put.wait()

    return pl.pallas_call(
        body, name="place_merge_cols",
        in_specs=[pl.BlockSpec(memory_space=pltpu.VMEM), ANY], out_specs=ANY,
        out_shape=jax.ShapeDtypeStruct(dwp.shape, F32), input_output_aliases={1: 0},
        scratch_shapes=[pltpu.VMEM((D_MODEL, tail), F32), pltpu.VMEM((D_MODEL, 128), F32), pltpu.SemaphoreType.DMA],
        compiler_params=pltpu.CompilerParams(vmem_limit_bytes=VMEM_LIMIT),
    )(dw_m, dwp)


def _ret_fill_decay(lg_ref, dm_scr):
    c = TM
    ii = lax.broadcasted_iota(jnp.int32, (c, c), 0)
    jj = lax.broadcasted_iota(jnp.int32, (c, c), 1)
    rel = (ii - jj).astype(F32)
    for h in range(RET_HEADS):
        dm_scr[h] = jnp.where(rel >= 0, jnp.exp(jnp.maximum(rel, 0.0) * lg_ref[h]), 0.0)


def _ret_consts(lg, dm_ref):
    c = TM
    idx = lax.broadcasted_iota(jnp.int32, (c, 1), 0).astype(F32)
    xi = jnp.exp((idx + 1.0) * lg)
    zeta = jnp.exp((c - 1.0 - idx) * lg)
    gc = jnp.exp(jnp.full((1, 1), c, F32) * lg)
    return dm_ref[...], xi, zeta, gc


def _ret_fwd_call(rqk, rv, rg, gain, lgam):
    tp = rqk.shape[0]
    nc = tp // TM

    def body(lg_ref, qk_ref, v_ref, rg_ref, g_ref, o_ref, a_ref, st_ref, s_scr, dm_scr):
        @pl.when(pl.program_id(0) == 0)
        def _():
            s_scr[...] = jnp.zeros_like(s_scr)
            _ret_fill_decay(lg_ref, dm_scr)

        for h in range(RET_HEADS):
            dm, xi, zeta, gc = _ret_consts(lg_ref[h], dm_scr.at[h])
            q = qk_ref[:, h * RET_QK:(h + 1) * RET_QK]
            k = qk_ref[:, D_MODEL + h * RET_QK:D_MODEL + (h + 1) * RET_QK]
            v = v_ref[:, h * RET_V:(h + 1) * RET_V]
            sb = s_scr[h].astype(BF16)
            st_ref[0, h] = sb
            s = _dot(q, k, NT) * dm
            o = _dot(s.astype(BF16), v, NN) + xi * _dot(q, sb, NN)
            kz = (k.astype(F32) * zeta).astype(BF16)
            s_scr[h] = gc * s_scr[h] + _dot(kz, v, TN)
            o_ref[:, h * RET_V:(h + 1) * RET_V] = o
            mu = jnp.mean(o, axis=-1, keepdims=True)
            xc = o - mu
            xh = xc * lax.rsqrt(jnp.mean(xc * xc, axis=-1, keepdims=True) + EPS)
            g = rg_ref[:, h * RET_V:(h + 1) * RET_V]
            a_ref[:, h * RET_V:(h + 1) * RET_V] = (
                xh * g_ref[:, h * RET_V:(h + 1) * RET_V] * (g * _sigmoid(g))).astype(BF16)

    return pl.pallas_call(
        body, name="ret_fwd", grid=(nc,),
        in_specs=[pl.BlockSpec(memory_space=pltpu.SMEM),
                  pl.BlockSpec((TM, 2 * D_MODEL), lambda n: (n, 0)),
                  pl.BlockSpec((TM, RET_W), lambda n: (n, 0)),
                  pl.BlockSpec((TM, RET_W), lambda n: (n, 0)),
                  pl.BlockSpec((1, RET_W), lambda n: (0, 0))],
        out_specs=[pl.BlockSpec((TM, RET_W), lambda n: (n, 0)),
                   pl.BlockSpec((TM, RET_W), lambda n: (n, 0)),
                   pl.BlockSpec((1, RET_HEADS, RET_QK, RET_V), lambda n: (n, 0, 0, 0))],
        out_shape=[jax.ShapeDtypeStruct((tp, RET_W), F32), jax.ShapeDtypeStruct((tp, RET_W), BF16),
                   jax.ShapeDtypeStruct((nc, RET_HEADS, RET_QK, RET_V), BF16)],
        scratch_shapes=[pltpu.VMEM((RET_HEADS, RET_QK, RET_V), F32), pltpu.VMEM((RET_HEADS, TM, TM), F32)],
        compiler_params=_cparams(1),
    )(lgam, rqk, rv, rg, gain)


def _ret_bwd_call(rqk, rv, rg, o_ret, da, states, gain, lgam, cos, sin):
    tp = rqk.shape[0]
    nc = tp // TM
    half = RET_QK // 2

    def body(lg_ref, qk_ref, v_ref, rg_ref, o_ref, da_ref, st_ref, g_ref, cos_ref, sin_ref, dp_ref, dg_ref, ds_scr,
             dm_scr):
        @pl.when(pl.program_id(0) == 0)
        def _():
            ds_scr[...] = jnp.zeros_like(ds_scr)
            dg_ref[...] = jnp.zeros_like(dg_ref)
            _ret_fill_decay(lg_ref, dm_scr)

        cos, sin = cos_ref[...], sin_ref[...]
        for h in range(RET_HEADS):
            hs = slice(h * RET_V, (h + 1) * RET_V)
            dm, xi, zeta, gc = _ret_consts(lg_ref[h], dm_scr.at[h])
            o = o_ref[:, hs]
            mu = jnp.mean(o, axis=-1, keepdims=True)
            xc = o - mu
            rstd = lax.rsqrt(jnp.mean(xc * xc, axis=-1, keepdims=True) + EPS)
            xh = xc * rstd
            gain_h = g_ref[:, hs]
            g = rg_ref[:, hs]
            sg = _sigmoid(g)
            silu = g * sg
            dah = da_ref[:, hs]
            dp_ref[:, 4 * D_MODEL + h * RET_V:4 * D_MODEL + (h + 1) * RET_V] = (
                dah * (xh * gain_h) * (sg * (1.0 + g * (1.0 - sg)))).astype(BF16)
            dn = dah * silu
            dg_ref[:, hs] += jnp.sum(dn * xh, axis=0, keepdims=True)
            dxh = dn * gain_h
            do = rstd * (dxh - jnp.mean(dxh, axis=-1, keepdims=True)
                         - xh * jnp.mean(dxh * xh, axis=-1, keepdims=True))
            dob = do.astype(BF16)
            q = qk_ref[:, h * RET_QK:(h + 1) * RET_QK]
            k = qk_ref[:, D_MODEL + h * RET_QK:D_MODEL + (h + 1) * RET_QK]
            v = v_ref[:, hs]
            sp = st_ref[0, h]
            ds = ds_scr[h]
            dsb = ds.astype(BF16)
            s = (_dot(q, k, NT) * dm).astype(BF16)
            dsc = (_dot(dob, v, NT) * dm).astype(BF16)
            dq = _dot(dsc, k, NN) + xi * _dot(dob, sp, NT)
            dk = _dot(dsc, q, TN) + zeta * _dot(v, dsb, NT)
            kz = (k.astype(F32) * zeta).astype(BF16)
            dv = _dot(s, dob, TN) + _dot(kz, dsb, NN)
            qx = (q.astype(F32) * xi).astype(BF16)
            ds_scr[h] = gc * ds + _dot(qx, dob, TN)
            dp_ref[:, 2 * D_MODEL + h * RET_V:2 * D_MODEL + (h + 1) * RET_V] = dv.astype(BF16)
            dk = dk * (RET_QK ** -0.5)
            for base, t in ((0, dq), (D_MODEL, dk)):
                t1, t2 = t[:, :half], t[:, half:]
                dp_ref[:, base + h * RET_QK:base + h * RET_QK + half] = (t1 * cos + t2 * sin).astype(BF16)
                dp_ref[:, base + h * RET_QK + half:base + (h + 1) * RET_QK] = (t2 * cos - t1 * sin).astype(BF16)

    rev = lambda n: (nc - 1 - n, 0)
    return pl.pallas_call(
        body, name="ret_bwd", grid=(nc,),
        in_specs=[pl.BlockSpec(memory_space=pltpu.SMEM),
                  pl.BlockSpec((TM, 2 * D_MODEL), rev),
                  pl.BlockSpec((TM, RET_W), rev),
                  pl.BlockSpec((TM, RET_W), rev),
                  pl.BlockSpec((TM, RET_W), rev),
                  pl.BlockSpec((TM, RET_W), rev),
                  pl.BlockSpec((1, RET_HEADS, RET_QK, RET_V), lambda n: (nc - 1 - n, 0, 0, 0)),
                  pl.BlockSpec((1, RET_W), lambda n: (0, 0)),
                  pl.BlockSpec((TM, half), rev),
                  pl.BlockSpec((TM, half), rev)],
        out_specs=[pl.BlockSpec((TM, W_R), rev), pl.BlockSpec((1, RET_W), lambda n: (0, 0))],
        out_shape=[jax.ShapeDtypeStruct((tp, W_R), BF16), jax.ShapeDtypeStruct((1, RET_W), F32)],
        scratch_shapes=[pltpu.VMEM((RET_HEADS, RET_QK, RET_V), F32), pltpu.VMEM((RET_HEADS, TM, TM), F32)],
        compiler_params=_cparams(1),
    )(lgam, rqk, rv, rg, o_ret, da, states, gain, cos, sin)


GLA_LEVELS = tuple(GC >> (s + 1) for s in range(int(math.log2(GC // GLA_SUB))))
NLEV = len(GLA_LEVELS)


def _gla_tril():
    return np.tril(np.ones((GC, GC), np.float32))


def _gla_masks():
    ii = lax.broadcasted_iota(jnp.int32, (GC, GC), 0)
    jj = lax.broadcasted_iota(jnp.int32, (GC, GC), 1)
    masks = []
    for m in GLA_LEVELS:
        sh = int(math.log2(2 * m))
        masks.append(((ii >> sh) == (jj >> sh)) & ((ii & m) != 0) & ((jj & m) == 0))
    sh = int(math.log2(GLA_SUB))
    md = ((ii >> sh) == (jj >> sh)) & (jj <= ii)
    row = lax.broadcasted_iota(jnp.int32, (GC, 1), 0)
    second = [(row & m) != 0 for m in GLA_LEVELS]
    return masks, md, second


def _gla_log_decay(glr_ref, wg_ref, bg_ref):
    z = _dot(glr_ref[...].astype(BF16), wg_ref[...], NN) + bg_ref[...]
    la = (jnp.minimum(z, 0.0) - jnp.log1p(jnp.exp(-jnp.abs(z)))) * (1.0 / GATE_TAU)
    return z, la


def _gla_row_steps(b_ref, cs, rows, size):
    parts = [jnp.zeros((size, GLA_K), F32) if r is None else jnp.broadcast_to(b_ref[r:r + 1, cs], (size, GLA_K))
             for r in rows]
    return parts[0] if len(parts) == 1 else jnp.concatenate(parts, axis=0)


def _gla_factors(b_ref, h, second):
    cs = slice(h * GLA_K, (h + 1) * GLA_K)
    b = b_ref[:, cs]
    fq, fk = [], []
    for l, m in enumerate(GLA_LEVELS):
        d = b - _gla_row_steps(b_ref, cs, [s + m - 1 for s in range(0, GC, 2 * m)], 2 * m)
        f = jnp.exp(jnp.where(second[l], d, -d))
        fq.append(jnp.where(second[l], f, 0.0))
        fk.append(jnp.where(second[l], 0.0, f))
    dd = b - _gla_row_steps(b_ref, cs, [None] + [s - 1 for s in range(GLA_SUB, GC, GLA_SUB)], GLA_SUB)
    ed = jnp.exp(dd)
    edi = jnp.exp(-dd)
    eb = jnp.exp(b)
    bl = b_ref[GC - 1:GC, cs]
    ee = jnp.exp(bl - b)
    ebl = jnp.exp(bl)
    return fq, fk, ed, edi, eb, ee, ebl


def _gla_scores(q, k, fq, fk, ed, edi, masks, md):
    qt = [(q * f).astype(BF16) for f in fq]
    kt = [(k * f).astype(BF16) for f in fk]
    qd = (q * ed).astype(BF16)
    kd = (k * edi).astype(BF16)
    a = jnp.where(md, _dot(qd, kd, NT), 0.0)
    for l in range(NLEV):
        a = a + jnp.where(masks[l], _dot(qt[l], kt[l], NT), 0.0)
    return a, qt, kt, qd, kd


def _gla_fwd_call(gqk, gv, glr, gg, wg, bg, gain, pmat, comm=None):
    tp = gqk.shape[0]
    nc = tp // GC
    n_xc = len(comm.srcs) if comm else 0

    def body(qk_ref, v_ref, glr_ref, gg_ref, wg_ref, bg_ref, g_ref, p_ref, *rest):
        xc_src = rest[:n_xc]
        o_ref, a_ref, st_ref = rest[n_xc:n_xc + 3]
        xc_dst = rest[n_xc + 3:2 * n_xc + 3]
        s_scr, b_scr = rest[2 * n_xc + 3:2 * n_xc + 5]
        n = pl.program_id(0)
        if n_xc:
            copies = comm.make(xc_src, xc_dst, rest[-2], rest[-1])

            @pl.when(n == 0)
            def _():
                for cp in copies:
                    cp.start()

            @pl.when(n == nc - 1)
            def _():
                for cp in copies:
                    cp.wait()

        @pl.when(n == 0)
        def _():
            s_scr[...] = jnp.zeros_like(s_scr)

        _, la = _gla_log_decay(glr_ref, wg_ref, bg_ref)
        b_scr[...] = _exact_pm(p_ref[...], la)
        masks, md, second = _gla_masks()
        for h in range(GLA_HEADS):
            q = qk_ref[:, h * GLA_K:(h + 1) * GLA_K]
            k = qk_ref[:, GLA_KW + h * GLA_K:GLA_KW + (h + 1) * GLA_K]
            vs = slice(h * GLA_V, (h + 1) * GLA_V)
            v = v_ref[:, vs]
            fq, fk, ed, edi, eb, ee, ebl = _gla_factors(b_scr, h, second)
            a, *_ = _gla_scores(q, k, fq, fk, ed, edi, masks, md)
            sb = s_scr[h].astype(BF16)
            st_ref[0, h] = sb
            o = _dot(a.astype(BF16), v, NN) + _dot((q * eb).astype(BF16), sb, NT)
            s_scr[h] = s_scr[h] * ebl + _dot(v, (k * ee).astype(BF16), TN)
            o_ref[:, vs] = o
            xh = o * lax.rsqrt(jnp.mean(o * o, axis=-1, keepdims=True) + EPS)
            g = gg_ref[:, vs]
            a_ref[:, vs] = (xh * g_ref[:, vs] * (g * _sigmoid(g))).astype(BF16)

    return pl.pallas_call(
        body, name="gla_fwd", grid=(nc,),
        in_specs=[pl.BlockSpec((GC, 2 * GLA_KW), lambda n: (n, 0)),
                  pl.BlockSpec((GC, GLA_W), lambda n: (n, 0)),
                  pl.BlockSpec((GC, 128), lambda n: (n, 0)),
                  pl.BlockSpec((GC, GLA_W), lambda n: (n, 0)),
                  pl.BlockSpec((128, GLA_KW), lambda n: (0, 0)),
                  pl.BlockSpec((1, GLA_KW), lambda n: (0, 0)),
                  pl.BlockSpec((1, GLA_W), lambda n: (0, 0)),
                  pl.BlockSpec((GC, GC), lambda n: (0, 0))] + [ANY] * n_xc,
        out_specs=[pl.BlockSpec((GC, GLA_W), lambda n: (n, 0)),
                   pl.BlockSpec((GC, GLA_W), lambda n: (n, 0)),
                   pl.BlockSpec((1, GLA_HEADS, GLA_V, GLA_K), lambda n: (n, 0, 0, 0))] + [ANY] * n_xc,
        out_shape=[jax.ShapeDtypeStruct((tp, GLA_W), F32), jax.ShapeDtypeStruct((tp, GLA_W), BF16),
                   jax.ShapeDtypeStruct((nc, GLA_HEADS, GLA_V, GLA_K), BF16)] + (list(comm.out_shapes) if comm else []),
        scratch_shapes=[pltpu.VMEM((GLA_HEADS, GLA_V, GLA_K), F32), pltpu.VMEM((GC, GLA_KW), F32)]
        + (_comm_sems(comm) if comm else []),
        compiler_params=_cparams(1),
    )(gqk, gv, glr, gg, wg, bg, gain, pmat, *(comm.srcs if comm else ()))


def _gla_bwd_call(gqk, gv, glr, gg, o_gla, da, states, wg, bg, gain, pmat, pmat_t, comm=None):
    tp = gqk.shape[0]
    nc = tp // GC
    o_gv, o_gg, o_lr = 2 * GLA_KW, 2 * GLA_KW + GLA_W, 2 * GLA_KW + 2 * GLA_W
    n_xc = len(comm.srcs) if comm else 0

    def body(qk_ref, v_ref, glr_ref, gg_ref, o_ref, da_ref, st_ref, wg_ref, bg_ref, g_ref, p_ref, pt_ref, *rest):
        xc_src = rest[:n_xc]
        dp_ref, dwg_ref, dbg_ref, dg_ref = rest[n_xc:n_xc + 4]
        xc_dst = rest[n_xc + 4:2 * n_xc + 4]
        ds_scr, b_scr, db_scr = rest[2 * n_xc + 4:2 * n_xc + 7]
        n = pl.program_id(0)
        if n_xc:
            copies = comm.make(xc_src, xc_dst, rest[-2], rest[-1])

            @pl.when(n == 0)
            def _():
                for cp in copies:
                    cp.start()

            @pl.when(n == nc - 1)
            def _():
                for cp in copies:
                    cp.wait()

        @pl.when(n == 0)
        def _():
            ds_scr[...] = jnp.zeros_like(ds_scr)
            dwg_ref[...] = jnp.zeros_like(dwg_ref)
            dbg_ref[...] = jnp.zeros_like(dbg_ref)
            dg_ref[...] = jnp.zeros_like(dg_ref)

        z, la = _gla_log_decay(glr_ref, wg_ref, bg_ref)
        b_scr[...] = _exact_pm(p_ref[...], la)
        masks, md, second = _gla_masks()
        for h in range(GLA_HEADS):
            cs = slice(h * GLA_K, (h + 1) * GLA_K)
            vs = slice(h * GLA_V, (h + 1) * GLA_V)
            o = o_ref[:, vs]
            rstd = lax.rsqrt(jnp.mean(o * o, axis=-1, keepdims=True) + EPS)
            xh = o * rstd
            gain_h = g_ref[:, vs]
            g = gg_ref[:, vs]
            sg = _sigmoid(g)
            dah = da_ref[:, vs]
            dp_ref[:, o_gg + h * GLA_V:o_gg + (h + 1) * GLA_V] = (
                dah * (xh * gain_h) * (sg * (1.0 + g * (1.0 - sg)))).astype(BF16)
            dn = dah * (g * sg)
            dg_ref[:, vs] += jnp.sum(dn * xh, axis=0, keepdims=True)
            dxh = dn * gain_h
            do = rstd * (dxh - xh * jnp.mean(dxh * xh, axis=-1, keepdims=True))
            dob = do.astype(BF16)
            q = qk_ref[:, cs]
            k = qk_ref[:, GLA_KW + h * GLA_K:GLA_KW + (h + 1) * GLA_K]
            v = v_ref[:, vs]
            fq, fk, ed, edi, eb, ee, ebl = _gla_factors(b_scr, h, second)
            a, qt, kt, qd, kd = _gla_scores(q, k, fq, fk, ed, edi, masks, md)
            sp = st_ref[0, h]
            ds = ds_scr[h]
            dsb = ds.astype(BF16)
            q_in = q * eb
            k_end = k * ee
            da_s = _dot(dob, v, NT)
            dv = _dot(a.astype(BF16), dob, TN) + _dot(k_end.astype(BF16), dsb, NT)
            dq_in = _dot(dob, sp, NN)
            dk_end = _dot(v, dsb, NN)
            dbl = jnp.sum(sp.astype(F32) * ds, axis=0, keepdims=True) * ebl
            ds_scr[h] = ds * ebl + _dot(dob, q_in.astype(BF16), TN)
            dq = dq_in * eb
            dk = dk_end * ee
            de_end = dk_end * k_end
            db = dq_in * q_in - de_end
            placed = [(GC - 1, jnp.sum(de_end, axis=0, keepdims=True) + dbl)]
            for l, m in enumerate(GLA_LEVELS):
                dal = jnp.where(masks[l], da_s, 0.0).astype(BF16)
                dqt = _dot(dal, kt[l], NN)
                dkt = _dot(dal, qt[l], TN)
                dq = dq + dqt * fq[l]
                dk = dk + dkt * fk[l]
                gl = dqt * (q * fq[l]) - dkt * (k * fk[l])
                db = db + gl
                placed += [(s + m - 1, -jnp.sum(gl[s:s + 2 * m], axis=0, keepdims=True)) for s in range(0, GC, 2 * m)]
            dad = jnp.where(md, da_s, 0.0).astype(BF16)
            dqd = _dot(dad, kd, NN)
            dkd = _dot(dad, qd, TN)
            dq = dq + dqd * ed
            dk = dk + dkd * edi
            gd = dqd * (q * ed) - dkd * (k * edi)
            db = db + gd
            placed += [(s - 1, -jnp.sum(gd[s:s + GLA_SUB], axis=0, keepdims=True)) for s in range(GLA_SUB, GC, GLA_SUB)]
            db_scr[:, cs] = db
            for r, val in placed:
                db_scr[r:r + 1, cs] += val
            dp_ref[:, cs] = (dq * (GLA_K ** -0.5)).astype(BF16)
            dp_ref[:, GLA_KW + h * GLA_K:GLA_KW + (h + 1) * GLA_K] = dk.astype(BF16)
            dp_ref[:, o_gv + h * GLA_V:o_gv + (h + 1) * GLA_V] = dv.astype(BF16)
        dla = _exact_pm(pt_ref[...], db_scr[...])
        row = (nc - 1 - n) * GC + lax.broadcasted_iota(jnp.int32, (GC, 1), 0)
        dz = jnp.where(row >= PADF, dla * (1.0 / GATE_TAU) * _sigmoid(-z), 0.0)
        dzb = dz.astype(BF16)
        dp_ref[:, o_lr:] = _dot(dzb, wg_ref[...], NT).astype(BF16)
        dwg_ref[...] += _dot(glr_ref[...].astype(BF16), dzb, TN)
        dbg_ref[...] += jnp.sum(dz, axis=0, keepdims=True)

    rev = lambda n: (nc - 1 - n, 0)
    const = lambda n: (0, 0)
    xc_shapes, xc_sems = (list(comm.out_shapes), _comm_sems(comm)) if n_xc else ([], [])
    return pl.pallas_call(
        body, name="gla_bwd", grid=(nc,),
        in_specs=[pl.BlockSpec((GC, 2 * GLA_KW), rev),
                  pl.BlockSpec((GC, GLA_W), rev),
                  pl.BlockSpec((GC, 128), rev),
                  pl.BlockSpec((GC, GLA_W), rev),
                  pl.BlockSpec((GC, GLA_W), rev),
                  pl.BlockSpec((GC, GLA_W), rev),
                  pl.BlockSpec((1, GLA_HEADS, GLA_V, GLA_K), lambda n: (nc - 1 - n, 0, 0, 0)),
                  pl.BlockSpec((128, GLA_KW), const),
                  pl.BlockSpec((1, GLA_KW), const),
                  pl.BlockSpec((1, GLA_W), const),
                  pl.BlockSpec((GC, GC), const),
                  pl.BlockSpec((GC, GC), const)] + [ANY] * n_xc,
        out_specs=[pl.BlockSpec((GC, W_GP), rev), pl.BlockSpec((128, GLA_KW), const),
                   pl.BlockSpec((1, GLA_KW), const), pl.BlockSpec((1, GLA_W), const)] + [ANY] * n_xc,
        out_shape=[jax.ShapeDtypeStruct((tp, W_GP), BF16), jax.ShapeDtypeStruct((128, GLA_KW), F32),
                   jax.ShapeDtypeStruct((1, GLA_KW), F32), jax.ShapeDtypeStruct((1, GLA_W), F32)] + xc_shapes,
        scratch_shapes=[pltpu.VMEM((GLA_HEADS, GLA_V, GLA_K), F32), pltpu.VMEM((GC, GLA_KW), F32),
                        pltpu.VMEM((GC, GLA_KW), F32)] + xc_sems,
        compiler_params=_cparams(1),
    )(gqk, gv, glr, gg, o_gla, da, states, wg, bg, gain, pmat, pmat_t, *(comm.srcs if comm else ()))


def _mid_call(a_ret, a_gla, mg, h0, tgt, wbr, wbg, wout, gf):
    tp = h0.shape[0]
    nt = tp // TM

    def body(ar_ref, ag_ref, mg_ref, h_ref, t_ref, wbr_ref, wbg_ref, wo_ref, gf_ref,
             dh1_ref, dar_ref, dag_ref, dm_ref, mb_ref, dh1b_ref, dprb_ref, dpgb_ref, loss_ref, dgf_ref):
        i = pl.program_id(0)

        @pl.when(i == 0)
        def _():
            loss_ref[...] = jnp.zeros_like(loss_ref)
            dgf_ref[...] = jnp.zeros_like(dgf_ref)

        ar, ag = ar_ref[...], ag_ref[...]
        pr = _dot(ar, wbr_ref[...], NN)
        pg = _dot(ag, wbg_ref[...], NN)
        sr = _sigmoid(mg_ref[:, :D_MODEL])
        sg = _sigmoid(mg_ref[:, D_MODEL:])
        merged = (sr * pr + sg * pg).astype(BF16)
        mb_ref[...] = merged
        h1 = h_ref[...] + _dot(merged, wo_ref[...], NN)
        r1 = lax.rsqrt(jnp.mean(h1 * h1, axis=-1, keepdims=True) + EPS)
        xh = h1 * r1
        gfv = gf_ref[...]
        live = jnp.where(i > 0, 1.0, 0.0).astype(F32)
        err = (xh * gfv - t_ref[...]) * live
        loss_ref[...] += jnp.full(loss_ref.shape, 0.5 / D_MODEL, F32) * jnp.sum(err * err)
        dy = err * (1.0 / D_MODEL)
        dgf_ref[...] += jnp.sum(dy * xh, axis=0, keepdims=True)
        dxh = dy * gfv
        dh1 = r1 * (dxh - xh * jnp.mean(dxh * xh, axis=-1, keepdims=True))
        dh1_ref[...] = dh1
        dh1b = dh1.astype(BF16)
        dh1b_ref[...] = dh1b
        dmerged = _dot(dh1b, wo_ref[...], NT)
        dm_ref[:, :D_MODEL] = (dmerged * pr * sr * (1.0 - sr)).astype(BF16)
        dm_ref[:, D_MODEL:] = (dmerged * pg * sg * (1.0 - sg)).astype(BF16)
        dpr = (dmerged * sr).astype(BF16)
        dpg = (dmerged * sg).astype(BF16)
        dprb_ref[...] = dpr
        dpgb_ref[...] = dpg
        dar_ref[...] = _dot(dpr, wbr_ref[...], NT)
        dag_ref[...] = _dot(dpg, wbg_ref[...], NT)

    tile = lambda w: pl.BlockSpec((TM, w), lambda i: (i, 0))
    const = lambda r, w: pl.BlockSpec((r, w), lambda i: (0, 0))
    return pl.pallas_call(
        body, name="merge_out_loss", grid=(nt,),
        in_specs=[tile(RET_W), tile(GLA_W), tile(W_M), tile(D_MODEL),
                  pl.BlockSpec((TM, D_MODEL), lambda i: (jnp.maximum(i - 1, 0), 0)),
                  const(RET_W, D_MODEL), const(GLA_W, D_MODEL), const(D_MODEL, D_MODEL), const(1, D_MODEL)],
        out_specs=[tile(D_MODEL), tile(RET_W), tile(GLA_W), tile(W_M), tile(D_MODEL), tile(D_MODEL), tile(D_MODEL),
                   tile(D_MODEL), const(1, 128), const(1, D_MODEL)],
        out_shape=[jax.ShapeDtypeStruct((tp, D_MODEL), F32), jax.ShapeDtypeStruct((tp, RET_W), F32),
                   jax.ShapeDtypeStruct((tp, GLA_W), F32), jax.ShapeDtypeStruct((tp, W_M), BF16),
                   jax.ShapeDtypeStruct((tp, D_MODEL), BF16), jax.ShapeDtypeStruct((tp, D_MODEL), BF16),
                   jax.ShapeDtypeStruct((tp, D_MODEL), BF16), jax.ShapeDtypeStruct((tp, D_MODEL), BF16),
                   jax.ShapeDtypeStruct((1, 128), F32), jax.ShapeDtypeStruct((1, D_MODEL), F32)],
        compiler_params=_cparams(1),
    )(a_ret, a_gla, mg, h0, tgt, wbr, wbg, wout, gf)


def _device_step(x2d, tgt2d, meta, norm_gain, w_in_bf, w_gate_up, b_gate, ret_gain, gla_gain, branch_parts,
                 final_gain, ck):
    seq = x2d.shape[0]
    tp = T0 + seq
    head = jnp.concatenate([jnp.zeros((PADF, D_MODEL), F32), meta], axis=0)
    w_r = w_in_bf
    w_g = jnp.pad(w_in_bf[:, W_R:W_R + W_G], ((0, 0), (0, W_GP - W_G)))
    w_m = w_in_bf[:, W_R + W_G:]
    wg_pad = jnp.pad(w_gate_up, ((0, 128 - GATE_RANK), (0, 0))).astype(BF16)

    pos = jnp.arange(tp, dtype=F32) - PADF
    half = RET_QK // 2
    inv = ROPE_BASE ** (-jnp.arange(half, dtype=F32) / half)
    ang = pos[:, None] * inv[None, :]
    cos, sin = jnp.cos(ang), jnp.sin(ang)
    lgam = jnp.log1p(-(2.0 ** (-5.0 - jnp.arange(RET_HEADS, dtype=F32))))
    pmat = jnp.asarray(_gla_tril(), BF16)
    pmat_t = jnp.asarray(_gla_tril().T.copy(), BF16)

    h0, u = _rms_call(x2d, head, norm_gain)
    tab = pl.BlockSpec((_proj_rows(tp), half), lambda j, i: (i, 0))
    rqk = _mm_nn("proj_rqk", u, w_r, BF16, D_MODEL, 0, 2 * D_MODEL, _rope_epilogue, (cos, sin), (tab, tab))
    rv = _mm_nn("proj_rv", u, w_r, BF16, D_MODEL, 2 * D_MODEL, RET_W)
    rg = _mm_nn("proj_rg", u, w_r, F32, D_MODEL, 4 * D_MODEL, RET_W)
    gqk = _mm_nn("proj_gqk", u, w_g, F32, 2 * GLA_KW, 0, 2 * GLA_KW, _gqk_epilogue)
    gv = _mm_nn("proj_gv", u, w_g, BF16, GLA_W, 2 * GLA_KW, GLA_W)
    gg = _mm_nn("proj_gg", u, w_g, F32, GLA_W, 2 * GLA_KW + GLA_W, GLA_W)
    glr = _mm_nn("proj_glr", u, w_g, F32, 128, 2 * GLA_KW + 2 * GLA_W, 128)
    mg = _mm_nn("proj_mg", u, w_m, F32, D_MODEL, 0, W_M)

    o_ret, a_ret, st_ret = _ret_fwd_call(rqk, rv, rg, ret_gain, lgam)
    o_gla, a_gla, st_gla, g_br, g_bg, g_out = _gla_fwd_call(gqk, gv, glr, gg, wg_pad, b_gate, gla_gain, pmat,
                                                            comm=_spread_plan(branch_parts))
    wbr = g_br.reshape(RET_W, D_MODEL)
    wbg = g_bg.reshape(GLA_W, D_MODEL)
    wout = g_out.reshape(D_MODEL, D_MODEL)

    gf = final_gain.reshape(1, D_MODEL)
    (dh1, da_ret, da_gla, dm, merged_b, dh1_b, dpr_b, dpg_b, loss, dgf) = _mid_call(
        a_ret, a_gla, mg, h0, tgt2d, wbr, wbg, wout, gf)

    names_b = ("w_branch_ret", "w_branch_gla", "w_out")
    g2_b = [_mm_tn("dw_br", a_ret, dpr_b, D_MODEL).reshape(4, 2, RET_W // 8, D_MODEL).transpose(1, 0, 2, 3),
            _mm_tn("dw_bg", a_gla, dpg_b, D_MODEL).reshape(4, 2, GLA_W // 8, D_MODEL).transpose(1, 0, 2, 3),
            _mm_tn("dw_out", merged_b, dh1_b, D_MODEL).reshape(4, 2, D_MODEL // 8, D_MODEL).transpose(1, 0, 2, 3)]
    sib_b = _swap_halves_call("swap_halves_branch", g2_b)
    sum_b = [_add_half_call("add_half_" + nm, g, b, ck) for nm, g, b in zip(names_b, g2_b, sib_b)]
    d_g, dwg, dbg, dgla_gain, *chips_b = _gla_bwd_call(gqk, gv, glr, gg, o_gla, da_gla, st_gla, wg_pad, b_gate,
                                                       gla_gain, pmat, pmat_t, comm=_exchange_plan(sum_b))
    mine = [_add_chips_call("add_chips_" + nm, g, b, p, ck) for nm, g, b, p in zip(names_b, g2_b, sib_b, chips_b)]

    d_r, dret_gain = _ret_bwd_call(rqk, rv, rg, o_ret, da_ret, st_ret, ret_gain, lgam, cos, sin)

    dwp = _mm_tn("dw_r", u, d_r, 2 * D_MODEL, out_cols=IN_PAD)
    dwp = _mm_tn("dw_g", u, d_g, D_MODEL, ncols=W_GP - 128, into=dwp, col0=W_R)
    dwp = _mm_tn("dw_glr", u, d_g, 128, ncols=128, bcol0=W_GP - 128, into=dwp, col0=W_R + W_GP - 128)
    g2_in = _place_merge_cols_call(dwp, _mm_tn("dw_m", u, dm, 2 * D_MODEL)).reshape(2, D_MODEL // 2, IN_PAD)

    du, sib_in = _mm_nt_acc("du_m", dm, w_m, W_M, comm=_swap_plan([g2_in]))
    sum_in = _add_rows_call("add_half_w_in", g2_in, sib_in, ck)
    du = _mm_nt_acc("du_g", d_g, w_g, W_GP, acc_in=du)[0]
    tile = pl.BlockSpec((TB, D_MODEL), lambda i, kk: (i, 0))
    row = pl.BlockSpec((1, D_MODEL), lambda i, kk: (0, 0))
    dx, dmeta, dnorm_gain, chips_in = _mm_nt_acc(
        "du_r", d_r, w_r, 2 * D_MODEL, acc_in=du, epilogue=_rms_bwd_epilogue, extras=(h0, norm_gain, dh1),
        extra_specs=(tile, row, tile),
        extra_out_shapes=(jax.ShapeDtypeStruct((seq, D_MODEL), F32), jax.ShapeDtypeStruct((N_META, D_MODEL), F32),
                          jax.ShapeDtypeStruct((1, D_MODEL), F32)),
        extra_out_specs=(ANY, pl.BlockSpec((N_META, D_MODEL), lambda i, kk: (0, 0)), row),
        extra_scratch=(pltpu.VMEM((2, TB, D_MODEL), F32), pltpu.SemaphoreType.DMA((2,))),
        comm=_exchange_window_plan(sum_in))
    mine = [_add_window_call("add_chips_w_in", g2_in, sib_in, chips_in, ck)] + mine
    full = _join_halves_call("join_halves", mine)

    return dict(loss=loss[0, 0], dx=dx, dmeta=dmeta, norm_gain=dnorm_gain, w_gate_up=dwg[:GATE_RANK], b_gate=dbg,
                ret_norm_gain=dret_gain, gla_norm_gain=dgla_gain, final_norm_gain=dgf.reshape(D_MODEL),
                w_in=full[0], w_branch_ret=full[1], w_branch_gla=full[2], w_out=full[3])


MESH = pl.DeviceIdType.MESH
ANY = pl.BlockSpec(memory_space=pl.ANY)


def _place():
    return lax.axis_index("x"), lax.axis_index("y"), lax.axis_index("c")


def _gather8_call(name, parts, relay=()):
    n = len(parts)
    relay = tuple(relay) + (False,) * (n - len(relay))

    def body(*refs):
        x_refs, out_refs = refs[:n], refs[n:2 * n]
        send_sems, recv_sems, local_sems = refs[2 * n:]
        x, y, c = _place()
        me, sibling = (x, y, c), (x, y, 1 - c)
        xn, yn, dg = (1 - x, y), (x, 1 - y), (1 - x, 1 - y)

        def slot(t, px, py, pc, half=None):
            ref = out_refs[t].at[4 * px + 2 * py + pc]
            if half is None:
                return ref
            rows = ref.shape[0] // 2
            return ref.at[pl.ds(half * rows, rows)]

        def copy(t, k, dst, to, src=None):
            return pltpu.make_async_remote_copy(
                src_ref=dst if src is None else src, dst_ref=dst, send_sem=send_sems.at[8 * t + k],
                recv_sem=recv_sems.at[8 * t + k], device_id=to, device_id_type=MESH)

        mine = [pltpu.make_async_copy(x_refs[t], slot(t, *me), local_sems.at[t]) for t in range(n)]
        for cp in mine:
            cp.start()
        sent = []
        for t in range(n):
            sent.append(copy(t, 0, slot(t, *me), sibling, src=x_refs[t]))
            sent.append(copy(t, 1, slot(t, *me), (*xn, c), src=x_refs[t]))
            sent.append(copy(t, 2, slot(t, *me), (*yn, c), src=x_refs[t]))
            if not relay[t]:
                sent.append(copy(t, 3, slot(t, *me), (*dg, c), src=x_refs[t]))
        for cp in sent:
            cp.start()

        def start(cp):
            cp.start()
            sent.append(cp)

        for t in range(n):
            copy(t, 2, slot(t, *yn, c), me).wait_recv()
            if relay[t]:
                start(copy(t, 3, slot(t, *yn, c, half=0), (*xn, c)))
            start(copy(t, 6, slot(t, *yn, c), sibling))
        for t in range(n):
            copy(t, 1, slot(t, *xn, c), me).wait_recv()
            if relay[t]:
                start(copy(t, 4, slot(t, *xn, c, half=1), (*yn, c)))
            start(copy(t, 5, slot(t, *xn, c), sibling))
        for t in range(n):
            if relay[t]:
                copy(t, 3, slot(t, *dg, c, half=0), me).wait_recv()
                copy(t, 4, slot(t, *dg, c, half=1), me).wait_recv()
            else:
                copy(t, 3, slot(t, *dg, c), me).wait_recv()
            start(copy(t, 7, slot(t, *dg, c), sibling))
        for t in range(n):
            copy(t, 0, slot(t, *sibling), me).wait_recv()
            copy(t, 5, slot(t, *xn, 1 - c), me).wait_recv()
            copy(t, 6, slot(t, *yn, 1 - c), me).wait_recv()
            copy(t, 7, slot(t, *dg, 1 - c), me).wait_recv()
        for cp in sent:
            cp.wait_send()
        for cp in mine:
            cp.wait()

    return pl.pallas_call(
        body, name=name,
        out_shape=[jax.ShapeDtypeStruct((8,) + p.shape, p.dtype) for p in parts],
        in_specs=[ANY] * n, out_specs=[ANY] * n,
        scratch_shapes=[pltpu.SemaphoreType.DMA((8 * n,)), pltpu.SemaphoreType.DMA((8 * n,)),
                        pltpu.SemaphoreType.DMA((n,))],
    )(*parts)


def _swap_halves_call(name, gs):
    n = len(gs)

    def body(*refs):
        g_refs, b_refs = refs[:n], refs[n:2 * n]
        send_sems, recv_sems = refs[2 * n:]
        x, y, c = _place()
        copies = [pltpu.make_async_remote_copy(
            src_ref=g_refs[t].at[1 - c], dst_ref=b_refs[t], send_sem=send_sems.at[t], recv_sem=recv_sems.at[t],
            device_id=(x, y, 1 - c), device_id_type=MESH) for t in range(n)]
        for cp in copies:
            cp.start()
        for cp in copies:
            cp.wait()

    return pl.pallas_call(
        body, name=name,
        out_shape=[jax.ShapeDtypeStruct(g.shape[1:], g.dtype) for g in gs],
        in_specs=[ANY] * n, out_specs=[ANY] * n,
        scratch_shapes=[pltpu.SemaphoreType.DMA((n,)), pltpu.SemaphoreType.DMA((n,))],
    )(*gs)


def _join_halves_call(name, ts):
    n = len(ts)

    def body(*refs):
        o_refs = refs[n:2 * n]
        send_sems, recv_sems = refs[2 * n:]
        x, y, c = _place()
        copies = [pltpu.make_async_remote_copy(
            src_ref=o_refs[t].at[c], dst_ref=o_refs[t].at[c], send_sem=send_sems.at[t], recv_sem=recv_sems.at[t],
            device_id=(x, y, 1 - c), device_id_type=MESH) for t in range(n)]
        for cp in copies:
            cp.start()
        for t in range(n):
            copies[t].wait_send()
            pltpu.make_async_remote_copy(
                src_ref=o_refs[t].at[c], dst_ref=o_refs[t].at[1 - c], send_sem=send_sems.at[t],
                recv_sem=recv_sems.at[t], device_id=(x, y, 1 - c), device_id_type=MESH).wait_recv()

    return pl.pallas_call(
        body, name=name,
        out_shape=[jax.ShapeDtypeStruct(t.shape, t.dtype) for t in ts],
        in_specs=[ANY] * n, out_specs=[ANY] * n, input_output_aliases={t: t for t in range(n)},
        scratch_shapes=[pltpu.SemaphoreType.DMA((n,)), pltpu.SemaphoreType.DMA((n,))],
    )(*ts)


def _row_block(rows, cols, budget):
    best = 8
    for rb in range(8, rows + 1, 8):
        if rows % rb == 0 and rb * cols * 4 <= budget:
            best = rb
    return best


def _add_half_call(name, g, b, ck):
    _, _, r, cc = g.shape
    rb = _row_block(r, cc, 2 * 1024 * 1024)

    def body(ck_ref, g_ref, b_ref, o_ref):
        o_ref[...] = (g_ref[...] + b_ref[...]).astype(BF16)

    return pl.pallas_call(
        body, name=name,
        grid_spec=pltpu.PrefetchScalarGridSpec(
            num_scalar_prefetch=1, grid=(4, r // rb),
            in_specs=[pl.BlockSpec((None, None, rb, cc), lambda k, i, ck_ref: (ck_ref[0], k, i, 0)),
                      pl.BlockSpec((None, rb, cc), lambda k, i, ck_ref: (k, i, 0))],
            out_specs=pl.BlockSpec((None, rb, cc), lambda k, i, ck_ref: (k, i, 0))),
        out_shape=jax.ShapeDtypeStruct(b.shape, BF16),
        compiler_params=_cparams(2),
    )(ck, g, b)


def _add_rows_call(name, g, b, ck):
    _, r, cc = g.shape
    rb = _row_block(r, cc, 2 * 1024 * 1024)

    def body(ck_ref, g_ref, b_ref, o_ref):
        o_ref[...] = (g_ref[...] + b_ref[...]).astype(BF16)

    return pl.pallas_call(
        body, name=name,
        grid_spec=pltpu.PrefetchScalarGridSpec(
            num_scalar_prefetch=1, grid=(r // rb,),
            in_specs=[pl.BlockSpec((None, rb, cc), lambda i, ck_ref: (ck_ref[0], i, 0)),
                      pl.BlockSpec((rb, cc), lambda i, ck_ref: (i, 0))],
            out_specs=pl.BlockSpec((rb, cc), lambda i, ck_ref: (i, 0))),
        out_shape=jax.ShapeDtypeStruct((r, cc), BF16),
        compiler_params=_cparams(1),
    )(ck, g, b)


def _add_window_call(name, g, b, p, ck):
    _, r, _ = g.shape
    nb, step = WIN_W // 128, WIN_STEP // 128

    def body(ck_ref, g_ref, b_ref, p0_ref, p1_ref, p2_ref, o_ref):
        own = g_ref[...] + b_ref[...]
        o_ref[...] = ((own + p0_ref[...].astype(F32)) + p1_ref[...].astype(F32)) + p2_ref[...].astype(F32)

    def peer(j):
        return pl.BlockSpec((None, r, 128), lambda i, ck_ref: (j, 0, i))

    return pl.pallas_call(
        body, name=name,
        grid_spec=pltpu.PrefetchScalarGridSpec(
            num_scalar_prefetch=1, grid=(nb,),
            in_specs=[pl.BlockSpec((None, r, 128), lambda i, ck_ref: (ck_ref[0], 0, step * ck_ref[1] + i)),
                      pl.BlockSpec((r, 128), lambda i, ck_ref: (0, step * ck_ref[1] + i)),
                      peer(0), peer(1), peer(2)],
            out_specs=pl.BlockSpec((None, r, 128), lambda i, ck_ref: (ck_ref[0], 0, i))),
        out_shape=jax.ShapeDtypeStruct((2, r, WIN_W), F32),
        compiler_params=_cparams(1),
    )(ck, g, b, p, p, p)


def _add_chips_call(name, g, b, p, ck):
    _, _, r, cc = g.shape
    rb = _row_block(r, cc, 2 * 1024 * 1024)

    def body(ck_ref, g_ref, b_ref, p0_ref, p1_ref, p2_ref, o_ref):
        own = g_ref[...] + b_ref[...]
        o_ref[...] = ((own + p0_ref[...].astype(F32)) + p1_ref[...].astype(F32)) + p2_ref[...].astype(F32)

    def peer(j):
        return pl.BlockSpec((None, rb, cc), lambda i, ck_ref: (j, i, 0))

    return pl.pallas_call(
        body, name=name,
        grid_spec=pltpu.PrefetchScalarGridSpec(
            num_scalar_prefetch=1, grid=(r // rb,),
            in_specs=[pl.BlockSpec((None, None, rb, cc), lambda i, ck_ref: (ck_ref[0], ck_ref[1], i, 0)),
                      pl.BlockSpec((None, rb, cc), lambda i, ck_ref: (ck_ref[1], i, 0)),
                      peer(0), peer(1), peer(2)],
            out_specs=pl.BlockSpec((None, rb, cc), lambda i, ck_ref: (ck_ref[0], i, 0))),
        out_shape=jax.ShapeDtypeStruct((2, r, cc), F32),
        compiler_params=_cparams(1),
    )(ck, g, b, p, p, p)


def _sum8_call(name, g):
    def body(g_ref, o_ref):
        acc = g_ref[0]
        for d in range(1, 8):
            acc = acc + g_ref[d]
        o_ref[...] = acc

    return pl.pallas_call(body, name=name, out_shape=jax.ShapeDtypeStruct(g.shape[1:], F32))(g)


def _adamw_call(name, w, g, m, v):
    r, cc = w.shape
    if r % 8 == 0 or r * cc * 4 <= 1024 * 1024:
        rb = _row_block(r, cc, 1024 * 1024) if r % 8 == 0 else r
        grid, spec = (r // rb,), pl.BlockSpec((rb, cc), lambda i: (i, 0))
    else:
        grid, spec = (cc // 128,), pl.BlockSpec((r, 128), lambda i: (0, i))

    def body(w_ref, g_ref, m_ref, v_ref, d_ref, m2_ref, v2_ref):
        gv = g_ref[...]
        m2 = ADAM_B1 * m_ref[...] + (1.0 - ADAM_B1) * gv
        v2 = ADAM_B2 * v_ref[...] + (1.0 - ADAM_B2) * (gv * gv)
        m_hat = m2 / (1.0 - ADAM_B1 ** ADAM_STEP)
        v_hat = v2 / (1.0 - ADAM_B2 ** ADAM_STEP)
        d_ref[...] = -ADAM_LR * (m_hat / (jnp.sqrt(v_hat) + ADAM_EPS) + ADAM_WD * w_ref[...])
        m2_ref[...] = m2
        v2_ref[...] = v2

    return pl.pallas_call(
        body, name=name, grid=grid, in_specs=[spec] * 4, out_specs=[spec] * 3,
        out_shape=[jax.ShapeDtypeStruct((r, cc), F32)] * 3, compiler_params=_cparams(1),
    )(w, g, m, v)


SMALL = (("norm_gain", D_MODEL), ("b_gate", GLA_KW), ("ret_norm_gain", RET_W), ("gla_norm_gain", GLA_W),
         ("final_norm_gain", D_MODEL), ("w_gate_up", GATE_RANK * GLA_KW), ("meta_tokens", N_META * D_MODEL),
         ("loss", 1))


def _pack_rows(vecs, rows):
    flat = jnp.concatenate([v.reshape(-1) for v in vecs])
    return jnp.pad(flat, (0, rows * 128 - flat.shape[0])).reshape(rows, 128)


def kernel(x, meta_tokens, norm_gain, w_in, w_gate_up, b_gate, ret_norm_gain, gla_norm_gain, w_branch_ret, w_branch_gla, w_out, final_norm_gain, loss_target, m_meta_tokens, m_norm_gain, m_w_in, m_w_gate_up, m_b_gate, m_ret_norm_gain, m_gla_norm_gain, m_w_branch_ret, m_w_branch_gla, m_w_out, m_final_norm_gain, v_meta_tokens, v_norm_gain, v_w_in, v_w_gate_up, v_b_gate, v_ret_norm_gain, v_gla_norm_gain, v_w_branch_ret, v_w_branch_gla, v_w_out, v_final_norm_gain):
    xi, yi, ci = _place()
    kme = 2 * xi + yi
    ck = jnp.stack([ci, kme]).astype(jnp.int32)
    sw_in = w_in.shape[2]

    def my_half(a, dtype):
        r, cc = a.shape
        return lax.dynamic_index_in_dim(a.reshape(2, r // 2, cc), ci, 0, keepdims=False).astype(dtype)

    g_in, g_meta, g_wg = _gather8_call(
        "gather_weights", [my_half(w_in[0], BF16), my_half(meta_tokens, F32), my_half(w_gate_up[0], F32)],
        relay=(True,))
    branch_parts = [my_half(w_branch_ret[0], BF16), my_half(w_branch_gla[0], BF16), my_half(w_out[0], BF16)]
    w_in_bf = g_in.reshape(4, 2, D_MODEL // 2, sw_in).transpose(1, 2, 0, 3).reshape(D_MODEL, 4 * sw_in)
    meta = g_meta.reshape(4, 2, N_META // 2, D_MODEL // 4).transpose(1, 2, 0, 3).reshape(N_META, D_MODEL)
    wg_full = g_wg.reshape(4, 2, GATE_RANK // 2, GLA_KW // 4).transpose(1, 2, 0, 3).reshape(GATE_RANK, GLA_KW)

    loc = _device_step(x[0], loss_target[0], meta, norm_gain, w_in_bf, wg_full, b_gate, ret_norm_gain, gla_norm_gain,
                       branch_parts, final_norm_gain, ck)
    names = ("w_in", "w_branch_ret", "w_branch_gla", "w_out")
    full = [loc[nm] for nm in names]
    big_w = dict(w_in=w_in[0], w_branch_ret=w_branch_ret[0], w_branch_gla=w_branch_gla[0], w_out=w_out[0])
    big_m = dict(w_in=m_w_in[0], w_branch_ret=m_w_branch_ret[0], w_branch_gla=m_w_branch_gla[0], w_out=m_w_out[0])
    big_v = dict(w_in=v_w_in[0], w_branch_ret=v_w_branch_ret[0], w_branch_gla=v_w_branch_gla[0], w_out=v_w_out[0])
    grads, deltas, new_m, new_v = {}, {}, {}, {}
    for nm, f in zip(names, full):
        shape = big_w[nm].shape
        if nm == "w_in":
            f = lax.dynamic_slice_in_dim(f, (sw_in - WIN_STEP) * kme, sw_in, axis=2)
        g = f.reshape(shape)
        if nm == "w_in":
            d, m2, v2 = (a.T for a in _adamw_call("adamw_" + nm, big_w[nm].T, g.T, big_m[nm].T, big_v[nm].T))
        else:
            d, m2, v2 = _adamw_call("adamw_" + nm, big_w[nm], g, big_m[nm], big_v[nm])
        grads[nm], deltas[nm], new_m[nm], new_v[nm] = (a.reshape((1,) + shape) for a in (g, d, m2, v2))

    small_g = dict(loc)
    small_g["meta_tokens"] = loc["dmeta"]
    n_small = sum(sz for _, sz in SMALL)
    rows = -(-n_small // 128 // 8) * 8
    (g_small,) = _gather8_call("gather_small_grads", [_pack_rows([small_g[nm] for nm, _ in SMALL], rows)])
    tot = _sum8_call("sum_small_grads", g_small).reshape(-1)
    off = 0
    sg = {}
    for nm, sz in SMALL:
        sg[nm] = tot[off:off + sz]
        off += sz
    loss = sg.pop("loss")[0]
    sg["w_gate_up"] = lax.dynamic_slice_in_dim(sg["w_gate_up"].reshape(GATE_RANK, GLA_KW), kme * (GLA_KW // 4),
                                               GLA_KW // 4, axis=1)
    sg["meta_tokens"] = lax.dynamic_slice_in_dim(sg["meta_tokens"].reshape(N_META, D_MODEL), kme * (D_MODEL // 4),
                                                 D_MODEL // 4, axis=1)
    small_w = dict(norm_gain=norm_gain, b_gate=b_gate, ret_norm_gain=ret_norm_gain, gla_norm_gain=gla_norm_gain,
                   final_norm_gain=final_norm_gain, w_gate_up=w_gate_up, meta_tokens=meta_tokens)
    small_m = dict(norm_gain=m_norm_gain, b_gate=m_b_gate, ret_norm_gain=m_ret_norm_gain,
                   gla_norm_gain=m_gla_norm_gain, final_norm_gain=m_final_norm_gain, w_gate_up=m_w_gate_up,
                   meta_tokens=m_meta_tokens)
    small_v = dict(norm_gain=v_norm_gain, b_gate=v_b_gate, ret_norm_gain=v_ret_norm_gain,
                   gla_norm_gain=v_gla_norm_gain, final_norm_gain=v_final_norm_gain, w_gate_up=v_w_gate_up,
                   meta_tokens=v_meta_tokens)
    for nm in small_w:
        shape = small_w[nm].shape
        as2d = lambda a: a.reshape((-1, shape[-1]))
        grads[nm] = sg[nm].reshape(shape)
        deltas[nm], new_m[nm], new_v[nm] = (a.reshape(shape) for a in _adamw_call(
            "adamw_" + nm, as2d(small_w[nm]), as2d(sg[nm]), as2d(small_m[nm]), as2d(small_v[nm])))

    out_order = ("meta_tokens", "norm_gain", "w_in", "w_gate_up", "b_gate", "ret_norm_gain", "gla_norm_gain",
                 "w_branch_ret", "w_branch_gla", "w_out", "final_norm_gain")
    dx = loc["dx"].reshape(x.shape)
    return (loss, dx, *[grads[nm] for nm in out_order], *[deltas[nm] for nm in out_order],
            *[new_m[nm] for nm in out_order], *[new_v[nm] for nm in out_order])
```

```python
import math
from typing import Callable, NamedTuple

import numpy as np
import jax
import jax.numpy as jnp
from jax import lax
from jax.experimental import pallas as pl
from jax.experimental.pallas import tpu as pltpu

F32 = jnp.float32
BF16 = jnp.bfloat16

D_MODEL = 1024
N_META = 16
EPS = 1e-6
ROPE_BASE = 10000.0
RET_HEADS, RET_QK, RET_V = 4, 256, 512
RET_W = RET_HEADS * RET_V
GLA_HEADS, GLA_K, GLA_V = 4, 128, 256
GLA_W = GLA_HEADS * GLA_V
GLA_KW = GLA_HEADS * GLA_K
GATE_RANK = 16
GATE_TAU = 16.0
GLA_SUB = 16

TM = 256
T0 = TM
PADF = T0 - N_META
GC = 128
TB = 768
TK = 768

W_R = 6144
W_G = 3088
W_GP = 3200
W_M = 2048
IN_COLS = W_R + W_G + W_M
WIN_STEP = (IN_COLS // 4) // 128 * 128
WIN_W = -(-(3 * (IN_COLS // 4 - WIN_STEP) + IN_COLS // 4) // 128) * 128
IN_PAD = 3 * WIN_STEP + WIN_W

ADAM_LR, ADAM_B1, ADAM_B2, ADAM_EPS, ADAM_WD, ADAM_STEP = 0.001, 0.9, 0.999, 1e-08, 0.01, 10

VMEM_LIMIT = 56 * 1024 * 1024

NN = ((1,), (0,))
NT = ((1,), (1,))
TN = ((0,), (0,))


def _dot(a, b, dims):
    return lax.dot_general(a, b, (dims, ((), ())), preferred_element_type=F32)


def _cparams(n_axes):
    return pltpu.CompilerParams(dimension_semantics=("arbitrary",) * n_axes, vmem_limit_bytes=VMEM_LIMIT)


def _sigmoid(x):
    return 0.5 * jnp.tanh(0.5 * x) + 0.5


def _split3(x):
    hi = x.astype(BF16)
    r1 = x - hi.astype(F32)
    mid = r1.astype(BF16)
    lo = (r1 - mid.astype(F32)).astype(BF16)
    return hi, mid, lo


def _exact_pm(p, x):
    hi, mid, lo = _split3(x)
    return _dot(p, hi, NN) + _dot(p, mid, NN) + _dot(p, lo, NN)


def _rms_call(x2d, head, gain):
    tp = T0 + x2d.shape[0]

    def body(x_ref, hd_ref, g_ref, h_ref, u_ref):
        h = jnp.where(pl.program_id(0) == 0, hd_ref[...], x_ref[...])
        h_ref[...] = h
        r = lax.rsqrt(jnp.mean(h * h, axis=-1, keepdims=True) + EPS)
        u_ref[...] = (h * r * g_ref[...]).astype(BF16)

    tile = pl.BlockSpec((TM, D_MODEL), lambda i: (i, 0))
    return pl.pallas_call(
        body, name="rms_in", grid=(tp // TM,),
        in_specs=[pl.BlockSpec((TM, D_MODEL), lambda i: (jnp.maximum(i - 1, 0), 0)),
                  pl.BlockSpec((T0, D_MODEL), lambda i: (0, 0)), pl.BlockSpec((1, D_MODEL), lambda i: (0, 0))],
        out_specs=[tile, tile],
        out_shape=[jax.ShapeDtypeStruct((tp, D_MODEL), F32), jax.ShapeDtypeStruct((tp, D_MODEL), BF16)],
        compiler_params=_cparams(1),
    )(x2d, head, gain)


PROJ_ROWS_MAX = 1408


def _proj_rows(m):
    return max(r for r in range(16, PROJ_ROWS_MAX + 1, 16) if m % r == 0)


def _mm_nn(name, a, b, out_dtype, tn, col0, ncols, epilogue=None, extras=(), extra_specs=()):
    m, k = a.shape
    nj, j0 = ncols // tn, col0 // tn
    tb = _proj_rows(m)

    def body(a_ref, b_ref, *rest):
        *ex, o_ref = rest
        acc = _dot(a_ref[...], b_ref[...], NN)
        if epilogue is None:
            o_ref[...] = acc.astype(out_dtype)
        else:
            epilogue(acc, o_ref, *ex)

    return pl.pallas_call(
        body, name=name, grid=(nj, m // tb),
        in_specs=[pl.BlockSpec((tb, k), lambda j, i: (i, 0)), pl.BlockSpec((k, tn), lambda j, i: (0, j0 + j))]
        + list(extra_specs),
        out_specs=pl.BlockSpec((tb, tn), lambda j, i: (i, j)),
        out_shape=jax.ShapeDtypeStruct((m, ncols), out_dtype),
        compiler_params=_cparams(2),
    )(a, b, *extras)


def _rope_epilogue(acc, o_ref, cos_ref, sin_ref):
    scale = jnp.where(pl.program_id(0) == 1, RET_QK ** -0.5, 1.0).astype(F32)
    cos, sin = cos_ref[...], sin_ref[...]
    half = RET_QK // 2
    for h in range(RET_HEADS):
        t1 = acc[:, h * RET_QK:h * RET_QK + half]
        t2 = acc[:, h * RET_QK + half:(h + 1) * RET_QK]
        o_ref[:, h * RET_QK:h * RET_QK + half] = ((t1 * cos - t2 * sin) * scale).astype(BF16)
        o_ref[:, h * RET_QK + half:(h + 1) * RET_QK] = ((t2 * cos + t1 * sin) * scale).astype(BF16)


def _gqk_epilogue(acc, o_ref):
    o_ref[:, :GLA_KW] = acc[:, :GLA_KW] * (GLA_K ** -0.5)
    o_ref[:, GLA_KW:] = acc[:, GLA_KW:]


class _Comm(NamedTuple):
    srcs: tuple
    out_shapes: tuple
    n_sems: int
    make: Callable


def _comm_sems(comm):
    return [pltpu.SemaphoreType.DMA((comm.n_sems,)), pltpu.SemaphoreType.DMA((comm.n_sems,))]


def _other_chips(x, y):
    return [(1 - x, y), (x, 1 - y), (1 - x, 1 - y)]


def _exchange_plan(ss):
    def make(s_refs, b_refs, send_sems, recv_sems):
        x, y, c = _place()
        return [pltpu.make_async_remote_copy(
            src_ref=s_refs[t].at[2 * chip[0] + chip[1]], dst_ref=b_refs[t].at[j], send_sem=send_sems.at[3 * t + j],
            recv_sem=recv_sems.at[3 * t + j], device_id=(*chip, c), device_id_type=MESH)
            for t in range(len(s_refs)) for j, chip in enumerate(_other_chips(x, y))]

    return _Comm(tuple(ss), tuple(jax.ShapeDtypeStruct((3,) + s.shape[1:], s.dtype) for s in ss), 3 * len(ss), make)


def _exchange_window_plan(s):
    def make(s_refs, b_refs, send_sems, recv_sems):
        x, y, c = _place()
        return [pltpu.make_async_remote_copy(
            src_ref=s_refs[0].at[:, pl.ds(pl.multiple_of((2 * chip[0] + chip[1]) * WIN_STEP, 128), WIN_W)],
            dst_ref=b_refs[0].at[j], send_sem=send_sems.at[j], recv_sem=recv_sems.at[j], device_id=(*chip, c),
            device_id_type=MESH) for j, chip in enumerate(_other_chips(x, y))]

    return _Comm((s,), (jax.ShapeDtypeStruct((3, s.shape[0], WIN_W), s.dtype),), 3, make)


def _swap_plan(gs):
    def make(g_refs, b_refs, send_sems, recv_sems):
        x, y, c = _place()
        return [pltpu.make_async_remote_copy(
            src_ref=g_refs[t].at[1 - c], dst_ref=b_refs[t], send_sem=send_sems.at[t], recv_sem=recv_sems.at[t],
            device_id=(x, y, 1 - c), device_id_type=MESH) for t in range(len(g_refs))]

    return _Comm(tuple(gs), tuple(jax.ShapeDtypeStruct(g.shape[1:], g.dtype) for g in gs), len(gs), make)


def _spread_plan(parts):
    def make(p_refs, o_refs, send_sems, recv_sems):
        x, y, c = _place()
        copies = []
        for t in range(len(p_refs)):
            mine = o_refs[t].at[4 * x + 2 * y + c]
            copies.append(pltpu.make_async_copy(p_refs[t], mine, send_sems.at[7 * len(p_refs) + t]))
            for r in range(1, 8):
                peer = (1 - x if r & 4 else x, 1 - y if r & 2 else y, 1 - c if r & 1 else c)
                copies.append(pltpu.make_async_remote_copy(
                    src_ref=p_refs[t], dst_ref=mine, send_sem=send_sems.at[7 * t + r - 1],
                    recv_sem=recv_sems.at[7 * t + r - 1], device_id=peer, device_id_type=MESH))
        return copies

    return _Comm(tuple(parts), tuple(jax.ShapeDtypeStruct((8,) + p.shape, p.dtype) for p in parts), 8 * len(parts),
                 make)


def _mm_nt_acc(name, a, w, tk, acc_in=None, epilogue=None, extras=(), extra_specs=(), extra_out_shapes=(),
               extra_out_specs=(), extra_scratch=(), comm=None):
    m, k = a.shape
    n = w.shape[0]
    nk, ni = k // tk, m // TB
    has_acc = acc_in is not None
    n_xc = len(comm.srcs) if comm else 0
    n_es = len(extra_scratch)

    def body(*refs):
        a_ref, w_ref = refs[0], refs[1]
        pos = 2
        acc_ref = None
        if has_acc:
            acc_ref = refs[pos]
            pos += 1
        ex = refs[pos:pos + len(extras)]
        pos += len(extras)
        xc_src = refs[pos:pos + n_xc]
        pos += n_xc
        n_scr = 1 + n_es + (2 if n_xc else 0)
        outs = refs[pos:len(refs) - n_scr - n_xc]
        xc_dst = refs[len(refs) - n_scr - n_xc:len(refs) - n_scr]
        scr = refs[len(refs) - n_scr]
        es = refs[len(refs) - n_scr + 1:len(refs) - n_scr + 1 + n_es]
        i, kk = pl.program_id(0), pl.program_id(1)
        if n_xc:
            copies = comm.make(xc_src, xc_dst, refs[-2], refs[-1])

            @pl.when((i == 0) & (kk == 0))
            def _():
                for cp in copies:
                    cp.start()

        @pl.when(kk == 0)
        def _():
            scr[...] = acc_ref[...] if has_acc else jnp.zeros_like(scr)

        scr[...] += _dot(a_ref[...], w_ref[...], NT)

        @pl.when(kk == nk - 1)
        def _():
            if epilogue is None:
                outs[0][...] = scr[...]
            else:
                epilogue(scr[...], outs, i, ni, *ex, *es)

        if n_xc:
            @pl.when((i == ni - 1) & (kk == nk - 1))
            def _():
                for cp in copies:
                    cp.wait()

    in_specs = [pl.BlockSpec((TB, tk), lambda i, kk: (i, kk)), pl.BlockSpec((n, tk), lambda i, kk: (0, kk))]
    args = [a, w]
    if has_acc:
        in_specs.append(pl.BlockSpec((TB, n), lambda i, kk: (i, 0)))
        args.append(acc_in)
    in_specs += list(extra_specs) + [ANY] * n_xc
    args += list(extras) + (list(comm.srcs) if comm else [])
    if epilogue is None:
        out_shape = [jax.ShapeDtypeStruct((m, n), F32)]
        out_specs = [pl.BlockSpec((TB, n), lambda i, kk: (i, 0))]
    else:
        out_shape, out_specs = list(extra_out_shapes), list(extra_out_specs)
    scratch = [pltpu.VMEM((TB, n), F32)] + list(extra_scratch)
    if n_xc:
        out_shape += list(comm.out_shapes)
        out_specs += [ANY] * n_xc
        scratch += _comm_sems(comm)
    return pl.pallas_call(
        body, name=name, grid=(ni, nk), in_specs=in_specs, out_specs=out_specs, out_shape=out_shape,
        scratch_shapes=scratch, compiler_params=_cparams(2),
    )(*args)


def _rms_bwd_epilogue(du, outs, i, ni, h_ref, g_ref, dh1_ref, obuf, sems):
    dx_ref, dmeta_ref, dg_ref = outs
    h = h_ref[...]
    r = lax.rsqrt(jnp.mean(h * h, axis=-1, keepdims=True) + EPS)
    xh = h * r
    dxh = du * g_ref[...]
    dh0 = dh1_ref[...] + r * (dxh - xh * jnp.mean(dxh * xh, axis=-1, keepdims=True))

    def put(slot, tile):
        return pltpu.make_async_copy(obuf.at[slot], dx_ref.at[pl.ds(pl.multiple_of(tile * TB - T0, 8), TB)],
                                     sems.at[slot])

    @pl.when(i == 0)
    def _():
        dg_ref[...] = jnp.zeros_like(dg_ref)
        dmeta_ref[...] = dh0[PADF:T0, :]
        obuf[0] = dh0
        first = pltpu.make_async_copy(obuf.at[0, pl.ds(T0, TB - T0)], dx_ref.at[pl.ds(0, TB - T0)], sems.at[0])
        first.start()
        first.wait()

    @pl.when(i >= 1)
    def _():
        slot = i % 2

        @pl.when(i >= 3)
        def _():
            put(slot, i - 2).wait()

        obuf[slot] = dh0
        put(slot, i).start()

    dg_ref[...] += jnp.sum(du * xh, axis=0, keepdims=True)

    @pl.when(i == ni - 1)
    def _():
        for tile in (ni - 2, ni - 1):
            if tile >= 1:
                put(tile % 2, tile).wait()


def _mm_tn(name, a, b, bn, ncols=None, bcol0=0, into=None, col0=0, out_cols=None):
    t, m = a.shape
    n = ncols or b.shape[1]
    j0, bj0 = col0 // bn, bcol0 // bn

    def body(a_ref, b_ref, *rest):
        o_ref = rest[-1]

        @pl.when(pl.program_id(1) == 0)
        def _():
            o_ref[...] = jnp.zeros_like(o_ref)

        o_ref[...] += _dot(a_ref[...], b_ref[...], TN)

    in_specs = [pl.BlockSpec((TK, m), lambda j, kk: (kk, 0)), pl.BlockSpec((TK, bn), lambda j, kk: (kk, bj0 + j))]
    args = [a, b]
    aliases = {}
    if into is not None:
        in_specs.append(ANY)
        args.append(into)
        aliases = {2: 0}
        out_cols = into.shape[1]
    return pl.pallas_call(
        body, name=name, grid=(n // bn, t // TK), in_specs=in_specs,
        out_specs=pl.BlockSpec((m, bn), lambda j, kk: (0, j0 + j)),
        out_shape=jax.ShapeDtypeStruct((m, out_cols or n), F32), input_output_aliases=aliases,
        compiler_params=_cparams(2),
    )(*args)


def _place_merge_cols_call(dwp, dw_m):
    c0 = W_R + W_GP - 128
    tail = IN_PAD - c0
    rows = 256

    def body(m_ref, p_ref, o_ref, buf, low, sem):
        get = pltpu.make_async_copy(o_ref.at[:, pl.ds(c0, 128)], low, sem)
        get.start()
        get.wait()
        for r in range(0, D_MODEL, rows):
            buf[r:r + rows, :] = jnp.concatenate(
                [low[r:r + rows, :GATE_RANK], m_ref[r:r + rows, :],
                 jnp.zeros((rows, tail - GATE_RANK - W_M), F32)], axis=1)
        put = pltpu.make_async_copy(buf, o_ref.at[:, pl.ds(c0, tail)], sem)
        put.start()
        put.wait()

    return pl.pallas_call(
        body, name="place_merge_cols",
        in_specs=[pl.BlockSpec(memory_space=pltpu.VMEM), ANY], out_specs=ANY,
        out_shape=jax.ShapeDtypeStruct(dwp.shape, F32), input_output_aliases={1: 0},
        scratch_shapes=[pltpu.VMEM((D_MODEL, tail), F32), pltpu.VMEM((D_MODEL, 128), F32), pltpu.SemaphoreType.DMA],
        compiler_params=pltpu.CompilerParams(vmem_limit_bytes=VMEM_LIMIT),
    )(dw_m, dwp)


def _ret_fill_decay(lg_ref, dm_scr):
    c = TM
    ii = lax.broadcasted_iota(jnp.int32, (c, c), 0)
    jj = lax.broadcasted_iota(jnp.int32, (c, c), 1)
    rel = (ii - jj).astype(F32)
    for h in range(RET_HEADS):
        dm_scr[h] = jnp.where(rel >= 0, jnp.exp(jnp.maximum(rel, 0.0) * lg_ref[h]), 0.0)


def _ret_consts(lg, dm_ref):
    c = TM
    idx = lax.broadcasted_iota(jnp.int32, (c, 1), 0).astype(F32)
    xi = jnp.exp((idx + 1.0) * lg)
    zeta = jnp.exp((c - 1.0 - idx) * lg)
    gc = jnp.exp(jnp.full((1, 1), c, F32) * lg)
    return dm_ref[...], xi, zeta, gc


def _ret_fwd_call(rqk, rv, rg, gain, lgam):
    tp = rqk.shape[0]
    nc = tp // TM

    def body(lg_ref, qk_ref, v_ref, rg_ref, g_ref, o_ref, a_ref, st_ref, s_scr, dm_scr):
        @pl.when(pl.program_id(0) == 0)
        def _():
            s_scr[...] = jnp.zeros_like(s_scr)
            _ret_fill_decay(lg_ref, dm_scr)

        for h in range(RET_HEADS):
            dm, xi, zeta, gc = _ret_consts(lg_ref[h], dm_scr.at[h])
            q = qk_ref[:, h * RET_QK:(h + 1) * RET_QK]
            k = qk_ref[:, D_MODEL + h * RET_QK:D_MODEL + (h + 1) * RET_QK]
            v = v_ref[:, h * RET_V:(h + 1) * RET_V]
            sb = s_scr[h].astype(BF16)
            st_ref[0, h] = sb
            s = _dot(q, k, NT) * dm
            o = _dot(s.astype(BF16), v, NN) + xi * _dot(q, sb, NN)
            kz = (k.astype(F32) * zeta).astype(BF16)
            s_scr[h] = gc * s_scr[h] + _dot(kz, v, TN)
            o_ref[:, h * RET_V:(h + 1) * RET_V] = o
            mu = jnp.mean(o, axis=-1, keepdims=True)
            xc = o - mu
            xh = xc * lax.rsqrt(jnp.mean(xc * xc, axis=-1, keepdims=True) + EPS)
            g = rg_ref[:, h * RET_V:(h + 1) * RET_V]
            a_ref[:, h * RET_V:(h + 1) * RET_V] = (
                xh * g_ref[:, h * RET_V:(h + 1) * RET_V] * (g * _sigmoid(g))).astype(BF16)

    return pl.pallas_call(
        body, name="ret_fwd", grid=(nc,),
        in_specs=[pl.BlockSpec(memory_space=pltpu.SMEM),
                  pl.BlockSpec((TM, 2 * D_MODEL), lambda n: (n, 0)),
                  pl.BlockSpec((TM, RET_W), lambda n: (n, 0)),
                  pl.BlockSpec((TM, RET_W), lambda n: (n, 0)),
                  pl.BlockSpec((1, RET_W), lambda n: (0, 0))],
        out_specs=[pl.BlockSpec((TM, RET_W), lambda n: (n, 0)),
                   pl.BlockSpec((TM, RET_W), lambda n: (n, 0)),
                   pl.BlockSpec((1, RET_HEADS, RET_QK, RET_V), lambda n: (n, 0, 0, 0))],
        out_shape=[jax.ShapeDtypeStruct((tp, RET_W), F32), jax.ShapeDtypeStruct((tp, RET_W), BF16),
                   jax.ShapeDtypeStruct((nc, RET_HEADS, RET_QK, RET_V), BF16)],
        scratch_shapes=[pltpu.VMEM((RET_HEADS, RET_QK, RET_V), F32), pltpu.VMEM((RET_HEADS, TM, TM), F32)],
        compiler_params=_cparams(1),
    )(lgam, rqk, rv, rg, gain)


def _ret_bwd_call(rqk, rv, rg, o_ret, dpr, wbr, states, gain, lgam, cos, sin):
    tp = rqk.shape[0]
    nc = tp // TM
    half = RET_QK // 2

    def body(lg_ref, qk_ref, v_ref, rg_ref, o_ref, dpr_ref, wbr_ref, st_ref, g_ref, cos_ref, sin_ref, dp_ref, dg_ref,
             ds_scr, dm_scr):
        @pl.when(pl.program_id(0) == 0)
        def _():
            ds_scr[...] = jnp.zeros_like(ds_scr)
            dg_ref[...] = jnp.zeros_like(dg_ref)
            _ret_fill_decay(lg_ref, dm_scr)

        cos, sin = cos_ref[...], sin_ref[...]
        for h in range(RET_HEADS):
            hs = slice(h * RET_V, (h + 1) * RET_V)
            dm, xi, zeta, gc = _ret_consts(lg_ref[h], dm_scr.at[h])
            o = o_ref[:, hs]
            mu = jnp.mean(o, axis=-1, keepdims=True)
            xc = o - mu
            rstd = lax.rsqrt(jnp.mean(xc * xc, axis=-1, keepdims=True) + EPS)
            xh = xc * rstd
            gain_h = g_ref[:, hs]
            g = rg_ref[:, hs]
            sg = _sigmoid(g)
            silu = g * sg
            dah = _dot(dpr_ref[...], wbr_ref[hs, :], NT)
            dp_ref[:, 4 * D_MODEL + h * RET_V:4 * D_MODEL + (h + 1) * RET_V] = (
                dah * (xh * gain_h) * (sg * (1.0 + g * (1.0 - sg)))).astype(BF16)
            dn = dah * silu
            dg_ref[:, hs] += jnp.sum(dn * xh, axis=0, keepdims=True)
            dxh = dn * gain_h
            do = rstd * (dxh - jnp.mean(dxh, axis=-1, keepdims=True)
                         - xh * jnp.mean(dxh * xh, axis=-1, keepdims=True))
            dob = do.astype(BF16)
            q = qk_ref[:, h * RET_QK:(h + 1) * RET_QK]
            k = qk_ref[:, D_MODEL + h * RET_QK:D_MODEL + (h + 1) * RET_QK]
            v = v_ref[:, hs]
            sp = st_ref[0, h]
            ds = ds_scr[h]
            dsb = ds.astype(BF16)
            s = (_dot(q, k, NT) * dm).astype(BF16)
            dsc = (_dot(dob, v, NT) * dm).astype(BF16)
            dq = _dot(dsc, k, NN) + xi * _dot(dob, sp, NT)
            dk = _dot(dsc, q, TN) + zeta * _dot(v, dsb, NT)
            kz = (k.astype(F32) * zeta).astype(BF16)
            dv = _dot(s, dob, TN) + _dot(kz, dsb, NN)
            qx = (q.astype(F32) * xi).astype(BF16)
            ds_scr[h] = gc * ds + _dot(qx, dob, TN)
            dp_ref[:, 2 * D_MODEL + h * RET_V:2 * D_MODEL + (h + 1) * RET_V] = dv.astype(BF16)
            dk = dk * (RET_QK ** -0.5)
            for base, t in ((0, dq), (D_MODEL, dk)):
                t1, t2 = t[:, :half], t[:, half:]
                dp_ref[:, base + h * RET_QK:base + h * RET_QK + half] = (t1 * cos + t2 * sin).astype(BF16)
                dp_ref[:, base + h * RET_QK + half:base + (h + 1) * RET_QK] = (t2 * cos - t1 * sin).astype(BF16)

    rev = lambda n: (nc - 1 - n, 0)
    return pl.pallas_call(
        body, name="ret_bwd", grid=(nc,),
        in_specs=[pl.BlockSpec(memory_space=pltpu.SMEM),
                  pl.BlockSpec((TM, 2 * D_MODEL), rev),
                  pl.BlockSpec((TM, RET_W), rev),
                  pl.BlockSpec((TM, RET_W), rev),
                  pl.BlockSpec((TM, RET_W), rev),
                  pl.BlockSpec((TM, D_MODEL), rev),
                  pl.BlockSpec((RET_W, D_MODEL), lambda n: (0, 0)),
                  pl.BlockSpec((1, RET_HEADS, RET_QK, RET_V), lambda n: (nc - 1 - n, 0, 0, 0)),
                  pl.BlockSpec((1, RET_W), lambda n: (0, 0)),
                  pl.BlockSpec((TM, half), rev),
                  pl.BlockSpec((TM, half), rev)],
        out_specs=[pl.BlockSpec((TM, W_R), rev), pl.BlockSpec((1, RET_W), lambda n: (0, 0))],
        out_shape=[jax.ShapeDtypeStruct((tp, W_R), BF16), jax.ShapeDtypeStruct((1, RET_W), F32)],
        scratch_shapes=[pltpu.VMEM((RET_HEADS, RET_QK, RET_V), F32), pltpu.VMEM((RET_HEADS, TM, TM), F32)],
        compiler_params=_cparams(1),
    )(lgam, rqk, rv, rg, o_ret, dpr, wbr, states, gain, cos, sin)


GLA_LEVELS = tuple(GC >> (s + 1) for s in range(int(math.log2(GC // GLA_SUB))))
NLEV = len(GLA_LEVELS)


def _gla_tril():
    return np.tril(np.ones((GC, GC), np.float32))


def _gla_masks():
    ii = lax.broadcasted_iota(jnp.int32, (GC, GC), 0)
    jj = lax.broadcasted_iota(jnp.int32, (GC, GC), 1)
    masks = []
    for m in GLA_LEVELS:
        sh = int(math.log2(2 * m))
        masks.append(((ii >> sh) == (jj >> sh)) & ((ii & m) != 0) & ((jj & m) == 0))
    sh = int(math.log2(GLA_SUB))
    md = ((ii >> sh) == (jj >> sh)) & (jj <= ii)
    row = lax.broadcasted_iota(jnp.int32, (GC, 1), 0)
    second = [(row & m) != 0 for m in GLA_LEVELS]
    return masks, md, second


def _gla_log_decay(glr_ref, wg_ref, bg_ref):
    z = _dot(glr_ref[...].astype(BF16), wg_ref[...], NN) + bg_ref[...]
    la = (jnp.minimum(z, 0.0) - jnp.log1p(jnp.exp(-jnp.abs(z)))) * (1.0 / GATE_TAU)
    return z, la


def _gla_row_steps(b_ref, cs, rows, size):
    parts = [jnp.zeros((size, GLA_K), F32) if r is None else jnp.broadcast_to(b_ref[r:r + 1, cs], (size, GLA_K))
             for r in rows]
    return parts[0] if len(parts) == 1 else jnp.concatenate(parts, axis=0)


def _gla_factors(b_ref, h, second):
    cs = slice(h * GLA_K, (h + 1) * GLA_K)
    b = b_ref[:, cs]
    fq, fk = [], []
    for l, m in enumerate(GLA_LEVELS):
        d = b - _gla_row_steps(b_ref, cs, [s + m - 1 for s in range(0, GC, 2 * m)], 2 * m)
        f = jnp.exp(jnp.where(second[l], d, -d))
        fq.append(jnp.where(second[l], f, 0.0))
        fk.append(jnp.where(second[l], 0.0, f))
    dd = b - _gla_row_steps(b_ref, cs, [None] + [s - 1 for s in range(GLA_SUB, GC, GLA_SUB)], GLA_SUB)
    ed = jnp.exp(dd)
    edi = jnp.exp(-dd)
    eb = jnp.exp(b)
    bl = b_ref[GC - 1:GC, cs]
    ee = jnp.exp(bl - b)
    ebl = jnp.exp(bl)
    return fq, fk, ed, edi, eb, ee, ebl


def _gla_scores(q, k, fq, fk, ed, edi, masks, md):
    qt = [(q * f).astype(BF16) for f in fq]
    kt = [(k * f).astype(BF16) for f in fk]
    qd = (q * ed).astype(BF16)
    kd = (k * edi).astype(BF16)
    a = jnp.where(md, _dot(qd, kd, NT), 0.0)
    for l in range(NLEV):
        a = a + jnp.where(masks[l], _dot(qt[l], kt[l], NT), 0.0)
    return a, qt, kt, qd, kd


def _gla_fwd_call(gqk, gv, glr, gg, wg, bg, gain, pmat, comm=None):
    tp = gqk.shape[0]
    nc = tp // GC
    n_xc = len(comm.srcs) if comm else 0

    def body(qk_ref, v_ref, glr_ref, gg_ref, wg_ref, bg_ref, g_ref, p_ref, *rest):
        xc_src = rest[:n_xc]
        o_ref, a_ref, st_ref = rest[n_xc:n_xc + 3]
        xc_dst = rest[n_xc + 3:2 * n_xc + 3]
        s_scr, b_scr = rest[2 * n_xc + 3:2 * n_xc + 5]
        n = pl.program_id(0)
        if n_xc:
            copies = comm.make(xc_src, xc_dst, rest[-2], rest[-1])

            @pl.when(n == 0)
            def _():
                for cp in copies:
                    cp.start()

            @pl.when(n == nc - 1)
            def _():
                for cp in copies:
                    cp.wait()

        @pl.when(n == 0)
        def _():
            s_scr[...] = jnp.zeros_like(s_scr)

        _, la = _gla_log_decay(glr_ref, wg_ref, bg_ref)
        b_scr[...] = _exact_pm(p_ref[...], la)
        masks, md, second = _gla_masks()
        for h in range(GLA_HEADS):
            q = qk_ref[:, h * GLA_K:(h + 1) * GLA_K]
            k = qk_ref[:, GLA_KW + h * GLA_K:GLA_KW + (h + 1) * GLA_K]
            vs = slice(h * GLA_V, (h + 1) * GLA_V)
            v = v_ref[:, vs]
            fq, fk, ed, edi, eb, ee, ebl = _gla_factors(b_scr, h, second)
            a, *_ = _gla_scores(q, k, fq, fk, ed, edi, masks, md)
            sb = s_scr[h].astype(BF16)
            st_ref[0, h] = sb
            o = _dot(a.astype(BF16), v, NN) + _dot((q * eb).astype(BF16), sb, NT)
            s_scr[h] = s_scr[h] * ebl + _dot(v, (k * ee).astype(BF16), TN)
            o_ref[:, vs] = o
            xh = o * lax.rsqrt(jnp.mean(o * o, axis=-1, keepdims=True) + EPS)
            g = gg_ref[:, vs]
            a_ref[:, vs] = (xh * g_ref[:, vs] * (g * _sigmoid(g))).astype(BF16)

    return pl.pallas_call(
        body, name="gla_fwd", grid=(nc,),
        in_specs=[pl.BlockSpec((GC, 2 * GLA_KW), lambda n: (n, 0)),
                  pl.BlockSpec((GC, GLA_W), lambda n: (n, 0)),
                  pl.BlockSpec((GC, 128), lambda n: (n, 0)),
                  pl.BlockSpec((GC, GLA_W), lambda n: (n, 0)),
                  pl.BlockSpec((128, GLA_KW), lambda n: (0, 0)),
                  pl.BlockSpec((1, GLA_KW), lambda n: (0, 0)),
                  pl.BlockSpec((1, GLA_W), lambda n: (0, 0)),
                  pl.BlockSpec((GC, GC), lambda n: (0, 0))] + [ANY] * n_xc,
        out_specs=[pl.BlockSpec((GC, GLA_W), lambda n: (n, 0)),
                   pl.BlockSpec((GC, GLA_W), lambda n: (n, 0)),
                   pl.BlockSpec((1, GLA_HEADS, GLA_V, GLA_K), lambda n: (n, 0, 0, 0))] + [ANY] * n_xc,
        out_shape=[jax.ShapeDtypeStruct((tp, GLA_W), F32), jax.ShapeDtypeStruct((tp, GLA_W), BF16),
                   jax.ShapeDtypeStruct((nc, GLA_HEADS, GLA_V, GLA_K), BF16)] + (list(comm.out_shapes) if comm else []),
        scratch_shapes=[pltpu.VMEM((GLA_HEADS, GLA_V, GLA_K), F32), pltpu.VMEM((GC, GLA_KW), F32)]
        + (_comm_sems(comm) if comm else []),
        compiler_params=_cparams(1),
    )(gqk, gv, glr, gg, wg, bg, gain, pmat, *(comm.srcs if comm else ()))


def _gla_bwd_call(gqk, gv, glr, gg, o_gla, dpg, wbg, states, wg, bg, gain, pmat, pmat_t, comm=None):
    tp = gqk.shape[0]
    nc = tp // GC
    o_gv, o_gg, o_lr = 2 * GLA_KW, 2 * GLA_KW + GLA_W, 2 * GLA_KW + 2 * GLA_W
    n_xc = len(comm.srcs) if comm else 0

    def body(qk_ref, v_ref, glr_ref, gg_ref, o_ref, dpg_ref, wbg_ref, st_ref, wg_ref, bg_ref, g_ref, p_ref, pt_ref,
             *rest):
        xc_src = rest[:n_xc]
        dp_ref, dwg_ref, dbg_ref, dg_ref = rest[n_xc:n_xc + 4]
        xc_dst = rest[n_xc + 4:2 * n_xc + 4]
        ds_scr, b_scr, db_scr = rest[2 * n_xc + 4:2 * n_xc + 7]
        n = pl.program_id(0)
        if n_xc:
            copies = comm.make(xc_src, xc_dst, rest[-2], rest[-1])

            @pl.when(n == 0)
            def _():
                for cp in copies:
                    cp.start()

            @pl.when(n == nc - 1)
            def _():
                for cp in copies:
                    cp.wait()

        @pl.when(n == 0)
        def _():
            ds_scr[...] = jnp.zeros_like(ds_scr)
            dwg_ref[...] = jnp.zeros_like(dwg_ref)
            dbg_ref[...] = jnp.zeros_like(dbg_ref)
            dg_ref[...] = jnp.zeros_like(dg_ref)

        z, la = _gla_log_decay(glr_ref, wg_ref, bg_ref)
        b_scr[...] = _exact_pm(p_ref[...], la)
        masks, md, second = _gla_masks()
        for h in range(GLA_HEADS):
            cs = slice(h * GLA_K, (h + 1) * GLA_K)
            vs = slice(h * GLA_V, (h + 1) * GLA_V)
            o = o_ref[:, vs]
            rstd = lax.rsqrt(jnp.mean(o * o, axis=-1, keepdims=True) + EPS)
            xh = o * rstd
            gain_h = g_ref[:, vs]
            g = gg_ref[:, vs]
            sg = _sigmoid(g)
            dah = _dot(dpg_ref[...], wbg_ref[vs, :], NT)
            dp_ref[:, o_gg + h * GLA_V:o_gg + (h + 1) * GLA_V] = (
                dah * (xh * gain_h) * (sg * (1.0 + g * (1.0 - sg)))).astype(BF16)
            dn = dah * (g * sg)
            dg_ref[:, vs] += jnp.sum(dn * xh, axis=0, keepdims=True)
            dxh = dn * gain_h
            do = rstd * (dxh - xh * jnp.mean(dxh * xh, axis=-1, keepdims=True))
            dob = do.astype(BF16)
            q = qk_ref[:, cs]
            k = qk_ref[:, GLA_KW + h * GLA_K:GLA_KW + (h + 1) * GLA_K]
            v = v_ref[:, vs]
            fq, fk, ed, edi, eb, ee, ebl = _gla_factors(b_scr, h, second)
            a, qt, kt, qd, kd = _gla_scores(q, k, fq, fk, ed, edi, masks, md)
            sp = st_ref[0, h]
            ds = ds_scr[h]
            dsb = ds.astype(BF16)
            q_in = q * eb
            k_end = k * ee
            da_s = _dot(dob, v, NT)
            dv = _dot(a.astype(BF16), dob, TN) + _dot(k_end.astype(BF16), dsb, NT)
            dq_in = _dot(dob, sp, NN)
            dk_end = _dot(v, dsb, NN)
            dbl = jnp.sum(sp.astype(F32) * ds, axis=0, keepdims=True) * ebl
            ds_scr[h] = ds * ebl + _dot(dob, q_in.astype(BF16), TN)
            dq = dq_in * eb
            dk = dk_end * ee
            de_end = dk_end * k_end
            db = dq_in * q_in - de_end
            placed = [(GC - 1, jnp.sum(de_end, axis=0, keepdims=True) + dbl)]
            for l, m in enumerate(GLA_LEVELS):
                dal = jnp.where(masks[l], da_s, 0.0).astype(BF16)
                dqt = _dot(dal, kt[l], NN)
                dkt = _dot(dal, qt[l], TN)
                dq = dq + dqt * fq[l]
                dk = dk + dkt * fk[l]
                gl = dqt * (q * fq[l]) - dkt * (k * fk[l])
                db = db + gl
                placed += [(s + m - 1, -jnp.sum(gl[s:s + 2 * m], axis=0, keepdims=True)) for s in range(0, GC, 2 * m)]
            dad = jnp.where(md, da_s, 0.0).astype(BF16)
            dqd = _dot(dad, kd, NN)
            dkd = _dot(dad, qd, TN)
            dq = dq + dqd * ed
            dk = dk + dkd * edi
            gd = dqd * (q * ed) - dkd * (k * edi)
            db = db + gd
            placed += [(s - 1, -jnp.sum(gd[s:s + GLA_SUB], axis=0, keepdims=True)) for s in range(GLA_SUB, GC, GLA_SUB)]
            db_scr[:, cs] = db
            for r, val in placed:
                db_scr[r:r + 1, cs] += val
            dp_ref[:, cs] = (dq * (GLA_K ** -0.5)).astype(BF16)
            dp_ref[:, GLA_KW + h * GLA_K:GLA_KW + (h + 1) * GLA_K] = dk.astype(BF16)
            dp_ref[:, o_gv + h * GLA_V:o_gv + (h + 1) * GLA_V] = dv.astype(BF16)
        dla = _exact_pm(pt_ref[...], db_scr[...])
        row = (nc - 1 - n) * GC + lax.broadcasted_iota(jnp.int32, (GC, 1), 0)
        dz = jnp.where(row >= PADF, dla * (1.0 / GATE_TAU) * _sigmoid(-z), 0.0)
        dzb = dz.astype(BF16)
        dp_ref[:, o_lr:] = _dot(dzb, wg_ref[...], NT).astype(BF16)
        dwg_ref[...] += _dot(glr_ref[...].astype(BF16), dzb, TN)
        dbg_ref[...] += jnp.sum(dz, axis=0, keepdims=True)

    rev = lambda n: (nc - 1 - n, 0)
    const = lambda n: (0, 0)
    xc_shapes, xc_sems = (list(comm.out_shapes), _comm_sems(comm)) if n_xc else ([], [])
    return pl.pallas_call(
        body, name="gla_bwd", grid=(nc,),
        in_specs=[pl.BlockSpec((GC, 2 * GLA_KW), rev),
                  pl.BlockSpec((GC, GLA_W), rev),
                  pl.BlockSpec((GC, 128), rev),
                  pl.BlockSpec((GC, GLA_W), rev),
                  pl.BlockSpec((GC, GLA_W), rev),
                  pl.BlockSpec((GC, D_MODEL), rev),
                  pl.BlockSpec((GLA_W, D_MODEL), const),
                  pl.BlockSpec((1, GLA_HEADS, GLA_V, GLA_K), lambda n: (nc - 1 - n, 0, 0, 0)),
                  pl.BlockSpec((128, GLA_KW), const),
                  pl.BlockSpec((1, GLA_KW), const),
                  pl.BlockSpec((1, GLA_W), const),
                  pl.BlockSpec((GC, GC), const),
                  pl.BlockSpec((GC, GC), const)] + [ANY] * n_xc,
        out_specs=[pl.BlockSpec((GC, W_GP), rev), pl.BlockSpec((128, GLA_KW), const),
                   pl.BlockSpec((1, GLA_KW), const), pl.BlockSpec((1, GLA_W), const)] + [ANY] * n_xc,
        out_shape=[jax.ShapeDtypeStruct((tp, W_GP), BF16), jax.ShapeDtypeStruct((128, GLA_KW), F32),
                   jax.ShapeDtypeStruct((1, GLA_KW), F32), jax.ShapeDtypeStruct((1, GLA_W), F32)] + xc_shapes,
        scratch_shapes=[pltpu.VMEM((GLA_HEADS, GLA_V, GLA_K), F32), pltpu.VMEM((GC, GLA_KW), F32),
                        pltpu.VMEM((GC, GLA_KW), F32)] + xc_sems,
        compiler_params=_cparams(1),
    )(gqk, gv, glr, gg, o_gla, dpg, wbg, states, wg, bg, gain, pmat, pmat_t, *(comm.srcs if comm else ()))


def _mid_call(a_ret, a_gla, mg, h0, tgt, wbr, wbg, wout, gf):
    tp = h0.shape[0]
    nt = tp // TM

    def body(ar_ref, ag_ref, mg_ref, h_ref, t_ref, wbr_ref, wbg_ref, wo_ref, gf_ref,
             dh1_ref, dm_ref, mb_ref, dh1b_ref, dprb_ref, dpgb_ref, loss_ref, dgf_ref):
        i = pl.program_id(0)

        @pl.when(i == 0)
        def _():
            loss_ref[...] = jnp.zeros_like(loss_ref)
            dgf_ref[...] = jnp.zeros_like(dgf_ref)

        ar, ag = ar_ref[...], ag_ref[...]
        pr = _dot(ar, wbr_ref[...], NN)
        pg = _dot(ag, wbg_ref[...], NN)
        sr = _sigmoid(mg_ref[:, :D_MODEL])
        sg = _sigmoid(mg_ref[:, D_MODEL:])
        merged = (sr * pr + sg * pg).astype(BF16)
        mb_ref[...] = merged
        h1 = h_ref[...] + _dot(merged, wo_ref[...], NN)
        r1 = lax.rsqrt(jnp.mean(h1 * h1, axis=-1, keepdims=True) + EPS)
        xh = h1 * r1
        gfv = gf_ref[...]
        live = jnp.where(i > 0, 1.0, 0.0).astype(F32)
        err = (xh * gfv - t_ref[...]) * live
        loss_ref[...] += jnp.full(loss_ref.shape, 0.5 / D_MODEL, F32) * jnp.sum(err * err)
        dy = err * (1.0 / D_MODEL)
        dgf_ref[...] += jnp.sum(dy * xh, axis=0, keepdims=True)
        dxh = dy * gfv
        dh1 = r1 * (dxh - xh * jnp.mean(dxh * xh, axis=-1, keepdims=True))
        dh1_ref[...] = dh1
        dh1b = dh1.astype(BF16)
        dh1b_ref[...] = dh1b
        dmerged = _dot(dh1b, wo_ref[...], NT)
        dm_ref[:, :D_MODEL] = (dmerged * pr * sr * (1.0 - sr)).astype(BF16)
        dm_ref[:, D_MODEL:] = (dmerged * pg * sg * (1.0 - sg)).astype(BF16)
        dpr = (dmerged * sr).astype(BF16)
        dpg = (dmerged * sg).astype(BF16)
        dprb_ref[...] = dpr
        dpgb_ref[...] = dpg

    tile = lambda w: pl.BlockSpec((TM, w), lambda i: (i, 0))
    const = lambda r, w: pl.BlockSpec((r, w), lambda i: (0, 0))
    return pl.pallas_call(
        body, name="merge_out_loss", grid=(nt,),
        in_specs=[tile(RET_W), tile(GLA_W), tile(W_M), tile(D_MODEL),
                  pl.BlockSpec((TM, D_MODEL), lambda i: (jnp.maximum(i - 1, 0), 0)),
                  const(RET_W, D_MODEL), const(GLA_W, D_MODEL), const(D_MODEL, D_MODEL), const(1, D_MODEL)],
        out_specs=[tile(D_MODEL), tile(W_M), tile(D_MODEL), tile(D_MODEL), tile(D_MODEL),
                   tile(D_MODEL), const(1, 128), const(1, D_MODEL)],
        out_shape=[jax.ShapeDtypeStruct((tp, D_MODEL), F32), jax.ShapeDtypeStruct((tp, W_M), BF16),
                   jax.ShapeDtypeStruct((tp, D_MODEL), BF16), jax.ShapeDtypeStruct((tp, D_MODEL), BF16),
                   jax.ShapeDtypeStruct((tp, D_MODEL), BF16), jax.ShapeDtypeStruct((tp, D_MODEL), BF16),
                   jax.ShapeDtypeStruct((1, 128), F32), jax.ShapeDtypeStruct((1, D_MODEL), F32)],
        compiler_params=_cparams(1),
    )(a_ret, a_gla, mg, h0, tgt, wbr, wbg, wout, gf)


def _device_step(x2d, tgt2d, meta, norm_gain, w_in_bf, w_gate_up, b_gate, ret_gain, gla_gain, branch_parts,
                 final_gain, ck):
    seq = x2d.shape[0]
    tp = T0 + seq
    head = jnp.concatenate([jnp.zeros((PADF, D_MODEL), F32), meta], axis=0)
    w_r = w_in_bf
    w_g = jnp.pad(w_in_bf[:, W_R:W_R + W_G], ((0, 0), (0, W_GP - W_G)))
    w_m = w_in_bf[:, W_R + W_G:]
    wg_pad = jnp.pad(w_gate_up, ((0, 128 - GATE_RANK), (0, 0))).astype(BF16)

    pos = jnp.arange(tp, dtype=F32) - PADF
    half = RET_QK // 2
    inv = ROPE_BASE ** (-jnp.arange(half, dtype=F32) / half)
    ang = pos[:, None] * inv[None, :]
    cos, sin = jnp.cos(ang), jnp.sin(ang)
    lgam = jnp.log1p(-(2.0 ** (-5.0 - jnp.arange(RET_HEADS, dtype=F32))))
    pmat = jnp.asarray(_gla_tril(), BF16)
    pmat_t = jnp.asarray(_gla_tril().T.copy(), BF16)

    h0, u = _rms_call(x2d, head, norm_gain)
    tab = pl.BlockSpec((_proj_rows(tp), half), lambda j, i: (i, 0))
    rqk = _mm_nn("proj_rqk", u, w_r, BF16, D_MODEL, 0, 2 * D_MODEL, _rope_epilogue, (cos, sin), (tab, tab))
    rv = _mm_nn("proj_rv", u, w_r, BF16, D_MODEL, 2 * D_MODEL, RET_W)
    rg = _mm_nn("proj_rg", u, w_r, F32, D_MODEL, 4 * D_MODEL, RET_W)
    gqk = _mm_nn("proj_gqk", u, w_g, F32, 2 * GLA_KW, 0, 2 * GLA_KW, _gqk_epilogue)
    gv = _mm_nn("proj_gv", u, w_g, BF16, GLA_W, 2 * GLA_KW, GLA_W)
    gg = _mm_nn("proj_gg", u, w_g, F32, GLA_W, 2 * GLA_KW + GLA_W, GLA_W)
    glr = _mm_nn("proj_glr", u, w_g, F32, 128, 2 * GLA_KW + 2 * GLA_W, 128)
    mg = _mm_nn("proj_mg", u, w_m, F32, D_MODEL, 0, W_M)

    o_ret, a_ret, st_ret = _ret_fwd_call(rqk, rv, rg, ret_gain, lgam)
    o_gla, a_gla, st_gla, g_br, g_bg, g_out = _gla_fwd_call(gqk, gv, glr, gg, wg_pad, b_gate, gla_gain, pmat,
                                                            comm=_spread_plan(branch_parts))
    wbr = g_br.reshape(RET_W, D_MODEL)
    wbg = g_bg.reshape(GLA_W, D_MODEL)
    wout = g_out.reshape(D_MODEL, D_MODEL)

    gf = final_gain.reshape(1, D_MODEL)
    (dh1, dm, merged_b, dh1_b, dpr_b, dpg_b, loss, dgf) = _mid_call(
        a_ret, a_gla, mg, h0, tgt2d, wbr, wbg, wout, gf)

    names_b = ("w_branch_ret", "w_branch_gla", "w_out")
    g2_b = [_mm_tn("dw_br", a_ret, dpr_b, D_MODEL).reshape(4, 2, RET_W // 8, D_MODEL).transpose(1, 0, 2, 3),
            _mm_tn("dw_bg", a_gla, dpg_b, D_MODEL).reshape(4, 2, GLA_W // 8, D_MODEL).transpose(1, 0, 2, 3),
            _mm_tn("dw_out", merged_b, dh1_b, D_MODEL).reshape(4, 2, D_MODEL // 8, D_MODEL).transpose(1, 0, 2, 3)]
    sib_b = _swap_halves_call("swap_halves_branch", g2_b)
    sum_b = [_add_half_call("add_half_" + nm, g, b, ck) for nm, g, b in zip(names_b, g2_b, sib_b)]
    d_g, dwg, dbg, dgla_gain, *chips_b = _gla_bwd_call(gqk, gv, glr, gg, o_gla, dpg_b, wbg, st_gla, wg_pad, b_gate,
                                                       gla_gain, pmat, pmat_t, comm=_exchange_plan(sum_b))
    mine = [_add_chips_call("add_chips_" + nm, g, b, p, ck) for nm, g, b, p in zip(names_b, g2_b, sib_b, chips_b)]

    d_r, dret_gain = _ret_bwd_call(rqk, rv, rg, o_ret, dpr_b, wbr, st_ret, ret_gain, lgam, cos, sin)

    dwp = _mm_tn("dw_r", u, d_r, 2 * D_MODEL, out_cols=IN_PAD)
    dwp = _mm_tn("dw_g", u, d_g, D_MODEL, ncols=W_GP - 128, into=dwp, col0=W_R)
    dwp = _mm_tn("dw_glr", u, d_g, 128, ncols=128, bcol0=W_GP - 128, into=dwp, col0=W_R + W_GP - 128)
    g2_in = _place_merge_cols_call(dwp, _mm_tn("dw_m", u, dm, 2 * D_MODEL)).reshape(2, D_MODEL // 2, IN_PAD)

    du, sib_in = _mm_nt_acc("du_g", d_g, w_g, W_GP, comm=_swap_plan([g2_in]))
    sum_in = _add_rows_call("add_half_w_in", g2_in, sib_in, ck)
    du, chips_in = _mm_nt_acc("du_r", d_r, w_r, 2 * D_MODEL, acc_in=du, comm=_exchange_window_plan(sum_in))
    tile = pl.BlockSpec((TB, D_MODEL), lambda i, kk: (i, 0))
    row = pl.BlockSpec((1, D_MODEL), lambda i, kk: (0, 0))
    dx, dmeta, dnorm_gain = _mm_nt_acc(
        "du_m", dm, w_m, W_M, acc_in=du, epilogue=_rms_bwd_epilogue, extras=(h0, norm_gain, dh1),
        extra_specs=(tile, row, tile),
        extra_out_shapes=(jax.ShapeDtypeStruct((seq, D_MODEL), F32), jax.ShapeDtypeStruct((N_META, D_MODEL), F32),
                          jax.ShapeDtypeStruct((1, D_MODEL), F32)),
        extra_out_specs=(ANY, pl.BlockSpec((N_META, D_MODEL), lambda i, kk: (0, 0)), row),
        extra_scratch=(pltpu.VMEM((2, TB, D_MODEL), F32), pltpu.SemaphoreType.DMA((2,))))
    mine = [_add_window_call("add_chips_w_in", g2_in, sib_in, chips_in, ck)] + mine
    full = _join_halves_call("join_halves", mine)

    return dict(loss=loss[0, 0], dx=dx, dmeta=dmeta, norm_gain=dnorm_gain, w_gate_up=dwg[:GATE_RANK], b_gate=dbg,
                ret_norm_gain=dret_gain, gla_norm_gain=dgla_gain, final_norm_gain=dgf.reshape(D_MODEL),
                w_in=full[0], w_branch_ret=full[1], w_branch_gla=full[2], w_out=full[3])


MESH = pl.DeviceIdType.MESH
ANY = pl.BlockSpec(memory_space=pl.ANY)


def _place():
    return lax.axis_index("x"), lax.axis_index("y"), lax.axis_index("c")


def _gather8_call(name, parts, relay=()):
    n = len(parts)
    relay = tuple(relay) + (False,) * (n - len(relay))

    def body(*refs):
        x_refs, out_refs = refs[:n], refs[n:2 * n]
        send_sems, recv_sems, local_sems = refs[2 * n:]
        x, y, c = _place()
        me, sibling = (x, y, c), (x, y, 1 - c)
        xn, yn, dg = (1 - x, y), (x, 1 - y), (1 - x, 1 - y)

        def slot(t, px, py, pc, half=None):
            ref = out_refs[t].at[4 * px + 2 * py + pc]
            if half is None:
                return ref
            rows = ref.shape[0] // 2
            return ref.at[pl.ds(half * rows, rows)]

        def copy(t, k, dst, to, src=None):
            return pltpu.make_async_remote_copy(
                src_ref=dst if src is None else src, dst_ref=dst, send_sem=send_sems.at[8 * t + k],
                recv_sem=recv_sems.at[8 * t + k], device_id=to, device_id_type=MESH)

        mine = [pltpu.make_async_copy(x_refs[t], slot(t, *me), local_sems.at[t]) for t in range(n)]
        for cp in mine:
            cp.start()
        sent = []
        for t in range(n):
            sent.append(copy(t, 0, slot(t, *me), sibling, src=x_refs[t]))
            sent.append(copy(t, 1, slot(t, *me), (*xn, c), src=x_refs[t]))
            sent.append(copy(t, 2, slot(t, *me), (*yn, c), src=x_refs[t]))
            if not relay[t]:
                sent.append(copy(t, 3, slot(t, *me), (*dg, c), src=x_refs[t]))
        for cp in sent:
            cp.start()

        def start(cp):
            cp.start()
            sent.append(cp)

        for t in range(n):
            copy(t, 2, slot(t, *yn, c), me).wait_recv()
            if relay[t]:
                start(copy(t, 3, slot(t, *yn, c, half=0), (*xn, c)))
            start(copy(t, 6, slot(t, *yn, c), sibling))
        for t in range(n):
            copy(t, 1, slot(t, *xn, c), me).wait_recv()
            if relay[t]:
                start(copy(t, 4, slot(t, *xn, c, half=1), (*yn, c)))
            start(copy(t, 5, slot(t, *xn, c), sibling))
        for t in range(n):
            if relay[t]:
                copy(t, 3, slot(t, *dg, c, half=0), me).wait_recv()
                copy(t, 4, slot(t, *dg, c, half=1), me).wait_recv()
            else:
                copy(t, 3, slot(t, *dg, c), me).wait_recv()
            start(copy(t, 7, slot(t, *dg, c), sibling))
        for t in range(n):
            copy(t, 0, slot(t, *sibling), me).wait_recv()
            copy(t, 5, slot(t, *xn, 1 - c), me).wait_recv()
            copy(t, 6, slot(t, *yn, 1 - c), me).wait_recv()
            copy(t, 7, slot(t, *dg, 1 - c), me).wait_recv()
        for cp in sent:
            cp.wait_send()
        for cp in mine:
            cp.wait()

    return pl.pallas_call(
        body, name=name,
        out_shape=[jax.ShapeDtypeStruct((8,) + p.shape, p.dtype) for p in parts],
        in_specs=[ANY] * n, out_specs=[ANY] * n,
        scratch_shapes=[pltpu.SemaphoreType.DMA((8 * n,)), pltpu.SemaphoreType.DMA((8 * n,)),
                        pltpu.SemaphoreType.DMA((n,))],
    )(*parts)


def _swap_halves_call(name, gs):
    n = len(gs)

    def body(*refs):
        g_refs, b_refs = refs[:n], refs[n:2 * n]
        send_sems, recv_sems = refs[2 * n:]
        x, y, c = _place()
        copies = [pltpu.make_async_remote_copy(
            src_ref=g_refs[t].at[1 - c], dst_ref=b_refs[t], send_sem=send_sems.at[t], recv_sem=recv_sems.at[t],
            device_id=(x, y, 1 - c), device_id_type=MESH) for t in range(n)]
        for cp in copies:
            cp.start()
        for cp in copies:
            cp.wait()

    return pl.pallas_call(
        body, name=name,
        out_shape=[jax.ShapeDtypeStruct(g.shape[1:], g.dtype) for g in gs],
        in_specs=[ANY] * n, out_specs=[ANY] * n,
        scratch_shapes=[pltpu.SemaphoreType.DMA((n,)), pltpu.SemaphoreType.DMA((n,))],
    )(*gs)


def _join_halves_call(name, ts):
    n = len(ts)

    def body(*refs):
        o_refs = refs[n:2 * n]
        send_sems, recv_sems = refs[2 * n:]
        x, y, c = _place()
        copies = [pltpu.make_async_remote_copy(
            src_ref=o_refs[t].at[c], dst_ref=o_refs[t].at[c], send_sem=send_sems.at[t], recv_sem=recv_sems.at[t],
            device_id=(x, y, 1 - c), device_id_type=MESH) for t in range(n)]
        for cp in copies:
            cp.start()
        for t in range(n):
            copies[t].wait_send()
            pltpu.make_async_remote_copy(
                src_ref=o_refs[t].at[c], dst_ref=o_refs[t].at[1 - c], send_sem=send_sems.at[t],
                recv_sem=recv_sems.at[t], device_id=(x, y, 1 - c), device_id_type=MESH).wait_recv()

    return pl.pallas_call(
        body, name=name,
        out_shape=[jax.ShapeDtypeStruct(t.shape, t.dtype) for t in ts],
        in_specs=[ANY] * n, out_specs=[ANY] * n, input_output_aliases={t: t for t in range(n)},
        scratch_shapes=[pltpu.SemaphoreType.DMA((n,)), pltpu.SemaphoreType.DMA((n,))],
    )(*ts)


def _row_block(rows, cols, budget):
    best = 8
    for rb in range(8, rows + 1, 8):
        if rows % rb == 0 and rb * cols * 4 <= budget:
            best = rb
    return best


def _add_half_call(name, g, b, ck):
    _, _, r, cc = g.shape
    rb = _row_block(r, cc, 2 * 1024 * 1024)

    def body(ck_ref, g_ref, b_ref, o_ref):
        o_ref[...] = (g_ref[...] + b_ref[...]).astype(BF16)

    return pl.pallas_call(
        body, name=name,
        grid_spec=pltpu.PrefetchScalarGridSpec(
            num_scalar_prefetch=1, grid=(4, r // rb),
            in_specs=[pl.BlockSpec((None, None, rb, cc), lambda k, i, ck_ref: (ck_ref[0], k, i, 0)),
                      pl.BlockSpec((None, rb, cc), lambda k, i, ck_ref: (k, i, 0))],
            out_specs=pl.BlockSpec((None, rb, cc), lambda k, i, ck_ref: (k, i, 0))),
        out_shape=jax.ShapeDtypeStruct(b.shape, BF16),
        compiler_params=_cparams(2),
    )(ck, g, b)


def _add_rows_call(name, g, b, ck):
    _, r, cc = g.shape
    rb = _row_block(r, cc, 2 * 1024 * 1024)

    def body(ck_ref, g_ref, b_ref, o_ref):
        o_ref[...] = (g_ref[...] + b_ref[...]).astype(BF16)

    return pl.pallas_call(
        body, name=name,
        grid_spec=pltpu.PrefetchScalarGridSpec(
            num_scalar_prefetch=1, grid=(r // rb,),
            in_specs=[pl.BlockSpec((None, rb, cc), lambda i, ck_ref: (ck_ref[0], i, 0)),
                      pl.BlockSpec((rb, cc), lambda i, ck_ref: (i, 0))],
            out_specs=pl.BlockSpec((rb, cc), lambda i, ck_ref: (i, 0))),
        out_shape=jax.ShapeDtypeStruct((r, cc), BF16),
        compiler_params=_cparams(1),
    )(ck, g, b)


def _add_window_call(name, g, b, p, ck):
    _, r, _ = g.shape
    nb, step = WIN_W // 128, WIN_STEP // 128

    def body(ck_ref, g_ref, b_ref, p0_ref, p1_ref, p2_ref, o_ref):
        own = g_ref[...] + b_ref[...]
        o_ref[...] = ((own + p0_ref[...].astype(F32)) + p1_ref[...].astype(F32)) + p2_ref[...].astype(F32)

    def peer(j):
        return pl.BlockSpec((None, r, 128), lambda i, ck_ref: (j, 0, i))

    return pl.pallas_call(
        body, name=name,
        grid_spec=pltpu.PrefetchScalarGridSpec(
            num_scalar_prefetch=1, grid=(nb,),
            in_specs=[pl.BlockSpec((None, r, 128), lambda i, ck_ref: (ck_ref[0], 0, step * ck_ref[1] + i)),
                      pl.BlockSpec((r, 128), lambda i, ck_ref: (0, step * ck_ref[1] + i)),
                      peer(0), peer(1), peer(2)],
            out_specs=pl.BlockSpec((None, r, 128), lambda i, ck_ref: (ck_ref[0], 0, i))),
        out_shape=jax.ShapeDtypeStruct((2, r, WIN_W), F32),
        compiler_params=_cparams(1),
    )(ck, g, b, p, p, p)


def _add_chips_call(name, g, b, p, ck):
    _, _, r, cc = g.shape
    rb = _row_block(r, cc, 2 * 1024 * 1024)

    def body(ck_ref, g_ref, b_ref, p0_ref, p1_ref, p2_ref, o_ref):
        own = g_ref[...] + b_ref[...]
        o_ref[...] = ((own + p0_ref[...].astype(F32)) + p1_ref[...].astype(F32)) + p2_ref[...].astype(F32)

    def peer(j):
        return pl.BlockSpec((None, rb, cc), lambda i, ck_ref: (j, i, 0))

    return pl.pallas_call(
        body, name=name,
        grid_spec=pltpu.PrefetchScalarGridSpec(
            num_scalar_prefetch=1, grid=(r // rb,),
            in_specs=[pl.BlockSpec((None, None, rb, cc), lambda i, ck_ref: (ck_ref[0], ck_ref[1], i, 0)),
                      pl.BlockSpec((None, rb, cc), lambda i, ck_ref: (ck_ref[1], i, 0)),
                      peer(0), peer(1), peer(2)],
            out_specs=pl.BlockSpec((None, rb, cc), lambda i, ck_ref: (ck_ref[0], i, 0))),
        out_shape=jax.ShapeDtypeStruct((2, r, cc), F32),
        compiler_params=_cparams(1),
    )(ck, g, b, p, p, p)


def _sum8_call(name, g):
    def body(g_ref, o_ref):
        acc = g_ref[0]
        for d in range(1, 8):
            acc = acc + g_ref[d]
        o_ref[...] = acc

    return pl.pallas_call(body, name=name, out_shape=jax.ShapeDtypeStruct(g.shape[1:], F32))(g)


def _adamw_call(name, w, g, m, v):
    r, cc = w.shape
    if r % 8 == 0 or r * cc * 4 <= 1024 * 1024:
        rb = _row_block(r, cc, 1024 * 1024) if r % 8 == 0 else r
        grid, spec = (r // rb,), pl.BlockSpec((rb, cc), lambda i: (i, 0))
    else:
        grid, spec = (cc // 128,), pl.BlockSpec((r, 128), lambda i: (0, i))

    def body(w_ref, g_ref, m_ref, v_ref, d_ref, m2_ref, v2_ref):
        gv = g_ref[...]
        m2 = ADAM_B1 * m_ref[...] + (1.0 - ADAM_B1) * gv
        v2 = ADAM_B2 * v_ref[...] + (1.0 - ADAM_B2) * (gv * gv)
        m_hat = m2 / (1.0 - ADAM_B1 ** ADAM_STEP)
        v_hat = v2 / (1.0 - ADAM_B2 ** ADAM_STEP)
        d_ref[...] = -ADAM_LR * (m_hat / (jnp.sqrt(v_hat) + ADAM_EPS) + ADAM_WD * w_ref[...])
        m2_ref[...] = m2
        v2_ref[...] = v2

    return pl.pallas_call(
        body, name=name, grid=grid, in_specs=[spec] * 4, out_specs=[spec] * 3,
        out_shape=[jax.ShapeDtypeStruct((r, cc), F32)] * 3, compiler_params=_cparams(1),
    )(w, g, m, v)


SMALL = (("norm_gain", D_MODEL), ("b_gate", GLA_KW), ("ret_norm_gain", RET_W), ("gla_norm_gain", GLA_W),
         ("final_norm_gain", D_MODEL), ("w_gate_up", GATE_RANK * GLA_KW), ("meta_tokens", N_META * D_MODEL),
         ("loss", 1))


def _pack_rows(vecs, rows):
    flat = jnp.concatenate([v.reshape(-1) for v in vecs])
    return jnp.pad(flat, (0, rows * 128 - flat.shape[0])).reshape(rows, 128)


def kernel(x, meta_tokens, norm_gain, w_in, w_gate_up, b_gate, ret_norm_gain, gla_norm_gain, w_branch_ret, w_branch_gla, w_out, final_norm_gain, loss_target, m_meta_tokens, m_norm_gain, m_w_in, m_w_gate_up, m_b_gate, m_ret_norm_gain, m_gla_norm_gain, m_w_branch_ret, m_w_branch_gla, m_w_out, m_final_norm_gain, v_meta_tokens, v_norm_gain, v_w_in, v_w_gate_up, v_b_gate, v_ret_norm_gain, v_gla_norm_gain, v_w_branch_ret, v_w_branch_gla, v_w_out, v_final_norm_gain):
    xi, yi, ci = _place()
    kme = 2 * xi + yi
    ck = jnp.stack([ci, kme]).astype(jnp.int32)
    sw_in = w_in.shape[2]

    def my_half(a, dtype):
        r, cc = a.shape
        return lax.dynamic_index_in_dim(a.reshape(2, r // 2, cc), ci, 0, keepdims=False).astype(dtype)

    g_in, g_meta, g_wg = _gather8_call(
        "gather_weights", [my_half(w_in[0], BF16), my_half(meta_tokens, F32), my_half(w_gate_up[0], F32)],
        relay=(True,))
    branch_parts = [my_half(w_branch_ret[0], BF16), my_half(w_branch_gla[0], BF16), my_half(w_out[0], BF16)]
    w_in_bf = g_in.reshape(4, 2, D_MODEL // 2, sw_in).transpose(1, 2, 0, 3).reshape(D_MODEL, 4 * sw_in)
    meta = g_meta.reshape(4, 2, N_META // 2, D_MODEL // 4).transpose(1, 2, 0, 3).reshape(N_META, D_MODEL)
    wg_full = g_wg.reshape(4, 2, GATE_RANK // 2, GLA_KW // 4).transpose(1, 2, 0, 3).reshape(GATE_RANK, GLA_KW)

    loc = _device_step(x[0], loss_target[0], meta, norm_gain, w_in_bf, wg_full, b_gate, ret_norm_gain, gla_norm_gain,
                       branch_parts, final_norm_gain, ck)
    names = ("w_in", "w_branch_ret", "w_branch_gla", "w_out")
    full = [loc[nm] for nm in names]
    big_w = dict(w_in=w_in[0], w_branch_ret=w_branch_ret[0], w_branch_gla=w_branch_gla[0], w_out=w_out[0])
    big_m = dict(w_in=m_w_in[0], w_branch_ret=m_w_branch_ret[0], w_branch_gla=m_w_branch_gla[0], w_out=m_w_out[0])
    big_v = dict(w_in=v_w_in[0], w_branch_ret=v_w_branch_ret[0], w_branch_gla=v_w_branch_gla[0], w_out=v_w_out[0])
    grads, deltas, new_m, new_v = {}, {}, {}, {}
    for nm, f in zip(names, full):
        shape = big_w[nm].shape
        if nm == "w_in":
            f = lax.dynamic_slice_in_dim(f, (sw_in - WIN_STEP) * kme, sw_in, axis=2)
        g = f.reshape(shape)
        if nm == "w_in":
            d, m2, v2 = (a.T for a in _adamw_call("adamw_" + nm, big_w[nm].T, g.T, big_m[nm].T, big_v[nm].T))
        else:
            d, m2, v2 = _adamw_call("adamw_" + nm, big_w[nm], g, big_m[nm], big_v[nm])
        grads[nm], deltas[nm], new_m[nm], new_v[nm] = (a.reshape((1,) + shape) for a in (g, d, m2, v2))

    small_g = dict(loc)
    small_g["meta_tokens"] = loc["dmeta"]
    n_small = sum(sz for _, sz in SMALL)
    rows = -(-n_small // 128 // 8) * 8
    (g_small,) = _gather8_call("gather_small_grads", [_pack_rows([small_g[nm] for nm, _ in SMALL], rows)])
    tot = _sum8_call("sum_small_grads", g_small).reshape(-1)
    off = 0
    sg = {}
    for nm, sz in SMALL:
        sg[nm] = tot[off:off + sz]
        off += sz
    loss = sg.pop("loss")[0]
    sg["w_gate_up"] = lax.dynamic_slice_in_dim(sg["w_gate_up"].reshape(GATE_RANK, GLA_KW), kme * (GLA_KW // 4),
                                               GLA_KW // 4, axis=1)
    sg["meta_tokens"] = lax.dynamic_slice_in_dim(sg["meta_tokens"].reshape(N_META, D_MODEL), kme * (D_MODEL // 4),
                                                 D_MODEL // 4, axis=1)
    small_w = dict(norm_gain=norm_gain, b_gate=b_gate, ret_norm_gain=ret_norm_gain, gla_norm_gain=gla_norm_gain,
                   final_norm_gain=final_norm_gain, w_gate_up=w_gate_up, meta_tokens=meta_tokens)
    small_m = dict(norm_gain=m_norm_gain, b_gate=m_b_gate, ret_norm_gain=m_ret_norm_gain,
                   gla_norm_gain=m_gla_norm_gain, final_norm_gain=m_final_norm_gain, w_gate_up=m_w_gate_up,
                   meta_tokens=m_meta_tokens)
    small_v = dict(norm_gain=v_norm_gain, b_gate=v_b_gate, ret_norm_gain=v_ret_norm_gain,
                   gla_norm_gain=v_gla_norm_gain, final_norm_gain=v_final_norm_gain, w_gate_up=v_w_gate_up,
                   meta_tokens=v_meta_tokens)
    for nm in small_w:
        shape = small_w[nm].shape
        as2d = lambda a: a.reshape((-1, shape[-1]))
        grads[nm] = sg[nm].reshape(shape)
        deltas[nm], new_m[nm], new_v[nm] = (a.reshape(shape) for a in _adamw_call(
            "adamw_" + nm, as2d(small_w[nm]), as2d(sg[nm]), as2d(small_m[nm]), as2d(small_v[nm])))

    out_order = ("meta_tokens", "norm_gain", "w_in", "w_gate_up", "b_gate", "ret_norm_gain", "gla_norm_gain",
                 "w_branch_ret", "w_branch_gla", "w_out", "final_norm_gain")
    dx = loc["dx"].reshape(x.shape)
    return (loss, dx, *[grads[nm] for nm in out_order], *[deltas[nm] for nm in out_order],
            *[new_m[nm] for nm in out_order], *[new_v[nm] for nm in out_order])
```

```python
import math
from typing import Callable, NamedTuple

import numpy as np
import jax
import jax.numpy as jnp
from jax import lax
from jax.experimental import pallas as pl
from jax.experimental.pallas import tpu as pltpu

F32 = jnp.float32
BF16 = jnp.bfloat16

D_MODEL = 1024
N_META = 16
EPS = 1e-6
ROPE_BASE = 10000.0
RET_HEADS, RET_QK, RET_V = 4, 256, 512
RET_W = RET_HEADS * RET_V
GLA_HEADS, GLA_K, GLA_V = 4, 128, 256
GLA_W = GLA_HEADS * GLA_V
GLA_KW = GLA_HEADS * GLA_K
GATE_RANK = 16
GATE_TAU = 16.0
GLA_SUB = 16

TM = 256
T0 = TM
PADF = T0 - N_META
GC = 128
TB = 768
TK = 768

W_R = 6144
W_G = 3088
W_GP = 3200
W_M = 2048
IN_COLS = W_R + W_G + W_M
WIN_STEP = (IN_COLS // 4) // 128 * 128
WIN_W = -(-(3 * (IN_COLS // 4 - WIN_STEP) + IN_COLS // 4) // 128) * 128
IN_PAD = 3 * WIN_STEP + WIN_W

ADAM_LR, ADAM_B1, ADAM_B2, ADAM_EPS, ADAM_WD, ADAM_STEP = 0.001, 0.9, 0.999, 1e-08, 0.01, 10

VMEM_LIMIT = 56 * 1024 * 1024

NN = ((1,), (0,))
NT = ((1,), (1,))
TN = ((0,), (0,))


def _dot(a, b, dims):
    return lax.dot_general(a, b, (dims, ((), ())), preferred_element_type=F32)


def _cparams(n_axes):
    return pltpu.CompilerParams(dimension_semantics=("arbitrary",) * n_axes, vmem_limit_bytes=VMEM_LIMIT)


def _sigmoid(x):
    return 0.5 * jnp.tanh(0.5 * x) + 0.5


def _split3(x):
    hi = x.astype(BF16)
    r1 = x - hi.astype(F32)
    mid = r1.astype(BF16)
    lo = (r1 - mid.astype(F32)).astype(BF16)
    return hi, mid, lo


def _exact_pm(p, x):
    hi, mid, lo = _split3(x)
    return _dot(p, hi, NN) + _dot(p, mid, NN) + _dot(p, lo, NN)


def _rms_call(x2d, head, gain):
    tp = T0 + x2d.shape[0]

    def body(x_ref, hd_ref, g_ref, h_ref, u_ref):
        h = jnp.where(pl.program_id(0) == 0, hd_ref[...], x_ref[...])
        h_ref[...] = h
        r = lax.rsqrt(jnp.mean(h * h, axis=-1, keepdims=True) + EPS)
        u_ref[...] = (h * r * g_ref[...]).astype(BF16)

    tile = pl.BlockSpec((TM, D_MODEL), lambda i: (i, 0))
    return pl.pallas_call(
        body, name="rms_in", grid=(tp // TM,),
        in_specs=[pl.BlockSpec((TM, D_MODEL), lambda i: (jnp.maximum(i - 1, 0), 0)),
                  pl.BlockSpec((T0, D_MODEL), lambda i: (0, 0)), pl.BlockSpec((1, D_MODEL), lambda i: (0, 0))],
        out_specs=[tile, tile],
        out_shape=[jax.ShapeDtypeStruct((tp, D_MODEL), F32), jax.ShapeDtypeStruct((tp, D_MODEL), BF16)],
        compiler_params=_cparams(1),
    )(x2d, head, gain)


PROJ_ROWS_MAX = 1408


def _proj_rows(m):
    return max(r for r in range(16, PROJ_ROWS_MAX + 1, 16) if m % r == 0)


def _mm_nn(name, a, b, out_dtype, tn, col0, ncols, epilogue=None, extras=(), extra_specs=()):
    m, k = a.shape
    nj, j0 = ncols // tn, col0 // tn
    tb = _proj_rows(m)

    def body(a_ref, b_ref, *rest):
        *ex, o_ref = rest
        acc = _dot(a_ref[...], b_ref[...], NN)
        if epilogue is None:
            o_ref[...] = acc.astype(out_dtype)
        else:
            epilogue(acc, o_ref, *ex)

    return pl.pallas_call(
        body, name=name, grid=(nj, m // tb),
        in_specs=[pl.BlockSpec((tb, k), lambda j, i: (i, 0)), pl.BlockSpec((k, tn), lambda j, i: (0, j0 + j))]
        + list(extra_specs),
        out_specs=pl.BlockSpec((tb, tn), lambda j, i: (i, j)),
        out_shape=jax.ShapeDtypeStruct((m, ncols), out_dtype),
        compiler_params=_cparams(2),
    )(a, b, *extras)


def _rope_epilogue(acc, o_ref, cos_ref, sin_ref):
    scale = jnp.where(pl.program_id(0) == 1, RET_QK ** -0.5, 1.0).astype(F32)
    cos, sin = cos_ref[...], sin_ref[...]
    half = RET_QK // 2
    for h in range(RET_HEADS):
        t1 = acc[:, h * RET_QK:h * RET_QK + half]
        t2 = acc[:, h * RET_QK + half:(h + 1) * RET_QK]
        o_ref[:, h * RET_QK:h * RET_QK + half] = ((t1 * cos - t2 * sin) * scale).astype(BF16)
        o_ref[:, h * RET_QK + half:(h + 1) * RET_QK] = ((t2 * cos + t1 * sin) * scale).astype(BF16)


def _gqk_epilogue(acc, o_ref):
    o_ref[:, :GLA_KW] = acc[:, :GLA_KW] * (GLA_K ** -0.5)
    o_ref[:, GLA_KW:] = acc[:, GLA_KW:]


class _Comm(NamedTuple):
    srcs: tuple
    out_shapes: tuple
    n_sems: int
    make: Callable


def _comm_sems(comm):
    return [pltpu.SemaphoreType.DMA((comm.n_sems,)), pltpu.SemaphoreType.DMA((comm.n_sems,))]


def _other_chips(x, y):
    return [(1 - x, y), (x, 1 - y), (1 - x, 1 - y)]


def _exchange_plan(ss):
    def make(s_refs, b_refs, send_sems, recv_sems):
        x, y, c = _place()
        return [pltpu.make_async_remote_copy(
            src_ref=s_refs[t].at[2 * chip[0] + chip[1]], dst_ref=b_refs[t].at[j], send_sem=send_sems.at[3 * t + j],
            recv_sem=recv_sems.at[3 * t + j], device_id=(*chip, c), device_id_type=MESH)
            for t in range(len(s_refs)) for j, chip in enumerate(_other_chips(x, y))]

    return _Comm(tuple(ss), tuple(jax.ShapeDtypeStruct((3,) + s.shape[1:], s.dtype) for s in ss), 3 * len(ss), make)


def _exchange_window_plan(s):
    def make(s_refs, b_refs, send_sems, recv_sems):
        x, y, c = _place()
        return [pltpu.make_async_remote_copy(
            src_ref=s_refs[0].at[:, pl.ds(pl.multiple_of((2 * chip[0] + chip[1]) * WIN_STEP, 128), WIN_W)],
            dst_ref=b_refs[0].at[j], send_sem=send_sems.at[j], recv_sem=recv_sems.at[j], device_id=(*chip, c),
            device_id_type=MESH) for j, chip in enumerate(_other_chips(x, y))]

    return _Comm((s,), (jax.ShapeDtypeStruct((3, s.shape[0], WIN_W), s.dtype),), 3, make)


def _swap_plan(gs):
    def make(g_refs, b_refs, send_sems, recv_sems):
        x, y, c = _place()
        return [pltpu.make_async_remote_copy(
            src_ref=g_refs[t].at[1 - c], dst_ref=b_refs[t], send_sem=send_sems.at[t], recv_sem=recv_sems.at[t],
            device_id=(x, y, 1 - c), device_id_type=MESH) for t in range(len(g_refs))]

    return _Comm(tuple(gs), tuple(jax.ShapeDtypeStruct(g.shape[1:], g.dtype) for g in gs), len(gs), make)


def _spread_plan(parts):
    def make(p_refs, o_refs, send_sems, recv_sems):
        x, y, c = _place()
        copies = []
        for t in range(len(p_refs)):
            mine = o_refs[t].at[4 * x + 2 * y + c]
            copies.append(pltpu.make_async_copy(p_refs[t], mine, send_sems.at[7 * len(p_refs) + t]))
            for r in range(1, 8):
                peer = (1 - x if r & 4 else x, 1 - y if r & 2 else y, 1 - c if r & 1 else c)
                copies.append(pltpu.make_async_remote_copy(
                    src_ref=p_refs[t], dst_ref=mine, send_sem=send_sems.at[7 * t + r - 1],
                    recv_sem=recv_sems.at[7 * t + r - 1], device_id=peer, device_id_type=MESH))
        return copies

    return _Comm(tuple(parts), tuple(jax.ShapeDtypeStruct((8,) + p.shape, p.dtype) for p in parts), 8 * len(parts),
                 make)


def _mm_nt_acc(name, a, w, tk, acc_in=None, epilogue=None, extras=(), extra_specs=(), extra_out_shapes=(),
               extra_out_specs=(), extra_scratch=(), comm=None):
    m, k = a.shape
    n = w.shape[0]
    nk, ni = k // tk, m // TB
    has_acc = acc_in is not None
    n_xc = len(comm.srcs) if comm else 0
    n_es = len(extra_scratch)

    def body(*refs):
        a_ref, w_ref = refs[0], refs[1]
        pos = 2
        acc_ref = None
        if has_acc:
            acc_ref = refs[pos]
            pos += 1
        ex = refs[pos:pos + len(extras)]
        pos += len(extras)
        xc_src = refs[pos:pos + n_xc]
        pos += n_xc
        n_scr = 1 + n_es + (2 if n_xc else 0)
        outs = refs[pos:len(refs) - n_scr - n_xc]
        xc_dst = refs[len(refs) - n_scr - n_xc:len(refs) - n_scr]
        scr = refs[len(refs) - n_scr]
        es = refs[len(refs) - n_scr + 1:len(refs) - n_scr + 1 + n_es]
        i, kk = pl.program_id(0), pl.program_id(1)
        if n_xc:
            copies = comm.make(xc_src, xc_dst, refs[-2], refs[-1])

            @pl.when((i == 0) & (kk == 0))
            def _():
                for cp in copies:
                    cp.start()

        @pl.when(kk == 0)
        def _():
            scr[...] = acc_ref[...] if has_acc else jnp.zeros_like(scr)

        scr[...] += _dot(a_ref[...], w_ref[...], NT)

        @pl.when(kk == nk - 1)
        def _():
            if epilogue is None:
                outs[0][...] = scr[...]
            else:
                epilogue(scr[...], outs, i, ni, *ex, *es)

        if n_xc:
            @pl.when((i == ni - 1) & (kk == nk - 1))
            def _():
                for cp in copies:
                    cp.wait()

    in_specs = [pl.BlockSpec((TB, tk), lambda i, kk: (i, kk)), pl.BlockSpec((n, tk), lambda i, kk: (0, kk))]
    args = [a, w]
    if has_acc:
        in_specs.append(pl.BlockSpec((TB, n), lambda i, kk: (i, 0)))
        args.append(acc_in)
    in_specs += list(extra_specs) + [ANY] * n_xc
    args += list(extras) + (list(comm.srcs) if comm else [])
    if epilogue is None:
        out_shape = [jax.ShapeDtypeStruct((m, n), F32)]
        out_specs = [pl.BlockSpec((TB, n), lambda i, kk: (i, 0))]
    else:
        out_shape, out_specs = list(extra_out_shapes), list(extra_out_specs)
    scratch = [pltpu.VMEM((TB, n), F32)] + list(extra_scratch)
    if n_xc:
        out_shape += list(comm.out_shapes)
        out_specs += [ANY] * n_xc
        scratch += _comm_sems(comm)
    return pl.pallas_call(
        body, name=name, grid=(ni, nk), in_specs=in_specs, out_specs=out_specs, out_shape=out_shape,
        scratch_shapes=scratch, compiler_params=_cparams(2),
    )(*args)


def _rms_bwd_epilogue(du, outs, i, ni, h_ref, g_ref, dh1_ref, obuf, sems):
    dx_ref, dmeta_ref, dg_ref = outs
    h = h_ref[...]
    r = lax.rsqrt(jnp.mean(h * h, axis=-1, keepdims=True) + EPS)
    xh = h * r
    dxh = du * g_ref[...]
    dh0 = dh1_ref[...] + r * (dxh - xh * jnp.mean(dxh * xh, axis=-1, keepdims=True))

    def put(slot, tile):
        return pltpu.make_async_copy(obuf.at[slot], dx_ref.at[pl.ds(pl.multiple_of(tile * TB - T0, 8), TB)],
                                     sems.at[slot])

    @pl.when(i == 0)
    def _():
        dg_ref[...] = jnp.zeros_like(dg_ref)
        dmeta_ref[...] = dh0[PADF:T0, :]
        obuf[0] = dh0
        first = pltpu.make_async_copy(obuf.at[0, pl.ds(T0, TB - T0)], dx_ref.at[pl.ds(0, TB - T0)], sems.at[0])
        first.start()
        first.wait()

    @pl.when(i >= 1)
    def _():
        slot = i % 2

        @pl.when(i >= 3)
        def _():
            put(slot, i - 2).wait()

        obuf[slot] = dh0
        put(slot, i).start()

    dg_ref[...] += jnp.sum(du * xh, axis=0, keepdims=True)

    @pl.when(i == ni - 1)
    def _():
        for tile in (ni - 2, ni - 1):
            if tile >= 1:
                put(tile % 2, tile).wait()


def _mm_tn(name, a, b, bn, ncols=None, bcol0=0, into=None, col0=0, out_cols=None):
    t, m = a.shape
    n = ncols or b.shape[1]
    j0, bj0 = col0 // bn, bcol0 // bn

    def body(a_ref, b_ref, *rest):
        o_ref = rest[-1]

        @pl.when(pl.program_id(1) == 0)
        def _():
            o_ref[...] = jnp.zeros_like(o_ref)

        o_ref[...] += _dot(a_ref[...], b_ref[...], TN)

    in_specs = [pl.BlockSpec((TK, m), lambda j, kk: (kk, 0)), pl.BlockSpec((TK, bn), lambda j, kk: (kk, bj0 + j))]
    args = [a, b]
    aliases = {}
    if into is not None:
        in_specs.append(ANY)
        args.append(into)
        aliases = {2: 0}
        out_cols = into.shape[1]
    return pl.pallas_call(
        body, name=name, grid=(n // bn, t // TK), in_specs=in_specs,
        out_specs=pl.BlockSpec((m, bn), lambda j, kk: (0, j0 + j)),
        out_shape=jax.ShapeDtypeStruct((m, out_cols or n), F32), input_output_aliases=aliases,
        compiler_params=_cparams(2),
    )(*args)


def _place_merge_cols_call(dwp, dw_m):
    c0 = W_R + W_GP - 128
    tail = IN_PAD - c0
    rows = 256

    def body(m_ref, p_ref, o_ref, buf, low, sem):
        get = pltpu.make_async_copy(o_ref.at[:, pl.ds(c0, 128)], low, sem)
        get.start()
        get.wait()
        for r in range(0, D_MODEL, rows):
            buf[r:r + rows, :] = jnp.concatenate(
                [low[r:r + rows, :GATE_RANK], m_ref[r:r + rows, :],
                 jnp.zeros((rows, tail - GATE_RANK - W_M), F32)], axis=1)
        put = pltpu.make_async_copy(buf, o_ref.at[:, pl.ds(c0, tail)], sem)
        put.start()
        put.wait()

    return pl.pallas_call(
        body, name="place_merge_cols",
        in_specs=[pl.BlockSpec(memory_space=pltpu.VMEM), ANY], out_specs=ANY,
        out_shape=jax.ShapeDtypeStruct(dwp.shape, F32), input_output_aliases={1: 0},
        scratch_shapes=[pltpu.VMEM((D_MODEL, tail), F32), pltpu.VMEM((D_MODEL, 128), F32), pltpu.SemaphoreType.DMA],
        compiler_params=pltpu.CompilerParams(vmem_limit_bytes=VMEM_LIMIT),
    )(dw_m, dwp)


def _ret_fill_decay(lg_ref, dm_scr):
    c = TM
    ii = lax.broadcasted_iota(jnp.int32, (c, c), 0)
    jj = lax.broadcasted_iota(jnp.int32, (c, c), 1)
    rel = (ii - jj).astype(F32)
    for h in range(RET_HEADS):
        dm_scr[h] = jnp.where(rel >= 0, jnp.exp(jnp.maximum(rel, 0.0) * lg_ref[h]), 0.0)


def _ret_consts(lg, dm_ref):
    c = TM
    idx = lax.broadcasted_iota(jnp.int32, (c, 1), 0).astype(F32)
    xi = jnp.exp((idx + 1.0) * lg)
    zeta = jnp.exp((c - 1.0 - idx) * lg)
    gc = jnp.exp(jnp.full((1, 1), c, F32) * lg)
    return dm_ref[...], xi, zeta, gc


def _ret_fwd_call(rqk, rv, rg, gain, lgam):
    tp = rqk.shape[0]
    nc = tp // TM

    def body(lg_ref, qk_ref, v_ref, rg_ref, g_ref, o_ref, a_ref, st_ref, s_scr, dm_scr):
        @pl.when(pl.program_id(0) == 0)
        def _():
            s_scr[...] = jnp.zeros_like(s_scr)
            _ret_fill_decay(lg_ref, dm_scr)

        for h in range(RET_HEADS):
            dm, xi, zeta, gc = _ret_consts(lg_ref[h], dm_scr.at[h])
            q = qk_ref[:, h * RET_QK:(h + 1) * RET_QK]
            k = qk_ref[:, D_MODEL + h * RET_QK:D_MODEL + (h + 1) * RET_QK]
            v = v_ref[:, h * RET_V:(h + 1) * RET_V]
            sb = s_scr[h].astype(BF16)
            st_ref[0, h] = sb
            s = _dot(q, k, NT) * dm
            o = _dot(s.astype(BF16), v, NN) + xi * _dot(q, sb, NN)
            kz = (k.astype(F32) * zeta).astype(BF16)
            s_scr[h] = gc * s_scr[h] + _dot(kz, v, TN)
            o_ref[:, h * RET_V:(h + 1) * RET_V] = o
            mu = jnp.mean(o, axis=-1, keepdims=True)
            xc = o - mu
            xh = xc * lax.rsqrt(jnp.mean(xc * xc, axis=-1, keepdims=True) + EPS)
            g = rg_ref[:, h * RET_V:(h + 1) * RET_V]
            a_ref[:, h * RET_V:(h + 1) * RET_V] = (
                xh * g_ref[:, h * RET_V:(h + 1) * RET_V] * (g * _sigmoid(g))).astype(BF16)

    return pl.pallas_call(
        body, name="ret_fwd", grid=(nc,),
        in_specs=[pl.BlockSpec(memory_space=pltpu.SMEM),
                  pl.BlockSpec((TM, 2 * D_MODEL), lambda n: (n, 0)),
                  pl.BlockSpec((TM, RET_W), lambda n: (n, 0)),
                  pl.BlockSpec((TM, RET_W), lambda n: (n, 0)),
                  pl.BlockSpec((1, RET_W), lambda n: (0, 0))],
        out_specs=[pl.BlockSpec((TM, RET_W), lambda n: (n, 0)),
                   pl.BlockSpec((TM, RET_W), lambda n: (n, 0)),
                   pl.BlockSpec((1, RET_HEADS, RET_QK, RET_V), lambda n: (n, 0, 0, 0))],
        out_shape=[jax.ShapeDtypeStruct((tp, RET_W), F32), jax.ShapeDtypeStruct((tp, RET_W), BF16),
                   jax.ShapeDtypeStruct((nc, RET_HEADS, RET_QK, RET_V), BF16)],
        scratch_shapes=[pltpu.VMEM((RET_HEADS, RET_QK, RET_V), F32), pltpu.VMEM((RET_HEADS, TM, TM), F32)],
        compiler_params=_cparams(1),
    )(lgam, rqk, rv, rg, gain)


def _ret_bwd_call(rqk, rv, rg, o_ret, dpr, wbr, states, gain, lgam, cos, sin):
    tp = rqk.shape[0]
    nc = tp // TM
    half = RET_QK // 2

    def body(lg_ref, qk_ref, v_ref, rg_ref, o_ref, dpr_ref, wbr_ref, st_ref, g_ref, cos_ref, sin_ref, dp_ref, dg_ref,
             ds_scr, dm_scr):
        @pl.when(pl.program_id(0) == 0)
        def _():
            ds_scr[...] = jnp.zeros_like(ds_scr)
            dg_ref[...] = jnp.zeros_like(dg_ref)
            _ret_fill_decay(lg_ref, dm_scr)

        cos, sin = cos_ref[...], sin_ref[...]
        for h in range(RET_HEADS):
            hs = slice(h * RET_V, (h + 1) * RET_V)
            dm, xi, zeta, gc = _ret_consts(lg_ref[h], dm_scr.at[h])
            o = o_ref[:, hs]
            mu = jnp.mean(o, axis=-1, keepdims=True)
            xc = o - mu
            rstd = lax.rsqrt(jnp.mean(xc * xc, axis=-1, keepdims=True) + EPS)
            xh = xc * rstd
            gain_h = g_ref[:, hs]
            g = rg_ref[:, hs]
            sg = _sigmoid(g)
            silu = g * sg
            dah = _dot(dpr_ref[...], wbr_ref[hs, :], NT)
            dp_ref[:, 4 * D_MODEL + h * RET_V:4 * D_MODEL + (h + 1) * RET_V] = (
                dah * (xh * gain_h) * (sg * (1.0 + g * (1.0 - sg)))).astype(BF16)
            dn = dah * silu
            dg_ref[:, hs] += jnp.sum(dn * xh, axis=0, keepdims=True)
            dxh = dn * gain_h
            do = rstd * (dxh - jnp.mean(dxh, axis=-1, keepdims=True)
                         - xh * jnp.mean(dxh * xh, axis=-1, keepdims=True))
            dob = do.astype(BF16)
            q = qk_ref[:, h * RET_QK:(h + 1) * RET_QK]
            k = qk_ref[:, D_MODEL + h * RET_QK:D_MODEL + (h + 1) * RET_QK]
            v = v_ref[:, hs]
            sp = st_ref[0, h]
            ds = ds_scr[h]
            dsb = ds.astype(BF16)
            s = (_dot(q, k, NT) * dm).astype(BF16)
            dsc = (_dot(dob, v, NT) * dm).astype(BF16)
            dq = _dot(dsc, k, NN) + xi * _dot(dob, sp, NT)
            dk = _dot(dsc, q, TN) + zeta * _dot(v, dsb, NT)
            kz = (k.astype(F32) * zeta).astype(BF16)
            dv = _dot(s, dob, TN) + _dot(kz, dsb, NN)
            qx = (q.astype(F32) * xi).astype(BF16)
            ds_scr[h] = gc * ds + _dot(qx, dob, TN)
            dp_ref[:, 2 * D_MODEL + h * RET_V:2 * D_MODEL + (h + 1) * RET_V] = dv.astype(BF16)
            dk = dk * (RET_QK ** -0.5)
            for base, t in ((0, dq), (D_MODEL, dk)):
                t1, t2 = t[:, :half], t[:, half:]
                dp_ref[:, base + h * RET_QK:base + h * RET_QK + half] = (t1 * cos + t2 * sin).astype(BF16)
                dp_ref[:, base + h * RET_QK + half:base + (h + 1) * RET_QK] = (t2 * cos - t1 * sin).astype(BF16)

    rev = lambda n: (nc - 1 - n, 0)
    return pl.pallas_call(
        body, name="ret_bwd", grid=(nc,),
        in_specs=[pl.BlockSpec(memory_space=pltpu.SMEM),
                  pl.BlockSpec((TM, 2 * D_MODEL), rev),
                  pl.BlockSpec((TM, RET_W), rev),
                  pl.BlockSpec((TM, RET_W), rev),
                  pl.BlockSpec((TM, RET_W), rev),
                  pl.BlockSpec((TM, D_MODEL), rev),
                  pl.BlockSpec((RET_W, D_MODEL), lambda n: (0, 0)),
                  pl.BlockSpec((1, RET_HEADS, RET_QK, RET_V), lambda n: (nc - 1 - n, 0, 0, 0)),
                  pl.BlockSpec((1, RET_W), lambda n: (0, 0)),
                  pl.BlockSpec((TM, half), rev),
                  pl.BlockSpec((TM, half), rev)],
        out_specs=[pl.BlockSpec((TM, W_R), rev), pl.BlockSpec((1, RET_W), lambda n: (0, 0))],
        out_shape=[jax.ShapeDtypeStruct((tp, W_R), BF16), jax.ShapeDtypeStruct((1, RET_W), F32)],
        scratch_shapes=[pltpu.VMEM((RET_HEADS, RET_QK, RET_V), F32), pltpu.VMEM((RET_HEADS, TM, TM), F32)],
        compiler_params=_cparams(1),
    )(lgam, rqk, rv, rg, o_ret, dpr, wbr, states, gain, cos, sin)


GLA_LEVELS = tuple(GC >> (s + 1) for s in range(int(math.log2(GC // GLA_SUB))))
NLEV = len(GLA_LEVELS)


def _gla_tril():
    return np.tril(np.ones((GC, GC), np.float32))


def _gla_masks():
    ii = lax.broadcasted_iota(jnp.int32, (GC, GC), 0)
    jj = lax.broadcasted_iota(jnp.int32, (GC, GC), 1)
    masks = []
    for m in GLA_LEVELS:
        sh = int(math.log2(2 * m))
        masks.append(((ii >> sh) == (jj >> sh)) & ((ii & m) != 0) & ((jj & m) == 0))
    sh = int(math.log2(GLA_SUB))
    md = ((ii >> sh) == (jj >> sh)) & (jj <= ii)
    row = lax.broadcasted_iota(jnp.int32, (GC, 1), 0)
    second = [(row & m) != 0 for m in GLA_LEVELS]
    return masks, md, second


def _gla_log_decay(glr_ref, wg_ref, bg_ref):
    z = _dot(glr_ref[...].astype(BF16), wg_ref[...], NN) + bg_ref[...]
    la = (jnp.minimum(z, 0.0) - jnp.log1p(jnp.exp(-jnp.abs(z)))) * (1.0 / GATE_TAU)
    return z, la


def _gla_row_steps(b_ref, cs, rows, size):
    parts = [jnp.zeros((size, GLA_K), F32) if r is None else jnp.broadcast_to(b_ref[r:r + 1, cs], (size, GLA_K))
             for r in rows]
    return parts[0] if len(parts) == 1 else jnp.concatenate(parts, axis=0)


def _gla_factors(b_ref, h, second):
    cs = slice(h * GLA_K, (h + 1) * GLA_K)
    b = b_ref[:, cs]
    fq, fk = [], []
    for l, m in enumerate(GLA_LEVELS):
        d = b - _gla_row_steps(b_ref, cs, [s + m - 1 for s in range(0, GC, 2 * m)], 2 * m)
        f = jnp.exp(jnp.where(second[l], d, -d))
        fq.append(jnp.where(second[l], f, 0.0))
        fk.append(jnp.where(second[l], 0.0, f))
    dd = b - _gla_row_steps(b_ref, cs, [None] + [s - 1 for s in range(GLA_SUB, GC, GLA_SUB)], GLA_SUB)
    ed = jnp.exp(dd)
    edi = jnp.exp(-dd)
    eb = jnp.exp(b)
    bl = b_ref[GC - 1:GC, cs]
    ee = jnp.exp(bl - b)
    ebl = jnp.exp(bl)
    return fq, fk, ed, edi, eb, ee, ebl


def _gla_scores(q, k, fq, fk, ed, edi, masks, md):
    qt = [(q * f).astype(BF16) for f in fq]
    kt = [(k * f).astype(BF16) for f in fk]
    qd = (q * ed).astype(BF16)
    kd = (k * edi).astype(BF16)
    a = jnp.where(md, _dot(qd, kd, NT), 0.0)
    for l in range(NLEV):
        a = a + jnp.where(masks[l], _dot(qt[l], kt[l], NT), 0.0)
    return a, qt, kt, qd, kd


def _gla_fwd_call(gqk, gv, glr, gg, wg, bg, gain, pmat, comm=None):
    tp = gqk.shape[0]
    nc = tp // GC
    n_xc = len(comm.srcs) if comm else 0

    def body(qk_ref, v_ref, glr_ref, gg_ref, wg_ref, bg_ref, g_ref, p_ref, *rest):
        xc_src = rest[:n_xc]
        o_ref, a_ref, st_ref = rest[n_xc:n_xc + 3]
        xc_dst = rest[n_xc + 3:2 * n_xc + 3]
        s_scr, b_scr = rest[2 * n_xc + 3:2 * n_xc + 5]
        n = pl.program_id(0)
        if n_xc:
            copies = comm.make(xc_src, xc_dst, rest[-2], rest[-1])

            @pl.when(n == 0)
            def _():
                for cp in copies:
                    cp.start()

            @pl.when(n == nc - 1)
            def _():
                for cp in copies:
                    cp.wait()

        @pl.when(n == 0)
        def _():
            s_scr[...] = jnp.zeros_like(s_scr)

        _, la = _gla_log_decay(glr_ref, wg_ref, bg_ref)
        b_scr[...] = _exact_pm(p_ref[...], la)
        masks, md, second = _gla_masks()
        for h in range(GLA_HEADS):
            q = qk_ref[:, h * GLA_K:(h + 1) * GLA_K]
            k = qk_ref[:, GLA_KW + h * GLA_K:GLA_KW + (h + 1) * GLA_K]
            vs = slice(h * GLA_V, (h + 1) * GLA_V)
            v = v_ref[:, vs]
            fq, fk, ed, edi, eb, ee, ebl = _gla_factors(b_scr, h, second)
            a, *_ = _gla_scores(q, k, fq, fk, ed, edi, masks, md)
            sb = s_scr[h].astype(BF16)
            st_ref[0, h] = sb
            o = _dot(a.astype(BF16), v, NN) + _dot((q * eb).astype(BF16), sb, NT)
            s_scr[h] = s_scr[h] * ebl + _dot(v, (k * ee).astype(BF16), TN)
            o_ref[:, vs] = o
            xh = o * lax.rsqrt(jnp.mean(o * o, axis=-1, keepdims=True) + EPS)
            g = gg_ref[:, vs]
            a_ref[:, vs] = (xh * g_ref[:, vs] * (g * _sigmoid(g))).astype(BF16)

    return pl.pallas_call(
        body, name="gla_fwd", grid=(nc,),
        in_specs=[pl.BlockSpec((GC, 2 * GLA_KW), lambda n: (n, 0)),
                  pl.BlockSpec((GC, GLA_W), lambda n: (n, 0)),
                  pl.BlockSpec((GC, 128), lambda n: (n, 0)),
                  pl.BlockSpec((GC, GLA_W), lambda n: (n, 0)),
                  pl.BlockSpec((128, GLA_KW), lambda n: (0, 0)),
                  pl.BlockSpec((1, GLA_KW), lambda n: (0, 0)),
                  pl.BlockSpec((1, GLA_W), lambda n: (0, 0)),
                  pl.BlockSpec((GC, GC), lambda n: (0, 0))] + [ANY] * n_xc,
        out_specs=[pl.BlockSpec((GC, GLA_W), lambda n: (n, 0)),
                   pl.BlockSpec((GC, GLA_W), lambda n: (n, 0)),
                   pl.BlockSpec((1, GLA_HEADS, GLA_V, GLA_K), lambda n: (n, 0, 0, 0))] + [ANY] * n_xc,
        out_shape=[jax.ShapeDtypeStruct((tp, GLA_W), F32), jax.ShapeDtypeStruct((tp, GLA_W), BF16),
                   jax.ShapeDtypeStruct((nc, GLA_HEADS, GLA_V, GLA_K), BF16)] + (list(comm.out_shapes) if comm else []),
        scratch_shapes=[pltpu.VMEM((GLA_HEADS, GLA_V, GLA_K), F32), pltpu.VMEM((GC, GLA_KW), F32)]
        + (_comm_sems(comm) if comm else []),
        compiler_params=_cparams(1),
    )(gqk, gv, glr, gg, wg, bg, gain, pmat, *(comm.srcs if comm else ()))


def _gla_bwd_call(gqk, gv, glr, gg, o_gla, da, states, wg, bg, gain, pmat, pmat_t, comm=None):
    tp = gqk.shape[0]
    nc = tp // GC
    o_gv, o_gg, o_lr = 2 * GLA_KW, 2 * GLA_KW + GLA_W, 2 * GLA_KW + 2 * GLA_W
    n_xc = len(comm.srcs) if comm else 0

    def body(qk_ref, v_ref, glr_ref, gg_ref, o_ref, da_ref, st_ref, wg_ref, bg_ref, g_ref, p_ref, pt_ref, *rest):
        xc_src = rest[:n_xc]
        dp_ref, dwg_ref, dbg_ref, dg_ref = rest[n_xc:n_xc + 4]
        xc_dst = rest[n_xc + 4:2 * n_xc + 4]
        ds_scr, b_scr, db_scr = rest[2 * n_xc + 4:2 * n_xc + 7]
        n = pl.program_id(0)
        if n_xc:
            copies = comm.make(xc_src, xc_dst, rest[-2], rest[-1])

            @pl.when(n == 0)
            def _():
                for cp in copies:
                    cp.start()

            @pl.when(n == nc - 1)
            def _():
                for cp in copies:
                    cp.wait()

        @pl.when(n == 0)
        def _():
            ds_scr[...] = jnp.zeros_like(ds_scr)
            dwg_ref[...] = jnp.zeros_like(dwg_ref)
            dbg_ref[...] = jnp.zeros_like(dbg_ref)
            dg_ref[...] = jnp.zeros_like(dg_ref)

        z, la = _gla_log_decay(glr_ref, wg_ref, bg_ref)
        b_scr[...] = _exact_pm(p_ref[...], la)
        masks, md, second = _gla_masks()
        for h in range(GLA_HEADS):
            cs = slice(h * GLA_K, (h + 1) * GLA_K)
            vs = slice(h * GLA_V, (h + 1) * GLA_V)
            o = o_ref[:, vs]
            rstd = lax.rsqrt(jnp.mean(o * o, axis=-1, keepdims=True) + EPS)
            xh = o * rstd
            gain_h = g_ref[:, vs]
            g = gg_ref[:, vs]
            sg = _sigmoid(g)
            dah = da_ref[:, vs]
            dp_ref[:, o_gg + h * GLA_V:o_gg + (h + 1) * GLA_V] = (
                dah * (xh * gain_h) * (sg * (1.0 + g * (1.0 - sg)))).astype(BF16)
            dn = dah * (g * sg)
            dg_ref[:, vs] += jnp.sum(dn * xh, axis=0, keepdims=True)
            dxh = dn * gain_h
            do = rstd * (dxh - xh * jnp.mean(dxh * xh, axis=-1, keepdims=True))
            dob = do.astype(BF16)
            q = qk_ref[:, cs]
            k = qk_ref[:, GLA_KW + h * GLA_K:GLA_KW + (h + 1) * GLA_K]
            v = v_ref[:, vs]
            fq, fk, ed, edi, eb, ee, ebl = _gla_factors(b_scr, h, second)
            a, qt, kt, qd, kd = _gla_scores(q, k, fq, fk, ed, edi, masks, md)
            sp = st_ref[0, h]
            ds = ds_scr[h]
            dsb = ds.astype(BF16)
            q_in = q * eb
            k_end = k * ee
            da_s = _dot(dob, v, NT)
            dv = _dot(a.astype(BF16), dob, TN) + _dot(k_end.astype(BF16), dsb, NT)
            dq_in = _dot(dob, sp, NN)
            dk_end = _dot(v, dsb, NN)
            dbl = jnp.sum(sp.astype(F32) * ds, axis=0, keepdims=True) * ebl
            ds_scr[h] = ds * ebl + _dot(dob, q_in.astype(BF16), TN)
            dq = dq_in * eb
            dk = dk_end * ee
            de_end = dk_end * k_end
            db = dq_in * q_in - de_end
            placed = [(GC - 1, jnp.sum(de_end, axis=0, keepdims=True) + dbl)]
            for l, m in enumerate(GLA_LEVELS):
                dal = jnp.where(masks[l], da_s, 0.0).astype(BF16)
                dqt = _dot(dal, kt[l], NN)
                dkt = _dot(dal, qt[l], TN)
                dq = dq + dqt * fq[l]
                dk = dk + dkt * fk[l]
                gl = dqt * (q * fq[l]) - dkt * (k * fk[l])
                db = db + gl
                placed += [(s + m - 1, -jnp.sum(gl[s:s + 2 * m], axis=0, keepdims=True)) for s in range(0, GC, 2 * m)]
            dad = jnp.where(md, da_s, 0.0).astype(BF16)
            dqd = _dot(dad, kd, NN)
            dkd = _dot(dad, qd, TN)
            dq = dq + dqd * ed
            dk = dk + dkd * edi
            gd = dqd * (q * ed) - dkd * (k * edi)
            db = db + gd
            placed += [(s - 1, -jnp.sum(gd[s:s + GLA_SUB], axis=0, keepdims=True)) for s in range(GLA_SUB, GC, GLA_SUB)]
            db_scr[:, cs] = db
            for r, val in placed:
                db_scr[r:r + 1, cs] += val
            dp_ref[:, cs] = (dq * (GLA_K ** -0.5)).astype(BF16)
            dp_ref[:, GLA_KW + h * GLA_K:GLA_KW + (h + 1) * GLA_K] = dk.astype(BF16)
            dp_ref[:, o_gv + h * GLA_V:o_gv + (h + 1) * GLA_V] = dv.astype(BF16)
        dla = _exact_pm(pt_ref[...], db_scr[...])
        row = (nc - 1 - n) * GC + lax.broadcasted_iota(jnp.int32, (GC, 1), 0)
        dz = jnp.where(row >= PADF, dla * (1.0 / GATE_TAU) * _sigmoid(-z), 0.0)
        dzb = dz.astype(BF16)
        dp_ref[:, o_lr:] = _dot(dzb, wg_ref[...], NT).astype(BF16)
        dwg_ref[...] += _dot(glr_ref[...].astype(BF16), dzb, TN)
        dbg_ref[...] += jnp.sum(dz, axis=0, keepdims=True)

    rev = lambda n: (nc - 1 - n, 0)
    const = lambda n: (0, 0)
    xc_shapes, xc_sems = (list(comm.out_shapes), _comm_sems(comm)) if n_xc else ([], [])
    return pl.pallas_call(
        body, name="gla_bwd", grid=(nc,),
        in_specs=[pl.BlockSpec((GC, 2 * GLA_KW), rev),
                  pl.BlockSpec((GC, GLA_W), rev),
                  pl.BlockSpec((GC, 128), rev),
                  pl.BlockSpec((GC, GLA_W), rev),
                  pl.BlockSpec((GC, GLA_W), rev),
                  pl.BlockSpec((GC, GLA_W), rev),
                  pl.BlockSpec((1, GLA_HEADS, GLA_V, GLA_K), lambda n: (nc - 1 - n, 0, 0, 0)),
                  pl.BlockSpec((128, GLA_KW), const),
                  pl.BlockSpec((1, GLA_KW), const),
                  pl.BlockSpec((1, GLA_W), const),
                  pl.BlockSpec((GC, GC), const),
                  pl.BlockSpec((GC, GC), const)] + [ANY] * n_xc,
        out_specs=[pl.BlockSpec((GC, W_GP), rev), pl.BlockSpec((128, GLA_KW), const),
                   pl.BlockSpec((1, GLA_KW), const), pl.BlockSpec((1, GLA_W), const)] + [ANY] * n_xc,
        out_shape=[jax.ShapeDtypeStruct((tp, W_GP), BF16), jax.ShapeDtypeStruct((128, GLA_KW), F32),
                   jax.ShapeDtypeStruct((1, GLA_KW), F32), jax.ShapeDtypeStruct((1, GLA_W), F32)] + xc_shapes,
        scratch_shapes=[pltpu.VMEM((GLA_HEADS, GLA_V, GLA_K), F32), pltpu.VMEM((GC, GLA_KW), F32),
                        pltpu.VMEM((GC, GLA_KW), F32)] + xc_sems,
        compiler_params=_cparams(1),
    )(gqk, gv, glr, gg, o_gla, da, states, wg, bg, gain, pmat, pmat_t, *(comm.srcs if comm else ()))


def _mid_call(a_ret, a_gla, mg, h0, tgt, wbr, wbg, wout, gf):
    tp = h0.shape[0]
    nt = tp // TM

    def body(ar_ref, ag_ref, mg_ref, h_ref, t_ref, wbr_ref, wbg_ref, wo_ref, gf_ref,
             dh1_ref, dag_ref, dm_ref, mb_ref, dh1b_ref, dprb_ref, dpgb_ref, loss_ref, dgf_ref):
        i = pl.program_id(0)

        @pl.when(i == 0)
        def _():
            loss_ref[...] = jnp.zeros_like(loss_ref)
            dgf_ref[...] = jnp.zeros_like(dgf_ref)

        ar, ag = ar_ref[...], ag_ref[...]
        pr = _dot(ar, wbr_ref[...], NN)
        pg = _dot(ag, wbg_ref[...], NN)
        sr = _sigmoid(mg_ref[:, :D_MODEL])
        sg = _sigmoid(mg_ref[:, D_MODEL:])
        merged = (sr * pr + sg * pg).astype(BF16)
        mb_ref[...] = merged
        h1 = h_ref[...] + _dot(merged, wo_ref[...], NN)
        r1 = lax.rsqrt(jnp.mean(h1 * h1, axis=-1, keepdims=True) + EPS)
        xh = h1 * r1
        gfv = gf_ref[...]
        live = jnp.where(i > 0, 1.0, 0.0).astype(F32)
        err = (xh * gfv - t_ref[...]) * live
        loss_ref[...] += jnp.full(loss_ref.shape, 0.5 / D_MODEL, F32) * jnp.sum(err * err)
        dy = err * (1.0 / D_MODEL)
        dgf_ref[...] += jnp.sum(dy * xh, axis=0, keepdims=True)
        dxh = dy * gfv
        dh1 = r1 * (dxh - xh * jnp.mean(dxh * xh, axis=-1, keepdims=True))
        dh1_ref[...] = dh1
        dh1b = dh1.astype(BF16)
        dh1b_ref[...] = dh1b
        dmerged = _dot(dh1b, wo_ref[...], NT)
        dm_ref[:, :D_MODEL] = (dmerged * pr * sr * (1.0 - sr)).astype(BF16)
        dm_ref[:, D_MODEL:] = (dmerged * pg * sg * (1.0 - sg)).astype(BF16)
        dpr = (dmerged * sr).astype(BF16)
        dpg = (dmerged * sg).astype(BF16)
        dprb_ref[...] = dpr
        dpgb_ref[...] = dpg
        dag_ref[...] = _dot(dpg, wbg_ref[...], NT)

    tile = lambda w: pl.BlockSpec((TM, w), lambda i: (i, 0))
    const = lambda r, w: pl.BlockSpec((r, w), lambda i: (0, 0))
    return pl.pallas_call(
        body, name="merge_out_loss", grid=(nt,),
        in_specs=[tile(RET_W), tile(GLA_W), tile(W_M), tile(D_MODEL),
                  pl.BlockSpec((TM, D_MODEL), lambda i: (jnp.maximum(i - 1, 0), 0)),
                  const(RET_W, D_MODEL), const(GLA_W, D_MODEL), const(D_MODEL, D_MODEL), const(1, D_MODEL)],
        out_specs=[tile(D_MODEL), tile(GLA_W), tile(W_M), tile(D_MODEL), tile(D_MODEL), tile(D_MODEL),
                   tile(D_MODEL), const(1, 128), const(1, D_MODEL)],
        out_shape=[jax.ShapeDtypeStruct((tp, D_MODEL), F32), jax.ShapeDtypeStruct((tp, GLA_W), F32),
                   jax.ShapeDtypeStruct((tp, W_M), BF16),
                   jax.ShapeDtypeStruct((tp, D_MODEL), BF16), jax.ShapeDtypeStruct((tp, D_MODEL), BF16),
                   jax.ShapeDtypeStruct((tp, D_MODEL), BF16), jax.ShapeDtypeStruct((tp, D_MODEL), BF16),
                   jax.ShapeDtypeStruct((1, 128), F32), jax.ShapeDtypeStruct((1, D_MODEL), F32)],
        compiler_params=_cparams(1),
    )(a_ret, a_gla, mg, h0, tgt, wbr, wbg, wout, gf)


def _device_step(x2d, tgt2d, meta, norm_gain, w_in_bf, w_gate_up, b_gate, ret_gain, gla_gain, branch_parts,
                 final_gain, ck):
    seq = x2d.shape[0]
    tp = T0 + seq
    head = jnp.concatenate([jnp.zeros((PADF, D_MODEL), F32), meta], axis=0)
    w_r = w_in_bf
    w_g = jnp.pad(w_in_bf[:, W_R:W_R + W_G], ((0, 0), (0, W_GP - W_G)))
    w_m = w_in_bf[:, W_R + W_G:]
    wg_pad = jnp.pad(w_gate_up, ((0, 128 - GATE_RANK), (0, 0))).astype(BF16)

    pos = jnp.arange(tp, dtype=F32) - PADF
    half = RET_QK // 2
    inv = ROPE_BASE ** (-jnp.arange(half, dtype=F32) / half)
    ang = pos[:, None] * inv[None, :]
    cos, sin = jnp.cos(ang), jnp.sin(ang)
    lgam = jnp.log1p(-(2.0 ** (-5.0 - jnp.arange(RET_HEADS, dtype=F32))))
    pmat = jnp.asarray(_gla_tril(), BF16)
    pmat_t = jnp.asarray(_gla_tril().T.copy(), BF16)

    h0, u = _rms_call(x2d, head, norm_gain)
    tab = pl.BlockSpec((_proj_rows(tp), half), lambda j, i: (i, 0))
    rqk = _mm_nn("proj_rqk", u, w_r, BF16, D_MODEL, 0, 2 * D_MODEL, _rope_epilogue, (cos, sin), (tab, tab))
    rv = _mm_nn("proj_rv", u, w_r, BF16, D_MODEL, 2 * D_MODEL, RET_W)
    rg = _mm_nn("proj_rg", u, w_r, F32, D_MODEL, 4 * D_MODEL, RET_W)
    gqk = _mm_nn("proj_gqk", u, w_g, F32, 2 * GLA_KW, 0, 2 * GLA_KW, _gqk_epilogue)
    gv = _mm_nn("proj_gv", u, w_g, BF16, GLA_W, 2 * GLA_KW, GLA_W)
    gg = _mm_nn("proj_gg", u, w_g, F32, GLA_W, 2 * GLA_KW + GLA_W, GLA_W)
    glr = _mm_nn("proj_glr", u, w_g, F32, 128, 2 * GLA_KW + 2 * GLA_W, 128)
    mg = _mm_nn("proj_mg", u, w_m, F32, D_MODEL, 0, W_M)

    o_ret, a_ret, st_ret = _ret_fwd_call(rqk, rv, rg, ret_gain, lgam)
    o_gla, a_gla, st_gla, g_br, g_bg, g_out = _gla_fwd_call(gqk, gv, glr, gg, wg_pad, b_gate, gla_gain, pmat,
                                                            comm=_spread_plan(branch_parts))
    wbr = g_br.reshape(RET_W, D_MODEL)
    wbg = g_bg.reshape(GLA_W, D_MODEL)
    wout = g_out.reshape(D_MODEL, D_MODEL)

    gf = final_gain.reshape(1, D_MODEL)
    (dh1, da_gla, dm, merged_b, dh1_b, dpr_b, dpg_b, loss, dgf) = _mid_call(
        a_ret, a_gla, mg, h0, tgt2d, wbr, wbg, wout, gf)

    names_b = ("w_branch_ret", "w_branch_gla", "w_out")
    g2_b = [_mm_tn("dw_br", a_ret, dpr_b, D_MODEL).reshape(4, 2, RET_W // 8, D_MODEL).transpose(1, 0, 2, 3),
            _mm_tn("dw_bg", a_gla, dpg_b, D_MODEL).reshape(4, 2, GLA_W // 8, D_MODEL).transpose(1, 0, 2, 3),
            _mm_tn("dw_out", merged_b, dh1_b, D_MODEL).reshape(4, 2, D_MODEL // 8, D_MODEL).transpose(1, 0, 2, 3)]
    sib_b = _swap_halves_call("swap_halves_branch", g2_b)
    sum_b = [_add_half_call("add_half_" + nm, g, b, ck) for nm, g, b in zip(names_b, g2_b, sib_b)]
    d_g, dwg, dbg, dgla_gain, *chips_b = _gla_bwd_call(gqk, gv, glr, gg, o_gla, da_gla, st_gla, wg_pad, b_gate,
                                                       gla_gain, pmat, pmat_t, comm=_exchange_plan(sum_b))
    mine = [_add_chips_call("add_chips_" + nm, g, b, p, ck) for nm, g, b, p in zip(names_b, g2_b, sib_b, chips_b)]

    d_r, dret_gain = _ret_bwd_call(rqk, rv, rg, o_ret, dpr_b, wbr, st_ret, ret_gain, lgam, cos, sin)

    dwp = _mm_tn("dw_r", u, d_r, 2 * D_MODEL, out_cols=IN_PAD)
    dwp = _mm_tn("dw_g", u, d_g, D_MODEL, ncols=W_GP - 128, into=dwp, col0=W_R)
    dwp = _mm_tn("dw_glr", u, d_g, 128, ncols=128, bcol0=W_GP - 128, into=dwp, col0=W_R + W_GP - 128)
    g2_in = _place_merge_cols_call(dwp, _mm_tn("dw_m", u, dm, 2 * D_MODEL)).reshape(2, D_MODEL // 2, IN_PAD)

    du, sib_in = _mm_nt_acc("du_g", d_g, w_g, W_GP, comm=_swap_plan([g2_in]))
    sum_in = _add_rows_call("add_half_w_in", g2_in, sib_in, ck)
    du, chips_in = _mm_nt_acc("du_r", d_r, w_r, 2 * D_MODEL, acc_in=du, comm=_exchange_window_plan(sum_in))
    tile = pl.BlockSpec((TB, D_MODEL), lambda i, kk: (i, 0))
    row = pl.BlockSpec((1, D_MODEL), lambda i, kk: (0, 0))
    dx, dmeta, dnorm_gain = _mm_nt_acc(
        "du_m", dm, w_m, W_M, acc_in=du, epilogue=_rms_bwd_epilogue, extras=(h0, norm_gain, dh1),
        extra_specs=(tile, row, tile),
        extra_out_shapes=(jax.ShapeDtypeStruct((seq, D_MODEL), F32), jax.ShapeDtypeStruct((N_META, D_MODEL), F32),
                          jax.ShapeDtypeStruct((1, D_MODEL), F32)),
        extra_out_specs=(ANY, pl.BlockSpec((N_META, D_MODEL), lambda i, kk: (0, 0)), row),
        extra_scratch=(pltpu.VMEM((2, TB, D_MODEL), F32), pltpu.SemaphoreType.DMA((2,))))
    mine = [_add_window_call("add_chips_w_in", g2_in, sib_in, chips_in, ck)] + mine
    full = _join_halves_call("join_halves", mine)

    return dict(loss=loss[0, 0], dx=dx, dmeta=dmeta, norm_gain=dnorm_gain, w_gate_up=dwg[:GATE_RANK], b_gate=dbg,
                ret_norm_gain=dret_gain, gla_norm_gain=dgla_gain, final_norm_gain=dgf.reshape(D_MODEL),
                w_in=full[0], w_branch_ret=full[1], w_branch_gla=full[2], w_out=full[3])


MESH = pl.DeviceIdType.MESH
ANY = pl.BlockSpec(memory_space=pl.ANY)


def _place():
    return lax.axis_index("x"), lax.axis_index("y"), lax.axis_index("c")


def _gather8_call(name, parts, relay=()):
    n = len(parts)
    relay = tuple(relay) + (False,) * (n - len(relay))

    def body(*refs):
        x_refs, out_refs = refs[:n], refs[n:2 * n]
        send_sems, recv_sems, local_sems = refs[2 * n:]
        x, y, c = _place()
        me, sibling = (x, y, c), (x, y, 1 - c)
        xn, yn, dg = (1 - x, y), (x, 1 - y), (1 - x, 1 - y)

        def slot(t, px, py, pc, half=None):
            ref = out_refs[t].at[4 * px + 2 * py + pc]
            if half is None:
                return ref
            rows = ref.shape[0] // 2
            return ref.at[pl.ds(half * rows, rows)]

        def copy(t, k, dst, to, src=None):
            return pltpu.make_async_remote_copy(
                src_ref=dst if src is None else src, dst_ref=dst, send_sem=send_sems.at[8 * t + k],
                recv_sem=recv_sems.at[8 * t + k], device_id=to, device_id_type=MESH)

        mine = [pltpu.make_async_copy(x_refs[t], slot(t, *me), local_sems.at[t]) for t in range(n)]
        for cp in mine:
            cp.start()
        sent = []
        for t in range(n):
            sent.append(copy(t, 0, slot(t, *me), sibling, src=x_refs[t]))
            sent.append(copy(t, 1, slot(t, *me), (*xn, c), src=x_refs[t]))
            sent.append(copy(t, 2, slot(t, *me), (*yn, c), src=x_refs[t]))
            if not relay[t]:
                sent.append(copy(t, 3, slot(t, *me), (*dg, c), src=x_refs[t]))
        for cp in sent:
            cp.start()

        def start(cp):
            cp.start()
            sent.append(cp)

        for t in range(n):
            copy(t, 2, slot(t, *yn, c), me).wait_recv()
            if relay[t]:
                start(copy(t, 3, slot(t, *yn, c, half=0), (*xn, c)))
            start(copy(t, 6, slot(t, *yn, c), sibling))
        for t in range(n):
            copy(t, 1, slot(t, *xn, c), me).wait_recv()
            if relay[t]:
                start(copy(t, 4, slot(t, *xn, c, half=1), (*yn, c)))
            start(copy(t, 5, slot(t, *xn, c), sibling))
        for t in range(n):
            if relay[t]:
                copy(t, 3, slot(t, *dg, c, half=0), me).wait_recv()
                copy(t, 4, slot(t, *dg, c, half=1), me).wait_recv()
            else:
                copy(t, 3, slot(t, *dg, c), me).wait_recv()
            start(copy(t, 7, slot(t, *dg, c), sibling))
        for t in range(n):
            copy(t, 0, slot(t, *sibling), me).wait_recv()
            copy(t, 5, slot(t, *xn, 1 - c), me).wait_recv()
            copy(t, 6, slot(t, *yn, 1 - c), me).wait_recv()
            copy(t, 7, slot(t, *dg, 1 - c), me).wait_recv()
        for cp in sent:
            cp.wait_send()
        for cp in mine:
            cp.wait()

    return pl.pallas_call(
        body, name=name,
        out_shape=[jax.ShapeDtypeStruct((8,) + p.shape, p.dtype) for p in parts],
        in_specs=[ANY] * n, out_specs=[ANY] * n,
        scratch_shapes=[pltpu.SemaphoreType.DMA((8 * n,)), pltpu.SemaphoreType.DMA((8 * n,)),
                        pltpu.SemaphoreType.DMA((n,))],
    )(*parts)


def _swap_halves_call(name, gs):
    n = len(gs)

    def body(*refs):
        g_refs, b_refs = refs[:n], refs[n:2 * n]
        send_sems, recv_sems = refs[2 * n:]
        x, y, c = _place()
        copies = [pltpu.make_async_remote_copy(
            src_ref=g_refs[t].at[1 - c], dst_ref=b_refs[t], send_sem=send_sems.at[t], recv_sem=recv_sems.at[t],
            device_id=(x, y, 1 - c), device_id_type=MESH) for t in range(n)]
        for cp in copies:
            cp.start()
        for cp in copies:
            cp.wait()

    return pl.pallas_call(
        body, name=name,
        out_shape=[jax.ShapeDtypeStruct(g.shape[1:], g.dtype) for g in gs],
        in_specs=[ANY] * n, out_specs=[ANY] * n,
        scratch_shapes=[pltpu.SemaphoreType.DMA((n,)), pltpu.SemaphoreType.DMA((n,))],
    )(*gs)


def _join_halves_call(name, ts):
    n = len(ts)

    def body(*refs):
        o_refs = refs[n:2 * n]
        send_sems, recv_sems = refs[2 * n:]
        x, y, c = _place()
        copies = [pltpu.make_async_remote_copy(
            src_ref=o_refs[t].at[c], dst_ref=o_refs[t].at[c], send_sem=send_sems.at[t], recv_sem=recv_sems.at[t],
            device_id=(x, y, 1 - c), device_id_type=MESH) for t in range(n)]
        for cp in copies:
            cp.start()
        for t in range(n):
            copies[t].wait_send()
            pltpu.make_async_remote_copy(
                src_ref=o_refs[t].at[c], dst_ref=o_refs[t].at[1 - c], send_sem=send_sems.at[t],
                recv_sem=recv_sems.at[t], device_id=(x, y, 1 - c), device_id_type=MESH).wait_recv()

    return pl.pallas_call(
        body, name=name,
        out_shape=[jax.ShapeDtypeStruct(t.shape, t.dtype) for t in ts],
        in_specs=[ANY] * n, out_specs=[ANY] * n, input_output_aliases={t: t for t in range(n)},
        scratch_shapes=[pltpu.SemaphoreType.DMA((n,)), pltpu.SemaphoreType.DMA((n,))],
    )(*ts)


def _row_block(rows, cols, budget):
    best = 8
    for rb in range(8, rows + 1, 8):
        if rows % rb == 0 and rb * cols * 4 <= budget:
            best = rb
    return best


def _add_half_call(name, g, b, ck):
    _, _, r, cc = g.shape
    rb = _row_block(r, cc, 2 * 1024 * 1024)

    def body(ck_ref, g_ref, b_ref, o_ref):
        o_ref[...] = (g_ref[...] + b_ref[...]).astype(BF16)

    return pl.pallas_call(
        body, name=name,
        grid_spec=pltpu.PrefetchScalarGridSpec(
            num_scalar_prefetch=1, grid=(4, r // rb),
            in_specs=[pl.BlockSpec((None, None, rb, cc), lambda k, i, ck_ref: (ck_ref[0], k, i, 0)),
                      pl.BlockSpec((None, rb, cc), lambda k, i, ck_ref: (k, i, 0))],
            out_specs=pl.BlockSpec((None, rb, cc), lambda k, i, ck_ref: (k, i, 0))),
        out_shape=jax.ShapeDtypeStruct(b.shape, BF16),
        compiler_params=_cparams(2),
    )(ck, g, b)


def _add_rows_call(name, g, b, ck):
    _, r, cc = g.shape
    rb = _row_block(r, cc, 2 * 1024 * 1024)

    def body(ck_ref, g_ref, b_ref, o_ref):
        o_ref[...] = (g_ref[...] + b_ref[...]).astype(BF16)

    return pl.pallas_call(
        body, name=name,
        grid_spec=pltpu.PrefetchScalarGridSpec(
            num_scalar_prefetch=1, grid=(r // rb,),
            in_specs=[pl.BlockSpec((None, rb, cc), lambda i, ck_ref: (ck_ref[0], i, 0)),
                      pl.BlockSpec((rb, cc), lambda i, ck_ref: (i, 0))],
            out_specs=pl.BlockSpec((rb, cc), lambda i, ck_ref: (i, 0))),
        out_shape=jax.ShapeDtypeStruct((r, cc), BF16),
        compiler_params=_cparams(1),
    )(ck, g, b)


def _add_window_call(name, g, b, p, ck):
    _, r, _ = g.shape
    nb, step = WIN_W // 128, WIN_STEP // 128

    def body(ck_ref, g_ref, b_ref, p0_ref, p1_ref, p2_ref, o_ref):
        own = g_ref[...] + b_ref[...]
        o_ref[...] = ((own + p0_ref[...].astype(F32)) + p1_ref[...].astype(F32)) + p2_ref[...].astype(F32)

    def peer(j):
        return pl.BlockSpec((None, r, 128), lambda i, ck_ref: (j, 0, i))

    return pl.pallas_call(
        body, name=name,
        grid_spec=pltpu.PrefetchScalarGridSpec(
            num_scalar_prefetch=1, grid=(nb,),
            in_specs=[pl.BlockSpec((None, r, 128), lambda i, ck_ref: (ck_ref[0], 0, step * ck_ref[1] + i)),
                      pl.BlockSpec((r, 128), lambda i, ck_ref: (0, step * ck_ref[1] + i)),
                      peer(0), peer(1), peer(2)],
            out_specs=pl.BlockSpec((None, r, 128), lambda i, ck_ref: (ck_ref[0], 0, i))),
        out_shape=jax.ShapeDtypeStruct((2, r, WIN_W), F32),
        compiler_params=_cparams(1),
    )(ck, g, b, p, p, p)


def _add_chips_call(name, g, b, p, ck):
    _, _, r, cc = g.shape
    rb = _row_block(r, cc, 2 * 1024 * 1024)

    def body(ck_ref, g_ref, b_ref, p0_ref, p1_ref, p2_ref, o_ref):
        own = g_ref[...] + b_ref[...]
        o_ref[...] = ((own + p0_ref[...].astype(F32)) + p1_ref[...].astype(F32)) + p2_ref[...].astype(F32)

    def peer(j):
        return pl.BlockSpec((None, rb, cc), lambda i, ck_ref: (j, i, 0))

    return pl.pallas_call(
        body, name=name,
        grid_spec=pltpu.PrefetchScalarGridSpec(
            num_scalar_prefetch=1, grid=(r // rb,),
            in_specs=[pl.BlockSpec((None, None, rb, cc), lambda i, ck_ref: (ck_ref[0], ck_ref[1], i, 0)),
                      pl.BlockSpec((None, rb, cc), lambda i, ck_ref: (ck_ref[1], i, 0)),
                      peer(0), peer(1), peer(2)],
            out_specs=pl.BlockSpec((None, rb, cc), lambda i, ck_ref: (ck_ref[0], i, 0))),
        out_shape=jax.ShapeDtypeStruct((2, r, cc), F32),
        compiler_params=_cparams(1),
    )(ck, g, b, p, p, p)


def _sum8_call(name, g):
    def body(g_ref, o_ref):
        acc = g_ref[0]
        for d in range(1, 8):
            acc = acc + g_ref[d]
        o_ref[...] = acc

    return pl.pallas_call(body, name=name, out_shape=jax.ShapeDtypeStruct(g.shape[1:], F32))(g)


def _adamw_call(name, w, g, m, v):
    r, cc = w.shape
    if r % 8 == 0 or r * cc * 4 <= 1024 * 1024:
        rb = _row_block(r, cc, 1024 * 1024) if r % 8 == 0 else r
        grid, spec = (r // rb,), pl.BlockSpec((rb, cc), lambda i: (i, 0))
    else:
        grid, spec = (cc // 128,), pl.BlockSpec((r, 128), lambda i: (0, i))

    def body(w_ref, g_ref, m_ref, v_ref, d_ref, m2_ref, v2_ref):
        gv = g_ref[...]
        m2 = ADAM_B1 * m_ref[...] + (1.0 - ADAM_B1) * gv
        v2 = ADAM_B2 * v_ref[...] + (1.0 - ADAM_B2) * (gv * gv)
        m_hat = m2 / (1.0 - ADAM_B1 ** ADAM_STEP)
        v_hat = v2 / (1.0 - ADAM_B2 ** ADAM_STEP)
        d_ref[...] = -ADAM_LR * (m_hat / (jnp.sqrt(v_hat) + ADAM_EPS) + ADAM_WD * w_ref[...])
        m2_ref[...] = m2
        v2_ref[...] = v2

    return pl.pallas_call(
        body, name=name, grid=grid, in_specs=[spec] * 4, out_specs=[spec] * 3,
        out_shape=[jax.ShapeDtypeStruct((r, cc), F32)] * 3, compiler_params=_cparams(1),
    )(w, g, m, v)


SMALL = (("norm_gain", D_MODEL), ("b_gate", GLA_KW), ("ret_norm_gain", RET_W), ("gla_norm_gain", GLA_W),
         ("final_norm_gain", D_MODEL), ("w_gate_up", GATE_RANK * GLA_KW), ("meta_tokens", N_META * D_MODEL),
         ("loss", 1))


def _pack_rows(vecs, rows):
    flat = jnp.concatenate([v.reshape(-1) for v in vecs])
    return jnp.pad(flat, (0, rows * 128 - flat.shape[0])).reshape(rows, 128)


def kernel(x, meta_tokens, norm_gain, w_in, w_gate_up, b_gate, ret_norm_gain, gla_norm_gain, w_branch_ret, w_branch_gla, w_out, final_norm_gain, loss_target, m_meta_tokens, m_norm_gain, m_w_in, m_w_gate_up, m_b_gate, m_ret_norm_gain, m_gla_norm_gain, m_w_branch_ret, m_w_branch_gla, m_w_out, m_final_norm_gain, v_meta_tokens, v_norm_gain, v_w_in, v_w_gate_up, v_b_gate, v_ret_norm_gain, v_gla_norm_gain, v_w_branch_ret, v_w_branch_gla, v_w_out, v_final_norm_gain):
    xi, yi, ci = _place()
    kme = 2 * xi + yi
    ck = jnp.stack([ci, kme]).astype(jnp.int32)
    sw_in = w_in.shape[2]

    def my_half(a, dtype):
        r, cc = a.shape
        return lax.dynamic_index_in_dim(a.reshape(2, r // 2, cc), ci, 0, keepdims=False).astype(dtype)

    g_in, g_meta, g_wg = _gather8_call(
        "gather_weights", [my_half(w_in[0], BF16), my_half(meta_tokens, F32), my_half(w_gate_up[0], F32)],
        relay=(True,))
    branch_parts = [my_half(w_branch_ret[0], BF16), my_half(w_branch_gla[0], BF16), my_half(w_out[0], BF16)]
    w_in_bf = g_in.reshape(4, 2, D_MODEL // 2, sw_in).transpose(1, 2, 0, 3).reshape(D_MODEL, 4 * sw_in)
    meta = g_meta.reshape(4, 2, N_META // 2, D_MODEL // 4).transpose(1, 2, 0, 3).reshape(N_META, D_MODEL)
    wg_full = g_wg.reshape(4, 2, GATE_RANK // 2, GLA_KW // 4).transpose(1, 2, 0, 3).reshape(GATE_RANK, GLA_KW)

    loc = _device_step(x[0], loss_target[0], meta, norm_gain, w_in_bf, wg_full, b_gate, ret_norm_gain, gla_norm_gain,
                       branch_parts, final_norm_gain, ck)
    names = ("w_in", "w_branch_ret", "w_branch_gla", "w_out")
    full = [loc[nm] for nm in names]
    big_w = dict(w_in=w_in[0], w_branch_ret=w_branch_ret[0], w_branch_gla=w_branch_gla[0], w_out=w_out[0])
    big_m = dict(w_in=m_w_in[0], w_branch_ret=m_w_branch_ret[0], w_branch_gla=m_w_branch_gla[0], w_out=m_w_out[0])
    big_v = dict(w_in=v_w_in[0], w_branch_ret=v_w_branch_ret[0], w_branch_gla=v_w_branch_gla[0], w_out=v_w_out[0])
    grads, deltas, new_m, new_v = {}, {}, {}, {}
    for nm, f in zip(names, full):
        shape = big_w[nm].shape
        if nm == "w_in":
            f = lax.dynamic_slice_in_dim(f, (sw_in - WIN_STEP) * kme, sw_in, axis=2)
        g = f.reshape(shape)
        if nm == "w_in":
            d, m2, v2 = (a.T for a in _adamw_call("adamw_" + nm, big_w[nm].T, g.T, big_m[nm].T, big_v[nm].T))
        else:
            d, m2, v2 = _adamw_call("adamw_" + nm, big_w[nm], g, big_m[nm], big_v[nm])
        grads[nm], deltas[nm], new_m[nm], new_v[nm] = (a.reshape((1,) + shape) for a in (g, d, m2, v2))

    small_g = dict(loc)
    small_g["meta_tokens"] = loc["dmeta"]
    n_small = sum(sz for _, sz in SMALL)
    rows = -(-n_small // 128 // 8) * 8
    (g_small,) = _gather8_call("gather_small_grads", [_pack_rows([small_g[nm] for nm, _ in SMALL], rows)])
    tot = _sum8_call("sum_small_grads", g_small).reshape(-1)
    off = 0
    sg = {}
    for nm, sz in SMALL:
        sg[nm] = tot[off:off + sz]
        off += sz
    loss = sg.pop("loss")[0]
    sg["w_gate_up"] = lax.dynamic_slice_in_dim(sg["w_gate_up"].reshape(GATE_RANK, GLA_KW), kme * (GLA_KW // 4),
                                               GLA_KW // 4, axis=1)
    sg["meta_tokens"] = lax.dynamic_slice_in_dim(sg["meta_tokens"].reshape(N_META, D_MODEL), kme * (D_MODEL // 4),
                                                 D_MODEL // 4, axis=1)
    small_w = dict(norm_gain=norm_gain, b_gate=b_gate, ret_norm_gain=ret_norm_gain, gla_norm_gain=gla_norm_gain,
                   final_norm_gain=final_norm_gain, w_gate_up=w_gate_up, meta_tokens=meta_tokens)
    small_m = dict(norm_gain=m_norm_gain, b_gate=m_b_gate, ret_norm_gain=m_ret_norm_gain,
                   gla_norm_gain=m_gla_norm_gain, final_norm_gain=m_final_norm_gain, w_gate_up=m_w_gate_up,
                   meta_tokens=m_meta_tokens)
    small_v = dict(norm_gain=v_norm_gain, b_gate=v_b_gate, ret_norm_gain=v_ret_norm_gain,
                   gla_norm_gain=v_gla_norm_gain, final_norm_gain=v_final_norm_gain, w_gate_up=v_w_gate_up,
                   meta_tokens=v_meta_tokens)
    for nm in small_w:
        shape = small_w[nm].shape
        as2d = lambda a: a.reshape((-1, shape[-1]))
        grads[nm] = sg[nm].reshape(shape)
        deltas[nm], new_m[nm], new_v[nm] = (a.reshape(shape) for a in _adamw_call(
            "adamw_" + nm, as2d(small_w[nm]), as2d(sg[nm]), as2d(small_m[nm]), as2d(small_v[nm])))

    out_order = ("meta_tokens", "norm_gain", "w_in", "w_gate_up", "b_gate", "ret_norm_gain", "gla_norm_gain",
                 "w_branch_ret", "w_branch_gla", "w_out", "final_norm_gain")
    dx = loc["dx"].reshape(x.shape)
    return (loss, dx, *[grads[nm] for nm in out_order], *[deltas[nm] for nm in out_order],
            *[new_m[nm] for nm in out_order], *[new_v[nm] for nm in out_order])
```

```python
import math
from typing import Callable, NamedTuple

import numpy as np
import jax
import jax.numpy as jnp
from jax import lax
from jax.experimental import pallas as pl
from jax.experimental.pallas import tpu as pltpu

F32 = jnp.float32
BF16 = jnp.bfloat16

D_MODEL = 1024
N_META = 16
EPS = 1e-6
ROPE_BASE = 10000.0
RET_HEADS, RET_QK, RET_V = 4, 256, 512
RET_W = RET_HEADS * RET_V
GLA_HEADS, GLA_K, GLA_V = 4, 128, 256
GLA_W = GLA_HEADS * GLA_V
GLA_KW = GLA_HEADS * GLA_K
GATE_RANK = 16
GATE_TAU = 16.0
GLA_SUB = 16

TM = 256
T0 = TM
PADF = T0 - N_META
GC = 128
TB = 768
TK = 768

W_R = 6144
W_G = 3088
W_GP = 3200
W_M = 2048
IN_COLS = W_R + W_G + W_M
WIN_STEP = (IN_COLS // 4) // 128 * 128
WIN_W = -(-(3 * (IN_COLS // 4 - WIN_STEP) + IN_COLS // 4) // 128) * 128
IN_PAD = 3 * WIN_STEP + WIN_W

ADAM_LR, ADAM_B1, ADAM_B2, ADAM_EPS, ADAM_WD, ADAM_STEP = 0.001, 0.9, 0.999, 1e-08, 0.01, 10

VMEM_LIMIT = 56 * 1024 * 1024

NN = ((1,), (0,))
NT = ((1,), (1,))
TN = ((0,), (0,))


def _dot(a, b, dims):
    return lax.dot_general(a, b, (dims, ((), ())), preferred_element_type=F32)


def _cparams(n_axes):
    return pltpu.CompilerParams(dimension_semantics=("arbitrary",) * n_axes, vmem_limit_bytes=VMEM_LIMIT)


def _sigmoid(x):
    return 0.5 * jnp.tanh(0.5 * x) + 0.5


def _split3(x):
    hi = x.astype(BF16)
    r1 = x - hi.astype(F32)
    mid = r1.astype(BF16)
    lo = (r1 - mid.astype(F32)).astype(BF16)
    return hi, mid, lo


def _exact_pm(p, x):
    hi, mid, lo = _split3(x)
    return _dot(p, hi, NN) + _dot(p, mid, NN) + _dot(p, lo, NN)


def _rms_call(x2d, head, gain, comm):
    tp = T0 + x2d.shape[0]
    nt = tp // TM
    n_xc = len(comm.srcs)

    def body(x_ref, hd_ref, g_ref, *rest):
        xc_src = rest[:n_xc]
        h_ref, u_ref = rest[n_xc:n_xc + 2]
        xc_dst = rest[n_xc + 2:2 * n_xc + 2]
        i = pl.program_id(0)
        begin, finish = comm.make(xc_src, xc_dst, rest[-2], rest[-1])
        pl.when(i == 0)(begin)
        h = jnp.where(i == 0, hd_ref[...], x_ref[...])
        h_ref[...] = h
        r = lax.rsqrt(jnp.mean(h * h, axis=-1, keepdims=True) + EPS)
        u_ref[...] = (h * r * g_ref[...]).astype(BF16)
        pl.when(i == nt - 1)(finish)

    tile = pl.BlockSpec((TM, D_MODEL), lambda i: (i, 0))
    return pl.pallas_call(
        body, name="rms_in", grid=(nt,),
        in_specs=[pl.BlockSpec((TM, D_MODEL), lambda i: (jnp.maximum(i - 1, 0), 0)),
                  pl.BlockSpec((T0, D_MODEL), lambda i: (0, 0)), pl.BlockSpec((1, D_MODEL), lambda i: (0, 0))]
        + [ANY] * n_xc,
        out_specs=[tile, tile] + [ANY] * n_xc,
        out_shape=[jax.ShapeDtypeStruct((tp, D_MODEL), F32), jax.ShapeDtypeStruct((tp, D_MODEL), BF16)]
        + list(comm.out_shapes),
        scratch_shapes=_comm_sems(comm), compiler_params=_cparams(1),
    )(x2d, head, gain, *comm.srcs)


PROJ_ROWS_MAX = 1408


def _proj_rows(m):
    return max(r for r in range(16, PROJ_ROWS_MAX + 1, 16) if m % r == 0)


def _mm_nn(name, a, b, out_dtype, tn, col0, ncols, epilogue=None, extras=(), extra_specs=()):
    m, k = a.shape
    nj, j0 = ncols // tn, col0 // tn
    tb = _proj_rows(m)

    def body(a_ref, b_ref, *rest):
        *ex, o_ref = rest
        acc = _dot(a_ref[...], b_ref[...], NN)
        if epilogue is None:
            o_ref[...] = acc.astype(out_dtype)
        else:
            epilogue(acc, o_ref, *ex)

    return pl.pallas_call(
        body, name=name, grid=(nj, m // tb),
        in_specs=[pl.BlockSpec((tb, k), lambda j, i: (i, 0)), pl.BlockSpec((k, tn), lambda j, i: (0, j0 + j))]
        + list(extra_specs),
        out_specs=pl.BlockSpec((tb, tn), lambda j, i: (i, j)),
        out_shape=jax.ShapeDtypeStruct((m, ncols), out_dtype),
        compiler_params=_cparams(2),
    )(a, b, *extras)


def _rope_epilogue(acc, o_ref, cos_ref, sin_ref):
    scale = jnp.where(pl.program_id(0) == 1, RET_QK ** -0.5, 1.0).astype(F32)
    cos, sin = cos_ref[...], sin_ref[...]
    half = RET_QK // 2
    for h in range(RET_HEADS):
        t1 = acc[:, h * RET_QK:h * RET_QK + half]
        t2 = acc[:, h * RET_QK + half:(h + 1) * RET_QK]
        o_ref[:, h * RET_QK:h * RET_QK + half] = ((t1 * cos - t2 * sin) * scale).astype(BF16)
        o_ref[:, h * RET_QK + half:(h + 1) * RET_QK] = ((t2 * cos + t1 * sin) * scale).astype(BF16)


def _gqk_epilogue(acc, o_ref):
    o_ref[:, :GLA_KW] = acc[:, :GLA_KW] * (GLA_K ** -0.5)
    o_ref[:, GLA_KW:] = acc[:, GLA_KW:]


class _Comm(NamedTuple):
    srcs: tuple
    out_shapes: tuple
    n_sems: int
    make: Callable


def _comm_sems(comm):
    return [pltpu.SemaphoreType.DMA((comm.n_sems,)), pltpu.SemaphoreType.DMA((comm.n_sems,))]


def _start_wait(copies):
    def begin():
        for cp in copies:
            cp.start()

    def finish():
        for cp in copies:
            cp.wait()

    return begin, finish


def _other_chips(x, y):
    return [(1 - x, y), (x, 1 - y), (1 - x, 1 - y)]


def _gather_plan(parts, relay=()):
    n = len(parts)
    relay = tuple(relay) + (False,) * (n - len(relay))

    def make(x_refs, out_refs, send_sems, recv_sems):
        x, y, c = _place()
        me, sibling = (x, y, c), (x, y, 1 - c)
        xn, yn, dg = (1 - x, y), (x, 1 - y), (1 - x, 1 - y)

        def slot(t, px, py, pc, half=None):
            ref = out_refs[t].at[4 * px + 2 * py + pc]
            if half is None:
                return ref
            rows = ref.shape[0] // 2
            return ref.at[pl.ds(half * rows, rows)]

        def copy(t, k, dst, to, src=None):
            return pltpu.make_async_remote_copy(
                src_ref=dst if src is None else src, dst_ref=dst, send_sem=send_sems.at[8 * t + k],
                recv_sem=recv_sems.at[8 * t + k], device_id=to, device_id_type=MESH)

        mine = [pltpu.make_async_copy(x_refs[t], slot(t, *me), send_sems.at[8 * n + t]) for t in range(n)]
        sent = []
        for t in range(n):
            sent.append(copy(t, 0, slot(t, *me), sibling, src=x_refs[t]))
            sent.append(copy(t, 1, slot(t, *me), (*xn, c), src=x_refs[t]))
            sent.append(copy(t, 2, slot(t, *me), (*yn, c), src=x_refs[t]))
            if not relay[t]:
                sent.append(copy(t, 3, slot(t, *me), (*dg, c), src=x_refs[t]))

        def begin():
            for cp in mine + sent:
                cp.start()

        def finish():
            later = []

            def start(cp):
                cp.start()
                later.append(cp)

            for t in range(n):
                copy(t, 2, slot(t, *yn, c), me).wait_recv()
                if relay[t]:
                    start(copy(t, 3, slot(t, *yn, c, half=0), (*xn, c)))
                start(copy(t, 6, slot(t, *yn, c), sibling))
            for t in range(n):
                copy(t, 1, slot(t, *xn, c), me).wait_recv()
                if relay[t]:
                    start(copy(t, 4, slot(t, *xn, c, half=1), (*yn, c)))
                start(copy(t, 5, slot(t, *xn, c), sibling))
            for t in range(n):
                if relay[t]:
                    copy(t, 3, slot(t, *dg, c, half=0), me).wait_recv()
                    copy(t, 4, slot(t, *dg, c, half=1), me).wait_recv()
                else:
                    copy(t, 3, slot(t, *dg, c), me).wait_recv()
                start(copy(t, 7, slot(t, *dg, c), sibling))
            for t in range(n):
                copy(t, 0, slot(t, *sibling), me).wait_recv()
                copy(t, 5, slot(t, *xn, 1 - c), me).wait_recv()
                copy(t, 6, slot(t, *yn, 1 - c), me).wait_recv()
                copy(t, 7, slot(t, *dg, 1 - c), me).wait_recv()
            for cp in sent + later:
                cp.wait_send()
            for cp in mine:
                cp.wait()

        return begin, finish

    return _Comm(tuple(parts), tuple(jax.ShapeDtypeStruct((8,) + p.shape, p.dtype) for p in parts), 9 * n, make)


def _exchange_plan(ss):
    def make(s_refs, b_refs, send_sems, recv_sems):
        x, y, c = _place()
        return _start_wait([pltpu.make_async_remote_copy(
            src_ref=s_refs[t].at[2 * chip[0] + chip[1]], dst_ref=b_refs[t].at[j], send_sem=send_sems.at[3 * t + j],
            recv_sem=recv_sems.at[3 * t + j], device_id=(*chip, c), device_id_type=MESH)
            for t in range(len(s_refs)) for j, chip in enumerate(_other_chips(x, y))])

    return _Comm(tuple(ss), tuple(jax.ShapeDtypeStruct((3,) + s.shape[1:], s.dtype) for s in ss), 3 * len(ss), make)


def _exchange_window_plan(s):
    def make(s_refs, b_refs, send_sems, recv_sems):
        x, y, c = _place()
        return _start_wait([pltpu.make_async_remote_copy(
            src_ref=s_refs[0].at[:, pl.ds(pl.multiple_of((2 * chip[0] + chip[1]) * WIN_STEP, 128), WIN_W)],
            dst_ref=b_refs[0].at[j], send_sem=send_sems.at[j], recv_sem=recv_sems.at[j], device_id=(*chip, c),
            device_id_type=MESH) for j, chip in enumerate(_other_chips(x, y))])

    return _Comm((s,), (jax.ShapeDtypeStruct((3, s.shape[0], WIN_W), s.dtype),), 3, make)


def _swap_plan(gs):
    def make(g_refs, b_refs, send_sems, recv_sems):
        x, y, c = _place()
        return _start_wait([pltpu.make_async_remote_copy(
            src_ref=g_refs[t].at[1 - c], dst_ref=b_refs[t], send_sem=send_sems.at[t], recv_sem=recv_sems.at[t],
            device_id=(x, y, 1 - c), device_id_type=MESH) for t in range(len(g_refs))])

    return _Comm(tuple(gs), tuple(jax.ShapeDtypeStruct(g.shape[1:], g.dtype) for g in gs), len(gs), make)


def _spread_plan(parts):
    def make(p_refs, o_refs, send_sems, recv_sems):
        x, y, c = _place()
        copies = []
        for t in range(len(p_refs)):
            mine = o_refs[t].at[4 * x + 2 * y + c]
            copies.append(pltpu.make_async_copy(p_refs[t], mine, send_sems.at[7 * len(p_refs) + t]))
            for r in range(1, 8):
                peer = (1 - x if r & 4 else x, 1 - y if r & 2 else y, 1 - c if r & 1 else c)
                copies.append(pltpu.make_async_remote_copy(
                    src_ref=p_refs[t], dst_ref=mine, send_sem=send_sems.at[7 * t + r - 1],
                    recv_sem=recv_sems.at[7 * t + r - 1], device_id=peer, device_id_type=MESH))
        return _start_wait(copies)

    return _Comm(tuple(parts), tuple(jax.ShapeDtypeStruct((8,) + p.shape, p.dtype) for p in parts), 8 * len(parts),
                 make)


def _mm_nt_acc(name, a, w, tk, acc_in=None, epilogue=None, extras=(), extra_specs=(), extra_out_shapes=(),
               extra_out_specs=(), extra_scratch=(), comm=None):
    m, k = a.shape
    n = w.shape[0]
    nk, ni = k // tk, m // TB
    has_acc = acc_in is not None
    n_xc = len(comm.srcs) if comm else 0
    n_es = len(extra_scratch)

    def body(*refs):
        a_ref, w_ref = refs[0], refs[1]
        pos = 2
        acc_ref = None
        if has_acc:
            acc_ref = refs[pos]
            pos += 1
        ex = refs[pos:pos + len(extras)]
        pos += len(extras)
        xc_src = refs[pos:pos + n_xc]
        pos += n_xc
        n_scr = 1 + n_es + (2 if n_xc else 0)
        outs = refs[pos:len(refs) - n_scr - n_xc]
        xc_dst = refs[len(refs) - n_scr - n_xc:len(refs) - n_scr]
        scr = refs[len(refs) - n_scr]
        es = refs[len(refs) - n_scr + 1:len(refs) - n_scr + 1 + n_es]
        i, kk = pl.program_id(0), pl.program_id(1)
        if n_xc:
            begin, finish = comm.make(xc_src, xc_dst, refs[-2], refs[-1])
            pl.when((i == 0) & (kk == 0))(begin)

        @pl.when(kk == 0)
        def _():
            scr[...] = acc_ref[...] if has_acc else jnp.zeros_like(scr)

        scr[...] += _dot(a_ref[...], w_ref[...], NT)

        @pl.when(kk == nk - 1)
        def _():
            if epilogue is None:
                outs[0][...] = scr[...]
            else:
                epilogue(scr[...], outs, i, ni, *ex, *es)

        if n_xc:
            pl.when((i == ni - 1) & (kk == nk - 1))(finish)

    in_specs = [pl.BlockSpec((TB, tk), lambda i, kk: (i, kk)), pl.BlockSpec((n, tk), lambda i, kk: (0, kk))]
    args = [a, w]
    if has_acc:
        in_specs.append(pl.BlockSpec((TB, n), lambda i, kk: (i, 0)))
        args.append(acc_in)
    in_specs += list(extra_specs) + [ANY] * n_xc
    args += list(extras) + (list(comm.srcs) if comm else [])
    if epilogue is None:
        out_shape = [jax.ShapeDtypeStruct((m, n), F32)]
        out_specs = [pl.BlockSpec((TB, n), lambda i, kk: (i, 0))]
    else:
        out_shape, out_specs = list(extra_out_shapes), list(extra_out_specs)
    scratch = [pltpu.VMEM((TB, n), F32)] + list(extra_scratch)
    if n_xc:
        out_shape += list(comm.out_shapes)
        out_specs += [ANY] * n_xc
        scratch += _comm_sems(comm)
    return pl.pallas_call(
        body, name=name, grid=(ni, nk), in_specs=in_specs, out_specs=out_specs, out_shape=out_shape,
        scratch_shapes=scratch, compiler_params=_cparams(2),
    )(*args)


def _rms_bwd_epilogue(du, outs, i, ni, h_ref, g_ref, dh1_ref, obuf, sems):
    dx_ref, dmeta_ref, dg_ref = outs
    h = h_ref[...]
    r = lax.rsqrt(jnp.mean(h * h, axis=-1, keepdims=True) + EPS)
    xh = h * r
    dxh = du * g_ref[...]
    dh0 = dh1_ref[...] + r * (dxh - xh * jnp.mean(dxh * xh, axis=-1, keepdims=True))

    def put(slot, tile):
        return pltpu.make_async_copy(obuf.at[slot], dx_ref.at[pl.ds(pl.multiple_of(tile * TB - T0, 8), TB)],
                                     sems.at[slot])

    @pl.when(i == 0)
    def _():
        dg_ref[...] = jnp.zeros_like(dg_ref)
        dmeta_ref[...] = dh0[PADF:T0, :]
        obuf[0] = dh0
        first = pltpu.make_async_copy(obuf.at[0, pl.ds(T0, TB - T0)], dx_ref.at[pl.ds(0, TB - T0)], sems.at[0])
        first.start()
        first.wait()

    @pl.when(i >= 1)
    def _():
        slot = i % 2

        @pl.when(i >= 3)
        def _():
            put(slot, i - 2).wait()

        obuf[slot] = dh0
        put(slot, i).start()

    dg_ref[...] += jnp.sum(du * xh, axis=0, keepdims=True)

    @pl.when(i == ni - 1)
    def _():
        for tile in (ni - 2, ni - 1):
            if tile >= 1:
                put(tile % 2, tile).wait()


def _mm_tn(name, a, b, bn, ncols=None, bcol0=0, into=None, col0=0, out_cols=None):
    t, m = a.shape
    n = ncols or b.shape[1]
    j0, bj0 = col0 // bn, bcol0 // bn

    def body(a_ref, b_ref, *rest):
        o_ref = rest[-1]

        @pl.when(pl.program_id(1) == 0)
        def _():
            o_ref[...] = jnp.zeros_like(o_ref)

        o_ref[...] += _dot(a_ref[...], b_ref[...], TN)

    in_specs = [pl.BlockSpec((TK, m), lambda j, kk: (kk, 0)), pl.BlockSpec((TK, bn), lambda j, kk: (kk, bj0 + j))]
    args = [a, b]
    aliases = {}
    if into is not None:
        in_specs.append(ANY)
        args.append(into)
        aliases = {2: 0}
        out_cols = into.shape[1]
    return pl.pallas_call(
        body, name=name, grid=(n // bn, t // TK), in_specs=in_specs,
        out_specs=pl.BlockSpec((m, bn), lambda j, kk: (0, j0 + j)),
        out_shape=jax.ShapeDtypeStruct((m, out_cols or n), F32), input_output_aliases=aliases,
        compiler_params=_cparams(2),
    )(*args)


def _place_merge_cols_call(dwp, dw_m):
    c0 = W_R + W_GP - 128
    tail = IN_PAD - c0
    rows = 256

    def body(m_ref, p_ref, o_ref, buf, low, sem):
        get = pltpu.make_async_copy(o_ref.at[:, pl.ds(c0, 128)], low, sem)
        get.start()
        get.wait()
        for r in range(0, D_MODEL, rows):
            buf[r:r + rows, :] = jnp.concatenate(
                [low[r:r + rows, :GATE_RANK], m_ref[r:r + rows, :],
                 jnp.zeros((rows, tail - GATE_RANK - W_M), F32)], axis=1)
        put = pltpu.make_async_copy(buf, o_ref.at[:, pl.ds(c0, tail)], sem)
        put.start()
        put.wait()

    return pl.pallas_call(
        body, name="place_merge_cols",
        in_specs=[pl.BlockSpec(memory_space=pltpu.VMEM), ANY], out_specs=ANY,
        out_shape=jax.ShapeDtypeStruct(dwp.shape, F32), input_output_aliases={1: 0},
        scratch_shapes=[pltpu.VMEM((D_MODEL, tail), F32), pltpu.VMEM((D_MODEL, 128), F32), pltpu.SemaphoreType.DMA],
        compiler_params=pltpu.CompilerParams(vmem_limit_bytes=VMEM_LIMIT),
    )(dw_m, dwp)


def _ret_fill_decay(lg_ref, dm_scr):
    c = TM
    ii = lax.broadcasted_iota(jnp.int32, (c, c), 0)
    jj = lax.broadcasted_iota(jnp.int32, (c, c), 1)
    rel = (ii - jj).astype(F32)
    for h in range(RET_HEADS):
        dm_scr[h] = jnp.where(rel >= 0, jnp.exp(jnp.maximum(rel, 0.0) * lg_ref[h]), 0.0)


def _ret_consts(lg, dm_ref):
    c = TM
    idx = lax.broadcasted_iota(jnp.int32, (c, 1), 0).astype(F32)
    xi = jnp.exp((idx + 1.0) * lg)
    zeta = jnp.exp((c - 1.0 - idx) * lg)
    gc = jnp.exp(jnp.full((1, 1), c, F32) * lg)
    return dm_ref[...], xi, zeta, gc


def _ret_fwd_call(rqk, rv, rg, gain, lgam):
    tp = rqk.shape[0]
    nc = tp // TM

    def body(lg_ref, qk_ref, v_ref, rg_ref, g_ref, o_ref, a_ref, st_ref, s_scr, dm_scr):
        @pl.when(pl.program_id(0) == 0)
        def _():
            s_scr[...] = jnp.zeros_like(s_scr)
            _ret_fill_decay(lg_ref, dm_scr)

        for h in range(RET_HEADS):
            dm, xi, zeta, gc = _ret_consts(lg_ref[h], dm_scr.at[h])
            q = qk_ref[:, h * RET_QK:(h + 1) * RET_QK]
            k = qk_ref[:, D_MODEL + h * RET_QK:D_MODEL + (h + 1) * RET_QK]
            v = v_ref[:, h * RET_V:(h + 1) * RET_V]
            sb = s_scr[h].astype(BF16)
            st_ref[0, h] = sb
            s = _dot(q, k, NT) * dm
            o = _dot(s.astype(BF16), v, NN) + xi * _dot(q, sb, NN)
            kz = (k.astype(F32) * zeta).astype(BF16)
            s_scr[h] = gc * s_scr[h] + _dot(kz, v, TN)
            o_ref[:, h * RET_V:(h + 1) * RET_V] = o
            mu = jnp.mean(o, axis=-1, keepdims=True)
            xc = o - mu
            xh = xc * lax.rsqrt(jnp.mean(xc * xc, axis=-1, keepdims=True) + EPS)
            g = rg_ref[:, h * RET_V:(h + 1) * RET_V]
            a_ref[:, h * RET_V:(h + 1) * RET_V] = (
                xh * g_ref[:, h * RET_V:(h + 1) * RET_V] * (g * _sigmoid(g))).astype(BF16)

    return pl.pallas_call(
        body, name="ret_fwd", grid=(nc,),
        in_specs=[pl.BlockSpec(memory_space=pltpu.SMEM),
                  pl.BlockSpec((TM, 2 * D_MODEL), lambda n: (n, 0)),
                  pl.BlockSpec((TM, RET_W), lambda n: (n, 0)),
                  pl.BlockSpec((TM, RET_W), lambda n: (n, 0)),
                  pl.BlockSpec((1, RET_W), lambda n: (0, 0))],
        out_specs=[pl.BlockSpec((TM, RET_W), lambda n: (n, 0)),
                   pl.BlockSpec((TM, RET_W), lambda n: (n, 0)),
                   pl.BlockSpec((1, RET_HEADS, RET_QK, RET_V), lambda n: (n, 0, 0, 0))],
        out_shape=[jax.ShapeDtypeStruct((tp, RET_W), F32), jax.ShapeDtypeStruct((tp, RET_W), BF16),
                   jax.ShapeDtypeStruct((nc, RET_HEADS, RET_QK, RET_V), BF16)],
        scratch_shapes=[pltpu.VMEM((RET_HEADS, RET_QK, RET_V), F32), pltpu.VMEM((RET_HEADS, TM, TM), F32)],
        compiler_params=_cparams(1),
    )(lgam, rqk, rv, rg, gain)


def _ret_bwd_call(rqk, rv, rg, o_ret, dpr, wbr, states, gain, lgam, cos, sin):
    tp = rqk.shape[0]
    nc = tp // TM
    half = RET_QK // 2

    def body(lg_ref, qk_ref, v_ref, rg_ref, o_ref, dpr_ref, wbr_ref, st_ref, g_ref, cos_ref, sin_ref, dp_ref, dg_ref,
             ds_scr, dm_scr):
        @pl.when(pl.program_id(0) == 0)
        def _():
            ds_scr[...] = jnp.zeros_like(ds_scr)
            dg_ref[...] = jnp.zeros_like(dg_ref)
            _ret_fill_decay(lg_ref, dm_scr)

        cos, sin = cos_ref[...], sin_ref[...]
        for h in range(RET_HEADS):
            hs = slice(h * RET_V, (h + 1) * RET_V)
            dm, xi, zeta, gc = _ret_consts(lg_ref[h], dm_scr.at[h])
            o = o_ref[:, hs]
            mu = jnp.mean(o, axis=-1, keepdims=True)
            xc = o - mu
            rstd = lax.rsqrt(jnp.mean(xc * xc, axis=-1, keepdims=True) + EPS)
            xh = xc * rstd
            gain_h = g_ref[:, hs]
            g = rg_ref[:, hs]
            sg = _sigmoid(g)
            silu = g * sg
            dah = _dot(dpr_ref[...], wbr_ref[hs, :], NT)
            dp_ref[:, 4 * D_MODEL + h * RET_V:4 * D_MODEL + (h + 1) * RET_V] = (
                dah * (xh * gain_h) * (sg * (1.0 + g * (1.0 - sg)))).astype(BF16)
            dn = dah * silu
            dg_ref[:, hs] += jnp.sum(dn * xh, axis=0, keepdims=True)
            dxh = dn * gain_h
            do = rstd * (dxh - jnp.mean(dxh, axis=-1, keepdims=True)
                         - xh * jnp.mean(dxh * xh, axis=-1, keepdims=True))
            dob = do.astype(BF16)
            q = qk_ref[:, h * RET_QK:(h + 1) * RET_QK]
            k = qk_ref[:, D_MODEL + h * RET_QK:D_MODEL + (h + 1) * RET_QK]
            v = v_ref[:, hs]
            sp = st_ref[0, h]
            ds = ds_scr[h]
            dsb = ds.astype(BF16)
            s = (_dot(q, k, NT) * dm).astype(BF16)
            dsc = (_dot(dob, v, NT) * dm).astype(BF16)
            dq = _dot(dsc, k, NN) + xi * _dot(dob, sp, NT)
            dk = _dot(dsc, q, TN) + zeta * _dot(v, dsb, NT)
            kz = (k.astype(F32) * zeta).astype(BF16)
            dv = _dot(s, dob, TN) + _dot(kz, dsb, NN)
            qx = (q.astype(F32) * xi).astype(BF16)
            ds_scr[h] = gc * ds + _dot(qx, dob, TN)
            dp_ref[:, 2 * D_MODEL + h * RET_V:2 * D_MODEL + (h + 1) * RET_V] = dv.astype(BF16)
            dk = dk * (RET_QK ** -0.5)
            for base, t in ((0, dq), (D_MODEL, dk)):
                t1, t2 = t[:, :half], t[:, half:]
                dp_ref[:, base + h * RET_QK:base + h * RET_QK + half] = (t1 * cos + t2 * sin).astype(BF16)
                dp_ref[:, base + h * RET_QK + half:base + (h + 1) * RET_QK] = (t2 * cos - t1 * sin).astype(BF16)

    rev = lambda n: (nc - 1 - n, 0)
    return pl.pallas_call(
        body, name="ret_bwd", grid=(nc,),
        in_specs=[pl.BlockSpec(memory_space=pltpu.SMEM),
                  pl.BlockSpec((TM, 2 * D_MODEL), rev),
                  pl.BlockSpec((TM, RET_W), rev),
                  pl.BlockSpec((TM, RET_W), rev),
                  pl.BlockSpec((TM, RET_W), rev),
                  pl.BlockSpec((TM, D_MODEL), rev),
                  pl.BlockSpec((RET_W, D_MODEL), lambda n: (0, 0)),
                  pl.BlockSpec((1, RET_HEADS, RET_QK, RET_V), lambda n: (nc - 1 - n, 0, 0, 0)),
                  pl.BlockSpec((1, RET_W), lambda n: (0, 0)),
                  pl.BlockSpec((TM, half), rev),
                  pl.BlockSpec((TM, half), rev)],
        out_specs=[pl.BlockSpec((TM, W_R), rev), pl.BlockSpec((1, RET_W), lambda n: (0, 0))],
        out_shape=[jax.ShapeDtypeStruct((tp, W_R), BF16), jax.ShapeDtypeStruct((1, RET_W), F32)],
        scratch_shapes=[pltpu.VMEM((RET_HEADS, RET_QK, RET_V), F32), pltpu.VMEM((RET_HEADS, TM, TM), F32)],
        compiler_params=_cparams(1),
    )(lgam, rqk, rv, rg, o_ret, dpr, wbr, states, gain, cos, sin)


GLA_LEVELS = tuple(GC >> (s + 1) for s in range(int(math.log2(GC // GLA_SUB))))
NLEV = len(GLA_LEVELS)


def _gla_tril():
    return np.tril(np.ones((GC, GC), np.float32))


def _gla_masks():
    ii = lax.broadcasted_iota(jnp.int32, (GC, GC), 0)
    jj = lax.broadcasted_iota(jnp.int32, (GC, GC), 1)
    masks = []
    for m in GLA_LEVELS:
        sh = int(math.log2(2 * m))
        masks.append(((ii >> sh) == (jj >> sh)) & ((ii & m) != 0) & ((jj & m) == 0))
    sh = int(math.log2(GLA_SUB))
    md = ((ii >> sh) == (jj >> sh)) & (jj <= ii)
    row = lax.broadcasted_iota(jnp.int32, (GC, 1), 0)
    second = [(row & m) != 0 for m in GLA_LEVELS]
    return masks, md, second


def _gla_log_decay(glr_ref, wg_ref, bg_ref):
    z = _dot(glr_ref[...].astype(BF16), wg_ref[...], NN) + bg_ref[...]
    la = (jnp.minimum(z, 0.0) - jnp.log1p(jnp.exp(-jnp.abs(z)))) * (1.0 / GATE_TAU)
    return z, la


def _gla_row_steps(b_ref, cs, rows, size):
    parts = [jnp.zeros((size, GLA_K), F32) if r is None else jnp.broadcast_to(b_ref[r:r + 1, cs], (size, GLA_K))
             for r in rows]
    return parts[0] if len(parts) == 1 else jnp.concatenate(parts, axis=0)


def _gla_factors(b_ref, h, second):
    cs = slice(h * GLA_K, (h + 1) * GLA_K)
    b = b_ref[:, cs]
    fq, fk = [], []
    for l, m in enumerate(GLA_LEVELS):
        d = b - _gla_row_steps(b_ref, cs, [s + m - 1 for s in range(0, GC, 2 * m)], 2 * m)
        f = jnp.exp(jnp.where(second[l], d, -d))
        fq.append(jnp.where(second[l], f, 0.0))
        fk.append(jnp.where(second[l], 0.0, f))
    dd = b - _gla_row_steps(b_ref, cs, [None] + [s - 1 for s in range(GLA_SUB, GC, GLA_SUB)], GLA_SUB)
    ed = jnp.exp(dd)
    edi = jnp.exp(-dd)
    eb = jnp.exp(b)
    bl = b_ref[GC - 1:GC, cs]
    ee = jnp.exp(bl - b)
    ebl = jnp.exp(bl)
    return fq, fk, ed, edi, eb, ee, ebl


def _gla_scores(q, k, fq, fk, ed, edi, masks, md):
    qt = [(q * f).astype(BF16) for f in fq]
    kt = [(k * f).astype(BF16) for f in fk]
    qd = (q * ed).astype(BF16)
    kd = (k * edi).astype(BF16)
    a = jnp.where(md, _dot(qd, kd, NT), 0.0)
    for l in range(NLEV):
        a = a + jnp.where(masks[l], _dot(qt[l], kt[l], NT), 0.0)
    return a, qt, kt, qd, kd


def _gla_fwd_call(gqk, gv, glr, gg, wg, bg, gain, pmat, comm=None):
    tp = gqk.shape[0]
    nc = tp // GC
    n_xc = len(comm.srcs) if comm else 0

    def body(qk_ref, v_ref, glr_ref, gg_ref, wg_ref, bg_ref, g_ref, p_ref, *rest):
        xc_src = rest[:n_xc]
        o_ref, a_ref, st_ref = rest[n_xc:n_xc + 3]
        xc_dst = rest[n_xc + 3:2 * n_xc + 3]
        s_scr, b_scr = rest[2 * n_xc + 3:2 * n_xc + 5]
        n = pl.program_id(0)
        if n_xc:
            begin, finish = comm.make(xc_src, xc_dst, rest[-2], rest[-1])
            pl.when(n == 0)(begin)
            pl.when(n == nc - 1)(finish)

        @pl.when(n == 0)
        def _():
            s_scr[...] = jnp.zeros_like(s_scr)

        _, la = _gla_log_decay(glr_ref, wg_ref, bg_ref)
        b_scr[...] = _exact_pm(p_ref[...], la)
        masks, md, second = _gla_masks()
        for h in range(GLA_HEADS):
            q = qk_ref[:, h * GLA_K:(h + 1) * GLA_K]
            k = qk_ref[:, GLA_KW + h * GLA_K:GLA_KW + (h + 1) * GLA_K]
            vs = slice(h * GLA_V, (h + 1) * GLA_V)
            v = v_ref[:, vs]
            fq, fk, ed, edi, eb, ee, ebl = _gla_factors(b_scr, h, second)
            a, *_ = _gla_scores(q, k, fq, fk, ed, edi, masks, md)
            sb = s_scr[h].astype(BF16)
            st_ref[0, h] = sb
            o = _dot(a.astype(BF16), v, NN) + _dot((q * eb).astype(BF16), sb, NT)
            s_scr[h] = s_scr[h] * ebl + _dot(v, (k * ee).astype(BF16), TN)
            o_ref[:, vs] = o
            xh = o * lax.rsqrt(jnp.mean(o * o, axis=-1, keepdims=True) + EPS)
            g = gg_ref[:, vs]
            a_ref[:, vs] = (xh * g_ref[:, vs] * (g * _sigmoid(g))).astype(BF16)

    return pl.pallas_call(
        body, name="gla_fwd", grid=(nc,),
        in_specs=[pl.BlockSpec((GC, 2 * GLA_KW), lambda n: (n, 0)),
                  pl.BlockSpec((GC, GLA_W), lambda n: (n, 0)),
                  pl.BlockSpec((GC, 128), lambda n: (n, 0)),
                  pl.BlockSpec((GC, GLA_W), lambda n: (n, 0)),
                  pl.BlockSpec((128, GLA_KW), lambda n: (0, 0)),
                  pl.BlockSpec((1, GLA_KW), lambda n: (0, 0)),
                  pl.BlockSpec((1, GLA_W), lambda n: (0, 0)),
                  pl.BlockSpec((GC, GC), lambda n: (0, 0))] + [ANY] * n_xc,
        out_specs=[pl.BlockSpec((GC, GLA_W), lambda n: (n, 0)),
                   pl.BlockSpec((GC, GLA_W), lambda n: (n, 0)),
                   pl.BlockSpec((1, GLA_HEADS, GLA_V, GLA_K), lambda n: (n, 0, 0, 0))] + [ANY] * n_xc,
        out_shape=[jax.ShapeDtypeStruct((tp, GLA_W), F32), jax.ShapeDtypeStruct((tp, GLA_W), BF16),
                   jax.ShapeDtypeStruct((nc, GLA_HEADS, GLA_V, GLA_K), BF16)] + (list(comm.out_shapes) if comm else []),
        scratch_shapes=[pltpu.VMEM((GLA_HEADS, GLA_V, GLA_K), F32), pltpu.VMEM((GC, GLA_KW), F32)]
        + (_comm_sems(comm) if comm else []),
        compiler_params=_cparams(1),
    )(gqk, gv, glr, gg, wg, bg, gain, pmat, *(comm.srcs if comm else ()))


def _gla_bwd_call(gqk, gv, glr, gg, o_gla, da, states, wg, bg, gain, pmat, pmat_t, comm=None):
    tp = gqk.shape[0]
    nc = tp // GC
    o_gv, o_gg, o_lr = 2 * GLA_KW, 2 * GLA_KW + GLA_W, 2 * GLA_KW + 2 * GLA_W
    n_xc = len(comm.srcs) if comm else 0

    def body(qk_ref, v_ref, glr_ref, gg_ref, o_ref, da_ref, st_ref, wg_ref, bg_ref, g_ref, p_ref, pt_ref, *rest):
        xc_src = rest[:n_xc]
        dp_ref, dwg_ref, dbg_ref, dg_ref = rest[n_xc:n_xc + 4]
        xc_dst = rest[n_xc + 4:2 * n_xc + 4]
        ds_scr, b_scr, db_scr = rest[2 * n_xc + 4:2 * n_xc + 7]
        n = pl.program_id(0)
        if n_xc:
            begin, finish = comm.make(xc_src, xc_dst, rest[-2], rest[-1])
            pl.when(n == 0)(begin)
            pl.when(n == nc - 1)(finish)

        @pl.when(n == 0)
        def _():
            ds_scr[...] = jnp.zeros_like(ds_scr)
            dwg_ref[...] = jnp.zeros_like(dwg_ref)
            dbg_ref[...] = jnp.zeros_like(dbg_ref)
            dg_ref[...] = jnp.zeros_like(dg_ref)

        z, la = _gla_log_decay(glr_ref, wg_ref, bg_ref)
        b_scr[...] = _exact_pm(p_ref[...], la)
        masks, md, second = _gla_masks()
        for h in range(GLA_HEADS):
            cs = slice(h * GLA_K, (h + 1) * GLA_K)
            vs = slice(h * GLA_V, (h + 1) * GLA_V)
            o = o_ref[:, vs]
            rstd = lax.rsqrt(jnp.mean(o * o, axis=-1, keepdims=True) + EPS)
            xh = o * rstd
            gain_h = g_ref[:, vs]
            g = gg_ref[:, vs]
            sg = _sigmoid(g)
            dah = da_ref[:, vs]
            dp_ref[:, o_gg + h * GLA_V:o_gg + (h + 1) * GLA_V] = (
                dah * (xh * gain_h) * (sg * (1.0 + g * (1.0 - sg)))).astype(BF16)
            dn = dah * (g * sg)
            dg_ref[:, vs] += jnp.sum(dn * xh, axis=0, keepdims=True)
            dxh = dn * gain_h
            do = rstd * (dxh - xh * jnp.mean(dxh * xh, axis=-1, keepdims=True))
            dob = do.astype(BF16)
            q = qk_ref[:, cs]
            k = qk_ref[:, GLA_KW + h * GLA_K:GLA_KW + (h + 1) * GLA_K]
            v = v_ref[:, vs]
            fq, fk, ed, edi, eb, ee, ebl = _gla_factors(b_scr, h, second)
            a, qt, kt, qd, kd = _gla_scores(q, k, fq, fk, ed, edi, masks, md)
            sp = st_ref[0, h]
            ds = ds_scr[h]
            dsb = ds.astype(BF16)
            q_in = q * eb
            k_end = k * ee
            da_s = _dot(dob, v, NT)
            dv = _dot(a.astype(BF16), dob, TN) + _dot(k_end.astype(BF16), dsb, NT)
            dq_in = _dot(dob, sp, NN)
            dk_end = _dot(v, dsb, NN)
            dbl = jnp.sum(sp.astype(F32) * ds, axis=0, keepdims=True) * ebl
            ds_scr[h] = ds * ebl + _dot(dob, q_in.astype(BF16), TN)
            dq = dq_in * eb
            dk = dk_end * ee
            de_end = dk_end * k_end
            db = dq_in * q_in - de_end
            placed = [(GC - 1, jnp.sum(de_end, axis=0, keepdims=True) + dbl)]
            for l, m in enumerate(GLA_LEVELS):
                dal = jnp.where(masks[l], da_s, 0.0).astype(BF16)
                dqt = _dot(dal, kt[l], NN)
                dkt = _dot(dal, qt[l], TN)
                dq = dq + dqt * fq[l]
                dk = dk + dkt * fk[l]
                gl = dqt * (q * fq[l]) - dkt * (k * fk[l])
                db = db + gl
                placed += [(s + m - 1, -jnp.sum(gl[s:s + 2 * m], axis=0, keepdims=True)) for s in range(0, GC, 2 * m)]
            dad = jnp.where(md, da_s, 0.0).astype(BF16)
            dqd = _dot(dad, kd, NN)
            dkd = _dot(dad, qd, TN)
            dq = dq + dqd * ed
            dk = dk + dkd * edi
            gd = dqd * (q * ed) - dkd * (k * edi)
            db = db + gd
            placed += [(s - 1, -jnp.sum(gd[s:s + GLA_SUB], axis=0, keepdims=True)) for s in range(GLA_SUB, GC, GLA_SUB)]
            db_scr[:, cs] = db
            for r, val in placed:
                db_scr[r:r + 1, cs] += val
            dp_ref[:, cs] = (dq * (GLA_K ** -0.5)).astype(BF16)
            dp_ref[:, GLA_KW + h * GLA_K:GLA_KW + (h + 1) * GLA_K] = dk.astype(BF16)
            dp_ref[:, o_gv + h * GLA_V:o_gv + (h + 1) * GLA_V] = dv.astype(BF16)
        dla = _exact_pm(pt_ref[...], db_scr[...])
        row = (nc - 1 - n) * GC + lax.broadcasted_iota(jnp.int32, (GC, 1), 0)
        dz = jnp.where(row >= PADF, dla * (1.0 / GATE_TAU) * _sigmoid(-z), 0.0)
        dzb = dz.astype(BF16)
        dp_ref[:, o_lr:] = _dot(dzb, wg_ref[...], NT).astype(BF16)
        dwg_ref[...] += _dot(glr_ref[...].astype(BF16), dzb, TN)
        dbg_ref[...] += jnp.sum(dz, axis=0, keepdims=True)

    rev = lambda n: (nc - 1 - n, 0)
    const = lambda n: (0, 0)
    xc_shapes, xc_sems = (list(comm.out_shapes), _comm_sems(comm)) if n_xc else ([], [])
    return pl.pallas_call(
        body, name="gla_bwd", grid=(nc,),
        in_specs=[pl.BlockSpec((GC, 2 * GLA_KW), rev),
                  pl.BlockSpec((GC, GLA_W), rev),
                  pl.BlockSpec((GC, 128), rev),
                  pl.BlockSpec((GC, GLA_W), rev),
                  pl.BlockSpec((GC, GLA_W), rev),
                  pl.BlockSpec((GC, GLA_W), rev),
                  pl.BlockSpec((1, GLA_HEADS, GLA_V, GLA_K), lambda n: (nc - 1 - n, 0, 0, 0)),
                  pl.BlockSpec((128, GLA_KW), const),
                  pl.BlockSpec((1, GLA_KW), const),
                  pl.BlockSpec((1, GLA_W), const),
                  pl.BlockSpec((GC, GC), const),
                  pl.BlockSpec((GC, GC), const)] + [ANY] * n_xc,
        out_specs=[pl.BlockSpec((GC, W_GP), rev), pl.BlockSpec((128, GLA_KW), const),
                   pl.BlockSpec((1, GLA_KW), const), pl.BlockSpec((1, GLA_W), const)] + [ANY] * n_xc,
        out_shape=[jax.ShapeDtypeStruct((tp, W_GP), BF16), jax.ShapeDtypeStruct((128, GLA_KW), F32),
                   jax.ShapeDtypeStruct((1, GLA_KW), F32), jax.ShapeDtypeStruct((1, GLA_W), F32)] + xc_shapes,
        scratch_shapes=[pltpu.VMEM((GLA_HEADS, GLA_V, GLA_K), F32), pltpu.VMEM((GC, GLA_KW), F32),
                        pltpu.VMEM((GC, GLA_KW), F32)] + xc_sems,
        compiler_params=_cparams(1),
    )(gqk, gv, glr, gg, o_gla, da, states, wg, bg, gain, pmat, pmat_t, *(comm.srcs if comm else ()))


def _mid_call(a_ret, a_gla, mg, h0, tgt, wbr, wbg, wout, gf):
    tp = h0.shape[0]
    nt = tp // TM

    def body(ar_ref, ag_ref, mg_ref, h_ref, t_ref, wbr_ref, wbg_ref, wo_ref, gf_ref,
             dh1_ref, dag_ref, dm_ref, mb_ref, dh1b_ref, dprb_ref, dpgb_ref, loss_ref, dgf_ref):
        i = pl.program_id(0)

        @pl.when(i == 0)
        def _():
            loss_ref[...] = jnp.zeros_like(loss_ref)
            dgf_ref[...] = jnp.zeros_like(dgf_ref)

        ar, ag = ar_ref[...], ag_ref[...]
        pr = _dot(ar, wbr_ref[...], NN)
        pg = _dot(ag, wbg_ref[...], NN)
        sr = _sigmoid(mg_ref[:, :D_MODEL])
        sg = _sigmoid(mg_ref[:, D_MODEL:])
        merged = (sr * pr + sg * pg).astype(BF16)
        mb_ref[...] = merged
        h1 = h_ref[...] + _dot(merged, wo_ref[...], NN)
        r1 = lax.rsqrt(jnp.mean(h1 * h1, axis=-1, keepdims=True) + EPS)
        xh = h1 * r1
        gfv = gf_ref[...]
        live = jnp.where(i > 0, 1.0, 0.0).astype(F32)
        err = (xh * gfv - t_ref[...]) * live
        loss_ref[...] += jnp.full(loss_ref.shape, 0.5 / D_MODEL, F32) * jnp.sum(err * err)
        dy = err * (1.0 / D_MODEL)
        dgf_ref[...] += jnp.sum(dy * xh, axis=0, keepdims=True)
        dxh = dy * gfv
        dh1 = r1 * (dxh - xh * jnp.mean(dxh * xh, axis=-1, keepdims=True))
        dh1_ref[...] = dh1
        dh1b = dh1.astype(BF16)
        dh1b_ref[...] = dh1b
        dmerged = _dot(dh1b, wo_ref[...], NT)
        dm_ref[:, :D_MODEL] = (dmerged * pr * sr * (1.0 - sr)).astype(BF16)
        dm_ref[:, D_MODEL:] = (dmerged * pg * sg * (1.0 - sg)).astype(BF16)
        dpr = (dmerged * sr).astype(BF16)
        dpg = (dmerged * sg).astype(BF16)
        dprb_ref[...] = dpr
        dpgb_ref[...] = dpg
        dag_ref[...] = _dot(dpg, wbg_ref[...], NT)

    tile = lambda w: pl.BlockSpec((TM, w), lambda i: (i, 0))
    const = lambda r, w: pl.BlockSpec((r, w), lambda i: (0, 0))
    return pl.pallas_call(
        body, name="merge_out_loss", grid=(nt,),
        in_specs=[tile(RET_W), tile(GLA_W), tile(W_M), tile(D_MODEL),
                  pl.BlockSpec((TM, D_MODEL), lambda i: (jnp.maximum(i - 1, 0), 0)),
                  const(RET_W, D_MODEL), const(GLA_W, D_MODEL), const(D_MODEL, D_MODEL), const(1, D_MODEL)],
        out_specs=[tile(D_MODEL), tile(GLA_W), tile(W_M), tile(D_MODEL), tile(D_MODEL), tile(D_MODEL),
                   tile(D_MODEL), const(1, 128), const(1, D_MODEL)],
        out_shape=[jax.ShapeDtypeStruct((tp, D_MODEL), F32), jax.ShapeDtypeStruct((tp, GLA_W), F32),
                   jax.ShapeDtypeStruct((tp, W_M), BF16),
                   jax.ShapeDtypeStruct((tp, D_MODEL), BF16), jax.ShapeDtypeStruct((tp, D_MODEL), BF16),
                   jax.ShapeDtypeStruct((tp, D_MODEL), BF16), jax.ShapeDtypeStruct((tp, D_MODEL), BF16),
                   jax.ShapeDtypeStruct((1, 128), F32), jax.ShapeDtypeStruct((1, D_MODEL), F32)],
        compiler_params=_cparams(1),
    )(a_ret, a_gla, mg, h0, tgt, wbr, wbg, wout, gf)


def _device_step(x2d, tgt2d, meta, norm_gain, w_in_part, w_gate_up, b_gate, ret_gain, gla_gain, branch_parts,
                 final_gain, ck):
    seq = x2d.shape[0]
    tp = T0 + seq
    head = jnp.concatenate([jnp.zeros((PADF, D_MODEL), F32), meta], axis=0)
    wg_pad = jnp.pad(w_gate_up, ((0, 128 - GATE_RANK), (0, 0))).astype(BF16)

    pos = jnp.arange(tp, dtype=F32) - PADF
    half = RET_QK // 2
    inv = ROPE_BASE ** (-jnp.arange(half, dtype=F32) / half)
    ang = pos[:, None] * inv[None, :]
    cos, sin = jnp.cos(ang), jnp.sin(ang)
    lgam = jnp.log1p(-(2.0 ** (-5.0 - jnp.arange(RET_HEADS, dtype=F32))))
    pmat = jnp.asarray(_gla_tril(), BF16)
    pmat_t = jnp.asarray(_gla_tril().T.copy(), BF16)

    h0, u, g_in = _rms_call(x2d, head, norm_gain, _gather_plan([w_in_part], relay=(True,)))
    hr, sw = w_in_part.shape
    w_in_bf = g_in.reshape(4, 2, hr, sw).transpose(1, 2, 0, 3).reshape(2 * hr, 4 * sw)
    w_r = w_in_bf
    w_g = jnp.pad(w_in_bf[:, W_R:W_R + W_G], ((0, 0), (0, W_GP - W_G)))
    w_m = w_in_bf[:, W_R + W_G:]
    tab = pl.BlockSpec((_proj_rows(tp), half), lambda j, i: (i, 0))
    rqk = _mm_nn("proj_rqk", u, w_r, BF16, D_MODEL, 0, 2 * D_MODEL, _rope_epilogue, (cos, sin), (tab, tab))
    rv = _mm_nn("proj_rv", u, w_r, BF16, RET_W, 2 * D_MODEL, RET_W)
    rg = _mm_nn("proj_rg", u, w_r, F32, RET_W, 4 * D_MODEL, RET_W)
    gqk = _mm_nn("proj_gqk", u, w_g, F32, 2 * GLA_KW, 0, 2 * GLA_KW, _gqk_epilogue)
    gv = _mm_nn("proj_gv", u, w_g, BF16, GLA_W, 2 * GLA_KW, GLA_W)
    gg = _mm_nn("proj_gg", u, w_g, F32, GLA_W, 2 * GLA_KW + GLA_W, GLA_W)
    glr = _mm_nn("proj_glr", u, w_g, F32, 128, 2 * GLA_KW + 2 * GLA_W, 128)
    mg = _mm_nn("proj_mg", u, w_m, F32, W_M, 0, W_M)

    o_ret, a_ret, st_ret = _ret_fwd_call(rqk, rv, rg, ret_gain, lgam)
    o_gla, a_gla, st_gla, g_br, g_bg, g_out = _gla_fwd_call(gqk, gv, glr, gg, wg_pad, b_gate, gla_gain, pmat,
                                                            comm=_spread_plan(branch_parts))
    wbr = g_br.reshape(RET_W, D_MODEL)
    wbg = g_bg.reshape(GLA_W, D_MODEL)
    wout = g_out.reshape(D_MODEL, D_MODEL)

    gf = final_gain.reshape(1, D_MODEL)
    (dh1, da_gla, dm, merged_b, dh1_b, dpr_b, dpg_b, loss, dgf) = _mid_call(
        a_ret, a_gla, mg, h0, tgt2d, wbr, wbg, wout, gf)

    names_b = ("w_branch_ret", "w_branch_gla", "w_out")
    g2_b = [_mm_tn("dw_br", a_ret, dpr_b, D_MODEL).reshape(4, 2, RET_W // 8, D_MODEL).transpose(1, 0, 2, 3),
            _mm_tn("dw_bg", a_gla, dpg_b, D_MODEL).reshape(4, 2, GLA_W // 8, D_MODEL).transpose(1, 0, 2, 3),
            _mm_tn("dw_out", merged_b, dh1_b, D_MODEL).reshape(4, 2, D_MODEL // 8, D_MODEL).transpose(1, 0, 2, 3)]
    sib_b = _swap_halves_call("swap_halves_branch", g2_b)
    sum_b = [_add_half_call("add_half_" + nm, g, b, ck) for nm, g, b in zip(names_b, g2_b, sib_b)]
    d_g, dwg, dbg, dgla_gain, *chips_b = _gla_bwd_call(gqk, gv, glr, gg, o_gla, da_gla, st_gla, wg_pad, b_gate,
                                                       gla_gain, pmat, pmat_t, comm=_exchange_plan(sum_b))
    mine = [_add_chips_call("add_chips_" + nm, g, b, p, ck) for nm, g, b, p in zip(names_b, g2_b, sib_b, chips_b)]

    d_r, dret_gain = _ret_bwd_call(rqk, rv, rg, o_ret, dpr_b, wbr, st_ret, ret_gain, lgam, cos, sin)

    dwp = _mm_tn("dw_r", u, d_r, 2 * D_MODEL, out_cols=IN_PAD)
    dwp = _mm_tn("dw_g", u, d_g, D_MODEL, ncols=W_GP - 128, into=dwp, col0=W_R)
    dwp = _mm_tn("dw_glr", u, d_g, 128, ncols=128, bcol0=W_GP - 128, into=dwp, col0=W_R + W_GP - 128)
    g2_in = _place_merge_cols_call(dwp, _mm_tn("dw_m", u, dm, 2 * D_MODEL)).reshape(2, D_MODEL // 2, IN_PAD)

    du, sib_in = _mm_nt_acc("du_g", d_g, w_g, W_GP, comm=_swap_plan([g2_in]))
    sum_in = _add_rows_call("add_half_w_in", g2_in, sib_in, ck)
    du, chips_in = _mm_nt_acc("du_r", d_r, w_r, 2 * D_MODEL, acc_in=du, comm=_exchange_window_plan(sum_in))
    tile = pl.BlockSpec((TB, D_MODEL), lambda i, kk: (i, 0))
    row = pl.BlockSpec((1, D_MODEL), lambda i, kk: (0, 0))
    dx, dmeta, dnorm_gain = _mm_nt_acc(
        "du_m", dm, w_m, W_M, acc_in=du, epilogue=_rms_bwd_epilogue, extras=(h0, norm_gain, dh1),
        extra_specs=(tile, row, tile),
        extra_out_shapes=(jax.ShapeDtypeStruct((seq, D_MODEL), F32), jax.ShapeDtypeStruct((N_META, D_MODEL), F32),
                          jax.ShapeDtypeStruct((1, D_MODEL), F32)),
        extra_out_specs=(ANY, pl.BlockSpec((N_META, D_MODEL), lambda i, kk: (0, 0)), row),
        extra_scratch=(pltpu.VMEM((2, TB, D_MODEL), F32), pltpu.SemaphoreType.DMA((2,))))
    mine = [_add_window_call("add_chips_w_in", g2_in, sib_in, chips_in, ck)] + mine
    full = _join_halves_call("join_halves", mine)

    return dict(loss=loss[0, 0], dx=dx, dmeta=dmeta, norm_gain=dnorm_gain, w_gate_up=dwg[:GATE_RANK], b_gate=dbg,
                ret_norm_gain=dret_gain, gla_norm_gain=dgla_gain, final_norm_gain=dgf.reshape(D_MODEL),
                w_in=full[0], w_branch_ret=full[1], w_branch_gla=full[2], w_out=full[3])


MESH = pl.DeviceIdType.MESH
ANY = pl.BlockSpec(memory_space=pl.ANY)


def _place():
    return lax.axis_index("x"), lax.axis_index("y"), lax.axis_index("c")


def _gather8_call(name, parts):
    comm = _gather_plan(parts)
    n = len(parts)

    def body(*refs):
        begin, finish = comm.make(refs[:n], refs[n:2 * n], refs[-2], refs[-1])
        begin()
        finish()

    return pl.pallas_call(
        body, name=name, out_shape=list(comm.out_shapes), in_specs=[ANY] * n, out_specs=[ANY] * n,
        scratch_shapes=_comm_sems(comm),
    )(*parts)


def _swap_halves_call(name, gs):
    n = len(gs)

    def body(*refs):
        g_refs, b_refs = refs[:n], refs[n:2 * n]
        send_sems, recv_sems = refs[2 * n:]
        x, y, c = _place()
        copies = [pltpu.make_async_remote_copy(
            src_ref=g_refs[t].at[1 - c], dst_ref=b_refs[t], send_sem=send_sems.at[t], recv_sem=recv_sems.at[t],
            device_id=(x, y, 1 - c), device_id_type=MESH) for t in range(n)]
        for cp in copies:
            cp.start()
        for cp in copies:
            cp.wait()

    return pl.pallas_call(
        body, name=name,
        out_shape=[jax.ShapeDtypeStruct(g.shape[1:], g.dtype) for g in gs],
        in_specs=[ANY] * n, out_specs=[ANY] * n,
        scratch_shapes=[pltpu.SemaphoreType.DMA((n,)), pltpu.SemaphoreType.DMA((n,))],
    )(*gs)


def _join_halves_call(name, ts):
    n = len(ts)

    def body(*refs):
        o_refs = refs[n:2 * n]
        send_sems, recv_sems = refs[2 * n:]
        x, y, c = _place()
        copies = [pltpu.make_async_remote_copy(
            src_ref=o_refs[t].at[c], dst_ref=o_refs[t].at[c], send_sem=send_sems.at[t], recv_sem=recv_sems.at[t],
            device_id=(x, y, 1 - c), device_id_type=MESH) for t in range(n)]
        for cp in copies:
            cp.start()
        for t in range(n):
            copies[t].wait_send()
            pltpu.make_async_remote_copy(
                src_ref=o_refs[t].at[c], dst_ref=o_refs[t].at[1 - c], send_sem=send_sems.at[t],
                recv_sem=recv_sems.at[t], device_id=(x, y, 1 - c), device_id_type=MESH).wait_recv()

    return pl.pallas_call(
        body, name=name,
        out_shape=[jax.ShapeDtypeStruct(t.shape, t.dtype) for t in ts],
        in_specs=[ANY] * n, out_specs=[ANY] * n, input_output_aliases={t: t for t in range(n)},
        scratch_shapes=[pltpu.SemaphoreType.DMA((n,)), pltpu.SemaphoreType.DMA((n,))],
    )(*ts)


def _row_block(rows, cols, budget):
    best = 8
    for rb in range(8, rows + 1, 8):
        if rows % rb == 0 and rb * cols * 4 <= budget:
            best = rb
    return best


def _add_half_call(name, g, b, ck):
    _, _, r, cc = g.shape
    rb = _row_block(r, cc, 2 * 1024 * 1024)

    def body(ck_ref, g_ref, b_ref, o_ref):
        o_ref[...] = (g_ref[...] + b_ref[...]).astype(BF16)

    return pl.pallas_call(
        body, name=name,
        grid_spec=pltpu.PrefetchScalarGridSpec(
            num_scalar_prefetch=1, grid=(4, r // rb),
            in_specs=[pl.BlockSpec((None, None, rb, cc), lambda k, i, ck_ref: (ck_ref[0], k, i, 0)),
                      pl.BlockSpec((None, rb, cc), lambda k, i, ck_ref: (k, i, 0))],
            out_specs=pl.BlockSpec((None, rb, cc), lambda k, i, ck_ref: (k, i, 0))),
        out_shape=jax.ShapeDtypeStruct(b.shape, BF16),
        compiler_params=_cparams(2),
    )(ck, g, b)


def _add_rows_call(name, g, b, ck):
    _, r, cc = g.shape
    rb = _row_block(r, cc, 2 * 1024 * 1024)

    def body(ck_ref, g_ref, b_ref, o_ref):
        o_ref[...] = (g_ref[...] + b_ref[...]).astype(BF16)

    return pl.pallas_call(
        body, name=name,
        grid_spec=pltpu.PrefetchScalarGridSpec(
            num_scalar_prefetch=1, grid=(r // rb,),
            in_specs=[pl.BlockSpec((None, rb, cc), lambda i, ck_ref: (ck_ref[0], i, 0)),
                      pl.BlockSpec((rb, cc), lambda i, ck_ref: (i, 0))],
            out_specs=pl.BlockSpec((rb, cc), lambda i, ck_ref: (i, 0))),
        out_shape=jax.ShapeDtypeStruct((r, cc), BF16),
        compiler_params=_cparams(1),
    )(ck, g, b)


def _add_window_call(name, g, b, p, ck):
    _, r, _ = g.shape
    nb, step = WIN_W // 128, WIN_STEP // 128

    def body(ck_ref, g_ref, b_ref, p0_ref, p1_ref, p2_ref, o_ref):
        own = g_ref[...] + b_ref[...]
        o_ref[...] = ((own + p0_ref[...].astype(F32)) + p1_ref[...].astype(F32)) + p2_ref[...].astype(F32)

    def peer(j):
        return pl.BlockSpec((None, r, 128), lambda i, ck_ref: (j, 0, i))

    return pl.pallas_call(
        body, name=name,
        grid_spec=pltpu.PrefetchScalarGridSpec(
            num_scalar_prefetch=1, grid=(nb,),
            in_specs=[pl.BlockSpec((None, r, 128), lambda i, ck_ref: (ck_ref[0], 0, step * ck_ref[1] + i)),
                      pl.BlockSpec((r, 128), lambda i, ck_ref: (0, step * ck_ref[1] + i)),
                      peer(0), peer(1), peer(2)],
            out_specs=pl.BlockSpec((None, r, 128), lambda i, ck_ref: (ck_ref[0], 0, i))),
        out_shape=jax.ShapeDtypeStruct((2, r, WIN_W), F32),
        compiler_params=_cparams(1),
    )(ck, g, b, p, p, p)


def _add_chips_call(name, g, b, p, ck):
    _, _, r, cc = g.shape
    rb = _row_block(r, cc, 2 * 1024 * 1024)

    def body(ck_ref, g_ref, b_ref, p0_ref, p1_ref, p2_ref, o_ref):
        own = g_ref[...] + b_ref[...]
        o_ref[...] = ((own + p0_ref[...].astype(F32)) + p1_ref[...].astype(F32)) + p2_ref[...].astype(F32)

    def peer(j):
        return pl.BlockSpec((None, rb, cc), lambda i, ck_ref: (j, i, 0))

    return pl.pallas_call(
        body, name=name,
        grid_spec=pltpu.PrefetchScalarGridSpec(
            num_scalar_prefetch=1, grid=(r // rb,),
            in_specs=[pl.BlockSpec((None, None, rb, cc), lambda i, ck_ref: (ck_ref[0], ck_ref[1], i, 0)),
                      pl.BlockSpec((None, rb, cc), lambda i, ck_ref: (ck_ref[1], i, 0)),
                      peer(0), peer(1), peer(2)],
            out_specs=pl.BlockSpec((None, rb, cc), lambda i, ck_ref: (ck_ref[0], i, 0))),
        out_shape=jax.ShapeDtypeStruct((2, r, cc), F32),
        compiler_params=_cparams(1),
    )(ck, g, b, p, p, p)


def _sum8_call(name, g):
    def body(g_ref, o_ref):
        acc = g_ref[0]
        for d in range(1, 8):
            acc = acc + g_ref[d]
        o_ref[...] = acc

    return pl.pallas_call(body, name=name, out_shape=jax.ShapeDtypeStruct(g.shape[1:], F32))(g)


def _adamw_call(name, w, g, m, v):
    r, cc = w.shape
    if r % 8 == 0 or r * cc * 4 <= 1024 * 1024:
        rb = _row_block(r, cc, 1024 * 1024) if r % 8 == 0 else r
        grid, spec = (r // rb,), pl.BlockSpec((rb, cc), lambda i: (i, 0))
    else:
        grid, spec = (cc // 128,), pl.BlockSpec((r, 128), lambda i: (0, i))

    def body(w_ref, g_ref, m_ref, v_ref, d_ref, m2_ref, v2_ref):
        gv = g_ref[...]
        m2 = ADAM_B1 * m_ref[...] + (1.0 - ADAM_B1) * gv
        v2 = ADAM_B2 * v_ref[...] + (1.0 - ADAM_B2) * (gv * gv)
        m_hat = m2 / (1.0 - ADAM_B1 ** ADAM_STEP)
        v_hat = v2 / (1.0 - ADAM_B2 ** ADAM_STEP)
        d_ref[...] = -ADAM_LR * (m_hat / (jnp.sqrt(v_hat) + ADAM_EPS) + ADAM_WD * w_ref[...])
        m2_ref[...] = m2
        v2_ref[...] = v2

    return pl.pallas_call(
        body, name=name, grid=grid, in_specs=[spec] * 4, out_specs=[spec] * 3,
        out_shape=[jax.ShapeDtypeStruct((r, cc), F32)] * 3, compiler_params=_cparams(1),
    )(w, g, m, v)


SMALL = (("norm_gain", D_MODEL), ("b_gate", GLA_KW), ("ret_norm_gain", RET_W), ("gla_norm_gain", GLA_W),
         ("final_norm_gain", D_MODEL), ("w_gate_up", GATE_RANK * GLA_KW), ("meta_tokens", N_META * D_MODEL),
         ("loss", 1))


def _pack_rows(vecs, rows):
    flat = jnp.concatenate([v.reshape(-1) for v in vecs])
    return jnp.pad(flat, (0, rows * 128 - flat.shape[0])).reshape(rows, 128)


def kernel(x, meta_tokens, norm_gain, w_in, w_gate_up, b_gate, ret_norm_gain, gla_norm_gain, w_branch_ret, w_branch_gla, w_out, final_norm_gain, loss_target, m_meta_tokens, m_norm_gain, m_w_in, m_w_gate_up, m_b_gate, m_ret_norm_gain, m_gla_norm_gain, m_w_branch_ret, m_w_branch_gla, m_w_out, m_final_norm_gain, v_meta_tokens, v_norm_gain, v_w_in, v_w_gate_up, v_b_gate, v_ret_norm_gain, v_gla_norm_gain, v_w_branch_ret, v_w_branch_gla, v_w_out, v_final_norm_gain):
    xi, yi, ci = _place()
    kme = 2 * xi + yi
    ck = jnp.stack([ci, kme]).astype(jnp.int32)
    sw_in = w_in.shape[2]

    def my_half(a, dtype):
        r, cc = a.shape
        return lax.dynamic_index_in_dim(a.reshape(2, r // 2, cc), ci, 0, keepdims=False).astype(dtype)

    g_meta, g_wg = _gather8_call("gather_small_weights", [my_half(meta_tokens, F32), my_half(w_gate_up[0], F32)])
    branch_parts = [my_half(w_branch_ret[0], BF16), my_half(w_branch_gla[0], BF16), my_half(w_out[0], BF16)]
    meta = g_meta.reshape(4, 2, N_META // 2, D_MODEL // 4).transpose(1, 2, 0, 3).reshape(N_META, D_MODEL)
    wg_full = g_wg.reshape(4, 2, GATE_RANK // 2, GLA_KW // 4).transpose(1, 2, 0, 3).reshape(GATE_RANK, GLA_KW)

    loc = _device_step(x[0], loss_target[0], meta, norm_gain, my_half(w_in[0], BF16), wg_full, b_gate, ret_norm_gain,
                       gla_norm_gain,
                       branch_parts, final_norm_gain, ck)
    names = ("w_in", "w_branch_ret", "w_branch_gla", "w_out")
    full = [loc[nm] for nm in names]
    big_w = dict(w_in=w_in[0], w_branch_ret=w_branch_ret[0], w_branch_gla=w_branch_gla[0], w_out=w_out[0])
    big_m = dict(w_in=m_w_in[0], w_branch_ret=m_w_branch_ret[0], w_branch_gla=m_w_branch_gla[0], w_out=m_w_out[0])
    big_v = dict(w_in=v_w_in[0], w_branch_ret=v_w_branch_ret[0], w_branch_gla=v_w_branch_gla[0], w_out=v_w_out[0])
    grads, deltas, new_m, new_v = {}, {}, {}, {}
    for nm, f in zip(names, full):
        shape = big_w[nm].shape
        if nm == "w_in":
            f = lax.dynamic_slice_in_dim(f, (sw_in - WIN_STEP) * kme, sw_in, axis=2)
        g = f.reshape(shape)
        if nm == "w_in":
            d, m2, v2 = (a.T for a in _adamw_call("adamw_" + nm, big_w[nm].T, g.T, big_m[nm].T, big_v[nm].T))
        else:
            d, m2, v2 = _adamw_call("adamw_" + nm, big_w[nm], g, big_m[nm], big_v[nm])
        grads[nm], deltas[nm], new_m[nm], new_v[nm] = (a.reshape((1,) + shape) for a in (g, d, m2, v2))

    small_g = dict(loc)
    small_g["meta_tokens"] = loc["dmeta"]
    n_small = sum(sz for _, sz in SMALL)
    rows = -(-n_small // 128 // 8) * 8
    (g_small,) = _gather8_call("gather_small_grads", [_pack_rows([small_g[nm] for nm, _ in SMALL], rows)])
    tot = _sum8_call("sum_small_grads", g_small).reshape(-1)
    off = 0
    sg = {}
    for nm, sz in SMALL:
        sg[nm] = tot[off:off + sz]
        off += sz
    loss = sg.pop("loss")[0]
    sg["w_gate_up"] = lax.dynamic_slice_in_dim(sg["w_gate_up"].reshape(GATE_RANK, GLA_KW), kme * (GLA_KW // 4),
                                               GLA_KW // 4, axis=1)
    sg["meta_tokens"] = lax.dynamic_slice_in_dim(sg["meta_tokens"].reshape(N_META, D_MODEL), kme * (D_MODEL // 4),
                                                 D_MODEL // 4, axis=1)
    small_w = dict(norm_gain=norm_gain, b_gate=b_gate, ret_norm_gain=ret_norm_gain, gla_norm_gain=gla_norm_gain,
                   final_norm_gain=final_norm_gain, w_gate_up=w_gate_up, meta_tokens=meta_tokens)
    small_m = dict(norm_gain=m_norm_gain, b_gate=m_b_gate, ret_norm_gain=m_ret_norm_gain,
                   gla_norm_gain=m_gla_norm_gain, final_norm_gain=m_final_norm_gain, w_gate_up=m_w_gate_up,
                   meta_tokens=m_meta_tokens)
    small_v = dict(norm_gain=v_norm_gain, b_gate=v_b_gate, ret_norm_gain=v_ret_norm_gain,
                   gla_norm_gain=v_gla_norm_gain, final_norm_gain=v_final_norm_gain, w_gate_up=v_w_gate_up,
                   meta_tokens=v_meta_tokens)
    for nm in small_w:
        shape = small_w[nm].shape
        as2d = lambda a: a.reshape((-1, shape[-1]))
        grads[nm] = sg[nm].reshape(shape)
        deltas[nm], new_m[nm], new_v[nm] = (a.reshape(shape) for a in _adamw_call(
            "adamw_" + nm, as2d(small_w[nm]), as2d(sg[nm]), as2d(small_m[nm]), as2d(small_v[nm])))

    out_order = ("meta_tokens", "norm_gain", "w_in", "w_gate_up", "b_gate", "ret_norm_gain", "gla_norm_gain",
                 "w_branch_ret", "w_branch_gla", "w_out", "final_norm_gain")
    dx = loc["dx"].reshape(x.shape)
    return (loss, dx, *[grads[nm] for nm in out_order], *[deltas[nm] for nm in out_order],
            *[new_m[nm] for nm in out_order], *[new_v[nm] for nm in out_order])
```

```python
import math
from typing import Callable, NamedTuple

import numpy as np
import jax
import jax.numpy as jnp
from jax import lax
from jax.experimental import pallas as pl
from jax.experimental.pallas import tpu as pltpu

F32 = jnp.float32
BF16 = jnp.bfloat16

D_MODEL = 1024
N_META = 16
EPS = 1e-6
ROPE_BASE = 10000.0
RET_HEADS, RET_QK, RET_V = 4, 256, 512
RET_W = RET_HEADS * RET_V
GLA_HEADS, GLA_K, GLA_V = 4, 128, 256
GLA_W = GLA_HEADS * GLA_V
GLA_KW = GLA_HEADS * GLA_K
GATE_RANK = 16
GATE_TAU = 16.0
GLA_SUB = 16

TM = 256
T0 = TM
PADF = T0 - N_META
GC = 128
TB = 768
TK = 768

W_R = 6144
W_G = 3088
W_GP = 3200
W_M = 2048
IN_COLS = W_R + W_G + W_M
WIN_STEP = (IN_COLS // 4) // 128 * 128
WIN_W = -(-(3 * (IN_COLS // 4 - WIN_STEP) + IN_COLS // 4) // 128) * 128
IN_PAD = 3 * WIN_STEP + WIN_W

ADAM_LR, ADAM_B1, ADAM_B2, ADAM_EPS, ADAM_WD, ADAM_STEP = 0.001, 0.9, 0.999, 1e-08, 0.01, 10

VMEM_LIMIT = 56 * 1024 * 1024

NN = ((1,), (0,))
NT = ((1,), (1,))
TN = ((0,), (0,))


def _dot(a, b, dims):
    return lax.dot_general(a, b, (dims, ((), ())), preferred_element_type=F32)


def _cparams(n_axes):
    return pltpu.CompilerParams(dimension_semantics=("arbitrary",) * n_axes, vmem_limit_bytes=VMEM_LIMIT)


def _sigmoid(x):
    return 0.5 * jnp.tanh(0.5 * x) + 0.5


def _split3(x):
    hi = x.astype(BF16)
    r1 = x - hi.astype(F32)
    mid = r1.astype(BF16)
    lo = (r1 - mid.astype(F32)).astype(BF16)
    return hi, mid, lo


def _exact_pm(p, x):
    hi, mid, lo = _split3(x)
    return _dot(p, hi, NN) + _dot(p, mid, NN) + _dot(p, lo, NN)


def _rms_call(x2d, head, gain, comm):
    tp = T0 + x2d.shape[0]
    nt = tp // TM
    n_xc = len(comm.srcs)

    def body(x_ref, hd_ref, g_ref, *rest):
        xc_src = rest[:n_xc]
        h_ref, u_ref = rest[n_xc:n_xc + 2]
        xc_dst = rest[n_xc + 2:2 * n_xc + 2]
        i = pl.program_id(0)
        begin, finish = comm.make(xc_src, xc_dst, rest[-2], rest[-1])
        pl.when(i == 0)(begin)
        h = jnp.where(i == 0, hd_ref[...], x_ref[...])
        h_ref[...] = h
        r = lax.rsqrt(jnp.mean(h * h, axis=-1, keepdims=True) + EPS)
        u_ref[...] = (h * r * g_ref[...]).astype(BF16)
        pl.when(i == nt - 1)(finish)

    tile = pl.BlockSpec((TM, D_MODEL), lambda i: (i, 0))
    return pl.pallas_call(
        body, name="rms_in", grid=(nt,),
        in_specs=[pl.BlockSpec((TM, D_MODEL), lambda i: (jnp.maximum(i - 1, 0), 0)),
                  pl.BlockSpec((T0, D_MODEL), lambda i: (0, 0)), pl.BlockSpec((1, D_MODEL), lambda i: (0, 0))]
        + [ANY] * n_xc,
        out_specs=[tile, tile] + [ANY] * n_xc,
        out_shape=[jax.ShapeDtypeStruct((tp, D_MODEL), F32), jax.ShapeDtypeStruct((tp, D_MODEL), BF16)]
        + list(comm.out_shapes),
        scratch_shapes=_comm_sems(comm), compiler_params=_cparams(1),
    )(x2d, head, gain, *comm.srcs)


PROJ_ROWS_MAX = 1408


def _proj_rows(m):
    return max(r for r in range(16, PROJ_ROWS_MAX + 1, 16) if m % r == 0)


def _mm_nn(name, a, b, out_dtype, tn, col0, ncols, epilogue=None, extras=(), extra_specs=()):
    m, k = a.shape
    nj, j0 = ncols // tn, col0 // tn
    tb = _proj_rows(m)

    def body(a_ref, b_ref, *rest):
        *ex, o_ref = rest
        acc = _dot(a_ref[...], b_ref[...], NN)
        if epilogue is None:
            o_ref[...] = acc.astype(out_dtype)
        else:
            epilogue(acc, o_ref, *ex)

    return pl.pallas_call(
        body, name=name, grid=(nj, m // tb),
        in_specs=[pl.BlockSpec((tb, k), lambda j, i: (i, 0)), pl.BlockSpec((k, tn), lambda j, i: (0, j0 + j))]
        + list(extra_specs),
        out_specs=pl.BlockSpec((tb, tn), lambda j, i: (i, j)),
        out_shape=jax.ShapeDtypeStruct((m, ncols), out_dtype),
        compiler_params=_cparams(2),
    )(a, b, *extras)


def _rope_tables(tp):
    half = RET_QK // 2
    pos = np.arange(tp, dtype=np.float32) - np.float32(PADF)
    inv = (ROPE_BASE ** (-np.arange(half, dtype=np.float64) / half)).astype(np.float32)
    ang = (pos[:, None] * inv[None, :]).astype(np.float64)
    return np.cos(ang).astype(np.float32), np.sin(ang).astype(np.float32)


def _rope_epilogue(acc, o_ref, cos_ref, sin_ref):
    scale = jnp.where(pl.program_id(0) == 1, RET_QK ** -0.5, 1.0).astype(F32)
    cos, sin = cos_ref[...], sin_ref[...]
    half = RET_QK // 2
    for h in range(RET_HEADS):
        t1 = acc[:, h * RET_QK:h * RET_QK + half]
        t2 = acc[:, h * RET_QK + half:(h + 1) * RET_QK]
        o_ref[:, h * RET_QK:h * RET_QK + half] = ((t1 * cos - t2 * sin) * scale).astype(BF16)
        o_ref[:, h * RET_QK + half:(h + 1) * RET_QK] = ((t2 * cos + t1 * sin) * scale).astype(BF16)


def _gqk_epilogue(acc, o_ref):
    o_ref[:, :GLA_KW] = acc[:, :GLA_KW] * (GLA_K ** -0.5)
    o_ref[:, GLA_KW:] = acc[:, GLA_KW:]


class _Comm(NamedTuple):
    srcs: tuple
    out_shapes: tuple
    n_sems: int
    make: Callable


def _comm_sems(comm):
    return [pltpu.SemaphoreType.DMA((comm.n_sems,)), pltpu.SemaphoreType.DMA((comm.n_sems,))]


def _start_wait(copies):
    def begin():
        for cp in copies:
            cp.start()

    def finish():
        for cp in copies:
            cp.wait()

    return begin, finish


def _other_chips(x, y):
    return [(1 - x, y), (x, 1 - y), (1 - x, 1 - y)]


def _gather_plan(parts, relay=()):
    n = len(parts)
    relay = tuple(relay) + (False,) * (n - len(relay))

    def make(x_refs, out_refs, send_sems, recv_sems):
        x, y, c = _place()
        me, sibling = (x, y, c), (x, y, 1 - c)
        xn, yn, dg = (1 - x, y), (x, 1 - y), (1 - x, 1 - y)

        def slot(t, px, py, pc, half=None):
            ref = out_refs[t].at[4 * px + 2 * py + pc]
            if half is None:
                return ref
            rows = ref.shape[0] // 2
            return ref.at[pl.ds(half * rows, rows)]

        def copy(t, k, dst, to, src=None):
            return pltpu.make_async_remote_copy(
                src_ref=dst if src is None else src, dst_ref=dst, send_sem=send_sems.at[8 * t + k],
                recv_sem=recv_sems.at[8 * t + k], device_id=to, device_id_type=MESH)

        mine = [pltpu.make_async_copy(x_refs[t], slot(t, *me), send_sems.at[8 * n + t]) for t in range(n)]
        sent = []
        for t in range(n):
            sent.append(copy(t, 0, slot(t, *me), sibling, src=x_refs[t]))
            sent.append(copy(t, 1, slot(t, *me), (*xn, c), src=x_refs[t]))
            sent.append(copy(t, 2, slot(t, *me), (*yn, c), src=x_refs[t]))
            if not relay[t]:
                sent.append(copy(t, 3, slot(t, *me), (*dg, c), src=x_refs[t]))

        def begin():
            for cp in mine + sent:
                cp.start()

        def finish():
            later = []

            def start(cp):
                cp.start()
                later.append(cp)

            for t in range(n):
                copy(t, 2, slot(t, *yn, c), me).wait_recv()
                if relay[t]:
                    start(copy(t, 3, slot(t, *yn, c, half=0), (*xn, c)))
                start(copy(t, 6, slot(t, *yn, c), sibling))
            for t in range(n):
                copy(t, 1, slot(t, *xn, c), me).wait_recv()
                if relay[t]:
                    start(copy(t, 4, slot(t, *xn, c, half=1), (*yn, c)))
                start(copy(t, 5, slot(t, *xn, c), sibling))
            for t in range(n):
                if relay[t]:
                    copy(t, 3, slot(t, *dg, c, half=0), me).wait_recv()
                    copy(t, 4, slot(t, *dg, c, half=1), me).wait_recv()
                else:
                    copy(t, 3, slot(t, *dg, c), me).wait_recv()
                start(copy(t, 7, slot(t, *dg, c), sibling))
            for t in range(n):
                copy(t, 0, slot(t, *sibling), me).wait_recv()
                copy(t, 5, slot(t, *xn, 1 - c), me).wait_recv()
                copy(t, 6, slot(t, *yn, 1 - c), me).wait_recv()
                copy(t, 7, slot(t, *dg, 1 - c), me).wait_recv()
            for cp in sent + later:
                cp.wait_send()
            for cp in mine:
                cp.wait()

        return begin, finish

    return _Comm(tuple(parts), tuple(jax.ShapeDtypeStruct((8,) + p.shape, p.dtype) for p in parts), 9 * n, make)


def _exchange_plan(ss):
    def make(s_refs, b_refs, send_sems, recv_sems):
        x, y, c = _place()
        return _start_wait([pltpu.make_async_remote_copy(
            src_ref=s_refs[t].at[2 * chip[0] + chip[1]], dst_ref=b_refs[t].at[j], send_sem=send_sems.at[3 * t + j],
            recv_sem=recv_sems.at[3 * t + j], device_id=(*chip, c), device_id_type=MESH)
            for t in range(len(s_refs)) for j, chip in enumerate(_other_chips(x, y))])

    return _Comm(tuple(ss), tuple(jax.ShapeDtypeStruct((3,) + s.shape[1:], s.dtype) for s in ss), 3 * len(ss), make)


def _exchange_window_plan(s):
    def make(s_refs, b_refs, send_sems, recv_sems):
        x, y, c = _place()
        return _start_wait([pltpu.make_async_remote_copy(
            src_ref=s_refs[0].at[:, pl.ds(pl.multiple_of((2 * chip[0] + chip[1]) * WIN_STEP, 128), WIN_W)],
            dst_ref=b_refs[0].at[j], send_sem=send_sems.at[j], recv_sem=recv_sems.at[j], device_id=(*chip, c),
            device_id_type=MESH) for j, chip in enumerate(_other_chips(x, y))])

    return _Comm((s,), (jax.ShapeDtypeStruct((3, s.shape[0], WIN_W), s.dtype),), 3, make)


def _swap_plan(gs):
    def make(g_refs, b_refs, send_sems, recv_sems):
        x, y, c = _place()
        return _start_wait([pltpu.make_async_remote_copy(
            src_ref=g_refs[t].at[1 - c], dst_ref=b_refs[t], send_sem=send_sems.at[t], recv_sem=recv_sems.at[t],
            device_id=(x, y, 1 - c), device_id_type=MESH) for t in range(len(g_refs))])

    return _Comm(tuple(gs), tuple(jax.ShapeDtypeStruct(g.shape[1:], g.dtype) for g in gs), len(gs), make)


def _spread_plan(parts):
    def make(p_refs, o_refs, send_sems, recv_sems):
        x, y, c = _place()
        copies = []
        for t in range(len(p_refs)):
            mine = o_refs[t].at[4 * x + 2 * y + c]
            copies.append(pltpu.make_async_copy(p_refs[t], mine, send_sems.at[7 * len(p_refs) + t]))
            for r in range(1, 8):
                peer = (1 - x if r & 4 else x, 1 - y if r & 2 else y, 1 - c if r & 1 else c)
                copies.append(pltpu.make_async_remote_copy(
                    src_ref=p_refs[t], dst_ref=mine, send_sem=send_sems.at[7 * t + r - 1],
                    recv_sem=recv_sems.at[7 * t + r - 1], device_id=peer, device_id_type=MESH))
        return _start_wait(copies)

    return _Comm(tuple(parts), tuple(jax.ShapeDtypeStruct((8,) + p.shape, p.dtype) for p in parts), 8 * len(parts),
                 make)


def _mm_nt_acc(name, a, w, tk, acc_in=None, epilogue=None, extras=(), extra_specs=(), extra_out_shapes=(),
               extra_out_specs=(), extra_scratch=(), comm=None):
    m, k = a.shape
    n = w.shape[0]
    nk, ni = k // tk, m // TB
    has_acc = acc_in is not None
    n_xc = len(comm.srcs) if comm else 0
    n_es = len(extra_scratch)

    def body(*refs):
        a_ref, w_ref = refs[0], refs[1]
        pos = 2
        acc_ref = None
        if has_acc:
            acc_ref = refs[pos]
            pos += 1
        ex = refs[pos:pos + len(extras)]
        pos += len(extras)
        xc_src = refs[pos:pos + n_xc]
        pos += n_xc
        n_scr = 1 + n_es + (2 if n_xc else 0)
        outs = refs[pos:len(refs) - n_scr - n_xc]
        xc_dst = refs[len(refs) - n_scr - n_xc:len(refs) - n_scr]
        scr = refs[len(refs) - n_scr]
        es = refs[len(refs) - n_scr + 1:len(refs) - n_scr + 1 + n_es]
        i, kk = pl.program_id(0), pl.program_id(1)
        if n_xc:
            begin, finish = comm.make(xc_src, xc_dst, refs[-2], refs[-1])
            pl.when((i == 0) & (kk == 0))(begin)

        @pl.when(kk == 0)
        def _():
            scr[...] = acc_ref[...] if has_acc else jnp.zeros_like(scr)

        scr[...] += _dot(a_ref[...], w_ref[...], NT)

        @pl.when(kk == nk - 1)
        def _():
            if epilogue is None:
                outs[0][...] = scr[...]
            else:
                epilogue(scr[...], outs, i, ni, *ex, *es)

        if n_xc:
            pl.when((i == ni - 1) & (kk == nk - 1))(finish)

    in_specs = [pl.BlockSpec((TB, tk), lambda i, kk: (i, kk)), pl.BlockSpec((n, tk), lambda i, kk: (0, kk))]
    args = [a, w]
    if has_acc:
        in_specs.append(pl.BlockSpec((TB, n), lambda i, kk: (i, 0)))
        args.append(acc_in)
    in_specs += list(extra_specs) + [ANY] * n_xc
    args += list(extras) + (list(comm.srcs) if comm else [])
    if epilogue is None:
        out_shape = [jax.ShapeDtypeStruct((m, n), F32)]
        out_specs = [pl.BlockSpec((TB, n), lambda i, kk: (i, 0))]
    else:
        out_shape, out_specs = list(extra_out_shapes), list(extra_out_specs)
    scratch = [pltpu.VMEM((TB, n), F32)] + list(extra_scratch)
    if n_xc:
        out_shape += list(comm.out_shapes)
        out_specs += [ANY] * n_xc
        scratch += _comm_sems(comm)
    return pl.pallas_call(
        body, name=name, grid=(ni, nk), in_specs=in_specs, out_specs=out_specs, out_shape=out_shape,
        scratch_shapes=scratch, compiler_params=_cparams(2),
    )(*args)


def _rms_bwd_epilogue(du, outs, i, ni, h_ref, g_ref, dh1_ref, obuf, sems):
    dx_ref, dmeta_ref, dg_ref = outs
    h = h_ref[...]
    r = lax.rsqrt(jnp.mean(h * h, axis=-1, keepdims=True) + EPS)
    xh = h * r
    dxh = du * g_ref[...]
    dh0 = dh1_ref[...] + r * (dxh - xh * jnp.mean(dxh * xh, axis=-1, keepdims=True))

    def put(slot, tile):
        return pltpu.make_async_copy(obuf.at[slot], dx_ref.at[pl.ds(pl.multiple_of(tile * TB - T0, 8), TB)],
                                     sems.at[slot])

    @pl.when(i == 0)
    def _():
        dg_ref[...] = jnp.zeros_like(dg_ref)
        dmeta_ref[...] = dh0[PADF:T0, :]
        obuf[0] = dh0
        first = pltpu.make_async_copy(obuf.at[0, pl.ds(T0, TB - T0)], dx_ref.at[pl.ds(0, TB - T0)], sems.at[0])
        first.start()
        first.wait()

    @pl.when(i >= 1)
    def _():
        slot = i % 2

        @pl.when(i >= 3)
        def _():
            put(slot, i - 2).wait()

        obuf[slot] = dh0
        put(slot, i).start()

    dg_ref[...] += jnp.sum(du * xh, axis=0, keepdims=True)

    @pl.when(i == ni - 1)
    def _():
        for tile in (ni - 2, ni - 1):
            if tile >= 1:
                put(tile % 2, tile).wait()


def _mm_tn(name, a, b, bn, ncols=None, bcol0=0, into=None, col0=0, out_cols=None):
    t, m = a.shape
    n = ncols or b.shape[1]
    j0, bj0 = col0 // bn, bcol0 // bn

    def body(a_ref, b_ref, *rest):
        o_ref = rest[-1]

        @pl.when(pl.program_id(1) == 0)
        def _():
            o_ref[...] = jnp.zeros_like(o_ref)

        o_ref[...] += _dot(a_ref[...], b_ref[...], TN)

    in_specs = [pl.BlockSpec((TK, m), lambda j, kk: (kk, 0)), pl.BlockSpec((TK, bn), lambda j, kk: (kk, bj0 + j))]
    args = [a, b]
    aliases = {}
    if into is not None:
        in_specs.append(ANY)
        args.append(into)
        aliases = {2: 0}
        out_cols = into.shape[1]
    return pl.pallas_call(
        body, name=name, grid=(n // bn, t // TK), in_specs=in_specs,
        out_specs=pl.BlockSpec((m, bn), lambda j, kk: (0, j0 + j)),
        out_shape=jax.ShapeDtypeStruct((m, out_cols or n), F32), input_output_aliases=aliases,
        compiler_params=_cparams(2),
    )(*args)


def _place_merge_cols_call(dwp, dw_m):
    c0 = W_R + W_GP - 128
    tail = IN_PAD - c0
    rows = 256

    def body(m_ref, p_ref, o_ref, buf, low, sem):
        get = pltpu.make_async_copy(o_ref.at[:, pl.ds(c0, 128)], low, sem)
        get.start()
        get.wait()
        for r in range(0, D_MODEL, rows):
            buf[r:r + rows, :] = jnp.concatenate(
                [low[r:r + rows, :GATE_RANK], m_ref[r:r + rows, :],
                 jnp.zeros((rows, tail - GATE_RANK - W_M), F32)], axis=1)
        put = pltpu.make_async_copy(buf, o_ref.at[:, pl.ds(c0, tail)], sem)
        put.start()
        put.wait()

    return pl.pallas_call(
        body, name="place_merge_cols",
        in_specs=[pl.BlockSpec(memory_space=pltpu.VMEM), ANY], out_specs=ANY,
        out_shape=jax.ShapeDtypeStruct(dwp.shape, F32), input_output_aliases={1: 0},
        scratch_shapes=[pltpu.VMEM((D_MODEL, tail), F32), pltpu.VMEM((D_MODEL, 128), F32), pltpu.SemaphoreType.DMA],
        compiler_params=pltpu.CompilerParams(vmem_limit_bytes=VMEM_LIMIT),
    )(dw_m, dwp)


def _ret_fill_decay(lg_ref, dm_scr):
    c = TM
    ii = lax.broadcasted_iota(jnp.int32, (c, c), 0)
    jj = lax.broadcasted_iota(jnp.int32, (c, c), 1)
    rel = (ii - jj).astype(F32)
    for h in range(RET_HEADS):
        dm_scr[h] = jnp.where(rel >= 0, jnp.exp(jnp.maximum(rel, 0.0) * lg_ref[h]), 0.0)


def _ret_consts(lg, dm_ref):
    c = TM
    idx = lax.broadcasted_iota(jnp.int32, (c, 1), 0).astype(F32)
    xi = jnp.exp((idx + 1.0) * lg)
    zeta = jnp.exp((c - 1.0 - idx) * lg)
    gc = jnp.exp(jnp.full((1, 1), c, F32) * lg)
    return dm_ref[...], xi, zeta, gc


def _ret_fwd_call(rqk, rv, rg, gain, lgam):
    tp = rqk.shape[0]
    nc = tp // TM

    def body(lg_ref, qk_ref, v_ref, rg_ref, g_ref, o_ref, a_ref, st_ref, sc_ref, s_scr, dm_scr):
        @pl.when(pl.program_id(0) == 0)
        def _():
            s_scr[...] = jnp.zeros_like(s_scr)
            _ret_fill_decay(lg_ref, dm_scr)

        for h in range(RET_HEADS):
            dm, xi, zeta, gc = _ret_consts(lg_ref[h], dm_scr.at[h])
            q = qk_ref[:, h * RET_QK:(h + 1) * RET_QK]
            k = qk_ref[:, D_MODEL + h * RET_QK:D_MODEL + (h + 1) * RET_QK]
            v = v_ref[:, h * RET_V:(h + 1) * RET_V]
            sb = s_scr[h].astype(BF16)
            st_ref[0, h] = sb
            s = (_dot(q, k, NT) * dm).astype(BF16)
            sc_ref[0, h] = s
            o = _dot(s, v, NN) + xi * _dot(q, sb, NN)
            kz = (k.astype(F32) * zeta).astype(BF16)
            s_scr[h] = gc * s_scr[h] + _dot(kz, v, TN)
            o_ref[:, h * RET_V:(h + 1) * RET_V] = o
            mu = jnp.mean(o, axis=-1, keepdims=True)
            xc = o - mu
            xh = xc * lax.rsqrt(jnp.mean(xc * xc, axis=-1, keepdims=True) + EPS)
            g = rg_ref[:, h * RET_V:(h + 1) * RET_V]
            a_ref[:, h * RET_V:(h + 1) * RET_V] = (
                xh * g_ref[:, h * RET_V:(h + 1) * RET_V] * (g * _sigmoid(g))).astype(BF16)

    return pl.pallas_call(
        body, name="ret_fwd", grid=(nc,),
        in_specs=[pl.BlockSpec(memory_space=pltpu.SMEM),
                  pl.BlockSpec((TM, 2 * D_MODEL), lambda n: (n, 0)),
                  pl.BlockSpec((TM, RET_W), lambda n: (n, 0)),
                  pl.BlockSpec((TM, RET_W), lambda n: (n, 0)),
                  pl.BlockSpec((1, RET_W), lambda n: (0, 0))],
        out_specs=[pl.BlockSpec((TM, RET_W), lambda n: (n, 0)),
                   pl.BlockSpec((TM, RET_W), lambda n: (n, 0)),
                   pl.BlockSpec((1, RET_HEADS, RET_QK, RET_V), lambda n: (n, 0, 0, 0)),
                   pl.BlockSpec((1, RET_HEADS, TM, TM), lambda n: (n, 0, 0, 0))],
        out_shape=[jax.ShapeDtypeStruct((tp, RET_W), F32), jax.ShapeDtypeStruct((tp, RET_W), BF16),
                   jax.ShapeDtypeStruct((nc, RET_HEADS, RET_QK, RET_V), BF16),
                   jax.ShapeDtypeStruct((nc, RET_HEADS, TM, TM), BF16)],
        scratch_shapes=[pltpu.VMEM((RET_HEADS, RET_QK, RET_V), F32), pltpu.VMEM((RET_HEADS, TM, TM), F32)],
        compiler_params=_cparams(1),
    )(lgam, rqk, rv, rg, gain)


def _ret_bwd_call(rqk, rv, rg, o_ret, dpr, wbr, states, scores, gain, lgam, cos, sin):
    tp = rqk.shape[0]
    nc = tp // TM
    half = RET_QK // 2

    def body(lg_ref, qk_ref, v_ref, rg_ref, o_ref, dpr_ref, wbr_ref, st_ref, sc_ref, g_ref, cos_ref, sin_ref, dp_ref,
             dg_ref, ds_scr, dm_scr):
        @pl.when(pl.program_id(0) == 0)
        def _():
            ds_scr[...] = jnp.zeros_like(ds_scr)
            dg_ref[...] = jnp.zeros_like(dg_ref)
            _ret_fill_decay(lg_ref, dm_scr)

        cos, sin = cos_ref[...], sin_ref[...]
        for h in range(RET_HEADS):
            hs = slice(h * RET_V, (h + 1) * RET_V)
            dm, xi, zeta, gc = _ret_consts(lg_ref[h], dm_scr.at[h])
            o = o_ref[:, hs]
            mu = jnp.mean(o, axis=-1, keepdims=True)
            xc = o - mu
            rstd = lax.rsqrt(jnp.mean(xc * xc, axis=-1, keepdims=True) + EPS)
            xh = xc * rstd
            gain_h = g_ref[:, hs]
            g = rg_ref[:, hs]
            sg = _sigmoid(g)
            silu = g * sg
            dah = _dot(dpr_ref[...], wbr_ref[hs, :], NT)
            dp_ref[:, 4 * D_MODEL + h * RET_V:4 * D_MODEL + (h + 1) * RET_V] = (
                dah * (xh * gain_h) * (sg * (1.0 + g * (1.0 - sg)))).astype(BF16)
            dn = dah * silu
            dg_ref[:, hs] += jnp.sum(dn * xh, axis=0, keepdims=True)
            dxh = dn * gain_h
            do = rstd * (dxh - jnp.mean(dxh, axis=-1, keepdims=True)
                         - xh * jnp.mean(dxh * xh, axis=-1, keepdims=True))
            dob = do.astype(BF16)
            q = qk_ref[:, h * RET_QK:(h + 1) * RET_QK]
            k = qk_ref[:, D_MODEL + h * RET_QK:D_MODEL + (h + 1) * RET_QK]
            v = v_ref[:, hs]
            sp = st_ref[0, h]
            ds = ds_scr[h]
            dsb = ds.astype(BF16)
            s = sc_ref[0, h]
            dsc = (_dot(dob, v, NT) * dm).astype(BF16)
            dq = _dot(dsc, k, NN) + xi * _dot(dob, sp, NT)
            dk = _dot(dsc, q, TN) + zeta * _dot(v, dsb, NT)
            kz = (k.astype(F32) * zeta).astype(BF16)
            dv = _dot(s, dob, TN) + _dot(kz, dsb, NN)
            qx = (q.astype(F32) * xi).astype(BF16)
            ds_scr[h] = gc * ds + _dot(qx, dob, TN)
            dp_ref[:, 2 * D_MODEL + h * RET_V:2 * D_MODEL + (h + 1) * RET_V] = dv.astype(BF16)
            dk = dk * (RET_QK ** -0.5)
            for base, t in ((0, dq), (D_MODEL, dk)):
                t1, t2 = t[:, :half], t[:, half:]
                dp_ref[:, base + h * RET_QK:base + h * RET_QK + half] = (t1 * cos + t2 * sin).astype(BF16)
                dp_ref[:, base + h * RET_QK + half:base + (h + 1) * RET_QK] = (t2 * cos - t1 * sin).astype(BF16)

    rev = lambda n: (nc - 1 - n, 0)
    return pl.pallas_call(
        body, name="ret_bwd", grid=(nc,),
        in_specs=[pl.BlockSpec(memory_space=pltpu.SMEM),
                  pl.BlockSpec((TM, 2 * D_MODEL), rev),
                  pl.BlockSpec((TM, RET_W), rev),
                  pl.BlockSpec((TM, RET_W), rev),
                  pl.BlockSpec((TM, RET_W), rev),
                  pl.BlockSpec((TM, D_MODEL), rev),
                  pl.BlockSpec((RET_W, D_MODEL), lambda n: (0, 0)),
                  pl.BlockSpec((1, RET_HEADS, RET_QK, RET_V), lambda n: (nc - 1 - n, 0, 0, 0)),
                  pl.BlockSpec((1, RET_HEADS, TM, TM), lambda n: (nc - 1 - n, 0, 0, 0)),
                  pl.BlockSpec((1, RET_W), lambda n: (0, 0)),
                  pl.BlockSpec((TM, half), rev),
                  pl.BlockSpec((TM, half), rev)],
        out_specs=[pl.BlockSpec((TM, W_R), rev), pl.BlockSpec((1, RET_W), lambda n: (0, 0))],
        out_shape=[jax.ShapeDtypeStruct((tp, W_R), BF16), jax.ShapeDtypeStruct((1, RET_W), F32)],
        scratch_shapes=[pltpu.VMEM((RET_HEADS, RET_QK, RET_V), F32), pltpu.VMEM((RET_HEADS, TM, TM), F32)],
        compiler_params=_cparams(1),
    )(lgam, rqk, rv, rg, o_ret, dpr, wbr, states, scores, gain, cos, sin)


GLA_LEVELS = tuple(GC >> (s + 1) for s in range(int(math.log2(GC // GLA_SUB))))
NLEV = len(GLA_LEVELS)


def _gla_tril():
    return np.tril(np.ones((GC, GC), np.float32))


def _gla_masks():
    ii = lax.broadcasted_iota(jnp.int32, (GC, GC), 0)
    jj = lax.broadcasted_iota(jnp.int32, (GC, GC), 1)
    masks = []
    for m in GLA_LEVELS:
        sh = int(math.log2(2 * m))
        masks.append(((ii >> sh) == (jj >> sh)) & ((ii & m) != 0) & ((jj & m) == 0))
    sh = int(math.log2(GLA_SUB))
    md = ((ii >> sh) == (jj >> sh)) & (jj <= ii)
    row = lax.broadcasted_iota(jnp.int32, (GC, 1), 0)
    second = [(row & m) != 0 for m in GLA_LEVELS]
    return masks, md, second


def _gla_log_decay(glr_ref, wg_ref, bg_ref):
    z = _dot(glr_ref[...].astype(BF16), wg_ref[...], NN) + bg_ref[...]
    la = (jnp.minimum(z, 0.0) - jnp.log1p(jnp.exp(-jnp.abs(z)))) * (1.0 / GATE_TAU)
    return z, la


def _gla_row_steps(b_ref, cs, rows, size):
    parts = [jnp.zeros((size, GLA_K), F32) if r is None else jnp.broadcast_to(b_ref[r:r + 1, cs], (size, GLA_K))
             for r in rows]
    return parts[0] if len(parts) == 1 else jnp.concatenate(parts, axis=0)


def _gla_factors(b_ref, h, second):
    cs = slice(h * GLA_K, (h + 1) * GLA_K)
    b = b_ref[:, cs]
    fq, fk = [], []
    for l, m in enumerate(GLA_LEVELS):
        d = b - _gla_row_steps(b_ref, cs, [s + m - 1 for s in range(0, GC, 2 * m)], 2 * m)
        f = jnp.exp(jnp.where(second[l], d, -d))
        fq.append(jnp.where(second[l], f, 0.0))
        fk.append(jnp.where(second[l], 0.0, f))
    dd = b - _gla_row_steps(b_ref, cs, [None] + [s - 1 for s in range(GLA_SUB, GC, GLA_SUB)], GLA_SUB)
    ed = jnp.exp(dd)
    edi = jnp.exp(-dd)
    eb = jnp.exp(b)
    bl = b_ref[GC - 1:GC, cs]
    ee = jnp.exp(bl - b)
    ebl = jnp.exp(bl)
    return fq, fk, ed, edi, eb, ee, ebl


def _gla_scaled(q, k, fq, fk, ed, edi):
    qt = [(q * f).astype(BF16) for f in fq]
    kt = [(k * f).astype(BF16) for f in fk]
    return qt, kt, (q * ed).astype(BF16), (k * edi).astype(BF16)


def _gla_scores(qt, kt, qd, kd, masks, md):
    a = jnp.where(md, _dot(qd, kd, NT), 0.0)
    for l in range(NLEV):
        a = a + jnp.where(masks[l], _dot(qt[l], kt[l], NT), 0.0)
    return a.astype(BF16)


def _gla_fwd_call(gqk, gv, glr, gg, wg, bg, gain, pmat, comm=None):
    tp = gqk.shape[0]
    nc = tp // GC
    n_xc = len(comm.srcs) if comm else 0

    def body(qk_ref, v_ref, glr_ref, gg_ref, wg_ref, bg_ref, g_ref, p_ref, *rest):
        xc_src = rest[:n_xc]
        o_ref, a_ref, st_ref, am_ref = rest[n_xc:n_xc + 4]
        xc_dst = rest[n_xc + 4:2 * n_xc + 4]
        s_scr, b_scr = rest[2 * n_xc + 4:2 * n_xc + 6]
        n = pl.program_id(0)
        if n_xc:
            begin, finish = comm.make(xc_src, xc_dst, rest[-2], rest[-1])
            pl.when(n == 0)(begin)
            pl.when(n == nc - 1)(finish)

        @pl.when(n == 0)
        def _():
            s_scr[...] = jnp.zeros_like(s_scr)

        _, la = _gla_log_decay(glr_ref, wg_ref, bg_ref)
        b_scr[...] = _exact_pm(p_ref[...], la)
        masks, md, second = _gla_masks()
        for h in range(GLA_HEADS):
            q = qk_ref[:, h * GLA_K:(h + 1) * GLA_K]
            k = qk_ref[:, GLA_KW + h * GLA_K:GLA_KW + (h + 1) * GLA_K]
            vs = slice(h * GLA_V, (h + 1) * GLA_V)
            v = v_ref[:, vs]
            fq, fk, ed, edi, eb, ee, ebl = _gla_factors(b_scr, h, second)
            a = _gla_scores(*_gla_scaled(q, k, fq, fk, ed, edi), masks, md)
            am_ref[0, h] = a
            sb = s_scr[h].astype(BF16)
            st_ref[0, h] = sb
            o = _dot(a, v, NN) + _dot((q * eb).astype(BF16), sb, NT)
            s_scr[h] = s_scr[h] * ebl + _dot(v, (k * ee).astype(BF16), TN)
            o_ref[:, vs] = o
            xh = o * lax.rsqrt(jnp.mean(o * o, axis=-1, keepdims=True) + EPS)
            g = gg_ref[:, vs]
            a_ref[:, vs] = (xh * g_ref[:, vs] * (g * _sigmoid(g))).astype(BF16)

    return pl.pallas_call(
        body, name="gla_fwd", grid=(nc,),
        in_specs=[pl.BlockSpec((GC, 2 * GLA_KW), lambda n: (n, 0)),
                  pl.BlockSpec((GC, GLA_W), lambda n: (n, 0)),
                  pl.BlockSpec((GC, 128), lambda n: (n, 0)),
                  pl.BlockSpec((GC, GLA_W), lambda n: (n, 0)),
                  pl.BlockSpec((128, GLA_KW), lambda n: (0, 0)),
                  pl.BlockSpec((1, GLA_KW), lambda n: (0, 0)),
                  pl.BlockSpec((1, GLA_W), lambda n: (0, 0)),
                  pl.BlockSpec((GC, GC), lambda n: (0, 0))] + [ANY] * n_xc,
        out_specs=[pl.BlockSpec((GC, GLA_W), lambda n: (n, 0)),
                   pl.BlockSpec((GC, GLA_W), lambda n: (n, 0)),
                   pl.BlockSpec((1, GLA_HEADS, GLA_V, GLA_K), lambda n: (n, 0, 0, 0)),
                   pl.BlockSpec((1, GLA_HEADS, GC, GC), lambda n: (n, 0, 0, 0))] + [ANY] * n_xc,
        out_shape=[jax.ShapeDtypeStruct((tp, GLA_W), F32), jax.ShapeDtypeStruct((tp, GLA_W), BF16),
                   jax.ShapeDtypeStruct((nc, GLA_HEADS, GLA_V, GLA_K), BF16),
                   jax.ShapeDtypeStruct((nc, GLA_HEADS, GC, GC), BF16)] + (list(comm.out_shapes) if comm else []),
        scratch_shapes=[pltpu.VMEM((GLA_HEADS, GLA_V, GLA_K), F32), pltpu.VMEM((GC, GLA_KW), F32)]
        + (_comm_sems(comm) if comm else []),
        compiler_params=_cparams(1),
    )(gqk, gv, glr, gg, wg, bg, gain, pmat, *(comm.srcs if comm else ()))


def _gla_bwd_call(gqk, gv, glr, gg, o_gla, da, states, scores, wg, bg, gain, pmat, pmat_t, comm=None):
    tp = gqk.shape[0]
    nc = tp // GC
    o_gv, o_gg, o_lr = 2 * GLA_KW, 2 * GLA_KW + GLA_W, 2 * GLA_KW + 2 * GLA_W
    n_xc = len(comm.srcs) if comm else 0

    def body(qk_ref, v_ref, glr_ref, gg_ref, o_ref, da_ref, st_ref, am_ref, wg_ref, bg_ref, g_ref, p_ref, pt_ref,
             *rest):
        xc_src = rest[:n_xc]
        dp_ref, dwg_ref, dbg_ref, dg_ref = rest[n_xc:n_xc + 4]
        xc_dst = rest[n_xc + 4:2 * n_xc + 4]
        ds_scr, b_scr, db_scr = rest[2 * n_xc + 4:2 * n_xc + 7]
        n = pl.program_id(0)
        if n_xc:
            begin, finish = comm.make(xc_src, xc_dst, rest[-2], rest[-1])
            pl.when(n == 0)(begin)
            pl.when(n == nc - 1)(finish)

        @pl.when(n == 0)
        def _():
            ds_scr[...] = jnp.zeros_like(ds_scr)
            dwg_ref[...] = jnp.zeros_like(dwg_ref)
            dbg_ref[...] = jnp.zeros_like(dbg_ref)
            dg_ref[...] = jnp.zeros_like(dg_ref)

        z, la = _gla_log_decay(glr_ref, wg_ref, bg_ref)
        b_scr[...] = _exact_pm(p_ref[...], la)
        masks, md, second = _gla_masks()
        for h in range(GLA_HEADS):
            cs = slice(h * GLA_K, (h + 1) * GLA_K)
            vs = slice(h * GLA_V, (h + 1) * GLA_V)
            o = o_ref[:, vs]
            rstd = lax.rsqrt(jnp.mean(o * o, axis=-1, keepdims=True) + EPS)
            xh = o * rstd
            gain_h = g_ref[:, vs]
            g = gg_ref[:, vs]
            sg = _sigmoid(g)
            dah = da_ref[:, vs]
            dp_ref[:, o_gg + h * GLA_V:o_gg + (h + 1) * GLA_V] = (
                dah * (xh * gain_h) * (sg * (1.0 + g * (1.0 - sg)))).astype(BF16)
            dn = dah * (g * sg)
            dg_ref[:, vs] += jnp.sum(dn * xh, axis=0, keepdims=True)
            dxh = dn * gain_h
            do = rstd * (dxh - xh * jnp.mean(dxh * xh, axis=-1, keepdims=True))
            dob = do.astype(BF16)
            q = qk_ref[:, cs]
            k = qk_ref[:, GLA_KW + h * GLA_K:GLA_KW + (h + 1) * GLA_K]
            v = v_ref[:, vs]
            fq, fk, ed, edi, eb, ee, ebl = _gla_factors(b_scr, h, second)
            qt, kt, qd, kd = _gla_scaled(q, k, fq, fk, ed, edi)
            sp = st_ref[0, h]
            ds = ds_scr[h]
            dsb = ds.astype(BF16)
            q_in = q * eb
            k_end = k * ee
            da_s = _dot(dob, v, NT)
            dv = _dot(am_ref[0, h], dob, TN) + _dot(k_end.astype(BF16), dsb, NT)
            dq_in = _dot(dob, sp, NN)
            dk_end = _dot(v, dsb, NN)
            dbl = jnp.sum(sp.astype(F32) * ds, axis=0, keepdims=True) * ebl
            ds_scr[h] = ds * ebl + _dot(dob, q_in.astype(BF16), TN)
            dq = dq_in * eb
            dk = dk_end * ee
            de_end = dk_end * k_end
            db = dq_in * q_in - de_end
            placed = [(GC - 1, jnp.sum(de_end, axis=0, keepdims=True) + dbl)]
            for l, m in enumerate(GLA_LEVELS):
                dal = jnp.where(masks[l], da_s, 0.0).astype(BF16)
                dqt = _dot(dal, kt[l], NN)
                dkt = _dot(dal, qt[l], TN)
                dq = dq + dqt * fq[l]
                dk = dk + dkt * fk[l]
                gl = dqt * (q * fq[l]) - dkt * (k * fk[l])
                db = db + gl
                placed += [(s + m - 1, -jnp.sum(gl[s:s + 2 * m], axis=0, keepdims=True)) for s in range(0, GC, 2 * m)]
            dad = jnp.where(md, da_s, 0.0).astype(BF16)
            dqd = _dot(dad, kd, NN)
            dkd = _dot(dad, qd, TN)
            dq = dq + dqd * ed
            dk = dk + dkd * edi
            gd = dqd * (q * ed) - dkd * (k * edi)
            db = db + gd
            placed += [(s - 1, -jnp.sum(gd[s:s + GLA_SUB], axis=0, keepdims=True)) for s in range(GLA_SUB, GC, GLA_SUB)]
            db_scr[:, cs] = db
            for r, val in placed:
                db_scr[r:r + 1, cs] += val
            dp_ref[:, cs] = (dq * (GLA_K ** -0.5)).astype(BF16)
            dp_ref[:, GLA_KW + h * GLA_K:GLA_KW + (h + 1) * GLA_K] = dk.astype(BF16)
            dp_ref[:, o_gv + h * GLA_V:o_gv + (h + 1) * GLA_V] = dv.astype(BF16)
        dla = _exact_pm(pt_ref[...], db_scr[...])
        row = (nc - 1 - n) * GC + lax.broadcasted_iota(jnp.int32, (GC, 1), 0)
        dz = jnp.where(row >= PADF, dla * (1.0 / GATE_TAU) * _sigmoid(-z), 0.0)
        dzb = dz.astype(BF16)
        dp_ref[:, o_lr:] = _dot(dzb, wg_ref[...], NT).astype(BF16)
        dwg_ref[...] += _dot(glr_ref[...].astype(BF16), dzb, TN)
        dbg_ref[...] += jnp.sum(dz, axis=0, keepdims=True)

    rev = lambda n: (nc - 1 - n, 0)
    const = lambda n: (0, 0)
    xc_shapes, xc_sems = (list(comm.out_shapes), _comm_sems(comm)) if n_xc else ([], [])
    return pl.pallas_call(
        body, name="gla_bwd", grid=(nc,),
        in_specs=[pl.BlockSpec((GC, 2 * GLA_KW), rev),
                  pl.BlockSpec((GC, GLA_W), rev),
                  pl.BlockSpec((GC, 128), rev),
                  pl.BlockSpec((GC, GLA_W), rev),
                  pl.BlockSpec((GC, GLA_W), rev),
                  pl.BlockSpec((GC, GLA_W), rev),
                  pl.BlockSpec((1, GLA_HEADS, GLA_V, GLA_K), lambda n: (nc - 1 - n, 0, 0, 0)),
                  pl.BlockSpec((1, GLA_HEADS, GC, GC), lambda n: (nc - 1 - n, 0, 0, 0)),
                  pl.BlockSpec((128, GLA_KW), const),
                  pl.BlockSpec((1, GLA_KW), const),
                  pl.BlockSpec((1, GLA_W), const),
                  pl.BlockSpec((GC, GC), const),
                  pl.BlockSpec((GC, GC), const)] + [ANY] * n_xc,
        out_specs=[pl.BlockSpec((GC, W_GP), rev), pl.BlockSpec((128, GLA_KW), const),
                   pl.BlockSpec((1, GLA_KW), const), pl.BlockSpec((1, GLA_W), const)] + [ANY] * n_xc,
        out_shape=[jax.ShapeDtypeStruct((tp, W_GP), BF16), jax.ShapeDtypeStruct((128, GLA_KW), F32),
                   jax.ShapeDtypeStruct((1, GLA_KW), F32), jax.ShapeDtypeStruct((1, GLA_W), F32)] + xc_shapes,
        scratch_shapes=[pltpu.VMEM((GLA_HEADS, GLA_V, GLA_K), F32), pltpu.VMEM((GC, GLA_KW), F32),
                        pltpu.VMEM((GC, GLA_KW), F32)] + xc_sems,
        compiler_params=_cparams(1),
    )(gqk, gv, glr, gg, o_gla, da, states, scores, wg, bg, gain, pmat, pmat_t, *(comm.srcs if comm else ()))


def _mid_call(a_ret, a_gla, mg, h0, tgt, wbr, wbg, wout, gf):
    tp = h0.shape[0]
    nt = tp // TM

    def body(ar_ref, ag_ref, mg_ref, h_ref, t_ref, wbr_ref, wbg_ref, wo_ref, gf_ref,
             dh1_ref, dag_ref, dm_ref, mb_ref, dh1b_ref, dprb_ref, dpgb_ref, loss_ref, dgf_ref):
        i = pl.program_id(0)

        @pl.when(i == 0)
        def _():
            loss_ref[...] = jnp.zeros_like(loss_ref)
            dgf_ref[...] = jnp.zeros_like(dgf_ref)

        ar, ag = ar_ref[...], ag_ref[...]
        pr = _dot(ar, wbr_ref[...], NN)
        pg = _dot(ag, wbg_ref[...], NN)
        sr = _sigmoid(mg_ref[:, :D_MODEL])
        sg = _sigmoid(mg_ref[:, D_MODEL:])
        merged = (sr * pr + sg * pg).astype(BF16)
        mb_ref[...] = merged
        h1 = h_ref[...] + _dot(merged, wo_ref[...], NN)
        r1 = lax.rsqrt(jnp.mean(h1 * h1, axis=-1, keepdims=True) + EPS)
        xh = h1 * r1
        gfv = gf_ref[...]
        live = jnp.where(i > 0, 1.0, 0.0).astype(F32)
        err = (xh * gfv - t_ref[...]) * live
        loss_ref[...] += jnp.full(loss_ref.shape, 0.5 / D_MODEL, F32) * jnp.sum(err * err)
        dy = err * (1.0 / D_MODEL)
        dgf_ref[...] += jnp.sum(dy * xh, axis=0, keepdims=True)
        dxh = dy * gfv
        dh1 = r1 * (dxh - xh * jnp.mean(dxh * xh, axis=-1, keepdims=True))
        dh1_ref[...] = dh1
        dh1b = dh1.astype(BF16)
        dh1b_ref[...] = dh1b
        dmerged = _dot(dh1b, wo_ref[...], NT)
        dm_ref[:, :D_MODEL] = (dmerged * pr * sr * (1.0 - sr)).astype(BF16)
        dm_ref[:, D_MODEL:] = (dmerged * pg * sg * (1.0 - sg)).astype(BF16)
        dpr = (dmerged * sr).astype(BF16)
        dpg = (dmerged * sg).astype(BF16)
        dprb_ref[...] = dpr
        dpgb_ref[...] = dpg
        dag_ref[...] = _dot(dpg, wbg_ref[...], NT)

    tile = lambda w: pl.BlockSpec((TM, w), lambda i: (i, 0))
    const = lambda r, w: pl.BlockSpec((r, w), lambda i: (0, 0))
    return pl.pallas_call(
        body, name="merge_out_loss", grid=(nt,),
        in_specs=[tile(RET_W), tile(GLA_W), tile(W_M), tile(D_MODEL),
                  pl.BlockSpec((TM, D_MODEL), lambda i: (jnp.maximum(i - 1, 0), 0)),
                  const(RET_W, D_MODEL), const(GLA_W, D_MODEL), const(D_MODEL, D_MODEL), const(1, D_MODEL)],
        out_specs=[tile(D_MODEL), tile(GLA_W), tile(W_M), tile(D_MODEL), tile(D_MODEL), tile(D_MODEL),
                   tile(D_MODEL), const(1, 128), const(1, D_MODEL)],
        out_shape=[jax.ShapeDtypeStruct((tp, D_MODEL), F32), jax.ShapeDtypeStruct((tp, GLA_W), F32),
                   jax.ShapeDtypeStruct((tp, W_M), BF16),
                   jax.ShapeDtypeStruct((tp, D_MODEL), BF16), jax.ShapeDtypeStruct((tp, D_MODEL), BF16),
                   jax.ShapeDtypeStruct((tp, D_MODEL), BF16), jax.ShapeDtypeStruct((tp, D_MODEL), BF16),
                   jax.ShapeDtypeStruct((1, 128), F32), jax.ShapeDtypeStruct((1, D_MODEL), F32)],
        compiler_params=_cparams(1),
    )(a_ret, a_gla, mg, h0, tgt, wbr, wbg, wout, gf)


def _device_step(x2d, tgt2d, meta, norm_gain, w_in_part, w_gate_up, b_gate, ret_gain, gla_gain, branch_parts,
                 final_gain, ck):
    seq = x2d.shape[0]
    tp = T0 + seq
    head = jnp.concatenate([jnp.zeros((PADF, D_MODEL), F32), meta], axis=0)
    wg_pad = jnp.pad(w_gate_up, ((0, 128 - GATE_RANK), (0, 0))).astype(BF16)

    half = RET_QK // 2
    cos, sin = (jnp.asarray(t) for t in _rope_tables(tp))
    lgam = jnp.log1p(-(2.0 ** (-5.0 - jnp.arange(RET_HEADS, dtype=F32))))
    pmat = jnp.asarray(_gla_tril(), BF16)
    pmat_t = jnp.asarray(_gla_tril().T.copy(), BF16)

    h0, u, g_in = _rms_call(x2d, head, norm_gain, _gather_plan([w_in_part], relay=(True,)))
    hr, sw = w_in_part.shape
    w_in_bf = g_in.reshape(4, 2, hr, sw).transpose(1, 2, 0, 3).reshape(2 * hr, 4 * sw)
    w_r = w_in_bf
    w_g = jnp.pad(w_in_bf[:, W_R:W_R + W_G], ((0, 0), (0, W_GP - W_G)))
    w_m = w_in_bf[:, W_R + W_G:]
    tab = pl.BlockSpec((_proj_rows(tp), half), lambda j, i: (i, 0))
    rqk = _mm_nn("proj_rqk", u, w_r, BF16, D_MODEL, 0, 2 * D_MODEL, _rope_epilogue, (cos, sin), (tab, tab))
    rv = _mm_nn("proj_rv", u, w_r, BF16, RET_W, 2 * D_MODEL, RET_W)
    rg = _mm_nn("proj_rg", u, w_r, F32, RET_W, 4 * D_MODEL, RET_W)
    gqk = _mm_nn("proj_gqk", u, w_g, F32, 2 * GLA_KW, 0, 2 * GLA_KW, _gqk_epilogue)
    gv = _mm_nn("proj_gv", u, w_g, BF16, GLA_W, 2 * GLA_KW, GLA_W)
    gg = _mm_nn("proj_gg", u, w_g, F32, GLA_W, 2 * GLA_KW + GLA_W, GLA_W)
    glr = _mm_nn("proj_glr", u, w_g, F32, 128, 2 * GLA_KW + 2 * GLA_W, 128)
    mg = _mm_nn("proj_mg", u, w_m, F32, W_M, 0, W_M)

    o_ret, a_ret, st_ret, sc_ret = _ret_fwd_call(rqk, rv, rg, ret_gain, lgam)
    o_gla, a_gla, st_gla, sc_gla, g_br, g_bg, g_out = _gla_fwd_call(gqk, gv, glr, gg, wg_pad, b_gate, gla_gain, pmat,
                                                            comm=_spread_plan(branch_parts))
    wbr = g_br.reshape(RET_W, D_MODEL)
    wbg = g_bg.reshape(GLA_W, D_MODEL)
    wout = g_out.reshape(D_MODEL, D_MODEL)

    gf = final_gain.reshape(1, D_MODEL)
    (dh1, da_gla, dm, merged_b, dh1_b, dpr_b, dpg_b, loss, dgf) = _mid_call(
        a_ret, a_gla, mg, h0, tgt2d, wbr, wbg, wout, gf)

    names_b = ("w_branch_ret", "w_branch_gla", "w_out")
    g2_b = [_mm_tn("dw_br", a_ret, dpr_b, D_MODEL).reshape(4, 2, RET_W // 8, D_MODEL).transpose(1, 0, 2, 3),
            _mm_tn("dw_bg", a_gla, dpg_b, D_MODEL).reshape(4, 2, GLA_W // 8, D_MODEL).transpose(1, 0, 2, 3),
            _mm_tn("dw_out", merged_b, dh1_b, D_MODEL).reshape(4, 2, D_MODEL // 8, D_MODEL).transpose(1, 0, 2, 3)]
    sib_b = _swap_halves_call("swap_halves_branch", g2_b)
    sum_b = [_add_half_call("add_half_" + nm, g, b, ck) for nm, g, b in zip(names_b, g2_b, sib_b)]
    d_g, dwg, dbg, dgla_gain, *chips_b = _gla_bwd_call(gqk, gv, glr, gg, o_gla, da_gla, st_gla, sc_gla, wg_pad, b_gate,
                                                       gla_gain, pmat, pmat_t, comm=_exchange_plan(sum_b))
    mine = [_add_chips_call("add_chips_" + nm, g, b, p, ck) for nm, g, b, p in zip(names_b, g2_b, sib_b, chips_b)]

    d_r, dret_gain = _ret_bwd_call(rqk, rv, rg, o_ret, dpr_b, wbr, st_ret, sc_ret, ret_gain, lgam, cos, sin)

    dwp = _mm_tn("dw_r", u, d_r, 2 * D_MODEL, out_cols=IN_PAD)
    dwp = _mm_tn("dw_g", u, d_g, D_MODEL, ncols=W_GP - 128, into=dwp, col0=W_R)
    dwp = _mm_tn("dw_glr", u, d_g, 128, ncols=128, bcol0=W_GP - 128, into=dwp, col0=W_R + W_GP - 128)
    g2_in = _place_merge_cols_call(dwp, _mm_tn("dw_m", u, dm, 2 * D_MODEL)).reshape(2, D_MODEL // 2, IN_PAD)

    du, sib_in = _mm_nt_acc("du_g", d_g, w_g, W_GP, comm=_swap_plan([g2_in]))
    sum_in = _add_rows_call("add_half_w_in", g2_in, sib_in, ck)
    du, chips_in = _mm_nt_acc("du_r", d_r, w_r, 2 * D_MODEL, acc_in=du, comm=_exchange_window_plan(sum_in))
    tile = pl.BlockSpec((TB, D_MODEL), lambda i, kk: (i, 0))
    row = pl.BlockSpec((1, D_MODEL), lambda i, kk: (0, 0))
    dx, dmeta, dnorm_gain = _mm_nt_acc(
        "du_m", dm, w_m, W_M, acc_in=du, epilogue=_rms_bwd_epilogue, extras=(h0, norm_gain, dh1),
        extra_specs=(tile, row, tile),
        extra_out_shapes=(jax.ShapeDtypeStruct((seq, D_MODEL), F32), jax.ShapeDtypeStruct((N_META, D_MODEL), F32),
                          jax.ShapeDtypeStruct((1, D_MODEL), F32)),
        extra_out_specs=(ANY, pl.BlockSpec((N_META, D_MODEL), lambda i, kk: (0, 0)), row),
        extra_scratch=(pltpu.VMEM((2, TB, D_MODEL), F32), pltpu.SemaphoreType.DMA((2,))))
    mine = [_add_window_call("add_chips_w_in", g2_in, sib_in, chips_in, ck)] + mine
    full = _join_halves_call("join_halves", mine)

    return dict(loss=loss[0, 0], dx=dx, dmeta=dmeta, norm_gain=dnorm_gain, w_gate_up=dwg[:GATE_RANK], b_gate=dbg,
                ret_norm_gain=dret_gain, gla_norm_gain=dgla_gain, final_norm_gain=dgf.reshape(D_MODEL),
                w_in=full[0], w_branch_ret=full[1], w_branch_gla=full[2], w_out=full[3])


MESH = pl.DeviceIdType.MESH
ANY = pl.BlockSpec(memory_space=pl.ANY)


def _place():
    return lax.axis_index("x"), lax.axis_index("y"), lax.axis_index("c")


def _gather8_call(name, parts):
    comm = _gather_plan(parts)
    n = len(parts)

    def body(*refs):
        begin, finish = comm.make(refs[:n], refs[n:2 * n], refs[-2], refs[-1])
        begin()
        finish()

    return pl.pallas_call(
        body, name=name, out_shape=list(comm.out_shapes), in_specs=[ANY] * n, out_specs=[ANY] * n,
        scratch_shapes=_comm_sems(comm),
    )(*parts)


def _swap_halves_call(name, gs):
    n = len(gs)

    def body(*refs):
        g_refs, b_refs = refs[:n], refs[n:2 * n]
        send_sems, recv_sems = refs[2 * n:]
        x, y, c = _place()
        copies = [pltpu.make_async_remote_copy(
            src_ref=g_refs[t].at[1 - c], dst_ref=b_refs[t], send_sem=send_sems.at[t], recv_sem=recv_sems.at[t],
            device_id=(x, y, 1 - c), device_id_type=MESH) for t in range(n)]
        for cp in copies:
            cp.start()
        for cp in copies:
            cp.wait()

    return pl.pallas_call(
        body, name=name,
        out_shape=[jax.ShapeDtypeStruct(g.shape[1:], g.dtype) for g in gs],
        in_specs=[ANY] * n, out_specs=[ANY] * n,
        scratch_shapes=[pltpu.SemaphoreType.DMA((n,)), pltpu.SemaphoreType.DMA((n,))],
    )(*gs)


def _join_halves_call(name, ts):
    n = len(ts)

    def body(*refs):
        o_refs = refs[n:2 * n]
        send_sems, recv_sems = refs[2 * n:]
        x, y, c = _place()
        copies = [pltpu.make_async_remote_copy(
            src_ref=o_refs[t].at[c], dst_ref=o_refs[t].at[c], send_sem=send_sems.at[t], recv_sem=recv_sems.at[t],
            device_id=(x, y, 1 - c), device_id_type=MESH) for t in range(n)]
        for cp in copies:
            cp.start()
        for t in range(n):
            copies[t].wait_send()
            pltpu.make_async_remote_copy(
                src_ref=o_refs[t].at[c], dst_ref=o_refs[t].at[1 - c], send_sem=send_sems.at[t],
                recv_sem=recv_sems.at[t], device_id=(x, y, 1 - c), device_id_type=MESH).wait_recv()

    return pl.pallas_call(
        body, name=name,
        out_shape=[jax.ShapeDtypeStruct(t.shape, t.dtype) for t in ts],
        in_specs=[ANY] * n, out_specs=[ANY] * n, input_output_aliases={t: t for t in range(n)},
        scratch_shapes=[pltpu.SemaphoreType.DMA((n,)), pltpu.SemaphoreType.DMA((n,))],
    )(*ts)


def _row_block(rows, cols, budget):
    best = 8
    for rb in range(8, rows + 1, 8):
        if rows % rb == 0 and rb * cols * 4 <= budget:
            best = rb
    return best


def _add_half_call(name, g, b, ck):
    _, _, r, cc = g.shape
    rb = _row_block(r, cc, 2 * 1024 * 1024)

    def body(ck_ref, g_ref, b_ref, o_ref):
        o_ref[...] = (g_ref[...] + b_ref[...]).astype(BF16)

    return pl.pallas_call(
        body, name=name,
        grid_spec=pltpu.PrefetchScalarGridSpec(
            num_scalar_prefetch=1, grid=(4, r // rb),
            in_specs=[pl.BlockSpec((None, None, rb, cc), lambda k, i, ck_ref: (ck_ref[0], k, i, 0)),
                      pl.BlockSpec((None, rb, cc), lambda k, i, ck_ref: (k, i, 0))],
            out_specs=pl.BlockSpec((None, rb, cc), lambda k, i, ck_ref: (k, i, 0))),
        out_shape=jax.ShapeDtypeStruct(b.shape, BF16),
        compiler_params=_cparams(2),
    )(ck, g, b)


def _add_rows_call(name, g, b, ck):
    _, r, cc = g.shape
    rb = _row_block(r, cc, 2 * 1024 * 1024)

    def body(ck_ref, g_ref, b_ref, o_ref):
        o_ref[...] = (g_ref[...] + b_ref[...]).astype(BF16)

    return pl.pallas_call(
        body, name=name,
        grid_spec=pltpu.PrefetchScalarGridSpec(
            num_scalar_prefetch=1, grid=(r // rb,),
            in_specs=[pl.BlockSpec((None, rb, cc), lambda i, ck_ref: (ck_ref[0], i, 0)),
                      pl.BlockSpec((rb, cc), lambda i, ck_ref: (i, 0))],
            out_specs=pl.BlockSpec((rb, cc), lambda i, ck_ref: (i, 0))),
        out_shape=jax.ShapeDtypeStruct((r, cc), BF16),
        compiler_params=_cparams(1),
    )(ck, g, b)


def _add_window_call(name, g, b, p, ck):
    _, r, _ = g.shape
    nb, step = WIN_W // 128, WIN_STEP // 128

    def body(ck_ref, g_ref, b_ref, p0_ref, p1_ref, p2_ref, o_ref):
        own = g_ref[...] + b_ref[...]
        o_ref[...] = ((own + p0_ref[...].astype(F32)) + p1_ref[...].astype(F32)) + p2_ref[...].astype(F32)

    def peer(j):
        return pl.BlockSpec((None, r, 128), lambda i, ck_ref: (j, 0, i))

    return pl.pallas_call(
        body, name=name,
        grid_spec=pltpu.PrefetchScalarGridSpec(
            num_scalar_prefetch=1, grid=(nb,),
            in_specs=[pl.BlockSpec((None, r, 128), lambda i, ck_ref: (ck_ref[0], 0, step * ck_ref[1] + i)),
                      pl.BlockSpec((r, 128), lambda i, ck_ref: (0, step * ck_ref[1] + i)),
                      peer(0), peer(1), peer(2)],
            out_specs=pl.BlockSpec((None, r, 128), lambda i, ck_ref: (ck_ref[0], 0, i))),
        out_shape=jax.ShapeDtypeStruct((2, r, WIN_W), F32),
        compiler_params=_cparams(1),
    )(ck, g, b, p, p, p)


def _add_chips_call(name, g, b, p, ck):
    _, _, r, cc = g.shape
    rb = _row_block(r, cc, 2 * 1024 * 1024)

    def body(ck_ref, g_ref, b_ref, p0_ref, p1_ref, p2_ref, o_ref):
        own = g_ref[...] + b_ref[...]
        o_ref[...] = ((own + p0_ref[...].astype(F32)) + p1_ref[...].astype(F32)) + p2_ref[...].astype(F32)

    def peer(j):
        return pl.BlockSpec((None, rb, cc), lambda i, ck_ref: (j, i, 0))

    return pl.pallas_call(
        body, name=name,
        grid_spec=pltpu.PrefetchScalarGridSpec(
            num_scalar_prefetch=1, grid=(r // rb,),
            in_specs=[pl.BlockSpec((None, None, rb, cc), lambda i, ck_ref: (ck_ref[0], ck_ref[1], i, 0)),
                      pl.BlockSpec((None, rb, cc), lambda i, ck_ref: (ck_ref[1], i, 0)),
                      peer(0), peer(1), peer(2)],
            out_specs=pl.BlockSpec((None, rb, cc), lambda i, ck_ref: (ck_ref[0], i, 0))),
        out_shape=jax.ShapeDtypeStruct((2, r, cc), F32),
        compiler_params=_cparams(1),
    )(ck, g, b, p, p, p)


def _sum8_call(name, g):
    def body(g_ref, o_ref):
        acc = g_ref[0]
        for d in range(1, 8):
            acc = acc + g_ref[d]
        o_ref[...] = acc

    return pl.pallas_call(body, name=name, out_shape=jax.ShapeDtypeStruct(g.shape[1:], F32))(g)


def _adamw_call(name, w, g, m, v):
    r, cc = w.shape
    if r % 8 == 0 or r * cc * 4 <= 1024 * 1024:
        rb = _row_block(r, cc, 1024 * 1024) if r % 8 == 0 else r
        grid, spec = (r // rb,), pl.BlockSpec((rb, cc), lambda i: (i, 0))
    else:
        grid, spec = (cc // 128,), pl.BlockSpec((r, 128), lambda i: (0, i))

    def body(w_ref, g_ref, m_ref, v_ref, d_ref, m2_ref, v2_ref):
        gv = g_ref[...]
        m2 = ADAM_B1 * m_ref[...] + (1.0 - ADAM_B1) * gv
        v2 = ADAM_B2 * v_ref[...] + (1.0 - ADAM_B2) * (gv * gv)
        m_hat = m2 / (1.0 - ADAM_B1 ** ADAM_STEP)
        v_hat = v2 / (1.0 - ADAM_B2 ** ADAM_STEP)
        d_ref[...] = -ADAM_LR * (m_hat / (jnp.sqrt(v_hat) + ADAM_EPS) + ADAM_WD * w_ref[...])
        m2_ref[...] = m2
        v2_ref[...] = v2

    return pl.pallas_call(
        body, name=name, grid=grid, in_specs=[spec] * 4, out_specs=[spec] * 3,
        out_shape=[jax.ShapeDtypeStruct((r, cc), F32)] * 3, compiler_params=_cparams(1),
    )(w, g, m, v)


SMALL = (("norm_gain", D_MODEL), ("b_gate", GLA_KW), ("ret_norm_gain", RET_W), ("gla_norm_gain", GLA_W),
         ("final_norm_gain", D_MODEL), ("w_gate_up", GATE_RANK * GLA_KW), ("meta_tokens", N_META * D_MODEL),
         ("loss", 1))


def _pack_rows(vecs, rows):
    flat = jnp.concatenate([v.reshape(-1) for v in vecs])
    return jnp.pad(flat, (0, rows * 128 - flat.shape[0])).reshape(rows, 128)


def kernel(x, meta_tokens, norm_gain, w_in, w_gate_up, b_gate, ret_norm_gain, gla_norm_gain, w_branch_ret, w_branch_gla, w_out, final_norm_gain, loss_target, m_meta_tokens, m_norm_gain, m_w_in, m_w_gate_up, m_b_gate, m_ret_norm_gain, m_gla_norm_gain, m_w_branch_ret, m_w_branch_gla, m_w_out, m_final_norm_gain, v_meta_tokens, v_norm_gain, v_w_in, v_w_gate_up, v_b_gate, v_ret_norm_gain, v_gla_norm_gain, v_w_branch_ret, v_w_branch_gla, v_w_out, v_final_norm_gain):
    xi, yi, ci = _place()
    kme = 2 * xi + yi
    ck = jnp.stack([ci, kme]).astype(jnp.int32)
    sw_in = w_in.shape[2]

    def my_half(a, dtype):
        r, cc = a.shape
        return lax.dynamic_index_in_dim(a.reshape(2, r // 2, cc), ci, 0, keepdims=False).astype(dtype)

    g_meta, g_wg = _gather8_call("gather_small_weights", [my_half(meta_tokens, F32), my_half(w_gate_up[0], F32)])
    branch_parts = [my_half(w_branch_ret[0], BF16), my_half(w_branch_gla[0], BF16), my_half(w_out[0], BF16)]
    meta = g_meta.reshape(4, 2, N_META // 2, D_MODEL // 4).transpose(1, 2, 0, 3).reshape(N_META, D_MODEL)
    wg_full = g_wg.reshape(4, 2, GATE_RANK // 2, GLA_KW // 4).transpose(1, 2, 0, 3).reshape(GATE_RANK, GLA_KW)

    loc = _device_step(x[0], loss_target[0], meta, norm_gain, my_half(w_in[0], BF16), wg_full, b_gate, ret_norm_gain,
                       gla_norm_gain,
                       branch_parts, final_norm_gain, ck)
    names = ("w_in", "w_branch_ret", "w_branch_gla", "w_out")
    full = [loc[nm] for nm in names]
    big_w = dict(w_in=w_in[0], w_branch_ret=w_branch_ret[0], w_branch_gla=w_branch_gla[0], w_out=w_out[0])
    big_m = dict(w_in=m_w_in[0], w_branch_ret=m_w_branch_ret[0], w_branch_gla=m_w_branch_gla[0], w_out=m_w_out[0])
    big_v = dict(w_in=v_w_in[0], w_branch_ret=v_w_branch_ret[0], w_branch_gla=v_w_branch_gla[0], w_out=v_w_out[0])
    grads, deltas, new_m, new_v = {}, {}, {}, {}
    for nm, f in zip(names, full):
        shape = big_w[nm].shape
        if nm == "w_in":
            f = lax.dynamic_slice_in_dim(f, (sw_in - WIN_STEP) * kme, sw_in, axis=2)
        g = f.reshape(shape)
        if nm == "w_in":
            d, m2, v2 = (a.T for a in _adamw_call("adamw_" + nm, big_w[nm].T, g.T, big_m[nm].T, big_v[nm].T))
        else:
            d, m2, v2 = _adamw_call("adamw_" + nm, big_w[nm], g, big_m[nm], big_v[nm])
        grads[nm], deltas[nm], new_m[nm], new_v[nm] = (a.reshape((1,) + shape) for a in (g, d, m2, v2))

    small_g = dict(loc)
    small_g["meta_tokens"] = loc["dmeta"]
    n_small = sum(sz for _, sz in SMALL)
    rows = -(-n_small // 128 // 8) * 8
    (g_small,) = _gather8_call("gather_small_grads", [_pack_rows([small_g[nm] for nm, _ in SMALL], rows)])
    tot = _sum8_call("sum_small_grads", g_small).reshape(-1)
    off = 0
    sg = {}
    for nm, sz in SMALL:
        sg[nm] = tot[off:off + sz]
        off += sz
    loss = sg.pop("loss")[0]
    sg["w_gate_up"] = lax.dynamic_slice_in_dim(sg["w_gate_up"].reshape(GATE_RANK, GLA_KW), kme * (GLA_KW // 4),
                                               GLA_KW // 4, axis=1)
    sg["meta_tokens"] = lax.dynamic_slice_in_dim(sg["meta_tokens"].reshape(N_META, D_MODEL), kme * (D_MODEL // 4),
                                                 D_MODEL // 4, axis=1)
    small_w = dict(norm_gain=norm_gain, b_gate=b_gate, ret_norm_gain=ret_norm_gain, gla_norm_gain=gla_norm_gain,
                   final_norm_gain=final_norm_gain, w_gate_up=w_gate_up, meta_tokens=meta_tokens)
    small_m = dict(norm_gain=m_norm_gain, b_gate=m_b_gate, ret_norm_gain=m_ret_norm_gain,
                   gla_norm_gain=m_gla_norm_gain, final_norm_gain=m_final_norm_gain, w_gate_up=m_w_gate_up,
                   meta_tokens=m_meta_tokens)
    small_v = dict(norm_gain=v_norm_gain, b_gate=v_b_gate, ret_norm_gain=v_ret_norm_gain,
                   gla_norm_gain=v_gla_norm_gain, final_norm_gain=v_final_norm_gain, w_gate_up=v_w_gate_up,
                   meta_tokens=v_meta_tokens)
    for nm in small_w:
        shape = small_w[nm].shape
        as2d = lambda a: a.reshape((-1, shape[-1]))
        grads[nm] = sg[nm].reshape(shape)
        deltas[nm], new_m[nm], new_v[nm] = (a.reshape(shape) for a in _adamw_call(
            "adamw_" + nm, as2d(small_w[nm]), as2d(sg[nm]), as2d(small_m[nm]), as2d(small_v[nm])))

    out_order = ("meta_tokens", "norm_gain", "w_in", "w_gate_up", "b_gate", "ret_norm_gain", "gla_norm_gain",
                 "w_branch_ret", "w_branch_gla", "w_out", "final_norm_gain")
    dx = loc["dx"].reshape(x.shape)
    return (loss, dx, *[grads[nm] for nm in out_order], *[deltas[nm] for nm in out_order],
            *[new_m[nm] for nm in out_order], *[new_v[nm] for nm in out_order])
```

```python
import math
from typing import Callable, NamedTuple

import numpy as np
import jax
import jax.numpy as jnp
from jax import lax
from jax.experimental import pallas as pl
from jax.experimental.pallas import tpu as pltpu

F32 = jnp.float32
BF16 = jnp.bfloat16

D_MODEL = 1024
N_META = 16
EPS = 1e-6
ROPE_BASE = 10000.0
RET_HEADS, RET_QK, RET_V = 4, 256, 512
RET_W = RET_HEADS * RET_V
GLA_HEADS, GLA_K, GLA_V = 4, 128, 256
GLA_W = GLA_HEADS * GLA_V
GLA_KW = GLA_HEADS * GLA_K
GATE_RANK = 16
GATE_TAU = 16.0
GLA_SUB = 16

TM = 256
T0 = TM
PADF = T0 - N_META
GC = 128
TB = 768
TK = 768

W_R = 6144
W_G = 3088
W_GP = 3200
W_M = 2048
IN_COLS = W_R + W_G + W_M
WIN_STEP = (IN_COLS // 4) // 128 * 128
WIN_W = -(-(3 * (IN_COLS // 4 - WIN_STEP) + IN_COLS // 4) // 128) * 128
IN_PAD = 3 * WIN_STEP + WIN_W

ADAM_LR, ADAM_B1, ADAM_B2, ADAM_EPS, ADAM_WD, ADAM_STEP = 0.001, 0.9, 0.999, 1e-08, 0.01, 10

VMEM_LIMIT = 56 * 1024 * 1024

NN = ((1,), (0,))
NT = ((1,), (1,))
TN = ((0,), (0,))


def _dot(a, b, dims):
    return lax.dot_general(a, b, (dims, ((), ())), preferred_element_type=F32)


def _cparams(n_axes):
    return pltpu.CompilerParams(dimension_semantics=("arbitrary",) * n_axes, vmem_limit_bytes=VMEM_LIMIT)


def _sigmoid(x):
    return 0.5 * jnp.tanh(0.5 * x) + 0.5


def _split3(x):
    hi = x.astype(BF16)
    r1 = x - hi.astype(F32)
    mid = r1.astype(BF16)
    lo = (r1 - mid.astype(F32)).astype(BF16)
    return hi, mid, lo


def _exact_pm(p, x):
    hi, mid, lo = _split3(x)
    return _dot(p, hi, NN) + _dot(p, mid, NN) + _dot(p, lo, NN)


def _rms_call(x2d, head, gain, comm):
    tp = T0 + x2d.shape[0]
    nt = tp // TM
    n_xc = len(comm.srcs)

    def body(x_ref, hd_ref, g_ref, *rest):
        xc_src = rest[:n_xc]
        h_ref, u_ref = rest[n_xc:n_xc + 2]
        xc_dst = rest[n_xc + 2:2 * n_xc + 2]
        i = pl.program_id(0)
        begin, finish = comm.make(xc_src, xc_dst, rest[-2], rest[-1])
        pl.when(i == 0)(begin)
        h = jnp.where(i == 0, hd_ref[...], x_ref[...])
        h_ref[...] = h
        r = lax.rsqrt(jnp.mean(h * h, axis=-1, keepdims=True) + EPS)
        u_ref[...] = (h * r * g_ref[...]).astype(BF16)
        pl.when(i == nt - 1)(finish)

    tile = pl.BlockSpec((TM, D_MODEL), lambda i: (i, 0))
    return pl.pallas_call(
        body, name="rms_in", grid=(nt,),
        in_specs=[pl.BlockSpec((TM, D_MODEL), lambda i: (jnp.maximum(i - 1, 0), 0)),
                  pl.BlockSpec((T0, D_MODEL), lambda i: (0, 0)), pl.BlockSpec((1, D_MODEL), lambda i: (0, 0))]
        + [ANY] * n_xc,
        out_specs=[tile, tile] + [ANY] * n_xc,
        out_shape=[jax.ShapeDtypeStruct((tp, D_MODEL), F32), jax.ShapeDtypeStruct((tp, D_MODEL), BF16)]
        + list(comm.out_shapes),
        scratch_shapes=_comm_sems(comm), compiler_params=_cparams(1),
    )(x2d, head, gain, *comm.srcs)


PROJ_ROWS_MAX = 1408


def _proj_rows(m):
    return max(r for r in range(16, PROJ_ROWS_MAX + 1, 16) if m % r == 0)


def _mm_nn(name, a, b, out_dtype, tn, col0, ncols, epilogue=None, extras=(), extra_specs=()):
    m, k = a.shape
    nj, j0 = ncols // tn, col0 // tn
    tb = _proj_rows(m)

    def body(a_ref, b_ref, *rest):
        *ex, o_ref = rest
        acc = _dot(a_ref[...], b_ref[...], NN)
        if epilogue is None:
            o_ref[...] = acc.astype(out_dtype)
        else:
            epilogue(acc, o_ref, *ex)

    return pl.pallas_call(
        body, name=name, grid=(nj, m // tb),
        in_specs=[pl.BlockSpec((tb, k), lambda j, i: (i, 0)), pl.BlockSpec((k, tn), lambda j, i: (0, j0 + j))]
        + list(extra_specs),
        out_specs=pl.BlockSpec((tb, tn), lambda j, i: (i, j)),
        out_shape=jax.ShapeDtypeStruct((m, ncols), out_dtype),
        compiler_params=_cparams(2),
    )(a, b, *extras)


def _rope_tables(tp):
    half = RET_QK // 2
    pos = np.arange(tp, dtype=np.float32) - np.float32(PADF)
    inv = (ROPE_BASE ** (-np.arange(half, dtype=np.float64) / half)).astype(np.float32)
    ang = (pos[:, None] * inv[None, :]).astype(np.float64)
    return np.cos(ang).astype(np.float32), np.sin(ang).astype(np.float32)


def _rope_epilogue(acc, o_ref, cos_ref, sin_ref):
    scale = jnp.where(pl.program_id(0) == 1, RET_QK ** -0.5, 1.0).astype(F32)
    cos, sin = cos_ref[...], sin_ref[...]
    half = RET_QK // 2
    for h in range(RET_HEADS):
        t1 = acc[:, h * RET_QK:h * RET_QK + half]
        t2 = acc[:, h * RET_QK + half:(h + 1) * RET_QK]
        o_ref[:, h * RET_QK:h * RET_QK + half] = ((t1 * cos - t2 * sin) * scale).astype(BF16)
        o_ref[:, h * RET_QK + half:(h + 1) * RET_QK] = ((t2 * cos + t1 * sin) * scale).astype(BF16)


def _gqk_epilogue(acc, o_ref):
    o_ref[:, :GLA_KW] = acc[:, :GLA_KW] * (GLA_K ** -0.5)
    o_ref[:, GLA_KW:] = acc[:, GLA_KW:]


class _Comm(NamedTuple):
    srcs: tuple
    out_shapes: tuple
    n_sems: int
    make: Callable


def _comm_sems(comm):
    return [pltpu.SemaphoreType.DMA((comm.n_sems,)), pltpu.SemaphoreType.DMA((comm.n_sems,))]


def _start_wait(copies):
    def begin():
        for cp in copies:
            cp.start()

    def finish():
        for cp in copies:
            cp.wait()

    return begin, finish


def _other_chips(x, y):
    return [(1 - x, y), (x, 1 - y), (1 - x, 1 - y)]


def _gather_plan(parts, relay=()):
    n = len(parts)
    relay = tuple(relay) + (False,) * (n - len(relay))

    def make(x_refs, out_refs, send_sems, recv_sems):
        x, y, c = _place()
        me, sibling = (x, y, c), (x, y, 1 - c)
        xn, yn, dg = (1 - x, y), (x, 1 - y), (1 - x, 1 - y)

        def slot(t, px, py, pc, half=None):
            ref = out_refs[t].at[4 * px + 2 * py + pc]
            if half is None:
                return ref
            rows = ref.shape[0] // 2
            return ref.at[pl.ds(half * rows, rows)]

        def copy(t, k, dst, to, src=None):
            return pltpu.make_async_remote_copy(
                src_ref=dst if src is None else src, dst_ref=dst, send_sem=send_sems.at[8 * t + k],
                recv_sem=recv_sems.at[8 * t + k], device_id=to, device_id_type=MESH)

        mine = [pltpu.make_async_copy(x_refs[t], slot(t, *me), send_sems.at[8 * n + t]) for t in range(n)]
        sent = []
        for t in range(n):
            sent.append(copy(t, 0, slot(t, *me), sibling, src=x_refs[t]))
            sent.append(copy(t, 1, slot(t, *me), (*xn, c), src=x_refs[t]))
            sent.append(copy(t, 2, slot(t, *me), (*yn, c), src=x_refs[t]))
            if not relay[t]:
                sent.append(copy(t, 3, slot(t, *me), (*dg, c), src=x_refs[t]))

        def begin():
            for cp in mine + sent:
                cp.start()

        def finish():
            later = []

            def start(cp):
                cp.start()
                later.append(cp)

            for t in range(n):
                copy(t, 2, slot(t, *yn, c), me).wait_recv()
                if relay[t]:
                    start(copy(t, 3, slot(t, *yn, c, half=0), (*xn, c)))
                start(copy(t, 6, slot(t, *yn, c), sibling))
            for t in range(n):
                copy(t, 1, slot(t, *xn, c), me).wait_recv()
                if relay[t]:
                    start(copy(t, 4, slot(t, *xn, c, half=1), (*yn, c)))
                start(copy(t, 5, slot(t, *xn, c), sibling))
            for t in range(n):
                if relay[t]:
                    copy(t, 3, slot(t, *dg, c, half=0), me).wait_recv()
                    copy(t, 4, slot(t, *dg, c, half=1), me).wait_recv()
                else:
                    copy(t, 3, slot(t, *dg, c), me).wait_recv()
                start(copy(t, 7, slot(t, *dg, c), sibling))
            for t in range(n):
                copy(t, 0, slot(t, *sibling), me).wait_recv()
                copy(t, 5, slot(t, *xn, 1 - c), me).wait_recv()
                copy(t, 6, slot(t, *yn, 1 - c), me).wait_recv()
                copy(t, 7, slot(t, *dg, 1 - c), me).wait_recv()
            for cp in sent + later:
                cp.wait_send()
            for cp in mine:
                cp.wait()

        return begin, finish

    return _Comm(tuple(parts), tuple(jax.ShapeDtypeStruct((8,) + p.shape, p.dtype) for p in parts), 9 * n, make)


def _exchange_plan(ss):
    def make(s_refs, b_refs, send_sems, recv_sems):
        x, y, c = _place()
        return _start_wait([pltpu.make_async_remote_copy(
            src_ref=s_refs[t].at[2 * chip[0] + chip[1]], dst_ref=b_refs[t].at[j], send_sem=send_sems.at[3 * t + j],
            recv_sem=recv_sems.at[3 * t + j], device_id=(*chip, c), device_id_type=MESH)
            for t in range(len(s_refs)) for j, chip in enumerate(_other_chips(x, y))])

    return _Comm(tuple(ss), tuple(jax.ShapeDtypeStruct((3,) + s.shape[1:], s.dtype) for s in ss), 3 * len(ss), make)


def _exchange_window_plan(s):
    def make(s_refs, b_refs, send_sems, recv_sems):
        x, y, c = _place()
        return _start_wait([pltpu.make_async_remote_copy(
            src_ref=s_refs[0].at[:, pl.ds(pl.multiple_of((2 * chip[0] + chip[1]) * WIN_STEP, 128), WIN_W)],
            dst_ref=b_refs[0].at[j], send_sem=send_sems.at[j], recv_sem=recv_sems.at[j], device_id=(*chip, c),
            device_id_type=MESH) for j, chip in enumerate(_other_chips(x, y))])

    return _Comm((s,), (jax.ShapeDtypeStruct((3, s.shape[0], WIN_W), s.dtype),), 3, make)


def _swap_plan(gs):
    def make(g_refs, b_refs, send_sems, recv_sems):
        x, y, c = _place()
        return _start_wait([pltpu.make_async_remote_copy(
            src_ref=g_refs[t].at[1 - c], dst_ref=b_refs[t], send_sem=send_sems.at[t], recv_sem=recv_sems.at[t],
            device_id=(x, y, 1 - c), device_id_type=MESH) for t in range(len(g_refs))])

    return _Comm(tuple(gs), tuple(jax.ShapeDtypeStruct(g.shape[1:], g.dtype) for g in gs), len(gs), make)


def _spread_plan(parts):
    def make(p_refs, o_refs, send_sems, recv_sems):
        x, y, c = _place()
        copies = []
        for t in range(len(p_refs)):
            mine = o_refs[t].at[4 * x + 2 * y + c]
            copies.append(pltpu.make_async_copy(p_refs[t], mine, send_sems.at[7 * len(p_refs) + t]))
            for r in range(1, 8):
                peer = (1 - x if r & 4 else x, 1 - y if r & 2 else y, 1 - c if r & 1 else c)
                copies.append(pltpu.make_async_remote_copy(
                    src_ref=p_refs[t], dst_ref=mine, send_sem=send_sems.at[7 * t + r - 1],
                    recv_sem=recv_sems.at[7 * t + r - 1], device_id=peer, device_id_type=MESH))
        return _start_wait(copies)

    return _Comm(tuple(parts), tuple(jax.ShapeDtypeStruct((8,) + p.shape, p.dtype) for p in parts), 8 * len(parts),
                 make)


def _mm_nt_acc(name, a, w, tk, acc_in=None, epilogue=None, extras=(), extra_specs=(), extra_out_shapes=(),
               extra_out_specs=(), extra_scratch=(), comm=None, tb=TB):
    m, k = a.shape
    n = w.shape[0]
    nk, ni = k // tk, m // tb
    has_acc = acc_in is not None
    n_xc = len(comm.srcs) if comm else 0
    n_es = len(extra_scratch)

    def body(*refs):
        a_ref, w_ref = refs[0], refs[1]
        pos = 2
        acc_ref = None
        if has_acc:
            acc_ref = refs[pos]
            pos += 1
        ex = refs[pos:pos + len(extras)]
        pos += len(extras)
        xc_src = refs[pos:pos + n_xc]
        pos += n_xc
        n_scr = 1 + n_es + (2 if n_xc else 0)
        outs = refs[pos:len(refs) - n_scr - n_xc]
        xc_dst = refs[len(refs) - n_scr - n_xc:len(refs) - n_scr]
        scr = refs[len(refs) - n_scr]
        es = refs[len(refs) - n_scr + 1:len(refs) - n_scr + 1 + n_es]
        i, kk = pl.program_id(0), pl.program_id(1)
        if n_xc:
            begin, finish = comm.make(xc_src, xc_dst, refs[-2], refs[-1])
            pl.when((i == 0) & (kk == 0))(begin)

        @pl.when(kk == 0)
        def _():
            scr[...] = acc_ref[...] if has_acc else jnp.zeros_like(scr)

        scr[...] += _dot(a_ref[...], w_ref[...], NT)

        @pl.when(kk == nk - 1)
        def _():
            if epilogue is None:
                outs[0][...] = scr[...]
            else:
                epilogue(scr[...], outs, i, ni, *ex, *es)

        if n_xc:
            pl.when((i == ni - 1) & (kk == nk - 1))(finish)

    in_specs = [pl.BlockSpec((tb, tk), lambda i, kk: (i, kk)), pl.BlockSpec((n, tk), lambda i, kk: (0, kk))]
    args = [a, w]
    if has_acc:
        in_specs.append(pl.BlockSpec((tb, n), lambda i, kk: (i, 0)))
        args.append(acc_in)
    in_specs += list(extra_specs) + [ANY] * n_xc
    args += list(extras) + (list(comm.srcs) if comm else [])
    if epilogue is None:
        out_shape = [jax.ShapeDtypeStruct((m, n), F32)]
        out_specs = [pl.BlockSpec((tb, n), lambda i, kk: (i, 0))]
    else:
        out_shape, out_specs = list(extra_out_shapes), list(extra_out_specs)
    scratch = [pltpu.VMEM((tb, n), F32)] + list(extra_scratch)
    if n_xc:
        out_shape += list(comm.out_shapes)
        out_specs += [ANY] * n_xc
        scratch += _comm_sems(comm)
    return pl.pallas_call(
        body, name=name, grid=(ni, nk), in_specs=in_specs, out_specs=out_specs, out_shape=out_shape,
        scratch_shapes=scratch, compiler_params=_cparams(2),
    )(*args)


def _rms_bwd_epilogue(du, outs, i, ni, h_ref, g_ref, dh1_ref, obuf, sems):
    dx_ref, dmeta_ref, dg_ref = outs
    h = h_ref[...]
    r = lax.rsqrt(jnp.mean(h * h, axis=-1, keepdims=True) + EPS)
    xh = h * r
    dxh = du * g_ref[...]
    dh0 = dh1_ref[...] + r * (dxh - xh * jnp.mean(dxh * xh, axis=-1, keepdims=True))

    def put(slot, tile):
        return pltpu.make_async_copy(obuf.at[slot], dx_ref.at[pl.ds(pl.multiple_of(tile * TB - T0, 8), TB)],
                                     sems.at[slot])

    @pl.when(i == 0)
    def _():
        dg_ref[...] = jnp.zeros_like(dg_ref)
        dmeta_ref[...] = dh0[PADF:T0, :]
        obuf[0] = dh0
        first = pltpu.make_async_copy(obuf.at[0, pl.ds(T0, TB - T0)], dx_ref.at[pl.ds(0, TB - T0)], sems.at[0])
        first.start()
        first.wait()

    @pl.when(i >= 1)
    def _():
        slot = i % 2

        @pl.when(i >= 3)
        def _():
            put(slot, i - 2).wait()

        obuf[slot] = dh0
        put(slot, i).start()

    dg_ref[...] += jnp.sum(du * xh, axis=0, keepdims=True)

    @pl.when(i == ni - 1)
    def _():
        for tile in (ni - 2, ni - 1):
            if tile >= 1:
                put(tile % 2, tile).wait()


def _mm_tn(name, a, b, bn, ncols=None, bcol0=0, into=None, col0=0, out_cols=None):
    t, m = a.shape
    n = ncols or b.shape[1]
    j0, bj0 = col0 // bn, bcol0 // bn

    def body(a_ref, b_ref, *rest):
        o_ref = rest[-1]

        @pl.when(pl.program_id(1) == 0)
        def _():
            o_ref[...] = jnp.zeros_like(o_ref)

        o_ref[...] += _dot(a_ref[...], b_ref[...], TN)

    in_specs = [pl.BlockSpec((TK, m), lambda j, kk: (kk, 0)), pl.BlockSpec((TK, bn), lambda j, kk: (kk, bj0 + j))]
    args = [a, b]
    aliases = {}
    if into is not None:
        in_specs.append(ANY)
        args.append(into)
        aliases = {2: 0}
        out_cols = into.shape[1]
    return pl.pallas_call(
        body, name=name, grid=(n // bn, t // TK), in_specs=in_specs,
        out_specs=pl.BlockSpec((m, bn), lambda j, kk: (0, j0 + j)),
        out_shape=jax.ShapeDtypeStruct((m, out_cols or n), F32), input_output_aliases=aliases,
        compiler_params=_cparams(2),
    )(*args)


def _place_merge_cols_call(dwp, dw_m):
    c0 = W_R + W_GP - 128
    tail = IN_PAD - c0
    rows = 256

    def body(m_ref, p_ref, o_ref, buf, low, sem):
        get = pltpu.make_async_copy(o_ref.at[:, pl.ds(c0, 128)], low, sem)
        get.start()
        get.wait()
        for r in range(0, D_MODEL, rows):
            buf[r:r + rows, :] = jnp.concatenate(
                [low[r:r + rows, :GATE_RANK], m_ref[r:r + rows, :],
                 jnp.zeros((rows, tail - GATE_RANK - W_M), F32)], axis=1)
        put = pltpu.make_async_copy(buf, o_ref.at[:, pl.ds(c0, tail)], sem)
        put.start()
        put.wait()

    return pl.pallas_call(
        body, name="place_merge_cols",
        in_specs=[pl.BlockSpec(memory_space=pltpu.VMEM), ANY], out_specs=ANY,
        out_shape=jax.ShapeDtypeStruct(dwp.shape, F32), input_output_aliases={1: 0},
        scratch_shapes=[pltpu.VMEM((D_MODEL, tail), F32), pltpu.VMEM((D_MODEL, 128), F32), pltpu.SemaphoreType.DMA],
        compiler_params=pltpu.CompilerParams(vmem_limit_bytes=VMEM_LIMIT),
    )(dw_m, dwp)


def _ret_fill_decay(lg_ref, dm_scr):
    c = TM
    ii = lax.broadcasted_iota(jnp.int32, (c, c), 0)
    jj = lax.broadcasted_iota(jnp.int32, (c, c), 1)
    rel = (ii - jj).astype(F32)
    for h in range(RET_HEADS):
        dm_scr[h] = jnp.where(rel >= 0, jnp.exp(jnp.maximum(rel, 0.0) * lg_ref[h]), 0.0)


def _ret_consts(lg, dm_ref):
    c = TM
    idx = lax.broadcasted_iota(jnp.int32, (c, 1), 0).astype(F32)
    xi = jnp.exp((idx + 1.0) * lg)
    zeta = jnp.exp((c - 1.0 - idx) * lg)
    gc = jnp.exp(jnp.full((1, 1), c, F32) * lg)
    return dm_ref[...], xi, zeta, gc


def _ret_fwd_call(rqk, rv, rg, gain, lgam):
    tp = rqk.shape[0]
    nc = tp // TM

    def body(lg_ref, qk_ref, v_ref, rg_ref, g_ref, o_ref, a_ref, st_ref, sc_ref, s_scr, dm_scr):
        @pl.when(pl.program_id(0) == 0)
        def _():
            s_scr[...] = jnp.zeros_like(s_scr)
            _ret_fill_decay(lg_ref, dm_scr)

        for h in range(RET_HEADS):
            dm, xi, zeta, gc = _ret_consts(lg_ref[h], dm_scr.at[h])
            q = qk_ref[:, h * RET_QK:(h + 1) * RET_QK]
            k = qk_ref[:, D_MODEL + h * RET_QK:D_MODEL + (h + 1) * RET_QK]
            v = v_ref[:, h * RET_V:(h + 1) * RET_V]
            sb = s_scr[h].astype(BF16)
            st_ref[0, h] = sb
            s = (_dot(q, k, NT) * dm).astype(BF16)
            sc_ref[0, h] = s
            o = _dot(s, v, NN) + xi * _dot(q, sb, NN)
            kz = (k.astype(F32) * zeta).astype(BF16)
            s_scr[h] = gc * s_scr[h] + _dot(kz, v, TN)
            o_ref[:, h * RET_V:(h + 1) * RET_V] = o
            mu = jnp.mean(o, axis=-1, keepdims=True)
            xc = o - mu
            xh = xc * lax.rsqrt(jnp.mean(xc * xc, axis=-1, keepdims=True) + EPS)
            g = rg_ref[:, h * RET_V:(h + 1) * RET_V]
            a_ref[:, h * RET_V:(h + 1) * RET_V] = (
                xh * g_ref[:, h * RET_V:(h + 1) * RET_V] * (g * _sigmoid(g))).astype(BF16)

    return pl.pallas_call(
        body, name="ret_fwd", grid=(nc,),
        in_specs=[pl.BlockSpec(memory_space=pltpu.SMEM),
                  pl.BlockSpec((TM, 2 * D_MODEL), lambda n: (n, 0)),
                  pl.BlockSpec((TM, RET_W), lambda n: (n, 0)),
                  pl.BlockSpec((TM, RET_W), lambda n: (n, 0)),
                  pl.BlockSpec((1, RET_W), lambda n: (0, 0))],
        out_specs=[pl.BlockSpec((TM, RET_W), lambda n: (n, 0)),
                   pl.BlockSpec((TM, RET_W), lambda n: (n, 0)),
                   pl.BlockSpec((1, RET_HEADS, RET_QK, RET_V), lambda n: (n, 0, 0, 0)),
                   pl.BlockSpec((1, RET_HEADS, TM, TM), lambda n: (n, 0, 0, 0))],
        out_shape=[jax.ShapeDtypeStruct((tp, RET_W), F32), jax.ShapeDtypeStruct((tp, RET_W), BF16),
                   jax.ShapeDtypeStruct((nc, RET_HEADS, RET_QK, RET_V), BF16),
                   jax.ShapeDtypeStruct((nc, RET_HEADS, TM, TM), BF16)],
        scratch_shapes=[pltpu.VMEM((RET_HEADS, RET_QK, RET_V), F32), pltpu.VMEM((RET_HEADS, TM, TM), F32)],
        compiler_params=_cparams(1),
    )(lgam, rqk, rv, rg, gain)


def _ret_bwd_call(rqk, rv, rg, o_ret, dpr, wbr, states, scores, gain, lgam, cos, sin):
    tp = rqk.shape[0]
    nc = tp // TM
    half = RET_QK // 2

    def body(lg_ref, qk_ref, v_ref, rg_ref, o_ref, dpr_ref, wbr_ref, st_ref, sc_ref, g_ref, cos_ref, sin_ref, dp_ref,
             dg_ref, ds_scr, dm_scr):
        @pl.when(pl.program_id(0) == 0)
        def _():
            ds_scr[...] = jnp.zeros_like(ds_scr)
            dg_ref[...] = jnp.zeros_like(dg_ref)
            _ret_fill_decay(lg_ref, dm_scr)

        cos, sin = cos_ref[...], sin_ref[...]
        for h in range(RET_HEADS):
            hs = slice(h * RET_V, (h + 1) * RET_V)
            dm, xi, zeta, gc = _ret_consts(lg_ref[h], dm_scr.at[h])
            o = o_ref[:, hs]
            mu = jnp.mean(o, axis=-1, keepdims=True)
            xc = o - mu
            rstd = lax.rsqrt(jnp.mean(xc * xc, axis=-1, keepdims=True) + EPS)
            xh = xc * rstd
            gain_h = g_ref[:, hs]
            g = rg_ref[:, hs]
            sg = _sigmoid(g)
            silu = g * sg
            dah = _dot(dpr_ref[...], wbr_ref[hs, :], NT)
            dp_ref[:, 4 * D_MODEL + h * RET_V:4 * D_MODEL + (h + 1) * RET_V] = (
                dah * (xh * gain_h) * (sg * (1.0 + g * (1.0 - sg)))).astype(BF16)
            dn = dah * silu
            dg_ref[:, hs] += jnp.sum(dn * xh, axis=0, keepdims=True)
            dxh = dn * gain_h
            do = rstd * (dxh - jnp.mean(dxh, axis=-1, keepdims=True)
                         - xh * jnp.mean(dxh * xh, axis=-1, keepdims=True))
            dob = do.astype(BF16)
            q = qk_ref[:, h * RET_QK:(h + 1) * RET_QK]
            k = qk_ref[:, D_MODEL + h * RET_QK:D_MODEL + (h + 1) * RET_QK]
            v = v_ref[:, hs]
            sp = st_ref[0, h]
            ds = ds_scr[h]
            dsb = ds.astype(BF16)
            s = sc_ref[0, h]
            dsc = (_dot(dob, v, NT) * dm).astype(BF16)
            dq = _dot(dsc, k, NN) + xi * _dot(dob, sp, NT)
            dk = _dot(dsc, q, TN) + zeta * _dot(v, dsb, NT)
            kz = (k.astype(F32) * zeta).astype(BF16)
            dv = _dot(s, dob, TN) + _dot(kz, dsb, NN)
            qx = (q.astype(F32) * xi).astype(BF16)
            ds_scr[h] = gc * ds + _dot(qx, dob, TN)
            dp_ref[:, 2 * D_MODEL + h * RET_V:2 * D_MODEL + (h + 1) * RET_V] = dv.astype(BF16)
            dk = dk * (RET_QK ** -0.5)
            for base, t in ((0, dq), (D_MODEL, dk)):
                t1, t2 = t[:, :half], t[:, half:]
                dp_ref[:, base + h * RET_QK:base + h * RET_QK + half] = (t1 * cos + t2 * sin).astype(BF16)
                dp_ref[:, base + h * RET_QK + half:base + (h + 1) * RET_QK] = (t2 * cos - t1 * sin).astype(BF16)

    rev = lambda n: (nc - 1 - n, 0)
    return pl.pallas_call(
        body, name="ret_bwd", grid=(nc,),
        in_specs=[pl.BlockSpec(memory_space=pltpu.SMEM),
                  pl.BlockSpec((TM, 2 * D_MODEL), rev),
                  pl.BlockSpec((TM, RET_W), rev),
                  pl.BlockSpec((TM, RET_W), rev),
                  pl.BlockSpec((TM, RET_W), rev),
                  pl.BlockSpec((TM, D_MODEL), rev),
                  pl.BlockSpec((RET_W, D_MODEL), lambda n: (0, 0)),
                  pl.BlockSpec((1, RET_HEADS, RET_QK, RET_V), lambda n: (nc - 1 - n, 0, 0, 0)),
                  pl.BlockSpec((1, RET_HEADS, TM, TM), lambda n: (nc - 1 - n, 0, 0, 0)),
                  pl.BlockSpec((1, RET_W), lambda n: (0, 0)),
                  pl.BlockSpec((TM, half), rev),
                  pl.BlockSpec((TM, half), rev)],
        out_specs=[pl.BlockSpec((TM, W_R), rev), pl.BlockSpec((1, RET_W), lambda n: (0, 0))],
        out_shape=[jax.ShapeDtypeStruct((tp, W_R), BF16), jax.ShapeDtypeStruct((1, RET_W), F32)],
        scratch_shapes=[pltpu.VMEM((RET_HEADS, RET_QK, RET_V), F32), pltpu.VMEM((RET_HEADS, TM, TM), F32)],
        compiler_params=_cparams(1),
    )(lgam, rqk, rv, rg, o_ret, dpr, wbr, states, scores, gain, cos, sin)


GLA_LEVELS = tuple(GC >> (s + 1) for s in range(int(math.log2(GC // GLA_SUB))))
NLEV = len(GLA_LEVELS)


def _gla_tril():
    return np.tril(np.ones((GC, GC), np.float32))


def _gla_masks():
    ii = lax.broadcasted_iota(jnp.int32, (GC, GC), 0)
    jj = lax.broadcasted_iota(jnp.int32, (GC, GC), 1)
    masks = []
    for m in GLA_LEVELS:
        sh = int(math.log2(2 * m))
        masks.append(((ii >> sh) == (jj >> sh)) & ((ii & m) != 0) & ((jj & m) == 0))
    sh = int(math.log2(GLA_SUB))
    md = ((ii >> sh) == (jj >> sh)) & (jj <= ii)
    row = lax.broadcasted_iota(jnp.int32, (GC, 1), 0)
    second = [(row & m) != 0 for m in GLA_LEVELS]
    return masks, md, second


def _gla_log_decay(glr_ref, wg_ref, bg_ref):
    z = _dot(glr_ref[...].astype(BF16), wg_ref[...], NN) + bg_ref[...]
    la = (jnp.minimum(z, 0.0) - jnp.log1p(jnp.exp(-jnp.abs(z)))) * (1.0 / GATE_TAU)
    return z, la


def _gla_row_steps(b_ref, cs, rows, size):
    parts = [jnp.zeros((size, GLA_K), F32) if r is None else jnp.broadcast_to(b_ref[r:r + 1, cs], (size, GLA_K))
             for r in rows]
    return parts[0] if len(parts) == 1 else jnp.concatenate(parts, axis=0)


def _gla_factors(b_ref, h, second):
    cs = slice(h * GLA_K, (h + 1) * GLA_K)
    b = b_ref[:, cs]
    fq, fk = [], []
    for l, m in enumerate(GLA_LEVELS):
        d = b - _gla_row_steps(b_ref, cs, [s + m - 1 for s in range(0, GC, 2 * m)], 2 * m)
        f = jnp.exp(jnp.where(second[l], d, -d))
        fq.append(jnp.where(second[l], f, 0.0))
        fk.append(jnp.where(second[l], 0.0, f))
    dd = b - _gla_row_steps(b_ref, cs, [None] + [s - 1 for s in range(GLA_SUB, GC, GLA_SUB)], GLA_SUB)
    ed = jnp.exp(dd)
    edi = jnp.exp(-dd)
    eb = jnp.exp(b)
    bl = b_ref[GC - 1:GC, cs]
    ee = jnp.exp(bl - b)
    ebl = jnp.exp(bl)
    return fq, fk, ed, edi, eb, ee, ebl


def _gla_scaled(q, k, fq, fk, ed, edi):
    qt = [(q * f).astype(BF16) for f in fq]
    kt = [(k * f).astype(BF16) for f in fk]
    return qt, kt, (q * ed).astype(BF16), (k * edi).astype(BF16)


def _gla_scores(qt, kt, qd, kd, masks, md):
    a = jnp.where(md, _dot(qd, kd, NT), 0.0)
    for l in range(NLEV):
        a = a + jnp.where(masks[l], _dot(qt[l], kt[l], NT), 0.0)
    return a.astype(BF16)


def _gla_fwd_call(gqk, gv, glr, gg, wg, bg, gain, pmat, comm=None):
    tp = gqk.shape[0]
    nc = tp // GC
    n_xc = len(comm.srcs) if comm else 0

    def body(qk_ref, v_ref, glr_ref, gg_ref, wg_ref, bg_ref, g_ref, p_ref, *rest):
        xc_src = rest[:n_xc]
        o_ref, a_ref, st_ref, am_ref = rest[n_xc:n_xc + 4]
        xc_dst = rest[n_xc + 4:2 * n_xc + 4]
        s_scr, b_scr = rest[2 * n_xc + 4:2 * n_xc + 6]
        n = pl.program_id(0)
        if n_xc:
            begin, finish = comm.make(xc_src, xc_dst, rest[-2], rest[-1])
            pl.when(n == 0)(begin)
            pl.when(n == nc - 1)(finish)

        @pl.when(n == 0)
        def _():
            s_scr[...] = jnp.zeros_like(s_scr)

        _, la = _gla_log_decay(glr_ref, wg_ref, bg_ref)
        b_scr[...] = _exact_pm(p_ref[...], la)
        masks, md, second = _gla_masks()
        for h in range(GLA_HEADS):
            q = qk_ref[:, h * GLA_K:(h + 1) * GLA_K]
            k = qk_ref[:, GLA_KW + h * GLA_K:GLA_KW + (h + 1) * GLA_K]
            vs = slice(h * GLA_V, (h + 1) * GLA_V)
            v = v_ref[:, vs]
            fq, fk, ed, edi, eb, ee, ebl = _gla_factors(b_scr, h, second)
            a = _gla_scores(*_gla_scaled(q, k, fq, fk, ed, edi), masks, md)
            am_ref[0, h] = a
            sb = s_scr[h].astype(BF16)
            st_ref[0, h] = sb
            o = _dot(a, v, NN) + _dot((q * eb).astype(BF16), sb, NT)
            s_scr[h] = s_scr[h] * ebl + _dot(v, (k * ee).astype(BF16), TN)
            o_ref[:, vs] = o
            xh = o * lax.rsqrt(jnp.mean(o * o, axis=-1, keepdims=True) + EPS)
            g = gg_ref[:, vs]
            a_ref[:, vs] = (xh * g_ref[:, vs] * (g * _sigmoid(g))).astype(BF16)

    return pl.pallas_call(
        body, name="gla_fwd", grid=(nc,),
        in_specs=[pl.BlockSpec((GC, 2 * GLA_KW), lambda n: (n, 0)),
                  pl.BlockSpec((GC, GLA_W), lambda n: (n, 0)),
                  pl.BlockSpec((GC, 128), lambda n: (n, 0)),
                  pl.BlockSpec((GC, GLA_W), lambda n: (n, 0)),
                  pl.BlockSpec((128, GLA_KW), lambda n: (0, 0)),
                  pl.BlockSpec((1, GLA_KW), lambda n: (0, 0)),
                  pl.BlockSpec((1, GLA_W), lambda n: (0, 0)),
                  pl.BlockSpec((GC, GC), lambda n: (0, 0))] + [ANY] * n_xc,
        out_specs=[pl.BlockSpec((GC, GLA_W), lambda n: (n, 0)),
                   pl.BlockSpec((GC, GLA_W), lambda n: (n, 0)),
                   pl.BlockSpec((1, GLA_HEADS, GLA_V, GLA_K), lambda n: (n, 0, 0, 0)),
                   pl.BlockSpec((1, GLA_HEADS, GC, GC), lambda n: (n, 0, 0, 0))] + [ANY] * n_xc,
        out_shape=[jax.ShapeDtypeStruct((tp, GLA_W), F32), jax.ShapeDtypeStruct((tp, GLA_W), BF16),
                   jax.ShapeDtypeStruct((nc, GLA_HEADS, GLA_V, GLA_K), BF16),
                   jax.ShapeDtypeStruct((nc, GLA_HEADS, GC, GC), BF16)] + (list(comm.out_shapes) if comm else []),
        scratch_shapes=[pltpu.VMEM((GLA_HEADS, GLA_V, GLA_K), F32), pltpu.VMEM((GC, GLA_KW), F32)]
        + (_comm_sems(comm) if comm else []),
        compiler_params=_cparams(1),
    )(gqk, gv, glr, gg, wg, bg, gain, pmat, *(comm.srcs if comm else ()))


def _gla_bwd_call(gqk, gv, glr, gg, o_gla, da, states, scores, wg, bg, gain, pmat, pmat_t, comm=None):
    tp = gqk.shape[0]
    nc = tp // GC
    o_gv, o_gg, o_lr = 2 * GLA_KW, 2 * GLA_KW + GLA_W, 2 * GLA_KW + 2 * GLA_W
    n_xc = len(comm.srcs) if comm else 0

    def body(qk_ref, v_ref, glr_ref, gg_ref, o_ref, da_ref, st_ref, am_ref, wg_ref, bg_ref, g_ref, p_ref, pt_ref,
             *rest):
        xc_src = rest[:n_xc]
        dp_ref, dwg_ref, dbg_ref, dg_ref = rest[n_xc:n_xc + 4]
        xc_dst = rest[n_xc + 4:2 * n_xc + 4]
        ds_scr, b_scr, db_scr = rest[2 * n_xc + 4:2 * n_xc + 7]
        n = pl.program_id(0)
        if n_xc:
            begin, finish = comm.make(xc_src, xc_dst, rest[-2], rest[-1])
            pl.when(n == 0)(begin)
            pl.when(n == nc - 1)(finish)

        @pl.when(n == 0)
        def _():
            ds_scr[...] = jnp.zeros_like(ds_scr)
            dwg_ref[...] = jnp.zeros_like(dwg_ref)
            dbg_ref[...] = jnp.zeros_like(dbg_ref)
            dg_ref[...] = jnp.zeros_like(dg_ref)

        z, la = _gla_log_decay(glr_ref, wg_ref, bg_ref)
        b_scr[...] = _exact_pm(p_ref[...], la)
        masks, md, second = _gla_masks()
        for h in range(GLA_HEADS):
            cs = slice(h * GLA_K, (h + 1) * GLA_K)
            vs = slice(h * GLA_V, (h + 1) * GLA_V)
            o = o_ref[:, vs]
            rstd = lax.rsqrt(jnp.mean(o * o, axis=-1, keepdims=True) + EPS)
            xh = o * rstd
            gain_h = g_ref[:, vs]
            g = gg_ref[:, vs]
            sg = _sigmoid(g)
            dah = da_ref[:, vs]
            dp_ref[:, o_gg + h * GLA_V:o_gg + (h + 1) * GLA_V] = (
                dah * (xh * gain_h) * (sg * (1.0 + g * (1.0 - sg)))).astype(BF16)
            dn = dah * (g * sg)
            dg_ref[:, vs] += jnp.sum(dn * xh, axis=0, keepdims=True)
            dxh = dn * gain_h
            do = rstd * (dxh - xh * jnp.mean(dxh * xh, axis=-1, keepdims=True))
            dob = do.astype(BF16)
            q = qk_ref[:, cs]
            k = qk_ref[:, GLA_KW + h * GLA_K:GLA_KW + (h + 1) * GLA_K]
            v = v_ref[:, vs]
            fq, fk, ed, edi, eb, ee, ebl = _gla_factors(b_scr, h, second)
            qt, kt, qd, kd = _gla_scaled(q, k, fq, fk, ed, edi)
            sp = st_ref[0, h]
            ds = ds_scr[h]
            dsb = ds.astype(BF16)
            q_in = q * eb
            k_end = k * ee
            da_s = _dot(dob, v, NT)
            dv = _dot(am_ref[0, h], dob, TN) + _dot(k_end.astype(BF16), dsb, NT)
            dq_in = _dot(dob, sp, NN)
            dk_end = _dot(v, dsb, NN)
            dbl = jnp.sum(sp.astype(F32) * ds, axis=0, keepdims=True) * ebl
            ds_scr[h] = ds * ebl + _dot(dob, q_in.astype(BF16), TN)
            dq = dq_in * eb
            dk = dk_end * ee
            de_end = dk_end * k_end
            db = dq_in * q_in - de_end
            placed = [(GC - 1, jnp.sum(de_end, axis=0, keepdims=True) + dbl)]
            for l, m in enumerate(GLA_LEVELS):
                dal = jnp.where(masks[l], da_s, 0.0).astype(BF16)
                dqt = _dot(dal, kt[l], NN)
                dkt = _dot(dal, qt[l], TN)
                dq = dq + dqt * fq[l]
                dk = dk + dkt * fk[l]
                gl = dqt * (q * fq[l]) - dkt * (k * fk[l])
                db = db + gl
                placed += [(s + m - 1, -jnp.sum(gl[s:s + 2 * m], axis=0, keepdims=True)) for s in range(0, GC, 2 * m)]
            dad = jnp.where(md, da_s, 0.0).astype(BF16)
            dqd = _dot(dad, kd, NN)
            dkd = _dot(dad, qd, TN)
            dq = dq + dqd * ed
            dk = dk + dkd * edi
            gd = dqd * (q * ed) - dkd * (k * edi)
            db = db + gd
            placed += [(s - 1, -jnp.sum(gd[s:s + GLA_SUB], axis=0, keepdims=True)) for s in range(GLA_SUB, GC, GLA_SUB)]
            db_scr[:, cs] = db
            for r, val in placed:
                db_scr[r:r + 1, cs] += val
            dp_ref[:, cs] = (dq * (GLA_K ** -0.5)).astype(BF16)
            dp_ref[:, GLA_KW + h * GLA_K:GLA_KW + (h + 1) * GLA_K] = dk.astype(BF16)
            dp_ref[:, o_gv + h * GLA_V:o_gv + (h + 1) * GLA_V] = dv.astype(BF16)
        dla = _exact_pm(pt_ref[...], db_scr[...])
        row = (nc - 1 - n) * GC + lax.broadcasted_iota(jnp.int32, (GC, 1), 0)
        dz = jnp.where(row >= PADF, dla * (1.0 / GATE_TAU) * _sigmoid(-z), 0.0)
        dzb = dz.astype(BF16)
        dp_ref[:, o_lr:] = _dot(dzb, wg_ref[...], NT).astype(BF16)
        dwg_ref[...] += _dot(glr_ref[...].astype(BF16), dzb, TN)
        dbg_ref[...] += jnp.sum(dz, axis=0, keepdims=True)

    rev = lambda n: (nc - 1 - n, 0)
    const = lambda n: (0, 0)
    xc_shapes, xc_sems = (list(comm.out_shapes), _comm_sems(comm)) if n_xc else ([], [])
    return pl.pallas_call(
        body, name="gla_bwd", grid=(nc,),
        in_specs=[pl.BlockSpec((GC, 2 * GLA_KW), rev),
                  pl.BlockSpec((GC, GLA_W), rev),
                  pl.BlockSpec((GC, 128), rev),
                  pl.BlockSpec((GC, GLA_W), rev),
                  pl.BlockSpec((GC, GLA_W), rev),
                  pl.BlockSpec((GC, GLA_W), rev),
                  pl.BlockSpec((1, GLA_HEADS, GLA_V, GLA_K), lambda n: (nc - 1 - n, 0, 0, 0)),
                  pl.BlockSpec((1, GLA_HEADS, GC, GC), lambda n: (nc - 1 - n, 0, 0, 0)),
                  pl.BlockSpec((128, GLA_KW), const),
                  pl.BlockSpec((1, GLA_KW), const),
                  pl.BlockSpec((1, GLA_W), const),
                  pl.BlockSpec((GC, GC), const),
                  pl.BlockSpec((GC, GC), const)] + [ANY] * n_xc,
        out_specs=[pl.BlockSpec((GC, W_GP), rev), pl.BlockSpec((128, GLA_KW), const),
                   pl.BlockSpec((1, GLA_KW), const), pl.BlockSpec((1, GLA_W), const)] + [ANY] * n_xc,
        out_shape=[jax.ShapeDtypeStruct((tp, W_GP), BF16), jax.ShapeDtypeStruct((128, GLA_KW), F32),
                   jax.ShapeDtypeStruct((1, GLA_KW), F32), jax.ShapeDtypeStruct((1, GLA_W), F32)] + xc_shapes,
        scratch_shapes=[pltpu.VMEM((GLA_HEADS, GLA_V, GLA_K), F32), pltpu.VMEM((GC, GLA_KW), F32),
                        pltpu.VMEM((GC, GLA_KW), F32)] + xc_sems,
        compiler_params=_cparams(1),
    )(gqk, gv, glr, gg, o_gla, da, states, scores, wg, bg, gain, pmat, pmat_t, *(comm.srcs if comm else ()))


def _mid_call(a_ret, a_gla, mg, h0, tgt, wbr, wbg, wout, gf):
    tp = h0.shape[0]
    nt = tp // TM

    def body(ar_ref, ag_ref, mg_ref, h_ref, t_ref, wbr_ref, wbg_ref, wo_ref, gf_ref,
             dh1_ref, dag_ref, dm_ref, mb_ref, dh1b_ref, dprb_ref, dpgb_ref, loss_ref, dgf_ref):
        i = pl.program_id(0)

        @pl.when(i == 0)
        def _():
            loss_ref[...] = jnp.zeros_like(loss_ref)
            dgf_ref[...] = jnp.zeros_like(dgf_ref)

        ar, ag = ar_ref[...], ag_ref[...]
        pr = _dot(ar, wbr_ref[...], NN)
        pg = _dot(ag, wbg_ref[...], NN)
        sr = _sigmoid(mg_ref[:, :D_MODEL])
        sg = _sigmoid(mg_ref[:, D_MODEL:])
        merged = (sr * pr + sg * pg).astype(BF16)
        mb_ref[...] = merged
        h1 = h_ref[...] + _dot(merged, wo_ref[...], NN)
        r1 = lax.rsqrt(jnp.mean(h1 * h1, axis=-1, keepdims=True) + EPS)
        xh = h1 * r1
        gfv = gf_ref[...]
        live = jnp.where(i > 0, 1.0, 0.0).astype(F32)
        err = (xh * gfv - t_ref[...]) * live
        loss_ref[...] += jnp.full(loss_ref.shape, 0.5 / D_MODEL, F32) * jnp.sum(err * err)
        dy = err * (1.0 / D_MODEL)
        dgf_ref[...] += jnp.sum(dy * xh, axis=0, keepdims=True)
        dxh = dy * gfv
        dh1 = r1 * (dxh - xh * jnp.mean(dxh * xh, axis=-1, keepdims=True))
        dh1_ref[...] = dh1
        dh1b = dh1.astype(BF16)
        dh1b_ref[...] = dh1b
        dmerged = _dot(dh1b, wo_ref[...], NT)
        dm_ref[:, :D_MODEL] = (dmerged * pr * sr * (1.0 - sr)).astype(BF16)
        dm_ref[:, D_MODEL:] = (dmerged * pg * sg * (1.0 - sg)).astype(BF16)
        dpr = (dmerged * sr).astype(BF16)
        dpg = (dmerged * sg).astype(BF16)
        dprb_ref[...] = dpr
        dpgb_ref[...] = dpg
        dag_ref[...] = _dot(dpg, wbg_ref[...], NT)

    tile = lambda w: pl.BlockSpec((TM, w), lambda i: (i, 0))
    const = lambda r, w: pl.BlockSpec((r, w), lambda i: (0, 0))
    return pl.pallas_call(
        body, name="merge_out_loss", grid=(nt,),
        in_specs=[tile(RET_W), tile(GLA_W), tile(W_M), tile(D_MODEL),
                  pl.BlockSpec((TM, D_MODEL), lambda i: (jnp.maximum(i - 1, 0), 0)),
                  const(RET_W, D_MODEL), const(GLA_W, D_MODEL), const(D_MODEL, D_MODEL), const(1, D_MODEL)],
        out_specs=[tile(D_MODEL), tile(GLA_W), tile(W_M), tile(D_MODEL), tile(D_MODEL), tile(D_MODEL),
                   tile(D_MODEL), const(1, 128), const(1, D_MODEL)],
        out_shape=[jax.ShapeDtypeStruct((tp, D_MODEL), F32), jax.ShapeDtypeStruct((tp, GLA_W), F32),
                   jax.ShapeDtypeStruct((tp, W_M), BF16),
                   jax.ShapeDtypeStruct((tp, D_MODEL), BF16), jax.ShapeDtypeStruct((tp, D_MODEL), BF16),
                   jax.ShapeDtypeStruct((tp, D_MODEL), BF16), jax.ShapeDtypeStruct((tp, D_MODEL), BF16),
                   jax.ShapeDtypeStruct((1, 128), F32), jax.ShapeDtypeStruct((1, D_MODEL), F32)],
        compiler_params=_cparams(1),
    )(a_ret, a_gla, mg, h0, tgt, wbr, wbg, wout, gf)


def _device_step(x2d, tgt2d, meta, norm_gain, w_in_part, w_gate_up, b_gate, ret_gain, gla_gain, branch_parts,
                 final_gain, ck):
    seq = x2d.shape[0]
    tp = T0 + seq
    head = jnp.concatenate([jnp.zeros((PADF, D_MODEL), F32), meta], axis=0)
    wg_pad = jnp.pad(w_gate_up, ((0, 128 - GATE_RANK), (0, 0))).astype(BF16)

    half = RET_QK // 2
    cos, sin = (jnp.asarray(t) for t in _rope_tables(tp))
    lgam = jnp.log1p(-(2.0 ** (-5.0 - jnp.arange(RET_HEADS, dtype=F32))))
    pmat = jnp.asarray(_gla_tril(), BF16)
    pmat_t = jnp.asarray(_gla_tril().T.copy(), BF16)

    h0, u, g_in = _rms_call(x2d, head, norm_gain, _gather_plan([w_in_part], relay=(True,)))
    hr, sw = w_in_part.shape
    w_in_bf = g_in.reshape(4, 2, hr, sw).transpose(1, 2, 0, 3).reshape(2 * hr, 4 * sw)
    w_r = w_in_bf
    w_g = jnp.pad(w_in_bf[:, W_R:W_R + W_G], ((0, 0), (0, W_GP - W_G)))
    w_m = w_in_bf[:, W_R + W_G:]
    tab = pl.BlockSpec((_proj_rows(tp), half), lambda j, i: (i, 0))
    rqk = _mm_nn("proj_rqk", u, w_r, BF16, D_MODEL, 0, 2 * D_MODEL, _rope_epilogue, (cos, sin), (tab, tab))
    rv = _mm_nn("proj_rv", u, w_r, BF16, RET_W, 2 * D_MODEL, RET_W)
    rg = _mm_nn("proj_rg", u, w_r, F32, RET_W, 4 * D_MODEL, RET_W)
    gqk = _mm_nn("proj_gqk", u, w_g, F32, 2 * GLA_KW, 0, 2 * GLA_KW, _gqk_epilogue)
    gv = _mm_nn("proj_gv", u, w_g, BF16, GLA_W, 2 * GLA_KW, GLA_W)
    gg = _mm_nn("proj_gg", u, w_g, F32, GLA_W, 2 * GLA_KW + GLA_W, GLA_W)
    glr = _mm_nn("proj_glr", u, w_g, F32, 128, 2 * GLA_KW + 2 * GLA_W, 128)
    mg = _mm_nn("proj_mg", u, w_m, F32, W_M, 0, W_M)

    o_ret, a_ret, st_ret, sc_ret = _ret_fwd_call(rqk, rv, rg, ret_gain, lgam)
    o_gla, a_gla, st_gla, sc_gla, g_br, g_bg, g_out = _gla_fwd_call(gqk, gv, glr, gg, wg_pad, b_gate, gla_gain, pmat,
                                                            comm=_spread_plan(branch_parts))
    wbr = g_br.reshape(RET_W, D_MODEL)
    wbg = g_bg.reshape(GLA_W, D_MODEL)
    wout = g_out.reshape(D_MODEL, D_MODEL)

    gf = final_gain.reshape(1, D_MODEL)
    (dh1, da_gla, dm, merged_b, dh1_b, dpr_b, dpg_b, loss, dgf) = _mid_call(
        a_ret, a_gla, mg, h0, tgt2d, wbr, wbg, wout, gf)

    names_b = ("w_branch_ret", "w_branch_gla", "w_out")
    g2_b = [_mm_tn("dw_br", a_ret, dpr_b, D_MODEL).reshape(4, 2, RET_W // 8, D_MODEL).transpose(1, 0, 2, 3),
            _mm_tn("dw_bg", a_gla, dpg_b, D_MODEL).reshape(4, 2, GLA_W // 8, D_MODEL).transpose(1, 0, 2, 3),
            _mm_tn("dw_out", merged_b, dh1_b, D_MODEL).reshape(4, 2, D_MODEL // 8, D_MODEL).transpose(1, 0, 2, 3)]
    sib_b = _swap_halves_call("swap_halves_branch", g2_b)
    sum_b = [_add_half_call("add_half_" + nm, g, b, ck) for nm, g, b in zip(names_b, g2_b, sib_b)]
    d_g, dwg, dbg, dgla_gain, *chips_b = _gla_bwd_call(gqk, gv, glr, gg, o_gla, da_gla, st_gla, sc_gla, wg_pad, b_gate,
                                                       gla_gain, pmat, pmat_t, comm=_exchange_plan(sum_b))
    mine = [_add_chips_call("add_chips_" + nm, g, b, p, ck) for nm, g, b, p in zip(names_b, g2_b, sib_b, chips_b)]

    d_r, dret_gain = _ret_bwd_call(rqk, rv, rg, o_ret, dpr_b, wbr, st_ret, sc_ret, ret_gain, lgam, cos, sin)

    dwp = _mm_tn("dw_r", u, d_r, 2 * D_MODEL, out_cols=IN_PAD)
    dwp = _mm_tn("dw_g", u, d_g, D_MODEL, ncols=W_GP - 128, into=dwp, col0=W_R)
    dwp = _mm_tn("dw_glr", u, d_g, 128, ncols=128, bcol0=W_GP - 128, into=dwp, col0=W_R + W_GP - 128)
    g2_in = _place_merge_cols_call(dwp, _mm_tn("dw_m", u, dm, 2 * D_MODEL)).reshape(2, D_MODEL // 2, IN_PAD)

    du, sib_in = _mm_nt_acc("du_g", d_g, w_g, W_GP, comm=_swap_plan([g2_in]), tb=_proj_rows(tp))
    sum_in = _add_rows_call("add_half_w_in", g2_in, sib_in, ck)
    du, chips_in = _mm_nt_acc("du_r", d_r, w_r, 2 * D_MODEL, acc_in=du, comm=_exchange_window_plan(sum_in),
                              tb=_proj_rows(tp))
    tile = pl.BlockSpec((TB, D_MODEL), lambda i, kk: (i, 0))
    row = pl.BlockSpec((1, D_MODEL), lambda i, kk: (0, 0))
    dx, dmeta, dnorm_gain = _mm_nt_acc(
        "du_m", dm, w_m, W_M, acc_in=du, epilogue=_rms_bwd_epilogue, extras=(h0, norm_gain, dh1),
        extra_specs=(tile, row, tile),
        extra_out_shapes=(jax.ShapeDtypeStruct((seq, D_MODEL), F32), jax.ShapeDtypeStruct((N_META, D_MODEL), F32),
                          jax.ShapeDtypeStruct((1, D_MODEL), F32)),
        extra_out_specs=(ANY, pl.BlockSpec((N_META, D_MODEL), lambda i, kk: (0, 0)), row),
        extra_scratch=(pltpu.VMEM((2, TB, D_MODEL), F32), pltpu.SemaphoreType.DMA((2,))))
    mine = [_add_window_call("add_chips_w_in", g2_in, sib_in, chips_in, ck)] + mine
    full = _join_halves_call("join_halves", mine)

    return dict(loss=loss[0, 0], dx=dx, dmeta=dmeta, norm_gain=dnorm_gain, w_gate_up=dwg[:GATE_RANK], b_gate=dbg,
                ret_norm_gain=dret_gain, gla_norm_gain=dgla_gain, final_norm_gain=dgf.reshape(D_MODEL),
                w_in=full[0], w_branch_ret=full[1], w_branch_gla=full[2], w_out=full[3])


MESH = pl.DeviceIdType.MESH
ANY = pl.BlockSpec(memory_space=pl.ANY)


def _place():
    return lax.axis_index("x"), lax.axis_index("y"), lax.axis_index("c")


def _gather8_call(name, parts):
    comm = _gather_plan(parts)
    n = len(parts)

    def body(*refs):
        begin, finish = comm.make(refs[:n], refs[n:2 * n], refs[-2], refs[-1])
        begin()
        finish()

    return pl.pallas_call(
        body, name=name, out_shape=list(comm.out_shapes), in_specs=[ANY] * n, out_specs=[ANY] * n,
        scratch_shapes=_comm_sems(comm),
    )(*parts)


def _swap_halves_call(name, gs):
    n = len(gs)

    def body(*refs):
        g_refs, b_refs = refs[:n], refs[n:2 * n]
        send_sems, recv_sems = refs[2 * n:]
        x, y, c = _place()
        copies = [pltpu.make_async_remote_copy(
            src_ref=g_refs[t].at[1 - c], dst_ref=b_refs[t], send_sem=send_sems.at[t], recv_sem=recv_sems.at[t],
            device_id=(x, y, 1 - c), device_id_type=MESH) for t in range(n)]
        for cp in copies:
            cp.start()
        for cp in copies:
            cp.wait()

    return pl.pallas_call(
        body, name=name,
        out_shape=[jax.ShapeDtypeStruct(g.shape[1:], g.dtype) for g in gs],
        in_specs=[ANY] * n, out_specs=[ANY] * n,
        scratch_shapes=[pltpu.SemaphoreType.DMA((n,)), pltpu.SemaphoreType.DMA((n,))],
    )(*gs)


def _join_halves_call(name, ts):
    n = len(ts)

    def body(*refs):
        o_refs = refs[n:2 * n]
        send_sems, recv_sems = refs[2 * n:]
        x, y, c = _place()
        copies = [pltpu.make_async_remote_copy(
            src_ref=o_refs[t].at[c], dst_ref=o_refs[t].at[c], send_sem=send_sems.at[t], recv_sem=recv_sems.at[t],
            device_id=(x, y, 1 - c), device_id_type=MESH) for t in range(n)]
        for cp in copies:
            cp.start()
        for t in range(n):
            copies[t].wait_send()
            pltpu.make_async_remote_copy(
                src_ref=o_refs[t].at[c], dst_ref=o_refs[t].at[1 - c], send_sem=send_sems.at[t],
                recv_sem=recv_sems.at[t], device_id=(x, y, 1 - c), device_id_type=MESH).wait_recv()

    return pl.pallas_call(
        body, name=name,
        out_shape=[jax.ShapeDtypeStruct(t.shape, t.dtype) for t in ts],
        in_specs=[ANY] * n, out_specs=[ANY] * n, input_output_aliases={t: t for t in range(n)},
        scratch_shapes=[pltpu.SemaphoreType.DMA((n,)), pltpu.SemaphoreType.DMA((n,))],
    )(*ts)


def _row_block(rows, cols, budget):
    best = 8
    for rb in range(8, rows + 1, 8):
        if rows % rb == 0 and rb * cols * 4 <= budget:
            best = rb
    return best


def _add_half_call(name, g, b, ck):
    _, _, r, cc = g.shape
    rb = _row_block(r, cc, 2 * 1024 * 1024)

    def body(ck_ref, g_ref, b_ref, o_ref):
        o_ref[...] = (g_ref[...] + b_ref[...]).astype(BF16)

    return pl.pallas_call(
        body, name=name,
        grid_spec=pltpu.PrefetchScalarGridSpec(
            num_scalar_prefetch=1, grid=(4, r // rb),
            in_specs=[pl.BlockSpec((None, None, rb, cc), lambda k, i, ck_ref: (ck_ref[0], k, i, 0)),
                      pl.BlockSpec((None, rb, cc), lambda k, i, ck_ref: (k, i, 0))],
            out_specs=pl.BlockSpec((None, rb, cc), lambda k, i, ck_ref: (k, i, 0))),
        out_shape=jax.ShapeDtypeStruct(b.shape, BF16),
        compiler_params=_cparams(2),
    )(ck, g, b)


def _add_rows_call(name, g, b, ck):
    _, r, cc = g.shape
    rb = _row_block(r, cc, 2 * 1024 * 1024)

    def body(ck_ref, g_ref, b_ref, o_ref):
        o_ref[...] = (g_ref[...] + b_ref[...]).astype(BF16)

    return pl.pallas_call(
        body, name=name,
        grid_spec=pltpu.PrefetchScalarGridSpec(
            num_scalar_prefetch=1, grid=(r // rb,),
            in_specs=[pl.BlockSpec((None, rb, cc), lambda i, ck_ref: (ck_ref[0], i, 0)),
                      pl.BlockSpec((rb, cc), lambda i, ck_ref: (i, 0))],
            out_specs=pl.BlockSpec((rb, cc), lambda i, ck_ref: (i, 0))),
        out_shape=jax.ShapeDtypeStruct((r, cc), BF16),
        compiler_params=_cparams(1),
    )(ck, g, b)


def _add_window_call(name, g, b, p, ck):
    _, r, _ = g.shape
    nb, step = WIN_W // 128, WIN_STEP // 128

    def body(ck_ref, g_ref, b_ref, p0_ref, p1_ref, p2_ref, o_ref):
        own = g_ref[...] + b_ref[...]
        o_ref[...] = ((own + p0_ref[...].astype(F32)) + p1_ref[...].astype(F32)) + p2_ref[...].astype(F32)

    def peer(j):
        return pl.BlockSpec((None, r, 128), lambda i, ck_ref: (j, 0, i))

    return pl.pallas_call(
        body, name=name,
        grid_spec=pltpu.PrefetchScalarGridSpec(
            num_scalar_prefetch=1, grid=(nb,),
            in_specs=[pl.BlockSpec((None, r, 128), lambda i, ck_ref: (ck_ref[0], 0, step * ck_ref[1] + i)),
                      pl.BlockSpec((r, 128), lambda i, ck_ref: (0, step * ck_ref[1] + i)),
                      peer(0), peer(1), peer(2)],
            out_specs=pl.BlockSpec((None, r, 128), lambda i, ck_ref: (ck_ref[0], 0, i))),
        out_shape=jax.ShapeDtypeStruct((2, r, WIN_W), F32),
        compiler_params=_cparams(1),
    )(ck, g, b, p, p, p)


def _add_chips_call(name, g, b, p, ck):
    _, _, r, cc = g.shape
    rb = _row_block(r, cc, 2 * 1024 * 1024)

    def body(ck_ref, g_ref, b_ref, p0_ref, p1_ref, p2_ref, o_ref):
        own = g_ref[...] + b_ref[...]
        o_ref[...] = ((own + p0_ref[...].astype(F32)) + p1_ref[...].astype(F32)) + p2_ref[...].astype(F32)

    def peer(j):
        return pl.BlockSpec((None, rb, cc), lambda i, ck_ref: (j, i, 0))

    return pl.pallas_call(
        body, name=name,
        grid_spec=pltpu.PrefetchScalarGridSpec(
            num_scalar_prefetch=1, grid=(r // rb,),
            in_specs=[pl.BlockSpec((None, None, rb, cc), lambda i, ck_ref: (ck_ref[0], ck_ref[1], i, 0)),
                      pl.BlockSpec((None, rb, cc), lambda i, ck_ref: (ck_ref[1], i, 0)),
                      peer(0), peer(1), peer(2)],
            out_specs=pl.BlockSpec((None, rb, cc), lambda i, ck_ref: (ck_ref[0], i, 0))),
        out_shape=jax.ShapeDtypeStruct((2, r, cc), F32),
        compiler_params=_cparams(1),
    )(ck, g, b, p, p, p)


def _sum8_call(name, g):
    def body(g_ref, o_ref):
        acc = g_ref[0]
        for d in range(1, 8):
            acc = acc + g_ref[d]
        o_ref[...] = acc

    return pl.pallas_call(body, name=name, out_shape=jax.ShapeDtypeStruct(g.shape[1:], F32))(g)


def _adamw_call(name, w, g, m, v):
    r, cc = w.shape
    if r % 8 == 0 or r * cc * 4 <= 1024 * 1024:
        rb = _row_block(r, cc, 1024 * 1024) if r % 8 == 0 else r
        grid, spec = (r // rb,), pl.BlockSpec((rb, cc), lambda i: (i, 0))
    else:
        grid, spec = (cc // 128,), pl.BlockSpec((r, 128), lambda i: (0, i))

    def body(w_ref, g_ref, m_ref, v_ref, d_ref, m2_ref, v2_ref):
        gv = g_ref[...]
        m2 = ADAM_B1 * m_ref[...] + (1.0 - ADAM_B1) * gv
        v2 = ADAM_B2 * v_ref[...] + (1.0 - ADAM_B2) * (gv * gv)
        m_hat = m2 / (1.0 - ADAM_B1 ** ADAM_STEP)
        v_hat = v2 / (1.0 - ADAM_B2 ** ADAM_STEP)
        d_ref[...] = -ADAM_LR * (m_hat / (jnp.sqrt(v_hat) + ADAM_EPS) + ADAM_WD * w_ref[...])
        m2_ref[...] = m2
        v2_ref[...] = v2

    return pl.pallas_call(
        body, name=name, grid=grid, in_specs=[spec] * 4, out_specs=[spec] * 3,
        out_shape=[jax.ShapeDtypeStruct((r, cc), F32)] * 3, compiler_params=_cparams(1),
    )(w, g, m, v)


SMALL = (("norm_gain", D_MODEL), ("b_gate", GLA_KW), ("ret_norm_gain", RET_W), ("gla_norm_gain", GLA_W),
         ("final_norm_gain", D_MODEL), ("w_gate_up", GATE_RANK * GLA_KW), ("meta_tokens", N_META * D_MODEL),
         ("loss", 1))


def _pack_rows(vecs, rows):
    flat = jnp.concatenate([v.reshape(-1) for v in vecs])
    return jnp.pad(flat, (0, rows * 128 - flat.shape[0])).reshape(rows, 128)


def kernel(x, meta_tokens, norm_gain, w_in, w_gate_up, b_gate, ret_norm_gain, gla_norm_gain, w_branch_ret, w_branch_gla, w_out, final_norm_gain, loss_target, m_meta_tokens, m_norm_gain, m_w_in, m_w_gate_up, m_b_gate, m_ret_norm_gain, m_gla_norm_gain, m_w_branch_ret, m_w_branch_gla, m_w_out, m_final_norm_gain, v_meta_tokens, v_norm_gain, v_w_in, v_w_gate_up, v_b_gate, v_ret_norm_gain, v_gla_norm_gain, v_w_branch_ret, v_w_branch_gla, v_w_out, v_final_norm_gain):
    xi, yi, ci = _place()
    kme = 2 * xi + yi
    ck = jnp.stack([ci, kme]).astype(jnp.int32)
    sw_in = w_in.shape[2]

    def my_half(a, dtype):
        r, cc = a.shape
        return lax.dynamic_index_in_dim(a.reshape(2, r // 2, cc), ci, 0, keepdims=False).astype(dtype)

    g_meta, g_wg = _gather8_call("gather_small_weights", [my_half(meta_tokens, F32), my_half(w_gate_up[0], F32)])
    branch_parts = [my_half(w_branch_ret[0], BF16), my_half(w_branch_gla[0], BF16), my_half(w_out[0], BF16)]
    meta = g_meta.reshape(4, 2, N_META // 2, D_MODEL // 4).transpose(1, 2, 0, 3).reshape(N_META, D_MODEL)
    wg_full = g_wg.reshape(4, 2, GATE_RANK // 2, GLA_KW // 4).transpose(1, 2, 0, 3).reshape(GATE_RANK, GLA_KW)

    loc = _device_step(x[0], loss_target[0], meta, norm_gain, my_half(w_in[0], BF16), wg_full, b_gate, ret_norm_gain,
                       gla_norm_gain,
                       branch_parts, final_norm_gain, ck)
    names = ("w_in", "w_branch_ret", "w_branch_gla", "w_out")
    full = [loc[nm] for nm in names]
    big_w = dict(w_in=w_in[0], w_branch_ret=w_branch_ret[0], w_branch_gla=w_branch_gla[0], w_out=w_out[0])
    big_m = dict(w_in=m_w_in[0], w_branch_ret=m_w_branch_ret[0], w_branch_gla=m_w_branch_gla[0], w_out=m_w_out[0])
    big_v = dict(w_in=v_w_in[0], w_branch_ret=v_w_branch_ret[0], w_branch_gla=v_w_branch_gla[0], w_out=v_w_out[0])
    grads, deltas, new_m, new_v = {}, {}, {}, {}
    for nm, f in zip(names, full):
        shape = big_w[nm].shape
        if nm == "w_in":
            f = lax.dynamic_slice_in_dim(f, (sw_in - WIN_STEP) * kme, sw_in, axis=2)
        g = f.reshape(shape)
        if nm == "w_in":
            d, m2, v2 = (a.T for a in _adamw_call("adamw_" + nm, big_w[nm].T, g.T, big_m[nm].T, big_v[nm].T))
        else:
            d, m2, v2 = _adamw_call("adamw_" + nm, big_w[nm], g, big_m[nm], big_v[nm])
        grads[nm], deltas[nm], new_m[nm], new_v[nm] = (a.reshape((1,) + shape) for a in (g, d, m2, v2))

    small_g = dict(loc)
    small_g["meta_tokens"] = loc["dmeta"]
    n_small = sum(sz for _, sz in SMALL)
    rows = -(-n_small // 128 // 8) * 8
    (g_small,) = _gather8_call("gather_small_grads", [_pack_rows([small_g[nm] for nm, _ in SMALL], rows)])
    tot = _sum8_call("sum_small_grads", g_small).reshape(-1)
    off = 0
    sg = {}
    for nm, sz in SMALL:
        sg[nm] = tot[off:off + sz]
        off += sz
    loss = sg.pop("loss")[0]
    sg["w_gate_up"] = lax.dynamic_slice_in_dim(sg["w_gate_up"].reshape(GATE_RANK, GLA_KW), kme * (GLA_KW // 4),
                                               GLA_KW // 4, axis=1)
    sg["meta_tokens"] = lax.dynamic_slice_in_dim(sg["meta_tokens"].reshape(N_META, D_MODEL), kme * (D_MODEL // 4),
                                                 D_MODEL // 4, axis=1)
    small_w = dict(norm_gain=norm_gain, b_gate=b_gate, ret_norm_gain=ret_norm_gain, gla_norm_gain=gla_norm_gain,
                   final_norm_gain=final_norm_gain, w_gate_up=w_gate_up, meta_tokens=meta_tokens)
    small_m = dict(norm_gain=m_norm_gain, b_gate=m_b_gate, ret_norm_gain=m_ret_norm_gain,
                   gla_norm_gain=m_gla_norm_gain, final_norm_gain=m_final_norm_gain, w_gate_up=m_w_gate_up,
                   meta_tokens=m_meta_tokens)
    small_v = dict(norm_gain=v_norm_gain, b_gate=v_b_gate, ret_norm_gain=v_ret_norm_gain,
                   gla_norm_gain=v_gla_norm_gain, final_norm_gain=v_final_norm_gain, w_gate_up=v_w_gate_up,
                   meta_tokens=v_meta_tokens)
    for nm in small_w:
        shape = small_w[nm].shape
        as2d = lambda a: a.reshape((-1, shape[-1]))
        grads[nm] = sg[nm].reshape(shape)
        deltas[nm], new_m[nm], new_v[nm] = (a.reshape(shape) for a in _adamw_call(
            "adamw_" + nm, as2d(small_w[nm]), as2d(sg[nm]), as2d(small_m[nm]), as2d(small_v[nm])))

    out_order = ("meta_tokens", "norm_gain", "w_in", "w_gate_up", "b_gate", "ret_norm_gain", "gla_norm_gain",
                 "w_branch_ret", "w_branch_gla", "w_out", "final_norm_gain")
    dx = loc["dx"].reshape(x.shape)
    return (loss, dx, *[grads[nm] for nm in out_order], *[deltas[nm] for nm in out_order],
            *[new_m[nm] for nm in out_order], *[new_v[nm] for nm in out_order])
```

```python
import math
from typing import Callable, NamedTuple

import numpy as np
import jax
import jax.numpy as jnp
from jax import lax
from jax.experimental import pallas as pl
from jax.experimental.pallas import tpu as pltpu

F32 = jnp.float32
BF16 = jnp.bfloat16

D_MODEL = 1024
N_META = 16
EPS = 1e-6
ROPE_BASE = 10000.0
RET_HEADS, RET_QK, RET_V = 4, 256, 512
RET_W = RET_HEADS * RET_V
GLA_HEADS, GLA_K, GLA_V = 4, 128, 256
GLA_W = GLA_HEADS * GLA_V
GLA_KW = GLA_HEADS * GLA_K
GATE_RANK = 16
GATE_TAU = 16.0
GLA_SUB = 16

TM = 256
T0 = TM
PADF = T0 - N_META
GC = 128
TB = 768
TK = 768

W_R = 6144
W_G = 3088
W_GP = 3200
W_M = 2048
IN_COLS = W_R + W_G + W_M
WIN_STEP = (IN_COLS // 4) // 128 * 128
WIN_W = -(-(3 * (IN_COLS // 4 - WIN_STEP) + IN_COLS // 4) // 128) * 128
IN_PAD = 3 * WIN_STEP + WIN_W

ADAM_LR, ADAM_B1, ADAM_B2, ADAM_EPS, ADAM_WD, ADAM_STEP = 0.001, 0.9, 0.999, 1e-08, 0.01, 10

VMEM_LIMIT = 56 * 1024 * 1024

NN = ((1,), (0,))
NT = ((1,), (1,))
TN = ((0,), (0,))


def _dot(a, b, dims):
    return lax.dot_general(a, b, (dims, ((), ())), preferred_element_type=F32)


def _cparams(n_axes):
    return pltpu.CompilerParams(dimension_semantics=("arbitrary",) * n_axes, vmem_limit_bytes=VMEM_LIMIT)


def _sigmoid(x):
    return 0.5 * jnp.tanh(0.5 * x) + 0.5


def _split3(x):
    hi = x.astype(BF16)
    r1 = x - hi.astype(F32)
    mid = r1.astype(BF16)
    lo = (r1 - mid.astype(F32)).astype(BF16)
    return hi, mid, lo


def _exact_pm(p, x):
    hi, mid, lo = _split3(x)
    return _dot(p, hi, NN) + _dot(p, mid, NN) + _dot(p, lo, NN)


def _rms_call(x2d, head, gain, comm):
    tp = T0 + x2d.shape[0]
    nt = tp // TM
    n_xc = len(comm.srcs)

    def body(x_ref, hd_ref, g_ref, *rest):
        xc_src = rest[:n_xc]
        h_ref, u_ref = rest[n_xc:n_xc + 2]
        xc_dst = rest[n_xc + 2:2 * n_xc + 2]
        i = pl.program_id(0)
        begin, finish = comm.make(xc_src, xc_dst, rest[-2], rest[-1])
        pl.when(i == 0)(begin)
        h = jnp.where(i == 0, hd_ref[...], x_ref[...])
        h_ref[...] = h
        r = lax.rsqrt(jnp.mean(h * h, axis=-1, keepdims=True) + EPS)
        u_ref[...] = (h * r * g_ref[...]).astype(BF16)
        pl.when(i == nt - 1)(finish)

    tile = pl.BlockSpec((TM, D_MODEL), lambda i: (i, 0))
    return pl.pallas_call(
        body, name="rms_in", grid=(nt,),
        in_specs=[pl.BlockSpec((TM, D_MODEL), lambda i: (jnp.maximum(i - 1, 0), 0)),
                  pl.BlockSpec((T0, D_MODEL), lambda i: (0, 0)), pl.BlockSpec((1, D_MODEL), lambda i: (0, 0))]
        + [ANY] * n_xc,
        out_specs=[tile, tile] + [ANY] * n_xc,
        out_shape=[jax.ShapeDtypeStruct((tp, D_MODEL), F32), jax.ShapeDtypeStruct((tp, D_MODEL), BF16)]
        + list(comm.out_shapes),
        scratch_shapes=_comm_sems(comm), compiler_params=_cparams(1),
    )(x2d, head, gain, *comm.srcs)


PROJ_ROWS_MAX = 1408


def _proj_rows(m):
    return max(r for r in range(16, PROJ_ROWS_MAX + 1, 16) if m % r == 0)


def _mm_nn(name, a, b, out_dtype, tn, col0, ncols, epilogue=None, extras=(), extra_specs=()):
    m, k = a.shape
    nj, j0 = ncols // tn, col0 // tn
    tb = _proj_rows(m)

    def body(a_ref, b_ref, *rest):
        *ex, o_ref = rest
        acc = _dot(a_ref[...], b_ref[...], NN)
        if epilogue is None:
            o_ref[...] = acc.astype(out_dtype)
        else:
            epilogue(acc, o_ref, *ex)

    return pl.pallas_call(
        body, name=name, grid=(nj, m // tb),
        in_specs=[pl.BlockSpec((tb, k), lambda j, i: (i, 0)), pl.BlockSpec((k, tn), lambda j, i: (0, j0 + j))]
        + list(extra_specs),
        out_specs=pl.BlockSpec((tb, tn), lambda j, i: (i, j)),
        out_shape=jax.ShapeDtypeStruct((m, ncols), out_dtype),
        compiler_params=_cparams(2),
    )(a, b, *extras)


def _rope_tables(tp):
    half = RET_QK // 2
    pos = np.arange(tp, dtype=np.float32) - np.float32(PADF)
    inv = (ROPE_BASE ** (-np.arange(half, dtype=np.float64) / half)).astype(np.float32)
    ang = (pos[:, None] * inv[None, :]).astype(np.float64)
    return np.cos(ang).astype(np.float32), np.sin(ang).astype(np.float32)


def _rope_epilogue(acc, o_ref, cos_ref, sin_ref):
    scale = jnp.where(pl.program_id(0) == 1, RET_QK ** -0.5, 1.0).astype(F32)
    cos, sin = cos_ref[...], sin_ref[...]
    half = RET_QK // 2
    for h in range(RET_HEADS):
        t1 = acc[:, h * RET_QK:h * RET_QK + half]
        t2 = acc[:, h * RET_QK + half:(h + 1) * RET_QK]
        o_ref[:, h * RET_QK:h * RET_QK + half] = ((t1 * cos - t2 * sin) * scale).astype(BF16)
        o_ref[:, h * RET_QK + half:(h + 1) * RET_QK] = ((t2 * cos + t1 * sin) * scale).astype(BF16)


def _gqk_epilogue(acc, o_ref):
    o_ref[:, :GLA_KW] = acc[:, :GLA_KW] * (GLA_K ** -0.5)
    o_ref[:, GLA_KW:] = acc[:, GLA_KW:]


class _Comm(NamedTuple):
    srcs: tuple
    out_shapes: tuple
    n_sems: int
    make: Callable


def _comm_sems(comm):
    return [pltpu.SemaphoreType.DMA((comm.n_sems,)), pltpu.SemaphoreType.DMA((comm.n_sems,))]


def _start_wait(copies):
    def begin():
        for cp in copies:
            cp.start()

    def finish():
        for cp in copies:
            cp.wait()

    return begin, finish


def _other_chips(x, y):
    return [(1 - x, y), (x, 1 - y), (1 - x, 1 - y)]


def _gather_plan(parts, relay=()):
    n = len(parts)
    relay = tuple(relay) + (False,) * (n - len(relay))

    def make(x_refs, out_refs, send_sems, recv_sems):
        x, y, c = _place()
        me, sibling = (x, y, c), (x, y, 1 - c)
        xn, yn, dg = (1 - x, y), (x, 1 - y), (1 - x, 1 - y)

        def slot(t, px, py, pc, half=None):
            ref = out_refs[t].at[4 * px + 2 * py + pc]
            if half is None:
                return ref
            rows = ref.shape[0] // 2
            return ref.at[pl.ds(half * rows, rows)]

        def copy(t, k, dst, to, src=None):
            return pltpu.make_async_remote_copy(
                src_ref=dst if src is None else src, dst_ref=dst, send_sem=send_sems.at[8 * t + k],
                recv_sem=recv_sems.at[8 * t + k], device_id=to, device_id_type=MESH)

        mine = [pltpu.make_async_copy(x_refs[t], slot(t, *me), send_sems.at[8 * n + t]) for t in range(n)]
        sent = []
        for t in range(n):
            sent.append(copy(t, 0, slot(t, *me), sibling, src=x_refs[t]))
            sent.append(copy(t, 1, slot(t, *me), (*xn, c), src=x_refs[t]))
            sent.append(copy(t, 2, slot(t, *me), (*yn, c), src=x_refs[t]))
            if not relay[t]:
                sent.append(copy(t, 3, slot(t, *me), (*dg, c), src=x_refs[t]))

        def begin():
            for cp in mine + sent:
                cp.start()

        def finish():
            later = []

            def start(cp):
                cp.start()
                later.append(cp)

            for t in range(n):
                copy(t, 2, slot(t, *yn, c), me).wait_recv()
                if relay[t]:
                    start(copy(t, 3, slot(t, *yn, c, half=0), (*xn, c)))
                start(copy(t, 6, slot(t, *yn, c), sibling))
            for t in range(n):
                copy(t, 1, slot(t, *xn, c), me).wait_recv()
                if relay[t]:
                    start(copy(t, 4, slot(t, *xn, c, half=1), (*yn, c)))
                start(copy(t, 5, slot(t, *xn, c), sibling))
            for t in range(n):
                if relay[t]:
                    copy(t, 3, slot(t, *dg, c, half=0), me).wait_recv()
                    copy(t, 4, slot(t, *dg, c, half=1), me).wait_recv()
                else:
                    copy(t, 3, slot(t, *dg, c), me).wait_recv()
                start(copy(t, 7, slot(t, *dg, c), sibling))
            for t in range(n):
                copy(t, 0, slot(t, *sibling), me).wait_recv()
                copy(t, 5, slot(t, *xn, 1 - c), me).wait_recv()
                copy(t, 6, slot(t, *yn, 1 - c), me).wait_recv()
                copy(t, 7, slot(t, *dg, 1 - c), me).wait_recv()
            for cp in sent + later:
                cp.wait_send()
            for cp in mine:
                cp.wait()

        return begin, finish

    return _Comm(tuple(parts), tuple(jax.ShapeDtypeStruct((8,) + p.shape, p.dtype) for p in parts), 9 * n, make)


def _exchange_plan(ss):
    def make(s_refs, b_refs, send_sems, recv_sems):
        x, y, c = _place()
        return _start_wait([pltpu.make_async_remote_copy(
            src_ref=s_refs[t].at[2 * chip[0] + chip[1]], dst_ref=b_refs[t].at[j], send_sem=send_sems.at[3 * t + j],
            recv_sem=recv_sems.at[3 * t + j], device_id=(*chip, c), device_id_type=MESH)
            for t in range(len(s_refs)) for j, chip in enumerate(_other_chips(x, y))])

    return _Comm(tuple(ss), tuple(jax.ShapeDtypeStruct((3,) + s.shape[1:], s.dtype) for s in ss), 3 * len(ss), make)


def _exchange_window_plan(s):
    def make(s_refs, b_refs, send_sems, recv_sems):
        x, y, c = _place()
        return _start_wait([pltpu.make_async_remote_copy(
            src_ref=s_refs[0].at[:, pl.ds(pl.multiple_of((2 * chip[0] + chip[1]) * WIN_STEP, 128), WIN_W)],
            dst_ref=b_refs[0].at[j], send_sem=send_sems.at[j], recv_sem=recv_sems.at[j], device_id=(*chip, c),
            device_id_type=MESH) for j, chip in enumerate(_other_chips(x, y))])

    return _Comm((s,), (jax.ShapeDtypeStruct((3, s.shape[0], WIN_W), s.dtype),), 3, make)


def _swap_plan(gs):
    def make(g_refs, b_refs, send_sems, recv_sems):
        x, y, c = _place()
        return _start_wait([pltpu.make_async_remote_copy(
            src_ref=g_refs[t].at[1 - c], dst_ref=b_refs[t], send_sem=send_sems.at[t], recv_sem=recv_sems.at[t],
            device_id=(x, y, 1 - c), device_id_type=MESH) for t in range(len(g_refs))])

    return _Comm(tuple(gs), tuple(jax.ShapeDtypeStruct(g.shape[1:], g.dtype) for g in gs), len(gs), make)


def _spread_plan(parts):
    def make(p_refs, o_refs, send_sems, recv_sems):
        x, y, c = _place()
        copies = []
        for t in range(len(p_refs)):
            mine = o_refs[t].at[4 * x + 2 * y + c]
            copies.append(pltpu.make_async_copy(p_refs[t], mine, send_sems.at[7 * len(p_refs) + t]))
            for r in range(1, 8):
                peer = (1 - x if r & 4 else x, 1 - y if r & 2 else y, 1 - c if r & 1 else c)
                copies.append(pltpu.make_async_remote_copy(
                    src_ref=p_refs[t], dst_ref=mine, send_sem=send_sems.at[7 * t + r - 1],
                    recv_sem=recv_sems.at[7 * t + r - 1], device_id=peer, device_id_type=MESH))
        return _start_wait(copies)

    return _Comm(tuple(parts), tuple(jax.ShapeDtypeStruct((8,) + p.shape, p.dtype) for p in parts), 8 * len(parts),
                 make)


def _mm_nt_acc(name, a, w, tk, acc_in=None, epilogue=None, extras=(), extra_specs=(), extra_out_shapes=(),
               extra_out_specs=(), extra_scratch=(), comm=None, tb=TB):
    m, k = a.shape
    n = w.shape[0]
    nk, ni = k // tk, m // tb
    has_acc = acc_in is not None
    n_xc = len(comm.srcs) if comm else 0
    n_es = len(extra_scratch)

    def body(*refs):
        a_ref, w_ref = refs[0], refs[1]
        pos = 2
        acc_ref = None
        if has_acc:
            acc_ref = refs[pos]
            pos += 1
        ex = refs[pos:pos + len(extras)]
        pos += len(extras)
        xc_src = refs[pos:pos + n_xc]
        pos += n_xc
        n_scr = 1 + n_es + (2 if n_xc else 0)
        outs = refs[pos:len(refs) - n_scr - n_xc]
        xc_dst = refs[len(refs) - n_scr - n_xc:len(refs) - n_scr]
        scr = refs[len(refs) - n_scr]
        es = refs[len(refs) - n_scr + 1:len(refs) - n_scr + 1 + n_es]
        i, kk = pl.program_id(0), pl.program_id(1)
        if n_xc:
            begin, finish = comm.make(xc_src, xc_dst, refs[-2], refs[-1])
            pl.when((i == 0) & (kk == 0))(begin)

        @pl.when(kk == 0)
        def _():
            scr[...] = acc_ref[...] if has_acc else jnp.zeros_like(scr)

        scr[...] += _dot(a_ref[...], w_ref[...], NT)

        @pl.when(kk == nk - 1)
        def _():
            if epilogue is None:
                outs[0][...] = scr[...]
            else:
                epilogue(scr[...], outs, i, ni, *ex, *es)

        if n_xc:
            pl.when((i == ni - 1) & (kk == nk - 1))(finish)

    in_specs = [pl.BlockSpec((tb, tk), lambda i, kk: (i, kk)), pl.BlockSpec((n, tk), lambda i, kk: (0, kk))]
    args = [a, w]
    if has_acc:
        in_specs.append(pl.BlockSpec((tb, n), lambda i, kk: (i, 0)))
        args.append(acc_in)
    in_specs += list(extra_specs) + [ANY] * n_xc
    args += list(extras) + (list(comm.srcs) if comm else [])
    if epilogue is None:
        out_shape = [jax.ShapeDtypeStruct((m, n), F32)]
        out_specs = [pl.BlockSpec((tb, n), lambda i, kk: (i, 0))]
    else:
        out_shape, out_specs = list(extra_out_shapes), list(extra_out_specs)
    scratch = [pltpu.VMEM((tb, n), F32)] + list(extra_scratch)
    if n_xc:
        out_shape += list(comm.out_shapes)
        out_specs += [ANY] * n_xc
        scratch += _comm_sems(comm)
    return pl.pallas_call(
        body, name=name, grid=(ni, nk), in_specs=in_specs, out_specs=out_specs, out_shape=out_shape,
        scratch_shapes=scratch, compiler_params=_cparams(2),
    )(*args)


def _rms_bwd_epilogue(du, outs, i, ni, h_ref, g_ref, dh1_ref, obuf, sems):
    dx_ref, dmeta_ref, dg_ref = outs
    h = h_ref[...]
    r = lax.rsqrt(jnp.mean(h * h, axis=-1, keepdims=True) + EPS)
    xh = h * r
    dxh = du * g_ref[...]
    dh0 = dh1_ref[...] + r * (dxh - xh * jnp.mean(dxh * xh, axis=-1, keepdims=True))

    def put(slot, tile):
        return pltpu.make_async_copy(obuf.at[slot], dx_ref.at[pl.ds(pl.multiple_of(tile * TB - T0, 8), TB)],
                                     sems.at[slot])

    @pl.when(i == 0)
    def _():
        dg_ref[...] = jnp.zeros_like(dg_ref)
        dmeta_ref[...] = dh0[PADF:T0, :]
        obuf[0] = dh0
        first = pltpu.make_async_copy(obuf.at[0, pl.ds(T0, TB - T0)], dx_ref.at[pl.ds(0, TB - T0)], sems.at[0])
        first.start()
        first.wait()

    @pl.when(i >= 1)
    def _():
        slot = i % 2

        @pl.when(i >= 3)
        def _():
            put(slot, i - 2).wait()

        obuf[slot] = dh0
        put(slot, i).start()

    dg_ref[...] += jnp.sum(du * xh, axis=0, keepdims=True)

    @pl.when(i == ni - 1)
    def _():
        for tile in (ni - 2, ni - 1):
            if tile >= 1:
                put(tile % 2, tile).wait()


def _mm_tn(name, a, b, bn, ncols=None, bcol0=0, into=None, col0=0, out_cols=None):
    t, m = a.shape
    n = ncols or b.shape[1]
    j0, bj0 = col0 // bn, bcol0 // bn

    def body(a_ref, b_ref, *rest):
        o_ref = rest[-1]

        @pl.when(pl.program_id(1) == 0)
        def _():
            o_ref[...] = jnp.zeros_like(o_ref)

        o_ref[...] += _dot(a_ref[...], b_ref[...], TN)

    in_specs = [pl.BlockSpec((TK, m), lambda j, kk: (kk, 0)), pl.BlockSpec((TK, bn), lambda j, kk: (kk, bj0 + j))]
    args = [a, b]
    aliases = {}
    if into is not None:
        in_specs.append(ANY)
        args.append(into)
        aliases = {2: 0}
        out_cols = into.shape[1]
    return pl.pallas_call(
        body, name=name, grid=(n // bn, t // TK), in_specs=in_specs,
        out_specs=pl.BlockSpec((m, bn), lambda j, kk: (0, j0 + j)),
        out_shape=jax.ShapeDtypeStruct((m, out_cols or n), F32), input_output_aliases=aliases,
        compiler_params=_cparams(2),
    )(*args)


def _place_merge_cols_call(dwp, dw_m):
    c0 = W_R + W_GP - 128
    tail = IN_PAD - c0
    rows = 256

    def body(m_ref, p_ref, o_ref, buf, low, sem):
        get = pltpu.make_async_copy(o_ref.at[:, pl.ds(c0, 128)], low, sem)
        get.start()
        get.wait()
        for r in range(0, D_MODEL, rows):
            buf[r:r + rows, :] = jnp.concatenate(
                [low[r:r + rows, :GATE_RANK], m_ref[r:r + rows, :],
                 jnp.zeros((rows, tail - GATE_RANK - W_M), F32)], axis=1)
        put = pltpu.make_async_copy(buf, o_ref.at[:, pl.ds(c0, tail)], sem)
        put.start()
        put.wait()

    return pl.pallas_call(
        body, name="place_merge_cols",
        in_specs=[pl.BlockSpec(memory_space=pltpu.VMEM), ANY], out_specs=ANY,
        out_shape=jax.ShapeDtypeStruct(dwp.shape, F32), input_output_aliases={1: 0},
        scratch_shapes=[pltpu.VMEM((D_MODEL, tail), F32), pltpu.VMEM((D_MODEL, 128), F32), pltpu.SemaphoreType.DMA],
        compiler_params=pltpu.CompilerParams(vmem_limit_bytes=VMEM_LIMIT),
    )(dw_m, dwp)


def _ret_fill_decay(lg_ref, dm_scr):
    c = TM
    ii = lax.broadcasted_iota(jnp.int32, (c, c), 0)
    jj = lax.broadcasted_iota(jnp.int32, (c, c), 1)
    rel = (ii - jj).astype(F32)
    for h in range(RET_HEADS):
        dm_scr[h] = jnp.where(rel >= 0, jnp.exp(jnp.maximum(rel, 0.0) * lg_ref[h]), 0.0)


def _ret_consts(lg, dm_ref):
    c = TM
    idx = lax.broadcasted_iota(jnp.int32, (c, 1), 0).astype(F32)
    xi = jnp.exp((idx + 1.0) * lg)
    zeta = jnp.exp((c - 1.0 - idx) * lg)
    gc = jnp.exp(jnp.full((1, 1), c, F32) * lg)
    return dm_ref[...], xi, zeta, gc


def _ret_fwd_call(rqk, rv, rg, gain, lgam):
    tp = rqk.shape[0]
    nc = tp // TM

    def body(lg_ref, qk_ref, v_ref, rg_ref, g_ref, o_ref, a_ref, st_ref, sc_ref, s_scr, dm_scr):
        @pl.when(pl.program_id(0) == 0)
        def _():
            s_scr[...] = jnp.zeros_like(s_scr)
            _ret_fill_decay(lg_ref, dm_scr)

        for h in range(RET_HEADS):
            dm, xi, zeta, gc = _ret_consts(lg_ref[h], dm_scr.at[h])
            q = qk_ref[:, h * RET_QK:(h + 1) * RET_QK]
            k = qk_ref[:, D_MODEL + h * RET_QK:D_MODEL + (h + 1) * RET_QK]
            v = v_ref[:, h * RET_V:(h + 1) * RET_V]
            sb = s_scr[h].astype(BF16)
            st_ref[0, h] = sb
            s = (_dot(q, k, NT) * dm).astype(BF16)
            sc_ref[0, h] = s
            o = _dot(s, v, NN) + xi * _dot(q, sb, NN)
            kz = (k.astype(F32) * zeta).astype(BF16)
            s_scr[h] = gc * s_scr[h] + _dot(kz, v, TN)
            o_ref[:, h * RET_V:(h + 1) * RET_V] = o
            mu = jnp.mean(o, axis=-1, keepdims=True)
            xc = o - mu
            xh = xc * lax.rsqrt(jnp.mean(xc * xc, axis=-1, keepdims=True) + EPS)
            g = rg_ref[:, h * RET_V:(h + 1) * RET_V]
            a_ref[:, h * RET_V:(h + 1) * RET_V] = (
                xh * g_ref[:, h * RET_V:(h + 1) * RET_V] * (g * _sigmoid(g))).astype(BF16)

    return pl.pallas_call(
        body, name="ret_fwd", grid=(nc,),
        in_specs=[pl.BlockSpec(memory_space=pltpu.SMEM),
                  pl.BlockSpec((TM, 2 * D_MODEL), lambda n: (n, 0)),
                  pl.BlockSpec((TM, RET_W), lambda n: (n, 0)),
                  pl.BlockSpec((TM, RET_W), lambda n: (n, 0)),
                  pl.BlockSpec((1, RET_W), lambda n: (0, 0))],
        out_specs=[pl.BlockSpec((TM, RET_W), lambda n: (n, 0)),
                   pl.BlockSpec((TM, RET_W), lambda n: (n, 0)),
                   pl.BlockSpec((1, RET_HEADS, RET_QK, RET_V), lambda n: (n, 0, 0, 0)),
                   pl.BlockSpec((1, RET_HEADS, TM, TM), lambda n: (n, 0, 0, 0))],
        out_shape=[jax.ShapeDtypeStruct((tp, RET_W), F32), jax.ShapeDtypeStruct((tp, RET_W), BF16),
                   jax.ShapeDtypeStruct((nc, RET_HEADS, RET_QK, RET_V), BF16),
                   jax.ShapeDtypeStruct((nc, RET_HEADS, TM, TM), BF16)],
        scratch_shapes=[pltpu.VMEM((RET_HEADS, RET_QK, RET_V), F32), pltpu.VMEM((RET_HEADS, TM, TM), F32)],
        compiler_params=_cparams(1),
    )(lgam, rqk, rv, rg, gain)


def _ret_bwd_call(rqk, rv, rg, o_ret, dpr, wbr, states, scores, gain, lgam, cos, sin):
    tp = rqk.shape[0]
    nc = tp // TM
    half = RET_QK // 2

    def body(lg_ref, qk_ref, v_ref, rg_ref, o_ref, dpr_ref, wbr_ref, st_ref, sc_ref, g_ref, cos_ref, sin_ref, dp_ref,
             dg_ref, ds_scr, dm_scr):
        @pl.when(pl.program_id(0) == 0)
        def _():
            ds_scr[...] = jnp.zeros_like(ds_scr)
            dg_ref[...] = jnp.zeros_like(dg_ref)
            _ret_fill_decay(lg_ref, dm_scr)

        cos, sin = cos_ref[...], sin_ref[...]
        for h in range(RET_HEADS):
            hs = slice(h * RET_V, (h + 1) * RET_V)
            dm, xi, zeta, gc = _ret_consts(lg_ref[h], dm_scr.at[h])
            o = o_ref[:, hs]
            mu = jnp.mean(o, axis=-1, keepdims=True)
            xc = o - mu
            rstd = lax.rsqrt(jnp.mean(xc * xc, axis=-1, keepdims=True) + EPS)
            xh = xc * rstd
            gain_h = g_ref[:, hs]
            g = rg_ref[:, hs]
            sg = _sigmoid(g)
            silu = g * sg
            dah = _dot(dpr_ref[...], wbr_ref[hs, :], NT)
            dp_ref[:, 4 * D_MODEL + h * RET_V:4 * D_MODEL + (h + 1) * RET_V] = (
                dah * (xh * gain_h) * (sg * (1.0 + g * (1.0 - sg)))).astype(BF16)
            dn = dah * silu
            dg_ref[:, hs] += jnp.sum(dn * xh, axis=0, keepdims=True)
            dxh = dn * gain_h
            do = rstd * (dxh - jnp.mean(dxh, axis=-1, keepdims=True)
                         - xh * jnp.mean(dxh * xh, axis=-1, keepdims=True))
            dob = do.astype(BF16)
            q = qk_ref[:, h * RET_QK:(h + 1) * RET_QK]
            k = qk_ref[:, D_MODEL + h * RET_QK:D_MODEL + (h + 1) * RET_QK]
            v = v_ref[:, hs]
            sp = st_ref[0, h]
            ds = ds_scr[h]
            dsb = ds.astype(BF16)
            s = sc_ref[0, h]
            dsc = (_dot(dob, v, NT) * dm).astype(BF16)
            dq = _dot(dsc, k, NN) + xi * _dot(dob, sp, NT)
            dk = _dot(dsc, q, TN) + zeta * _dot(v, dsb, NT)
            kz = (k.astype(F32) * zeta).astype(BF16)
            dv = _dot(s, dob, TN) + _dot(kz, dsb, NN)
            qx = (q.astype(F32) * xi).astype(BF16)
            ds_scr[h] = gc * ds + _dot(qx, dob, TN)
            dp_ref[:, 2 * D_MODEL + h * RET_V:2 * D_MODEL + (h + 1) * RET_V] = dv.astype(BF16)
            dk = dk * (RET_QK ** -0.5)
            for base, t in ((0, dq), (D_MODEL, dk)):
                t1, t2 = t[:, :half], t[:, half:]
                dp_ref[:, base + h * RET_QK:base + h * RET_QK + half] = (t1 * cos + t2 * sin).astype(BF16)
                dp_ref[:, base + h * RET_QK + half:base + (h + 1) * RET_QK] = (t2 * cos - t1 * sin).astype(BF16)

    rev = lambda n: (nc - 1 - n, 0)
    return pl.pallas_call(
        body, name="ret_bwd", grid=(nc,),
        in_specs=[pl.BlockSpec(memory_space=pltpu.SMEM),
                  pl.BlockSpec((TM, 2 * D_MODEL), rev),
                  pl.BlockSpec((TM, RET_W), rev),
                  pl.BlockSpec((TM, RET_W), rev),
                  pl.BlockSpec((TM, RET_W), rev),
                  pl.BlockSpec((TM, D_MODEL), rev),
                  pl.BlockSpec((RET_W, D_MODEL), lambda n: (0, 0)),
                  pl.BlockSpec((1, RET_HEADS, RET_QK, RET_V), lambda n: (nc - 1 - n, 0, 0, 0)),
                  pl.BlockSpec((1, RET_HEADS, TM, TM), lambda n: (nc - 1 - n, 0, 0, 0)),
                  pl.BlockSpec((1, RET_W), lambda n: (0, 0)),
                  pl.BlockSpec((TM, half), rev),
                  pl.BlockSpec((TM, half), rev)],
        out_specs=[pl.BlockSpec((TM, W_R), rev), pl.BlockSpec((1, RET_W), lambda n: (0, 0))],
        out_shape=[jax.ShapeDtypeStruct((tp, W_R), BF16), jax.ShapeDtypeStruct((1, RET_W), F32)],
        scratch_shapes=[pltpu.VMEM((RET_HEADS, RET_QK, RET_V), F32), pltpu.VMEM((RET_HEADS, TM, TM), F32)],
        compiler_params=_cparams(1),
    )(lgam, rqk, rv, rg, o_ret, dpr, wbr, states, scores, gain, cos, sin)


GLA_LEVELS = tuple(GC >> (s + 1) for s in range(int(math.log2(GC // GLA_SUB))))
NLEV = len(GLA_LEVELS)


def _gla_tril():
    return np.tril(np.ones((GC, GC), np.float32))


def _gla_masks():
    ii = lax.broadcasted_iota(jnp.int32, (GC, GC), 0)
    jj = lax.broadcasted_iota(jnp.int32, (GC, GC), 1)
    masks = []
    for m in GLA_LEVELS:
        sh = int(math.log2(2 * m))
        masks.append(((ii >> sh) == (jj >> sh)) & ((ii & m) != 0) & ((jj & m) == 0))
    sh = int(math.log2(GLA_SUB))
    md = ((ii >> sh) == (jj >> sh)) & (jj <= ii)
    row = lax.broadcasted_iota(jnp.int32, (GC, 1), 0)
    second = [(row & m) != 0 for m in GLA_LEVELS]
    return masks, md, second


def _gla_gate_call(glr, wg, bg, pmat):
    tp = glr.shape[0]
    gb = _proj_rows(tp)
    assert gb % GC == 0

    def body(glr_ref, wg_ref, bg_ref, p_ref, z_ref, b_ref):
        z = _dot(glr_ref[...].astype(BF16), wg_ref[...], NN) + bg_ref[...]
        z_ref[...] = z
        la = (jnp.minimum(z, 0.0) - jnp.log1p(jnp.exp(-jnp.abs(z)))) * (1.0 / GATE_TAU)
        for r in range(0, gb, GC):
            b_ref[r:r + GC, :] = _exact_pm(p_ref[...], la[r:r + GC, :])

    tile = pl.BlockSpec((gb, GLA_KW), lambda i: (i, 0))
    return pl.pallas_call(
        body, name="gla_gate", grid=(tp // gb,),
        in_specs=[pl.BlockSpec((gb, 128), lambda i: (i, 0)), pl.BlockSpec((128, GLA_KW), lambda i: (0, 0)),
                  pl.BlockSpec((1, GLA_KW), lambda i: (0, 0)), pl.BlockSpec((GC, GC), lambda i: (0, 0))],
        out_specs=[tile, tile],
        out_shape=[jax.ShapeDtypeStruct((tp, GLA_KW), F32), jax.ShapeDtypeStruct((tp, GLA_KW), F32)],
        compiler_params=_cparams(1),
    )(glr, wg, bg, pmat)


def _gla_gate_bwd_call(db, z, glr, wg, pmat_t, d_g):
    tp = db.shape[0]
    gb = _proj_rows(tp)
    assert gb % GC == 0 and (W_GP - 128) % 128 == 0

    def body(db_ref, z_ref, glr_ref, wg_ref, pt_ref, dgin_ref, dg_ref, dwg_ref, dbg_ref):
        i = pl.program_id(0)

        @pl.when(i == 0)
        def _():
            dwg_ref[...] = jnp.zeros_like(dwg_ref)
            dbg_ref[...] = jnp.zeros_like(dbg_ref)

        dla = jnp.concatenate([_exact_pm(pt_ref[...], db_ref[r:r + GC, :]) for r in range(0, gb, GC)], axis=0)
        row = i * gb + lax.broadcasted_iota(jnp.int32, (gb, 1), 0)
        dz = jnp.where(row >= PADF, dla * (1.0 / GATE_TAU) * _sigmoid(-z_ref[...]), 0.0)
        dzb = dz.astype(BF16)
        dg_ref[...] = _dot(dzb, wg_ref[...], NT).astype(BF16)
        dwg_ref[...] += _dot(glr_ref[...].astype(BF16), dzb, TN)
        dbg_ref[...] += jnp.sum(dz, axis=0, keepdims=True)

    tile = pl.BlockSpec((gb, GLA_KW), lambda i: (i, 0))
    const = lambda i: (0, 0)
    return pl.pallas_call(
        body, name="gla_gate_bwd", grid=(tp // gb,),
        in_specs=[tile, tile, pl.BlockSpec((gb, 128), lambda i: (i, 0)), pl.BlockSpec((128, GLA_KW), const),
                  pl.BlockSpec((GC, GC), const), ANY],
        out_specs=[pl.BlockSpec((gb, 128), lambda i: (i, (W_GP - 128) // 128)), pl.BlockSpec((128, GLA_KW), const),
                   pl.BlockSpec((1, GLA_KW), const)],
        out_shape=[jax.ShapeDtypeStruct(d_g.shape, BF16), jax.ShapeDtypeStruct((128, GLA_KW), F32),
                   jax.ShapeDtypeStruct((1, GLA_KW), F32)],
        input_output_aliases={5: 0}, compiler_params=_cparams(1),
    )(db, z, glr, wg, pmat_t, d_g)


def _gla_row_steps(b_ref, cs, rows, size):
    parts = [jnp.zeros((size, GLA_K), F32) if r is None else jnp.broadcast_to(b_ref[r:r + 1, cs], (size, GLA_K))
             for r in rows]
    return parts[0] if len(parts) == 1 else jnp.concatenate(parts, axis=0)


def _gla_factors(b_ref, h, second):
    cs = slice(h * GLA_K, (h + 1) * GLA_K)
    b = b_ref[:, cs]
    fq, fk = [], []
    for l, m in enumerate(GLA_LEVELS):
        d = b - _gla_row_steps(b_ref, cs, [s + m - 1 for s in range(0, GC, 2 * m)], 2 * m)
        f = jnp.exp(jnp.where(second[l], d, -d))
        fq.append(jnp.where(second[l], f, 0.0))
        fk.append(jnp.where(second[l], 0.0, f))
    dd = b - _gla_row_steps(b_ref, cs, [None] + [s - 1 for s in range(GLA_SUB, GC, GLA_SUB)], GLA_SUB)
    ed = jnp.exp(dd)
    edi = jnp.exp(-dd)
    eb = jnp.exp(b)
    bl = b_ref[GC - 1:GC, cs]
    ee = jnp.exp(bl - b)
    ebl = jnp.exp(bl)
    return fq, fk, ed, edi, eb, ee, ebl


def _gla_scaled(q, k, fq, fk, ed, edi):
    qt = [(q * f).astype(BF16) for f in fq]
    kt = [(k * f).astype(BF16) for f in fk]
    return qt, kt, (q * ed).astype(BF16), (k * edi).astype(BF16)


def _gla_scores(qt, kt, qd, kd, masks, md):
    a = jnp.where(md, _dot(qd, kd, NT), 0.0)
    for l in range(NLEV):
        a = a + jnp.where(masks[l], _dot(qt[l], kt[l], NT), 0.0)
    return a.astype(BF16)


def _gla_fwd_call(gqk, gv, b, gg, gain, comm=None):
    tp = gqk.shape[0]
    nc = tp // GC
    n_xc = len(comm.srcs) if comm else 0

    def body(qk_ref, v_ref, b_scr, gg_ref, g_ref, *rest):
        xc_src = rest[:n_xc]
        o_ref, a_ref, st_ref, am_ref = rest[n_xc:n_xc + 4]
        xc_dst = rest[n_xc + 4:2 * n_xc + 4]
        s_scr = rest[2 * n_xc + 4]
        n = pl.program_id(0)
        if n_xc:
            begin, finish = comm.make(xc_src, xc_dst, rest[-2], rest[-1])
            pl.when(n == 0)(begin)
            pl.when(n == nc - 1)(finish)

        @pl.when(n == 0)
        def _():
            s_scr[...] = jnp.zeros_like(s_scr)

        masks, md, second = _gla_masks()
        for h in range(GLA_HEADS):
            q = qk_ref[:, h * GLA_K:(h + 1) * GLA_K]
            k = qk_ref[:, GLA_KW + h * GLA_K:GLA_KW + (h + 1) * GLA_K]
            vs = slice(h * GLA_V, (h + 1) * GLA_V)
            v = v_ref[:, vs]
            fq, fk, ed, edi, eb, ee, ebl = _gla_factors(b_scr, h, second)
            a = _gla_scores(*_gla_scaled(q, k, fq, fk, ed, edi), masks, md)
            am_ref[0, h] = a
            sb = s_scr[h].astype(BF16)
            st_ref[0, h] = sb
            o = _dot(a, v, NN) + _dot((q * eb).astype(BF16), sb, NT)
            s_scr[h] = s_scr[h] * ebl + _dot(v, (k * ee).astype(BF16), TN)
            o_ref[:, vs] = o
            xh = o * lax.rsqrt(jnp.mean(o * o, axis=-1, keepdims=True) + EPS)
            g = gg_ref[:, vs]
            a_ref[:, vs] = (xh * g_ref[:, vs] * (g * _sigmoid(g))).astype(BF16)

    return pl.pallas_call(
        body, name="gla_fwd", grid=(nc,),
        in_specs=[pl.BlockSpec((GC, 2 * GLA_KW), lambda n: (n, 0)),
                  pl.BlockSpec((GC, GLA_W), lambda n: (n, 0)),
                  pl.BlockSpec((GC, GLA_KW), lambda n: (n, 0)),
                  pl.BlockSpec((GC, GLA_W), lambda n: (n, 0)),
                  pl.BlockSpec((1, GLA_W), lambda n: (0, 0))] + [ANY] * n_xc,
        out_specs=[pl.BlockSpec((GC, GLA_W), lambda n: (n, 0)),
                   pl.BlockSpec((GC, GLA_W), lambda n: (n, 0)),
                   pl.BlockSpec((1, GLA_HEADS, GLA_V, GLA_K), lambda n: (n, 0, 0, 0)),
                   pl.BlockSpec((1, GLA_HEADS, GC, GC), lambda n: (n, 0, 0, 0))] + [ANY] * n_xc,
        out_shape=[jax.ShapeDtypeStruct((tp, GLA_W), F32), jax.ShapeDtypeStruct((tp, GLA_W), BF16),
                   jax.ShapeDtypeStruct((nc, GLA_HEADS, GLA_V, GLA_K), BF16),
                   jax.ShapeDtypeStruct((nc, GLA_HEADS, GC, GC), BF16)] + (list(comm.out_shapes) if comm else []),
        scratch_shapes=[pltpu.VMEM((GLA_HEADS, GLA_V, GLA_K), F32)] + (_comm_sems(comm) if comm else []),
        compiler_params=_cparams(1),
    )(gqk, gv, b, gg, gain, *(comm.srcs if comm else ()))


def _gla_bwd_call(gqk, gv, b, gg, o_gla, da, states, scores, gain, comm=None):
    tp = gqk.shape[0]
    nc = tp // GC
    o_gv, o_gg = 2 * GLA_KW, 2 * GLA_KW + GLA_W
    n_xc = len(comm.srcs) if comm else 0

    def body(qk_ref, v_ref, b_scr, gg_ref, o_ref, da_ref, st_ref, am_ref, g_ref, *rest):
        xc_src = rest[:n_xc]
        dp_ref, db_scr, dg_ref = rest[n_xc:n_xc + 3]
        xc_dst = rest[n_xc + 3:2 * n_xc + 3]
        ds_scr = rest[2 * n_xc + 3]
        n = pl.program_id(0)
        if n_xc:
            begin, finish = comm.make(xc_src, xc_dst, rest[-2], rest[-1])
            pl.when(n == 0)(begin)
            pl.when(n == nc - 1)(finish)

        @pl.when(n == 0)
        def _():
            ds_scr[...] = jnp.zeros_like(ds_scr)
            dg_ref[...] = jnp.zeros_like(dg_ref)

        masks, md, second = _gla_masks()
        for h in range(GLA_HEADS):
            cs = slice(h * GLA_K, (h + 1) * GLA_K)
            vs = slice(h * GLA_V, (h + 1) * GLA_V)
            o = o_ref[:, vs]
            rstd = lax.rsqrt(jnp.mean(o * o, axis=-1, keepdims=True) + EPS)
            xh = o * rstd
            gain_h = g_ref[:, vs]
            g = gg_ref[:, vs]
            sg = _sigmoid(g)
            dah = da_ref[:, vs]
            dp_ref[:, o_gg + h * GLA_V:o_gg + (h + 1) * GLA_V] = (
                dah * (xh * gain_h) * (sg * (1.0 + g * (1.0 - sg)))).astype(BF16)
            dn = dah * (g * sg)
            dg_ref[:, vs] += jnp.sum(dn * xh, axis=0, keepdims=True)
            dxh = dn * gain_h
            do = rstd * (dxh - xh * jnp.mean(dxh * xh, axis=-1, keepdims=True))
            dob = do.astype(BF16)
            q = qk_ref[:, cs]
            k = qk_ref[:, GLA_KW + h * GLA_K:GLA_KW + (h + 1) * GLA_K]
            v = v_ref[:, vs]
            fq, fk, ed, edi, eb, ee, ebl = _gla_factors(b_scr, h, second)
            qt, kt, qd, kd = _gla_scaled(q, k, fq, fk, ed, edi)
            sp = st_ref[0, h]
            ds = ds_scr[h]
            dsb = ds.astype(BF16)
            q_in = q * eb
            k_end = k * ee
            da_s = _dot(dob, v, NT)
            dv = _dot(am_ref[0, h], dob, TN) + _dot(k_end.astype(BF16), dsb, NT)
            dq_in = _dot(dob, sp, NN)
            dk_end = _dot(v, dsb, NN)
            dbl = jnp.sum(sp.astype(F32) * ds, axis=0, keepdims=True) * ebl
            ds_scr[h] = ds * ebl + _dot(dob, q_in.astype(BF16), TN)
            dq = dq_in * eb
            dk = dk_end * ee
            de_end = dk_end * k_end
            db = dq_in * q_in - de_end
            placed = [(GC - 1, jnp.sum(de_end, axis=0, keepdims=True) + dbl)]
            for l, m in enumerate(GLA_LEVELS):
                dal = jnp.where(masks[l], da_s, 0.0).astype(BF16)
                dqt = _dot(dal, kt[l], NN)
                dkt = _dot(dal, qt[l], TN)
                dq = dq + dqt * fq[l]
                dk = dk + dkt * fk[l]
                gl = dqt * (q * fq[l]) - dkt * (k * fk[l])
                db = db + gl
                placed += [(s + m - 1, -jnp.sum(gl[s:s + 2 * m], axis=0, keepdims=True)) for s in range(0, GC, 2 * m)]
            dad = jnp.where(md, da_s, 0.0).astype(BF16)
            dqd = _dot(dad, kd, NN)
            dkd = _dot(dad, qd, TN)
            dq = dq + dqd * ed
            dk = dk + dkd * edi
            gd = dqd * (q * ed) - dkd * (k * edi)
            db = db + gd
            placed += [(s - 1, -jnp.sum(gd[s:s + GLA_SUB], axis=0, keepdims=True)) for s in range(GLA_SUB, GC, GLA_SUB)]
            db_scr[:, cs] = db
            for r, val in placed:
                db_scr[r:r + 1, cs] += val
            dp_ref[:, cs] = (dq * (GLA_K ** -0.5)).astype(BF16)
            dp_ref[:, GLA_KW + h * GLA_K:GLA_KW + (h + 1) * GLA_K] = dk.astype(BF16)
            dp_ref[:, o_gv + h * GLA_V:o_gv + (h + 1) * GLA_V] = dv.astype(BF16)

    rev = lambda n: (nc - 1 - n, 0)
    const = lambda n: (0, 0)
    xc_shapes, xc_sems = (list(comm.out_shapes), _comm_sems(comm)) if n_xc else ([], [])
    return pl.pallas_call(
        body, name="gla_bwd", grid=(nc,),
        in_specs=[pl.BlockSpec((GC, 2 * GLA_KW), rev),
                  pl.BlockSpec((GC, GLA_W), rev),
                  pl.BlockSpec((GC, GLA_KW), rev),
                  pl.BlockSpec((GC, GLA_W), rev),
                  pl.BlockSpec((GC, GLA_W), rev),
                  pl.BlockSpec((GC, GLA_W), rev),
                  pl.BlockSpec((1, GLA_HEADS, GLA_V, GLA_K), lambda n: (nc - 1 - n, 0, 0, 0)),
                  pl.BlockSpec((1, GLA_HEADS, GC, GC), lambda n: (nc - 1 - n, 0, 0, 0)),
                  pl.BlockSpec((1, GLA_W), const)] + [ANY] * n_xc,
        out_specs=[pl.BlockSpec((GC, W_GP), rev), pl.BlockSpec((GC, GLA_KW), rev),
                   pl.BlockSpec((1, GLA_W), const)] + [ANY] * n_xc,
        out_shape=[jax.ShapeDtypeStruct((tp, W_GP), BF16), jax.ShapeDtypeStruct((tp, GLA_KW), F32),
                   jax.ShapeDtypeStruct((1, GLA_W), F32)] + xc_shapes,
        scratch_shapes=[pltpu.VMEM((GLA_HEADS, GLA_V, GLA_K), F32)] + xc_sems,
        compiler_params=_cparams(1),
    )(gqk, gv, b, gg, o_gla, da, states, scores, gain, *(comm.srcs if comm else ()))


def _mid_call(a_ret, a_gla, mg, h0, tgt, wbr, wbg, wout, gf):
    tp = h0.shape[0]
    nt = tp // TM

    def body(ar_ref, ag_ref, mg_ref, h_ref, t_ref, wbr_ref, wbg_ref, wo_ref, gf_ref,
             dh1_ref, dag_ref, dm_ref, mb_ref, dh1b_ref, dprb_ref, dpgb_ref, loss_ref, dgf_ref):
        i = pl.program_id(0)

        @pl.when(i == 0)
        def _():
            loss_ref[...] = jnp.zeros_like(loss_ref)
            dgf_ref[...] = jnp.zeros_like(dgf_ref)

        ar, ag = ar_ref[...], ag_ref[...]
        pr = _dot(ar, wbr_ref[...], NN)
        pg = _dot(ag, wbg_ref[...], NN)
        sr = _sigmoid(mg_ref[:, :D_MODEL])
        sg = _sigmoid(mg_ref[:, D_MODEL:])
        merged = (sr * pr + sg * pg).astype(BF16)
        mb_ref[...] = merged
        h1 = h_ref[...] + _dot(merged, wo_ref[...], NN)
        r1 = lax.rsqrt(jnp.mean(h1 * h1, axis=-1, keepdims=True) + EPS)
        xh = h1 * r1
        gfv = gf_ref[...]
        live = jnp.where(i > 0, 1.0, 0.0).astype(F32)
        err = (xh * gfv - t_ref[...]) * live
        loss_ref[...] += jnp.full(loss_ref.shape, 0.5 / D_MODEL, F32) * jnp.sum(err * err)
        dy = err * (1.0 / D_MODEL)
        dgf_ref[...] += jnp.sum(dy * xh, axis=0, keepdims=True)
        dxh = dy * gfv
        dh1 = r1 * (dxh - xh * jnp.mean(dxh * xh, axis=-1, keepdims=True))
        dh1_ref[...] = dh1
        dh1b = dh1.astype(BF16)
        dh1b_ref[...] = dh1b
        dmerged = _dot(dh1b, wo_ref[...], NT)
        dm_ref[:, :D_MODEL] = (dmerged * pr * sr * (1.0 - sr)).astype(BF16)
        dm_ref[:, D_MODEL:] = (dmerged * pg * sg * (1.0 - sg)).astype(BF16)
        dpr = (dmerged * sr).astype(BF16)
        dpg = (dmerged * sg).astype(BF16)
        dprb_ref[...] = dpr
        dpgb_ref[...] = dpg
        dag_ref[...] = _dot(dpg, wbg_ref[...], NT)

    tile = lambda w: pl.BlockSpec((TM, w), lambda i: (i, 0))
    const = lambda r, w: pl.BlockSpec((r, w), lambda i: (0, 0))
    return pl.pallas_call(
        body, name="merge_out_loss", grid=(nt,),
        in_specs=[tile(RET_W), tile(GLA_W), tile(W_M), tile(D_MODEL),
                  pl.BlockSpec((TM, D_MODEL), lambda i: (jnp.maximum(i - 1, 0), 0)),
                  const(RET_W, D_MODEL), const(GLA_W, D_MODEL), const(D_MODEL, D_MODEL), const(1, D_MODEL)],
        out_specs=[tile(D_MODEL), tile(GLA_W), tile(W_M), tile(D_MODEL), tile(D_MODEL), tile(D_MODEL),
                   tile(D_MODEL), const(1, 128), const(1, D_MODEL)],
        out_shape=[jax.ShapeDtypeStruct((tp, D_MODEL), F32), jax.ShapeDtypeStruct((tp, GLA_W), F32),
                   jax.ShapeDtypeStruct((tp, W_M), BF16),
                   jax.ShapeDtypeStruct((tp, D_MODEL), BF16), jax.ShapeDtypeStruct((tp, D_MODEL), BF16),
                   jax.ShapeDtypeStruct((tp, D_MODEL), BF16), jax.ShapeDtypeStruct((tp, D_MODEL), BF16),
                   jax.ShapeDtypeStruct((1, 128), F32), jax.ShapeDtypeStruct((1, D_MODEL), F32)],
        compiler_params=_cparams(1),
    )(a_ret, a_gla, mg, h0, tgt, wbr, wbg, wout, gf)


def _device_step(x2d, tgt2d, meta, norm_gain, w_in_part, w_gate_up, b_gate, ret_gain, gla_gain, branch_parts,
                 final_gain, ck):
    seq = x2d.shape[0]
    tp = T0 + seq
    head = jnp.concatenate([jnp.zeros((PADF, D_MODEL), F32), meta], axis=0)
    wg_pad = jnp.pad(w_gate_up, ((0, 128 - GATE_RANK), (0, 0))).astype(BF16)

    half = RET_QK // 2
    cos, sin = (jnp.asarray(t) for t in _rope_tables(tp))
    lgam = jnp.log1p(-(2.0 ** (-5.0 - jnp.arange(RET_HEADS, dtype=F32))))
    pmat = jnp.asarray(_gla_tril(), BF16)
    pmat_t = jnp.asarray(_gla_tril().T.copy(), BF16)

    h0, u, g_in = _rms_call(x2d, head, norm_gain, _gather_plan([w_in_part], relay=(True,)))
    hr, sw = w_in_part.shape
    w_in_bf = g_in.reshape(4, 2, hr, sw).transpose(1, 2, 0, 3).reshape(2 * hr, 4 * sw)
    w_r = w_in_bf
    w_g = jnp.pad(w_in_bf[:, W_R:W_R + W_G], ((0, 0), (0, W_GP - W_G)))
    w_m = w_in_bf[:, W_R + W_G:]
    tab = pl.BlockSpec((_proj_rows(tp), half), lambda j, i: (i, 0))
    rqk = _mm_nn("proj_rqk", u, w_r, BF16, D_MODEL, 0, 2 * D_MODEL, _rope_epilogue, (cos, sin), (tab, tab))
    rv = _mm_nn("proj_rv", u, w_r, BF16, RET_W, 2 * D_MODEL, RET_W)
    rg = _mm_nn("proj_rg", u, w_r, F32, RET_W, 4 * D_MODEL, RET_W)
    gqk = _mm_nn("proj_gqk", u, w_g, F32, 2 * GLA_KW, 0, 2 * GLA_KW, _gqk_epilogue)
    gv = _mm_nn("proj_gv", u, w_g, BF16, GLA_W, 2 * GLA_KW, GLA_W)
    gg = _mm_nn("proj_gg", u, w_g, F32, GLA_W, 2 * GLA_KW + GLA_W, GLA_W)
    glr = _mm_nn("proj_glr", u, w_g, F32, 128, 2 * GLA_KW + 2 * GLA_W, 128)
    mg = _mm_nn("proj_mg", u, w_m, F32, W_M, 0, W_M)

    o_ret, a_ret, st_ret, sc_ret = _ret_fwd_call(rqk, rv, rg, ret_gain, lgam)
    z_gate, b_dec = _gla_gate_call(glr, wg_pad, b_gate, pmat)
    o_gla, a_gla, st_gla, sc_gla, g_br, g_bg, g_out = _gla_fwd_call(gqk, gv, b_dec, gg, gla_gain,
                                                                    comm=_spread_plan(branch_parts))
    wbr = g_br.reshape(RET_W, D_MODEL)
    wbg = g_bg.reshape(GLA_W, D_MODEL)
    wout = g_out.reshape(D_MODEL, D_MODEL)

    gf = final_gain.reshape(1, D_MODEL)
    (dh1, da_gla, dm, merged_b, dh1_b, dpr_b, dpg_b, loss, dgf) = _mid_call(
        a_ret, a_gla, mg, h0, tgt2d, wbr, wbg, wout, gf)

    names_b = ("w_branch_ret", "w_branch_gla", "w_out")
    g2_b = [_mm_tn("dw_br", a_ret, dpr_b, D_MODEL).reshape(4, 2, RET_W // 8, D_MODEL).transpose(1, 0, 2, 3),
            _mm_tn("dw_bg", a_gla, dpg_b, D_MODEL).reshape(4, 2, GLA_W // 8, D_MODEL).transpose(1, 0, 2, 3),
            _mm_tn("dw_out", merged_b, dh1_b, D_MODEL).reshape(4, 2, D_MODEL // 8, D_MODEL).transpose(1, 0, 2, 3)]
    sib_b = _swap_halves_call("swap_halves_branch", g2_b)
    sum_b = [_add_half_call("add_half_" + nm, g, b, ck) for nm, g, b in zip(names_b, g2_b, sib_b)]
    d_g, db_dec, dgla_gain, *chips_b = _gla_bwd_call(gqk, gv, b_dec, gg, o_gla, da_gla, st_gla, sc_gla, gla_gain,
                                                     comm=_exchange_plan(sum_b))
    d_g, dwg, dbg = _gla_gate_bwd_call(db_dec, z_gate, glr, wg_pad, pmat_t, d_g)
    mine = [_add_chips_call("add_chips_" + nm, g, b, p, ck) for nm, g, b, p in zip(names_b, g2_b, sib_b, chips_b)]

    d_r, dret_gain = _ret_bwd_call(rqk, rv, rg, o_ret, dpr_b, wbr, st_ret, sc_ret, ret_gain, lgam, cos, sin)

    dwp = _mm_tn("dw_r", u, d_r, 2 * D_MODEL, out_cols=IN_PAD)
    dwp = _mm_tn("dw_g", u, d_g, D_MODEL, ncols=W_GP - 128, into=dwp, col0=W_R)
    dwp = _mm_tn("dw_glr", u, d_g, 128, ncols=128, bcol0=W_GP - 128, into=dwp, col0=W_R + W_GP - 128)
    g2_in = _place_merge_cols_call(dwp, _mm_tn("dw_m", u, dm, 2 * D_MODEL)).reshape(2, D_MODEL // 2, IN_PAD)

    du, sib_in = _mm_nt_acc("du_g", d_g, w_g, W_GP, comm=_swap_plan([g2_in]), tb=_proj_rows(tp))
    sum_in = _add_rows_call("add_half_w_in", g2_in, sib_in, ck)
    du, chips_in = _mm_nt_acc("du_r", d_r, w_r, 2 * D_MODEL, acc_in=du, comm=_exchange_window_plan(sum_in),
                              tb=_proj_rows(tp))
    tile = pl.BlockSpec((TB, D_MODEL), lambda i, kk: (i, 0))
    row = pl.BlockSpec((1, D_MODEL), lambda i, kk: (0, 0))
    dx, dmeta, dnorm_gain = _mm_nt_acc(
        "du_m", dm, w_m, W_M, acc_in=du, epilogue=_rms_bwd_epilogue, extras=(h0, norm_gain, dh1),
        extra_specs=(tile, row, tile),
        extra_out_shapes=(jax.ShapeDtypeStruct((seq, D_MODEL), F32), jax.ShapeDtypeStruct((N_META, D_MODEL), F32),
                          jax.ShapeDtypeStruct((1, D_MODEL), F32)),
        extra_out_specs=(ANY, pl.BlockSpec((N_META, D_MODEL), lambda i, kk: (0, 0)), row),
        extra_scratch=(pltpu.VMEM((2, TB, D_MODEL), F32), pltpu.SemaphoreType.DMA((2,))))
    mine = [_add_window_call("add_chips_w_in", g2_in, sib_in, chips_in, ck)] + mine
    full = _join_halves_call("join_halves", mine)

    return dict(loss=loss[0, 0], dx=dx, dmeta=dmeta, norm_gain=dnorm_gain, w_gate_up=dwg[:GATE_RANK], b_gate=dbg,
                ret_norm_gain=dret_gain, gla_norm_gain=dgla_gain, final_norm_gain=dgf.reshape(D_MODEL),
                w_in=full[0], w_branch_ret=full[1], w_branch_gla=full[2], w_out=full[3])


MESH = pl.DeviceIdType.MESH
ANY = pl.BlockSpec(memory_space=pl.ANY)


def _place():
    return lax.axis_index("x"), lax.axis_index("y"), lax.axis_index("c")


def _gather8_call(name, parts):
    comm = _gather_plan(parts)
    n = len(parts)

    def body(*refs):
        begin, finish = comm.make(refs[:n], refs[n:2 * n], refs[-2], refs[-1])
        begin()
        finish()

    return pl.pallas_call(
        body, name=name, out_shape=list(comm.out_shapes), in_specs=[ANY] * n, out_specs=[ANY] * n,
        scratch_shapes=_comm_sems(comm),
    )(*parts)


def _swap_halves_call(name, gs):
    n = len(gs)

    def body(*refs):
        g_refs, b_refs = refs[:n], refs[n:2 * n]
        send_sems, recv_sems = refs[2 * n:]
        x, y, c = _place()
        copies = [pltpu.make_async_remote_copy(
            src_ref=g_refs[t].at[1 - c], dst_ref=b_refs[t], send_sem=send_sems.at[t], recv_sem=recv_sems.at[t],
            device_id=(x, y, 1 - c), device_id_type=MESH) for t in range(n)]
        for cp in copies:
            cp.start()
        for cp in copies:
            cp.wait()

    return pl.pallas_call(
        body, name=name,
        out_shape=[jax.ShapeDtypeStruct(g.shape[1:], g.dtype) for g in gs],
        in_specs=[ANY] * n, out_specs=[ANY] * n,
        scratch_shapes=[pltpu.SemaphoreType.DMA((n,)), pltpu.SemaphoreType.DMA((n,))],
    )(*gs)


def _join_halves_call(name, ts):
    n = len(ts)

    def body(*refs):
        o_refs = refs[n:2 * n]
        send_sems, recv_sems = refs[2 * n:]
        x, y, c = _place()
        copies = [pltpu.make_async_remote_copy(
            src_ref=o_refs[t].at[c], dst_ref=o_refs[t].at[c], send_sem=send_sems.at[t], recv_sem=recv_sems.at[t],
            device_id=(x, y, 1 - c), device_id_type=MESH) for t in range(n)]
        for cp in copies:
            cp.start()
        for t in range(n):
            copies[t].wait_send()
            pltpu.make_async_remote_copy(
                src_ref=o_refs[t].at[c], dst_ref=o_refs[t].at[1 - c], send_sem=send_sems.at[t],
                recv_sem=recv_sems.at[t], device_id=(x, y, 1 - c), device_id_type=MESH).wait_recv()

    return pl.pallas_call(
        body, name=name,
        out_shape=[jax.ShapeDtypeStruct(t.shape, t.dtype) for t in ts],
        in_specs=[ANY] * n, out_specs=[ANY] * n, input_output_aliases={t: t for t in range(n)},
        scratch_shapes=[pltpu.SemaphoreType.DMA((n,)), pltpu.SemaphoreType.DMA((n,))],
    )(*ts)


def _row_block(rows, cols, budget):
    best = 8
    for rb in range(8, rows + 1, 8):
        if rows % rb == 0 and rb * cols * 4 <= budget:
            best = rb
    return best


def _add_half_call(name, g, b, ck):
    _, _, r, cc = g.shape
    rb = _row_block(r, cc, 2 * 1024 * 1024)

    def body(ck_ref, g_ref, b_ref, o_ref):
        o_ref[...] = (g_ref[...] + b_ref[...]).astype(BF16)

    return pl.pallas_call(
        body, name=name,
        grid_spec=pltpu.PrefetchScalarGridSpec(
            num_scalar_prefetch=1, grid=(4, r // rb),
            in_specs=[pl.BlockSpec((None, None, rb, cc), lambda k, i, ck_ref: (ck_ref[0], k, i, 0)),
                      pl.BlockSpec((None, rb, cc), lambda k, i, ck_ref: (k, i, 0))],
            out_specs=pl.BlockSpec((None, rb, cc), lambda k, i, ck_ref: (k, i, 0))),
        out_shape=jax.ShapeDtypeStruct(b.shape, BF16),
        compiler_params=_cparams(2),
    )(ck, g, b)


def _add_rows_call(name, g, b, ck):
    _, r, cc = g.shape
    rb = _row_block(r, cc, 2 * 1024 * 1024)

    def body(ck_ref, g_ref, b_ref, o_ref):
        o_ref[...] = (g_ref[...] + b_ref[...]).astype(BF16)

    return pl.pallas_call(
        body, name=name,
        grid_spec=pltpu.PrefetchScalarGridSpec(
            num_scalar_prefetch=1, grid=(r // rb,),
            in_specs=[pl.BlockSpec((None, rb, cc), lambda i, ck_ref: (ck_ref[0], i, 0)),
                      pl.BlockSpec((rb, cc), lambda i, ck_ref: (i, 0))],
            out_specs=pl.BlockSpec((rb, cc), lambda i, ck_ref: (i, 0))),
        out_shape=jax.ShapeDtypeStruct((r, cc), BF16),
        compiler_params=_cparams(1),
    )(ck, g, b)


def _add_window_call(name, g, b, p, ck):
    _, r, _ = g.shape
    nb, step = WIN_W // 128, WIN_STEP // 128

    def body(ck_ref, g_ref, b_ref, p0_ref, p1_ref, p2_ref, o_ref):
        own = g_ref[...] + b_ref[...]
        o_ref[...] = ((own + p0_ref[...].astype(F32)) + p1_ref[...].astype(F32)) + p2_ref[...].astype(F32)

    def peer(j):
        return pl.BlockSpec((None, r, 128), lambda i, ck_ref: (j, 0, i))

    return pl.pallas_call(
        body, name=name,
        grid_spec=pltpu.PrefetchScalarGridSpec(
            num_scalar_prefetch=1, grid=(nb,),
            in_specs=[pl.BlockSpec((None, r, 128), lambda i, ck_ref: (ck_ref[0], 0, step * ck_ref[1] + i)),
                      pl.BlockSpec((r, 128), lambda i, ck_ref: (0, step * ck_ref[1] + i)),
                      peer(0), peer(1), peer(2)],
            out_specs=pl.BlockSpec((None, r, 128), lambda i, ck_ref: (ck_ref[0], 0, i))),
        out_shape=jax.ShapeDtypeStruct((2, r, WIN_W), F32),
        compiler_params=_cparams(1),
    )(ck, g, b, p, p, p)


def _add_chips_call(name, g, b, p, ck):
    _, _, r, cc = g.shape
    rb = _row_block(r, cc, 2 * 1024 * 1024)

    def body(ck_ref, g_ref, b_ref, p0_ref, p1_ref, p2_ref, o_ref):
        own = g_ref[...] + b_ref[...]
        o_ref[...] = ((own + p0_ref[...].astype(F32)) + p1_ref[...].astype(F32)) + p2_ref[...].astype(F32)

    def peer(j):
        return pl.BlockSpec((None, rb, cc), lambda i, ck_ref: (j, i, 0))

    return pl.pallas_call(
        body, name=name,
        grid_spec=pltpu.PrefetchScalarGridSpec(
            num_scalar_prefetch=1, grid=(r // rb,),
            in_specs=[pl.BlockSpec((None, None, rb, cc), lambda i, ck_ref: (ck_ref[0], ck_ref[1], i, 0)),
                      pl.BlockSpec((None, rb, cc), lambda i, ck_ref: (ck_ref[1], i, 0)),
                      peer(0), peer(1), peer(2)],
            out_specs=pl.BlockSpec((None, rb, cc), lambda i, ck_ref: (ck_ref[0], i, 0))),
        out_shape=jax.ShapeDtypeStruct((2, r, cc), F32),
        compiler_params=_cparams(1),
    )(ck, g, b, p, p, p)


def _sum8_call(name, g):
    def body(g_ref, o_ref):
        acc = g_ref[0]
        for d in range(1, 8):
            acc = acc + g_ref[d]
        o_ref[...] = acc

    return pl.pallas_call(body, name=name, out_shape=jax.ShapeDtypeStruct(g.shape[1:], F32))(g)


def _adamw_call(name, w, g, m, v):
    r, cc = w.shape
    if r % 8 == 0 or r * cc * 4 <= 1024 * 1024:
        rb = _row_block(r, cc, 1024 * 1024) if r % 8 == 0 else r
        grid, spec = (r // rb,), pl.BlockSpec((rb, cc), lambda i: (i, 0))
    else:
        grid, spec = (cc // 128,), pl.BlockSpec((r, 128), lambda i: (0, i))

    def body(w_ref, g_ref, m_ref, v_ref, d_ref, m2_ref, v2_ref):
        gv = g_ref[...]
        m2 = ADAM_B1 * m_ref[...] + (1.0 - ADAM_B1) * gv
        v2 = ADAM_B2 * v_ref[...] + (1.0 - ADAM_B2) * (gv * gv)
        m_hat = m2 / (1.0 - ADAM_B1 ** ADAM_STEP)
        v_hat = v2 / (1.0 - ADAM_B2 ** ADAM_STEP)
        d_ref[...] = -ADAM_LR * (m_hat / (jnp.sqrt(v_hat) + ADAM_EPS) + ADAM_WD * w_ref[...])
        m2_ref[...] = m2
        v2_ref[...] = v2

    return pl.pallas_call(
        body, name=name, grid=grid, in_specs=[spec] * 4, out_specs=[spec] * 3,
        out_shape=[jax.ShapeDtypeStruct((r, cc), F32)] * 3, compiler_params=_cparams(1),
    )(w, g, m, v)


SMALL = (("norm_gain", D_MODEL), ("b_gate", GLA_KW), ("ret_norm_gain", RET_W), ("gla_norm_gain", GLA_W),
         ("final_norm_gain", D_MODEL), ("w_gate_up", GATE_RANK * GLA_KW), ("meta_tokens", N_META * D_MODEL),
         ("loss", 1))


def _pack_rows(vecs, rows):
    flat = jnp.concatenate([v.reshape(-1) for v in vecs])
    return jnp.pad(flat, (0, rows * 128 - flat.shape[0])).reshape(rows, 128)


def kernel(x, meta_tokens, norm_gain, w_in, w_gate_up, b_gate, ret_norm_gain, gla_norm_gain, w_branch_ret, w_branch_gla, w_out, final_norm_gain, loss_target, m_meta_tokens, m_norm_gain, m_w_in, m_w_gate_up, m_b_gate, m_ret_norm_gain, m_gla_norm_gain, m_w_branch_ret, m_w_branch_gla, m_w_out, m_final_norm_gain, v_meta_tokens, v_norm_gain, v_w_in, v_w_gate_up, v_b_gate, v_ret_norm_gain, v_gla_norm_gain, v_w_branch_ret, v_w_branch_gla, v_w_out, v_final_norm_gain):
    xi, yi, ci = _place()
    kme = 2 * xi + yi
    ck = jnp.stack([ci, kme]).astype(jnp.int32)
    sw_in = w_in.shape[2]

    def my_half(a, dtype):
        r, cc = a.shape
        return lax.dynamic_index_in_dim(a.reshape(2, r // 2, cc), ci, 0, keepdims=False).astype(dtype)

    g_meta, g_wg = _gather8_call("gather_small_weights", [my_half(meta_tokens, F32), my_half(w_gate_up[0], F32)])
    branch_parts = [my_half(w_branch_ret[0], BF16), my_half(w_branch_gla[0], BF16), my_half(w_out[0], BF16)]
    meta = g_meta.reshape(4, 2, N_META // 2, D_MODEL // 4).transpose(1, 2, 0, 3).reshape(N_META, D_MODEL)
    wg_full = g_wg.reshape(4, 2, GATE_RANK // 2, GLA_KW // 4).transpose(1, 2, 0, 3).reshape(GATE_RANK, GLA_KW)

    loc = _device_step(x[0], loss_target[0], meta, norm_gain, my_half(w_in[0], BF16), wg_full, b_gate, ret_norm_gain,
                       gla_norm_gain,
                       branch_parts, final_norm_gain, ck)
    names = ("w_in", "w_branch_ret", "w_branch_gla", "w_out")
    full = [loc[nm] for nm in names]
    big_w = dict(w_in=w_in[0], w_branch_ret=w_branch_ret[0], w_branch_gla=w_branch_gla[0], w_out=w_out[0])
    big_m = dict(w_in=m_w_in[0], w_branch_ret=m_w_branch_ret[0], w_branch_gla=m_w_branch_gla[0], w_out=m_w_out[0])
    big_v = dict(w_in=v_w_in[0], w_branch_ret=v_w_branch_ret[0], w_branch_gla=v_w_branch_gla[0], w_out=v_w_out[0])
    grads, deltas, new_m, new_v = {}, {}, {}, {}
    for nm, f in zip(names, full):
        shape = big_w[nm].shape
        if nm == "w_in":
            f = lax.dynamic_slice_in_dim(f, (sw_in - WIN_STEP) * kme, sw_in, axis=2)
        g = f.reshape(shape)
        if nm == "w_in":
            d, m2, v2 = (a.T for a in _adamw_call("adamw_" + nm, big_w[nm].T, g.T, big_m[nm].T, big_v[nm].T))
        else:
            d, m2, v2 = _adamw_call("adamw_" + nm, big_w[nm], g, big_m[nm], big_v[nm])
        grads[nm], deltas[nm], new_m[nm], new_v[nm] = (a.reshape((1,) + shape) for a in (g, d, m2, v2))

    small_g = dict(loc)
    small_g["meta_tokens"] = loc["dmeta"]
    n_small = sum(sz for _, sz in SMALL)
    rows = -(-n_small // 128 // 8) * 8
    (g_small,) = _gather8_call("gather_small_grads", [_pack_rows([small_g[nm] for nm, _ in SMALL], rows)])
    tot = _sum8_call("sum_small_grads", g_small).reshape(-1)
    off = 0
    sg = {}
    for nm, sz in SMALL:
        sg[nm] = tot[off:off + sz]
        off += sz
    loss = sg.pop("loss")[0]
    sg["w_gate_up"] = lax.dynamic_slice_in_dim(sg["w_gate_up"].reshape(GATE_RANK, GLA_KW), kme * (GLA_KW // 4),
                                               GLA_KW // 4, axis=1)
    sg["meta_tokens"] = lax.dynamic_slice_in_dim(sg["meta_tokens"].reshape(N_META, D_MODEL), kme * (D_MODEL // 4),
                                                 D_MODEL // 4, axis=1)
    small_w = dict(norm_gain=norm_gain, b_gate=b_gate, ret_norm_gain=ret_norm_gain, gla_norm_gain=gla_norm_gain,
                   final_norm_gain=final_norm_gain, w_gate_up=w_gate_up, meta_tokens=meta_tokens)
    small_m = dict(norm_gain=m_norm_gain, b_gate=m_b_gate, ret_norm_gain=m_ret_norm_gain,
                   gla_norm_gain=m_gla_norm_gain, final_norm_gain=m_final_norm_gain, w_gate_up=m_w_gate_up,
                   meta_tokens=m_meta_tokens)
    small_v = dict(norm_gain=v_norm_gain, b_gate=v_b_gate, ret_norm_gain=v_ret_norm_gain,
                   gla_norm_gain=v_gla_norm_gain, final_norm_gain=v_final_norm_gain, w_gate_up=v_w_gate_up,
                   meta_tokens=v_meta_tokens)
    for nm in small_w:
        shape = small_w[nm].shape
        as2d = lambda a: a.reshape((-1, shape[-1]))
        grads[nm] = sg[nm].reshape(shape)
        deltas[nm], new_m[nm], new_v[nm] = (a.reshape(shape) for a in _adamw_call(
            "adamw_" + nm, as2d(small_w[nm]), as2d(sg[nm]), as2d(small_m[nm]), as2d(small_v[nm])))

    out_order = ("meta_tokens", "norm_gain", "w_in", "w_gate_up", "b_gate", "ret_norm_gain", "gla_norm_gain",
                 "w_branch_ret", "w_branch_gla", "w_out", "final_norm_gain")
    dx = loc["dx"].reshape(x.shape)
    return (loss, dx, *[grads[nm] for nm in out_order], *[deltas[nm] for nm in out_order],
            *[new_m[nm] for nm in out_order], *[new_v[nm] for nm in out_order])
```

```python
import math
from typing import Callable, NamedTuple

import numpy as np
import jax
import jax.numpy as jnp
from jax import lax
from jax.experimental import pallas as pl
from jax.experimental.pallas import tpu as pltpu

F32 = jnp.float32
BF16 = jnp.bfloat16

D_MODEL = 1024
N_META = 16
EPS = 1e-6
ROPE_BASE = 10000.0
RET_HEADS, RET_QK, RET_V = 4, 256, 512
RET_W = RET_HEADS * RET_V
GLA_HEADS, GLA_K, GLA_V = 4, 128, 256
GLA_W = GLA_HEADS * GLA_V
GLA_KW = GLA_HEADS * GLA_K
GATE_RANK = 16
GATE_TAU = 16.0
GLA_SUB = 16

TM = 256
T0 = TM
PADF = T0 - N_META
GC = 128
TB = 768
TK = 768

W_R = 6144
W_G = 3088
W_GP = 3200
W_M = 2048
IN_COLS = W_R + W_G + W_M
WIN_STEP = (IN_COLS // 4) // 128 * 128
WIN_W = -(-(3 * (IN_COLS // 4 - WIN_STEP) + IN_COLS // 4) // 128) * 128
IN_PAD = 3 * WIN_STEP + WIN_W

ADAM_LR, ADAM_B1, ADAM_B2, ADAM_EPS, ADAM_WD, ADAM_STEP = 0.001, 0.9, 0.999, 1e-08, 0.01, 10

VMEM_LIMIT = 56 * 1024 * 1024

NN = ((1,), (0,))
NT = ((1,), (1,))
TN = ((0,), (0,))


def _dot(a, b, dims):
    return lax.dot_general(a, b, (dims, ((), ())), preferred_element_type=F32)


def _cparams(n_axes):
    return pltpu.CompilerParams(dimension_semantics=("arbitrary",) * n_axes, vmem_limit_bytes=VMEM_LIMIT)


def _sigmoid(x):
    return 0.5 * jnp.tanh(0.5 * x) + 0.5


def _silu(x):
    h = 0.5 * x
    return h + h * jnp.tanh(h)


def _head_mean(x):
    w = x.shape[-1]
    return _dot(x.astype(BF16), jnp.ones((w, 128), BF16), NN)[:, :1] * (1.0 / w)


def _split3(x):
    hi = x.astype(BF16)
    r1 = x - hi.astype(F32)
    mid = r1.astype(BF16)
    lo = (r1 - mid.astype(F32)).astype(BF16)
    return hi, mid, lo


def _exact_pm(p, x):
    hi, mid, lo = _split3(x)
    return _dot(p, hi, NN) + _dot(p, mid, NN) + _dot(p, lo, NN)


def _rms_call(x2d, head, gain, comm):
    tp = T0 + x2d.shape[0]
    nt = tp // TM
    n_xc = len(comm.srcs)

    def body(x_ref, hd_ref, g_ref, *rest):
        xc_src = rest[:n_xc]
        h_ref, u_ref = rest[n_xc:n_xc + 2]
        xc_dst = rest[n_xc + 2:2 * n_xc + 2]
        i = pl.program_id(0)
        begin, finish = comm.make(xc_src, xc_dst, rest[-2], rest[-1])
        pl.when(i == 0)(begin)
        h = jnp.where(i == 0, hd_ref[...], x_ref[...])
        h_ref[...] = h
        r = lax.rsqrt(jnp.mean(h * h, axis=-1, keepdims=True) + EPS)
        u_ref[...] = (h * r * g_ref[...]).astype(BF16)
        pl.when(i == nt - 1)(finish)

    tile = pl.BlockSpec((TM, D_MODEL), lambda i: (i, 0))
    return pl.pallas_call(
        body, name="rms_in", grid=(nt,),
        in_specs=[pl.BlockSpec((TM, D_MODEL), lambda i: (jnp.maximum(i - 1, 0), 0)),
                  pl.BlockSpec((T0, D_MODEL), lambda i: (0, 0)), pl.BlockSpec((1, D_MODEL), lambda i: (0, 0))]
        + [ANY] * n_xc,
        out_specs=[tile, tile] + [ANY] * n_xc,
        out_shape=[jax.ShapeDtypeStruct((tp, D_MODEL), F32), jax.ShapeDtypeStruct((tp, D_MODEL), BF16)]
        + list(comm.out_shapes),
        scratch_shapes=_comm_sems(comm), compiler_params=_cparams(1),
    )(x2d, head, gain, *comm.srcs)


PROJ_ROWS_MAX = 1408


def _proj_rows(m):
    return max(r for r in range(16, PROJ_ROWS_MAX + 1, 16) if m % r == 0)


def _mm_nn(name, a, b, out_dtype, tn, col0, ncols, epilogue=None, extras=(), extra_specs=()):
    m, k = a.shape
    nj, j0 = ncols // tn, col0 // tn
    tb = _proj_rows(m)

    def body(a_ref, b_ref, *rest):
        *ex, o_ref = rest
        acc = _dot(a_ref[...], b_ref[...], NN)
        if epilogue is None:
            o_ref[...] = acc.astype(out_dtype)
        else:
            epilogue(acc, o_ref, *ex)

    return pl.pallas_call(
        body, name=name, grid=(nj, m // tb),
        in_specs=[pl.BlockSpec((tb, k), lambda j, i: (i, 0)), pl.BlockSpec((k, tn), lambda j, i: (0, j0 + j))]
        + list(extra_specs),
        out_specs=pl.BlockSpec((tb, tn), lambda j, i: (i, j)),
        out_shape=jax.ShapeDtypeStruct((m, ncols), out_dtype),
        compiler_params=_cparams(2),
    )(a, b, *extras)


def _rope_tables(tp):
    half = RET_QK // 2
    pos = np.arange(tp, dtype=np.float32) - np.float32(PADF)
    inv = (ROPE_BASE ** (-np.arange(half, dtype=np.float64) / half)).astype(np.float32)
    ang = (pos[:, None] * inv[None, :]).astype(np.float64)
    return np.cos(ang).astype(np.float32), np.sin(ang).astype(np.float32)


def _rope_epilogue(acc, o_ref, cos_ref, sin_ref):
    scale = jnp.where(pl.program_id(0) == 1, RET_QK ** -0.5, 1.0).astype(F32)
    cos, sin = cos_ref[...], sin_ref[...]
    half = RET_QK // 2
    for h in range(RET_HEADS):
        t1 = acc[:, h * RET_QK:h * RET_QK + half]
        t2 = acc[:, h * RET_QK + half:(h + 1) * RET_QK]
        o_ref[:, h * RET_QK:h * RET_QK + half] = ((t1 * cos - t2 * sin) * scale).astype(BF16)
        o_ref[:, h * RET_QK + half:(h + 1) * RET_QK] = ((t2 * cos + t1 * sin) * scale).astype(BF16)


def _gqk_epilogue(acc, o_ref):
    o_ref[:, :GLA_KW] = acc[:, :GLA_KW] * (GLA_K ** -0.5)
    o_ref[:, GLA_KW:] = acc[:, GLA_KW:]


class _Comm(NamedTuple):
    srcs: tuple
    out_shapes: tuple
    n_sems: int
    make: Callable


def _comm_sems(comm):
    return [pltpu.SemaphoreType.DMA((comm.n_sems,)), pltpu.SemaphoreType.DMA((comm.n_sems,))]


def _start_wait(copies):
    def begin():
        for cp in copies:
            cp.start()

    def finish():
        for cp in copies:
            cp.wait()

    return begin, finish


def _other_chips(x, y):
    return [(1 - x, y), (x, 1 - y), (1 - x, 1 - y)]


def _gather_plan(parts, relay=()):
    n = len(parts)
    relay = tuple(relay) + (False,) * (n - len(relay))

    def make(x_refs, out_refs, send_sems, recv_sems):
        x, y, c = _place()
        me, sibling = (x, y, c), (x, y, 1 - c)
        xn, yn, dg = (1 - x, y), (x, 1 - y), (1 - x, 1 - y)

        def slot(t, px, py, pc, half=None):
            ref = out_refs[t].at[4 * px + 2 * py + pc]
            if half is None:
                return ref
            rows = ref.shape[0] // 2
            return ref.at[pl.ds(half * rows, rows)]

        def copy(t, k, dst, to, src=None):
            return pltpu.make_async_remote_copy(
                src_ref=dst if src is None else src, dst_ref=dst, send_sem=send_sems.at[8 * t + k],
                recv_sem=recv_sems.at[8 * t + k], device_id=to, device_id_type=MESH)

        mine = [pltpu.make_async_copy(x_refs[t], slot(t, *me), send_sems.at[8 * n + t]) for t in range(n)]
        sent = []
        for t in range(n):
            sent.append(copy(t, 0, slot(t, *me), sibling, src=x_refs[t]))
            sent.append(copy(t, 1, slot(t, *me), (*xn, c), src=x_refs[t]))
            sent.append(copy(t, 2, slot(t, *me), (*yn, c), src=x_refs[t]))
            if not relay[t]:
                sent.append(copy(t, 3, slot(t, *me), (*dg, c), src=x_refs[t]))

        def begin():
            for cp in mine + sent:
                cp.start()

        def finish():
            later = []

            def start(cp):
                cp.start()
                later.append(cp)

            for t in range(n):
                copy(t, 2, slot(t, *yn, c), me).wait_recv()
                if relay[t]:
                    start(copy(t, 3, slot(t, *yn, c, half=0), (*xn, c)))
                start(copy(t, 6, slot(t, *yn, c), sibling))
            for t in range(n):
                copy(t, 1, slot(t, *xn, c), me).wait_recv()
                if relay[t]:
                    start(copy(t, 4, slot(t, *xn, c, half=1), (*yn, c)))
                start(copy(t, 5, slot(t, *xn, c), sibling))
            for t in range(n):
                if relay[t]:
                    copy(t, 3, slot(t, *dg, c, half=0), me).wait_recv()
                    copy(t, 4, slot(t, *dg, c, half=1), me).wait_recv()
                else:
                    copy(t, 3, slot(t, *dg, c), me).wait_recv()
                start(copy(t, 7, slot(t, *dg, c), sibling))
            for t in range(n):
                copy(t, 0, slot(t, *sibling), me).wait_recv()
                copy(t, 5, slot(t, *xn, 1 - c), me).wait_recv()
                copy(t, 6, slot(t, *yn, 1 - c), me).wait_recv()
                copy(t, 7, slot(t, *dg, 1 - c), me).wait_recv()
            for cp in sent + later:
                cp.wait_send()
            for cp in mine:
                cp.wait()

        return begin, finish

    return _Comm(tuple(parts), tuple(jax.ShapeDtypeStruct((8,) + p.shape, p.dtype) for p in parts), 9 * n, make)


def _exchange_plan(ss):
    def make(s_refs, b_refs, send_sems, recv_sems):
        x, y, c = _place()
        return _start_wait([pltpu.make_async_remote_copy(
            src_ref=s_refs[t].at[2 * chip[0] + chip[1]], dst_ref=b_refs[t].at[j], send_sem=send_sems.at[3 * t + j],
            recv_sem=recv_sems.at[3 * t + j], device_id=(*chip, c), device_id_type=MESH)
            for t in range(len(s_refs)) for j, chip in enumerate(_other_chips(x, y))])

    return _Comm(tuple(ss), tuple(jax.ShapeDtypeStruct((3,) + s.shape[1:], s.dtype) for s in ss), 3 * len(ss), make)


def _exchange_window_plan(s):
    def make(s_refs, b_refs, send_sems, recv_sems):
        x, y, c = _place()
        return _start_wait([pltpu.make_async_remote_copy(
            src_ref=s_refs[0].at[:, pl.ds(pl.multiple_of((2 * chip[0] + chip[1]) * WIN_STEP, 128), WIN_W)],
            dst_ref=b_refs[0].at[j], send_sem=send_sems.at[j], recv_sem=recv_sems.at[j], device_id=(*chip, c),
            device_id_type=MESH) for j, chip in enumerate(_other_chips(x, y))])

    return _Comm((s,), (jax.ShapeDtypeStruct((3, s.shape[0], WIN_W), s.dtype),), 3, make)


def _swap_plan(gs):
    def make(g_refs, b_refs, send_sems, recv_sems):
        x, y, c = _place()
        return _start_wait([pltpu.make_async_remote_copy(
            src_ref=g_refs[t].at[1 - c], dst_ref=b_refs[t], send_sem=send_sems.at[t], recv_sem=recv_sems.at[t],
            device_id=(x, y, 1 - c), device_id_type=MESH) for t in range(len(g_refs))])

    return _Comm(tuple(gs), tuple(jax.ShapeDtypeStruct(g.shape[1:], g.dtype) for g in gs), len(gs), make)


def _spread_plan(parts):
    def make(p_refs, o_refs, send_sems, recv_sems):
        x, y, c = _place()
        copies = []
        for t in range(len(p_refs)):
            mine = o_refs[t].at[4 * x + 2 * y + c]
            copies.append(pltpu.make_async_copy(p_refs[t], mine, send_sems.at[7 * len(p_refs) + t]))
            for r in range(1, 8):
                peer = (1 - x if r & 4 else x, 1 - y if r & 2 else y, 1 - c if r & 1 else c)
                copies.append(pltpu.make_async_remote_copy(
                    src_ref=p_refs[t], dst_ref=mine, send_sem=send_sems.at[7 * t + r - 1],
                    recv_sem=recv_sems.at[7 * t + r - 1], device_id=peer, device_id_type=MESH))
        return _start_wait(copies)

    return _Comm(tuple(parts), tuple(jax.ShapeDtypeStruct((8,) + p.shape, p.dtype) for p in parts), 8 * len(parts),
                 make)


def _mm_nt_acc(name, a, w, tk, acc_in=None, epilogue=None, extras=(), extra_specs=(), extra_out_shapes=(),
               extra_out_specs=(), extra_scratch=(), comm=None, tb=TB):
    m, k = a.shape
    n = w.shape[0]
    nk, ni = k // tk, m // tb
    has_acc = acc_in is not None
    n_xc = len(comm.srcs) if comm else 0
    n_es = len(extra_scratch)

    def body(*refs):
        a_ref, w_ref = refs[0], refs[1]
        pos = 2
        acc_ref = None
        if has_acc:
            acc_ref = refs[pos]
            pos += 1
        ex = refs[pos:pos + len(extras)]
        pos += len(extras)
        xc_src = refs[pos:pos + n_xc]
        pos += n_xc
        n_scr = 1 + n_es + (2 if n_xc else 0)
        outs = refs[pos:len(refs) - n_scr - n_xc]
        xc_dst = refs[len(refs) - n_scr - n_xc:len(refs) - n_scr]
        scr = refs[len(refs) - n_scr]
        es = refs[len(refs) - n_scr + 1:len(refs) - n_scr + 1 + n_es]
        i, kk = pl.program_id(0), pl.program_id(1)
        if n_xc:
            begin, finish = comm.make(xc_src, xc_dst, refs[-2], refs[-1])
            pl.when((i == 0) & (kk == 0))(begin)

        @pl.when(kk == 0)
        def _():
            scr[...] = acc_ref[...] if has_acc else jnp.zeros_like(scr)

        scr[...] += _dot(a_ref[...], w_ref[...], NT)

        @pl.when(kk == nk - 1)
        def _():
            if epilogue is None:
                outs[0][...] = scr[...]
            else:
                epilogue(scr[...], outs, i, ni, *ex, *es)

        if n_xc:
            pl.when((i == ni - 1) & (kk == nk - 1))(finish)

    in_specs = [pl.BlockSpec((tb, tk), lambda i, kk: (i, kk)), pl.BlockSpec((n, tk), lambda i, kk: (0, kk))]
    args = [a, w]
    if has_acc:
        in_specs.append(pl.BlockSpec((tb, n), lambda i, kk: (i, 0)))
        args.append(acc_in)
    in_specs += list(extra_specs) + [ANY] * n_xc
    args += list(extras) + (list(comm.srcs) if comm else [])
    if epilogue is None:
        out_shape = [jax.ShapeDtypeStruct((m, n), F32)]
        out_specs = [pl.BlockSpec((tb, n), lambda i, kk: (i, 0))]
    else:
        out_shape, out_specs = list(extra_out_shapes), list(extra_out_specs)
    scratch = [pltpu.VMEM((tb, n), F32)] + list(extra_scratch)
    if n_xc:
        out_shape += list(comm.out_shapes)
        out_specs += [ANY] * n_xc
        scratch += _comm_sems(comm)
    return pl.pallas_call(
        body, name=name, grid=(ni, nk), in_specs=in_specs, out_specs=out_specs, out_shape=out_shape,
        scratch_shapes=scratch, compiler_params=_cparams(2),
    )(*args)


def _rms_bwd_epilogue(du, outs, i, ni, h_ref, g_ref, dh1_ref, obuf, sems):
    dx_ref, dmeta_ref, dg_ref = outs
    h = h_ref[...]
    r = lax.rsqrt(jnp.mean(h * h, axis=-1, keepdims=True) + EPS)
    xh = h * r
    dxh = du * g_ref[...]
    dh0 = dh1_ref[...] + r * (dxh - xh * jnp.mean(dxh * xh, axis=-1, keepdims=True))

    def put(slot, tile):
        return pltpu.make_async_copy(obuf.at[slot], dx_ref.at[pl.ds(pl.multiple_of(tile * TB - T0, 8), TB)],
                                     sems.at[slot])

    @pl.when(i == 0)
    def _():
        dg_ref[...] = jnp.zeros_like(dg_ref)
        dmeta_ref[...] = dh0[PADF:T0, :]
        obuf[0] = dh0
        first = pltpu.make_async_copy(obuf.at[0, pl.ds(T0, TB - T0)], dx_ref.at[pl.ds(0, TB - T0)], sems.at[0])
        first.start()
        first.wait()

    @pl.when(i >= 1)
    def _():
        slot = i % 2

        @pl.when(i >= 3)
        def _():
            put(slot, i - 2).wait()

        obuf[slot] = dh0
        put(slot, i).start()

    dg_ref[...] += jnp.sum(du * xh, axis=0, keepdims=True)

    @pl.when(i == ni - 1)
    def _():
        for tile in (ni - 2, ni - 1):
            if tile >= 1:
                put(tile % 2, tile).wait()


def _mm_tn(name, a, b, bn, ncols=None, bcol0=0, into=None, col0=0, out_cols=None):
    t, m = a.shape
    n = ncols or b.shape[1]
    j0, bj0 = col0 // bn, bcol0 // bn

    def body(a_ref, b_ref, *rest):
        o_ref = rest[-1]

        @pl.when(pl.program_id(1) == 0)
        def _():
            o_ref[...] = jnp.zeros_like(o_ref)

        o_ref[...] += _dot(a_ref[...], b_ref[...], TN)

    in_specs = [pl.BlockSpec((TK, m), lambda j, kk: (kk, 0)), pl.BlockSpec((TK, bn), lambda j, kk: (kk, bj0 + j))]
    args = [a, b]
    aliases = {}
    if into is not None:
        in_specs.append(ANY)
        args.append(into)
        aliases = {2: 0}
        out_cols = into.shape[1]
    return pl.pallas_call(
        body, name=name, grid=(n // bn, t // TK), in_specs=in_specs,
        out_specs=pl.BlockSpec((m, bn), lambda j, kk: (0, j0 + j)),
        out_shape=jax.ShapeDtypeStruct((m, out_cols or n), F32), input_output_aliases=aliases,
        compiler_params=_cparams(2),
    )(*args)


def _place_merge_cols_call(dwp, dw_m):
    c0 = W_R + W_GP - 128
    tail = IN_PAD - c0
    rows = 256

    def body(m_ref, p_ref, o_ref, buf, low, sem):
        get = pltpu.make_async_copy(o_ref.at[:, pl.ds(c0, 128)], low, sem)
        get.start()
        get.wait()
        for r in range(0, D_MODEL, rows):
            buf[r:r + rows, :] = jnp.concatenate(
                [low[r:r + rows, :GATE_RANK], m_ref[r:r + rows, :],
                 jnp.zeros((rows, tail - GATE_RANK - W_M), F32)], axis=1)
        put = pltpu.make_async_copy(buf, o_ref.at[:, pl.ds(c0, tail)], sem)
        put.start()
        put.wait()

    return pl.pallas_call(
        body, name="place_merge_cols",
        in_specs=[pl.BlockSpec(memory_space=pltpu.VMEM), ANY], out_specs=ANY,
        out_shape=jax.ShapeDtypeStruct(dwp.shape, F32), input_output_aliases={1: 0},
        scratch_shapes=[pltpu.VMEM((D_MODEL, tail), F32), pltpu.VMEM((D_MODEL, 128), F32), pltpu.SemaphoreType.DMA],
        compiler_params=pltpu.CompilerParams(vmem_limit_bytes=VMEM_LIMIT),
    )(dw_m, dwp)


def _ret_fill_decay(lg_ref, dm_scr):
    c = TM
    ii = lax.broadcasted_iota(jnp.int32, (c, c), 0)
    jj = lax.broadcasted_iota(jnp.int32, (c, c), 1)
    rel = (ii - jj).astype(F32)
    for h in range(RET_HEADS):
        dm_scr[h] = jnp.where(rel >= 0, jnp.exp(jnp.maximum(rel, 0.0) * lg_ref[h]), 0.0)


def _ret_consts(lg, dm_ref):
    c = TM
    idx = lax.broadcasted_iota(jnp.int32, (c, 1), 0).astype(F32)
    xi = jnp.exp((idx + 1.0) * lg)
    zeta = jnp.exp((c - 1.0 - idx) * lg)
    gc = jnp.exp(jnp.full((1, 1), c, F32) * lg)
    return dm_ref[...], xi, zeta, gc


def _ret_fwd_call(rqk, rv, rg, gain, lgam):
    tp = rqk.shape[0]
    nc = tp // TM

    def body(lg_ref, qk_ref, v_ref, rg_ref, g_ref, o_ref, a_ref, st_ref, sc_ref, s_scr, dm_scr):
        @pl.when(pl.program_id(0) == 0)
        def _():
            s_scr[...] = jnp.zeros_like(s_scr)
            _ret_fill_decay(lg_ref, dm_scr)

        for h in range(RET_HEADS):
            dm, xi, zeta, gc = _ret_consts(lg_ref[h], dm_scr.at[h])
            q = qk_ref[:, h * RET_QK:(h + 1) * RET_QK]
            k = qk_ref[:, D_MODEL + h * RET_QK:D_MODEL + (h + 1) * RET_QK]
            v = v_ref[:, h * RET_V:(h + 1) * RET_V]
            sb = s_scr[h].astype(BF16)
            st_ref[0, h] = sb
            s = (_dot(q, k, NT) * dm).astype(BF16)
            sc_ref[0, h] = s
            o = _dot(s, v, NN) + xi * _dot(q, sb, NN)
            kz = (k.astype(F32) * zeta).astype(BF16)
            s_scr[h] = gc * s_scr[h] + _dot(kz, v, TN)
            o_ref[:, h * RET_V:(h + 1) * RET_V] = o
            mu = _head_mean(o)
            xc = o - mu
            xh = xc * lax.rsqrt(_head_mean(xc * xc) + EPS)
            a_ref[:, h * RET_V:(h + 1) * RET_V] = (
                xh * g_ref[:, h * RET_V:(h + 1) * RET_V] * _silu(rg_ref[:, h * RET_V:(h + 1) * RET_V])).astype(BF16)

    return pl.pallas_call(
        body, name="ret_fwd", grid=(nc,),
        in_specs=[pl.BlockSpec(memory_space=pltpu.SMEM),
                  pl.BlockSpec((TM, 2 * D_MODEL), lambda n: (n, 0)),
                  pl.BlockSpec((TM, RET_W), lambda n: (n, 0)),
                  pl.BlockSpec((TM, RET_W), lambda n: (n, 0)),
                  pl.BlockSpec((1, RET_W), lambda n: (0, 0))],
        out_specs=[pl.BlockSpec((TM, RET_W), lambda n: (n, 0)),
                   pl.BlockSpec((TM, RET_W), lambda n: (n, 0)),
                   pl.BlockSpec((1, RET_HEADS, RET_QK, RET_V), lambda n: (n, 0, 0, 0)),
                   pl.BlockSpec((1, RET_HEADS, TM, TM), lambda n: (n, 0, 0, 0))],
        out_shape=[jax.ShapeDtypeStruct((tp, RET_W), F32), jax.ShapeDtypeStruct((tp, RET_W), BF16),
                   jax.ShapeDtypeStruct((nc, RET_HEADS, RET_QK, RET_V), BF16),
                   jax.ShapeDtypeStruct((nc, RET_HEADS, TM, TM), BF16)],
        scratch_shapes=[pltpu.VMEM((RET_HEADS, RET_QK, RET_V), F32), pltpu.VMEM((RET_HEADS, TM, TM), F32)],
        compiler_params=_cparams(1),
    )(lgam, rqk, rv, rg, gain)


def _ret_bwd_call(rqk, rv, rg, o_ret, dpr, wbr, states, scores, gain, lgam, cos, sin):
    tp = rqk.shape[0]
    nc = tp // TM
    half = RET_QK // 2

    def body(lg_ref, qk_ref, v_ref, rg_ref, o_ref, dpr_ref, wbr_ref, st_ref, sc_ref, g_ref, cos_ref, sin_ref, dp_ref,
             dg_ref, ds_scr, dm_scr):
        @pl.when(pl.program_id(0) == 0)
        def _():
            ds_scr[...] = jnp.zeros_like(ds_scr)
            dg_ref[...] = jnp.zeros_like(dg_ref)
            _ret_fill_decay(lg_ref, dm_scr)

        cos, sin = cos_ref[...], sin_ref[...]
        for h in range(RET_HEADS):
            hs = slice(h * RET_V, (h + 1) * RET_V)
            dm, xi, zeta, gc = _ret_consts(lg_ref[h], dm_scr.at[h])
            o = o_ref[:, hs]
            mu = _head_mean(o)
            xc = o - mu
            rstd = lax.rsqrt(_head_mean(xc * xc) + EPS)
            xh = xc * rstd
            gain_h = g_ref[:, hs]
            g = rg_ref[:, hs]
            sg = _sigmoid(g)
            silu = g * sg
            dah = _dot(dpr_ref[...], wbr_ref[hs, :], NT)
            dp_ref[:, 4 * D_MODEL + h * RET_V:4 * D_MODEL + (h + 1) * RET_V] = (
                dah * (xh * gain_h) * (sg * (1.0 + g * (1.0 - sg)))).astype(BF16)
            dn = dah * silu
            dg_ref[:, hs] += jnp.sum(dn * xh, axis=0, keepdims=True)
            dxh = dn * gain_h
            do = rstd * (dxh - _head_mean(dxh) - xh * _head_mean(dxh * xh))
            dob = do.astype(BF16)
            q = qk_ref[:, h * RET_QK:(h + 1) * RET_QK]
            k = qk_ref[:, D_MODEL + h * RET_QK:D_MODEL + (h + 1) * RET_QK]
            v = v_ref[:, hs]
            sp = st_ref[0, h]
            ds = ds_scr[h]
            dsb = ds.astype(BF16)
            s = sc_ref[0, h]
            dsc = (_dot(dob, v, NT) * dm).astype(BF16)
            dq = _dot(dsc, k, NN) + xi * _dot(dob, sp, NT)
            dk = _dot(dsc, q, TN) + zeta * _dot(v, dsb, NT)
            kz = (k.astype(F32) * zeta).astype(BF16)
            dv = _dot(s, dob, TN) + _dot(kz, dsb, NN)
            qx = (q.astype(F32) * xi).astype(BF16)
            ds_scr[h] = gc * ds + _dot(qx, dob, TN)
            dp_ref[:, 2 * D_MODEL + h * RET_V:2 * D_MODEL + (h + 1) * RET_V] = dv.astype(BF16)
            dk = dk * (RET_QK ** -0.5)
            for base, t in ((0, dq), (D_MODEL, dk)):
                t1, t2 = t[:, :half], t[:, half:]
                dp_ref[:, base + h * RET_QK:base + h * RET_QK + half] = (t1 * cos + t2 * sin).astype(BF16)
                dp_ref[:, base + h * RET_QK + half:base + (h + 1) * RET_QK] = (t2 * cos - t1 * sin).astype(BF16)

    rev = lambda n: (nc - 1 - n, 0)
    return pl.pallas_call(
        body, name="ret_bwd", grid=(nc,),
        in_specs=[pl.BlockSpec(memory_space=pltpu.SMEM),
                  pl.BlockSpec((TM, 2 * D_MODEL), rev),
                  pl.BlockSpec((TM, RET_W), rev),
                  pl.BlockSpec((TM, RET_W), rev),
                  pl.BlockSpec((TM, RET_W), rev),
                  pl.BlockSpec((TM, D_MODEL), rev),
                  pl.BlockSpec((RET_W, D_MODEL), lambda n: (0, 0)),
                  pl.BlockSpec((1, RET_HEADS, RET_QK, RET_V), lambda n: (nc - 1 - n, 0, 0, 0)),
                  pl.BlockSpec((1, RET_HEADS, TM, TM), lambda n: (nc - 1 - n, 0, 0, 0)),
                  pl.BlockSpec((1, RET_W), lambda n: (0, 0)),
                  pl.BlockSpec((TM, half), rev),
                  pl.BlockSpec((TM, half), rev)],
        out_specs=[pl.BlockSpec((TM, W_R), rev), pl.BlockSpec((1, RET_W), lambda n: (0, 0))],
        out_shape=[jax.ShapeDtypeStruct((tp, W_R), BF16), jax.ShapeDtypeStruct((1, RET_W), F32)],
        scratch_shapes=[pltpu.VMEM((RET_HEADS, RET_QK, RET_V), F32), pltpu.VMEM((RET_HEADS, TM, TM), F32)],
        compiler_params=_cparams(1),
    )(lgam, rqk, rv, rg, o_ret, dpr, wbr, states, scores, gain, cos, sin)


GLA_LEVELS = tuple(GC >> (s + 1) for s in range(int(math.log2(GC // GLA_SUB))))
NLEV = len(GLA_LEVELS)


def _gla_tril():
    return np.tril(np.ones((GC, GC), np.float32))


def _gla_masks():
    ii = lax.broadcasted_iota(jnp.int32, (GC, GC), 0)
    jj = lax.broadcasted_iota(jnp.int32, (GC, GC), 1)
    masks = []
    for m in GLA_LEVELS:
        sh = int(math.log2(2 * m))
        masks.append(((ii >> sh) == (jj >> sh)) & ((ii & m) != 0) & ((jj & m) == 0))
    sh = int(math.log2(GLA_SUB))
    md = ((ii >> sh) == (jj >> sh)) & (jj <= ii)
    row = lax.broadcasted_iota(jnp.int32, (GC, 1), 0)
    second = [(row & m) != 0 for m in GLA_LEVELS]
    return masks, md, second


def _gla_gate_call(glr, wg, bg, pmat):
    tp = glr.shape[0]
    gb = _proj_rows(tp)
    assert gb % GC == 0

    def body(glr_ref, wg_ref, bg_ref, p_ref, z_ref, b_ref):
        z = _dot(glr_ref[...].astype(BF16), wg_ref[...], NN) + bg_ref[...]
        z_ref[...] = z
        la = (jnp.minimum(z, 0.0) - jnp.log1p(jnp.exp(-jnp.abs(z)))) * (1.0 / GATE_TAU)
        for r in range(0, gb, GC):
            b_ref[r:r + GC, :] = _exact_pm(p_ref[...], la[r:r + GC, :])

    tile = pl.BlockSpec((gb, GLA_KW), lambda i: (i, 0))
    return pl.pallas_call(
        body, name="gla_gate", grid=(tp // gb,),
        in_specs=[pl.BlockSpec((gb, 128), lambda i: (i, 0)), pl.BlockSpec((128, GLA_KW), lambda i: (0, 0)),
                  pl.BlockSpec((1, GLA_KW), lambda i: (0, 0)), pl.BlockSpec((GC, GC), lambda i: (0, 0))],
        out_specs=[tile, tile],
        out_shape=[jax.ShapeDtypeStruct((tp, GLA_KW), F32), jax.ShapeDtypeStruct((tp, GLA_KW), F32)],
        compiler_params=_cparams(1),
    )(glr, wg, bg, pmat)


def _gla_gate_bwd_call(db, z, glr, wg, pmat_t, d_g):
    tp = db.shape[0]
    gb = _proj_rows(tp)
    assert gb % GC == 0 and (W_GP - 128) % 128 == 0

    def body(db_ref, z_ref, glr_ref, wg_ref, pt_ref, dgin_ref, dg_ref, dwg_ref, dbg_ref):
        i = pl.program_id(0)

        @pl.when(i == 0)
        def _():
            dwg_ref[...] = jnp.zeros_like(dwg_ref)
            dbg_ref[...] = jnp.zeros_like(dbg_ref)

        dla = jnp.concatenate([_exact_pm(pt_ref[...], db_ref[r:r + GC, :]) for r in range(0, gb, GC)], axis=0)
        row = i * gb + lax.broadcasted_iota(jnp.int32, (gb, 1), 0)
        dz = jnp.where(row >= PADF, dla * (1.0 / GATE_TAU) * _sigmoid(-z_ref[...]), 0.0)
        dzb = dz.astype(BF16)
        dg_ref[...] = _dot(dzb, wg_ref[...], NT).astype(BF16)
        dwg_ref[...] += _dot(glr_ref[...].astype(BF16), dzb, TN)
        dbg_ref[...] += jnp.sum(dz, axis=0, keepdims=True)

    tile = pl.BlockSpec((gb, GLA_KW), lambda i: (i, 0))
    const = lambda i: (0, 0)
    return pl.pallas_call(
        body, name="gla_gate_bwd", grid=(tp // gb,),
        in_specs=[tile, tile, pl.BlockSpec((gb, 128), lambda i: (i, 0)), pl.BlockSpec((128, GLA_KW), const),
                  pl.BlockSpec((GC, GC), const), ANY],
        out_specs=[pl.BlockSpec((gb, 128), lambda i: (i, (W_GP - 128) // 128)), pl.BlockSpec((128, GLA_KW), const),
                   pl.BlockSpec((1, GLA_KW), const)],
        out_shape=[jax.ShapeDtypeStruct(d_g.shape, BF16), jax.ShapeDtypeStruct((128, GLA_KW), F32),
                   jax.ShapeDtypeStruct((1, GLA_KW), F32)],
        input_output_aliases={5: 0}, compiler_params=_cparams(1),
    )(db, z, glr, wg, pmat_t, d_g)


def _gla_row_steps(b_ref, cs, rows, size):
    parts = [jnp.zeros((size, GLA_K), F32) if r is None else jnp.broadcast_to(b_ref[r:r + 1, cs], (size, GLA_K))
             for r in rows]
    return parts[0] if len(parts) == 1 else jnp.concatenate(parts, axis=0)


def _gla_factors(b_ref, h, second):
    cs = slice(h * GLA_K, (h + 1) * GLA_K)
    b = b_ref[:, cs]
    fq, fk = [], []
    for l, m in enumerate(GLA_LEVELS):
        d = b - _gla_row_steps(b_ref, cs, [s + m - 1 for s in range(0, GC, 2 * m)], 2 * m)
        f = jnp.exp(jnp.where(second[l], d, -d))
        fq.append(jnp.where(second[l], f, 0.0))
        fk.append(jnp.where(second[l], 0.0, f))
    dd = b - _gla_row_steps(b_ref, cs, [None] + [s - 1 for s in range(GLA_SUB, GC, GLA_SUB)], GLA_SUB)
    ed = jnp.exp(dd)
    edi = jnp.exp(-dd)
    eb = jnp.exp(b)
    bl = b_ref[GC - 1:GC, cs]
    ee = jnp.exp(bl - b)
    ebl = jnp.exp(bl)
    return fq, fk, ed, edi, eb, ee, ebl


def _gla_scaled(q, k, fq, fk, ed, edi):
    qt = [(q * f).astype(BF16) for f in fq]
    kt = [(k * f).astype(BF16) for f in fk]
    return qt, kt, (q * ed).astype(BF16), (k * edi).astype(BF16)


def _gla_scores(qt, kt, qd, kd, masks, md):
    a = jnp.where(md, _dot(qd, kd, NT), 0.0)
    for l in range(NLEV):
        a = a + jnp.where(masks[l], _dot(qt[l], kt[l], NT), 0.0)
    return a.astype(BF16)


def _gla_fwd_call(gqk, gv, b, gg, gain, comm=None):
    tp = gqk.shape[0]
    nc = tp // GC
    n_xc = len(comm.srcs) if comm else 0

    def body(qk_ref, v_ref, b_scr, gg_ref, g_ref, *rest):
        xc_src = rest[:n_xc]
        o_ref, a_ref, st_ref, am_ref = rest[n_xc:n_xc + 4]
        xc_dst = rest[n_xc + 4:2 * n_xc + 4]
        s_scr = rest[2 * n_xc + 4]
        n = pl.program_id(0)
        if n_xc:
            begin, finish = comm.make(xc_src, xc_dst, rest[-2], rest[-1])
            pl.when(n == 0)(begin)
            pl.when(n == nc - 1)(finish)

        @pl.when(n == 0)
        def _():
            s_scr[...] = jnp.zeros_like(s_scr)

        masks, md, second = _gla_masks()
        for h in range(GLA_HEADS):
            q = qk_ref[:, h * GLA_K:(h + 1) * GLA_K]
            k = qk_ref[:, GLA_KW + h * GLA_K:GLA_KW + (h + 1) * GLA_K]
            vs = slice(h * GLA_V, (h + 1) * GLA_V)
            v = v_ref[:, vs]
            fq, fk, ed, edi, eb, ee, ebl = _gla_factors(b_scr, h, second)
            a = _gla_scores(*_gla_scaled(q, k, fq, fk, ed, edi), masks, md)
            am_ref[0, h] = a
            sb = s_scr[h].astype(BF16)
            st_ref[0, h] = sb
            o = _dot(a, v, NN) + _dot((q * eb).astype(BF16), sb, NT)
            s_scr[h] = s_scr[h] * ebl + _dot(v, (k * ee).astype(BF16), TN)
            o_ref[:, vs] = o
            xh = o * lax.rsqrt(_head_mean(o * o) + EPS)
            a_ref[:, vs] = (xh * g_ref[:, vs] * _silu(gg_ref[:, vs])).astype(BF16)

    return pl.pallas_call(
        body, name="gla_fwd", grid=(nc,),
        in_specs=[pl.BlockSpec((GC, 2 * GLA_KW), lambda n: (n, 0)),
                  pl.BlockSpec((GC, GLA_W), lambda n: (n, 0)),
                  pl.BlockSpec((GC, GLA_KW), lambda n: (n, 0)),
                  pl.BlockSpec((GC, GLA_W), lambda n: (n, 0)),
                  pl.BlockSpec((1, GLA_W), lambda n: (0, 0))] + [ANY] * n_xc,
        out_specs=[pl.BlockSpec((GC, GLA_W), lambda n: (n, 0)),
                   pl.BlockSpec((GC, GLA_W), lambda n: (n, 0)),
                   pl.BlockSpec((1, GLA_HEADS, GLA_V, GLA_K), lambda n: (n, 0, 0, 0)),
                   pl.BlockSpec((1, GLA_HEADS, GC, GC), lambda n: (n, 0, 0, 0))] + [ANY] * n_xc,
        out_shape=[jax.ShapeDtypeStruct((tp, GLA_W), F32), jax.ShapeDtypeStruct((tp, GLA_W), BF16),
                   jax.ShapeDtypeStruct((nc, GLA_HEADS, GLA_V, GLA_K), BF16),
                   jax.ShapeDtypeStruct((nc, GLA_HEADS, GC, GC), BF16)] + (list(comm.out_shapes) if comm else []),
        scratch_shapes=[pltpu.VMEM((GLA_HEADS, GLA_V, GLA_K), F32)] + (_comm_sems(comm) if comm else []),
        compiler_params=_cparams(1),
    )(gqk, gv, b, gg, gain, *(comm.srcs if comm else ()))


def _gla_bwd_call(gqk, gv, b, gg, o_gla, da, states, scores, gain, comm=None):
    tp = gqk.shape[0]
    nc = tp // GC
    o_gv, o_gg = 2 * GLA_KW, 2 * GLA_KW + GLA_W
    n_xc = len(comm.srcs) if comm else 0

    def body(qk_ref, v_ref, b_scr, gg_ref, o_ref, da_ref, st_ref, am_ref, g_ref, *rest):
        xc_src = rest[:n_xc]
        dp_ref, db_scr, dg_ref = rest[n_xc:n_xc + 3]
        xc_dst = rest[n_xc + 3:2 * n_xc + 3]
        ds_scr = rest[2 * n_xc + 3]
        n = pl.program_id(0)
        if n_xc:
            begin, finish = comm.make(xc_src, xc_dst, rest[-2], rest[-1])
            pl.when(n == 0)(begin)
            pl.when(n == nc - 1)(finish)

        @pl.when(n == 0)
        def _():
            ds_scr[...] = jnp.zeros_like(ds_scr)
            dg_ref[...] = jnp.zeros_like(dg_ref)

        masks, md, second = _gla_masks()
        for h in range(GLA_HEADS):
            cs = slice(h * GLA_K, (h + 1) * GLA_K)
            vs = slice(h * GLA_V, (h + 1) * GLA_V)
            o = o_ref[:, vs]
            rstd = lax.rsqrt(_head_mean(o * o) + EPS)
            xh = o * rstd
            gain_h = g_ref[:, vs]
            g = gg_ref[:, vs]
            sg = _sigmoid(g)
            dah = da_ref[:, vs]
            dp_ref[:, o_gg + h * GLA_V:o_gg + (h + 1) * GLA_V] = (
                dah * (xh * gain_h) * (sg * (1.0 + g * (1.0 - sg)))).astype(BF16)
            dn = dah * (g * sg)
            dg_ref[:, vs] += jnp.sum(dn * xh, axis=0, keepdims=True)
            dxh = dn * gain_h
            do = rstd * (dxh - xh * _head_mean(dxh * xh))
            dob = do.astype(BF16)
            q = qk_ref[:, cs]
            k = qk_ref[:, GLA_KW + h * GLA_K:GLA_KW + (h + 1) * GLA_K]
            v = v_ref[:, vs]
            fq, fk, ed, edi, eb, ee, ebl = _gla_factors(b_scr, h, second)
            qt, kt, qd, kd = _gla_scaled(q, k, fq, fk, ed, edi)
            sp = st_ref[0, h]
            ds = ds_scr[h]
            dsb = ds.astype(BF16)
            q_in = q * eb
            k_end = k * ee
            da_s = _dot(dob, v, NT)
            dv = _dot(am_ref[0, h], dob, TN) + _dot(k_end.astype(BF16), dsb, NT)
            dq_in = _dot(dob, sp, NN)
            dk_end = _dot(v, dsb, NN)
            dbl = jnp.sum(sp.astype(F32) * ds, axis=0, keepdims=True) * ebl
            ds_scr[h] = ds * ebl + _dot(dob, q_in.astype(BF16), TN)
            dq = dq_in * eb
            dk = dk_end * ee
            de_end = dk_end * k_end
            db = dq_in * q_in - de_end
            placed = [(GC - 1, jnp.sum(de_end, axis=0, keepdims=True) + dbl)]
            for l, m in enumerate(GLA_LEVELS):
                dal = jnp.where(masks[l], da_s, 0.0).astype(BF16)
                dqt = _dot(dal, kt[l], NN)
                dkt = _dot(dal, qt[l], TN)
                dq = dq + dqt * fq[l]
                dk = dk + dkt * fk[l]
                gl = dqt * (q * fq[l]) - dkt * (k * fk[l])
                db = db + gl
                placed += [(s + m - 1, -jnp.sum(gl[s:s + 2 * m], axis=0, keepdims=True)) for s in range(0, GC, 2 * m)]
            dad = jnp.where(md, da_s, 0.0).astype(BF16)
            dqd = _dot(dad, kd, NN)
            dkd = _dot(dad, qd, TN)
            dq = dq + dqd * ed
            dk = dk + dkd * edi
            gd = dqd * (q * ed) - dkd * (k * edi)
            db = db + gd
            placed += [(s - 1, -jnp.sum(gd[s:s + GLA_SUB], axis=0, keepdims=True)) for s in range(GLA_SUB, GC, GLA_SUB)]
            db_scr[:, cs] = db
            for r, val in placed:
                db_scr[r:r + 1, cs] += val
            dp_ref[:, cs] = (dq * (GLA_K ** -0.5)).astype(BF16)
            dp_ref[:, GLA_KW + h * GLA_K:GLA_KW + (h + 1) * GLA_K] = dk.astype(BF16)
            dp_ref[:, o_gv + h * GLA_V:o_gv + (h + 1) * GLA_V] = dv.astype(BF16)

    rev = lambda n: (nc - 1 - n, 0)
    const = lambda n: (0, 0)
    xc_shapes, xc_sems = (list(comm.out_shapes), _comm_sems(comm)) if n_xc else ([], [])
    return pl.pallas_call(
        body, name="gla_bwd", grid=(nc,),
        in_specs=[pl.BlockSpec((GC, 2 * GLA_KW), rev),
                  pl.BlockSpec((GC, GLA_W), rev),
                  pl.BlockSpec((GC, GLA_KW), rev),
                  pl.BlockSpec((GC, GLA_W), rev),
                  pl.BlockSpec((GC, GLA_W), rev),
                  pl.BlockSpec((GC, GLA_W), rev),
                  pl.BlockSpec((1, GLA_HEADS, GLA_V, GLA_K), lambda n: (nc - 1 - n, 0, 0, 0)),
                  pl.BlockSpec((1, GLA_HEADS, GC, GC), lambda n: (nc - 1 - n, 0, 0, 0)),
                  pl.BlockSpec((1, GLA_W), const)] + [ANY] * n_xc,
        out_specs=[pl.BlockSpec((GC, W_GP), rev), pl.BlockSpec((GC, GLA_KW), rev),
                   pl.BlockSpec((1, GLA_W), const)] + [ANY] * n_xc,
        out_shape=[jax.ShapeDtypeStruct((tp, W_GP), BF16), jax.ShapeDtypeStruct((tp, GLA_KW), F32),
                   jax.ShapeDtypeStruct((1, GLA_W), F32)] + xc_shapes,
        scratch_shapes=[pltpu.VMEM((GLA_HEADS, GLA_V, GLA_K), F32)] + xc_sems,
        compiler_params=_cparams(1),
    )(gqk, gv, b, gg, o_gla, da, states, scores, gain, *(comm.srcs if comm else ()))


def _mid_call(a_ret, a_gla, mg, h0, tgt, wbr, wbg, wout, gf):
    tp = h0.shape[0]
    nt = tp // TM

    def body(ar_ref, ag_ref, mg_ref, h_ref, t_ref, wbr_ref, wbg_ref, wo_ref, gf_ref,
             dh1_ref, dag_ref, dm_ref, mb_ref, dh1b_ref, dprb_ref, dpgb_ref, loss_ref, dgf_ref):
        i = pl.program_id(0)

        @pl.when(i == 0)
        def _():
            loss_ref[...] = jnp.zeros_like(loss_ref)
            dgf_ref[...] = jnp.zeros_like(dgf_ref)

        ar, ag = ar_ref[...], ag_ref[...]
        pr = _dot(ar, wbr_ref[...], NN)
        pg = _dot(ag, wbg_ref[...], NN)
        sr = _sigmoid(mg_ref[:, :D_MODEL])
        sg = _sigmoid(mg_ref[:, D_MODEL:])
        merged = (sr * pr + sg * pg).astype(BF16)
        mb_ref[...] = merged
        h1 = h_ref[...] + _dot(merged, wo_ref[...], NN)
        r1 = lax.rsqrt(jnp.mean(h1 * h1, axis=-1, keepdims=True) + EPS)
        xh = h1 * r1
        gfv = gf_ref[...]
        live = jnp.where(i > 0, 1.0, 0.0).astype(F32)
        err = (xh * gfv - t_ref[...]) * live
        loss_ref[...] += jnp.full(loss_ref.shape, 0.5 / D_MODEL, F32) * jnp.sum(err * err)
        dy = err * (1.0 / D_MODEL)
        dgf_ref[...] += jnp.sum(dy * xh, axis=0, keepdims=True)
        dxh = dy * gfv
        dh1 = r1 * (dxh - xh * jnp.mean(dxh * xh, axis=-1, keepdims=True))
        dh1_ref[...] = dh1
        dh1b = dh1.astype(BF16)
        dh1b_ref[...] = dh1b
        dmerged = _dot(dh1b, wo_ref[...], NT)
        dm_ref[:, :D_MODEL] = (dmerged * pr * sr * (1.0 - sr)).astype(BF16)
        dm_ref[:, D_MODEL:] = (dmerged * pg * sg * (1.0 - sg)).astype(BF16)
        dpr = (dmerged * sr).astype(BF16)
        dpg = (dmerged * sg).astype(BF16)
        dprb_ref[...] = dpr
        dpgb_ref[...] = dpg
        dag_ref[...] = _dot(dpg, wbg_ref[...], NT)

    tile = lambda w: pl.BlockSpec((TM, w), lambda i: (i, 0))
    const = lambda r, w: pl.BlockSpec((r, w), lambda i: (0, 0))
    return pl.pallas_call(
        body, name="merge_out_loss", grid=(nt,),
        in_specs=[tile(RET_W), tile(GLA_W), tile(W_M), tile(D_MODEL),
                  pl.BlockSpec((TM, D_MODEL), lambda i: (jnp.maximum(i - 1, 0), 0)),
                  const(RET_W, D_MODEL), const(GLA_W, D_MODEL), const(D_MODEL, D_MODEL), const(1, D_MODEL)],
        out_specs=[tile(D_MODEL), tile(GLA_W), tile(W_M), tile(D_MODEL), tile(D_MODEL), tile(D_MODEL),
                   tile(D_MODEL), const(1, 128), const(1, D_MODEL)],
        out_shape=[jax.ShapeDtypeStruct((tp, D_MODEL), F32), jax.ShapeDtypeStruct((tp, GLA_W), F32),
                   jax.ShapeDtypeStruct((tp, W_M), BF16),
                   jax.ShapeDtypeStruct((tp, D_MODEL), BF16), jax.ShapeDtypeStruct((tp, D_MODEL), BF16),
                   jax.ShapeDtypeStruct((tp, D_MODEL), BF16), jax.ShapeDtypeStruct((tp, D_MODEL), BF16),
                   jax.ShapeDtypeStruct((1, 128), F32), jax.ShapeDtypeStruct((1, D_MODEL), F32)],
        compiler_params=_cparams(1),
    )(a_ret, a_gla, mg, h0, tgt, wbr, wbg, wout, gf)


def _device_step(x2d, tgt2d, meta, norm_gain, w_in_part, w_gate_up, b_gate, ret_gain, gla_gain, branch_parts,
                 final_gain, ck):
    seq = x2d.shape[0]
    tp = T0 + seq
    head = jnp.concatenate([jnp.zeros((PADF, D_MODEL), F32), meta], axis=0)
    wg_pad = jnp.pad(w_gate_up, ((0, 128 - GATE_RANK), (0, 0))).astype(BF16)

    half = RET_QK // 2
    cos, sin = (jnp.asarray(t) for t in _rope_tables(tp))
    lgam = jnp.log1p(-(2.0 ** (-5.0 - jnp.arange(RET_HEADS, dtype=F32))))
    pmat = jnp.asarray(_gla_tril(), BF16)
    pmat_t = jnp.asarray(_gla_tril().T.copy(), BF16)

    h0, u, g_in = _rms_call(x2d, head, norm_gain, _gather_plan([w_in_part], relay=(True,)))
    hr, sw = w_in_part.shape
    w_in_bf = g_in.reshape(4, 2, hr, sw).transpose(1, 2, 0, 3).reshape(2 * hr, 4 * sw)
    w_r = w_in_bf
    w_g = jnp.pad(w_in_bf[:, W_R:W_R + W_G], ((0, 0), (0, W_GP - W_G)))
    w_m = w_in_bf[:, W_R + W_G:]
    tab = pl.BlockSpec((_proj_rows(tp), half), lambda j, i: (i, 0))
    rqk = _mm_nn("proj_rqk", u, w_r, BF16, D_MODEL, 0, 2 * D_MODEL, _rope_epilogue, (cos, sin), (tab, tab))
    rv = _mm_nn("proj_rv", u, w_r, BF16, RET_W, 2 * D_MODEL, RET_W)
    rg = _mm_nn("proj_rg", u, w_r, F32, RET_W, 4 * D_MODEL, RET_W)
    gqk = _mm_nn("proj_gqk", u, w_g, F32, 2 * GLA_KW, 0, 2 * GLA_KW, _gqk_epilogue)
    gv = _mm_nn("proj_gv", u, w_g, BF16, GLA_W, 2 * GLA_KW, GLA_W)
    gg = _mm_nn("proj_gg", u, w_g, F32, GLA_W, 2 * GLA_KW + GLA_W, GLA_W)
    glr = _mm_nn("proj_glr", u, w_g, F32, 128, 2 * GLA_KW + 2 * GLA_W, 128)
    mg = _mm_nn("proj_mg", u, w_m, F32, W_M, 0, W_M)

    o_ret, a_ret, st_ret, sc_ret = _ret_fwd_call(rqk, rv, rg, ret_gain, lgam)
    z_gate, b_dec = _gla_gate_call(glr, wg_pad, b_gate, pmat)
    o_gla, a_gla, st_gla, sc_gla, g_br, g_bg, g_out = _gla_fwd_call(gqk, gv, b_dec, gg, gla_gain,
                                                                    comm=_spread_plan(branch_parts))
    wbr = g_br.reshape(RET_W, D_MODEL)
    wbg = g_bg.reshape(GLA_W, D_MODEL)
    wout = g_out.reshape(D_MODEL, D_MODEL)

    gf = final_gain.reshape(1, D_MODEL)
    (dh1, da_gla, dm, merged_b, dh1_b, dpr_b, dpg_b, loss, dgf) = _mid_call(
        a_ret, a_gla, mg, h0, tgt2d, wbr, wbg, wout, gf)

    names_b = ("w_branch_ret", "w_branch_gla", "w_out")
    g2_b = [_mm_tn("dw_br", a_ret, dpr_b, D_MODEL).reshape(4, 2, RET_W // 8, D_MODEL).transpose(1, 0, 2, 3),
            _mm_tn("dw_bg", a_gla, dpg_b, D_MODEL).reshape(4, 2, GLA_W // 8, D_MODEL).transpose(1, 0, 2, 3),
            _mm_tn("dw_out", merged_b, dh1_b, D_MODEL).reshape(4, 2, D_MODEL // 8, D_MODEL).transpose(1, 0, 2, 3)]
    sib_b = _swap_halves_call("swap_halves_branch", g2_b)
    sum_b = [_add_half_call("add_half_" + nm, g, b, ck) for nm, g, b in zip(names_b, g2_b, sib_b)]
    d_g, db_dec, dgla_gain, *chips_b = _gla_bwd_call(gqk, gv, b_dec, gg, o_gla, da_gla, st_gla, sc_gla, gla_gain,
                                                     comm=_exchange_plan(sum_b))
    d_g, dwg, dbg = _gla_gate_bwd_call(db_dec, z_gate, glr, wg_pad, pmat_t, d_g)
    mine = [_add_chips_call("add_chips_" + nm, g, b, p, ck) for nm, g, b, p in zip(names_b, g2_b, sib_b, chips_b)]

    d_r, dret_gain = _ret_bwd_call(rqk, rv, rg, o_ret, dpr_b, wbr, st_ret, sc_ret, ret_gain, lgam, cos, sin)

    dwp = _mm_tn("dw_r", u, d_r, 2 * D_MODEL, out_cols=IN_PAD)
    dwp = _mm_tn("dw_g", u, d_g, D_MODEL, ncols=W_GP - 128, into=dwp, col0=W_R)
    dwp = _mm_tn("dw_glr", u, d_g, 128, ncols=128, bcol0=W_GP - 128, into=dwp, col0=W_R + W_GP - 128)
    g2_in = _place_merge_cols_call(dwp, _mm_tn("dw_m", u, dm, 2 * D_MODEL)).reshape(2, D_MODEL // 2, IN_PAD)

    du, sib_in = _mm_nt_acc("du_g", d_g, w_g, W_GP, comm=_swap_plan([g2_in]), tb=_proj_rows(tp))
    sum_in = _add_rows_call("add_half_w_in", g2_in, sib_in, ck)
    du, chips_in = _mm_nt_acc("du_r", d_r, w_r, 2 * D_MODEL, acc_in=du, comm=_exchange_window_plan(sum_in),
                              tb=_proj_rows(tp))
    tile = pl.BlockSpec((TB, D_MODEL), lambda i, kk: (i, 0))
    row = pl.BlockSpec((1, D_MODEL), lambda i, kk: (0, 0))
    dx, dmeta, dnorm_gain = _mm_nt_acc(
        "du_m", dm, w_m, W_M, acc_in=du, epilogue=_rms_bwd_epilogue, extras=(h0, norm_gain, dh1),
        extra_specs=(tile, row, tile),
        extra_out_shapes=(jax.ShapeDtypeStruct((seq, D_MODEL), F32), jax.ShapeDtypeStruct((N_META, D_MODEL), F32),
                          jax.ShapeDtypeStruct((1, D_MODEL), F32)),
        extra_out_specs=(ANY, pl.BlockSpec((N_META, D_MODEL), lambda i, kk: (0, 0)), row),
        extra_scratch=(pltpu.VMEM((2, TB, D_MODEL), F32), pltpu.SemaphoreType.DMA((2,))))
    mine = [_add_window_call("add_chips_w_in", g2_in, sib_in, chips_in, ck)] + mine
    full = _join_halves_call("join_halves", mine)

    return dict(loss=loss[0, 0], dx=dx, dmeta=dmeta, norm_gain=dnorm_gain, w_gate_up=dwg[:GATE_RANK], b_gate=dbg,
                ret_norm_gain=dret_gain, gla_norm_gain=dgla_gain, final_norm_gain=dgf.reshape(D_MODEL),
                w_in=full[0], w_branch_ret=full[1], w_branch_gla=full[2], w_out=full[3])


MESH = pl.DeviceIdType.MESH
ANY = pl.BlockSpec(memory_space=pl.ANY)


def _place():
    return lax.axis_index("x"), lax.axis_index("y"), lax.axis_index("c")


def _gather8_call(name, parts):
    comm = _gather_plan(parts)
    n = len(parts)

    def body(*refs):
        begin, finish = comm.make(refs[:n], refs[n:2 * n], refs[-2], refs[-1])
        begin()
        finish()

    return pl.pallas_call(
        body, name=name, out_shape=list(comm.out_shapes), in_specs=[ANY] * n, out_specs=[ANY] * n,
        scratch_shapes=_comm_sems(comm),
    )(*parts)


def _swap_halves_call(name, gs):
    n = len(gs)

    def body(*refs):
        g_refs, b_refs = refs[:n], refs[n:2 * n]
        send_sems, recv_sems = refs[2 * n:]
        x, y, c = _place()
        copies = [pltpu.make_async_remote_copy(
            src_ref=g_refs[t].at[1 - c], dst_ref=b_refs[t], send_sem=send_sems.at[t], recv_sem=recv_sems.at[t],
            device_id=(x, y, 1 - c), device_id_type=MESH) for t in range(n)]
        for cp in copies:
            cp.start()
        for cp in copies:
            cp.wait()

    return pl.pallas_call(
        body, name=name,
        out_shape=[jax.ShapeDtypeStruct(g.shape[1:], g.dtype) for g in gs],
        in_specs=[ANY] * n, out_specs=[ANY] * n,
        scratch_shapes=[pltpu.SemaphoreType.DMA((n,)), pltpu.SemaphoreType.DMA((n,))],
    )(*gs)


def _join_halves_call(name, ts):
    n = len(ts)

    def body(*refs):
        o_refs = refs[n:2 * n]
        send_sems, recv_sems = refs[2 * n:]
        x, y, c = _place()
        copies = [pltpu.make_async_remote_copy(
            src_ref=o_refs[t].at[c], dst_ref=o_refs[t].at[c], send_sem=send_sems.at[t], recv_sem=recv_sems.at[t],
            device_id=(x, y, 1 - c), device_id_type=MESH) for t in range(n)]
        for cp in copies:
            cp.start()
        for t in range(n):
            copies[t].wait_send()
            pltpu.make_async_remote_copy(
                src_ref=o_refs[t].at[c], dst_ref=o_refs[t].at[1 - c], send_sem=send_sems.at[t],
                recv_sem=recv_sems.at[t], device_id=(x, y, 1 - c), device_id_type=MESH).wait_recv()

    return pl.pallas_call(
        body, name=name,
        out_shape=[jax.ShapeDtypeStruct(t.shape, t.dtype) for t in ts],
        in_specs=[ANY] * n, out_specs=[ANY] * n, input_output_aliases={t: t for t in range(n)},
        scratch_shapes=[pltpu.SemaphoreType.DMA((n,)), pltpu.SemaphoreType.DMA((n,))],
    )(*ts)


def _row_block(rows, cols, budget):
    best = 8
    for rb in range(8, rows + 1, 8):
        if rows % rb == 0 and rb * cols * 4 <= budget:
            best = rb
    return best


def _add_half_call(name, g, b, ck):
    _, _, r, cc = g.shape
    rb = _row_block(r, cc, 2 * 1024 * 1024)

    def body(ck_ref, g_ref, b_ref, o_ref):
        o_ref[...] = (g_ref[...] + b_ref[...]).astype(BF16)

    return pl.pallas_call(
        body, name=name,
        grid_spec=pltpu.PrefetchScalarGridSpec(
            num_scalar_prefetch=1, grid=(4, r // rb),
            in_specs=[pl.BlockSpec((None, None, rb, cc), lambda k, i, ck_ref: (ck_ref[0], k, i, 0)),
                      pl.BlockSpec((None, rb, cc), lambda k, i, ck_ref: (k, i, 0))],
            out_specs=pl.BlockSpec((None, rb, cc), lambda k, i, ck_ref: (k, i, 0))),
        out_shape=jax.ShapeDtypeStruct(b.shape, BF16),
        compiler_params=_cparams(2),
    )(ck, g, b)


def _add_rows_call(name, g, b, ck):
    _, r, cc = g.shape
    rb = _row_block(r, cc, 2 * 1024 * 1024)

    def body(ck_ref, g_ref, b_ref, o_ref):
        o_ref[...] = (g_ref[...] + b_ref[...]).astype(BF16)

    return pl.pallas_call(
        body, name=name,
        grid_spec=pltpu.PrefetchScalarGridSpec(
            num_scalar_prefetch=1, grid=(r // rb,),
            in_specs=[pl.BlockSpec((None, rb, cc), lambda i, ck_ref: (ck_ref[0], i, 0)),
                      pl.BlockSpec((rb, cc), lambda i, ck_ref: (i, 0))],
            out_specs=pl.BlockSpec((rb, cc), lambda i, ck_ref: (i, 0))),
        out_shape=jax.ShapeDtypeStruct((r, cc), BF16),
        compiler_params=_cparams(1),
    )(ck, g, b)


def _add_window_call(name, g, b, p, ck):
    _, r, _ = g.shape
    nb, step = WIN_W // 128, WIN_STEP // 128

    def body(ck_ref, g_ref, b_ref, p0_ref, p1_ref, p2_ref, o_ref):
        own = g_ref[...] + b_ref[...]
        o_ref[...] = ((own + p0_ref[...].astype(F32)) + p1_ref[...].astype(F32)) + p2_ref[...].astype(F32)

    def peer(j):
        return pl.BlockSpec((None, r, 128), lambda i, ck_ref: (j, 0, i))

    return pl.pallas_call(
        body, name=name,
        grid_spec=pltpu.PrefetchScalarGridSpec(
            num_scalar_prefetch=1, grid=(nb,),
            in_specs=[pl.BlockSpec((None, r, 128), lambda i, ck_ref: (ck_ref[0], 0, step * ck_ref[1] + i)),
                      pl.BlockSpec((r, 128), lambda i, ck_ref: (0, step * ck_ref[1] + i)),
                      peer(0), peer(1), peer(2)],
            out_specs=pl.BlockSpec((None, r, 128), lambda i, ck_ref: (ck_ref[0], 0, i))),
        out_shape=jax.ShapeDtypeStruct((2, r, WIN_W), F32),
        compiler_params=_cparams(1),
    )(ck, g, b, p, p, p)


def _add_chips_call(name, g, b, p, ck):
    _, _, r, cc = g.shape
    rb = _row_block(r, cc, 2 * 1024 * 1024)

    def body(ck_ref, g_ref, b_ref, p0_ref, p1_ref, p2_ref, o_ref):
        own = g_ref[...] + b_ref[...]
        o_ref[...] = ((own + p0_ref[...].astype(F32)) + p1_ref[...].astype(F32)) + p2_ref[...].astype(F32)

    def peer(j):
        return pl.BlockSpec((None, rb, cc), lambda i, ck_ref: (j, i, 0))

    return pl.pallas_call(
        body, name=name,
        grid_spec=pltpu.PrefetchScalarGridSpec(
            num_scalar_prefetch=1, grid=(r // rb,),
            in_specs=[pl.BlockSpec((None, None, rb, cc), lambda i, ck_ref: (ck_ref[0], ck_ref[1], i, 0)),
                      pl.BlockSpec((None, rb, cc), lambda i, ck_ref: (ck_ref[1], i, 0)),
                      peer(0), peer(1), peer(2)],
            out_specs=pl.BlockSpec((None, rb, cc), lambda i, ck_ref: (ck_ref[0], i, 0))),
        out_shape=jax.ShapeDtypeStruct((2, r, cc), F32),
        compiler_params=_cparams(1),
    )(ck, g, b, p, p, p)


def _sum8_call(name, g):
    def body(g_ref, o_ref):
        acc = g_ref[0]
        for d in range(1, 8):
            acc = acc + g_ref[d]
        o_ref[...] = acc

    return pl.pallas_call(body, name=name, out_shape=jax.ShapeDtypeStruct(g.shape[1:], F32))(g)


def _adamw_call(name, w, g, m, v):
    r, cc = w.shape
    if r % 8 == 0 or r * cc * 4 <= 1024 * 1024:
        rb = _row_block(r, cc, 1024 * 1024) if r % 8 == 0 else r
        grid, spec = (r // rb,), pl.BlockSpec((rb, cc), lambda i: (i, 0))
    else:
        grid, spec = (cc // 128,), pl.BlockSpec((r, 128), lambda i: (0, i))

    def body(w_ref, g_ref, m_ref, v_ref, d_ref, m2_ref, v2_ref):
        gv = g_ref[...]
        m2 = ADAM_B1 * m_ref[...] + (1.0 - ADAM_B1) * gv
        v2 = ADAM_B2 * v_ref[...] + (1.0 - ADAM_B2) * (gv * gv)
        m_hat = m2 / (1.0 - ADAM_B1 ** ADAM_STEP)
        v_hat = v2 / (1.0 - ADAM_B2 ** ADAM_STEP)
        d_ref[...] = -ADAM_LR * (m_hat / (jnp.sqrt(v_hat) + ADAM_EPS) + ADAM_WD * w_ref[...])
        m2_ref[...] = m2
        v2_ref[...] = v2

    return pl.pallas_call(
        body, name=name, grid=grid, in_specs=[spec] * 4, out_specs=[spec] * 3,
        out_shape=[jax.ShapeDtypeStruct((r, cc), F32)] * 3, compiler_params=_cparams(1),
    )(w, g, m, v)


SMALL = (("norm_gain", D_MODEL), ("b_gate", GLA_KW), ("ret_norm_gain", RET_W), ("gla_norm_gain", GLA_W),
         ("final_norm_gain", D_MODEL), ("w_gate_up", GATE_RANK * GLA_KW), ("meta_tokens", N_META * D_MODEL),
         ("loss", 1))


def _pack_rows(vecs, rows):
    flat = jnp.concatenate([v.reshape(-1) for v in vecs])
    return jnp.pad(flat, (0, rows * 128 - flat.shape[0])).reshape(rows, 128)


def kernel(x, meta_tokens, norm_gain, w_in, w_gate_up, b_gate, ret_norm_gain, gla_norm_gain, w_branch_ret, w_branch_gla, w_out, final_norm_gain, loss_target, m_meta_tokens, m_norm_gain, m_w_in, m_w_gate_up, m_b_gate, m_ret_norm_gain, m_gla_norm_gain, m_w_branch_ret, m_w_branch_gla, m_w_out, m_final_norm_gain, v_meta_tokens, v_norm_gain, v_w_in, v_w_gate_up, v_b_gate, v_ret_norm_gain, v_gla_norm_gain, v_w_branch_ret, v_w_branch_gla, v_w_out, v_final_norm_gain):
    xi, yi, ci = _place()
    kme = 2 * xi + yi
    ck = jnp.stack([ci, kme]).astype(jnp.int32)
    sw_in = w_in.shape[2]

    def my_half(a, dtype):
        r, cc = a.shape
        return lax.dynamic_index_in_dim(a.reshape(2, r // 2, cc), ci, 0, keepdims=False).astype(dtype)

    g_meta, g_wg = _gather8_call("gather_small_weights", [my_half(meta_tokens, F32), my_half(w_gate_up[0], F32)])
    branch_parts = [my_half(w_branch_ret[0], BF16), my_half(w_branch_gla[0], BF16), my_half(w_out[0], BF16)]
    meta = g_meta.reshape(4, 2, N_META // 2, D_MODEL // 4).transpose(1, 2, 0, 3).reshape(N_META, D_MODEL)
    wg_full = g_wg.reshape(4, 2, GATE_RANK // 2, GLA_KW // 4).transpose(1, 2, 0, 3).reshape(GATE_RANK, GLA_KW)

    loc = _device_step(x[0], loss_target[0], meta, norm_gain, my_half(w_in[0], BF16), wg_full, b_gate, ret_norm_gain,
                       gla_norm_gain,
                       branch_parts, final_norm_gain, ck)
    names = ("w_in", "w_branch_ret", "w_branch_gla", "w_out")
    full = [loc[nm] for nm in names]
    big_w = dict(w_in=w_in[0], w_branch_ret=w_branch_ret[0], w_branch_gla=w_branch_gla[0], w_out=w_out[0])
    big_m = dict(w_in=m_w_in[0], w_branch_ret=m_w_branch_ret[0], w_branch_gla=m_w_branch_gla[0], w_out=m_w_out[0])
    big_v = dict(w_in=v_w_in[0], w_branch_ret=v_w_branch_ret[0], w_branch_gla=v_w_branch_gla[0], w_out=v_w_out[0])
    grads, deltas, new_m, new_v = {}, {}, {}, {}
    for nm, f in zip(names, full):
        shape = big_w[nm].shape
        if nm == "w_in":
            f = lax.dynamic_slice_in_dim(f, (sw_in - WIN_STEP) * kme, sw_in, axis=2)
        g = f.reshape(shape)
        if nm == "w_in":
            d, m2, v2 = (a.T for a in _adamw_call("adamw_" + nm, big_w[nm].T, g.T, big_m[nm].T, big_v[nm].T))
        else:
            d, m2, v2 = _adamw_call("adamw_" + nm, big_w[nm], g, big_m[nm], big_v[nm])
        grads[nm], deltas[nm], new_m[nm], new_v[nm] = (a.reshape((1,) + shape) for a in (g, d, m2, v2))

    small_g = dict(loc)
    small_g["meta_tokens"] = loc["dmeta"]
    n_small = sum(sz for _, sz in SMALL)
    rows = -(-n_small // 128 // 8) * 8
    (g_small,) = _gather8_call("gather_small_grads", [_pack_rows([small_g[nm] for nm, _ in SMALL], rows)])
    tot = _sum8_call("sum_small_grads", g_small).reshape(-1)
    off = 0
    sg = {}
    for nm, sz in SMALL:
        sg[nm] = tot[off:off + sz]
        off += sz
    loss = sg.pop("loss")[0]
    sg["w_gate_up"] = lax.dynamic_slice_in_dim(sg["w_gate_up"].reshape(GATE_RANK, GLA_KW), kme * (GLA_KW // 4),
                                               GLA_KW // 4, axis=1)
    sg["meta_tokens"] = lax.dynamic_slice_in_dim(sg["meta_tokens"].reshape(N_META, D_MODEL), kme * (D_MODEL // 4),
                                                 D_MODEL // 4, axis=1)
    small_w = dict(norm_gain=norm_gain, b_gate=b_gate, ret_norm_gain=ret_norm_gain, gla_norm_gain=gla_norm_gain,
                   final_norm_gain=final_norm_gain, w_gate_up=w_gate_up, meta_tokens=meta_tokens)
    small_m = dict(norm_gain=m_norm_gain, b_gate=m_b_gate, ret_norm_gain=m_ret_norm_gain,
                   gla_norm_gain=m_gla_norm_gain, final_norm_gain=m_final_norm_gain, w_gate_up=m_w_gate_up,
                   meta_tokens=m_meta_tokens)
    small_v = dict(norm_gain=v_norm_gain, b_gate=v_b_gate, ret_norm_gain=v_ret_norm_gain,
                   gla_norm_gain=v_gla_norm_gain, final_norm_gain=v_final_norm_gain, w_gate_up=v_w_gate_up,
                   meta_tokens=v_meta_tokens)
    for nm in small_w:
        shape = small_w[nm].shape
        as2d = lambda a: a.reshape((-1, shape[-1]))
        grads[nm] = sg[nm].reshape(shape)
        deltas[nm], new_m[nm], new_v[nm] = (a.reshape(shape) for a in _adamw_call(
            "adamw_" + nm, as2d(small_w[nm]), as2d(sg[nm]), as2d(small_m[nm]), as2d(small_v[nm])))

    out_order = ("meta_tokens", "norm_gain", "w_in", "w_gate_up", "b_gate", "ret_norm_gain", "gla_norm_gain",
                 "w_branch_ret", "w_branch_gla", "w_out", "final_norm_gain")
    dx = loc["dx"].reshape(x.shape)
    return (loss, dx, *[grads[nm] for nm in out_order], *[deltas[nm] for nm in out_order],
            *[new_m[nm] for nm in out_order], *[new_v[nm] for nm in out_order])
```

```python
import math
from typing import Callable, NamedTuple

import numpy as np
import jax
import jax.numpy as jnp
from jax import lax
from jax.experimental import pallas as pl
from jax.experimental.pallas import tpu as pltpu

F32 = jnp.float32
BF16 = jnp.bfloat16

D_MODEL = 1024
N_META = 16
EPS = 1e-6
ROPE_BASE = 10000.0
RET_HEADS, RET_QK, RET_V = 4, 256, 512
RET_W = RET_HEADS * RET_V
GLA_HEADS, GLA_K, GLA_V = 4, 128, 256
GLA_W = GLA_HEADS * GLA_V
GLA_KW = GLA_HEADS * GLA_K
GATE_RANK = 16
GATE_TAU = 16.0
GLA_SUB = 16

TM = 256
T0 = TM
PADF = T0 - N_META
GC = 128
GS = 2
TB = 768
TK = 768

W_R = 6144
W_G = 3088
W_GP = 3200
W_M = 2048
IN_COLS = W_R + W_G + W_M
WIN_STEP = (IN_COLS // 4) // 128 * 128
WIN_W = -(-(3 * (IN_COLS // 4 - WIN_STEP) + IN_COLS // 4) // 128) * 128
IN_PAD = 3 * WIN_STEP + WIN_W

ADAM_LR, ADAM_B1, ADAM_B2, ADAM_EPS, ADAM_WD, ADAM_STEP = 0.001, 0.9, 0.999, 1e-08, 0.01, 10

VMEM_LIMIT = 56 * 1024 * 1024

NN = ((1,), (0,))
NT = ((1,), (1,))
TN = ((0,), (0,))


def _dot(a, b, dims):
    return lax.dot_general(a, b, (dims, ((), ())), preferred_element_type=F32)


def _cparams(n_axes):
    return pltpu.CompilerParams(dimension_semantics=("arbitrary",) * n_axes, vmem_limit_bytes=VMEM_LIMIT)


def _sigmoid(x):
    return 0.5 * jnp.tanh(0.5 * x) + 0.5


def _silu(x):
    h = 0.5 * x
    return h + h * jnp.tanh(h)


def _head_mean(x):
    return jnp.mean(x, axis=-1, keepdims=True)


def _split3(x):
    hi = x.astype(BF16)
    r1 = x - hi.astype(F32)
    mid = r1.astype(BF16)
    lo = (r1 - mid.astype(F32)).astype(BF16)
    return hi, mid, lo


def _exact_pm(p, x):
    hi, mid, lo = _split3(x)
    return _dot(p, hi, NN) + _dot(p, mid, NN) + _dot(p, lo, NN)


def _rms_call(x2d, head, gain, comm):
    tp = T0 + x2d.shape[0]
    nt = tp // TM
    n_xc = len(comm.srcs)

    def body(x_ref, hd_ref, g_ref, *rest):
        xc_src = rest[:n_xc]
        h_ref, u_ref = rest[n_xc:n_xc + 2]
        xc_dst = rest[n_xc + 2:2 * n_xc + 2]
        i = pl.program_id(0)
        begin, finish = comm.make(xc_src, xc_dst, rest[-2], rest[-1])
        pl.when(i == 0)(begin)
        h = jnp.where(i == 0, hd_ref[...], x_ref[...])
        h_ref[...] = h
        r = lax.rsqrt(jnp.mean(h * h, axis=-1, keepdims=True) + EPS)
        u_ref[...] = (h * r * g_ref[...]).astype(BF16)
        pl.when(i == nt - 1)(finish)

    tile = pl.BlockSpec((TM, D_MODEL), lambda i: (i, 0))
    return pl.pallas_call(
        body, name="rms_in", grid=(nt,),
        in_specs=[pl.BlockSpec((TM, D_MODEL), lambda i: (jnp.maximum(i - 1, 0), 0)),
                  pl.BlockSpec((T0, D_MODEL), lambda i: (0, 0)), pl.BlockSpec((1, D_MODEL), lambda i: (0, 0))]
        + [ANY] * n_xc,
        out_specs=[tile, tile] + [ANY] * n_xc,
        out_shape=[jax.ShapeDtypeStruct((tp, D_MODEL), F32), jax.ShapeDtypeStruct((tp, D_MODEL), BF16)]
        + list(comm.out_shapes),
        scratch_shapes=_comm_sems(comm), compiler_params=_cparams(1),
    )(x2d, head, gain, *comm.srcs)


PROJ_ROWS_MAX = 1408


def _proj_rows(m):
    return max(r for r in range(16, PROJ_ROWS_MAX + 1, 16) if m % r == 0)


def _mm_nn(name, a, b, out_dtype, tn, col0, ncols, epilogue=None, extras=(), extra_specs=()):
    m, k = a.shape
    nj, j0 = ncols // tn, col0 // tn
    tb = _proj_rows(m)

    def body(a_ref, b_ref, *rest):
        *ex, o_ref = rest
        acc = _dot(a_ref[...], b_ref[...], NN)
        if epilogue is None:
            o_ref[...] = acc.astype(out_dtype)
        else:
            epilogue(acc, o_ref, *ex)

    return pl.pallas_call(
        body, name=name, grid=(nj, m // tb),
        in_specs=[pl.BlockSpec((tb, k), lambda j, i: (i, 0)), pl.BlockSpec((k, tn), lambda j, i: (0, j0 + j))]
        + list(extra_specs),
        out_specs=pl.BlockSpec((tb, tn), lambda j, i: (i, j)),
        out_shape=jax.ShapeDtypeStruct((m, ncols), out_dtype),
        compiler_params=_cparams(2),
    )(a, b, *extras)


def _rope_tables(tp):
    half = RET_QK // 2
    pos = np.arange(tp, dtype=np.float32) - np.float32(PADF)
    inv = (ROPE_BASE ** (-np.arange(half, dtype=np.float64) / half)).astype(np.float32)
    ang = (pos[:, None] * inv[None, :]).astype(np.float64)
    return np.cos(ang).astype(np.float32), np.sin(ang).astype(np.float32)


def _rope_epilogue(acc, o_ref, cos_ref, sin_ref):
    scale = jnp.where(pl.program_id(0) == 1, RET_QK ** -0.5, 1.0).astype(F32)
    cos, sin = cos_ref[...], sin_ref[...]
    half = RET_QK // 2
    for h in range(RET_HEADS):
        t1 = acc[:, h * RET_QK:h * RET_QK + half]
        t2 = acc[:, h * RET_QK + half:(h + 1) * RET_QK]
        o_ref[:, h * RET_QK:h * RET_QK + half] = ((t1 * cos - t2 * sin) * scale).astype(BF16)
        o_ref[:, h * RET_QK + half:(h + 1) * RET_QK] = ((t2 * cos + t1 * sin) * scale).astype(BF16)


def _gqk_epilogue(acc, o_ref):
    o_ref[:, :GLA_KW] = acc[:, :GLA_KW] * (GLA_K ** -0.5)
    o_ref[:, GLA_KW:] = acc[:, GLA_KW:]


class _Comm(NamedTuple):
    srcs: tuple
    out_shapes: tuple
    n_sems: int
    make: Callable


def _comm_sems(comm):
    return [pltpu.SemaphoreType.DMA((comm.n_sems,)), pltpu.SemaphoreType.DMA((comm.n_sems,))]


def _start_wait(copies):
    def begin():
        for cp in copies:
            cp.start()

    def finish():
        for cp in copies:
            cp.wait()

    return begin, finish


def _other_chips(x, y):
    return [(1 - x, y), (x, 1 - y), (1 - x, 1 - y)]


def _gather_plan(parts, relay=()):
    n = len(parts)
    relay = tuple(relay) + (False,) * (n - len(relay))

    def make(x_refs, out_refs, send_sems, recv_sems):
        x, y, c = _place()
        me, sibling = (x, y, c), (x, y, 1 - c)
        xn, yn, dg = (1 - x, y), (x, 1 - y), (1 - x, 1 - y)

        def slot(t, px, py, pc, half=None):
            ref = out_refs[t].at[4 * px + 2 * py + pc]
            if half is None:
                return ref
            rows = ref.shape[0] // 2
            return ref.at[pl.ds(half * rows, rows)]

        def copy(t, k, dst, to, src=None):
            return pltpu.make_async_remote_copy(
                src_ref=dst if src is None else src, dst_ref=dst, send_sem=send_sems.at[8 * t + k],
                recv_sem=recv_sems.at[8 * t + k], device_id=to, device_id_type=MESH)

        mine = [pltpu.make_async_copy(x_refs[t], slot(t, *me), send_sems.at[8 * n + t]) for t in range(n)]
        sent = []
        for t in range(n):
            sent.append(copy(t, 0, slot(t, *me), sibling, src=x_refs[t]))
            sent.append(copy(t, 1, slot(t, *me), (*xn, c), src=x_refs[t]))
            sent.append(copy(t, 2, slot(t, *me), (*yn, c), src=x_refs[t]))
            if not relay[t]:
                sent.append(copy(t, 3, slot(t, *me), (*dg, c), src=x_refs[t]))

        def begin():
            for cp in mine + sent:
                cp.start()

        def finish():
            later = []

            def start(cp):
                cp.start()
                later.append(cp)

            for t in range(n):
                copy(t, 2, slot(t, *yn, c), me).wait_recv()
                if relay[t]:
                    start(copy(t, 3, slot(t, *yn, c, half=0), (*xn, c)))
                start(copy(t, 6, slot(t, *yn, c), sibling))
            for t in range(n):
                copy(t, 1, slot(t, *xn, c), me).wait_recv()
                if relay[t]:
                    start(copy(t, 4, slot(t, *xn, c, half=1), (*yn, c)))
                start(copy(t, 5, slot(t, *xn, c), sibling))
            for t in range(n):
                if relay[t]:
                    copy(t, 3, slot(t, *dg, c, half=0), me).wait_recv()
                    copy(t, 4, slot(t, *dg, c, half=1), me).wait_recv()
                else:
                    copy(t, 3, slot(t, *dg, c), me).wait_recv()
                start(copy(t, 7, slot(t, *dg, c), sibling))
            for t in range(n):
                copy(t, 0, slot(t, *sibling), me).wait_recv()
                copy(t, 5, slot(t, *xn, 1 - c), me).wait_recv()
                copy(t, 6, slot(t, *yn, 1 - c), me).wait_recv()
                copy(t, 7, slot(t, *dg, 1 - c), me).wait_recv()
            for cp in sent + later:
                cp.wait_send()
            for cp in mine:
                cp.wait()

        return begin, finish

    return _Comm(tuple(parts), tuple(jax.ShapeDtypeStruct((8,) + p.shape, p.dtype) for p in parts), 9 * n, make)


def _exchange_plan(ss):
    def make(s_refs, b_refs, send_sems, recv_sems):
        x, y, c = _place()
        return _start_wait([pltpu.make_async_remote_copy(
            src_ref=s_refs[t].at[2 * chip[0] + chip[1]], dst_ref=b_refs[t].at[j], send_sem=send_sems.at[3 * t + j],
            recv_sem=recv_sems.at[3 * t + j], device_id=(*chip, c), device_id_type=MESH)
            for t in range(len(s_refs)) for j, chip in enumerate(_other_chips(x, y))])

    return _Comm(tuple(ss), tuple(jax.ShapeDtypeStruct((3,) + s.shape[1:], s.dtype) for s in ss), 3 * len(ss), make)


def _exchange_window_plan(s):
    def make(s_refs, b_refs, send_sems, recv_sems):
        x, y, c = _place()
        return _start_wait([pltpu.make_async_remote_copy(
            src_ref=s_refs[0].at[:, pl.ds(pl.multiple_of((2 * chip[0] + chip[1]) * WIN_STEP, 128), WIN_W)],
            dst_ref=b_refs[0].at[j], send_sem=send_sems.at[j], recv_sem=recv_sems.at[j], device_id=(*chip, c),
            device_id_type=MESH) for j, chip in enumerate(_other_chips(x, y))])

    return _Comm((s,), (jax.ShapeDtypeStruct((3, s.shape[0], WIN_W), s.dtype),), 3, make)


def _swap_plan(gs):
    def make(g_refs, b_refs, send_sems, recv_sems):
        x, y, c = _place()
        return _start_wait([pltpu.make_async_remote_copy(
            src_ref=g_refs[t].at[1 - c], dst_ref=b_refs[t], send_sem=send_sems.at[t], recv_sem=recv_sems.at[t],
            device_id=(x, y, 1 - c), device_id_type=MESH) for t in range(len(g_refs))])

    return _Comm(tuple(gs), tuple(jax.ShapeDtypeStruct(g.shape[1:], g.dtype) for g in gs), len(gs), make)


def _spread_plan(parts):
    def make(p_refs, o_refs, send_sems, recv_sems):
        x, y, c = _place()
        copies = []
        for t in range(len(p_refs)):
            mine = o_refs[t].at[4 * x + 2 * y + c]
            copies.append(pltpu.make_async_copy(p_refs[t], mine, send_sems.at[7 * len(p_refs) + t]))
            for r in range(1, 8):
                peer = (1 - x if r & 4 else x, 1 - y if r & 2 else y, 1 - c if r & 1 else c)
                copies.append(pltpu.make_async_remote_copy(
                    src_ref=p_refs[t], dst_ref=mine, send_sem=send_sems.at[7 * t + r - 1],
                    recv_sem=recv_sems.at[7 * t + r - 1], device_id=peer, device_id_type=MESH))
        return _start_wait(copies)

    return _Comm(tuple(parts), tuple(jax.ShapeDtypeStruct((8,) + p.shape, p.dtype) for p in parts), 8 * len(parts),
                 make)


def _mm_nt_acc(name, a, w, tk, acc_in=None, epilogue=None, extras=(), extra_specs=(), extra_out_shapes=(),
               extra_out_specs=(), extra_scratch=(), comm=None, tb=TB):
    m, k = a.shape
    n = w.shape[0]
    nk, ni = k // tk, m // tb
    has_acc = acc_in is not None
    n_xc = len(comm.srcs) if comm else 0
    n_es = len(extra_scratch)

    def body(*refs):
        a_ref, w_ref = refs[0], refs[1]
        pos = 2
        acc_ref = None
        if has_acc:
            acc_ref = refs[pos]
            pos += 1
        ex = refs[pos:pos + len(extras)]
        pos += len(extras)
        xc_src = refs[pos:pos + n_xc]
        pos += n_xc
        n_scr = 1 + n_es + (2 if n_xc else 0)
        outs = refs[pos:len(refs) - n_scr - n_xc]
        xc_dst = refs[len(refs) - n_scr - n_xc:len(refs) - n_scr]
        scr = refs[len(refs) - n_scr]
        es = refs[len(refs) - n_scr + 1:len(refs) - n_scr + 1 + n_es]
        i, kk = pl.program_id(0), pl.program_id(1)
        if n_xc:
            begin, finish = comm.make(xc_src, xc_dst, refs[-2], refs[-1])
            pl.when((i == 0) & (kk == 0))(begin)

        @pl.when(kk == 0)
        def _():
            scr[...] = acc_ref[...] if has_acc else jnp.zeros_like(scr)

        scr[...] += _dot(a_ref[...], w_ref[...], NT)

        @pl.when(kk == nk - 1)
        def _():
            if epilogue is None:
                outs[0][...] = scr[...]
            else:
                epilogue(scr[...], outs, i, ni, *ex, *es)

        if n_xc:
            pl.when((i == ni - 1) & (kk == nk - 1))(finish)

    in_specs = [pl.BlockSpec((tb, tk), lambda i, kk: (i, kk)), pl.BlockSpec((n, tk), lambda i, kk: (0, kk))]
    args = [a, w]
    if has_acc:
        in_specs.append(pl.BlockSpec((tb, n), lambda i, kk: (i, 0)))
        args.append(acc_in)
    in_specs += list(extra_specs) + [ANY] * n_xc
    args += list(extras) + (list(comm.srcs) if comm else [])
    if epilogue is None:
        out_shape = [jax.ShapeDtypeStruct((m, n), F32)]
        out_specs = [pl.BlockSpec((tb, n), lambda i, kk: (i, 0))]
    else:
        out_shape, out_specs = list(extra_out_shapes), list(extra_out_specs)
    scratch = [pltpu.VMEM((tb, n), F32)] + list(extra_scratch)
    if n_xc:
        out_shape += list(comm.out_shapes)
        out_specs += [ANY] * n_xc
        scratch += _comm_sems(comm)
    return pl.pallas_call(
        body, name=name, grid=(ni, nk), in_specs=in_specs, out_specs=out_specs, out_shape=out_shape,
        scratch_shapes=scratch, compiler_params=_cparams(2),
    )(*args)


def _rms_bwd_epilogue(du, outs, i, ni, h_ref, g_ref, dh1_ref, obuf, sems):
    dx_ref, dmeta_ref, dg_ref = outs
    h = h_ref[...]
    r = lax.rsqrt(jnp.mean(h * h, axis=-1, keepdims=True) + EPS)
    xh = h * r
    dxh = du * g_ref[...]
    dh0 = dh1_ref[...] + r * (dxh - xh * jnp.mean(dxh * xh, axis=-1, keepdims=True))

    def put(slot, tile):
        return pltpu.make_async_copy(obuf.at[slot], dx_ref.at[pl.ds(pl.multiple_of(tile * TB - T0, 8), TB)],
                                     sems.at[slot])

    @pl.when(i == 0)
    def _():
        dg_ref[...] = jnp.zeros_like(dg_ref)
        dmeta_ref[...] = dh0[PADF:T0, :]
        obuf[0] = dh0
        first = pltpu.make_async_copy(obuf.at[0, pl.ds(T0, TB - T0)], dx_ref.at[pl.ds(0, TB - T0)], sems.at[0])
        first.start()
        first.wait()

    @pl.when(i >= 1)
    def _():
        slot = i % 2

        @pl.when(i >= 3)
        def _():
            put(slot, i - 2).wait()

        obuf[slot] = dh0
        put(slot, i).start()

    dg_ref[...] += jnp.sum(du * xh, axis=0, keepdims=True)

    @pl.when(i == ni - 1)
    def _():
        for tile in (ni - 2, ni - 1):
            if tile >= 1:
                put(tile % 2, tile).wait()


def _mm_tn(name, a, b, bn, ncols=None, bcol0=0, into=None, col0=0, out_cols=None):
    t, m = a.shape
    n = ncols or b.shape[1]
    j0, bj0 = col0 // bn, bcol0 // bn

    def body(a_ref, b_ref, *rest):
        o_ref = rest[-1]

        @pl.when(pl.program_id(1) == 0)
        def _():
            o_ref[...] = jnp.zeros_like(o_ref)

        o_ref[...] += _dot(a_ref[...], b_ref[...], TN)

    in_specs = [pl.BlockSpec((TK, m), lambda j, kk: (kk, 0)), pl.BlockSpec((TK, bn), lambda j, kk: (kk, bj0 + j))]
    args = [a, b]
    aliases = {}
    if into is not None:
        in_specs.append(ANY)
        args.append(into)
        aliases = {2: 0}
        out_cols = into.shape[1]
    return pl.pallas_call(
        body, name=name, grid=(n // bn, t // TK), in_specs=in_specs,
        out_specs=pl.BlockSpec((m, bn), lambda j, kk: (0, j0 + j)),
        out_shape=jax.ShapeDtypeStruct((m, out_cols or n), F32), input_output_aliases=aliases,
        compiler_params=_cparams(2),
    )(*args)


def _place_merge_cols_call(dwp, dw_m):
    c0 = W_R + W_GP - 128
    tail = IN_PAD - c0
    rows = 256

    def body(m_ref, p_ref, o_ref, buf, low, sem):
        get = pltpu.make_async_copy(o_ref.at[:, pl.ds(c0, 128)], low, sem)
        get.start()
        get.wait()
        for r in range(0, D_MODEL, rows):
            buf[r:r + rows, :] = jnp.concatenate(
                [low[r:r + rows, :GATE_RANK], m_ref[r:r + rows, :],
                 jnp.zeros((rows, tail - GATE_RANK - W_M), F32)], axis=1)
        put = pltpu.make_async_copy(buf, o_ref.at[:, pl.ds(c0, tail)], sem)
        put.start()
        put.wait()

    return pl.pallas_call(
        body, name="place_merge_cols",
        in_specs=[pl.BlockSpec(memory_space=pltpu.VMEM), ANY], out_specs=ANY,
        out_shape=jax.ShapeDtypeStruct(dwp.shape, F32), input_output_aliases={1: 0},
        scratch_shapes=[pltpu.VMEM((D_MODEL, tail), F32), pltpu.VMEM((D_MODEL, 128), F32), pltpu.SemaphoreType.DMA],
        compiler_params=pltpu.CompilerParams(vmem_limit_bytes=VMEM_LIMIT),
    )(dw_m, dwp)


def _ret_fill_decay(lg_ref, dm_scr):
    c = TM
    ii = lax.broadcasted_iota(jnp.int32, (c, c), 0)
    jj = lax.broadcasted_iota(jnp.int32, (c, c), 1)
    rel = (ii - jj).astype(F32)
    for h in range(RET_HEADS):
        dm_scr[h] = jnp.where(rel >= 0, jnp.exp(jnp.maximum(rel, 0.0) * lg_ref[h]), 0.0)


def _ret_consts(lg, dm_ref):
    c = TM
    idx = lax.broadcasted_iota(jnp.int32, (c, 1), 0).astype(F32)
    xi = jnp.exp((idx + 1.0) * lg)
    zeta = jnp.exp((c - 1.0 - idx) * lg)
    gc = jnp.exp(jnp.full((1, 1), c, F32) * lg)
    return dm_ref[...], xi, zeta, gc


def _ret_fwd_call(rqk, rv, rg, gain, lgam):
    tp = rqk.shape[0]
    nc = tp // TM

    def body(lg_ref, qk_ref, v_ref, rg_ref, g_ref, o_ref, a_ref, st_ref, sc_ref, s_scr, dm_scr):
        @pl.when(pl.program_id(0) == 0)
        def _():
            s_scr[...] = jnp.zeros_like(s_scr)
            _ret_fill_decay(lg_ref, dm_scr)

        for h in range(RET_HEADS):
            dm, xi, zeta, gc = _ret_consts(lg_ref[h], dm_scr.at[h])
            q = qk_ref[:, h * RET_QK:(h + 1) * RET_QK]
            k = qk_ref[:, D_MODEL + h * RET_QK:D_MODEL + (h + 1) * RET_QK]
            v = v_ref[:, h * RET_V:(h + 1) * RET_V]
            sb = s_scr[h].astype(BF16)
            st_ref[0, h] = sb
            s = (_dot(q, k, NT) * dm).astype(BF16)
            sc_ref[0, h] = s
            o = _dot(s, v, NN) + xi * _dot(q, sb, NN)
            kz = (k.astype(F32) * zeta).astype(BF16)
            s_scr[h] = gc * s_scr[h] + _dot(kz, v, TN)
            o_ref[:, h * RET_V:(h + 1) * RET_V] = o
            mu = _head_mean(o)
            xc = o - mu
            xh = xc * lax.rsqrt(_head_mean(xc * xc) + EPS)
            a_ref[:, h * RET_V:(h + 1) * RET_V] = (
                xh * g_ref[:, h * RET_V:(h + 1) * RET_V] * _silu(rg_ref[:, h * RET_V:(h + 1) * RET_V])).astype(BF16)

    return pl.pallas_call(
        body, name="ret_fwd", grid=(nc,),
        in_specs=[pl.BlockSpec(memory_space=pltpu.SMEM),
                  pl.BlockSpec((TM, 2 * D_MODEL), lambda n: (n, 0)),
                  pl.BlockSpec((TM, RET_W), lambda n: (n, 0)),
                  pl.BlockSpec((TM, RET_W), lambda n: (n, 0)),
                  pl.BlockSpec((1, RET_W), lambda n: (0, 0))],
        out_specs=[pl.BlockSpec((TM, RET_W), lambda n: (n, 0)),
                   pl.BlockSpec((TM, RET_W), lambda n: (n, 0)),
                   pl.BlockSpec((1, RET_HEADS, RET_QK, RET_V), lambda n: (n, 0, 0, 0)),
                   pl.BlockSpec((1, RET_HEADS, TM, TM), lambda n: (n, 0, 0, 0))],
        out_shape=[jax.ShapeDtypeStruct((tp, RET_W), F32), jax.ShapeDtypeStruct((tp, RET_W), BF16),
                   jax.ShapeDtypeStruct((nc, RET_HEADS, RET_QK, RET_V), BF16),
                   jax.ShapeDtypeStruct((nc, RET_HEADS, TM, TM), BF16)],
        scratch_shapes=[pltpu.VMEM((RET_HEADS, RET_QK, RET_V), F32), pltpu.VMEM((RET_HEADS, TM, TM), F32)],
        compiler_params=_cparams(1),
    )(lgam, rqk, rv, rg, gain)


def _ret_bwd_call(rqk, rv, rg, o_ret, dpr, wbr, states, scores, gain, lgam, cos, sin):
    tp = rqk.shape[0]
    nc = tp // TM
    half = RET_QK // 2

    def body(lg_ref, qk_ref, v_ref, rg_ref, o_ref, dpr_ref, wbr_ref, st_ref, sc_ref, g_ref, cos_ref, sin_ref, dp_ref,
             dg_ref, ds_scr, dm_scr):
        @pl.when(pl.program_id(0) == 0)
        def _():
            ds_scr[...] = jnp.zeros_like(ds_scr)
            dg_ref[...] = jnp.zeros_like(dg_ref)
            _ret_fill_decay(lg_ref, dm_scr)

        cos, sin = cos_ref[...], sin_ref[...]
        for h in range(RET_HEADS):
            hs = slice(h * RET_V, (h + 1) * RET_V)
            dm, xi, zeta, gc = _ret_consts(lg_ref[h], dm_scr.at[h])
            o = o_ref[:, hs]
            mu = _head_mean(o)
            xc = o - mu
            rstd = lax.rsqrt(_head_mean(xc * xc) + EPS)
            xh = xc * rstd
            gain_h = g_ref[:, hs]
            g = rg_ref[:, hs]
            sg = _sigmoid(g)
            silu = g * sg
            dah = _dot(dpr_ref[...], wbr_ref[hs, :], NT)
            dp_ref[:, 4 * D_MODEL + h * RET_V:4 * D_MODEL + (h + 1) * RET_V] = (
                dah * (xh * gain_h) * (sg * (1.0 + g * (1.0 - sg)))).astype(BF16)
            dn = dah * silu
            dg_ref[:, hs] += jnp.sum(dn * xh, axis=0, keepdims=True)
            dxh = dn * gain_h
            do = rstd * (dxh - _head_mean(dxh) - xh * _head_mean(dxh * xh))
            dob = do.astype(BF16)
            q = qk_ref[:, h * RET_QK:(h + 1) * RET_QK]
            k = qk_ref[:, D_MODEL + h * RET_QK:D_MODEL + (h + 1) * RET_QK]
            v = v_ref[:, hs]
            sp = st_ref[0, h]
            ds = ds_scr[h]
            dsb = ds.astype(BF16)
            s = sc_ref[0, h]
            dsc = (_dot(dob, v, NT) * dm).astype(BF16)
            dq = _dot(dsc, k, NN) + xi * _dot(dob, sp, NT)
            dk = _dot(dsc, q, TN) + zeta * _dot(v, dsb, NT)
            kz = (k.astype(F32) * zeta).astype(BF16)
            dv = _dot(s, dob, TN) + _dot(kz, dsb, NN)
            qx = (q.astype(F32) * xi).astype(BF16)
            ds_scr[h] = gc * ds + _dot(qx, dob, TN)
            dp_ref[:, 2 * D_MODEL + h * RET_V:2 * D_MODEL + (h + 1) * RET_V] = dv.astype(BF16)
            dk = dk * (RET_QK ** -0.5)
            for base, t in ((0, dq), (D_MODEL, dk)):
                t1, t2 = t[:, :half], t[:, half:]
                dp_ref[:, base + h * RET_QK:base + h * RET_QK + half] = (t1 * cos + t2 * sin).astype(BF16)
                dp_ref[:, base + h * RET_QK + half:base + (h + 1) * RET_QK] = (t2 * cos - t1 * sin).astype(BF16)

    rev = lambda n: (nc - 1 - n, 0)
    return pl.pallas_call(
        body, name="ret_bwd", grid=(nc,),
        in_specs=[pl.BlockSpec(memory_space=pltpu.SMEM),
                  pl.BlockSpec((TM, 2 * D_MODEL), rev),
                  pl.BlockSpec((TM, RET_W), rev),
                  pl.BlockSpec((TM, RET_W), rev),
                  pl.BlockSpec((TM, RET_W), rev),
                  pl.BlockSpec((TM, D_MODEL), rev),
                  pl.BlockSpec((RET_W, D_MODEL), lambda n: (0, 0)),
                  pl.BlockSpec((1, RET_HEADS, RET_QK, RET_V), lambda n: (nc - 1 - n, 0, 0, 0)),
                  pl.BlockSpec((1, RET_HEADS, TM, TM), lambda n: (nc - 1 - n, 0, 0, 0)),
                  pl.BlockSpec((1, RET_W), lambda n: (0, 0)),
                  pl.BlockSpec((TM, half), rev),
                  pl.BlockSpec((TM, half), rev)],
        out_specs=[pl.BlockSpec((TM, W_R), rev), pl.BlockSpec((1, RET_W), lambda n: (0, 0))],
        out_shape=[jax.ShapeDtypeStruct((tp, W_R), BF16), jax.ShapeDtypeStruct((1, RET_W), F32)],
        scratch_shapes=[pltpu.VMEM((RET_HEADS, RET_QK, RET_V), F32), pltpu.VMEM((RET_HEADS, TM, TM), F32)],
        compiler_params=_cparams(1),
    )(lgam, rqk, rv, rg, o_ret, dpr, wbr, states, scores, gain, cos, sin)


GLA_LEVELS = tuple(GC >> (s + 1) for s in range(int(math.log2(GC // GLA_SUB))))
NLEV = len(GLA_LEVELS)


def _gla_tril():
    return np.tril(np.ones((GC, GC), np.float32))


def _gla_masks():
    ii = lax.broadcasted_iota(jnp.int32, (GC, GC), 0)
    jj = lax.broadcasted_iota(jnp.int32, (GC, GC), 1)
    masks = []
    for m in GLA_LEVELS:
        sh = int(math.log2(2 * m))
        masks.append(((ii >> sh) == (jj >> sh)) & ((ii & m) != 0) & ((jj & m) == 0))
    sh = int(math.log2(GLA_SUB))
    md = ((ii >> sh) == (jj >> sh)) & (jj <= ii)
    row = lax.broadcasted_iota(jnp.int32, (GC, 1), 0)
    second = [(row & m) != 0 for m in GLA_LEVELS]
    return masks, md, second


def _gla_gate_call(glr, wg, bg, pmat):
    tp = glr.shape[0]
    gb = _proj_rows(tp)
    assert gb % GC == 0

    def body(glr_ref, wg_ref, bg_ref, p_ref, z_ref, b_ref):
        z = _dot(glr_ref[...].astype(BF16), wg_ref[...], NN) + bg_ref[...]
        z_ref[...] = z
        la = (jnp.minimum(z, 0.0) - jnp.log1p(jnp.exp(-jnp.abs(z)))) * (1.0 / GATE_TAU)
        for r in range(0, gb, GC):
            b_ref[r:r + GC, :] = _exact_pm(p_ref[...], la[r:r + GC, :])

    tile = pl.BlockSpec((gb, GLA_KW), lambda i: (i, 0))
    return pl.pallas_call(
        body, name="gla_gate", grid=(tp // gb,),
        in_specs=[pl.BlockSpec((gb, 128), lambda i: (i, 0)), pl.BlockSpec((128, GLA_KW), lambda i: (0, 0)),
                  pl.BlockSpec((1, GLA_KW), lambda i: (0, 0)), pl.BlockSpec((GC, GC), lambda i: (0, 0))],
        out_specs=[tile, tile],
        out_shape=[jax.ShapeDtypeStruct((tp, GLA_KW), F32), jax.ShapeDtypeStruct((tp, GLA_KW), F32)],
        compiler_params=_cparams(1),
    )(glr, wg, bg, pmat)


def _gla_gate_bwd_call(db, z, glr, wg, pmat_t, d_g):
    tp = db.shape[0]
    gb = _proj_rows(tp)
    assert gb % GC == 0 and (W_GP - 128) % 128 == 0

    def body(db_ref, z_ref, glr_ref, wg_ref, pt_ref, dgin_ref, dg_ref, dwg_ref, dbg_ref):
        i = pl.program_id(0)

        @pl.when(i == 0)
        def _():
            dwg_ref[...] = jnp.zeros_like(dwg_ref)
            dbg_ref[...] = jnp.zeros_like(dbg_ref)

        dla = jnp.concatenate([_exact_pm(pt_ref[...], db_ref[r:r + GC, :]) for r in range(0, gb, GC)], axis=0)
        row = i * gb + lax.broadcasted_iota(jnp.int32, (gb, 1), 0)
        dz = jnp.where(row >= PADF, dla * (1.0 / GATE_TAU) * _sigmoid(-z_ref[...]), 0.0)
        dzb = dz.astype(BF16)
        dg_ref[...] = _dot(dzb, wg_ref[...], NT).astype(BF16)
        dwg_ref[...] += _dot(glr_ref[...].astype(BF16), dzb, TN)
        dbg_ref[...] += jnp.sum(dz, axis=0, keepdims=True)

    tile = pl.BlockSpec((gb, GLA_KW), lambda i: (i, 0))
    const = lambda i: (0, 0)
    return pl.pallas_call(
        body, name="gla_gate_bwd", grid=(tp // gb,),
        in_specs=[tile, tile, pl.BlockSpec((gb, 128), lambda i: (i, 0)), pl.BlockSpec((128, GLA_KW), const),
                  pl.BlockSpec((GC, GC), const), ANY],
        out_specs=[pl.BlockSpec((gb, 128), lambda i: (i, (W_GP - 128) // 128)), pl.BlockSpec((128, GLA_KW), const),
                   pl.BlockSpec((1, GLA_KW), const)],
        out_shape=[jax.ShapeDtypeStruct(d_g.shape, BF16), jax.ShapeDtypeStruct((128, GLA_KW), F32),
                   jax.ShapeDtypeStruct((1, GLA_KW), F32)],
        input_output_aliases={5: 0}, compiler_params=_cparams(1),
    )(db, z, glr, wg, pmat_t, d_g)


def _gla_row_steps(b_ref, cs, rows, size):
    parts = [jnp.zeros((size, GLA_K), F32) if r is None else jnp.broadcast_to(b_ref[r:r + 1, cs], (size, GLA_K))
             for r in rows]
    return parts[0] if len(parts) == 1 else jnp.concatenate(parts, axis=0)


def _gla_factors(b_ref, h, second):
    cs = slice(h * GLA_K, (h + 1) * GLA_K)
    b = b_ref[:, cs]
    fq, fk = [], []
    for l, m in enumerate(GLA_LEVELS):
        d = b - _gla_row_steps(b_ref, cs, [s + m - 1 for s in range(0, GC, 2 * m)], 2 * m)
        f = jnp.exp(jnp.where(second[l], d, -d))
        fq.append(jnp.where(second[l], f, 0.0))
        fk.append(jnp.where(second[l], 0.0, f))
    dd = b - _gla_row_steps(b_ref, cs, [None] + [s - 1 for s in range(GLA_SUB, GC, GLA_SUB)], GLA_SUB)
    ed = jnp.exp(dd)
    edi = jnp.exp(-dd)
    eb = jnp.exp(b)
    bl = b_ref[GC - 1:GC, cs]
    ee = jnp.exp(bl - b)
    ebl = jnp.exp(bl)
    return fq, fk, ed, edi, eb, ee, ebl


def _gla_scaled(q, k, fq, fk, ed, edi):
    qt = [(q * f).astype(BF16) for f in fq]
    kt = [(k * f).astype(BF16) for f in fk]
    return qt, kt, (q * ed).astype(BF16), (k * edi).astype(BF16)


def _gla_scores(qt, kt, qd, kd, masks, md):
    a = jnp.where(md, _dot(qd, kd, NT), 0.0)
    for l in range(NLEV):
        a = a + jnp.where(masks[l], _dot(qt[l], kt[l], NT), 0.0)
    return a.astype(BF16)


def _gla_fwd_call(gqk, gv, b, gg, gain, comm=None):
    tp = gqk.shape[0]
    nc = tp // GC
    ns = nc // GS
    n_xc = len(comm.srcs) if comm else 0

    def body(qk_ref, v_ref, b_ref, gg_ref, g_ref, *rest):
        xc_src = rest[:n_xc]
        o_ref, a_ref, st_ref, am_ref = rest[n_xc:n_xc + 4]
        xc_dst = rest[n_xc + 4:2 * n_xc + 4]
        s_scr = rest[2 * n_xc + 4]
        n = pl.program_id(0)
        if n_xc:
            begin, finish = comm.make(xc_src, xc_dst, rest[-2], rest[-1])
            pl.when(n == 0)(begin)
            pl.when(n == ns - 1)(finish)

        @pl.when(n == 0)
        def _():
            s_scr[...] = jnp.zeros_like(s_scr)

        masks, md, second = _gla_masks()
        for cc in range(GS):
            rows = pl.ds(cc * GC, GC)
            qk_c, v_c, b_c, gg_c, o_c, a_c = (r.at[rows] for r in (qk_ref, v_ref, b_ref, gg_ref, o_ref, a_ref))
            for h in range(GLA_HEADS):
                q = qk_c[:, h * GLA_K:(h + 1) * GLA_K]
                k = qk_c[:, GLA_KW + h * GLA_K:GLA_KW + (h + 1) * GLA_K]
                vs = slice(h * GLA_V, (h + 1) * GLA_V)
                v = v_c[:, vs]
                fq, fk, ed, edi, eb, ee, ebl = _gla_factors(b_c, h, second)
                a = _gla_scores(*_gla_scaled(q, k, fq, fk, ed, edi), masks, md)
                am_ref[cc, h] = a
                sb = s_scr[h].astype(BF16)
                st_ref[cc, h] = sb
                o = _dot(a, v, NN) + _dot((q * eb).astype(BF16), sb, NT)
                s_scr[h] = s_scr[h] * ebl + _dot(v, (k * ee).astype(BF16), TN)
                o_c[:, vs] = o
                xh = o * lax.rsqrt(_head_mean(o * o) + EPS)
                a_c[:, vs] = (xh * g_ref[:, vs] * _silu(gg_c[:, vs])).astype(BF16)

    return pl.pallas_call(
        body, name="gla_fwd", grid=(ns,),
        in_specs=[pl.BlockSpec((GS * GC, 2 * GLA_KW), lambda n: (n, 0)),
                  pl.BlockSpec((GS * GC, GLA_W), lambda n: (n, 0)),
                  pl.BlockSpec((GS * GC, GLA_KW), lambda n: (n, 0)),
                  pl.BlockSpec((GS * GC, GLA_W), lambda n: (n, 0)),
                  pl.BlockSpec((1, GLA_W), lambda n: (0, 0))] + [ANY] * n_xc,
        out_specs=[pl.BlockSpec((GS * GC, GLA_W), lambda n: (n, 0)),
                   pl.BlockSpec((GS * GC, GLA_W), lambda n: (n, 0)),
                   pl.BlockSpec((GS, GLA_HEADS, GLA_V, GLA_K), lambda n: (n, 0, 0, 0)),
                   pl.BlockSpec((GS, GLA_HEADS, GC, GC), lambda n: (n, 0, 0, 0))] + [ANY] * n_xc,
        out_shape=[jax.ShapeDtypeStruct((tp, GLA_W), F32), jax.ShapeDtypeStruct((tp, GLA_W), BF16),
                   jax.ShapeDtypeStruct((nc, GLA_HEADS, GLA_V, GLA_K), BF16),
                   jax.ShapeDtypeStruct((nc, GLA_HEADS, GC, GC), BF16)] + (list(comm.out_shapes) if comm else []),
        scratch_shapes=[pltpu.VMEM((GLA_HEADS, GLA_V, GLA_K), F32)] + (_comm_sems(comm) if comm else []),
        compiler_params=_cparams(1),
    )(gqk, gv, b, gg, gain, *(comm.srcs if comm else ()))


def _gla_bwd_call(gqk, gv, b, gg, o_gla, da, states, scores, gain, comm=None):
    tp = gqk.shape[0]
    nc = tp // GC
    ns = nc // GS
    o_gv, o_gg = 2 * GLA_KW, 2 * GLA_KW + GLA_W
    n_xc = len(comm.srcs) if comm else 0

    def body(qk_all, v_all, b_all, gg_all, o_all, da_all, st_ref, am_ref, g_ref, *rest):
        xc_src = rest[:n_xc]
        dp_all, db_all, dg_ref = rest[n_xc:n_xc + 3]
        xc_dst = rest[n_xc + 3:2 * n_xc + 3]
        ds_scr = rest[2 * n_xc + 3]
        n = pl.program_id(0)
        if n_xc:
            begin, finish = comm.make(xc_src, xc_dst, rest[-2], rest[-1])
            pl.when(n == 0)(begin)
            pl.when(n == ns - 1)(finish)

        @pl.when(n == 0)
        def _():
            ds_scr[...] = jnp.zeros_like(ds_scr)
            dg_ref[...] = jnp.zeros_like(dg_ref)

        masks, md, second = _gla_masks()
        for cc, h in [(cc, h) for cc in reversed(range(GS)) for h in range(GLA_HEADS)]:
            rows = pl.ds(cc * GC, GC)
            qk_ref, v_ref, b_scr, gg_ref, o_ref, da_ref, dp_ref, db_scr = (
                r.at[rows] for r in (qk_all, v_all, b_all, gg_all, o_all, da_all, dp_all, db_all))
            cs = slice(h * GLA_K, (h + 1) * GLA_K)
            vs = slice(h * GLA_V, (h + 1) * GLA_V)
            o = o_ref[:, vs]
            rstd = lax.rsqrt(_head_mean(o * o) + EPS)
            xh = o * rstd
            gain_h = g_ref[:, vs]
            g = gg_ref[:, vs]
            sg = _sigmoid(g)
            dah = da_ref[:, vs]
            dp_ref[:, o_gg + h * GLA_V:o_gg + (h + 1) * GLA_V] = (
                dah * (xh * gain_h) * (sg * (1.0 + g * (1.0 - sg)))).astype(BF16)
            dn = dah * (g * sg)
            dg_ref[:, vs] += jnp.sum(dn * xh, axis=0, keepdims=True)
            dxh = dn * gain_h
            do = rstd * (dxh - xh * _head_mean(dxh * xh))
            dob = do.astype(BF16)
            q = qk_ref[:, cs]
            k = qk_ref[:, GLA_KW + h * GLA_K:GLA_KW + (h + 1) * GLA_K]
            v = v_ref[:, vs]
            fq, fk, ed, edi, eb, ee, ebl = _gla_factors(b_scr, h, second)
            qt, kt, qd, kd = _gla_scaled(q, k, fq, fk, ed, edi)
            sp = st_ref[cc, h]
            ds = ds_scr[h]
            dsb = ds.astype(BF16)
            q_in = q * eb
            k_end = k * ee
            da_s = _dot(dob, v, NT)
            dv = _dot(am_ref[cc, h], dob, TN) + _dot(k_end.astype(BF16), dsb, NT)
            dq_in = _dot(dob, sp, NN)
            dk_end = _dot(v, dsb, NN)
            dbl = jnp.sum(sp.astype(F32) * ds, axis=0, keepdims=True) * ebl
            ds_scr[h] = ds * ebl + _dot(dob, q_in.astype(BF16), TN)
            dq = dq_in * eb
            dk = dk_end * ee
            de_end = dk_end * k_end
            db = dq_in * q_in - de_end
            placed = [(GC - 1, jnp.sum(de_end, axis=0, keepdims=True) + dbl)]
            for l, m in enumerate(GLA_LEVELS):
                dal = jnp.where(masks[l], da_s, 0.0).astype(BF16)
                dqt = _dot(dal, kt[l], NN)
                dkt = _dot(dal, qt[l], TN)
                dq = dq + dqt * fq[l]
                dk = dk + dkt * fk[l]
                gl = dqt * (q * fq[l]) - dkt * (k * fk[l])
                db = db + gl
                placed += [(s + m - 1, -jnp.sum(gl[s:s + 2 * m], axis=0, keepdims=True)) for s in range(0, GC, 2 * m)]
            dad = jnp.where(md, da_s, 0.0).astype(BF16)
            dqd = _dot(dad, kd, NN)
            dkd = _dot(dad, qd, TN)
            dq = dq + dqd * ed
            dk = dk + dkd * edi
            gd = dqd * (q * ed) - dkd * (k * edi)
            db = db + gd
            placed += [(s - 1, -jnp.sum(gd[s:s + GLA_SUB], axis=0, keepdims=True)) for s in range(GLA_SUB, GC, GLA_SUB)]
            db_scr[:, cs] = db
            for r, val in placed:
                db_scr[r:r + 1, cs] += val
            dp_ref[:, cs] = (dq * (GLA_K ** -0.5)).astype(BF16)
            dp_ref[:, GLA_KW + h * GLA_K:GLA_KW + (h + 1) * GLA_K] = dk.astype(BF16)
            dp_ref[:, o_gv + h * GLA_V:o_gv + (h + 1) * GLA_V] = dv.astype(BF16)

    rev = lambda n: (ns - 1 - n, 0)
    const = lambda n: (0, 0)
    xc_shapes, xc_sems = (list(comm.out_shapes), _comm_sems(comm)) if n_xc else ([], [])
    return pl.pallas_call(
        body, name="gla_bwd", grid=(ns,),
        in_specs=[pl.BlockSpec((GS * GC, 2 * GLA_KW), rev),
                  pl.BlockSpec((GS * GC, GLA_W), rev),
                  pl.BlockSpec((GS * GC, GLA_KW), rev),
                  pl.BlockSpec((GS * GC, GLA_W), rev),
                  pl.BlockSpec((GS * GC, GLA_W), rev),
                  pl.BlockSpec((GS * GC, GLA_W), rev),
                  pl.BlockSpec((GS, GLA_HEADS, GLA_V, GLA_K), lambda n: (ns - 1 - n, 0, 0, 0)),
                  pl.BlockSpec((GS, GLA_HEADS, GC, GC), lambda n: (ns - 1 - n, 0, 0, 0)),
                  pl.BlockSpec((1, GLA_W), const)] + [ANY] * n_xc,
        out_specs=[pl.BlockSpec((GS * GC, W_GP), rev), pl.BlockSpec((GS * GC, GLA_KW), rev),
                   pl.BlockSpec((1, GLA_W), const)] + [ANY] * n_xc,
        out_shape=[jax.ShapeDtypeStruct((tp, W_GP), BF16), jax.ShapeDtypeStruct((tp, GLA_KW), F32),
                   jax.ShapeDtypeStruct((1, GLA_W), F32)] + xc_shapes,
        scratch_shapes=[pltpu.VMEM((GLA_HEADS, GLA_V, GLA_K), F32)] + xc_sems,
        compiler_params=_cparams(1),
    )(gqk, gv, b, gg, o_gla, da, states, scores, gain, *(comm.srcs if comm else ()))


def _mid_call(a_ret, a_gla, mg, h0, tgt, wbr, wbg, wout, gf):
    tp = h0.shape[0]
    nt = tp // TM

    def body(ar_ref, ag_ref, mg_ref, h_ref, t_ref, wbr_ref, wbg_ref, wo_ref, gf_ref,
             dh1_ref, dag_ref, dm_ref, mb_ref, dh1b_ref, dprb_ref, dpgb_ref, loss_ref, dgf_ref):
        i = pl.program_id(0)

        @pl.when(i == 0)
        def _():
            loss_ref[...] = jnp.zeros_like(loss_ref)
            dgf_ref[...] = jnp.zeros_like(dgf_ref)

        ar, ag = ar_ref[...], ag_ref[...]
        pr = _dot(ar, wbr_ref[...], NN)
        pg = _dot(ag, wbg_ref[...], NN)
        sr = _sigmoid(mg_ref[:, :D_MODEL])
        sg = _sigmoid(mg_ref[:, D_MODEL:])
        merged = (sr * pr + sg * pg).astype(BF16)
        mb_ref[...] = merged
        h1 = h_ref[...] + _dot(merged, wo_ref[...], NN)
        r1 = lax.rsqrt(jnp.mean(h1 * h1, axis=-1, keepdims=True) + EPS)
        xh = h1 * r1
        gfv = gf_ref[...]
        live = jnp.where(i > 0, 1.0, 0.0).astype(F32)
        err = (xh * gfv - t_ref[...]) * live
        loss_ref[...] += jnp.full(loss_ref.shape, 0.5 / D_MODEL, F32) * jnp.sum(err * err)
        dy = err * (1.0 / D_MODEL)
        dgf_ref[...] += jnp.sum(dy * xh, axis=0, keepdims=True)
        dxh = dy * gfv
        dh1 = r1 * (dxh - xh * jnp.mean(dxh * xh, axis=-1, keepdims=True))
        dh1_ref[...] = dh1
        dh1b = dh1.astype(BF16)
        dh1b_ref[...] = dh1b
        dmerged = _dot(dh1b, wo_ref[...], NT)
        dm_ref[:, :D_MODEL] = (dmerged * pr * sr * (1.0 - sr)).astype(BF16)
        dm_ref[:, D_MODEL:] = (dmerged * pg * sg * (1.0 - sg)).astype(BF16)
        dpr = (dmerged * sr).astype(BF16)
        dpg = (dmerged * sg).astype(BF16)
        dprb_ref[...] = dpr
        dpgb_ref[...] = dpg
        dag_ref[...] = _dot(dpg, wbg_ref[...], NT)

    tile = lambda w: pl.BlockSpec((TM, w), lambda i: (i, 0))
    const = lambda r, w: pl.BlockSpec((r, w), lambda i: (0, 0))
    return pl.pallas_call(
        body, name="merge_out_loss", grid=(nt,),
        in_specs=[tile(RET_W), tile(GLA_W), tile(W_M), tile(D_MODEL),
                  pl.BlockSpec((TM, D_MODEL), lambda i: (jnp.maximum(i - 1, 0), 0)),
                  const(RET_W, D_MODEL), const(GLA_W, D_MODEL), const(D_MODEL, D_MODEL), const(1, D_MODEL)],
        out_specs=[tile(D_MODEL), tile(GLA_W), tile(W_M), tile(D_MODEL), tile(D_MODEL), tile(D_MODEL),
                   tile(D_MODEL), const(1, 128), const(1, D_MODEL)],
        out_shape=[jax.ShapeDtypeStruct((tp, D_MODEL), F32), jax.ShapeDtypeStruct((tp, GLA_W), F32),
                   jax.ShapeDtypeStruct((tp, W_M), BF16),
                   jax.ShapeDtypeStruct((tp, D_MODEL), BF16), jax.ShapeDtypeStruct((tp, D_MODEL), BF16),
                   jax.ShapeDtypeStruct((tp, D_MODEL), BF16), jax.ShapeDtypeStruct((tp, D_MODEL), BF16),
                   jax.ShapeDtypeStruct((1, 128), F32), jax.ShapeDtypeStruct((1, D_MODEL), F32)],
        compiler_params=_cparams(1),
    )(a_ret, a_gla, mg, h0, tgt, wbr, wbg, wout, gf)


def _device_step(x2d, tgt2d, meta, norm_gain, w_in_part, w_gate_up, b_gate, ret_gain, gla_gain, branch_parts,
                 final_gain, ck):
    seq = x2d.shape[0]
    tp = T0 + seq
    head = jnp.concatenate([jnp.zeros((PADF, D_MODEL), F32), meta], axis=0)
    wg_pad = jnp.pad(w_gate_up, ((0, 128 - GATE_RANK), (0, 0))).astype(BF16)

    half = RET_QK // 2
    cos, sin = (jnp.asarray(t) for t in _rope_tables(tp))
    lgam = jnp.log1p(-(2.0 ** (-5.0 - jnp.arange(RET_HEADS, dtype=F32))))
    pmat = jnp.asarray(_gla_tril(), BF16)
    pmat_t = jnp.asarray(_gla_tril().T.copy(), BF16)

    h0, u, g_in = _rms_call(x2d, head, norm_gain, _gather_plan([w_in_part], relay=(True,)))
    hr, sw = w_in_part.shape
    w_in_bf = g_in.reshape(4, 2, hr, sw).transpose(1, 2, 0, 3).reshape(2 * hr, 4 * sw)
    w_r = w_in_bf
    w_g = jnp.pad(w_in_bf[:, W_R:W_R + W_G], ((0, 0), (0, W_GP - W_G)))
    w_m = w_in_bf[:, W_R + W_G:]
    tab = pl.BlockSpec((_proj_rows(tp), half), lambda j, i: (i, 0))
    rqk = _mm_nn("proj_rqk", u, w_r, BF16, D_MODEL, 0, 2 * D_MODEL, _rope_epilogue, (cos, sin), (tab, tab))
    rv = _mm_nn("proj_rv", u, w_r, BF16, RET_W, 2 * D_MODEL, RET_W)
    rg = _mm_nn("proj_rg", u, w_r, F32, RET_W, 4 * D_MODEL, RET_W)
    gqk = _mm_nn("proj_gqk", u, w_g, F32, 2 * GLA_KW, 0, 2 * GLA_KW, _gqk_epilogue)
    gv = _mm_nn("proj_gv", u, w_g, BF16, GLA_W, 2 * GLA_KW, GLA_W)
    gg = _mm_nn("proj_gg", u, w_g, F32, GLA_W, 2 * GLA_KW + GLA_W, GLA_W)
    glr = _mm_nn("proj_glr", u, w_g, F32, 128, 2 * GLA_KW + 2 * GLA_W, 128)
    mg = _mm_nn("proj_mg", u, w_m, F32, W_M, 0, W_M)

    o_ret, a_ret, st_ret, sc_ret = _ret_fwd_call(rqk, rv, rg, ret_gain, lgam)
    z_gate, b_dec = _gla_gate_call(glr, wg_pad, b_gate, pmat)
    o_gla, a_gla, st_gla, sc_gla, g_br, g_bg, g_out = _gla_fwd_call(gqk, gv, b_dec, gg, gla_gain,
                                                                    comm=_spread_plan(branch_parts))
    wbr = g_br.reshape(RET_W, D_MODEL)
    wbg = g_bg.reshape(GLA_W, D_MODEL)
    wout = g_out.reshape(D_MODEL, D_MODEL)

    gf = final_gain.reshape(1, D_MODEL)
    (dh1, da_gla, dm, merged_b, dh1_b, dpr_b, dpg_b, loss, dgf) = _mid_call(
        a_ret, a_gla, mg, h0, tgt2d, wbr, wbg, wout, gf)

    names_b = ("w_branch_ret", "w_branch_gla", "w_out")
    g2_b = [_mm_tn("dw_br", a_ret, dpr_b, D_MODEL).reshape(4, 2, RET_W // 8, D_MODEL).transpose(1, 0, 2, 3),
            _mm_tn("dw_bg", a_gla, dpg_b, D_MODEL).reshape(4, 2, GLA_W // 8, D_MODEL).transpose(1, 0, 2, 3),
            _mm_tn("dw_out", merged_b, dh1_b, D_MODEL).reshape(4, 2, D_MODEL // 8, D_MODEL).transpose(1, 0, 2, 3)]
    sib_b = _swap_halves_call("swap_halves_branch", g2_b)
    sum_b = [_add_half_call("add_half_" + nm, g, b, ck) for nm, g, b in zip(names_b, g2_b, sib_b)]
    d_g, db_dec, dgla_gain, *chips_b = _gla_bwd_call(gqk, gv, b_dec, gg, o_gla, da_gla, st_gla, sc_gla, gla_gain,
                                                     comm=_exchange_plan(sum_b))
    d_g, dwg, dbg = _gla_gate_bwd_call(db_dec, z_gate, glr, wg_pad, pmat_t, d_g)
    mine = [_add_chips_call("add_chips_" + nm, g, b, p, ck) for nm, g, b, p in zip(names_b, g2_b, sib_b, chips_b)]

    d_r, dret_gain = _ret_bwd_call(rqk, rv, rg, o_ret, dpr_b, wbr, st_ret, sc_ret, ret_gain, lgam, cos, sin)

    dwp = _mm_tn("dw_r", u, d_r, 2 * D_MODEL, out_cols=IN_PAD)
    dwp = _mm_tn("dw_g", u, d_g, D_MODEL, ncols=W_GP - 128, into=dwp, col0=W_R)
    dwp = _mm_tn("dw_glr", u, d_g, 128, ncols=128, bcol0=W_GP - 128, into=dwp, col0=W_R + W_GP - 128)
    g2_in = _place_merge_cols_call(dwp, _mm_tn("dw_m", u, dm, 2 * D_MODEL)).reshape(2, D_MODEL // 2, IN_PAD)

    du, sib_in = _mm_nt_acc("du_g", d_g, w_g, W_GP, comm=_swap_plan([g2_in]), tb=_proj_rows(tp))
    sum_in = _add_rows_call("add_half_w_in", g2_in, sib_in, ck)
    du, chips_in = _mm_nt_acc("du_r", d_r, w_r, 2 * D_MODEL, acc_in=du, comm=_exchange_window_plan(sum_in),
                              tb=_proj_rows(tp))
    tile = pl.BlockSpec((TB, D_MODEL), lambda i, kk: (i, 0))
    row = pl.BlockSpec((1, D_MODEL), lambda i, kk: (0, 0))
    dx, dmeta, dnorm_gain = _mm_nt_acc(
        "du_m", dm, w_m, W_M, acc_in=du, epilogue=_rms_bwd_epilogue, extras=(h0, norm_gain, dh1),
        extra_specs=(tile, row, tile),
        extra_out_shapes=(jax.ShapeDtypeStruct((seq, D_MODEL), F32), jax.ShapeDtypeStruct((N_META, D_MODEL), F32),
                          jax.ShapeDtypeStruct((1, D_MODEL), F32)),
        extra_out_specs=(ANY, pl.BlockSpec((N_META, D_MODEL), lambda i, kk: (0, 0)), row),
        extra_scratch=(pltpu.VMEM((2, TB, D_MODEL), F32), pltpu.SemaphoreType.DMA((2,))))
    mine = [_add_window_call("add_chips_w_in", g2_in, sib_in, chips_in, ck)] + mine
    full = _join_halves_call("join_halves", mine)

    return dict(loss=loss[0, 0], dx=dx, dmeta=dmeta, norm_gain=dnorm_gain, w_gate_up=dwg[:GATE_RANK], b_gate=dbg,
                ret_norm_gain=dret_gain, gla_norm_gain=dgla_gain, final_norm_gain=dgf.reshape(D_MODEL),
                w_in=full[0], w_branch_ret=full[1], w_branch_gla=full[2], w_out=full[3])


MESH = pl.DeviceIdType.MESH
ANY = pl.BlockSpec(memory_space=pl.ANY)


def _place():
    return lax.axis_index("x"), lax.axis_index("y"), lax.axis_index("c")


def _gather8_call(name, parts):
    comm = _gather_plan(parts)
    n = len(parts)

    def body(*refs):
        begin, finish = comm.make(refs[:n], refs[n:2 * n], refs[-2], refs[-1])
        begin()
        finish()

    return pl.pallas_call(
        body, name=name, out_shape=list(comm.out_shapes), in_specs=[ANY] * n, out_specs=[ANY] * n,
        scratch_shapes=_comm_sems(comm),
    )(*parts)


def _swap_halves_call(name, gs):
    n = len(gs)

    def body(*refs):
        g_refs, b_refs = refs[:n], refs[n:2 * n]
        send_sems, recv_sems = refs[2 * n:]
        x, y, c = _place()
        copies = [pltpu.make_async_remote_copy(
            src_ref=g_refs[t].at[1 - c], dst_ref=b_refs[t], send_sem=send_sems.at[t], recv_sem=recv_sems.at[t],
            device_id=(x, y, 1 - c), device_id_type=MESH) for t in range(n)]
        for cp in copies:
            cp.start()
        for cp in copies:
            cp.wait()

    return pl.pallas_call(
        body, name=name,
        out_shape=[jax.ShapeDtypeStruct(g.shape[1:], g.dtype) for g in gs],
        in_specs=[ANY] * n, out_specs=[ANY] * n,
        scratch_shapes=[pltpu.SemaphoreType.DMA((n,)), pltpu.SemaphoreType.DMA((n,))],
    )(*gs)


def _join_halves_call(name, ts):
    n = len(ts)

    def body(*refs):
        o_refs = refs[n:2 * n]
        send_sems, recv_sems = refs[2 * n:]
        x, y, c = _place()
        copies = [pltpu.make_async_remote_copy(
            src_ref=o_refs[t].at[c], dst_ref=o_refs[t].at[c], send_sem=send_sems.at[t], recv_sem=recv_sems.at[t],
            device_id=(x, y, 1 - c), device_id_type=MESH) for t in range(n)]
        for cp in copies:
            cp.start()
        for t in range(n):
            copies[t].wait_send()
            pltpu.make_async_remote_copy(
                src_ref=o_refs[t].at[c], dst_ref=o_refs[t].at[1 - c], send_sem=send_sems.at[t],
                recv_sem=recv_sems.at[t], device_id=(x, y, 1 - c), device_id_type=MESH).wait_recv()

    return pl.pallas_call(
        body, name=name,
        out_shape=[jax.ShapeDtypeStruct(t.shape, t.dtype) for t in ts],
        in_specs=[ANY] * n, out_specs=[ANY] * n, input_output_aliases={t: t for t in range(n)},
        scratch_shapes=[pltpu.SemaphoreType.DMA((n,)), pltpu.SemaphoreType.DMA((n,))],
    )(*ts)


def _row_block(rows, cols, budget):
    best = 8
    for rb in range(8, rows + 1, 8):
        if rows % rb == 0 and rb * cols * 4 <= budget:
            best = rb
    return best


def _add_half_call(name, g, b, ck):
    _, _, r, cc = g.shape
    rb = _row_block(r, cc, 2 * 1024 * 1024)

    def body(ck_ref, g_ref, b_ref, o_ref):
        o_ref[...] = (g_ref[...] + b_ref[...]).astype(BF16)

    return pl.pallas_call(
        body, name=name,
        grid_spec=pltpu.PrefetchScalarGridSpec(
            num_scalar_prefetch=1, grid=(4, r // rb),
            in_specs=[pl.BlockSpec((None, None, rb, cc), lambda k, i, ck_ref: (ck_ref[0], k, i, 0)),
                      pl.BlockSpec((None, rb, cc), lambda k, i, ck_ref: (k, i, 0))],
            out_specs=pl.BlockSpec((None, rb, cc), lambda k, i, ck_ref: (k, i, 0))),
        out_shape=jax.ShapeDtypeStruct(b.shape, BF16),
        compiler_params=_cparams(2),
    )(ck, g, b)


def _add_rows_call(name, g, b, ck):
    _, r, cc = g.shape
    rb = _row_block(r, cc, 2 * 1024 * 1024)

    def body(ck_ref, g_ref, b_ref, o_ref):
        o_ref[...] = (g_ref[...] + b_ref[...]).astype(BF16)

    return pl.pallas_call(
        body, name=name,
        grid_spec=pltpu.PrefetchScalarGridSpec(
            num_scalar_prefetch=1, grid=(r // rb,),
            in_specs=[pl.BlockSpec((None, rb, cc), lambda i, ck_ref: (ck_ref[0], i, 0)),
                      pl.BlockSpec((rb, cc), lambda i, ck_ref: (i, 0))],
            out_specs=pl.BlockSpec((rb, cc), lambda i, ck_ref: (i, 0))),
        out_shape=jax.ShapeDtypeStruct((r, cc), BF16),
        compiler_params=_cparams(1),
    )(ck, g, b)


def _add_window_call(name, g, b, p, ck):
    _, r, _ = g.shape
    nb, step = WIN_W // 128, WIN_STEP // 128

    def body(ck_ref, g_ref, b_ref, p0_ref, p1_ref, p2_ref, o_ref):
        own = g_ref[...] + b_ref[...]
        o_ref[...] = ((own + p0_ref[...].astype(F32)) + p1_ref[...].astype(F32)) + p2_ref[...].astype(F32)

    def peer(j):
        return pl.BlockSpec((None, r, 128), lambda i, ck_ref: (j, 0, i))

    return pl.pallas_call(
        body, name=name,
        grid_spec=pltpu.PrefetchScalarGridSpec(
            num_scalar_prefetch=1, grid=(nb,),
            in_specs=[pl.BlockSpec((None, r, 128), lambda i, ck_ref: (ck_ref[0], 0, step * ck_ref[1] + i)),
                      pl.BlockSpec((r, 128), lambda i, ck_ref: (0, step * ck_ref[1] + i)),
                      peer(0), peer(1), peer(2)],
            out_specs=pl.BlockSpec((None, r, 128), lambda i, ck_ref: (ck_ref[0], 0, i))),
        out_shape=jax.ShapeDtypeStruct((2, r, WIN_W), F32),
        compiler_params=_cparams(1),
    )(ck, g, b, p, p, p)


def _add_chips_call(name, g, b, p, ck):
    _, _, r, cc = g.shape
    rb = _row_block(r, cc, 2 * 1024 * 1024)

    def body(ck_ref, g_ref, b_ref, p0_ref, p1_ref, p2_ref, o_ref):
        own = g_ref[...] + b_ref[...]
        o_ref[...] = ((own + p0_ref[...].astype(F32)) + p1_ref[...].astype(F32)) + p2_ref[...].astype(F32)

    def peer(j):
        return pl.BlockSpec((None, rb, cc), lambda i, ck_ref: (j, i, 0))

    return pl.pallas_call(
        body, name=name,
        grid_spec=pltpu.PrefetchScalarGridSpec(
            num_scalar_prefetch=1, grid=(r // rb,),
            in_specs=[pl.BlockSpec((None, None, rb, cc), lambda i, ck_ref: (ck_ref[0], ck_ref[1], i, 0)),
                      pl.BlockSpec((None, rb, cc), lambda i, ck_ref: (ck_ref[1], i, 0)),
                      peer(0), peer(1), peer(2)],
            out_specs=pl.BlockSpec((None, rb, cc), lambda i, ck_ref: (ck_ref[0], i, 0))),
        out_shape=jax.ShapeDtypeStruct((2, r, cc), F32),
        compiler_params=_cparams(1),
    )(ck, g, b, p, p, p)


def _sum8_call(name, g):
    def body(g_ref, o_ref):
        acc = g_ref[0]
        for d in range(1, 8):
            acc = acc + g_ref[d]
        o_ref[...] = acc

    return pl.pallas_call(body, name=name, out_shape=jax.ShapeDtypeStruct(g.shape[1:], F32))(g)


def _adamw_call(name, w, g, m, v):
    r, cc = w.shape
    if r % 8 == 0 or r * cc * 4 <= 1024 * 1024:
        rb = _row_block(r, cc, 1024 * 1024) if r % 8 == 0 else r
        grid, spec = (r // rb,), pl.BlockSpec((rb, cc), lambda i: (i, 0))
    else:
        grid, spec = (cc // 128,), pl.BlockSpec((r, 128), lambda i: (0, i))

    def body(w_ref, g_ref, m_ref, v_ref, d_ref, m2_ref, v2_ref):
        gv = g_ref[...]
        m2 = ADAM_B1 * m_ref[...] + (1.0 - ADAM_B1) * gv
        v2 = ADAM_B2 * v_ref[...] + (1.0 - ADAM_B2) * (gv * gv)
        m_hat = m2 / (1.0 - ADAM_B1 ** ADAM_STEP)
        v_hat = v2 / (1.0 - ADAM_B2 ** ADAM_STEP)
        d_ref[...] = -ADAM_LR * (m_hat / (jnp.sqrt(v_hat) + ADAM_EPS) + ADAM_WD * w_ref[...])
        m2_ref[...] = m2
        v2_ref[...] = v2

    return pl.pallas_call(
        body, name=name, grid=grid, in_specs=[spec] * 4, out_specs=[spec] * 3,
        out_shape=[jax.ShapeDtypeStruct((r, cc), F32)] * 3, compiler_params=_cparams(1),
    )(w, g, m, v)


SMALL = (("norm_gain", D_MODEL), ("b_gate", GLA_KW), ("ret_norm_gain", RET_W), ("gla_norm_gain", GLA_W),
         ("final_norm_gain", D_MODEL), ("w_gate_up", GATE_RANK * GLA_KW), ("meta_tokens", N_META * D_MODEL),
         ("loss", 1))


def _pack_rows(vecs, rows):
    flat = jnp.concatenate([v.reshape(-1) for v in vecs])
    return jnp.pad(flat, (0, rows * 128 - flat.shape[0])).reshape(rows, 128)


def kernel(x, meta_tokens, norm_gain, w_in, w_gate_up, b_gate, ret_norm_gain, gla_norm_gain, w_branch_ret, w_branch_gla, w_out, final_norm_gain, loss_target, m_meta_tokens, m_norm_gain, m_w_in, m_w_gate_up, m_b_gate, m_ret_norm_gain, m_gla_norm_gain, m_w_branch_ret, m_w_branch_gla, m_w_out, m_final_norm_gain, v_meta_tokens, v_norm_gain, v_w_in, v_w_gate_up, v_b_gate, v_ret_norm_gain, v_gla_norm_gain, v_w_branch_ret, v_w_branch_gla, v_w_out, v_final_norm_gain):
    xi, yi, ci = _place()
    kme = 2 * xi + yi
    ck = jnp.stack([ci, kme]).astype(jnp.int32)
    sw_in = w_in.shape[2]

    def my_half(a, dtype):
        r, cc = a.shape
        return lax.dynamic_index_in_dim(a.reshape(2, r // 2, cc), ci, 0, keepdims=False).astype(dtype)

    g_meta, g_wg = _gather8_call("gather_small_weights", [my_half(meta_tokens, F32), my_half(w_gate_up[0], F32)])
    branch_parts = [my_half(w_branch_ret[0], BF16), my_half(w_branch_gla[0], BF16), my_half(w_out[0], BF16)]
    meta = g_meta.reshape(4, 2, N_META // 2, D_MODEL // 4).transpose(1, 2, 0, 3).reshape(N_META, D_MODEL)
    wg_full = g_wg.reshape(4, 2, GATE_RANK // 2, GLA_KW // 4).transpose(1, 2, 0, 3).reshape(GATE_RANK, GLA_KW)

    loc = _device_step(x[0], loss_target[0], meta, norm_gain, my_half(w_in[0], BF16), wg_full, b_gate, ret_norm_gain,
                       gla_norm_gain,
                       branch_parts, final_norm_gain, ck)
    names = ("w_in", "w_branch_ret", "w_branch_gla", "w_out")
    full = [loc[nm] for nm in names]
    big_w = dict(w_in=w_in[0], w_branch_ret=w_branch_ret[0], w_branch_gla=w_branch_gla[0], w_out=w_out[0])
    big_m = dict(w_in=m_w_in[0], w_branch_ret=m_w_branch_ret[0], w_branch_gla=m_w_branch_gla[0], w_out=m_w_out[0])
    big_v = dict(w_in=v_w_in[0], w_branch_ret=v_w_branch_ret[0], w_branch_gla=v_w_branch_gla[0], w_out=v_w_out[0])
    grads, deltas, new_m, new_v = {}, {}, {}, {}
    for nm, f in zip(names, full):
        shape = big_w[nm].shape
        if nm == "w_in":
            f = lax.dynamic_slice_in_dim(f, (sw_in - WIN_STEP) * kme, sw_in, axis=2)
        g = f.reshape(shape)
        if nm == "w_in":
            d, m2, v2 = (a.T for a in _adamw_call("adamw_" + nm, big_w[nm].T, g.T, big_m[nm].T, big_v[nm].T))
        else:
            d, m2, v2 = _adamw_call("adamw_" + nm, big_w[nm], g, big_m[nm], big_v[nm])
        grads[nm], deltas[nm], new_m[nm], new_v[nm] = (a.reshape((1,) + shape) for a in (g, d, m2, v2))

    small_g = dict(loc)
    small_g["meta_tokens"] = loc["dmeta"]
    n_small = sum(sz for _, sz in SMALL)
    rows = -(-n_small // 128 // 8) * 8
    (g_small,) = _gather8_call("gather_small_grads", [_pack_rows([small_g[nm] for nm, _ in SMALL], rows)])
    tot = _sum8_call("sum_small_grads", g_small).reshape(-1)
    off = 0
    sg = {}
    for nm, sz in SMALL:
        sg[nm] = tot[off:off + sz]
        off += sz
    loss = sg.pop("loss")[0]
    sg["w_gate_up"] = lax.dynamic_slice_in_dim(sg["w_gate_up"].reshape(GATE_RANK, GLA_KW), kme * (GLA_KW // 4),
                                               GLA_KW // 4, axis=1)
    sg["meta_tokens"] = lax.dynamic_slice_in_dim(sg["meta_tokens"].reshape(N_META, D_MODEL), kme * (D_MODEL // 4),
                                                 D_MODEL // 4, axis=1)
    small_w = dict(norm_gain=norm_gain, b_gate=b_gate, ret_norm_gain=ret_norm_gain, gla_norm_gain=gla_norm_gain,
                   final_norm_gain=final_norm_gain, w_gate_up=w_gate_up, meta_tokens=meta_tokens)
    small_m = dict(norm_gain=m_norm_gain, b_gate=m_b_gate, ret_norm_gain=m_ret_norm_gain,
                   gla_norm_gain=m_gla_norm_gain, final_norm_gain=m_final_norm_gain, w_gate_up=m_w_gate_up,
                   meta_tokens=m_meta_tokens)
    small_v = dict(norm_gain=v_norm_gain, b_gate=v_b_gate, ret_norm_gain=v_ret_norm_gain,
                   gla_norm_gain=v_gla_norm_gain, final_norm_gain=v_final_norm_gain, w_gate_up=v_w_gate_up,
                   meta_tokens=v_meta_tokens)
    for nm in small_w:
        shape = small_w[nm].shape
        as2d = lambda a: a.reshape((-1, shape[-1]))
        grads[nm] = sg[nm].reshape(shape)
        deltas[nm], new_m[nm], new_v[nm] = (a.reshape(shape) for a in _adamw_call(
            "adamw_" + nm, as2d(small_w[nm]), as2d(sg[nm]), as2d(small_m[nm]), as2d(small_v[nm])))

    out_order = ("meta_tokens", "norm_gain", "w_in", "w_gate_up", "b_gate", "ret_norm_gain", "gla_norm_gain",
                 "w_branch_ret", "w_branch_gla", "w_out", "final_norm_gain")
    dx = loc["dx"].reshape(x.shape)
    return (loss, dx, *[grads[nm] for nm in out_order], *[deltas[nm] for nm in out_order],
            *[new_m[nm] for nm in out_order], *[new_v[nm] for nm in out_order])
```

```python
import math
from typing import Callable, NamedTuple

import numpy as np
import jax
import jax.numpy as jnp
from jax import lax
from jax.experimental import pallas as pl
from jax.experimental.pallas import tpu as pltpu

F32 = jnp.float32
BF16 = jnp.bfloat16

D_MODEL = 1024
N_META = 16
EPS = 1e-6
ROPE_BASE = 10000.0
RET_HEADS, RET_QK, RET_V = 4, 256, 512
RET_W = RET_HEADS * RET_V
GLA_HEADS, GLA_K, GLA_V = 4, 128, 256
GLA_W = GLA_HEADS * GLA_V
GLA_KW = GLA_HEADS * GLA_K
GATE_RANK = 16
GATE_TAU = 16.0
GLA_SUB = 16

TM = 256
T0 = TM
PADF = T0 - N_META
GC = 128
GS = 3
TB = 768
TK = 768

W_R = 6144
W_G = 3088
W_GP = 3200
W_M = 2048
IN_COLS = W_R + W_G + W_M
WIN_STEP = (IN_COLS // 4) // 128 * 128
WIN_W = -(-(3 * (IN_COLS // 4 - WIN_STEP) + IN_COLS // 4) // 128) * 128
IN_PAD = 3 * WIN_STEP + WIN_W

ADAM_LR, ADAM_B1, ADAM_B2, ADAM_EPS, ADAM_WD, ADAM_STEP = 0.001, 0.9, 0.999, 1e-08, 0.01, 10

VMEM_LIMIT = 56 * 1024 * 1024

NN = ((1,), (0,))
NT = ((1,), (1,))
TN = ((0,), (0,))


def _dot(a, b, dims):
    return lax.dot_general(a, b, (dims, ((), ())), preferred_element_type=F32)


def _cparams(n_axes):
    return pltpu.CompilerParams(dimension_semantics=("arbitrary",) * n_axes, vmem_limit_bytes=VMEM_LIMIT)


def _sigmoid(x):
    return 0.5 * jnp.tanh(0.5 * x) + 0.5


def _silu(x):
    h = 0.5 * x
    return h + h * jnp.tanh(h)


def _head_mean(x):
    return jnp.mean(x, axis=-1, keepdims=True)


def _split3(x):
    hi = x.astype(BF16)
    r1 = x - hi.astype(F32)
    mid = r1.astype(BF16)
    lo = (r1 - mid.astype(F32)).astype(BF16)
    return hi, mid, lo


def _exact_pm(p, x):
    hi, mid, lo = _split3(x)
    return _dot(p, hi, NN) + _dot(p, mid, NN) + _dot(p, lo, NN)


def _rms_call(x2d, head, gain, comm):
    tp = T0 + x2d.shape[0]
    nt = tp // TM
    n_xc = len(comm.srcs)

    def body(x_ref, hd_ref, g_ref, *rest):
        xc_src = rest[:n_xc]
        h_ref, u_ref = rest[n_xc:n_xc + 2]
        xc_dst = rest[n_xc + 2:2 * n_xc + 2]
        i = pl.program_id(0)
        begin, finish = comm.make(xc_src, xc_dst, rest[-2], rest[-1])
        pl.when(i == 0)(begin)
        h = jnp.where(i == 0, hd_ref[...], x_ref[...])
        h_ref[...] = h
        r = lax.rsqrt(jnp.mean(h * h, axis=-1, keepdims=True) + EPS)
        u_ref[...] = (h * r * g_ref[...]).astype(BF16)
        pl.when(i == nt - 1)(finish)

    tile = pl.BlockSpec((TM, D_MODEL), lambda i: (i, 0))
    return pl.pallas_call(
        body, name="rms_in", grid=(nt,),
        in_specs=[pl.BlockSpec((TM, D_MODEL), lambda i: (jnp.maximum(i - 1, 0), 0)),
                  pl.BlockSpec((T0, D_MODEL), lambda i: (0, 0)), pl.BlockSpec((1, D_MODEL), lambda i: (0, 0))]
        + [ANY] * n_xc,
        out_specs=[tile, tile] + [ANY] * n_xc,
        out_shape=[jax.ShapeDtypeStruct((tp, D_MODEL), F32), jax.ShapeDtypeStruct((tp, D_MODEL), BF16)]
        + list(comm.out_shapes),
        scratch_shapes=_comm_sems(comm), compiler_params=_cparams(1),
    )(x2d, head, gain, *comm.srcs)


PROJ_ROWS_MAX = 1408


def _proj_rows(m):
    return max(r for r in range(16, PROJ_ROWS_MAX + 1, 16) if m % r == 0)


def _mm_nn(name, a, b, out_dtype, tn, col0, ncols, epilogue=None, extras=(), extra_specs=()):
    m, k = a.shape
    nj, j0 = ncols // tn, col0 // tn
    tb = _proj_rows(m)

    def body(a_ref, b_ref, *rest):
        *ex, o_ref = rest
        acc = _dot(a_ref[...], b_ref[...], NN)
        if epilogue is None:
            o_ref[...] = acc.astype(out_dtype)
        else:
            epilogue(acc, o_ref, *ex)

    return pl.pallas_call(
        body, name=name, grid=(nj, m // tb),
        in_specs=[pl.BlockSpec((tb, k), lambda j, i: (i, 0)), pl.BlockSpec((k, tn), lambda j, i: (0, j0 + j))]
        + list(extra_specs),
        out_specs=pl.BlockSpec((tb, tn), lambda j, i: (i, j)),
        out_shape=jax.ShapeDtypeStruct((m, ncols), out_dtype),
        compiler_params=_cparams(2),
    )(a, b, *extras)


def _rope_tables(tp):
    half = RET_QK // 2
    pos = np.arange(tp, dtype=np.float32) - np.float32(PADF)
    inv = (ROPE_BASE ** (-np.arange(half, dtype=np.float64) / half)).astype(np.float32)
    ang = (pos[:, None] * inv[None, :]).astype(np.float64)
    return np.cos(ang).astype(np.float32), np.sin(ang).astype(np.float32)


def _rope_epilogue(acc, o_ref, cos_ref, sin_ref):
    scale = jnp.where(pl.program_id(0) == 1, RET_QK ** -0.5, 1.0).astype(F32)
    cos, sin = cos_ref[...], sin_ref[...]
    half = RET_QK // 2
    for h in range(RET_HEADS):
        t1 = acc[:, h * RET_QK:h * RET_QK + half]
        t2 = acc[:, h * RET_QK + half:(h + 1) * RET_QK]
        o_ref[:, h * RET_QK:h * RET_QK + half] = ((t1 * cos - t2 * sin) * scale).astype(BF16)
        o_ref[:, h * RET_QK + half:(h + 1) * RET_QK] = ((t2 * cos + t1 * sin) * scale).astype(BF16)


def _gqk_epilogue(acc, o_ref):
    o_ref[:, :GLA_KW] = acc[:, :GLA_KW] * (GLA_K ** -0.5)
    o_ref[:, GLA_KW:] = acc[:, GLA_KW:]


class _Comm(NamedTuple):
    srcs: tuple
    out_shapes: tuple
    n_sems: int
    make: Callable


def _comm_sems(comm):
    return [pltpu.SemaphoreType.DMA((comm.n_sems,)), pltpu.SemaphoreType.DMA((comm.n_sems,))]


def _start_wait(copies):
    def begin():
        for cp in copies:
            cp.start()

    def finish():
        for cp in copies:
            cp.wait()

    return begin, finish


def _other_chips(x, y):
    return [(1 - x, y), (x, 1 - y), (1 - x, 1 - y)]


def _gather_plan(parts, relay=()):
    n = len(parts)
    relay = tuple(relay) + (False,) * (n - len(relay))

    def make(x_refs, out_refs, send_sems, recv_sems):
        x, y, c = _place()
        me, sibling = (x, y, c), (x, y, 1 - c)
        xn, yn, dg = (1 - x, y), (x, 1 - y), (1 - x, 1 - y)

        def slot(t, px, py, pc, half=None):
            ref = out_refs[t].at[4 * px + 2 * py + pc]
            if half is None:
                return ref
            rows = ref.shape[0] // 2
            return ref.at[pl.ds(half * rows, rows)]

        def copy(t, k, dst, to, src=None):
            return pltpu.make_async_remote_copy(
                src_ref=dst if src is None else src, dst_ref=dst, send_sem=send_sems.at[8 * t + k],
                recv_sem=recv_sems.at[8 * t + k], device_id=to, device_id_type=MESH)

        mine = [pltpu.make_async_copy(x_refs[t], slot(t, *me), send_sems.at[8 * n + t]) for t in range(n)]
        sent = []
        for t in range(n):
            sent.append(copy(t, 0, slot(t, *me), sibling, src=x_refs[t]))
            sent.append(copy(t, 1, slot(t, *me), (*xn, c), src=x_refs[t]))
            sent.append(copy(t, 2, slot(t, *me), (*yn, c), src=x_refs[t]))
            if not relay[t]:
                sent.append(copy(t, 3, slot(t, *me), (*dg, c), src=x_refs[t]))

        def begin():
            for cp in mine + sent:
                cp.start()

        def finish():
            later = []

            def start(cp):
                cp.start()
                later.append(cp)

            for t in range(n):
                copy(t, 2, slot(t, *yn, c), me).wait_recv()
                if relay[t]:
                    start(copy(t, 3, slot(t, *yn, c, half=0), (*xn, c)))
                start(copy(t, 6, slot(t, *yn, c), sibling))
            for t in range(n):
                copy(t, 1, slot(t, *xn, c), me).wait_recv()
                if relay[t]:
                    start(copy(t, 4, slot(t, *xn, c, half=1), (*yn, c)))
                start(copy(t, 5, slot(t, *xn, c), sibling))
            for t in range(n):
                if relay[t]:
                    copy(t, 3, slot(t, *dg, c, half=0), me).wait_recv()
                    copy(t, 4, slot(t, *dg, c, half=1), me).wait_recv()
                else:
                    copy(t, 3, slot(t, *dg, c), me).wait_recv()
                start(copy(t, 7, slot(t, *dg, c), sibling))
            for t in range(n):
                copy(t, 0, slot(t, *sibling), me).wait_recv()
                copy(t, 5, slot(t, *xn, 1 - c), me).wait_recv()
                copy(t, 6, slot(t, *yn, 1 - c), me).wait_recv()
                copy(t, 7, slot(t, *dg, 1 - c), me).wait_recv()
            for cp in sent + later:
                cp.wait_send()
            for cp in mine:
                cp.wait()

        return begin, finish

    return _Comm(tuple(parts), tuple(jax.ShapeDtypeStruct((8,) + p.shape, p.dtype) for p in parts), 9 * n, make)


def _exchange_plan(ss):
    def make(s_refs, b_refs, send_sems, recv_sems):
        x, y, c = _place()
        return _start_wait([pltpu.make_async_remote_copy(
            src_ref=s_refs[t].at[2 * chip[0] + chip[1]], dst_ref=b_refs[t].at[j], send_sem=send_sems.at[3 * t + j],
            recv_sem=recv_sems.at[3 * t + j], device_id=(*chip, c), device_id_type=MESH)
            for t in range(len(s_refs)) for j, chip in enumerate(_other_chips(x, y))])

    return _Comm(tuple(ss), tuple(jax.ShapeDtypeStruct((3,) + s.shape[1:], s.dtype) for s in ss), 3 * len(ss), make)


def _exchange_window_plan(s):
    def make(s_refs, b_refs, send_sems, recv_sems):
        x, y, c = _place()
        return _start_wait([pltpu.make_async_remote_copy(
            src_ref=s_refs[0].at[:, pl.ds(pl.multiple_of((2 * chip[0] + chip[1]) * WIN_STEP, 128), WIN_W)],
            dst_ref=b_refs[0].at[j], send_sem=send_sems.at[j], recv_sem=recv_sems.at[j], device_id=(*chip, c),
            device_id_type=MESH) for j, chip in enumerate(_other_chips(x, y))])

    return _Comm((s,), (jax.ShapeDtypeStruct((3, s.shape[0], WIN_W), s.dtype),), 3, make)


def _swap_plan(gs):
    def make(g_refs, b_refs, send_sems, recv_sems):
        x, y, c = _place()
        return _start_wait([pltpu.make_async_remote_copy(
            src_ref=g_refs[t].at[1 - c], dst_ref=b_refs[t], send_sem=send_sems.at[t], recv_sem=recv_sems.at[t],
            device_id=(x, y, 1 - c), device_id_type=MESH) for t in range(len(g_refs))])

    return _Comm(tuple(gs), tuple(jax.ShapeDtypeStruct(g.shape[1:], g.dtype) for g in gs), len(gs), make)


def _spread_plan(parts):
    def make(p_refs, o_refs, send_sems, recv_sems):
        x, y, c = _place()
        copies = []
        for t in range(len(p_refs)):
            mine = o_refs[t].at[4 * x + 2 * y + c]
            copies.append(pltpu.make_async_copy(p_refs[t], mine, send_sems.at[7 * len(p_refs) + t]))
            for r in range(1, 8):
                peer = (1 - x if r & 4 else x, 1 - y if r & 2 else y, 1 - c if r & 1 else c)
                copies.append(pltpu.make_async_remote_copy(
                    src_ref=p_refs[t], dst_ref=mine, send_sem=send_sems.at[7 * t + r - 1],
                    recv_sem=recv_sems.at[7 * t + r - 1], device_id=peer, device_id_type=MESH))
        return _start_wait(copies)

    return _Comm(tuple(parts), tuple(jax.ShapeDtypeStruct((8,) + p.shape, p.dtype) for p in parts), 8 * len(parts),
                 make)


def _mm_nt_acc(name, a, w, tk, acc_in=None, epilogue=None, extras=(), extra_specs=(), extra_out_shapes=(),
               extra_out_specs=(), extra_scratch=(), comm=None, tb=TB):
    m, k = a.shape
    n = w.shape[0]
    nk, ni = k // tk, m // tb
    has_acc = acc_in is not None
    n_xc = len(comm.srcs) if comm else 0
    n_es = len(extra_scratch)

    def body(*refs):
        a_ref, w_ref = refs[0], refs[1]
        pos = 2
        acc_ref = None
        if has_acc:
            acc_ref = refs[pos]
            pos += 1
        ex = refs[pos:pos + len(extras)]
        pos += len(extras)
        xc_src = refs[pos:pos + n_xc]
        pos += n_xc
        n_scr = 1 + n_es + (2 if n_xc else 0)
        outs = refs[pos:len(refs) - n_scr - n_xc]
        xc_dst = refs[len(refs) - n_scr - n_xc:len(refs) - n_scr]
        scr = refs[len(refs) - n_scr]
        es = refs[len(refs) - n_scr + 1:len(refs) - n_scr + 1 + n_es]
        i, kk = pl.program_id(0), pl.program_id(1)
        if n_xc:
            begin, finish = comm.make(xc_src, xc_dst, refs[-2], refs[-1])
            pl.when((i == 0) & (kk == 0))(begin)

        @pl.when(kk == 0)
        def _():
            scr[...] = acc_ref[...] if has_acc else jnp.zeros_like(scr)

        scr[...] += _dot(a_ref[...], w_ref[...], NT)

        @pl.when(kk == nk - 1)
        def _():
            if epilogue is None:
                outs[0][...] = scr[...]
            else:
                epilogue(scr[...], outs, i, ni, *ex, *es)

        if n_xc:
            pl.when((i == ni - 1) & (kk == nk - 1))(finish)

    in_specs = [pl.BlockSpec((tb, tk), lambda i, kk: (i, kk)), pl.BlockSpec((n, tk), lambda i, kk: (0, kk))]
    args = [a, w]
    if has_acc:
        in_specs.append(pl.BlockSpec((tb, n), lambda i, kk: (i, 0)))
        args.append(acc_in)
    in_specs += list(extra_specs) + [ANY] * n_xc
    args += list(extras) + (list(comm.srcs) if comm else [])
    if epilogue is None:
        out_shape = [jax.ShapeDtypeStruct((m, n), F32)]
        out_specs = [pl.BlockSpec((tb, n), lambda i, kk: (i, 0))]
    else:
        out_shape, out_specs = list(extra_out_shapes), list(extra_out_specs)
    scratch = [pltpu.VMEM((tb, n), F32)] + list(extra_scratch)
    if n_xc:
        out_shape += list(comm.out_shapes)
        out_specs += [ANY] * n_xc
        scratch += _comm_sems(comm)
    return pl.pallas_call(
        body, name=name, grid=(ni, nk), in_specs=in_specs, out_specs=out_specs, out_shape=out_shape,
        scratch_shapes=scratch, compiler_params=_cparams(2),
    )(*args)


def _rms_bwd_epilogue(du, outs, i, ni, h_ref, g_ref, dh1_ref, obuf, sems):
    dx_ref, dmeta_ref, dg_ref = outs
    h = h_ref[...]
    r = lax.rsqrt(jnp.mean(h * h, axis=-1, keepdims=True) + EPS)
    xh = h * r
    dxh = du * g_ref[...]
    dh0 = dh1_ref[...] + r * (dxh - xh * jnp.mean(dxh * xh, axis=-1, keepdims=True))

    def put(slot, tile):
        return pltpu.make_async_copy(obuf.at[slot], dx_ref.at[pl.ds(pl.multiple_of(tile * TB - T0, 8), TB)],
                                     sems.at[slot])

    @pl.when(i == 0)
    def _():
        dg_ref[...] = jnp.zeros_like(dg_ref)
        dmeta_ref[...] = dh0[PADF:T0, :]
        obuf[0] = dh0
        first = pltpu.make_async_copy(obuf.at[0, pl.ds(T0, TB - T0)], dx_ref.at[pl.ds(0, TB - T0)], sems.at[0])
        first.start()
        first.wait()

    @pl.when(i >= 1)
    def _():
        slot = i % 2

        @pl.when(i >= 3)
        def _():
            put(slot, i - 2).wait()

        obuf[slot] = dh0
        put(slot, i).start()

    dg_ref[...] += jnp.sum(du * xh, axis=0, keepdims=True)

    @pl.when(i == ni - 1)
    def _():
        for tile in (ni - 2, ni - 1):
            if tile >= 1:
                put(tile % 2, tile).wait()


def _mm_tn(name, a, b, bn, ncols=None, bcol0=0, into=None, col0=0, out_cols=None):
    t, m = a.shape
    n = ncols or b.shape[1]
    j0, bj0 = col0 // bn, bcol0 // bn

    def body(a_ref, b_ref, *rest):
        o_ref = rest[-1]

        @pl.when(pl.program_id(1) == 0)
        def _():
            o_ref[...] = jnp.zeros_like(o_ref)

        o_ref[...] += _dot(a_ref[...], b_ref[...], TN)

    in_specs = [pl.BlockSpec((TK, m), lambda j, kk: (kk, 0)), pl.BlockSpec((TK, bn), lambda j, kk: (kk, bj0 + j))]
    args = [a, b]
    aliases = {}
    if into is not None:
        in_specs.append(ANY)
        args.append(into)
        aliases = {2: 0}
        out_cols = into.shape[1]
    return pl.pallas_call(
        body, name=name, grid=(n // bn, t // TK), in_specs=in_specs,
        out_specs=pl.BlockSpec((m, bn), lambda j, kk: (0, j0 + j)),
        out_shape=jax.ShapeDtypeStruct((m, out_cols or n), F32), input_output_aliases=aliases,
        compiler_params=_cparams(2),
    )(*args)


def _place_merge_cols_call(dwp, dw_m):
    c0 = W_R + W_GP - 128
    tail = IN_PAD - c0
    rows = 256

    def body(m_ref, p_ref, o_ref, buf, low, sem):
        get = pltpu.make_async_copy(o_ref.at[:, pl.ds(c0, 128)], low, sem)
        get.start()
        get.wait()
        for r in range(0, D_MODEL, rows):
            buf[r:r + rows, :] = jnp.concatenate(
                [low[r:r + rows, :GATE_RANK], m_ref[r:r + rows, :],
                 jnp.zeros((rows, tail - GATE_RANK - W_M), F32)], axis=1)
        put = pltpu.make_async_copy(buf, o_ref.at[:, pl.ds(c0, tail)], sem)
        put.start()
        put.wait()

    return pl.pallas_call(
        body, name="place_merge_cols",
        in_specs=[pl.BlockSpec(memory_space=pltpu.VMEM), ANY], out_specs=ANY,
        out_shape=jax.ShapeDtypeStruct(dwp.shape, F32), input_output_aliases={1: 0},
        scratch_shapes=[pltpu.VMEM((D_MODEL, tail), F32), pltpu.VMEM((D_MODEL, 128), F32), pltpu.SemaphoreType.DMA],
        compiler_params=pltpu.CompilerParams(vmem_limit_bytes=VMEM_LIMIT),
    )(dw_m, dwp)


def _ret_fill_decay(lg_ref, dm_scr):
    c = TM
    ii = lax.broadcasted_iota(jnp.int32, (c, c), 0)
    jj = lax.broadcasted_iota(jnp.int32, (c, c), 1)
    rel = (ii - jj).astype(F32)
    for h in range(RET_HEADS):
        dm_scr[h] = jnp.where(rel >= 0, jnp.exp(jnp.maximum(rel, 0.0) * lg_ref[h]), 0.0)


def _ret_consts(lg, dm_ref):
    c = TM
    idx = lax.broadcasted_iota(jnp.int32, (c, 1), 0).astype(F32)
    xi = jnp.exp((idx + 1.0) * lg)
    zeta = jnp.exp((c - 1.0 - idx) * lg)
    gc = jnp.exp(jnp.full((1, 1), c, F32) * lg)
    return dm_ref[...], xi, zeta, gc


def _ret_fwd_call(rqk, rv, rg, gain, lgam):
    tp = rqk.shape[0]
    nc = tp // TM

    def body(lg_ref, qk_ref, v_ref, rg_ref, g_ref, o_ref, a_ref, st_ref, sc_ref, s_scr, dm_scr):
        @pl.when(pl.program_id(0) == 0)
        def _():
            s_scr[...] = jnp.zeros_like(s_scr)
            _ret_fill_decay(lg_ref, dm_scr)

        for h in range(RET_HEADS):
            dm, xi, zeta, gc = _ret_consts(lg_ref[h], dm_scr.at[h])
            q = qk_ref[:, h * RET_QK:(h + 1) * RET_QK]
            k = qk_ref[:, D_MODEL + h * RET_QK:D_MODEL + (h + 1) * RET_QK]
            v = v_ref[:, h * RET_V:(h + 1) * RET_V]
            sb = s_scr[h].astype(BF16)
            st_ref[0, h] = sb
            s = (_dot(q, k, NT) * dm).astype(BF16)
            sc_ref[0, h] = s
            o = _dot(s, v, NN) + xi * _dot(q, sb, NN)
            kz = (k.astype(F32) * zeta).astype(BF16)
            s_scr[h] = gc * s_scr[h] + _dot(kz, v, TN)
            o_ref[:, h * RET_V:(h + 1) * RET_V] = o
            mu = _head_mean(o)
            xc = o - mu
            xh = xc * lax.rsqrt(_head_mean(xc * xc) + EPS)
            a_ref[:, h * RET_V:(h + 1) * RET_V] = (
                xh * g_ref[:, h * RET_V:(h + 1) * RET_V] * _silu(rg_ref[:, h * RET_V:(h + 1) * RET_V])).astype(BF16)

    return pl.pallas_call(
        body, name="ret_fwd", grid=(nc,),
        in_specs=[pl.BlockSpec(memory_space=pltpu.SMEM),
                  pl.BlockSpec((TM, 2 * D_MODEL), lambda n: (n, 0)),
                  pl.BlockSpec((TM, RET_W), lambda n: (n, 0)),
                  pl.BlockSpec((TM, RET_W), lambda n: (n, 0)),
                  pl.BlockSpec((1, RET_W), lambda n: (0, 0))],
        out_specs=[pl.BlockSpec((TM, RET_W), lambda n: (n, 0)),
                   pl.BlockSpec((TM, RET_W), lambda n: (n, 0)),
                   pl.BlockSpec((1, RET_HEADS, RET_QK, RET_V), lambda n: (n, 0, 0, 0)),
                   pl.BlockSpec((1, RET_HEADS, TM, TM), lambda n: (n, 0, 0, 0))],
        out_shape=[jax.ShapeDtypeStruct((tp, RET_W), F32), jax.ShapeDtypeStruct((tp, RET_W), BF16),
                   jax.ShapeDtypeStruct((nc, RET_HEADS, RET_QK, RET_V), BF16),
                   jax.ShapeDtypeStruct((nc, RET_HEADS, TM, TM), BF16)],
        scratch_shapes=[pltpu.VMEM((RET_HEADS, RET_QK, RET_V), F32), pltpu.VMEM((RET_HEADS, TM, TM), F32)],
        compiler_params=_cparams(1),
    )(lgam, rqk, rv, rg, gain)


def _ret_bwd_call(rqk, rv, rg, o_ret, dpr, wbr, states, scores, gain, lgam, cos, sin):
    tp = rqk.shape[0]
    nc = tp // TM
    half = RET_QK // 2

    def body(lg_ref, qk_ref, v_ref, rg_ref, o_ref, dpr_ref, wbr_ref, st_ref, sc_ref, g_ref, cos_ref, sin_ref, dp_ref,
             dg_ref, ds_scr, dm_scr):
        @pl.when(pl.program_id(0) == 0)
        def _():
            ds_scr[...] = jnp.zeros_like(ds_scr)
            dg_ref[...] = jnp.zeros_like(dg_ref)
            _ret_fill_decay(lg_ref, dm_scr)

        cos, sin = cos_ref[...], sin_ref[...]
        for h in range(RET_HEADS):
            hs = slice(h * RET_V, (h + 1) * RET_V)
            dm, xi, zeta, gc = _ret_consts(lg_ref[h], dm_scr.at[h])
            o = o_ref[:, hs]
            mu = _head_mean(o)
            xc = o - mu
            rstd = lax.rsqrt(_head_mean(xc * xc) + EPS)
            xh = xc * rstd
            gain_h = g_ref[:, hs]
            g = rg_ref[:, hs]
            sg = _sigmoid(g)
            silu = g * sg
            dah = _dot(dpr_ref[...], wbr_ref[hs, :], NT)
            dp_ref[:, 4 * D_MODEL + h * RET_V:4 * D_MODEL + (h + 1) * RET_V] = (
                dah * (xh * gain_h) * (sg * (1.0 + g * (1.0 - sg)))).astype(BF16)
            dn = dah * silu
            dg_ref[:, hs] += jnp.sum(dn * xh, axis=0, keepdims=True)
            dxh = dn * gain_h
            do = rstd * (dxh - _head_mean(dxh) - xh * _head_mean(dxh * xh))
            dob = do.astype(BF16)
            q = qk_ref[:, h * RET_QK:(h + 1) * RET_QK]
            k = qk_ref[:, D_MODEL + h * RET_QK:D_MODEL + (h + 1) * RET_QK]
            v = v_ref[:, hs]
            sp = st_ref[0, h]
            ds = ds_scr[h]
            dsb = ds.astype(BF16)
            s = sc_ref[0, h]
            dsc = (_dot(dob, v, NT) * dm).astype(BF16)
            dq = _dot(dsc, k, NN) + xi * _dot(dob, sp, NT)
            dk = _dot(dsc, q, TN) + zeta * _dot(v, dsb, NT)
            kz = (k.astype(F32) * zeta).astype(BF16)
            dv = _dot(s, dob, TN) + _dot(kz, dsb, NN)
            qx = (q.astype(F32) * xi).astype(BF16)
            ds_scr[h] = gc * ds + _dot(qx, dob, TN)
            dp_ref[:, 2 * D_MODEL + h * RET_V:2 * D_MODEL + (h + 1) * RET_V] = dv.astype(BF16)
            dk = dk * (RET_QK ** -0.5)
            for base, t in ((0, dq), (D_MODEL, dk)):
                t1, t2 = t[:, :half], t[:, half:]
                dp_ref[:, base + h * RET_QK:base + h * RET_QK + half] = (t1 * cos + t2 * sin).astype(BF16)
                dp_ref[:, base + h * RET_QK + half:base + (h + 1) * RET_QK] = (t2 * cos - t1 * sin).astype(BF16)

    rev = lambda n: (nc - 1 - n, 0)
    return pl.pallas_call(
        body, name="ret_bwd", grid=(nc,),
        in_specs=[pl.BlockSpec(memory_space=pltpu.SMEM),
                  pl.BlockSpec((TM, 2 * D_MODEL), rev),
                  pl.BlockSpec((TM, RET_W), rev),
                  pl.BlockSpec((TM, RET_W), rev),
                  pl.BlockSpec((TM, RET_W), rev),
                  pl.BlockSpec((TM, D_MODEL), rev),
                  pl.BlockSpec((RET_W, D_MODEL), lambda n: (0, 0)),
                  pl.BlockSpec((1, RET_HEADS, RET_QK, RET_V), lambda n: (nc - 1 - n, 0, 0, 0)),
                  pl.BlockSpec((1, RET_HEADS, TM, TM), lambda n: (nc - 1 - n, 0, 0, 0)),
                  pl.BlockSpec((1, RET_W), lambda n: (0, 0)),
                  pl.BlockSpec((TM, half), rev),
                  pl.BlockSpec((TM, half), rev)],
        out_specs=[pl.BlockSpec((TM, W_R), rev), pl.BlockSpec((1, RET_W), lambda n: (0, 0))],
        out_shape=[jax.ShapeDtypeStruct((tp, W_R), BF16), jax.ShapeDtypeStruct((1, RET_W), F32)],
        scratch_shapes=[pltpu.VMEM((RET_HEADS, RET_QK, RET_V), F32), pltpu.VMEM((RET_HEADS, TM, TM), F32)],
        compiler_params=_cparams(1),
    )(lgam, rqk, rv, rg, o_ret, dpr, wbr, states, scores, gain, cos, sin)


GLA_LEVELS = tuple(GC >> (s + 1) for s in range(int(math.log2(GC // GLA_SUB))))
NLEV = len(GLA_LEVELS)


def _gla_tril():
    return np.tril(np.ones((GC, GC), np.float32))


def _gla_masks():
    ii = lax.broadcasted_iota(jnp.int32, (GC, GC), 0)
    jj = lax.broadcasted_iota(jnp.int32, (GC, GC), 1)
    masks = []
    for m in GLA_LEVELS:
        sh = int(math.log2(2 * m))
        masks.append(((ii >> sh) == (jj >> sh)) & ((ii & m) != 0) & ((jj & m) == 0))
    sh = int(math.log2(GLA_SUB))
    md = ((ii >> sh) == (jj >> sh)) & (jj <= ii)
    row = lax.broadcasted_iota(jnp.int32, (GC, 1), 0)
    second = [(row & m) != 0 for m in GLA_LEVELS]
    return masks, md, second


def _gla_gate_call(glr, wg, bg, pmat):
    tp = glr.shape[0]
    gb = _proj_rows(tp)
    assert gb % GC == 0

    def body(glr_ref, wg_ref, bg_ref, p_ref, z_ref, b_ref):
        z = _dot(glr_ref[...].astype(BF16), wg_ref[...], NN) + bg_ref[...]
        z_ref[...] = z
        la = (jnp.minimum(z, 0.0) - jnp.log1p(jnp.exp(-jnp.abs(z)))) * (1.0 / GATE_TAU)
        for r in range(0, gb, GC):
            b_ref[r:r + GC, :] = _exact_pm(p_ref[...], la[r:r + GC, :])

    tile = pl.BlockSpec((gb, GLA_KW), lambda i: (i, 0))
    return pl.pallas_call(
        body, name="gla_gate", grid=(tp // gb,),
        in_specs=[pl.BlockSpec((gb, 128), lambda i: (i, 0)), pl.BlockSpec((128, GLA_KW), lambda i: (0, 0)),
                  pl.BlockSpec((1, GLA_KW), lambda i: (0, 0)), pl.BlockSpec((GC, GC), lambda i: (0, 0))],
        out_specs=[tile, tile],
        out_shape=[jax.ShapeDtypeStruct((tp, GLA_KW), F32), jax.ShapeDtypeStruct((tp, GLA_KW), F32)],
        compiler_params=_cparams(1),
    )(glr, wg, bg, pmat)


def _gla_gate_bwd_call(db, z, glr, wg, pmat_t, d_g):
    tp = db.shape[0]
    gb = _proj_rows(tp)
    assert gb % GC == 0 and (W_GP - 128) % 128 == 0

    def body(db_ref, z_ref, glr_ref, wg_ref, pt_ref, dgin_ref, dg_ref, dwg_ref, dbg_ref):
        i = pl.program_id(0)

        @pl.when(i == 0)
        def _():
            dwg_ref[...] = jnp.zeros_like(dwg_ref)
            dbg_ref[...] = jnp.zeros_like(dbg_ref)

        dla = jnp.concatenate([_exact_pm(pt_ref[...], db_ref[r:r + GC, :]) for r in range(0, gb, GC)], axis=0)
        row = i * gb + lax.broadcasted_iota(jnp.int32, (gb, 1), 0)
        dz = jnp.where(row >= PADF, dla * (1.0 / GATE_TAU) * _sigmoid(-z_ref[...]), 0.0)
        dzb = dz.astype(BF16)
        dg_ref[...] = _dot(dzb, wg_ref[...], NT).astype(BF16)
        dwg_ref[...] += _dot(glr_ref[...].astype(BF16), dzb, TN)
        dbg_ref[...] += jnp.sum(dz, axis=0, keepdims=True)

    tile = pl.BlockSpec((gb, GLA_KW), lambda i: (i, 0))
    const = lambda i: (0, 0)
    return pl.pallas_call(
        body, name="gla_gate_bwd", grid=(tp // gb,),
        in_specs=[tile, tile, pl.BlockSpec((gb, 128), lambda i: (i, 0)), pl.BlockSpec((128, GLA_KW), const),
                  pl.BlockSpec((GC, GC), const), ANY],
        out_specs=[pl.BlockSpec((gb, 128), lambda i: (i, (W_GP - 128) // 128)), pl.BlockSpec((128, GLA_KW), const),
                   pl.BlockSpec((1, GLA_KW), const)],
        out_shape=[jax.ShapeDtypeStruct(d_g.shape, BF16), jax.ShapeDtypeStruct((128, GLA_KW), F32),
                   jax.ShapeDtypeStruct((1, GLA_KW), F32)],
        input_output_aliases={5: 0}, compiler_params=_cparams(1),
    )(db, z, glr, wg, pmat_t, d_g)


def _gla_row_steps(b_ref, cs, rows, size):
    parts = [jnp.zeros((size, GLA_K), F32) if r is None else jnp.broadcast_to(b_ref[r:r + 1, cs], (size, GLA_K))
             for r in rows]
    return parts[0] if len(parts) == 1 else jnp.concatenate(parts, axis=0)


def _gla_factors(b_ref, h, second):
    cs = slice(h * GLA_K, (h + 1) * GLA_K)
    b = b_ref[:, cs]
    fq, fk = [], []
    for l, m in enumerate(GLA_LEVELS):
        d = b - _gla_row_steps(b_ref, cs, [s + m - 1 for s in range(0, GC, 2 * m)], 2 * m)
        f = jnp.exp(jnp.where(second[l], d, -d))
        fq.append(jnp.where(second[l], f, 0.0))
        fk.append(jnp.where(second[l], 0.0, f))
    dd = b - _gla_row_steps(b_ref, cs, [None] + [s - 1 for s in range(GLA_SUB, GC, GLA_SUB)], GLA_SUB)
    ed = jnp.exp(dd)
    edi = jnp.exp(-dd)
    eb = jnp.exp(b)
    bl = b_ref[GC - 1:GC, cs]
    ee = jnp.exp(bl - b)
    ebl = jnp.exp(bl)
    return fq, fk, ed, edi, eb, ee, ebl


def _gla_scaled(q, k, fq, fk, ed, edi):
    qt = [(q * f).astype(BF16) for f in fq]
    kt = [(k * f).astype(BF16) for f in fk]
    return qt, kt, (q * ed).astype(BF16), (k * edi).astype(BF16)


def _gla_scores(qt, kt, qd, kd, masks, md):
    a = jnp.where(md, _dot(qd, kd, NT), 0.0)
    for l in range(NLEV):
        a = a + jnp.where(masks[l], _dot(qt[l], kt[l], NT), 0.0)
    return a.astype(BF16)


def _gla_fwd_call(gqk, gv, b, gg, gain, comm=None):
    tp = gqk.shape[0]
    nc = tp // GC
    ns = nc // GS
    n_xc = len(comm.srcs) if comm else 0

    def body(qk_ref, v_ref, b_ref, gg_ref, g_ref, *rest):
        xc_src = rest[:n_xc]
        o_ref, a_ref, st_ref, am_ref = rest[n_xc:n_xc + 4]
        xc_dst = rest[n_xc + 4:2 * n_xc + 4]
        s_scr = rest[2 * n_xc + 4]
        n = pl.program_id(0)
        if n_xc:
            begin, finish = comm.make(xc_src, xc_dst, rest[-2], rest[-1])
            pl.when(n == 0)(begin)
            pl.when(n == ns - 1)(finish)

        @pl.when(n == 0)
        def _():
            s_scr[...] = jnp.zeros_like(s_scr)

        masks, md, second = _gla_masks()
        for cc in range(GS):
            rows = pl.ds(cc * GC, GC)
            qk_c, v_c, b_c, gg_c, o_c, a_c = (r.at[rows] for r in (qk_ref, v_ref, b_ref, gg_ref, o_ref, a_ref))
            for h in range(GLA_HEADS):
                q = qk_c[:, h * GLA_K:(h + 1) * GLA_K]
                k = qk_c[:, GLA_KW + h * GLA_K:GLA_KW + (h + 1) * GLA_K]
                vs = slice(h * GLA_V, (h + 1) * GLA_V)
                v = v_c[:, vs]
                fq, fk, ed, edi, eb, ee, ebl = _gla_factors(b_c, h, second)
                a = _gla_scores(*_gla_scaled(q, k, fq, fk, ed, edi), masks, md)
                am_ref[cc, h] = a
                sb = s_scr[h].astype(BF16)
                st_ref[cc, h] = sb
                o = _dot(a, v, NN) + _dot((q * eb).astype(BF16), sb, NT)
                s_scr[h] = s_scr[h] * ebl + _dot(v, (k * ee).astype(BF16), TN)
                o_c[:, vs] = o
                xh = o * lax.rsqrt(_head_mean(o * o) + EPS)
                a_c[:, vs] = (xh * g_ref[:, vs] * _silu(gg_c[:, vs])).astype(BF16)

    return pl.pallas_call(
        body, name="gla_fwd", grid=(ns,),
        in_specs=[pl.BlockSpec((GS * GC, 2 * GLA_KW), lambda n: (n, 0)),
                  pl.BlockSpec((GS * GC, GLA_W), lambda n: (n, 0)),
                  pl.BlockSpec((GS * GC, GLA_KW), lambda n: (n, 0)),
                  pl.BlockSpec((GS * GC, GLA_W), lambda n: (n, 0)),
                  pl.BlockSpec((1, GLA_W), lambda n: (0, 0))] + [ANY] * n_xc,
        out_specs=[pl.BlockSpec((GS * GC, GLA_W), lambda n: (n, 0)),
                   pl.BlockSpec((GS * GC, GLA_W), lambda n: (n, 0)),
                   pl.BlockSpec((GS, GLA_HEADS, GLA_V, GLA_K), lambda n: (n, 0, 0, 0)),
                   pl.BlockSpec((GS, GLA_HEADS, GC, GC), lambda n: (n, 0, 0, 0))] + [ANY] * n_xc,
        out_shape=[jax.ShapeDtypeStruct((tp, GLA_W), F32), jax.ShapeDtypeStruct((tp, GLA_W), BF16),
                   jax.ShapeDtypeStruct((nc, GLA_HEADS, GLA_V, GLA_K), BF16),
                   jax.ShapeDtypeStruct((nc, GLA_HEADS, GC, GC), BF16)] + (list(comm.out_shapes) if comm else []),
        scratch_shapes=[pltpu.VMEM((GLA_HEADS, GLA_V, GLA_K), F32)] + (_comm_sems(comm) if comm else []),
        compiler_params=_cparams(1),
    )(gqk, gv, b, gg, gain, *(comm.srcs if comm else ()))


def _gla_bwd_call(gqk, gv, b, gg, o_gla, da, states, scores, gain, comm=None):
    tp = gqk.shape[0]
    nc = tp // GC
    ns = nc // GS
    o_gv, o_gg = 2 * GLA_KW, 2 * GLA_KW + GLA_W
    n_xc = len(comm.srcs) if comm else 0

    def body(qk_all, v_all, b_all, gg_all, o_all, da_all, st_ref, am_ref, g_ref, *rest):
        xc_src = rest[:n_xc]
        dp_all, db_all, dg_ref = rest[n_xc:n_xc + 3]
        xc_dst = rest[n_xc + 3:2 * n_xc + 3]
        ds_scr = rest[2 * n_xc + 3]
        n = pl.program_id(0)
        if n_xc:
            begin, finish = comm.make(xc_src, xc_dst, rest[-2], rest[-1])
            pl.when(n == 0)(begin)
            pl.when(n == ns - 1)(finish)

        @pl.when(n == 0)
        def _():
            ds_scr[...] = jnp.zeros_like(ds_scr)
            dg_ref[...] = jnp.zeros_like(dg_ref)

        masks, md, second = _gla_masks()
        for cc, h in [(cc, h) for cc in reversed(range(GS)) for h in range(GLA_HEADS)]:
            rows = pl.ds(cc * GC, GC)
            qk_ref, v_ref, b_scr, gg_ref, o_ref, da_ref, dp_ref, db_scr = (
                r.at[rows] for r in (qk_all, v_all, b_all, gg_all, o_all, da_all, dp_all, db_all))
            cs = slice(h * GLA_K, (h + 1) * GLA_K)
            vs = slice(h * GLA_V, (h + 1) * GLA_V)
            o = o_ref[:, vs]
            rstd = lax.rsqrt(_head_mean(o * o) + EPS)
            xh = o * rstd
            gain_h = g_ref[:, vs]
            g = gg_ref[:, vs]
            sg = _sigmoid(g)
            dah = da_ref[:, vs]
            dp_ref[:, o_gg + h * GLA_V:o_gg + (h + 1) * GLA_V] = (
                dah * (xh * gain_h) * (sg * (1.0 + g * (1.0 - sg)))).astype(BF16)
            dn = dah * (g * sg)
            dg_ref[:, vs] += jnp.sum(dn * xh, axis=0, keepdims=True)
            dxh = dn * gain_h
            do = rstd * (dxh - xh * _head_mean(dxh * xh))
            dob = do.astype(BF16)
            q = qk_ref[:, cs]
            k = qk_ref[:, GLA_KW + h * GLA_K:GLA_KW + (h + 1) * GLA_K]
            v = v_ref[:, vs]
            fq, fk, ed, edi, eb, ee, ebl = _gla_factors(b_scr, h, second)
            qt, kt, qd, kd = _gla_scaled(q, k, fq, fk, ed, edi)
            sp = st_ref[cc, h]
            ds = ds_scr[h]
            dsb = ds.astype(BF16)
            q_in = q * eb
            k_end = k * ee
            da_s = _dot(dob, v, NT)
            dv = _dot(am_ref[cc, h], dob, TN) + _dot(k_end.astype(BF16), dsb, NT)
            dq_in = _dot(dob, sp, NN)
            dk_end = _dot(v, dsb, NN)
            dbl = jnp.sum(sp.astype(F32) * ds, axis=0, keepdims=True) * ebl
            ds_scr[h] = ds * ebl + _dot(dob, q_in.astype(BF16), TN)
            dq = dq_in * eb
            dk = dk_end * ee
            de_end = dk_end * k_end
            db = dq_in * q_in - de_end
            placed = [(GC - 1, jnp.sum(de_end, axis=0, keepdims=True) + dbl)]
            for l, m in enumerate(GLA_LEVELS):
                dal = jnp.where(masks[l], da_s, 0.0).astype(BF16)
                dqt = _dot(dal, kt[l], NN)
                dkt = _dot(dal, qt[l], TN)
                dq = dq + dqt * fq[l]
                dk = dk + dkt * fk[l]
                gl = dqt * (q * fq[l]) - dkt * (k * fk[l])
                db = db + gl
                placed += [(s + m - 1, -jnp.sum(gl[s:s + 2 * m], axis=0, keepdims=True)) for s in range(0, GC, 2 * m)]
            dad = jnp.where(md, da_s, 0.0).astype(BF16)
            dqd = _dot(dad, kd, NN)
            dkd = _dot(dad, qd, TN)
            dq = dq + dqd * ed
            dk = dk + dkd * edi
            gd = dqd * (q * ed) - dkd * (k * edi)
            db = db + gd
            placed += [(s - 1, -jnp.sum(gd[s:s + GLA_SUB], axis=0, keepdims=True)) for s in range(GLA_SUB, GC, GLA_SUB)]
            db_scr[:, cs] = db
            for r, val in placed:
                db_scr[r:r + 1, cs] += val
            dp_ref[:, cs] = (dq * (GLA_K ** -0.5)).astype(BF16)
            dp_ref[:, GLA_KW + h * GLA_K:GLA_KW + (h + 1) * GLA_K] = dk.astype(BF16)
            dp_ref[:, o_gv + h * GLA_V:o_gv + (h + 1) * GLA_V] = dv.astype(BF16)

    rev = lambda n: (ns - 1 - n, 0)
    const = lambda n: (0, 0)
    xc_shapes, xc_sems = (list(comm.out_shapes), _comm_sems(comm)) if n_xc else ([], [])
    return pl.pallas_call(
        body, name="gla_bwd", grid=(ns,),
        in_specs=[pl.BlockSpec((GS * GC, 2 * GLA_KW), rev),
                  pl.BlockSpec((GS * GC, GLA_W), rev),
                  pl.BlockSpec((GS * GC, GLA_KW), rev),
                  pl.BlockSpec((GS * GC, GLA_W), rev),
                  pl.BlockSpec((GS * GC, GLA_W), rev),
                  pl.BlockSpec((GS * GC, GLA_W), rev),
                  pl.BlockSpec((GS, GLA_HEADS, GLA_V, GLA_K), lambda n: (ns - 1 - n, 0, 0, 0)),
                  pl.BlockSpec((GS, GLA_HEADS, GC, GC), lambda n: (ns - 1 - n, 0, 0, 0)),
                  pl.BlockSpec((1, GLA_W), const)] + [ANY] * n_xc,
        out_specs=[pl.BlockSpec((GS * GC, W_GP), rev), pl.BlockSpec((GS * GC, GLA_KW), rev),
                   pl.BlockSpec((1, GLA_W), const)] + [ANY] * n_xc,
        out_shape=[jax.ShapeDtypeStruct((tp, W_GP), BF16), jax.ShapeDtypeStruct((tp, GLA_KW), F32),
                   jax.ShapeDtypeStruct((1, GLA_W), F32)] + xc_shapes,
        scratch_shapes=[pltpu.VMEM((GLA_HEADS, GLA_V, GLA_K), F32)] + xc_sems,
        compiler_params=_cparams(1),
    )(gqk, gv, b, gg, o_gla, da, states, scores, gain, *(comm.srcs if comm else ()))


def _mid_call(a_ret, a_gla, mg, h0, tgt, wbr, wbg, wout, gf):
    tp = h0.shape[0]
    nt = tp // TM

    def body(ar_ref, ag_ref, mg_ref, h_ref, t_ref, wbr_ref, wbg_ref, wo_ref, gf_ref,
             dh1_ref, dag_ref, dm_ref, mb_ref, dh1b_ref, dprb_ref, dpgb_ref, loss_ref, dgf_ref):
        i = pl.program_id(0)

        @pl.when(i == 0)
        def _():
            loss_ref[...] = jnp.zeros_like(loss_ref)
            dgf_ref[...] = jnp.zeros_like(dgf_ref)

        ar, ag = ar_ref[...], ag_ref[...]
        pr = _dot(ar, wbr_ref[...], NN)
        pg = _dot(ag, wbg_ref[...], NN)
        sr = _sigmoid(mg_ref[:, :D_MODEL])
        sg = _sigmoid(mg_ref[:, D_MODEL:])
        merged = (sr * pr + sg * pg).astype(BF16)
        mb_ref[...] = merged
        h1 = h_ref[...] + _dot(merged, wo_ref[...], NN)
        r1 = lax.rsqrt(jnp.mean(h1 * h1, axis=-1, keepdims=True) + EPS)
        xh = h1 * r1
        gfv = gf_ref[...]
        live = jnp.where(i > 0, 1.0, 0.0).astype(F32)
        err = (xh * gfv - t_ref[...]) * live
        loss_ref[...] += jnp.full(loss_ref.shape, 0.5 / D_MODEL, F32) * jnp.sum(err * err)
        dy = err * (1.0 / D_MODEL)
        dgf_ref[...] += jnp.sum(dy * xh, axis=0, keepdims=True)
        dxh = dy * gfv
        dh1 = r1 * (dxh - xh * jnp.mean(dxh * xh, axis=-1, keepdims=True))
        dh1_ref[...] = dh1
        dh1b = dh1.astype(BF16)
        dh1b_ref[...] = dh1b
        dmerged = _dot(dh1b, wo_ref[...], NT)
        dm_ref[:, :D_MODEL] = (dmerged * pr * sr * (1.0 - sr)).astype(BF16)
        dm_ref[:, D_MODEL:] = (dmerged * pg * sg * (1.0 - sg)).astype(BF16)
        dpr = (dmerged * sr).astype(BF16)
        dpg = (dmerged * sg).astype(BF16)
        dprb_ref[...] = dpr
        dpgb_ref[...] = dpg
        dag_ref[...] = _dot(dpg, wbg_ref[...], NT)

    tile = lambda w: pl.BlockSpec((TM, w), lambda i: (i, 0))
    const = lambda r, w: pl.BlockSpec((r, w), lambda i: (0, 0))
    return pl.pallas_call(
        body, name="merge_out_loss", grid=(nt,),
        in_specs=[tile(RET_W), tile(GLA_W), tile(W_M), tile(D_MODEL),
                  pl.BlockSpec((TM, D_MODEL), lambda i: (jnp.maximum(i - 1, 0), 0)),
                  const(RET_W, D_MODEL), const(GLA_W, D_MODEL), const(D_MODEL, D_MODEL), const(1, D_MODEL)],
        out_specs=[tile(D_MODEL), tile(GLA_W), tile(W_M), tile(D_MODEL), tile(D_MODEL), tile(D_MODEL),
                   tile(D_MODEL), const(1, 128), const(1, D_MODEL)],
        out_shape=[jax.ShapeDtypeStruct((tp, D_MODEL), F32), jax.ShapeDtypeStruct((tp, GLA_W), F32),
                   jax.ShapeDtypeStruct((tp, W_M), BF16),
                   jax.ShapeDtypeStruct((tp, D_MODEL), BF16), jax.ShapeDtypeStruct((tp, D_MODEL), BF16),
                   jax.ShapeDtypeStruct((tp, D_MODEL), BF16), jax.ShapeDtypeStruct((tp, D_MODEL), BF16),
                   jax.ShapeDtypeStruct((1, 128), F32), jax.ShapeDtypeStruct((1, D_MODEL), F32)],
        compiler_params=_cparams(1),
    )(a_ret, a_gla, mg, h0, tgt, wbr, wbg, wout, gf)


def _device_step(x2d, tgt2d, meta, norm_gain, w_in_part, w_gate_up, b_gate, ret_gain, gla_gain, branch_parts,
                 final_gain, ck):
    seq = x2d.shape[0]
    tp = T0 + seq
    head = jnp.concatenate([jnp.zeros((PADF, D_MODEL), F32), meta], axis=0)
    wg_pad = jnp.pad(w_gate_up, ((0, 128 - GATE_RANK), (0, 0))).astype(BF16)

    half = RET_QK // 2
    cos, sin = (jnp.asarray(t) for t in _rope_tables(tp))
    lgam = jnp.log1p(-(2.0 ** (-5.0 - jnp.arange(RET_HEADS, dtype=F32))))
    pmat = jnp.asarray(_gla_tril(), BF16)
    pmat_t = jnp.asarray(_gla_tril().T.copy(), BF16)

    h0, u, g_in = _rms_call(x2d, head, norm_gain, _gather_plan([w_in_part], relay=(True,)))
    hr, sw = w_in_part.shape
    w_in_bf = g_in.reshape(4, 2, hr, sw).transpose(1, 2, 0, 3).reshape(2 * hr, 4 * sw)
    w_r = w_in_bf
    w_g = jnp.pad(w_in_bf[:, W_R:W_R + W_G], ((0, 0), (0, W_GP - W_G)))
    w_m = w_in_bf[:, W_R + W_G:]
    tab = pl.BlockSpec((_proj_rows(tp), half), lambda j, i: (i, 0))
    rqk = _mm_nn("proj_rqk", u, w_r, BF16, D_MODEL, 0, 2 * D_MODEL, _rope_epilogue, (cos, sin), (tab, tab))
    rv = _mm_nn("proj_rv", u, w_r, BF16, RET_W, 2 * D_MODEL, RET_W)
    rg = _mm_nn("proj_rg", u, w_r, F32, RET_W, 4 * D_MODEL, RET_W)
    gqk = _mm_nn("proj_gqk", u, w_g, F32, 2 * GLA_KW, 0, 2 * GLA_KW, _gqk_epilogue)
    gv = _mm_nn("proj_gv", u, w_g, BF16, GLA_W, 2 * GLA_KW, GLA_W)
    gg = _mm_nn("proj_gg", u, w_g, F32, GLA_W, 2 * GLA_KW + GLA_W, GLA_W)
    glr = _mm_nn("proj_glr", u, w_g, F32, 128, 2 * GLA_KW + 2 * GLA_W, 128)
    mg = _mm_nn("proj_mg", u, w_m, F32, W_M, 0, W_M)

    o_ret, a_ret, st_ret, sc_ret = _ret_fwd_call(rqk, rv, rg, ret_gain, lgam)
    z_gate, b_dec = _gla_gate_call(glr, wg_pad, b_gate, pmat)
    o_gla, a_gla, st_gla, sc_gla, g_br, g_bg, g_out = _gla_fwd_call(gqk, gv, b_dec, gg, gla_gain,
                                                                    comm=_spread_plan(branch_parts))
    wbr = g_br.reshape(RET_W, D_MODEL)
    wbg = g_bg.reshape(GLA_W, D_MODEL)
    wout = g_out.reshape(D_MODEL, D_MODEL)

    gf = final_gain.reshape(1, D_MODEL)
    (dh1, da_gla, dm, merged_b, dh1_b, dpr_b, dpg_b, loss, dgf) = _mid_call(
        a_ret, a_gla, mg, h0, tgt2d, wbr, wbg, wout, gf)

    names_b = ("w_branch_ret", "w_branch_gla", "w_out")
    g2_b = [_mm_tn("dw_br", a_ret, dpr_b, D_MODEL).reshape(4, 2, RET_W // 8, D_MODEL).transpose(1, 0, 2, 3),
            _mm_tn("dw_bg", a_gla, dpg_b, D_MODEL).reshape(4, 2, GLA_W // 8, D_MODEL).transpose(1, 0, 2, 3),
            _mm_tn("dw_out", merged_b, dh1_b, D_MODEL).reshape(4, 2, D_MODEL // 8, D_MODEL).transpose(1, 0, 2, 3)]
    sib_b = _swap_halves_call("swap_halves_branch", g2_b)
    sum_b = [_add_half_call("add_half_" + nm, g, b, ck) for nm, g, b in zip(names_b, g2_b, sib_b)]
    d_g, db_dec, dgla_gain, *chips_b = _gla_bwd_call(gqk, gv, b_dec, gg, o_gla, da_gla, st_gla, sc_gla, gla_gain,
                                                     comm=_exchange_plan(sum_b))
    d_g, dwg, dbg = _gla_gate_bwd_call(db_dec, z_gate, glr, wg_pad, pmat_t, d_g)
    mine = [_add_chips_call("add_chips_" + nm, g, b, p, ck) for nm, g, b, p in zip(names_b, g2_b, sib_b, chips_b)]

    d_r, dret_gain = _ret_bwd_call(rqk, rv, rg, o_ret, dpr_b, wbr, st_ret, sc_ret, ret_gain, lgam, cos, sin)

    dwp = _mm_tn("dw_r", u, d_r, 2 * D_MODEL, out_cols=IN_PAD)
    dwp = _mm_tn("dw_g", u, d_g, D_MODEL, ncols=W_GP - 128, into=dwp, col0=W_R)
    dwp = _mm_tn("dw_glr", u, d_g, 128, ncols=128, bcol0=W_GP - 128, into=dwp, col0=W_R + W_GP - 128)
    g2_in = _place_merge_cols_call(dwp, _mm_tn("dw_m", u, dm, 2 * D_MODEL)).reshape(2, D_MODEL // 2, IN_PAD)

    du, sib_in = _mm_nt_acc("du_g", d_g, w_g, W_GP, comm=_swap_plan([g2_in]), tb=_proj_rows(tp))
    sum_in = _add_rows_call("add_half_w_in", g2_in, sib_in, ck)
    du, chips_in = _mm_nt_acc("du_r", d_r, w_r, 2 * D_MODEL, acc_in=du, comm=_exchange_window_plan(sum_in),
                              tb=_proj_rows(tp))
    tile = pl.BlockSpec((TB, D_MODEL), lambda i, kk: (i, 0))
    row = pl.BlockSpec((1, D_MODEL), lambda i, kk: (0, 0))
    dx, dmeta, dnorm_gain = _mm_nt_acc(
        "du_m", dm, w_m, W_M, acc_in=du, epilogue=_rms_bwd_epilogue, extras=(h0, norm_gain, dh1),
        extra_specs=(tile, row, tile),
        extra_out_shapes=(jax.ShapeDtypeStruct((seq, D_MODEL), F32), jax.ShapeDtypeStruct((N_META, D_MODEL), F32),
                          jax.ShapeDtypeStruct((1, D_MODEL), F32)),
        extra_out_specs=(ANY, pl.BlockSpec((N_META, D_MODEL), lambda i, kk: (0, 0)), row),
        extra_scratch=(pltpu.VMEM((2, TB, D_MODEL), F32), pltpu.SemaphoreType.DMA((2,))))
    mine = [_add_window_call("add_chips_w_in", g2_in, sib_in, chips_in, ck)] + mine
    full = _join_halves_call("join_halves", mine)

    return dict(loss=loss[0, 0], dx=dx, dmeta=dmeta, norm_gain=dnorm_gain, w_gate_up=dwg[:GATE_RANK], b_gate=dbg,
                ret_norm_gain=dret_gain, gla_norm_gain=dgla_gain, final_norm_gain=dgf.reshape(D_MODEL),
                w_in=full[0], w_branch_ret=full[1], w_branch_gla=full[2], w_out=full[3])


MESH = pl.DeviceIdType.MESH
ANY = pl.BlockSpec(memory_space=pl.ANY)


def _place():
    return lax.axis_index("x"), lax.axis_index("y"), lax.axis_index("c")


def _gather8_call(name, parts):
    comm = _gather_plan(parts)
    n = len(parts)

    def body(*refs):
        begin, finish = comm.make(refs[:n], refs[n:2 * n], refs[-2], refs[-1])
        begin()
        finish()

    return pl.pallas_call(
        body, name=name, out_shape=list(comm.out_shapes), in_specs=[ANY] * n, out_specs=[ANY] * n,
        scratch_shapes=_comm_sems(comm),
    )(*parts)


def _swap_halves_call(name, gs):
    n = len(gs)

    def body(*refs):
        g_refs, b_refs = refs[:n], refs[n:2 * n]
        send_sems, recv_sems = refs[2 * n:]
        x, y, c = _place()
        copies = [pltpu.make_async_remote_copy(
            src_ref=g_refs[t].at[1 - c], dst_ref=b_refs[t], send_sem=send_sems.at[t], recv_sem=recv_sems.at[t],
            device_id=(x, y, 1 - c), device_id_type=MESH) for t in range(n)]
        for cp in copies:
            cp.start()
        for cp in copies:
            cp.wait()

    return pl.pallas_call(
        body, name=name,
        out_shape=[jax.ShapeDtypeStruct(g.shape[1:], g.dtype) for g in gs],
        in_specs=[ANY] * n, out_specs=[ANY] * n,
        scratch_shapes=[pltpu.SemaphoreType.DMA((n,)), pltpu.SemaphoreType.DMA((n,))],
    )(*gs)


def _join_halves_call(name, ts):
    n = len(ts)

    def body(*refs):
        o_refs = refs[n:2 * n]
        send_sems, recv_sems = refs[2 * n:]
        x, y, c = _place()
        copies = [pltpu.make_async_remote_copy(
            src_ref=o_refs[t].at[c], dst_ref=o_refs[t].at[c], send_sem=send_sems.at[t], recv_sem=recv_sems.at[t],
            device_id=(x, y, 1 - c), device_id_type=MESH) for t in range(n)]
        for cp in copies:
            cp.start()
        for t in range(n):
            copies[t].wait_send()
            pltpu.make_async_remote_copy(
                src_ref=o_refs[t].at[c], dst_ref=o_refs[t].at[1 - c], send_sem=send_sems.at[t],
                recv_sem=recv_sems.at[t], device_id=(x, y, 1 - c), device_id_type=MESH).wait_recv()

    return pl.pallas_call(
        body, name=name,
        out_shape=[jax.ShapeDtypeStruct(t.shape, t.dtype) for t in ts],
        in_specs=[ANY] * n, out_specs=[ANY] * n, input_output_aliases={t: t for t in range(n)},
        scratch_shapes=[pltpu.SemaphoreType.DMA((n,)), pltpu.SemaphoreType.DMA((n,))],
    )(*ts)


def _row_block(rows, cols, budget):
    best = 8
    for rb in range(8, rows + 1, 8):
        if rows % rb == 0 and rb * cols * 4 <= budget:
            best = rb
    return best


def _add_half_call(name, g, b, ck):
    _, _, r, cc = g.shape
    rb = _row_block(r, cc, 2 * 1024 * 1024)

    def body(ck_ref, g_ref, b_ref, o_ref):
        o_ref[...] = (g_ref[...] + b_ref[...]).astype(BF16)

    return pl.pallas_call(
        body, name=name,
        grid_spec=pltpu.PrefetchScalarGridSpec(
            num_scalar_prefetch=1, grid=(4, r // rb),
            in_specs=[pl.BlockSpec((None, None, rb, cc), lambda k, i, ck_ref: (ck_ref[0], k, i, 0)),
                      pl.BlockSpec((None, rb, cc), lambda k, i, ck_ref: (k, i, 0))],
            out_specs=pl.BlockSpec((None, rb, cc), lambda k, i, ck_ref: (k, i, 0))),
        out_shape=jax.ShapeDtypeStruct(b.shape, BF16),
        compiler_params=_cparams(2),
    )(ck, g, b)


def _add_rows_call(name, g, b, ck):
    _, r, cc = g.shape
    rb = _row_block(r, cc, 2 * 1024 * 1024)

    def body(ck_ref, g_ref, b_ref, o_ref):
        o_ref[...] = (g_ref[...] + b_ref[...]).astype(BF16)

    return pl.pallas_call(
        body, name=name,
        grid_spec=pltpu.PrefetchScalarGridSpec(
            num_scalar_prefetch=1, grid=(r // rb,),
            in_specs=[pl.BlockSpec((None, rb, cc), lambda i, ck_ref: (ck_ref[0], i, 0)),
                      pl.BlockSpec((rb, cc), lambda i, ck_ref: (i, 0))],
            out_specs=pl.BlockSpec((rb, cc), lambda i, ck_ref: (i, 0))),
        out_shape=jax.ShapeDtypeStruct((r, cc), BF16),
        compiler_params=_cparams(1),
    )(ck, g, b)


def _add_window_call(name, g, b, p, ck):
    _, r, _ = g.shape
    nb, step = WIN_W // 128, WIN_STEP // 128

    def body(ck_ref, g_ref, b_ref, p0_ref, p1_ref, p2_ref, o_ref):
        own = g_ref[...] + b_ref[...]
        o_ref[...] = ((own + p0_ref[...].astype(F32)) + p1_ref[...].astype(F32)) + p2_ref[...].astype(F32)

    def peer(j):
        return pl.BlockSpec((None, r, 128), lambda i, ck_ref: (j, 0, i))

    return pl.pallas_call(
        body, name=name,
        grid_spec=pltpu.PrefetchScalarGridSpec(
            num_scalar_prefetch=1, grid=(nb,),
            in_specs=[pl.BlockSpec((None, r, 128), lambda i, ck_ref: (ck_ref[0], 0, step * ck_ref[1] + i)),
                      pl.BlockSpec((r, 128), lambda i, ck_ref: (0, step * ck_ref[1] + i)),
                      peer(0), peer(1), peer(2)],
            out_specs=pl.BlockSpec((None, r, 128), lambda i, ck_ref: (ck_ref[0], 0, i))),
        out_shape=jax.ShapeDtypeStruct((2, r, WIN_W), F32),
        compiler_params=_cparams(1),
    )(ck, g, b, p, p, p)


def _add_chips_call(name, g, b, p, ck):
    _, _, r, cc = g.shape
    rb = _row_block(r, cc, 2 * 1024 * 1024)

    def body(ck_ref, g_ref, b_ref, p0_ref, p1_ref, p2_ref, o_ref):
        own = g_ref[...] + b_ref[...]
        o_ref[...] = ((own + p0_ref[...].astype(F32)) + p1_ref[...].astype(F32)) + p2_ref[...].astype(F32)

    def peer(j):
        return pl.BlockSpec((None, rb, cc), lambda i, ck_ref: (j, i, 0))

    return pl.pallas_call(
        body, name=name,
        grid_spec=pltpu.PrefetchScalarGridSpec(
            num_scalar_prefetch=1, grid=(r // rb,),
            in_specs=[pl.BlockSpec((None, None, rb, cc), lambda i, ck_ref: (ck_ref[0], ck_ref[1], i, 0)),
                      pl.BlockSpec((None, rb, cc), lambda i, ck_ref: (ck_ref[1], i, 0)),
                      peer(0), peer(1), peer(2)],
            out_specs=pl.BlockSpec((None, rb, cc), lambda i, ck_ref: (ck_ref[0], i, 0))),
        out_shape=jax.ShapeDtypeStruct((2, r, cc), F32),
        compiler_params=_cparams(1),
    )(ck, g, b, p, p, p)


def _sum8_call(name, g):
    def body(g_ref, o_ref):
        acc = g_ref[0]
        for d in range(1, 8):
            acc = acc + g_ref[d]
        o_ref[...] = acc

    return pl.pallas_call(body, name=name, out_shape=jax.ShapeDtypeStruct(g.shape[1:], F32))(g)


def _adamw_call(name, w, g, m, v):
    r, cc = w.shape
    if r % 8 == 0 or r * cc * 4 <= 1024 * 1024:
        rb = _row_block(r, cc, 1024 * 1024) if r % 8 == 0 else r
        grid, spec = (r // rb,), pl.BlockSpec((rb, cc), lambda i: (i, 0))
    else:
        grid, spec = (cc // 128,), pl.BlockSpec((r, 128), lambda i: (0, i))

    def body(w_ref, g_ref, m_ref, v_ref, d_ref, m2_ref, v2_ref):
        gv = g_ref[...]
        m2 = ADAM_B1 * m_ref[...] + (1.0 - ADAM_B1) * gv
        v2 = ADAM_B2 * v_ref[...] + (1.0 - ADAM_B2) * (gv * gv)
        m_hat = m2 / (1.0 - ADAM_B1 ** ADAM_STEP)
        v_hat = v2 / (1.0 - ADAM_B2 ** ADAM_STEP)
        d_ref[...] = -ADAM_LR * (m_hat / (jnp.sqrt(v_hat) + ADAM_EPS) + ADAM_WD * w_ref[...])
        m2_ref[...] = m2
        v2_ref[...] = v2

    return pl.pallas_call(
        body, name=name, grid=grid, in_specs=[spec] * 4, out_specs=[spec] * 3,
        out_shape=[jax.ShapeDtypeStruct((r, cc), F32)] * 3, compiler_params=_cparams(1),
    )(w, g, m, v)


SMALL = (("norm_gain", D_MODEL), ("b_gate", GLA_KW), ("ret_norm_gain", RET_W), ("gla_norm_gain", GLA_W),
         ("final_norm_gain", D_MODEL), ("w_gate_up", GATE_RANK * GLA_KW), ("meta_tokens", N_META * D_MODEL),
         ("loss", 1))


def _pack_rows(vecs, rows):
    flat = jnp.concatenate([v.reshape(-1) for v in vecs])
    return jnp.pad(flat, (0, rows * 128 - flat.shape[0])).reshape(rows, 128)


def kernel(x, meta_tokens, norm_gain, w_in, w_gate_up, b_gate, ret_norm_gain, gla_norm_gain, w_branch_ret, w_branch_gla, w_out, final_norm_gain, loss_target, m_meta_tokens, m_norm_gain, m_w_in, m_w_gate_up, m_b_gate, m_ret_norm_gain, m_gla_norm_gain, m_w_branch_ret, m_w_branch_gla, m_w_out, m_final_norm_gain, v_meta_tokens, v_norm_gain, v_w_in, v_w_gate_up, v_b_gate, v_ret_norm_gain, v_gla_norm_gain, v_w_branch_ret, v_w_branch_gla, v_w_out, v_final_norm_gain):
    xi, yi, ci = _place()
    kme = 2 * xi + yi
    ck = jnp.stack([ci, kme]).astype(jnp.int32)
    sw_in = w_in.shape[2]

    def my_half(a, dtype):
        r, cc = a.shape
        return lax.dynamic_index_in_dim(a.reshape(2, r // 2, cc), ci, 0, keepdims=False).astype(dtype)

    g_meta, g_wg = _gather8_call("gather_small_weights", [my_half(meta_tokens, F32), my_half(w_gate_up[0], F32)])
    branch_parts = [my_half(w_branch_ret[0], BF16), my_half(w_branch_gla[0], BF16), my_half(w_out[0], BF16)]
    meta = g_meta.reshape(4, 2, N_META // 2, D_MODEL // 4).transpose(1, 2, 0, 3).reshape(N_META, D_MODEL)
    wg_full = g_wg.reshape(4, 2, GATE_RANK // 2, GLA_KW // 4).transpose(1, 2, 0, 3).reshape(GATE_RANK, GLA_KW)

    loc = _device_step(x[0], loss_target[0], meta, norm_gain, my_half(w_in[0], BF16), wg_full, b_gate, ret_norm_gain,
                       gla_norm_gain,
                       branch_parts, final_norm_gain, ck)
    names = ("w_in", "w_branch_ret", "w_branch_gla", "w_out")
    full = [loc[nm] for nm in names]
    big_w = dict(w_in=w_in[0], w_branch_ret=w_branch_ret[0], w_branch_gla=w_branch_gla[0], w_out=w_out[0])
    big_m = dict(w_in=m_w_in[0], w_branch_ret=m_w_branch_ret[0], w_branch_gla=m_w_branch_gla[0], w_out=m_w_out[0])
    big_v = dict(w_in=v_w_in[0], w_branch_ret=v_w_branch_ret[0], w_branch_gla=v_w_branch_gla[0], w_out=v_w_out[0])
    grads, deltas, new_m, new_v = {}, {}, {}, {}
    for nm, f in zip(names, full):
        shape = big_w[nm].shape
        if nm == "w_in":
            f = lax.dynamic_slice_in_dim(f, (sw_in - WIN_STEP) * kme, sw_in, axis=2)
        g = f.reshape(shape)
        if nm == "w_in":
            d, m2, v2 = (a.T for a in _adamw_call("adamw_" + nm, big_w[nm].T, g.T, big_m[nm].T, big_v[nm].T))
        else:
            d, m2, v2 = _adamw_call("adamw_" + nm, big_w[nm], g, big_m[nm], big_v[nm])
        grads[nm], deltas[nm], new_m[nm], new_v[nm] = (a.reshape((1,) + shape) for a in (g, d, m2, v2))

    small_g = dict(loc)
    small_g["meta_tokens"] = loc["dmeta"]
    n_small = sum(sz for _, sz in SMALL)
    rows = -(-n_small // 128 // 8) * 8
    (g_small,) = _gather8_call("gather_small_grads", [_pack_rows([small_g[nm] for nm, _ in SMALL], rows)])
    tot = _sum8_call("sum_small_grads", g_small).reshape(-1)
    off = 0
    sg = {}
    for nm, sz in SMALL:
        sg[nm] = tot[off:off + sz]
        off += sz
    loss = sg.pop("loss")[0]
    sg["w_gate_up"] = lax.dynamic_slice_in_dim(sg["w_gate_up"].reshape(GATE_RANK, GLA_KW), kme * (GLA_KW // 4),
                                               GLA_KW // 4, axis=1)
    sg["meta_tokens"] = lax.dynamic_slice_in_dim(sg["meta_tokens"].reshape(N_META, D_MODEL), kme * (D_MODEL // 4),
                                                 D_MODEL // 4, axis=1)
    small_w = dict(norm_gain=norm_gain, b_gate=b_gate, ret_norm_gain=ret_norm_gain, gla_norm_gain=gla_norm_gain,
                   final_norm_gain=final_norm_gain, w_gate_up=w_gate_up, meta_tokens=meta_tokens)
    small_m = dict(norm_gain=m_norm_gain, b_gate=m_b_gate, ret_norm_gain=m_ret_norm_gain,
                   gla_norm_gain=m_gla_norm_gain, final_norm_gain=m_final_norm_gain, w_gate_up=m_w_gate_up,
                   meta_tokens=m_meta_tokens)
    small_v = dict(norm_gain=v_norm_gain, b_gate=v_b_gate, ret_norm_gain=v_ret_norm_gain,
                   gla_norm_gain=v_gla_norm_gain, final_norm_gain=v_final_norm_gain, w_gate_up=v_w_gate_up,
                   meta_tokens=v_meta_tokens)
    for nm in small_w:
        shape = small_w[nm].shape
        as2d = lambda a: a.reshape((-1, shape[-1]))
        grads[nm] = sg[nm].reshape(shape)
        deltas[nm], new_m[nm], new_v[nm] = (a.reshape(shape) for a in _adamw_call(
            "adamw_" + nm, as2d(small_w[nm]), as2d(sg[nm]), as2d(small_m[nm]), as2d(small_v[nm])))

    out_order = ("meta_tokens", "norm_gain", "w_in", "w_gate_up", "b_gate", "ret_norm_gain", "gla_norm_gain",
                 "w_branch_ret", "w_branch_gla", "w_out", "final_norm_gain")
    dx = loc["dx"].reshape(x.shape)
    return (loss, dx, *[grads[nm] for nm in out_order], *[deltas[nm] for nm in out_order],
            *[new_m[nm] for nm in out_order], *[new_v[nm] for nm in out_order])
```

```python
import math
from typing import Callable, NamedTuple

import numpy as np
import jax
import jax.numpy as jnp
from jax import lax
from jax.experimental import pallas as pl
from jax.experimental.pallas import tpu as pltpu

F32 = jnp.float32
BF16 = jnp.bfloat16

D_MODEL = 1024
N_META = 16
EPS = 1e-6
ROPE_BASE = 10000.0
RET_HEADS, RET_QK, RET_V = 4, 256, 512
RET_W = RET_HEADS * RET_V
GLA_HEADS, GLA_K, GLA_V = 4, 128, 256
GLA_W = GLA_HEADS * GLA_V
GLA_KW = GLA_HEADS * GLA_K
GATE_RANK = 16
GATE_TAU = 16.0
GLA_SUB = 16

TM = 256
T0 = TM
PADF = T0 - N_META
GC = 128
GS = 3
TB = 768
TK = 768

W_R = 6144
W_G = 3088
W_GP = 3200
W_M = 2048
IN_COLS = W_R + W_G + W_M
WIN_STEP = (IN_COLS // 4) // 128 * 128
WIN_W = -(-(3 * (IN_COLS // 4 - WIN_STEP) + IN_COLS // 4) // 128) * 128
IN_PAD = 3 * WIN_STEP + WIN_W

ADAM_LR, ADAM_B1, ADAM_B2, ADAM_EPS, ADAM_WD, ADAM_STEP = 0.001, 0.9, 0.999, 1e-08, 0.01, 10

VMEM_LIMIT = 56 * 1024 * 1024

NN = ((1,), (0,))
NT = ((1,), (1,))
TN = ((0,), (0,))


def _dot(a, b, dims):
    return lax.dot_general(a, b, (dims, ((), ())), preferred_element_type=F32)


def _cparams(n_axes):
    return pltpu.CompilerParams(dimension_semantics=("arbitrary",) * n_axes, vmem_limit_bytes=VMEM_LIMIT)


def _sigmoid(x):
    return 0.5 * jnp.tanh(0.5 * x) + 0.5


def _silu(x):
    h = 0.5 * x
    return h + h * jnp.tanh(h)


def _head_mean(x):
    return jnp.mean(x, axis=-1, keepdims=True)


def _split3(x):
    hi = x.astype(BF16)
    r1 = x - hi.astype(F32)
    mid = r1.astype(BF16)
    lo = (r1 - mid.astype(F32)).astype(BF16)
    return hi, mid, lo


def _exact_pm(p, x):
    hi, mid, lo = _split3(x)
    return _dot(p, hi, NN) + _dot(p, mid, NN) + _dot(p, lo, NN)


def _rms_call(x2d, head, gain, comm):
    tp = T0 + x2d.shape[0]
    nt = tp // TM
    n_xc = len(comm.srcs)

    def body(x_ref, hd_ref, g_ref, *rest):
        xc_src = rest[:n_xc]
        h_ref, u_ref = rest[n_xc:n_xc + 2]
        xc_dst = rest[n_xc + 2:2 * n_xc + 2]
        i = pl.program_id(0)
        begin, finish = comm.make(xc_src, xc_dst, rest[-2], rest[-1])
        pl.when(i == 0)(begin)
        h = jnp.where(i == 0, hd_ref[...], x_ref[...])
        h_ref[...] = h
        r = lax.rsqrt(jnp.mean(h * h, axis=-1, keepdims=True) + EPS)
        u_ref[...] = (h * r * g_ref[...]).astype(BF16)
        pl.when(i == nt - 1)(finish)

    tile = pl.BlockSpec((TM, D_MODEL), lambda i: (i, 0))
    return pl.pallas_call(
        body, name="rms_in", grid=(nt,),
        in_specs=[pl.BlockSpec((TM, D_MODEL), lambda i: (jnp.maximum(i - 1, 0), 0)),
                  pl.BlockSpec((T0, D_MODEL), lambda i: (0, 0)), pl.BlockSpec((1, D_MODEL), lambda i: (0, 0))]
        + [ANY] * n_xc,
        out_specs=[tile, tile] + [ANY] * n_xc,
        out_shape=[jax.ShapeDtypeStruct((tp, D_MODEL), F32), jax.ShapeDtypeStruct((tp, D_MODEL), BF16)]
        + list(comm.out_shapes),
        scratch_shapes=_comm_sems(comm), compiler_params=_cparams(1),
    )(x2d, head, gain, *comm.srcs)


PROJ_ROWS_MAX = 1408


def _proj_rows(m):
    return max(r for r in range(16, PROJ_ROWS_MAX + 1, 16) if m % r == 0)


def _mm_nn(name, a, b, out_dtype, tn, col0, ncols, epilogue=None, extras=(), extra_specs=()):
    m, k = a.shape
    nj, j0 = ncols // tn, col0 // tn
    tb = _proj_rows(m)

    def body(a_ref, b_ref, *rest):
        *ex, o_ref = rest
        acc = _dot(a_ref[...], b_ref[...], NN)
        if epilogue is None:
            o_ref[...] = acc.astype(out_dtype)
        else:
            epilogue(acc, o_ref, *ex)

    return pl.pallas_call(
        body, name=name, grid=(nj, m // tb),
        in_specs=[pl.BlockSpec((tb, k), lambda j, i: (i, 0)), pl.BlockSpec((k, tn), lambda j, i: (0, j0 + j))]
        + list(extra_specs),
        out_specs=pl.BlockSpec((tb, tn), lambda j, i: (i, j)),
        out_shape=jax.ShapeDtypeStruct((m, ncols), out_dtype),
        compiler_params=_cparams(2),
    )(a, b, *extras)


def _rope_tables(tp):
    half = RET_QK // 2
    pos = np.arange(tp, dtype=np.float32) - np.float32(PADF)
    inv = (ROPE_BASE ** (-np.arange(half, dtype=np.float64) / half)).astype(np.float32)
    ang = (pos[:, None] * inv[None, :]).astype(np.float64)
    return np.cos(ang).astype(np.float32), np.sin(ang).astype(np.float32)


def _rope_epilogue(acc, o_ref, cos_ref, sin_ref):
    scale = jnp.where(pl.program_id(0) == 1, RET_QK ** -0.5, 1.0).astype(F32)
    cos, sin = cos_ref[...], sin_ref[...]
    half = RET_QK // 2
    for h in range(RET_HEADS):
        t1 = acc[:, h * RET_QK:h * RET_QK + half]
        t2 = acc[:, h * RET_QK + half:(h + 1) * RET_QK]
        o_ref[:, h * RET_QK:h * RET_QK + half] = ((t1 * cos - t2 * sin) * scale).astype(BF16)
        o_ref[:, h * RET_QK + half:(h + 1) * RET_QK] = ((t2 * cos + t1 * sin) * scale).astype(BF16)


def _gqk_epilogue(acc, o_ref):
    o_ref[:, :GLA_KW] = acc[:, :GLA_KW] * (GLA_K ** -0.5)
    o_ref[:, GLA_KW:] = acc[:, GLA_KW:]


class _Comm(NamedTuple):
    srcs: tuple
    out_shapes: tuple
    n_sems: int
    make: Callable


def _comm_sems(comm):
    return [pltpu.SemaphoreType.DMA((comm.n_sems,)), pltpu.SemaphoreType.DMA((comm.n_sems,))]


def _start_wait(copies):
    def begin():
        for cp in copies:
            cp.start()

    def finish():
        for cp in copies:
            cp.wait()

    return begin, finish


def _other_chips(x, y):
    return [(1 - x, y), (x, 1 - y), (1 - x, 1 - y)]


def _gather_plan(parts, relay=()):
    n = len(parts)
    relay = tuple(relay) + (False,) * (n - len(relay))

    def make(x_refs, out_refs, send_sems, recv_sems):
        x, y, c = _place()
        me, sibling = (x, y, c), (x, y, 1 - c)
        xn, yn, dg = (1 - x, y), (x, 1 - y), (1 - x, 1 - y)

        def slot(t, px, py, pc, half=None):
            ref = out_refs[t].at[4 * px + 2 * py + pc]
            if half is None:
                return ref
            rows = ref.shape[0] // 2
            return ref.at[pl.ds(half * rows, rows)]

        def copy(t, k, dst, to, src=None):
            return pltpu.make_async_remote_copy(
                src_ref=dst if src is None else src, dst_ref=dst, send_sem=send_sems.at[8 * t + k],
                recv_sem=recv_sems.at[8 * t + k], device_id=to, device_id_type=MESH)

        mine = [pltpu.make_async_copy(x_refs[t], slot(t, *me), send_sems.at[8 * n + t]) for t in range(n)]
        sent = []
        for t in range(n):
            sent.append(copy(t, 0, slot(t, *me), sibling, src=x_refs[t]))
            sent.append(copy(t, 1, slot(t, *me), (*xn, c), src=x_refs[t]))
            sent.append(copy(t, 2, slot(t, *me), (*yn, c), src=x_refs[t]))
            if not relay[t]:
                sent.append(copy(t, 3, slot(t, *me), (*dg, c), src=x_refs[t]))

        def begin():
            for cp in mine + sent:
                cp.start()

        def finish():
            later = []

            def start(cp):
                cp.start()
                later.append(cp)

            for t in range(n):
                copy(t, 2, slot(t, *yn, c), me).wait_recv()
                if relay[t]:
                    start(copy(t, 3, slot(t, *yn, c, half=0), (*xn, c)))
                start(copy(t, 6, slot(t, *yn, c), sibling))
            for t in range(n):
                copy(t, 1, slot(t, *xn, c), me).wait_recv()
                if relay[t]:
                    start(copy(t, 4, slot(t, *xn, c, half=1), (*yn, c)))
                start(copy(t, 5, slot(t, *xn, c), sibling))
            for t in range(n):
                if relay[t]:
                    copy(t, 3, slot(t, *dg, c, half=0), me).wait_recv()
                    copy(t, 4, slot(t, *dg, c, half=1), me).wait_recv()
                else:
                    copy(t, 3, slot(t, *dg, c), me).wait_recv()
                start(copy(t, 7, slot(t, *dg, c), sibling))
            for t in range(n):
                copy(t, 0, slot(t, *sibling), me).wait_recv()
                copy(t, 5, slot(t, *xn, 1 - c), me).wait_recv()
                copy(t, 6, slot(t, *yn, 1 - c), me).wait_recv()
                copy(t, 7, slot(t, *dg, 1 - c), me).wait_recv()
            for cp in sent + later:
                cp.wait_send()
            for cp in mine:
                cp.wait()

        return begin, finish

    return _Comm(tuple(parts), tuple(jax.ShapeDtypeStruct((8,) + p.shape, p.dtype) for p in parts), 9 * n, make)


def _exchange_plan(ss):
    def make(s_refs, b_refs, send_sems, recv_sems):
        x, y, c = _place()
        return _start_wait([pltpu.make_async_remote_copy(
            src_ref=s_refs[t].at[2 * chip[0] + chip[1]], dst_ref=b_refs[t].at[j], send_sem=send_sems.at[3 * t + j],
            recv_sem=recv_sems.at[3 * t + j], device_id=(*chip, c), device_id_type=MESH)
            for t in range(len(s_refs)) for j, chip in enumerate(_other_chips(x, y))])

    return _Comm(tuple(ss), tuple(jax.ShapeDtypeStruct((3,) + s.shape[1:], s.dtype) for s in ss), 3 * len(ss), make)


def _exchange_window_plan(s):
    def make(s_refs, b_refs, send_sems, recv_sems):
        x, y, c = _place()
        return _start_wait([pltpu.make_async_remote_copy(
            src_ref=s_refs[0].at[:, pl.ds(pl.multiple_of((2 * chip[0] + chip[1]) * WIN_STEP, 128), WIN_W)],
            dst_ref=b_refs[0].at[j], send_sem=send_sems.at[j], recv_sem=recv_sems.at[j], device_id=(*chip, c),
            device_id_type=MESH) for j, chip in enumerate(_other_chips(x, y))])

    return _Comm((s,), (jax.ShapeDtypeStruct((3, s.shape[0], WIN_W), s.dtype),), 3, make)


def _swap_plan(gs):
    def make(g_refs, b_refs, send_sems, recv_sems):
        x, y, c = _place()
        return _start_wait([pltpu.make_async_remote_copy(
            src_ref=g_refs[t].at[1 - c], dst_ref=b_refs[t], send_sem=send_sems.at[t], recv_sem=recv_sems.at[t],
            device_id=(x, y, 1 - c), device_id_type=MESH) for t in range(len(g_refs))])

    return _Comm(tuple(gs), tuple(jax.ShapeDtypeStruct(g.shape[1:], g.dtype) for g in gs), len(gs), make)


def _spread_plan(parts):
    def make(p_refs, o_refs, send_sems, recv_sems):
        x, y, c = _place()
        copies = []
        for t in range(len(p_refs)):
            mine = o_refs[t].at[4 * x + 2 * y + c]
            copies.append(pltpu.make_async_copy(p_refs[t], mine, send_sems.at[7 * len(p_refs) + t]))
            for r in range(1, 8):
                peer = (1 - x if r & 4 else x, 1 - y if r & 2 else y, 1 - c if r & 1 else c)
                copies.append(pltpu.make_async_remote_copy(
                    src_ref=p_refs[t], dst_ref=mine, send_sem=send_sems.at[7 * t + r - 1],
                    recv_sem=recv_sems.at[7 * t + r - 1], device_id=peer, device_id_type=MESH))
        return _start_wait(copies)

    return _Comm(tuple(parts), tuple(jax.ShapeDtypeStruct((8,) + p.shape, p.dtype) for p in parts), 8 * len(parts),
                 make)


def _mm_nt_acc(name, a, w, tk, acc_in=None, epilogue=None, extras=(), extra_specs=(), extra_out_shapes=(),
               extra_out_specs=(), extra_scratch=(), comm=None, tb=TB):
    m, k = a.shape
    n = w.shape[0]
    nk, ni = k // tk, m // tb
    has_acc = acc_in is not None
    n_xc = len(comm.srcs) if comm else 0
    n_es = len(extra_scratch)

    def body(*refs):
        a_ref, w_ref = refs[0], refs[1]
        pos = 2
        acc_ref = None
        if has_acc:
            acc_ref = refs[pos]
            pos += 1
        ex = refs[pos:pos + len(extras)]
        pos += len(extras)
        xc_src = refs[pos:pos + n_xc]
        pos += n_xc
        n_scr = 1 + n_es + (2 if n_xc else 0)
        outs = refs[pos:len(refs) - n_scr - n_xc]
        xc_dst = refs[len(refs) - n_scr - n_xc:len(refs) - n_scr]
        scr = refs[len(refs) - n_scr]
        es = refs[len(refs) - n_scr + 1:len(refs) - n_scr + 1 + n_es]
        i, kk = pl.program_id(0), pl.program_id(1)
        if n_xc:
            begin, finish = comm.make(xc_src, xc_dst, refs[-2], refs[-1])
            pl.when((i == 0) & (kk == 0))(begin)

        @pl.when(kk == 0)
        def _():
            scr[...] = acc_ref[...] if has_acc else jnp.zeros_like(scr)

        scr[...] += _dot(a_ref[...], w_ref[...], NT)

        @pl.when(kk == nk - 1)
        def _():
            if epilogue is None:
                outs[0][...] = scr[...]
            else:
                epilogue(scr[...], outs, i, ni, *ex, *es)

        if n_xc:
            pl.when((i == ni - 1) & (kk == nk - 1))(finish)

    in_specs = [pl.BlockSpec((tb, tk), lambda i, kk: (i, kk)), pl.BlockSpec((n, tk), lambda i, kk: (0, kk))]
    args = [a, w]
    if has_acc:
        in_specs.append(pl.BlockSpec((tb, n), lambda i, kk: (i, 0)))
        args.append(acc_in)
    in_specs += list(extra_specs) + [ANY] * n_xc
    args += list(extras) + (list(comm.srcs) if comm else [])
    if epilogue is None:
        out_shape = [jax.ShapeDtypeStruct((m, n), F32)]
        out_specs = [pl.BlockSpec((tb, n), lambda i, kk: (i, 0))]
    else:
        out_shape, out_specs = list(extra_out_shapes), list(extra_out_specs)
    scratch = [pltpu.VMEM((tb, n), F32)] + list(extra_scratch)
    if n_xc:
        out_shape += list(comm.out_shapes)
        out_specs += [ANY] * n_xc
        scratch += _comm_sems(comm)
    return pl.pallas_call(
        body, name=name, grid=(ni, nk), in_specs=in_specs, out_specs=out_specs, out_shape=out_shape,
        scratch_shapes=scratch, compiler_params=_cparams(2),
    )(*args)


def _rms_bwd_epilogue(du, outs, i, ni, h_ref, g_ref, dh1_ref, obuf, sems):
    dx_ref, dmeta_ref, dg_ref = outs
    h = h_ref[...]
    r = lax.rsqrt(jnp.mean(h * h, axis=-1, keepdims=True) + EPS)
    xh = h * r
    dxh = du * g_ref[...]
    dh0 = dh1_ref[...] + r * (dxh - xh * jnp.mean(dxh * xh, axis=-1, keepdims=True))

    def put(slot, tile):
        return pltpu.make_async_copy(obuf.at[slot], dx_ref.at[pl.ds(pl.multiple_of(tile * TB - T0, 8), TB)],
                                     sems.at[slot])

    @pl.when(i == 0)
    def _():
        dg_ref[...] = jnp.zeros_like(dg_ref)
        dmeta_ref[...] = dh0[PADF:T0, :]
        obuf[0] = dh0
        first = pltpu.make_async_copy(obuf.at[0, pl.ds(T0, TB - T0)], dx_ref.at[pl.ds(0, TB - T0)], sems.at[0])
        first.start()
        first.wait()

    @pl.when(i >= 1)
    def _():
        slot = i % 2

        @pl.when(i >= 3)
        def _():
            put(slot, i - 2).wait()

        obuf[slot] = dh0
        put(slot, i).start()

    dg_ref[...] += jnp.sum(du * xh, axis=0, keepdims=True)

    @pl.when(i == ni - 1)
    def _():
        for tile in (ni - 2, ni - 1):
            if tile >= 1:
                put(tile % 2, tile).wait()


def _mm_tn(name, a, b, bn, ncols=None, bcol0=0, into=None, col0=0, out_cols=None):
    t, m = a.shape
    n = ncols or b.shape[1]
    j0, bj0 = col0 // bn, bcol0 // bn

    def body(a_ref, b_ref, *rest):
        o_ref = rest[-1]

        @pl.when(pl.program_id(1) == 0)
        def _():
            o_ref[...] = jnp.zeros_like(o_ref)

        o_ref[...] += _dot(a_ref[...], b_ref[...], TN)

    in_specs = [pl.BlockSpec((TK, m), lambda j, kk: (kk, 0)), pl.BlockSpec((TK, bn), lambda j, kk: (kk, bj0 + j))]
    args = [a, b]
    aliases = {}
    if into is not None:
        in_specs.append(ANY)
        args.append(into)
        aliases = {2: 0}
        out_cols = into.shape[1]
    return pl.pallas_call(
        body, name=name, grid=(n // bn, t // TK), in_specs=in_specs,
        out_specs=pl.BlockSpec((m, bn), lambda j, kk: (0, j0 + j)),
        out_shape=jax.ShapeDtypeStruct((m, out_cols or n), F32), input_output_aliases=aliases,
        compiler_params=_cparams(2),
    )(*args)


def _place_merge_cols_call(dwp, dw_m):
    c0 = W_R + W_GP - 128
    tail = IN_PAD - c0
    rows = 256

    def body(m_ref, p_ref, o_ref, buf, low, sem):
        get = pltpu.make_async_copy(o_ref.at[:, pl.ds(c0, 128)], low, sem)
        get.start()
        get.wait()
        for r in range(0, D_MODEL, rows):
            buf[r:r + rows, :] = jnp.concatenate(
                [low[r:r + rows, :GATE_RANK], m_ref[r:r + rows, :],
                 jnp.zeros((rows, tail - GATE_RANK - W_M), F32)], axis=1)
        put = pltpu.make_async_copy(buf, o_ref.at[:, pl.ds(c0, tail)], sem)
        put.start()
        put.wait()

    return pl.pallas_call(
        body, name="place_merge_cols",
        in_specs=[pl.BlockSpec(memory_space=pltpu.VMEM), ANY], out_specs=ANY,
        out_shape=jax.ShapeDtypeStruct(dwp.shape, F32), input_output_aliases={1: 0},
        scratch_shapes=[pltpu.VMEM((D_MODEL, tail), F32), pltpu.VMEM((D_MODEL, 128), F32), pltpu.SemaphoreType.DMA],
        compiler_params=pltpu.CompilerParams(vmem_limit_bytes=VMEM_LIMIT),
    )(dw_m, dwp)


def _ret_fill_decay(lg_ref, dm_scr):
    c = TM
    ii = lax.broadcasted_iota(jnp.int32, (c, c), 0)
    jj = lax.broadcasted_iota(jnp.int32, (c, c), 1)
    rel = (ii - jj).astype(F32)
    for h in range(RET_HEADS):
        dm_scr[h] = jnp.where(rel >= 0, jnp.exp(jnp.maximum(rel, 0.0) * lg_ref[h]), 0.0)


def _ret_consts(lg, dm_ref):
    c = TM
    idx = lax.broadcasted_iota(jnp.int32, (c, 1), 0).astype(F32)
    xi = jnp.exp((idx + 1.0) * lg)
    zeta = jnp.exp((c - 1.0 - idx) * lg)
    gc = jnp.exp(jnp.full((1, 1), c, F32) * lg)
    return dm_ref[...], xi, zeta, gc


def _ret_fwd_call(rqk, rv, rg, gain, lgam):
    tp = rqk.shape[0]
    nc = tp // TM

    def body(lg_ref, qk_ref, v_ref, rg_ref, g_ref, o_ref, a_ref, st_ref, sc_ref, nm_ref, s_scr, dm_scr):
        @pl.when(pl.program_id(0) == 0)
        def _():
            s_scr[...] = jnp.zeros_like(s_scr)
            _ret_fill_decay(lg_ref, dm_scr)

        for h in range(RET_HEADS):
            dm, xi, zeta, gc = _ret_consts(lg_ref[h], dm_scr.at[h])
            q = qk_ref[:, h * RET_QK:(h + 1) * RET_QK]
            k = qk_ref[:, D_MODEL + h * RET_QK:D_MODEL + (h + 1) * RET_QK]
            v = v_ref[:, h * RET_V:(h + 1) * RET_V]
            sb = s_scr[h].astype(BF16)
            st_ref[0, h] = sb
            s = (_dot(q, k, NT) * dm).astype(BF16)
            sc_ref[0, h] = s
            o = _dot(s, v, NN) + xi * _dot(q, sb, NN)
            kz = (k.astype(F32) * zeta).astype(BF16)
            s_scr[h] = gc * s_scr[h] + _dot(kz, v, TN)
            o_ref[:, h * RET_V:(h + 1) * RET_V] = o
            mu = _head_mean(o)
            xc = o - mu
            rstd = lax.rsqrt(_head_mean(xc * xc) + EPS)
            nm_ref[h] = mu
            nm_ref[RET_HEADS + h] = rstd
            xh = xc * rstd
            a_ref[:, h * RET_V:(h + 1) * RET_V] = (
                xh * g_ref[:, h * RET_V:(h + 1) * RET_V] * _silu(rg_ref[:, h * RET_V:(h + 1) * RET_V])).astype(BF16)

    return pl.pallas_call(
        body, name="ret_fwd", grid=(nc,),
        in_specs=[pl.BlockSpec(memory_space=pltpu.SMEM),
                  pl.BlockSpec((TM, 2 * D_MODEL), lambda n: (n, 0)),
                  pl.BlockSpec((TM, RET_W), lambda n: (n, 0)),
                  pl.BlockSpec((TM, RET_W), lambda n: (n, 0)),
                  pl.BlockSpec((1, RET_W), lambda n: (0, 0))],
        out_specs=[pl.BlockSpec((TM, RET_W), lambda n: (n, 0)),
                   pl.BlockSpec((TM, RET_W), lambda n: (n, 0)),
                   pl.BlockSpec((1, RET_HEADS, RET_QK, RET_V), lambda n: (n, 0, 0, 0)),
                   pl.BlockSpec((1, RET_HEADS, TM, TM), lambda n: (n, 0, 0, 0)),
                   pl.BlockSpec((2 * RET_HEADS, TM, 1), lambda n: (0, n, 0))],
        out_shape=[jax.ShapeDtypeStruct((tp, RET_W), F32), jax.ShapeDtypeStruct((tp, RET_W), BF16),
                   jax.ShapeDtypeStruct((nc, RET_HEADS, RET_QK, RET_V), BF16),
                   jax.ShapeDtypeStruct((nc, RET_HEADS, TM, TM), BF16),
                   jax.ShapeDtypeStruct((2 * RET_HEADS, tp, 1), F32)],
        scratch_shapes=[pltpu.VMEM((RET_HEADS, RET_QK, RET_V), F32), pltpu.VMEM((RET_HEADS, TM, TM), F32)],
        compiler_params=_cparams(1),
    )(lgam, rqk, rv, rg, gain)


def _ret_bwd_call(rqk, rv, rg, o_ret, dpr, wbr, states, scores, stats, gain, lgam, cos, sin):
    tp = rqk.shape[0]
    nc = tp // TM
    half = RET_QK // 2

    def body(lg_ref, qk_ref, v_ref, rg_ref, o_ref, dpr_ref, wbr_ref, st_ref, sc_ref, nm_ref, g_ref, cos_ref, sin_ref,
             dp_ref, dg_ref, ds_scr, dm_scr):
        @pl.when(pl.program_id(0) == 0)
        def _():
            ds_scr[...] = jnp.zeros_like(ds_scr)
            dg_ref[...] = jnp.zeros_like(dg_ref)
            _ret_fill_decay(lg_ref, dm_scr)

        cos, sin = cos_ref[...], sin_ref[...]
        for h in range(RET_HEADS):
            hs = slice(h * RET_V, (h + 1) * RET_V)
            dm, xi, zeta, gc = _ret_consts(lg_ref[h], dm_scr.at[h])
            rstd = nm_ref[RET_HEADS + h]
            xh = (o_ref[:, hs] - nm_ref[h]) * rstd
            gain_h = g_ref[:, hs]
            g = rg_ref[:, hs]
            sg = _sigmoid(g)
            silu = g * sg
            dah = _dot(dpr_ref[...], wbr_ref[hs, :], NT)
            dp_ref[:, 4 * D_MODEL + h * RET_V:4 * D_MODEL + (h + 1) * RET_V] = (
                dah * (xh * gain_h) * (sg * (1.0 + g * (1.0 - sg)))).astype(BF16)
            dn = dah * silu
            dg_ref[:, hs] += jnp.sum(dn * xh, axis=0, keepdims=True)
            dxh = dn * gain_h
            do = rstd * (dxh - _head_mean(dxh) - xh * _head_mean(dxh * xh))
            dob = do.astype(BF16)
            q = qk_ref[:, h * RET_QK:(h + 1) * RET_QK]
            k = qk_ref[:, D_MODEL + h * RET_QK:D_MODEL + (h + 1) * RET_QK]
            v = v_ref[:, hs]
            sp = st_ref[0, h]
            ds = ds_scr[h]
            dsb = ds.astype(BF16)
            s = sc_ref[0, h]
            dsc = (_dot(dob, v, NT) * dm).astype(BF16)
            dq = _dot(dsc, k, NN) + xi * _dot(dob, sp, NT)
            dk = _dot(dsc, q, TN) + zeta * _dot(v, dsb, NT)
            kz = (k.astype(F32) * zeta).astype(BF16)
            dv = _dot(s, dob, TN) + _dot(kz, dsb, NN)
            qx = (q.astype(F32) * xi).astype(BF16)
            ds_scr[h] = gc * ds + _dot(qx, dob, TN)
            dp_ref[:, 2 * D_MODEL + h * RET_V:2 * D_MODEL + (h + 1) * RET_V] = dv.astype(BF16)
            dk = dk * (RET_QK ** -0.5)
            for base, t in ((0, dq), (D_MODEL, dk)):
                t1, t2 = t[:, :half], t[:, half:]
                dp_ref[:, base + h * RET_QK:base + h * RET_QK + half] = (t1 * cos + t2 * sin).astype(BF16)
                dp_ref[:, base + h * RET_QK + half:base + (h + 1) * RET_QK] = (t2 * cos - t1 * sin).astype(BF16)

    rev = lambda n: (nc - 1 - n, 0)
    return pl.pallas_call(
        body, name="ret_bwd", grid=(nc,),
        in_specs=[pl.BlockSpec(memory_space=pltpu.SMEM),
                  pl.BlockSpec((TM, 2 * D_MODEL), rev),
                  pl.BlockSpec((TM, RET_W), rev),
                  pl.BlockSpec((TM, RET_W), rev),
                  pl.BlockSpec((TM, RET_W), rev),
                  pl.BlockSpec((TM, D_MODEL), rev),
                  pl.BlockSpec((RET_W, D_MODEL), lambda n: (0, 0)),
                  pl.BlockSpec((1, RET_HEADS, RET_QK, RET_V), lambda n: (nc - 1 - n, 0, 0, 0)),
                  pl.BlockSpec((1, RET_HEADS, TM, TM), lambda n: (nc - 1 - n, 0, 0, 0)),
                  pl.BlockSpec((2 * RET_HEADS, TM, 1), lambda n: (0, nc - 1 - n, 0)),
                  pl.BlockSpec((1, RET_W), lambda n: (0, 0)),
                  pl.BlockSpec((TM, half), rev),
                  pl.BlockSpec((TM, half), rev)],
        out_specs=[pl.BlockSpec((TM, W_R), rev), pl.BlockSpec((1, RET_W), lambda n: (0, 0))],
        out_shape=[jax.ShapeDtypeStruct((tp, W_R), BF16), jax.ShapeDtypeStruct((1, RET_W), F32)],
        scratch_shapes=[pltpu.VMEM((RET_HEADS, RET_QK, RET_V), F32), pltpu.VMEM((RET_HEADS, TM, TM), F32)],
        compiler_params=_cparams(1),
    )(lgam, rqk, rv, rg, o_ret, dpr, wbr, states, scores, stats, gain, cos, sin)


GLA_LEVELS = tuple(GC >> (s + 1) for s in range(int(math.log2(GC // GLA_SUB))))
NLEV = len(GLA_LEVELS)


def _gla_tril():
    return np.tril(np.ones((GC, GC), np.float32))


def _gla_masks():
    ii = lax.broadcasted_iota(jnp.int32, (GC, GC), 0)
    jj = lax.broadcasted_iota(jnp.int32, (GC, GC), 1)
    masks = []
    for m in GLA_LEVELS:
        sh = int(math.log2(2 * m))
        masks.append(((ii >> sh) == (jj >> sh)) & ((ii & m) != 0) & ((jj & m) == 0))
    sh = int(math.log2(GLA_SUB))
    md = ((ii >> sh) == (jj >> sh)) & (jj <= ii)
    row = lax.broadcasted_iota(jnp.int32, (GC, 1), 0)
    second = [(row & m) != 0 for m in GLA_LEVELS]
    return masks, md, second


def _gla_gate_call(glr, wg, bg, pmat):
    tp = glr.shape[0]
    gb = _proj_rows(tp)
    assert gb % GC == 0

    def body(glr_ref, wg_ref, bg_ref, p_ref, z_ref, b_ref):
        z = _dot(glr_ref[...].astype(BF16), wg_ref[...], NN) + bg_ref[...]
        z_ref[...] = z
        la = (jnp.minimum(z, 0.0) - jnp.log1p(jnp.exp(-jnp.abs(z)))) * (1.0 / GATE_TAU)
        for r in range(0, gb, GC):
            b_ref[r:r + GC, :] = _exact_pm(p_ref[...], la[r:r + GC, :])

    tile = pl.BlockSpec((gb, GLA_KW), lambda i: (i, 0))
    return pl.pallas_call(
        body, name="gla_gate", grid=(tp // gb,),
        in_specs=[pl.BlockSpec((gb, 128), lambda i: (i, 0)), pl.BlockSpec((128, GLA_KW), lambda i: (0, 0)),
                  pl.BlockSpec((1, GLA_KW), lambda i: (0, 0)), pl.BlockSpec((GC, GC), lambda i: (0, 0))],
        out_specs=[tile, tile],
        out_shape=[jax.ShapeDtypeStruct((tp, GLA_KW), F32), jax.ShapeDtypeStruct((tp, GLA_KW), F32)],
        compiler_params=_cparams(1),
    )(glr, wg, bg, pmat)


def _gla_gate_bwd_call(db, z, glr, wg, pmat_t, d_g):
    tp = db.shape[0]
    gb = _proj_rows(tp)
    assert gb % GC == 0 and (W_GP - 128) % 128 == 0

    def body(db_ref, z_ref, glr_ref, wg_ref, pt_ref, dgin_ref, dg_ref, dwg_ref, dbg_ref):
        i = pl.program_id(0)

        @pl.when(i == 0)
        def _():
            dwg_ref[...] = jnp.zeros_like(dwg_ref)
            dbg_ref[...] = jnp.zeros_like(dbg_ref)

        dla = jnp.concatenate([_exact_pm(pt_ref[...], db_ref[r:r + GC, :]) for r in range(0, gb, GC)], axis=0)
        row = i * gb + lax.broadcasted_iota(jnp.int32, (gb, 1), 0)
        dz = jnp.where(row >= PADF, dla * (1.0 / GATE_TAU) * _sigmoid(-z_ref[...]), 0.0)
        dzb = dz.astype(BF16)
        dg_ref[...] = _dot(dzb, wg_ref[...], NT).astype(BF16)
        dwg_ref[...] += _dot(glr_ref[...].astype(BF16), dzb, TN)
        dbg_ref[...] += jnp.sum(dz, axis=0, keepdims=True)

    tile = pl.BlockSpec((gb, GLA_KW), lambda i: (i, 0))
    const = lambda i: (0, 0)
    return pl.pallas_call(
        body, name="gla_gate_bwd", grid=(tp // gb,),
        in_specs=[tile, tile, pl.BlockSpec((gb, 128), lambda i: (i, 0)), pl.BlockSpec((128, GLA_KW), const),
                  pl.BlockSpec((GC, GC), const), ANY],
        out_specs=[pl.BlockSpec((gb, 128), lambda i: (i, (W_GP - 128) // 128)), pl.BlockSpec((128, GLA_KW), const),
                   pl.BlockSpec((1, GLA_KW), const)],
        out_shape=[jax.ShapeDtypeStruct(d_g.shape, BF16), jax.ShapeDtypeStruct((128, GLA_KW), F32),
                   jax.ShapeDtypeStruct((1, GLA_KW), F32)],
        input_output_aliases={5: 0}, compiler_params=_cparams(1),
    )(db, z, glr, wg, pmat_t, d_g)


def _gla_row_steps(b_ref, cs, rows, size):
    parts = [jnp.zeros((size, GLA_K), F32) if r is None else jnp.broadcast_to(b_ref[r:r + 1, cs], (size, GLA_K))
             for r in rows]
    return parts[0] if len(parts) == 1 else jnp.concatenate(parts, axis=0)


def _gla_factors(b_ref, h, second):
    cs = slice(h * GLA_K, (h + 1) * GLA_K)
    b = b_ref[:, cs]
    fq, fk = [], []
    for l, m in enumerate(GLA_LEVELS):
        d = b - _gla_row_steps(b_ref, cs, [s + m - 1 for s in range(0, GC, 2 * m)], 2 * m)
        f = jnp.exp(jnp.where(second[l], d, -d))
        fq.append(jnp.where(second[l], f, 0.0))
        fk.append(jnp.where(second[l], 0.0, f))
    dd = b - _gla_row_steps(b_ref, cs, [None] + [s - 1 for s in range(GLA_SUB, GC, GLA_SUB)], GLA_SUB)
    ed = jnp.exp(dd)
    edi = jnp.exp(-dd)
    eb = jnp.exp(b)
    bl = b_ref[GC - 1:GC, cs]
    ee = jnp.exp(bl - b)
    ebl = jnp.exp(bl)
    return fq, fk, ed, edi, eb, ee, ebl


def _gla_scaled(q, k, fq, fk, ed, edi):
    qt = [(q * f).astype(BF16) for f in fq]
    kt = [(k * f).astype(BF16) for f in fk]
    return qt, kt, (q * ed).astype(BF16), (k * edi).astype(BF16)


def _gla_scores(qt, kt, qd, kd, masks, md):
    a = jnp.where(md, _dot(qd, kd, NT), 0.0)
    for l in range(NLEV):
        a = a + jnp.where(masks[l], _dot(qt[l], kt[l], NT), 0.0)
    return a.astype(BF16)


def _gla_fwd_call(gqk, gv, b, gg, gain, comm=None):
    tp = gqk.shape[0]
    nc = tp // GC
    ns = nc // GS
    n_xc = len(comm.srcs) if comm else 0

    def body(qk_ref, v_ref, b_ref, gg_ref, g_ref, *rest):
        xc_src = rest[:n_xc]
        o_ref, a_ref, st_ref, am_ref, nr_ref = rest[n_xc:n_xc + 5]
        xc_dst = rest[n_xc + 5:2 * n_xc + 5]
        s_scr = rest[2 * n_xc + 5]
        n = pl.program_id(0)
        if n_xc:
            begin, finish = comm.make(xc_src, xc_dst, rest[-2], rest[-1])
            pl.when(n == 0)(begin)
            pl.when(n == ns - 1)(finish)

        @pl.when(n == 0)
        def _():
            s_scr[...] = jnp.zeros_like(s_scr)

        masks, md, second = _gla_masks()
        for cc in range(GS):
            rows = pl.ds(cc * GC, GC)
            qk_c, v_c, b_c, gg_c, o_c, a_c = (r.at[rows] for r in (qk_ref, v_ref, b_ref, gg_ref, o_ref, a_ref))
            for h in range(GLA_HEADS):
                q = qk_c[:, h * GLA_K:(h + 1) * GLA_K]
                k = qk_c[:, GLA_KW + h * GLA_K:GLA_KW + (h + 1) * GLA_K]
                vs = slice(h * GLA_V, (h + 1) * GLA_V)
                v = v_c[:, vs]
                fq, fk, ed, edi, eb, ee, ebl = _gla_factors(b_c, h, second)
                a = _gla_scores(*_gla_scaled(q, k, fq, fk, ed, edi), masks, md)
                am_ref[cc, h] = a
                sb = s_scr[h].astype(BF16)
                st_ref[cc, h] = sb
                o = _dot(a, v, NN) + _dot((q * eb).astype(BF16), sb, NT)
                s_scr[h] = s_scr[h] * ebl + _dot(v, (k * ee).astype(BF16), TN)
                o_c[:, vs] = o
                rstd = lax.rsqrt(_head_mean(o * o) + EPS)
                nr_ref[h, rows, :] = rstd
                a_c[:, vs] = (o * rstd * g_ref[:, vs] * _silu(gg_c[:, vs])).astype(BF16)

    return pl.pallas_call(
        body, name="gla_fwd", grid=(ns,),
        in_specs=[pl.BlockSpec((GS * GC, 2 * GLA_KW), lambda n: (n, 0)),
                  pl.BlockSpec((GS * GC, GLA_W), lambda n: (n, 0)),
                  pl.BlockSpec((GS * GC, GLA_KW), lambda n: (n, 0)),
                  pl.BlockSpec((GS * GC, GLA_W), lambda n: (n, 0)),
                  pl.BlockSpec((1, GLA_W), lambda n: (0, 0))] + [ANY] * n_xc,
        out_specs=[pl.BlockSpec((GS * GC, GLA_W), lambda n: (n, 0)),
                   pl.BlockSpec((GS * GC, GLA_W), lambda n: (n, 0)),
                   pl.BlockSpec((GS, GLA_HEADS, GLA_V, GLA_K), lambda n: (n, 0, 0, 0)),
                   pl.BlockSpec((GS, GLA_HEADS, GC, GC), lambda n: (n, 0, 0, 0)),
                   pl.BlockSpec((GLA_HEADS, GS * GC, 1), lambda n: (0, n, 0))] + [ANY] * n_xc,
        out_shape=[jax.ShapeDtypeStruct((tp, GLA_W), F32), jax.ShapeDtypeStruct((tp, GLA_W), BF16),
                   jax.ShapeDtypeStruct((nc, GLA_HEADS, GLA_V, GLA_K), BF16),
                   jax.ShapeDtypeStruct((nc, GLA_HEADS, GC, GC), BF16),
                   jax.ShapeDtypeStruct((GLA_HEADS, tp, 1), F32)] + (list(comm.out_shapes) if comm else []),
        scratch_shapes=[pltpu.VMEM((GLA_HEADS, GLA_V, GLA_K), F32)] + (_comm_sems(comm) if comm else []),
        compiler_params=_cparams(1),
    )(gqk, gv, b, gg, gain, *(comm.srcs if comm else ()))


def _gla_bwd_call(gqk, gv, b, gg, o_gla, da, states, scores, rstds, gain, comm=None):
    tp = gqk.shape[0]
    nc = tp // GC
    ns = nc // GS
    o_gv, o_gg = 2 * GLA_KW, 2 * GLA_KW + GLA_W
    n_xc = len(comm.srcs) if comm else 0

    def body(qk_all, v_all, b_all, gg_all, o_all, da_all, st_ref, am_ref, nr_ref, g_ref, *rest):
        xc_src = rest[:n_xc]
        dp_all, db_all, dg_ref = rest[n_xc:n_xc + 3]
        xc_dst = rest[n_xc + 3:2 * n_xc + 3]
        ds_scr = rest[2 * n_xc + 3]
        n = pl.program_id(0)
        if n_xc:
            begin, finish = comm.make(xc_src, xc_dst, rest[-2], rest[-1])
            pl.when(n == 0)(begin)
            pl.when(n == ns - 1)(finish)

        @pl.when(n == 0)
        def _():
            ds_scr[...] = jnp.zeros_like(ds_scr)
            dg_ref[...] = jnp.zeros_like(dg_ref)

        masks, md, second = _gla_masks()
        for cc, h in [(cc, h) for cc in reversed(range(GS)) for h in range(GLA_HEADS)]:
            rows = pl.ds(cc * GC, GC)
            qk_ref, v_ref, b_scr, gg_ref, o_ref, da_ref, dp_ref, db_scr = (
                r.at[rows] for r in (qk_all, v_all, b_all, gg_all, o_all, da_all, dp_all, db_all))
            cs = slice(h * GLA_K, (h + 1) * GLA_K)
            vs = slice(h * GLA_V, (h + 1) * GLA_V)
            rstd = nr_ref[h, rows, :]
            xh = o_ref[:, vs] * rstd
            gain_h = g_ref[:, vs]
            g = gg_ref[:, vs]
            sg = _sigmoid(g)
            dah = da_ref[:, vs]
            dp_ref[:, o_gg + h * GLA_V:o_gg + (h + 1) * GLA_V] = (
                dah * (xh * gain_h) * (sg * (1.0 + g * (1.0 - sg)))).astype(BF16)
            dn = dah * (g * sg)
            dg_ref[:, vs] += jnp.sum(dn * xh, axis=0, keepdims=True)
            dxh = dn * gain_h
            do = rstd * (dxh - xh * _head_mean(dxh * xh))
            dob = do.astype(BF16)
            q = qk_ref[:, cs]
            k = qk_ref[:, GLA_KW + h * GLA_K:GLA_KW + (h + 1) * GLA_K]
            v = v_ref[:, vs]
            fq, fk, ed, edi, eb, ee, ebl = _gla_factors(b_scr, h, second)
            qt, kt, qd, kd = _gla_scaled(q, k, fq, fk, ed, edi)
            sp = st_ref[cc, h]
            ds = ds_scr[h]
            dsb = ds.astype(BF16)
            q_in = q * eb
            k_end = k * ee
            da_s = _dot(dob, v, NT)
            dv = _dot(am_ref[cc, h], dob, TN) + _dot(k_end.astype(BF16), dsb, NT)
            dq_in = _dot(dob, sp, NN)
            dk_end = _dot(v, dsb, NN)
            dbl = jnp.sum(sp.astype(F32) * ds, axis=0, keepdims=True) * ebl
            ds_scr[h] = ds * ebl + _dot(dob, q_in.astype(BF16), TN)
            dq = dq_in * eb
            dk = dk_end * ee
            de_end = dk_end * k_end
            db = dq_in * q_in - de_end
            placed = [(GC - 1, jnp.sum(de_end, axis=0, keepdims=True) + dbl)]
            for l, m in enumerate(GLA_LEVELS):
                dal = jnp.where(masks[l], da_s, 0.0).astype(BF16)
                dqt = _dot(dal, kt[l], NN)
                dkt = _dot(dal, qt[l], TN)
                dq = dq + dqt * fq[l]
                dk = dk + dkt * fk[l]
                gl = dqt * (q * fq[l]) - dkt * (k * fk[l])
                db = db + gl
                placed += [(s + m - 1, -jnp.sum(gl[s:s + 2 * m], axis=0, keepdims=True)) for s in range(0, GC, 2 * m)]
            dad = jnp.where(md, da_s, 0.0).astype(BF16)
            dqd = _dot(dad, kd, NN)
            dkd = _dot(dad, qd, TN)
            dq = dq + dqd * ed
            dk = dk + dkd * edi
            gd = dqd * (q * ed) - dkd * (k * edi)
            db = db + gd
            placed += [(s - 1, -jnp.sum(gd[s:s + GLA_SUB], axis=0, keepdims=True)) for s in range(GLA_SUB, GC, GLA_SUB)]
            db_scr[:, cs] = db
            for r, val in placed:
                db_scr[r:r + 1, cs] += val
            dp_ref[:, cs] = (dq * (GLA_K ** -0.5)).astype(BF16)
            dp_ref[:, GLA_KW + h * GLA_K:GLA_KW + (h + 1) * GLA_K] = dk.astype(BF16)
            dp_ref[:, o_gv + h * GLA_V:o_gv + (h + 1) * GLA_V] = dv.astype(BF16)

    rev = lambda n: (ns - 1 - n, 0)
    const = lambda n: (0, 0)
    xc_shapes, xc_sems = (list(comm.out_shapes), _comm_sems(comm)) if n_xc else ([], [])
    return pl.pallas_call(
        body, name="gla_bwd", grid=(ns,),
        in_specs=[pl.BlockSpec((GS * GC, 2 * GLA_KW), rev),
                  pl.BlockSpec((GS * GC, GLA_W), rev),
                  pl.BlockSpec((GS * GC, GLA_KW), rev),
                  pl.BlockSpec((GS * GC, GLA_W), rev),
                  pl.BlockSpec((GS * GC, GLA_W), rev),
                  pl.BlockSpec((GS * GC, GLA_W), rev),
                  pl.BlockSpec((GS, GLA_HEADS, GLA_V, GLA_K), lambda n: (ns - 1 - n, 0, 0, 0)),
                  pl.BlockSpec((GS, GLA_HEADS, GC, GC), lambda n: (ns - 1 - n, 0, 0, 0)),
                  pl.BlockSpec((GLA_HEADS, GS * GC, 1), lambda n: (0, ns - 1 - n, 0)),
                  pl.BlockSpec((1, GLA_W), const)] + [ANY] * n_xc,
        out_specs=[pl.BlockSpec((GS * GC, W_GP), rev), pl.BlockSpec((GS * GC, GLA_KW), rev),
                   pl.BlockSpec((1, GLA_W), const)] + [ANY] * n_xc,
        out_shape=[jax.ShapeDtypeStruct((tp, W_GP), BF16), jax.ShapeDtypeStruct((tp, GLA_KW), F32),
                   jax.ShapeDtypeStruct((1, GLA_W), F32)] + xc_shapes,
        scratch_shapes=[pltpu.VMEM((GLA_HEADS, GLA_V, GLA_K), F32)] + xc_sems,
        compiler_params=_cparams(1),
    )(gqk, gv, b, gg, o_gla, da, states, scores, rstds, gain, *(comm.srcs if comm else ()))


def _mid_call(a_ret, a_gla, mg, h0, tgt, wbr, wbg, wout, gf):
    tp = h0.shape[0]
    nt = tp // TM

    def body(ar_ref, ag_ref, mg_ref, h_ref, t_ref, wbr_ref, wbg_ref, wo_ref, gf_ref,
             dh1_ref, dag_ref, dm_ref, mb_ref, dh1b_ref, dprb_ref, dpgb_ref, loss_ref, dgf_ref):
        i = pl.program_id(0)

        @pl.when(i == 0)
        def _():
            loss_ref[...] = jnp.zeros_like(loss_ref)
            dgf_ref[...] = jnp.zeros_like(dgf_ref)

        ar, ag = ar_ref[...], ag_ref[...]
        pr = _dot(ar, wbr_ref[...], NN)
        pg = _dot(ag, wbg_ref[...], NN)
        sr = _sigmoid(mg_ref[:, :D_MODEL])
        sg = _sigmoid(mg_ref[:, D_MODEL:])
        merged = (sr * pr + sg * pg).astype(BF16)
        mb_ref[...] = merged
        h1 = h_ref[...] + _dot(merged, wo_ref[...], NN)
        r1 = lax.rsqrt(jnp.mean(h1 * h1, axis=-1, keepdims=True) + EPS)
        xh = h1 * r1
        gfv = gf_ref[...]
        live = jnp.where(i > 0, 1.0, 0.0).astype(F32)
        err = (xh * gfv - t_ref[...]) * live
        loss_ref[...] += jnp.full(loss_ref.shape, 0.5 / D_MODEL, F32) * jnp.sum(err * err)
        dy = err * (1.0 / D_MODEL)
        dgf_ref[...] += jnp.sum(dy * xh, axis=0, keepdims=True)
        dxh = dy * gfv
        dh1 = r1 * (dxh - xh * jnp.mean(dxh * xh, axis=-1, keepdims=True))
        dh1_ref[...] = dh1
        dh1b = dh1.astype(BF16)
        dh1b_ref[...] = dh1b
        dmerged = _dot(dh1b, wo_ref[...], NT)
        dm_ref[:, :D_MODEL] = (dmerged * pr * sr * (1.0 - sr)).astype(BF16)
        dm_ref[:, D_MODEL:] = (dmerged * pg * sg * (1.0 - sg)).astype(BF16)
        dpr = (dmerged * sr).astype(BF16)
        dpg = (dmerged * sg).astype(BF16)
        dprb_ref[...] = dpr
        dpgb_ref[...] = dpg
        dag_ref[...] = _dot(dpg, wbg_ref[...], NT)

    tile = lambda w: pl.BlockSpec((TM, w), lambda i: (i, 0))
    const = lambda r, w: pl.BlockSpec((r, w), lambda i: (0, 0))
    return pl.pallas_call(
        body, name="merge_out_loss", grid=(nt,),
        in_specs=[tile(RET_W), tile(GLA_W), tile(W_M), tile(D_MODEL),
                  pl.BlockSpec((TM, D_MODEL), lambda i: (jnp.maximum(i - 1, 0), 0)),
                  const(RET_W, D_MODEL), const(GLA_W, D_MODEL), const(D_MODEL, D_MODEL), const(1, D_MODEL)],
        out_specs=[tile(D_MODEL), tile(GLA_W), tile(W_M), tile(D_MODEL), tile(D_MODEL), tile(D_MODEL),
                   tile(D_MODEL), const(1, 128), const(1, D_MODEL)],
        out_shape=[jax.ShapeDtypeStruct((tp, D_MODEL), F32), jax.ShapeDtypeStruct((tp, GLA_W), F32),
                   jax.ShapeDtypeStruct((tp, W_M), BF16),
                   jax.ShapeDtypeStruct((tp, D_MODEL), BF16), jax.ShapeDtypeStruct((tp, D_MODEL), BF16),
                   jax.ShapeDtypeStruct((tp, D_MODEL), BF16), jax.ShapeDtypeStruct((tp, D_MODEL), BF16),
                   jax.ShapeDtypeStruct((1, 128), F32), jax.ShapeDtypeStruct((1, D_MODEL), F32)],
        compiler_params=_cparams(1),
    )(a_ret, a_gla, mg, h0, tgt, wbr, wbg, wout, gf)


def _device_step(x2d, tgt2d, meta, norm_gain, w_in_part, w_gate_up, b_gate, ret_gain, gla_gain, branch_parts,
                 final_gain, ck):
    seq = x2d.shape[0]
    tp = T0 + seq
    head = jnp.concatenate([jnp.zeros((PADF, D_MODEL), F32), meta], axis=0)
    wg_pad = jnp.pad(w_gate_up, ((0, 128 - GATE_RANK), (0, 0))).astype(BF16)

    half = RET_QK // 2
    cos, sin = (jnp.asarray(t) for t in _rope_tables(tp))
    lgam = jnp.log1p(-(2.0 ** (-5.0 - jnp.arange(RET_HEADS, dtype=F32))))
    pmat = jnp.asarray(_gla_tril(), BF16)
    pmat_t = jnp.asarray(_gla_tril().T.copy(), BF16)

    h0, u, g_in = _rms_call(x2d, head, norm_gain, _gather_plan([w_in_part], relay=(True,)))
    hr, sw = w_in_part.shape
    w_in_bf = g_in.reshape(4, 2, hr, sw).transpose(1, 2, 0, 3).reshape(2 * hr, 4 * sw)
    w_r = w_in_bf
    w_g = jnp.pad(w_in_bf[:, W_R:W_R + W_G], ((0, 0), (0, W_GP - W_G)))
    w_m = w_in_bf[:, W_R + W_G:]
    tab = pl.BlockSpec((_proj_rows(tp), half), lambda j, i: (i, 0))
    rqk = _mm_nn("proj_rqk", u, w_r, BF16, D_MODEL, 0, 2 * D_MODEL, _rope_epilogue, (cos, sin), (tab, tab))
    rv = _mm_nn("proj_rv", u, w_r, BF16, RET_W, 2 * D_MODEL, RET_W)
    rg = _mm_nn("proj_rg", u, w_r, F32, RET_W, 4 * D_MODEL, RET_W)
    gqk = _mm_nn("proj_gqk", u, w_g, F32, 2 * GLA_KW, 0, 2 * GLA_KW, _gqk_epilogue)
    gv = _mm_nn("proj_gv", u, w_g, BF16, GLA_W, 2 * GLA_KW, GLA_W)
    gg = _mm_nn("proj_gg", u, w_g, F32, GLA_W, 2 * GLA_KW + GLA_W, GLA_W)
    glr = _mm_nn("proj_glr", u, w_g, F32, 128, 2 * GLA_KW + 2 * GLA_W, 128)
    mg = _mm_nn("proj_mg", u, w_m, F32, W_M, 0, W_M)

    o_ret, a_ret, st_ret, sc_ret, nm_ret = _ret_fwd_call(rqk, rv, rg, ret_gain, lgam)
    z_gate, b_dec = _gla_gate_call(glr, wg_pad, b_gate, pmat)
    o_gla, a_gla, st_gla, sc_gla, nr_gla, g_br, g_bg, g_out = _gla_fwd_call(gqk, gv, b_dec, gg, gla_gain,
                                                                    comm=_spread_plan(branch_parts))
    wbr = g_br.reshape(RET_W, D_MODEL)
    wbg = g_bg.reshape(GLA_W, D_MODEL)
    wout = g_out.reshape(D_MODEL, D_MODEL)

    gf = final_gain.reshape(1, D_MODEL)
    (dh1, da_gla, dm, merged_b, dh1_b, dpr_b, dpg_b, loss, dgf) = _mid_call(
        a_ret, a_gla, mg, h0, tgt2d, wbr, wbg, wout, gf)

    names_b = ("w_branch_ret", "w_branch_gla", "w_out")
    g2_b = [_mm_tn("dw_br", a_ret, dpr_b, D_MODEL).reshape(4, 2, RET_W // 8, D_MODEL).transpose(1, 0, 2, 3),
            _mm_tn("dw_bg", a_gla, dpg_b, D_MODEL).reshape(4, 2, GLA_W // 8, D_MODEL).transpose(1, 0, 2, 3),
            _mm_tn("dw_out", merged_b, dh1_b, D_MODEL).reshape(4, 2, D_MODEL // 8, D_MODEL).transpose(1, 0, 2, 3)]
    sib_b = _swap_halves_call("swap_halves_branch", g2_b)
    sum_b = [_add_half_call("add_half_" + nm, g, b, ck) for nm, g, b in zip(names_b, g2_b, sib_b)]
    d_g, db_dec, dgla_gain, *chips_b = _gla_bwd_call(gqk, gv, b_dec, gg, o_gla, da_gla, st_gla, sc_gla, nr_gla,
                                                     gla_gain,
                                                     comm=_exchange_plan(sum_b))
    d_g, dwg, dbg = _gla_gate_bwd_call(db_dec, z_gate, glr, wg_pad, pmat_t, d_g)
    mine = [_add_chips_call("add_chips_" + nm, g, b, p, ck) for nm, g, b, p in zip(names_b, g2_b, sib_b, chips_b)]

    d_r, dret_gain = _ret_bwd_call(rqk, rv, rg, o_ret, dpr_b, wbr, st_ret, sc_ret, nm_ret, ret_gain, lgam, cos, sin)

    dwp = _mm_tn("dw_r", u, d_r, 2 * D_MODEL, out_cols=IN_PAD)
    dwp = _mm_tn("dw_g", u, d_g, D_MODEL, ncols=W_GP - 128, into=dwp, col0=W_R)
    dwp = _mm_tn("dw_glr", u, d_g, 128, ncols=128, bcol0=W_GP - 128, into=dwp, col0=W_R + W_GP - 128)
    g2_in = _place_merge_cols_call(dwp, _mm_tn("dw_m", u, dm, 2 * D_MODEL)).reshape(2, D_MODEL // 2, IN_PAD)

    du, sib_in = _mm_nt_acc("du_g", d_g, w_g, W_GP, comm=_swap_plan([g2_in]), tb=_proj_rows(tp))
    sum_in = _add_rows_call("add_half_w_in", g2_in, sib_in, ck)
    du, chips_in = _mm_nt_acc("du_r", d_r, w_r, 2 * D_MODEL, acc_in=du, comm=_exchange_window_plan(sum_in),
                              tb=_proj_rows(tp))
    tile = pl.BlockSpec((TB, D_MODEL), lambda i, kk: (i, 0))
    row = pl.BlockSpec((1, D_MODEL), lambda i, kk: (0, 0))
    dx, dmeta, dnorm_gain = _mm_nt_acc(
        "du_m", dm, w_m, W_M, acc_in=du, epilogue=_rms_bwd_epilogue, extras=(h0, norm_gain, dh1),
        extra_specs=(tile, row, tile),
        extra_out_shapes=(jax.ShapeDtypeStruct((seq, D_MODEL), F32), jax.ShapeDtypeStruct((N_META, D_MODEL), F32),
                          jax.ShapeDtypeStruct((1, D_MODEL), F32)),
        extra_out_specs=(ANY, pl.BlockSpec((N_META, D_MODEL), lambda i, kk: (0, 0)), row),
        extra_scratch=(pltpu.VMEM((2, TB, D_MODEL), F32), pltpu.SemaphoreType.DMA((2,))))
    mine = [_add_window_call("add_chips_w_in", g2_in, sib_in, chips_in, ck)] + mine
    full = _join_halves_call("join_halves", mine)

    return dict(loss=loss[0, 0], dx=dx, dmeta=dmeta, norm_gain=dnorm_gain, w_gate_up=dwg[:GATE_RANK], b_gate=dbg,
                ret_norm_gain=dret_gain, gla_norm_gain=dgla_gain, final_norm_gain=dgf.reshape(D_MODEL),
                w_in=full[0], w_branch_ret=full[1], w_branch_gla=full[2], w_out=full[3])


MESH = pl.DeviceIdType.MESH
ANY = pl.BlockSpec(memory_space=pl.ANY)


def _place():
    return lax.axis_index("x"), lax.axis_index("y"), lax.axis_index("c")


def _gather8_call(name, parts):
    comm = _gather_plan(parts)
    n = len(parts)

    def body(*refs):
        begin, finish = comm.make(refs[:n], refs[n:2 * n], refs[-2], refs[-1])
        begin()
        finish()

    return pl.pallas_call(
        body, name=name, out_shape=list(comm.out_shapes), in_specs=[ANY] * n, out_specs=[ANY] * n,
        scratch_shapes=_comm_sems(comm),
    )(*parts)


def _swap_halves_call(name, gs):
    n = len(gs)

    def body(*refs):
        g_refs, b_refs = refs[:n], refs[n:2 * n]
        send_sems, recv_sems = refs[2 * n:]
        x, y, c = _place()
        copies = [pltpu.make_async_remote_copy(
            src_ref=g_refs[t].at[1 - c], dst_ref=b_refs[t], send_sem=send_sems.at[t], recv_sem=recv_sems.at[t],
            device_id=(x, y, 1 - c), device_id_type=MESH) for t in range(n)]
        for cp in copies:
            cp.start()
        for cp in copies:
            cp.wait()

    return pl.pallas_call(
        body, name=name,
        out_shape=[jax.ShapeDtypeStruct(g.shape[1:], g.dtype) for g in gs],
        in_specs=[ANY] * n, out_specs=[ANY] * n,
        scratch_shapes=[pltpu.SemaphoreType.DMA((n,)), pltpu.SemaphoreType.DMA((n,))],
    )(*gs)


def _join_halves_call(name, ts):
    n = len(ts)

    def body(*refs):
        o_refs = refs[n:2 * n]
        send_sems, recv_sems = refs[2 * n:]
        x, y, c = _place()
        copies = [pltpu.make_async_remote_copy(
            src_ref=o_refs[t].at[c], dst_ref=o_refs[t].at[c], send_sem=send_sems.at[t], recv_sem=recv_sems.at[t],
            device_id=(x, y, 1 - c), device_id_type=MESH) for t in range(n)]
        for cp in copies:
            cp.start()
        for t in range(n):
            copies[t].wait_send()
            pltpu.make_async_remote_copy(
                src_ref=o_refs[t].at[c], dst_ref=o_refs[t].at[1 - c], send_sem=send_sems.at[t],
                recv_sem=recv_sems.at[t], device_id=(x, y, 1 - c), device_id_type=MESH).wait_recv()

    return pl.pallas_call(
        body, name=name,
        out_shape=[jax.ShapeDtypeStruct(t.shape, t.dtype) for t in ts],
        in_specs=[ANY] * n, out_specs=[ANY] * n, input_output_aliases={t: t for t in range(n)},
        scratch_shapes=[pltpu.SemaphoreType.DMA((n,)), pltpu.SemaphoreType.DMA((n,))],
    )(*ts)


def _row_block(rows, cols, budget):
    best = 8
    for rb in range(8, rows + 1, 8):
        if rows % rb == 0 and rb * cols * 4 <= budget:
            best = rb
    return best


def _add_half_call(name, g, b, ck):
    _, _, r, cc = g.shape
    rb = _row_block(r, cc, 2 * 1024 * 1024)

    def body(ck_ref, g_ref, b_ref, o_ref):
        o_ref[...] = (g_ref[...] + b_ref[...]).astype(BF16)

    return pl.pallas_call(
        body, name=name,
        grid_spec=pltpu.PrefetchScalarGridSpec(
            num_scalar_prefetch=1, grid=(4, r // rb),
            in_specs=[pl.BlockSpec((None, None, rb, cc), lambda k, i, ck_ref: (ck_ref[0], k, i, 0)),
                      pl.BlockSpec((None, rb, cc), lambda k, i, ck_ref: (k, i, 0))],
            out_specs=pl.BlockSpec((None, rb, cc), lambda k, i, ck_ref: (k, i, 0))),
        out_shape=jax.ShapeDtypeStruct(b.shape, BF16),
        compiler_params=_cparams(2),
    )(ck, g, b)


def _add_rows_call(name, g, b, ck):
    _, r, cc = g.shape
    rb = _row_block(r, cc, 2 * 1024 * 1024)

    def body(ck_ref, g_ref, b_ref, o_ref):
        o_ref[...] = (g_ref[...] + b_ref[...]).astype(BF16)

    return pl.pallas_call(
        body, name=name,
        grid_spec=pltpu.PrefetchScalarGridSpec(
            num_scalar_prefetch=1, grid=(r // rb,),
            in_specs=[pl.BlockSpec((None, rb, cc), lambda i, ck_ref: (ck_ref[0], i, 0)),
                      pl.BlockSpec((rb, cc), lambda i, ck_ref: (i, 0))],
            out_specs=pl.BlockSpec((rb, cc), lambda i, ck_ref: (i, 0))),
        out_shape=jax.ShapeDtypeStruct((r, cc), BF16),
        compiler_params=_cparams(1),
    )(ck, g, b)


def _add_window_call(name, g, b, p, ck):
    _, r, _ = g.shape
    nb, step = WIN_W // 128, WIN_STEP // 128

    def body(ck_ref, g_ref, b_ref, p0_ref, p1_ref, p2_ref, o_ref):
        own = g_ref[...] + b_ref[...]
        o_ref[...] = ((own + p0_ref[...].astype(F32)) + p1_ref[...].astype(F32)) + p2_ref[...].astype(F32)

    def peer(j):
        return pl.BlockSpec((None, r, 128), lambda i, ck_ref: (j, 0, i))

    return pl.pallas_call(
        body, name=name,
        grid_spec=pltpu.PrefetchScalarGridSpec(
            num_scalar_prefetch=1, grid=(nb,),
            in_specs=[pl.BlockSpec((None, r, 128), lambda i, ck_ref: (ck_ref[0], 0, step * ck_ref[1] + i)),
                      pl.BlockSpec((r, 128), lambda i, ck_ref: (0, step * ck_ref[1] + i)),
                      peer(0), peer(1), peer(2)],
            out_specs=pl.BlockSpec((None, r, 128), lambda i, ck_ref: (ck_ref[0], 0, i))),
        out_shape=jax.ShapeDtypeStruct((2, r, WIN_W), F32),
        compiler_params=_cparams(1),
    )(ck, g, b, p, p, p)


def _add_chips_call(name, g, b, p, ck):
    _, _, r, cc = g.shape
    rb = _row_block(r, cc, 2 * 1024 * 1024)

    def body(ck_ref, g_ref, b_ref, p0_ref, p1_ref, p2_ref, o_ref):
        own = g_ref[...] + b_ref[...]
        o_ref[...] = ((own + p0_ref[...].astype(F32)) + p1_ref[...].astype(F32)) + p2_ref[...].astype(F32)

    def peer(j):
        return pl.BlockSpec((None, rb, cc), lambda i, ck_ref: (j, i, 0))

    return pl.pallas_call(
        body, name=name,
        grid_spec=pltpu.PrefetchScalarGridSpec(
            num_scalar_prefetch=1, grid=(r // rb,),
            in_specs=[pl.BlockSpec((None, None, rb, cc), lambda i, ck_ref: (ck_ref[0], ck_ref[1], i, 0)),
                      pl.BlockSpec((None, rb, cc), lambda i, ck_ref: (ck_ref[1], i, 0)),
                      peer(0), peer(1), peer(2)],
            out_specs=pl.BlockSpec((None, rb, cc), lambda i, ck_ref: (ck_ref[0], i, 0))),
        out_shape=jax.ShapeDtypeStruct((2, r, cc), F32),
        compiler_params=_cparams(1),
    )(ck, g, b, p, p, p)


def _sum8_call(name, g):
    def body(g_ref, o_ref):
        acc = g_ref[0]
        for d in range(1, 8):
            acc = acc + g_ref[d]
        o_ref[...] = acc

    return pl.pallas_call(body, name=name, out_shape=jax.ShapeDtypeStruct(g.shape[1:], F32))(g)


def _adamw_call(name, w, g, m, v):
    r, cc = w.shape
    if r % 8 == 0 or r * cc * 4 <= 1024 * 1024:
        rb = _row_block(r, cc, 1024 * 1024) if r % 8 == 0 else r
        grid, spec = (r // rb,), pl.BlockSpec((rb, cc), lambda i: (i, 0))
    else:
        grid, spec = (cc // 128,), pl.BlockSpec((r, 128), lambda i: (0, i))

    def body(w_ref, g_ref, m_ref, v_ref, d_ref, m2_ref, v2_ref):
        gv = g_ref[...]
        m2 = ADAM_B1 * m_ref[...] + (1.0 - ADAM_B1) * gv
        v2 = ADAM_B2 * v_ref[...] + (1.0 - ADAM_B2) * (gv * gv)
        m_hat = m2 / (1.0 - ADAM_B1 ** ADAM_STEP)
        v_hat = v2 / (1.0 - ADAM_B2 ** ADAM_STEP)
        d_ref[...] = -ADAM_LR * (m_hat / (jnp.sqrt(v_hat) + ADAM_EPS) + ADAM_WD * w_ref[...])
        m2_ref[...] = m2
        v2_ref[...] = v2

    return pl.pallas_call(
        body, name=name, grid=grid, in_specs=[spec] * 4, out_specs=[spec] * 3,
        out_shape=[jax.ShapeDtypeStruct((r, cc), F32)] * 3, compiler_params=_cparams(1),
    )(w, g, m, v)


SMALL = (("norm_gain", D_MODEL), ("b_gate", GLA_KW), ("ret_norm_gain", RET_W), ("gla_norm_gain", GLA_W),
         ("final_norm_gain", D_MODEL), ("w_gate_up", GATE_RANK * GLA_KW), ("meta_tokens", N_META * D_MODEL),
         ("loss", 1))


def _pack_rows(vecs, rows):
    flat = jnp.concatenate([v.reshape(-1) for v in vecs])
    return jnp.pad(flat, (0, rows * 128 - flat.shape[0])).reshape(rows, 128)


def kernel(x, meta_tokens, norm_gain, w_in, w_gate_up, b_gate, ret_norm_gain, gla_norm_gain, w_branch_ret, w_branch_gla, w_out, final_norm_gain, loss_target, m_meta_tokens, m_norm_gain, m_w_in, m_w_gate_up, m_b_gate, m_ret_norm_gain, m_gla_norm_gain, m_w_branch_ret, m_w_branch_gla, m_w_out, m_final_norm_gain, v_meta_tokens, v_norm_gain, v_w_in, v_w_gate_up, v_b_gate, v_ret_norm_gain, v_gla_norm_gain, v_w_branch_ret, v_w_branch_gla, v_w_out, v_final_norm_gain):
    xi, yi, ci = _place()
    kme = 2 * xi + yi
    ck = jnp.stack([ci, kme]).astype(jnp.int32)
    sw_in = w_in.shape[2]

    def my_half(a, dtype):
        r, cc = a.shape
        return lax.dynamic_index_in_dim(a.reshape(2, r // 2, cc), ci, 0, keepdims=False).astype(dtype)

    g_meta, g_wg = _gather8_call("gather_small_weights", [my_half(meta_tokens, F32), my_half(w_gate_up[0], F32)])
    branch_parts = [my_half(w_branch_ret[0], BF16), my_half(w_branch_gla[0], BF16), my_half(w_out[0], BF16)]
    meta = g_meta.reshape(4, 2, N_META // 2, D_MODEL // 4).transpose(1, 2, 0, 3).reshape(N_META, D_MODEL)
    wg_full = g_wg.reshape(4, 2, GATE_RANK // 2, GLA_KW // 4).transpose(1, 2, 0, 3).reshape(GATE_RANK, GLA_KW)

    loc = _device_step(x[0], loss_target[0], meta, norm_gain, my_half(w_in[0], BF16), wg_full, b_gate, ret_norm_gain,
                       gla_norm_gain,
                       branch_parts, final_norm_gain, ck)
    names = ("w_in", "w_branch_ret", "w_branch_gla", "w_out")
    full = [loc[nm] for nm in names]
    big_w = dict(w_in=w_in[0], w_branch_ret=w_branch_ret[0], w_branch_gla=w_branch_gla[0], w_out=w_out[0])
    big_m = dict(w_in=m_w_in[0], w_branch_ret=m_w_branch_ret[0], w_branch_gla=m_w_branch_gla[0], w_out=m_w_out[0])
    big_v = dict(w_in=v_w_in[0], w_branch_ret=v_w_branch_ret[0], w_branch_gla=v_w_branch_gla[0], w_out=v_w_out[0])
    grads, deltas, new_m, new_v = {}, {}, {}, {}
    for nm, f in zip(names, full):
        shape = big_w[nm].shape
        if nm == "w_in":
            f = lax.dynamic_slice_in_dim(f, (sw_in - WIN_STEP) * kme, sw_in, axis=2)
        g = f.reshape(shape)
        if nm == "w_in":
            d, m2, v2 = (a.T for a in _adamw_call("adamw_" + nm, big_w[nm].T, g.T, big_m[nm].T, big_v[nm].T))
        else:
            d, m2, v2 = _adamw_call("adamw_" + nm, big_w[nm], g, big_m[nm], big_v[nm])
        grads[nm], deltas[nm], new_m[nm], new_v[nm] = (a.reshape((1,) + shape) for a in (g, d, m2, v2))

    small_g = dict(loc)
    small_g["meta_tokens"] = loc["dmeta"]
    n_small = sum(sz for _, sz in SMALL)
    rows = -(-n_small // 128 // 8) * 8
    (g_small,) = _gather8_call("gather_small_grads", [_pack_rows([small_g[nm] for nm, _ in SMALL], rows)])
    tot = _sum8_call("sum_small_grads", g_small).reshape(-1)
    off = 0
    sg = {}
    for nm, sz in SMALL:
        sg[nm] = tot[off:off + sz]
        off += sz
    loss = sg.pop("loss")[0]
    sg["w_gate_up"] = lax.dynamic_slice_in_dim(sg["w_gate_up"].reshape(GATE_RANK, GLA_KW), kme * (GLA_KW // 4),
                                               GLA_KW // 4, axis=1)
    sg["meta_tokens"] = lax.dynamic_slice_in_dim(sg["meta_tokens"].reshape(N_META, D_MODEL), kme * (D_MODEL // 4),
                                                 D_MODEL // 4, axis=1)
    small_w = dict(norm_gain=norm_gain, b_gate=b_gate, ret_norm_gain=ret_norm_gain, gla_norm_gain=gla_norm_gain,
                   final_norm_gain=final_norm_gain, w_gate_up=w_gate_up, meta_tokens=meta_tokens)
    small_m = dict(norm_gain=m_norm_gain, b_gate=m_b_gate, ret_norm_gain=m_ret_norm_gain,
                   gla_norm_gain=m_gla_norm_gain, final_norm_gain=m_final_norm_gain, w_gate_up=m_w_gate_up,
                   meta_tokens=m_meta_tokens)
    small_v = dict(norm_gain=v_norm_gain, b_gate=v_b_gate, ret_norm_gain=v_ret_norm_gain,
                   gla_norm_gain=v_gla_norm_gain, final_norm_gain=v_final_norm_gain, w_gate_up=v_w_gate_up,
                   meta_tokens=v_meta_tokens)
    for nm in small_w:
        shape = small_w[nm].shape
        as2d = lambda a: a.reshape((-1, shape[-1]))
        grads[nm] = sg[nm].reshape(shape)
        deltas[nm], new_m[nm], new_v[nm] = (a.reshape(shape) for a in _adamw_call(
            "adamw_" + nm, as2d(small_w[nm]), as2d(sg[nm]), as2d(small_m[nm]), as2d(small_v[nm])))

    out_order = ("meta_tokens", "norm_gain", "w_in", "w_gate_up", "b_gate", "ret_norm_gain", "gla_norm_gain",
                 "w_branch_ret", "w_branch_gla", "w_out", "final_norm_gain")
    dx = loc["dx"].reshape(x.shape)
    return (loss, dx, *[grads[nm] for nm in out_order], *[deltas[nm] for nm in out_order],
            *[new_m[nm] for nm in out_order], *[new_v[nm] for nm in out_order])
```

```python
import math
from typing import Callable, NamedTuple

import numpy as np
import jax
import jax.numpy as jnp
from jax import lax
from jax.experimental import pallas as pl
from jax.experimental.pallas import tpu as pltpu

F32 = jnp.float32
BF16 = jnp.bfloat16

D_MODEL = 1024
N_META = 16
EPS = 1e-6
ROPE_BASE = 10000.0
RET_HEADS, RET_QK, RET_V = 4, 256, 512
RET_W = RET_HEADS * RET_V
GLA_HEADS, GLA_K, GLA_V = 4, 128, 256
GLA_W = GLA_HEADS * GLA_V
GLA_KW = GLA_HEADS * GLA_K
GATE_RANK = 16
GATE_TAU = 16.0
GLA_SUB = 16

TM = 256
T0 = TM
PADF = T0 - N_META
GC = 128
GS = 3
TB = 768
TK = 768

W_R = 6144
W_G = 3088
W_GP = 3200
W_M = 2048
IN_COLS = W_R + W_G + W_M
WIN_STEP = (IN_COLS // 4) // 128 * 128
WIN_W = -(-(3 * (IN_COLS // 4 - WIN_STEP) + IN_COLS // 4) // 128) * 128
IN_PAD = 3 * WIN_STEP + WIN_W

ADAM_LR, ADAM_B1, ADAM_B2, ADAM_EPS, ADAM_WD, ADAM_STEP = 0.001, 0.9, 0.999, 1e-08, 0.01, 10

VMEM_LIMIT = 56 * 1024 * 1024

NN = ((1,), (0,))
NT = ((1,), (1,))
TN = ((0,), (0,))


def _dot(a, b, dims):
    return lax.dot_general(a, b, (dims, ((), ())), preferred_element_type=F32)


def _cparams(n_axes):
    return pltpu.CompilerParams(dimension_semantics=("arbitrary",) * n_axes, vmem_limit_bytes=VMEM_LIMIT)


def _sigmoid(x):
    return 0.5 * jnp.tanh(0.5 * x) + 0.5


def _silu(x):
    h = 0.5 * x
    return h + h * jnp.tanh(h)


def _head_mean(x):
    return jnp.mean(x, axis=-1, keepdims=True)


def _split3(x):
    hi = x.astype(BF16)
    r1 = x - hi.astype(F32)
    mid = r1.astype(BF16)
    lo = (r1 - mid.astype(F32)).astype(BF16)
    return hi, mid, lo


def _exact_pm(p, x):
    hi, mid, lo = _split3(x)
    return _dot(p, hi, NN) + _dot(p, mid, NN) + _dot(p, lo, NN)


def _rms_call(x2d, head, gain, comm):
    tp = T0 + x2d.shape[0]
    nt = tp // TM
    n_xc = len(comm.srcs)

    def body(x_ref, hd_ref, g_ref, *rest):
        xc_src = rest[:n_xc]
        h_ref, u_ref = rest[n_xc:n_xc + 2]
        xc_dst = rest[n_xc + 2:2 * n_xc + 2]
        i = pl.program_id(0)
        begin, finish = comm.make(xc_src, xc_dst, rest[-2], rest[-1])
        pl.when(i == 0)(begin)
        h = jnp.where(i == 0, hd_ref[...], x_ref[...])
        h_ref[...] = h
        r = lax.rsqrt(jnp.mean(h * h, axis=-1, keepdims=True) + EPS)
        u_ref[...] = (h * r * g_ref[...]).astype(BF16)
        pl.when(i == nt - 1)(finish)

    tile = pl.BlockSpec((TM, D_MODEL), lambda i: (i, 0))
    return pl.pallas_call(
        body, name="rms_in", grid=(nt,),
        in_specs=[pl.BlockSpec((TM, D_MODEL), lambda i: (jnp.maximum(i - 1, 0), 0)),
                  pl.BlockSpec((T0, D_MODEL), lambda i: (0, 0)), pl.BlockSpec((1, D_MODEL), lambda i: (0, 0))]
        + [ANY] * n_xc,
        out_specs=[tile, tile] + [ANY] * n_xc,
        out_shape=[jax.ShapeDtypeStruct((tp, D_MODEL), F32), jax.ShapeDtypeStruct((tp, D_MODEL), BF16)]
        + list(comm.out_shapes),
        scratch_shapes=_comm_sems(comm), compiler_params=_cparams(1),
    )(x2d, head, gain, *comm.srcs)


PROJ_ROWS_MAX = 1408


def _proj_rows(m):
    return max(r for r in range(16, PROJ_ROWS_MAX + 1, 16) if m % r == 0)


def _mm_nn(name, a, b, out_dtype, tn, col0, ncols, epilogue=None, extras=(), extra_specs=()):
    m, k = a.shape
    nj, j0 = ncols // tn, col0 // tn
    tb = _proj_rows(m)

    def body(a_ref, b_ref, *rest):
        *ex, o_ref = rest
        acc = _dot(a_ref[...], b_ref[...], NN)
        if epilogue is None:
            o_ref[...] = acc.astype(out_dtype)
        else:
            epilogue(acc, o_ref, *ex)

    return pl.pallas_call(
        body, name=name, grid=(nj, m // tb),
        in_specs=[pl.BlockSpec((tb, k), lambda j, i: (i, 0)), pl.BlockSpec((k, tn), lambda j, i: (0, j0 + j))]
        + list(extra_specs),
        out_specs=pl.BlockSpec((tb, tn), lambda j, i: (i, j)),
        out_shape=jax.ShapeDtypeStruct((m, ncols), out_dtype),
        compiler_params=_cparams(2),
    )(a, b, *extras)


def _rope_tables(tp):
    half = RET_QK // 2
    pos = np.arange(tp, dtype=np.float32) - np.float32(PADF)
    inv = (ROPE_BASE ** (-np.arange(half, dtype=np.float64) / half)).astype(np.float32)
    ang = (pos[:, None] * inv[None, :]).astype(np.float64)
    return np.cos(ang).astype(np.float32), np.sin(ang).astype(np.float32)


def _rope_epilogue(acc, o_ref, cos_ref, sin_ref):
    scale = jnp.where(pl.program_id(0) == 1, RET_QK ** -0.5, 1.0).astype(F32)
    cos, sin = cos_ref[...], sin_ref[...]
    half = RET_QK // 2
    for h in range(RET_HEADS):
        t1 = acc[:, h * RET_QK:h * RET_QK + half]
        t2 = acc[:, h * RET_QK + half:(h + 1) * RET_QK]
        o_ref[:, h * RET_QK:h * RET_QK + half] = ((t1 * cos - t2 * sin) * scale).astype(BF16)
        o_ref[:, h * RET_QK + half:(h + 1) * RET_QK] = ((t2 * cos + t1 * sin) * scale).astype(BF16)


def _gqk_epilogue(acc, o_ref):
    o_ref[:, :GLA_KW] = acc[:, :GLA_KW] * (GLA_K ** -0.5)
    o_ref[:, GLA_KW:] = acc[:, GLA_KW:]


class _Comm(NamedTuple):
    srcs: tuple
    out_shapes: tuple
    n_sems: int
    make: Callable


def _comm_sems(comm):
    return [pltpu.SemaphoreType.DMA((comm.n_sems,)), pltpu.SemaphoreType.DMA((comm.n_sems,))]


def _start_wait(copies):
    def begin():
        for cp in copies:
            cp.start()

    def finish():
        for cp in copies:
            cp.wait()

    return begin, finish


def _other_chips(x, y):
    return [(1 - x, y), (x, 1 - y), (1 - x, 1 - y)]


def _gather_plan(parts, relay=()):
    n = len(parts)
    relay = tuple(relay) + (False,) * (n - len(relay))

    def make(x_refs, out_refs, send_sems, recv_sems):
        x, y, c = _place()
        me, sibling = (x, y, c), (x, y, 1 - c)
        xn, yn, dg = (1 - x, y), (x, 1 - y), (1 - x, 1 - y)

        def slot(t, px, py, pc, half=None):
            ref = out_refs[t].at[4 * px + 2 * py + pc]
            if half is None:
                return ref
            rows = ref.shape[0] // 2
            return ref.at[pl.ds(half * rows, rows)]

        def copy(t, k, dst, to, src=None):
            return pltpu.make_async_remote_copy(
                src_ref=dst if src is None else src, dst_ref=dst, send_sem=send_sems.at[8 * t + k],
                recv_sem=recv_sems.at[8 * t + k], device_id=to, device_id_type=MESH)

        mine = [pltpu.make_async_copy(x_refs[t], slot(t, *me), send_sems.at[8 * n + t]) for t in range(n)]
        sent = []
        for t in range(n):
            sent.append(copy(t, 0, slot(t, *me), sibling, src=x_refs[t]))
            sent.append(copy(t, 1, slot(t, *me), (*xn, c), src=x_refs[t]))
            sent.append(copy(t, 2, slot(t, *me), (*yn, c), src=x_refs[t]))
            if not relay[t]:
                sent.append(copy(t, 3, slot(t, *me), (*dg, c), src=x_refs[t]))

        def begin():
            for cp in mine + sent:
                cp.start()

        def finish():
            later = []

            def start(cp):
                cp.start()
                later.append(cp)

            for t in range(n):
                copy(t, 2, slot(t, *yn, c), me).wait_recv()
                if relay[t]:
                    start(copy(t, 3, slot(t, *yn, c, half=0), (*xn, c)))
                start(copy(t, 6, slot(t, *yn, c), sibling))
            for t in range(n):
                copy(t, 1, slot(t, *xn, c), me).wait_recv()
                if relay[t]:
                    start(copy(t, 4, slot(t, *xn, c, half=1), (*yn, c)))
                start(copy(t, 5, slot(t, *xn, c), sibling))
            for t in range(n):
                if relay[t]:
                    copy(t, 3, slot(t, *dg, c, half=0), me).wait_recv()
                    copy(t, 4, slot(t, *dg, c, half=1), me).wait_recv()
                else:
                    copy(t, 3, slot(t, *dg, c), me).wait_recv()
                start(copy(t, 7, slot(t, *dg, c), sibling))
            for t in range(n):
                copy(t, 0, slot(t, *sibling), me).wait_recv()
                copy(t, 5, slot(t, *xn, 1 - c), me).wait_recv()
                copy(t, 6, slot(t, *yn, 1 - c), me).wait_recv()
                copy(t, 7, slot(t, *dg, 1 - c), me).wait_recv()
            for cp in sent + later:
                cp.wait_send()
            for cp in mine:
                cp.wait()

        return begin, finish

    return _Comm(tuple(parts), tuple(jax.ShapeDtypeStruct((8,) + p.shape, p.dtype) for p in parts), 9 * n, make)


def _exchange_plan(ss):
    def make(s_refs, b_refs, send_sems, recv_sems):
        x, y, c = _place()
        return _start_wait([pltpu.make_async_remote_copy(
            src_ref=s_refs[t].at[2 * chip[0] + chip[1]], dst_ref=b_refs[t].at[j], send_sem=send_sems.at[3 * t + j],
            recv_sem=recv_sems.at[3 * t + j], device_id=(*chip, c), device_id_type=MESH)
            for t in range(len(s_refs)) for j, chip in enumerate(_other_chips(x, y))])

    return _Comm(tuple(ss), tuple(jax.ShapeDtypeStruct((3,) + s.shape[1:], s.dtype) for s in ss), 3 * len(ss), make)


def _exchange_window_plan(s):
    def make(s_refs, b_refs, send_sems, recv_sems):
        x, y, c = _place()
        return _start_wait([pltpu.make_async_remote_copy(
            src_ref=s_refs[0].at[:, pl.ds(pl.multiple_of((2 * chip[0] + chip[1]) * WIN_STEP, 128), WIN_W)],
            dst_ref=b_refs[0].at[j], send_sem=send_sems.at[j], recv_sem=recv_sems.at[j], device_id=(*chip, c),
            device_id_type=MESH) for j, chip in enumerate(_other_chips(x, y))])

    return _Comm((s,), (jax.ShapeDtypeStruct((3, s.shape[0], WIN_W), s.dtype),), 3, make)


def _swap_plan(gs):
    def make(g_refs, b_refs, send_sems, recv_sems):
        x, y, c = _place()
        return _start_wait([pltpu.make_async_remote_copy(
            src_ref=g_refs[t].at[1 - c], dst_ref=b_refs[t], send_sem=send_sems.at[t], recv_sem=recv_sems.at[t],
            device_id=(x, y, 1 - c), device_id_type=MESH) for t in range(len(g_refs))])

    return _Comm(tuple(gs), tuple(jax.ShapeDtypeStruct(g.shape[1:], g.dtype) for g in gs), len(gs), make)


def _spread_plan(parts):
    def make(p_refs, o_refs, send_sems, recv_sems):
        x, y, c = _place()
        copies = []
        for t in range(len(p_refs)):
            mine = o_refs[t].at[4 * x + 2 * y + c]
            copies.append(pltpu.make_async_copy(p_refs[t], mine, send_sems.at[7 * len(p_refs) + t]))
            for r in range(1, 8):
                peer = (1 - x if r & 4 else x, 1 - y if r & 2 else y, 1 - c if r & 1 else c)
                copies.append(pltpu.make_async_remote_copy(
                    src_ref=p_refs[t], dst_ref=mine, send_sem=send_sems.at[7 * t + r - 1],
                    recv_sem=recv_sems.at[7 * t + r - 1], device_id=peer, device_id_type=MESH))
        return _start_wait(copies)

    return _Comm(tuple(parts), tuple(jax.ShapeDtypeStruct((8,) + p.shape, p.dtype) for p in parts), 8 * len(parts),
                 make)


def _mm_nt_acc(name, a, w, tk, acc_in=None, epilogue=None, extras=(), extra_specs=(), extra_out_shapes=(),
               extra_out_specs=(), extra_scratch=(), comm=None, tb=TB):
    m, k = a.shape
    n = w.shape[0]
    nk, ni = k // tk, m // tb
    has_acc = acc_in is not None
    n_xc = len(comm.srcs) if comm else 0
    n_es = len(extra_scratch)

    def body(*refs):
        a_ref, w_ref = refs[0], refs[1]
        pos = 2
        acc_ref = None
        if has_acc:
            acc_ref = refs[pos]
            pos += 1
        ex = refs[pos:pos + len(extras)]
        pos += len(extras)
        xc_src = refs[pos:pos + n_xc]
        pos += n_xc
        n_scr = 1 + n_es + (2 if n_xc else 0)
        outs = refs[pos:len(refs) - n_scr - n_xc]
        xc_dst = refs[len(refs) - n_scr - n_xc:len(refs) - n_scr]
        scr = refs[len(refs) - n_scr]
        es = refs[len(refs) - n_scr + 1:len(refs) - n_scr + 1 + n_es]
        i, kk = pl.program_id(0), pl.program_id(1)
        if n_xc:
            begin, finish = comm.make(xc_src, xc_dst, refs[-2], refs[-1])
            pl.when((i == 0) & (kk == 0))(begin)

        @pl.when(kk == 0)
        def _():
            scr[...] = acc_ref[...] if has_acc else jnp.zeros_like(scr)

        scr[...] += _dot(a_ref[...], w_ref[...], NT)

        @pl.when(kk == nk - 1)
        def _():
            if epilogue is None:
                outs[0][...] = scr[...]
            else:
                epilogue(scr[...], outs, i, ni, *ex, *es)

        if n_xc:
            pl.when((i == ni - 1) & (kk == nk - 1))(finish)

    in_specs = [pl.BlockSpec((tb, tk), lambda i, kk: (i, kk)), pl.BlockSpec((n, tk), lambda i, kk: (0, kk))]
    args = [a, w]
    if has_acc:
        in_specs.append(pl.BlockSpec((tb, n), lambda i, kk: (i, 0)))
        args.append(acc_in)
    in_specs += list(extra_specs) + [ANY] * n_xc
    args += list(extras) + (list(comm.srcs) if comm else [])
    if epilogue is None:
        out_shape = [jax.ShapeDtypeStruct((m, n), F32)]
        out_specs = [pl.BlockSpec((tb, n), lambda i, kk: (i, 0))]
    else:
        out_shape, out_specs = list(extra_out_shapes), list(extra_out_specs)
    scratch = [pltpu.VMEM((tb, n), F32)] + list(extra_scratch)
    if n_xc:
        out_shape += list(comm.out_shapes)
        out_specs += [ANY] * n_xc
        scratch += _comm_sems(comm)
    return pl.pallas_call(
        body, name=name, grid=(ni, nk), in_specs=in_specs, out_specs=out_specs, out_shape=out_shape,
        scratch_shapes=scratch, compiler_params=_cparams(2),
    )(*args)


def _rms_bwd_epilogue(du, outs, i, ni, h_ref, g_ref, dh1_ref, obuf, sems):
    dx_ref, dmeta_ref, dg_ref = outs
    h = h_ref[...]
    r = lax.rsqrt(jnp.mean(h * h, axis=-1, keepdims=True) + EPS)
    xh = h * r
    dxh = du * g_ref[...]
    dh0 = dh1_ref[...] + r * (dxh - xh * jnp.mean(dxh * xh, axis=-1, keepdims=True))

    def put(slot, tile):
        return pltpu.make_async_copy(obuf.at[slot], dx_ref.at[pl.ds(pl.multiple_of(tile * TB - T0, 8), TB)],
                                     sems.at[slot])

    @pl.when(i == 0)
    def _():
        dg_ref[...] = jnp.zeros_like(dg_ref)
        dmeta_ref[...] = dh0[PADF:T0, :]
        obuf[0] = dh0
        first = pltpu.make_async_copy(obuf.at[0, pl.ds(T0, TB - T0)], dx_ref.at[pl.ds(0, TB - T0)], sems.at[0])
        first.start()
        first.wait()

    @pl.when(i >= 1)
    def _():
        slot = i % 2

        @pl.when(i >= 3)
        def _():
            put(slot, i - 2).wait()

        obuf[slot] = dh0
        put(slot, i).start()

    dg_ref[...] += jnp.sum(du * xh, axis=0, keepdims=True)

    @pl.when(i == ni - 1)
    def _():
        for tile in (ni - 2, ni - 1):
            if tile >= 1:
                put(tile % 2, tile).wait()


def _mm_tn(name, a, b, bn, ncols=None, bcol0=0, into=None, col0=0, out_cols=None):
    t, m = a.shape
    n = ncols or b.shape[1]
    j0, bj0 = col0 // bn, bcol0 // bn

    def body(a_ref, b_ref, *rest):
        o_ref = rest[-1]

        @pl.when(pl.program_id(1) == 0)
        def _():
            o_ref[...] = jnp.zeros_like(o_ref)

        o_ref[...] += _dot(a_ref[...], b_ref[...], TN)

    in_specs = [pl.BlockSpec((TK, m), lambda j, kk: (kk, 0)), pl.BlockSpec((TK, bn), lambda j, kk: (kk, bj0 + j))]
    args = [a, b]
    aliases = {}
    if into is not None:
        in_specs.append(ANY)
        args.append(into)
        aliases = {2: 0}
        out_cols = into.shape[1]
    return pl.pallas_call(
        body, name=name, grid=(n // bn, t // TK), in_specs=in_specs,
        out_specs=pl.BlockSpec((m, bn), lambda j, kk: (0, j0 + j)),
        out_shape=jax.ShapeDtypeStruct((m, out_cols or n), F32), input_output_aliases=aliases,
        compiler_params=_cparams(2),
    )(*args)


def _place_merge_cols_call(dwp, dw_m):
    c0 = W_R + W_GP - 128
    tail = IN_PAD - c0
    rows = 256

    def body(m_ref, p_ref, o_ref, buf, low, sem):
        get = pltpu.make_async_copy(o_ref.at[:, pl.ds(c0, 128)], low, sem)
        get.start()
        get.wait()
        for r in range(0, D_MODEL, rows):
            buf[r:r + rows, :] = jnp.concatenate(
                [low[r:r + rows, :GATE_RANK], m_ref[r:r + rows, :],
                 jnp.zeros((rows, tail - GATE_RANK - W_M), F32)], axis=1)
        put = pltpu.make_async_copy(buf, o_ref.at[:, pl.ds(c0, tail)], sem)
        put.start()
        put.wait()

    return pl.pallas_call(
        body, name="place_merge_cols",
        in_specs=[pl.BlockSpec(memory_space=pltpu.VMEM), ANY], out_specs=ANY,
        out_shape=jax.ShapeDtypeStruct(dwp.shape, F32), input_output_aliases={1: 0},
        scratch_shapes=[pltpu.VMEM((D_MODEL, tail), F32), pltpu.VMEM((D_MODEL, 128), F32), pltpu.SemaphoreType.DMA],
        compiler_params=pltpu.CompilerParams(vmem_limit_bytes=VMEM_LIMIT),
    )(dw_m, dwp)


def _ret_fill_decay(lg_ref, dm_scr):
    c = TM
    ii = lax.broadcasted_iota(jnp.int32, (c, c), 0)
    jj = lax.broadcasted_iota(jnp.int32, (c, c), 1)
    rel = (ii - jj).astype(F32)
    for h in range(RET_HEADS):
        dm_scr[h] = jnp.where(rel >= 0, jnp.exp(jnp.maximum(rel, 0.0) * lg_ref[h]), 0.0)


def _ret_consts(lg, dm_ref):
    c = TM
    idx = lax.broadcasted_iota(jnp.int32, (c, 1), 0).astype(F32)
    xi = jnp.exp((idx + 1.0) * lg)
    zeta = jnp.exp((c - 1.0 - idx) * lg)
    gc = jnp.exp(jnp.full((1, 1), c, F32) * lg)
    return dm_ref[...], xi, zeta, gc


def _ret_fwd_call(rqk, rv, rg, gain, lgam):
    tp = rqk.shape[0]
    nc = tp // TM

    def body(lg_ref, qk_ref, v_ref, rg_ref, g_ref, o_ref, a_ref, st_ref, sc_ref, s_scr, dm_scr):
        @pl.when(pl.program_id(0) == 0)
        def _():
            s_scr[...] = jnp.zeros_like(s_scr)
            _ret_fill_decay(lg_ref, dm_scr)

        for h in range(RET_HEADS):
            dm, xi, zeta, gc = _ret_consts(lg_ref[h], dm_scr.at[h])
            q = qk_ref[:, h * RET_QK:(h + 1) * RET_QK]
            k = qk_ref[:, D_MODEL + h * RET_QK:D_MODEL + (h + 1) * RET_QK]
            v = v_ref[:, h * RET_V:(h + 1) * RET_V]
            sb = s_scr[h].astype(BF16)
            st_ref[0, h] = sb
            s = (_dot(q, k, NT) * dm).astype(BF16)
            sc_ref[0, h] = s
            o = _dot(s, v, NN) + xi * _dot(q, sb, NN)
            kz = (k.astype(F32) * zeta).astype(BF16)
            s_scr[h] = gc * s_scr[h] + _dot(kz, v, TN)
            o_ref[:, h * RET_V:(h + 1) * RET_V] = o
            mu = _head_mean(o)
            xc = o - mu
            xh = xc * lax.rsqrt(_head_mean(xc * xc) + EPS)
            a_ref[:, h * RET_V:(h + 1) * RET_V] = (
                xh * g_ref[:, h * RET_V:(h + 1) * RET_V] * _silu(rg_ref[:, h * RET_V:(h + 1) * RET_V])).astype(BF16)

    return pl.pallas_call(
        body, name="ret_fwd", grid=(nc,),
        in_specs=[pl.BlockSpec(memory_space=pltpu.SMEM),
                  pl.BlockSpec((TM, 2 * D_MODEL), lambda n: (n, 0)),
                  pl.BlockSpec((TM, RET_W), lambda n: (n, 0)),
                  pl.BlockSpec((TM, RET_W), lambda n: (n, 0)),
                  pl.BlockSpec((1, RET_W), lambda n: (0, 0))],
        out_specs=[pl.BlockSpec((TM, RET_W), lambda n: (n, 0)),
                   pl.BlockSpec((TM, RET_W), lambda n: (n, 0)),
                   pl.BlockSpec((1, RET_HEADS, RET_QK, RET_V), lambda n: (n, 0, 0, 0)),
                   pl.BlockSpec((1, RET_HEADS, TM, TM), lambda n: (n, 0, 0, 0))],
        out_shape=[jax.ShapeDtypeStruct((tp, RET_W), F32), jax.ShapeDtypeStruct((tp, RET_W), BF16),
                   jax.ShapeDtypeStruct((nc, RET_HEADS, RET_QK, RET_V), BF16),
                   jax.ShapeDtypeStruct((nc, RET_HEADS, TM, TM), BF16)],
        scratch_shapes=[pltpu.VMEM((RET_HEADS, RET_QK, RET_V), F32), pltpu.VMEM((RET_HEADS, TM, TM), F32)],
        compiler_params=_cparams(1),
    )(lgam, rqk, rv, rg, gain)


def _ret_bwd_call(rqk, rv, rg, o_ret, dpr, wbr, states, scores, gain, lgam, cos, sin):
    tp = rqk.shape[0]
    nc = tp // TM
    half = RET_QK // 2

    def body(lg_ref, qk_ref, v_ref, rg_ref, o_ref, dpr_ref, wbr_ref, st_ref, sc_ref, g_ref, cos_ref, sin_ref, dp_ref,
             dg_ref, ds_scr, dm_scr):
        @pl.when(pl.program_id(0) == 0)
        def _():
            ds_scr[...] = jnp.zeros_like(ds_scr)
            dg_ref[...] = jnp.zeros_like(dg_ref)
            _ret_fill_decay(lg_ref, dm_scr)

        cos, sin = cos_ref[...], sin_ref[...]
        for h in range(RET_HEADS):
            hs = slice(h * RET_V, (h + 1) * RET_V)
            dm, xi, zeta, gc = _ret_consts(lg_ref[h], dm_scr.at[h])
            o = o_ref[:, hs]
            mu = _head_mean(o)
            xc = o - mu
            rstd = lax.rsqrt(_head_mean(xc * xc) + EPS)
            xh = xc * rstd
            gain_h = g_ref[:, hs]
            g = rg_ref[:, hs]
            sg = _sigmoid(g)
            silu = g * sg
            dah = _dot(dpr_ref[...], wbr_ref[hs, :], NT)
            dp_ref[:, 4 * D_MODEL + h * RET_V:4 * D_MODEL + (h + 1) * RET_V] = (
                dah * (xh * gain_h) * (sg * (1.0 + g * (1.0 - sg)))).astype(BF16)
            dn = dah * silu
            dg_ref[:, hs] += jnp.sum(dn * xh, axis=0, keepdims=True)
            dxh = dn * gain_h
            do = rstd * (dxh - _head_mean(dxh) - xh * _head_mean(dxh * xh))
            dob = do.astype(BF16)
            q = qk_ref[:, h * RET_QK:(h + 1) * RET_QK]
            k = qk_ref[:, D_MODEL + h * RET_QK:D_MODEL + (h + 1) * RET_QK]
            v = v_ref[:, hs]
            sp = st_ref[0, h]
            ds = ds_scr[h]
            dsb = ds.astype(BF16)
            s = sc_ref[0, h]
            dsc = (_dot(dob, v, NT) * dm).astype(BF16)
            dq = _dot(dsc, k, NN) + xi * _dot(dob, sp, NT)
            dk = _dot(dsc, q, TN) + zeta * _dot(v, dsb, NT)
            kz = (k.astype(F32) * zeta).astype(BF16)
            dv = _dot(s, dob, TN) + _dot(kz, dsb, NN)
            qx = (q.astype(F32) * xi).astype(BF16)
            ds_scr[h] = gc * ds + _dot(qx, dob, TN)
            dp_ref[:, 2 * D_MODEL + h * RET_V:2 * D_MODEL + (h + 1) * RET_V] = dv.astype(BF16)
            dk = dk * (RET_QK ** -0.5)
            for base, t in ((0, dq), (D_MODEL, dk)):
                t1, t2 = t[:, :half], t[:, half:]
                dp_ref[:, base + h * RET_QK:base + h * RET_QK + half] = (t1 * cos + t2 * sin).astype(BF16)
                dp_ref[:, base + h * RET_QK + half:base + (h + 1) * RET_QK] = (t2 * cos - t1 * sin).astype(BF16)

    rev = lambda n: (nc - 1 - n, 0)
    return pl.pallas_call(
        body, name="ret_bwd", grid=(nc,),
        in_specs=[pl.BlockSpec(memory_space=pltpu.SMEM),
                  pl.BlockSpec((TM, 2 * D_MODEL), rev),
                  pl.BlockSpec((TM, RET_W), rev),
                  pl.BlockSpec((TM, RET_W), rev),
                  pl.BlockSpec((TM, RET_W), rev),
                  pl.BlockSpec((TM, D_MODEL), rev),
                  pl.BlockSpec((RET_W, D_MODEL), lambda n: (0, 0)),
                  pl.BlockSpec((1, RET_HEADS, RET_QK, RET_V), lambda n: (nc - 1 - n, 0, 0, 0)),
                  pl.BlockSpec((1, RET_HEADS, TM, TM), lambda n: (nc - 1 - n, 0, 0, 0)),
                  pl.BlockSpec((1, RET_W), lambda n: (0, 0)),
                  pl.BlockSpec((TM, half), rev),
                  pl.BlockSpec((TM, half), rev)],
        out_specs=[pl.BlockSpec((TM, W_R), rev), pl.BlockSpec((1, RET_W), lambda n: (0, 0))],
        out_shape=[jax.ShapeDtypeStruct((tp, W_R), BF16), jax.ShapeDtypeStruct((1, RET_W), F32)],
        scratch_shapes=[pltpu.VMEM((RET_HEADS, RET_QK, RET_V), F32), pltpu.VMEM((RET_HEADS, TM, TM), F32)],
        compiler_params=_cparams(1),
    )(lgam, rqk, rv, rg, o_ret, dpr, wbr, states, scores, gain, cos, sin)


GLA_LEVELS = tuple(GC >> (s + 1) for s in range(int(math.log2(GC // GLA_SUB))))
NLEV = len(GLA_LEVELS)


def _gla_tril():
    return np.tril(np.ones((GC, GC), np.float32))


def _gla_masks():
    ii = lax.broadcasted_iota(jnp.int32, (GC, GC), 0)
    jj = lax.broadcasted_iota(jnp.int32, (GC, GC), 1)
    masks = []
    for m in GLA_LEVELS:
        sh = int(math.log2(2 * m))
        masks.append(((ii >> sh) == (jj >> sh)) & ((ii & m) != 0) & ((jj & m) == 0))
    sh = int(math.log2(GLA_SUB))
    md = ((ii >> sh) == (jj >> sh)) & (jj <= ii)
    row = lax.broadcasted_iota(jnp.int32, (GC, 1), 0)
    second = [(row & m) != 0 for m in GLA_LEVELS]
    return masks, md, second


def _gla_gate_call(glr, wg, bg, pmat):
    tp = glr.shape[0]
    gb = _proj_rows(tp)
    assert gb % GC == 0

    def body(glr_ref, wg_ref, bg_ref, p_ref, z_ref, b_ref):
        z = _dot(glr_ref[...].astype(BF16), wg_ref[...], NN) + bg_ref[...]
        z_ref[...] = z
        la = (jnp.minimum(z, 0.0) - jnp.log1p(jnp.exp(-jnp.abs(z)))) * (1.0 / GATE_TAU)
        for r in range(0, gb, GC):
            b_ref[r:r + GC, :] = _exact_pm(p_ref[...], la[r:r + GC, :])

    tile = pl.BlockSpec((gb, GLA_KW), lambda i: (i, 0))
    return pl.pallas_call(
        body, name="gla_gate", grid=(tp // gb,),
        in_specs=[pl.BlockSpec((gb, 128), lambda i: (i, 0)), pl.BlockSpec((128, GLA_KW), lambda i: (0, 0)),
                  pl.BlockSpec((1, GLA_KW), lambda i: (0, 0)), pl.BlockSpec((GC, GC), lambda i: (0, 0))],
        out_specs=[tile, tile],
        out_shape=[jax.ShapeDtypeStruct((tp, GLA_KW), F32), jax.ShapeDtypeStruct((tp, GLA_KW), F32)],
        compiler_params=_cparams(1),
    )(glr, wg, bg, pmat)


def _gla_gate_bwd_call(db, z, glr, wg, pmat_t, d_g):
    tp = db.shape[0]
    gb = _proj_rows(tp)
    assert gb % GC == 0 and (W_GP - 128) % 128 == 0

    def body(db_ref, z_ref, glr_ref, wg_ref, pt_ref, dgin_ref, dg_ref, dwg_ref, dbg_ref):
        i = pl.program_id(0)

        @pl.when(i == 0)
        def _():
            dwg_ref[...] = jnp.zeros_like(dwg_ref)
            dbg_ref[...] = jnp.zeros_like(dbg_ref)

        dla = jnp.concatenate([_exact_pm(pt_ref[...], db_ref[r:r + GC, :]) for r in range(0, gb, GC)], axis=0)
        row = i * gb + lax.broadcasted_iota(jnp.int32, (gb, 1), 0)
        dz = jnp.where(row >= PADF, dla * (1.0 / GATE_TAU) * _sigmoid(-z_ref[...]), 0.0)
        dzb = dz.astype(BF16)
        dg_ref[...] = _dot(dzb, wg_ref[...], NT).astype(BF16)
        dwg_ref[...] += _dot(glr_ref[...].astype(BF16), dzb, TN)
        dbg_ref[...] += jnp.sum(dz, axis=0, keepdims=True)

    tile = pl.BlockSpec((gb, GLA_KW), lambda i: (i, 0))
    const = lambda i: (0, 0)
    return pl.pallas_call(
        body, name="gla_gate_bwd", grid=(tp // gb,),
        in_specs=[tile, tile, pl.BlockSpec((gb, 128), lambda i: (i, 0)), pl.BlockSpec((128, GLA_KW), const),
                  pl.BlockSpec((GC, GC), const), ANY],
        out_specs=[pl.BlockSpec((gb, 128), lambda i: (i, (W_GP - 128) // 128)), pl.BlockSpec((128, GLA_KW), const),
                   pl.BlockSpec((1, GLA_KW), const)],
        out_shape=[jax.ShapeDtypeStruct(d_g.shape, BF16), jax.ShapeDtypeStruct((128, GLA_KW), F32),
                   jax.ShapeDtypeStruct((1, GLA_KW), F32)],
        input_output_aliases={5: 0}, compiler_params=_cparams(1),
    )(db, z, glr, wg, pmat_t, d_g)


def _gla_row_steps(b_ref, cs, rows, size):
    parts = [jnp.zeros((size, GLA_K), F32) if r is None else jnp.broadcast_to(b_ref[r:r + 1, cs], (size, GLA_K))
             for r in rows]
    return parts[0] if len(parts) == 1 else jnp.concatenate(parts, axis=0)


def _gla_factors(b_ref, h, second):
    cs = slice(h * GLA_K, (h + 1) * GLA_K)
    b = b_ref[:, cs]
    fq, fk = [], []
    for l, m in enumerate(GLA_LEVELS):
        d = b - _gla_row_steps(b_ref, cs, [s + m - 1 for s in range(0, GC, 2 * m)], 2 * m)
        f = jnp.exp(jnp.where(second[l], d, -d))
        fq.append(jnp.where(second[l], f, 0.0))
        fk.append(jnp.where(second[l], 0.0, f))
    dd = b - _gla_row_steps(b_ref, cs, [None] + [s - 1 for s in range(GLA_SUB, GC, GLA_SUB)], GLA_SUB)
    ed = jnp.exp(dd)
    edi = jnp.exp(-dd)
    eb = jnp.exp(b)
    bl = b_ref[GC - 1:GC, cs]
    ee = jnp.exp(bl - b)
    ebl = jnp.exp(bl)
    return fq, fk, ed, edi, eb, ee, ebl


def _gla_scaled(q, k, fq, fk, ed, edi):
    qt = [(q * f).astype(BF16) for f in fq]
    kt = [(k * f).astype(BF16) for f in fk]
    return qt, kt, (q * ed).astype(BF16), (k * edi).astype(BF16)


def _gla_scores(qt, kt, qd, kd, masks, md):
    a = jnp.where(md, _dot(qd, kd, NT), 0.0)
    for l in range(NLEV):
        a = a + jnp.where(masks[l], _dot(qt[l], kt[l], NT), 0.0)
    return a.astype(BF16)


def _gla_fwd_call(gqk, gv, b, gg, gain, comm=None):
    tp = gqk.shape[0]
    nc = tp // GC
    ns = nc // GS
    n_xc = len(comm.srcs) if comm else 0

    def body(qk_ref, v_ref, b_ref, gg_ref, g_ref, *rest):
        xc_src = rest[:n_xc]
        o_ref, a_ref, st_ref, am_ref = rest[n_xc:n_xc + 4]
        xc_dst = rest[n_xc + 4:2 * n_xc + 4]
        s_scr = rest[2 * n_xc + 4]
        n = pl.program_id(0)
        if n_xc:
            begin, finish = comm.make(xc_src, xc_dst, rest[-2], rest[-1])
            pl.when(n == 0)(begin)
            pl.when(n == ns - 1)(finish)

        @pl.when(n == 0)
        def _():
            s_scr[...] = jnp.zeros_like(s_scr)

        masks, md, second = _gla_masks()
        for cc in range(GS):
            rows = pl.ds(cc * GC, GC)
            qk_c, v_c, b_c, gg_c, o_c, a_c = (r.at[rows] for r in (qk_ref, v_ref, b_ref, gg_ref, o_ref, a_ref))
            for h in range(GLA_HEADS):
                q = qk_c[:, h * GLA_K:(h + 1) * GLA_K]
                k = qk_c[:, GLA_KW + h * GLA_K:GLA_KW + (h + 1) * GLA_K]
                vs = slice(h * GLA_V, (h + 1) * GLA_V)
                v = v_c[:, vs]
                fq, fk, ed, edi, eb, ee, ebl = _gla_factors(b_c, h, second)
                a = _gla_scores(*_gla_scaled(q, k, fq, fk, ed, edi), masks, md)
                am_ref[cc, h] = a
                sb = s_scr[h].astype(BF16)
                st_ref[cc, h] = sb
                o = _dot(a, v, NN) + _dot((q * eb).astype(BF16), sb, NT)
                s_scr[h] = s_scr[h] * ebl + _dot(v, (k * ee).astype(BF16), TN)
                o_c[:, vs] = o
                xh = o * lax.rsqrt(_head_mean(o * o) + EPS)
                a_c[:, vs] = (xh * g_ref[:, vs] * _silu(gg_c[:, vs])).astype(BF16)

    return pl.pallas_call(
        body, name="gla_fwd", grid=(ns,),
        in_specs=[pl.BlockSpec((GS * GC, 2 * GLA_KW), lambda n: (n, 0)),
                  pl.BlockSpec((GS * GC, GLA_W), lambda n: (n, 0)),
                  pl.BlockSpec((GS * GC, GLA_KW), lambda n: (n, 0)),
                  pl.BlockSpec((GS * GC, GLA_W), lambda n: (n, 0)),
                  pl.BlockSpec((1, GLA_W), lambda n: (0, 0))] + [ANY] * n_xc,
        out_specs=[pl.BlockSpec((GS * GC, GLA_W), lambda n: (n, 0)),
                   pl.BlockSpec((GS * GC, GLA_W), lambda n: (n, 0)),
                   pl.BlockSpec((GS, GLA_HEADS, GLA_V, GLA_K), lambda n: (n, 0, 0, 0)),
                   pl.BlockSpec((GS, GLA_HEADS, GC, GC), lambda n: (n, 0, 0, 0))] + [ANY] * n_xc,
        out_shape=[jax.ShapeDtypeStruct((tp, GLA_W), F32), jax.ShapeDtypeStruct((tp, GLA_W), BF16),
                   jax.ShapeDtypeStruct((nc, GLA_HEADS, GLA_V, GLA_K), BF16),
                   jax.ShapeDtypeStruct((nc, GLA_HEADS, GC, GC), BF16)] + (list(comm.out_shapes) if comm else []),
        scratch_shapes=[pltpu.VMEM((GLA_HEADS, GLA_V, GLA_K), F32)] + (_comm_sems(comm) if comm else []),
        compiler_params=_cparams(1),
    )(gqk, gv, b, gg, gain, *(comm.srcs if comm else ()))


def _gla_bwd_call(gqk, gv, b, gg, o_gla, da, states, scores, gain, comm=None):
    tp = gqk.shape[0]
    nc = tp // GC
    ns = nc // GS
    o_gv, o_gg = 2 * GLA_KW, 2 * GLA_KW + GLA_W
    n_xc = len(comm.srcs) if comm else 0

    def body(qk_all, v_all, b_all, gg_all, o_all, da_all, st_ref, am_ref, g_ref, *rest):
        xc_src = rest[:n_xc]
        dp_all, db_all, dg_ref = rest[n_xc:n_xc + 3]
        xc_dst = rest[n_xc + 3:2 * n_xc + 3]
        ds_scr = rest[2 * n_xc + 3]
        n = pl.program_id(0)
        if n_xc:
            begin, finish = comm.make(xc_src, xc_dst, rest[-2], rest[-1])
            pl.when(n == 0)(begin)
            pl.when(n == ns - 1)(finish)

        @pl.when(n == 0)
        def _():
            ds_scr[...] = jnp.zeros_like(ds_scr)
            dg_ref[...] = jnp.zeros_like(dg_ref)

        masks, md, second = _gla_masks()
        for cc, h in [(cc, h) for cc in reversed(range(GS)) for h in range(GLA_HEADS)]:
            rows = pl.ds(cc * GC, GC)
            qk_ref, v_ref, b_scr, gg_ref, o_ref, da_ref, dp_ref, db_scr = (
                r.at[rows] for r in (qk_all, v_all, b_all, gg_all, o_all, da_all, dp_all, db_all))
            cs = slice(h * GLA_K, (h + 1) * GLA_K)
            vs = slice(h * GLA_V, (h + 1) * GLA_V)
            o = o_ref[:, vs]
            rstd = lax.rsqrt(_head_mean(o * o) + EPS)
            xh = o * rstd
            gain_h = g_ref[:, vs]
            g = gg_ref[:, vs]
            sg = _sigmoid(g)
            dah = da_ref[:, vs]
            dp_ref[:, o_gg + h * GLA_V:o_gg + (h + 1) * GLA_V] = (
                dah * (xh * gain_h) * (sg * (1.0 + g * (1.0 - sg)))).astype(BF16)
            dn = dah * (g * sg)
            dg_ref[:, vs] += jnp.sum(dn * xh, axis=0, keepdims=True)
            dxh = dn * gain_h
            do = rstd * (dxh - xh * _head_mean(dxh * xh))
            dob = do.astype(BF16)
            q = qk_ref[:, cs]
            k = qk_ref[:, GLA_KW + h * GLA_K:GLA_KW + (h + 1) * GLA_K]
            v = v_ref[:, vs]
            fq, fk, ed, edi, eb, ee, ebl = _gla_factors(b_scr, h, second)
            qt, kt, qd, kd = _gla_scaled(q, k, fq, fk, ed, edi)
            sp = st_ref[cc, h]
            ds = ds_scr[h]
            dsb = ds.astype(BF16)
            q_in = q * eb
            k_end = k * ee
            da_s = _dot(dob, v, NT)
            dv = _dot(am_ref[cc, h], dob, TN) + _dot(k_end.astype(BF16), dsb, NT)
            dq_in = _dot(dob, sp, NN)
            dk_end = _dot(v, dsb, NN)
            dbl = jnp.sum(sp.astype(F32) * ds, axis=0, keepdims=True) * ebl
            ds_scr[h] = ds * ebl + _dot(dob, q_in.astype(BF16), TN)
            dq = dq_in * eb
            dk = dk_end * ee
            de_end = dk_end * k_end
            db = dq_in * q_in - de_end
            placed = [(GC - 1, jnp.sum(de_end, axis=0, keepdims=True) + dbl)]
            for l, m in enumerate(GLA_LEVELS):
                dal = jnp.where(masks[l], da_s, 0.0).astype(BF16)
                dqt = _dot(dal, kt[l], NN)
                dkt = _dot(dal, qt[l], TN)
                dq = dq + dqt * fq[l]
                dk = dk + dkt * fk[l]
                gl = dqt * (q * fq[l]) - dkt * (k * fk[l])
                db = db + gl
                placed += [(s + m - 1, -jnp.sum(gl[s:s + 2 * m], axis=0, keepdims=True)) for s in range(0, GC, 2 * m)]
            dad = jnp.where(md, da_s, 0.0).astype(BF16)
            dqd = _dot(dad, kd, NN)
            dkd = _dot(dad, qd, TN)
            dq = dq + dqd * ed
            dk = dk + dkd * edi
            gd = dqd * (q * ed) - dkd * (k * edi)
            db = db + gd
            placed += [(s - 1, -jnp.sum(gd[s:s + GLA_SUB], axis=0, keepdims=True)) for s in range(GLA_SUB, GC, GLA_SUB)]
            db_scr[:, cs] = db
            for r, val in placed:
                db_scr[r:r + 1, cs] += val
            dp_ref[:, cs] = (dq * (GLA_K ** -0.5)).astype(BF16)
            dp_ref[:, GLA_KW + h * GLA_K:GLA_KW + (h + 1) * GLA_K] = dk.astype(BF16)
            dp_ref[:, o_gv + h * GLA_V:o_gv + (h + 1) * GLA_V] = dv.astype(BF16)

    rev = lambda n: (ns - 1 - n, 0)
    const = lambda n: (0, 0)
    xc_shapes, xc_sems = (list(comm.out_shapes), _comm_sems(comm)) if n_xc else ([], [])
    return pl.pallas_call(
        body, name="gla_bwd", grid=(ns,),
        in_specs=[pl.BlockSpec((GS * GC, 2 * GLA_KW), rev),
                  pl.BlockSpec((GS * GC, GLA_W), rev),
                  pl.BlockSpec((GS * GC, GLA_KW), rev),
                  pl.BlockSpec((GS * GC, GLA_W), rev),
                  pl.BlockSpec((GS * GC, GLA_W), rev),
                  pl.BlockSpec((GS * GC, GLA_W), rev),
                  pl.BlockSpec((GS, GLA_HEADS, GLA_V, GLA_K), lambda n: (ns - 1 - n, 0, 0, 0)),
                  pl.BlockSpec((GS, GLA_HEADS, GC, GC), lambda n: (ns - 1 - n, 0, 0, 0)),
                  pl.BlockSpec((1, GLA_W), const)] + [ANY] * n_xc,
        out_specs=[pl.BlockSpec((GS * GC, W_GP), rev), pl.BlockSpec((GS * GC, GLA_KW), rev),
                   pl.BlockSpec((1, GLA_W), const)] + [ANY] * n_xc,
        out_shape=[jax.ShapeDtypeStruct((tp, W_GP), BF16), jax.ShapeDtypeStruct((tp, GLA_KW), F32),
                   jax.ShapeDtypeStruct((1, GLA_W), F32)] + xc_shapes,
        scratch_shapes=[pltpu.VMEM((GLA_HEADS, GLA_V, GLA_K), F32)] + xc_sems,
        compiler_params=_cparams(1),
    )(gqk, gv, b, gg, o_gla, da, states, scores, gain, *(comm.srcs if comm else ()))


def _mid_call(a_ret, a_gla, mg, h0, tgt, wbr, wbg, wout, gf):
    tp = h0.shape[0]
    nt = tp // TM

    def body(ar_ref, ag_ref, mg_ref, h_ref, t_ref, wbr_ref, wbg_ref, wo_ref, gf_ref,
             dh1_ref, dag_ref, dm_ref, mb_ref, dh1b_ref, dprb_ref, dpgb_ref, loss_ref, dgf_ref):
        i = pl.program_id(0)

        @pl.when(i == 0)
        def _():
            loss_ref[...] = jnp.zeros_like(loss_ref)
            dgf_ref[...] = jnp.zeros_like(dgf_ref)

        ar, ag = ar_ref[...], ag_ref[...]
        pr = _dot(ar, wbr_ref[...], NN)
        pg = _dot(ag, wbg_ref[...], NN)
        sr = _sigmoid(mg_ref[:, :D_MODEL])
        sg = _sigmoid(mg_ref[:, D_MODEL:])
        merged = (sr * pr + sg * pg).astype(BF16)
        mb_ref[...] = merged
        h1 = h_ref[...] + _dot(merged, wo_ref[...], NN)
        r1 = lax.rsqrt(jnp.mean(h1 * h1, axis=-1, keepdims=True) + EPS)
        xh = h1 * r1
        gfv = gf_ref[...]
        live = jnp.where(i > 0, 1.0, 0.0).astype(F32)
        err = (xh * gfv - t_ref[...]) * live
        loss_ref[...] += jnp.full(loss_ref.shape, 0.5 / D_MODEL, F32) * jnp.sum(err * err)
        dy = err * (1.0 / D_MODEL)
        dgf_ref[...] += jnp.sum(dy * xh, axis=0, keepdims=True)
        dxh = dy * gfv
        dh1 = r1 * (dxh - xh * jnp.mean(dxh * xh, axis=-1, keepdims=True))
        dh1_ref[...] = dh1
        dh1b = dh1.astype(BF16)
        dh1b_ref[...] = dh1b
        dmerged = _dot(dh1b, wo_ref[...], NT)
        dm_ref[:, :D_MODEL] = (dmerged * pr * sr * (1.0 - sr)).astype(BF16)
        dm_ref[:, D_MODEL:] = (dmerged * pg * sg * (1.0 - sg)).astype(BF16)
        dpr = (dmerged * sr).astype(BF16)
        dpg = (dmerged * sg).astype(BF16)
        dprb_ref[...] = dpr
        dpgb_ref[...] = dpg
        dag_ref[...] = _dot(dpg, wbg_ref[...], NT)

    tile = lambda w: pl.BlockSpec((TM, w), lambda i: (i, 0))
    const = lambda r, w: pl.BlockSpec((r, w), lambda i: (0, 0))
    return pl.pallas_call(
        body, name="merge_out_loss", grid=(nt,),
        in_specs=[tile(RET_W), tile(GLA_W), tile(W_M), tile(D_MODEL),
                  pl.BlockSpec((TM, D_MODEL), lambda i: (jnp.maximum(i - 1, 0), 0)),
                  const(RET_W, D_MODEL), const(GLA_W, D_MODEL), const(D_MODEL, D_MODEL), const(1, D_MODEL)],
        out_specs=[tile(D_MODEL), tile(GLA_W), tile(W_M), tile(D_MODEL), tile(D_MODEL), tile(D_MODEL),
                   tile(D_MODEL), const(1, 128), const(1, D_MODEL)],
        out_shape=[jax.ShapeDtypeStruct((tp, D_MODEL), F32), jax.ShapeDtypeStruct((tp, GLA_W), F32),
                   jax.ShapeDtypeStruct((tp, W_M), BF16),
                   jax.ShapeDtypeStruct((tp, D_MODEL), BF16), jax.ShapeDtypeStruct((tp, D_MODEL), BF16),
                   jax.ShapeDtypeStruct((tp, D_MODEL), BF16), jax.ShapeDtypeStruct((tp, D_MODEL), BF16),
                   jax.ShapeDtypeStruct((1, 128), F32), jax.ShapeDtypeStruct((1, D_MODEL), F32)],
        compiler_params=_cparams(1),
    )(a_ret, a_gla, mg, h0, tgt, wbr, wbg, wout, gf)


def _device_step(x2d, tgt2d, meta, norm_gain, w_in_part, w_gate_up, b_gate, ret_gain, gla_gain, branch_parts,
                 final_gain, ck):
    seq = x2d.shape[0]
    tp = T0 + seq
    head = jnp.concatenate([jnp.zeros((PADF, D_MODEL), F32), meta], axis=0)
    wg_pad = jnp.pad(w_gate_up, ((0, 128 - GATE_RANK), (0, 0))).astype(BF16)

    half = RET_QK // 2
    cos, sin = (jnp.asarray(t) for t in _rope_tables(tp))
    lgam = jnp.log1p(-(2.0 ** (-5.0 - jnp.arange(RET_HEADS, dtype=F32))))
    pmat = jnp.asarray(_gla_tril(), BF16)
    pmat_t = jnp.asarray(_gla_tril().T.copy(), BF16)

    h0, u, g_in = _rms_call(x2d, head, norm_gain, _gather_plan([w_in_part], relay=(True,)))
    hr, sw = w_in_part.shape
    w_in_bf = g_in.reshape(4, 2, hr, sw).transpose(1, 2, 0, 3).reshape(2 * hr, 4 * sw)
    w_r = w_in_bf
    w_g = jnp.pad(w_in_bf[:, W_R:W_R + W_G], ((0, 0), (0, W_GP - W_G)))
    w_m = w_in_bf[:, W_R + W_G:]
    tab = pl.BlockSpec((_proj_rows(tp), half), lambda j, i: (i, 0))
    rqk = _mm_nn("proj_rqk", u, w_r, BF16, D_MODEL, 0, 2 * D_MODEL, _rope_epilogue, (cos, sin), (tab, tab))
    rv = _mm_nn("proj_rv", u, w_r, BF16, RET_W, 2 * D_MODEL, RET_W)
    rg = _mm_nn("proj_rg", u, w_r, F32, RET_W, 4 * D_MODEL, RET_W)
    gqk = _mm_nn("proj_gqk", u, w_g, F32, 2 * GLA_KW, 0, 2 * GLA_KW, _gqk_epilogue)
    gv = _mm_nn("proj_gv", u, w_g, BF16, GLA_W, 2 * GLA_KW, GLA_W)
    gg = _mm_nn("proj_gg", u, w_g, F32, GLA_W, 2 * GLA_KW + GLA_W, GLA_W)
    glr = _mm_nn("proj_glr", u, w_g, F32, 128, 2 * GLA_KW + 2 * GLA_W, 128)
    mg = _mm_nn("proj_mg", u, w_m, F32, W_M, 0, W_M)

    o_ret, a_ret, st_ret, sc_ret = _ret_fwd_call(rqk, rv, rg, ret_gain, lgam)
    z_gate, b_dec = _gla_gate_call(glr, wg_pad, b_gate, pmat)
    o_gla, a_gla, st_gla, sc_gla, g_br, g_bg, g_out = _gla_fwd_call(gqk, gv, b_dec, gg, gla_gain,
                                                                    comm=_spread_plan(branch_parts))
    wbr = g_br.reshape(RET_W, D_MODEL)
    wbg = g_bg.reshape(GLA_W, D_MODEL)
    wout = g_out.reshape(D_MODEL, D_MODEL)

    gf = final_gain.reshape(1, D_MODEL)
    (dh1, da_gla, dm, merged_b, dh1_b, dpr_b, dpg_b, loss, dgf) = _mid_call(
        a_ret, a_gla, mg, h0, tgt2d, wbr, wbg, wout, gf)

    names_b = ("w_branch_ret", "w_branch_gla", "w_out")
    g2_b = [_mm_tn("dw_br", a_ret, dpr_b, D_MODEL).reshape(4, 2, RET_W // 8, D_MODEL).transpose(1, 0, 2, 3),
            _mm_tn("dw_bg", a_gla, dpg_b, D_MODEL).reshape(4, 2, GLA_W // 8, D_MODEL).transpose(1, 0, 2, 3),
            _mm_tn("dw_out", merged_b, dh1_b, D_MODEL).reshape(4, 2, D_MODEL // 8, D_MODEL).transpose(1, 0, 2, 3)]
    sib_b = _swap_halves_call("swap_halves_branch", g2_b)
    sum_b = [_add_half_call("add_half_" + nm, g, b, ck) for nm, g, b in zip(names_b, g2_b, sib_b)]
    d_g, db_dec, dgla_gain, *chips_b = _gla_bwd_call(gqk, gv, b_dec, gg, o_gla, da_gla, st_gla, sc_gla, gla_gain,
                                                     comm=_exchange_plan(sum_b))
    d_g, dwg, dbg = _gla_gate_bwd_call(db_dec, z_gate, glr, wg_pad, pmat_t, d_g)
    mine = [_add_chips_call("add_chips_" + nm, g, b, p, ck) for nm, g, b, p in zip(names_b, g2_b, sib_b, chips_b)]

    d_r, dret_gain = _ret_bwd_call(rqk, rv, rg, o_ret, dpr_b, wbr, st_ret, sc_ret, ret_gain, lgam, cos, sin)

    dwp = _mm_tn("dw_r", u, d_r, 2 * D_MODEL, out_cols=IN_PAD)
    dwp = _mm_tn("dw_g", u, d_g, D_MODEL, ncols=W_GP - 128, into=dwp, col0=W_R)
    dwp = _mm_tn("dw_glr", u, d_g, 128, ncols=128, bcol0=W_GP - 128, into=dwp, col0=W_R + W_GP - 128)
    g2_in = _place_merge_cols_call(dwp, _mm_tn("dw_m", u, dm, 2 * D_MODEL)).reshape(2, D_MODEL // 2, IN_PAD)

    du, sib_in = _mm_nt_acc("du_g", d_g, w_g, W_GP, comm=_swap_plan([g2_in]), tb=_proj_rows(tp))
    sum_in = _add_rows_call("add_half_w_in", g2_in, sib_in, ck)
    du, chips_in = _mm_nt_acc("du_r", d_r, w_r, 2 * D_MODEL, acc_in=du, comm=_exchange_window_plan(sum_in),
                              tb=_proj_rows(tp))
    tile = pl.BlockSpec((TB, D_MODEL), lambda i, kk: (i, 0))
    row = pl.BlockSpec((1, D_MODEL), lambda i, kk: (0, 0))
    dx, dmeta, dnorm_gain = _mm_nt_acc(
        "du_m", dm, w_m, W_M, acc_in=du, epilogue=_rms_bwd_epilogue, extras=(h0, norm_gain, dh1),
        extra_specs=(tile, row, tile),
        extra_out_shapes=(jax.ShapeDtypeStruct((seq, D_MODEL), F32), jax.ShapeDtypeStruct((N_META, D_MODEL), F32),
                          jax.ShapeDtypeStruct((1, D_MODEL), F32)),
        extra_out_specs=(ANY, pl.BlockSpec((N_META, D_MODEL), lambda i, kk: (0, 0)), row),
        extra_scratch=(pltpu.VMEM((2, TB, D_MODEL), F32), pltpu.SemaphoreType.DMA((2,))))
    small = dict(norm_gain=dnorm_gain, b_gate=dbg, ret_norm_gain=dret_gain, gla_norm_gain=dgla_gain,
                 final_norm_gain=dgf, w_gate_up=dwg[:GATE_RANK], meta_tokens=dmeta, loss=loss[0, 0])
    rows = -(-sum(sz for _, sz in SMALL) // 128 // 8) * 8
    mine_in, g_small = _add_window_call("add_chips_w_in", g2_in, sib_in, chips_in, ck,
                                        _gather_plan([_pack_rows([small[nm] for nm, _ in SMALL], rows)]))
    full = _join_halves_call("join_halves", [mine_in] + mine)

    return dict(dx=dx, small=g_small, w_in=full[0], w_branch_ret=full[1], w_branch_gla=full[2], w_out=full[3])


MESH = pl.DeviceIdType.MESH
ANY = pl.BlockSpec(memory_space=pl.ANY)


def _place():
    return lax.axis_index("x"), lax.axis_index("y"), lax.axis_index("c")


def _gather8_call(name, parts):
    comm = _gather_plan(parts)
    n = len(parts)

    def body(*refs):
        begin, finish = comm.make(refs[:n], refs[n:2 * n], refs[-2], refs[-1])
        begin()
        finish()

    return pl.pallas_call(
        body, name=name, out_shape=list(comm.out_shapes), in_specs=[ANY] * n, out_specs=[ANY] * n,
        scratch_shapes=_comm_sems(comm),
    )(*parts)


def _swap_halves_call(name, gs):
    n = len(gs)

    def body(*refs):
        g_refs, b_refs = refs[:n], refs[n:2 * n]
        send_sems, recv_sems = refs[2 * n:]
        x, y, c = _place()
        copies = [pltpu.make_async_remote_copy(
            src_ref=g_refs[t].at[1 - c], dst_ref=b_refs[t], send_sem=send_sems.at[t], recv_sem=recv_sems.at[t],
            device_id=(x, y, 1 - c), device_id_type=MESH) for t in range(n)]
        for cp in copies:
            cp.start()
        for cp in copies:
            cp.wait()

    return pl.pallas_call(
        body, name=name,
        out_shape=[jax.ShapeDtypeStruct(g.shape[1:], g.dtype) for g in gs],
        in_specs=[ANY] * n, out_specs=[ANY] * n,
        scratch_shapes=[pltpu.SemaphoreType.DMA((n,)), pltpu.SemaphoreType.DMA((n,))],
    )(*gs)


def _join_halves_call(name, ts):
    n = len(ts)

    def body(*refs):
        o_refs = refs[n:2 * n]
        send_sems, recv_sems = refs[2 * n:]
        x, y, c = _place()
        copies = [pltpu.make_async_remote_copy(
            src_ref=o_refs[t].at[c], dst_ref=o_refs[t].at[c], send_sem=send_sems.at[t], recv_sem=recv_sems.at[t],
            device_id=(x, y, 1 - c), device_id_type=MESH) for t in range(n)]
        for cp in copies:
            cp.start()
        for t in range(n):
            copies[t].wait_send()
            pltpu.make_async_remote_copy(
                src_ref=o_refs[t].at[c], dst_ref=o_refs[t].at[1 - c], send_sem=send_sems.at[t],
                recv_sem=recv_sems.at[t], device_id=(x, y, 1 - c), device_id_type=MESH).wait_recv()

    return pl.pallas_call(
        body, name=name,
        out_shape=[jax.ShapeDtypeStruct(t.shape, t.dtype) for t in ts],
        in_specs=[ANY] * n, out_specs=[ANY] * n, input_output_aliases={t: t for t in range(n)},
        scratch_shapes=[pltpu.SemaphoreType.DMA((n,)), pltpu.SemaphoreType.DMA((n,))],
    )(*ts)


def _row_block(rows, cols, budget):
    best = 8
    for rb in range(8, rows + 1, 8):
        if rows % rb == 0 and rb * cols * 4 <= budget:
            best = rb
    return best


def _add_half_call(name, g, b, ck):
    _, _, r, cc = g.shape
    rb = _row_block(r, cc, 2 * 1024 * 1024)

    def body(ck_ref, g_ref, b_ref, o_ref):
        o_ref[...] = (g_ref[...] + b_ref[...]).astype(BF16)

    return pl.pallas_call(
        body, name=name,
        grid_spec=pltpu.PrefetchScalarGridSpec(
            num_scalar_prefetch=1, grid=(4, r // rb),
            in_specs=[pl.BlockSpec((None, None, rb, cc), lambda k, i, ck_ref: (ck_ref[0], k, i, 0)),
                      pl.BlockSpec((None, rb, cc), lambda k, i, ck_ref: (k, i, 0))],
            out_specs=pl.BlockSpec((None, rb, cc), lambda k, i, ck_ref: (k, i, 0))),
        out_shape=jax.ShapeDtypeStruct(b.shape, BF16),
        compiler_params=_cparams(2),
    )(ck, g, b)


def _add_rows_call(name, g, b, ck):
    _, r, cc = g.shape
    rb = _row_block(r, cc, 2 * 1024 * 1024)

    def body(ck_ref, g_ref, b_ref, o_ref):
        o_ref[...] = (g_ref[...] + b_ref[...]).astype(BF16)

    return pl.pallas_call(
        body, name=name,
        grid_spec=pltpu.PrefetchScalarGridSpec(
            num_scalar_prefetch=1, grid=(r // rb,),
            in_specs=[pl.BlockSpec((None, rb, cc), lambda i, ck_ref: (ck_ref[0], i, 0)),
                      pl.BlockSpec((rb, cc), lambda i, ck_ref: (i, 0))],
            out_specs=pl.BlockSpec((rb, cc), lambda i, ck_ref: (i, 0))),
        out_shape=jax.ShapeDtypeStruct((r, cc), BF16),
        compiler_params=_cparams(1),
    )(ck, g, b)


def _add_window_call(name, g, b, p, ck, comm):
    _, r, _ = g.shape
    nb, step = WIN_W // 128, WIN_STEP // 128
    n_xc = len(comm.srcs)

    def body(ck_ref, g_ref, b_ref, p0_ref, p1_ref, p2_ref, *rest):
        o_ref = rest[n_xc]
        i = pl.program_id(0)
        begin, finish = comm.make(rest[:n_xc], rest[n_xc + 1:2 * n_xc + 1], rest[-2], rest[-1])
        pl.when(i == 0)(begin)
        own = g_ref[...] + b_ref[...]
        o_ref[...] = ((own + p0_ref[...].astype(F32)) + p1_ref[...].astype(F32)) + p2_ref[...].astype(F32)
        pl.when(i == nb - 1)(finish)

    def peer(j):
        return pl.BlockSpec((None, r, 128), lambda i, ck_ref: (j, 0, i))

    return pl.pallas_call(
        body, name=name,
        grid_spec=pltpu.PrefetchScalarGridSpec(
            num_scalar_prefetch=1, grid=(nb,),
            in_specs=[pl.BlockSpec((None, r, 128), lambda i, ck_ref: (ck_ref[0], 0, step * ck_ref[1] + i)),
                      pl.BlockSpec((r, 128), lambda i, ck_ref: (0, step * ck_ref[1] + i)),
                      peer(0), peer(1), peer(2)] + [ANY] * n_xc,
            out_specs=[pl.BlockSpec((None, r, 128), lambda i, ck_ref: (ck_ref[0], 0, i))] + [ANY] * n_xc,
            scratch_shapes=_comm_sems(comm)),
        out_shape=[jax.ShapeDtypeStruct((2, r, WIN_W), F32)] + list(comm.out_shapes),
        compiler_params=_cparams(1),
    )(ck, g, b, p, p, p, *comm.srcs)


def _add_chips_call(name, g, b, p, ck):
    _, _, r, cc = g.shape
    rb = _row_block(r, cc, 2 * 1024 * 1024)

    def body(ck_ref, g_ref, b_ref, p0_ref, p1_ref, p2_ref, o_ref):
        own = g_ref[...] + b_ref[...]
        o_ref[...] = ((own + p0_ref[...].astype(F32)) + p1_ref[...].astype(F32)) + p2_ref[...].astype(F32)

    def peer(j):
        return pl.BlockSpec((None, rb, cc), lambda i, ck_ref: (j, i, 0))

    return pl.pallas_call(
        body, name=name,
        grid_spec=pltpu.PrefetchScalarGridSpec(
            num_scalar_prefetch=1, grid=(r // rb,),
            in_specs=[pl.BlockSpec((None, None, rb, cc), lambda i, ck_ref: (ck_ref[0], ck_ref[1], i, 0)),
                      pl.BlockSpec((None, rb, cc), lambda i, ck_ref: (ck_ref[1], i, 0)),
                      peer(0), peer(1), peer(2)],
            out_specs=pl.BlockSpec((None, rb, cc), lambda i, ck_ref: (ck_ref[0], i, 0))),
        out_shape=jax.ShapeDtypeStruct((2, r, cc), F32),
        compiler_params=_cparams(1),
    )(ck, g, b, p, p, p)


def _sum8_call(name, g):
    def body(g_ref, o_ref):
        acc = g_ref[0]
        for d in range(1, 8):
            acc = acc + g_ref[d]
        o_ref[...] = acc

    return pl.pallas_call(body, name=name, out_shape=jax.ShapeDtypeStruct(g.shape[1:], F32))(g)


def _adamw_call(name, w, g, m, v):
    r, cc = w.shape
    if r % 8 == 0 or r * cc * 4 <= 1024 * 1024:
        rb = _row_block(r, cc, 1024 * 1024) if r % 8 == 0 else r
        grid, spec = (r // rb,), pl.BlockSpec((rb, cc), lambda i: (i, 0))
    else:
        grid, spec = (cc // 128,), pl.BlockSpec((r, 128), lambda i: (0, i))

    def body(w_ref, g_ref, m_ref, v_ref, d_ref, m2_ref, v2_ref):
        gv = g_ref[...]
        m2 = ADAM_B1 * m_ref[...] + (1.0 - ADAM_B1) * gv
        v2 = ADAM_B2 * v_ref[...] + (1.0 - ADAM_B2) * (gv * gv)
        m_hat = m2 / (1.0 - ADAM_B1 ** ADAM_STEP)
        v_hat = v2 / (1.0 - ADAM_B2 ** ADAM_STEP)
        d_ref[...] = -ADAM_LR * (m_hat / (jnp.sqrt(v_hat) + ADAM_EPS) + ADAM_WD * w_ref[...])
        m2_ref[...] = m2
        v2_ref[...] = v2

    return pl.pallas_call(
        body, name=name, grid=grid, in_specs=[spec] * 4, out_specs=[spec] * 3,
        out_shape=[jax.ShapeDtypeStruct((r, cc), F32)] * 3, compiler_params=_cparams(1),
    )(w, g, m, v)


SMALL = (("norm_gain", D_MODEL), ("b_gate", GLA_KW), ("ret_norm_gain", RET_W), ("gla_norm_gain", GLA_W),
         ("final_norm_gain", D_MODEL), ("w_gate_up", GATE_RANK * GLA_KW), ("meta_tokens", N_META * D_MODEL),
         ("loss", 1))


def _pack_rows(vecs, rows):
    flat = jnp.concatenate([v.reshape(-1) for v in vecs])
    return jnp.pad(flat, (0, rows * 128 - flat.shape[0])).reshape(rows, 128)


def kernel(x, meta_tokens, norm_gain, w_in, w_gate_up, b_gate, ret_norm_gain, gla_norm_gain, w_branch_ret, w_branch_gla, w_out, final_norm_gain, loss_target, m_meta_tokens, m_norm_gain, m_w_in, m_w_gate_up, m_b_gate, m_ret_norm_gain, m_gla_norm_gain, m_w_branch_ret, m_w_branch_gla, m_w_out, m_final_norm_gain, v_meta_tokens, v_norm_gain, v_w_in, v_w_gate_up, v_b_gate, v_ret_norm_gain, v_gla_norm_gain, v_w_branch_ret, v_w_branch_gla, v_w_out, v_final_norm_gain):
    xi, yi, ci = _place()
    kme = 2 * xi + yi
    ck = jnp.stack([ci, kme]).astype(jnp.int32)
    sw_in = w_in.shape[2]

    def my_half(a, dtype):
        r, cc = a.shape
        return lax.dynamic_index_in_dim(a.reshape(2, r // 2, cc), ci, 0, keepdims=False).astype(dtype)

    g_meta, g_wg = _gather8_call("gather_small_weights", [my_half(meta_tokens, F32), my_half(w_gate_up[0], F32)])
    branch_parts = [my_half(w_branch_ret[0], BF16), my_half(w_branch_gla[0], BF16), my_half(w_out[0], BF16)]
    meta = g_meta.reshape(4, 2, N_META // 2, D_MODEL // 4).transpose(1, 2, 0, 3).reshape(N_META, D_MODEL)
    wg_full = g_wg.reshape(4, 2, GATE_RANK // 2, GLA_KW // 4).transpose(1, 2, 0, 3).reshape(GATE_RANK, GLA_KW)

    loc = _device_step(x[0], loss_target[0], meta, norm_gain, my_half(w_in[0], BF16), wg_full, b_gate, ret_norm_gain,
                       gla_norm_gain,
                       branch_parts, final_norm_gain, ck)
    names = ("w_in", "w_branch_ret", "w_branch_gla", "w_out")
    full = [loc[nm] for nm in names]
    big_w = dict(w_in=w_in[0], w_branch_ret=w_branch_ret[0], w_branch_gla=w_branch_gla[0], w_out=w_out[0])
    big_m = dict(w_in=m_w_in[0], w_branch_ret=m_w_branch_ret[0], w_branch_gla=m_w_branch_gla[0], w_out=m_w_out[0])
    big_v = dict(w_in=v_w_in[0], w_branch_ret=v_w_branch_ret[0], w_branch_gla=v_w_branch_gla[0], w_out=v_w_out[0])
    grads, deltas, new_m, new_v = {}, {}, {}, {}
    for nm, f in zip(names, full):
        shape = big_w[nm].shape
        if nm == "w_in":
            f = lax.dynamic_slice_in_dim(f, (sw_in - WIN_STEP) * kme, sw_in, axis=2)
        g = f.reshape(shape)
        if nm == "w_in":
            d, m2, v2 = (a.T for a in _adamw_call("adamw_" + nm, big_w[nm].T, g.T, big_m[nm].T, big_v[nm].T))
        else:
            d, m2, v2 = _adamw_call("adamw_" + nm, big_w[nm], g, big_m[nm], big_v[nm])
        grads[nm], deltas[nm], new_m[nm], new_v[nm] = (a.reshape((1,) + shape) for a in (g, d, m2, v2))

    tot = _sum8_call("sum_small_grads", loc["small"]).reshape(-1)
    off = 0
    sg = {}
    for nm, sz in SMALL:
        sg[nm] = tot[off:off + sz]
        off += sz
    loss = sg.pop("loss")[0]
    sg["w_gate_up"] = lax.dynamic_slice_in_dim(sg["w_gate_up"].reshape(GATE_RANK, GLA_KW), kme * (GLA_KW // 4),
                                               GLA_KW // 4, axis=1)
    sg["meta_tokens"] = lax.dynamic_slice_in_dim(sg["meta_tokens"].reshape(N_META, D_MODEL), kme * (D_MODEL // 4),
                                                 D_MODEL // 4, axis=1)
    small_w = dict(norm_gain=norm_gain, b_gate=b_gate, ret_norm_gain=ret_norm_gain, gla_norm_gain=gla_norm_gain,
                   final_norm_gain=final_norm_gain, w_gate_up=w_gate_up, meta_tokens=meta_tokens)
    small_m = dict(norm_gain=m_norm_gain, b_gate=m_b_gate, ret_norm_gain=m_ret_norm_gain,
                   gla_norm_gain=m_gla_norm_gain, final_norm_gain=m_final_norm_gain, w_gate_up=m_w_gate_up,
                   meta_tokens=m_meta_tokens)
    small_v = dict(norm_gain=v_norm_gain, b_gate=v_b_gate, ret_norm_gain=v_ret_norm_gain,
                   gla_norm_gain=v_gla_norm_gain, final_norm_gain=v_final_norm_gain, w_gate_up=v_w_gate_up,
                   meta_tokens=v_meta_tokens)
    for nm in small_w:
        shape = small_w[nm].shape
        as2d = lambda a: a.reshape((-1, shape[-1]))
        grads[nm] = sg[nm].reshape(shape)
        deltas[nm], new_m[nm], new_v[nm] = (a.reshape(shape) for a in _adamw_call(
            "adamw_" + nm, as2d(small_w[nm]), as2d(sg[nm]), as2d(small_m[nm]), as2d(small_v[nm])))

    out_order = ("meta_tokens", "norm_gain", "w_in", "w_gate_up", "b_gate", "ret_norm_gain", "gla_norm_gain",
                 "w_branch_ret", "w_branch_gla", "w_out", "final_norm_gain")
    dx = loc["dx"].reshape(x.shape)
    return (loss, dx, *[grads[nm] for nm in out_order], *[deltas[nm] for nm in out_order],
            *[new_m[nm] for nm in out_order], *[new_v[nm] for nm in out_order])
```

```python
import math
from typing import Callable, NamedTuple

import numpy as np
import jax
import jax.numpy as jnp
from jax import lax
from jax.experimental import pallas as pl
from jax.experimental.pallas import tpu as pltpu

F32 = jnp.float32
BF16 = jnp.bfloat16

D_MODEL = 1024
N_META = 16
EPS = 1e-6
ROPE_BASE = 10000.0
RET_HEADS, RET_QK, RET_V = 4, 256, 512
RET_W = RET_HEADS * RET_V
GLA_HEADS, GLA_K, GLA_V = 4, 128, 256
GLA_W = GLA_HEADS * GLA_V
GLA_KW = GLA_HEADS * GLA_K
GATE_RANK = 16
GATE_TAU = 16.0
GLA_SUB = 16

TM = 256
T0 = TM
PADF = T0 - N_META
GC = 128
GS = 3
TB = 768
TK = 768

W_R = 6144
W_G = 3088
W_GP = 3200
W_M = 2048
IN_COLS = W_R + W_G + W_M
WIN_STEP = (IN_COLS // 4) // 128 * 128
WIN_W = -(-(3 * (IN_COLS // 4 - WIN_STEP) + IN_COLS // 4) // 128) * 128
IN_PAD = 3 * WIN_STEP + WIN_W

ADAM_LR, ADAM_B1, ADAM_B2, ADAM_EPS, ADAM_WD, ADAM_STEP = 0.001, 0.9, 0.999, 1e-08, 0.01, 10

VMEM_LIMIT = 56 * 1024 * 1024

NN = ((1,), (0,))
NT = ((1,), (1,))
TN = ((0,), (0,))


def _dot(a, b, dims):
    return lax.dot_general(a, b, (dims, ((), ())), preferred_element_type=F32)


def _cparams(n_axes):
    return pltpu.CompilerParams(dimension_semantics=("arbitrary",) * n_axes, vmem_limit_bytes=VMEM_LIMIT)


def _sigmoid(x):
    return 0.5 * jnp.tanh(0.5 * x) + 0.5


def _silu(x):
    h = 0.5 * x
    return h + h * jnp.tanh(h)


def _head_mean(x):
    return jnp.mean(x, axis=-1, keepdims=True)


def _split3(x):
    hi = x.astype(BF16)
    r1 = x - hi.astype(F32)
    mid = r1.astype(BF16)
    lo = (r1 - mid.astype(F32)).astype(BF16)
    return hi, mid, lo


def _exact_pm(p, x):
    hi, mid, lo = _split3(x)
    return _dot(p, hi, NN) + _dot(p, mid, NN) + _dot(p, lo, NN)


def _rms_call(x2d, head, gain, comm):
    tp = T0 + x2d.shape[0]
    nt = tp // TM
    n_xc = len(comm.srcs)

    def body(x_ref, hd_ref, g_ref, *rest):
        xc_src = rest[:n_xc]
        h_ref, u_ref = rest[n_xc:n_xc + 2]
        xc_dst = rest[n_xc + 2:2 * n_xc + 2]
        i = pl.program_id(0)
        begin, finish = comm.make(xc_src, xc_dst, rest[-2], rest[-1])
        pl.when(i == 0)(begin)
        h = jnp.where(i == 0, hd_ref[...], x_ref[...])
        h_ref[...] = h
        r = lax.rsqrt(jnp.mean(h * h, axis=-1, keepdims=True) + EPS)
        u_ref[...] = (h * r * g_ref[...]).astype(BF16)
        pl.when(i == nt - 1)(finish)

    tile = pl.BlockSpec((TM, D_MODEL), lambda i: (i, 0))
    return pl.pallas_call(
        body, name="rms_in", grid=(nt,),
        in_specs=[pl.BlockSpec((TM, D_MODEL), lambda i: (jnp.maximum(i - 1, 0), 0)),
                  pl.BlockSpec((T0, D_MODEL), lambda i: (0, 0)), pl.BlockSpec((1, D_MODEL), lambda i: (0, 0))]
        + [ANY] * n_xc,
        out_specs=[tile, tile] + [ANY] * n_xc,
        out_shape=[jax.ShapeDtypeStruct((tp, D_MODEL), F32), jax.ShapeDtypeStruct((tp, D_MODEL), BF16)]
        + list(comm.out_shapes),
        scratch_shapes=_comm_sems(comm), compiler_params=_cparams(1),
    )(x2d, head, gain, *comm.srcs)


PROJ_ROWS_MAX = 1408


def _proj_rows(m):
    return max(r for r in range(16, PROJ_ROWS_MAX + 1, 16) if m % r == 0)


def _mm_nn(name, a, b, out_dtype, tn, col0, ncols, epilogue=None, extras=(), extra_specs=()):
    m, k = a.shape
    nj, j0 = ncols // tn, col0 // tn
    tb = _proj_rows(m)

    def body(a_ref, b_ref, *rest):
        *ex, o_ref = rest
        acc = _dot(a_ref[...], b_ref[...], NN)
        if epilogue is None:
            o_ref[...] = acc.astype(out_dtype)
        else:
            epilogue(acc, o_ref, *ex)

    return pl.pallas_call(
        body, name=name, grid=(nj, m // tb),
        in_specs=[pl.BlockSpec((tb, k), lambda j, i: (i, 0)), pl.BlockSpec((k, tn), lambda j, i: (0, j0 + j))]
        + list(extra_specs),
        out_specs=pl.BlockSpec((tb, tn), lambda j, i: (i, j)),
        out_shape=jax.ShapeDtypeStruct((m, ncols), out_dtype),
        compiler_params=_cparams(2),
    )(a, b, *extras)


def _rope_tables(tp):
    half = RET_QK // 2
    pos = np.arange(tp, dtype=np.float32) - np.float32(PADF)
    inv = (ROPE_BASE ** (-np.arange(half, dtype=np.float64) / half)).astype(np.float32)
    ang = (pos[:, None] * inv[None, :]).astype(np.float64)
    return np.cos(ang).astype(np.float32), np.sin(ang).astype(np.float32)


def _rope_epilogue(acc, o_ref, cos_ref, sin_ref):
    scale = jnp.where(pl.program_id(0) == 1, RET_QK ** -0.5, 1.0).astype(F32)
    cos, sin = cos_ref[...], sin_ref[...]
    half = RET_QK // 2
    for h in range(RET_HEADS):
        t1 = acc[:, h * RET_QK:h * RET_QK + half]
        t2 = acc[:, h * RET_QK + half:(h + 1) * RET_QK]
        o_ref[:, h * RET_QK:h * RET_QK + half] = ((t1 * cos - t2 * sin) * scale).astype(BF16)
        o_ref[:, h * RET_QK + half:(h + 1) * RET_QK] = ((t2 * cos + t1 * sin) * scale).astype(BF16)


def _gqk_epilogue(acc, o_ref):
    o_ref[:, :GLA_KW] = acc[:, :GLA_KW] * (GLA_K ** -0.5)
    o_ref[:, GLA_KW:] = acc[:, GLA_KW:]


class _Comm(NamedTuple):
    srcs: tuple
    out_shapes: tuple
    n_sems: int
    make: Callable


def _comm_sems(comm):
    return [pltpu.SemaphoreType.DMA((comm.n_sems,)), pltpu.SemaphoreType.DMA((comm.n_sems,))]


def _start_wait(copies):
    def begin():
        for cp in copies:
            cp.start()

    def finish():
        for cp in copies:
            cp.wait()

    return begin, finish


def _other_chips(x, y):
    return [(1 - x, y), (x, 1 - y), (1 - x, 1 - y)]


def _gather_plan(parts, relay=()):
    n = len(parts)
    relay = tuple(relay) + (False,) * (n - len(relay))

    def make(x_refs, out_refs, send_sems, recv_sems):
        x, y, c = _place()
        me, sibling = (x, y, c), (x, y, 1 - c)
        xn, yn, dg = (1 - x, y), (x, 1 - y), (1 - x, 1 - y)

        def slot(t, px, py, pc, half=None):
            ref = out_refs[t].at[4 * px + 2 * py + pc]
            if half is None:
                return ref
            rows = ref.shape[0] // 2
            return ref.at[pl.ds(half * rows, rows)]

        def copy(t, k, dst, to, src=None):
            return pltpu.make_async_remote_copy(
                src_ref=dst if src is None else src, dst_ref=dst, send_sem=send_sems.at[8 * t + k],
                recv_sem=recv_sems.at[8 * t + k], device_id=to, device_id_type=MESH)

        mine = [pltpu.make_async_copy(x_refs[t], slot(t, *me), send_sems.at[8 * n + t]) for t in range(n)]
        sent = []
        for t in range(n):
            sent.append(copy(t, 0, slot(t, *me), sibling, src=x_refs[t]))
            sent.append(copy(t, 1, slot(t, *me), (*xn, c), src=x_refs[t]))
            sent.append(copy(t, 2, slot(t, *me), (*yn, c), src=x_refs[t]))
            if not relay[t]:
                sent.append(copy(t, 3, slot(t, *me), (*dg, c), src=x_refs[t]))

        def begin():
            for cp in mine + sent:
                cp.start()

        def finish():
            later = []

            def start(cp):
                cp.start()
                later.append(cp)

            for t in range(n):
                copy(t, 2, slot(t, *yn, c), me).wait_recv()
                if relay[t]:
                    start(copy(t, 3, slot(t, *yn, c, half=0), (*xn, c)))
                start(copy(t, 6, slot(t, *yn, c), sibling))
            for t in range(n):
                copy(t, 1, slot(t, *xn, c), me).wait_recv()
                if relay[t]:
                    start(copy(t, 4, slot(t, *xn, c, half=1), (*yn, c)))
                start(copy(t, 5, slot(t, *xn, c), sibling))
            for t in range(n):
                if relay[t]:
                    copy(t, 3, slot(t, *dg, c, half=0), me).wait_recv()
                    copy(t, 4, slot(t, *dg, c, half=1), me).wait_recv()
                else:
                    copy(t, 3, slot(t, *dg, c), me).wait_recv()
                start(copy(t, 7, slot(t, *dg, c), sibling))
            for t in range(n):
                copy(t, 0, slot(t, *sibling), me).wait_recv()
                copy(t, 5, slot(t, *xn, 1 - c), me).wait_recv()
                copy(t, 6, slot(t, *yn, 1 - c), me).wait_recv()
                copy(t, 7, slot(t, *dg, 1 - c), me).wait_recv()
            for cp in sent + later:
                cp.wait_send()
            for cp in mine:
                cp.wait()

        return begin, finish

    return _Comm(tuple(parts), tuple(jax.ShapeDtypeStruct((8,) + p.shape, p.dtype) for p in parts), 9 * n, make)


def _exchange_plan(ss):
    def make(s_refs, b_refs, send_sems, recv_sems):
        x, y, c = _place()
        return _start_wait([pltpu.make_async_remote_copy(
            src_ref=s_refs[t].at[2 * chip[0] + chip[1]], dst_ref=b_refs[t].at[j], send_sem=send_sems.at[3 * t + j],
            recv_sem=recv_sems.at[3 * t + j], device_id=(*chip, c), device_id_type=MESH)
            for t in range(len(s_refs)) for j, chip in enumerate(_other_chips(x, y))])

    return _Comm(tuple(ss), tuple(jax.ShapeDtypeStruct((3,) + s.shape[1:], s.dtype) for s in ss), 3 * len(ss), make)


def _exchange_window_plan(s):
    def make(s_refs, b_refs, send_sems, recv_sems):
        x, y, c = _place()
        return _start_wait([pltpu.make_async_remote_copy(
            src_ref=s_refs[0].at[:, pl.ds(pl.multiple_of((2 * chip[0] + chip[1]) * WIN_STEP, 128), WIN_W)],
            dst_ref=b_refs[0].at[j], send_sem=send_sems.at[j], recv_sem=recv_sems.at[j], device_id=(*chip, c),
            device_id_type=MESH) for j, chip in enumerate(_other_chips(x, y))])

    return _Comm((s,), (jax.ShapeDtypeStruct((3, s.shape[0], WIN_W), s.dtype),), 3, make)


def _swap_plan(gs):
    def make(g_refs, b_refs, send_sems, recv_sems):
        x, y, c = _place()
        return _start_wait([pltpu.make_async_remote_copy(
            src_ref=g_refs[t].at[1 - c], dst_ref=b_refs[t], send_sem=send_sems.at[t], recv_sem=recv_sems.at[t],
            device_id=(x, y, 1 - c), device_id_type=MESH) for t in range(len(g_refs))])

    return _Comm(tuple(gs), tuple(jax.ShapeDtypeStruct(g.shape[1:], g.dtype) for g in gs), len(gs), make)


def _spread_plan(parts):
    def make(p_refs, o_refs, send_sems, recv_sems):
        x, y, c = _place()
        copies = []
        for t in range(len(p_refs)):
            mine = o_refs[t].at[4 * x + 2 * y + c]
            copies.append(pltpu.make_async_copy(p_refs[t], mine, send_sems.at[7 * len(p_refs) + t]))
            for r in range(1, 8):
                peer = (1 - x if r & 4 else x, 1 - y if r & 2 else y, 1 - c if r & 1 else c)
                copies.append(pltpu.make_async_remote_copy(
                    src_ref=p_refs[t], dst_ref=mine, send_sem=send_sems.at[7 * t + r - 1],
                    recv_sem=recv_sems.at[7 * t + r - 1], device_id=peer, device_id_type=MESH))
        return _start_wait(copies)

    return _Comm(tuple(parts), tuple(jax.ShapeDtypeStruct((8,) + p.shape, p.dtype) for p in parts), 8 * len(parts),
                 make)


def _mm_nt_acc(name, a, w, tk, acc_in=None, epilogue=None, extras=(), extra_specs=(), extra_out_shapes=(),
               extra_out_specs=(), extra_scratch=(), comm=None, tb=TB):
    m, k = a.shape
    n = w.shape[0]
    nk, ni = k // tk, m // tb
    has_acc = acc_in is not None
    n_xc = len(comm.srcs) if comm else 0
    n_es = len(extra_scratch)

    def body(*refs):
        a_ref, w_ref = refs[0], refs[1]
        pos = 2
        acc_ref = None
        if has_acc:
            acc_ref = refs[pos]
            pos += 1
        ex = refs[pos:pos + len(extras)]
        pos += len(extras)
        xc_src = refs[pos:pos + n_xc]
        pos += n_xc
        n_scr = 1 + n_es + (2 if n_xc else 0)
        outs = refs[pos:len(refs) - n_scr - n_xc]
        xc_dst = refs[len(refs) - n_scr - n_xc:len(refs) - n_scr]
        scr = refs[len(refs) - n_scr]
        es = refs[len(refs) - n_scr + 1:len(refs) - n_scr + 1 + n_es]
        i, kk = pl.program_id(0), pl.program_id(1)
        if n_xc:
            begin, finish = comm.make(xc_src, xc_dst, refs[-2], refs[-1])
            pl.when((i == 0) & (kk == 0))(begin)

        @pl.when(kk == 0)
        def _():
            scr[...] = acc_ref[...] if has_acc else jnp.zeros_like(scr)

        scr[...] += _dot(a_ref[...], w_ref[...], NT)

        @pl.when(kk == nk - 1)
        def _():
            if epilogue is None:
                outs[0][...] = scr[...]
            else:
                epilogue(scr[...], outs, i, ni, *ex, *es)

        if n_xc:
            pl.when((i == ni - 1) & (kk == nk - 1))(finish)

    in_specs = [pl.BlockSpec((tb, tk), lambda i, kk: (i, kk)), pl.BlockSpec((n, tk), lambda i, kk: (0, kk))]
    args = [a, w]
    if has_acc:
        in_specs.append(pl.BlockSpec((tb, n), lambda i, kk: (i, 0)))
        args.append(acc_in)
    in_specs += list(extra_specs) + [ANY] * n_xc
    args += list(extras) + (list(comm.srcs) if comm else [])
    if epilogue is None:
        out_shape = [jax.ShapeDtypeStruct((m, n), F32)]
        out_specs = [pl.BlockSpec((tb, n), lambda i, kk: (i, 0))]
    else:
        out_shape, out_specs = list(extra_out_shapes), list(extra_out_specs)
    scratch = [pltpu.VMEM((tb, n), F32)] + list(extra_scratch)
    if n_xc:
        out_shape += list(comm.out_shapes)
        out_specs += [ANY] * n_xc
        scratch += _comm_sems(comm)
    return pl.pallas_call(
        body, name=name, grid=(ni, nk), in_specs=in_specs, out_specs=out_specs, out_shape=out_shape,
        scratch_shapes=scratch, compiler_params=_cparams(2),
    )(*args)


def _rms_bwd_epilogue(du, outs, i, ni, h_ref, g_ref, dh1_ref, obuf, sems):
    dx_ref, dmeta_ref, dg_ref = outs
    h = h_ref[...]
    r = lax.rsqrt(jnp.mean(h * h, axis=-1, keepdims=True) + EPS)
    xh = h * r
    dxh = du * g_ref[...]
    dh0 = dh1_ref[...] + r * (dxh - xh * jnp.mean(dxh * xh, axis=-1, keepdims=True))

    def put(slot, tile):
        return pltpu.make_async_copy(obuf.at[slot], dx_ref.at[pl.ds(pl.multiple_of(tile * TB - T0, 8), TB)],
                                     sems.at[slot])

    @pl.when(i == 0)
    def _():
        dg_ref[...] = jnp.zeros_like(dg_ref)
        dmeta_ref[...] = dh0[PADF:T0, :]
        obuf[0] = dh0
        first = pltpu.make_async_copy(obuf.at[0, pl.ds(T0, TB - T0)], dx_ref.at[pl.ds(0, TB - T0)], sems.at[0])
        first.start()
        first.wait()

    @pl.when(i >= 1)
    def _():
        slot = i % 2

        @pl.when(i >= 3)
        def _():
            put(slot, i - 2).wait()

        obuf[slot] = dh0
        put(slot, i).start()

    dg_ref[...] += jnp.sum(du * xh, axis=0, keepdims=True)

    @pl.when(i == ni - 1)
    def _():
        for tile in (ni - 2, ni - 1):
            if tile >= 1:
                put(tile % 2, tile).wait()


def _mm_tn(name, a, b, bn, ncols=None, bcol0=0, into=None, col0=0, out_cols=None):
    t, m = a.shape
    n = ncols or b.shape[1]
    j0, bj0 = col0 // bn, bcol0 // bn

    def body(a_ref, b_ref, *rest):
        o_ref = rest[-1]

        @pl.when(pl.program_id(1) == 0)
        def _():
            o_ref[...] = jnp.zeros_like(o_ref)

        o_ref[...] += _dot(a_ref[...], b_ref[...], TN)

    in_specs = [pl.BlockSpec((TK, m), lambda j, kk: (kk, 0)), pl.BlockSpec((TK, bn), lambda j, kk: (kk, bj0 + j))]
    args = [a, b]
    aliases = {}
    if into is not None:
        in_specs.append(ANY)
        args.append(into)
        aliases = {2: 0}
        out_cols = into.shape[1]
    return pl.pallas_call(
        body, name=name, grid=(n // bn, t // TK), in_specs=in_specs,
        out_specs=pl.BlockSpec((m, bn), lambda j, kk: (0, j0 + j)),
        out_shape=jax.ShapeDtypeStruct((m, out_cols or n), F32), input_output_aliases=aliases,
        compiler_params=_cparams(2),
    )(*args)


def _place_merge_cols_call(dwp, dw_m, dw_glr):
    c0 = W_R + W_GP - 128
    tail = IN_PAD - c0
    rows = 256

    def body(m_ref, low, p_ref, o_ref, buf, sem):
        for r in range(0, D_MODEL, rows):
            buf[r:r + rows, :] = jnp.concatenate(
                [low[r:r + rows, :GATE_RANK], m_ref[r:r + rows, :],
                 jnp.zeros((rows, tail - GATE_RANK - W_M), F32)], axis=1)
        put = pltpu.make_async_copy(buf, o_ref.at[:, pl.ds(c0, tail)], sem)
        put.start()
        put.wait()

    return pl.pallas_call(
        body, name="place_merge_cols",
        in_specs=[pl.BlockSpec(memory_space=pltpu.VMEM), pl.BlockSpec(memory_space=pltpu.VMEM), ANY], out_specs=ANY,
        out_shape=jax.ShapeDtypeStruct(dwp.shape, F32), input_output_aliases={2: 0},
        scratch_shapes=[pltpu.VMEM((D_MODEL, tail), F32), pltpu.SemaphoreType.DMA],
        compiler_params=pltpu.CompilerParams(vmem_limit_bytes=VMEM_LIMIT),
    )(dw_m, dw_glr, dwp)


def _ret_fill_decay(lg_ref, dm_scr):
    c = TM
    ii = lax.broadcasted_iota(jnp.int32, (c, c), 0)
    jj = lax.broadcasted_iota(jnp.int32, (c, c), 1)
    rel = (ii - jj).astype(F32)
    for h in range(RET_HEADS):
        dm_scr[h] = jnp.where(rel >= 0, jnp.exp(jnp.maximum(rel, 0.0) * lg_ref[h]), 0.0)


def _ret_consts(lg, dm_ref):
    c = TM
    idx = lax.broadcasted_iota(jnp.int32, (c, 1), 0).astype(F32)
    xi = jnp.exp((idx + 1.0) * lg)
    zeta = jnp.exp((c - 1.0 - idx) * lg)
    gc = jnp.exp(jnp.full((1, 1), c, F32) * lg)
    return dm_ref[...], xi, zeta, gc


def _ret_fwd_call(rqk, rv, rg, gain, lgam):
    tp = rqk.shape[0]
    nc = tp // TM

    def body(lg_ref, qk_ref, v_ref, rg_ref, g_ref, o_ref, a_ref, st_ref, sc_ref, s_scr, dm_scr):
        @pl.when(pl.program_id(0) == 0)
        def _():
            s_scr[...] = jnp.zeros_like(s_scr)
            _ret_fill_decay(lg_ref, dm_scr)

        for h in range(RET_HEADS):
            dm, xi, zeta, gc = _ret_consts(lg_ref[h], dm_scr.at[h])
            q = qk_ref[:, h * RET_QK:(h + 1) * RET_QK]
            k = qk_ref[:, D_MODEL + h * RET_QK:D_MODEL + (h + 1) * RET_QK]
            v = v_ref[:, h * RET_V:(h + 1) * RET_V]
            sb = s_scr[h].astype(BF16)
            st_ref[0, h] = sb
            s = (_dot(q, k, NT) * dm).astype(BF16)
            sc_ref[0, h] = s
            o = _dot(s, v, NN) + xi * _dot(q, sb, NN)
            kz = (k.astype(F32) * zeta).astype(BF16)
            s_scr[h] = gc * s_scr[h] + _dot(kz, v, TN)
            o_ref[:, h * RET_V:(h + 1) * RET_V] = o
            mu = _head_mean(o)
            xc = o - mu
            xh = xc * lax.rsqrt(_head_mean(xc * xc) + EPS)
            a_ref[:, h * RET_V:(h + 1) * RET_V] = (
                xh * g_ref[:, h * RET_V:(h + 1) * RET_V] * _silu(rg_ref[:, h * RET_V:(h + 1) * RET_V])).astype(BF16)

    return pl.pallas_call(
        body, name="ret_fwd", grid=(nc,),
        in_specs=[pl.BlockSpec(memory_space=pltpu.SMEM),
                  pl.BlockSpec((TM, 2 * D_MODEL), lambda n: (n, 0)),
                  pl.BlockSpec((TM, RET_W), lambda n: (n, 0)),
                  pl.BlockSpec((TM, RET_W), lambda n: (n, 0)),
                  pl.BlockSpec((1, RET_W), lambda n: (0, 0))],
        out_specs=[pl.BlockSpec((TM, RET_W), lambda n: (n, 0)),
                   pl.BlockSpec((TM, RET_W), lambda n: (n, 0)),
                   pl.BlockSpec((1, RET_HEADS, RET_QK, RET_V), lambda n: (n, 0, 0, 0)),
                   pl.BlockSpec((1, RET_HEADS, TM, TM), lambda n: (n, 0, 0, 0))],
        out_shape=[jax.ShapeDtypeStruct((tp, RET_W), F32), jax.ShapeDtypeStruct((tp, RET_W), BF16),
                   jax.ShapeDtypeStruct((nc, RET_HEADS, RET_QK, RET_V), BF16),
                   jax.ShapeDtypeStruct((nc, RET_HEADS, TM, TM), BF16)],
        scratch_shapes=[pltpu.VMEM((RET_HEADS, RET_QK, RET_V), F32), pltpu.VMEM((RET_HEADS, TM, TM), F32)],
        compiler_params=_cparams(1),
    )(lgam, rqk, rv, rg, gain)


def _ret_bwd_call(rqk, rv, rg, o_ret, dpr, wbr, states, scores, gain, lgam, cos, sin):
    tp = rqk.shape[0]
    nc = tp // TM
    half = RET_QK // 2

    def body(lg_ref, qk_ref, v_ref, rg_ref, o_ref, dpr_ref, wbr_ref, st_ref, sc_ref, g_ref, cos_ref, sin_ref, dp_ref,
             dg_ref, ds_scr, dm_scr):
        @pl.when(pl.program_id(0) == 0)
        def _():
            ds_scr[...] = jnp.zeros_like(ds_scr)
            dg_ref[...] = jnp.zeros_like(dg_ref)
            _ret_fill_decay(lg_ref, dm_scr)

        cos, sin = cos_ref[...], sin_ref[...]
        for h in range(RET_HEADS):
            hs = slice(h * RET_V, (h + 1) * RET_V)
            dm, xi, zeta, gc = _ret_consts(lg_ref[h], dm_scr.at[h])
            o = o_ref[:, hs]
            mu = _head_mean(o)
            xc = o - mu
            rstd = lax.rsqrt(_head_mean(xc * xc) + EPS)
            xh = xc * rstd
            gain_h = g_ref[:, hs]
            g = rg_ref[:, hs]
            sg = _sigmoid(g)
            silu = g * sg
            dah = _dot(dpr_ref[...], wbr_ref[hs, :], NT)
            dp_ref[:, 4 * D_MODEL + h * RET_V:4 * D_MODEL + (h + 1) * RET_V] = (
                dah * (xh * gain_h) * (sg * (1.0 + g * (1.0 - sg)))).astype(BF16)
            dn = dah * silu
            dg_ref[:, hs] += jnp.sum(dn * xh, axis=0, keepdims=True)
            dxh = dn * gain_h
            do = rstd * (dxh - _head_mean(dxh) - xh * _head_mean(dxh * xh))
            dob = do.astype(BF16)
            q = qk_ref[:, h * RET_QK:(h + 1) * RET_QK]
            k = qk_ref[:, D_MODEL + h * RET_QK:D_MODEL + (h + 1) * RET_QK]
            v = v_ref[:, hs]
            sp = st_ref[0, h]
            ds = ds_scr[h]
            dsb = ds.astype(BF16)
            s = sc_ref[0, h]
            dsc = (_dot(dob, v, NT) * dm).astype(BF16)
            dq = _dot(dsc, k, NN) + xi * _dot(dob, sp, NT)
            dk = _dot(dsc, q, TN) + zeta * _dot(v, dsb, NT)
            kz = (k.astype(F32) * zeta).astype(BF16)
            dv = _dot(s, dob, TN) + _dot(kz, dsb, NN)
            qx = (q.astype(F32) * xi).astype(BF16)
            ds_scr[h] = gc * ds + _dot(qx, dob, TN)
            dp_ref[:, 2 * D_MODEL + h * RET_V:2 * D_MODEL + (h + 1) * RET_V] = dv.astype(BF16)
            dk = dk * (RET_QK ** -0.5)
            for base, t in ((0, dq), (D_MODEL, dk)):
                t1, t2 = t[:, :half], t[:, half:]
                dp_ref[:, base + h * RET_QK:base + h * RET_QK + half] = (t1 * cos + t2 * sin).astype(BF16)
                dp_ref[:, base + h * RET_QK + half:base + (h + 1) * RET_QK] = (t2 * cos - t1 * sin).astype(BF16)

    rev = lambda n: (nc - 1 - n, 0)
    return pl.pallas_call(
        body, name="ret_bwd", grid=(nc,),
        in_specs=[pl.BlockSpec(memory_space=pltpu.SMEM),
                  pl.BlockSpec((TM, 2 * D_MODEL), rev),
                  pl.BlockSpec((TM, RET_W), rev),
                  pl.BlockSpec((TM, RET_W), rev),
                  pl.BlockSpec((TM, RET_W), rev),
                  pl.BlockSpec((TM, D_MODEL), rev),
                  pl.BlockSpec((RET_W, D_MODEL), lambda n: (0, 0)),
                  pl.BlockSpec((1, RET_HEADS, RET_QK, RET_V), lambda n: (nc - 1 - n, 0, 0, 0)),
                  pl.BlockSpec((1, RET_HEADS, TM, TM), lambda n: (nc - 1 - n, 0, 0, 0)),
                  pl.BlockSpec((1, RET_W), lambda n: (0, 0)),
                  pl.BlockSpec((TM, half), rev),
                  pl.BlockSpec((TM, half), rev)],
        out_specs=[pl.BlockSpec((TM, W_R), rev), pl.BlockSpec((1, RET_W), lambda n: (0, 0))],
        out_shape=[jax.ShapeDtypeStruct((tp, W_R), BF16), jax.ShapeDtypeStruct((1, RET_W), F32)],
        scratch_shapes=[pltpu.VMEM((RET_HEADS, RET_QK, RET_V), F32), pltpu.VMEM((RET_HEADS, TM, TM), F32)],
        compiler_params=_cparams(1),
    )(lgam, rqk, rv, rg, o_ret, dpr, wbr, states, scores, gain, cos, sin)


GLA_LEVELS = tuple(GC >> (s + 1) for s in range(int(math.log2(GC // GLA_SUB))))
NLEV = len(GLA_LEVELS)


def _gla_tril():
    return np.tril(np.ones((GC, GC), np.float32))


def _gla_masks():
    ii = lax.broadcasted_iota(jnp.int32, (GC, GC), 0)
    jj = lax.broadcasted_iota(jnp.int32, (GC, GC), 1)
    masks = []
    for m in GLA_LEVELS:
        sh = int(math.log2(2 * m))
        masks.append(((ii >> sh) == (jj >> sh)) & ((ii & m) != 0) & ((jj & m) == 0))
    sh = int(math.log2(GLA_SUB))
    md = ((ii >> sh) == (jj >> sh)) & (jj <= ii)
    row = lax.broadcasted_iota(jnp.int32, (GC, 1), 0)
    second = [(row & m) != 0 for m in GLA_LEVELS]
    return masks, md, second


def _gla_gate_call(u, w_g, wg, bg, pmat):
    tp = u.shape[0]
    gb = _proj_rows(tp)
    assert gb % GC == 0

    def body(u_ref, w_ref, wg_ref, bg_ref, p_ref, glr_ref, z_ref, b_ref):
        glr = _dot(u_ref[...], w_ref[...], NN)
        glr_ref[...] = glr
        z = _dot(glr.astype(BF16), wg_ref[...], NN) + bg_ref[...]
        z_ref[...] = z
        la = (jnp.minimum(z, 0.0) - jnp.log1p(jnp.exp(-jnp.abs(z)))) * (1.0 / GATE_TAU)
        for r in range(0, gb, GC):
            b_ref[r:r + GC, :] = _exact_pm(p_ref[...], la[r:r + GC, :])

    tile = pl.BlockSpec((gb, GLA_KW), lambda i: (i, 0))
    return pl.pallas_call(
        body, name="gla_gate", grid=(tp // gb,),
        in_specs=[pl.BlockSpec((gb, D_MODEL), lambda i: (i, 0)),
                  pl.BlockSpec((D_MODEL, 128), lambda i: (0, (W_GP - 128) // 128)),
                  pl.BlockSpec((128, GLA_KW), lambda i: (0, 0)),
                  pl.BlockSpec((1, GLA_KW), lambda i: (0, 0)), pl.BlockSpec((GC, GC), lambda i: (0, 0))],
        out_specs=[pl.BlockSpec((gb, 128), lambda i: (i, 0)), tile, tile],
        out_shape=[jax.ShapeDtypeStruct((tp, 128), F32), jax.ShapeDtypeStruct((tp, GLA_KW), F32),
                   jax.ShapeDtypeStruct((tp, GLA_KW), F32)],
        compiler_params=_cparams(1),
    )(u, w_g, wg, bg, pmat)


def _gla_gate_bwd_call(db, z, glr, u, wg, pmat_t, d_g):
    tp = db.shape[0]
    gb = _proj_rows(tp)
    assert gb % GC == 0 and (W_GP - 128) % 128 == 0

    def body(db_ref, z_ref, glr_ref, u_ref, wg_ref, pt_ref, dgin_ref, dg_ref, dwg_ref, dbg_ref, dwl_ref):
        i = pl.program_id(0)

        @pl.when(i == 0)
        def _():
            dwg_ref[...] = jnp.zeros_like(dwg_ref)
            dbg_ref[...] = jnp.zeros_like(dbg_ref)
            dwl_ref[...] = jnp.zeros_like(dwl_ref)

        dla = jnp.concatenate([_exact_pm(pt_ref[...], db_ref[r:r + GC, :]) for r in range(0, gb, GC)], axis=0)
        row = i * gb + lax.broadcasted_iota(jnp.int32, (gb, 1), 0)
        dz = jnp.where(row >= PADF, dla * (1.0 / GATE_TAU) * _sigmoid(-z_ref[...]), 0.0)
        dzb = dz.astype(BF16)
        dglr = _dot(dzb, wg_ref[...], NT).astype(BF16)
        dg_ref[...] = dglr
        dwg_ref[...] += _dot(glr_ref[...].astype(BF16), dzb, TN)
        dbg_ref[...] += jnp.sum(dz, axis=0, keepdims=True)
        dwl_ref[...] += _dot(u_ref[...], dglr, TN)

    tile = pl.BlockSpec((gb, GLA_KW), lambda i: (i, 0))
    const = lambda i: (0, 0)
    return pl.pallas_call(
        body, name="gla_gate_bwd", grid=(tp // gb,),
        in_specs=[tile, tile, pl.BlockSpec((gb, 128), lambda i: (i, 0)), pl.BlockSpec((gb, D_MODEL), lambda i: (i, 0)),
                  pl.BlockSpec((128, GLA_KW), const), pl.BlockSpec((GC, GC), const), ANY],
        out_specs=[pl.BlockSpec((gb, 128), lambda i: (i, (W_GP - 128) // 128)), pl.BlockSpec((128, GLA_KW), const),
                   pl.BlockSpec((1, GLA_KW), const), pl.BlockSpec((D_MODEL, 128), const)],
        out_shape=[jax.ShapeDtypeStruct(d_g.shape, BF16), jax.ShapeDtypeStruct((128, GLA_KW), F32),
                   jax.ShapeDtypeStruct((1, GLA_KW), F32), jax.ShapeDtypeStruct((D_MODEL, 128), F32)],
        input_output_aliases={6: 0}, compiler_params=_cparams(1),
    )(db, z, glr, u, wg, pmat_t, d_g)


def _gla_row_steps(b_ref, cs, rows, size):
    parts = [jnp.zeros((size, GLA_K), F32) if r is None else jnp.broadcast_to(b_ref[r:r + 1, cs], (size, GLA_K))
             for r in rows]
    return parts[0] if len(parts) == 1 else jnp.concatenate(parts, axis=0)


def _gla_factors(b_ref, h, second):
    cs = slice(h * GLA_K, (h + 1) * GLA_K)
    b = b_ref[:, cs]
    fq, fk = [], []
    for l, m in enumerate(GLA_LEVELS):
        d = b - _gla_row_steps(b_ref, cs, [s + m - 1 for s in range(0, GC, 2 * m)], 2 * m)
        f = jnp.exp(jnp.where(second[l], d, -d))
        fq.append(jnp.where(second[l], f, 0.0))
        fk.append(jnp.where(second[l], 0.0, f))
    dd = b - _gla_row_steps(b_ref, cs, [None] + [s - 1 for s in range(GLA_SUB, GC, GLA_SUB)], GLA_SUB)
    ed = jnp.exp(dd)
    edi = jnp.exp(-dd)
    eb = jnp.exp(b)
    bl = b_ref[GC - 1:GC, cs]
    ee = jnp.exp(bl - b)
    ebl = jnp.exp(bl)
    return fq, fk, ed, edi, eb, ee, ebl


def _gla_scaled(q, k, fq, fk, ed, edi):
    qt = [(q * f).astype(BF16) for f in fq]
    kt = [(k * f).astype(BF16) for f in fk]
    return qt, kt, (q * ed).astype(BF16), (k * edi).astype(BF16)


def _gla_scores(qt, kt, qd, kd, masks, md):
    a = jnp.where(md, _dot(qd, kd, NT), 0.0)
    for l in range(NLEV):
        a = a + jnp.where(masks[l], _dot(qt[l], kt[l], NT), 0.0)
    return a.astype(BF16)


def _gla_fwd_call(gqk, gv, b, gg, gain, comm=None):
    tp = gqk.shape[0]
    nc = tp // GC
    ns = nc // GS
    n_xc = len(comm.srcs) if comm else 0

    def body(qk_ref, v_ref, b_ref, gg_ref, g_ref, *rest):
        xc_src = rest[:n_xc]
        o_ref, a_ref, st_ref, am_ref = rest[n_xc:n_xc + 4]
        xc_dst = rest[n_xc + 4:2 * n_xc + 4]
        s_scr = rest[2 * n_xc + 4]
        n = pl.program_id(0)
        if n_xc:
            begin, finish = comm.make(xc_src, xc_dst, rest[-2], rest[-1])
            pl.when(n == 0)(begin)
            pl.when(n == ns - 1)(finish)

        @pl.when(n == 0)
        def _():
            s_scr[...] = jnp.zeros_like(s_scr)

        masks, md, second = _gla_masks()
        for cc in range(GS):
            rows = pl.ds(cc * GC, GC)
            qk_c, v_c, b_c, gg_c, o_c, a_c = (r.at[rows] for r in (qk_ref, v_ref, b_ref, gg_ref, o_ref, a_ref))
            for h in range(GLA_HEADS):
                q = qk_c[:, h * GLA_K:(h + 1) * GLA_K]
                k = qk_c[:, GLA_KW + h * GLA_K:GLA_KW + (h + 1) * GLA_K]
                vs = slice(h * GLA_V, (h + 1) * GLA_V)
                v = v_c[:, vs]
                fq, fk, ed, edi, eb, ee, ebl = _gla_factors(b_c, h, second)
                a = _gla_scores(*_gla_scaled(q, k, fq, fk, ed, edi), masks, md)
                am_ref[cc, h] = a
                sb = s_scr[h].astype(BF16)
                st_ref[cc, h] = sb
                o = _dot(a, v, NN) + _dot((q * eb).astype(BF16), sb, NT)
                s_scr[h] = s_scr[h] * ebl + _dot(v, (k * ee).astype(BF16), TN)
                o_c[:, vs] = o
                xh = o * lax.rsqrt(_head_mean(o * o) + EPS)
                a_c[:, vs] = (xh * g_ref[:, vs] * _silu(gg_c[:, vs])).astype(BF16)

    return pl.pallas_call(
        body, name="gla_fwd", grid=(ns,),
        in_specs=[pl.BlockSpec((GS * GC, 2 * GLA_KW), lambda n: (n, 0)),
                  pl.BlockSpec((GS * GC, GLA_W), lambda n: (n, 0)),
                  pl.BlockSpec((GS * GC, GLA_KW), lambda n: (n, 0)),
                  pl.BlockSpec((GS * GC, GLA_W), lambda n: (n, 0)),
                  pl.BlockSpec((1, GLA_W), lambda n: (0, 0))] + [ANY] * n_xc,
        out_specs=[pl.BlockSpec((GS * GC, GLA_W), lambda n: (n, 0)),
                   pl.BlockSpec((GS * GC, GLA_W), lambda n: (n, 0)),
                   pl.BlockSpec((GS, GLA_HEADS, GLA_V, GLA_K), lambda n: (n, 0, 0, 0)),
                   pl.BlockSpec((GS, GLA_HEADS, GC, GC), lambda n: (n, 0, 0, 0))] + [ANY] * n_xc,
        out_shape=[jax.ShapeDtypeStruct((tp, GLA_W), F32), jax.ShapeDtypeStruct((tp, GLA_W), BF16),
                   jax.ShapeDtypeStruct((nc, GLA_HEADS, GLA_V, GLA_K), BF16),
                   jax.ShapeDtypeStruct((nc, GLA_HEADS, GC, GC), BF16)] + (list(comm.out_shapes) if comm else []),
        scratch_shapes=[pltpu.VMEM((GLA_HEADS, GLA_V, GLA_K), F32)] + (_comm_sems(comm) if comm else []),
        compiler_params=_cparams(1),
    )(gqk, gv, b, gg, gain, *(comm.srcs if comm else ()))


def _gla_bwd_call(gqk, gv, b, gg, o_gla, da, states, scores, gain, comm=None):
    tp = gqk.shape[0]
    nc = tp // GC
    ns = nc // GS
    o_gv, o_gg = 2 * GLA_KW, 2 * GLA_KW + GLA_W
    n_xc = len(comm.srcs) if comm else 0

    def body(qk_all, v_all, b_all, gg_all, o_all, da_all, st_ref, am_ref, g_ref, *rest):
        xc_src = rest[:n_xc]
        dp_all, db_all, dg_ref = rest[n_xc:n_xc + 3]
        xc_dst = rest[n_xc + 3:2 * n_xc + 3]
        ds_scr = rest[2 * n_xc + 3]
        n = pl.program_id(0)
        if n_xc:
            begin, finish = comm.make(xc_src, xc_dst, rest[-2], rest[-1])
            pl.when(n == 0)(begin)
            pl.when(n == ns - 1)(finish)

        @pl.when(n == 0)
        def _():
            ds_scr[...] = jnp.zeros_like(ds_scr)
            dg_ref[...] = jnp.zeros_like(dg_ref)

        masks, md, second = _gla_masks()
        for cc, h in [(cc, h) for cc in reversed(range(GS)) for h in range(GLA_HEADS)]:
            rows = pl.ds(cc * GC, GC)
            qk_ref, v_ref, b_scr, gg_ref, o_ref, da_ref, dp_ref, db_scr = (
                r.at[rows] for r in (qk_all, v_all, b_all, gg_all, o_all, da_all, dp_all, db_all))
            cs = slice(h * GLA_K, (h + 1) * GLA_K)
            vs = slice(h * GLA_V, (h + 1) * GLA_V)
            o = o_ref[:, vs]
            rstd = lax.rsqrt(_head_mean(o * o) + EPS)
            xh = o * rstd
            gain_h = g_ref[:, vs]
            g = gg_ref[:, vs]
            sg = _sigmoid(g)
            dah = da_ref[:, vs]
            dp_ref[:, o_gg + h * GLA_V:o_gg + (h + 1) * GLA_V] = (
                dah * (xh * gain_h) * (sg * (1.0 + g * (1.0 - sg)))).astype(BF16)
            dn = dah * (g * sg)
            dg_ref[:, vs] += jnp.sum(dn * xh, axis=0, keepdims=True)
            dxh = dn * gain_h
            do = rstd * (dxh - xh * _head_mean(dxh * xh))
            dob = do.astype(BF16)
            q = qk_ref[:, cs]
            k = qk_ref[:, GLA_KW + h * GLA_K:GLA_KW + (h + 1) * GLA_K]
            v = v_ref[:, vs]
            fq, fk, ed, edi, eb, ee, ebl = _gla_factors(b_scr, h, second)
            qt, kt, qd, kd = _gla_scaled(q, k, fq, fk, ed, edi)
            sp = st_ref[cc, h]
            ds = ds_scr[h]
            dsb = ds.astype(BF16)
            q_in = q * eb
            k_end = k * ee
            da_s = _dot(dob, v, NT)
            dv = _dot(am_ref[cc, h], dob, TN) + _dot(k_end.astype(BF16), dsb, NT)
            dq_in = _dot(dob, sp, NN)
            dk_end = _dot(v, dsb, NN)
            dbl = jnp.sum(sp.astype(F32) * ds, axis=0, keepdims=True) * ebl
            ds_scr[h] = ds * ebl + _dot(dob, q_in.astype(BF16), TN)
            dq = dq_in * eb
            dk = dk_end * ee
            de_end = dk_end * k_end
            db = dq_in * q_in - de_end
            placed = [(GC - 1, jnp.sum(de_end, axis=0, keepdims=True) + dbl)]
            for l, m in enumerate(GLA_LEVELS):
                dal = jnp.where(masks[l], da_s, 0.0).astype(BF16)
                dqt = _dot(dal, kt[l], NN)
                dkt = _dot(dal, qt[l], TN)
                dq = dq + dqt * fq[l]
                dk = dk + dkt * fk[l]
                gl = dqt * (q * fq[l]) - dkt * (k * fk[l])
                db = db + gl
                placed += [(s + m - 1, -jnp.sum(gl[s:s + 2 * m], axis=0, keepdims=True)) for s in range(0, GC, 2 * m)]
            dad = jnp.where(md, da_s, 0.0).astype(BF16)
            dqd = _dot(dad, kd, NN)
            dkd = _dot(dad, qd, TN)
            dq = dq + dqd * ed
            dk = dk + dkd * edi
            gd = dqd * (q * ed) - dkd * (k * edi)
            db = db + gd
            placed += [(s - 1, -jnp.sum(gd[s:s + GLA_SUB], axis=0, keepdims=True)) for s in range(GLA_SUB, GC, GLA_SUB)]
            db_scr[:, cs] = db
            for r, val in placed:
                db_scr[r:r + 1, cs] += val
            dp_ref[:, cs] = (dq * (GLA_K ** -0.5)).astype(BF16)
            dp_ref[:, GLA_KW + h * GLA_K:GLA_KW + (h + 1) * GLA_K] = dk.astype(BF16)
            dp_ref[:, o_gv + h * GLA_V:o_gv + (h + 1) * GLA_V] = dv.astype(BF16)

    rev = lambda n: (ns - 1 - n, 0)
    const = lambda n: (0, 0)
    xc_shapes, xc_sems = (list(comm.out_shapes), _comm_sems(comm)) if n_xc else ([], [])
    return pl.pallas_call(
        body, name="gla_bwd", grid=(ns,),
        in_specs=[pl.BlockSpec((GS * GC, 2 * GLA_KW), rev),
                  pl.BlockSpec((GS * GC, GLA_W), rev),
                  pl.BlockSpec((GS * GC, GLA_KW), rev),
                  pl.BlockSpec((GS * GC, GLA_W), rev),
                  pl.BlockSpec((GS * GC, GLA_W), rev),
                  pl.BlockSpec((GS * GC, GLA_W), rev),
                  pl.BlockSpec((GS, GLA_HEADS, GLA_V, GLA_K), lambda n: (ns - 1 - n, 0, 0, 0)),
                  pl.BlockSpec((GS, GLA_HEADS, GC, GC), lambda n: (ns - 1 - n, 0, 0, 0)),
                  pl.BlockSpec((1, GLA_W), const)] + [ANY] * n_xc,
        out_specs=[pl.BlockSpec((GS * GC, W_GP), rev), pl.BlockSpec((GS * GC, GLA_KW), rev),
                   pl.BlockSpec((1, GLA_W), const)] + [ANY] * n_xc,
        out_shape=[jax.ShapeDtypeStruct((tp, W_GP), BF16), jax.ShapeDtypeStruct((tp, GLA_KW), F32),
                   jax.ShapeDtypeStruct((1, GLA_W), F32)] + xc_shapes,
        scratch_shapes=[pltpu.VMEM((GLA_HEADS, GLA_V, GLA_K), F32)] + xc_sems,
        compiler_params=_cparams(1),
    )(gqk, gv, b, gg, o_gla, da, states, scores, gain, *(comm.srcs if comm else ()))


def _mid_call(a_ret, a_gla, mg, h0, tgt, wbr, wbg, wout, gf):
    tp = h0.shape[0]
    nt = tp // TM

    def body(ar_ref, ag_ref, mg_ref, h_ref, t_ref, wbr_ref, wbg_ref, wo_ref, gf_ref,
             dh1_ref, dag_ref, dm_ref, mb_ref, dh1b_ref, dprb_ref, dpgb_ref, loss_ref, dgf_ref):
        i = pl.program_id(0)

        @pl.when(i == 0)
        def _():
            loss_ref[...] = jnp.zeros_like(loss_ref)
            dgf_ref[...] = jnp.zeros_like(dgf_ref)

        ar, ag = ar_ref[...], ag_ref[...]
        pr = _dot(ar, wbr_ref[...], NN)
        pg = _dot(ag, wbg_ref[...], NN)
        sr = _sigmoid(mg_ref[:, :D_MODEL])
        sg = _sigmoid(mg_ref[:, D_MODEL:])
        merged = (sr * pr + sg * pg).astype(BF16)
        mb_ref[...] = merged
        h1 = h_ref[...] + _dot(merged, wo_ref[...], NN)
        r1 = lax.rsqrt(jnp.mean(h1 * h1, axis=-1, keepdims=True) + EPS)
        xh = h1 * r1
        gfv = gf_ref[...]
        live = jnp.where(i > 0, 1.0, 0.0).astype(F32)
        err = (xh * gfv - t_ref[...]) * live
        loss_ref[...] += jnp.full(loss_ref.shape, 0.5 / D_MODEL, F32) * jnp.sum(err * err)
        dy = err * (1.0 / D_MODEL)
        dgf_ref[...] += jnp.sum(dy * xh, axis=0, keepdims=True)
        dxh = dy * gfv
        dh1 = r1 * (dxh - xh * jnp.mean(dxh * xh, axis=-1, keepdims=True))
        dh1_ref[...] = dh1
        dh1b = dh1.astype(BF16)
        dh1b_ref[...] = dh1b
        dmerged = _dot(dh1b, wo_ref[...], NT)
        dm_ref[:, :D_MODEL] = (dmerged * pr * sr * (1.0 - sr)).astype(BF16)
        dm_ref[:, D_MODEL:] = (dmerged * pg * sg * (1.0 - sg)).astype(BF16)
        dpr = (dmerged * sr).astype(BF16)
        dpg = (dmerged * sg).astype(BF16)
        dprb_ref[...] = dpr
        dpgb_ref[...] = dpg
        dag_ref[...] = _dot(dpg, wbg_ref[...], NT)

    tile = lambda w: pl.BlockSpec((TM, w), lambda i: (i, 0))
    const = lambda r, w: pl.BlockSpec((r, w), lambda i: (0, 0))
    return pl.pallas_call(
        body, name="merge_out_loss", grid=(nt,),
        in_specs=[tile(RET_W), tile(GLA_W), tile(W_M), tile(D_MODEL),
                  pl.BlockSpec((TM, D_MODEL), lambda i: (jnp.maximum(i - 1, 0), 0)),
                  const(RET_W, D_MODEL), const(GLA_W, D_MODEL), const(D_MODEL, D_MODEL), const(1, D_MODEL)],
        out_specs=[tile(D_MODEL), tile(GLA_W), tile(W_M), tile(D_MODEL), tile(D_MODEL), tile(D_MODEL),
                   tile(D_MODEL), const(1, 128), const(1, D_MODEL)],
        out_shape=[jax.ShapeDtypeStruct((tp, D_MODEL), F32), jax.ShapeDtypeStruct((tp, GLA_W), F32),
                   jax.ShapeDtypeStruct((tp, W_M), BF16),
                   jax.ShapeDtypeStruct((tp, D_MODEL), BF16), jax.ShapeDtypeStruct((tp, D_MODEL), BF16),
                   jax.ShapeDtypeStruct((tp, D_MODEL), BF16), jax.ShapeDtypeStruct((tp, D_MODEL), BF16),
                   jax.ShapeDtypeStruct((1, 128), F32), jax.ShapeDtypeStruct((1, D_MODEL), F32)],
        compiler_params=_cparams(1),
    )(a_ret, a_gla, mg, h0, tgt, wbr, wbg, wout, gf)


def _device_step(x2d, tgt2d, meta, norm_gain, w_in_part, w_gate_up, b_gate, ret_gain, gla_gain, branch_parts,
                 final_gain, ck):
    seq = x2d.shape[0]
    tp = T0 + seq
    head = jnp.concatenate([jnp.zeros((PADF, D_MODEL), F32), meta], axis=0)
    wg_pad = jnp.pad(w_gate_up, ((0, 128 - GATE_RANK), (0, 0))).astype(BF16)

    half = RET_QK // 2
    cos, sin = (jnp.asarray(t) for t in _rope_tables(tp))
    lgam = jnp.log1p(-(2.0 ** (-5.0 - jnp.arange(RET_HEADS, dtype=F32))))
    pmat = jnp.asarray(_gla_tril(), BF16)
    pmat_t = jnp.asarray(_gla_tril().T.copy(), BF16)

    h0, u, g_in = _rms_call(x2d, head, norm_gain, _gather_plan([w_in_part], relay=(True,)))
    hr, sw = w_in_part.shape
    w_in_bf = g_in.reshape(4, 2, hr, sw).transpose(1, 2, 0, 3).reshape(2 * hr, 4 * sw)
    w_r = w_in_bf
    w_g = jnp.pad(w_in_bf[:, W_R:W_R + W_G], ((0, 0), (0, W_GP - W_G)))
    w_m = w_in_bf[:, W_R + W_G:]
    tab = pl.BlockSpec((_proj_rows(tp), half), lambda j, i: (i, 0))
    rqk = _mm_nn("proj_rqk", u, w_r, BF16, D_MODEL, 0, 2 * D_MODEL, _rope_epilogue, (cos, sin), (tab, tab))
    rv = _mm_nn("proj_rv", u, w_r, BF16, RET_W, 2 * D_MODEL, RET_W)
    rg = _mm_nn("proj_rg", u, w_r, F32, RET_W, 4 * D_MODEL, RET_W)
    gqk = _mm_nn("proj_gqk", u, w_g, F32, 2 * GLA_KW, 0, 2 * GLA_KW, _gqk_epilogue)
    gv = _mm_nn("proj_gv", u, w_g, BF16, GLA_W, 2 * GLA_KW, GLA_W)
    gg = _mm_nn("proj_gg", u, w_g, F32, GLA_W, 2 * GLA_KW + GLA_W, GLA_W)
    mg = _mm_nn("proj_mg", u, w_m, F32, W_M, 0, W_M)

    o_ret, a_ret, st_ret, sc_ret = _ret_fwd_call(rqk, rv, rg, ret_gain, lgam)
    glr, z_gate, b_dec = _gla_gate_call(u, w_g, wg_pad, b_gate, pmat)
    o_gla, a_gla, st_gla, sc_gla, g_br, g_bg, g_out = _gla_fwd_call(gqk, gv, b_dec, gg, gla_gain,
                                                                    comm=_spread_plan(branch_parts))
    wbr = g_br.reshape(RET_W, D_MODEL)
    wbg = g_bg.reshape(GLA_W, D_MODEL)
    wout = g_out.reshape(D_MODEL, D_MODEL)

    gf = final_gain.reshape(1, D_MODEL)
    (dh1, da_gla, dm, merged_b, dh1_b, dpr_b, dpg_b, loss, dgf) = _mid_call(
        a_ret, a_gla, mg, h0, tgt2d, wbr, wbg, wout, gf)

    names_b = ("w_branch_ret", "w_branch_gla", "w_out")
    g2_b = [_mm_tn("dw_br", a_ret, dpr_b, D_MODEL).reshape(4, 2, RET_W // 8, D_MODEL).transpose(1, 0, 2, 3),
            _mm_tn("dw_bg", a_gla, dpg_b, D_MODEL).reshape(4, 2, GLA_W // 8, D_MODEL).transpose(1, 0, 2, 3),
            _mm_tn("dw_out", merged_b, dh1_b, D_MODEL).reshape(4, 2, D_MODEL // 8, D_MODEL).transpose(1, 0, 2, 3)]
    sib_b = _swap_halves_call("swap_halves_branch", g2_b)
    sum_b = [_add_half_call("add_half_" + nm, g, b, ck) for nm, g, b in zip(names_b, g2_b, sib_b)]
    d_g, db_dec, dgla_gain, *chips_b = _gla_bwd_call(gqk, gv, b_dec, gg, o_gla, da_gla, st_gla, sc_gla, gla_gain,
                                                     comm=_exchange_plan(sum_b))
    d_g, dwg, dbg, dw_glr = _gla_gate_bwd_call(db_dec, z_gate, glr, u, wg_pad, pmat_t, d_g)
    mine = [_add_chips_call("add_chips_" + nm, g, b, p, ck) for nm, g, b, p in zip(names_b, g2_b, sib_b, chips_b)]

    d_r, dret_gain = _ret_bwd_call(rqk, rv, rg, o_ret, dpr_b, wbr, st_ret, sc_ret, ret_gain, lgam, cos, sin)

    dwp = _mm_tn("dw_r", u, d_r, 2 * D_MODEL, out_cols=IN_PAD)
    dwp = _mm_tn("dw_g", u, d_g, D_MODEL, ncols=W_GP - 128, into=dwp, col0=W_R)
    g2_in = _place_merge_cols_call(dwp, _mm_tn("dw_m", u, dm, 2 * D_MODEL), dw_glr).reshape(2, D_MODEL // 2, IN_PAD)

    du, sib_in = _mm_nt_acc("du_g", d_g, w_g, W_GP, comm=_swap_plan([g2_in]), tb=_proj_rows(tp))
    sum_in = _add_rows_call("add_half_w_in", g2_in, sib_in, ck)
    du, chips_in = _mm_nt_acc("du_r", d_r, w_r, 2 * D_MODEL, acc_in=du, comm=_exchange_window_plan(sum_in),
                              tb=_proj_rows(tp))
    tile = pl.BlockSpec((TB, D_MODEL), lambda i, kk: (i, 0))
    row = pl.BlockSpec((1, D_MODEL), lambda i, kk: (0, 0))
    dx, dmeta, dnorm_gain = _mm_nt_acc(
        "du_m", dm, w_m, W_M, acc_in=du, epilogue=_rms_bwd_epilogue, extras=(h0, norm_gain, dh1),
        extra_specs=(tile, row, tile),
        extra_out_shapes=(jax.ShapeDtypeStruct((seq, D_MODEL), F32), jax.ShapeDtypeStruct((N_META, D_MODEL), F32),
                          jax.ShapeDtypeStruct((1, D_MODEL), F32)),
        extra_out_specs=(ANY, pl.BlockSpec((N_META, D_MODEL), lambda i, kk: (0, 0)), row),
        extra_scratch=(pltpu.VMEM((2, TB, D_MODEL), F32), pltpu.SemaphoreType.DMA((2,))))
    small = dict(norm_gain=dnorm_gain, b_gate=dbg, ret_norm_gain=dret_gain, gla_norm_gain=dgla_gain,
                 final_norm_gain=dgf, w_gate_up=dwg[:GATE_RANK], meta_tokens=dmeta, loss=loss[0, 0])
    rows = -(-sum(sz for _, sz in SMALL) // 128 // 8) * 8
    mine_in, g_small = _add_window_call("add_chips_w_in", g2_in, sib_in, chips_in, ck,
                                        _gather_plan([_pack_rows([small[nm] for nm, _ in SMALL], rows)]))
    full = _join_halves_call("join_halves", [mine_in] + mine)

    return dict(dx=dx, small=g_small, w_in=full[0], w_branch_ret=full[1], w_branch_gla=full[2], w_out=full[3])


MESH = pl.DeviceIdType.MESH
ANY = pl.BlockSpec(memory_space=pl.ANY)


def _place():
    return lax.axis_index("x"), lax.axis_index("y"), lax.axis_index("c")


def _gather8_call(name, parts):
    comm = _gather_plan(parts)
    n = len(parts)

    def body(*refs):
        begin, finish = comm.make(refs[:n], refs[n:2 * n], refs[-2], refs[-1])
        begin()
        finish()

    return pl.pallas_call(
        body, name=name, out_shape=list(comm.out_shapes), in_specs=[ANY] * n, out_specs=[ANY] * n,
        scratch_shapes=_comm_sems(comm),
    )(*parts)


def _swap_halves_call(name, gs):
    n = len(gs)

    def body(*refs):
        g_refs, b_refs = refs[:n], refs[n:2 * n]
        send_sems, recv_sems = refs[2 * n:]
        x, y, c = _place()
        copies = [pltpu.make_async_remote_copy(
            src_ref=g_refs[t].at[1 - c], dst_ref=b_refs[t], send_sem=send_sems.at[t], recv_sem=recv_sems.at[t],
            device_id=(x, y, 1 - c), device_id_type=MESH) for t in range(n)]
        for cp in copies:
            cp.start()
        for cp in copies:
            cp.wait()

    return pl.pallas_call(
        body, name=name,
        out_shape=[jax.ShapeDtypeStruct(g.shape[1:], g.dtype) for g in gs],
        in_specs=[ANY] * n, out_specs=[ANY] * n,
        scratch_shapes=[pltpu.SemaphoreType.DMA((n,)), pltpu.SemaphoreType.DMA((n,))],
    )(*gs)


def _join_halves_call(name, ts):
    n = len(ts)

    def body(*refs):
        o_refs = refs[n:2 * n]
        send_sems, recv_sems = refs[2 * n:]
        x, y, c = _place()
        copies = [pltpu.make_async_remote_copy(
            src_ref=o_refs[t].at[c], dst_ref=o_refs[t].at[c], send_sem=send_sems.at[t], recv_sem=recv_sems.at[t],
            device_id=(x, y, 1 - c), device_id_type=MESH) for t in range(n)]
        for cp in copies:
            cp.start()
        for t in range(n):
            copies[t].wait_send()
            pltpu.make_async_remote_copy(
                src_ref=o_refs[t].at[c], dst_ref=o_refs[t].at[1 - c], send_sem=send_sems.at[t],
                recv_sem=recv_sems.at[t], device_id=(x, y, 1 - c), device_id_type=MESH).wait_recv()

    return pl.pallas_call(
        body, name=name,
        out_shape=[jax.ShapeDtypeStruct(t.shape, t.dtype) for t in ts],
        in_specs=[ANY] * n, out_specs=[ANY] * n, input_output_aliases={t: t for t in range(n)},
        scratch_shapes=[pltpu.SemaphoreType.DMA((n,)), pltpu.SemaphoreType.DMA((n,))],
    )(*ts)


def _row_block(rows, cols, budget):
    best = 8
    for rb in range(8, rows + 1, 8):
        if rows % rb == 0 and rb * cols * 4 <= budget:
            best = rb
    return best


def _add_half_call(name, g, b, ck):
    _, _, r, cc = g.shape
    rb = _row_block(r, cc, 2 * 1024 * 1024)

    def body(ck_ref, g_ref, b_ref, o_ref):
        o_ref[...] = (g_ref[...] + b_ref[...]).astype(BF16)

    return pl.pallas_call(
        body, name=name,
        grid_spec=pltpu.PrefetchScalarGridSpec(
            num_scalar_prefetch=1, grid=(4, r // rb),
            in_specs=[pl.BlockSpec((None, None, rb, cc), lambda k, i, ck_ref: (ck_ref[0], k, i, 0)),
                      pl.BlockSpec((None, rb, cc), lambda k, i, ck_ref: (k, i, 0))],
            out_specs=pl.BlockSpec((None, rb, cc), lambda k, i, ck_ref: (k, i, 0))),
        out_shape=jax.ShapeDtypeStruct(b.shape, BF16),
        compiler_params=_cparams(2),
    )(ck, g, b)


def _add_rows_call(name, g, b, ck):
    _, r, cc = g.shape
    rb = _row_block(r, cc, 2 * 1024 * 1024)

    def body(ck_ref, g_ref, b_ref, o_ref):
        o_ref[...] = (g_ref[...] + b_ref[...]).astype(BF16)

    return pl.pallas_call(
        body, name=name,
        grid_spec=pltpu.PrefetchScalarGridSpec(
            num_scalar_prefetch=1, grid=(r // rb,),
            in_specs=[pl.BlockSpec((None, rb, cc), lambda i, ck_ref: (ck_ref[0], i, 0)),
                      pl.BlockSpec((rb, cc), lambda i, ck_ref: (i, 0))],
            out_specs=pl.BlockSpec((rb, cc), lambda i, ck_ref: (i, 0))),
        out_shape=jax.ShapeDtypeStruct((r, cc), BF16),
        compiler_params=_cparams(1),
    )(ck, g, b)


def _add_window_call(name, g, b, p, ck, comm):
    _, r, _ = g.shape
    nb, step = WIN_W // 128, WIN_STEP // 128
    n_xc = len(comm.srcs)

    def body(ck_ref, g_ref, b_ref, p0_ref, p1_ref, p2_ref, *rest):
        o_ref = rest[n_xc]
        i = pl.program_id(0)
        begin, finish = comm.make(rest[:n_xc], rest[n_xc + 1:2 * n_xc + 1], rest[-2], rest[-1])
        pl.when(i == 0)(begin)
        own = g_ref[...] + b_ref[...]
        o_ref[...] = ((own + p0_ref[...].astype(F32)) + p1_ref[...].astype(F32)) + p2_ref[...].astype(F32)
        pl.when(i == nb - 1)(finish)

    def peer(j):
        return pl.BlockSpec((None, r, 128), lambda i, ck_ref: (j, 0, i))

    return pl.pallas_call(
        body, name=name,
        grid_spec=pltpu.PrefetchScalarGridSpec(
            num_scalar_prefetch=1, grid=(nb,),
            in_specs=[pl.BlockSpec((None, r, 128), lambda i, ck_ref: (ck_ref[0], 0, step * ck_ref[1] + i)),
                      pl.BlockSpec((r, 128), lambda i, ck_ref: (0, step * ck_ref[1] + i)),
                      peer(0), peer(1), peer(2)] + [ANY] * n_xc,
            out_specs=[pl.BlockSpec((None, r, 128), lambda i, ck_ref: (ck_ref[0], 0, i))] + [ANY] * n_xc,
            scratch_shapes=_comm_sems(comm)),
        out_shape=[jax.ShapeDtypeStruct((2, r, WIN_W), F32)] + list(comm.out_shapes),
        compiler_params=_cparams(1),
    )(ck, g, b, p, p, p, *comm.srcs)


def _add_chips_call(name, g, b, p, ck):
    _, _, r, cc = g.shape
    rb = _row_block(r, cc, 2 * 1024 * 1024)

    def body(ck_ref, g_ref, b_ref, p0_ref, p1_ref, p2_ref, o_ref):
        own = g_ref[...] + b_ref[...]
        o_ref[...] = ((own + p0_ref[...].astype(F32)) + p1_ref[...].astype(F32)) + p2_ref[...].astype(F32)

    def peer(j):
        return pl.BlockSpec((None, rb, cc), lambda i, ck_ref: (j, i, 0))

    return pl.pallas_call(
        body, name=name,
        grid_spec=pltpu.PrefetchScalarGridSpec(
            num_scalar_prefetch=1, grid=(r // rb,),
            in_specs=[pl.BlockSpec((None, None, rb, cc), lambda i, ck_ref: (ck_ref[0], ck_ref[1], i, 0)),
                      pl.BlockSpec((None, rb, cc), lambda i, ck_ref: (ck_ref[1], i, 0)),
                      peer(0), peer(1), peer(2)],
            out_specs=pl.BlockSpec((None, rb, cc), lambda i, ck_ref: (ck_ref[0], i, 0))),
        out_shape=jax.ShapeDtypeStruct((2, r, cc), F32),
        compiler_params=_cparams(1),
    )(ck, g, b, p, p, p)


def _sum8_call(name, g):
    def body(g_ref, o_ref):
        acc = g_ref[0]
        for d in range(1, 8):
            acc = acc + g_ref[d]
        o_ref[...] = acc

    return pl.pallas_call(body, name=name, out_shape=jax.ShapeDtypeStruct(g.shape[1:], F32))(g)


def _adamw_call(name, w, g, m, v):
    r, cc = w.shape
    if r % 8 == 0 or r * cc * 4 <= 1024 * 1024:
        rb = _row_block(r, cc, 1024 * 1024) if r % 8 == 0 else r
        grid, spec = (r // rb,), pl.BlockSpec((rb, cc), lambda i: (i, 0))
    else:
        grid, spec = (cc // 128,), pl.BlockSpec((r, 128), lambda i: (0, i))

    def body(w_ref, g_ref, m_ref, v_ref, d_ref, m2_ref, v2_ref):
        gv = g_ref[...]
        m2 = ADAM_B1 * m_ref[...] + (1.0 - ADAM_B1) * gv
        v2 = ADAM_B2 * v_ref[...] + (1.0 - ADAM_B2) * (gv * gv)
        m_hat = m2 / (1.0 - ADAM_B1 ** ADAM_STEP)
        v_hat = v2 / (1.0 - ADAM_B2 ** ADAM_STEP)
        d_ref[...] = -ADAM_LR * (m_hat / (jnp.sqrt(v_hat) + ADAM_EPS) + ADAM_WD * w_ref[...])
        m2_ref[...] = m2
        v2_ref[...] = v2

    return pl.pallas_call(
        body, name=name, grid=grid, in_specs=[spec] * 4, out_specs=[spec] * 3,
        out_shape=[jax.ShapeDtypeStruct((r, cc), F32)] * 3, compiler_params=_cparams(1),
    )(w, g, m, v)


SMALL = (("norm_gain", D_MODEL), ("b_gate", GLA_KW), ("ret_norm_gain", RET_W), ("gla_norm_gain", GLA_W),
         ("final_norm_gain", D_MODEL), ("w_gate_up", GATE_RANK * GLA_KW), ("meta_tokens", N_META * D_MODEL),
         ("loss", 1))


def _pack_rows(vecs, rows):
    flat = jnp.concatenate([v.reshape(-1) for v in vecs])
    return jnp.pad(flat, (0, rows * 128 - flat.shape[0])).reshape(rows, 128)


def kernel(x, meta_tokens, norm_gain, w_in, w_gate_up, b_gate, ret_norm_gain, gla_norm_gain, w_branch_ret, w_branch_gla, w_out, final_norm_gain, loss_target, m_meta_tokens, m_norm_gain, m_w_in, m_w_gate_up, m_b_gate, m_ret_norm_gain, m_gla_norm_gain, m_w_branch_ret, m_w_branch_gla, m_w_out, m_final_norm_gain, v_meta_tokens, v_norm_gain, v_w_in, v_w_gate_up, v_b_gate, v_ret_norm_gain, v_gla_norm_gain, v_w_branch_ret, v_w_branch_gla, v_w_out, v_final_norm_gain):
    xi, yi, ci = _place()
    kme = 2 * xi + yi
    ck = jnp.stack([ci, kme]).astype(jnp.int32)
    sw_in = w_in.shape[2]

    def my_half(a, dtype):
        r, cc = a.shape
        return lax.dynamic_index_in_dim(a.reshape(2, r // 2, cc), ci, 0, keepdims=False).astype(dtype)

    g_meta, g_wg = _gather8_call("gather_small_weights", [my_half(meta_tokens, F32), my_half(w_gate_up[0], F32)])
    branch_parts = [my_half(w_branch_ret[0], BF16), my_half(w_branch_gla[0], BF16), my_half(w_out[0], BF16)]
    meta = g_meta.reshape(4, 2, N_META // 2, D_MODEL // 4).transpose(1, 2, 0, 3).reshape(N_META, D_MODEL)
    wg_full = g_wg.reshape(4, 2, GATE_RANK // 2, GLA_KW // 4).transpose(1, 2, 0, 3).reshape(GATE_RANK, GLA_KW)

    loc = _device_step(x[0], loss_target[0], meta, norm_gain, my_half(w_in[0], BF16), wg_full, b_gate, ret_norm_gain,
                       gla_norm_gain,
                       branch_parts, final_norm_gain, ck)
    names = ("w_in", "w_branch_ret", "w_branch_gla", "w_out")
    full = [loc[nm] for nm in names]
    big_w = dict(w_in=w_in[0], w_branch_ret=w_branch_ret[0], w_branch_gla=w_branch_gla[0], w_out=w_out[0])
    big_m = dict(w_in=m_w_in[0], w_branch_ret=m_w_branch_ret[0], w_branch_gla=m_w_branch_gla[0], w_out=m_w_out[0])
    big_v = dict(w_in=v_w_in[0], w_branch_ret=v_w_branch_ret[0], w_branch_gla=v_w_branch_gla[0], w_out=v_w_out[0])
    grads, deltas, new_m, new_v = {}, {}, {}, {}
    for nm, f in zip(names, full):
        shape = big_w[nm].shape
        if nm == "w_in":
            f = lax.dynamic_slice_in_dim(f, (sw_in - WIN_STEP) * kme, sw_in, axis=2)
        g = f.reshape(shape)
        if nm == "w_in":
            d, m2, v2 = (a.T for a in _adamw_call("adamw_" + nm, big_w[nm].T, g.T, big_m[nm].T, big_v[nm].T))
        else:
            d, m2, v2 = _adamw_call("adamw_" + nm, big_w[nm], g, big_m[nm], big_v[nm])
        grads[nm], deltas[nm], new_m[nm], new_v[nm] = (a.reshape((1,) + shape) for a in (g, d, m2, v2))

    tot = _sum8_call("sum_small_grads", loc["small"]).reshape(-1)
    off = 0
    sg = {}
    for nm, sz in SMALL:
        sg[nm] = tot[off:off + sz]
        off += sz
    loss = sg.pop("loss")[0]
    sg["w_gate_up"] = lax.dynamic_slice_in_dim(sg["w_gate_up"].reshape(GATE_RANK, GLA_KW), kme * (GLA_KW // 4),
                                               GLA_KW // 4, axis=1)
    sg["meta_tokens"] = lax.dynamic_slice_in_dim(sg["meta_tokens"].reshape(N_META, D_MODEL), kme * (D_MODEL // 4),
                                                 D_MODEL // 4, axis=1)
    small_w = dict(norm_gain=norm_gain, b_gate=b_gate, ret_norm_gain=ret_norm_gain, gla_norm_gain=gla_norm_gain,
                   final_norm_gain=final_norm_gain, w_gate_up=w_gate_up, meta_tokens=meta_tokens)
    small_m = dict(norm_gain=m_norm_gain, b_gate=m_b_gate, ret_norm_gain=m_ret_norm_gain,
                   gla_norm_gain=m_gla_norm_gain, final_norm_gain=m_final_norm_gain, w_gate_up=m_w_gate_up,
                   meta_tokens=m_meta_tokens)
    small_v = dict(norm_gain=v_norm_gain, b_gate=v_b_gate, ret_norm_gain=v_ret_norm_gain,
                   gla_norm_gain=v_gla_norm_gain, final_norm_gain=v_final_norm_gain, w_gate_up=v_w_gate_up,
                   meta_tokens=v_meta_tokens)
    for nm in small_w:
        shape = small_w[nm].shape
        as2d = lambda a: a.reshape((-1, shape[-1]))
        grads[nm] = sg[nm].reshape(shape)
        deltas[nm], new_m[nm], new_v[nm] = (a.reshape(shape) for a in _adamw_call(
            "adamw_" + nm, as2d(small_w[nm]), as2d(sg[nm]), as2d(small_m[nm]), as2d(small_v[nm])))

    out_order = ("meta_tokens", "norm_gain", "w_in", "w_gate_up", "b_gate", "ret_norm_gain", "gla_norm_gain",
                 "w_branch_ret", "w_branch_gla", "w_out", "final_norm_gain")
    dx = loc["dx"].reshape(x.shape)
    return (loss, dx, *[grads[nm] for nm in out_order], *[deltas[nm] for nm in out_order],
            *[new_m[nm] for nm in out_order], *[new_v[nm] for nm in out_order])
```

```python
import math
from typing import Callable, NamedTuple

import numpy as np
import jax
import jax.numpy as jnp
from jax import lax
from jax.experimental import pallas as pl
from jax.experimental.pallas import tpu as pltpu

F32 = jnp.float32
BF16 = jnp.bfloat16

D_MODEL = 1024
N_META = 16
EPS = 1e-6
ROPE_BASE = 10000.0
RET_HEADS, RET_QK, RET_V = 4, 256, 512
RET_W = RET_HEADS * RET_V
GLA_HEADS, GLA_K, GLA_V = 4, 128, 256
GLA_W = GLA_HEADS * GLA_V
GLA_KW = GLA_HEADS * GLA_K
GATE_RANK = 16
GATE_TAU = 16.0
GLA_SUB = 16

TM = 256
T0 = TM
PADF = T0 - N_META
GC = 128
GS = 3
TB = 768
TK = 768

W_R = 6144
W_G = 3088
W_GP = 3200
W_M = 2048
IN_COLS = W_R + W_G + W_M
WIN_STEP = (IN_COLS // 4) // 128 * 128
WIN_W = -(-(3 * (IN_COLS // 4 - WIN_STEP) + IN_COLS // 4) // 128) * 128
IN_PAD = 3 * WIN_STEP + WIN_W

ADAM_LR, ADAM_B1, ADAM_B2, ADAM_EPS, ADAM_WD, ADAM_STEP = 0.001, 0.9, 0.999, 1e-08, 0.01, 10

VMEM_LIMIT = 56 * 1024 * 1024

NN = ((1,), (0,))
NT = ((1,), (1,))
TN = ((0,), (0,))


def _dot(a, b, dims):
    return lax.dot_general(a, b, (dims, ((), ())), preferred_element_type=F32)


def _cparams(n_axes):
    return pltpu.CompilerParams(dimension_semantics=("arbitrary",) * n_axes, vmem_limit_bytes=VMEM_LIMIT)


def _sigmoid(x):
    return 0.5 * jnp.tanh(0.5 * x) + 0.5


def _silu(x):
    h = 0.5 * x
    return h + h * jnp.tanh(h)


def _head_mean(x):
    return jnp.mean(x, axis=-1, keepdims=True)


def _split3(x):
    hi = x.astype(BF16)
    r1 = x - hi.astype(F32)
    mid = r1.astype(BF16)
    lo = (r1 - mid.astype(F32)).astype(BF16)
    return hi, mid, lo


def _exact_pm(p, x):
    hi, mid, lo = _split3(x)
    return _dot(p, hi, NN) + _dot(p, mid, NN) + _dot(p, lo, NN)


def _rms_call(x2d, head, gain, comm):
    tp = T0 + x2d.shape[0]
    nt = tp // TM
    n_xc = len(comm.srcs)

    def body(x_ref, hd_ref, g_ref, *rest):
        xc_src = rest[:n_xc]
        h_ref, u_ref = rest[n_xc:n_xc + 2]
        xc_dst = rest[n_xc + 2:2 * n_xc + 2]
        i = pl.program_id(0)
        begin, finish = comm.make(xc_src, xc_dst, rest[-2], rest[-1])
        pl.when(i == 0)(begin)
        h = jnp.where(i == 0, hd_ref[...], x_ref[...])
        h_ref[...] = h
        r = lax.rsqrt(jnp.mean(h * h, axis=-1, keepdims=True) + EPS)
        u_ref[...] = (h * r * g_ref[...]).astype(BF16)
        pl.when(i == nt - 1)(finish)

    tile = pl.BlockSpec((TM, D_MODEL), lambda i: (i, 0))
    return pl.pallas_call(
        body, name="rms_in", grid=(nt,),
        in_specs=[pl.BlockSpec((TM, D_MODEL), lambda i: (jnp.maximum(i - 1, 0), 0)),
                  pl.BlockSpec((T0, D_MODEL), lambda i: (0, 0)), pl.BlockSpec((1, D_MODEL), lambda i: (0, 0))]
        + [ANY] * n_xc,
        out_specs=[tile, tile] + [ANY] * n_xc,
        out_shape=[jax.ShapeDtypeStruct((tp, D_MODEL), F32), jax.ShapeDtypeStruct((tp, D_MODEL), BF16)]
        + list(comm.out_shapes),
        scratch_shapes=_comm_sems(comm), compiler_params=_cparams(1),
    )(x2d, head, gain, *comm.srcs)


PROJ_ROWS_MAX = 1408


def _proj_rows(m):
    return max(r for r in range(16, PROJ_ROWS_MAX + 1, 16) if m % r == 0)


def _mm_nn(name, a, b, out_dtype, tn, col0, ncols, epilogue=None, extras=(), extra_specs=()):
    m, k = a.shape
    nj, j0 = ncols // tn, col0 // tn
    tb = _proj_rows(m)

    def body(a_ref, b_ref, *rest):
        *ex, o_ref = rest
        acc = _dot(a_ref[...], b_ref[...], NN)
        if epilogue is None:
            o_ref[...] = acc.astype(out_dtype)
        else:
            epilogue(acc, o_ref, *ex)

    return pl.pallas_call(
        body, name=name, grid=(nj, m // tb),
        in_specs=[pl.BlockSpec((tb, k), lambda j, i: (i, 0)), pl.BlockSpec((k, tn), lambda j, i: (0, j0 + j))]
        + list(extra_specs),
        out_specs=pl.BlockSpec((tb, tn), lambda j, i: (i, j)),
        out_shape=jax.ShapeDtypeStruct((m, ncols), out_dtype),
        compiler_params=_cparams(2),
    )(a, b, *extras)


def _rope_tables(tp):
    half = RET_QK // 2
    pos = np.arange(tp, dtype=np.float32) - np.float32(PADF)
    inv = (ROPE_BASE ** (-np.arange(half, dtype=np.float64) / half)).astype(np.float32)
    ang = (pos[:, None] * inv[None, :]).astype(np.float64)
    return np.cos(ang).astype(np.float32), np.sin(ang).astype(np.float32)


def _rope_epilogue(acc, o_ref, cos_ref, sin_ref):
    scale = jnp.where(pl.program_id(0) == 1, RET_QK ** -0.5, 1.0).astype(F32)
    cos, sin = cos_ref[...], sin_ref[...]
    half = RET_QK // 2
    for h in range(RET_HEADS):
        t1 = acc[:, h * RET_QK:h * RET_QK + half]
        t2 = acc[:, h * RET_QK + half:(h + 1) * RET_QK]
        o_ref[:, h * RET_QK:h * RET_QK + half] = ((t1 * cos - t2 * sin) * scale).astype(BF16)
        o_ref[:, h * RET_QK + half:(h + 1) * RET_QK] = ((t2 * cos + t1 * sin) * scale).astype(BF16)


def _gqk_epilogue(acc, o_ref):
    o_ref[:, :GLA_KW] = acc[:, :GLA_KW] * (GLA_K ** -0.5)
    o_ref[:, GLA_KW:] = acc[:, GLA_KW:]


class _Comm(NamedTuple):
    srcs: tuple
    out_shapes: tuple
    n_sems: int
    make: Callable


def _comm_sems(comm):
    return [pltpu.SemaphoreType.DMA((comm.n_sems,)), pltpu.SemaphoreType.DMA((comm.n_sems,))]


def _start_wait(copies):
    def begin():
        for cp in copies:
            cp.start()

    def finish():
        for cp in copies:
            cp.wait()

    return begin, finish


def _other_chips(x, y):
    return [(1 - x, y), (x, 1 - y), (1 - x, 1 - y)]


def _gather_plan(parts, relay=()):
    n = len(parts)
    relay = tuple(relay) + (False,) * (n - len(relay))

    def make(x_refs, out_refs, send_sems, recv_sems):
        x, y, c = _place()
        me, sibling = (x, y, c), (x, y, 1 - c)
        xn, yn, dg = (1 - x, y), (x, 1 - y), (1 - x, 1 - y)

        def slot(t, px, py, pc, half=None):
            ref = out_refs[t].at[4 * px + 2 * py + pc]
            if half is None:
                return ref
            rows = ref.shape[0] // 2
            return ref.at[pl.ds(half * rows, rows)]

        def copy(t, k, dst, to, src=None):
            return pltpu.make_async_remote_copy(
                src_ref=dst if src is None else src, dst_ref=dst, send_sem=send_sems.at[8 * t + k],
                recv_sem=recv_sems.at[8 * t + k], device_id=to, device_id_type=MESH)

        mine = [pltpu.make_async_copy(x_refs[t], slot(t, *me), send_sems.at[8 * n + t]) for t in range(n)]
        sent = []
        for t in range(n):
            sent.append(copy(t, 0, slot(t, *me), sibling, src=x_refs[t]))
            sent.append(copy(t, 1, slot(t, *me), (*xn, c), src=x_refs[t]))
            sent.append(copy(t, 2, slot(t, *me), (*yn, c), src=x_refs[t]))
            if not relay[t]:
                sent.append(copy(t, 3, slot(t, *me), (*dg, c), src=x_refs[t]))

        def begin():
            for cp in mine + sent:
                cp.start()

        def finish():
            later = []

            def start(cp):
                cp.start()
                later.append(cp)

            for t in range(n):
                copy(t, 2, slot(t, *yn, c), me).wait_recv()
                if relay[t]:
                    start(copy(t, 3, slot(t, *yn, c, half=0), (*xn, c)))
                start(copy(t, 6, slot(t, *yn, c), sibling))
            for t in range(n):
                copy(t, 1, slot(t, *xn, c), me).wait_recv()
                if relay[t]:
                    start(copy(t, 4, slot(t, *xn, c, half=1), (*yn, c)))
                start(copy(t, 5, slot(t, *xn, c), sibling))
            for t in range(n):
                if relay[t]:
                    copy(t, 3, slot(t, *dg, c, half=0), me).wait_recv()
                    copy(t, 4, slot(t, *dg, c, half=1), me).wait_recv()
                else:
                    copy(t, 3, slot(t, *dg, c), me).wait_recv()
                start(copy(t, 7, slot(t, *dg, c), sibling))
            for t in range(n):
                copy(t, 0, slot(t, *sibling), me).wait_recv()
                copy(t, 5, slot(t, *xn, 1 - c), me).wait_recv()
                copy(t, 6, slot(t, *yn, 1 - c), me).wait_recv()
                copy(t, 7, slot(t, *dg, 1 - c), me).wait_recv()
            for cp in sent + later:
                cp.wait_send()
            for cp in mine:
                cp.wait()

        return begin, finish

    return _Comm(tuple(parts), tuple(jax.ShapeDtypeStruct((8,) + p.shape, p.dtype) for p in parts), 9 * n, make)


def _exchange_plan(ss):
    def make(s_refs, b_refs, send_sems, recv_sems):
        x, y, c = _place()
        return _start_wait([pltpu.make_async_remote_copy(
            src_ref=s_refs[t].at[2 * chip[0] + chip[1]], dst_ref=b_refs[t].at[j], send_sem=send_sems.at[3 * t + j],
            recv_sem=recv_sems.at[3 * t + j], device_id=(*chip, c), device_id_type=MESH)
            for t in range(len(s_refs)) for j, chip in enumerate(_other_chips(x, y))])

    return _Comm(tuple(ss), tuple(jax.ShapeDtypeStruct((3,) + s.shape[1:], s.dtype) for s in ss), 3 * len(ss), make)


def _exchange_window_plan(s):
    def make(s_refs, b_refs, send_sems, recv_sems):
        x, y, c = _place()
        return _start_wait([pltpu.make_async_remote_copy(
            src_ref=s_refs[0].at[:, pl.ds(pl.multiple_of((2 * chip[0] + chip[1]) * WIN_STEP, 128), WIN_W)],
            dst_ref=b_refs[0].at[j], send_sem=send_sems.at[j], recv_sem=recv_sems.at[j], device_id=(*chip, c),
            device_id_type=MESH) for j, chip in enumerate(_other_chips(x, y))])

    return _Comm((s,), (jax.ShapeDtypeStruct((3, s.shape[0], WIN_W), s.dtype),), 3, make)


def _swap_plan(gs):
    def make(g_refs, b_refs, send_sems, recv_sems):
        x, y, c = _place()
        return _start_wait([pltpu.make_async_remote_copy(
            src_ref=g_refs[t].at[1 - c], dst_ref=b_refs[t], send_sem=send_sems.at[t], recv_sem=recv_sems.at[t],
            device_id=(x, y, 1 - c), device_id_type=MESH) for t in range(len(g_refs))])

    return _Comm(tuple(gs), tuple(jax.ShapeDtypeStruct(g.shape[1:], g.dtype) for g in gs), len(gs), make)


def _spread_plan(parts):
    def make(p_refs, o_refs, send_sems, recv_sems):
        x, y, c = _place()
        copies = []
        for t in range(len(p_refs)):
            mine = o_refs[t].at[4 * x + 2 * y + c]
            copies.append(pltpu.make_async_copy(p_refs[t], mine, send_sems.at[7 * len(p_refs) + t]))
            for r in range(1, 8):
                peer = (1 - x if r & 4 else x, 1 - y if r & 2 else y, 1 - c if r & 1 else c)
                copies.append(pltpu.make_async_remote_copy(
                    src_ref=p_refs[t], dst_ref=mine, send_sem=send_sems.at[7 * t + r - 1],
                    recv_sem=recv_sems.at[7 * t + r - 1], device_id=peer, device_id_type=MESH))
        return _start_wait(copies)

    return _Comm(tuple(parts), tuple(jax.ShapeDtypeStruct((8,) + p.shape, p.dtype) for p in parts), 8 * len(parts),
                 make)


def _mm_nt_acc(name, a, w, tk, acc_in=None, epilogue=None, extras=(), extra_specs=(), extra_out_shapes=(),
               extra_out_specs=(), extra_scratch=(), comm=None, tb=TB):
    m, k = a.shape
    n = w.shape[0]
    nk, ni = k // tk, m // tb
    has_acc = acc_in is not None
    n_xc = len(comm.srcs) if comm else 0
    n_es = len(extra_scratch)

    def body(*refs):
        a_ref, w_ref = refs[0], refs[1]
        pos = 2
        acc_ref = None
        if has_acc:
            acc_ref = refs[pos]
            pos += 1
        ex = refs[pos:pos + len(extras)]
        pos += len(extras)
        xc_src = refs[pos:pos + n_xc]
        pos += n_xc
        n_scr = 1 + n_es + (2 if n_xc else 0)
        outs = refs[pos:len(refs) - n_scr - n_xc]
        xc_dst = refs[len(refs) - n_scr - n_xc:len(refs) - n_scr]
        scr = refs[len(refs) - n_scr]
        es = refs[len(refs) - n_scr + 1:len(refs) - n_scr + 1 + n_es]
        i, kk = pl.program_id(0), pl.program_id(1)
        if n_xc:
            begin, finish = comm.make(xc_src, xc_dst, refs[-2], refs[-1])
            pl.when((i == 0) & (kk == 0))(begin)

        @pl.when(kk == 0)
        def _():
            scr[...] = acc_ref[...] if has_acc else jnp.zeros_like(scr)

        scr[...] += _dot(a_ref[...], w_ref[...], NT)

        @pl.when(kk == nk - 1)
        def _():
            if epilogue is None:
                outs[0][...] = scr[...]
            else:
                epilogue(scr[...], outs, i, ni, *ex, *es)

        if n_xc:
            pl.when((i == ni - 1) & (kk == nk - 1))(finish)

    in_specs = [pl.BlockSpec((tb, tk), lambda i, kk: (i, kk)), pl.BlockSpec((n, tk), lambda i, kk: (0, kk))]
    args = [a, w]
    if has_acc:
        in_specs.append(pl.BlockSpec((tb, n), lambda i, kk: (i, 0)))
        args.append(acc_in)
    in_specs += list(extra_specs) + [ANY] * n_xc
    args += list(extras) + (list(comm.srcs) if comm else [])
    if epilogue is None:
        out_shape = [jax.ShapeDtypeStruct((m, n), F32)]
        out_specs = [pl.BlockSpec((tb, n), lambda i, kk: (i, 0))]
    else:
        out_shape, out_specs = list(extra_out_shapes), list(extra_out_specs)
    scratch = [pltpu.VMEM((tb, n), F32)] + list(extra_scratch)
    if n_xc:
        out_shape += list(comm.out_shapes)
        out_specs += [ANY] * n_xc
        scratch += _comm_sems(comm)
    return pl.pallas_call(
        body, name=name, grid=(ni, nk), in_specs=in_specs, out_specs=out_specs, out_shape=out_shape,
        scratch_shapes=scratch, compiler_params=_cparams(2),
    )(*args)


def _rms_bwd_epilogue(du, outs, i, ni, h_ref, g_ref, dh1_ref, obuf, sems):
    dx_ref, dmeta_ref, dg_ref = outs
    h = h_ref[...]
    r = lax.rsqrt(jnp.mean(h * h, axis=-1, keepdims=True) + EPS)
    xh = h * r
    dxh = du * g_ref[...]
    dh0 = dh1_ref[...] + r * (dxh - xh * jnp.mean(dxh * xh, axis=-1, keepdims=True))

    def put(slot, tile):
        return pltpu.make_async_copy(obuf.at[slot], dx_ref.at[pl.ds(pl.multiple_of(tile * TB - T0, 8), TB)],
                                     sems.at[slot])

    @pl.when(i == 0)
    def _():
        dg_ref[...] = jnp.zeros_like(dg_ref)
        dmeta_ref[...] = dh0[PADF:T0, :]
        obuf[0] = dh0
        first = pltpu.make_async_copy(obuf.at[0, pl.ds(T0, TB - T0)], dx_ref.at[pl.ds(0, TB - T0)], sems.at[0])
        first.start()
        first.wait()

    @pl.when(i >= 1)
    def _():
        slot = i % 2

        @pl.when(i >= 3)
        def _():
            put(slot, i - 2).wait()

        obuf[slot] = dh0
        put(slot, i).start()

    dg_ref[...] += jnp.sum(du * xh, axis=0, keepdims=True)

    @pl.when(i == ni - 1)
    def _():
        for tile in (ni - 2, ni - 1):
            if tile >= 1:
                put(tile % 2, tile).wait()


def _mm_tn(name, a, b, bn, ncols=None, bcol0=0, into=None, col0=0, out_cols=None):
    t, m = a.shape
    n = ncols or b.shape[1]
    j0, bj0 = col0 // bn, bcol0 // bn

    def body(a_ref, b_ref, *rest):
        o_ref = rest[-1]

        @pl.when(pl.program_id(1) == 0)
        def _():
            o_ref[...] = jnp.zeros_like(o_ref)

        o_ref[...] += _dot(a_ref[...], b_ref[...], TN)

    in_specs = [pl.BlockSpec((TK, m), lambda j, kk: (kk, 0)), pl.BlockSpec((TK, bn), lambda j, kk: (kk, bj0 + j))]
    args = [a, b]
    aliases = {}
    if into is not None:
        in_specs.append(ANY)
        args.append(into)
        aliases = {2: 0}
        out_cols = into.shape[1]
    return pl.pallas_call(
        body, name=name, grid=(n // bn, t // TK), in_specs=in_specs,
        out_specs=pl.BlockSpec((m, bn), lambda j, kk: (0, j0 + j)),
        out_shape=jax.ShapeDtypeStruct((m, out_cols or n), F32), input_output_aliases=aliases,
        compiler_params=_cparams(2),
    )(*args)


def _place_merge_cols_call(dwp, dw_m, dw_glr):
    c0 = W_R + W_GP - 128
    tail = IN_PAD - c0
    rows = 256

    def body(m_ref, low, p_ref, o_ref, buf, sem):
        for r in range(0, D_MODEL, rows):
            buf[r:r + rows, :] = jnp.concatenate(
                [low[r:r + rows, :GATE_RANK], m_ref[r:r + rows, :],
                 jnp.zeros((rows, tail - GATE_RANK - W_M), F32)], axis=1)
        put = pltpu.make_async_copy(buf, o_ref.at[:, pl.ds(c0, tail)], sem)
        put.start()
        put.wait()

    return pl.pallas_call(
        body, name="place_merge_cols",
        in_specs=[pl.BlockSpec(memory_space=pltpu.VMEM), pl.BlockSpec(memory_space=pltpu.VMEM), ANY], out_specs=ANY,
        out_shape=jax.ShapeDtypeStruct(dwp.shape, F32), input_output_aliases={2: 0},
        scratch_shapes=[pltpu.VMEM((D_MODEL, tail), F32), pltpu.SemaphoreType.DMA],
        compiler_params=pltpu.CompilerParams(vmem_limit_bytes=VMEM_LIMIT),
    )(dw_m, dw_glr, dwp)


def _ret_fill_decay(lg_ref, dm_scr):
    c = TM
    ii = lax.broadcasted_iota(jnp.int32, (c, c), 0)
    jj = lax.broadcasted_iota(jnp.int32, (c, c), 1)
    rel = (ii - jj).astype(F32)
    for h in range(RET_HEADS):
        dm_scr[h] = jnp.where(rel >= 0, jnp.exp(jnp.maximum(rel, 0.0) * lg_ref[h]), 0.0)


def _ret_consts(lg, dm_ref):
    c = TM
    idx = lax.broadcasted_iota(jnp.int32, (c, 1), 0).astype(F32)
    xi = jnp.exp((idx + 1.0) * lg)
    zeta = jnp.exp((c - 1.0 - idx) * lg)
    gc = jnp.exp(jnp.full((1, 1), c, F32) * lg)
    return dm_ref[...], xi, zeta, gc


def _ret_fwd_call(rqk, rv, rg, gain, lgam):
    tp = rqk.shape[0]
    nc = tp // TM

    def body(lg_ref, qk_ref, v_ref, rg_ref, g_ref, o_ref, a_ref, st_ref, sc_ref, s_scr, dm_scr):
        @pl.when(pl.program_id(0) == 0)
        def _():
            s_scr[...] = jnp.zeros_like(s_scr)
            _ret_fill_decay(lg_ref, dm_scr)

        for h in range(RET_HEADS):
            dm, xi, zeta, gc = _ret_consts(lg_ref[h], dm_scr.at[h])
            q = qk_ref[:, h * RET_QK:(h + 1) * RET_QK]
            k = qk_ref[:, D_MODEL + h * RET_QK:D_MODEL + (h + 1) * RET_QK]
            v = v_ref[:, h * RET_V:(h + 1) * RET_V]
            sb = s_scr[h].astype(BF16)
            st_ref[0, h] = sb
            s = (_dot(q, k, NT) * dm).astype(BF16)
            sc_ref[0, h] = s
            o = _dot(s, v, NN) + xi * _dot(q, sb, NN)
            kz = (k.astype(F32) * zeta).astype(BF16)
            s_scr[h] = gc * s_scr[h] + _dot(kz, v, TN)
            o_ref[:, h * RET_V:(h + 1) * RET_V] = o
            mu = _head_mean(o)
            xc = o - mu
            xh = xc * lax.rsqrt(_head_mean(xc * xc) + EPS)
            a_ref[:, h * RET_V:(h + 1) * RET_V] = (
                xh * g_ref[:, h * RET_V:(h + 1) * RET_V] * _silu(rg_ref[:, h * RET_V:(h + 1) * RET_V])).astype(BF16)

    return pl.pallas_call(
        body, name="ret_fwd", grid=(nc,),
        in_specs=[pl.BlockSpec(memory_space=pltpu.SMEM),
                  pl.BlockSpec((TM, 2 * D_MODEL), lambda n: (n, 0)),
                  pl.BlockSpec((TM, RET_W), lambda n: (n, 0)),
                  pl.BlockSpec((TM, RET_W), lambda n: (n, 0)),
                  pl.BlockSpec((1, RET_W), lambda n: (0, 0))],
        out_specs=[pl.BlockSpec((TM, RET_W), lambda n: (n, 0)),
                   pl.BlockSpec((TM, RET_W), lambda n: (n, 0)),
                   pl.BlockSpec((1, RET_HEADS, RET_QK, RET_V), lambda n: (n, 0, 0, 0)),
                   pl.BlockSpec((1, RET_HEADS, TM, TM), lambda n: (n, 0, 0, 0))],
        out_shape=[jax.ShapeDtypeStruct((tp, RET_W), F32), jax.ShapeDtypeStruct((tp, RET_W), BF16),
                   jax.ShapeDtypeStruct((nc, RET_HEADS, RET_QK, RET_V), BF16),
                   jax.ShapeDtypeStruct((nc, RET_HEADS, TM, TM), BF16)],
        scratch_shapes=[pltpu.VMEM((RET_HEADS, RET_QK, RET_V), F32), pltpu.VMEM((RET_HEADS, TM, TM), F32)],
        compiler_params=_cparams(1),
    )(lgam, rqk, rv, rg, gain)


def _ret_bwd_call(rqk, rv, rg, o_ret, dpr, wbr, states, scores, gain, lgam, cos, sin):
    tp = rqk.shape[0]
    nc = tp // TM
    half = RET_QK // 2

    def body(lg_ref, qk_ref, v_ref, rg_ref, o_ref, dpr_ref, wbr_ref, st_ref, sc_ref, g_ref, cos_ref, sin_ref, dp_ref,
             dg_ref, ds_scr, dm_scr):
        @pl.when(pl.program_id(0) == 0)
        def _():
            ds_scr[...] = jnp.zeros_like(ds_scr)
            dg_ref[...] = jnp.zeros_like(dg_ref)
            _ret_fill_decay(lg_ref, dm_scr)

        cos, sin = cos_ref[...], sin_ref[...]
        for h in range(RET_HEADS):
            hs = slice(h * RET_V, (h + 1) * RET_V)
            dm, xi, zeta, gc = _ret_consts(lg_ref[h], dm_scr.at[h])
            o = o_ref[:, hs]
            mu = _head_mean(o)
            xc = o - mu
            rstd = lax.rsqrt(_head_mean(xc * xc) + EPS)
            xh = xc * rstd
            gain_h = g_ref[:, hs]
            g = rg_ref[:, hs]
            sg = _sigmoid(g)
            silu = g * sg
            dah = _dot(dpr_ref[...], wbr_ref[hs, :], NT)
            dp_ref[:, 4 * D_MODEL + h * RET_V:4 * D_MODEL + (h + 1) * RET_V] = (
                dah * (xh * gain_h) * (sg * (1.0 + g * (1.0 - sg)))).astype(BF16)
            dn = dah * silu
            dg_ref[:, hs] += jnp.sum(dn * xh, axis=0, keepdims=True)
            dxh = dn * gain_h
            do = rstd * (dxh - _head_mean(dxh) - xh * _head_mean(dxh * xh))
            dob = do.astype(BF16)
            q = qk_ref[:, h * RET_QK:(h + 1) * RET_QK]
            k = qk_ref[:, D_MODEL + h * RET_QK:D_MODEL + (h + 1) * RET_QK]
            v = v_ref[:, hs]
            sp = st_ref[0, h]
            ds = ds_scr[h]
            dsb = ds.astype(BF16)
            s = sc_ref[0, h]
            dsc = (_dot(dob, v, NT) * dm).astype(BF16)
            dq = _dot(dsc, k, NN) + xi * _dot(dob, sp, NT)
            dk = _dot(dsc, q, TN) + zeta * _dot(v, dsb, NT)
            kz = (k.astype(F32) * zeta).astype(BF16)
            dv = _dot(s, dob, TN) + _dot(kz, dsb, NN)
            qx = (q.astype(F32) * xi).astype(BF16)
            ds_scr[h] = gc * ds + _dot(qx, dob, TN)
            dp_ref[:, 2 * D_MODEL + h * RET_V:2 * D_MODEL + (h + 1) * RET_V] = dv.astype(BF16)
            dk = dk * (RET_QK ** -0.5)
            for base, t in ((0, dq), (D_MODEL, dk)):
                t1, t2 = t[:, :half], t[:, half:]
                dp_ref[:, base + h * RET_QK:base + h * RET_QK + half] = (t1 * cos + t2 * sin).astype(BF16)
                dp_ref[:, base + h * RET_QK + half:base + (h + 1) * RET_QK] = (t2 * cos - t1 * sin).astype(BF16)

    rev = lambda n: (nc - 1 - n, 0)
    return pl.pallas_call(
        body, name="ret_bwd", grid=(nc,),
        in_specs=[pl.BlockSpec(memory_space=pltpu.SMEM),
                  pl.BlockSpec((TM, 2 * D_MODEL), rev),
                  pl.BlockSpec((TM, RET_W), rev),
                  pl.BlockSpec((TM, RET_W), rev),
                  pl.BlockSpec((TM, RET_W), rev),
                  pl.BlockSpec((TM, D_MODEL), rev),
                  pl.BlockSpec((RET_W, D_MODEL), lambda n: (0, 0)),
                  pl.BlockSpec((1, RET_HEADS, RET_QK, RET_V), lambda n: (nc - 1 - n, 0, 0, 0)),
                  pl.BlockSpec((1, RET_HEADS, TM, TM), lambda n: (nc - 1 - n, 0, 0, 0)),
                  pl.BlockSpec((1, RET_W), lambda n: (0, 0)),
                  pl.BlockSpec((TM, half), rev),
                  pl.BlockSpec((TM, half), rev)],
        out_specs=[pl.BlockSpec((TM, W_R), rev), pl.BlockSpec((1, RET_W), lambda n: (0, 0))],
        out_shape=[jax.ShapeDtypeStruct((tp, W_R), BF16), jax.ShapeDtypeStruct((1, RET_W), F32)],
        scratch_shapes=[pltpu.VMEM((RET_HEADS, RET_QK, RET_V), F32), pltpu.VMEM((RET_HEADS, TM, TM), F32)],
        compiler_params=_cparams(1),
    )(lgam, rqk, rv, rg, o_ret, dpr, wbr, states, scores, gain, cos, sin)


GLA_LEVELS = tuple(GC >> (s + 1) for s in range(int(math.log2(GC // GLA_SUB))))
NLEV = len(GLA_LEVELS)


def _gla_tril():
    return np.tril(np.ones((GC, GC), np.float32))


def _gla_masks():
    ii = lax.broadcasted_iota(jnp.int32, (GC, GC), 0)
    jj = lax.broadcasted_iota(jnp.int32, (GC, GC), 1)
    masks = []
    for m in GLA_LEVELS:
        sh = int(math.log2(2 * m))
        masks.append(((ii >> sh) == (jj >> sh)) & ((ii & m) != 0) & ((jj & m) == 0))
    sh = int(math.log2(GLA_SUB))
    md = ((ii >> sh) == (jj >> sh)) & (jj <= ii)
    row = lax.broadcasted_iota(jnp.int32, (GC, 1), 0)
    second = [(row & m) != 0 for m in GLA_LEVELS]
    return masks, md, second


def _gla_gate_call(u, w_g, wg, bg, pmat):
    tp = u.shape[0]
    gb = _proj_rows(tp)
    assert gb % GC == 0

    def body(u_ref, w_ref, wg_ref, bg_ref, p_ref, glr_ref, z_ref, b_ref):
        glr = _dot(u_ref[...], w_ref[...], NN)
        glr_ref[...] = glr
        z = _dot(glr.astype(BF16), wg_ref[...], NN) + bg_ref[...]
        z_ref[...] = z
        la = (jnp.minimum(z, 0.0) - jnp.log1p(jnp.exp(-jnp.abs(z)))) * (1.0 / GATE_TAU)
        for r in range(0, gb, GC):
            b_ref[r:r + GC, :] = _exact_pm(p_ref[...], la[r:r + GC, :])

    tile = pl.BlockSpec((gb, GLA_KW), lambda i: (i, 0))
    return pl.pallas_call(
        body, name="gla_gate", grid=(tp // gb,),
        in_specs=[pl.BlockSpec((gb, D_MODEL), lambda i: (i, 0)),
                  pl.BlockSpec((D_MODEL, 128), lambda i: (0, (W_GP - 128) // 128)),
                  pl.BlockSpec((128, GLA_KW), lambda i: (0, 0)),
                  pl.BlockSpec((1, GLA_KW), lambda i: (0, 0)), pl.BlockSpec((GC, GC), lambda i: (0, 0))],
        out_specs=[pl.BlockSpec((gb, 128), lambda i: (i, 0)), tile, tile],
        out_shape=[jax.ShapeDtypeStruct((tp, 128), F32), jax.ShapeDtypeStruct((tp, GLA_KW), F32),
                   jax.ShapeDtypeStruct((tp, GLA_KW), F32)],
        compiler_params=_cparams(1),
    )(u, w_g, wg, bg, pmat)


def _gla_gate_bwd_call(db, z, glr, u, wg, pmat_t, d_g):
    tp = db.shape[0]
    gb = _proj_rows(tp)
    assert gb % GC == 0 and (W_GP - 128) % 128 == 0

    def body(db_ref, z_ref, glr_ref, u_ref, wg_ref, pt_ref, dgin_ref, dg_ref, dwg_ref, dbg_ref, dwl_ref):
        i = pl.program_id(0)

        @pl.when(i == 0)
        def _():
            dwg_ref[...] = jnp.zeros_like(dwg_ref)
            dbg_ref[...] = jnp.zeros_like(dbg_ref)
            dwl_ref[...] = jnp.zeros_like(dwl_ref)

        dla = jnp.concatenate([_exact_pm(pt_ref[...], db_ref[r:r + GC, :]) for r in range(0, gb, GC)], axis=0)
        row = i * gb + lax.broadcasted_iota(jnp.int32, (gb, 1), 0)
        dz = jnp.where(row >= PADF, dla * (1.0 / GATE_TAU) * _sigmoid(-z_ref[...]), 0.0)
        dzb = dz.astype(BF16)
        dglr = _dot(dzb, wg_ref[...], NT).astype(BF16)
        dg_ref[...] = dglr
        dwg_ref[...] += _dot(glr_ref[...].astype(BF16), dzb, TN)
        dbg_ref[...] += jnp.sum(dz, axis=0, keepdims=True)
        dwl_ref[...] += _dot(u_ref[...], dglr, TN)

    tile = pl.BlockSpec((gb, GLA_KW), lambda i: (i, 0))
    const = lambda i: (0, 0)
    return pl.pallas_call(
        body, name="gla_gate_bwd", grid=(tp // gb,),
        in_specs=[tile, tile, pl.BlockSpec((gb, 128), lambda i: (i, 0)), pl.BlockSpec((gb, D_MODEL), lambda i: (i, 0)),
                  pl.BlockSpec((128, GLA_KW), const), pl.BlockSpec((GC, GC), const), ANY],
        out_specs=[pl.BlockSpec((gb, 128), lambda i: (i, (W_GP - 128) // 128)), pl.BlockSpec((128, GLA_KW), const),
                   pl.BlockSpec((1, GLA_KW), const), pl.BlockSpec((D_MODEL, 128), const)],
        out_shape=[jax.ShapeDtypeStruct(d_g.shape, BF16), jax.ShapeDtypeStruct((128, GLA_KW), F32),
                   jax.ShapeDtypeStruct((1, GLA_KW), F32), jax.ShapeDtypeStruct((D_MODEL, 128), F32)],
        input_output_aliases={6: 0}, compiler_params=_cparams(1),
    )(db, z, glr, u, wg, pmat_t, d_g)


def _gla_row_steps(b_ref, cs, rows, size):
    parts = [jnp.zeros((size, GLA_K), F32) if r is None else jnp.broadcast_to(b_ref[r:r + 1, cs], (size, GLA_K))
             for r in rows]
    return parts[0] if len(parts) == 1 else jnp.concatenate(parts, axis=0)


def _gla_factors(b_ref, h, second):
    cs = slice(h * GLA_K, (h + 1) * GLA_K)
    b = b_ref[:, cs]
    fq, fk = [], []
    for l, m in enumerate(GLA_LEVELS):
        d = b - _gla_row_steps(b_ref, cs, [s + m - 1 for s in range(0, GC, 2 * m)], 2 * m)
        f = jnp.exp(jnp.where(second[l], d, -d))
        fq.append(jnp.where(second[l], f, 0.0))
        fk.append(jnp.where(second[l], 0.0, f))
    dd = b - _gla_row_steps(b_ref, cs, [None] + [s - 1 for s in range(GLA_SUB, GC, GLA_SUB)], GLA_SUB)
    ed = jnp.exp(dd)
    edi = jnp.exp(-dd)
    eb = jnp.exp(b)
    bl = b_ref[GC - 1:GC, cs]
    ee = jnp.exp(bl - b)
    ebl = jnp.exp(bl)
    return fq, fk, ed, edi, eb, ee, ebl


def _gla_scaled(q, k, fq, fk, ed, edi):
    qt = [(q * f).astype(BF16) for f in fq]
    kt = [(k * f).astype(BF16) for f in fk]
    return qt, kt, (q * ed).astype(BF16), (k * edi).astype(BF16)


def _gla_scores(qt, kt, qd, kd, masks, md):
    a = jnp.where(md, _dot(qd, kd, NT), 0.0)
    for l in range(NLEV):
        a = a + jnp.where(masks[l], _dot(qt[l], kt[l], NT), 0.0)
    return a.astype(BF16)


def _gla_fwd_call(gqk, gv, b, gg, gain, comm=None):
    tp = gqk.shape[0]
    nc = tp // GC
    ns = nc // GS
    n_xc = len(comm.srcs) if comm else 0

    def body(qk_ref, v_ref, b_ref, gg_ref, g_ref, *rest):
        xc_src = rest[:n_xc]
        o_ref, a_ref, st_ref, am_ref = rest[n_xc:n_xc + 4]
        xc_dst = rest[n_xc + 4:2 * n_xc + 4]
        s_scr = rest[2 * n_xc + 4]
        n = pl.program_id(0)
        if n_xc:
            begin, finish = comm.make(xc_src, xc_dst, rest[-2], rest[-1])
            pl.when(n == 0)(begin)
            pl.when(n == ns - 1)(finish)

        @pl.when(n == 0)
        def _():
            s_scr[...] = jnp.zeros_like(s_scr)

        masks, md, second = _gla_masks()
        for cc in range(GS):
            rows = pl.ds(cc * GC, GC)
            qk_c, v_c, b_c, gg_c, o_c, a_c = (r.at[rows] for r in (qk_ref, v_ref, b_ref, gg_ref, o_ref, a_ref))
            for h in range(GLA_HEADS):
                q = qk_c[:, h * GLA_K:(h + 1) * GLA_K]
                k = qk_c[:, GLA_KW + h * GLA_K:GLA_KW + (h + 1) * GLA_K]
                vs = slice(h * GLA_V, (h + 1) * GLA_V)
                v = v_c[:, vs]
                fq, fk, ed, edi, eb, ee, ebl = _gla_factors(b_c, h, second)
                a = _gla_scores(*_gla_scaled(q, k, fq, fk, ed, edi), masks, md)
                am_ref[cc, h] = a
                sb = s_scr[h].astype(BF16)
                st_ref[cc, h] = sb
                o = _dot(a, v, NN) + _dot((q * eb).astype(BF16), sb, NT)
                s_scr[h] = s_scr[h] * ebl + _dot(v, (k * ee).astype(BF16), TN)
                o_c[:, vs] = o
                xh = o * lax.rsqrt(_head_mean(o * o) + EPS)
                a_c[:, vs] = (xh * g_ref[:, vs] * _silu(gg_c[:, vs])).astype(BF16)

    return pl.pallas_call(
        body, name="gla_fwd", grid=(ns,),
        in_specs=[pl.BlockSpec((GS * GC, 2 * GLA_KW), lambda n: (n, 0)),
                  pl.BlockSpec((GS * GC, GLA_W), lambda n: (n, 0)),
                  pl.BlockSpec((GS * GC, GLA_KW), lambda n: (n, 0)),
                  pl.BlockSpec((GS * GC, GLA_W), lambda n: (n, 0)),
                  pl.BlockSpec((1, GLA_W), lambda n: (0, 0))] + [ANY] * n_xc,
        out_specs=[pl.BlockSpec((GS * GC, GLA_W), lambda n: (n, 0)),
                   pl.BlockSpec((GS * GC, GLA_W), lambda n: (n, 0)),
                   pl.BlockSpec((GS, GLA_HEADS, GLA_V, GLA_K), lambda n: (n, 0, 0, 0)),
                   pl.BlockSpec((GS, GLA_HEADS, GC, GC), lambda n: (n, 0, 0, 0))] + [ANY] * n_xc,
        out_shape=[jax.ShapeDtypeStruct((tp, GLA_W), F32), jax.ShapeDtypeStruct((tp, GLA_W), BF16),
                   jax.ShapeDtypeStruct((nc, GLA_HEADS, GLA_V, GLA_K), BF16),
                   jax.ShapeDtypeStruct((nc, GLA_HEADS, GC, GC), BF16)] + (list(comm.out_shapes) if comm else []),
        scratch_shapes=[pltpu.VMEM((GLA_HEADS, GLA_V, GLA_K), F32)] + (_comm_sems(comm) if comm else []),
        compiler_params=_cparams(1),
    )(gqk, gv, b, gg, gain, *(comm.srcs if comm else ()))


def _gla_bwd_call(gqk, gv, b, gg, o_gla, da, states, scores, gain, comm=None):
    tp = gqk.shape[0]
    nc = tp // GC
    ns = nc // GS
    o_gv, o_gg = 2 * GLA_KW, 2 * GLA_KW + GLA_W
    n_xc = len(comm.srcs) if comm else 0

    def body(qk_all, v_all, b_all, gg_all, o_all, da_all, st_ref, am_ref, g_ref, *rest):
        xc_src = rest[:n_xc]
        dp_all, db_all, dg_ref = rest[n_xc:n_xc + 3]
        xc_dst = rest[n_xc + 3:2 * n_xc + 3]
        ds_scr = rest[2 * n_xc + 3]
        n = pl.program_id(0)
        if n_xc:
            begin, finish = comm.make(xc_src, xc_dst, rest[-2], rest[-1])
            pl.when(n == 0)(begin)
            pl.when(n == ns - 1)(finish)

        @pl.when(n == 0)
        def _():
            ds_scr[...] = jnp.zeros_like(ds_scr)
            dg_ref[...] = jnp.zeros_like(dg_ref)

        masks, md, second = _gla_masks()
        for cc, h in [(cc, h) for cc in reversed(range(GS)) for h in range(GLA_HEADS)]:
            rows = pl.ds(cc * GC, GC)
            qk_ref, v_ref, b_scr, gg_ref, o_ref, da_ref, dp_ref, db_scr = (
                r.at[rows] for r in (qk_all, v_all, b_all, gg_all, o_all, da_all, dp_all, db_all))
            cs = slice(h * GLA_K, (h + 1) * GLA_K)
            vs = slice(h * GLA_V, (h + 1) * GLA_V)
            o = o_ref[:, vs]
            rstd = lax.rsqrt(_head_mean(o * o) + EPS)
            xh = o * rstd
            gain_h = g_ref[:, vs]
            g = gg_ref[:, vs]
            sg = _sigmoid(g)
            dah = da_ref[:, vs]
            dp_ref[:, o_gg + h * GLA_V:o_gg + (h + 1) * GLA_V] = (
                dah * (xh * gain_h) * (sg * (1.0 + g * (1.0 - sg)))).astype(BF16)
            dn = dah * (g * sg)
            dg_ref[:, vs] += jnp.sum(dn * xh, axis=0, keepdims=True)
            dxh = dn * gain_h
            do = rstd * (dxh - xh * _head_mean(dxh * xh))
            dob = do.astype(BF16)
            q = qk_ref[:, cs]
            k = qk_ref[:, GLA_KW + h * GLA_K:GLA_KW + (h + 1) * GLA_K]
            v = v_ref[:, vs]
            fq, fk, ed, edi, eb, ee, ebl = _gla_factors(b_scr, h, second)
            qt, kt, qd, kd = _gla_scaled(q, k, fq, fk, ed, edi)
            sp = st_ref[cc, h]
            ds = ds_scr[h]
            dsb = ds.astype(BF16)
            q_in = q * eb
            k_end = k * ee
            da_s = _dot(dob, v, NT)
            dv = _dot(am_ref[cc, h], dob, TN) + _dot(k_end.astype(BF16), dsb, NT)
            dq_in = _dot(dob, sp, NN)
            dk_end = _dot(v, dsb, NN)
            dbl = jnp.sum(sp.astype(F32) * ds, axis=0, keepdims=True) * ebl
            ds_scr[h] = ds * ebl + _dot(dob, q_in.astype(BF16), TN)
            dq = dq_in * eb
            dk = dk_end * ee
            de_end = dk_end * k_end
            db = dq_in * q_in - de_end
            placed = [(GC - 1, jnp.sum(de_end, axis=0, keepdims=True) + dbl)]
            for l, m in enumerate(GLA_LEVELS):
                dal = jnp.where(masks[l], da_s, 0.0).astype(BF16)
                dqt = _dot(dal, kt[l], NN)
                dkt = _dot(dal, qt[l], TN)
                dq = dq + dqt * fq[l]
                dk = dk + dkt * fk[l]
                gl = dqt * (q * fq[l]) - dkt * (k * fk[l])
                db = db + gl
                placed += [(s + m - 1, -jnp.sum(gl[s:s + 2 * m], axis=0, keepdims=True)) for s in range(0, GC, 2 * m)]
            dad = jnp.where(md, da_s, 0.0).astype(BF16)
            dqd = _dot(dad, kd, NN)
            dkd = _dot(dad, qd, TN)
            dq = dq + dqd * ed
            dk = dk + dkd * edi
            gd = dqd * (q * ed) - dkd * (k * edi)
            db = db + gd
            placed += [(s - 1, -jnp.sum(gd[s:s + GLA_SUB], axis=0, keepdims=True)) for s in range(GLA_SUB, GC, GLA_SUB)]
            db_scr[:, cs] = db
            for r, val in placed:
                db_scr[r:r + 1, cs] += val
            dp_ref[:, cs] = (dq * (GLA_K ** -0.5)).astype(BF16)
            dp_ref[:, GLA_KW + h * GLA_K:GLA_KW + (h + 1) * GLA_K] = dk.astype(BF16)
            dp_ref[:, o_gv + h * GLA_V:o_gv + (h + 1) * GLA_V] = dv.astype(BF16)

    rev = lambda n: (ns - 1 - n, 0)
    const = lambda n: (0, 0)
    xc_shapes, xc_sems = (list(comm.out_shapes), _comm_sems(comm)) if n_xc else ([], [])
    return pl.pallas_call(
        body, name="gla_bwd", grid=(ns,),
        in_specs=[pl.BlockSpec((GS * GC, 2 * GLA_KW), rev),
                  pl.BlockSpec((GS * GC, GLA_W), rev),
                  pl.BlockSpec((GS * GC, GLA_KW), rev),
                  pl.BlockSpec((GS * GC, GLA_W), rev),
                  pl.BlockSpec((GS * GC, GLA_W), rev),
                  pl.BlockSpec((GS * GC, GLA_W), rev),
                  pl.BlockSpec((GS, GLA_HEADS, GLA_V, GLA_K), lambda n: (ns - 1 - n, 0, 0, 0)),
                  pl.BlockSpec((GS, GLA_HEADS, GC, GC), lambda n: (ns - 1 - n, 0, 0, 0)),
                  pl.BlockSpec((1, GLA_W), const)] + [ANY] * n_xc,
        out_specs=[pl.BlockSpec((GS * GC, W_GP), rev), pl.BlockSpec((GS * GC, GLA_KW), rev),
                   pl.BlockSpec((1, GLA_W), const)] + [ANY] * n_xc,
        out_shape=[jax.ShapeDtypeStruct((tp, W_GP), BF16), jax.ShapeDtypeStruct((tp, GLA_KW), F32),
                   jax.ShapeDtypeStruct((1, GLA_W), F32)] + xc_shapes,
        scratch_shapes=[pltpu.VMEM((GLA_HEADS, GLA_V, GLA_K), F32)] + xc_sems,
        compiler_params=_cparams(1),
    )(gqk, gv, b, gg, o_gla, da, states, scores, gain, *(comm.srcs if comm else ()))


def _mid_call(a_ret, a_gla, mg, h0, tgt, wbr, wbg, wout, gf):
    tp = h0.shape[0]
    nt = tp // TM

    def body(ar_ref, ag_ref, mg_ref, h_ref, t_ref, wbr_ref, wbg_ref, wo_ref, gf_ref,
             dh1_ref, dag_ref, dm_ref, mb_ref, dh1b_ref, dprb_ref, dpgb_ref, loss_ref, dgf_ref):
        i = pl.program_id(0)

        @pl.when(i == 0)
        def _():
            loss_ref[...] = jnp.zeros_like(loss_ref)
            dgf_ref[...] = jnp.zeros_like(dgf_ref)

        ar, ag = ar_ref[...], ag_ref[...]
        pr = _dot(ar, wbr_ref[...], NN)
        pg = _dot(ag, wbg_ref[...], NN)
        sr = _sigmoid(mg_ref[:, :D_MODEL])
        sg = _sigmoid(mg_ref[:, D_MODEL:])
        merged = (sr * pr + sg * pg).astype(BF16)
        mb_ref[...] = merged
        h1 = h_ref[...] + _dot(merged, wo_ref[...], NN)
        r1 = lax.rsqrt(jnp.mean(h1 * h1, axis=-1, keepdims=True) + EPS)
        xh = h1 * r1
        gfv = gf_ref[...]
        live = jnp.where(i > 0, 1.0, 0.0).astype(F32)
        err = (xh * gfv - t_ref[...]) * live
        loss_ref[...] += jnp.full(loss_ref.shape, 0.5 / D_MODEL, F32) * jnp.sum(err * err)
        dy = err * (1.0 / D_MODEL)
        dgf_ref[...] += jnp.sum(dy * xh, axis=0, keepdims=True)
        dxh = dy * gfv
        dh1 = r1 * (dxh - xh * jnp.mean(dxh * xh, axis=-1, keepdims=True))
        dh1_ref[...] = dh1
        dh1b = dh1.astype(BF16)
        dh1b_ref[...] = dh1b
        dmerged = _dot(dh1b, wo_ref[...], NT)
        dm_ref[:, :D_MODEL] = (dmerged * pr * sr * (1.0 - sr)).astype(BF16)
        dm_ref[:, D_MODEL:] = (dmerged * pg * sg * (1.0 - sg)).astype(BF16)
        dpr = (dmerged * sr).astype(BF16)
        dpg = (dmerged * sg).astype(BF16)
        dprb_ref[...] = dpr
        dpgb_ref[...] = dpg
        dag_ref[...] = _dot(dpg, wbg_ref[...], NT)

    tile = lambda w: pl.BlockSpec((TM, w), lambda i: (i, 0))
    const = lambda r, w: pl.BlockSpec((r, w), lambda i: (0, 0))
    return pl.pallas_call(
        body, name="merge_out_loss", grid=(nt,),
        in_specs=[tile(RET_W), tile(GLA_W), tile(W_M), tile(D_MODEL),
                  pl.BlockSpec((TM, D_MODEL), lambda i: (jnp.maximum(i - 1, 0), 0)),
                  const(RET_W, D_MODEL), const(GLA_W, D_MODEL), const(D_MODEL, D_MODEL), const(1, D_MODEL)],
        out_specs=[tile(D_MODEL), tile(GLA_W), tile(W_M), tile(D_MODEL), tile(D_MODEL), tile(D_MODEL),
                   tile(D_MODEL), const(1, 128), const(1, D_MODEL)],
        out_shape=[jax.ShapeDtypeStruct((tp, D_MODEL), F32), jax.ShapeDtypeStruct((tp, GLA_W), F32),
                   jax.ShapeDtypeStruct((tp, W_M), BF16),
                   jax.ShapeDtypeStruct((tp, D_MODEL), BF16), jax.ShapeDtypeStruct((tp, D_MODEL), BF16),
                   jax.ShapeDtypeStruct((tp, D_MODEL), BF16), jax.ShapeDtypeStruct((tp, D_MODEL), BF16),
                   jax.ShapeDtypeStruct((1, 128), F32), jax.ShapeDtypeStruct((1, D_MODEL), F32)],
        compiler_params=_cparams(1),
    )(a_ret, a_gla, mg, h0, tgt, wbr, wbg, wout, gf)


def _device_step(x2d, tgt2d, meta, norm_gain, w_in_part, w_gate_up, b_gate, ret_gain, gla_gain, branch_parts,
                 final_gain, ck):
    seq = x2d.shape[0]
    tp = T0 + seq
    head = jnp.concatenate([jnp.zeros((PADF, D_MODEL), F32), meta], axis=0)
    wg_pad = jnp.pad(w_gate_up, ((0, 128 - GATE_RANK), (0, 0))).astype(BF16)

    half = RET_QK // 2
    cos, sin = (jnp.asarray(t) for t in _rope_tables(tp))
    lgam = jnp.log1p(-(2.0 ** (-5.0 - jnp.arange(RET_HEADS, dtype=F32))))
    pmat = jnp.asarray(_gla_tril(), BF16)
    pmat_t = jnp.asarray(_gla_tril().T.copy(), BF16)

    h0, u, g_in = _rms_call(x2d, head, norm_gain, _gather_plan([w_in_part], relay=(True,)))
    hr, sw = w_in_part.shape
    w_in_bf = g_in.reshape(4, 2, hr, sw).transpose(1, 2, 0, 3).reshape(2 * hr, 4 * sw)
    w_r = w_in_bf
    w_g = jnp.pad(w_in_bf[:, W_R:W_R + W_G], ((0, 0), (0, W_GP - W_G)))
    w_m = w_in_bf[:, W_R + W_G:]
    tab = pl.BlockSpec((_proj_rows(tp), half), lambda j, i: (i, 0))
    rqk = _mm_nn("proj_rqk", u, w_r, BF16, D_MODEL, 0, 2 * D_MODEL, _rope_epilogue, (cos, sin), (tab, tab))
    rv = _mm_nn("proj_rv", u, w_r, BF16, RET_W, 2 * D_MODEL, RET_W)
    rg = _mm_nn("proj_rg", u, w_r, F32, RET_W, 4 * D_MODEL, RET_W)
    gqk = _mm_nn("proj_gqk", u, w_g, F32, 2 * GLA_KW, 0, 2 * GLA_KW, _gqk_epilogue)
    gv = _mm_nn("proj_gv", u, w_g, BF16, GLA_W, 2 * GLA_KW, GLA_W)
    gg = _mm_nn("proj_gg", u, w_g, F32, GLA_W, 2 * GLA_KW + GLA_W, GLA_W)
    mg = _mm_nn("proj_mg", u, w_m, F32, W_M, 0, W_M)

    o_ret, a_ret, st_ret, sc_ret = _ret_fwd_call(rqk, rv, rg, ret_gain, lgam)
    glr, z_gate, b_dec = _gla_gate_call(u, w_g, wg_pad, b_gate, pmat)
    o_gla, a_gla, st_gla, sc_gla, g_br, g_bg, g_out = _gla_fwd_call(gqk, gv, b_dec, gg, gla_gain,
                                                                    comm=_spread_plan(branch_parts))
    wbr = g_br.reshape(RET_W, D_MODEL)
    wbg = g_bg.reshape(GLA_W, D_MODEL)
    wout = g_out.reshape(D_MODEL, D_MODEL)

    gf = final_gain.reshape(1, D_MODEL)
    (dh1, da_gla, dm, merged_b, dh1_b, dpr_b, dpg_b, loss, dgf) = _mid_call(
        a_ret, a_gla, mg, h0, tgt2d, wbr, wbg, wout, gf)

    names_b = ("w_branch_ret", "w_branch_gla", "w_out")
    g2_b = [_mm_tn("dw_br", a_ret, dpr_b, D_MODEL).reshape(4, 2, RET_W // 8, D_MODEL).transpose(1, 0, 2, 3),
            _mm_tn("dw_bg", a_gla, dpg_b, D_MODEL).reshape(4, 2, GLA_W // 8, D_MODEL).transpose(1, 0, 2, 3),
            _mm_tn("dw_out", merged_b, dh1_b, D_MODEL).reshape(4, 2, D_MODEL // 8, D_MODEL).transpose(1, 0, 2, 3)]
    sib_b = _swap_halves_call("swap_halves_branch", g2_b)
    sum_b = [_add_half_call("add_half_" + nm, g, b, ck) for nm, g, b in zip(names_b, g2_b, sib_b)]
    d_g, db_dec, dgla_gain, *chips_b = _gla_bwd_call(gqk, gv, b_dec, gg, o_gla, da_gla, st_gla, sc_gla, gla_gain,
                                                     comm=_exchange_plan(sum_b))
    d_g, dwg, dbg, dw_glr = _gla_gate_bwd_call(db_dec, z_gate, glr, u, wg_pad, pmat_t, d_g)
    mine = [_add_chips_call("add_chips_" + nm, g, b, p, ck) for nm, g, b, p in zip(names_b, g2_b, sib_b, chips_b)]

    d_r, dret_gain = _ret_bwd_call(rqk, rv, rg, o_ret, dpr_b, wbr, st_ret, sc_ret, ret_gain, lgam, cos, sin)

    dwp = _mm_tn("dw_r", u, d_r, 3 * D_MODEL, out_cols=IN_PAD)
    dwp = _mm_tn("dw_g", u, d_g, 3 * D_MODEL, ncols=W_GP - 128, into=dwp, col0=W_R)
    g2_in = _place_merge_cols_call(dwp, _mm_tn("dw_m", u, dm, 2 * D_MODEL), dw_glr).reshape(2, D_MODEL // 2, IN_PAD)

    du, sib_in = _mm_nt_acc("du_g", d_g, w_g, W_GP, comm=_swap_plan([g2_in]), tb=_proj_rows(tp))
    sum_in = _add_rows_call("add_half_w_in", g2_in, sib_in, ck)
    du, chips_in = _mm_nt_acc("du_r", d_r, w_r, 2 * D_MODEL, acc_in=du, comm=_exchange_window_plan(sum_in),
                              tb=_proj_rows(tp))
    tile = pl.BlockSpec((TB, D_MODEL), lambda i, kk: (i, 0))
    row = pl.BlockSpec((1, D_MODEL), lambda i, kk: (0, 0))
    dx, dmeta, dnorm_gain = _mm_nt_acc(
        "du_m", dm, w_m, W_M, acc_in=du, epilogue=_rms_bwd_epilogue, extras=(h0, norm_gain, dh1),
        extra_specs=(tile, row, tile),
        extra_out_shapes=(jax.ShapeDtypeStruct((seq, D_MODEL), F32), jax.ShapeDtypeStruct((N_META, D_MODEL), F32),
                          jax.ShapeDtypeStruct((1, D_MODEL), F32)),
        extra_out_specs=(ANY, pl.BlockSpec((N_META, D_MODEL), lambda i, kk: (0, 0)), row),
        extra_scratch=(pltpu.VMEM((2, TB, D_MODEL), F32), pltpu.SemaphoreType.DMA((2,))))
    small = dict(norm_gain=dnorm_gain, b_gate=dbg, ret_norm_gain=dret_gain, gla_norm_gain=dgla_gain,
                 final_norm_gain=dgf, w_gate_up=dwg[:GATE_RANK], meta_tokens=dmeta, loss=loss[0, 0])
    rows = -(-sum(sz for _, sz in SMALL) // 128 // 8) * 8
    mine_in, g_small = _add_window_call("add_chips_w_in", g2_in, sib_in, chips_in, ck,
                                        _gather_plan([_pack_rows([small[nm] for nm, _ in SMALL], rows)]))
    full = _join_halves_call("join_halves", [mine_in] + mine)

    return dict(dx=dx, small=g_small, w_in=full[0], w_branch_ret=full[1], w_branch_gla=full[2], w_out=full[3])


MESH = pl.DeviceIdType.MESH
ANY = pl.BlockSpec(memory_space=pl.ANY)


def _place():
    return lax.axis_index("x"), lax.axis_index("y"), lax.axis_index("c")


def _gather8_call(name, parts):
    comm = _gather_plan(parts)
    n = len(parts)

    def body(*refs):
        begin, finish = comm.make(refs[:n], refs[n:2 * n], refs[-2], refs[-1])
        begin()
        finish()

    return pl.pallas_call(
        body, name=name, out_shape=list(comm.out_shapes), in_specs=[ANY] * n, out_specs=[ANY] * n,
        scratch_shapes=_comm_sems(comm),
    )(*parts)


def _swap_halves_call(name, gs):
    n = len(gs)

    def body(*refs):
        g_refs, b_refs = refs[:n], refs[n:2 * n]
        send_sems, recv_sems = refs[2 * n:]
        x, y, c = _place()
        copies = [pltpu.make_async_remote_copy(
            src_ref=g_refs[t].at[1 - c], dst_ref=b_refs[t], send_sem=send_sems.at[t], recv_sem=recv_sems.at[t],
            device_id=(x, y, 1 - c), device_id_type=MESH) for t in range(n)]
        for cp in copies:
            cp.start()
        for cp in copies:
            cp.wait()

    return pl.pallas_call(
        body, name=name,
        out_shape=[jax.ShapeDtypeStruct(g.shape[1:], g.dtype) for g in gs],
        in_specs=[ANY] * n, out_specs=[ANY] * n,
        scratch_shapes=[pltpu.SemaphoreType.DMA((n,)), pltpu.SemaphoreType.DMA((n,))],
    )(*gs)


def _join_halves_call(name, ts):
    n = len(ts)

    def body(*refs):
        o_refs = refs[n:2 * n]
        send_sems, recv_sems = refs[2 * n:]
        x, y, c = _place()
        copies = [pltpu.make_async_remote_copy(
            src_ref=o_refs[t].at[c], dst_ref=o_refs[t].at[c], send_sem=send_sems.at[t], recv_sem=recv_sems.at[t],
            device_id=(x, y, 1 - c), device_id_type=MESH) for t in range(n)]
        for cp in copies:
            cp.start()
        for t in range(n):
            copies[t].wait_send()
            pltpu.make_async_remote_copy(
                src_ref=o_refs[t].at[c], dst_ref=o_refs[t].at[1 - c], send_sem=send_sems.at[t],
                recv_sem=recv_sems.at[t], device_id=(x, y, 1 - c), device_id_type=MESH).wait_recv()

    return pl.pallas_call(
        body, name=name,
        out_shape=[jax.ShapeDtypeStruct(t.shape, t.dtype) for t in ts],
        in_specs=[ANY] * n, out_specs=[ANY] * n, input_output_aliases={t: t for t in range(n)},
        scratch_shapes=[pltpu.SemaphoreType.DMA((n,)), pltpu.SemaphoreType.DMA((n,))],
    )(*ts)


def _row_block(rows, cols, budget):
    best = 8
    for rb in range(8, rows + 1, 8):
        if rows % rb == 0 and rb * cols * 4 <= budget:
            best = rb
    return best


def _add_half_call(name, g, b, ck):
    _, _, r, cc = g.shape
    rb = _row_block(r, cc, 2 * 1024 * 1024)

    def body(ck_ref, g_ref, b_ref, o_ref):
        o_ref[...] = (g_ref[...] + b_ref[...]).astype(BF16)

    return pl.pallas_call(
        body, name=name,
        grid_spec=pltpu.PrefetchScalarGridSpec(
            num_scalar_prefetch=1, grid=(4, r // rb),
            in_specs=[pl.BlockSpec((None, None, rb, cc), lambda k, i, ck_ref: (ck_ref[0], k, i, 0)),
                      pl.BlockSpec((None, rb, cc), lambda k, i, ck_ref: (k, i, 0))],
            out_specs=pl.BlockSpec((None, rb, cc), lambda k, i, ck_ref: (k, i, 0))),
        out_shape=jax.ShapeDtypeStruct(b.shape, BF16),
        compiler_params=_cparams(2),
    )(ck, g, b)


def _add_rows_call(name, g, b, ck):
    _, r, cc = g.shape
    rb = _row_block(r, cc, 2 * 1024 * 1024)

    def body(ck_ref, g_ref, b_ref, o_ref):
        o_ref[...] = (g_ref[...] + b_ref[...]).astype(BF16)

    return pl.pallas_call(
        body, name=name,
        grid_spec=pltpu.PrefetchScalarGridSpec(
            num_scalar_prefetch=1, grid=(r // rb,),
            in_specs=[pl.BlockSpec((None, rb, cc), lambda i, ck_ref: (ck_ref[0], i, 0)),
                      pl.BlockSpec((rb, cc), lambda i, ck_ref: (i, 0))],
            out_specs=pl.BlockSpec((rb, cc), lambda i, ck_ref: (i, 0))),
        out_shape=jax.ShapeDtypeStruct((r, cc), BF16),
        compiler_params=_cparams(1),
    )(ck, g, b)


def _add_window_call(name, g, b, p, ck, comm):
    _, r, _ = g.shape
    nb, step = WIN_W // 128, WIN_STEP // 128
    n_xc = len(comm.srcs)

    def body(ck_ref, g_ref, b_ref, p0_ref, p1_ref, p2_ref, *rest):
        o_ref = rest[n_xc]
        i = pl.program_id(0)
        begin, finish = comm.make(rest[:n_xc], rest[n_xc + 1:2 * n_xc + 1], rest[-2], rest[-1])
        pl.when(i == 0)(begin)
        own = g_ref[...] + b_ref[...]
        o_ref[...] = ((own + p0_ref[...].astype(F32)) + p1_ref[...].astype(F32)) + p2_ref[...].astype(F32)
        pl.when(i == nb - 1)(finish)

    def peer(j):
        return pl.BlockSpec((None, r, 128), lambda i, ck_ref: (j, 0, i))

    return pl.pallas_call(
        body, name=name,
        grid_spec=pltpu.PrefetchScalarGridSpec(
            num_scalar_prefetch=1, grid=(nb,),
            in_specs=[pl.BlockSpec((None, r, 128), lambda i, ck_ref: (ck_ref[0], 0, step * ck_ref[1] + i)),
                      pl.BlockSpec((r, 128), lambda i, ck_ref: (0, step * ck_ref[1] + i)),
                      peer(0), peer(1), peer(2)] + [ANY] * n_xc,
            out_specs=[pl.BlockSpec((None, r, 128), lambda i, ck_ref: (ck_ref[0], 0, i))] + [ANY] * n_xc,
            scratch_shapes=_comm_sems(comm)),
        out_shape=[jax.ShapeDtypeStruct((2, r, WIN_W), F32)] + list(comm.out_shapes),
        compiler_params=_cparams(1),
    )(ck, g, b, p, p, p, *comm.srcs)


def _add_chips_call(name, g, b, p, ck):
    _, _, r, cc = g.shape
    rb = _row_block(r, cc, 2 * 1024 * 1024)

    def body(ck_ref, g_ref, b_ref, p0_ref, p1_ref, p2_ref, o_ref):
        own = g_ref[...] + b_ref[...]
        o_ref[...] = ((own + p0_ref[...].astype(F32)) + p1_ref[...].astype(F32)) + p2_ref[...].astype(F32)

    def peer(j):
        return pl.BlockSpec((None, rb, cc), lambda i, ck_ref: (j, i, 0))

    return pl.pallas_call(
        body, name=name,
        grid_spec=pltpu.PrefetchScalarGridSpec(
            num_scalar_prefetch=1, grid=(r // rb,),
            in_specs=[pl.BlockSpec((None, None, rb, cc), lambda i, ck_ref: (ck_ref[0], ck_ref[1], i, 0)),
                      pl.BlockSpec((None, rb, cc), lambda i, ck_ref: (ck_ref[1], i, 0)),
                      peer(0), peer(1), peer(2)],
            out_specs=pl.BlockSpec((None, rb, cc), lambda i, ck_ref: (ck_ref[0], i, 0))),
        out_shape=jax.ShapeDtypeStruct((2, r, cc), F32),
        compiler_params=_cparams(1),
    )(ck, g, b, p, p, p)


def _sum8_call(name, g):
    def body(g_ref, o_ref):
        acc = g_ref[0]
        for d in range(1, 8):
            acc = acc + g_ref[d]
        o_ref[...] = acc

    return pl.pallas_call(body, name=name, out_shape=jax.ShapeDtypeStruct(g.shape[1:], F32))(g)


def _adamw_call(name, w, g, m, v):
    r, cc = w.shape
    if r % 8 == 0 or r * cc * 4 <= 1024 * 1024:
        rb = _row_block(r, cc, 1024 * 1024) if r % 8 == 0 else r
        grid, spec = (r // rb,), pl.BlockSpec((rb, cc), lambda i: (i, 0))
    else:
        grid, spec = (cc // 128,), pl.BlockSpec((r, 128), lambda i: (0, i))

    def body(w_ref, g_ref, m_ref, v_ref, d_ref, m2_ref, v2_ref):
        gv = g_ref[...]
        m2 = ADAM_B1 * m_ref[...] + (1.0 - ADAM_B1) * gv
        v2 = ADAM_B2 * v_ref[...] + (1.0 - ADAM_B2) * (gv * gv)
        m_hat = m2 / (1.0 - ADAM_B1 ** ADAM_STEP)
        v_hat = v2 / (1.0 - ADAM_B2 ** ADAM_STEP)
        d_ref[...] = -ADAM_LR * (m_hat / (jnp.sqrt(v_hat) + ADAM_EPS) + ADAM_WD * w_ref[...])
        m2_ref[...] = m2
        v2_ref[...] = v2

    return pl.pallas_call(
        body, name=name, grid=grid, in_specs=[spec] * 4, out_specs=[spec] * 3,
        out_shape=[jax.ShapeDtypeStruct((r, cc), F32)] * 3, compiler_params=_cparams(1),
    )(w, g, m, v)


SMALL = (("norm_gain", D_MODEL), ("b_gate", GLA_KW), ("ret_norm_gain", RET_W), ("gla_norm_gain", GLA_W),
         ("final_norm_gain", D_MODEL), ("w_gate_up", GATE_RANK * GLA_KW), ("meta_tokens", N_META * D_MODEL),
         ("loss", 1))


def _pack_rows(vecs, rows):
    flat = jnp.concatenate([v.reshape(-1) for v in vecs])
    return jnp.pad(flat, (0, rows * 128 - flat.shape[0])).reshape(rows, 128)


def kernel(x, meta_tokens, norm_gain, w_in, w_gate_up, b_gate, ret_norm_gain, gla_norm_gain, w_branch_ret, w_branch_gla, w_out, final_norm_gain, loss_target, m_meta_tokens, m_norm_gain, m_w_in, m_w_gate_up, m_b_gate, m_ret_norm_gain, m_gla_norm_gain, m_w_branch_ret, m_w_branch_gla, m_w_out, m_final_norm_gain, v_meta_tokens, v_norm_gain, v_w_in, v_w_gate_up, v_b_gate, v_ret_norm_gain, v_gla_norm_gain, v_w_branch_ret, v_w_branch_gla, v_w_out, v_final_norm_gain):
    xi, yi, ci = _place()
    kme = 2 * xi + yi
    ck = jnp.stack([ci, kme]).astype(jnp.int32)
    sw_in = w_in.shape[2]

    def my_half(a, dtype):
        r, cc = a.shape
        return lax.dynamic_index_in_dim(a.reshape(2, r // 2, cc), ci, 0, keepdims=False).astype(dtype)

    g_meta, g_wg = _gather8_call("gather_small_weights", [my_half(meta_tokens, F32), my_half(w_gate_up[0], F32)])
    branch_parts = [my_half(w_branch_ret[0], BF16), my_half(w_branch_gla[0], BF16), my_half(w_out[0], BF16)]
    meta = g_meta.reshape(4, 2, N_META // 2, D_MODEL // 4).transpose(1, 2, 0, 3).reshape(N_META, D_MODEL)
    wg_full = g_wg.reshape(4, 2, GATE_RANK // 2, GLA_KW // 4).transpose(1, 2, 0, 3).reshape(GATE_RANK, GLA_KW)

    loc = _device_step(x[0], loss_target[0], meta, norm_gain, my_half(w_in[0], BF16), wg_full, b_gate, ret_norm_gain,
                       gla_norm_gain,
                       branch_parts, final_norm_gain, ck)
    names = ("w_in", "w_branch_ret", "w_branch_gla", "w_out")
    full = [loc[nm] for nm in names]
    big_w = dict(w_in=w_in[0], w_branch_ret=w_branch_ret[0], w_branch_gla=w_branch_gla[0], w_out=w_out[0])
    big_m = dict(w_in=m_w_in[0], w_branch_ret=m_w_branch_ret[0], w_branch_gla=m_w_branch_gla[0], w_out=m_w_out[0])
    big_v = dict(w_in=v_w_in[0], w_branch_ret=v_w_branch_ret[0], w_branch_gla=v_w_branch_gla[0], w_out=v_w_out[0])
    grads, deltas, new_m, new_v = {}, {}, {}, {}
    for nm, f in zip(names, full):
        shape = big_w[nm].shape
        if nm == "w_in":
            f = lax.dynamic_slice_in_dim(f, (sw_in - WIN_STEP) * kme, sw_in, axis=2)
        g = f.reshape(shape)
        if nm == "w_in":
            d, m2, v2 = (a.T for a in _adamw_call("adamw_" + nm, big_w[nm].T, g.T, big_m[nm].T, big_v[nm].T))
        else:
            d, m2, v2 = _adamw_call("adamw_" + nm, big_w[nm], g, big_m[nm], big_v[nm])
        grads[nm], deltas[nm], new_m[nm], new_v[nm] = (a.reshape((1,) + shape) for a in (g, d, m2, v2))

    tot = _sum8_call("sum_small_grads", loc["small"]).reshape(-1)
    off = 0
    sg = {}
    for nm, sz in SMALL:
        sg[nm] = tot[off:off + sz]
        off += sz
    loss = sg.pop("loss")[0]
    sg["w_gate_up"] = lax.dynamic_slice_in_dim(sg["w_gate_up"].reshape(GATE_RANK, GLA_KW), kme * (GLA_KW // 4),
                                               GLA_KW // 4, axis=1)
    sg["meta_tokens"] = lax.dynamic_slice_in_dim(sg["meta_tokens"].reshape(N_META, D_MODEL), kme * (D_MODEL // 4),
                                                 D_MODEL // 4, axis=1)
    small_w = dict(norm_gain=norm_gain, b_gate=b_gate, ret_norm_gain=ret_norm_gain, gla_norm_gain=gla_norm_gain,
                   final_norm_gain=final_norm_gain, w_gate_up=w_gate_up, meta_tokens=meta_tokens)
    small_m = dict(norm_gain=m_norm_gain, b_gate=m_b_gate, ret_norm_gain=m_ret_norm_gain,
                   gla_norm_gain=m_gla_norm_gain, final_norm_gain=m_final_norm_gain, w_gate_up=m_w_gate_up,
                   meta_tokens=m_meta_tokens)
    small_v = dict(norm_gain=v_norm_gain, b_gate=v_b_gate, ret_norm_gain=v_ret_norm_gain,
                   gla_norm_gain=v_gla_norm_gain, final_norm_gain=v_final_norm_gain, w_gate_up=v_w_gate_up,
                   meta_tokens=v_meta_tokens)
    for nm in small_w:
        shape = small_w[nm].shape
        as2d = lambda a: a.reshape((-1, shape[-1]))
        grads[nm] = sg[nm].reshape(shape)
        deltas[nm], new_m[nm], new_v[nm] = (a.reshape(shape) for a in _adamw_call(
            "adamw_" + nm, as2d(small_w[nm]), as2d(sg[nm]), as2d(small_m[nm]), as2d(small_v[nm])))

    out_order = ("meta_tokens", "norm_gain", "w_in", "w_gate_up", "b_gate", "ret_norm_gain", "gla_norm_gain",
                 "w_branch_ret", "w_branch_gla", "w_out", "final_norm_gain")
    dx = loc["dx"].reshape(x.shape)
    return (loss, dx, *[grads[nm] for nm in out_order], *[deltas[nm] for nm in out_order],
            *[new_m[nm] for nm in out_order], *[new_v[nm] for nm in out_order])
```

```python
import math
from typing import Callable, NamedTuple

import numpy as np
import jax
import jax.numpy as jnp
from jax import lax
from jax.experimental import pallas as pl
from jax.experimental.pallas import tpu as pltpu

F32 = jnp.float32
BF16 = jnp.bfloat16

D_MODEL = 1024
N_META = 16
EPS = 1e-6
ROPE_BASE = 10000.0
RET_HEADS, RET_QK, RET_V = 4, 256, 512
RET_W = RET_HEADS * RET_V
GLA_HEADS, GLA_K, GLA_V = 4, 128, 256
GLA_W = GLA_HEADS * GLA_V
GLA_KW = GLA_HEADS * GLA_K
GATE_RANK = 16
GATE_TAU = 16.0
GLA_SUB = 16

TM = 256
T0 = TM
PADF = T0 - N_META
GC = 128
GS = 3
TB = 768
TK = 768

W_R = 6144
W_G = 3088
W_GP = 3200
W_M = 2048
IN_COLS = W_R + W_G + W_M
WIN_STEP = (IN_COLS // 4) // 128 * 128
WIN_W = -(-(3 * (IN_COLS // 4 - WIN_STEP) + IN_COLS // 4) // 128) * 128
IN_PAD = 3 * WIN_STEP + WIN_W

ADAM_LR, ADAM_B1, ADAM_B2, ADAM_EPS, ADAM_WD, ADAM_STEP = 0.001, 0.9, 0.999, 1e-08, 0.01, 10

VMEM_LIMIT = 56 * 1024 * 1024

NN = ((1,), (0,))
NT = ((1,), (1,))
TN = ((0,), (0,))


def _dot(a, b, dims):
    return lax.dot_general(a, b, (dims, ((), ())), preferred_element_type=F32)


def _cparams(n_axes):
    return pltpu.CompilerParams(dimension_semantics=("arbitrary",) * n_axes, vmem_limit_bytes=VMEM_LIMIT)


def _sigmoid(x):
    return 0.5 * jnp.tanh(0.5 * x) + 0.5


def _silu(x):
    h = 0.5 * x
    return h + h * jnp.tanh(h)


def _head_mean(x):
    return jnp.mean(x, axis=-1, keepdims=True)


def _split3(x):
    hi = x.astype(BF16)
    r1 = x - hi.astype(F32)
    mid = r1.astype(BF16)
    lo = (r1 - mid.astype(F32)).astype(BF16)
    return hi, mid, lo


def _exact_pm(p, x):
    hi, mid, lo = _split3(x)
    return _dot(p, hi, NN) + _dot(p, mid, NN) + _dot(p, lo, NN)


def _rms_call(x2d, head, gain, comm):
    tp = T0 + x2d.shape[0]
    nt = tp // TM
    n_xc = len(comm.srcs)

    def body(x_ref, hd_ref, g_ref, *rest):
        xc_src = rest[:n_xc]
        h_ref, u_ref = rest[n_xc:n_xc + 2]
        xc_dst = rest[n_xc + 2:2 * n_xc + 2]
        i = pl.program_id(0)
        begin, finish = comm.make(xc_src, xc_dst, rest[-2], rest[-1])
        pl.when(i == 0)(begin)
        h = jnp.where(i == 0, hd_ref[...], x_ref[...])
        h_ref[...] = h
        r = lax.rsqrt(jnp.mean(h * h, axis=-1, keepdims=True) + EPS)
        u_ref[...] = (h * r * g_ref[...]).astype(BF16)
        pl.when(i == nt - 1)(finish)

    tile = pl.BlockSpec((TM, D_MODEL), lambda i: (i, 0))
    return pl.pallas_call(
        body, name="rms_in", grid=(nt,),
        in_specs=[pl.BlockSpec((TM, D_MODEL), lambda i: (jnp.maximum(i - 1, 0), 0)),
                  pl.BlockSpec((T0, D_MODEL), lambda i: (0, 0)), pl.BlockSpec((1, D_MODEL), lambda i: (0, 0))]
        + [ANY] * n_xc,
        out_specs=[tile, tile] + [ANY] * n_xc,
        out_shape=[jax.ShapeDtypeStruct((tp, D_MODEL), F32), jax.ShapeDtypeStruct((tp, D_MODEL), BF16)]
        + list(comm.out_shapes),
        scratch_shapes=_comm_sems(comm), compiler_params=_cparams(1),
    )(x2d, head, gain, *comm.srcs)


PROJ_ROWS_MAX = 1408


def _proj_rows(m):
    return max(r for r in range(16, PROJ_ROWS_MAX + 1, 16) if m % r == 0)


def _mm_nn(name, a, b, out_dtype, tn, col0, ncols, epilogue=None, extras=(), extra_specs=()):
    m, k = a.shape
    nj, j0 = ncols // tn, col0 // tn
    tb = _proj_rows(m)

    def body(a_ref, b_ref, *rest):
        *ex, o_ref = rest
        acc = _dot(a_ref[...], b_ref[...], NN)
        if epilogue is None:
            o_ref[...] = acc.astype(out_dtype)
        else:
            epilogue(acc, o_ref, *ex)

    return pl.pallas_call(
        body, name=name, grid=(nj, m // tb),
        in_specs=[pl.BlockSpec((tb, k), lambda j, i: (i, 0)), pl.BlockSpec((k, tn), lambda j, i: (0, j0 + j))]
        + list(extra_specs),
        out_specs=pl.BlockSpec((tb, tn), lambda j, i: (i, j)),
        out_shape=jax.ShapeDtypeStruct((m, ncols), out_dtype),
        compiler_params=_cparams(2),
    )(a, b, *extras)


def _rope_tables(tp):
    half = RET_QK // 2
    pos = np.arange(tp, dtype=np.float32) - np.float32(PADF)
    inv = (ROPE_BASE ** (-np.arange(half, dtype=np.float64) / half)).astype(np.float32)
    ang = (pos[:, None] * inv[None, :]).astype(np.float64)
    return np.cos(ang).astype(np.float32), np.sin(ang).astype(np.float32)


def _rope_epilogue(acc, o_ref, cos_ref, sin_ref):
    scale = jnp.where(pl.program_id(0) == 1, RET_QK ** -0.5, 1.0).astype(F32)
    cos, sin = cos_ref[...], sin_ref[...]
    half = RET_QK // 2
    for h in range(RET_HEADS):
        t1 = acc[:, h * RET_QK:h * RET_QK + half]
        t2 = acc[:, h * RET_QK + half:(h + 1) * RET_QK]
        o_ref[:, h * RET_QK:h * RET_QK + half] = ((t1 * cos - t2 * sin) * scale).astype(BF16)
        o_ref[:, h * RET_QK + half:(h + 1) * RET_QK] = ((t2 * cos + t1 * sin) * scale).astype(BF16)


def _gqk_epilogue(acc, o_ref):
    o_ref[:, :GLA_KW] = acc[:, :GLA_KW] * (GLA_K ** -0.5)
    o_ref[:, GLA_KW:] = acc[:, GLA_KW:]


class _Comm(NamedTuple):
    srcs: tuple
    out_shapes: tuple
    n_sems: int
    make: Callable


def _comm_sems(comm):
    return [pltpu.SemaphoreType.DMA((comm.n_sems,)), pltpu.SemaphoreType.DMA((comm.n_sems,))]


def _start_wait(copies):
    def begin():
        for cp in copies:
            cp.start()

    def finish():
        for cp in copies:
            cp.wait()

    return begin, finish


def _other_chips(x, y):
    return [(1 - x, y), (x, 1 - y), (1 - x, 1 - y)]


def _gather_plan(parts, relay=()):
    n = len(parts)
    relay = tuple(relay) + (False,) * (n - len(relay))

    def make(x_refs, out_refs, send_sems, recv_sems):
        x, y, c = _place()
        me, sibling = (x, y, c), (x, y, 1 - c)
        xn, yn, dg = (1 - x, y), (x, 1 - y), (1 - x, 1 - y)

        def slot(t, px, py, pc, half=None):
            ref = out_refs[t].at[4 * px + 2 * py + pc]
            if half is None:
                return ref
            rows = ref.shape[0] // 2
            return ref.at[pl.ds(half * rows, rows)]

        def copy(t, k, dst, to, src=None):
            return pltpu.make_async_remote_copy(
                src_ref=dst if src is None else src, dst_ref=dst, send_sem=send_sems.at[8 * t + k],
                recv_sem=recv_sems.at[8 * t + k], device_id=to, device_id_type=MESH)

        mine = [pltpu.make_async_copy(x_refs[t], slot(t, *me), send_sems.at[8 * n + t]) for t in range(n)]
        sent = []
        for t in range(n):
            sent.append(copy(t, 0, slot(t, *me), sibling, src=x_refs[t]))
            sent.append(copy(t, 1, slot(t, *me), (*xn, c), src=x_refs[t]))
            sent.append(copy(t, 2, slot(t, *me), (*yn, c), src=x_refs[t]))
            if not relay[t]:
                sent.append(copy(t, 3, slot(t, *me), (*dg, c), src=x_refs[t]))

        def begin():
            for cp in mine + sent:
                cp.start()

        def finish():
            later = []

            def start(cp):
                cp.start()
                later.append(cp)

            for t in range(n):
                copy(t, 2, slot(t, *yn, c), me).wait_recv()
                if relay[t]:
                    start(copy(t, 3, slot(t, *yn, c, half=0), (*xn, c)))
                start(copy(t, 6, slot(t, *yn, c), sibling))
            for t in range(n):
                copy(t, 1, slot(t, *xn, c), me).wait_recv()
                if relay[t]:
                    start(copy(t, 4, slot(t, *xn, c, half=1), (*yn, c)))
                start(copy(t, 5, slot(t, *xn, c), sibling))
            for t in range(n):
                if relay[t]:
                    copy(t, 3, slot(t, *dg, c, half=0), me).wait_recv()
                    copy(t, 4, slot(t, *dg, c, half=1), me).wait_recv()
                else:
                    copy(t, 3, slot(t, *dg, c), me).wait_recv()
                start(copy(t, 7, slot(t, *dg, c), sibling))
            for t in range(n):
                copy(t, 0, slot(t, *sibling), me).wait_recv()
                copy(t, 5, slot(t, *xn, 1 - c), me).wait_recv()
                copy(t, 6, slot(t, *yn, 1 - c), me).wait_recv()
                copy(t, 7, slot(t, *dg, 1 - c), me).wait_recv()
            for cp in sent + later:
                cp.wait_send()
            for cp in mine:
                cp.wait()

        return begin, finish

    return _Comm(tuple(parts), tuple(jax.ShapeDtypeStruct((8,) + p.shape, p.dtype) for p in parts), 9 * n, make)


def _exchange_plan(ss):
    def make(s_refs, b_refs, send_sems, recv_sems):
        x, y, c = _place()
        return _start_wait([pltpu.make_async_remote_copy(
            src_ref=s_refs[t].at[2 * chip[0] + chip[1]], dst_ref=b_refs[t].at[j], send_sem=send_sems.at[3 * t + j],
            recv_sem=recv_sems.at[3 * t + j], device_id=(*chip, c), device_id_type=MESH)
            for t in range(len(s_refs)) for j, chip in enumerate(_other_chips(x, y))])

    return _Comm(tuple(ss), tuple(jax.ShapeDtypeStruct((3,) + s.shape[1:], s.dtype) for s in ss), 3 * len(ss), make)


def _exchange_window_plan(s):
    def make(s_refs, b_refs, send_sems, recv_sems):
        x, y, c = _place()
        return _start_wait([pltpu.make_async_remote_copy(
            src_ref=s_refs[0].at[:, pl.ds(pl.multiple_of((2 * chip[0] + chip[1]) * WIN_STEP, 128), WIN_W)],
            dst_ref=b_refs[0].at[j], send_sem=send_sems.at[j], recv_sem=recv_sems.at[j], device_id=(*chip, c),
            device_id_type=MESH) for j, chip in enumerate(_other_chips(x, y))])

    return _Comm((s,), (jax.ShapeDtypeStruct((3, s.shape[0], WIN_W), s.dtype),), 3, make)


def _swap_plan(gs):
    def make(g_refs, b_refs, send_sems, recv_sems):
        x, y, c = _place()
        return _start_wait([pltpu.make_async_remote_copy(
            src_ref=g_refs[t].at[1 - c], dst_ref=b_refs[t], send_sem=send_sems.at[t], recv_sem=recv_sems.at[t],
            device_id=(x, y, 1 - c), device_id_type=MESH) for t in range(len(g_refs))])

    return _Comm(tuple(gs), tuple(jax.ShapeDtypeStruct(g.shape[1:], g.dtype) for g in gs), len(gs), make)


def _spread_plan(parts):
    def make(p_refs, o_refs, send_sems, recv_sems):
        x, y, c = _place()
        copies = []
        for t in range(len(p_refs)):
            mine = o_refs[t].at[4 * x + 2 * y + c]
            copies.append(pltpu.make_async_copy(p_refs[t], mine, send_sems.at[7 * len(p_refs) + t]))
            for r in range(1, 8):
                peer = (1 - x if r & 4 else x, 1 - y if r & 2 else y, 1 - c if r & 1 else c)
                copies.append(pltpu.make_async_remote_copy(
                    src_ref=p_refs[t], dst_ref=mine, send_sem=send_sems.at[7 * t + r - 1],
                    recv_sem=recv_sems.at[7 * t + r - 1], device_id=peer, device_id_type=MESH))
        return _start_wait(copies)

    return _Comm(tuple(parts), tuple(jax.ShapeDtypeStruct((8,) + p.shape, p.dtype) for p in parts), 8 * len(parts),
                 make)


def _mm_nt_acc(name, a, w, tk, acc_in=None, epilogue=None, extras=(), extra_specs=(), extra_out_shapes=(),
               extra_out_specs=(), extra_scratch=(), comm=None, tb=TB):
    m, k = a.shape
    n = w.shape[0]
    nk, ni = k // tk, m // tb
    has_acc = acc_in is not None
    n_xc = len(comm.srcs) if comm else 0
    n_es = len(extra_scratch)

    def body(*refs):
        a_ref, w_ref = refs[0], refs[1]
        pos = 2
        acc_ref = None
        if has_acc:
            acc_ref = refs[pos]
            pos += 1
        ex = refs[pos:pos + len(extras)]
        pos += len(extras)
        xc_src = refs[pos:pos + n_xc]
        pos += n_xc
        n_scr = 1 + n_es + (2 if n_xc else 0)
        outs = refs[pos:len(refs) - n_scr - n_xc]
        xc_dst = refs[len(refs) - n_scr - n_xc:len(refs) - n_scr]
        scr = refs[len(refs) - n_scr]
        es = refs[len(refs) - n_scr + 1:len(refs) - n_scr + 1 + n_es]
        i, kk = pl.program_id(0), pl.program_id(1)
        if n_xc:
            begin, finish = comm.make(xc_src, xc_dst, refs[-2], refs[-1])
            pl.when((i == 0) & (kk == 0))(begin)

        @pl.when(kk == 0)
        def _():
            scr[...] = acc_ref[...] if has_acc else jnp.zeros_like(scr)

        scr[...] += _dot(a_ref[...], w_ref[...], NT)

        @pl.when(kk == nk - 1)
        def _():
            if epilogue is None:
                outs[0][...] = scr[...]
            else:
                epilogue(scr[...], outs, i, ni, *ex, *es)

        if n_xc:
            pl.when((i == ni - 1) & (kk == nk - 1))(finish)

    in_specs = [pl.BlockSpec((tb, tk), lambda i, kk: (i, kk)), pl.BlockSpec((n, tk), lambda i, kk: (0, kk))]
    args = [a, w]
    if has_acc:
        in_specs.append(pl.BlockSpec((tb, n), lambda i, kk: (i, 0)))
        args.append(acc_in)
    in_specs += list(extra_specs) + [ANY] * n_xc
    args += list(extras) + (list(comm.srcs) if comm else [])
    if epilogue is None:
        out_shape = [jax.ShapeDtypeStruct((m, n), F32)]
        out_specs = [pl.BlockSpec((tb, n), lambda i, kk: (i, 0))]
    else:
        out_shape, out_specs = list(extra_out_shapes), list(extra_out_specs)
    scratch = [pltpu.VMEM((tb, n), F32)] + list(extra_scratch)
    if n_xc:
        out_shape += list(comm.out_shapes)
        out_specs += [ANY] * n_xc
        scratch += _comm_sems(comm)
    return pl.pallas_call(
        body, name=name, grid=(ni, nk), in_specs=in_specs, out_specs=out_specs, out_shape=out_shape,
        scratch_shapes=scratch, compiler_params=_cparams(2),
    )(*args)


def _rms_bwd_epilogue(du, outs, i, ni, h_ref, g_ref, dh1_ref, obuf, sems):
    dx_ref, dmeta_ref, dg_ref = outs
    h = h_ref[...]
    r = lax.rsqrt(jnp.mean(h * h, axis=-1, keepdims=True) + EPS)
    xh = h * r
    dxh = du * g_ref[...]
    dh0 = dh1_ref[...] + r * (dxh - xh * jnp.mean(dxh * xh, axis=-1, keepdims=True))

    def put(slot, tile):
        return pltpu.make_async_copy(obuf.at[slot], dx_ref.at[pl.ds(pl.multiple_of(tile * TB - T0, 8), TB)],
                                     sems.at[slot])

    @pl.when(i == 0)
    def _():
        dg_ref[...] = jnp.zeros_like(dg_ref)
        dmeta_ref[...] = dh0[PADF:T0, :]
        obuf[0] = dh0
        first = pltpu.make_async_copy(obuf.at[0, pl.ds(T0, TB - T0)], dx_ref.at[pl.ds(0, TB - T0)], sems.at[0])
        first.start()
        first.wait()

    @pl.when(i >= 1)
    def _():
        slot = i % 2

        @pl.when(i >= 3)
        def _():
            put(slot, i - 2).wait()

        obuf[slot] = dh0
        put(slot, i).start()

    dg_ref[...] += jnp.sum(du * xh, axis=0, keepdims=True)

    @pl.when(i == ni - 1)
    def _():
        for tile in (ni - 2, ni - 1):
            if tile >= 1:
                put(tile % 2, tile).wait()


def _mm_tn(name, a, b, bn, ncols=None, bcol0=0, into=None, col0=0, out_cols=None):
    t, m = a.shape
    n = ncols or b.shape[1]
    j0, bj0 = col0 // bn, bcol0 // bn

    def body(a_ref, b_ref, *rest):
        o_ref = rest[-1]

        @pl.when(pl.program_id(1) == 0)
        def _():
            o_ref[...] = jnp.zeros_like(o_ref)

        o_ref[...] += _dot(a_ref[...], b_ref[...], TN)

    in_specs = [pl.BlockSpec((TK, m), lambda j, kk: (kk, 0)), pl.BlockSpec((TK, bn), lambda j, kk: (kk, bj0 + j))]
    args = [a, b]
    aliases = {}
    if into is not None:
        in_specs.append(ANY)
        args.append(into)
        aliases = {2: 0}
        out_cols = into.shape[1]
    return pl.pallas_call(
        body, name=name, grid=(n // bn, t // TK), in_specs=in_specs,
        out_specs=pl.BlockSpec((m, bn), lambda j, kk: (0, j0 + j)),
        out_shape=jax.ShapeDtypeStruct((m, out_cols or n), F32), input_output_aliases=aliases,
        compiler_params=_cparams(2),
    )(*args)


def _place_merge_cols_call(dwp, dw_m, dw_glr):
    c0 = W_R + W_GP - 128
    tail = IN_PAD - c0
    rows = 256

    def body(m_ref, low, p_ref, o_ref, buf, sem):
        for r in range(0, D_MODEL, rows):
            buf[r:r + rows, :] = jnp.concatenate(
                [low[r:r + rows, :GATE_RANK], m_ref[r:r + rows, :],
                 jnp.zeros((rows, tail - GATE_RANK - W_M), F32)], axis=1)
        put = pltpu.make_async_copy(buf, o_ref.at[:, pl.ds(c0, tail)], sem)
        put.start()
        put.wait()

    return pl.pallas_call(
        body, name="place_merge_cols",
        in_specs=[pl.BlockSpec(memory_space=pltpu.VMEM), pl.BlockSpec(memory_space=pltpu.VMEM), ANY], out_specs=ANY,
        out_shape=jax.ShapeDtypeStruct(dwp.shape, F32), input_output_aliases={2: 0},
        scratch_shapes=[pltpu.VMEM((D_MODEL, tail), F32), pltpu.SemaphoreType.DMA],
        compiler_params=pltpu.CompilerParams(vmem_limit_bytes=VMEM_LIMIT),
    )(dw_m, dw_glr, dwp)


def _ret_fill_decay(lg_ref, dm_scr):
    c = TM
    ii = lax.broadcasted_iota(jnp.int32, (c, c), 0)
    jj = lax.broadcasted_iota(jnp.int32, (c, c), 1)
    rel = (ii - jj).astype(F32)
    for h in range(RET_HEADS):
        dm_scr[h] = jnp.where(rel >= 0, jnp.exp(jnp.maximum(rel, 0.0) * lg_ref[h]), 0.0)


def _ret_consts(lg, dm_ref):
    c = TM
    idx = lax.broadcasted_iota(jnp.int32, (c, 1), 0).astype(F32)
    xi = jnp.exp((idx + 1.0) * lg)
    zeta = jnp.exp((c - 1.0 - idx) * lg)
    gc = jnp.exp(jnp.full((1, 1), c, F32) * lg)
    return dm_ref[...], xi, zeta, gc


def _ret_fwd_call(rqk, rv, rg, gain, lgam):
    tp = rqk.shape[0]
    nc = tp // TM

    def body(lg_ref, qk_ref, v_ref, rg_ref, g_ref, o_ref, a_ref, st_ref, sc_ref, s_scr, dm_scr):
        @pl.when(pl.program_id(0) == 0)
        def _():
            s_scr[...] = jnp.zeros_like(s_scr)
            _ret_fill_decay(lg_ref, dm_scr)

        for h in range(RET_HEADS):
            dm, xi, zeta, gc = _ret_consts(lg_ref[h], dm_scr.at[h])
            q = qk_ref[:, h * RET_QK:(h + 1) * RET_QK]
            k = qk_ref[:, D_MODEL + h * RET_QK:D_MODEL + (h + 1) * RET_QK]
            v = v_ref[:, h * RET_V:(h + 1) * RET_V]
            sb = s_scr[h].astype(BF16)
            st_ref[0, h] = sb
            s = (_dot(q, k, NT) * dm).astype(BF16)
            sc_ref[0, h] = s
            o = _dot(s, v, NN) + xi * _dot(q, sb, NN)
            kz = (k.astype(F32) * zeta).astype(BF16)
            s_scr[h] = gc * s_scr[h] + _dot(kz, v, TN)
            o_ref[:, h * RET_V:(h + 1) * RET_V] = o
            mu = _head_mean(o)
            xc = o - mu
            xh = xc * lax.rsqrt(_head_mean(xc * xc) + EPS)
            a_ref[:, h * RET_V:(h + 1) * RET_V] = (
                xh * g_ref[:, h * RET_V:(h + 1) * RET_V] * _silu(rg_ref[:, h * RET_V:(h + 1) * RET_V])).astype(BF16)

    return pl.pallas_call(
        body, name="ret_fwd", grid=(nc,),
        in_specs=[pl.BlockSpec(memory_space=pltpu.SMEM),
                  pl.BlockSpec((TM, 2 * D_MODEL), lambda n: (n, 0)),
                  pl.BlockSpec((TM, RET_W), lambda n: (n, 0)),
                  pl.BlockSpec((TM, RET_W), lambda n: (n, 0)),
                  pl.BlockSpec((1, RET_W), lambda n: (0, 0))],
        out_specs=[pl.BlockSpec((TM, RET_W), lambda n: (n, 0)),
                   pl.BlockSpec((TM, RET_W), lambda n: (n, 0)),
                   pl.BlockSpec((1, RET_HEADS, RET_QK, RET_V), lambda n: (n, 0, 0, 0)),
                   pl.BlockSpec((1, RET_HEADS, TM, TM), lambda n: (n, 0, 0, 0))],
        out_shape=[jax.ShapeDtypeStruct((tp, RET_W), F32), jax.ShapeDtypeStruct((tp, RET_W), BF16),
                   jax.ShapeDtypeStruct((nc, RET_HEADS, RET_QK, RET_V), BF16),
                   jax.ShapeDtypeStruct((nc, RET_HEADS, TM, TM), BF16)],
        scratch_shapes=[pltpu.VMEM((RET_HEADS, RET_QK, RET_V), F32), pltpu.VMEM((RET_HEADS, TM, TM), F32)],
        compiler_params=_cparams(1),
    )(lgam, rqk, rv, rg, gain)


def _ret_bwd_call(rqk, rv, rg, o_ret, dpr, wbr, states, scores, gain, lgam, cos, sin):
    tp = rqk.shape[0]
    nc = tp // TM
    half = RET_QK // 2

    def body(lg_ref, qk_ref, v_ref, rg_ref, o_ref, dpr_ref, wbr_ref, st_ref, sc_ref, g_ref, cos_ref, sin_ref, dp_ref,
             dg_ref, ds_scr, dm_scr):
        @pl.when(pl.program_id(0) == 0)
        def _():
            ds_scr[...] = jnp.zeros_like(ds_scr)
            dg_ref[...] = jnp.zeros_like(dg_ref)
            _ret_fill_decay(lg_ref, dm_scr)

        cos, sin = cos_ref[...], sin_ref[...]
        for h in range(RET_HEADS):
            hs = slice(h * RET_V, (h + 1) * RET_V)
            dm, xi, zeta, gc = _ret_consts(lg_ref[h], dm_scr.at[h])
            o = o_ref[:, hs]
            mu = _head_mean(o)
            xc = o - mu
            rstd = lax.rsqrt(_head_mean(xc * xc) + EPS)
            xh = xc * rstd
            gain_h = g_ref[:, hs]
            g = rg_ref[:, hs]
            sg = _sigmoid(g)
            silu = g * sg
            dah = _dot(dpr_ref[...], wbr_ref[hs, :], NT)
            dp_ref[:, 4 * D_MODEL + h * RET_V:4 * D_MODEL + (h + 1) * RET_V] = (
                dah * (xh * gain_h) * (sg * (1.0 + g * (1.0 - sg)))).astype(BF16)
            dn = dah * silu
            dg_ref[:, hs] += jnp.sum(dn * xh, axis=0, keepdims=True)
            dxh = dn * gain_h
            do = rstd * (dxh - _head_mean(dxh) - xh * _head_mean(dxh * xh))
            dob = do.astype(BF16)
            q = qk_ref[:, h * RET_QK:(h + 1) * RET_QK]
            k = qk_ref[:, D_MODEL + h * RET_QK:D_MODEL + (h + 1) * RET_QK]
            v = v_ref[:, hs]
            sp = st_ref[0, h]
            ds = ds_scr[h]
            dsb = ds.astype(BF16)
            s = sc_ref[0, h]
            dsc = (_dot(dob, v, NT) * dm).astype(BF16)
            dq = _dot(dsc, k, NN) + xi * _dot(dob, sp, NT)
            dk = _dot(dsc, q, TN) + zeta * _dot(v, dsb, NT)
            kz = (k.astype(F32) * zeta).astype(BF16)
            dv = _dot(s, dob, TN) + _dot(kz, dsb, NN)
            qx = (q.astype(F32) * xi).astype(BF16)
            ds_scr[h] = gc * ds + _dot(qx, dob, TN)
            dp_ref[:, 2 * D_MODEL + h * RET_V:2 * D_MODEL + (h + 1) * RET_V] = dv.astype(BF16)
            dk = dk * (RET_QK ** -0.5)
            for base, t in ((0, dq), (D_MODEL, dk)):
                t1, t2 = t[:, :half], t[:, half:]
                dp_ref[:, base + h * RET_QK:base + h * RET_QK + half] = (t1 * cos + t2 * sin).astype(BF16)
                dp_ref[:, base + h * RET_QK + half:base + (h + 1) * RET_QK] = (t2 * cos - t1 * sin).astype(BF16)

    rev = lambda n: (nc - 1 - n, 0)
    return pl.pallas_call(
        body, name="ret_bwd", grid=(nc,),
        in_specs=[pl.BlockSpec(memory_space=pltpu.SMEM),
                  pl.BlockSpec((TM, 2 * D_MODEL), rev),
                  pl.BlockSpec((TM, RET_W), rev),
                  pl.BlockSpec((TM, RET_W), rev),
                  pl.BlockSpec((TM, RET_W), rev),
                  pl.BlockSpec((TM, D_MODEL), rev),
                  pl.BlockSpec((RET_W, D_MODEL), lambda n: (0, 0)),
                  pl.BlockSpec((1, RET_HEADS, RET_QK, RET_V), lambda n: (nc - 1 - n, 0, 0, 0)),
                  pl.BlockSpec((1, RET_HEADS, TM, TM), lambda n: (nc - 1 - n, 0, 0, 0)),
                  pl.BlockSpec((1, RET_W), lambda n: (0, 0)),
                  pl.BlockSpec((TM, half), rev),
                  pl.BlockSpec((TM, half), rev)],
        out_specs=[pl.BlockSpec((TM, W_R), rev), pl.BlockSpec((1, RET_W), lambda n: (0, 0))],
        out_shape=[jax.ShapeDtypeStruct((tp, W_R), BF16), jax.ShapeDtypeStruct((1, RET_W), F32)],
        scratch_shapes=[pltpu.VMEM((RET_HEADS, RET_QK, RET_V), F32), pltpu.VMEM((RET_HEADS, TM, TM), F32)],
        compiler_params=_cparams(1),
    )(lgam, rqk, rv, rg, o_ret, dpr, wbr, states, scores, gain, cos, sin)


GLA_LEVELS = tuple(GC >> (s + 1) for s in range(int(math.log2(GC // GLA_SUB))))
NLEV = len(GLA_LEVELS)


def _gla_tril():
    return np.tril(np.ones((GC, GC), np.float32))


def _gla_masks():
    ii = lax.broadcasted_iota(jnp.int32, (GC, GC), 0)
    jj = lax.broadcasted_iota(jnp.int32, (GC, GC), 1)
    masks = []
    for m in GLA_LEVELS:
        sh = int(math.log2(2 * m))
        masks.append(((ii >> sh) == (jj >> sh)) & ((ii & m) != 0) & ((jj & m) == 0))
    sh = int(math.log2(GLA_SUB))
    md = ((ii >> sh) == (jj >> sh)) & (jj <= ii)
    row = lax.broadcasted_iota(jnp.int32, (GC, 1), 0)
    second = [(row & m) != 0 for m in GLA_LEVELS]
    return masks, md, second


def _gla_gate_call(u, w_g, wg, bg, pmat):
    tp = u.shape[0]
    gb = _proj_rows(tp)
    assert gb % GC == 0

    def body(u_ref, w_ref, wg_ref, bg_ref, p_ref, glr_ref, z_ref, b_ref):
        glr = _dot(u_ref[...], w_ref[...], NN)
        glr_ref[...] = glr
        z = _dot(glr.astype(BF16), wg_ref[...], NN) + bg_ref[...]
        z_ref[...] = z
        la = (jnp.minimum(z, 0.0) - jnp.log1p(jnp.exp(-jnp.abs(z)))) * (1.0 / GATE_TAU)
        for r in range(0, gb, GC):
            b_ref[r:r + GC, :] = _exact_pm(p_ref[...], la[r:r + GC, :])

    tile = pl.BlockSpec((gb, GLA_KW), lambda i: (i, 0))
    return pl.pallas_call(
        body, name="gla_gate", grid=(tp // gb,),
        in_specs=[pl.BlockSpec((gb, D_MODEL), lambda i: (i, 0)),
                  pl.BlockSpec((D_MODEL, 128), lambda i: (0, (W_GP - 128) // 128)),
                  pl.BlockSpec((128, GLA_KW), lambda i: (0, 0)),
                  pl.BlockSpec((1, GLA_KW), lambda i: (0, 0)), pl.BlockSpec((GC, GC), lambda i: (0, 0))],
        out_specs=[pl.BlockSpec((gb, 128), lambda i: (i, 0)), tile, tile],
        out_shape=[jax.ShapeDtypeStruct((tp, 128), F32), jax.ShapeDtypeStruct((tp, GLA_KW), F32),
                   jax.ShapeDtypeStruct((tp, GLA_KW), F32)],
        compiler_params=_cparams(1),
    )(u, w_g, wg, bg, pmat)


def _gla_gate_bwd_call(db, z, glr, u, wg, pmat_t, d_g):
    tp = db.shape[0]
    gb = _proj_rows(tp)
    assert gb % GC == 0 and (W_GP - 128) % 128 == 0

    def body(db_ref, z_ref, glr_ref, u_ref, wg_ref, pt_ref, dgin_ref, dg_ref, dwg_ref, dbg_ref, dwl_ref):
        i = pl.program_id(0)

        @pl.when(i == 0)
        def _():
            dwg_ref[...] = jnp.zeros_like(dwg_ref)
            dbg_ref[...] = jnp.zeros_like(dbg_ref)
            dwl_ref[...] = jnp.zeros_like(dwl_ref)

        dla = jnp.concatenate([_exact_pm(pt_ref[...], db_ref[r:r + GC, :]) for r in range(0, gb, GC)], axis=0)
        row = i * gb + lax.broadcasted_iota(jnp.int32, (gb, 1), 0)
        dz = jnp.where(row >= PADF, dla * (1.0 / GATE_TAU) * _sigmoid(-z_ref[...]), 0.0)
        dzb = dz.astype(BF16)
        dglr = _dot(dzb, wg_ref[...], NT).astype(BF16)
        dg_ref[...] = dglr
        dwg_ref[...] += _dot(glr_ref[...].astype(BF16), dzb, TN)
        dbg_ref[...] += jnp.sum(dz, axis=0, keepdims=True)
        dwl_ref[...] += _dot(u_ref[...], dglr, TN)

    tile = pl.BlockSpec((gb, GLA_KW), lambda i: (i, 0))
    const = lambda i: (0, 0)
    return pl.pallas_call(
        body, name="gla_gate_bwd", grid=(tp // gb,),
        in_specs=[tile, tile, pl.BlockSpec((gb, 128), lambda i: (i, 0)), pl.BlockSpec((gb, D_MODEL), lambda i: (i, 0)),
                  pl.BlockSpec((128, GLA_KW), const), pl.BlockSpec((GC, GC), const), ANY],
        out_specs=[pl.BlockSpec((gb, 128), lambda i: (i, (W_GP - 128) // 128)), pl.BlockSpec((128, GLA_KW), const),
                   pl.BlockSpec((1, GLA_KW), const), pl.BlockSpec((D_MODEL, 128), const)],
        out_shape=[jax.ShapeDtypeStruct(d_g.shape, BF16), jax.ShapeDtypeStruct((128, GLA_KW), F32),
                   jax.ShapeDtypeStruct((1, GLA_KW), F32), jax.ShapeDtypeStruct((D_MODEL, 128), F32)],
        input_output_aliases={6: 0}, compiler_params=_cparams(1),
    )(db, z, glr, u, wg, pmat_t, d_g)


def _gla_row_steps(b_ref, cs, rows, size):
    parts = [jnp.zeros((size, GLA_K), F32) if r is None else jnp.broadcast_to(b_ref[r:r + 1, cs], (size, GLA_K))
             for r in rows]
    return parts[0] if len(parts) == 1 else jnp.concatenate(parts, axis=0)


def _gla_factors(b_ref, h, second):
    cs = slice(h * GLA_K, (h + 1) * GLA_K)
    b = b_ref[:, cs]
    fq, fk = [], []
    for l, m in enumerate(GLA_LEVELS):
        d = b - _gla_row_steps(b_ref, cs, [s + m - 1 for s in range(0, GC, 2 * m)], 2 * m)
        f = jnp.exp(jnp.where(second[l], d, -d))
        fq.append(jnp.where(second[l], f, 0.0))
        fk.append(jnp.where(second[l], 0.0, f))
    dd = b - _gla_row_steps(b_ref, cs, [None] + [s - 1 for s in range(GLA_SUB, GC, GLA_SUB)], GLA_SUB)
    ed = jnp.exp(dd)
    edi = jnp.exp(-dd)
    eb = jnp.exp(b)
    bl = b_ref[GC - 1:GC, cs]
    ee = jnp.exp(bl - b)
    ebl = jnp.exp(bl)
    return fq, fk, ed, edi, eb, ee, ebl


def _gla_scaled(q, k, fq, fk, ed, edi):
    qt = [(q * f).astype(BF16) for f in fq]
    kt = [(k * f).astype(BF16) for f in fk]
    return qt, kt, (q * ed).astype(BF16), (k * edi).astype(BF16)


def _gla_scores(qt, kt, qd, kd, masks, md):
    a = jnp.where(md, _dot(qd, kd, NT), 0.0)
    for l in range(NLEV):
        a = a + jnp.where(masks[l], _dot(qt[l], kt[l], NT), 0.0)
    return a.astype(BF16)


def _gla_fwd_call(gqk, gv, b, gg, gain, comm=None):
    tp = gqk.shape[0]
    nc = tp // GC
    ns = nc // GS
    n_xc = len(comm.srcs) if comm else 0

    def body(qk_ref, v_ref, b_ref, gg_ref, g_ref, *rest):
        xc_src = rest[:n_xc]
        o_ref, a_ref, st_ref, am_ref = rest[n_xc:n_xc + 4]
        xc_dst = rest[n_xc + 4:2 * n_xc + 4]
        s_scr = rest[2 * n_xc + 4]
        n = pl.program_id(0)
        if n_xc:
            begin, finish = comm.make(xc_src, xc_dst, rest[-2], rest[-1])
            pl.when(n == 0)(begin)
            pl.when(n == ns - 1)(finish)

        @pl.when(n == 0)
        def _():
            s_scr[...] = jnp.zeros_like(s_scr)

        masks, md, second = _gla_masks()
        for cc in range(GS):
            rows = pl.ds(cc * GC, GC)
            qk_c, v_c, b_c, gg_c, o_c, a_c = (r.at[rows] for r in (qk_ref, v_ref, b_ref, gg_ref, o_ref, a_ref))
            for h in range(GLA_HEADS):
                q = qk_c[:, h * GLA_K:(h + 1) * GLA_K]
                k = qk_c[:, GLA_KW + h * GLA_K:GLA_KW + (h + 1) * GLA_K]
                vs = slice(h * GLA_V, (h + 1) * GLA_V)
                v = v_c[:, vs]
                fq, fk, ed, edi, eb, ee, ebl = _gla_factors(b_c, h, second)
                a = _gla_scores(*_gla_scaled(q, k, fq, fk, ed, edi), masks, md)
                am_ref[cc, h] = a
                sb = s_scr[h].astype(BF16)
                st_ref[cc, h] = sb
                o = _dot(a, v, NN) + _dot((q * eb).astype(BF16), sb, NT)
                s_scr[h] = s_scr[h] * ebl + _dot(v, (k * ee).astype(BF16), TN)
                o_c[:, vs] = o
                xh = o * lax.rsqrt(_head_mean(o * o) + EPS)
                a_c[:, vs] = (xh * g_ref[:, vs] * _silu(gg_c[:, vs])).astype(BF16)

    return pl.pallas_call(
        body, name="gla_fwd", grid=(ns,),
        in_specs=[pl.BlockSpec((GS * GC, 2 * GLA_KW), lambda n: (n, 0)),
                  pl.BlockSpec((GS * GC, GLA_W), lambda n: (n, 0)),
                  pl.BlockSpec((GS * GC, GLA_KW), lambda n: (n, 0)),
                  pl.BlockSpec((GS * GC, GLA_W), lambda n: (n, 0)),
                  pl.BlockSpec((1, GLA_W), lambda n: (0, 0))] + [ANY] * n_xc,
        out_specs=[pl.BlockSpec((GS * GC, GLA_W), lambda n: (n, 0)),
                   pl.BlockSpec((GS * GC, GLA_W), lambda n: (n, 0)),
                   pl.BlockSpec((GS, GLA_HEADS, GLA_V, GLA_K), lambda n: (n, 0, 0, 0)),
                   pl.BlockSpec((GS, GLA_HEADS, GC, GC), lambda n: (n, 0, 0, 0))] + [ANY] * n_xc,
        out_shape=[jax.ShapeDtypeStruct((tp, GLA_W), F32), jax.ShapeDtypeStruct((tp, GLA_W), BF16),
                   jax.ShapeDtypeStruct((nc, GLA_HEADS, GLA_V, GLA_K), BF16),
                   jax.ShapeDtypeStruct((nc, GLA_HEADS, GC, GC), BF16)] + (list(comm.out_shapes) if comm else []),
        scratch_shapes=[pltpu.VMEM((GLA_HEADS, GLA_V, GLA_K), F32)] + (_comm_sems(comm) if comm else []),
        compiler_params=_cparams(1),
    )(gqk, gv, b, gg, gain, *(comm.srcs if comm else ()))


def _gla_bwd_call(gqk, gv, b, gg, o_gla, da, states, scores, gain, comm=None):
    tp = gqk.shape[0]
    nc = tp // GC
    ns = nc // GS
    o_gv, o_gg = 2 * GLA_KW, 2 * GLA_KW + GLA_W
    n_xc = len(comm.srcs) if comm else 0

    def body(qk_all, v_all, b_all, gg_all, o_all, da_all, st_ref, am_ref, g_ref, *rest):
        xc_src = rest[:n_xc]
        dp_all, db_all, dg_ref = rest[n_xc:n_xc + 3]
        xc_dst = rest[n_xc + 3:2 * n_xc + 3]
        ds_scr = rest[2 * n_xc + 3]
        n = pl.program_id(0)
        if n_xc:
            begin, finish = comm.make(xc_src, xc_dst, rest[-2], rest[-1])
            pl.when(n == 0)(begin)
            pl.when(n == ns - 1)(finish)

        @pl.when(n == 0)
        def _():
            ds_scr[...] = jnp.zeros_like(ds_scr)
            dg_ref[...] = jnp.zeros_like(dg_ref)

        masks, md, second = _gla_masks()
        for cc, h in [(cc, h) for cc in reversed(range(GS)) for h in range(GLA_HEADS)]:
            rows = pl.ds(cc * GC, GC)
            qk_ref, v_ref, b_scr, gg_ref, o_ref, da_ref, dp_ref, db_scr = (
                r.at[rows] for r in (qk_all, v_all, b_all, gg_all, o_all, da_all, dp_all, db_all))
            cs = slice(h * GLA_K, (h + 1) * GLA_K)
            vs = slice(h * GLA_V, (h + 1) * GLA_V)
            o = o_ref[:, vs]
            rstd = lax.rsqrt(_head_mean(o * o) + EPS)
            xh = o * rstd
            gain_h = g_ref[:, vs]
            g = gg_ref[:, vs]
            sg = _sigmoid(g)
            dah = da_ref[:, vs]
            dp_ref[:, o_gg + h * GLA_V:o_gg + (h + 1) * GLA_V] = (
                dah * (xh * gain_h) * (sg * (1.0 + g * (1.0 - sg)))).astype(BF16)
            dn = dah * (g * sg)
            dg_ref[:, vs] += jnp.sum(dn * xh, axis=0, keepdims=True)
            dxh = dn * gain_h
            do = rstd * (dxh - xh * _head_mean(dxh * xh))
            dob = do.astype(BF16)
            q = qk_ref[:, cs]
            k = qk_ref[:, GLA_KW + h * GLA_K:GLA_KW + (h + 1) * GLA_K]
            v = v_ref[:, vs]
            fq, fk, ed, edi, eb, ee, ebl = _gla_factors(b_scr, h, second)
            qt, kt, qd, kd = _gla_scaled(q, k, fq, fk, ed, edi)
            sp = st_ref[cc, h]
            ds = ds_scr[h]
            dsb = ds.astype(BF16)
            q_in = q * eb
            k_end = k * ee
            da_s = _dot(dob, v, NT)
            dv = _dot(am_ref[cc, h], dob, TN) + _dot(k_end.astype(BF16), dsb, NT)
            dq_in = _dot(dob, sp, NN)
            dk_end = _dot(v, dsb, NN)
            dbl = jnp.sum(sp.astype(F32) * ds, axis=0, keepdims=True) * ebl
            ds_scr[h] = ds * ebl + _dot(dob, q_in.astype(BF16), TN)
            dq = dq_in * eb
            dk = dk_end * ee
            de_end = dk_end * k_end
            db = dq_in * q_in - de_end
            placed = [(GC - 1, jnp.sum(de_end, axis=0, keepdims=True) + dbl)]
            for l, m in enumerate(GLA_LEVELS):
                dal = jnp.where(masks[l], da_s, 0.0).astype(BF16)
                dqt = _dot(dal, kt[l], NN)
                dkt = _dot(dal, qt[l], TN)
                dq = dq + dqt * fq[l]
                dk = dk + dkt * fk[l]
                gl = dqt * (q * fq[l]) - dkt * (k * fk[l])
                db = db + gl
                placed += [(s + m - 1, -jnp.sum(gl[s:s + 2 * m], axis=0, keepdims=True)) for s in range(0, GC, 2 * m)]
            dad = jnp.where(md, da_s, 0.0).astype(BF16)
            dqd = _dot(dad, kd, NN)
            dkd = _dot(dad, qd, TN)
            dq = dq + dqd * ed
            dk = dk + dkd * edi
            gd = dqd * (q * ed) - dkd * (k * edi)
            db = db + gd
            placed += [(s - 1, -jnp.sum(gd[s:s + GLA_SUB], axis=0, keepdims=True)) for s in range(GLA_SUB, GC, GLA_SUB)]
            db_scr[:, cs] = db
            for r, val in placed:
                db_scr[r:r + 1, cs] += val
            dp_ref[:, cs] = (dq * (GLA_K ** -0.5)).astype(BF16)
            dp_ref[:, GLA_KW + h * GLA_K:GLA_KW + (h + 1) * GLA_K] = dk.astype(BF16)
            dp_ref[:, o_gv + h * GLA_V:o_gv + (h + 1) * GLA_V] = dv.astype(BF16)

    rev = lambda n: (ns - 1 - n, 0)
    const = lambda n: (0, 0)
    xc_shapes, xc_sems = (list(comm.out_shapes), _comm_sems(comm)) if n_xc else ([], [])
    return pl.pallas_call(
        body, name="gla_bwd", grid=(ns,),
        in_specs=[pl.BlockSpec((GS * GC, 2 * GLA_KW), rev),
                  pl.BlockSpec((GS * GC, GLA_W), rev),
                  pl.BlockSpec((GS * GC, GLA_KW), rev),
                  pl.BlockSpec((GS * GC, GLA_W), rev),
                  pl.BlockSpec((GS * GC, GLA_W), rev),
                  pl.BlockSpec((GS * GC, GLA_W), rev),
                  pl.BlockSpec((GS, GLA_HEADS, GLA_V, GLA_K), lambda n: (ns - 1 - n, 0, 0, 0)),
                  pl.BlockSpec((GS, GLA_HEADS, GC, GC), lambda n: (ns - 1 - n, 0, 0, 0)),
                  pl.BlockSpec((1, GLA_W), const)] + [ANY] * n_xc,
        out_specs=[pl.BlockSpec((GS * GC, W_GP), rev), pl.BlockSpec((GS * GC, GLA_KW), rev),
                   pl.BlockSpec((1, GLA_W), const)] + [ANY] * n_xc,
        out_shape=[jax.ShapeDtypeStruct((tp, W_GP), BF16), jax.ShapeDtypeStruct((tp, GLA_KW), F32),
                   jax.ShapeDtypeStruct((1, GLA_W), F32)] + xc_shapes,
        scratch_shapes=[pltpu.VMEM((GLA_HEADS, GLA_V, GLA_K), F32)] + xc_sems,
        compiler_params=_cparams(1),
    )(gqk, gv, b, gg, o_gla, da, states, scores, gain, *(comm.srcs if comm else ()))


def _mid_call(a_ret, a_gla, mg, h0, tgt, wbr, wbg, wout, gf):
    tp = h0.shape[0]
    nt = tp // TM

    def body(ar_ref, ag_ref, mg_ref, h_ref, t_ref, wbr_ref, wbg_ref, wo_ref, gf_ref,
             dh1_ref, dag_ref, dm_ref, mb_ref, dh1b_ref, dprb_ref, dpgb_ref, loss_ref, dgf_ref):
        i = pl.program_id(0)

        @pl.when(i == 0)
        def _():
            loss_ref[...] = jnp.zeros_like(loss_ref)
            dgf_ref[...] = jnp.zeros_like(dgf_ref)

        ar, ag = ar_ref[...], ag_ref[...]
        pr = _dot(ar, wbr_ref[...], NN)
        pg = _dot(ag, wbg_ref[...], NN)
        sr = _sigmoid(mg_ref[:, :D_MODEL])
        sg = _sigmoid(mg_ref[:, D_MODEL:])
        merged = (sr * pr + sg * pg).astype(BF16)
        mb_ref[...] = merged
        h1 = h_ref[...] + _dot(merged, wo_ref[...], NN)
        r1 = lax.rsqrt(jnp.mean(h1 * h1, axis=-1, keepdims=True) + EPS)
        xh = h1 * r1
        gfv = gf_ref[...]
        live = jnp.where(i > 0, 1.0, 0.0).astype(F32)
        err = (xh * gfv - t_ref[...]) * live
        loss_ref[...] += jnp.full(loss_ref.shape, 0.5 / D_MODEL, F32) * jnp.sum(err * err)
        dy = err * (1.0 / D_MODEL)
        dgf_ref[...] += jnp.sum(dy * xh, axis=0, keepdims=True)
        dxh = dy * gfv
        dh1 = r1 * (dxh - xh * jnp.mean(dxh * xh, axis=-1, keepdims=True))
        dh1_ref[...] = dh1
        dh1b = dh1.astype(BF16)
        dh1b_ref[...] = dh1b
        dmerged = _dot(dh1b, wo_ref[...], NT)
        dm_ref[:, :D_MODEL] = (dmerged * pr * sr * (1.0 - sr)).astype(BF16)
        dm_ref[:, D_MODEL:] = (dmerged * pg * sg * (1.0 - sg)).astype(BF16)
        dpr = (dmerged * sr).astype(BF16)
        dpg = (dmerged * sg).astype(BF16)
        dprb_ref[...] = dpr
        dpgb_ref[...] = dpg
        dag_ref[...] = _dot(dpg, wbg_ref[...], NT)

    tile = lambda w: pl.BlockSpec((TM, w), lambda i: (i, 0))
    const = lambda r, w: pl.BlockSpec((r, w), lambda i: (0, 0))
    return pl.pallas_call(
        body, name="merge_out_loss", grid=(nt,),
        in_specs=[tile(RET_W), tile(GLA_W), tile(W_M), tile(D_MODEL),
                  pl.BlockSpec((TM, D_MODEL), lambda i: (jnp.maximum(i - 1, 0), 0)),
                  const(RET_W, D_MODEL), const(GLA_W, D_MODEL), const(D_MODEL, D_MODEL), const(1, D_MODEL)],
        out_specs=[tile(D_MODEL), tile(GLA_W), tile(W_M), tile(D_MODEL), tile(D_MODEL), tile(D_MODEL),
                   tile(D_MODEL), const(1, 128), const(1, D_MODEL)],
        out_shape=[jax.ShapeDtypeStruct((tp, D_MODEL), F32), jax.ShapeDtypeStruct((tp, GLA_W), F32),
                   jax.ShapeDtypeStruct((tp, W_M), BF16),
                   jax.ShapeDtypeStruct((tp, D_MODEL), BF16), jax.ShapeDtypeStruct((tp, D_MODEL), BF16),
                   jax.ShapeDtypeStruct((tp, D_MODEL), BF16), jax.ShapeDtypeStruct((tp, D_MODEL), BF16),
                   jax.ShapeDtypeStruct((1, 128), F32), jax.ShapeDtypeStruct((1, D_MODEL), F32)],
        compiler_params=_cparams(1),
    )(a_ret, a_gla, mg, h0, tgt, wbr, wbg, wout, gf)


def _device_step(x2d, tgt2d, meta, norm_gain, w_in_part, w_gate_up, b_gate, ret_gain, gla_gain, branch_parts,
                 final_gain, ck):
    seq = x2d.shape[0]
    tp = T0 + seq
    head = jnp.concatenate([jnp.zeros((PADF, D_MODEL), F32), meta], axis=0)
    wg_pad = jnp.pad(w_gate_up, ((0, 128 - GATE_RANK), (0, 0))).astype(BF16)

    half = RET_QK // 2
    cos, sin = (jnp.asarray(t) for t in _rope_tables(tp))
    lgam = jnp.log1p(-(2.0 ** (-5.0 - jnp.arange(RET_HEADS, dtype=F32))))
    pmat = jnp.asarray(_gla_tril(), BF16)
    pmat_t = jnp.asarray(_gla_tril().T.copy(), BF16)

    h0, u, g_in = _rms_call(x2d, head, norm_gain, _gather_plan([w_in_part], relay=(True,)))
    hr, sw = w_in_part.shape
    w_in_bf = g_in.reshape(4, 2, hr, sw).transpose(1, 2, 0, 3).reshape(2 * hr, 4 * sw)
    w_r = w_in_bf
    w_g = jnp.pad(w_in_bf[:, W_R:W_R + W_G], ((0, 0), (0, W_GP - W_G)))
    w_m = w_in_bf[:, W_R + W_G:]
    tab = pl.BlockSpec((_proj_rows(tp), half), lambda j, i: (i, 0))
    rqk = _mm_nn("proj_rqk", u, w_r, BF16, D_MODEL, 0, 2 * D_MODEL, _rope_epilogue, (cos, sin), (tab, tab))
    rv = _mm_nn("proj_rv", u, w_r, BF16, RET_W, 2 * D_MODEL, RET_W)
    rg = _mm_nn("proj_rg", u, w_r, F32, RET_W, 4 * D_MODEL, RET_W)
    gqk = _mm_nn("proj_gqk", u, w_g, F32, 2 * GLA_KW, 0, 2 * GLA_KW, _gqk_epilogue)
    gv = _mm_nn("proj_gv", u, w_g, BF16, GLA_W, 2 * GLA_KW, GLA_W)
    gg = _mm_nn("proj_gg", u, w_g, F32, GLA_W, 2 * GLA_KW + GLA_W, GLA_W)
    mg = _mm_nn("proj_mg", u, w_m, F32, W_M, 0, W_M)

    o_ret, a_ret, st_ret, sc_ret = _ret_fwd_call(rqk, rv, rg, ret_gain, lgam)
    glr, z_gate, b_dec = _gla_gate_call(u, w_g, wg_pad, b_gate, pmat)
    o_gla, a_gla, st_gla, sc_gla, g_br, g_bg, g_out = _gla_fwd_call(gqk, gv, b_dec, gg, gla_gain,
                                                                    comm=_spread_plan(branch_parts))
    wbr = g_br.reshape(RET_W, D_MODEL)
    wbg = g_bg.reshape(GLA_W, D_MODEL)
    wout = g_out.reshape(D_MODEL, D_MODEL)

    gf = final_gain.reshape(1, D_MODEL)
    (dh1, da_gla, dm, merged_b, dh1_b, dpr_b, dpg_b, loss, dgf) = _mid_call(
        a_ret, a_gla, mg, h0, tgt2d, wbr, wbg, wout, gf)

    names_b = ("w_branch_ret", "w_branch_gla", "w_out")
    g2_b = [_mm_tn("dw_br", a_ret, dpr_b, D_MODEL).reshape(4, 2, RET_W // 8, D_MODEL).transpose(1, 0, 2, 3),
            _mm_tn("dw_bg", a_gla, dpg_b, D_MODEL).reshape(4, 2, GLA_W // 8, D_MODEL).transpose(1, 0, 2, 3),
            _mm_tn("dw_out", merged_b, dh1_b, D_MODEL).reshape(4, 2, D_MODEL // 8, D_MODEL).transpose(1, 0, 2, 3)]
    sib_b = _swap_halves_call("swap_halves_branch", g2_b)
    sum_b = [_add_half_call("add_half_" + nm, g, b, ck) for nm, g, b in zip(names_b, g2_b, sib_b)]
    d_g, db_dec, dgla_gain, *chips_b = _gla_bwd_call(gqk, gv, b_dec, gg, o_gla, da_gla, st_gla, sc_gla, gla_gain,
                                                     comm=_exchange_plan(sum_b))
    d_g, dwg, dbg, dw_glr = _gla_gate_bwd_call(db_dec, z_gate, glr, u, wg_pad, pmat_t, d_g)
    mine = [_add_chips_call("add_chips_" + nm, g, b, p, ck) for nm, g, b, p in zip(names_b, g2_b, sib_b, chips_b)]

    d_r, dret_gain = _ret_bwd_call(rqk, rv, rg, o_ret, dpr_b, wbr, st_ret, sc_ret, ret_gain, lgam, cos, sin)

    dwp = _mm_tn("dw_r", u, d_r, 3 * D_MODEL, out_cols=IN_PAD)
    dwp = _mm_tn("dw_g", u, d_g, 3 * D_MODEL, ncols=W_GP - 128, into=dwp, col0=W_R)
    g2_in = _place_merge_cols_call(dwp, _mm_tn("dw_m", u, dm, 2 * D_MODEL), dw_glr).reshape(2, D_MODEL // 2, IN_PAD)

    du, sib_in = _mm_nt_acc("du_g", d_g, w_g, W_GP, comm=_swap_plan([g2_in]), tb=_proj_rows(tp))
    sum_in = _add_rows_call("add_half_w_in", g2_in, sib_in, ck)
    du, chips_in = _mm_nt_acc("du_r", d_r, w_r, 3 * D_MODEL, acc_in=du, comm=_exchange_window_plan(sum_in))
    tile = pl.BlockSpec((TB, D_MODEL), lambda i, kk: (i, 0))
    row = pl.BlockSpec((1, D_MODEL), lambda i, kk: (0, 0))
    dx, dmeta, dnorm_gain = _mm_nt_acc(
        "du_m", dm, w_m, W_M, acc_in=du, epilogue=_rms_bwd_epilogue, extras=(h0, norm_gain, dh1),
        extra_specs=(tile, row, tile),
        extra_out_shapes=(jax.ShapeDtypeStruct((seq, D_MODEL), F32), jax.ShapeDtypeStruct((N_META, D_MODEL), F32),
                          jax.ShapeDtypeStruct((1, D_MODEL), F32)),
        extra_out_specs=(ANY, pl.BlockSpec((N_META, D_MODEL), lambda i, kk: (0, 0)), row),
        extra_scratch=(pltpu.VMEM((2, TB, D_MODEL), F32), pltpu.SemaphoreType.DMA((2,))))
    small = dict(norm_gain=dnorm_gain, b_gate=dbg, ret_norm_gain=dret_gain, gla_norm_gain=dgla_gain,
                 final_norm_gain=dgf, w_gate_up=dwg[:GATE_RANK], meta_tokens=dmeta, loss=loss[0, 0])
    rows = -(-sum(sz for _, sz in SMALL) // 128 // 8) * 8
    mine_in, g_small = _add_window_call("add_chips_w_in", g2_in, sib_in, chips_in, ck,
                                        _gather_plan([_pack_rows([small[nm] for nm, _ in SMALL], rows)]))
    full = _join_halves_call("join_halves", [mine_in] + mine)

    return dict(dx=dx, small=g_small, w_in=full[0], w_branch_ret=full[1], w_branch_gla=full[2], w_out=full[3])


MESH = pl.DeviceIdType.MESH
ANY = pl.BlockSpec(memory_space=pl.ANY)


def _place():
    return lax.axis_index("x"), lax.axis_index("y"), lax.axis_index("c")


def _gather8_call(name, parts):
    comm = _gather_plan(parts)
    n = len(parts)

    def body(*refs):
        begin, finish = comm.make(refs[:n], refs[n:2 * n], refs[-2], refs[-1])
        begin()
        finish()

    return pl.pallas_call(
        body, name=name, out_shape=list(comm.out_shapes), in_specs=[ANY] * n, out_specs=[ANY] * n,
        scratch_shapes=_comm_sems(comm),
    )(*parts)


def _swap_halves_call(name, gs):
    n = len(gs)

    def body(*refs):
        g_refs, b_refs = refs[:n], refs[n:2 * n]
        send_sems, recv_sems = refs[2 * n:]
        x, y, c = _place()
        copies = [pltpu.make_async_remote_copy(
            src_ref=g_refs[t].at[1 - c], dst_ref=b_refs[t], send_sem=send_sems.at[t], recv_sem=recv_sems.at[t],
            device_id=(x, y, 1 - c), device_id_type=MESH) for t in range(n)]
        for cp in copies:
            cp.start()
        for cp in copies:
            cp.wait()

    return pl.pallas_call(
        body, name=name,
        out_shape=[jax.ShapeDtypeStruct(g.shape[1:], g.dtype) for g in gs],
        in_specs=[ANY] * n, out_specs=[ANY] * n,
        scratch_shapes=[pltpu.SemaphoreType.DMA((n,)), pltpu.SemaphoreType.DMA((n,))],
    )(*gs)


def _join_halves_call(name, ts):
    n = len(ts)

    def body(*refs):
        o_refs = refs[n:2 * n]
        send_sems, recv_sems = refs[2 * n:]
        x, y, c = _place()
        copies = [pltpu.make_async_remote_copy(
            src_ref=o_refs[t].at[c], dst_ref=o_refs[t].at[c], send_sem=send_sems.at[t], recv_sem=recv_sems.at[t],
            device_id=(x, y, 1 - c), device_id_type=MESH) for t in range(n)]
        for cp in copies:
            cp.start()
        for t in range(n):
            copies[t].wait_send()
            pltpu.make_async_remote_copy(
                src_ref=o_refs[t].at[c], dst_ref=o_refs[t].at[1 - c], send_sem=send_sems.at[t],
                recv_sem=recv_sems.at[t], device_id=(x, y, 1 - c), device_id_type=MESH).wait_recv()

    return pl.pallas_call(
        body, name=name,
        out_shape=[jax.ShapeDtypeStruct(t.shape, t.dtype) for t in ts],
        in_specs=[ANY] * n, out_specs=[ANY] * n, input_output_aliases={t: t for t in range(n)},
        scratch_shapes=[pltpu.SemaphoreType.DMA((n,)), pltpu.SemaphoreType.DMA((n,))],
    )(*ts)


def _row_block(rows, cols, budget):
    best = 8
    for rb in range(8, rows + 1, 8):
        if rows % rb == 0 and rb * cols * 4 <= budget:
            best = rb
    return best


def _add_half_call(name, g, b, ck):
    _, _, r, cc = g.shape
    rb = _row_block(r, cc, 2 * 1024 * 1024)

    def body(ck_ref, g_ref, b_ref, o_ref):
        o_ref[...] = (g_ref[...] + b_ref[...]).astype(BF16)

    return pl.pallas_call(
        body, name=name,
        grid_spec=pltpu.PrefetchScalarGridSpec(
            num_scalar_prefetch=1, grid=(4, r // rb),
            in_specs=[pl.BlockSpec((None, None, rb, cc), lambda k, i, ck_ref: (ck_ref[0], k, i, 0)),
                      pl.BlockSpec((None, rb, cc), lambda k, i, ck_ref: (k, i, 0))],
            out_specs=pl.BlockSpec((None, rb, cc), lambda k, i, ck_ref: (k, i, 0))),
        out_shape=jax.ShapeDtypeStruct(b.shape, BF16),
        compiler_params=_cparams(2),
    )(ck, g, b)


def _add_rows_call(name, g, b, ck):
    _, r, cc = g.shape
    rb = _row_block(r, cc, 2 * 1024 * 1024)

    def body(ck_ref, g_ref, b_ref, o_ref):
        o_ref[...] = (g_ref[...] + b_ref[...]).astype(BF16)

    return pl.pallas_call(
        body, name=name,
        grid_spec=pltpu.PrefetchScalarGridSpec(
            num_scalar_prefetch=1, grid=(r // rb,),
            in_specs=[pl.BlockSpec((None, rb, cc), lambda i, ck_ref: (ck_ref[0], i, 0)),
                      pl.BlockSpec((rb, cc), lambda i, ck_ref: (i, 0))],
            out_specs=pl.BlockSpec((rb, cc), lambda i, ck_ref: (i, 0))),
        out_shape=jax.ShapeDtypeStruct((r, cc), BF16),
        compiler_params=_cparams(1),
    )(ck, g, b)


def _add_window_call(name, g, b, p, ck, comm):
    _, r, _ = g.shape
    nb, step = WIN_W // 128, WIN_STEP // 128
    n_xc = len(comm.srcs)

    def body(ck_ref, g_ref, b_ref, p0_ref, p1_ref, p2_ref, *rest):
        o_ref = rest[n_xc]
        i = pl.program_id(0)
        begin, finish = comm.make(rest[:n_xc], rest[n_xc + 1:2 * n_xc + 1], rest[-2], rest[-1])
        pl.when(i == 0)(begin)
        own = g_ref[...] + b_ref[...]
        o_ref[...] = ((own + p0_ref[...].astype(F32)) + p1_ref[...].astype(F32)) + p2_ref[...].astype(F32)
        pl.when(i == nb - 1)(finish)

    def peer(j):
        return pl.BlockSpec((None, r, 128), lambda i, ck_ref: (j, 0, i))

    return pl.pallas_call(
        body, name=name,
        grid_spec=pltpu.PrefetchScalarGridSpec(
            num_scalar_prefetch=1, grid=(nb,),
            in_specs=[pl.BlockSpec((None, r, 128), lambda i, ck_ref: (ck_ref[0], 0, step * ck_ref[1] + i)),
                      pl.BlockSpec((r, 128), lambda i, ck_ref: (0, step * ck_ref[1] + i)),
                      peer(0), peer(1), peer(2)] + [ANY] * n_xc,
            out_specs=[pl.BlockSpec((None, r, 128), lambda i, ck_ref: (ck_ref[0], 0, i))] + [ANY] * n_xc,
            scratch_shapes=_comm_sems(comm)),
        out_shape=[jax.ShapeDtypeStruct((2, r, WIN_W), F32)] + list(comm.out_shapes),
        compiler_params=_cparams(1),
    )(ck, g, b, p, p, p, *comm.srcs)


def _add_chips_call(name, g, b, p, ck):
    _, _, r, cc = g.shape
    rb = _row_block(r, cc, 2 * 1024 * 1024)

    def body(ck_ref, g_ref, b_ref, p0_ref, p1_ref, p2_ref, o_ref):
        own = g_ref[...] + b_ref[...]
        o_ref[...] = ((own + p0_ref[...].astype(F32)) + p1_ref[...].astype(F32)) + p2_ref[...].astype(F32)

    def peer(j):
        return pl.BlockSpec((None, rb, cc), lambda i, ck_ref: (j, i, 0))

    return pl.pallas_call(
        body, name=name,
        grid_spec=pltpu.PrefetchScalarGridSpec(
            num_scalar_prefetch=1, grid=(r // rb,),
            in_specs=[pl.BlockSpec((None, None, rb, cc), lambda i, ck_ref: (ck_ref[0], ck_ref[1], i, 0)),
                      pl.BlockSpec((None, rb, cc), lambda i, ck_ref: (ck_ref[1], i, 0)),
                      peer(0), peer(1), peer(2)],
            out_specs=pl.BlockSpec((None, rb, cc), lambda i, ck_ref: (ck_ref[0], i, 0))),
        out_shape=jax.ShapeDtypeStruct((2, r, cc), F32),
        compiler_params=_cparams(1),
    )(ck, g, b, p, p, p)


def _sum8_call(name, g):
    def body(g_ref, o_ref):
        acc = g_ref[0]
        for d in range(1, 8):
            acc = acc + g_ref[d]
        o_ref[...] = acc

    return pl.pallas_call(body, name=name, out_shape=jax.ShapeDtypeStruct(g.shape[1:], F32))(g)


def _adamw_call(name, w, g, m, v):
    r, cc = w.shape
    if r % 8 == 0 or r * cc * 4 <= 1024 * 1024:
        rb = _row_block(r, cc, 1024 * 1024) if r % 8 == 0 else r
        grid, spec = (r // rb,), pl.BlockSpec((rb, cc), lambda i: (i, 0))
    else:
        grid, spec = (cc // 128,), pl.BlockSpec((r, 128), lambda i: (0, i))

    def body(w_ref, g_ref, m_ref, v_ref, d_ref, m2_ref, v2_ref):
        gv = g_ref[...]
        m2 = ADAM_B1 * m_ref[...] + (1.0 - ADAM_B1) * gv
        v2 = ADAM_B2 * v_ref[...] + (1.0 - ADAM_B2) * (gv * gv)
        m_hat = m2 / (1.0 - ADAM_B1 ** ADAM_STEP)
        v_hat = v2 / (1.0 - ADAM_B2 ** ADAM_STEP)
        d_ref[...] = -ADAM_LR * (m_hat / (jnp.sqrt(v_hat) + ADAM_EPS) + ADAM_WD * w_ref[...])
        m2_ref[...] = m2
        v2_ref[...] = v2

    return pl.pallas_call(
        body, name=name, grid=grid, in_specs=[spec] * 4, out_specs=[spec] * 3,
        out_shape=[jax.ShapeDtypeStruct((r, cc), F32)] * 3, compiler_params=_cparams(1),
    )(w, g, m, v)


SMALL = (("norm_gain", D_MODEL), ("b_gate", GLA_KW), ("ret_norm_gain", RET_W), ("gla_norm_gain", GLA_W),
         ("final_norm_gain", D_MODEL), ("w_gate_up", GATE_RANK * GLA_KW), ("meta_tokens", N_META * D_MODEL),
         ("loss", 1))


def _pack_rows(vecs, rows):
    flat = jnp.concatenate([v.reshape(-1) for v in vecs])
    return jnp.pad(flat, (0, rows * 128 - flat.shape[0])).reshape(rows, 128)


def kernel(x, meta_tokens, norm_gain, w_in, w_gate_up, b_gate, ret_norm_gain, gla_norm_gain, w_branch_ret, w_branch_gla, w_out, final_norm_gain, loss_target, m_meta_tokens, m_norm_gain, m_w_in, m_w_gate_up, m_b_gate, m_ret_norm_gain, m_gla_norm_gain, m_w_branch_ret, m_w_branch_gla, m_w_out, m_final_norm_gain, v_meta_tokens, v_norm_gain, v_w_in, v_w_gate_up, v_b_gate, v_ret_norm_gain, v_gla_norm_gain, v_w_branch_ret, v_w_branch_gla, v_w_out, v_final_norm_gain):
    xi, yi, ci = _place()
    kme = 2 * xi + yi
    ck = jnp.stack([ci, kme]).astype(jnp.int32)
    sw_in = w_in.shape[2]

    def my_half(a, dtype):
        r, cc = a.shape
        return lax.dynamic_index_in_dim(a.reshape(2, r // 2, cc), ci, 0, keepdims=False).astype(dtype)

    g_meta, g_wg = _gather8_call("gather_small_weights", [my_half(meta_tokens, F32), my_half(w_gate_up[0], F32)])
    branch_parts = [my_half(w_branch_ret[0], BF16), my_half(w_branch_gla[0], BF16), my_half(w_out[0], BF16)]
    meta = g_meta.reshape(4, 2, N_META // 2, D_MODEL // 4).transpose(1, 2, 0, 3).reshape(N_META, D_MODEL)
    wg_full = g_wg.reshape(4, 2, GATE_RANK // 2, GLA_KW // 4).transpose(1, 2, 0, 3).reshape(GATE_RANK, GLA_KW)

    loc = _device_step(x[0], loss_target[0], meta, norm_gain, my_half(w_in[0], BF16), wg_full, b_gate, ret_norm_gain,
                       gla_norm_gain,
                       branch_parts, final_norm_gain, ck)
    names = ("w_in", "w_branch_ret", "w_branch_gla", "w_out")
    full = [loc[nm] for nm in names]
    big_w = dict(w_in=w_in[0], w_branch_ret=w_branch_ret[0], w_branch_gla=w_branch_gla[0], w_out=w_out[0])
    big_m = dict(w_in=m_w_in[0], w_branch_ret=m_w_branch_ret[0], w_branch_gla=m_w_branch_gla[0], w_out=m_w_out[0])
    big_v = dict(w_in=v_w_in[0], w_branch_ret=v_w_branch_ret[0], w_branch_gla=v_w_branch_gla[0], w_out=v_w_out[0])
    grads, deltas, new_m, new_v = {}, {}, {}, {}
    for nm, f in zip(names, full):
        shape = big_w[nm].shape
        if nm == "w_in":
            f = lax.dynamic_slice_in_dim(f, (sw_in - WIN_STEP) * kme, sw_in, axis=2)
        g = f.reshape(shape)
        if nm == "w_in":
            d, m2, v2 = (a.T for a in _adamw_call("adamw_" + nm, big_w[nm].T, g.T, big_m[nm].T, big_v[nm].T))
        else:
            d, m2, v2 = _adamw_call("adamw_" + nm, big_w[nm], g, big_m[nm], big_v[nm])
        grads[nm], deltas[nm], new_m[nm], new_v[nm] = (a.reshape((1,) + shape) for a in (g, d, m2, v2))

    tot = _sum8_call("sum_small_grads", loc["small"]).reshape(-1)
    off = 0
    sg = {}
    for nm, sz in SMALL:
        sg[nm] = tot[off:off + sz]
        off += sz
    loss = sg.pop("loss")[0]
    sg["w_gate_up"] = lax.dynamic_slice_in_dim(sg["w_gate_up"].reshape(GATE_RANK, GLA_KW), kme * (GLA_KW // 4),
                                               GLA_KW // 4, axis=1)
    sg["meta_tokens"] = lax.dynamic_slice_in_dim(sg["meta_tokens"].reshape(N_META, D_MODEL), kme * (D_MODEL // 4),
                                                 D_MODEL // 4, axis=1)
    small_w = dict(norm_gain=norm_gain, b_gate=b_gate, ret_norm_gain=ret_norm_gain, gla_norm_gain=gla_norm_gain,
                   final_norm_gain=final_norm_gain, w_gate_up=w_gate_up, meta_tokens=meta_tokens)
    small_m = dict(norm_gain=m_norm_gain, b_gate=m_b_gate, ret_norm_gain=m_ret_norm_gain,
                   gla_norm_gain=m_gla_norm_gain, final_norm_gain=m_final_norm_gain, w_gate_up=m_w_gate_up,
                   meta_tokens=m_meta_tokens)
    small_v = dict(norm_gain=v_norm_gain, b_gate=v_b_gate, ret_norm_gain=v_ret_norm_gain,
                   gla_norm_gain=v_gla_norm_gain, final_norm_gain=v_final_norm_gain, w_gate_up=v_w_gate_up,
                   meta_tokens=v_meta_tokens)
    for nm in small_w:
        shape = small_w[nm].shape
        as2d = lambda a: a.reshape((-1, shape[-1]))
        grads[nm] = sg[nm].reshape(shape)
        deltas[nm], new_m[nm], new_v[nm] = (a.reshape(shape) for a in _adamw_call(
            "adamw_" + nm, as2d(small_w[nm]), as2d(sg[nm]), as2d(small_m[nm]), as2d(small_v[nm])))

    out_order = ("meta_tokens", "norm_gain", "w_in", "w_gate_up", "b_gate", "ret_norm_gain", "gla_norm_gain",
                 "w_branch_ret", "w_branch_gla", "w_out", "final_norm_gain")
    dx = loc["dx"].reshape(x.shape)
    return (loss, dx, *[grads[nm] for nm in out_order], *[deltas[nm] for nm in out_order],
            *[new_m[nm] for nm in out_order], *[new_v[nm] for nm in out_order])
```

```python
import math
from typing import Callable, NamedTuple

import numpy as np
import jax
import jax.numpy as jnp
from jax import lax
from jax.experimental import pallas as pl
from jax.experimental.pallas import tpu as pltpu

F32 = jnp.float32
BF16 = jnp.bfloat16

D_MODEL = 1024
N_META = 16
EPS = 1e-6
ROPE_BASE = 10000.0
RET_HEADS, RET_QK, RET_V = 4, 256, 512
RET_W = RET_HEADS * RET_V
GLA_HEADS, GLA_K, GLA_V = 4, 128, 256
GLA_W = GLA_HEADS * GLA_V
GLA_KW = GLA_HEADS * GLA_K
GATE_RANK = 16
GATE_TAU = 16.0
GLA_SUB = 16

TM = 256
T0 = TM
PADF = T0 - N_META
GC = 128
GS = 3
TB = 768
TK = 768

W_R = 6144
W_G = 3088
W_GP = 3200
W_M = 2048
IN_COLS = W_R + W_G + W_M
WIN_STEP = (IN_COLS // 4) // 128 * 128
WIN_W = -(-(3 * (IN_COLS // 4 - WIN_STEP) + IN_COLS // 4) // 128) * 128
IN_PAD = 3 * WIN_STEP + WIN_W

ADAM_LR, ADAM_B1, ADAM_B2, ADAM_EPS, ADAM_WD, ADAM_STEP = 0.001, 0.9, 0.999, 1e-08, 0.01, 10

VMEM_LIMIT = 56 * 1024 * 1024

NN = ((1,), (0,))
NT = ((1,), (1,))
TN = ((0,), (0,))


def _dot(a, b, dims):
    return lax.dot_general(a, b, (dims, ((), ())), preferred_element_type=F32)


def _cparams(n_axes):
    return pltpu.CompilerParams(dimension_semantics=("arbitrary",) * n_axes, vmem_limit_bytes=VMEM_LIMIT)


def _sigmoid(x):
    return 0.5 * jnp.tanh(0.5 * x) + 0.5


def _silu(x):
    h = 0.5 * x
    return h + h * jnp.tanh(h)


def _head_mean(x):
    return jnp.mean(x, axis=-1, keepdims=True)


def _split3(x):
    hi = x.astype(BF16)
    r1 = x - hi.astype(F32)
    mid = r1.astype(BF16)
    lo = (r1 - mid.astype(F32)).astype(BF16)
    return hi, mid, lo


def _exact_pm(p, x):
    hi, mid, lo = _split3(x)
    return _dot(p, hi, NN) + _dot(p, mid, NN) + _dot(p, lo, NN)


def _rms_call(x2d, head, gain, comm):
    tp = T0 + x2d.shape[0]
    nt = tp // TM
    n_xc = len(comm.srcs)

    def body(x_ref, hd_ref, g_ref, *rest):
        xc_src = rest[:n_xc]
        h_ref, u_ref = rest[n_xc:n_xc + 2]
        xc_dst = rest[n_xc + 2:2 * n_xc + 2]
        i = pl.program_id(0)
        begin, finish = comm.make(xc_src, xc_dst, rest[-2], rest[-1])
        pl.when(i == 0)(begin)
        h = jnp.where(i == 0, hd_ref[...], x_ref[...])
        h_ref[...] = h
        r = lax.rsqrt(jnp.mean(h * h, axis=-1, keepdims=True) + EPS)
        u_ref[...] = (h * r * g_ref[...]).astype(BF16)
        pl.when(i == nt - 1)(finish)

    tile = pl.BlockSpec((TM, D_MODEL), lambda i: (i, 0))
    return pl.pallas_call(
        body, name="rms_in", grid=(nt,),
        in_specs=[pl.BlockSpec((TM, D_MODEL), lambda i: (jnp.maximum(i - 1, 0), 0)),
                  pl.BlockSpec((T0, D_MODEL), lambda i: (0, 0)), pl.BlockSpec((1, D_MODEL), lambda i: (0, 0))]
        + [ANY] * n_xc,
        out_specs=[tile, tile] + [ANY] * n_xc,
        out_shape=[jax.ShapeDtypeStruct((tp, D_MODEL), F32), jax.ShapeDtypeStruct((tp, D_MODEL), BF16)]
        + list(comm.out_shapes),
        scratch_shapes=_comm_sems(comm), compiler_params=_cparams(1),
    )(x2d, head, gain, *comm.srcs)


PROJ_ROWS_MAX = 1408


def _proj_rows(m):
    return max(r for r in range(16, PROJ_ROWS_MAX + 1, 16) if m % r == 0)


def _mm_nn(name, a, bt, out_dtype, tn, col0, ncols, epilogue=None, extras=(), extra_specs=()):
    m, k = a.shape
    nj, j0 = ncols // tn, col0 // tn
    tb = _proj_rows(m)

    def body(a_ref, b_ref, *rest):
        *ex, o_ref = rest
        acc = _dot(a_ref[...], b_ref[...], NT)
        if epilogue is None:
            o_ref[...] = acc.astype(out_dtype)
        else:
            epilogue(acc, o_ref, *ex)

    return pl.pallas_call(
        body, name=name, grid=(nj, m // tb),
        in_specs=[pl.BlockSpec((tb, k), lambda j, i: (i, 0)), pl.BlockSpec((tn, k), lambda j, i: (j0 + j, 0))]
        + list(extra_specs),
        out_specs=pl.BlockSpec((tb, tn), lambda j, i: (i, j)),
        out_shape=jax.ShapeDtypeStruct((m, ncols), out_dtype),
        compiler_params=_cparams(2),
    )(a, bt, *extras)


def _rope_tables(tp):
    half = RET_QK // 2
    pos = np.arange(tp, dtype=np.float32) - np.float32(PADF)
    inv = (ROPE_BASE ** (-np.arange(half, dtype=np.float64) / half)).astype(np.float32)
    ang = (pos[:, None] * inv[None, :]).astype(np.float64)
    return np.cos(ang).astype(np.float32), np.sin(ang).astype(np.float32)


def _rope_epilogue(acc, o_ref, cos_ref, sin_ref):
    scale = jnp.where(pl.program_id(0) == 1, RET_QK ** -0.5, 1.0).astype(F32)
    cos, sin = cos_ref[...], sin_ref[...]
    half = RET_QK // 2
    for h in range(RET_HEADS):
        t1 = acc[:, h * RET_QK:h * RET_QK + half]
        t2 = acc[:, h * RET_QK + half:(h + 1) * RET_QK]
        o_ref[:, h * RET_QK:h * RET_QK + half] = ((t1 * cos - t2 * sin) * scale).astype(BF16)
        o_ref[:, h * RET_QK + half:(h + 1) * RET_QK] = ((t2 * cos + t1 * sin) * scale).astype(BF16)


def _gqk_epilogue(acc, o_ref):
    o_ref[:, :GLA_KW] = acc[:, :GLA_KW] * (GLA_K ** -0.5)
    o_ref[:, GLA_KW:] = acc[:, GLA_KW:]


class _Comm(NamedTuple):
    srcs: tuple
    out_shapes: tuple
    n_sems: int
    make: Callable


def _comm_sems(comm):
    return [pltpu.SemaphoreType.DMA((comm.n_sems,)), pltpu.SemaphoreType.DMA((comm.n_sems,))]


def _start_wait(copies):
    def begin():
        for cp in copies:
            cp.start()

    def finish():
        for cp in copies:
            cp.wait()

    return begin, finish


def _other_chips(x, y):
    return [(1 - x, y), (x, 1 - y), (1 - x, 1 - y)]


def _gather_plan(parts, relay=()):
    n = len(parts)
    relay = tuple(relay) + (False,) * (n - len(relay))

    def make(x_refs, out_refs, send_sems, recv_sems):
        x, y, c = _place()
        me, sibling = (x, y, c), (x, y, 1 - c)
        xn, yn, dg = (1 - x, y), (x, 1 - y), (1 - x, 1 - y)

        def slot(t, px, py, pc, half=None):
            ref = out_refs[t].at[4 * px + 2 * py + pc]
            if half is None:
                return ref
            cols = ref.shape[-1] // 2
            return ref.at[:, pl.ds(half * cols, cols)]

        def copy(t, k, dst, to, src=None):
            return pltpu.make_async_remote_copy(
                src_ref=dst if src is None else src, dst_ref=dst, send_sem=send_sems.at[8 * t + k],
                recv_sem=recv_sems.at[8 * t + k], device_id=to, device_id_type=MESH)

        mine = [pltpu.make_async_copy(x_refs[t], slot(t, *me), send_sems.at[8 * n + t]) for t in range(n)]
        sent = []
        for t in range(n):
            sent.append(copy(t, 0, slot(t, *me), sibling, src=x_refs[t]))
            sent.append(copy(t, 1, slot(t, *me), (*xn, c), src=x_refs[t]))
            sent.append(copy(t, 2, slot(t, *me), (*yn, c), src=x_refs[t]))
            if not relay[t]:
                sent.append(copy(t, 3, slot(t, *me), (*dg, c), src=x_refs[t]))

        def begin():
            for cp in mine + sent:
                cp.start()

        def finish():
            later = []

            def start(cp):
                cp.start()
                later.append(cp)

            for t in range(n):
                copy(t, 2, slot(t, *yn, c), me).wait_recv()
                if relay[t]:
                    start(copy(t, 3, slot(t, *yn, c, half=0), (*xn, c)))
                start(copy(t, 6, slot(t, *yn, c), sibling))
            for t in range(n):
                copy(t, 1, slot(t, *xn, c), me).wait_recv()
                if relay[t]:
                    start(copy(t, 4, slot(t, *xn, c, half=1), (*yn, c)))
                start(copy(t, 5, slot(t, *xn, c), sibling))
            for t in range(n):
                if relay[t]:
                    copy(t, 3, slot(t, *dg, c, half=0), me).wait_recv()
                    copy(t, 4, slot(t, *dg, c, half=1), me).wait_recv()
                else:
                    copy(t, 3, slot(t, *dg, c), me).wait_recv()
                start(copy(t, 7, slot(t, *dg, c), sibling))
            for t in range(n):
                copy(t, 0, slot(t, *sibling), me).wait_recv()
                copy(t, 5, slot(t, *xn, 1 - c), me).wait_recv()
                copy(t, 6, slot(t, *yn, 1 - c), me).wait_recv()
                copy(t, 7, slot(t, *dg, 1 - c), me).wait_recv()
            for cp in sent + later:
                cp.wait_send()
            for cp in mine:
                cp.wait()

        return begin, finish

    return _Comm(tuple(parts), tuple(jax.ShapeDtypeStruct((8,) + p.shape, p.dtype) for p in parts), 9 * n, make)


def _exchange_plan(ss):
    def make(s_refs, b_refs, send_sems, recv_sems):
        x, y, c = _place()
        return _start_wait([pltpu.make_async_remote_copy(
            src_ref=s_refs[t].at[2 * chip[0] + chip[1]], dst_ref=b_refs[t].at[j], send_sem=send_sems.at[3 * t + j],
            recv_sem=recv_sems.at[3 * t + j], device_id=(*chip, c), device_id_type=MESH)
            for t in range(len(s_refs)) for j, chip in enumerate(_other_chips(x, y))])

    return _Comm(tuple(ss), tuple(jax.ShapeDtypeStruct((3,) + s.shape[1:], s.dtype) for s in ss), 3 * len(ss), make)


def _exchange_window_plan(s):
    def make(s_refs, b_refs, send_sems, recv_sems):
        x, y, c = _place()
        return _start_wait([pltpu.make_async_remote_copy(
            src_ref=s_refs[0].at[:, pl.ds(pl.multiple_of((2 * chip[0] + chip[1]) * WIN_STEP, 128), WIN_W)],
            dst_ref=b_refs[0].at[j], send_sem=send_sems.at[j], recv_sem=recv_sems.at[j], device_id=(*chip, c),
            device_id_type=MESH) for j, chip in enumerate(_other_chips(x, y))])

    return _Comm((s,), (jax.ShapeDtypeStruct((3, s.shape[0], WIN_W), s.dtype),), 3, make)


def _swap_plan(gs):
    def make(g_refs, b_refs, send_sems, recv_sems):
        x, y, c = _place()
        return _start_wait([pltpu.make_async_remote_copy(
            src_ref=g_refs[t].at[1 - c], dst_ref=b_refs[t], send_sem=send_sems.at[t], recv_sem=recv_sems.at[t],
            device_id=(x, y, 1 - c), device_id_type=MESH) for t in range(len(g_refs))])

    return _Comm(tuple(gs), tuple(jax.ShapeDtypeStruct(g.shape[1:], g.dtype) for g in gs), len(gs), make)


def _spread_plan(parts):
    def make(p_refs, o_refs, send_sems, recv_sems):
        x, y, c = _place()
        copies = []
        for t in range(len(p_refs)):
            mine = o_refs[t].at[4 * x + 2 * y + c]
            copies.append(pltpu.make_async_copy(p_refs[t], mine, send_sems.at[7 * len(p_refs) + t]))
            for r in range(1, 8):
                peer = (1 - x if r & 4 else x, 1 - y if r & 2 else y, 1 - c if r & 1 else c)
                copies.append(pltpu.make_async_remote_copy(
                    src_ref=p_refs[t], dst_ref=mine, send_sem=send_sems.at[7 * t + r - 1],
                    recv_sem=recv_sems.at[7 * t + r - 1], device_id=peer, device_id_type=MESH))
        return _start_wait(copies)

    return _Comm(tuple(parts), tuple(jax.ShapeDtypeStruct((8,) + p.shape, p.dtype) for p in parts), 8 * len(parts),
                 make)


def _mm_nt_acc(name, a, w, tk, acc_in=None, epilogue=None, extras=(), extra_specs=(), extra_out_shapes=(),
               extra_out_specs=(), extra_scratch=(), comm=None, tb=TB):
    m, k = a.shape
    n = w.shape[1]
    nk, ni = k // tk, m // tb
    has_acc = acc_in is not None
    n_xc = len(comm.srcs) if comm else 0
    n_es = len(extra_scratch)

    def body(*refs):
        a_ref, w_ref = refs[0], refs[1]
        pos = 2
        acc_ref = None
        if has_acc:
            acc_ref = refs[pos]
            pos += 1
        ex = refs[pos:pos + len(extras)]
        pos += len(extras)
        xc_src = refs[pos:pos + n_xc]
        pos += n_xc
        n_scr = 1 + n_es + (2 if n_xc else 0)
        outs = refs[pos:len(refs) - n_scr - n_xc]
        xc_dst = refs[len(refs) - n_scr - n_xc:len(refs) - n_scr]
        scr = refs[len(refs) - n_scr]
        es = refs[len(refs) - n_scr + 1:len(refs) - n_scr + 1 + n_es]
        i, kk = pl.program_id(0), pl.program_id(1)
        if n_xc:
            begin, finish = comm.make(xc_src, xc_dst, refs[-2], refs[-1])
            pl.when((i == 0) & (kk == 0))(begin)

        @pl.when(kk == 0)
        def _():
            scr[...] = acc_ref[...] if has_acc else jnp.zeros_like(scr)

        scr[...] += _dot(a_ref[...], w_ref[...], NN)

        @pl.when(kk == nk - 1)
        def _():
            if epilogue is None:
                outs[0][...] = scr[...]
            else:
                epilogue(scr[...], outs, i, ni, *ex, *es)

        if n_xc:
            pl.when((i == ni - 1) & (kk == nk - 1))(finish)

    in_specs = [pl.BlockSpec((tb, tk), lambda i, kk: (i, kk)), pl.BlockSpec((tk, n), lambda i, kk: (kk, 0))]
    args = [a, w]
    if has_acc:
        in_specs.append(pl.BlockSpec((tb, n), lambda i, kk: (i, 0)))
        args.append(acc_in)
    in_specs += list(extra_specs) + [ANY] * n_xc
    args += list(extras) + (list(comm.srcs) if comm else [])
    if epilogue is None:
        out_shape = [jax.ShapeDtypeStruct((m, n), F32)]
        out_specs = [pl.BlockSpec((tb, n), lambda i, kk: (i, 0))]
    else:
        out_shape, out_specs = list(extra_out_shapes), list(extra_out_specs)
    scratch = [pltpu.VMEM((tb, n), F32)] + list(extra_scratch)
    if n_xc:
        out_shape += list(comm.out_shapes)
        out_specs += [ANY] * n_xc
        scratch += _comm_sems(comm)
    return pl.pallas_call(
        body, name=name, grid=(ni, nk), in_specs=in_specs, out_specs=out_specs, out_shape=out_shape,
        scratch_shapes=scratch, compiler_params=_cparams(2),
    )(*args)


def _rms_bwd_epilogue(du, outs, i, ni, h_ref, g_ref, dh1_ref, obuf, sems):
    dx_ref, dmeta_ref, dg_ref = outs
    h = h_ref[...]
    r = lax.rsqrt(jnp.mean(h * h, axis=-1, keepdims=True) + EPS)
    xh = h * r
    dxh = du * g_ref[...]
    dh0 = dh1_ref[...] + r * (dxh - xh * jnp.mean(dxh * xh, axis=-1, keepdims=True))

    def put(slot, tile):
        return pltpu.make_async_copy(obuf.at[slot], dx_ref.at[pl.ds(pl.multiple_of(tile * TB - T0, 8), TB)],
                                     sems.at[slot])

    @pl.when(i == 0)
    def _():
        dg_ref[...] = jnp.zeros_like(dg_ref)
        dmeta_ref[...] = dh0[PADF:T0, :]
        obuf[0] = dh0
        first = pltpu.make_async_copy(obuf.at[0, pl.ds(T0, TB - T0)], dx_ref.at[pl.ds(0, TB - T0)], sems.at[0])
        first.start()
        first.wait()

    @pl.when(i >= 1)
    def _():
        slot = i % 2

        @pl.when(i >= 3)
        def _():
            put(slot, i - 2).wait()

        obuf[slot] = dh0
        put(slot, i).start()

    dg_ref[...] += jnp.sum(du * xh, axis=0, keepdims=True)

    @pl.when(i == ni - 1)
    def _():
        for tile in (ni - 2, ni - 1):
            if tile >= 1:
                put(tile % 2, tile).wait()


def _mm_tn(name, a, b, bn, ncols=None, bcol0=0, into=None, col0=0, out_cols=None):
    t, m = a.shape
    n = ncols or b.shape[1]
    j0, bj0 = col0 // bn, bcol0 // bn

    def body(a_ref, b_ref, *rest):
        o_ref = rest[-1]

        @pl.when(pl.program_id(1) == 0)
        def _():
            o_ref[...] = jnp.zeros_like(o_ref)

        o_ref[...] += _dot(a_ref[...], b_ref[...], TN)

    in_specs = [pl.BlockSpec((TK, m), lambda j, kk: (kk, 0)), pl.BlockSpec((TK, bn), lambda j, kk: (kk, bj0 + j))]
    args = [a, b]
    aliases = {}
    if into is not None:
        in_specs.append(ANY)
        args.append(into)
        aliases = {2: 0}
        out_cols = into.shape[1]
    return pl.pallas_call(
        body, name=name, grid=(n // bn, t // TK), in_specs=in_specs,
        out_specs=pl.BlockSpec((m, bn), lambda j, kk: (0, j0 + j)),
        out_shape=jax.ShapeDtypeStruct((m, out_cols or n), F32), input_output_aliases=aliases,
        compiler_params=_cparams(2),
    )(*args)


def _place_merge_cols_call(dwp, dw_m, dw_glr):
    c0 = W_R + W_GP - 128
    tail = IN_PAD - c0
    rows = 256

    def body(m_ref, low, p_ref, o_ref, buf, sem):
        for r in range(0, D_MODEL, rows):
            buf[r:r + rows, :] = jnp.concatenate(
                [low[r:r + rows, :GATE_RANK], m_ref[r:r + rows, :],
                 jnp.zeros((rows, tail - GATE_RANK - W_M), F32)], axis=1)
        put = pltpu.make_async_copy(buf, o_ref.at[:, pl.ds(c0, tail)], sem)
        put.start()
        put.wait()

    return pl.pallas_call(
        body, name="place_merge_cols",
        in_specs=[pl.BlockSpec(memory_space=pltpu.VMEM), pl.BlockSpec(memory_space=pltpu.VMEM), ANY], out_specs=ANY,
        out_shape=jax.ShapeDtypeStruct(dwp.shape, F32), input_output_aliases={2: 0},
        scratch_shapes=[pltpu.VMEM((D_MODEL, tail), F32), pltpu.SemaphoreType.DMA],
        compiler_params=pltpu.CompilerParams(vmem_limit_bytes=VMEM_LIMIT),
    )(dw_m, dw_glr, dwp)


def _ret_fill_decay(lg_ref, dm_scr):
    c = TM
    ii = lax.broadcasted_iota(jnp.int32, (c, c), 0)
    jj = lax.broadcasted_iota(jnp.int32, (c, c), 1)
    rel = (ii - jj).astype(F32)
    for h in range(RET_HEADS):
        dm_scr[h] = jnp.where(rel >= 0, jnp.exp(jnp.maximum(rel, 0.0) * lg_ref[h]), 0.0)


def _ret_consts(lg, dm_ref):
    c = TM
    idx = lax.broadcasted_iota(jnp.int32, (c, 1), 0).astype(F32)
    xi = jnp.exp((idx + 1.0) * lg)
    zeta = jnp.exp((c - 1.0 - idx) * lg)
    gc = jnp.exp(jnp.full((1, 1), c, F32) * lg)
    return dm_ref[...], xi, zeta, gc


def _ret_fwd_call(rqk, rv, rg, gain, lgam):
    tp = rqk.shape[0]
    nc = tp // TM

    def body(lg_ref, qk_ref, v_ref, rg_ref, g_ref, o_ref, a_ref, st_ref, sc_ref, s_scr, dm_scr):
        @pl.when(pl.program_id(0) == 0)
        def _():
            s_scr[...] = jnp.zeros_like(s_scr)
            _ret_fill_decay(lg_ref, dm_scr)

        for h in range(RET_HEADS):
            dm, xi, zeta, gc = _ret_consts(lg_ref[h], dm_scr.at[h])
            q = qk_ref[:, h * RET_QK:(h + 1) * RET_QK]
            k = qk_ref[:, D_MODEL + h * RET_QK:D_MODEL + (h + 1) * RET_QK]
            v = v_ref[:, h * RET_V:(h + 1) * RET_V]
            sb = s_scr[h].astype(BF16)
            st_ref[0, h] = sb
            s = (_dot(q, k, NT) * dm).astype(BF16)
            sc_ref[0, h] = s
            o = _dot(s, v, NN) + xi * _dot(q, sb, NN)
            kz = (k.astype(F32) * zeta).astype(BF16)
            s_scr[h] = gc * s_scr[h] + _dot(kz, v, TN)
            o_ref[:, h * RET_V:(h + 1) * RET_V] = o
            mu = _head_mean(o)
            xc = o - mu
            xh = xc * lax.rsqrt(_head_mean(xc * xc) + EPS)
            a_ref[:, h * RET_V:(h + 1) * RET_V] = (
                xh * g_ref[:, h * RET_V:(h + 1) * RET_V] * _silu(rg_ref[:, h * RET_V:(h + 1) * RET_V])).astype(BF16)

    return pl.pallas_call(
        body, name="ret_fwd", grid=(nc,),
        in_specs=[pl.BlockSpec(memory_space=pltpu.SMEM),
                  pl.BlockSpec((TM, 2 * D_MODEL), lambda n: (n, 0)),
                  pl.BlockSpec((TM, RET_W), lambda n: (n, 0)),
                  pl.BlockSpec((TM, RET_W), lambda n: (n, 0)),
                  pl.BlockSpec((1, RET_W), lambda n: (0, 0))],
        out_specs=[pl.BlockSpec((TM, RET_W), lambda n: (n, 0)),
                   pl.BlockSpec((TM, RET_W), lambda n: (n, 0)),
                   pl.BlockSpec((1, RET_HEADS, RET_QK, RET_V), lambda n: (n, 0, 0, 0)),
                   pl.BlockSpec((1, RET_HEADS, TM, TM), lambda n: (n, 0, 0, 0))],
        out_shape=[jax.ShapeDtypeStruct((tp, RET_W), F32), jax.ShapeDtypeStruct((tp, RET_W), BF16),
                   jax.ShapeDtypeStruct((nc, RET_HEADS, RET_QK, RET_V), BF16),
                   jax.ShapeDtypeStruct((nc, RET_HEADS, TM, TM), BF16)],
        scratch_shapes=[pltpu.VMEM((RET_HEADS, RET_QK, RET_V), F32), pltpu.VMEM((RET_HEADS, TM, TM), F32)],
        compiler_params=_cparams(1),
    )(lgam, rqk, rv, rg, gain)


def _ret_bwd_call(rqk, rv, rg, o_ret, dpr, wbr, states, scores, gain, lgam, cos, sin):
    tp = rqk.shape[0]
    nc = tp // TM
    half = RET_QK // 2

    def body(lg_ref, qk_ref, v_ref, rg_ref, o_ref, dpr_ref, wbr_ref, st_ref, sc_ref, g_ref, cos_ref, sin_ref, dp_ref,
             dg_ref, ds_scr, dm_scr):
        @pl.when(pl.program_id(0) == 0)
        def _():
            ds_scr[...] = jnp.zeros_like(ds_scr)
            dg_ref[...] = jnp.zeros_like(dg_ref)
            _ret_fill_decay(lg_ref, dm_scr)

        cos, sin = cos_ref[...], sin_ref[...]
        for h in range(RET_HEADS):
            hs = slice(h * RET_V, (h + 1) * RET_V)
            dm, xi, zeta, gc = _ret_consts(lg_ref[h], dm_scr.at[h])
            o = o_ref[:, hs]
            mu = _head_mean(o)
            xc = o - mu
            rstd = lax.rsqrt(_head_mean(xc * xc) + EPS)
            xh = xc * rstd
            gain_h = g_ref[:, hs]
            g = rg_ref[:, hs]
            sg = _sigmoid(g)
            silu = g * sg
            dah = _dot(dpr_ref[...], wbr_ref[hs, :], NT)
            dp_ref[:, 4 * D_MODEL + h * RET_V:4 * D_MODEL + (h + 1) * RET_V] = (
                dah * (xh * gain_h) * (sg * (1.0 + g * (1.0 - sg)))).astype(BF16)
            dn = dah * silu
            dg_ref[:, hs] += jnp.sum(dn * xh, axis=0, keepdims=True)
            dxh = dn * gain_h
            do = rstd * (dxh - _head_mean(dxh) - xh * _head_mean(dxh * xh))
            dob = do.astype(BF16)
            q = qk_ref[:, h * RET_QK:(h + 1) * RET_QK]
            k = qk_ref[:, D_MODEL + h * RET_QK:D_MODEL + (h + 1) * RET_QK]
            v = v_ref[:, hs]
            sp = st_ref[0, h]
            ds = ds_scr[h]
            dsb = ds.astype(BF16)
            s = sc_ref[0, h]
            dsc = (_dot(dob, v, NT) * dm).astype(BF16)
            dq = _dot(dsc, k, NN) + xi * _dot(dob, sp, NT)
            dk = _dot(dsc, q, TN) + zeta * _dot(v, dsb, NT)
            kz = (k.astype(F32) * zeta).astype(BF16)
            dv = _dot(s, dob, TN) + _dot(kz, dsb, NN)
            qx = (q.astype(F32) * xi).astype(BF16)
            ds_scr[h] = gc * ds + _dot(qx, dob, TN)
            dp_ref[:, 2 * D_MODEL + h * RET_V:2 * D_MODEL + (h + 1) * RET_V] = dv.astype(BF16)
            dk = dk * (RET_QK ** -0.5)
            for base, t in ((0, dq), (D_MODEL, dk)):
                t1, t2 = t[:, :half], t[:, half:]
                dp_ref[:, base + h * RET_QK:base + h * RET_QK + half] = (t1 * cos + t2 * sin).astype(BF16)
                dp_ref[:, base + h * RET_QK + half:base + (h + 1) * RET_QK] = (t2 * cos - t1 * sin).astype(BF16)

    rev = lambda n: (nc - 1 - n, 0)
    return pl.pallas_call(
        body, name="ret_bwd", grid=(nc,),
        in_specs=[pl.BlockSpec(memory_space=pltpu.SMEM),
                  pl.BlockSpec((TM, 2 * D_MODEL), rev),
                  pl.BlockSpec((TM, RET_W), rev),
                  pl.BlockSpec((TM, RET_W), rev),
                  pl.BlockSpec((TM, RET_W), rev),
                  pl.BlockSpec((TM, D_MODEL), rev),
                  pl.BlockSpec((RET_W, D_MODEL), lambda n: (0, 0)),
                  pl.BlockSpec((1, RET_HEADS, RET_QK, RET_V), lambda n: (nc - 1 - n, 0, 0, 0)),
                  pl.BlockSpec((1, RET_HEADS, TM, TM), lambda n: (nc - 1 - n, 0, 0, 0)),
                  pl.BlockSpec((1, RET_W), lambda n: (0, 0)),
                  pl.BlockSpec((TM, half), rev),
                  pl.BlockSpec((TM, half), rev)],
        out_specs=[pl.BlockSpec((TM, W_R), rev), pl.BlockSpec((1, RET_W), lambda n: (0, 0))],
        out_shape=[jax.ShapeDtypeStruct((tp, W_R), BF16), jax.ShapeDtypeStruct((1, RET_W), F32)],
        scratch_shapes=[pltpu.VMEM((RET_HEADS, RET_QK, RET_V), F32), pltpu.VMEM((RET_HEADS, TM, TM), F32)],
        compiler_params=_cparams(1),
    )(lgam, rqk, rv, rg, o_ret, dpr, wbr, states, scores, gain, cos, sin)


GLA_LEVELS = tuple(GC >> (s + 1) for s in range(int(math.log2(GC // GLA_SUB))))
NLEV = len(GLA_LEVELS)


def _gla_tril():
    return np.tril(np.ones((GC, GC), np.float32))


def _gla_masks():
    ii = lax.broadcasted_iota(jnp.int32, (GC, GC), 0)
    jj = lax.broadcasted_iota(jnp.int32, (GC, GC), 1)
    masks = []
    for m in GLA_LEVELS:
        sh = int(math.log2(2 * m))
        masks.append(((ii >> sh) == (jj >> sh)) & ((ii & m) != 0) & ((jj & m) == 0))
    sh = int(math.log2(GLA_SUB))
    md = ((ii >> sh) == (jj >> sh)) & (jj <= ii)
    row = lax.broadcasted_iota(jnp.int32, (GC, 1), 0)
    second = [(row & m) != 0 for m in GLA_LEVELS]
    return masks, md, second


def _gla_gate_call(u, w_g, wg, bg, pmat):
    tp = u.shape[0]
    gb = _proj_rows(tp)
    assert gb % GC == 0

    def body(u_ref, w_ref, wg_ref, bg_ref, p_ref, glr_ref, z_ref, b_ref):
        glr = _dot(u_ref[...], w_ref[...], NT)
        glr_ref[...] = glr
        z = _dot(glr.astype(BF16), wg_ref[...], NN) + bg_ref[...]
        z_ref[...] = z
        la = (jnp.minimum(z, 0.0) - jnp.log1p(jnp.exp(-jnp.abs(z)))) * (1.0 / GATE_TAU)
        for r in range(0, gb, GC):
            b_ref[r:r + GC, :] = _exact_pm(p_ref[...], la[r:r + GC, :])

    tile = pl.BlockSpec((gb, GLA_KW), lambda i: (i, 0))
    return pl.pallas_call(
        body, name="gla_gate", grid=(tp // gb,),
        in_specs=[pl.BlockSpec((gb, D_MODEL), lambda i: (i, 0)),
                  pl.BlockSpec((128, D_MODEL), lambda i: ((W_GP - 128) // 128, 0)),
                  pl.BlockSpec((128, GLA_KW), lambda i: (0, 0)),
                  pl.BlockSpec((1, GLA_KW), lambda i: (0, 0)), pl.BlockSpec((GC, GC), lambda i: (0, 0))],
        out_specs=[pl.BlockSpec((gb, 128), lambda i: (i, 0)), tile, tile],
        out_shape=[jax.ShapeDtypeStruct((tp, 128), F32), jax.ShapeDtypeStruct((tp, GLA_KW), F32),
                   jax.ShapeDtypeStruct((tp, GLA_KW), F32)],
        compiler_params=_cparams(1),
    )(u, w_g, wg, bg, pmat)


def _gla_gate_bwd_call(db, z, glr, u, wg, pmat_t, d_g):
    tp = db.shape[0]
    gb = _proj_rows(tp)
    assert gb % GC == 0 and (W_GP - 128) % 128 == 0

    def body(db_ref, z_ref, glr_ref, u_ref, wg_ref, pt_ref, dgin_ref, dg_ref, dwg_ref, dbg_ref, dwl_ref):
        i = pl.program_id(0)

        @pl.when(i == 0)
        def _():
            dwg_ref[...] = jnp.zeros_like(dwg_ref)
            dbg_ref[...] = jnp.zeros_like(dbg_ref)
            dwl_ref[...] = jnp.zeros_like(dwl_ref)

        dla = jnp.concatenate([_exact_pm(pt_ref[...], db_ref[r:r + GC, :]) for r in range(0, gb, GC)], axis=0)
        row = i * gb + lax.broadcasted_iota(jnp.int32, (gb, 1), 0)
        dz = jnp.where(row >= PADF, dla * (1.0 / GATE_TAU) * _sigmoid(-z_ref[...]), 0.0)
        dzb = dz.astype(BF16)
        dglr = _dot(dzb, wg_ref[...], NT).astype(BF16)
        dg_ref[...] = dglr
        dwg_ref[...] += _dot(glr_ref[...].astype(BF16), dzb, TN)
        dbg_ref[...] += jnp.sum(dz, axis=0, keepdims=True)
        dwl_ref[...] += _dot(u_ref[...], dglr, TN)

    tile = pl.BlockSpec((gb, GLA_KW), lambda i: (i, 0))
    const = lambda i: (0, 0)
    return pl.pallas_call(
        body, name="gla_gate_bwd", grid=(tp // gb,),
        in_specs=[tile, tile, pl.BlockSpec((gb, 128), lambda i: (i, 0)), pl.BlockSpec((gb, D_MODEL), lambda i: (i, 0)),
                  pl.BlockSpec((128, GLA_KW), const), pl.BlockSpec((GC, GC), const), ANY],
        out_specs=[pl.BlockSpec((gb, 128), lambda i: (i, (W_GP - 128) // 128)), pl.BlockSpec((128, GLA_KW), const),
                   pl.BlockSpec((1, GLA_KW), const), pl.BlockSpec((D_MODEL, 128), const)],
        out_shape=[jax.ShapeDtypeStruct(d_g.shape, BF16), jax.ShapeDtypeStruct((128, GLA_KW), F32),
                   jax.ShapeDtypeStruct((1, GLA_KW), F32), jax.ShapeDtypeStruct((D_MODEL, 128), F32)],
        input_output_aliases={6: 0}, compiler_params=_cparams(1),
    )(db, z, glr, u, wg, pmat_t, d_g)


def _gla_row_steps(b_ref, cs, rows, size):
    parts = [jnp.zeros((size, GLA_K), F32) if r is None else jnp.broadcast_to(b_ref[r:r + 1, cs], (size, GLA_K))
             for r in rows]
    return parts[0] if len(parts) == 1 else jnp.concatenate(parts, axis=0)


def _gla_factors(b_ref, h, second):
    cs = slice(h * GLA_K, (h + 1) * GLA_K)
    b = b_ref[:, cs]
    fq, fk = [], []
    for l, m in enumerate(GLA_LEVELS):
        d = b - _gla_row_steps(b_ref, cs, [s + m - 1 for s in range(0, GC, 2 * m)], 2 * m)
        f = jnp.exp(jnp.where(second[l], d, -d))
        fq.append(jnp.where(second[l], f, 0.0))
        fk.append(jnp.where(second[l], 0.0, f))
    dd = b - _gla_row_steps(b_ref, cs, [None] + [s - 1 for s in range(GLA_SUB, GC, GLA_SUB)], GLA_SUB)
    ed = jnp.exp(dd)
    edi = jnp.exp(-dd)
    eb = jnp.exp(b)
    bl = b_ref[GC - 1:GC, cs]
    ee = jnp.exp(bl - b)
    ebl = jnp.exp(bl)
    return fq, fk, ed, edi, eb, ee, ebl


def _gla_scaled(q, k, fq, fk, ed, edi):
    qt = [(q * f).astype(BF16) for f in fq]
    kt = [(k * f).astype(BF16) for f in fk]
    return qt, kt, (q * ed).astype(BF16), (k * edi).astype(BF16)


def _gla_scores(qt, kt, qd, kd, masks, md):
    a = jnp.where(md, _dot(qd, kd, NT), 0.0)
    for l in range(NLEV):
        a = a + jnp.where(masks[l], _dot(qt[l], kt[l], NT), 0.0)
    return a.astype(BF16)


def _gla_fwd_call(gqk, gv, b, gg, gain, comm=None):
    tp = gqk.shape[0]
    nc = tp // GC
    ns = nc // GS
    n_xc = len(comm.srcs) if comm else 0

    def body(qk_ref, v_ref, b_ref, gg_ref, g_ref, *rest):
        xc_src = rest[:n_xc]
        o_ref, a_ref, st_ref, am_ref = rest[n_xc:n_xc + 4]
        xc_dst = rest[n_xc + 4:2 * n_xc + 4]
        s_scr = rest[2 * n_xc + 4]
        n = pl.program_id(0)
        if n_xc:
            begin, finish = comm.make(xc_src, xc_dst, rest[-2], rest[-1])
            pl.when(n == 0)(begin)
            pl.when(n == ns - 1)(finish)

        @pl.when(n == 0)
        def _():
            s_scr[...] = jnp.zeros_like(s_scr)

        masks, md, second = _gla_masks()
        for cc in range(GS):
            rows = pl.ds(cc * GC, GC)
            qk_c, v_c, b_c, gg_c, o_c, a_c = (r.at[rows] for r in (qk_ref, v_ref, b_ref, gg_ref, o_ref, a_ref))
            for h in range(GLA_HEADS):
                q = qk_c[:, h * GLA_K:(h + 1) * GLA_K]
                k = qk_c[:, GLA_KW + h * GLA_K:GLA_KW + (h + 1) * GLA_K]
                vs = slice(h * GLA_V, (h + 1) * GLA_V)
                v = v_c[:, vs]
                fq, fk, ed, edi, eb, ee, ebl = _gla_factors(b_c, h, second)
                a = _gla_scores(*_gla_scaled(q, k, fq, fk, ed, edi), masks, md)
                am_ref[cc, h] = a
                sb = s_scr[h].astype(BF16)
                st_ref[cc, h] = sb
                o = _dot(a, v, NN) + _dot((q * eb).astype(BF16), sb, NT)
                s_scr[h] = s_scr[h] * ebl + _dot(v, (k * ee).astype(BF16), TN)
                o_c[:, vs] = o
                xh = o * lax.rsqrt(_head_mean(o * o) + EPS)
                a_c[:, vs] = (xh * g_ref[:, vs] * _silu(gg_c[:, vs])).astype(BF16)

    return pl.pallas_call(
        body, name="gla_fwd", grid=(ns,),
        in_specs=[pl.BlockSpec((GS * GC, 2 * GLA_KW), lambda n: (n, 0)),
                  pl.BlockSpec((GS * GC, GLA_W), lambda n: (n, 0)),
                  pl.BlockSpec((GS * GC, GLA_KW), lambda n: (n, 0)),
                  pl.BlockSpec((GS * GC, GLA_W), lambda n: (n, 0)),
                  pl.BlockSpec((1, GLA_W), lambda n: (0, 0))] + [ANY] * n_xc,
        out_specs=[pl.BlockSpec((GS * GC, GLA_W), lambda n: (n, 0)),
                   pl.BlockSpec((GS * GC, GLA_W), lambda n: (n, 0)),
                   pl.BlockSpec((GS, GLA_HEADS, GLA_V, GLA_K), lambda n: (n, 0, 0, 0)),
                   pl.BlockSpec((GS, GLA_HEADS, GC, GC), lambda n: (n, 0, 0, 0))] + [ANY] * n_xc,
        out_shape=[jax.ShapeDtypeStruct((tp, GLA_W), F32), jax.ShapeDtypeStruct((tp, GLA_W), BF16),
                   jax.ShapeDtypeStruct((nc, GLA_HEADS, GLA_V, GLA_K), BF16),
                   jax.ShapeDtypeStruct((nc, GLA_HEADS, GC, GC), BF16)] + (list(comm.out_shapes) if comm else []),
        scratch_shapes=[pltpu.VMEM((GLA_HEADS, GLA_V, GLA_K), F32)] + (_comm_sems(comm) if comm else []),
        compiler_params=_cparams(1),
    )(gqk, gv, b, gg, gain, *(comm.srcs if comm else ()))


def _gla_bwd_call(gqk, gv, b, gg, o_gla, da, states, scores, gain, comm=None):
    tp = gqk.shape[0]
    nc = tp // GC
    ns = nc // GS
    o_gv, o_gg = 2 * GLA_KW, 2 * GLA_KW + GLA_W
    n_xc = len(comm.srcs) if comm else 0

    def body(qk_all, v_all, b_all, gg_all, o_all, da_all, st_ref, am_ref, g_ref, *rest):
        xc_src = rest[:n_xc]
        dp_all, db_all, dg_ref = rest[n_xc:n_xc + 3]
        xc_dst = rest[n_xc + 3:2 * n_xc + 3]
        ds_scr = rest[2 * n_xc + 3]
        n = pl.program_id(0)
        if n_xc:
            begin, finish = comm.make(xc_src, xc_dst, rest[-2], rest[-1])
            pl.when(n == 0)(begin)
            pl.when(n == ns - 1)(finish)

        @pl.when(n == 0)
        def _():
            ds_scr[...] = jnp.zeros_like(ds_scr)
            dg_ref[...] = jnp.zeros_like(dg_ref)

        masks, md, second = _gla_masks()
        for cc, h in [(cc, h) for cc in reversed(range(GS)) for h in range(GLA_HEADS)]:
            rows = pl.ds(cc * GC, GC)
            qk_ref, v_ref, b_scr, gg_ref, o_ref, da_ref, dp_ref, db_scr = (
                r.at[rows] for r in (qk_all, v_all, b_all, gg_all, o_all, da_all, dp_all, db_all))
            cs = slice(h * GLA_K, (h + 1) * GLA_K)
            vs = slice(h * GLA_V, (h + 1) * GLA_V)
            o = o_ref[:, vs]
            rstd = lax.rsqrt(_head_mean(o * o) + EPS)
            xh = o * rstd
            gain_h = g_ref[:, vs]
            g = gg_ref[:, vs]
            sg = _sigmoid(g)
            dah = da_ref[:, vs]
            dp_ref[:, o_gg + h * GLA_V:o_gg + (h + 1) * GLA_V] = (
                dah * (xh * gain_h) * (sg * (1.0 + g * (1.0 - sg)))).astype(BF16)
            dn = dah * (g * sg)
            dg_ref[:, vs] += jnp.sum(dn * xh, axis=0, keepdims=True)
            dxh = dn * gain_h
            do = rstd * (dxh - xh * _head_mean(dxh * xh))
            dob = do.astype(BF16)
            q = qk_ref[:, cs]
            k = qk_ref[:, GLA_KW + h * GLA_K:GLA_KW + (h + 1) * GLA_K]
            v = v_ref[:, vs]
            fq, fk, ed, edi, eb, ee, ebl = _gla_factors(b_scr, h, second)
            qt, kt, qd, kd = _gla_scaled(q, k, fq, fk, ed, edi)
            sp = st_ref[cc, h]
            ds = ds_scr[h]
            dsb = ds.astype(BF16)
            q_in = q * eb
            k_end = k * ee
            da_s = _dot(dob, v, NT)
            dv = _dot(am_ref[cc, h], dob, TN) + _dot(k_end.astype(BF16), dsb, NT)
            dq_in = _dot(dob, sp, NN)
            dk_end = _dot(v, dsb, NN)
            dbl = jnp.sum(sp.astype(F32) * ds, axis=0, keepdims=True) * ebl
            ds_scr[h] = ds * ebl + _dot(dob, q_in.astype(BF16), TN)
            dq = dq_in * eb
            dk = dk_end * ee
            de_end = dk_end * k_end
            db = dq_in * q_in - de_end
            placed = [(GC - 1, jnp.sum(de_end, axis=0, keepdims=True) + dbl)]
            for l, m in enumerate(GLA_LEVELS):
                dal = jnp.where(masks[l], da_s, 0.0).astype(BF16)
                dqt = _dot(dal, kt[l], NN)
                dkt = _dot(dal, qt[l], TN)
                dq = dq + dqt * fq[l]
                dk = dk + dkt * fk[l]
                gl = dqt * (q * fq[l]) - dkt * (k * fk[l])
                db = db + gl
                placed += [(s + m - 1, -jnp.sum(gl[s:s + 2 * m], axis=0, keepdims=True)) for s in range(0, GC, 2 * m)]
            dad = jnp.where(md, da_s, 0.0).astype(BF16)
            dqd = _dot(dad, kd, NN)
            dkd = _dot(dad, qd, TN)
            dq = dq + dqd * ed
            dk = dk + dkd * edi
            gd = dqd * (q * ed) - dkd * (k * edi)
            db = db + gd
            placed += [(s - 1, -jnp.sum(gd[s:s + GLA_SUB], axis=0, keepdims=True)) for s in range(GLA_SUB, GC, GLA_SUB)]
            db_scr[:, cs] = db
            for r, val in placed:
                db_scr[r:r + 1, cs] += val
            dp_ref[:, cs] = (dq * (GLA_K ** -0.5)).astype(BF16)
            dp_ref[:, GLA_KW + h * GLA_K:GLA_KW + (h + 1) * GLA_K] = dk.astype(BF16)
            dp_ref[:, o_gv + h * GLA_V:o_gv + (h + 1) * GLA_V] = dv.astype(BF16)

    rev = lambda n: (ns - 1 - n, 0)
    const = lambda n: (0, 0)
    xc_shapes, xc_sems = (list(comm.out_shapes), _comm_sems(comm)) if n_xc else ([], [])
    return pl.pallas_call(
        body, name="gla_bwd", grid=(ns,),
        in_specs=[pl.BlockSpec((GS * GC, 2 * GLA_KW), rev),
                  pl.BlockSpec((GS * GC, GLA_W), rev),
                  pl.BlockSpec((GS * GC, GLA_KW), rev),
                  pl.BlockSpec((GS * GC, GLA_W), rev),
                  pl.BlockSpec((GS * GC, GLA_W), rev),
                  pl.BlockSpec((GS * GC, GLA_W), rev),
                  pl.BlockSpec((GS, GLA_HEADS, GLA_V, GLA_K), lambda n: (ns - 1 - n, 0, 0, 0)),
                  pl.BlockSpec((GS, GLA_HEADS, GC, GC), lambda n: (ns - 1 - n, 0, 0, 0)),
                  pl.BlockSpec((1, GLA_W), const)] + [ANY] * n_xc,
        out_specs=[pl.BlockSpec((GS * GC, W_GP), rev), pl.BlockSpec((GS * GC, GLA_KW), rev),
                   pl.BlockSpec((1, GLA_W), const)] + [ANY] * n_xc,
        out_shape=[jax.ShapeDtypeStruct((tp, W_GP), BF16), jax.ShapeDtypeStruct((tp, GLA_KW), F32),
                   jax.ShapeDtypeStruct((1, GLA_W), F32)] + xc_shapes,
        scratch_shapes=[pltpu.VMEM((GLA_HEADS, GLA_V, GLA_K), F32)] + xc_sems,
        compiler_params=_cparams(1),
    )(gqk, gv, b, gg, o_gla, da, states, scores, gain, *(comm.srcs if comm else ()))


def _mid_call(a_ret, a_gla, mg, h0, tgt, wbr, wbg, wout, gf):
    tp = h0.shape[0]
    nt = tp // TM

    def body(ar_ref, ag_ref, mg_ref, h_ref, t_ref, wbr_ref, wbg_ref, wo_ref, gf_ref,
             dh1_ref, dag_ref, dm_ref, mb_ref, dh1b_ref, dprb_ref, dpgb_ref, loss_ref, dgf_ref):
        i = pl.program_id(0)

        @pl.when(i == 0)
        def _():
            loss_ref[...] = jnp.zeros_like(loss_ref)
            dgf_ref[...] = jnp.zeros_like(dgf_ref)

        ar, ag = ar_ref[...], ag_ref[...]
        pr = _dot(ar, wbr_ref[...], NN)
        pg = _dot(ag, wbg_ref[...], NN)
        sr = _sigmoid(mg_ref[:, :D_MODEL])
        sg = _sigmoid(mg_ref[:, D_MODEL:])
        merged = (sr * pr + sg * pg).astype(BF16)
        mb_ref[...] = merged
        h1 = h_ref[...] + _dot(merged, wo_ref[...], NN)
        r1 = lax.rsqrt(jnp.mean(h1 * h1, axis=-1, keepdims=True) + EPS)
        xh = h1 * r1
        gfv = gf_ref[...]
        live = jnp.where(i > 0, 1.0, 0.0).astype(F32)
        err = (xh * gfv - t_ref[...]) * live
        loss_ref[...] += jnp.full(loss_ref.shape, 0.5 / D_MODEL, F32) * jnp.sum(err * err)
        dy = err * (1.0 / D_MODEL)
        dgf_ref[...] += jnp.sum(dy * xh, axis=0, keepdims=True)
        dxh = dy * gfv
        dh1 = r1 * (dxh - xh * jnp.mean(dxh * xh, axis=-1, keepdims=True))
        dh1_ref[...] = dh1
        dh1b = dh1.astype(BF16)
        dh1b_ref[...] = dh1b
        dmerged = _dot(dh1b, wo_ref[...], NT)
        dm_ref[:, :D_MODEL] = (dmerged * pr * sr * (1.0 - sr)).astype(BF16)
        dm_ref[:, D_MODEL:] = (dmerged * pg * sg * (1.0 - sg)).astype(BF16)
        dpr = (dmerged * sr).astype(BF16)
        dpg = (dmerged * sg).astype(BF16)
        dprb_ref[...] = dpr
        dpgb_ref[...] = dpg
        dag_ref[...] = _dot(dpg, wbg_ref[...], NT)

    tile = lambda w: pl.BlockSpec((TM, w), lambda i: (i, 0))
    const = lambda r, w: pl.BlockSpec((r, w), lambda i: (0, 0))
    return pl.pallas_call(
        body, name="merge_out_loss", grid=(nt,),
        in_specs=[tile(RET_W), tile(GLA_W), tile(W_M), tile(D_MODEL),
                  pl.BlockSpec((TM, D_MODEL), lambda i: (jnp.maximum(i - 1, 0), 0)),
                  const(RET_W, D_MODEL), const(GLA_W, D_MODEL), const(D_MODEL, D_MODEL), const(1, D_MODEL)],
        out_specs=[tile(D_MODEL), tile(GLA_W), tile(W_M), tile(D_MODEL), tile(D_MODEL), tile(D_MODEL),
                   tile(D_MODEL), const(1, 128), const(1, D_MODEL)],
        out_shape=[jax.ShapeDtypeStruct((tp, D_MODEL), F32), jax.ShapeDtypeStruct((tp, GLA_W), F32),
                   jax.ShapeDtypeStruct((tp, W_M), BF16),
                   jax.ShapeDtypeStruct((tp, D_MODEL), BF16), jax.ShapeDtypeStruct((tp, D_MODEL), BF16),
                   jax.ShapeDtypeStruct((tp, D_MODEL), BF16), jax.ShapeDtypeStruct((tp, D_MODEL), BF16),
                   jax.ShapeDtypeStruct((1, 128), F32), jax.ShapeDtypeStruct((1, D_MODEL), F32)],
        compiler_params=_cparams(1),
    )(a_ret, a_gla, mg, h0, tgt, wbr, wbg, wout, gf)


def _device_step(x2d, tgt2d, meta, norm_gain, w_in_part, w_gate_up, b_gate, ret_gain, gla_gain, branch_parts,
                 final_gain, ck):
    seq = x2d.shape[0]
    tp = T0 + seq
    head = jnp.concatenate([jnp.zeros((PADF, D_MODEL), F32), meta], axis=0)
    wg_pad = jnp.pad(w_gate_up, ((0, 128 - GATE_RANK), (0, 0))).astype(BF16)

    half = RET_QK // 2
    cos, sin = (jnp.asarray(t) for t in _rope_tables(tp))
    lgam = jnp.log1p(-(2.0 ** (-5.0 - jnp.arange(RET_HEADS, dtype=F32))))
    pmat = jnp.asarray(_gla_tril(), BF16)
    pmat_t = jnp.asarray(_gla_tril().T.copy(), BF16)

    h0, u, g_in = _rms_call(x2d, head, norm_gain, _gather_plan([w_in_part], relay=(True,)))
    sw, hc = w_in_part.shape
    w_in_t = g_in.reshape(4, 2, sw, hc).transpose(0, 2, 1, 3).reshape(4 * sw, 2 * hc)
    w_r = w_in_t
    w_g = jnp.pad(w_in_t[W_R:W_R + W_G], ((0, W_GP - W_G), (0, 0)))
    w_m = w_in_t[W_R + W_G:]
    tab = pl.BlockSpec((_proj_rows(tp), half), lambda j, i: (i, 0))
    rqk = _mm_nn("proj_rqk", u, w_r, BF16, D_MODEL, 0, 2 * D_MODEL, _rope_epilogue, (cos, sin), (tab, tab))
    rv = _mm_nn("proj_rv", u, w_r, BF16, RET_W, 2 * D_MODEL, RET_W)
    rg = _mm_nn("proj_rg", u, w_r, F32, RET_W, 4 * D_MODEL, RET_W)
    gqk = _mm_nn("proj_gqk", u, w_g, F32, 2 * GLA_KW, 0, 2 * GLA_KW, _gqk_epilogue)
    gv = _mm_nn("proj_gv", u, w_g, BF16, GLA_W, 2 * GLA_KW, GLA_W)
    gg = _mm_nn("proj_gg", u, w_g, F32, GLA_W, 2 * GLA_KW + GLA_W, GLA_W)
    mg = _mm_nn("proj_mg", u, w_m, F32, W_M, 0, W_M)

    o_ret, a_ret, st_ret, sc_ret = _ret_fwd_call(rqk, rv, rg, ret_gain, lgam)
    glr, z_gate, b_dec = _gla_gate_call(u, w_g, wg_pad, b_gate, pmat)
    o_gla, a_gla, st_gla, sc_gla, g_br, g_bg, g_out = _gla_fwd_call(gqk, gv, b_dec, gg, gla_gain,
                                                                    comm=_spread_plan(branch_parts))
    wbr = g_br.reshape(RET_W, D_MODEL)
    wbg = g_bg.reshape(GLA_W, D_MODEL)
    wout = g_out.reshape(D_MODEL, D_MODEL)

    gf = final_gain.reshape(1, D_MODEL)
    (dh1, da_gla, dm, merged_b, dh1_b, dpr_b, dpg_b, loss, dgf) = _mid_call(
        a_ret, a_gla, mg, h0, tgt2d, wbr, wbg, wout, gf)

    names_b = ("w_branch_ret", "w_branch_gla", "w_out")
    g2_b = [_mm_tn("dw_br", a_ret, dpr_b, D_MODEL).reshape(4, 2, RET_W // 8, D_MODEL).transpose(1, 0, 2, 3),
            _mm_tn("dw_bg", a_gla, dpg_b, D_MODEL).reshape(4, 2, GLA_W // 8, D_MODEL).transpose(1, 0, 2, 3),
            _mm_tn("dw_out", merged_b, dh1_b, D_MODEL).reshape(4, 2, D_MODEL // 8, D_MODEL).transpose(1, 0, 2, 3)]
    sib_b = _swap_halves_call("swap_halves_branch", g2_b)
    sum_b = [_add_half_call("add_half_" + nm, g, b, ck) for nm, g, b in zip(names_b, g2_b, sib_b)]
    d_g, db_dec, dgla_gain, *chips_b = _gla_bwd_call(gqk, gv, b_dec, gg, o_gla, da_gla, st_gla, sc_gla, gla_gain,
                                                     comm=_exchange_plan(sum_b))
    d_g, dwg, dbg, dw_glr = _gla_gate_bwd_call(db_dec, z_gate, glr, u, wg_pad, pmat_t, d_g)
    mine = [_add_chips_call("add_chips_" + nm, g, b, p, ck) for nm, g, b, p in zip(names_b, g2_b, sib_b, chips_b)]

    d_r, dret_gain = _ret_bwd_call(rqk, rv, rg, o_ret, dpr_b, wbr, st_ret, sc_ret, ret_gain, lgam, cos, sin)

    dwp = _mm_tn("dw_r", u, d_r, 3 * D_MODEL, out_cols=IN_PAD)
    dwp = _mm_tn("dw_g", u, d_g, 3 * D_MODEL, ncols=W_GP - 128, into=dwp, col0=W_R)
    g2_in = _place_merge_cols_call(dwp, _mm_tn("dw_m", u, dm, 2 * D_MODEL), dw_glr).reshape(2, D_MODEL // 2, IN_PAD)

    du, sib_in = _mm_nt_acc("du_g", d_g, w_g, W_GP, comm=_swap_plan([g2_in]), tb=_proj_rows(tp))
    sum_in = _add_rows_call("add_half_w_in", g2_in, sib_in, ck)
    du, chips_in = _mm_nt_acc("du_r", d_r, w_r, 2 * D_MODEL, acc_in=du, comm=_exchange_window_plan(sum_in),
                              tb=_proj_rows(tp))
    tile = pl.BlockSpec((TB, D_MODEL), lambda i, kk: (i, 0))
    row = pl.BlockSpec((1, D_MODEL), lambda i, kk: (0, 0))
    dx, dmeta, dnorm_gain = _mm_nt_acc(
        "du_m", dm, w_m, W_M, acc_in=du, epilogue=_rms_bwd_epilogue, extras=(h0, norm_gain, dh1),
        extra_specs=(tile, row, tile),
        extra_out_shapes=(jax.ShapeDtypeStruct((seq, D_MODEL), F32), jax.ShapeDtypeStruct((N_META, D_MODEL), F32),
                          jax.ShapeDtypeStruct((1, D_MODEL), F32)),
        extra_out_specs=(ANY, pl.BlockSpec((N_META, D_MODEL), lambda i, kk: (0, 0)), row),
        extra_scratch=(pltpu.VMEM((2, TB, D_MODEL), F32), pltpu.SemaphoreType.DMA((2,))))
    small = dict(norm_gain=dnorm_gain, b_gate=dbg, ret_norm_gain=dret_gain, gla_norm_gain=dgla_gain,
                 final_norm_gain=dgf, w_gate_up=dwg[:GATE_RANK], meta_tokens=dmeta, loss=loss[0, 0])
    rows = -(-sum(sz for _, sz in SMALL) // 128 // 8) * 8
    mine_in, g_small = _add_window_call("add_chips_w_in", g2_in, sib_in, chips_in, ck,
                                        _gather_plan([_pack_rows([small[nm] for nm, _ in SMALL], rows)]))
    full = _join_halves_call("join_halves", [mine_in] + mine)

    return dict(dx=dx, small=g_small, w_in=full[0], w_branch_ret=full[1], w_branch_gla=full[2], w_out=full[3])


MESH = pl.DeviceIdType.MESH
ANY = pl.BlockSpec(memory_space=pl.ANY)


def _place():
    return lax.axis_index("x"), lax.axis_index("y"), lax.axis_index("c")


def _gather8_call(name, parts):
    comm = _gather_plan(parts)
    n = len(parts)

    def body(*refs):
        begin, finish = comm.make(refs[:n], refs[n:2 * n], refs[-2], refs[-1])
        begin()
        finish()

    return pl.pallas_call(
        body, name=name, out_shape=list(comm.out_shapes), in_specs=[ANY] * n, out_specs=[ANY] * n,
        scratch_shapes=_comm_sems(comm),
    )(*parts)


def _swap_halves_call(name, gs):
    n = len(gs)

    def body(*refs):
        g_refs, b_refs = refs[:n], refs[n:2 * n]
        send_sems, recv_sems = refs[2 * n:]
        x, y, c = _place()
        copies = [pltpu.make_async_remote_copy(
            src_ref=g_refs[t].at[1 - c], dst_ref=b_refs[t], send_sem=send_sems.at[t], recv_sem=recv_sems.at[t],
            device_id=(x, y, 1 - c), device_id_type=MESH) for t in range(n)]
        for cp in copies:
            cp.start()
        for cp in copies:
            cp.wait()

    return pl.pallas_call(
        body, name=name,
        out_shape=[jax.ShapeDtypeStruct(g.shape[1:], g.dtype) for g in gs],
        in_specs=[ANY] * n, out_specs=[ANY] * n,
        scratch_shapes=[pltpu.SemaphoreType.DMA((n,)), pltpu.SemaphoreType.DMA((n,))],
    )(*gs)


def _join_halves_call(name, ts):
    n = len(ts)

    def body(*refs):
        o_refs = refs[n:2 * n]
        send_sems, recv_sems = refs[2 * n:]
        x, y, c = _place()
        copies = [pltpu.make_async_remote_copy(
            src_ref=o_refs[t].at[c], dst_ref=o_refs[t].at[c], send_sem=send_sems.at[t], recv_sem=recv_sems.at[t],
            device_id=(x, y, 1 - c), device_id_type=MESH) for t in range(n)]
        for cp in copies:
            cp.start()
        for t in range(n):
            copies[t].wait_send()
            pltpu.make_async_remote_copy(
                src_ref=o_refs[t].at[c], dst_ref=o_refs[t].at[1 - c], send_sem=send_sems.at[t],
                recv_sem=recv_sems.at[t], device_id=(x, y, 1 - c), device_id_type=MESH).wait_recv()

    return pl.pallas_call(
        body, name=name,
        out_shape=[jax.ShapeDtypeStruct(t.shape, t.dtype) for t in ts],
        in_specs=[ANY] * n, out_specs=[ANY] * n, input_output_aliases={t: t for t in range(n)},
        scratch_shapes=[pltpu.SemaphoreType.DMA((n,)), pltpu.SemaphoreType.DMA((n,))],
    )(*ts)


def _row_block(rows, cols, budget):
    best = 8
    for rb in range(8, rows + 1, 8):
        if rows % rb == 0 and rb * cols * 4 <= budget:
            best = rb
    return best


def _add_half_call(name, g, b, ck):
    _, _, r, cc = g.shape
    rb = _row_block(r, cc, 2 * 1024 * 1024)

    def body(ck_ref, g_ref, b_ref, o_ref):
        o_ref[...] = (g_ref[...] + b_ref[...]).astype(BF16)

    return pl.pallas_call(
        body, name=name,
        grid_spec=pltpu.PrefetchScalarGridSpec(
            num_scalar_prefetch=1, grid=(4, r // rb),
            in_specs=[pl.BlockSpec((None, None, rb, cc), lambda k, i, ck_ref: (ck_ref[0], k, i, 0)),
                      pl.BlockSpec((None, rb, cc), lambda k, i, ck_ref: (k, i, 0))],
            out_specs=pl.BlockSpec((None, rb, cc), lambda k, i, ck_ref: (k, i, 0))),
        out_shape=jax.ShapeDtypeStruct(b.shape, BF16),
        compiler_params=_cparams(2),
    )(ck, g, b)


def _add_rows_call(name, g, b, ck):
    _, r, cc = g.shape
    rb = _row_block(r, cc, 2 * 1024 * 1024)

    def body(ck_ref, g_ref, b_ref, o_ref):
        o_ref[...] = (g_ref[...] + b_ref[...]).astype(BF16)

    return pl.pallas_call(
        body, name=name,
        grid_spec=pltpu.PrefetchScalarGridSpec(
            num_scalar_prefetch=1, grid=(r // rb,),
            in_specs=[pl.BlockSpec((None, rb, cc), lambda i, ck_ref: (ck_ref[0], i, 0)),
                      pl.BlockSpec((rb, cc), lambda i, ck_ref: (i, 0))],
            out_specs=pl.BlockSpec((rb, cc), lambda i, ck_ref: (i, 0))),
        out_shape=jax.ShapeDtypeStruct((r, cc), BF16),
        compiler_params=_cparams(1),
    )(ck, g, b)


def _add_window_call(name, g, b, p, ck, comm):
    _, r, _ = g.shape
    nb, step = WIN_W // 128, WIN_STEP // 128
    n_xc = len(comm.srcs)

    def body(ck_ref, g_ref, b_ref, p0_ref, p1_ref, p2_ref, *rest):
        o_ref = rest[n_xc]
        i = pl.program_id(0)
        begin, finish = comm.make(rest[:n_xc], rest[n_xc + 1:2 * n_xc + 1], rest[-2], rest[-1])
        pl.when(i == 0)(begin)
        own = g_ref[...] + b_ref[...]
        o_ref[...] = ((own + p0_ref[...].astype(F32)) + p1_ref[...].astype(F32)) + p2_ref[...].astype(F32)
        pl.when(i == nb - 1)(finish)

    def peer(j):
        return pl.BlockSpec((None, r, 128), lambda i, ck_ref: (j, 0, i))

    return pl.pallas_call(
        body, name=name,
        grid_spec=pltpu.PrefetchScalarGridSpec(
            num_scalar_prefetch=1, grid=(nb,),
            in_specs=[pl.BlockSpec((None, r, 128), lambda i, ck_ref: (ck_ref[0], 0, step * ck_ref[1] + i)),
                      pl.BlockSpec((r, 128), lambda i, ck_ref: (0, step * ck_ref[1] + i)),
                      peer(0), peer(1), peer(2)] + [ANY] * n_xc,
            out_specs=[pl.BlockSpec((None, r, 128), lambda i, ck_ref: (ck_ref[0], 0, i))] + [ANY] * n_xc,
            scratch_shapes=_comm_sems(comm)),
        out_shape=[jax.ShapeDtypeStruct((2, r, WIN_W), F32)] + list(comm.out_shapes),
        compiler_params=_cparams(1),
    )(ck, g, b, p, p, p, *comm.srcs)


def _add_chips_call(name, g, b, p, ck):
    _, _, r, cc = g.shape
    rb = _row_block(r, cc, 2 * 1024 * 1024)

    def body(ck_ref, g_ref, b_ref, p0_ref, p1_ref, p2_ref, o_ref):
        own = g_ref[...] + b_ref[...]
        o_ref[...] = ((own + p0_ref[...].astype(F32)) + p1_ref[...].astype(F32)) + p2_ref[...].astype(F32)

    def peer(j):
        return pl.BlockSpec((None, rb, cc), lambda i, ck_ref: (j, i, 0))

    return pl.pallas_call(
        body, name=name,
        grid_spec=pltpu.PrefetchScalarGridSpec(
            num_scalar_prefetch=1, grid=(r // rb,),
            in_specs=[pl.BlockSpec((None, None, rb, cc), lambda i, ck_ref: (ck_ref[0], ck_ref[1], i, 0)),
                      pl.BlockSpec((None, rb, cc), lambda i, ck_ref: (ck_ref[1], i, 0)),
                      peer(0), peer(1), peer(2)],
            out_specs=pl.BlockSpec((None, rb, cc), lambda i, ck_ref: (ck_ref[0], i, 0))),
        out_shape=jax.ShapeDtypeStruct((2, r, cc), F32),
        compiler_params=_cparams(1),
    )(ck, g, b, p, p, p)


def _sum8_call(name, g):
    def body(g_ref, o_ref):
        acc = g_ref[0]
        for d in range(1, 8):
            acc = acc + g_ref[d]
        o_ref[...] = acc

    return pl.pallas_call(body, name=name, out_shape=jax.ShapeDtypeStruct(g.shape[1:], F32))(g)


def _adamw_call(name, w, g, m, v):
    r, cc = w.shape
    if r % 8 == 0 or r * cc * 4 <= 1024 * 1024:
        rb = _row_block(r, cc, 1024 * 1024) if r % 8 == 0 else r
        grid, spec = (r // rb,), pl.BlockSpec((rb, cc), lambda i: (i, 0))
    else:
        grid, spec = (cc // 128,), pl.BlockSpec((r, 128), lambda i: (0, i))

    def body(w_ref, g_ref, m_ref, v_ref, d_ref, m2_ref, v2_ref):
        gv = g_ref[...]
        m2 = ADAM_B1 * m_ref[...] + (1.0 - ADAM_B1) * gv
        v2 = ADAM_B2 * v_ref[...] + (1.0 - ADAM_B2) * (gv * gv)
        m_hat = m2 / (1.0 - ADAM_B1 ** ADAM_STEP)
        v_hat = v2 / (1.0 - ADAM_B2 ** ADAM_STEP)
        d_ref[...] = -ADAM_LR * (m_hat / (jnp.sqrt(v_hat) + ADAM_EPS) + ADAM_WD * w_ref[...])
        m2_ref[...] = m2
        v2_ref[...] = v2

    return pl.pallas_call(
        body, name=name, grid=grid, in_specs=[spec] * 4, out_specs=[spec] * 3,
        out_shape=[jax.ShapeDtypeStruct((r, cc), F32)] * 3, compiler_params=_cparams(1),
    )(w, g, m, v)


SMALL = (("norm_gain", D_MODEL), ("b_gate", GLA_KW), ("ret_norm_gain", RET_W), ("gla_norm_gain", GLA_W),
         ("final_norm_gain", D_MODEL), ("w_gate_up", GATE_RANK * GLA_KW), ("meta_tokens", N_META * D_MODEL),
         ("loss", 1))


def _pack_rows(vecs, rows):
    flat = jnp.concatenate([v.reshape(-1) for v in vecs])
    return jnp.pad(flat, (0, rows * 128 - flat.shape[0])).reshape(rows, 128)


def kernel(x, meta_tokens, norm_gain, w_in, w_gate_up, b_gate, ret_norm_gain, gla_norm_gain, w_branch_ret, w_branch_gla, w_out, final_norm_gain, loss_target, m_meta_tokens, m_norm_gain, m_w_in, m_w_gate_up, m_b_gate, m_ret_norm_gain, m_gla_norm_gain, m_w_branch_ret, m_w_branch_gla, m_w_out, m_final_norm_gain, v_meta_tokens, v_norm_gain, v_w_in, v_w_gate_up, v_b_gate, v_ret_norm_gain, v_gla_norm_gain, v_w_branch_ret, v_w_branch_gla, v_w_out, v_final_norm_gain):
    xi, yi, ci = _place()
    kme = 2 * xi + yi
    ck = jnp.stack([ci, kme]).astype(jnp.int32)
    sw_in = w_in.shape[2]

    def my_half(a, dtype):
        r, cc = a.shape
        return lax.dynamic_index_in_dim(a.reshape(2, r // 2, cc), ci, 0, keepdims=False).astype(dtype)

    g_meta, g_wg = _gather8_call("gather_small_weights", [my_half(meta_tokens, F32), my_half(w_gate_up[0], F32)])
    branch_parts = [my_half(w_branch_ret[0], BF16), my_half(w_branch_gla[0], BF16), my_half(w_out[0], BF16)]
    meta = g_meta.reshape(4, 2, N_META // 2, D_MODEL // 4).transpose(1, 2, 0, 3).reshape(N_META, D_MODEL)
    wg_full = g_wg.reshape(4, 2, GATE_RANK // 2, GLA_KW // 4).transpose(1, 2, 0, 3).reshape(GATE_RANK, GLA_KW)

    w_in_part = lax.dynamic_slice_in_dim(w_in[0].T, ci * (D_MODEL // 2), D_MODEL // 2, axis=1).astype(BF16)
    loc = _device_step(x[0], loss_target[0], meta, norm_gain, w_in_part, wg_full, b_gate, ret_norm_gain,
                       gla_norm_gain,
                       branch_parts, final_norm_gain, ck)
    names = ("w_in", "w_branch_ret", "w_branch_gla", "w_out")
    full = [loc[nm] for nm in names]
    big_w = dict(w_in=w_in[0], w_branch_ret=w_branch_ret[0], w_branch_gla=w_branch_gla[0], w_out=w_out[0])
    big_m = dict(w_in=m_w_in[0], w_branch_ret=m_w_branch_ret[0], w_branch_gla=m_w_branch_gla[0], w_out=m_w_out[0])
    big_v = dict(w_in=v_w_in[0], w_branch_ret=v_w_branch_ret[0], w_branch_gla=v_w_branch_gla[0], w_out=v_w_out[0])
    grads, deltas, new_m, new_v = {}, {}, {}, {}
    for nm, f in zip(names, full):
        shape = big_w[nm].shape
        if nm == "w_in":
            f = lax.dynamic_slice_in_dim(f, (sw_in - WIN_STEP) * kme, sw_in, axis=2)
        g = f.reshape(shape)
        if nm == "w_in":
            d, m2, v2 = (a.T for a in _adamw_call("adamw_" + nm, big_w[nm].T, g.T, big_m[nm].T, big_v[nm].T))
        else:
            d, m2, v2 = _adamw_call("adamw_" + nm, big_w[nm], g, big_m[nm], big_v[nm])
        grads[nm], deltas[nm], new_m[nm], new_v[nm] = (a.reshape((1,) + shape) for a in (g, d, m2, v2))

    tot = _sum8_call("sum_small_grads", loc["small"]).reshape(-1)
    off = 0
    sg = {}
    for nm, sz in SMALL:
        sg[nm] = tot[off:off + sz]
        off += sz
    loss = sg.pop("loss")[0]
    sg["w_gate_up"] = lax.dynamic_slice_in_dim(sg["w_gate_up"].reshape(GATE_RANK, GLA_KW), kme * (GLA_KW // 4),
                                               GLA_KW // 4, axis=1)
    sg["meta_tokens"] = lax.dynamic_slice_in_dim(sg["meta_tokens"].reshape(N_META, D_MODEL), kme * (D_MODEL // 4),
                                                 D_MODEL // 4, axis=1)
    small_w = dict(norm_gain=norm_gain, b_gate=b_gate, ret_norm_gain=ret_norm_gain, gla_norm_gain=gla_norm_gain,
                   final_norm_gain=final_norm_gain, w_gate_up=w_gate_up, meta_tokens=meta_tokens)
    small_m = dict(norm_gain=m_norm_gain, b_gate=m_b_gate, ret_norm_gain=m_ret_norm_gain,
                   gla_norm_gain=m_gla_norm_gain, final_norm_gain=m_final_norm_gain, w_gate_up=m_w_gate_up,
                   meta_tokens=m_meta_tokens)
    small_v = dict(norm_gain=v_norm_gain, b_gate=v_b_gate, ret_norm_gain=v_ret_norm_gain,
                   gla_norm_gain=v_gla_norm_gain, final_norm_gain=v_final_norm_gain, w_gate_up=v_w_gate_up,
                   meta_tokens=v_meta_tokens)
    for nm in small_w:
        shape = small_w[nm].shape
        as2d = lambda a: a.reshape((-1, shape[-1]))
        grads[nm] = sg[nm].reshape(shape)
        deltas[nm], new_m[nm], new_v[nm] = (a.reshape(shape) for a in _adamw_call(
            "adamw_" + nm, as2d(small_w[nm]), as2d(sg[nm]), as2d(small_m[nm]), as2d(small_v[nm])))

    out_order = ("meta_tokens", "norm_gain", "w_in", "w_gate_up", "b_gate", "ret_norm_gain", "gla_norm_gain",
                 "w_branch_ret", "w_branch_gla", "w_out", "final_norm_gain")
    dx = loc["dx"].reshape(x.shape)
    return (loss, dx, *[grads[nm] for nm in out_order], *[deltas[nm] for nm in out_order],
            *[new_m[nm] for nm in out_order], *[new_v[nm] for nm in out_order])
```

```python
import math
from typing import Callable, NamedTuple

import numpy as np
import jax
import jax.numpy as jnp
from jax import lax
from jax.experimental import pallas as pl
from jax.experimental.pallas import tpu as pltpu

F32 = jnp.float32
BF16 = jnp.bfloat16

D_MODEL = 1024
N_META = 16
EPS = 1e-6
ROPE_BASE = 10000.0
RET_HEADS, RET_QK, RET_V = 4, 256, 512
RET_W = RET_HEADS * RET_V
GLA_HEADS, GLA_K, GLA_V = 4, 128, 256
GLA_W = GLA_HEADS * GLA_V
GLA_KW = GLA_HEADS * GLA_K
GATE_RANK = 16
GATE_TAU = 16.0
GLA_SUB = 16

TM = 256
T0 = TM
PADF = T0 - N_META
GC = 128
GS = 3
TB = 768
TK = 768

W_R = 6144
W_G = 3088
W_GP = 3200
W_M = 2048
IN_COLS = W_R + W_G + W_M
WIN_STEP = (IN_COLS // 4) // 128 * 128
WIN_W = -(-(3 * (IN_COLS // 4 - WIN_STEP) + IN_COLS // 4) // 128) * 128
IN_PAD = 3 * WIN_STEP + WIN_W

ADAM_LR, ADAM_B1, ADAM_B2, ADAM_EPS, ADAM_WD, ADAM_STEP = 0.001, 0.9, 0.999, 1e-08, 0.01, 10

VMEM_LIMIT = 56 * 1024 * 1024

NN = ((1,), (0,))
NT = ((1,), (1,))
TN = ((0,), (0,))


def _dot(a, b, dims):
    return lax.dot_general(a, b, (dims, ((), ())), preferred_element_type=F32)


def _cparams(n_axes):
    return pltpu.CompilerParams(dimension_semantics=("arbitrary",) * n_axes, vmem_limit_bytes=VMEM_LIMIT)


def _sigmoid(x):
    return 0.5 * jnp.tanh(0.5 * x) + 0.5


def _silu(x):
    h = 0.5 * x
    return h + h * jnp.tanh(h)


def _head_mean(x):
    return jnp.mean(x, axis=-1, keepdims=True)


def _split3(x):
    hi = x.astype(BF16)
    r1 = x - hi.astype(F32)
    mid = r1.astype(BF16)
    lo = (r1 - mid.astype(F32)).astype(BF16)
    return hi, mid, lo


def _exact_pm(p, x):
    hi, mid, lo = _split3(x)
    return _dot(p, hi, NN) + _dot(p, mid, NN) + _dot(p, lo, NN)


def _rms_call(x2d, head, gain, comm):
    tp = T0 + x2d.shape[0]
    nt = tp // TM
    n_xc = len(comm.srcs)

    def body(x_ref, hd_ref, g_ref, *rest):
        xc_src = rest[:n_xc]
        h_ref, u_ref = rest[n_xc:n_xc + 2]
        xc_dst = rest[n_xc + 2:2 * n_xc + 2]
        i = pl.program_id(0)
        begin, finish = comm.make(xc_src, xc_dst, rest[-2], rest[-1])
        pl.when(i == 0)(begin)
        h = jnp.where(i == 0, hd_ref[...], x_ref[...])
        h_ref[...] = h
        r = lax.rsqrt(jnp.mean(h * h, axis=-1, keepdims=True) + EPS)
        u_ref[...] = (h * r * g_ref[...]).astype(BF16)
        pl.when(i == nt - 1)(finish)

    tile = pl.BlockSpec((TM, D_MODEL), lambda i: (i, 0))
    return pl.pallas_call(
        body, name="rms_in", grid=(nt,),
        in_specs=[pl.BlockSpec((TM, D_MODEL), lambda i: (jnp.maximum(i - 1, 0), 0)),
                  pl.BlockSpec((T0, D_MODEL), lambda i: (0, 0)), pl.BlockSpec((1, D_MODEL), lambda i: (0, 0))]
        + [ANY] * n_xc,
        out_specs=[tile, tile] + [ANY] * n_xc,
        out_shape=[jax.ShapeDtypeStruct((tp, D_MODEL), F32), jax.ShapeDtypeStruct((tp, D_MODEL), BF16)]
        + list(comm.out_shapes),
        scratch_shapes=_comm_sems(comm), compiler_params=_cparams(1),
    )(x2d, head, gain, *comm.srcs)


PROJ_ROWS_MAX = 1408


def _proj_rows(m):
    return max(r for r in range(16, PROJ_ROWS_MAX + 1, 16) if m % r == 0)


def _mm_nn(name, a, bt, out_dtype, tn, col0, ncols, epilogue=None, extras=(), extra_specs=()):
    m, k = a.shape
    nj, j0 = ncols // tn, col0 // tn
    tb = _proj_rows(m)

    def body(a_ref, b_ref, *rest):
        *ex, o_ref = rest
        acc = _dot(a_ref[...], b_ref[...], NT)
        if epilogue is None:
            o_ref[...] = acc.astype(out_dtype)
        else:
            epilogue(acc, o_ref, *ex)

    return pl.pallas_call(
        body, name=name, grid=(nj, m // tb),
        in_specs=[pl.BlockSpec((tb, k), lambda j, i: (i, 0)), pl.BlockSpec((tn, k), lambda j, i: (j0 + j, 0))]
        + list(extra_specs),
        out_specs=pl.BlockSpec((tb, tn), lambda j, i: (i, j)),
        out_shape=jax.ShapeDtypeStruct((m, ncols), out_dtype),
        compiler_params=_cparams(2),
    )(a, bt, *extras)


def _rope_tables(tp):
    half = RET_QK // 2
    pos = np.arange(tp, dtype=np.float32) - np.float32(PADF)
    inv = (ROPE_BASE ** (-np.arange(half, dtype=np.float64) / half)).astype(np.float32)
    ang = (pos[:, None] * inv[None, :]).astype(np.float64)
    return np.cos(ang).astype(np.float32), np.sin(ang).astype(np.float32)


def _rope_epilogue(acc, o_ref, cos_ref, sin_ref):
    scale = jnp.where(pl.program_id(0) == 1, RET_QK ** -0.5, 1.0).astype(F32)
    cos, sin = cos_ref[...], sin_ref[...]
    half = RET_QK // 2
    for h in range(RET_HEADS):
        t1 = acc[:, h * RET_QK:h * RET_QK + half]
        t2 = acc[:, h * RET_QK + half:(h + 1) * RET_QK]
        o_ref[:, h * RET_QK:h * RET_QK + half] = ((t1 * cos - t2 * sin) * scale).astype(BF16)
        o_ref[:, h * RET_QK + half:(h + 1) * RET_QK] = ((t2 * cos + t1 * sin) * scale).astype(BF16)


def _gqk_epilogue(acc, o_ref):
    o_ref[:, :GLA_KW] = acc[:, :GLA_KW] * (GLA_K ** -0.5)
    o_ref[:, GLA_KW:] = acc[:, GLA_KW:]


class _Comm(NamedTuple):
    srcs: tuple
    out_shapes: tuple
    n_sems: int
    make: Callable


def _comm_sems(comm):
    return [pltpu.SemaphoreType.DMA((comm.n_sems,)), pltpu.SemaphoreType.DMA((comm.n_sems,))]


def _start_wait(copies):
    def begin():
        for cp in copies:
            cp.start()

    def finish():
        for cp in copies:
            cp.wait()

    return begin, finish


def _other_chips(x, y):
    return [(1 - x, y), (x, 1 - y), (1 - x, 1 - y)]


def _gather_plan(parts, relay=()):
    n = len(parts)
    relay = tuple(relay) + (False,) * (n - len(relay))

    def make(x_refs, out_refs, send_sems, recv_sems):
        x, y, c = _place()
        me, sibling = (x, y, c), (x, y, 1 - c)
        xn, yn, dg = (1 - x, y), (x, 1 - y), (1 - x, 1 - y)

        def slot(t, px, py, pc, half=None):
            ref = out_refs[t].at[4 * px + 2 * py + pc]
            if half is None:
                return ref
            cols = ref.shape[-1] // 2
            return ref.at[:, pl.ds(half * cols, cols)]

        def copy(t, k, dst, to, src=None):
            return pltpu.make_async_remote_copy(
                src_ref=dst if src is None else src, dst_ref=dst, send_sem=send_sems.at[8 * t + k],
                recv_sem=recv_sems.at[8 * t + k], device_id=to, device_id_type=MESH)

        mine = [pltpu.make_async_copy(x_refs[t], slot(t, *me), send_sems.at[8 * n + t]) for t in range(n)]
        sent = []
        for t in range(n):
            sent.append(copy(t, 0, slot(t, *me), sibling, src=x_refs[t]))
            sent.append(copy(t, 1, slot(t, *me), (*xn, c), src=x_refs[t]))
            sent.append(copy(t, 2, slot(t, *me), (*yn, c), src=x_refs[t]))
            if not relay[t]:
                sent.append(copy(t, 3, slot(t, *me), (*dg, c), src=x_refs[t]))

        def begin():
            for cp in mine + sent:
                cp.start()

        def finish():
            later = []

            def start(cp):
                cp.start()
                later.append(cp)

            for t in range(n):
                copy(t, 2, slot(t, *yn, c), me).wait_recv()
                if relay[t]:
                    start(copy(t, 3, slot(t, *yn, c, half=0), (*xn, c)))
                start(copy(t, 6, slot(t, *yn, c), sibling))
            for t in range(n):
                copy(t, 1, slot(t, *xn, c), me).wait_recv()
                if relay[t]:
                    start(copy(t, 4, slot(t, *xn, c, half=1), (*yn, c)))
                start(copy(t, 5, slot(t, *xn, c), sibling))
            for t in range(n):
                if relay[t]:
                    copy(t, 3, slot(t, *dg, c, half=0), me).wait_recv()
                    copy(t, 4, slot(t, *dg, c, half=1), me).wait_recv()
                else:
                    copy(t, 3, slot(t, *dg, c), me).wait_recv()
                start(copy(t, 7, slot(t, *dg, c), sibling))
            for t in range(n):
                copy(t, 0, slot(t, *sibling), me).wait_recv()
                copy(t, 5, slot(t, *xn, 1 - c), me).wait_recv()
                copy(t, 6, slot(t, *yn, 1 - c), me).wait_recv()
                copy(t, 7, slot(t, *dg, 1 - c), me).wait_recv()
            for cp in sent + later:
                cp.wait_send()
            for cp in mine:
                cp.wait()

        return begin, finish

    return _Comm(tuple(parts), tuple(jax.ShapeDtypeStruct((8,) + p.shape, p.dtype) for p in parts), 9 * n, make)


def _exchange_plan(ss):
    def make(s_refs, b_refs, send_sems, recv_sems):
        x, y, c = _place()
        return _start_wait([pltpu.make_async_remote_copy(
            src_ref=s_refs[t].at[2 * chip[0] + chip[1]], dst_ref=b_refs[t].at[j], send_sem=send_sems.at[3 * t + j],
            recv_sem=recv_sems.at[3 * t + j], device_id=(*chip, c), device_id_type=MESH)
            for t in range(len(s_refs)) for j, chip in enumerate(_other_chips(x, y))])

    return _Comm(tuple(ss), tuple(jax.ShapeDtypeStruct((3,) + s.shape[1:], s.dtype) for s in ss), 3 * len(ss), make)


def _exchange_window_plan(s):
    def make(s_refs, b_refs, send_sems, recv_sems):
        x, y, c = _place()
        return _start_wait([pltpu.make_async_remote_copy(
            src_ref=s_refs[0].at[:, pl.ds(pl.multiple_of((2 * chip[0] + chip[1]) * WIN_STEP, 128), WIN_W)],
            dst_ref=b_refs[0].at[j], send_sem=send_sems.at[j], recv_sem=recv_sems.at[j], device_id=(*chip, c),
            device_id_type=MESH) for j, chip in enumerate(_other_chips(x, y))])

    return _Comm((s,), (jax.ShapeDtypeStruct((3, s.shape[0], WIN_W), s.dtype),), 3, make)


def _swap_plan(gs):
    def make(g_refs, b_refs, send_sems, recv_sems):
        x, y, c = _place()
        return _start_wait([pltpu.make_async_remote_copy(
            src_ref=g_refs[t].at[1 - c], dst_ref=b_refs[t], send_sem=send_sems.at[t], recv_sem=recv_sems.at[t],
            device_id=(x, y, 1 - c), device_id_type=MESH) for t in range(len(g_refs))])

    return _Comm(tuple(gs), tuple(jax.ShapeDtypeStruct(g.shape[1:], g.dtype) for g in gs), len(gs), make)


def _spread_plan(parts):
    def make(p_refs, o_refs, send_sems, recv_sems):
        x, y, c = _place()
        copies = []
        for t in range(len(p_refs)):
            mine = o_refs[t].at[4 * x + 2 * y + c]
            copies.append(pltpu.make_async_copy(p_refs[t], mine, send_sems.at[7 * len(p_refs) + t]))
            for r in range(1, 8):
                peer = (1 - x if r & 4 else x, 1 - y if r & 2 else y, 1 - c if r & 1 else c)
                copies.append(pltpu.make_async_remote_copy(
                    src_ref=p_refs[t], dst_ref=mine, send_sem=send_sems.at[7 * t + r - 1],
                    recv_sem=recv_sems.at[7 * t + r - 1], device_id=peer, device_id_type=MESH))
        return _start_wait(copies)

    return _Comm(tuple(parts), tuple(jax.ShapeDtypeStruct((8,) + p.shape, p.dtype) for p in parts), 8 * len(parts),
                 make)


def _mm_nt_acc(name, a, w, tk, acc_in=None, epilogue=None, extras=(), extra_specs=(), extra_out_shapes=(),
               extra_out_specs=(), extra_scratch=(), comm=None, tb=TB):
    m, k = a.shape
    n = w.shape[1]
    nk, ni = k // tk, m // tb
    has_acc = acc_in is not None
    n_xc = len(comm.srcs) if comm else 0
    n_es = len(extra_scratch)

    def body(*refs):
        a_ref, w_ref = refs[0], refs[1]
        pos = 2
        acc_ref = None
        if has_acc:
            acc_ref = refs[pos]
            pos += 1
        ex = refs[pos:pos + len(extras)]
        pos += len(extras)
        xc_src = refs[pos:pos + n_xc]
        pos += n_xc
        n_scr = 1 + n_es + (2 if n_xc else 0)
        outs = refs[pos:len(refs) - n_scr - n_xc]
        xc_dst = refs[len(refs) - n_scr - n_xc:len(refs) - n_scr]
        scr = refs[len(refs) - n_scr]
        es = refs[len(refs) - n_scr + 1:len(refs) - n_scr + 1 + n_es]
        i, kk = pl.program_id(0), pl.program_id(1)
        if n_xc:
            begin, finish = comm.make(xc_src, xc_dst, refs[-2], refs[-1])
            pl.when((i == 0) & (kk == 0))(begin)

        @pl.when(kk == 0)
        def _():
            scr[...] = acc_ref[...] if has_acc else jnp.zeros_like(scr)

        scr[...] += _dot(a_ref[...], w_ref[...], NN)

        @pl.when(kk == nk - 1)
        def _():
            if epilogue is None:
                outs[0][...] = scr[...]
            else:
                epilogue(scr[...], outs, i, ni, *ex, *es)

        if n_xc:
            pl.when((i == ni - 1) & (kk == nk - 1))(finish)

    in_specs = [pl.BlockSpec((tb, tk), lambda i, kk: (i, kk)), pl.BlockSpec((tk, n), lambda i, kk: (kk, 0))]
    args = [a, w]
    if has_acc:
        in_specs.append(pl.BlockSpec((tb, n), lambda i, kk: (i, 0)))
        args.append(acc_in)
    in_specs += list(extra_specs) + [ANY] * n_xc
    args += list(extras) + (list(comm.srcs) if comm else [])
    if epilogue is None:
        out_shape = [jax.ShapeDtypeStruct((m, n), F32)]
        out_specs = [pl.BlockSpec((tb, n), lambda i, kk: (i, 0))]
    else:
        out_shape, out_specs = list(extra_out_shapes), list(extra_out_specs)
    scratch = [pltpu.VMEM((tb, n), F32)] + list(extra_scratch)
    if n_xc:
        out_shape += list(comm.out_shapes)
        out_specs += [ANY] * n_xc
        scratch += _comm_sems(comm)
    return pl.pallas_call(
        body, name=name, grid=(ni, nk), in_specs=in_specs, out_specs=out_specs, out_shape=out_shape,
        scratch_shapes=scratch, compiler_params=_cparams(2),
    )(*args)


def _rms_bwd_epilogue(du, outs, i, ni, h_ref, g_ref, dh1_ref, obuf, sems):
    dx_ref, dmeta_ref, dg_ref = outs
    h = h_ref[...]
    r = lax.rsqrt(jnp.mean(h * h, axis=-1, keepdims=True) + EPS)
    xh = h * r
    dxh = du * g_ref[...]
    dh0 = dh1_ref[...] + r * (dxh - xh * jnp.mean(dxh * xh, axis=-1, keepdims=True))

    def put(slot, tile):
        return pltpu.make_async_copy(obuf.at[slot], dx_ref.at[pl.ds(pl.multiple_of(tile * TB - T0, 8), TB)],
                                     sems.at[slot])

    @pl.when(i == 0)
    def _():
        dg_ref[...] = jnp.zeros_like(dg_ref)
        dmeta_ref[...] = dh0[PADF:T0, :]
        obuf[0] = dh0
        first = pltpu.make_async_copy(obuf.at[0, pl.ds(T0, TB - T0)], dx_ref.at[pl.ds(0, TB - T0)], sems.at[0])
        first.start()
        first.wait()

    @pl.when(i >= 1)
    def _():
        slot = i % 2

        @pl.when(i >= 3)
        def _():
            put(slot, i - 2).wait()

        obuf[slot] = dh0
        put(slot, i).start()

    dg_ref[...] += jnp.sum(du * xh, axis=0, keepdims=True)

    @pl.when(i == ni - 1)
    def _():
        for tile in (ni - 2, ni - 1):
            if tile >= 1:
                put(tile % 2, tile).wait()


def _mm_tn(name, a, b, bn, ncols=None, bcol0=0, into=None, col0=0, out_cols=None):
    t, m = a.shape
    n = ncols or b.shape[1]
    j0, bj0 = col0 // bn, bcol0 // bn

    def body(a_ref, b_ref, *rest):
        o_ref = rest[-1]

        @pl.when(pl.program_id(1) == 0)
        def _():
            o_ref[...] = jnp.zeros_like(o_ref)

        o_ref[...] += _dot(a_ref[...], b_ref[...], TN)

    in_specs = [pl.BlockSpec((TK, m), lambda j, kk: (kk, 0)), pl.BlockSpec((TK, bn), lambda j, kk: (kk, bj0 + j))]
    args = [a, b]
    aliases = {}
    if into is not None:
        in_specs.append(ANY)
        args.append(into)
        aliases = {2: 0}
        out_cols = into.shape[1]
    return pl.pallas_call(
        body, name=name, grid=(n // bn, t // TK), in_specs=in_specs,
        out_specs=pl.BlockSpec((m, bn), lambda j, kk: (0, j0 + j)),
        out_shape=jax.ShapeDtypeStruct((m, out_cols or n), F32), input_output_aliases=aliases,
        compiler_params=_cparams(2),
    )(*args)


def _place_merge_cols_call(dwp, dw_m, dw_glr):
    c0 = W_R + W_GP - 128
    tail = IN_PAD - c0
    rows = 256

    def body(m_ref, low, p_ref, o_ref, buf, sem):
        for r in range(0, D_MODEL, rows):
            buf[r:r + rows, :] = jnp.concatenate(
                [low[r:r + rows, :GATE_RANK], m_ref[r:r + rows, :],
                 jnp.zeros((rows, tail - GATE_RANK - W_M), F32)], axis=1)
        put = pltpu.make_async_copy(buf, o_ref.at[:, pl.ds(c0, tail)], sem)
        put.start()
        put.wait()

    return pl.pallas_call(
        body, name="place_merge_cols",
        in_specs=[pl.BlockSpec(memory_space=pltpu.VMEM), pl.BlockSpec(memory_space=pltpu.VMEM), ANY], out_specs=ANY,
        out_shape=jax.ShapeDtypeStruct(dwp.shape, F32), input_output_aliases={2: 0},
        scratch_shapes=[pltpu.VMEM((D_MODEL, tail), F32), pltpu.SemaphoreType.DMA],
        compiler_params=pltpu.CompilerParams(vmem_limit_bytes=VMEM_LIMIT),
    )(dw_m, dw_glr, dwp)


def _ret_fill_decay(lg_ref, dm_scr):
    c = TM
    ii = lax.broadcasted_iota(jnp.int32, (c, c), 0)
    jj = lax.broadcasted_iota(jnp.int32, (c, c), 1)
    rel = (ii - jj).astype(F32)
    for h in range(RET_HEADS):
        dm_scr[h] = jnp.where(rel >= 0, jnp.exp(jnp.maximum(rel, 0.0) * lg_ref[h]), 0.0)


def _ret_consts(lg, dm_ref):
    c = TM
    idx = lax.broadcasted_iota(jnp.int32, (c, 1), 0).astype(F32)
    xi = jnp.exp((idx + 1.0) * lg)
    zeta = jnp.exp((c - 1.0 - idx) * lg)
    gc = jnp.exp(jnp.full((1, 1), c, F32) * lg)
    return dm_ref[...], xi, zeta, gc


def _ret_fwd_call(rqk, rv, rg, gain, lgam):
    tp = rqk.shape[0]
    nc = tp // TM

    def body(lg_ref, qk_ref, v_ref, rg_ref, g_ref, o_ref, a_ref, st_ref, sc_ref, s_scr, dm_scr):
        @pl.when(pl.program_id(0) == 0)
        def _():
            s_scr[...] = jnp.zeros_like(s_scr)
            _ret_fill_decay(lg_ref, dm_scr)

        for h in range(RET_HEADS):
            dm, xi, zeta, gc = _ret_consts(lg_ref[h], dm_scr.at[h])
            q = qk_ref[:, h * RET_QK:(h + 1) * RET_QK]
            k = qk_ref[:, D_MODEL + h * RET_QK:D_MODEL + (h + 1) * RET_QK]
            v = v_ref[:, h * RET_V:(h + 1) * RET_V]
            sb = s_scr[h].astype(BF16)
            st_ref[0, h] = sb
            s = (_dot(q, k, NT) * dm).astype(BF16)
            sc_ref[0, h] = s
            o = _dot(s, v, NN) + xi * _dot(q, sb, NN)
            kz = (k.astype(F32) * zeta).astype(BF16)
            s_scr[h] = gc * s_scr[h] + _dot(kz, v, TN)
            o_ref[:, h * RET_V:(h + 1) * RET_V] = o
            mu = _head_mean(o)
            xc = o - mu
            xh = xc * lax.rsqrt(_head_mean(xc * xc) + EPS)
            a_ref[:, h * RET_V:(h + 1) * RET_V] = (
                xh * g_ref[:, h * RET_V:(h + 1) * RET_V] * _silu(rg_ref[:, h * RET_V:(h + 1) * RET_V])).astype(BF16)

    return pl.pallas_call(
        body, name="ret_fwd", grid=(nc,),
        in_specs=[pl.BlockSpec(memory_space=pltpu.SMEM),
                  pl.BlockSpec((TM, 2 * D_MODEL), lambda n: (n, 0)),
                  pl.BlockSpec((TM, RET_W), lambda n: (n, 0)),
                  pl.BlockSpec((TM, RET_W), lambda n: (n, 0)),
                  pl.BlockSpec((1, RET_W), lambda n: (0, 0))],
        out_specs=[pl.BlockSpec((TM, RET_W), lambda n: (n, 0)),
                   pl.BlockSpec((TM, RET_W), lambda n: (n, 0)),
                   pl.BlockSpec((1, RET_HEADS, RET_QK, RET_V), lambda n: (n, 0, 0, 0)),
                   pl.BlockSpec((1, RET_HEADS, TM, TM), lambda n: (n, 0, 0, 0))],
        out_shape=[jax.ShapeDtypeStruct((tp, RET_W), F32), jax.ShapeDtypeStruct((tp, RET_W), BF16),
                   jax.ShapeDtypeStruct((nc, RET_HEADS, RET_QK, RET_V), BF16),
                   jax.ShapeDtypeStruct((nc, RET_HEADS, TM, TM), BF16)],
        scratch_shapes=[pltpu.VMEM((RET_HEADS, RET_QK, RET_V), F32), pltpu.VMEM((RET_HEADS, TM, TM), F32)],
        compiler_params=_cparams(1),
    )(lgam, rqk, rv, rg, gain)


def _ret_bwd_call(rqk, rv, rg, o_ret, dpr, wbr, states, scores, gain, lgam, cos, sin):
    tp = rqk.shape[0]
    nc = tp // TM
    half = RET_QK // 2

    def body(lg_ref, qk_ref, v_ref, rg_ref, o_ref, dpr_ref, wbr_ref, st_ref, sc_ref, g_ref, cos_ref, sin_ref, dp_ref,
             dg_ref, ds_scr, dm_scr):
        @pl.when(pl.program_id(0) == 0)
        def _():
            ds_scr[...] = jnp.zeros_like(ds_scr)
            dg_ref[...] = jnp.zeros_like(dg_ref)
            _ret_fill_decay(lg_ref, dm_scr)

        cos, sin = cos_ref[...], sin_ref[...]
        for h in range(RET_HEADS):
            hs = slice(h * RET_V, (h + 1) * RET_V)
            dm, xi, zeta, gc = _ret_consts(lg_ref[h], dm_scr.at[h])
            o = o_ref[:, hs]
            mu = _head_mean(o)
            xc = o - mu
            rstd = lax.rsqrt(_head_mean(xc * xc) + EPS)
            xh = xc * rstd
            gain_h = g_ref[:, hs]
            g = rg_ref[:, hs]
            sg = _sigmoid(g)
            silu = g * sg
            dah = _dot(dpr_ref[...], wbr_ref[hs, :], NT)
            dp_ref[:, 4 * D_MODEL + h * RET_V:4 * D_MODEL + (h + 1) * RET_V] = (
                dah * (xh * gain_h) * (sg * (1.0 + g * (1.0 - sg)))).astype(BF16)
            dn = dah * silu
            dg_ref[:, hs] += jnp.sum(dn * xh, axis=0, keepdims=True)
            dxh = dn * gain_h
            do = rstd * (dxh - _head_mean(dxh) - xh * _head_mean(dxh * xh))
            dob = do.astype(BF16)
            q = qk_ref[:, h * RET_QK:(h + 1) * RET_QK]
            k = qk_ref[:, D_MODEL + h * RET_QK:D_MODEL + (h + 1) * RET_QK]
            v = v_ref[:, hs]
            sp = st_ref[0, h]
            ds = ds_scr[h]
            dsb = ds.astype(BF16)
            s = sc_ref[0, h]
            dsc = (_dot(dob, v, NT) * dm).astype(BF16)
            dq = _dot(dsc, k, NN) + xi * _dot(dob, sp, NT)
            dk = _dot(dsc, q, TN) + zeta * _dot(v, dsb, NT)
            kz = (k.astype(F32) * zeta).astype(BF16)
            dv = _dot(s, dob, TN) + _dot(kz, dsb, NN)
            qx = (q.astype(F32) * xi).astype(BF16)
            ds_scr[h] = gc * ds + _dot(qx, dob, TN)
            dp_ref[:, 2 * D_MODEL + h * RET_V:2 * D_MODEL + (h + 1) * RET_V] = dv.astype(BF16)
            dk = dk * (RET_QK ** -0.5)
            for base, t in ((0, dq), (D_MODEL, dk)):
                t1, t2 = t[:, :half], t[:, half:]
                dp_ref[:, base + h * RET_QK:base + h * RET_QK + half] = (t1 * cos + t2 * sin).astype(BF16)
                dp_ref[:, base + h * RET_QK + half:base + (h + 1) * RET_QK] = (t2 * cos - t1 * sin).astype(BF16)

    rev = lambda n: (nc - 1 - n, 0)
    return pl.pallas_call(
        body, name="ret_bwd", grid=(nc,),
        in_specs=[pl.BlockSpec(memory_space=pltpu.SMEM),
                  pl.BlockSpec((TM, 2 * D_MODEL), rev),
                  pl.BlockSpec((TM, RET_W), rev),
                  pl.BlockSpec((TM, RET_W), rev),
                  pl.BlockSpec((TM, RET_W), rev),
                  pl.BlockSpec((TM, D_MODEL), rev),
                  pl.BlockSpec((RET_W, D_MODEL), lambda n: (0, 0)),
                  pl.BlockSpec((1, RET_HEADS, RET_QK, RET_V), lambda n: (nc - 1 - n, 0, 0, 0)),
                  pl.BlockSpec((1, RET_HEADS, TM, TM), lambda n: (nc - 1 - n, 0, 0, 0)),
                  pl.BlockSpec((1, RET_W), lambda n: (0, 0)),
                  pl.BlockSpec((TM, half), rev),
                  pl.BlockSpec((TM, half), rev)],
        out_specs=[pl.BlockSpec((TM, W_R), rev), pl.BlockSpec((1, RET_W), lambda n: (0, 0))],
        out_shape=[jax.ShapeDtypeStruct((tp, W_R), BF16), jax.ShapeDtypeStruct((1, RET_W), F32)],
        scratch_shapes=[pltpu.VMEM((RET_HEADS, RET_QK, RET_V), F32), pltpu.VMEM((RET_HEADS, TM, TM), F32)],
        compiler_params=_cparams(1),
    )(lgam, rqk, rv, rg, o_ret, dpr, wbr, states, scores, gain, cos, sin)


GLA_LEVELS = tuple(GC >> (s + 1) for s in range(int(math.log2(GC // GLA_SUB))))
NLEV = len(GLA_LEVELS)


def _gla_tril():
    return np.tril(np.ones((GC, GC), np.float32))


def _gla_masks():
    ii = lax.broadcasted_iota(jnp.int32, (GC, GC), 0)
    jj = lax.broadcasted_iota(jnp.int32, (GC, GC), 1)
    masks = []
    for m in GLA_LEVELS:
        sh = int(math.log2(2 * m))
        masks.append(((ii >> sh) == (jj >> sh)) & ((ii & m) != 0) & ((jj & m) == 0))
    sh = int(math.log2(GLA_SUB))
    md = ((ii >> sh) == (jj >> sh)) & (jj <= ii)
    row = lax.broadcasted_iota(jnp.int32, (GC, 1), 0)
    second = [(row & m) != 0 for m in GLA_LEVELS]
    return masks, md, second


def _gla_gate_call(u, w_g, wg, bg, pmat):
    tp = u.shape[0]
    gb = _proj_rows(tp)
    assert gb % GC == 0

    def body(u_ref, w_ref, wg_ref, bg_ref, p_ref, glr_ref, z_ref, b_ref):
        glr = _dot(u_ref[...], w_ref[...], NT)
        glr_ref[...] = glr
        z = _dot(glr.astype(BF16), wg_ref[...], NN) + bg_ref[...]
        z_ref[...] = z
        la = (jnp.minimum(z, 0.0) - jnp.log1p(jnp.exp(-jnp.abs(z)))) * (1.0 / GATE_TAU)
        for r in range(0, gb, GC):
            b_ref[r:r + GC, :] = _exact_pm(p_ref[...], la[r:r + GC, :])

    tile = pl.BlockSpec((gb, GLA_KW), lambda i: (i, 0))
    return pl.pallas_call(
        body, name="gla_gate", grid=(tp // gb,),
        in_specs=[pl.BlockSpec((gb, D_MODEL), lambda i: (i, 0)),
                  pl.BlockSpec((128, D_MODEL), lambda i: ((W_GP - 128) // 128, 0)),
                  pl.BlockSpec((128, GLA_KW), lambda i: (0, 0)),
                  pl.BlockSpec((1, GLA_KW), lambda i: (0, 0)), pl.BlockSpec((GC, GC), lambda i: (0, 0))],
        out_specs=[pl.BlockSpec((gb, 128), lambda i: (i, 0)), tile, tile],
        out_shape=[jax.ShapeDtypeStruct((tp, 128), F32), jax.ShapeDtypeStruct((tp, GLA_KW), F32),
                   jax.ShapeDtypeStruct((tp, GLA_KW), F32)],
        compiler_params=_cparams(1),
    )(u, w_g, wg, bg, pmat)


def _gla_gate_bwd_call(db, z, glr, u, wg, pmat_t, d_g):
    tp = db.shape[0]
    gb = _proj_rows(tp)
    assert gb % GC == 0 and (W_GP - 128) % 128 == 0

    def body(db_ref, z_ref, glr_ref, u_ref, wg_ref, pt_ref, dgin_ref, dg_ref, dwg_ref, dbg_ref, dwl_ref):
        i = pl.program_id(0)

        @pl.when(i == 0)
        def _():
            dwg_ref[...] = jnp.zeros_like(dwg_ref)
            dbg_ref[...] = jnp.zeros_like(dbg_ref)
            dwl_ref[...] = jnp.zeros_like(dwl_ref)

        dla = jnp.concatenate([_exact_pm(pt_ref[...], db_ref[r:r + GC, :]) for r in range(0, gb, GC)], axis=0)
        row = i * gb + lax.broadcasted_iota(jnp.int32, (gb, 1), 0)
        dz = jnp.where(row >= PADF, dla * (1.0 / GATE_TAU) * _sigmoid(-z_ref[...]), 0.0)
        dzb = dz.astype(BF16)
        dglr = _dot(dzb, wg_ref[...], NT).astype(BF16)
        dg_ref[...] = dglr
        dwg_ref[...] += _dot(glr_ref[...].astype(BF16), dzb, TN)
        dbg_ref[...] += jnp.sum(dz, axis=0, keepdims=True)
        dwl_ref[...] += _dot(u_ref[...], dglr, TN)

    tile = pl.BlockSpec((gb, GLA_KW), lambda i: (i, 0))
    const = lambda i: (0, 0)
    return pl.pallas_call(
        body, name="gla_gate_bwd", grid=(tp // gb,),
        in_specs=[tile, tile, pl.BlockSpec((gb, 128), lambda i: (i, 0)), pl.BlockSpec((gb, D_MODEL), lambda i: (i, 0)),
                  pl.BlockSpec((128, GLA_KW), const), pl.BlockSpec((GC, GC), const), ANY],
        out_specs=[pl.BlockSpec((gb, 128), lambda i: (i, (W_GP - 128) // 128)), pl.BlockSpec((128, GLA_KW), const),
                   pl.BlockSpec((1, GLA_KW), const), pl.BlockSpec((D_MODEL, 128), const)],
        out_shape=[jax.ShapeDtypeStruct(d_g.shape, BF16), jax.ShapeDtypeStruct((128, GLA_KW), F32),
                   jax.ShapeDtypeStruct((1, GLA_KW), F32), jax.ShapeDtypeStruct((D_MODEL, 128), F32)],
        input_output_aliases={6: 0}, compiler_params=_cparams(1),
    )(db, z, glr, u, wg, pmat_t, d_g)


def _gla_row_steps(b_ref, cs, rows, size):
    parts = [jnp.zeros((size, GLA_K), F32) if r is None else jnp.broadcast_to(b_ref[r:r + 1, cs], (size, GLA_K))
             for r in rows]
    return parts[0] if len(parts) == 1 else jnp.concatenate(parts, axis=0)


def _gla_factors(b_ref, h, second):
    cs = slice(h * GLA_K, (h + 1) * GLA_K)
    b = b_ref[:, cs]
    fq, fk = [], []
    for l, m in enumerate(GLA_LEVELS):
        d = b - _gla_row_steps(b_ref, cs, [s + m - 1 for s in range(0, GC, 2 * m)], 2 * m)
        f = jnp.exp(jnp.where(second[l], d, -d))
        fq.append(jnp.where(second[l], f, 0.0))
        fk.append(jnp.where(second[l], 0.0, f))
    dd = b - _gla_row_steps(b_ref, cs, [None] + [s - 1 for s in range(GLA_SUB, GC, GLA_SUB)], GLA_SUB)
    ed = jnp.exp(dd)
    edi = jnp.exp(-dd)
    eb = jnp.exp(b)
    bl = b_ref[GC - 1:GC, cs]
    ee = jnp.exp(bl - b)
    ebl = jnp.exp(bl)
    return fq, fk, ed, edi, eb, ee, ebl


def _gla_scaled(q, k, fq, fk, ed, edi):
    qt = [(q * f).astype(BF16) for f in fq]
    kt = [(k * f).astype(BF16) for f in fk]
    return qt, kt, (q * ed).astype(BF16), (k * edi).astype(BF16)


def _gla_scores(qt, kt, qd, kd, masks, md):
    a = jnp.where(md, _dot(qd, kd, NT), 0.0)
    for l in range(NLEV):
        a = a + jnp.where(masks[l], _dot(qt[l], kt[l], NT), 0.0)
    return a.astype(BF16)


def _gla_fwd_call(gqk, gv, b, gg, gain, comm=None):
    tp = gqk.shape[0]
    nc = tp // GC
    ns = nc // GS
    n_xc = len(comm.srcs) if comm else 0

    def body(qk_ref, v_ref, b_ref, gg_ref, g_ref, *rest):
        xc_src = rest[:n_xc]
        o_ref, a_ref, st_ref, am_ref = rest[n_xc:n_xc + 4]
        xc_dst = rest[n_xc + 4:2 * n_xc + 4]
        s_scr = rest[2 * n_xc + 4]
        n = pl.program_id(0)
        if n_xc:
            begin, finish = comm.make(xc_src, xc_dst, rest[-2], rest[-1])
            pl.when(n == 0)(begin)
            pl.when(n == ns - 1)(finish)

        @pl.when(n == 0)
        def _():
            s_scr[...] = jnp.zeros_like(s_scr)

        masks, md, second = _gla_masks()
        for cc in range(GS):
            rows = pl.ds(cc * GC, GC)
            qk_c, v_c, b_c, gg_c, o_c, a_c = (r.at[rows] for r in (qk_ref, v_ref, b_ref, gg_ref, o_ref, a_ref))
            for h in range(GLA_HEADS):
                q = qk_c[:, h * GLA_K:(h + 1) * GLA_K]
                k = qk_c[:, GLA_KW + h * GLA_K:GLA_KW + (h + 1) * GLA_K]
                vs = slice(h * GLA_V, (h + 1) * GLA_V)
                v = v_c[:, vs]
                fq, fk, ed, edi, eb, ee, ebl = _gla_factors(b_c, h, second)
                a = _gla_scores(*_gla_scaled(q, k, fq, fk, ed, edi), masks, md)
                am_ref[cc, h] = a
                sb = s_scr[h].astype(BF16)
                st_ref[cc, h] = sb
                o = _dot(a, v, NN) + _dot((q * eb).astype(BF16), sb, NT)
                s_scr[h] = s_scr[h] * ebl + _dot(v, (k * ee).astype(BF16), TN)
                o_c[:, vs] = o
                xh = o * lax.rsqrt(_head_mean(o * o) + EPS)
                a_c[:, vs] = (xh * g_ref[:, vs] * _silu(gg_c[:, vs])).astype(BF16)

    return pl.pallas_call(
        body, name="gla_fwd", grid=(ns,),
        in_specs=[pl.BlockSpec((GS * GC, 2 * GLA_KW), lambda n: (n, 0)),
                  pl.BlockSpec((GS * GC, GLA_W), lambda n: (n, 0)),
                  pl.BlockSpec((GS * GC, GLA_KW), lambda n: (n, 0)),
                  pl.BlockSpec((GS * GC, GLA_W), lambda n: (n, 0)),
                  pl.BlockSpec((1, GLA_W), lambda n: (0, 0))] + [ANY] * n_xc,
        out_specs=[pl.BlockSpec((GS * GC, GLA_W), lambda n: (n, 0)),
                   pl.BlockSpec((GS * GC, GLA_W), lambda n: (n, 0)),
                   pl.BlockSpec((GS, GLA_HEADS, GLA_V, GLA_K), lambda n: (n, 0, 0, 0)),
                   pl.BlockSpec((GS, GLA_HEADS, GC, GC), lambda n: (n, 0, 0, 0))] + [ANY] * n_xc,
        out_shape=[jax.ShapeDtypeStruct((tp, GLA_W), F32), jax.ShapeDtypeStruct((tp, GLA_W), BF16),
                   jax.ShapeDtypeStruct((nc, GLA_HEADS, GLA_V, GLA_K), BF16),
                   jax.ShapeDtypeStruct((nc, GLA_HEADS, GC, GC), BF16)] + (list(comm.out_shapes) if comm else []),
        scratch_shapes=[pltpu.VMEM((GLA_HEADS, GLA_V, GLA_K), F32)] + (_comm_sems(comm) if comm else []),
        compiler_params=_cparams(1),
    )(gqk, gv, b, gg, gain, *(comm.srcs if comm else ()))


def _gla_bwd_call(gqk, gv, b, gg, o_gla, da, states, scores, gain, comm=None):
    tp = gqk.shape[0]
    nc = tp // GC
    ns = nc // GS
    o_gv, o_gg = 2 * GLA_KW, 2 * GLA_KW + GLA_W
    n_xc = len(comm.srcs) if comm else 0

    def body(qk_all, v_all, b_all, gg_all, o_all, da_all, st_ref, am_ref, g_ref, *rest):
        xc_src = rest[:n_xc]
        dp_all, db_all, dg_ref = rest[n_xc:n_xc + 3]
        xc_dst = rest[n_xc + 3:2 * n_xc + 3]
        ds_scr = rest[2 * n_xc + 3]
        n = pl.program_id(0)
        if n_xc:
            begin, finish = comm.make(xc_src, xc_dst, rest[-2], rest[-1])
            pl.when(n == 0)(begin)
            pl.when(n == ns - 1)(finish)

        @pl.when(n == 0)
        def _():
            ds_scr[...] = jnp.zeros_like(ds_scr)
            dg_ref[...] = jnp.zeros_like(dg_ref)

        masks, md, second = _gla_masks()
        for cc, h in [(cc, h) for cc in reversed(range(GS)) for h in range(GLA_HEADS)]:
            rows = pl.ds(cc * GC, GC)
            qk_ref, v_ref, b_scr, gg_ref, o_ref, da_ref, dp_ref, db_scr = (
                r.at[rows] for r in (qk_all, v_all, b_all, gg_all, o_all, da_all, dp_all, db_all))
            cs = slice(h * GLA_K, (h + 1) * GLA_K)
            vs = slice(h * GLA_V, (h + 1) * GLA_V)
            o = o_ref[:, vs]
            rstd = lax.rsqrt(_head_mean(o * o) + EPS)
            xh = o * rstd
            gain_h = g_ref[:, vs]
            g = gg_ref[:, vs]
            sg = _sigmoid(g)
            dah = da_ref[:, vs]
            dp_ref[:, o_gg + h * GLA_V:o_gg + (h + 1) * GLA_V] = (
                dah * (xh * gain_h) * (sg * (1.0 + g * (1.0 - sg)))).astype(BF16)
            dn = dah * (g * sg)
            dg_ref[:, vs] += jnp.sum(dn * xh, axis=0, keepdims=True)
            dxh = dn * gain_h
            do = rstd * (dxh - xh * _head_mean(dxh * xh))
            dob = do.astype(BF16)
            q = qk_ref[:, cs]
            k = qk_ref[:, GLA_KW + h * GLA_K:GLA_KW + (h + 1) * GLA_K]
            v = v_ref[:, vs]
            fq, fk, ed, edi, eb, ee, ebl = _gla_factors(b_scr, h, second)
            qt, kt, qd, kd = _gla_scaled(q, k, fq, fk, ed, edi)
            sp = st_ref[cc, h]
            ds = ds_scr[h]
            dsb = ds.astype(BF16)
            q_in = q * eb
            k_end = k * ee
            da_s = _dot(dob, v, NT)
            dv = _dot(am_ref[cc, h], dob, TN) + _dot(k_end.astype(BF16), dsb, NT)
            dq_in = _dot(dob, sp, NN)
            dk_end = _dot(v, dsb, NN)
            dbl = jnp.sum(sp.astype(F32) * ds, axis=0, keepdims=True) * ebl
            ds_scr[h] = ds * ebl + _dot(dob, q_in.astype(BF16), TN)
            dq = dq_in * eb
            dk = dk_end * ee
            de_end = dk_end * k_end
            db = dq_in * q_in - de_end
            placed = [(GC - 1, jnp.sum(de_end, axis=0, keepdims=True) + dbl)]
            for l, m in enumerate(GLA_LEVELS):
                dal = jnp.where(masks[l], da_s, 0.0).astype(BF16)
                dqt = _dot(dal, kt[l], NN)
                dkt = _dot(dal, qt[l], TN)
                dq = dq + dqt * fq[l]
                dk = dk + dkt * fk[l]
                gl = dqt * (q * fq[l]) - dkt * (k * fk[l])
                db = db + gl
                placed += [(s + m - 1, -jnp.sum(gl[s:s + 2 * m], axis=0, keepdims=True)) for s in range(0, GC, 2 * m)]
            dad = jnp.where(md, da_s, 0.0).astype(BF16)
            dqd = _dot(dad, kd, NN)
            dkd = _dot(dad, qd, TN)
            dq = dq + dqd * ed
            dk = dk + dkd * edi
            gd = dqd * (q * ed) - dkd * (k * edi)
            db = db + gd
            placed += [(s - 1, -jnp.sum(gd[s:s + GLA_SUB], axis=0, keepdims=True)) for s in range(GLA_SUB, GC, GLA_SUB)]
            db_scr[:, cs] = db
            for r, val in placed:
                db_scr[r:r + 1, cs] += val
            dp_ref[:, cs] = (dq * (GLA_K ** -0.5)).astype(BF16)
            dp_ref[:, GLA_KW + h * GLA_K:GLA_KW + (h + 1) * GLA_K] = dk.astype(BF16)
            dp_ref[:, o_gv + h * GLA_V:o_gv + (h + 1) * GLA_V] = dv.astype(BF16)

    rev = lambda n: (ns - 1 - n, 0)
    const = lambda n: (0, 0)
    xc_shapes, xc_sems = (list(comm.out_shapes), _comm_sems(comm)) if n_xc else ([], [])
    return pl.pallas_call(
        body, name="gla_bwd", grid=(ns,),
        in_specs=[pl.BlockSpec((GS * GC, 2 * GLA_KW), rev),
                  pl.BlockSpec((GS * GC, GLA_W), rev),
                  pl.BlockSpec((GS * GC, GLA_KW), rev),
                  pl.BlockSpec((GS * GC, GLA_W), rev),
                  pl.BlockSpec((GS * GC, GLA_W), rev),
                  pl.BlockSpec((GS * GC, GLA_W), rev),
                  pl.BlockSpec((GS, GLA_HEADS, GLA_V, GLA_K), lambda n: (ns - 1 - n, 0, 0, 0)),
                  pl.BlockSpec((GS, GLA_HEADS, GC, GC), lambda n: (ns - 1 - n, 0, 0, 0)),
                  pl.BlockSpec((1, GLA_W), const)] + [ANY] * n_xc,
        out_specs=[pl.BlockSpec((GS * GC, W_GP), rev), pl.BlockSpec((GS * GC, GLA_KW), rev),
                   pl.BlockSpec((1, GLA_W), const)] + [ANY] * n_xc,
        out_shape=[jax.ShapeDtypeStruct((tp, W_GP), BF16), jax.ShapeDtypeStruct((tp, GLA_KW), F32),
                   jax.ShapeDtypeStruct((1, GLA_W), F32)] + xc_shapes,
        scratch_shapes=[pltpu.VMEM((GLA_HEADS, GLA_V, GLA_K), F32)] + xc_sems,
        compiler_params=_cparams(1),
    )(gqk, gv, b, gg, o_gla, da, states, scores, gain, *(comm.srcs if comm else ()))


def _mid_call(a_ret, a_gla, mg, h0, tgt, wbr, wbg, wout, gf):
    tp = h0.shape[0]
    nt = tp // TM

    def body(ar_ref, ag_ref, mg_ref, h_ref, t_ref, wbr_ref, wbg_ref, wo_ref, gf_ref,
             dh1_ref, dag_ref, dm_ref, mb_ref, dh1b_ref, dprb_ref, dpgb_ref, loss_ref, dgf_ref):
        i = pl.program_id(0)

        @pl.when(i == 0)
        def _():
            loss_ref[...] = jnp.zeros_like(loss_ref)
            dgf_ref[...] = jnp.zeros_like(dgf_ref)

        ar, ag = ar_ref[...], ag_ref[...]
        pr = _dot(ar, wbr_ref[...], NN)
        pg = _dot(ag, wbg_ref[...], NN)
        sr = _sigmoid(mg_ref[:, :D_MODEL])
        sg = _sigmoid(mg_ref[:, D_MODEL:])
        merged = (sr * pr + sg * pg).astype(BF16)
        mb_ref[...] = merged
        h1 = h_ref[...] + _dot(merged, wo_ref[...], NN)
        r1 = lax.rsqrt(jnp.mean(h1 * h1, axis=-1, keepdims=True) + EPS)
        xh = h1 * r1
        gfv = gf_ref[...]
        live = jnp.where(i > 0, 1.0, 0.0).astype(F32)
        err = (xh * gfv - t_ref[...]) * live
        loss_ref[...] += jnp.full(loss_ref.shape, 0.5 / D_MODEL, F32) * jnp.sum(err * err)
        dy = err * (1.0 / D_MODEL)
        dgf_ref[...] += jnp.sum(dy * xh, axis=0, keepdims=True)
        dxh = dy * gfv
        dh1 = r1 * (dxh - xh * jnp.mean(dxh * xh, axis=-1, keepdims=True))
        dh1_ref[...] = dh1
        dh1b = dh1.astype(BF16)
        dh1b_ref[...] = dh1b
        dmerged = _dot(dh1b, wo_ref[...], NT)
        dm_ref[:, :D_MODEL] = (dmerged * pr * sr * (1.0 - sr)).astype(BF16)
        dm_ref[:, D_MODEL:] = (dmerged * pg * sg * (1.0 - sg)).astype(BF16)
        dpr = (dmerged * sr).astype(BF16)
        dpg = (dmerged * sg).astype(BF16)
        dprb_ref[...] = dpr
        dpgb_ref[...] = dpg
        dag_ref[...] = _dot(dpg, wbg_ref[...], NT)

    tile = lambda w: pl.BlockSpec((TM, w), lambda i: (i, 0))
    const = lambda r, w: pl.BlockSpec((r, w), lambda i: (0, 0))
    return pl.pallas_call(
        body, name="merge_out_loss", grid=(nt,),
        in_specs=[tile(RET_W), tile(GLA_W), tile(W_M), tile(D_MODEL),
                  pl.BlockSpec((TM, D_MODEL), lambda i: (jnp.maximum(i - 1, 0), 0)),
                  const(RET_W, D_MODEL), const(GLA_W, D_MODEL), const(D_MODEL, D_MODEL), const(1, D_MODEL)],
        out_specs=[tile(D_MODEL), tile(GLA_W), tile(W_M), tile(D_MODEL), tile(D_MODEL), tile(D_MODEL),
                   tile(D_MODEL), const(1, 128), const(1, D_MODEL)],
        out_shape=[jax.ShapeDtypeStruct((tp, D_MODEL), F32), jax.ShapeDtypeStruct((tp, GLA_W), F32),
                   jax.ShapeDtypeStruct((tp, W_M), BF16),
                   jax.ShapeDtypeStruct((tp, D_MODEL), BF16), jax.ShapeDtypeStruct((tp, D_MODEL), BF16),
                   jax.ShapeDtypeStruct((tp, D_MODEL), BF16), jax.ShapeDtypeStruct((tp, D_MODEL), BF16),
                   jax.ShapeDtypeStruct((1, 128), F32), jax.ShapeDtypeStruct((1, D_MODEL), F32)],
        compiler_params=_cparams(1),
    )(a_ret, a_gla, mg, h0, tgt, wbr, wbg, wout, gf)


def _device_step(x2d, tgt2d, meta, norm_gain, w_in_part, w_gate_up, b_gate, ret_gain, gla_gain, branch_parts,
                 final_gain, ck, side):
    seq = x2d.shape[0]
    tp = T0 + seq
    head = jnp.concatenate([jnp.zeros((PADF, D_MODEL), F32), meta], axis=0)
    wg_pad = jnp.pad(w_gate_up, ((0, 128 - GATE_RANK), (0, 0))).astype(BF16)

    half = RET_QK // 2
    cos, sin = (jnp.asarray(t) for t in _rope_tables(tp))
    lgam = jnp.log1p(-(2.0 ** (-5.0 - jnp.arange(RET_HEADS, dtype=F32))))
    pmat = jnp.asarray(_gla_tril(), BF16)
    pmat_t = jnp.asarray(_gla_tril().T.copy(), BF16)

    h0, u, g_in = _rms_call(x2d, head, norm_gain, _gather_plan([w_in_part], relay=(True,)))
    sw, hc = w_in_part.shape
    w_in_t = g_in.reshape(4, 2, sw, hc).transpose(0, 2, 1, 3).reshape(4 * sw, 2 * hc)
    w_in_t, side = lax.optimization_barrier((w_in_t, side))
    w_r = w_in_t
    w_g = jnp.pad(w_in_t[W_R:W_R + W_G], ((0, W_GP - W_G), (0, 0)))
    w_m = w_in_t[W_R + W_G:]
    tab = pl.BlockSpec((_proj_rows(tp), half), lambda j, i: (i, 0))
    rqk = _mm_nn("proj_rqk", u, w_r, BF16, D_MODEL, 0, 2 * D_MODEL, _rope_epilogue, (cos, sin), (tab, tab))
    rv = _mm_nn("proj_rv", u, w_r, BF16, RET_W, 2 * D_MODEL, RET_W)
    rg = _mm_nn("proj_rg", u, w_r, F32, RET_W, 4 * D_MODEL, RET_W)
    gqk = _mm_nn("proj_gqk", u, w_g, F32, 2 * GLA_KW, 0, 2 * GLA_KW, _gqk_epilogue)
    gv = _mm_nn("proj_gv", u, w_g, BF16, GLA_W, 2 * GLA_KW, GLA_W)
    gg = _mm_nn("proj_gg", u, w_g, F32, GLA_W, 2 * GLA_KW + GLA_W, GLA_W)
    mg = _mm_nn("proj_mg", u, w_m, F32, W_M, 0, W_M)

    o_ret, a_ret, st_ret, sc_ret = _ret_fwd_call(rqk, rv, rg, ret_gain, lgam)
    glr, z_gate, b_dec = _gla_gate_call(u, w_g, wg_pad, b_gate, pmat)
    o_gla, a_gla, st_gla, sc_gla, g_br, g_bg, g_out = _gla_fwd_call(gqk, gv, b_dec, gg, gla_gain,
                                                                    comm=_spread_plan(branch_parts))
    wbr = g_br.reshape(RET_W, D_MODEL)
    wbg = g_bg.reshape(GLA_W, D_MODEL)
    wout = g_out.reshape(D_MODEL, D_MODEL)

    gf = final_gain.reshape(1, D_MODEL)
    (dh1, da_gla, dm, merged_b, dh1_b, dpr_b, dpg_b, loss, dgf) = _mid_call(
        a_ret, a_gla, mg, h0, tgt2d, wbr, wbg, wout, gf)

    names_b = ("w_branch_ret", "w_branch_gla", "w_out")
    g2_b = [_mm_tn("dw_br", a_ret, dpr_b, D_MODEL).reshape(4, 2, RET_W // 8, D_MODEL).transpose(1, 0, 2, 3),
            _mm_tn("dw_bg", a_gla, dpg_b, D_MODEL).reshape(4, 2, GLA_W // 8, D_MODEL).transpose(1, 0, 2, 3),
            _mm_tn("dw_out", merged_b, dh1_b, D_MODEL).reshape(4, 2, D_MODEL // 8, D_MODEL).transpose(1, 0, 2, 3)]
    sib_b = _swap_halves_call("swap_halves_branch", g2_b)
    sum_b = [_add_half_call("add_half_" + nm, g, b, ck) for nm, g, b in zip(names_b, g2_b, sib_b)]
    d_g, db_dec, dgla_gain, *chips_b = _gla_bwd_call(gqk, gv, b_dec, gg, o_gla, da_gla, st_gla, sc_gla, gla_gain,
                                                     comm=_exchange_plan(sum_b))
    d_g, dwg, dbg, dw_glr = _gla_gate_bwd_call(db_dec, z_gate, glr, u, wg_pad, pmat_t, d_g)
    mine = [_add_chips_call("add_chips_" + nm, g, b, p, ck) for nm, g, b, p in zip(names_b, g2_b, sib_b, chips_b)]

    d_r, dret_gain = _ret_bwd_call(rqk, rv, rg, o_ret, dpr_b, wbr, st_ret, sc_ret, ret_gain, lgam, cos, sin)

    dwp = _mm_tn("dw_r", u, d_r, 3 * D_MODEL, out_cols=IN_PAD)
    dwp = _mm_tn("dw_g", u, d_g, 3 * D_MODEL, ncols=W_GP - 128, into=dwp, col0=W_R)
    g2_in = _place_merge_cols_call(dwp, _mm_tn("dw_m", u, dm, 2 * D_MODEL), dw_glr).reshape(2, D_MODEL // 2, IN_PAD)

    du, sib_in = _mm_nt_acc("du_g", d_g, w_g, W_GP, comm=_swap_plan([g2_in]), tb=_proj_rows(tp))
    sum_in = _add_rows_call("add_half_w_in", g2_in, sib_in, ck)
    du, chips_in = _mm_nt_acc("du_r", d_r, w_r, 2 * D_MODEL, acc_in=du, comm=_exchange_window_plan(sum_in),
                              tb=_proj_rows(tp))
    tile = pl.BlockSpec((TB, D_MODEL), lambda i, kk: (i, 0))
    row = pl.BlockSpec((1, D_MODEL), lambda i, kk: (0, 0))
    dx, dmeta, dnorm_gain = _mm_nt_acc(
        "du_m", dm, w_m, W_M, acc_in=du, epilogue=_rms_bwd_epilogue, extras=(h0, norm_gain, dh1),
        extra_specs=(tile, row, tile),
        extra_out_shapes=(jax.ShapeDtypeStruct((seq, D_MODEL), F32), jax.ShapeDtypeStruct((N_META, D_MODEL), F32),
                          jax.ShapeDtypeStruct((1, D_MODEL), F32)),
        extra_out_specs=(ANY, pl.BlockSpec((N_META, D_MODEL), lambda i, kk: (0, 0)), row),
        extra_scratch=(pltpu.VMEM((2, TB, D_MODEL), F32), pltpu.SemaphoreType.DMA((2,))))
    small = dict(norm_gain=dnorm_gain, b_gate=dbg, ret_norm_gain=dret_gain, gla_norm_gain=dgla_gain,
                 final_norm_gain=dgf, w_gate_up=dwg[:GATE_RANK], meta_tokens=dmeta, loss=loss[0, 0])
    rows = -(-sum(sz for _, sz in SMALL) // 128 // 8) * 8
    mine_in, g_small = _add_window_call("add_chips_w_in", g2_in, sib_in, chips_in, ck,
                                        _gather_plan([_pack_rows([small[nm] for nm, _ in SMALL], rows)]))
    full = _join_halves_call("join_halves", [mine_in] + mine)

    return dict(dx=dx, small=g_small, w_in=full[0], w_branch_ret=full[1], w_branch_gla=full[2], w_out=full[3],
                side=side)


MESH = pl.DeviceIdType.MESH
ANY = pl.BlockSpec(memory_space=pl.ANY)


def _place():
    return lax.axis_index("x"), lax.axis_index("y"), lax.axis_index("c")


def _gather8_call(name, parts):
    comm = _gather_plan(parts)
    n = len(parts)

    def body(*refs):
        begin, finish = comm.make(refs[:n], refs[n:2 * n], refs[-2], refs[-1])
        begin()
        finish()

    return pl.pallas_call(
        body, name=name, out_shape=list(comm.out_shapes), in_specs=[ANY] * n, out_specs=[ANY] * n,
        scratch_shapes=_comm_sems(comm),
    )(*parts)


def _swap_halves_call(name, gs):
    n = len(gs)

    def body(*refs):
        g_refs, b_refs = refs[:n], refs[n:2 * n]
        send_sems, recv_sems = refs[2 * n:]
        x, y, c = _place()
        copies = [pltpu.make_async_remote_copy(
            src_ref=g_refs[t].at[1 - c], dst_ref=b_refs[t], send_sem=send_sems.at[t], recv_sem=recv_sems.at[t],
            device_id=(x, y, 1 - c), device_id_type=MESH) for t in range(n)]
        for cp in copies:
            cp.start()
        for cp in copies:
            cp.wait()

    return pl.pallas_call(
        body, name=name,
        out_shape=[jax.ShapeDtypeStruct(g.shape[1:], g.dtype) for g in gs],
        in_specs=[ANY] * n, out_specs=[ANY] * n,
        scratch_shapes=[pltpu.SemaphoreType.DMA((n,)), pltpu.SemaphoreType.DMA((n,))],
    )(*gs)


def _join_halves_call(name, ts):
    n = len(ts)

    def body(*refs):
        o_refs = refs[n:2 * n]
        send_sems, recv_sems = refs[2 * n:]
        x, y, c = _place()
        copies = [pltpu.make_async_remote_copy(
            src_ref=o_refs[t].at[c], dst_ref=o_refs[t].at[c], send_sem=send_sems.at[t], recv_sem=recv_sems.at[t],
            device_id=(x, y, 1 - c), device_id_type=MESH) for t in range(n)]
        for cp in copies:
            cp.start()
        for t in range(n):
            copies[t].wait_send()
            pltpu.make_async_remote_copy(
                src_ref=o_refs[t].at[c], dst_ref=o_refs[t].at[1 - c], send_sem=send_sems.at[t],
                recv_sem=recv_sems.at[t], device_id=(x, y, 1 - c), device_id_type=MESH).wait_recv()

    return pl.pallas_call(
        body, name=name,
        out_shape=[jax.ShapeDtypeStruct(t.shape, t.dtype) for t in ts],
        in_specs=[ANY] * n, out_specs=[ANY] * n, input_output_aliases={t: t for t in range(n)},
        scratch_shapes=[pltpu.SemaphoreType.DMA((n,)), pltpu.SemaphoreType.DMA((n,))],
    )(*ts)


def _row_block(rows, cols, budget):
    best = 8
    for rb in range(8, rows + 1, 8):
        if rows % rb == 0 and rb * cols * 4 <= budget:
            best = rb
    return best


def _add_half_call(name, g, b, ck):
    _, _, r, cc = g.shape
    rb = _row_block(r, cc, 2 * 1024 * 1024)

    def body(ck_ref, g_ref, b_ref, o_ref):
        o_ref[...] = (g_ref[...] + b_ref[...]).astype(BF16)

    return pl.pallas_call(
        body, name=name,
        grid_spec=pltpu.PrefetchScalarGridSpec(
            num_scalar_prefetch=1, grid=(4, r // rb),
            in_specs=[pl.BlockSpec((None, None, rb, cc), lambda k, i, ck_ref: (ck_ref[0], k, i, 0)),
                      pl.BlockSpec((None, rb, cc), lambda k, i, ck_ref: (k, i, 0))],
            out_specs=pl.BlockSpec((None, rb, cc), lambda k, i, ck_ref: (k, i, 0))),
        out_shape=jax.ShapeDtypeStruct(b.shape, BF16),
        compiler_params=_cparams(2),
    )(ck, g, b)


def _add_rows_call(name, g, b, ck):
    _, r, cc = g.shape
    rb = _row_block(r, cc, 2 * 1024 * 1024)

    def body(ck_ref, g_ref, b_ref, o_ref):
        o_ref[...] = (g_ref[...] + b_ref[...]).astype(BF16)

    return pl.pallas_call(
        body, name=name,
        grid_spec=pltpu.PrefetchScalarGridSpec(
            num_scalar_prefetch=1, grid=(r // rb,),
            in_specs=[pl.BlockSpec((None, rb, cc), lambda i, ck_ref: (ck_ref[0], i, 0)),
                      pl.BlockSpec((rb, cc), lambda i, ck_ref: (i, 0))],
            out_specs=pl.BlockSpec((rb, cc), lambda i, ck_ref: (i, 0))),
        out_shape=jax.ShapeDtypeStruct((r, cc), BF16),
        compiler_params=_cparams(1),
    )(ck, g, b)


def _add_window_call(name, g, b, p, ck, comm):
    _, r, _ = g.shape
    nb, step = WIN_W // 128, WIN_STEP // 128
    n_xc = len(comm.srcs)

    def body(ck_ref, g_ref, b_ref, p0_ref, p1_ref, p2_ref, *rest):
        o_ref = rest[n_xc]
        i = pl.program_id(0)
        begin, finish = comm.make(rest[:n_xc], rest[n_xc + 1:2 * n_xc + 1], rest[-2], rest[-1])
        pl.when(i == 0)(begin)
        own = g_ref[...] + b_ref[...]
        o_ref[...] = ((own + p0_ref[...].astype(F32)) + p1_ref[...].astype(F32)) + p2_ref[...].astype(F32)
        pl.when(i == nb - 1)(finish)

    def peer(j):
        return pl.BlockSpec((None, r, 128), lambda i, ck_ref: (j, 0, i))

    return pl.pallas_call(
        body, name=name,
        grid_spec=pltpu.PrefetchScalarGridSpec(
            num_scalar_prefetch=1, grid=(nb,),
            in_specs=[pl.BlockSpec((None, r, 128), lambda i, ck_ref: (ck_ref[0], 0, step * ck_ref[1] + i)),
                      pl.BlockSpec((r, 128), lambda i, ck_ref: (0, step * ck_ref[1] + i)),
                      peer(0), peer(1), peer(2)] + [ANY] * n_xc,
            out_specs=[pl.BlockSpec((None, r, 128), lambda i, ck_ref: (ck_ref[0], 0, i))] + [ANY] * n_xc,
            scratch_shapes=_comm_sems(comm)),
        out_shape=[jax.ShapeDtypeStruct((2, r, WIN_W), F32)] + list(comm.out_shapes),
        compiler_params=_cparams(1),
    )(ck, g, b, p, p, p, *comm.srcs)


def _add_chips_call(name, g, b, p, ck):
    _, _, r, cc = g.shape
    rb = _row_block(r, cc, 2 * 1024 * 1024)

    def body(ck_ref, g_ref, b_ref, p0_ref, p1_ref, p2_ref, o_ref):
        own = g_ref[...] + b_ref[...]
        o_ref[...] = ((own + p0_ref[...].astype(F32)) + p1_ref[...].astype(F32)) + p2_ref[...].astype(F32)

    def peer(j):
        return pl.BlockSpec((None, rb, cc), lambda i, ck_ref: (j, i, 0))

    return pl.pallas_call(
        body, name=name,
        grid_spec=pltpu.PrefetchScalarGridSpec(
            num_scalar_prefetch=1, grid=(r // rb,),
            in_specs=[pl.BlockSpec((None, None, rb, cc), lambda i, ck_ref: (ck_ref[0], ck_ref[1], i, 0)),
                      pl.BlockSpec((None, rb, cc), lambda i, ck_ref: (ck_ref[1], i, 0)),
                      peer(0), peer(1), peer(2)],
            out_specs=pl.BlockSpec((None, rb, cc), lambda i, ck_ref: (ck_ref[0], i, 0))),
        out_shape=jax.ShapeDtypeStruct((2, r, cc), F32),
        compiler_params=_cparams(1),
    )(ck, g, b, p, p, p)


def _sum8_call(name, g):
    def body(g_ref, o_ref):
        acc = g_ref[0]
        for d in range(1, 8):
            acc = acc + g_ref[d]
        o_ref[...] = acc

    return pl.pallas_call(body, name=name, out_shape=jax.ShapeDtypeStruct(g.shape[1:], F32))(g)


def _adamw_call(name, w, g, m, v):
    r, cc = w.shape
    if r % 8 == 0 or r * cc * 4 <= 1024 * 1024:
        rb = _row_block(r, cc, 1024 * 1024) if r % 8 == 0 else r
        grid, spec = (r // rb,), pl.BlockSpec((rb, cc), lambda i: (i, 0))
    else:
        grid, spec = (cc // 128,), pl.BlockSpec((r, 128), lambda i: (0, i))

    def body(w_ref, g_ref, m_ref, v_ref, d_ref, m2_ref, v2_ref):
        gv = g_ref[...]
        m2 = ADAM_B1 * m_ref[...] + (1.0 - ADAM_B1) * gv
        v2 = ADAM_B2 * v_ref[...] + (1.0 - ADAM_B2) * (gv * gv)
        m_hat = m2 / (1.0 - ADAM_B1 ** ADAM_STEP)
        v_hat = v2 / (1.0 - ADAM_B2 ** ADAM_STEP)
        d_ref[...] = -ADAM_LR * (m_hat / (jnp.sqrt(v_hat) + ADAM_EPS) + ADAM_WD * w_ref[...])
        m2_ref[...] = m2
        v2_ref[...] = v2

    return pl.pallas_call(
        body, name=name, grid=grid, in_specs=[spec] * 4, out_specs=[spec] * 3,
        out_shape=[jax.ShapeDtypeStruct((r, cc), F32)] * 3, compiler_params=_cparams(1),
    )(w, g, m, v)


SMALL = (("norm_gain", D_MODEL), ("b_gate", GLA_KW), ("ret_norm_gain", RET_W), ("gla_norm_gain", GLA_W),
         ("final_norm_gain", D_MODEL), ("w_gate_up", GATE_RANK * GLA_KW), ("meta_tokens", N_META * D_MODEL),
         ("loss", 1))


def _pack_rows(vecs, rows):
    flat = jnp.concatenate([v.reshape(-1) for v in vecs])
    return jnp.pad(flat, (0, rows * 128 - flat.shape[0])).reshape(rows, 128)


def kernel(x, meta_tokens, norm_gain, w_in, w_gate_up, b_gate, ret_norm_gain, gla_norm_gain, w_branch_ret, w_branch_gla, w_out, final_norm_gain, loss_target, m_meta_tokens, m_norm_gain, m_w_in, m_w_gate_up, m_b_gate, m_ret_norm_gain, m_gla_norm_gain, m_w_branch_ret, m_w_branch_gla, m_w_out, m_final_norm_gain, v_meta_tokens, v_norm_gain, v_w_in, v_w_gate_up, v_b_gate, v_ret_norm_gain, v_gla_norm_gain, v_w_branch_ret, v_w_branch_gla, v_w_out, v_final_norm_gain):
    xi, yi, ci = _place()
    kme = 2 * xi + yi
    ck = jnp.stack([ci, kme]).astype(jnp.int32)
    sw_in = w_in.shape[2]

    def my_half(a, dtype):
        r, cc = a.shape
        return lax.dynamic_index_in_dim(a.reshape(2, r // 2, cc), ci, 0, keepdims=False).astype(dtype)

    g_meta, g_wg = _gather8_call("gather_small_weights", [my_half(meta_tokens, F32), my_half(w_gate_up[0], F32)])
    branch_parts = [my_half(w_branch_ret[0], BF16), my_half(w_branch_gla[0], BF16), my_half(w_out[0], BF16)]
    meta = g_meta.reshape(4, 2, N_META // 2, D_MODEL // 4).transpose(1, 2, 0, 3).reshape(N_META, D_MODEL)
    wg_full = g_wg.reshape(4, 2, GATE_RANK // 2, GLA_KW // 4).transpose(1, 2, 0, 3).reshape(GATE_RANK, GLA_KW)

    w_in_part = lax.dynamic_slice_in_dim(w_in[0].T, ci * (D_MODEL // 2), D_MODEL // 2, axis=1).astype(BF16)
    loc = _device_step(x[0], loss_target[0], meta, norm_gain, w_in_part, wg_full, b_gate, ret_norm_gain,
                       gla_norm_gain,
                       branch_parts, final_norm_gain, ck, (w_in[0].T, m_w_in[0].T, v_w_in[0].T))
    w_t, m_t, v_t = loc["side"]
    names = ("w_in", "w_branch_ret", "w_branch_gla", "w_out")
    full = [loc[nm] for nm in names]
    big_w = dict(w_in=w_in[0], w_branch_ret=w_branch_ret[0], w_branch_gla=w_branch_gla[0], w_out=w_out[0])
    big_m = dict(w_in=m_w_in[0], w_branch_ret=m_w_branch_ret[0], w_branch_gla=m_w_branch_gla[0], w_out=m_w_out[0])
    big_v = dict(w_in=v_w_in[0], w_branch_ret=v_w_branch_ret[0], w_branch_gla=v_w_branch_gla[0], w_out=v_w_out[0])
    grads, deltas, new_m, new_v = {}, {}, {}, {}
    for nm, f in zip(names, full):
        shape = big_w[nm].shape
        if nm == "w_in":
            f = lax.dynamic_slice_in_dim(f, (sw_in - WIN_STEP) * kme, sw_in, axis=2)
        g = f.reshape(shape)
        if nm == "w_in":
            d, m2, v2 = (a.T for a in _adamw_call("adamw_" + nm, w_t, g.T, m_t, v_t))
        else:
            d, m2, v2 = _adamw_call("adamw_" + nm, big_w[nm], g, big_m[nm], big_v[nm])
        grads[nm], deltas[nm], new_m[nm], new_v[nm] = (a.reshape((1,) + shape) for a in (g, d, m2, v2))

    tot = _sum8_call("sum_small_grads", loc["small"]).reshape(-1)
    off = 0
    sg = {}
    for nm, sz in SMALL:
        sg[nm] = tot[off:off + sz]
        off += sz
    loss = sg.pop("loss")[0]
    sg["w_gate_up"] = lax.dynamic_slice_in_dim(sg["w_gate_up"].reshape(GATE_RANK, GLA_KW), kme * (GLA_KW // 4),
                                               GLA_KW // 4, axis=1)
    sg["meta_tokens"] = lax.dynamic_slice_in_dim(sg["meta_tokens"].reshape(N_META, D_MODEL), kme * (D_MODEL // 4),
                                                 D_MODEL // 4, axis=1)
    small_w = dict(norm_gain=norm_gain, b_gate=b_gate, ret_norm_gain=ret_norm_gain, gla_norm_gain=gla_norm_gain,
                   final_norm_gain=final_norm_gain, w_gate_up=w_gate_up, meta_tokens=meta_tokens)
    small_m = dict(norm_gain=m_norm_gain, b_gate=m_b_gate, ret_norm_gain=m_ret_norm_gain,
                   gla_norm_gain=m_gla_norm_gain, final_norm_gain=m_final_norm_gain, w_gate_up=m_w_gate_up,
                   meta_tokens=m_meta_tokens)
    small_v = dict(norm_gain=v_norm_gain, b_gate=v_b_gate, ret_norm_gain=v_ret_norm_gain,
                   gla_norm_gain=v_gla_norm_gain, final_norm_gain=v_final_norm_gain, w_gate_up=v_w_gate_up,
                   meta_tokens=v_meta_tokens)
    for nm in small_w:
        shape = small_w[nm].shape
        as2d = lambda a: a.reshape((-1, shape[-1]))
        grads[nm] = sg[nm].reshape(shape)
        deltas[nm], new_m[nm], new_v[nm] = (a.reshape(shape) for a in _adamw_call(
            "adamw_" + nm, as2d(small_w[nm]), as2d(sg[nm]), as2d(small_m[nm]), as2d(small_v[nm])))

    out_order = ("meta_tokens", "norm_gain", "w_in", "w_gate_up", "b_gate", "ret_norm_gain", "gla_norm_gain",
                 "w_branch_ret", "w_branch_gla", "w_out", "final_norm_gain")
    dx = loc["dx"].reshape(x.shape)
    return (loss, dx, *[grads[nm] for nm in out_order], *[deltas[nm] for nm in out_order],
            *[new_m[nm] for nm in out_order], *[new_v[nm] for nm in out_order])
```

```python
import math
from typing import Callable, NamedTuple

import numpy as np
import jax
import jax.numpy as jnp
from jax import lax
from jax.experimental import pallas as pl
from jax.experimental.pallas import tpu as pltpu

F32 = jnp.float32
BF16 = jnp.bfloat16

D_MODEL = 1024
N_META = 16
EPS = 1e-6
ROPE_BASE = 10000.0
RET_HEADS, RET_QK, RET_V = 4, 256, 512
RET_W = RET_HEADS * RET_V
GLA_HEADS, GLA_K, GLA_V = 4, 128, 256
GLA_W = GLA_HEADS * GLA_V
GLA_KW = GLA_HEADS * GLA_K
GATE_RANK = 16
GATE_TAU = 16.0
GLA_SUB = 16

TM = 256
T0 = TM
PADF = T0 - N_META
GC = 128
GS = 3
TB = 768
TK = 768

W_R = 6144
W_G = 3088
W_GP = 3200
W_M = 2048
IN_COLS = W_R + W_G + W_M
WIN_STEP = (IN_COLS // 4) // 128 * 128
WIN_W = -(-(3 * (IN_COLS // 4 - WIN_STEP) + IN_COLS // 4) // 128) * 128
IN_PAD = 3 * WIN_STEP + WIN_W

ADAM_LR, ADAM_B1, ADAM_B2, ADAM_EPS, ADAM_WD, ADAM_STEP = 0.001, 0.9, 0.999, 1e-08, 0.01, 10

VMEM_LIMIT = 56 * 1024 * 1024

NN = ((1,), (0,))
NT = ((1,), (1,))
TN = ((0,), (0,))


def _dot(a, b, dims):
    return lax.dot_general(a, b, (dims, ((), ())), preferred_element_type=F32)


def _cparams(n_axes):
    return pltpu.CompilerParams(dimension_semantics=("arbitrary",) * n_axes, vmem_limit_bytes=VMEM_LIMIT)


def _sigmoid(x):
    return 0.5 * jnp.tanh(0.5 * x) + 0.5


def _silu(x):
    h = 0.5 * x
    return h + h * jnp.tanh(h)


def _head_mean(x):
    return jnp.mean(x, axis=-1, keepdims=True)


def _split3(x):
    hi = x.astype(BF16)
    r1 = x - hi.astype(F32)
    mid = r1.astype(BF16)
    lo = (r1 - mid.astype(F32)).astype(BF16)
    return hi, mid, lo


def _exact_pm(p, x):
    hi, mid, lo = _split3(x)
    return _dot(p, hi, NN) + _dot(p, mid, NN) + _dot(p, lo, NN)


def _rms_call(x2d, head, gain, comm):
    tp = T0 + x2d.shape[0]
    nt = tp // TM
    n_xc = len(comm.srcs)

    def body(x_ref, hd_ref, g_ref, *rest):
        xc_src = rest[:n_xc]
        h_ref, u_ref = rest[n_xc:n_xc + 2]
        xc_dst = rest[n_xc + 2:2 * n_xc + 2]
        i = pl.program_id(0)
        begin, finish = comm.make(xc_src, xc_dst, rest[-2], rest[-1])
        pl.when(i == 0)(begin)
        h = jnp.where(i == 0, hd_ref[...], x_ref[...])
        h_ref[...] = h
        r = lax.rsqrt(jnp.mean(h * h, axis=-1, keepdims=True) + EPS)
        u_ref[...] = (h * r * g_ref[...]).astype(BF16)
        pl.when(i == nt - 1)(finish)

    tile = pl.BlockSpec((TM, D_MODEL), lambda i: (i, 0))
    return pl.pallas_call(
        body, name="rms_in", grid=(nt,),
        in_specs=[pl.BlockSpec((TM, D_MODEL), lambda i: (jnp.maximum(i - 1, 0), 0)),
                  pl.BlockSpec((T0, D_MODEL), lambda i: (0, 0)), pl.BlockSpec((1, D_MODEL), lambda i: (0, 0))]
        + [ANY] * n_xc,
        out_specs=[tile, tile] + [ANY] * n_xc,
        out_shape=[jax.ShapeDtypeStruct((tp, D_MODEL), F32), jax.ShapeDtypeStruct((tp, D_MODEL), BF16)]
        + list(comm.out_shapes),
        scratch_shapes=_comm_sems(comm), compiler_params=_cparams(1),
    )(x2d, head, gain, *comm.srcs)


PROJ_ROWS_MAX = 1408


def _proj_rows(m):
    return max(r for r in range(16, PROJ_ROWS_MAX + 1, 16) if m % r == 0)


def _mm_nn(name, a, bt, out_dtype, tn, col0, ncols, epilogue=None, extras=(), extra_specs=()):
    m, k = a.shape
    nj, j0 = ncols // tn, col0 // tn
    tb = _proj_rows(m)

    def body(a_ref, b_ref, *rest):
        *ex, o_ref = rest
        acc = _dot(a_ref[...], b_ref[...], NT)
        if epilogue is None:
            o_ref[...] = acc.astype(out_dtype)
        else:
            epilogue(acc, o_ref, *ex)

    return pl.pallas_call(
        body, name=name, grid=(nj, m // tb),
        in_specs=[pl.BlockSpec((tb, k), lambda j, i: (i, 0)), pl.BlockSpec((tn, k), lambda j, i: (j0 + j, 0))]
        + list(extra_specs),
        out_specs=pl.BlockSpec((tb, tn), lambda j, i: (i, j)),
        out_shape=jax.ShapeDtypeStruct((m, ncols), out_dtype),
        compiler_params=_cparams(2),
    )(a, bt, *extras)


def _rope_tables(tp):
    half = RET_QK // 2
    pos = np.arange(tp, dtype=np.float32) - np.float32(PADF)
    inv = (ROPE_BASE ** (-np.arange(half, dtype=np.float64) / half)).astype(np.float32)
    ang = (pos[:, None] * inv[None, :]).astype(np.float64)
    return np.cos(ang).astype(np.float32), np.sin(ang).astype(np.float32)


def _rope_epilogue(acc, o_ref, cos_ref, sin_ref):
    scale = jnp.where(pl.program_id(0) == 1, RET_QK ** -0.5, 1.0).astype(F32)
    cos, sin = cos_ref[...], sin_ref[...]
    half = RET_QK // 2
    for h in range(RET_HEADS):
        t1 = acc[:, h * RET_QK:h * RET_QK + half]
        t2 = acc[:, h * RET_QK + half:(h + 1) * RET_QK]
        o_ref[:, h * RET_QK:h * RET_QK + half] = ((t1 * cos - t2 * sin) * scale).astype(BF16)
        o_ref[:, h * RET_QK + half:(h + 1) * RET_QK] = ((t2 * cos + t1 * sin) * scale).astype(BF16)


def _gqk_epilogue(acc, o_ref):
    o_ref[:, :GLA_KW] = acc[:, :GLA_KW] * (GLA_K ** -0.5)
    o_ref[:, GLA_KW:] = acc[:, GLA_KW:]


class _Comm(NamedTuple):
    srcs: tuple
    out_shapes: tuple
    n_sems: int
    make: Callable


def _comm_sems(comm):
    return [pltpu.SemaphoreType.DMA((comm.n_sems,)), pltpu.SemaphoreType.DMA((comm.n_sems,))]


def _start_wait(copies):
    def begin():
        for cp in copies:
            cp.start()

    def finish():
        for cp in copies:
            cp.wait()

    return begin, finish


def _other_chips(x, y):
    return [(1 - x, y), (x, 1 - y), (1 - x, 1 - y)]


def _gather_plan(parts, relay=()):
    n = len(parts)
    relay = tuple(relay) + (False,) * (n - len(relay))

    def make(x_refs, out_refs, send_sems, recv_sems):
        x, y, c = _place()
        me, sibling = (x, y, c), (x, y, 1 - c)
        xn, yn, dg = (1 - x, y), (x, 1 - y), (1 - x, 1 - y)

        def slot(t, px, py, pc, half=None):
            ref = out_refs[t].at[4 * px + 2 * py + pc]
            if half is None:
                return ref
            cols = ref.shape[-1] // 2
            return ref.at[:, pl.ds(half * cols, cols)]

        def copy(t, k, dst, to, src=None):
            return pltpu.make_async_remote_copy(
                src_ref=dst if src is None else src, dst_ref=dst, send_sem=send_sems.at[8 * t + k],
                recv_sem=recv_sems.at[8 * t + k], device_id=to, device_id_type=MESH)

        mine = [pltpu.make_async_copy(x_refs[t], slot(t, *me), send_sems.at[8 * n + t]) for t in range(n)]
        sent = []
        for t in range(n):
            sent.append(copy(t, 0, slot(t, *me), sibling, src=x_refs[t]))
            sent.append(copy(t, 1, slot(t, *me), (*xn, c), src=x_refs[t]))
            sent.append(copy(t, 2, slot(t, *me), (*yn, c), src=x_refs[t]))
            if not relay[t]:
                sent.append(copy(t, 3, slot(t, *me), (*dg, c), src=x_refs[t]))

        def begin():
            for cp in mine + sent:
                cp.start()

        def finish():
            later = []

            def start(cp):
                cp.start()
                later.append(cp)

            for t in range(n):
                copy(t, 2, slot(t, *yn, c), me).wait_recv()
                if relay[t]:
                    start(copy(t, 3, slot(t, *yn, c, half=0), (*xn, c)))
                start(copy(t, 6, slot(t, *yn, c), sibling))
            for t in range(n):
                copy(t, 1, slot(t, *xn, c), me).wait_recv()
                if relay[t]:
                    start(copy(t, 4, slot(t, *xn, c, half=1), (*yn, c)))
                start(copy(t, 5, slot(t, *xn, c), sibling))
            for t in range(n):
                if relay[t]:
                    copy(t, 3, slot(t, *dg, c, half=0), me).wait_recv()
                    copy(t, 4, slot(t, *dg, c, half=1), me).wait_recv()
                else:
                    copy(t, 3, slot(t, *dg, c), me).wait_recv()
                start(copy(t, 7, slot(t, *dg, c), sibling))
            for t in range(n):
                copy(t, 0, slot(t, *sibling), me).wait_recv()
                copy(t, 5, slot(t, *xn, 1 - c), me).wait_recv()
                copy(t, 6, slot(t, *yn, 1 - c), me).wait_recv()
                copy(t, 7, slot(t, *dg, 1 - c), me).wait_recv()
            for cp in sent + later:
                cp.wait_send()
            for cp in mine:
                cp.wait()

        return begin, finish

    return _Comm(tuple(parts), tuple(jax.ShapeDtypeStruct((8,) + p.shape, p.dtype) for p in parts), 9 * n, make)


def _exchange_plan(ss):
    def make(s_refs, b_refs, send_sems, recv_sems):
        x, y, c = _place()
        return _start_wait([pltpu.make_async_remote_copy(
            src_ref=s_refs[t].at[2 * chip[0] + chip[1]], dst_ref=b_refs[t].at[j], send_sem=send_sems.at[3 * t + j],
            recv_sem=recv_sems.at[3 * t + j], device_id=(*chip, c), device_id_type=MESH)
            for t in range(len(s_refs)) for j, chip in enumerate(_other_chips(x, y))])

    return _Comm(tuple(ss), tuple(jax.ShapeDtypeStruct((3,) + s.shape[1:], s.dtype) for s in ss), 3 * len(ss), make)


def _exchange_window_plan(s):
    def make(s_refs, b_refs, send_sems, recv_sems):
        x, y, c = _place()
        return _start_wait([pltpu.make_async_remote_copy(
            src_ref=s_refs[0].at[:, pl.ds(pl.multiple_of((2 * chip[0] + chip[1]) * WIN_STEP, 128), WIN_W)],
            dst_ref=b_refs[0].at[j], send_sem=send_sems.at[j], recv_sem=recv_sems.at[j], device_id=(*chip, c),
            device_id_type=MESH) for j, chip in enumerate(_other_chips(x, y))])

    return _Comm((s,), (jax.ShapeDtypeStruct((3, s.shape[0], WIN_W), s.dtype),), 3, make)


def _swap_plan(gs):
    def make(g_refs, b_refs, send_sems, recv_sems):
        x, y, c = _place()
        return _start_wait([pltpu.make_async_remote_copy(
            src_ref=g_refs[t].at[1 - c], dst_ref=b_refs[t], send_sem=send_sems.at[t], recv_sem=recv_sems.at[t],
            device_id=(x, y, 1 - c), device_id_type=MESH) for t in range(len(g_refs))])

    return _Comm(tuple(gs), tuple(jax.ShapeDtypeStruct(g.shape[1:], g.dtype) for g in gs), len(gs), make)


def _spread_plan(parts):
    def make(p_refs, o_refs, send_sems, recv_sems):
        x, y, c = _place()
        copies = []
        for t in range(len(p_refs)):
            mine = o_refs[t].at[4 * x + 2 * y + c]
            copies.append(pltpu.make_async_copy(p_refs[t], mine, send_sems.at[7 * len(p_refs) + t]))
            for r in range(1, 8):
                peer = (1 - x if r & 4 else x, 1 - y if r & 2 else y, 1 - c if r & 1 else c)
                copies.append(pltpu.make_async_remote_copy(
                    src_ref=p_refs[t], dst_ref=mine, send_sem=send_sems.at[7 * t + r - 1],
                    recv_sem=recv_sems.at[7 * t + r - 1], device_id=peer, device_id_type=MESH))
        return _start_wait(copies)

    return _Comm(tuple(parts), tuple(jax.ShapeDtypeStruct((8,) + p.shape, p.dtype) for p in parts), 8 * len(parts),
                 make)


def _mm_nt_acc(name, a, w, tk, acc_in=None, epilogue=None, extras=(), extra_specs=(), extra_out_shapes=(),
               extra_out_specs=(), extra_scratch=(), comm=None, tb=TB):
    m, k = a.shape
    n = w.shape[1]
    nk, ni = k // tk, m // tb
    has_acc = acc_in is not None
    n_xc = len(comm.srcs) if comm else 0
    n_es = len(extra_scratch)

    def body(*refs):
        a_ref, w_ref = refs[0], refs[1]
        pos = 2
        acc_ref = None
        if has_acc:
            acc_ref = refs[pos]
            pos += 1
        ex = refs[pos:pos + len(extras)]
        pos += len(extras)
        xc_src = refs[pos:pos + n_xc]
        pos += n_xc
        n_scr = 1 + n_es + (2 if n_xc else 0)
        outs = refs[pos:len(refs) - n_scr - n_xc]
        xc_dst = refs[len(refs) - n_scr - n_xc:len(refs) - n_scr]
        scr = refs[len(refs) - n_scr]
        es = refs[len(refs) - n_scr + 1:len(refs) - n_scr + 1 + n_es]
        i, kk = pl.program_id(0), pl.program_id(1)
        if n_xc:
            begin, finish = comm.make(xc_src, xc_dst, refs[-2], refs[-1])
            pl.when((i == 0) & (kk == 0))(begin)

        @pl.when(kk == 0)
        def _():
            scr[...] = acc_ref[...] if has_acc else jnp.zeros_like(scr)

        scr[...] += _dot(a_ref[...], w_ref[...], NN)

        @pl.when(kk == nk - 1)
        def _():
            if epilogue is None:
                outs[0][...] = scr[...]
            else:
                epilogue(scr[...], outs, i, ni, *ex, *es)

        if n_xc:
            pl.when((i == ni - 1) & (kk == nk - 1))(finish)

    in_specs = [pl.BlockSpec((tb, tk), lambda i, kk: (i, kk)), pl.BlockSpec((tk, n), lambda i, kk: (kk, 0))]
    args = [a, w]
    if has_acc:
        in_specs.append(pl.BlockSpec((tb, n), lambda i, kk: (i, 0)))
        args.append(acc_in)
    in_specs += list(extra_specs) + [ANY] * n_xc
    args += list(extras) + (list(comm.srcs) if comm else [])
    if epilogue is None:
        out_shape = [jax.ShapeDtypeStruct((m, n), F32)]
        out_specs = [pl.BlockSpec((tb, n), lambda i, kk: (i, 0))]
    else:
        out_shape, out_specs = list(extra_out_shapes), list(extra_out_specs)
    scratch = [pltpu.VMEM((tb, n), F32)] + list(extra_scratch)
    if n_xc:
        out_shape += list(comm.out_shapes)
        out_specs += [ANY] * n_xc
        scratch += _comm_sems(comm)
    return pl.pallas_call(
        body, name=name, grid=(ni, nk), in_specs=in_specs, out_specs=out_specs, out_shape=out_shape,
        scratch_shapes=scratch, compiler_params=_cparams(2),
    )(*args)


def _rms_bwd_epilogue(du, outs, i, ni, h_ref, g_ref, dh1_ref, obuf, sems):
    dx_ref, dmeta_ref, dg_ref = outs
    h = h_ref[...]
    r = lax.rsqrt(jnp.mean(h * h, axis=-1, keepdims=True) + EPS)
    xh = h * r
    dxh = du * g_ref[...]
    dh0 = dh1_ref[...] + r * (dxh - xh * jnp.mean(dxh * xh, axis=-1, keepdims=True))

    def put(slot, tile):
        return pltpu.make_async_copy(obuf.at[slot], dx_ref.at[pl.ds(pl.multiple_of(tile * TB - T0, 8), TB)],
                                     sems.at[slot])

    @pl.when(i == 0)
    def _():
        dg_ref[...] = jnp.zeros_like(dg_ref)
        dmeta_ref[...] = dh0[PADF:T0, :]
        obuf[0] = dh0
        first = pltpu.make_async_copy(obuf.at[0, pl.ds(T0, TB - T0)], dx_ref.at[pl.ds(0, TB - T0)], sems.at[0])
        first.start()
        first.wait()

    @pl.when(i >= 1)
    def _():
        slot = i % 2

        @pl.when(i >= 3)
        def _():
            put(slot, i - 2).wait()

        obuf[slot] = dh0
        put(slot, i).start()

    dg_ref[...] += jnp.sum(du * xh, axis=0, keepdims=True)

    @pl.when(i == ni - 1)
    def _():
        for tile in (ni - 2, ni - 1):
            if tile >= 1:
                put(tile % 2, tile).wait()


def _mm_tn(name, a, b, bn, ncols=None, bcol0=0, into=None, col0=0, out_cols=None):
    t, m = a.shape
    n = ncols or b.shape[1]
    j0, bj0 = col0 // bn, bcol0 // bn

    def body(a_ref, b_ref, *rest):
        o_ref = rest[-1]

        @pl.when(pl.program_id(1) == 0)
        def _():
            o_ref[...] = jnp.zeros_like(o_ref)

        o_ref[...] += _dot(a_ref[...], b_ref[...], TN)

    in_specs = [pl.BlockSpec((TK, m), lambda j, kk: (kk, 0)), pl.BlockSpec((TK, bn), lambda j, kk: (kk, bj0 + j))]
    args = [a, b]
    aliases = {}
    if into is not None:
        in_specs.append(ANY)
        args.append(into)
        aliases = {2: 0}
        out_cols = into.shape[1]
    return pl.pallas_call(
        body, name=name, grid=(n // bn, t // TK), in_specs=in_specs,
        out_specs=pl.BlockSpec((m, bn), lambda j, kk: (0, j0 + j)),
        out_shape=jax.ShapeDtypeStruct((m, out_cols or n), F32), input_output_aliases=aliases,
        compiler_params=_cparams(2),
    )(*args)


def _place_merge_cols_call(dwp, dw_m, dw_glr):
    c0 = W_R + W_GP - 128
    tail = IN_PAD - c0
    rows = 256

    def body(m_ref, low, p_ref, o_ref, buf, sem):
        for r in range(0, D_MODEL, rows):
            buf[r:r + rows, :] = jnp.concatenate(
                [low[r:r + rows, :GATE_RANK], m_ref[r:r + rows, :],
                 jnp.zeros((rows, tail - GATE_RANK - W_M), F32)], axis=1)
        put = pltpu.make_async_copy(buf, o_ref.at[:, pl.ds(c0, tail)], sem)
        put.start()
        put.wait()

    return pl.pallas_call(
        body, name="place_merge_cols",
        in_specs=[pl.BlockSpec(memory_space=pltpu.VMEM), pl.BlockSpec(memory_space=pltpu.VMEM), ANY], out_specs=ANY,
        out_shape=jax.ShapeDtypeStruct(dwp.shape, F32), input_output_aliases={2: 0},
        scratch_shapes=[pltpu.VMEM((D_MODEL, tail), F32), pltpu.SemaphoreType.DMA],
        compiler_params=pltpu.CompilerParams(vmem_limit_bytes=VMEM_LIMIT),
    )(dw_m, dw_glr, dwp)


def _ret_fill_decay(lg_ref, dm_scr):
    c = TM
    ii = lax.broadcasted_iota(jnp.int32, (c, c), 0)
    jj = lax.broadcasted_iota(jnp.int32, (c, c), 1)
    rel = (ii - jj).astype(F32)
    for h in range(RET_HEADS):
        dm_scr[h] = jnp.where(rel >= 0, jnp.exp(jnp.maximum(rel, 0.0) * lg_ref[h]), 0.0)


def _ret_consts(lg, dm_ref):
    c = TM
    idx = lax.broadcasted_iota(jnp.int32, (c, 1), 0).astype(F32)
    xi = jnp.exp((idx + 1.0) * lg)
    zeta = jnp.exp((c - 1.0 - idx) * lg)
    gc = jnp.exp(jnp.full((1, 1), c, F32) * lg)
    return dm_ref[...], xi, zeta, gc


def _ret_fwd_call(rqk, rv, rg, gain, lgam):
    tp = rqk.shape[0]
    nc = tp // TM

    def body(lg_ref, qk_ref, v_ref, rg_ref, g_ref, o_ref, a_ref, st_ref, sc_ref, s_scr, dm_scr):
        @pl.when(pl.program_id(0) == 0)
        def _():
            s_scr[...] = jnp.zeros_like(s_scr)
            _ret_fill_decay(lg_ref, dm_scr)

        for h in range(RET_HEADS):
            dm, xi, zeta, gc = _ret_consts(lg_ref[h], dm_scr.at[h])
            q = qk_ref[:, h * RET_QK:(h + 1) * RET_QK]
            k = qk_ref[:, D_MODEL + h * RET_QK:D_MODEL + (h + 1) * RET_QK]
            v = v_ref[:, h * RET_V:(h + 1) * RET_V]
            sb = s_scr[h].astype(BF16)
            st_ref[0, h] = sb
            s = (_dot(q, k, NT) * dm).astype(BF16)
            sc_ref[0, h] = s
            o = _dot(s, v, NN) + xi * _dot(q, sb, NN)
            kz = (k.astype(F32) * zeta).astype(BF16)
            s_scr[h] = gc * s_scr[h] + _dot(kz, v, TN)
            o_ref[:, h * RET_V:(h + 1) * RET_V] = o
            mu = _head_mean(o)
            xc = o - mu
            xh = xc * lax.rsqrt(_head_mean(xc * xc) + EPS)
            a_ref[:, h * RET_V:(h + 1) * RET_V] = (
                xh * g_ref[:, h * RET_V:(h + 1) * RET_V] * _silu(rg_ref[:, h * RET_V:(h + 1) * RET_V])).astype(BF16)

    return pl.pallas_call(
        body, name="ret_fwd", grid=(nc,),
        in_specs=[pl.BlockSpec(memory_space=pltpu.SMEM),
                  pl.BlockSpec((TM, 2 * D_MODEL), lambda n: (n, 0)),
                  pl.BlockSpec((TM, RET_W), lambda n: (n, 0)),
                  pl.BlockSpec((TM, RET_W), lambda n: (n, 0)),
                  pl.BlockSpec((1, RET_W), lambda n: (0, 0))],
        out_specs=[pl.BlockSpec((TM, RET_W), lambda n: (n, 0)),
                   pl.BlockSpec((TM, RET_W), lambda n: (n, 0)),
                   pl.BlockSpec((1, RET_HEADS, RET_QK, RET_V), lambda n: (n, 0, 0, 0)),
                   pl.BlockSpec((1, RET_HEADS, TM, TM), lambda n: (n, 0, 0, 0))],
        out_shape=[jax.ShapeDtypeStruct((tp, RET_W), F32), jax.ShapeDtypeStruct((tp, RET_W), BF16),
                   jax.ShapeDtypeStruct((nc, RET_HEADS, RET_QK, RET_V), BF16),
                   jax.ShapeDtypeStruct((nc, RET_HEADS, TM, TM), BF16)],
        scratch_shapes=[pltpu.VMEM((RET_HEADS, RET_QK, RET_V), F32), pltpu.VMEM((RET_HEADS, TM, TM), F32)],
        compiler_params=_cparams(1),
    )(lgam, rqk, rv, rg, gain)


def _ret_bwd_call(rqk, rv, rg, o_ret, dpr, wbr, states, scores, gain, lgam, cos, sin):
    tp = rqk.shape[0]
    nc = tp // TM
    half = RET_QK // 2

    def body(lg_ref, qk_ref, v_ref, rg_ref, o_ref, dpr_ref, wbr_ref, st_ref, sc_ref, g_ref, cos_ref, sin_ref, dp_ref,
             dg_ref, ds_scr, dm_scr):
        @pl.when(pl.program_id(0) == 0)
        def _():
            ds_scr[...] = jnp.zeros_like(ds_scr)
            dg_ref[...] = jnp.zeros_like(dg_ref)
            _ret_fill_decay(lg_ref, dm_scr)

        cos, sin = cos_ref[...], sin_ref[...]
        for h in range(RET_HEADS):
            hs = slice(h * RET_V, (h + 1) * RET_V)
            dm, xi, zeta, gc = _ret_consts(lg_ref[h], dm_scr.at[h])
            o = o_ref[:, hs]
            mu = _head_mean(o)
            xc = o - mu
            rstd = lax.rsqrt(_head_mean(xc * xc) + EPS)
            xh = xc * rstd
            gain_h = g_ref[:, hs]
            g = rg_ref[:, hs]
            sg = _sigmoid(g)
            silu = g * sg
            dah = _dot(dpr_ref[...], wbr_ref[hs, :], NT)
            dp_ref[:, 4 * D_MODEL + h * RET_V:4 * D_MODEL + (h + 1) * RET_V] = (
                dah * (xh * gain_h) * (sg * (1.0 + g * (1.0 - sg)))).astype(BF16)
            dn = dah * silu
            dg_ref[:, hs] += jnp.sum(dn * xh, axis=0, keepdims=True)
            dxh = dn * gain_h
            do = rstd * (dxh - _head_mean(dxh) - xh * _head_mean(dxh * xh))
            dob = do.astype(BF16)
            q = qk_ref[:, h * RET_QK:(h + 1) * RET_QK]
            k = qk_ref[:, D_MODEL + h * RET_QK:D_MODEL + (h + 1) * RET_QK]
            v = v_ref[:, hs]
            sp = st_ref[0, h]
            ds = ds_scr[h]
            dsb = ds.astype(BF16)
            s = sc_ref[0, h]
            dsc = (_dot(dob, v, NT) * dm).astype(BF16)
            dq = _dot(dsc, k, NN) + xi * _dot(dob, sp, NT)
            dk = _dot(dsc, q, TN) + zeta * _dot(v, dsb, NT)
            kz = (k.astype(F32) * zeta).astype(BF16)
            dv = _dot(s, dob, TN) + _dot(kz, dsb, NN)
            qx = (q.astype(F32) * xi).astype(BF16)
            ds_scr[h] = gc * ds + _dot(qx, dob, TN)
            dp_ref[:, 2 * D_MODEL + h * RET_V:2 * D_MODEL + (h + 1) * RET_V] = dv.astype(BF16)
            dk = dk * (RET_QK ** -0.5)
            for base, t in ((0, dq), (D_MODEL, dk)):
                t1, t2 = t[:, :half], t[:, half:]
                dp_ref[:, base + h * RET_QK:base + h * RET_QK + half] = (t1 * cos + t2 * sin).astype(BF16)
                dp_ref[:, base + h * RET_QK + half:base + (h + 1) * RET_QK] = (t2 * cos - t1 * sin).astype(BF16)

    rev = lambda n: (nc - 1 - n, 0)
    return pl.pallas_call(
        body, name="ret_bwd", grid=(nc,),
        in_specs=[pl.BlockSpec(memory_space=pltpu.SMEM),
                  pl.BlockSpec((TM, 2 * D_MODEL), rev),
                  pl.BlockSpec((TM, RET_W), rev),
                  pl.BlockSpec((TM, RET_W), rev),
                  pl.BlockSpec((TM, RET_W), rev),
                  pl.BlockSpec((TM, D_MODEL), rev),
                  pl.BlockSpec((RET_W, D_MODEL), lambda n: (0, 0)),
                  pl.BlockSpec((1, RET_HEADS, RET_QK, RET_V), lambda n: (nc - 1 - n, 0, 0, 0)),
                  pl.BlockSpec((1, RET_HEADS, TM, TM), lambda n: (nc - 1 - n, 0, 0, 0)),
                  pl.BlockSpec((1, RET_W), lambda n: (0, 0)),
                  pl.BlockSpec((TM, half), rev),
                  pl.BlockSpec((TM, half), rev)],
        out_specs=[pl.BlockSpec((TM, W_R), rev), pl.BlockSpec((1, RET_W), lambda n: (0, 0))],
        out_shape=[jax.ShapeDtypeStruct((tp, W_R), BF16), jax.ShapeDtypeStruct((1, RET_W), F32)],
        scratch_shapes=[pltpu.VMEM((RET_HEADS, RET_QK, RET_V), F32), pltpu.VMEM((RET_HEADS, TM, TM), F32)],
        compiler_params=_cparams(1),
    )(lgam, rqk, rv, rg, o_ret, dpr, wbr, states, scores, gain, cos, sin)


GLA_LEVELS = tuple(GC >> (s + 1) for s in range(int(math.log2(GC // GLA_SUB))))
NLEV = len(GLA_LEVELS)


def _gla_tril():
    return np.tril(np.ones((GC, GC), np.float32))


def _gla_masks():
    ii = lax.broadcasted_iota(jnp.int32, (GC, GC), 0)
    jj = lax.broadcasted_iota(jnp.int32, (GC, GC), 1)
    masks = []
    for m in GLA_LEVELS:
        sh = int(math.log2(2 * m))
        masks.append(((ii >> sh) == (jj >> sh)) & ((ii & m) != 0) & ((jj & m) == 0))
    sh = int(math.log2(GLA_SUB))
    md = ((ii >> sh) == (jj >> sh)) & (jj <= ii)
    row = lax.broadcasted_iota(jnp.int32, (GC, 1), 0)
    second = [(row & m) != 0 for m in GLA_LEVELS]
    return masks, md, second


def _gla_gate_call(u, w_g, wg, bg, pmat):
    tp = u.shape[0]
    gb = _proj_rows(tp)
    assert gb % GC == 0

    def body(u_ref, w_ref, wg_ref, bg_ref, p_ref, glr_ref, z_ref, b_ref):
        glr = _dot(u_ref[...], w_ref[...], NT)
        glr_ref[...] = glr
        z = _dot(glr.astype(BF16), wg_ref[...], NN) + bg_ref[...]
        z_ref[...] = z
        la = (jnp.minimum(z, 0.0) - jnp.log1p(jnp.exp(-jnp.abs(z)))) * (1.0 / GATE_TAU)
        for r in range(0, gb, GC):
            b_ref[r:r + GC, :] = _exact_pm(p_ref[...], la[r:r + GC, :])

    tile = pl.BlockSpec((gb, GLA_KW), lambda i: (i, 0))
    return pl.pallas_call(
        body, name="gla_gate", grid=(tp // gb,),
        in_specs=[pl.BlockSpec((gb, D_MODEL), lambda i: (i, 0)),
                  pl.BlockSpec((128, D_MODEL), lambda i: ((W_GP - 128) // 128, 0)),
                  pl.BlockSpec((128, GLA_KW), lambda i: (0, 0)),
                  pl.BlockSpec((1, GLA_KW), lambda i: (0, 0)), pl.BlockSpec((GC, GC), lambda i: (0, 0))],
        out_specs=[pl.BlockSpec((gb, 128), lambda i: (i, 0)), tile, tile],
        out_shape=[jax.ShapeDtypeStruct((tp, 128), F32), jax.ShapeDtypeStruct((tp, GLA_KW), F32),
                   jax.ShapeDtypeStruct((tp, GLA_KW), F32)],
        compiler_params=_cparams(1),
    )(u, w_g, wg, bg, pmat)


def _gla_gate_bwd_call(db, z, glr, u, wg, pmat_t, d_g):
    tp = db.shape[0]
    gb = _proj_rows(tp)
    assert gb % GC == 0 and (W_GP - 128) % 128 == 0

    def body(db_ref, z_ref, glr_ref, u_ref, wg_ref, pt_ref, dgin_ref, dg_ref, dwg_ref, dbg_ref, dwl_ref):
        i = pl.program_id(0)

        @pl.when(i == 0)
        def _():
            dwg_ref[...] = jnp.zeros_like(dwg_ref)
            dbg_ref[...] = jnp.zeros_like(dbg_ref)
            dwl_ref[...] = jnp.zeros_like(dwl_ref)

        dla = jnp.concatenate([_exact_pm(pt_ref[...], db_ref[r:r + GC, :]) for r in range(0, gb, GC)], axis=0)
        row = i * gb + lax.broadcasted_iota(jnp.int32, (gb, 1), 0)
        dz = jnp.where(row >= PADF, dla * (1.0 / GATE_TAU) * _sigmoid(-z_ref[...]), 0.0)
        dzb = dz.astype(BF16)
        dglr = _dot(dzb, wg_ref[...], NT).astype(BF16)
        dg_ref[...] = dglr
        dwg_ref[...] += _dot(glr_ref[...].astype(BF16), dzb, TN)
        dbg_ref[...] += jnp.sum(dz, axis=0, keepdims=True)
        dwl_ref[...] += _dot(u_ref[...], dglr, TN)

    tile = pl.BlockSpec((gb, GLA_KW), lambda i: (i, 0))
    const = lambda i: (0, 0)
    return pl.pallas_call(
        body, name="gla_gate_bwd", grid=(tp // gb,),
        in_specs=[tile, tile, pl.BlockSpec((gb, 128), lambda i: (i, 0)), pl.BlockSpec((gb, D_MODEL), lambda i: (i, 0)),
                  pl.BlockSpec((128, GLA_KW), const), pl.BlockSpec((GC, GC), const), ANY],
        out_specs=[pl.BlockSpec((gb, 128), lambda i: (i, (W_GP - 128) // 128)), pl.BlockSpec((128, GLA_KW), const),
                   pl.BlockSpec((1, GLA_KW), const), pl.BlockSpec((D_MODEL, 128), const)],
        out_shape=[jax.ShapeDtypeStruct(d_g.shape, BF16), jax.ShapeDtypeStruct((128, GLA_KW), F32),
                   jax.ShapeDtypeStruct((1, GLA_KW), F32), jax.ShapeDtypeStruct((D_MODEL, 128), F32)],
        input_output_aliases={6: 0}, compiler_params=_cparams(1),
    )(db, z, glr, u, wg, pmat_t, d_g)


def _gla_row_steps(b_ref, cs, rows, size):
    parts = [jnp.zeros((size, GLA_K), F32) if r is None else jnp.broadcast_to(b_ref[r:r + 1, cs], (size, GLA_K))
             for r in rows]
    return parts[0] if len(parts) == 1 else jnp.concatenate(parts, axis=0)


def _gla_factors(b_ref, h, second):
    cs = slice(h * GLA_K, (h + 1) * GLA_K)
    b = b_ref[:, cs]
    fq, fk = [], []
    for l, m in enumerate(GLA_LEVELS):
        d = b - _gla_row_steps(b_ref, cs, [s + m - 1 for s in range(0, GC, 2 * m)], 2 * m)
        f = jnp.exp(jnp.where(second[l], d, -d))
        fq.append(jnp.where(second[l], f, 0.0))
        fk.append(jnp.where(second[l], 0.0, f))
    dd = b - _gla_row_steps(b_ref, cs, [None] + [s - 1 for s in range(GLA_SUB, GC, GLA_SUB)], GLA_SUB)
    ed = jnp.exp(dd)
    edi = jnp.exp(-dd)
    eb = jnp.exp(b)
    bl = b_ref[GC - 1:GC, cs]
    ee = jnp.exp(bl - b)
    ebl = jnp.exp(bl)
    return fq, fk, ed, edi, eb, ee, ebl


def _gla_scaled(q, k, fq, fk, ed, edi):
    qt = [(q * f).astype(BF16) for f in fq]
    kt = [(k * f).astype(BF16) for f in fk]
    return qt, kt, (q * ed).astype(BF16), (k * edi).astype(BF16)


def _gla_scores(qt, kt, qd, kd, masks, md):
    a = jnp.where(md, _dot(qd, kd, NT), 0.0)
    for l in range(NLEV):
        a = a + jnp.where(masks[l], _dot(qt[l], kt[l], NT), 0.0)
    return a.astype(BF16)


def _gla_fwd_call(gqk, gv, b, gg, gain, comm=None):
    tp = gqk.shape[0]
    nc = tp // GC
    ns = nc // GS
    n_xc = len(comm.srcs) if comm else 0

    def body(qk_ref, v_ref, b_ref, gg_ref, g_ref, *rest):
        xc_src = rest[:n_xc]
        o_ref, a_ref, st_ref, am_ref = rest[n_xc:n_xc + 4]
        xc_dst = rest[n_xc + 4:2 * n_xc + 4]
        s_scr = rest[2 * n_xc + 4]
        n = pl.program_id(0)
        if n_xc:
            begin, finish = comm.make(xc_src, xc_dst, rest[-2], rest[-1])
            pl.when(n == 0)(begin)
            pl.when(n == ns - 1)(finish)

        @pl.when(n == 0)
        def _():
            s_scr[...] = jnp.zeros_like(s_scr)

        masks, md, second = _gla_masks()
        for cc in range(GS):
            rows = pl.ds(cc * GC, GC)
            qk_c, v_c, b_c, gg_c, o_c, a_c = (r.at[rows] for r in (qk_ref, v_ref, b_ref, gg_ref, o_ref, a_ref))
            for h in range(GLA_HEADS):
                q = qk_c[:, h * GLA_K:(h + 1) * GLA_K]
                k = qk_c[:, GLA_KW + h * GLA_K:GLA_KW + (h + 1) * GLA_K]
                vs = slice(h * GLA_V, (h + 1) * GLA_V)
                v = v_c[:, vs]
                fq, fk, ed, edi, eb, ee, ebl = _gla_factors(b_c, h, second)
                a = _gla_scores(*_gla_scaled(q, k, fq, fk, ed, edi), masks, md)
                am_ref[cc, h] = a
                sb = s_scr[h].astype(BF16)
                st_ref[cc, h] = sb
                o = _dot(a, v, NN) + _dot((q * eb).astype(BF16), sb, NT)
                s_scr[h] = s_scr[h] * ebl + _dot(v, (k * ee).astype(BF16), TN)
                o_c[:, vs] = o
                xh = o * lax.rsqrt(_head_mean(o * o) + EPS)
                a_c[:, vs] = (xh * g_ref[:, vs] * _silu(gg_c[:, vs])).astype(BF16)

    return pl.pallas_call(
        body, name="gla_fwd", grid=(ns,),
        in_specs=[pl.BlockSpec((GS * GC, 2 * GLA_KW), lambda n: (n, 0)),
                  pl.BlockSpec((GS * GC, GLA_W), lambda n: (n, 0)),
                  pl.BlockSpec((GS * GC, GLA_KW), lambda n: (n, 0)),
                  pl.BlockSpec((GS * GC, GLA_W), lambda n: (n, 0)),
                  pl.BlockSpec((1, GLA_W), lambda n: (0, 0))] + [ANY] * n_xc,
        out_specs=[pl.BlockSpec((GS * GC, GLA_W), lambda n: (n, 0)),
                   pl.BlockSpec((GS * GC, GLA_W), lambda n: (n, 0)),
                   pl.BlockSpec((GS, GLA_HEADS, GLA_V, GLA_K), lambda n: (n, 0, 0, 0)),
                   pl.BlockSpec((GS, GLA_HEADS, GC, GC), lambda n: (n, 0, 0, 0))] + [ANY] * n_xc,
        out_shape=[jax.ShapeDtypeStruct((tp, GLA_W), F32), jax.ShapeDtypeStruct((tp, GLA_W), BF16),
                   jax.ShapeDtypeStruct((nc, GLA_HEADS, GLA_V, GLA_K), BF16),
                   jax.ShapeDtypeStruct((nc, GLA_HEADS, GC, GC), BF16)] + (list(comm.out_shapes) if comm else []),
        scratch_shapes=[pltpu.VMEM((GLA_HEADS, GLA_V, GLA_K), F32)] + (_comm_sems(comm) if comm else []),
        compiler_params=_cparams(1),
    )(gqk, gv, b, gg, gain, *(comm.srcs if comm else ()))


def _gla_bwd_call(gqk, gv, b, gg, o_gla, da, states, scores, gain, comm=None):
    tp = gqk.shape[0]
    nc = tp // GC
    ns = nc // GS
    o_gv, o_gg = 2 * GLA_KW, 2 * GLA_KW + GLA_W
    n_xc = len(comm.srcs) if comm else 0

    def body(qk_all, v_all, b_all, gg_all, o_all, da_all, st_ref, am_ref, g_ref, *rest):
        xc_src = rest[:n_xc]
        dp_all, db_all, dg_ref = rest[n_xc:n_xc + 3]
        xc_dst = rest[n_xc + 3:2 * n_xc + 3]
        ds_scr = rest[2 * n_xc + 3]
        n = pl.program_id(0)
        if n_xc:
            begin, finish = comm.make(xc_src, xc_dst, rest[-2], rest[-1])
            pl.when(n == 0)(begin)
            pl.when(n == ns - 1)(finish)

        @pl.when(n == 0)
        def _():
            ds_scr[...] = jnp.zeros_like(ds_scr)
            dg_ref[...] = jnp.zeros_like(dg_ref)

        masks, md, second = _gla_masks()
        for cc, h in [(cc, h) for cc in reversed(range(GS)) for h in range(GLA_HEADS)]:
            rows = pl.ds(cc * GC, GC)
            qk_ref, v_ref, b_scr, gg_ref, o_ref, da_ref, dp_ref, db_scr = (
                r.at[rows] for r in (qk_all, v_all, b_all, gg_all, o_all, da_all, dp_all, db_all))
            cs = slice(h * GLA_K, (h + 1) * GLA_K)
            vs = slice(h * GLA_V, (h + 1) * GLA_V)
            o = o_ref[:, vs]
            rstd = lax.rsqrt(_head_mean(o * o) + EPS)
            xh = o * rstd
            gain_h = g_ref[:, vs]
            g = gg_ref[:, vs]
            sg = _sigmoid(g)
            dah = da_ref[:, vs]
            dp_ref[:, o_gg + h * GLA_V:o_gg + (h + 1) * GLA_V] = (
                dah * (xh * gain_h) * (sg * (1.0 + g * (1.0 - sg)))).astype(BF16)
            dn = dah * (g * sg)
            dg_ref[:, vs] += jnp.sum(dn * xh, axis=0, keepdims=True)
            dxh = dn * gain_h
            do = rstd * (dxh - xh * _head_mean(dxh * xh))
            dob = do.astype(BF16)
            q = qk_ref[:, cs]
            k = qk_ref[:, GLA_KW + h * GLA_K:GLA_KW + (h + 1) * GLA_K]
            v = v_ref[:, vs]
            fq, fk, ed, edi, eb, ee, ebl = _gla_factors(b_scr, h, second)
            qt, kt, qd, kd = _gla_scaled(q, k, fq, fk, ed, edi)
            sp = st_ref[cc, h]
            ds = ds_scr[h]
            dsb = ds.astype(BF16)
            q_in = q * eb
            k_end = k * ee
            da_s = _dot(dob, v, NT)
            dv = _dot(am_ref[cc, h], dob, TN) + _dot(k_end.astype(BF16), dsb, NT)
            dq_in = _dot(dob, sp, NN)
            dk_end = _dot(v, dsb, NN)
            dbl = jnp.sum(sp.astype(F32) * ds, axis=0, keepdims=True) * ebl
            ds_scr[h] = ds * ebl + _dot(dob, q_in.astype(BF16), TN)
            dq = dq_in * eb
            dk = dk_end * ee
            de_end = dk_end * k_end
            db = dq_in * q_in - de_end
            placed = [(GC - 1, jnp.sum(de_end, axis=0, keepdims=True) + dbl)]
            for l, m in enumerate(GLA_LEVELS):
                dal = jnp.where(masks[l], da_s, 0.0).astype(BF16)
                dqt = _dot(dal, kt[l], NN)
                dkt = _dot(dal, qt[l], TN)
                dq = dq + dqt * fq[l]
                dk = dk + dkt * fk[l]
                gl = dqt * (q * fq[l]) - dkt * (k * fk[l])
                db = db + gl
                placed += [(s + m - 1, -jnp.sum(gl[s:s + 2 * m], axis=0, keepdims=True)) for s in range(0, GC, 2 * m)]
            dad = jnp.where(md, da_s, 0.0).astype(BF16)
            dqd = _dot(dad, kd, NN)
            dkd = _dot(dad, qd, TN)
            dq = dq + dqd * ed
            dk = dk + dkd * edi
            gd = dqd * (q * ed) - dkd * (k * edi)
            db = db + gd
            placed += [(s - 1, -jnp.sum(gd[s:s + GLA_SUB], axis=0, keepdims=True)) for s in range(GLA_SUB, GC, GLA_SUB)]
            db_scr[:, cs] = db
            for r, val in placed:
                db_scr[r:r + 1, cs] += val
            dp_ref[:, cs] = (dq * (GLA_K ** -0.5)).astype(BF16)
            dp_ref[:, GLA_KW + h * GLA_K:GLA_KW + (h + 1) * GLA_K] = dk.astype(BF16)
            dp_ref[:, o_gv + h * GLA_V:o_gv + (h + 1) * GLA_V] = dv.astype(BF16)

    rev = lambda n: (ns - 1 - n, 0)
    const = lambda n: (0, 0)
    xc_shapes, xc_sems = (list(comm.out_shapes), _comm_sems(comm)) if n_xc else ([], [])
    return pl.pallas_call(
        body, name="gla_bwd", grid=(ns,),
        in_specs=[pl.BlockSpec((GS * GC, 2 * GLA_KW), rev),
                  pl.BlockSpec((GS * GC, GLA_W), rev),
                  pl.BlockSpec((GS * GC, GLA_KW), rev),
                  pl.BlockSpec((GS * GC, GLA_W), rev),
                  pl.BlockSpec((GS * GC, GLA_W), rev),
                  pl.BlockSpec((GS * GC, GLA_W), rev),
                  pl.BlockSpec((GS, GLA_HEADS, GLA_V, GLA_K), lambda n: (ns - 1 - n, 0, 0, 0)),
                  pl.BlockSpec((GS, GLA_HEADS, GC, GC), lambda n: (ns - 1 - n, 0, 0, 0)),
                  pl.BlockSpec((1, GLA_W), const)] + [ANY] * n_xc,
        out_specs=[pl.BlockSpec((GS * GC, W_GP), rev), pl.BlockSpec((GS * GC, GLA_KW), rev),
                   pl.BlockSpec((1, GLA_W), const)] + [ANY] * n_xc,
        out_shape=[jax.ShapeDtypeStruct((tp, W_GP), BF16), jax.ShapeDtypeStruct((tp, GLA_KW), F32),
                   jax.ShapeDtypeStruct((1, GLA_W), F32)] + xc_shapes,
        scratch_shapes=[pltpu.VMEM((GLA_HEADS, GLA_V, GLA_K), F32)] + xc_sems,
        compiler_params=_cparams(1),
    )(gqk, gv, b, gg, o_gla, da, states, scores, gain, *(comm.srcs if comm else ()))


def _mid_call(a_ret, a_gla, mg, h0, tgt, wbr, wbg, wout, gf):
    tp = h0.shape[0]
    nt = tp // TM

    def body(ar_ref, ag_ref, mg_ref, h_ref, t_ref, wbr_ref, wbg_ref, wo_ref, gf_ref,
             dh1_ref, dag_ref, dm_ref, mb_ref, dh1b_ref, dprb_ref, dpgb_ref, loss_ref, dgf_ref):
        i = pl.program_id(0)

        @pl.when(i == 0)
        def _():
            loss_ref[...] = jnp.zeros_like(loss_ref)
            dgf_ref[...] = jnp.zeros_like(dgf_ref)

        ar, ag = ar_ref[...], ag_ref[...]
        pr = _dot(ar, wbr_ref[...], NN)
        pg = _dot(ag, wbg_ref[...], NN)
        sr = _sigmoid(mg_ref[:, :D_MODEL])
        sg = _sigmoid(mg_ref[:, D_MODEL:])
        merged = (sr * pr + sg * pg).astype(BF16)
        mb_ref[...] = merged
        h1 = h_ref[...] + _dot(merged, wo_ref[...], NN)
        r1 = lax.rsqrt(jnp.mean(h1 * h1, axis=-1, keepdims=True) + EPS)
        xh = h1 * r1
        gfv = gf_ref[...]
        live = jnp.where(i > 0, 1.0, 0.0).astype(F32)
        err = (xh * gfv - t_ref[...]) * live
        loss_ref[...] += jnp.full(loss_ref.shape, 0.5 / D_MODEL, F32) * jnp.sum(err * err)
        dy = err * (1.0 / D_MODEL)
        dgf_ref[...] += jnp.sum(dy * xh, axis=0, keepdims=True)
        dxh = dy * gfv
        dh1 = r1 * (dxh - xh * jnp.mean(dxh * xh, axis=-1, keepdims=True))
        dh1_ref[...] = dh1
        dh1b = dh1.astype(BF16)
        dh1b_ref[...] = dh1b
        dmerged = _dot(dh1b, wo_ref[...], NT)
        dm_ref[:, :D_MODEL] = (dmerged * pr * sr * (1.0 - sr)).astype(BF16)
        dm_ref[:, D_MODEL:] = (dmerged * pg * sg * (1.0 - sg)).astype(BF16)
        dpr = (dmerged * sr).astype(BF16)
        dpg = (dmerged * sg).astype(BF16)
        dprb_ref[...] = dpr
        dpgb_ref[...] = dpg
        dag_ref[...] = _dot(dpg, wbg_ref[...], NT)

    tile = lambda w: pl.BlockSpec((TM, w), lambda i: (i, 0))
    const = lambda r, w: pl.BlockSpec((r, w), lambda i: (0, 0))
    return pl.pallas_call(
        body, name="merge_out_loss", grid=(nt,),
        in_specs=[tile(RET_W), tile(GLA_W), tile(W_M), tile(D_MODEL),
                  pl.BlockSpec((TM, D_MODEL), lambda i: (jnp.maximum(i - 1, 0), 0)),
                  const(RET_W, D_MODEL), const(GLA_W, D_MODEL), const(D_MODEL, D_MODEL), const(1, D_MODEL)],
        out_specs=[tile(D_MODEL), tile(GLA_W), tile(W_M), tile(D_MODEL), tile(D_MODEL), tile(D_MODEL),
                   tile(D_MODEL), const(1, 128), const(1, D_MODEL)],
        out_shape=[jax.ShapeDtypeStruct((tp, D_MODEL), F32), jax.ShapeDtypeStruct((tp, GLA_W), F32),
                   jax.ShapeDtypeStruct((tp, W_M), BF16),
                   jax.ShapeDtypeStruct((tp, D_MODEL), BF16), jax.ShapeDtypeStruct((tp, D_MODEL), BF16),
                   jax.ShapeDtypeStruct((tp, D_MODEL), BF16), jax.ShapeDtypeStruct((tp, D_MODEL), BF16),
                   jax.ShapeDtypeStruct((1, 128), F32), jax.ShapeDtypeStruct((1, D_MODEL), F32)],
        compiler_params=_cparams(1),
    )(a_ret, a_gla, mg, h0, tgt, wbr, wbg, wout, gf)


def _device_step(x2d, tgt2d, meta, norm_gain, w_in_part, w_gate_up, b_gate, ret_gain, gla_gain, branch_parts,
                 final_gain, ck):
    seq = x2d.shape[0]
    tp = T0 + seq
    head = jnp.concatenate([jnp.zeros((PADF, D_MODEL), F32), meta], axis=0)
    wg_pad = jnp.pad(w_gate_up, ((0, 128 - GATE_RANK), (0, 0))).astype(BF16)

    half = RET_QK // 2
    cos, sin = (jnp.asarray(t) for t in _rope_tables(tp))
    lgam = jnp.log1p(-(2.0 ** (-5.0 - jnp.arange(RET_HEADS, dtype=F32))))
    pmat = jnp.asarray(_gla_tril(), BF16)
    pmat_t = jnp.asarray(_gla_tril().T.copy(), BF16)

    h0, u, g_in = _rms_call(x2d, head, norm_gain, _gather_plan([w_in_part], relay=(True,)))
    sw, hc = w_in_part.shape
    def rows_of(lo, hi):
        parts = []
        for k in range(4):
            a, b = max(lo, k * sw), min(hi, (k + 1) * sw)
            if a < b:
                parts.append(g_in[2 * k:2 * k + 2, a - k * sw:b - k * sw].transpose(1, 0, 2).reshape(b - a, 2 * hc))
        return parts[0] if len(parts) == 1 else jnp.concatenate(parts, axis=0)

    w_r = rows_of(0, W_R)
    w_g = jnp.pad(rows_of(W_R, W_R + W_G), ((0, W_GP - W_G), (0, 0)))
    w_m = rows_of(W_R + W_G, IN_COLS)
    tab = pl.BlockSpec((_proj_rows(tp), half), lambda j, i: (i, 0))
    rqk = _mm_nn("proj_rqk", u, w_r, BF16, D_MODEL, 0, 2 * D_MODEL, _rope_epilogue, (cos, sin), (tab, tab))
    rv = _mm_nn("proj_rv", u, w_r, BF16, RET_W, 2 * D_MODEL, RET_W)
    rg = _mm_nn("proj_rg", u, w_r, F32, RET_W, 4 * D_MODEL, RET_W)
    gqk = _mm_nn("proj_gqk", u, w_g, F32, 2 * GLA_KW, 0, 2 * GLA_KW, _gqk_epilogue)
    gv = _mm_nn("proj_gv", u, w_g, BF16, GLA_W, 2 * GLA_KW, GLA_W)
    gg = _mm_nn("proj_gg", u, w_g, F32, GLA_W, 2 * GLA_KW + GLA_W, GLA_W)
    mg = _mm_nn("proj_mg", u, w_m, F32, W_M, 0, W_M)

    o_ret, a_ret, st_ret, sc_ret = _ret_fwd_call(rqk, rv, rg, ret_gain, lgam)
    glr, z_gate, b_dec = _gla_gate_call(u, w_g, wg_pad, b_gate, pmat)
    o_gla, a_gla, st_gla, sc_gla, g_br, g_bg, g_out = _gla_fwd_call(gqk, gv, b_dec, gg, gla_gain,
                                                                    comm=_spread_plan(branch_parts))
    wbr = g_br.reshape(RET_W, D_MODEL)
    wbg = g_bg.reshape(GLA_W, D_MODEL)
    wout = g_out.reshape(D_MODEL, D_MODEL)

    gf = final_gain.reshape(1, D_MODEL)
    (dh1, da_gla, dm, merged_b, dh1_b, dpr_b, dpg_b, loss, dgf) = _mid_call(
        a_ret, a_gla, mg, h0, tgt2d, wbr, wbg, wout, gf)

    names_b = ("w_branch_ret", "w_branch_gla", "w_out")
    g2_b = [_mm_tn("dw_br", a_ret, dpr_b, D_MODEL).reshape(4, 2, RET_W // 8, D_MODEL).transpose(1, 0, 2, 3),
            _mm_tn("dw_bg", a_gla, dpg_b, D_MODEL).reshape(4, 2, GLA_W // 8, D_MODEL).transpose(1, 0, 2, 3),
            _mm_tn("dw_out", merged_b, dh1_b, D_MODEL).reshape(4, 2, D_MODEL // 8, D_MODEL).transpose(1, 0, 2, 3)]
    sib_b = _swap_halves_call("swap_halves_branch", g2_b)
    sum_b = [_add_half_call("add_half_" + nm, g, b, ck) for nm, g, b in zip(names_b, g2_b, sib_b)]
    d_g, db_dec, dgla_gain, *chips_b = _gla_bwd_call(gqk, gv, b_dec, gg, o_gla, da_gla, st_gla, sc_gla, gla_gain,
                                                     comm=_exchange_plan(sum_b))
    d_g, dwg, dbg, dw_glr = _gla_gate_bwd_call(db_dec, z_gate, glr, u, wg_pad, pmat_t, d_g)
    mine = [_add_chips_call("add_chips_" + nm, g, b, p, ck) for nm, g, b, p in zip(names_b, g2_b, sib_b, chips_b)]

    d_r, dret_gain = _ret_bwd_call(rqk, rv, rg, o_ret, dpr_b, wbr, st_ret, sc_ret, ret_gain, lgam, cos, sin)

    dwp = _mm_tn("dw_r", u, d_r, 3 * D_MODEL, out_cols=IN_PAD)
    dwp = _mm_tn("dw_g", u, d_g, 3 * D_MODEL, ncols=W_GP - 128, into=dwp, col0=W_R)
    g2_in = _place_merge_cols_call(dwp, _mm_tn("dw_m", u, dm, 2 * D_MODEL), dw_glr).reshape(2, D_MODEL // 2, IN_PAD)

    du, sib_in = _mm_nt_acc("du_g", d_g, w_g, W_GP, comm=_swap_plan([g2_in]), tb=_proj_rows(tp))
    sum_in = _add_rows_call("add_half_w_in", g2_in, sib_in, ck)
    du, chips_in = _mm_nt_acc("du_r", d_r, w_r, 2 * D_MODEL, acc_in=du, comm=_exchange_window_plan(sum_in),
                              tb=_proj_rows(tp))
    tile = pl.BlockSpec((TB, D_MODEL), lambda i, kk: (i, 0))
    row = pl.BlockSpec((1, D_MODEL), lambda i, kk: (0, 0))
    dx, dmeta, dnorm_gain = _mm_nt_acc(
        "du_m", dm, w_m, W_M, acc_in=du, epilogue=_rms_bwd_epilogue, extras=(h0, norm_gain, dh1),
        extra_specs=(tile, row, tile),
        extra_out_shapes=(jax.ShapeDtypeStruct((seq, D_MODEL), F32), jax.ShapeDtypeStruct((N_META, D_MODEL), F32),
                          jax.ShapeDtypeStruct((1, D_MODEL), F32)),
        extra_out_specs=(ANY, pl.BlockSpec((N_META, D_MODEL), lambda i, kk: (0, 0)), row),
        extra_scratch=(pltpu.VMEM((2, TB, D_MODEL), F32), pltpu.SemaphoreType.DMA((2,))))
    small = dict(norm_gain=dnorm_gain, b_gate=dbg, ret_norm_gain=dret_gain, gla_norm_gain=dgla_gain,
                 final_norm_gain=dgf, w_gate_up=dwg[:GATE_RANK], meta_tokens=dmeta, loss=loss[0, 0])
    rows = -(-sum(sz for _, sz in SMALL) // 128 // 8) * 8
    mine_in, g_small = _add_window_call("add_chips_w_in", g2_in, sib_in, chips_in, ck,
                                        _gather_plan([_pack_rows([small[nm] for nm, _ in SMALL], rows)]))
    full = _join_halves_call("join_halves", [mine_in] + mine)

    return dict(dx=dx, small=g_small, w_in=full[0], w_branch_ret=full[1], w_branch_gla=full[2], w_out=full[3])


MESH = pl.DeviceIdType.MESH
ANY = pl.BlockSpec(memory_space=pl.ANY)


def _place():
    return lax.axis_index("x"), lax.axis_index("y"), lax.axis_index("c")


def _gather8_call(name, parts):
    comm = _gather_plan(parts)
    n = len(parts)

    def body(*refs):
        begin, finish = comm.make(refs[:n], refs[n:2 * n], refs[-2], refs[-1])
        begin()
        finish()

    return pl.pallas_call(
        body, name=name, out_shape=list(comm.out_shapes), in_specs=[ANY] * n, out_specs=[ANY] * n,
        scratch_shapes=_comm_sems(comm),
    )(*parts)


def _swap_halves_call(name, gs):
    n = len(gs)

    def body(*refs):
        g_refs, b_refs = refs[:n], refs[n:2 * n]
        send_sems, recv_sems = refs[2 * n:]
        x, y, c = _place()
        copies = [pltpu.make_async_remote_copy(
            src_ref=g_refs[t].at[1 - c], dst_ref=b_refs[t], send_sem=send_sems.at[t], recv_sem=recv_sems.at[t],
            device_id=(x, y, 1 - c), device_id_type=MESH) for t in range(n)]
        for cp in copies:
            cp.start()
        for cp in copies:
            cp.wait()

    return pl.pallas_call(
        body, name=name,
        out_shape=[jax.ShapeDtypeStruct(g.shape[1:], g.dtype) for g in gs],
        in_specs=[ANY] * n, out_specs=[ANY] * n,
        scratch_shapes=[pltpu.SemaphoreType.DMA((n,)), pltpu.SemaphoreType.DMA((n,))],
    )(*gs)


def _join_halves_call(name, ts):
    n = len(ts)

    def body(*refs):
        o_refs = refs[n:2 * n]
        send_sems, recv_sems = refs[2 * n:]
        x, y, c = _place()
        copies = [pltpu.make_async_remote_copy(
            src_ref=o_refs[t].at[c], dst_ref=o_refs[t].at[c], send_sem=send_sems.at[t], recv_sem=recv_sems.at[t],
            device_id=(x, y, 1 - c), device_id_type=MESH) for t in range(n)]
        for cp in copies:
            cp.start()
        for t in range(n):
            copies[t].wait_send()
            pltpu.make_async_remote_copy(
                src_ref=o_refs[t].at[c], dst_ref=o_refs[t].at[1 - c], send_sem=send_sems.at[t],
                recv_sem=recv_sems.at[t], device_id=(x, y, 1 - c), device_id_type=MESH).wait_recv()

    return pl.pallas_call(
        body, name=name,
        out_shape=[jax.ShapeDtypeStruct(t.shape, t.dtype) for t in ts],
        in_specs=[ANY] * n, out_specs=[ANY] * n, input_output_aliases={t: t for t in range(n)},
        scratch_shapes=[pltpu.SemaphoreType.DMA((n,)), pltpu.SemaphoreType.DMA((n,))],
    )(*ts)


def _row_block(rows, cols, budget):
    best = 8
    for rb in range(8, rows + 1, 8):
        if rows % rb == 0 and rb * cols * 4 <= budget:
            best = rb
    return best


def _add_half_call(name, g, b, ck):
    _, _, r, cc = g.shape
    rb = _row_block(r, cc, 2 * 1024 * 1024)

    def body(ck_ref, g_ref, b_ref, o_ref):
        o_ref[...] = (g_ref[...] + b_ref[...]).astype(BF16)

    return pl.pallas_call(
        body, name=name,
        grid_spec=pltpu.PrefetchScalarGridSpec(
            num_scalar_prefetch=1, grid=(4, r // rb),
            in_specs=[pl.BlockSpec((None, None, rb, cc), lambda k, i, ck_ref: (ck_ref[0], k, i, 0)),
                      pl.BlockSpec((None, rb, cc), lambda k, i, ck_ref: (k, i, 0))],
            out_specs=pl.BlockSpec((None, rb, cc), lambda k, i, ck_ref: (k, i, 0))),
        out_shape=jax.ShapeDtypeStruct(b.shape, BF16),
        compiler_params=_cparams(2),
    )(ck, g, b)


def _add_rows_call(name, g, b, ck):
    _, r, cc = g.shape
    rb = _row_block(r, cc, 2 * 1024 * 1024)

    def body(ck_ref, g_ref, b_ref, o_ref):
        o_ref[...] = (g_ref[...] + b_ref[...]).astype(BF16)

    return pl.pallas_call(
        body, name=name,
        grid_spec=pltpu.PrefetchScalarGridSpec(
            num_scalar_prefetch=1, grid=(r // rb,),
            in_specs=[pl.BlockSpec((None, rb, cc), lambda i, ck_ref: (ck_ref[0], i, 0)),
                      pl.BlockSpec((rb, cc), lambda i, ck_ref: (i, 0))],
            out_specs=pl.BlockSpec((rb, cc), lambda i, ck_ref: (i, 0))),
        out_shape=jax.ShapeDtypeStruct((r, cc), BF16),
        compiler_params=_cparams(1),
    )(ck, g, b)


def _add_window_call(name, g, b, p, ck, comm):
    _, r, _ = g.shape
    nb, step = WIN_W // 128, WIN_STEP // 128
    n_xc = len(comm.srcs)

    def body(ck_ref, g_ref, b_ref, p0_ref, p1_ref, p2_ref, *rest):
        o_ref = rest[n_xc]
        i = pl.program_id(0)
        begin, finish = comm.make(rest[:n_xc], rest[n_xc + 1:2 * n_xc + 1], rest[-2], rest[-1])
        pl.when(i == 0)(begin)
        own = g_ref[...] + b_ref[...]
        o_ref[...] = ((own + p0_ref[...].astype(F32)) + p1_ref[...].astype(F32)) + p2_ref[...].astype(F32)
        pl.when(i == nb - 1)(finish)

    def peer(j):
        return pl.BlockSpec((None, r, 128), lambda i, ck_ref: (j, 0, i))

    return pl.pallas_call(
        body, name=name,
        grid_spec=pltpu.PrefetchScalarGridSpec(
            num_scalar_prefetch=1, grid=(nb,),
            in_specs=[pl.BlockSpec((None, r, 128), lambda i, ck_ref: (ck_ref[0], 0, step * ck_ref[1] + i)),
                      pl.BlockSpec((r, 128), lambda i, ck_ref: (0, step * ck_ref[1] + i)),
                      peer(0), peer(1), peer(2)] + [ANY] * n_xc,
            out_specs=[pl.BlockSpec((None, r, 128), lambda i, ck_ref: (ck_ref[0], 0, i))] + [ANY] * n_xc,
            scratch_shapes=_comm_sems(comm)),
        out_shape=[jax.ShapeDtypeStruct((2, r, WIN_W), F32)] + list(comm.out_shapes),
        compiler_params=_cparams(1),
    )(ck, g, b, p, p, p, *comm.srcs)


def _add_chips_call(name, g, b, p, ck):
    _, _, r, cc = g.shape
    rb = _row_block(r, cc, 2 * 1024 * 1024)

    def body(ck_ref, g_ref, b_ref, p0_ref, p1_ref, p2_ref, o_ref):
        own = g_ref[...] + b_ref[...]
        o_ref[...] = ((own + p0_ref[...].astype(F32)) + p1_ref[...].astype(F32)) + p2_ref[...].astype(F32)

    def peer(j):
        return pl.BlockSpec((None, rb, cc), lambda i, ck_ref: (j, i, 0))

    return pl.pallas_call(
        body, name=name,
        grid_spec=pltpu.PrefetchScalarGridSpec(
            num_scalar_prefetch=1, grid=(r // rb,),
            in_specs=[pl.BlockSpec((None, None, rb, cc), lambda i, ck_ref: (ck_ref[0], ck_ref[1], i, 0)),
                      pl.BlockSpec((None, rb, cc), lambda i, ck_ref: (ck_ref[1], i, 0)),
                      peer(0), peer(1), peer(2)],
            out_specs=pl.BlockSpec((None, rb, cc), lambda i, ck_ref: (ck_ref[0], i, 0))),
        out_shape=jax.ShapeDtypeStruct((2, r, cc), F32),
        compiler_params=_cparams(1),
    )(ck, g, b, p, p, p)


def _sum8_call(name, g):
    def body(g_ref, o_ref):
        acc = g_ref[0]
        for d in range(1, 8):
            acc = acc + g_ref[d]
        o_ref[...] = acc

    return pl.pallas_call(body, name=name, out_shape=jax.ShapeDtypeStruct(g.shape[1:], F32))(g)


def _adamw_call(name, w, g, m, v):
    r, cc = w.shape
    if r % 8 == 0 or r * cc * 4 <= 1024 * 1024:
        rb = _row_block(r, cc, 1024 * 1024) if r % 8 == 0 else r
        grid, spec = (r // rb,), pl.BlockSpec((rb, cc), lambda i: (i, 0))
    else:
        grid, spec = (cc // 128,), pl.BlockSpec((r, 128), lambda i: (0, i))

    def body(w_ref, g_ref, m_ref, v_ref, d_ref, m2_ref, v2_ref):
        gv = g_ref[...]
        m2 = ADAM_B1 * m_ref[...] + (1.0 - ADAM_B1) * gv
        v2 = ADAM_B2 * v_ref[...] + (1.0 - ADAM_B2) * (gv * gv)
        m_hat = m2 / (1.0 - ADAM_B1 ** ADAM_STEP)
        v_hat = v2 / (1.0 - ADAM_B2 ** ADAM_STEP)
        d_ref[...] = -ADAM_LR * (m_hat / (jnp.sqrt(v_hat) + ADAM_EPS) + ADAM_WD * w_ref[...])
        m2_ref[...] = m2
        v2_ref[...] = v2

    return pl.pallas_call(
        body, name=name, grid=grid, in_specs=[spec] * 4, out_specs=[spec] * 3,
        out_shape=[jax.ShapeDtypeStruct((r, cc), F32)] * 3, compiler_params=_cparams(1),
    )(w, g, m, v)


SMALL = (("norm_gain", D_MODEL), ("b_gate", GLA_KW), ("ret_norm_gain", RET_W), ("gla_norm_gain", GLA_W),
         ("final_norm_gain", D_MODEL), ("w_gate_up", GATE_RANK * GLA_KW), ("meta_tokens", N_META * D_MODEL),
         ("loss", 1))


def _pack_rows(vecs, rows):
    flat = jnp.concatenate([v.reshape(-1) for v in vecs])
    return jnp.pad(flat, (0, rows * 128 - flat.shape[0])).reshape(rows, 128)


def kernel(x, meta_tokens, norm_gain, w_in, w_gate_up, b_gate, ret_norm_gain, gla_norm_gain, w_branch_ret, w_branch_gla, w_out, final_norm_gain, loss_target, m_meta_tokens, m_norm_gain, m_w_in, m_w_gate_up, m_b_gate, m_ret_norm_gain, m_gla_norm_gain, m_w_branch_ret, m_w_branch_gla, m_w_out, m_final_norm_gain, v_meta_tokens, v_norm_gain, v_w_in, v_w_gate_up, v_b_gate, v_ret_norm_gain, v_gla_norm_gain, v_w_branch_ret, v_w_branch_gla, v_w_out, v_final_norm_gain):
    xi, yi, ci = _place()
    kme = 2 * xi + yi
    ck = jnp.stack([ci, kme]).astype(jnp.int32)
    sw_in = w_in.shape[2]

    def my_half(a, dtype):
        r, cc = a.shape
        return lax.dynamic_index_in_dim(a.reshape(2, r // 2, cc), ci, 0, keepdims=False).astype(dtype)

    g_meta, g_wg = _gather8_call("gather_small_weights", [my_half(meta_tokens, F32), my_half(w_gate_up[0], F32)])
    branch_parts = [my_half(w_branch_ret[0], BF16), my_half(w_branch_gla[0], BF16), my_half(w_out[0], BF16)]
    meta = g_meta.reshape(4, 2, N_META // 2, D_MODEL // 4).transpose(1, 2, 0, 3).reshape(N_META, D_MODEL)
    wg_full = g_wg.reshape(4, 2, GATE_RANK // 2, GLA_KW // 4).transpose(1, 2, 0, 3).reshape(GATE_RANK, GLA_KW)

    w_in_part = lax.dynamic_slice_in_dim(w_in[0].T, ci * (D_MODEL // 2), D_MODEL // 2, axis=1).astype(BF16)
    loc = _device_step(x[0], loss_target[0], meta, norm_gain, w_in_part, wg_full, b_gate, ret_norm_gain,
                       gla_norm_gain,
                       branch_parts, final_norm_gain, ck)
    names = ("w_in", "w_branch_ret", "w_branch_gla", "w_out")
    full = [loc[nm] for nm in names]
    big_w = dict(w_in=w_in[0], w_branch_ret=w_branch_ret[0], w_branch_gla=w_branch_gla[0], w_out=w_out[0])
    big_m = dict(w_in=m_w_in[0], w_branch_ret=m_w_branch_ret[0], w_branch_gla=m_w_branch_gla[0], w_out=m_w_out[0])
    big_v = dict(w_in=v_w_in[0], w_branch_ret=v_w_branch_ret[0], w_branch_gla=v_w_branch_gla[0], w_out=v_w_out[0])
    grads, deltas, new_m, new_v = {}, {}, {}, {}
    for nm, f in zip(names, full):
        shape = big_w[nm].shape
        if nm == "w_in":
            f = lax.dynamic_slice_in_dim(f, (sw_in - WIN_STEP) * kme, sw_in, axis=2)
        g = f.reshape(shape)
        if nm == "w_in":
            d, m2, v2 = (a.T for a in _adamw_call("adamw_" + nm, big_w[nm].T, g.T, big_m[nm].T, big_v[nm].T))
        else:
            d, m2, v2 = _adamw_call("adamw_" + nm, big_w[nm], g, big_m[nm], big_v[nm])
        grads[nm], deltas[nm], new_m[nm], new_v[nm] = (a.reshape((1,) + shape) for a in (g, d, m2, v2))

    tot = _sum8_call("sum_small_grads", loc["small"]).reshape(-1)
    off = 0
    sg = {}
    for nm, sz in SMALL:
        sg[nm] = tot[off:off + sz]
        off += sz
    loss = sg.pop("loss")[0]
    sg["w_gate_up"] = lax.dynamic_slice_in_dim(sg["w_gate_up"].reshape(GATE_RANK, GLA_KW), kme * (GLA_KW // 4),
                                               GLA_KW // 4, axis=1)
    sg["meta_tokens"] = lax.dynamic_slice_in_dim(sg["meta_tokens"].reshape(N_META, D_MODEL), kme * (D_MODEL // 4),
                                                 D_MODEL // 4, axis=1)
    small_w = dict(norm_gain=norm_gain, b_gate=b_gate, ret_norm_gain=ret_norm_gain, gla_norm_gain=gla_norm_gain,
                   final_norm_gain=final_norm_gain, w_gate_up=w_gate_up, meta_tokens=meta_tokens)
    small_m = dict(norm_gain=m_norm_gain, b_gate=m_b_gate, ret_norm_gain=m_ret_norm_gain,
                   gla_norm_gain=m_gla_norm_gain, final_norm_gain=m_final_norm_gain, w_gate_up=m_w_gate_up,
                   meta_tokens=m_meta_tokens)
    small_v = dict(norm_gain=v_norm_gain, b_gate=v_b_gate, ret_norm_gain=v_ret_norm_gain,
                   gla_norm_gain=v_gla_norm_gain, final_norm_gain=v_final_norm_gain, w_gate_up=v_w_gate_up,
                   meta_tokens=v_meta_tokens)
    for nm in small_w:
        shape = small_w[nm].shape
        as2d = lambda a: a.reshape((-1, shape[-1]))
        grads[nm] = sg[nm].reshape(shape)
        deltas[nm], new_m[nm], new_v[nm] = (a.reshape(shape) for a in _adamw_call(
            "adamw_" + nm, as2d(small_w[nm]), as2d(sg[nm]), as2d(small_m[nm]), as2d(small_v[nm])))

    out_order = ("meta_tokens", "norm_gain", "w_in", "w_gate_up", "b_gate", "ret_norm_gain", "gla_norm_gain",
                 "w_branch_ret", "w_branch_gla", "w_out", "final_norm_gain")
    dx = loc["dx"].reshape(x.shape)
    return (loss, dx, *[grads[nm] for nm in out_order], *[deltas[nm] for nm in out_order],
            *[new_m[nm] for nm in out_order], *[new_v[nm] for nm in out_order])
```

```python
import math
from typing import Callable, NamedTuple

import numpy as np
import jax
import jax.numpy as jnp
from jax import lax
from jax.experimental import pallas as pl
from jax.experimental.pallas import tpu as pltpu

F32 = jnp.float32
BF16 = jnp.bfloat16

D_MODEL = 1024
N_META = 16
EPS = 1e-6
ROPE_BASE = 10000.0
RET_HEADS, RET_QK, RET_V = 4, 256, 512
RET_W = RET_HEADS * RET_V
GLA_HEADS, GLA_K, GLA_V = 4, 128, 256
GLA_W = GLA_HEADS * GLA_V
GLA_KW = GLA_HEADS * GLA_K
GATE_RANK = 16
GATE_TAU = 16.0
GLA_SUB = 16

TM = 256
T0 = TM
PADF = T0 - N_META
GC = 128
GS = 3
TB = 768
TK = 768

W_R = 6144
W_G = 3088
W_GP = 3200
W_M = 2048
IN_COLS = W_R + W_G + W_M
WIN_STEP = (IN_COLS // 4) // 128 * 128
WIN_W = -(-(3 * (IN_COLS // 4 - WIN_STEP) + IN_COLS // 4) // 128) * 128
IN_PAD = 3 * WIN_STEP + WIN_W

ADAM_LR, ADAM_B1, ADAM_B2, ADAM_EPS, ADAM_WD, ADAM_STEP = 0.001, 0.9, 0.999, 1e-08, 0.01, 10

VMEM_LIMIT = 56 * 1024 * 1024

NN = ((1,), (0,))
NT = ((1,), (1,))
TN = ((0,), (0,))


def _dot(a, b, dims):
    return lax.dot_general(a, b, (dims, ((), ())), preferred_element_type=F32)


def _cparams(n_axes):
    return pltpu.CompilerParams(dimension_semantics=("arbitrary",) * n_axes, vmem_limit_bytes=VMEM_LIMIT)


def _sigmoid(x):
    return 0.5 * jnp.tanh(0.5 * x) + 0.5


def _silu(x):
    h = 0.5 * x
    return h + h * jnp.tanh(h)


def _head_mean(x):
    return jnp.mean(x, axis=-1, keepdims=True)


def _split3(x):
    hi = x.astype(BF16)
    r1 = x - hi.astype(F32)
    mid = r1.astype(BF16)
    lo = (r1 - mid.astype(F32)).astype(BF16)
    return hi, mid, lo


def _exact_pm(p, x):
    hi, mid, lo = _split3(x)
    return _dot(p, hi, NN) + _dot(p, mid, NN) + _dot(p, lo, NN)


def _rms_call(x2d, head, gain, comm):
    tp = T0 + x2d.shape[0]
    nt = tp // TM
    n_xc = len(comm.srcs)

    def body(x_ref, hd_ref, g_ref, *rest):
        xc_src = rest[:n_xc]
        h_ref, u_ref = rest[n_xc:n_xc + 2]
        xc_dst = rest[n_xc + 2:2 * n_xc + 2]
        i = pl.program_id(0)
        begin, finish = comm.make(xc_src, xc_dst, rest[-2], rest[-1])
        pl.when(i == 0)(begin)
        h = jnp.where(i == 0, hd_ref[...], x_ref[...])
        h_ref[...] = h
        r = lax.rsqrt(jnp.mean(h * h, axis=-1, keepdims=True) + EPS)
        u_ref[...] = (h * r * g_ref[...]).astype(BF16)
        pl.when(i == nt - 1)(finish)

    tile = pl.BlockSpec((TM, D_MODEL), lambda i: (i, 0))
    return pl.pallas_call(
        body, name="rms_in", grid=(nt,),
        in_specs=[pl.BlockSpec((TM, D_MODEL), lambda i: (jnp.maximum(i - 1, 0), 0)),
                  pl.BlockSpec((T0, D_MODEL), lambda i: (0, 0)), pl.BlockSpec((1, D_MODEL), lambda i: (0, 0))]
        + [ANY] * n_xc,
        out_specs=[tile, tile] + [ANY] * n_xc,
        out_shape=[jax.ShapeDtypeStruct((tp, D_MODEL), F32), jax.ShapeDtypeStruct((tp, D_MODEL), BF16)]
        + list(comm.out_shapes),
        scratch_shapes=_comm_sems(comm), compiler_params=_cparams(1),
    )(x2d, head, gain, *comm.srcs)


PROJ_ROWS_MAX = 1408


def _proj_rows(m):
    return max(r for r in range(16, PROJ_ROWS_MAX + 1, 16) if m % r == 0)


def _mm_nn(name, a, bt, out_dtype, tn, col0, ncols, epilogue=None, extras=(), extra_specs=()):
    m, k = a.shape
    nj, j0 = ncols // tn, col0 // tn
    tb = _proj_rows(m)

    def body(a_ref, b_ref, *rest):
        *ex, o_ref = rest
        acc = _dot(a_ref[...], b_ref[...], NT)
        if epilogue is None:
            o_ref[...] = acc.astype(out_dtype)
        else:
            epilogue(acc, o_ref, *ex)

    return pl.pallas_call(
        body, name=name, grid=(nj, m // tb),
        in_specs=[pl.BlockSpec((tb, k), lambda j, i: (i, 0)), pl.BlockSpec((tn, k), lambda j, i: (j0 + j, 0))]
        + list(extra_specs),
        out_specs=pl.BlockSpec((tb, tn), lambda j, i: (i, j)),
        out_shape=jax.ShapeDtypeStruct((m, ncols), out_dtype),
        compiler_params=_cparams(2),
    )(a, bt, *extras)


def _rope_tables(tp):
    half = RET_QK // 2
    pos = np.arange(tp, dtype=np.float32) - np.float32(PADF)
    inv = (ROPE_BASE ** (-np.arange(half, dtype=np.float64) / half)).astype(np.float32)
    ang = (pos[:, None] * inv[None, :]).astype(np.float64)
    return np.cos(ang).astype(np.float32), np.sin(ang).astype(np.float32)


def _rope_epilogue(acc, o_ref, cos_ref, sin_ref):
    scale = jnp.where(pl.program_id(0) == 1, RET_QK ** -0.5, 1.0).astype(F32)
    cos, sin = cos_ref[...], sin_ref[...]
    half = RET_QK // 2
    for h in range(RET_HEADS):
        t1 = acc[:, h * RET_QK:h * RET_QK + half]
        t2 = acc[:, h * RET_QK + half:(h + 1) * RET_QK]
        o_ref[:, h * RET_QK:h * RET_QK + half] = ((t1 * cos - t2 * sin) * scale).astype(BF16)
        o_ref[:, h * RET_QK + half:(h + 1) * RET_QK] = ((t2 * cos + t1 * sin) * scale).astype(BF16)


def _gqk_epilogue(acc, o_ref):
    o_ref[:, :GLA_KW] = acc[:, :GLA_KW] * (GLA_K ** -0.5)
    o_ref[:, GLA_KW:] = acc[:, GLA_KW:]


class _Comm(NamedTuple):
    srcs: tuple
    out_shapes: tuple
    n_sems: int
    make: Callable


def _comm_sems(comm):
    return [pltpu.SemaphoreType.DMA((comm.n_sems,)), pltpu.SemaphoreType.DMA((comm.n_sems,))]


def _start_wait(copies):
    def begin():
        for cp in copies:
            cp.start()

    def finish():
        for cp in copies:
            cp.wait()

    return begin, finish


def _other_chips(x, y):
    return [(1 - x, y), (x, 1 - y), (1 - x, 1 - y)]


def _gather_plan(parts, relay=()):
    n = len(parts)
    relay = tuple(relay) + (False,) * (n - len(relay))

    def make(x_refs, out_refs, send_sems, recv_sems):
        x, y, c = _place()
        me, sibling = (x, y, c), (x, y, 1 - c)
        xn, yn, dg = (1 - x, y), (x, 1 - y), (1 - x, 1 - y)

        def slot(t, px, py, pc, half=None):
            ref = out_refs[t].at[4 * px + 2 * py + pc]
            if half is None:
                return ref
            cols = ref.shape[-1] // 2
            return ref.at[:, pl.ds(half * cols, cols)]

        def copy(t, k, dst, to, src=None):
            return pltpu.make_async_remote_copy(
                src_ref=dst if src is None else src, dst_ref=dst, send_sem=send_sems.at[8 * t + k],
                recv_sem=recv_sems.at[8 * t + k], device_id=to, device_id_type=MESH)

        mine = [pltpu.make_async_copy(x_refs[t], slot(t, *me), send_sems.at[8 * n + t]) for t in range(n)]
        sent = []
        for t in range(n):
            sent.append(copy(t, 0, slot(t, *me), sibling, src=x_refs[t]))
            sent.append(copy(t, 1, slot(t, *me), (*xn, c), src=x_refs[t]))
            sent.append(copy(t, 2, slot(t, *me), (*yn, c), src=x_refs[t]))
            if not relay[t]:
                sent.append(copy(t, 3, slot(t, *me), (*dg, c), src=x_refs[t]))

        def begin():
            for cp in mine + sent:
                cp.start()

        def finish():
            later = []

            def start(cp):
                cp.start()
                later.append(cp)

            for t in range(n):
                copy(t, 2, slot(t, *yn, c), me).wait_recv()
                if relay[t]:
                    start(copy(t, 3, slot(t, *yn, c, half=0), (*xn, c)))
                start(copy(t, 6, slot(t, *yn, c), sibling))
            for t in range(n):
                copy(t, 1, slot(t, *xn, c), me).wait_recv()
                if relay[t]:
                    start(copy(t, 4, slot(t, *xn, c, half=1), (*yn, c)))
                start(copy(t, 5, slot(t, *xn, c), sibling))
            for t in range(n):
                if relay[t]:
                    copy(t, 3, slot(t, *dg, c, half=0), me).wait_recv()
                    copy(t, 4, slot(t, *dg, c, half=1), me).wait_recv()
                else:
                    copy(t, 3, slot(t, *dg, c), me).wait_recv()
                start(copy(t, 7, slot(t, *dg, c), sibling))
            for t in range(n):
                copy(t, 0, slot(t, *sibling), me).wait_recv()
                copy(t, 5, slot(t, *xn, 1 - c), me).wait_recv()
                copy(t, 6, slot(t, *yn, 1 - c), me).wait_recv()
                copy(t, 7, slot(t, *dg, 1 - c), me).wait_recv()
            for cp in sent + later:
                cp.wait_send()
            for cp in mine:
                cp.wait()

        return begin, finish

    return _Comm(tuple(parts), tuple(jax.ShapeDtypeStruct((8,) + p.shape, p.dtype) for p in parts), 9 * n, make)


def _exchange_plan(ss):
    def make(s_refs, b_refs, send_sems, recv_sems):
        x, y, c = _place()
        return _start_wait([pltpu.make_async_remote_copy(
            src_ref=s_refs[t].at[2 * chip[0] + chip[1]], dst_ref=b_refs[t].at[j], send_sem=send_sems.at[3 * t + j],
            recv_sem=recv_sems.at[3 * t + j], device_id=(*chip, c), device_id_type=MESH)
            for t in range(len(s_refs)) for j, chip in enumerate(_other_chips(x, y))])

    return _Comm(tuple(ss), tuple(jax.ShapeDtypeStruct((3,) + s.shape[1:], s.dtype) for s in ss), 3 * len(ss), make)


def _exchange_window_plan(s):
    def make(s_refs, b_refs, send_sems, recv_sems):
        x, y, c = _place()
        return _start_wait([pltpu.make_async_remote_copy(
            src_ref=s_refs[0].at[:, pl.ds(pl.multiple_of((2 * chip[0] + chip[1]) * WIN_STEP, 128), WIN_W)],
            dst_ref=b_refs[0].at[j], send_sem=send_sems.at[j], recv_sem=recv_sems.at[j], device_id=(*chip, c),
            device_id_type=MESH) for j, chip in enumerate(_other_chips(x, y))])

    return _Comm((s,), (jax.ShapeDtypeStruct((3, s.shape[0], WIN_W), s.dtype),), 3, make)


def _swap_plan(gs):
    def make(g_refs, b_refs, send_sems, recv_sems):
        x, y, c = _place()
        return _start_wait([pltpu.make_async_remote_copy(
            src_ref=g_refs[t].at[1 - c], dst_ref=b_refs[t], send_sem=send_sems.at[t], recv_sem=recv_sems.at[t],
            device_id=(x, y, 1 - c), device_id_type=MESH) for t in range(len(g_refs))])

    return _Comm(tuple(gs), tuple(jax.ShapeDtypeStruct(g.shape[1:], g.dtype) for g in gs), len(gs), make)


def _spread_plan(parts):
    def make(p_refs, o_refs, send_sems, recv_sems):
        x, y, c = _place()
        copies = []
        for t in range(len(p_refs)):
            mine = o_refs[t].at[4 * x + 2 * y + c]
            copies.append(pltpu.make_async_copy(p_refs[t], mine, send_sems.at[7 * len(p_refs) + t]))
            for r in range(1, 8):
                peer = (1 - x if r & 4 else x, 1 - y if r & 2 else y, 1 - c if r & 1 else c)
                copies.append(pltpu.make_async_remote_copy(
                    src_ref=p_refs[t], dst_ref=mine, send_sem=send_sems.at[7 * t + r - 1],
                    recv_sem=recv_sems.at[7 * t + r - 1], device_id=peer, device_id_type=MESH))
        return _start_wait(copies)

    return _Comm(tuple(parts), tuple(jax.ShapeDtypeStruct((8,) + p.shape, p.dtype) for p in parts), 8 * len(parts),
                 make)


def _mm_nt_acc(name, a, w, tk, acc_in=None, epilogue=None, extras=(), extra_specs=(), extra_out_shapes=(),
               extra_out_specs=(), extra_scratch=(), comm=None, tb=TB):
    m, k = a.shape
    n = w.shape[1]
    nk, ni = k // tk, m // tb
    has_acc = acc_in is not None
    n_xc = len(comm.srcs) if comm else 0
    n_es = len(extra_scratch)

    def body(*refs):
        a_ref, w_ref = refs[0], refs[1]
        pos = 2
        acc_ref = None
        if has_acc:
            acc_ref = refs[pos]
            pos += 1
        ex = refs[pos:pos + len(extras)]
        pos += len(extras)
        xc_src = refs[pos:pos + n_xc]
        pos += n_xc
        n_scr = 1 + n_es + (2 if n_xc else 0)
        outs = refs[pos:len(refs) - n_scr - n_xc]
        xc_dst = refs[len(refs) - n_scr - n_xc:len(refs) - n_scr]
        scr = refs[len(refs) - n_scr]
        es = refs[len(refs) - n_scr + 1:len(refs) - n_scr + 1 + n_es]
        i, kk = pl.program_id(0), pl.program_id(1)
        if n_xc:
            begin, finish = comm.make(xc_src, xc_dst, refs[-2], refs[-1])
            pl.when((i == 0) & (kk == 0))(begin)

        @pl.when(kk == 0)
        def _():
            scr[...] = acc_ref[...] if has_acc else jnp.zeros_like(scr)

        scr[...] += _dot(a_ref[...], w_ref[...], NN)

        @pl.when(kk == nk - 1)
        def _():
            if epilogue is None:
                outs[0][...] = scr[...]
            else:
                epilogue(scr[...], outs, i, ni, *ex, *es)

        if n_xc:
            pl.when((i == ni - 1) & (kk == nk - 1))(finish)

    in_specs = [pl.BlockSpec((tb, tk), lambda i, kk: (i, kk)), pl.BlockSpec((tk, n), lambda i, kk: (kk, 0))]
    args = [a, w]
    if has_acc:
        in_specs.append(pl.BlockSpec((tb, n), lambda i, kk: (i, 0)))
        args.append(acc_in)
    in_specs += list(extra_specs) + [ANY] * n_xc
    args += list(extras) + (list(comm.srcs) if comm else [])
    if epilogue is None:
        out_shape = [jax.ShapeDtypeStruct((m, n), F32)]
        out_specs = [pl.BlockSpec((tb, n), lambda i, kk: (i, 0))]
    else:
        out_shape, out_specs = list(extra_out_shapes), list(extra_out_specs)
    scratch = [pltpu.VMEM((tb, n), F32)] + list(extra_scratch)
    if n_xc:
        out_shape += list(comm.out_shapes)
        out_specs += [ANY] * n_xc
        scratch += _comm_sems(comm)
    return pl.pallas_call(
        body, name=name, grid=(ni, nk), in_specs=in_specs, out_specs=out_specs, out_shape=out_shape,
        scratch_shapes=scratch, compiler_params=_cparams(2),
    )(*args)


def _rms_bwd_epilogue(du, outs, i, ni, h_ref, g_ref, dh1_ref, obuf, sems):
    dx_ref, dmeta_ref, dg_ref = outs
    h = h_ref[...]
    r = lax.rsqrt(jnp.mean(h * h, axis=-1, keepdims=True) + EPS)
    xh = h * r
    dxh = du * g_ref[...]
    dh0 = dh1_ref[...] + r * (dxh - xh * jnp.mean(dxh * xh, axis=-1, keepdims=True))

    def put(slot, tile):
        return pltpu.make_async_copy(obuf.at[slot], dx_ref.at[pl.ds(pl.multiple_of(tile * TB - T0, 8), TB)],
                                     sems.at[slot])

    @pl.when(i == 0)
    def _():
        dg_ref[...] = jnp.zeros_like(dg_ref)
        dmeta_ref[...] = dh0[PADF:T0, :]
        obuf[0] = dh0
        first = pltpu.make_async_copy(obuf.at[0, pl.ds(T0, TB - T0)], dx_ref.at[pl.ds(0, TB - T0)], sems.at[0])
        first.start()
        first.wait()

    @pl.when(i >= 1)
    def _():
        slot = i % 2

        @pl.when(i >= 3)
        def _():
            put(slot, i - 2).wait()

        obuf[slot] = dh0
        put(slot, i).start()

    dg_ref[...] += jnp.sum(du * xh, axis=0, keepdims=True)

    @pl.when(i == ni - 1)
    def _():
        for tile in (ni - 2, ni - 1):
            if tile >= 1:
                put(tile % 2, tile).wait()


def _mm_tn(name, a, b, bn, ncols=None, bcol0=0, into=None, col0=0, out_cols=None):
    t, m = a.shape
    n = ncols or b.shape[1]
    j0, bj0 = col0 // bn, bcol0 // bn

    def body(a_ref, b_ref, *rest):
        o_ref = rest[-1]

        @pl.when(pl.program_id(1) == 0)
        def _():
            o_ref[...] = jnp.zeros_like(o_ref)

        o_ref[...] += _dot(a_ref[...], b_ref[...], TN)

    in_specs = [pl.BlockSpec((TK, m), lambda j, kk: (kk, 0)), pl.BlockSpec((TK, bn), lambda j, kk: (kk, bj0 + j))]
    args = [a, b]
    aliases = {}
    if into is not None:
        in_specs.append(ANY)
        args.append(into)
        aliases = {2: 0}
        out_cols = into.shape[1]
    return pl.pallas_call(
        body, name=name, grid=(n // bn, t // TK), in_specs=in_specs,
        out_specs=pl.BlockSpec((m, bn), lambda j, kk: (0, j0 + j)),
        out_shape=jax.ShapeDtypeStruct((m, out_cols or n), F32), input_output_aliases=aliases,
        compiler_params=_cparams(2),
    )(*args)


def _place_merge_cols_call(dwp, dw_m, dw_glr):
    c0 = W_R + W_GP - 128
    tail = IN_PAD - c0
    rows = 256

    def body(m_ref, low, p_ref, o_ref, buf, sem):
        for r in range(0, D_MODEL, rows):
            buf[r:r + rows, :] = jnp.concatenate(
                [low[r:r + rows, :GATE_RANK], m_ref[r:r + rows, :],
                 jnp.zeros((rows, tail - GATE_RANK - W_M), F32)], axis=1)
        put = pltpu.make_async_copy(buf, o_ref.at[:, pl.ds(c0, tail)], sem)
        put.start()
        put.wait()

    return pl.pallas_call(
        body, name="place_merge_cols",
        in_specs=[pl.BlockSpec(memory_space=pltpu.VMEM), pl.BlockSpec(memory_space=pltpu.VMEM), ANY], out_specs=ANY,
        out_shape=jax.ShapeDtypeStruct(dwp.shape, F32), input_output_aliases={2: 0},
        scratch_shapes=[pltpu.VMEM((D_MODEL, tail), F32), pltpu.SemaphoreType.DMA],
        compiler_params=pltpu.CompilerParams(vmem_limit_bytes=VMEM_LIMIT),
    )(dw_m, dw_glr, dwp)


def _ret_fill_decay(lg_ref, dm_scr):
    c = TM
    ii = lax.broadcasted_iota(jnp.int32, (c, c), 0)
    jj = lax.broadcasted_iota(jnp.int32, (c, c), 1)
    rel = (ii - jj).astype(F32)
    for h in range(RET_HEADS):
        dm_scr[h] = jnp.where(rel >= 0, jnp.exp(jnp.maximum(rel, 0.0) * lg_ref[h]), 0.0)


def _ret_consts(lg, dm_ref):
    c = TM
    idx = lax.broadcasted_iota(jnp.int32, (c, 1), 0).astype(F32)
    xi = jnp.exp((idx + 1.0) * lg)
    zeta = jnp.exp((c - 1.0 - idx) * lg)
    gc = jnp.exp(jnp.full((1, 1), c, F32) * lg)
    return dm_ref[...], xi, zeta, gc


def _ret_fwd_call(rqk, rv, rg, gain, lgam):
    tp = rqk.shape[0]
    nc = tp // TM

    def body(lg_ref, qk_ref, v_ref, rg_ref, g_ref, o_ref, a_ref, st_ref, sc_ref, s_scr, dm_scr):
        @pl.when(pl.program_id(0) == 0)
        def _():
            s_scr[...] = jnp.zeros_like(s_scr)
            _ret_fill_decay(lg_ref, dm_scr)

        for h in range(RET_HEADS):
            dm, xi, zeta, gc = _ret_consts(lg_ref[h], dm_scr.at[h])
            q = qk_ref[:, h * RET_QK:(h + 1) * RET_QK]
            k = qk_ref[:, D_MODEL + h * RET_QK:D_MODEL + (h + 1) * RET_QK]
            v = v_ref[:, h * RET_V:(h + 1) * RET_V]
            sb = s_scr[h].astype(BF16)
            st_ref[0, h] = sb
            s = (_dot(q, k, NT) * dm).astype(BF16)
            sc_ref[0, h] = s
            o = _dot(s, v, NN) + xi * _dot(q, sb, NN)
            kz = (k.astype(F32) * zeta).astype(BF16)
            s_scr[h] = gc * s_scr[h] + _dot(kz, v, TN)
            o_ref[:, h * RET_V:(h + 1) * RET_V] = o
            mu = _head_mean(o)
            xc = o - mu
            xh = xc * lax.rsqrt(_head_mean(xc * xc) + EPS)
            a_ref[:, h * RET_V:(h + 1) * RET_V] = (
                xh * g_ref[:, h * RET_V:(h + 1) * RET_V] * _silu(rg_ref[:, h * RET_V:(h + 1) * RET_V])).astype(BF16)

    return pl.pallas_call(
        body, name="ret_fwd", grid=(nc,),
        in_specs=[pl.BlockSpec(memory_space=pltpu.SMEM),
                  pl.BlockSpec((TM, 2 * D_MODEL), lambda n: (n, 0)),
                  pl.BlockSpec((TM, RET_W), lambda n: (n, 0)),
                  pl.BlockSpec((TM, RET_W), lambda n: (n, 0)),
                  pl.BlockSpec((1, RET_W), lambda n: (0, 0))],
        out_specs=[pl.BlockSpec((TM, RET_W), lambda n: (n, 0)),
                   pl.BlockSpec((TM, RET_W), lambda n: (n, 0)),
                   pl.BlockSpec((1, RET_HEADS, RET_QK, RET_V), lambda n: (n, 0, 0, 0)),
                   pl.BlockSpec((1, RET_HEADS, TM, TM), lambda n: (n, 0, 0, 0))],
        out_shape=[jax.ShapeDtypeStruct((tp, RET_W), F32), jax.ShapeDtypeStruct((tp, RET_W), BF16),
                   jax.ShapeDtypeStruct((nc, RET_HEADS, RET_QK, RET_V), BF16),
                   jax.ShapeDtypeStruct((nc, RET_HEADS, TM, TM), BF16)],
        scratch_shapes=[pltpu.VMEM((RET_HEADS, RET_QK, RET_V), F32), pltpu.VMEM((RET_HEADS, TM, TM), F32)],
        compiler_params=_cparams(1),
    )(lgam, rqk, rv, rg, gain)


def _ret_bwd_call(rqk, rv, rg, o_ret, dpr, wbr, states, scores, gain, lgam, cos, sin):
    tp = rqk.shape[0]
    nc = tp // TM
    half = RET_QK // 2

    def body(lg_ref, qk_ref, v_ref, rg_ref, o_ref, dpr_ref, wbr_ref, st_ref, sc_ref, g_ref, cos_ref, sin_ref, dp_ref,
             dg_ref, ds_scr, dm_scr):
        @pl.when(pl.program_id(0) == 0)
        def _():
            ds_scr[...] = jnp.zeros_like(ds_scr)
            dg_ref[...] = jnp.zeros_like(dg_ref)
            _ret_fill_decay(lg_ref, dm_scr)

        cos, sin = cos_ref[...], sin_ref[...]
        for h in range(RET_HEADS):
            hs = slice(h * RET_V, (h + 1) * RET_V)
            dm, xi, zeta, gc = _ret_consts(lg_ref[h], dm_scr.at[h])
            o = o_ref[:, hs]
            mu = _head_mean(o)
            xc = o - mu
            rstd = lax.rsqrt(_head_mean(xc * xc) + EPS)
            xh = xc * rstd
            gain_h = g_ref[:, hs]
            g = rg_ref[:, hs]
            sg = _sigmoid(g)
            silu = g * sg
            dah = _dot(dpr_ref[...], wbr_ref[hs, :], NT)
            dp_ref[:, 4 * D_MODEL + h * RET_V:4 * D_MODEL + (h + 1) * RET_V] = (
                dah * (xh * gain_h) * (sg * (1.0 + g * (1.0 - sg)))).astype(BF16)
            dn = dah * silu
            dg_ref[:, hs] += jnp.sum(dn * xh, axis=0, keepdims=True)
            dxh = dn * gain_h
            do = rstd * (dxh - _head_mean(dxh) - xh * _head_mean(dxh * xh))
            dob = do.astype(BF16)
            q = qk_ref[:, h * RET_QK:(h + 1) * RET_QK]
            k = qk_ref[:, D_MODEL + h * RET_QK:D_MODEL + (h + 1) * RET_QK]
            v = v_ref[:, hs]
            sp = st_ref[0, h]
            ds = ds_scr[h]
            dsb = ds.astype(BF16)
            s = sc_ref[0, h]
            dsc = (_dot(dob, v, NT) * dm).astype(BF16)
            dq = _dot(dsc, k, NN) + xi * _dot(dob, sp, NT)
            dk = _dot(dsc, q, TN) + zeta * _dot(v, dsb, NT)
            kz = (k.astype(F32) * zeta).astype(BF16)
            dv = _dot(s, dob, TN) + _dot(kz, dsb, NN)
            qx = (q.astype(F32) * xi).astype(BF16)
            ds_scr[h] = gc * ds + _dot(qx, dob, TN)
            dp_ref[:, 2 * D_MODEL + h * RET_V:2 * D_MODEL + (h + 1) * RET_V] = dv.astype(BF16)
            dk = dk * (RET_QK ** -0.5)
            for base, t in ((0, dq), (D_MODEL, dk)):
                t1, t2 = t[:, :half], t[:, half:]
                dp_ref[:, base + h * RET_QK:base + h * RET_QK + half] = (t1 * cos + t2 * sin).astype(BF16)
                dp_ref[:, base + h * RET_QK + half:base + (h + 1) * RET_QK] = (t2 * cos - t1 * sin).astype(BF16)

    rev = lambda n: (nc - 1 - n, 0)
    return pl.pallas_call(
        body, name="ret_bwd", grid=(nc,),
        in_specs=[pl.BlockSpec(memory_space=pltpu.SMEM),
                  pl.BlockSpec((TM, 2 * D_MODEL), rev),
                  pl.BlockSpec((TM, RET_W), rev),
                  pl.BlockSpec((TM, RET_W), rev),
                  pl.BlockSpec((TM, RET_W), rev),
                  pl.BlockSpec((TM, D_MODEL), rev),
                  pl.BlockSpec((RET_W, D_MODEL), lambda n: (0, 0)),
                  pl.BlockSpec((1, RET_HEADS, RET_QK, RET_V), lambda n: (nc - 1 - n, 0, 0, 0)),
                  pl.BlockSpec((1, RET_HEADS, TM, TM), lambda n: (nc - 1 - n, 0, 0, 0)),
                  pl.BlockSpec((1, RET_W), lambda n: (0, 0)),
                  pl.BlockSpec((TM, half), rev),
                  pl.BlockSpec((TM, half), rev)],
        out_specs=[pl.BlockSpec((TM, W_R), rev), pl.BlockSpec((1, RET_W), lambda n: (0, 0))],
        out_shape=[jax.ShapeDtypeStruct((tp, W_R), BF16), jax.ShapeDtypeStruct((1, RET_W), F32)],
        scratch_shapes=[pltpu.VMEM((RET_HEADS, RET_QK, RET_V), F32), pltpu.VMEM((RET_HEADS, TM, TM), F32)],
        compiler_params=_cparams(1),
    )(lgam, rqk, rv, rg, o_ret, dpr, wbr, states, scores, gain, cos, sin)


GLA_LEVELS = tuple(GC >> (s + 1) for s in range(int(math.log2(GC // GLA_SUB))))
NLEV = len(GLA_LEVELS)


def _gla_tril():
    return np.tril(np.ones((GC, GC), np.float32))


def _gla_masks():
    ii = lax.broadcasted_iota(jnp.int32, (GC, GC), 0)
    jj = lax.broadcasted_iota(jnp.int32, (GC, GC), 1)
    masks = []
    for m in GLA_LEVELS:
        sh = int(math.log2(2 * m))
        masks.append(((ii >> sh) == (jj >> sh)) & ((ii & m) != 0) & ((jj & m) == 0))
    sh = int(math.log2(GLA_SUB))
    md = ((ii >> sh) == (jj >> sh)) & (jj <= ii)
    row = lax.broadcasted_iota(jnp.int32, (GC, 1), 0)
    second = [(row & m) != 0 for m in GLA_LEVELS]
    return masks, md, second


def _gla_gate_call(u, w_g, wg, bg, pmat):
    tp = u.shape[0]
    gb = _proj_rows(tp)
    assert gb % GC == 0

    def body(u_ref, w_ref, wg_ref, bg_ref, p_ref, glr_ref, z_ref, b_ref):
        glr = _dot(u_ref[...], w_ref[...], NT)
        glr_ref[...] = glr
        z = _dot(glr.astype(BF16), wg_ref[...], NN) + bg_ref[...]
        z_ref[...] = z
        la = (jnp.minimum(z, 0.0) - jnp.log1p(jnp.exp(-jnp.abs(z)))) * (1.0 / GATE_TAU)
        for r in range(0, gb, GC):
            b_ref[r:r + GC, :] = _exact_pm(p_ref[...], la[r:r + GC, :])

    tile = pl.BlockSpec((gb, GLA_KW), lambda i: (i, 0))
    return pl.pallas_call(
        body, name="gla_gate", grid=(tp // gb,),
        in_specs=[pl.BlockSpec((gb, D_MODEL), lambda i: (i, 0)),
                  pl.BlockSpec((128, D_MODEL), lambda i: ((W_GP - 128) // 128, 0)),
                  pl.BlockSpec((128, GLA_KW), lambda i: (0, 0)),
                  pl.BlockSpec((1, GLA_KW), lambda i: (0, 0)), pl.BlockSpec((GC, GC), lambda i: (0, 0))],
        out_specs=[pl.BlockSpec((gb, 128), lambda i: (i, 0)), tile, tile],
        out_shape=[jax.ShapeDtypeStruct((tp, 128), F32), jax.ShapeDtypeStruct((tp, GLA_KW), F32),
                   jax.ShapeDtypeStruct((tp, GLA_KW), F32)],
        compiler_params=_cparams(1),
    )(u, w_g, wg, bg, pmat)


def _gla_gate_bwd_call(db, z, glr, u, wg, pmat_t, d_g):
    tp = db.shape[0]
    gb = _proj_rows(tp)
    assert gb % GC == 0 and (W_GP - 128) % 128 == 0

    def body(db_ref, z_ref, glr_ref, u_ref, wg_ref, pt_ref, dgin_ref, dg_ref, dwg_ref, dbg_ref, dwl_ref):
        i = pl.program_id(0)

        @pl.when(i == 0)
        def _():
            dwg_ref[...] = jnp.zeros_like(dwg_ref)
            dbg_ref[...] = jnp.zeros_like(dbg_ref)
            dwl_ref[...] = jnp.zeros_like(dwl_ref)

        dla = jnp.concatenate([_exact_pm(pt_ref[...], db_ref[r:r + GC, :]) for r in range(0, gb, GC)], axis=0)
        row = i * gb + lax.broadcasted_iota(jnp.int32, (gb, 1), 0)
        dz = jnp.where(row >= PADF, dla * (1.0 / GATE_TAU) * _sigmoid(-z_ref[...]), 0.0)
        dzb = dz.astype(BF16)
        dglr = _dot(dzb, wg_ref[...], NT).astype(BF16)
        dg_ref[...] = dglr
        dwg_ref[...] += _dot(glr_ref[...].astype(BF16), dzb, TN)
        dbg_ref[...] += jnp.sum(dz, axis=0, keepdims=True)
        dwl_ref[...] += _dot(u_ref[...], dglr, TN)

    tile = pl.BlockSpec((gb, GLA_KW), lambda i: (i, 0))
    const = lambda i: (0, 0)
    return pl.pallas_call(
        body, name="gla_gate_bwd", grid=(tp // gb,),
        in_specs=[tile, tile, pl.BlockSpec((gb, 128), lambda i: (i, 0)), pl.BlockSpec((gb, D_MODEL), lambda i: (i, 0)),
                  pl.BlockSpec((128, GLA_KW), const), pl.BlockSpec((GC, GC), const), ANY],
        out_specs=[pl.BlockSpec((gb, 128), lambda i: (i, (W_GP - 128) // 128)), pl.BlockSpec((128, GLA_KW), const),
                   pl.BlockSpec((1, GLA_KW), const), pl.BlockSpec((D_MODEL, 128), const)],
        out_shape=[jax.ShapeDtypeStruct(d_g.shape, BF16), jax.ShapeDtypeStruct((128, GLA_KW), F32),
                   jax.ShapeDtypeStruct((1, GLA_KW), F32), jax.ShapeDtypeStruct((D_MODEL, 128), F32)],
        input_output_aliases={6: 0}, compiler_params=_cparams(1),
    )(db, z, glr, u, wg, pmat_t, d_g)


def _gla_row_steps(b_ref, cs, rows, size):
    parts = [jnp.zeros((size, GLA_K), F32) if r is None else jnp.broadcast_to(b_ref[r:r + 1, cs], (size, GLA_K))
             for r in rows]
    return parts[0] if len(parts) == 1 else jnp.concatenate(parts, axis=0)


def _gla_factors(b_ref, h, second):
    cs = slice(h * GLA_K, (h + 1) * GLA_K)
    b = b_ref[:, cs]
    fq, fk = [], []
    for l, m in enumerate(GLA_LEVELS):
        d = b - _gla_row_steps(b_ref, cs, [s + m - 1 for s in range(0, GC, 2 * m)], 2 * m)
        f = jnp.exp(jnp.where(second[l], d, -d))
        fq.append(jnp.where(second[l], f, 0.0))
        fk.append(jnp.where(second[l], 0.0, f))
    dd = b - _gla_row_steps(b_ref, cs, [None] + [s - 1 for s in range(GLA_SUB, GC, GLA_SUB)], GLA_SUB)
    ed = jnp.exp(dd)
    edi = jnp.exp(-dd)
    eb = jnp.exp(b)
    bl = b_ref[GC - 1:GC, cs]
    ee = jnp.exp(bl - b)
    ebl = jnp.exp(bl)
    return fq, fk, ed, edi, eb, ee, ebl


def _gla_scaled(q, k, fq, fk, ed, edi):
    qt = [(q * f).astype(BF16) for f in fq]
    kt = [(k * f).astype(BF16) for f in fk]
    return qt, kt, (q * ed).astype(BF16), (k * edi).astype(BF16)


def _gla_scores(qt, kt, qd, kd, masks, md):
    a = jnp.where(md, _dot(qd, kd, NT), 0.0)
    for l in range(NLEV):
        a = a + jnp.where(masks[l], _dot(qt[l], kt[l], NT), 0.0)
    return a.astype(BF16)


def _gla_fwd_call(gqk, gv, b, gg, gain, comm=None):
    tp = gqk.shape[0]
    nc = tp // GC
    ns = nc // GS
    n_xc = len(comm.srcs) if comm else 0

    def body(qk_ref, v_ref, b_ref, gg_ref, g_ref, *rest):
        xc_src = rest[:n_xc]
        o_ref, a_ref, st_ref, am_ref = rest[n_xc:n_xc + 4]
        xc_dst = rest[n_xc + 4:2 * n_xc + 4]
        s_scr = rest[2 * n_xc + 4]
        n = pl.program_id(0)
        if n_xc:
            begin, finish = comm.make(xc_src, xc_dst, rest[-2], rest[-1])
            pl.when(n == 0)(begin)
            pl.when(n == ns - 1)(finish)

        @pl.when(n == 0)
        def _():
            s_scr[...] = jnp.zeros_like(s_scr)

        masks, md, second = _gla_masks()
        for cc in range(GS):
            rows = pl.ds(cc * GC, GC)
            qk_c, v_c, b_c, gg_c, o_c, a_c = (r.at[rows] for r in (qk_ref, v_ref, b_ref, gg_ref, o_ref, a_ref))
            for h in range(GLA_HEADS):
                q = qk_c[:, h * GLA_K:(h + 1) * GLA_K]
                k = qk_c[:, GLA_KW + h * GLA_K:GLA_KW + (h + 1) * GLA_K]
                vs = slice(h * GLA_V, (h + 1) * GLA_V)
                v = v_c[:, vs]
                fq, fk, ed, edi, eb, ee, ebl = _gla_factors(b_c, h, second)
                a = _gla_scores(*_gla_scaled(q, k, fq, fk, ed, edi), masks, md)
                am_ref[cc, h] = a
                sb = s_scr[h].astype(BF16)
                st_ref[cc, h] = sb
                o = _dot(a, v, NN) + _dot((q * eb).astype(BF16), sb, NT)
                s_scr[h] = s_scr[h] * ebl + _dot(v, (k * ee).astype(BF16), TN)
                o_c[:, vs] = o
                xh = o * lax.rsqrt(_head_mean(o * o) + EPS)
                a_c[:, vs] = (xh * g_ref[:, vs] * _silu(gg_c[:, vs])).astype(BF16)

    return pl.pallas_call(
        body, name="gla_fwd", grid=(ns,),
        in_specs=[pl.BlockSpec((GS * GC, 2 * GLA_KW), lambda n: (n, 0)),
                  pl.BlockSpec((GS * GC, GLA_W), lambda n: (n, 0)),
                  pl.BlockSpec((GS * GC, GLA_KW), lambda n: (n, 0)),
                  pl.BlockSpec((GS * GC, GLA_W), lambda n: (n, 0)),
                  pl.BlockSpec((1, GLA_W), lambda n: (0, 0))] + [ANY] * n_xc,
        out_specs=[pl.BlockSpec((GS * GC, GLA_W), lambda n: (n, 0)),
                   pl.BlockSpec((GS * GC, GLA_W), lambda n: (n, 0)),
                   pl.BlockSpec((GS, GLA_HEADS, GLA_V, GLA_K), lambda n: (n, 0, 0, 0)),
                   pl.BlockSpec((GS, GLA_HEADS, GC, GC), lambda n: (n, 0, 0, 0))] + [ANY] * n_xc,
        out_shape=[jax.ShapeDtypeStruct((tp, GLA_W), F32), jax.ShapeDtypeStruct((tp, GLA_W), BF16),
                   jax.ShapeDtypeStruct((nc, GLA_HEADS, GLA_V, GLA_K), BF16),
                   jax.ShapeDtypeStruct((nc, GLA_HEADS, GC, GC), BF16)] + (list(comm.out_shapes) if comm else []),
        scratch_shapes=[pltpu.VMEM((GLA_HEADS, GLA_V, GLA_K), F32)] + (_comm_sems(comm) if comm else []),
        compiler_params=_cparams(1),
    )(gqk, gv, b, gg, gain, *(comm.srcs if comm else ()))


def _gla_bwd_call(gqk, gv, b, gg, o_gla, da, states, scores, gain, comm=None):
    tp = gqk.shape[0]
    nc = tp // GC
    ns = nc // GS
    o_gv, o_gg = 2 * GLA_KW, 2 * GLA_KW + GLA_W
    n_xc = len(comm.srcs) if comm else 0

    def body(qk_all, v_all, b_all, gg_all, o_all, da_all, st_ref, am_ref, g_ref, *rest):
        xc_src = rest[:n_xc]
        dp_all, db_all, dg_ref = rest[n_xc:n_xc + 3]
        xc_dst = rest[n_xc + 3:2 * n_xc + 3]
        ds_scr = rest[2 * n_xc + 3]
        n = pl.program_id(0)
        if n_xc:
            begin, finish = comm.make(xc_src, xc_dst, rest[-2], rest[-1])
            pl.when(n == 0)(begin)
            pl.when(n == ns - 1)(finish)

        @pl.when(n == 0)
        def _():
            ds_scr[...] = jnp.zeros_like(ds_scr)
            dg_ref[...] = jnp.zeros_like(dg_ref)

        masks, md, second = _gla_masks()
        for cc, h in [(cc, h) for cc in reversed(range(GS)) for h in range(GLA_HEADS)]:
            rows = pl.ds(cc * GC, GC)
            qk_ref, v_ref, b_scr, gg_ref, o_ref, da_ref, dp_ref, db_scr = (
                r.at[rows] for r in (qk_all, v_all, b_all, gg_all, o_all, da_all, dp_all, db_all))
            cs = slice(h * GLA_K, (h + 1) * GLA_K)
            vs = slice(h * GLA_V, (h + 1) * GLA_V)
            o = o_ref[:, vs]
            rstd = lax.rsqrt(_head_mean(o * o) + EPS)
            xh = o * rstd
            gain_h = g_ref[:, vs]
            g = gg_ref[:, vs]
            sg = _sigmoid(g)
            dah = da_ref[:, vs]
            dp_ref[:, o_gg + h * GLA_V:o_gg + (h + 1) * GLA_V] = (
                dah * (xh * gain_h) * (sg * (1.0 + g * (1.0 - sg)))).astype(BF16)
            dn = dah * (g * sg)
            dg_ref[:, vs] += jnp.sum(dn * xh, axis=0, keepdims=True)
            dxh = dn * gain_h
            do = rstd * (dxh - xh * _head_mean(dxh * xh))
            dob = do.astype(BF16)
            q = qk_ref[:, cs]
            k = qk_ref[:, GLA_KW + h * GLA_K:GLA_KW + (h + 1) * GLA_K]
            v = v_ref[:, vs]
            fq, fk, ed, edi, eb, ee, ebl = _gla_factors(b_scr, h, second)
            qt, kt, qd, kd = _gla_scaled(q, k, fq, fk, ed, edi)
            sp = st_ref[cc, h]
            ds = ds_scr[h]
            dsb = ds.astype(BF16)
            q_in = q * eb
            k_end = k * ee
            da_s = _dot(dob, v, NT)
            dv = _dot(am_ref[cc, h], dob, TN) + _dot(k_end.astype(BF16), dsb, NT)
            dq_in = _dot(dob, sp, NN)
            dk_end = _dot(v, dsb, NN)
            dbl = jnp.sum(sp.astype(F32) * ds, axis=0, keepdims=True) * ebl
            ds_scr[h] = ds * ebl + _dot(dob, q_in.astype(BF16), TN)
            dq = dq_in * eb
            dk = dk_end * ee
            de_end = dk_end * k_end
            db = dq_in * q_in - de_end
            placed = [(GC - 1, jnp.sum(de_end, axis=0, keepdims=True) + dbl)]
            for l, m in enumerate(GLA_LEVELS):
                dal = jnp.where(masks[l], da_s, 0.0).astype(BF16)
                dqt = _dot(dal, kt[l], NN)
                dkt = _dot(dal, qt[l], TN)
                dq = dq + dqt * fq[l]
                dk = dk + dkt * fk[l]
                gl = dqt * (q * fq[l]) - dkt * (k * fk[l])
                db = db + gl
                placed += [(s + m - 1, -jnp.sum(gl[s:s + 2 * m], axis=0, keepdims=True)) for s in range(0, GC, 2 * m)]
            dad = jnp.where(md, da_s, 0.0).astype(BF16)
            dqd = _dot(dad, kd, NN)
            dkd = _dot(dad, qd, TN)
            dq = dq + dqd * ed
            dk = dk + dkd * edi
            gd = dqd * (q * ed) - dkd * (k * edi)
            db = db + gd
            placed += [(s - 1, -jnp.sum(gd[s:s + GLA_SUB], axis=0, keepdims=True)) for s in range(GLA_SUB, GC, GLA_SUB)]
            db_scr[:, cs] = db
            for r, val in placed:
                db_scr[r:r + 1, cs] += val
            dp_ref[:, cs] = (dq * (GLA_K ** -0.5)).astype(BF16)
            dp_ref[:, GLA_KW + h * GLA_K:GLA_KW + (h + 1) * GLA_K] = dk.astype(BF16)
            dp_ref[:, o_gv + h * GLA_V:o_gv + (h + 1) * GLA_V] = dv.astype(BF16)

    rev = lambda n: (ns - 1 - n, 0)
    const = lambda n: (0, 0)
    xc_shapes, xc_sems = (list(comm.out_shapes), _comm_sems(comm)) if n_xc else ([], [])
    return pl.pallas_call(
        body, name="gla_bwd", grid=(ns,),
        in_specs=[pl.BlockSpec((GS * GC, 2 * GLA_KW), rev),
                  pl.BlockSpec((GS * GC, GLA_W), rev),
                  pl.BlockSpec((GS * GC, GLA_KW), rev),
                  pl.BlockSpec((GS * GC, GLA_W), rev),
                  pl.BlockSpec((GS * GC, GLA_W), rev),
                  pl.BlockSpec((GS * GC, GLA_W), rev),
                  pl.BlockSpec((GS, GLA_HEADS, GLA_V, GLA_K), lambda n: (ns - 1 - n, 0, 0, 0)),
                  pl.BlockSpec((GS, GLA_HEADS, GC, GC), lambda n: (ns - 1 - n, 0, 0, 0)),
                  pl.BlockSpec((1, GLA_W), const)] + [ANY] * n_xc,
        out_specs=[pl.BlockSpec((GS * GC, W_GP), rev), pl.BlockSpec((GS * GC, GLA_KW), rev),
                   pl.BlockSpec((1, GLA_W), const)] + [ANY] * n_xc,
        out_shape=[jax.ShapeDtypeStruct((tp, W_GP), BF16), jax.ShapeDtypeStruct((tp, GLA_KW), F32),
                   jax.ShapeDtypeStruct((1, GLA_W), F32)] + xc_shapes,
        scratch_shapes=[pltpu.VMEM((GLA_HEADS, GLA_V, GLA_K), F32)] + xc_sems,
        compiler_params=_cparams(1),
    )(gqk, gv, b, gg, o_gla, da, states, scores, gain, *(comm.srcs if comm else ()))


def _mid_call(a_ret, a_gla, mg, h0, tgt, wbr, wbg, wout, gf):
    tp = h0.shape[0]
    nt = tp // TM

    def body(ar_ref, ag_ref, mg_ref, h_ref, t_ref, wbr_ref, wbg_ref, wo_ref, gf_ref,
             dh1_ref, dag_ref, dm_ref, mb_ref, dh1b_ref, dprb_ref, dpgb_ref, loss_ref, dgf_ref):
        i = pl.program_id(0)

        @pl.when(i == 0)
        def _():
            loss_ref[...] = jnp.zeros_like(loss_ref)
            dgf_ref[...] = jnp.zeros_like(dgf_ref)

        ar, ag = ar_ref[...], ag_ref[...]
        pr = _dot(ar, wbr_ref[...], NN)
        pg = _dot(ag, wbg_ref[...], NN)
        sr = _sigmoid(mg_ref[:, :D_MODEL])
        sg = _sigmoid(mg_ref[:, D_MODEL:])
        merged = (sr * pr + sg * pg).astype(BF16)
        mb_ref[...] = merged
        h1 = h_ref[...] + _dot(merged, wo_ref[...], NN)
        r1 = lax.rsqrt(jnp.mean(h1 * h1, axis=-1, keepdims=True) + EPS)
        xh = h1 * r1
        gfv = gf_ref[...]
        live = jnp.where(i > 0, 1.0, 0.0).astype(F32)
        err = (xh * gfv - t_ref[...]) * live
        loss_ref[...] += jnp.full(loss_ref.shape, 0.5 / D_MODEL, F32) * jnp.sum(err * err)
        dy = err * (1.0 / D_MODEL)
        dgf_ref[...] += jnp.sum(dy * xh, axis=0, keepdims=True)
        dxh = dy * gfv
        dh1 = r1 * (dxh - xh * jnp.mean(dxh * xh, axis=-1, keepdims=True))
        dh1_ref[...] = dh1
        dh1b = dh1.astype(BF16)
        dh1b_ref[...] = dh1b
        dmerged = _dot(dh1b, wo_ref[...], NT)
        dm_ref[:, :D_MODEL] = (dmerged * pr * sr * (1.0 - sr)).astype(BF16)
        dm_ref[:, D_MODEL:] = (dmerged * pg * sg * (1.0 - sg)).astype(BF16)
        dpr = (dmerged * sr).astype(BF16)
        dpg = (dmerged * sg).astype(BF16)
        dprb_ref[...] = dpr
        dpgb_ref[...] = dpg
        dag_ref[...] = _dot(dpg, wbg_ref[...], NT)

    tile = lambda w: pl.BlockSpec((TM, w), lambda i: (i, 0))
    const = lambda r, w: pl.BlockSpec((r, w), lambda i: (0, 0))
    return pl.pallas_call(
        body, name="merge_out_loss", grid=(nt,),
        in_specs=[tile(RET_W), tile(GLA_W), tile(W_M), tile(D_MODEL),
                  pl.BlockSpec((TM, D_MODEL), lambda i: (jnp.maximum(i - 1, 0), 0)),
                  const(RET_W, D_MODEL), const(GLA_W, D_MODEL), const(D_MODEL, D_MODEL), const(1, D_MODEL)],
        out_specs=[tile(D_MODEL), tile(GLA_W), tile(W_M), tile(D_MODEL), tile(D_MODEL), tile(D_MODEL),
                   tile(D_MODEL), const(1, 128), const(1, D_MODEL)],
        out_shape=[jax.ShapeDtypeStruct((tp, D_MODEL), F32), jax.ShapeDtypeStruct((tp, GLA_W), F32),
                   jax.ShapeDtypeStruct((tp, W_M), BF16),
                   jax.ShapeDtypeStruct((tp, D_MODEL), BF16), jax.ShapeDtypeStruct((tp, D_MODEL), BF16),
                   jax.ShapeDtypeStruct((tp, D_MODEL), BF16), jax.ShapeDtypeStruct((tp, D_MODEL), BF16),
                   jax.ShapeDtypeStruct((1, 128), F32), jax.ShapeDtypeStruct((1, D_MODEL), F32)],
        compiler_params=_cparams(1),
    )(a_ret, a_gla, mg, h0, tgt, wbr, wbg, wout, gf)


def _device_step(x2d, tgt2d, meta, norm_gain, w_in_part, w_gate_up, b_gate, ret_gain, gla_gain, branch_parts,
                 final_gain, ck):
    seq = x2d.shape[0]
    tp = T0 + seq
    head = jnp.concatenate([jnp.zeros((PADF, D_MODEL), F32), meta], axis=0)
    wg_pad = jnp.pad(w_gate_up, ((0, 128 - GATE_RANK), (0, 0))).astype(BF16)

    half = RET_QK // 2
    cos, sin = (jnp.asarray(t) for t in _rope_tables(tp))
    lgam = jnp.log1p(-(2.0 ** (-5.0 - jnp.arange(RET_HEADS, dtype=F32))))
    pmat = jnp.asarray(_gla_tril(), BF16)
    pmat_t = jnp.asarray(_gla_tril().T.copy(), BF16)

    h0, u, g_in = _rms_call(x2d, head, norm_gain, _gather_plan([w_in_part], relay=(True,)))
    sw, hc = w_in_part.shape
    w_in_t = g_in.reshape(4, 2, sw, hc).transpose(0, 2, 1, 3).reshape(4 * sw, 2 * hc)
    w_r = w_in_t
    w_g = jnp.pad(w_in_t[W_R:W_R + W_G], ((0, W_GP - W_G), (0, 0)))
    w_m = w_in_t[W_R + W_G:]
    tab = pl.BlockSpec((_proj_rows(tp), half), lambda j, i: (i, 0))
    rqk = _mm_nn("proj_rqk", u, w_r, BF16, D_MODEL, 0, 2 * D_MODEL, _rope_epilogue, (cos, sin), (tab, tab))
    rv = _mm_nn("proj_rv", u, w_r, BF16, RET_W, 2 * D_MODEL, RET_W)
    rg = _mm_nn("proj_rg", u, w_r, F32, RET_W, 4 * D_MODEL, RET_W)
    gqk = _mm_nn("proj_gqk", u, w_g, F32, 2 * GLA_KW, 0, 2 * GLA_KW, _gqk_epilogue)
    gv = _mm_nn("proj_gv", u, w_g, BF16, GLA_W, 2 * GLA_KW, GLA_W)
    gg = _mm_nn("proj_gg", u, w_g, F32, GLA_W, 2 * GLA_KW + GLA_W, GLA_W)
    mg = _mm_nn("proj_mg", u, w_m, F32, W_M, 0, W_M)

    o_ret, a_ret, st_ret, sc_ret = _ret_fwd_call(rqk, rv, rg, ret_gain, lgam)
    glr, z_gate, b_dec = _gla_gate_call(u, w_g, wg_pad, b_gate, pmat)
    o_gla, a_gla, st_gla, sc_gla, g_br, g_bg, g_out = _gla_fwd_call(gqk, gv, b_dec, gg, gla_gain,
                                                                    comm=_spread_plan(branch_parts))
    wbr = g_br.reshape(RET_W, D_MODEL)
    wbg = g_bg.reshape(GLA_W, D_MODEL)
    wout = g_out.reshape(D_MODEL, D_MODEL)

    gf = final_gain.reshape(1, D_MODEL)
    (dh1, da_gla, dm, merged_b, dh1_b, dpr_b, dpg_b, loss, dgf) = _mid_call(
        a_ret, a_gla, mg, h0, tgt2d, wbr, wbg, wout, gf)

    names_b = ("w_branch_ret", "w_branch_gla", "w_out")
    g2_b = [_mm_tn("dw_br", a_ret, dpr_b, D_MODEL).reshape(4, 2, RET_W // 8, D_MODEL).transpose(1, 0, 2, 3),
            _mm_tn("dw_bg", a_gla, dpg_b, D_MODEL).reshape(4, 2, GLA_W // 8, D_MODEL).transpose(1, 0, 2, 3),
            _mm_tn("dw_out", merged_b, dh1_b, D_MODEL).reshape(4, 2, D_MODEL // 8, D_MODEL).transpose(1, 0, 2, 3)]
    sib_b = _swap_halves_call("swap_halves_branch", g2_b)
    sum_b = [_add_half_call("add_half_" + nm, g, b, ck) for nm, g, b in zip(names_b, g2_b, sib_b)]
    d_g, db_dec, dgla_gain, *chips_b = _gla_bwd_call(gqk, gv, b_dec, gg, o_gla, da_gla, st_gla, sc_gla, gla_gain,
                                                     comm=_exchange_plan(sum_b))
    d_g, dwg, dbg, dw_glr = _gla_gate_bwd_call(db_dec, z_gate, glr, u, wg_pad, pmat_t, d_g)
    mine = [_add_chips_call("add_chips_" + nm, g, b, p, ck) for nm, g, b, p in zip(names_b, g2_b, sib_b, chips_b)]

    d_r, dret_gain = _ret_bwd_call(rqk, rv, rg, o_ret, dpr_b, wbr, st_ret, sc_ret, ret_gain, lgam, cos, sin)

    dwp = _mm_tn("dw_r", u, d_r, 3 * D_MODEL, out_cols=IN_PAD)
    dwp = _mm_tn("dw_g", u, d_g, 3 * D_MODEL, ncols=W_GP - 128, into=dwp, col0=W_R)
    g2_in = _place_merge_cols_call(dwp, _mm_tn("dw_m", u, dm, 2 * D_MODEL), dw_glr).reshape(2, D_MODEL // 2, IN_PAD)

    du, sib_in = _mm_nt_acc("du_g", d_g, w_g, W_GP, comm=_swap_plan([g2_in]), tb=_proj_rows(tp))
    sum_in = _add_rows_call("add_half_w_in", g2_in, sib_in, ck)
    du, chips_in = _mm_nt_acc("du_r", d_r, w_r, 2 * D_MODEL, acc_in=du, comm=_exchange_window_plan(sum_in),
                              tb=_proj_rows(tp))
    tile = pl.BlockSpec((TB, D_MODEL), lambda i, kk: (i, 0))
    row = pl.BlockSpec((1, D_MODEL), lambda i, kk: (0, 0))
    dx, dmeta, dnorm_gain = _mm_nt_acc(
        "du_m", dm, w_m, W_M, acc_in=du, epilogue=_rms_bwd_epilogue, extras=(h0, norm_gain, dh1),
        extra_specs=(tile, row, tile),
        extra_out_shapes=(jax.ShapeDtypeStruct((seq, D_MODEL), F32), jax.ShapeDtypeStruct((N_META, D_MODEL), F32),
                          jax.ShapeDtypeStruct((1, D_MODEL), F32)),
        extra_out_specs=(ANY, pl.BlockSpec((N_META, D_MODEL), lambda i, kk: (0, 0)), row),
        extra_scratch=(pltpu.VMEM((2, TB, D_MODEL), F32), pltpu.SemaphoreType.DMA((2,))))
    small = dict(norm_gain=dnorm_gain, b_gate=dbg, ret_norm_gain=dret_gain, gla_norm_gain=dgla_gain,
                 final_norm_gain=dgf, w_gate_up=dwg[:GATE_RANK], meta_tokens=dmeta, loss=loss[0, 0])
    rows = -(-sum(sz for _, sz in SMALL) // 128 // 8) * 8
    mine_in, g_small = _add_window_call("add_chips_w_in", g2_in, sib_in, chips_in, ck,
                                        _gather_plan([_pack_rows([small[nm] for nm, _ in SMALL], rows)]))
    full = _join_halves_call("join_halves", [mine_in] + mine)

    return dict(dx=dx, small=g_small, w_in=full[0], w_branch_ret=full[1], w_branch_gla=full[2], w_out=full[3])


MESH = pl.DeviceIdType.MESH
ANY = pl.BlockSpec(memory_space=pl.ANY)


def _place():
    return lax.axis_index("x"), lax.axis_index("y"), lax.axis_index("c")


def _gather8_call(name, parts):
    comm = _gather_plan(parts)
    n = len(parts)

    def body(*refs):
        begin, finish = comm.make(refs[:n], refs[n:2 * n], refs[-2], refs[-1])
        begin()
        finish()

    return pl.pallas_call(
        body, name=name, out_shape=list(comm.out_shapes), in_specs=[ANY] * n, out_specs=[ANY] * n,
        scratch_shapes=_comm_sems(comm),
    )(*parts)


def _swap_halves_call(name, gs):
    n = len(gs)

    def body(*refs):
        g_refs, b_refs = refs[:n], refs[n:2 * n]
        send_sems, recv_sems = refs[2 * n:]
        x, y, c = _place()
        copies = [pltpu.make_async_remote_copy(
            src_ref=g_refs[t].at[1 - c], dst_ref=b_refs[t], send_sem=send_sems.at[t], recv_sem=recv_sems.at[t],
            device_id=(x, y, 1 - c), device_id_type=MESH) for t in range(n)]
        for cp in copies:
            cp.start()
        for cp in copies:
            cp.wait()

    return pl.pallas_call(
        body, name=name,
        out_shape=[jax.ShapeDtypeStruct(g.shape[1:], g.dtype) for g in gs],
        in_specs=[ANY] * n, out_specs=[ANY] * n,
        scratch_shapes=[pltpu.SemaphoreType.DMA((n,)), pltpu.SemaphoreType.DMA((n,))],
    )(*gs)


def _join_halves_call(name, ts):
    n = len(ts)

    def body(*refs):
        o_refs = refs[n:2 * n]
        send_sems, recv_sems = refs[2 * n:]
        x, y, c = _place()
        copies = [pltpu.make_async_remote_copy(
            src_ref=o_refs[t].at[c], dst_ref=o_refs[t].at[c], send_sem=send_sems.at[t], recv_sem=recv_sems.at[t],
            device_id=(x, y, 1 - c), device_id_type=MESH) for t in range(n)]
        for cp in copies:
            cp.start()
        for t in range(n):
            copies[t].wait_send()
            pltpu.make_async_remote_copy(
                src_ref=o_refs[t].at[c], dst_ref=o_refs[t].at[1 - c], send_sem=send_sems.at[t],
                recv_sem=recv_sems.at[t], device_id=(x, y, 1 - c), device_id_type=MESH).wait_recv()

    return pl.pallas_call(
        body, name=name,
        out_shape=[jax.ShapeDtypeStruct(t.shape, t.dtype) for t in ts],
        in_specs=[ANY] * n, out_specs=[ANY] * n, input_output_aliases={t: t for t in range(n)},
        scratch_shapes=[pltpu.SemaphoreType.DMA((n,)), pltpu.SemaphoreType.DMA((n,))],
    )(*ts)


def _row_block(rows, cols, budget):
    best = 8
    for rb in range(8, rows + 1, 8):
        if rows % rb == 0 and rb * cols * 4 <= budget:
            best = rb
    return best


def _add_half_call(name, g, b, ck):
    _, _, r, cc = g.shape
    rb = _row_block(r, cc, 2 * 1024 * 1024)

    def body(ck_ref, g_ref, b_ref, o_ref):
        o_ref[...] = (g_ref[...] + b_ref[...]).astype(BF16)

    return pl.pallas_call(
        body, name=name,
        grid_spec=pltpu.PrefetchScalarGridSpec(
            num_scalar_prefetch=1, grid=(4, r // rb),
            in_specs=[pl.BlockSpec((None, None, rb, cc), lambda k, i, ck_ref: (ck_ref[0], k, i, 0)),
                      pl.BlockSpec((None, rb, cc), lambda k, i, ck_ref: (k, i, 0))],
            out_specs=pl.BlockSpec((None, rb, cc), lambda k, i, ck_ref: (k, i, 0))),
        out_shape=jax.ShapeDtypeStruct(b.shape, BF16),
        compiler_params=_cparams(2),
    )(ck, g, b)


def _add_rows_call(name, g, b, ck):
    _, r, cc = g.shape
    rb = _row_block(r, cc, 2 * 1024 * 1024)

    def body(ck_ref, g_ref, b_ref, o_ref):
        o_ref[...] = (g_ref[...] + b_ref[...]).astype(BF16)

    return pl.pallas_call(
        body, name=name,
        grid_spec=pltpu.PrefetchScalarGridSpec(
            num_scalar_prefetch=1, grid=(r // rb,),
            in_specs=[pl.BlockSpec((None, rb, cc), lambda i, ck_ref: (ck_ref[0], i, 0)),
                      pl.BlockSpec((rb, cc), lambda i, ck_ref: (i, 0))],
            out_specs=pl.BlockSpec((rb, cc), lambda i, ck_ref: (i, 0))),
        out_shape=jax.ShapeDtypeStruct((r, cc), BF16),
        compiler_params=_cparams(1),
    )(ck, g, b)


def _add_window_call(name, g, b, p, ck, comm):
    _, r, _ = g.shape
    nb, step = WIN_W // 128, WIN_STEP // 128
    n_xc = len(comm.srcs)

    def body(ck_ref, g_ref, b_ref, p0_ref, p1_ref, p2_ref, *rest):
        o_ref = rest[n_xc]
        i = pl.program_id(0)
        begin, finish = comm.make(rest[:n_xc], rest[n_xc + 1:2 * n_xc + 1], rest[-2], rest[-1])
        pl.when(i == 0)(begin)
        own = g_ref[...] + b_ref[...]
        o_ref[...] = ((own + p0_ref[...].astype(F32)) + p1_ref[...].astype(F32)) + p2_ref[...].astype(F32)
        pl.when(i == nb - 1)(finish)

    def peer(j):
        return pl.BlockSpec((None, r, 128), lambda i, ck_ref: (j, 0, i))

    return pl.pallas_call(
        body, name=name,
        grid_spec=pltpu.PrefetchScalarGridSpec(
            num_scalar_prefetch=1, grid=(nb,),
            in_specs=[pl.BlockSpec((None, r, 128), lambda i, ck_ref: (ck_ref[0], 0, step * ck_ref[1] + i)),
                      pl.BlockSpec((r, 128), lambda i, ck_ref: (0, step * ck_ref[1] + i)),
                      peer(0), peer(1), peer(2)] + [ANY] * n_xc,
            out_specs=[pl.BlockSpec((None, r, 128), lambda i, ck_ref: (ck_ref[0], 0, i))] + [ANY] * n_xc,
            scratch_shapes=_comm_sems(comm)),
        out_shape=[jax.ShapeDtypeStruct((2, r, WIN_W), F32)] + list(comm.out_shapes),
        compiler_params=_cparams(1),
    )(ck, g, b, p, p, p, *comm.srcs)


def _add_chips_call(name, g, b, p, ck):
    _, _, r, cc = g.shape
    rb = _row_block(r, cc, 2 * 1024 * 1024)

    def body(ck_ref, g_ref, b_ref, p0_ref, p1_ref, p2_ref, o_ref):
        own = g_ref[...] + b_ref[...]
        o_ref[...] = ((own + p0_ref[...].astype(F32)) + p1_ref[...].astype(F32)) + p2_ref[...].astype(F32)

    def peer(j):
        return pl.BlockSpec((None, rb, cc), lambda i, ck_ref: (j, i, 0))

    return pl.pallas_call(
        body, name=name,
        grid_spec=pltpu.PrefetchScalarGridSpec(
            num_scalar_prefetch=1, grid=(r // rb,),
            in_specs=[pl.BlockSpec((None, None, rb, cc), lambda i, ck_ref: (ck_ref[0], ck_ref[1], i, 0)),
                      pl.BlockSpec((None, rb, cc), lambda i, ck_ref: (ck_ref[1], i, 0)),
                      peer(0), peer(1), peer(2)],
            out_specs=pl.BlockSpec((None, rb, cc), lambda i, ck_ref: (ck_ref[0], i, 0))),
        out_shape=jax.ShapeDtypeStruct((2, r, cc), F32),
        compiler_params=_cparams(1),
    )(ck, g, b, p, p, p)


def _sum8_call(name, g):
    def body(g_ref, o_ref):
        acc = g_ref[0]
        for d in range(1, 8):
            acc = acc + g_ref[d]
        o_ref[...] = acc

    return pl.pallas_call(body, name=name, out_shape=jax.ShapeDtypeStruct(g.shape[1:], F32))(g)


def _adamw_call(name, w, g, m, v):
    r, cc = w.shape
    if r % 8 == 0 or r * cc * 4 <= 1024 * 1024:
        rb = _row_block(r, cc, 1024 * 1024) if r % 8 == 0 else r
        grid, spec = (r // rb,), pl.BlockSpec((rb, cc), lambda i: (i, 0))
    else:
        grid, spec = (cc // 128,), pl.BlockSpec((r, 128), lambda i: (0, i))

    def body(w_ref, g_ref, m_ref, v_ref, d_ref, m2_ref, v2_ref):
        _adamw_update(g_ref[...], w_ref, m_ref, v_ref, d_ref, m2_ref, v2_ref)

    return pl.pallas_call(
        body, name=name, grid=grid, in_specs=[spec] * 4, out_specs=[spec] * 3,
        out_shape=[jax.ShapeDtypeStruct((r, cc), F32)] * 3, compiler_params=_cparams(1),
    )(w, g, m, v)


def _adamw_update(gv, w_ref, m_ref, v_ref, d_ref, m2_ref, v2_ref):
    m2 = ADAM_B1 * m_ref[...] + (1.0 - ADAM_B1) * gv
    v2 = ADAM_B2 * v_ref[...] + (1.0 - ADAM_B2) * (gv * gv)
    m_hat = m2 / (1.0 - ADAM_B1 ** ADAM_STEP)
    v_hat = v2 / (1.0 - ADAM_B2 ** ADAM_STEP)
    d_ref[...] = -ADAM_LR * (m_hat / (jnp.sqrt(v_hat) + ADAM_EPS) + ADAM_WD * w_ref[...])
    m2_ref[...] = m2
    v2_ref[...] = v2


def _adamw_window_call(name, w_t, f, m_t, v_t, lane0):
    s, r = w_t.shape
    wl = f.shape[1]

    def body(l0_ref, w_ref, f_ref, m_ref, v_ref, g_ref, d_ref, m2_ref, v2_ref):
        gv = pltpu.roll(f_ref[...], lax.rem(wl - l0_ref[0], wl), axis=1).T[:s]
        g_ref[...] = gv
        _adamw_update(gv, w_ref, m_ref, v_ref, d_ref, m2_ref, v2_ref)

    spec = pl.BlockSpec((s, 128), lambda i, l0: (0, i))
    return pl.pallas_call(
        body, name=name,
        grid_spec=pltpu.PrefetchScalarGridSpec(
            num_scalar_prefetch=1, grid=(r // 128,),
            in_specs=[spec, pl.BlockSpec((128, wl), lambda i, l0: (i, 0)), spec, spec], out_specs=[spec] * 4),
        out_shape=[jax.ShapeDtypeStruct((s, r), F32)] * 4, compiler_params=_cparams(1),
    )(lane0, w_t, f, m_t, v_t)


SMALL = (("norm_gain", D_MODEL), ("b_gate", GLA_KW), ("ret_norm_gain", RET_W), ("gla_norm_gain", GLA_W),
         ("final_norm_gain", D_MODEL), ("w_gate_up", GATE_RANK * GLA_KW), ("meta_tokens", N_META * D_MODEL),
         ("loss", 1))


def _pack_rows(vecs, rows):
    flat = jnp.concatenate([v.reshape(-1) for v in vecs])
    return jnp.pad(flat, (0, rows * 128 - flat.shape[0])).reshape(rows, 128)


def kernel(x, meta_tokens, norm_gain, w_in, w_gate_up, b_gate, ret_norm_gain, gla_norm_gain, w_branch_ret, w_branch_gla, w_out, final_norm_gain, loss_target, m_meta_tokens, m_norm_gain, m_w_in, m_w_gate_up, m_b_gate, m_ret_norm_gain, m_gla_norm_gain, m_w_branch_ret, m_w_branch_gla, m_w_out, m_final_norm_gain, v_meta_tokens, v_norm_gain, v_w_in, v_w_gate_up, v_b_gate, v_ret_norm_gain, v_gla_norm_gain, v_w_branch_ret, v_w_branch_gla, v_w_out, v_final_norm_gain):
    xi, yi, ci = _place()
    kme = 2 * xi + yi
    ck = jnp.stack([ci, kme]).astype(jnp.int32)
    sw_in = w_in.shape[2]

    def my_half(a, dtype):
        r, cc = a.shape
        return lax.dynamic_index_in_dim(a.reshape(2, r // 2, cc), ci, 0, keepdims=False).astype(dtype)

    g_meta, g_wg = _gather8_call("gather_small_weights", [my_half(meta_tokens, F32), my_half(w_gate_up[0], F32)])
    branch_parts = [my_half(w_branch_ret[0], BF16), my_half(w_branch_gla[0], BF16), my_half(w_out[0], BF16)]
    meta = g_meta.reshape(4, 2, N_META // 2, D_MODEL // 4).transpose(1, 2, 0, 3).reshape(N_META, D_MODEL)
    wg_full = g_wg.reshape(4, 2, GATE_RANK // 2, GLA_KW // 4).transpose(1, 2, 0, 3).reshape(GATE_RANK, GLA_KW)

    w_in_part = lax.dynamic_slice_in_dim(w_in[0].T, ci * (D_MODEL // 2), D_MODEL // 2, axis=1).astype(BF16)
    loc = _device_step(x[0], loss_target[0], meta, norm_gain, w_in_part, wg_full, b_gate, ret_norm_gain,
                       gla_norm_gain,
                       branch_parts, final_norm_gain, ck)
    names = ("w_in", "w_branch_ret", "w_branch_gla", "w_out")
    full = [loc[nm] for nm in names]
    big_w = dict(w_in=w_in[0], w_branch_ret=w_branch_ret[0], w_branch_gla=w_branch_gla[0], w_out=w_out[0])
    big_m = dict(w_in=m_w_in[0], w_branch_ret=m_w_branch_ret[0], w_branch_gla=m_w_branch_gla[0], w_out=m_w_out[0])
    big_v = dict(w_in=v_w_in[0], w_branch_ret=v_w_branch_ret[0], w_branch_gla=v_w_branch_gla[0], w_out=v_w_out[0])
    grads, deltas, new_m, new_v = {}, {}, {}, {}
    for nm, f in zip(names, full):
        shape = big_w[nm].shape
        if nm == "w_in":
            lane0 = ((sw_in - WIN_STEP) * kme).astype(jnp.int32).reshape(1)
            g, d, m2, v2 = (a.T for a in _adamw_window_call(
                "adamw_" + nm, big_w[nm].T, f.reshape(shape[0], WIN_W), big_m[nm].T, big_v[nm].T, lane0))
        else:
            g = f.reshape(shape)
            d, m2, v2 = _adamw_call("adamw_" + nm, big_w[nm], g, big_m[nm], big_v[nm])
        grads[nm], deltas[nm], new_m[nm], new_v[nm] = (a.reshape((1,) + shape) for a in (g, d, m2, v2))

    tot = _sum8_call("sum_small_grads", loc["small"]).reshape(-1)
    off = 0
    sg = {}
    for nm, sz in SMALL:
        sg[nm] = tot[off:off + sz]
        off += sz
    loss = sg.pop("loss")[0]
    sg["w_gate_up"] = lax.dynamic_slice_in_dim(sg["w_gate_up"].reshape(GATE_RANK, GLA_KW), kme * (GLA_KW // 4),
                                               GLA_KW // 4, axis=1)
    sg["meta_tokens"] = lax.dynamic_slice_in_dim(sg["meta_tokens"].reshape(N_META, D_MODEL), kme * (D_MODEL // 4),
                                                 D_MODEL // 4, axis=1)
    small_w = dict(norm_gain=norm_gain, b_gate=b_gate, ret_norm_gain=ret_norm_gain, gla_norm_gain=gla_norm_gain,
                   final_norm_gain=final_norm_gain, w_gate_up=w_gate_up, meta_tokens=meta_tokens)
    small_m = dict(norm_gain=m_norm_gain, b_gate=m_b_gate, ret_norm_gain=m_ret_norm_gain,
                   gla_norm_gain=m_gla_norm_gain, final_norm_gain=m_final_norm_gain, w_gate_up=m_w_gate_up,
                   meta_tokens=m_meta_tokens)
    small_v = dict(norm_gain=v_norm_gain, b_gate=v_b_gate, ret_norm_gain=v_ret_norm_gain,
                   gla_norm_gain=v_gla_norm_gain, final_norm_gain=v_final_norm_gain, w_gate_up=v_w_gate_up,
                   meta_tokens=v_meta_tokens)
    for nm in small_w:
        shape = small_w[nm].shape
        as2d = lambda a: a.reshape((-1, shape[-1]))
        grads[nm] = sg[nm].reshape(shape)
        deltas[nm], new_m[nm], new_v[nm] = (a.reshape(shape) for a in _adamw_call(
            "adamw_" + nm, as2d(small_w[nm]), as2d(sg[nm]), as2d(small_m[nm]), as2d(small_v[nm])))

    out_order = ("meta_tokens", "norm_gain", "w_in", "w_gate_up", "b_gate", "ret_norm_gain", "gla_norm_gain",
                 "w_branch_ret", "w_branch_gla", "w_out", "final_norm_gain")
    dx = loc["dx"].reshape(x.shape)
    return (loss, dx, *[grads[nm] for nm in out_order], *[deltas[nm] for nm in out_order],
            *[new_m[nm] for nm in out_order], *[new_v[nm] for nm in out_order])
```

```python
import math
from typing import Callable, NamedTuple

import numpy as np
import jax
import jax.numpy as jnp
from jax import lax
from jax.experimental import pallas as pl
from jax.experimental.pallas import tpu as pltpu

F32 = jnp.float32
BF16 = jnp.bfloat16

D_MODEL = 1024
N_META = 16
EPS = 1e-6
ROPE_BASE = 10000.0
RET_HEADS, RET_QK, RET_V = 4, 256, 512
RET_W = RET_HEADS * RET_V
GLA_HEADS, GLA_K, GLA_V = 4, 128, 256
GLA_W = GLA_HEADS * GLA_V
GLA_KW = GLA_HEADS * GLA_K
GATE_RANK = 16
GATE_TAU = 16.0
GLA_SUB = 16

TM = 256
T0 = TM
PADF = T0 - N_META
GC = 128
GS = 3
TB = 768
TK = 768

W_R = 6144
W_G = 3088
W_GP = 3200
W_M = 2048
IN_COLS = W_R + W_G + W_M
WIN_STEP = (IN_COLS // 4) // 128 * 128
WIN_W = -(-(3 * (IN_COLS // 4 - WIN_STEP) + IN_COLS // 4) // 128) * 128
IN_PAD = 3 * WIN_STEP + WIN_W

ADAM_LR, ADAM_B1, ADAM_B2, ADAM_EPS, ADAM_WD, ADAM_STEP = 0.001, 0.9, 0.999, 1e-08, 0.01, 10

VMEM_LIMIT = 56 * 1024 * 1024

NN = ((1,), (0,))
NT = ((1,), (1,))
TN = ((0,), (0,))


def _dot(a, b, dims):
    return lax.dot_general(a, b, (dims, ((), ())), preferred_element_type=F32)


def _cparams(n_axes):
    return pltpu.CompilerParams(dimension_semantics=("arbitrary",) * n_axes, vmem_limit_bytes=VMEM_LIMIT)


def _sigmoid(x):
    return 0.5 * jnp.tanh(0.5 * x) + 0.5


def _silu(x):
    h = 0.5 * x
    return h + h * jnp.tanh(h)


def _head_mean(x):
    return jnp.mean(x, axis=-1, keepdims=True)


def _split3(x):
    hi = x.astype(BF16)
    r1 = x - hi.astype(F32)
    mid = r1.astype(BF16)
    lo = (r1 - mid.astype(F32)).astype(BF16)
    return hi, mid, lo


def _exact_pm(p, x):
    hi, mid, lo = _split3(x)
    return _dot(p, hi, NN) + _dot(p, mid, NN) + _dot(p, lo, NN)


def _rms_call(x2d, head, gain, comm):
    tp = T0 + x2d.shape[0]
    nt = tp // TM
    n_xc = len(comm.srcs)

    def body(x_ref, hd_ref, g_ref, *rest):
        xc_src = rest[:n_xc]
        h_ref, u_ref = rest[n_xc:n_xc + 2]
        xc_dst = rest[n_xc + 2:2 * n_xc + 2]
        i = pl.program_id(0)
        begin, finish = comm.make(xc_src, xc_dst, rest[-2], rest[-1])
        pl.when(i == 0)(begin)
        h = jnp.where(i == 0, hd_ref[...], x_ref[...])
        h_ref[...] = h
        r = lax.rsqrt(jnp.mean(h * h, axis=-1, keepdims=True) + EPS)
        u_ref[...] = (h * r * g_ref[...]).astype(BF16)
        pl.when(i == nt - 1)(finish)

    tile = pl.BlockSpec((TM, D_MODEL), lambda i: (i, 0))
    return pl.pallas_call(
        body, name="rms_in", grid=(nt,),
        in_specs=[pl.BlockSpec((TM, D_MODEL), lambda i: (jnp.maximum(i - 1, 0), 0)),
                  pl.BlockSpec((T0, D_MODEL), lambda i: (0, 0)), pl.BlockSpec((1, D_MODEL), lambda i: (0, 0))]
        + [ANY] * n_xc,
        out_specs=[tile, tile] + [ANY] * n_xc,
        out_shape=[jax.ShapeDtypeStruct((tp, D_MODEL), F32), jax.ShapeDtypeStruct((tp, D_MODEL), BF16)]
        + list(comm.out_shapes),
        scratch_shapes=_comm_sems(comm), compiler_params=_cparams(1),
    )(x2d, head, gain, *comm.srcs)


PROJ_ROWS_MAX = 1408


def _proj_rows(m):
    return max(r for r in range(16, PROJ_ROWS_MAX + 1, 16) if m % r == 0)


def _mm_nn(name, a, bt, out_dtype, tn, col0, ncols, epilogue=None, extras=(), extra_specs=()):
    m, k = a.shape
    nj, j0 = ncols // tn, col0 // tn
    tb = _proj_rows(m)

    def body(a_ref, b_ref, *rest):
        *ex, o_ref = rest
        acc = _dot(a_ref[...], b_ref[...], NT)
        if epilogue is None:
            o_ref[...] = acc.astype(out_dtype)
        else:
            epilogue(acc, o_ref, *ex)

    return pl.pallas_call(
        body, name=name, grid=(nj, m // tb),
        in_specs=[pl.BlockSpec((tb, k), lambda j, i: (i, 0)), pl.BlockSpec((tn, k), lambda j, i: (j0 + j, 0))]
        + list(extra_specs),
        out_specs=pl.BlockSpec((tb, tn), lambda j, i: (i, j)),
        out_shape=jax.ShapeDtypeStruct((m, ncols), out_dtype),
        compiler_params=_cparams(2),
    )(a, bt, *extras)


def _rope_tables(tp):
    half = RET_QK // 2
    pos = np.arange(tp, dtype=np.float32) - np.float32(PADF)
    inv = (ROPE_BASE ** (-np.arange(half, dtype=np.float64) / half)).astype(np.float32)
    ang = (pos[:, None] * inv[None, :]).astype(np.float64)
    return np.cos(ang).astype(np.float32), np.sin(ang).astype(np.float32)


def _rope_epilogue(acc, o_ref, cos_ref, sin_ref):
    scale = jnp.where(pl.program_id(0) == 1, RET_QK ** -0.5, 1.0).astype(F32)
    cos, sin = cos_ref[...], sin_ref[...]
    half = RET_QK // 2
    for h in range(RET_HEADS):
        t1 = acc[:, h * RET_QK:h * RET_QK + half]
        t2 = acc[:, h * RET_QK + half:(h + 1) * RET_QK]
        o_ref[:, h * RET_QK:h * RET_QK + half] = ((t1 * cos - t2 * sin) * scale).astype(BF16)
        o_ref[:, h * RET_QK + half:(h + 1) * RET_QK] = ((t2 * cos + t1 * sin) * scale).astype(BF16)


def _gqk_epilogue(acc, o_ref):
    o_ref[:, :GLA_KW] = acc[:, :GLA_KW] * (GLA_K ** -0.5)
    o_ref[:, GLA_KW:] = acc[:, GLA_KW:]


class _Comm(NamedTuple):
    srcs: tuple
    out_shapes: tuple
    n_sems: int
    make: Callable


def _comm_sems(comm):
    return [pltpu.SemaphoreType.DMA((comm.n_sems,)), pltpu.SemaphoreType.DMA((comm.n_sems,))]


def _start_wait(copies):
    def begin():
        for cp in copies:
            cp.start()

    def finish():
        for cp in copies:
            cp.wait()

    return begin, finish


def _other_chips(x, y):
    return [(1 - x, y), (x, 1 - y), (1 - x, 1 - y)]


def _gather_plan(parts, relay=()):
    n = len(parts)
    relay = tuple(relay) + (False,) * (n - len(relay))

    def make(x_refs, out_refs, send_sems, recv_sems):
        x, y, c = _place()
        me, sibling = (x, y, c), (x, y, 1 - c)
        xn, yn, dg = (1 - x, y), (x, 1 - y), (1 - x, 1 - y)

        def slot(t, px, py, pc, half=None):
            ref = out_refs[t].at[4 * px + 2 * py + pc]
            if half is None:
                return ref
            cols = ref.shape[-1] // 2
            return ref.at[:, pl.ds(half * cols, cols)]

        def copy(t, k, dst, to, src=None):
            return pltpu.make_async_remote_copy(
                src_ref=dst if src is None else src, dst_ref=dst, send_sem=send_sems.at[8 * t + k],
                recv_sem=recv_sems.at[8 * t + k], device_id=to, device_id_type=MESH)

        mine = [pltpu.make_async_copy(x_refs[t], slot(t, *me), send_sems.at[8 * n + t]) for t in range(n)]
        sent = []
        for t in range(n):
            sent.append(copy(t, 0, slot(t, *me), sibling, src=x_refs[t]))
            sent.append(copy(t, 1, slot(t, *me), (*xn, c), src=x_refs[t]))
            sent.append(copy(t, 2, slot(t, *me), (*yn, c), src=x_refs[t]))
            if not relay[t]:
                sent.append(copy(t, 3, slot(t, *me), (*dg, c), src=x_refs[t]))

        def begin():
            for cp in mine + sent:
                cp.start()

        def finish():
            later = []

            def start(cp):
                cp.start()
                later.append(cp)

            for t in range(n):
                copy(t, 2, slot(t, *yn, c), me).wait_recv()
                if relay[t]:
                    start(copy(t, 3, slot(t, *yn, c, half=0), (*xn, c)))
                start(copy(t, 6, slot(t, *yn, c), sibling))
            for t in range(n):
                copy(t, 1, slot(t, *xn, c), me).wait_recv()
                if relay[t]:
                    start(copy(t, 4, slot(t, *xn, c, half=1), (*yn, c)))
                start(copy(t, 5, slot(t, *xn, c), sibling))
            for t in range(n):
                if relay[t]:
                    copy(t, 3, slot(t, *dg, c, half=0), me).wait_recv()
                    copy(t, 4, slot(t, *dg, c, half=1), me).wait_recv()
                else:
                    copy(t, 3, slot(t, *dg, c), me).wait_recv()
                start(copy(t, 7, slot(t, *dg, c), sibling))
            for t in range(n):
                copy(t, 0, slot(t, *sibling), me).wait_recv()
                copy(t, 5, slot(t, *xn, 1 - c), me).wait_recv()
                copy(t, 6, slot(t, *yn, 1 - c), me).wait_recv()
                copy(t, 7, slot(t, *dg, 1 - c), me).wait_recv()
            for cp in sent + later:
                cp.wait_send()
            for cp in mine:
                cp.wait()

        return begin, finish

    return _Comm(tuple(parts), tuple(jax.ShapeDtypeStruct((8,) + p.shape, p.dtype) for p in parts), 9 * n, make)


def _exchange_plan(ss):
    def make(s_refs, b_refs, send_sems, recv_sems):
        x, y, c = _place()
        return _start_wait([pltpu.make_async_remote_copy(
            src_ref=s_refs[t].at[2 * chip[0] + chip[1]], dst_ref=b_refs[t].at[j], send_sem=send_sems.at[3 * t + j],
            recv_sem=recv_sems.at[3 * t + j], device_id=(*chip, c), device_id_type=MESH)
            for t in range(len(s_refs)) for j, chip in enumerate(_other_chips(x, y))])

    return _Comm(tuple(ss), tuple(jax.ShapeDtypeStruct((3,) + s.shape[1:], s.dtype) for s in ss), 3 * len(ss), make)


def _exchange_window_plan(s):
    def make(s_refs, b_refs, send_sems, recv_sems):
        x, y, c = _place()
        return _start_wait([pltpu.make_async_remote_copy(
            src_ref=s_refs[0].at[:, pl.ds(pl.multiple_of((2 * chip[0] + chip[1]) * WIN_STEP, 128), WIN_W)],
            dst_ref=b_refs[0].at[j], send_sem=send_sems.at[j], recv_sem=recv_sems.at[j], device_id=(*chip, c),
            device_id_type=MESH) for j, chip in enumerate(_other_chips(x, y))])

    return _Comm((s,), (jax.ShapeDtypeStruct((3, s.shape[0], WIN_W), s.dtype),), 3, make)


def _swap_plan(gs):
    def make(g_refs, b_refs, send_sems, recv_sems):
        x, y, c = _place()
        return _start_wait([pltpu.make_async_remote_copy(
            src_ref=g_refs[t].at[1 - c], dst_ref=b_refs[t], send_sem=send_sems.at[t], recv_sem=recv_sems.at[t],
            device_id=(x, y, 1 - c), device_id_type=MESH) for t in range(len(g_refs))])

    return _Comm(tuple(gs), tuple(jax.ShapeDtypeStruct(g.shape[1:], g.dtype) for g in gs), len(gs), make)


def _spread_plan(parts):
    def make(p_refs, o_refs, send_sems, recv_sems):
        x, y, c = _place()
        copies = []
        for t in range(len(p_refs)):
            mine = o_refs[t].at[4 * x + 2 * y + c]
            copies.append(pltpu.make_async_copy(p_refs[t], mine, send_sems.at[7 * len(p_refs) + t]))
            for r in range(1, 8):
                peer = (1 - x if r & 4 else x, 1 - y if r & 2 else y, 1 - c if r & 1 else c)
                copies.append(pltpu.make_async_remote_copy(
                    src_ref=p_refs[t], dst_ref=mine, send_sem=send_sems.at[7 * t + r - 1],
                    recv_sem=recv_sems.at[7 * t + r - 1], device_id=peer, device_id_type=MESH))
        return _start_wait(copies)

    return _Comm(tuple(parts), tuple(jax.ShapeDtypeStruct((8,) + p.shape, p.dtype) for p in parts), 8 * len(parts),
                 make)


def _mm_nt_acc(name, a, w, tk, acc_in=None, epilogue=None, extras=(), extra_specs=(), extra_out_shapes=(),
               extra_out_specs=(), extra_scratch=(), comm=None, tb=TB):
    m, k = a.shape
    n = w.shape[1]
    nk, ni = k // tk, m // tb
    has_acc = acc_in is not None
    n_xc = len(comm.srcs) if comm else 0
    n_es = len(extra_scratch)

    def body(*refs):
        a_ref, w_ref = refs[0], refs[1]
        pos = 2
        acc_ref = None
        if has_acc:
            acc_ref = refs[pos]
            pos += 1
        ex = refs[pos:pos + len(extras)]
        pos += len(extras)
        xc_src = refs[pos:pos + n_xc]
        pos += n_xc
        n_scr = 1 + n_es + (2 if n_xc else 0)
        outs = refs[pos:len(refs) - n_scr - n_xc]
        xc_dst = refs[len(refs) - n_scr - n_xc:len(refs) - n_scr]
        scr = refs[len(refs) - n_scr]
        es = refs[len(refs) - n_scr + 1:len(refs) - n_scr + 1 + n_es]
        i, kk = pl.program_id(0), pl.program_id(1)
        if n_xc:
            begin, finish = comm.make(xc_src, xc_dst, refs[-2], refs[-1])
            pl.when((i == 0) & (kk == 0))(begin)

        @pl.when(kk == 0)
        def _():
            scr[...] = acc_ref[...] if has_acc else jnp.zeros_like(scr)

        scr[...] += _dot(a_ref[...], w_ref[...], NN)

        @pl.when(kk == nk - 1)
        def _():
            if epilogue is None:
                outs[0][...] = scr[...]
            else:
                epilogue(scr[...], outs, i, ni, *ex, *es)

        if n_xc:
            pl.when((i == ni - 1) & (kk == nk - 1))(finish)

    in_specs = [pl.BlockSpec((tb, tk), lambda i, kk: (i, kk)), pl.BlockSpec((tk, n), lambda i, kk: (kk, 0))]
    args = [a, w]
    if has_acc:
        in_specs.append(pl.BlockSpec((tb, n), lambda i, kk: (i, 0)))
        args.append(acc_in)
    in_specs += list(extra_specs) + [ANY] * n_xc
    args += list(extras) + (list(comm.srcs) if comm else [])
    if epilogue is None:
        out_shape = [jax.ShapeDtypeStruct((m, n), F32)]
        out_specs = [pl.BlockSpec((tb, n), lambda i, kk: (i, 0))]
    else:
        out_shape, out_specs = list(extra_out_shapes), list(extra_out_specs)
    scratch = [pltpu.VMEM((tb, n), F32)] + list(extra_scratch)
    if n_xc:
        out_shape += list(comm.out_shapes)
        out_specs += [ANY] * n_xc
        scratch += _comm_sems(comm)
    return pl.pallas_call(
        body, name=name, grid=(ni, nk), in_specs=in_specs, out_specs=out_specs, out_shape=out_shape,
        scratch_shapes=scratch, compiler_params=_cparams(2),
    )(*args)


def _rms_bwd_epilogue(du, outs, i, ni, h_ref, g_ref, dh1_ref, obuf, sems):
    dx_ref, dmeta_ref, dg_ref = outs
    h = h_ref[...]
    r = lax.rsqrt(jnp.mean(h * h, axis=-1, keepdims=True) + EPS)
    xh = h * r
    dxh = du * g_ref[...]
    dh0 = dh1_ref[...] + r * (dxh - xh * jnp.mean(dxh * xh, axis=-1, keepdims=True))

    def put(slot, tile):
        return pltpu.make_async_copy(obuf.at[slot], dx_ref.at[pl.ds(pl.multiple_of(tile * TB - T0, 8), TB)],
                                     sems.at[slot])

    @pl.when(i == 0)
    def _():
        dg_ref[...] = jnp.zeros_like(dg_ref)
        dmeta_ref[...] = dh0[PADF:T0, :]
        obuf[0] = dh0
        first = pltpu.make_async_copy(obuf.at[0, pl.ds(T0, TB - T0)], dx_ref.at[pl.ds(0, TB - T0)], sems.at[0])
        first.start()
        first.wait()

    @pl.when(i >= 1)
    def _():
        slot = i % 2

        @pl.when(i >= 3)
        def _():
            put(slot, i - 2).wait()

        obuf[slot] = dh0
        put(slot, i).start()

    dg_ref[...] += jnp.sum(du * xh, axis=0, keepdims=True)

    @pl.when(i == ni - 1)
    def _():
        for tile in (ni - 2, ni - 1):
            if tile >= 1:
                put(tile % 2, tile).wait()


def _mm_tn(name, a, b, bn, ncols=None, bcol0=0, into=None, col0=0, out_cols=None):
    t, m = a.shape
    n = ncols or b.shape[1]
    j0, bj0 = col0 // bn, bcol0 // bn

    def body(a_ref, b_ref, *rest):
        o_ref = rest[-1]

        @pl.when(pl.program_id(1) == 0)
        def _():
            o_ref[...] = jnp.zeros_like(o_ref)

        o_ref[...] += _dot(a_ref[...], b_ref[...], TN)

    in_specs = [pl.BlockSpec((TK, m), lambda j, kk: (kk, 0)), pl.BlockSpec((TK, bn), lambda j, kk: (kk, bj0 + j))]
    args = [a, b]
    aliases = {}
    if into is not None:
        in_specs.append(ANY)
        args.append(into)
        aliases = {2: 0}
        out_cols = into.shape[1]
    return pl.pallas_call(
        body, name=name, grid=(n // bn, t // TK), in_specs=in_specs,
        out_specs=pl.BlockSpec((m, bn), lambda j, kk: (0, j0 + j)),
        out_shape=jax.ShapeDtypeStruct((m, out_cols or n), F32), input_output_aliases=aliases,
        compiler_params=_cparams(2),
    )(*args)


def _place_merge_cols_call(dwp, dw_m, dw_glr):
    c0 = W_R + W_GP - 128
    tail = IN_PAD - c0
    rows = 256

    def body(m_ref, low, p_ref, o_ref, buf, sem):
        for r in range(0, D_MODEL, rows):
            buf[r:r + rows, :] = jnp.concatenate(
                [low[r:r + rows, :GATE_RANK], m_ref[r:r + rows, :],
                 jnp.zeros((rows, tail - GATE_RANK - W_M), F32)], axis=1)
        put = pltpu.make_async_copy(buf, o_ref.at[:, pl.ds(c0, tail)], sem)
        put.start()
        put.wait()

    return pl.pallas_call(
        body, name="place_merge_cols",
        in_specs=[pl.BlockSpec(memory_space=pltpu.VMEM), pl.BlockSpec(memory_space=pltpu.VMEM), ANY], out_specs=ANY,
        out_shape=jax.ShapeDtypeStruct(dwp.shape, F32), input_output_aliases={2: 0},
        scratch_shapes=[pltpu.VMEM((D_MODEL, tail), F32), pltpu.SemaphoreType.DMA],
        compiler_params=pltpu.CompilerParams(vmem_limit_bytes=VMEM_LIMIT),
    )(dw_m, dw_glr, dwp)


def _ret_fill_decay(lg_ref, dm_scr):
    c = TM
    ii = lax.broadcasted_iota(jnp.int32, (c, c), 0)
    jj = lax.broadcasted_iota(jnp.int32, (c, c), 1)
    rel = (ii - jj).astype(F32)
    for h in range(RET_HEADS):
        dm_scr[h] = jnp.where(rel >= 0, jnp.exp(jnp.maximum(rel, 0.0) * lg_ref[h]), 0.0)


def _ret_consts(lg, dm_ref):
    c = TM
    idx = lax.broadcasted_iota(jnp.int32, (c, 1), 0).astype(F32)
    xi = jnp.exp((idx + 1.0) * lg)
    zeta = jnp.exp((c - 1.0 - idx) * lg)
    gc = jnp.exp(jnp.full((1, 1), c, F32) * lg)
    return dm_ref[...], xi, zeta, gc


def _ret_fwd_call(rqk, rv, rg, gain, lgam):
    tp = rqk.shape[0]
    nc = tp // TM

    def body(lg_ref, qk_ref, v_ref, rg_ref, g_ref, o_ref, a_ref, st_ref, sc_ref, s_scr, dm_scr):
        @pl.when(pl.program_id(0) == 0)
        def _():
            s_scr[...] = jnp.zeros_like(s_scr)
            _ret_fill_decay(lg_ref, dm_scr)

        for h in range(RET_HEADS):
            dm, xi, zeta, gc = _ret_consts(lg_ref[h], dm_scr.at[h])
            q = qk_ref[:, h * RET_QK:(h + 1) * RET_QK]
            k = qk_ref[:, D_MODEL + h * RET_QK:D_MODEL + (h + 1) * RET_QK]
            v = v_ref[:, h * RET_V:(h + 1) * RET_V]
            sb = s_scr[h].astype(BF16)
            st_ref[0, h] = sb
            s = (_dot(q, k, NT) * dm).astype(BF16)
            sc_ref[0, h] = s
            o = _dot(s, v, NN) + xi * _dot(q, sb, NN)
            kz = (k.astype(F32) * zeta).astype(BF16)
            s_scr[h] = gc * s_scr[h] + _dot(kz, v, TN)
            o_ref[:, h * RET_V:(h + 1) * RET_V] = o
            mu = _head_mean(o)
            xc = o - mu
            xh = xc * lax.rsqrt(_head_mean(xc * xc) + EPS)
            a_ref[:, h * RET_V:(h + 1) * RET_V] = (
                xh * g_ref[:, h * RET_V:(h + 1) * RET_V] * _silu(rg_ref[:, h * RET_V:(h + 1) * RET_V])).astype(BF16)

    return pl.pallas_call(
        body, name="ret_fwd", grid=(nc,),
        in_specs=[pl.BlockSpec(memory_space=pltpu.SMEM),
                  pl.BlockSpec((TM, 2 * D_MODEL), lambda n: (n, 0)),
                  pl.BlockSpec((TM, RET_W), lambda n: (n, 0)),
                  pl.BlockSpec((TM, RET_W), lambda n: (n, 0)),
                  pl.BlockSpec((1, RET_W), lambda n: (0, 0))],
        out_specs=[pl.BlockSpec((TM, RET_W), lambda n: (n, 0)),
                   pl.BlockSpec((TM, RET_W), lambda n: (n, 0)),
                   pl.BlockSpec((1, RET_HEADS, RET_QK, RET_V), lambda n: (n, 0, 0, 0)),
                   pl.BlockSpec((1, RET_HEADS, TM, TM), lambda n: (n, 0, 0, 0))],
        out_shape=[jax.ShapeDtypeStruct((tp, RET_W), F32), jax.ShapeDtypeStruct((tp, RET_W), BF16),
                   jax.ShapeDtypeStruct((nc, RET_HEADS, RET_QK, RET_V), BF16),
                   jax.ShapeDtypeStruct((nc, RET_HEADS, TM, TM), BF16)],
        scratch_shapes=[pltpu.VMEM((RET_HEADS, RET_QK, RET_V), F32), pltpu.VMEM((RET_HEADS, TM, TM), F32)],
        compiler_params=_cparams(1),
    )(lgam, rqk, rv, rg, gain)


def _ret_bwd_call(rqk, rv, rg, o_ret, dpr, wbr, states, scores, gain, lgam, cos, sin):
    tp = rqk.shape[0]
    nc = tp // TM
    half = RET_QK // 2

    def body(lg_ref, qk_ref, v_ref, rg_ref, o_ref, dpr_ref, wbr_ref, st_ref, sc_ref, g_ref, cos_ref, sin_ref, dp_ref,
             dg_ref, ds_scr, dm_scr):
        @pl.when(pl.program_id(0) == 0)
        def _():
            ds_scr[...] = jnp.zeros_like(ds_scr)
            dg_ref[...] = jnp.zeros_like(dg_ref)
            _ret_fill_decay(lg_ref, dm_scr)

        cos, sin = cos_ref[...], sin_ref[...]
        for h in range(RET_HEADS):
            hs = slice(h * RET_V, (h + 1) * RET_V)
            dm, xi, zeta, gc = _ret_consts(lg_ref[h], dm_scr.at[h])
            o = o_ref[:, hs]
            mu = _head_mean(o)
            xc = o - mu
            rstd = lax.rsqrt(_head_mean(xc * xc) + EPS)
            xh = xc * rstd
            gain_h = g_ref[:, hs]
            g = rg_ref[:, hs]
            sg = _sigmoid(g)
            silu = g * sg
            dah = _dot(dpr_ref[...], wbr_ref[hs, :], NT)
            dp_ref[:, 4 * D_MODEL + h * RET_V:4 * D_MODEL + (h + 1) * RET_V] = (
                dah * (xh * gain_h) * (sg * (1.0 + g * (1.0 - sg)))).astype(BF16)
            dn = dah * silu
            dg_ref[:, hs] += jnp.sum(dn * xh, axis=0, keepdims=True)
            dxh = dn * gain_h
            do = rstd * (dxh - _head_mean(dxh) - xh * _head_mean(dxh * xh))
            dob = do.astype(BF16)
            q = qk_ref[:, h * RET_QK:(h + 1) * RET_QK]
            k = qk_ref[:, D_MODEL + h * RET_QK:D_MODEL + (h + 1) * RET_QK]
            v = v_ref[:, hs]
            sp = st_ref[0, h]
            ds = ds_scr[h]
            dsb = ds.astype(BF16)
            s = sc_ref[0, h]
            dsc = (_dot(dob, v, NT) * dm).astype(BF16)
            dq = _dot(dsc, k, NN) + xi * _dot(dob, sp, NT)
            dk = _dot(dsc, q, TN) + zeta * _dot(v, dsb, NT)
            kz = (k.astype(F32) * zeta).astype(BF16)
            dv = _dot(s, dob, TN) + _dot(kz, dsb, NN)
            qx = (q.astype(F32) * xi).astype(BF16)
            ds_scr[h] = gc * ds + _dot(qx, dob, TN)
            dp_ref[:, 2 * D_MODEL + h * RET_V:2 * D_MODEL + (h + 1) * RET_V] = dv.astype(BF16)
            dk = dk * (RET_QK ** -0.5)
            for base, t in ((0, dq), (D_MODEL, dk)):
                t1, t2 = t[:, :half], t[:, half:]
                dp_ref[:, base + h * RET_QK:base + h * RET_QK + half] = (t1 * cos + t2 * sin).astype(BF16)
                dp_ref[:, base + h * RET_QK + half:base + (h + 1) * RET_QK] = (t2 * cos - t1 * sin).astype(BF16)

    rev = lambda n: (nc - 1 - n, 0)
    return pl.pallas_call(
        body, name="ret_bwd", grid=(nc,),
        in_specs=[pl.BlockSpec(memory_space=pltpu.SMEM),
                  pl.BlockSpec((TM, 2 * D_MODEL), rev),
                  pl.BlockSpec((TM, RET_W), rev),
                  pl.BlockSpec((TM, RET_W), rev),
                  pl.BlockSpec((TM, RET_W), rev),
                  pl.BlockSpec((TM, D_MODEL), rev),
                  pl.BlockSpec((RET_W, D_MODEL), lambda n: (0, 0)),
                  pl.BlockSpec((1, RET_HEADS, RET_QK, RET_V), lambda n: (nc - 1 - n, 0, 0, 0)),
                  pl.BlockSpec((1, RET_HEADS, TM, TM), lambda n: (nc - 1 - n, 0, 0, 0)),
                  pl.BlockSpec((1, RET_W), lambda n: (0, 0)),
                  pl.BlockSpec((TM, half), rev),
                  pl.BlockSpec((TM, half), rev)],
        out_specs=[pl.BlockSpec((TM, W_R), rev), pl.BlockSpec((1, RET_W), lambda n: (0, 0))],
        out_shape=[jax.ShapeDtypeStruct((tp, W_R), BF16), jax.ShapeDtypeStruct((1, RET_W), F32)],
        scratch_shapes=[pltpu.VMEM((RET_HEADS, RET_QK, RET_V), F32), pltpu.VMEM((RET_HEADS, TM, TM), F32)],
        compiler_params=_cparams(1),
    )(lgam, rqk, rv, rg, o_ret, dpr, wbr, states, scores, gain, cos, sin)


GLA_LEVELS = tuple(GC >> (s + 1) for s in range(int(math.log2(GC // GLA_SUB))))
NLEV = len(GLA_LEVELS)


def _gla_tril():
    return np.tril(np.ones((GC, GC), np.float32))


def _gla_masks():
    ii = lax.broadcasted_iota(jnp.int32, (GC, GC), 0)
    jj = lax.broadcasted_iota(jnp.int32, (GC, GC), 1)
    masks = []
    for m in GLA_LEVELS:
        sh = int(math.log2(2 * m))
        masks.append(((ii >> sh) == (jj >> sh)) & ((ii & m) != 0) & ((jj & m) == 0))
    sh = int(math.log2(GLA_SUB))
    md = ((ii >> sh) == (jj >> sh)) & (jj <= ii)
    row = lax.broadcasted_iota(jnp.int32, (GC, 1), 0)
    second = [(row & m) != 0 for m in GLA_LEVELS]
    return masks, md, second


def _gla_gate_call(u, w_g, wg, bg, pmat):
    tp = u.shape[0]
    gb = _proj_rows(tp)
    assert gb % GC == 0

    def body(u_ref, w_ref, wg_ref, bg_ref, p_ref, glr_ref, z_ref, b_ref):
        glr = _dot(u_ref[...], w_ref[...], NT)
        glr_ref[...] = glr
        z = _dot(glr.astype(BF16), wg_ref[...], NN) + bg_ref[...]
        z_ref[...] = z
        la = (jnp.minimum(z, 0.0) - jnp.log1p(jnp.exp(-jnp.abs(z)))) * (1.0 / GATE_TAU)
        for r in range(0, gb, GC):
            b_ref[r:r + GC, :] = _exact_pm(p_ref[...], la[r:r + GC, :])

    tile = pl.BlockSpec((gb, GLA_KW), lambda i: (i, 0))
    return pl.pallas_call(
        body, name="gla_gate", grid=(tp // gb,),
        in_specs=[pl.BlockSpec((gb, D_MODEL), lambda i: (i, 0)),
                  pl.BlockSpec((128, D_MODEL), lambda i: ((W_GP - 128) // 128, 0)),
                  pl.BlockSpec((128, GLA_KW), lambda i: (0, 0)),
                  pl.BlockSpec((1, GLA_KW), lambda i: (0, 0)), pl.BlockSpec((GC, GC), lambda i: (0, 0))],
        out_specs=[pl.BlockSpec((gb, 128), lambda i: (i, 0)), tile, tile],
        out_shape=[jax.ShapeDtypeStruct((tp, 128), F32), jax.ShapeDtypeStruct((tp, GLA_KW), F32),
                   jax.ShapeDtypeStruct((tp, GLA_KW), F32)],
        compiler_params=_cparams(1),
    )(u, w_g, wg, bg, pmat)


def _gla_gate_bwd_call(db, z, glr, u, wg, pmat_t, d_g):
    tp = db.shape[0]
    gb = _proj_rows(tp)
    assert gb % GC == 0 and (W_GP - 128) % 128 == 0

    def body(db_ref, z_ref, glr_ref, u_ref, wg_ref, pt_ref, dgin_ref, dg_ref, dwg_ref, dbg_ref, dwl_ref):
        i = pl.program_id(0)

        @pl.when(i == 0)
        def _():
            dwg_ref[...] = jnp.zeros_like(dwg_ref)
            dbg_ref[...] = jnp.zeros_like(dbg_ref)
            dwl_ref[...] = jnp.zeros_like(dwl_ref)

        dla = jnp.concatenate([_exact_pm(pt_ref[...], db_ref[r:r + GC, :]) for r in range(0, gb, GC)], axis=0)
        row = i * gb + lax.broadcasted_iota(jnp.int32, (gb, 1), 0)
        dz = jnp.where(row >= PADF, dla * (1.0 / GATE_TAU) * _sigmoid(-z_ref[...]), 0.0)
        dzb = dz.astype(BF16)
        dglr = _dot(dzb, wg_ref[...], NT).astype(BF16)
        dg_ref[...] = dglr
        dwg_ref[...] += _dot(glr_ref[...].astype(BF16), dzb, TN)
        dbg_ref[...] += jnp.sum(dz, axis=0, keepdims=True)
        dwl_ref[...] += _dot(u_ref[...], dglr, TN)

    tile = pl.BlockSpec((gb, GLA_KW), lambda i: (i, 0))
    const = lambda i: (0, 0)
    return pl.pallas_call(
        body, name="gla_gate_bwd", grid=(tp // gb,),
        in_specs=[tile, tile, pl.BlockSpec((gb, 128), lambda i: (i, 0)), pl.BlockSpec((gb, D_MODEL), lambda i: (i, 0)),
                  pl.BlockSpec((128, GLA_KW), const), pl.BlockSpec((GC, GC), const), ANY],
        out_specs=[pl.BlockSpec((gb, 128), lambda i: (i, (W_GP - 128) // 128)), pl.BlockSpec((128, GLA_KW), const),
                   pl.BlockSpec((1, GLA_KW), const), pl.BlockSpec((D_MODEL, 128), const)],
        out_shape=[jax.ShapeDtypeStruct(d_g.shape, BF16), jax.ShapeDtypeStruct((128, GLA_KW), F32),
                   jax.ShapeDtypeStruct((1, GLA_KW), F32), jax.ShapeDtypeStruct((D_MODEL, 128), F32)],
        input_output_aliases={6: 0}, compiler_params=_cparams(1),
    )(db, z, glr, u, wg, pmat_t, d_g)


def _gla_row_steps(b_ref, cs, rows, size):
    parts = [jnp.zeros((size, GLA_K), F32) if r is None else jnp.broadcast_to(b_ref[r:r + 1, cs], (size, GLA_K))
             for r in rows]
    return parts[0] if len(parts) == 1 else jnp.concatenate(parts, axis=0)


def _gla_factors(b_ref, h, second):
    cs = slice(h * GLA_K, (h + 1) * GLA_K)
    b = b_ref[:, cs]
    fq, fk = [], []
    for l, m in enumerate(GLA_LEVELS):
        d = b - _gla_row_steps(b_ref, cs, [s + m - 1 for s in range(0, GC, 2 * m)], 2 * m)
        f = jnp.exp(jnp.where(second[l], d, -d))
        fq.append(jnp.where(second[l], f, 0.0))
        fk.append(jnp.where(second[l], 0.0, f))
    dd = b - _gla_row_steps(b_ref, cs, [None] + [s - 1 for s in range(GLA_SUB, GC, GLA_SUB)], GLA_SUB)
    ed = jnp.exp(dd)
    edi = jnp.exp(-dd)
    eb = jnp.exp(b)
    bl = b_ref[GC - 1:GC, cs]
    ee = jnp.exp(bl - b)
    ebl = jnp.exp(bl)
    return fq, fk, ed, edi, eb, ee, ebl


def _gla_scaled(q, k, fq, fk, ed, edi):
    qt = [(q * f).astype(BF16) for f in fq]
    kt = [(k * f).astype(BF16) for f in fk]
    return qt, kt, (q * ed).astype(BF16), (k * edi).astype(BF16)


def _gla_scores(qt, kt, qd, kd, masks, md):
    a = jnp.where(md, _dot(qd, kd, NT), 0.0)
    for l in range(NLEV):
        a = a + jnp.where(masks[l], _dot(qt[l], kt[l], NT), 0.0)
    return a.astype(BF16)


def _gla_fwd_call(gqk, gv, b, gg, gain, comm=None):
    tp = gqk.shape[0]
    nc = tp // GC
    ns = nc // GS
    n_xc = len(comm.srcs) if comm else 0

    def body(qk_ref, v_ref, b_ref, gg_ref, g_ref, *rest):
        xc_src = rest[:n_xc]
        o_ref, a_ref, st_ref, am_ref = rest[n_xc:n_xc + 4]
        xc_dst = rest[n_xc + 4:2 * n_xc + 4]
        s_scr = rest[2 * n_xc + 4]
        n = pl.program_id(0)
        if n_xc:
            begin, finish = comm.make(xc_src, xc_dst, rest[-2], rest[-1])
            pl.when(n == 0)(begin)
            pl.when(n == ns - 1)(finish)

        @pl.when(n == 0)
        def _():
            s_scr[...] = jnp.zeros_like(s_scr)

        masks, md, second = _gla_masks()
        for cc in range(GS):
            rows = pl.ds(cc * GC, GC)
            qk_c, v_c, b_c, gg_c, o_c, a_c = (r.at[rows] for r in (qk_ref, v_ref, b_ref, gg_ref, o_ref, a_ref))
            for h in range(GLA_HEADS):
                q = qk_c[:, h * GLA_K:(h + 1) * GLA_K]
                k = qk_c[:, GLA_KW + h * GLA_K:GLA_KW + (h + 1) * GLA_K]
                vs = slice(h * GLA_V, (h + 1) * GLA_V)
                v = v_c[:, vs]
                fq, fk, ed, edi, eb, ee, ebl = _gla_factors(b_c, h, second)
                a = _gla_scores(*_gla_scaled(q, k, fq, fk, ed, edi), masks, md)
                am_ref[cc, h] = a
                sb = s_scr[h].astype(BF16)
                st_ref[cc, h] = sb
                o = _dot(a, v, NN) + _dot((q * eb).astype(BF16), sb, NT)
                s_scr[h] = s_scr[h] * ebl + _dot(v, (k * ee).astype(BF16), TN)
                o_c[:, vs] = o
                xh = o * lax.rsqrt(_head_mean(o * o) + EPS)
                a_c[:, vs] = (xh * g_ref[:, vs] * _silu(gg_c[:, vs])).astype(BF16)

    return pl.pallas_call(
        body, name="gla_fwd", grid=(ns,),
        in_specs=[pl.BlockSpec((GS * GC, 2 * GLA_KW), lambda n: (n, 0)),
                  pl.BlockSpec((GS * GC, GLA_W), lambda n: (n, 0)),
                  pl.BlockSpec((GS * GC, GLA_KW), lambda n: (n, 0)),
                  pl.BlockSpec((GS * GC, GLA_W), lambda n: (n, 0)),
                  pl.BlockSpec((1, GLA_W), lambda n: (0, 0))] + [ANY] * n_xc,
        out_specs=[pl.BlockSpec((GS * GC, GLA_W), lambda n: (n, 0)),
                   pl.BlockSpec((GS * GC, GLA_W), lambda n: (n, 0)),
                   pl.BlockSpec((GS, GLA_HEADS, GLA_V, GLA_K), lambda n: (n, 0, 0, 0)),
                   pl.BlockSpec((GS, GLA_HEADS, GC, GC), lambda n: (n, 0, 0, 0))] + [ANY] * n_xc,
        out_shape=[jax.ShapeDtypeStruct((tp, GLA_W), F32), jax.ShapeDtypeStruct((tp, GLA_W), BF16),
                   jax.ShapeDtypeStruct((nc, GLA_HEADS, GLA_V, GLA_K), BF16),
                   jax.ShapeDtypeStruct((nc, GLA_HEADS, GC, GC), BF16)] + (list(comm.out_shapes) if comm else []),
        scratch_shapes=[pltpu.VMEM((GLA_HEADS, GLA_V, GLA_K), F32)] + (_comm_sems(comm) if comm else []),
        compiler_params=_cparams(1),
    )(gqk, gv, b, gg, gain, *(comm.srcs if comm else ()))


def _gla_bwd_call(gqk, gv, b, gg, o_gla, da, states, scores, gain, comm=None):
    tp = gqk.shape[0]
    nc = tp // GC
    ns = nc // GS
    o_gv, o_gg = 2 * GLA_KW, 2 * GLA_KW + GLA_W
    n_xc = len(comm.srcs) if comm else 0

    def body(qk_all, v_all, b_all, gg_all, o_all, da_all, st_ref, am_ref, g_ref, *rest):
        xc_src = rest[:n_xc]
        dp_all, db_all, dg_ref = rest[n_xc:n_xc + 3]
        xc_dst = rest[n_xc + 3:2 * n_xc + 3]
        ds_scr = rest[2 * n_xc + 3]
        n = pl.program_id(0)
        if n_xc:
            begin, finish = comm.make(xc_src, xc_dst, rest[-2], rest[-1])
            pl.when(n == 0)(begin)
            pl.when(n == ns - 1)(finish)

        @pl.when(n == 0)
        def _():
            ds_scr[...] = jnp.zeros_like(ds_scr)
            dg_ref[...] = jnp.zeros_like(dg_ref)

        masks, md, second = _gla_masks()
        for cc, h in [(cc, h) for cc in reversed(range(GS)) for h in range(GLA_HEADS)]:
            rows = pl.ds(cc * GC, GC)
            qk_ref, v_ref, b_scr, gg_ref, o_ref, da_ref, dp_ref, db_scr = (
                r.at[rows] for r in (qk_all, v_all, b_all, gg_all, o_all, da_all, dp_all, db_all))
            cs = slice(h * GLA_K, (h + 1) * GLA_K)
            vs = slice(h * GLA_V, (h + 1) * GLA_V)
            o = o_ref[:, vs]
            rstd = lax.rsqrt(_head_mean(o * o) + EPS)
            xh = o * rstd
            gain_h = g_ref[:, vs]
            g = gg_ref[:, vs]
            sg = _sigmoid(g)
            dah = da_ref[:, vs]
            dp_ref[:, o_gg + h * GLA_V:o_gg + (h + 1) * GLA_V] = (
                dah * (xh * gain_h) * (sg * (1.0 + g * (1.0 - sg)))).astype(BF16)
            dn = dah * (g * sg)
            dg_ref[:, vs] += jnp.sum(dn * xh, axis=0, keepdims=True)
            dxh = dn * gain_h
            do = rstd * (dxh - xh * _head_mean(dxh * xh))
            dob = do.astype(BF16)
            q = qk_ref[:, cs]
            k = qk_ref[:, GLA_KW + h * GLA_K:GLA_KW + (h + 1) * GLA_K]
            v = v_ref[:, vs]
            fq, fk, ed, edi, eb, ee, ebl = _gla_factors(b_scr, h, second)
            qt, kt, qd, kd = _gla_scaled(q, k, fq, fk, ed, edi)
            sp = st_ref[cc, h]
            ds = ds_scr[h]
            dsb = ds.astype(BF16)
            q_in = q * eb
            k_end = k * ee
            da_s = _dot(dob, v, NT)
            dv = _dot(am_ref[cc, h], dob, TN) + _dot(k_end.astype(BF16), dsb, NT)
            dq_in = _dot(dob, sp, NN)
            dk_end = _dot(v, dsb, NN)
            dbl = jnp.sum(sp.astype(F32) * ds, axis=0, keepdims=True) * ebl
            ds_scr[h] = ds * ebl + _dot(dob, q_in.astype(BF16), TN)
            dq = dq_in * eb
            dk = dk_end * ee
            de_end = dk_end * k_end
            db = dq_in * q_in - de_end
            placed = [(GC - 1, jnp.sum(de_end, axis=0, keepdims=True) + dbl)]
            for l, m in enumerate(GLA_LEVELS):
                dal = jnp.where(masks[l], da_s, 0.0).astype(BF16)
                dqt = _dot(dal, kt[l], NN)
                dkt = _dot(dal, qt[l], TN)
                dq = dq + dqt * fq[l]
                dk = dk + dkt * fk[l]
                gl = dqt * (q * fq[l]) - dkt * (k * fk[l])
                db = db + gl
                placed += [(s + m - 1, -jnp.sum(gl[s:s + 2 * m], axis=0, keepdims=True)) for s in range(0, GC, 2 * m)]
            dad = jnp.where(md, da_s, 0.0).astype(BF16)
            dqd = _dot(dad, kd, NN)
            dkd = _dot(dad, qd, TN)
            dq = dq + dqd * ed
            dk = dk + dkd * edi
            gd = dqd * (q * ed) - dkd * (k * edi)
            db = db + gd
            placed += [(s - 1, -jnp.sum(gd[s:s + GLA_SUB], axis=0, keepdims=True)) for s in range(GLA_SUB, GC, GLA_SUB)]
            db_scr[:, cs] = db
            for r, val in placed:
                db_scr[r:r + 1, cs] += val
            dp_ref[:, cs] = (dq * (GLA_K ** -0.5)).astype(BF16)
            dp_ref[:, GLA_KW + h * GLA_K:GLA_KW + (h + 1) * GLA_K] = dk.astype(BF16)
            dp_ref[:, o_gv + h * GLA_V:o_gv + (h + 1) * GLA_V] = dv.astype(BF16)

    rev = lambda n: (ns - 1 - n, 0)
    const = lambda n: (0, 0)
    xc_shapes, xc_sems = (list(comm.out_shapes), _comm_sems(comm)) if n_xc else ([], [])
    return pl.pallas_call(
        body, name="gla_bwd", grid=(ns,),
        in_specs=[pl.BlockSpec((GS * GC, 2 * GLA_KW), rev),
                  pl.BlockSpec((GS * GC, GLA_W), rev),
                  pl.BlockSpec((GS * GC, GLA_KW), rev),
                  pl.BlockSpec((GS * GC, GLA_W), rev),
                  pl.BlockSpec((GS * GC, GLA_W), rev),
                  pl.BlockSpec((GS * GC, GLA_W), rev),
                  pl.BlockSpec((GS, GLA_HEADS, GLA_V, GLA_K), lambda n: (ns - 1 - n, 0, 0, 0)),
                  pl.BlockSpec((GS, GLA_HEADS, GC, GC), lambda n: (ns - 1 - n, 0, 0, 0)),
                  pl.BlockSpec((1, GLA_W), const)] + [ANY] * n_xc,
        out_specs=[pl.BlockSpec((GS * GC, W_GP), rev), pl.BlockSpec((GS * GC, GLA_KW), rev),
                   pl.BlockSpec((1, GLA_W), const)] + [ANY] * n_xc,
        out_shape=[jax.ShapeDtypeStruct((tp, W_GP), BF16), jax.ShapeDtypeStruct((tp, GLA_KW), F32),
                   jax.ShapeDtypeStruct((1, GLA_W), F32)] + xc_shapes,
        scratch_shapes=[pltpu.VMEM((GLA_HEADS, GLA_V, GLA_K), F32)] + xc_sems,
        compiler_params=_cparams(1),
    )(gqk, gv, b, gg, o_gla, da, states, scores, gain, *(comm.srcs if comm else ()))


def _mid_call(a_ret, a_gla, mg, h0, tgt, wbr, wbg, wout, gf):
    tp = h0.shape[0]
    nt = tp // TM

    def body(ar_ref, ag_ref, mg_ref, h_ref, t_ref, wbr_ref, wbg_ref, wo_ref, gf_ref,
             dh1_ref, dag_ref, dm_ref, mb_ref, dh1b_ref, dprb_ref, dpgb_ref, loss_ref, dgf_ref):
        i = pl.program_id(0)

        @pl.when(i == 0)
        def _():
            loss_ref[...] = jnp.zeros_like(loss_ref)
            dgf_ref[...] = jnp.zeros_like(dgf_ref)

        ar, ag = ar_ref[...], ag_ref[...]
        pr = _dot(ar, wbr_ref[...], NN)
        pg = _dot(ag, wbg_ref[...], NN)
        sr = _sigmoid(mg_ref[:, :D_MODEL])
        sg = _sigmoid(mg_ref[:, D_MODEL:])
        merged = (sr * pr + sg * pg).astype(BF16)
        mb_ref[...] = merged
        h1 = h_ref[...] + _dot(merged, wo_ref[...], NN)
        r1 = lax.rsqrt(jnp.mean(h1 * h1, axis=-1, keepdims=True) + EPS)
        xh = h1 * r1
        gfv = gf_ref[...]
        live = jnp.where(i > 0, 1.0, 0.0).astype(F32)
        err = (xh * gfv - t_ref[...]) * live
        loss_ref[...] += jnp.full(loss_ref.shape, 0.5 / D_MODEL, F32) * jnp.sum(err * err)
        dy = err * (1.0 / D_MODEL)
        dgf_ref[...] += jnp.sum(dy * xh, axis=0, keepdims=True)
        dxh = dy * gfv
        dh1 = r1 * (dxh - xh * jnp.mean(dxh * xh, axis=-1, keepdims=True))
        dh1_ref[...] = dh1
        dh1b = dh1.astype(BF16)
        dh1b_ref[...] = dh1b
        dmerged = _dot(dh1b, wo_ref[...], NT)
        dm_ref[:, :D_MODEL] = (dmerged * pr * sr * (1.0 - sr)).astype(BF16)
        dm_ref[:, D_MODEL:] = (dmerged * pg * sg * (1.0 - sg)).astype(BF16)
        dpr = (dmerged * sr).astype(BF16)
        dpg = (dmerged * sg).astype(BF16)
        dprb_ref[...] = dpr
        dpgb_ref[...] = dpg
        dag_ref[...] = _dot(dpg, wbg_ref[...], NT)

    tile = lambda w: pl.BlockSpec((TM, w), lambda i: (i, 0))
    const = lambda r, w: pl.BlockSpec((r, w), lambda i: (0, 0))
    return pl.pallas_call(
        body, name="merge_out_loss", grid=(nt,),
        in_specs=[tile(RET_W), tile(GLA_W), tile(W_M), tile(D_MODEL),
                  pl.BlockSpec((TM, D_MODEL), lambda i: (jnp.maximum(i - 1, 0), 0)),
                  const(RET_W, D_MODEL), const(GLA_W, D_MODEL), const(D_MODEL, D_MODEL), const(1, D_MODEL)],
        out_specs=[tile(D_MODEL), tile(GLA_W), tile(W_M), tile(D_MODEL), tile(D_MODEL), tile(D_MODEL),
                   tile(D_MODEL), const(1, 128), const(1, D_MODEL)],
        out_shape=[jax.ShapeDtypeStruct((tp, D_MODEL), F32), jax.ShapeDtypeStruct((tp, GLA_W), F32),
                   jax.ShapeDtypeStruct((tp, W_M), BF16),
                   jax.ShapeDtypeStruct((tp, D_MODEL), BF16), jax.ShapeDtypeStruct((tp, D_MODEL), BF16),
                   jax.ShapeDtypeStruct((tp, D_MODEL), BF16), jax.ShapeDtypeStruct((tp, D_MODEL), BF16),
                   jax.ShapeDtypeStruct((1, 128), F32), jax.ShapeDtypeStruct((1, D_MODEL), F32)],
        compiler_params=_cparams(1),
    )(a_ret, a_gla, mg, h0, tgt, wbr, wbg, wout, gf)


def _device_step(x2d, tgt2d, meta, norm_gain, w_in_part, w_gate_up, b_gate, ret_gain, gla_gain, branch_parts,
                 final_gain, ck):
    seq = x2d.shape[0]
    tp = T0 + seq
    head = jnp.concatenate([jnp.zeros((PADF, D_MODEL), F32), meta], axis=0)
    wg_pad = jnp.pad(w_gate_up, ((0, 128 - GATE_RANK), (0, 0))).astype(BF16)

    half = RET_QK // 2
    cos, sin = (jnp.asarray(t) for t in _rope_tables(tp))
    lgam = jnp.log1p(-(2.0 ** (-5.0 - jnp.arange(RET_HEADS, dtype=F32))))
    pmat = jnp.asarray(_gla_tril(), BF16)
    pmat_t = jnp.asarray(_gla_tril().T.copy(), BF16)

    h0, u, g_in = _rms_call(x2d, head, norm_gain, _gather_plan([w_in_part], relay=(True,)))
    sw, hc = w_in_part.shape
    w_in_t = g_in.reshape(4, 2, sw, hc).transpose(0, 2, 1, 3).reshape(4 * sw, 2 * hc)
    w_r = w_in_t
    w_g = jnp.pad(w_in_t[W_R:W_R + W_G], ((0, W_GP - W_G), (0, 0)))
    w_m = w_in_t[W_R + W_G:]
    tab = pl.BlockSpec((_proj_rows(tp), half), lambda j, i: (i, 0))
    rqk = _mm_nn("proj_rqk", u, w_r, BF16, D_MODEL, 0, 2 * D_MODEL, _rope_epilogue, (cos, sin), (tab, tab))
    rv = _mm_nn("proj_rv", u, w_r, BF16, RET_W, 2 * D_MODEL, RET_W)
    rg = _mm_nn("proj_rg", u, w_r, F32, RET_W, 4 * D_MODEL, RET_W)
    gqk = _mm_nn("proj_gqk", u, w_g, F32, 2 * GLA_KW, 0, 2 * GLA_KW, _gqk_epilogue)
    gv = _mm_nn("proj_gv", u, w_g, BF16, GLA_W, 2 * GLA_KW, GLA_W)
    gg = _mm_nn("proj_gg", u, w_g, F32, GLA_W, 2 * GLA_KW + GLA_W, GLA_W)
    mg = _mm_nn("proj_mg", u, w_m, F32, W_M, 0, W_M)

    o_ret, a_ret, st_ret, sc_ret = _ret_fwd_call(rqk, rv, rg, ret_gain, lgam)
    glr, z_gate, b_dec = _gla_gate_call(u, w_g, wg_pad, b_gate, pmat)
    o_gla, a_gla, st_gla, sc_gla, g_br, g_bg, g_out = _gla_fwd_call(gqk, gv, b_dec, gg, gla_gain,
                                                                    comm=_spread_plan(branch_parts))
    wbr = g_br.reshape(RET_W, D_MODEL)
    wbg = g_bg.reshape(GLA_W, D_MODEL)
    wout = g_out.reshape(D_MODEL, D_MODEL)

    gf = final_gain.reshape(1, D_MODEL)
    (dh1, da_gla, dm, merged_b, dh1_b, dpr_b, dpg_b, loss, dgf) = _mid_call(
        a_ret, a_gla, mg, h0, tgt2d, wbr, wbg, wout, gf)

    names_b = ("w_branch_ret", "w_branch_gla", "w_out")
    g2_b = [_mm_tn("dw_br", a_ret, dpr_b, D_MODEL).reshape(4, 2, RET_W // 8, D_MODEL).transpose(1, 0, 2, 3),
            _mm_tn("dw_bg", a_gla, dpg_b, D_MODEL).reshape(4, 2, GLA_W // 8, D_MODEL).transpose(1, 0, 2, 3),
            _mm_tn("dw_out", merged_b, dh1_b, D_MODEL).reshape(4, 2, D_MODEL // 8, D_MODEL).transpose(1, 0, 2, 3)]
    sib_b = _swap_halves_call("swap_halves_branch", g2_b)
    sum_b = [_add_half_call("add_half_" + nm, g, b, ck) for nm, g, b in zip(names_b, g2_b, sib_b)]
    d_g, db_dec, dgla_gain, *chips_b = _gla_bwd_call(gqk, gv, b_dec, gg, o_gla, da_gla, st_gla, sc_gla, gla_gain,
                                                     comm=_exchange_plan(sum_b))
    d_g, dwg, dbg, dw_glr = _gla_gate_bwd_call(db_dec, z_gate, glr, u, wg_pad, pmat_t, d_g)
    mine = [_add_chips_call("add_chips_" + nm, g, b, p, ck) for nm, g, b, p in zip(names_b, g2_b, sib_b, chips_b)]

    d_r, dret_gain = _ret_bwd_call(rqk, rv, rg, o_ret, dpr_b, wbr, st_ret, sc_ret, ret_gain, lgam, cos, sin)

    dwp = _mm_tn("dw_r", u, d_r, 3 * D_MODEL, out_cols=IN_PAD)
    dwp = _mm_tn("dw_g", u, d_g, 3 * D_MODEL, ncols=W_GP - 128, into=dwp, col0=W_R)
    g2_in = _place_merge_cols_call(dwp, _mm_tn("dw_m", u, dm, 2 * D_MODEL), dw_glr).reshape(2, D_MODEL // 2, IN_PAD)

    du, sib_in = _mm_nt_acc("du_g", d_g, w_g, W_GP, comm=_swap_plan([g2_in]), tb=_proj_rows(tp))
    sum_in = _add_rows_call("add_half_w_in", g2_in, sib_in, ck)
    du, chips_in = _mm_nt_acc("du_r", d_r, w_r, 2 * D_MODEL, acc_in=du, comm=_exchange_window_plan(sum_in),
                              tb=_proj_rows(tp))
    tile = pl.BlockSpec((TB, D_MODEL), lambda i, kk: (i, 0))
    row = pl.BlockSpec((1, D_MODEL), lambda i, kk: (0, 0))
    dx, dmeta, dnorm_gain = _mm_nt_acc(
        "du_m", dm, w_m, W_M, acc_in=du, epilogue=_rms_bwd_epilogue, extras=(h0, norm_gain, dh1),
        extra_specs=(tile, row, tile),
        extra_out_shapes=(jax.ShapeDtypeStruct((seq, D_MODEL), F32), jax.ShapeDtypeStruct((N_META, D_MODEL), F32),
                          jax.ShapeDtypeStruct((1, D_MODEL), F32)),
        extra_out_specs=(ANY, pl.BlockSpec((N_META, D_MODEL), lambda i, kk: (0, 0)), row),
        extra_scratch=(pltpu.VMEM((2, TB, D_MODEL), F32), pltpu.SemaphoreType.DMA((2,))))
    small = dict(norm_gain=dnorm_gain, b_gate=dbg, ret_norm_gain=dret_gain, gla_norm_gain=dgla_gain,
                 final_norm_gain=dgf, w_gate_up=dwg[:GATE_RANK], meta_tokens=dmeta, loss=loss[0, 0])
    rows = -(-sum(sz for _, sz in SMALL) // 128 // 8) * 8
    mine_in, g_small = _add_window_call("add_chips_w_in", g2_in, sib_in, chips_in, ck,
                                        _gather_plan([_pack_rows([small[nm] for nm, _ in SMALL], rows)]))
    full = _join_halves_call("join_halves", [mine_in] + mine)

    return dict(dx=dx, small=g_small, w_in=full[0], w_branch_ret=full[1], w_branch_gla=full[2], w_out=full[3])


MESH = pl.DeviceIdType.MESH
ANY = pl.BlockSpec(memory_space=pl.ANY)


def _place():
    return lax.axis_index("x"), lax.axis_index("y"), lax.axis_index("c")


def _gather8_call(name, parts):
    comm = _gather_plan(parts)
    n = len(parts)

    def body(*refs):
        begin, finish = comm.make(refs[:n], refs[n:2 * n], refs[-2], refs[-1])
        begin()
        finish()

    return pl.pallas_call(
        body, name=name, out_shape=list(comm.out_shapes), in_specs=[ANY] * n, out_specs=[ANY] * n,
        scratch_shapes=_comm_sems(comm),
    )(*parts)


def _swap_halves_call(name, gs):
    n = len(gs)

    def body(*refs):
        g_refs, b_refs = refs[:n], refs[n:2 * n]
        send_sems, recv_sems = refs[2 * n:]
        x, y, c = _place()
        copies = [pltpu.make_async_remote_copy(
            src_ref=g_refs[t].at[1 - c], dst_ref=b_refs[t], send_sem=send_sems.at[t], recv_sem=recv_sems.at[t],
            device_id=(x, y, 1 - c), device_id_type=MESH) for t in range(n)]
        for cp in copies:
            cp.start()
        for cp in copies:
            cp.wait()

    return pl.pallas_call(
        body, name=name,
        out_shape=[jax.ShapeDtypeStruct(g.shape[1:], g.dtype) for g in gs],
        in_specs=[ANY] * n, out_specs=[ANY] * n,
        scratch_shapes=[pltpu.SemaphoreType.DMA((n,)), pltpu.SemaphoreType.DMA((n,))],
    )(*gs)


def _join_halves_call(name, ts):
    n = len(ts)

    def body(*refs):
        o_refs = refs[n:2 * n]
        send_sems, recv_sems = refs[2 * n:]
        x, y, c = _place()
        copies = [pltpu.make_async_remote_copy(
            src_ref=o_refs[t].at[c], dst_ref=o_refs[t].at[c], send_sem=send_sems.at[t], recv_sem=recv_sems.at[t],
            device_id=(x, y, 1 - c), device_id_type=MESH) for t in range(n)]
        for cp in copies:
            cp.start()
        for t in range(n):
            copies[t].wait_send()
            pltpu.make_async_remote_copy(
                src_ref=o_refs[t].at[c], dst_ref=o_refs[t].at[1 - c], send_sem=send_sems.at[t],
                recv_sem=recv_sems.at[t], device_id=(x, y, 1 - c), device_id_type=MESH).wait_recv()

    return pl.pallas_call(
        body, name=name,
        out_shape=[jax.ShapeDtypeStruct(t.shape, t.dtype) for t in ts],
        in_specs=[ANY] * n, out_specs=[ANY] * n, input_output_aliases={t: t for t in range(n)},
        scratch_shapes=[pltpu.SemaphoreType.DMA((n,)), pltpu.SemaphoreType.DMA((n,))],
    )(*ts)


def _row_block(rows, cols, budget):
    best = 8
    for rb in range(8, rows + 1, 8):
        if rows % rb == 0 and rb * cols * 4 <= budget:
            best = rb
    return best


def _add_half_call(name, g, b, ck):
    _, _, r, cc = g.shape
    rb = _row_block(r, cc, 2 * 1024 * 1024)

    def body(ck_ref, g_ref, b_ref, o_ref):
        o_ref[...] = (g_ref[...] + b_ref[...]).astype(BF16)

    return pl.pallas_call(
        body, name=name,
        grid_spec=pltpu.PrefetchScalarGridSpec(
            num_scalar_prefetch=1, grid=(4, r // rb),
            in_specs=[pl.BlockSpec((None, None, rb, cc), lambda k, i, ck_ref: (ck_ref[0], k, i, 0)),
                      pl.BlockSpec((None, rb, cc), lambda k, i, ck_ref: (k, i, 0))],
            out_specs=pl.BlockSpec((None, rb, cc), lambda k, i, ck_ref: (k, i, 0))),
        out_shape=jax.ShapeDtypeStruct(b.shape, BF16),
        compiler_params=_cparams(2),
    )(ck, g, b)


def _add_rows_call(name, g, b, ck):
    _, r, cc = g.shape
    rb = _row_block(r, cc, 2 * 1024 * 1024)

    def body(ck_ref, g_ref, b_ref, o_ref):
        o_ref[...] = (g_ref[...] + b_ref[...]).astype(BF16)

    return pl.pallas_call(
        body, name=name,
        grid_spec=pltpu.PrefetchScalarGridSpec(
            num_scalar_prefetch=1, grid=(r // rb,),
            in_specs=[pl.BlockSpec((None, rb, cc), lambda i, ck_ref: (ck_ref[0], i, 0)),
                      pl.BlockSpec((rb, cc), lambda i, ck_ref: (i, 0))],
            out_specs=pl.BlockSpec((rb, cc), lambda i, ck_ref: (i, 0))),
        out_shape=jax.ShapeDtypeStruct((r, cc), BF16),
        compiler_params=_cparams(1),
    )(ck, g, b)


def _add_window_call(name, g, b, p, ck, comm):
    _, r, _ = g.shape
    nb, step = WIN_W // 128, WIN_STEP // 128
    n_xc = len(comm.srcs)

    def body(ck_ref, g_ref, b_ref, p0_ref, p1_ref, p2_ref, *rest):
        o_ref = rest[n_xc]
        i = pl.program_id(0)
        begin, finish = comm.make(rest[:n_xc], rest[n_xc + 1:2 * n_xc + 1], rest[-2], rest[-1])
        pl.when(i == 0)(begin)
        own = g_ref[...] + b_ref[...]
        o_ref[...] = ((own + p0_ref[...].astype(F32)) + p1_ref[...].astype(F32)) + p2_ref[...].astype(F32)
        pl.when(i == nb - 1)(finish)

    def peer(j):
        return pl.BlockSpec((None, r, 128), lambda i, ck_ref: (j, 0, i))

    return pl.pallas_call(
        body, name=name,
        grid_spec=pltpu.PrefetchScalarGridSpec(
            num_scalar_prefetch=1, grid=(nb,),
            in_specs=[pl.BlockSpec((None, r, 128), lambda i, ck_ref: (ck_ref[0], 0, step * ck_ref[1] + i)),
                      pl.BlockSpec((r, 128), lambda i, ck_ref: (0, step * ck_ref[1] + i)),
                      peer(0), peer(1), peer(2)] + [ANY] * n_xc,
            out_specs=[pl.BlockSpec((None, r, 128), lambda i, ck_ref: (ck_ref[0], 0, i))] + [ANY] * n_xc,
            scratch_shapes=_comm_sems(comm)),
        out_shape=[jax.ShapeDtypeStruct((2, r, WIN_W), F32)] + list(comm.out_shapes),
        compiler_params=_cparams(1),
    )(ck, g, b, p, p, p, *comm.srcs)


def _add_chips_call(name, g, b, p, ck):
    _, _, r, cc = g.shape
    rb = _row_block(r, cc, 2 * 1024 * 1024)

    def body(ck_ref, g_ref, b_ref, p0_ref, p1_ref, p2_ref, o_ref):
        own = g_ref[...] + b_ref[...]
        o_ref[...] = ((own + p0_ref[...].astype(F32)) + p1_ref[...].astype(F32)) + p2_ref[...].astype(F32)

    def peer(j):
        return pl.BlockSpec((None, rb, cc), lambda i, ck_ref: (j, i, 0))

    return pl.pallas_call(
        body, name=name,
        grid_spec=pltpu.PrefetchScalarGridSpec(
            num_scalar_prefetch=1, grid=(r // rb,),
            in_specs=[pl.BlockSpec((None, None, rb, cc), lambda i, ck_ref: (ck_ref[0], ck_ref[1], i, 0)),
                      pl.BlockSpec((None, rb, cc), lambda i, ck_ref: (ck_ref[1], i, 0)),
                      peer(0), peer(1), peer(2)],
            out_specs=pl.BlockSpec((None, rb, cc), lambda i, ck_ref: (ck_ref[0], i, 0))),
        out_shape=jax.ShapeDtypeStruct((2, r, cc), F32),
        compiler_params=_cparams(1),
    )(ck, g, b, p, p, p)


def _sum8_call(name, g):
    def body(g_ref, o_ref):
        acc = g_ref[0]
        for d in range(1, 8):
            acc = acc + g_ref[d]
        o_ref[...] = acc

    return pl.pallas_call(body, name=name, out_shape=jax.ShapeDtypeStruct(g.shape[1:], F32))(g)


def _adamw_call(name, w, g, m, v):
    r, cc = w.shape
    if r % 8 == 0 or r * cc * 4 <= 1024 * 1024:
        rb = _row_block(r, cc, 1024 * 1024) if r % 8 == 0 else r
        grid, spec = (r // rb,), pl.BlockSpec((rb, cc), lambda i: (i, 0))
    else:
        grid, spec = (cc // 128,), pl.BlockSpec((r, 128), lambda i: (0, i))

    def body(w_ref, g_ref, m_ref, v_ref, d_ref, m2_ref, v2_ref):
        _adamw_update(g_ref[...], w_ref, m_ref, v_ref, d_ref, m2_ref, v2_ref)

    return pl.pallas_call(
        body, name=name, grid=grid, in_specs=[spec] * 4, out_specs=[spec] * 3,
        out_shape=[jax.ShapeDtypeStruct((r, cc), F32)] * 3, compiler_params=_cparams(1),
    )(w, g, m, v)


def _adamw_update(gv, w_ref, m_ref, v_ref, d_ref, m2_ref, v2_ref):
    m2 = ADAM_B1 * m_ref[...] + (1.0 - ADAM_B1) * gv
    v2 = ADAM_B2 * v_ref[...] + (1.0 - ADAM_B2) * (gv * gv)
    m_hat = m2 / (1.0 - ADAM_B1 ** ADAM_STEP)
    v_hat = v2 / (1.0 - ADAM_B2 ** ADAM_STEP)
    d_ref[...] = -ADAM_LR * (m_hat / (jnp.sqrt(v_hat) + ADAM_EPS) + ADAM_WD * w_ref[...])
    m2_ref[...] = m2
    v2_ref[...] = v2


def _adamw_window_call(name, w, f, m, v, lane0):
    r, s = w.shape
    wl = f.shape[1]
    rb = 128

    def body(l0_ref, w_ref, f_ref, m_ref, v_ref, g_ref, d_ref, m2_ref, v2_ref):
        gv = pltpu.roll(f_ref[...], lax.rem(wl - l0_ref[0], wl), axis=1)[:, :s]
        g_ref[...] = gv
        _adamw_update(gv, w_ref, m_ref, v_ref, d_ref, m2_ref, v2_ref)

    spec = pl.BlockSpec((rb, s), lambda i, l0: (i, 0))
    return pl.pallas_call(
        body, name=name,
        grid_spec=pltpu.PrefetchScalarGridSpec(
            num_scalar_prefetch=1, grid=(r // rb,),
            in_specs=[spec, pl.BlockSpec((rb, wl), lambda i, l0: (i, 0)), spec, spec], out_specs=[spec] * 4),
        out_shape=[jax.ShapeDtypeStruct((r, s), F32)] * 4, compiler_params=_cparams(1),
    )(lane0, w, f, m, v)


SMALL = (("norm_gain", D_MODEL), ("b_gate", GLA_KW), ("ret_norm_gain", RET_W), ("gla_norm_gain", GLA_W),
         ("final_norm_gain", D_MODEL), ("w_gate_up", GATE_RANK * GLA_KW), ("meta_tokens", N_META * D_MODEL),
         ("loss", 1))


def _pack_rows(vecs, rows):
    flat = jnp.concatenate([v.reshape(-1) for v in vecs])
    return jnp.pad(flat, (0, rows * 128 - flat.shape[0])).reshape(rows, 128)


def kernel(x, meta_tokens, norm_gain, w_in, w_gate_up, b_gate, ret_norm_gain, gla_norm_gain, w_branch_ret, w_branch_gla, w_out, final_norm_gain, loss_target, m_meta_tokens, m_norm_gain, m_w_in, m_w_gate_up, m_b_gate, m_ret_norm_gain, m_gla_norm_gain, m_w_branch_ret, m_w_branch_gla, m_w_out, m_final_norm_gain, v_meta_tokens, v_norm_gain, v_w_in, v_w_gate_up, v_b_gate, v_ret_norm_gain, v_gla_norm_gain, v_w_branch_ret, v_w_branch_gla, v_w_out, v_final_norm_gain):
    xi, yi, ci = _place()
    kme = 2 * xi + yi
    ck = jnp.stack([ci, kme]).astype(jnp.int32)
    sw_in = w_in.shape[2]

    def my_half(a, dtype):
        r, cc = a.shape
        return lax.dynamic_index_in_dim(a.reshape(2, r // 2, cc), ci, 0, keepdims=False).astype(dtype)

    g_meta, g_wg = _gather8_call("gather_small_weights", [my_half(meta_tokens, F32), my_half(w_gate_up[0], F32)])
    branch_parts = [my_half(w_branch_ret[0], BF16), my_half(w_branch_gla[0], BF16), my_half(w_out[0], BF16)]
    meta = g_meta.reshape(4, 2, N_META // 2, D_MODEL // 4).transpose(1, 2, 0, 3).reshape(N_META, D_MODEL)
    wg_full = g_wg.reshape(4, 2, GATE_RANK // 2, GLA_KW // 4).transpose(1, 2, 0, 3).reshape(GATE_RANK, GLA_KW)

    w_in_part = lax.dynamic_slice_in_dim(w_in[0].T, ci * (D_MODEL // 2), D_MODEL // 2, axis=1).astype(BF16)
    loc = _device_step(x[0], loss_target[0], meta, norm_gain, w_in_part, wg_full, b_gate, ret_norm_gain,
                       gla_norm_gain,
                       branch_parts, final_norm_gain, ck)
    names = ("w_in", "w_branch_ret", "w_branch_gla", "w_out")
    full = [loc[nm] for nm in names]
    big_w = dict(w_in=w_in[0], w_branch_ret=w_branch_ret[0], w_branch_gla=w_branch_gla[0], w_out=w_out[0])
    big_m = dict(w_in=m_w_in[0], w_branch_ret=m_w_branch_ret[0], w_branch_gla=m_w_branch_gla[0], w_out=m_w_out[0])
    big_v = dict(w_in=v_w_in[0], w_branch_ret=v_w_branch_ret[0], w_branch_gla=v_w_branch_gla[0], w_out=v_w_out[0])
    grads, deltas, new_m, new_v = {}, {}, {}, {}
    for nm, f in zip(names, full):
        shape = big_w[nm].shape
        if nm == "w_in":
            lane0 = ((sw_in - WIN_STEP) * kme).astype(jnp.int32).reshape(1)
            g, d, m2, v2 = _adamw_window_call(
                "adamw_" + nm, big_w[nm], f.reshape(shape[0], WIN_W), big_m[nm], big_v[nm], lane0)
        else:
            g = f.reshape(shape)
            d, m2, v2 = _adamw_call("adamw_" + nm, big_w[nm], g, big_m[nm], big_v[nm])
        grads[nm], deltas[nm], new_m[nm], new_v[nm] = (a.reshape((1,) + shape) for a in (g, d, m2, v2))

    tot = _sum8_call("sum_small_grads", loc["small"]).reshape(-1)
    off = 0
    sg = {}
    for nm, sz in SMALL:
        sg[nm] = tot[off:off + sz]
        off += sz
    loss = sg.pop("loss")[0]
    sg["w_gate_up"] = lax.dynamic_slice_in_dim(sg["w_gate_up"].reshape(GATE_RANK, GLA_KW), kme * (GLA_KW // 4),
                                               GLA_KW // 4, axis=1)
    sg["meta_tokens"] = lax.dynamic_slice_in_dim(sg["meta_tokens"].reshape(N_META, D_MODEL), kme * (D_MODEL // 4),
                                                 D_MODEL // 4, axis=1)
    small_w = dict(norm_gain=norm_gain, b_gate=b_gate, ret_norm_gain=ret_norm_gain, gla_norm_gain=gla_norm_gain,
                   final_norm_gain=final_norm_gain, w_gate_up=w_gate_up, meta_tokens=meta_tokens)
    small_m = dict(norm_gain=m_norm_gain, b_gate=m_b_gate, ret_norm_gain=m_ret_norm_gain,
                   gla_norm_gain=m_gla_norm_gain, final_norm_gain=m_final_norm_gain, w_gate_up=m_w_gate_up,
                   meta_tokens=m_meta_tokens)
    small_v = dict(norm_gain=v_norm_gain, b_gate=v_b_gate, ret_norm_gain=v_ret_norm_gain,
                   gla_norm_gain=v_gla_norm_gain, final_norm_gain=v_final_norm_gain, w_gate_up=v_w_gate_up,
                   meta_tokens=v_meta_tokens)
    for nm in small_w:
        shape = small_w[nm].shape
        as2d = lambda a: a.reshape((-1, shape[-1]))
        grads[nm] = sg[nm].reshape(shape)
        deltas[nm], new_m[nm], new_v[nm] = (a.reshape(shape) for a in _adamw_call(
            "adamw_" + nm, as2d(small_w[nm]), as2d(sg[nm]), as2d(small_m[nm]), as2d(small_v[nm])))

    out_order = ("meta_tokens", "norm_gain", "w_in", "w_gate_up", "b_gate", "ret_norm_gain", "gla_norm_gain",
                 "w_branch_ret", "w_branch_gla", "w_out", "final_norm_gain")
    dx = loc["dx"].reshape(x.shape)
    return (loss, dx, *[grads[nm] for nm in out_order], *[deltas[nm] for nm in out_order],
            *[new_m[nm] for nm in out_order], *[new_v[nm] for nm in out_order])
```

```python
import math
from typing import Callable, NamedTuple

import numpy as np
import jax
import jax.numpy as jnp
from jax import lax
from jax.experimental import pallas as pl
from jax.experimental.pallas import tpu as pltpu

F32 = jnp.float32
BF16 = jnp.bfloat16

D_MODEL = 1024
N_META = 16
EPS = 1e-6
ROPE_BASE = 10000.0
RET_HEADS, RET_QK, RET_V = 4, 256, 512
RET_W = RET_HEADS * RET_V
GLA_HEADS, GLA_K, GLA_V = 4, 128, 256
GLA_W = GLA_HEADS * GLA_V
GLA_KW = GLA_HEADS * GLA_K
GATE_RANK = 16
GATE_TAU = 16.0
GLA_SUB = 16

TM = 256
T0 = TM
PADF = T0 - N_META
GC = 128
GS = 3
TB = 768
TK = 768

W_R = 6144
W_G = 3088
W_GP = 3200
W_M = 2048
IN_COLS = W_R + W_G + W_M
WIN_STEP = (IN_COLS // 4) // 128 * 128
WIN_W = -(-(3 * (IN_COLS // 4 - WIN_STEP) + IN_COLS // 4) // 128) * 128
IN_PAD = 3 * WIN_STEP + WIN_W

ADAM_LR, ADAM_B1, ADAM_B2, ADAM_EPS, ADAM_WD, ADAM_STEP = 0.001, 0.9, 0.999, 1e-08, 0.01, 10

VMEM_LIMIT = 56 * 1024 * 1024

NN = ((1,), (0,))
NT = ((1,), (1,))
TN = ((0,), (0,))


def _dot(a, b, dims):
    return lax.dot_general(a, b, (dims, ((), ())), preferred_element_type=F32)


def _cparams(n_axes):
    return pltpu.CompilerParams(dimension_semantics=("arbitrary",) * n_axes, vmem_limit_bytes=VMEM_LIMIT)


def _sigmoid(x):
    return 0.5 * jnp.tanh(0.5 * x) + 0.5


def _silu(x):
    h = 0.5 * x
    return h + h * jnp.tanh(h)


def _head_mean(x):
    return jnp.mean(x, axis=-1, keepdims=True)


def _split3(x):
    hi = x.astype(BF16)
    r1 = x - hi.astype(F32)
    mid = r1.astype(BF16)
    lo = (r1 - mid.astype(F32)).astype(BF16)
    return hi, mid, lo


def _exact_pm(p, x):
    hi, mid, lo = _split3(x)
    return _dot(p, hi, NN) + _dot(p, mid, NN) + _dot(p, lo, NN)


def _rms_call(x2d, head, gain, comm):
    tp = T0 + x2d.shape[0]
    nt = tp // TM
    n_xc = len(comm.srcs)

    def body(x_ref, hd_ref, g_ref, *rest):
        xc_src = rest[:n_xc]
        h_ref, u_ref = rest[n_xc:n_xc + 2]
        xc_dst = rest[n_xc + 2:2 * n_xc + 2]
        i = pl.program_id(0)
        begin, finish = comm.make(xc_src, xc_dst, rest[-2], rest[-1])
        pl.when(i == 0)(begin)
        h = jnp.where(i == 0, hd_ref[...], x_ref[...])
        h_ref[...] = h
        r = lax.rsqrt(jnp.mean(h * h, axis=-1, keepdims=True) + EPS)
        u_ref[...] = (h * r * g_ref[...]).astype(BF16)
        pl.when(i == nt - 1)(finish)

    tile = pl.BlockSpec((TM, D_MODEL), lambda i: (i, 0))
    return pl.pallas_call(
        body, name="rms_in", grid=(nt,),
        in_specs=[pl.BlockSpec((TM, D_MODEL), lambda i: (jnp.maximum(i - 1, 0), 0)),
                  pl.BlockSpec((T0, D_MODEL), lambda i: (0, 0)), pl.BlockSpec((1, D_MODEL), lambda i: (0, 0))]
        + [ANY] * n_xc,
        out_specs=[tile, tile] + [ANY] * n_xc,
        out_shape=[jax.ShapeDtypeStruct((tp, D_MODEL), F32), jax.ShapeDtypeStruct((tp, D_MODEL), BF16)]
        + list(comm.out_shapes),
        scratch_shapes=_comm_sems(comm), compiler_params=_cparams(1),
    )(x2d, head, gain, *comm.srcs)


PROJ_ROWS_MAX = 1408


def _proj_rows(m):
    return max(r for r in range(16, PROJ_ROWS_MAX + 1, 16) if m % r == 0)


def _mm_nn(name, a, bt, out_dtype, tn, col0, ncols, epilogue=None, extras=(), extra_specs=()):
    m, k = a.shape
    nj, j0 = ncols // tn, col0 // tn
    tb = _proj_rows(m)

    def body(a_ref, b_ref, *rest):
        *ex, o_ref = rest
        acc = _dot(a_ref[...], b_ref[...], NT)
        if epilogue is None:
            o_ref[...] = acc.astype(out_dtype)
        else:
            epilogue(acc, o_ref, *ex)

    return pl.pallas_call(
        body, name=name, grid=(nj, m // tb),
        in_specs=[pl.BlockSpec((tb, k), lambda j, i: (i, 0)), pl.BlockSpec((tn, k), lambda j, i: (j0 + j, 0))]
        + list(extra_specs),
        out_specs=pl.BlockSpec((tb, tn), lambda j, i: (i, j)),
        out_shape=jax.ShapeDtypeStruct((m, ncols), out_dtype),
        compiler_params=_cparams(2),
    )(a, bt, *extras)


def _rope_tables(tp):
    half = RET_QK // 2
    pos = np.arange(tp, dtype=np.float32) - np.float32(PADF)
    inv = (ROPE_BASE ** (-np.arange(half, dtype=np.float64) / half)).astype(np.float32)
    ang = (pos[:, None] * inv[None, :]).astype(np.float64)
    return np.cos(ang).astype(np.float32), np.sin(ang).astype(np.float32)


def _rope_epilogue(acc, o_ref, cos_ref, sin_ref):
    scale = jnp.where(pl.program_id(0) == 1, RET_QK ** -0.5, 1.0).astype(F32)
    cos, sin = cos_ref[...], sin_ref[...]
    half = RET_QK // 2
    for h in range(RET_HEADS):
        t1 = acc[:, h * RET_QK:h * RET_QK + half]
        t2 = acc[:, h * RET_QK + half:(h + 1) * RET_QK]
        o_ref[:, h * RET_QK:h * RET_QK + half] = ((t1 * cos - t2 * sin) * scale).astype(BF16)
        o_ref[:, h * RET_QK + half:(h + 1) * RET_QK] = ((t2 * cos + t1 * sin) * scale).astype(BF16)


def _gqk_epilogue(acc, o_ref):
    o_ref[:, :GLA_KW] = acc[:, :GLA_KW] * (GLA_K ** -0.5)
    o_ref[:, GLA_KW:] = acc[:, GLA_KW:]


class _Comm(NamedTuple):
    srcs: tuple
    out_shapes: tuple
    n_sems: int
    make: Callable


def _comm_sems(comm):
    return [pltpu.SemaphoreType.DMA((comm.n_sems,)), pltpu.SemaphoreType.DMA((comm.n_sems,))]


def _start_wait(copies):
    def begin():
        for cp in copies:
            cp.start()

    def finish():
        for cp in copies:
            cp.wait()

    return begin, finish


def _other_chips(x, y):
    return [(1 - x, y), (x, 1 - y), (1 - x, 1 - y)]


def _gather_plan(parts, relay=()):
    n = len(parts)
    relay = tuple(relay) + (False,) * (n - len(relay))

    def make(x_refs, out_refs, send_sems, recv_sems):
        x, y, c = _place()
        me, sibling = (x, y, c), (x, y, 1 - c)
        xn, yn, dg = (1 - x, y), (x, 1 - y), (1 - x, 1 - y)

        def slot(t, px, py, pc, half=None):
            ref = out_refs[t].at[4 * px + 2 * py + pc]
            if half is None:
                return ref
            cols = ref.shape[-1] // 2
            return ref.at[:, pl.ds(half * cols, cols)]

        def copy(t, k, dst, to, src=None):
            return pltpu.make_async_remote_copy(
                src_ref=dst if src is None else src, dst_ref=dst, send_sem=send_sems.at[8 * t + k],
                recv_sem=recv_sems.at[8 * t + k], device_id=to, device_id_type=MESH)

        mine = [pltpu.make_async_copy(x_refs[t], slot(t, *me), send_sems.at[8 * n + t]) for t in range(n)]
        sent = []
        for t in range(n):
            sent.append(copy(t, 0, slot(t, *me), sibling, src=x_refs[t]))
            sent.append(copy(t, 1, slot(t, *me), (*xn, c), src=x_refs[t]))
            sent.append(copy(t, 2, slot(t, *me), (*yn, c), src=x_refs[t]))
            if not relay[t]:
                sent.append(copy(t, 3, slot(t, *me), (*dg, c), src=x_refs[t]))

        def begin():
            for cp in mine + sent:
                cp.start()

        def finish():
            later = []

            def start(cp):
                cp.start()
                later.append(cp)

            for t in range(n):
                copy(t, 2, slot(t, *yn, c), me).wait_recv()
                if relay[t]:
                    start(copy(t, 3, slot(t, *yn, c, half=0), (*xn, c)))
                start(copy(t, 6, slot(t, *yn, c), sibling))
            for t in range(n):
                copy(t, 1, slot(t, *xn, c), me).wait_recv()
                if relay[t]:
                    start(copy(t, 4, slot(t, *xn, c, half=1), (*yn, c)))
                start(copy(t, 5, slot(t, *xn, c), sibling))
            for t in range(n):
                if relay[t]:
                    copy(t, 3, slot(t, *dg, c, half=0), me).wait_recv()
                    copy(t, 4, slot(t, *dg, c, half=1), me).wait_recv()
                else:
                    copy(t, 3, slot(t, *dg, c), me).wait_recv()
                start(copy(t, 7, slot(t, *dg, c), sibling))
            for t in range(n):
                copy(t, 0, slot(t, *sibling), me).wait_recv()
                copy(t, 5, slot(t, *xn, 1 - c), me).wait_recv()
                copy(t, 6, slot(t, *yn, 1 - c), me).wait_recv()
                copy(t, 7, slot(t, *dg, 1 - c), me).wait_recv()
            for cp in sent + later:
                cp.wait_send()
            for cp in mine:
                cp.wait()

        return begin, finish

    return _Comm(tuple(parts), tuple(jax.ShapeDtypeStruct((8,) + p.shape, p.dtype) for p in parts), 9 * n, make)


def _exchange_plan(ss):
    def make(s_refs, b_refs, send_sems, recv_sems):
        x, y, c = _place()
        return _start_wait([pltpu.make_async_remote_copy(
            src_ref=s_refs[t].at[2 * chip[0] + chip[1]], dst_ref=b_refs[t].at[j], send_sem=send_sems.at[3 * t + j],
            recv_sem=recv_sems.at[3 * t + j], device_id=(*chip, c), device_id_type=MESH)
            for t in range(len(s_refs)) for j, chip in enumerate(_other_chips(x, y))])

    return _Comm(tuple(ss), tuple(jax.ShapeDtypeStruct((3,) + s.shape[1:], s.dtype) for s in ss), 3 * len(ss), make)


def _exchange_window_plan(s):
    def make(s_refs, b_refs, send_sems, recv_sems):
        x, y, c = _place()
        return _start_wait([pltpu.make_async_remote_copy(
            src_ref=s_refs[0].at[:, pl.ds(pl.multiple_of((2 * chip[0] + chip[1]) * WIN_STEP, 128), WIN_W)],
            dst_ref=b_refs[0].at[j], send_sem=send_sems.at[j], recv_sem=recv_sems.at[j], device_id=(*chip, c),
            device_id_type=MESH) for j, chip in enumerate(_other_chips(x, y))])

    return _Comm((s,), (jax.ShapeDtypeStruct((3, s.shape[0], WIN_W), s.dtype),), 3, make)


def _swap_plan(gs):
    def make(g_refs, b_refs, send_sems, recv_sems):
        x, y, c = _place()
        return _start_wait([pltpu.make_async_remote_copy(
            src_ref=g_refs[t].at[1 - c], dst_ref=b_refs[t], send_sem=send_sems.at[t], recv_sem=recv_sems.at[t],
            device_id=(x, y, 1 - c), device_id_type=MESH) for t in range(len(g_refs))])

    return _Comm(tuple(gs), tuple(jax.ShapeDtypeStruct(g.shape[1:], g.dtype) for g in gs), len(gs), make)


def _spread_plan(parts):
    def make(p_refs, o_refs, send_sems, recv_sems):
        x, y, c = _place()
        copies = []
        for t in range(len(p_refs)):
            mine = o_refs[t].at[4 * x + 2 * y + c]
            copies.append(pltpu.make_async_copy(p_refs[t], mine, send_sems.at[7 * len(p_refs) + t]))
            for r in range(1, 8):
                peer = (1 - x if r & 4 else x, 1 - y if r & 2 else y, 1 - c if r & 1 else c)
                copies.append(pltpu.make_async_remote_copy(
                    src_ref=p_refs[t], dst_ref=mine, send_sem=send_sems.at[7 * t + r - 1],
                    recv_sem=recv_sems.at[7 * t + r - 1], device_id=peer, device_id_type=MESH))
        return _start_wait(copies)

    return _Comm(tuple(parts), tuple(jax.ShapeDtypeStruct((8,) + p.shape, p.dtype) for p in parts), 8 * len(parts),
                 make)


def _mm_nt_acc(name, a, w, tk, acc_in=None, epilogue=None, extras=(), extra_specs=(), extra_out_shapes=(),
               extra_out_specs=(), extra_scratch=(), comm=None, tb=TB):
    m, k = a.shape
    n = w.shape[1]
    nk, ni = k // tk, m // tb
    has_acc = acc_in is not None
    n_xc = len(comm.srcs) if comm else 0
    n_es = len(extra_scratch)

    def body(*refs):
        a_ref, w_ref = refs[0], refs[1]
        pos = 2
        acc_ref = None
        if has_acc:
            acc_ref = refs[pos]
            pos += 1
        ex = refs[pos:pos + len(extras)]
        pos += len(extras)
        xc_src = refs[pos:pos + n_xc]
        pos += n_xc
        n_scr = 1 + n_es + (2 if n_xc else 0)
        outs = refs[pos:len(refs) - n_scr - n_xc]
        xc_dst = refs[len(refs) - n_scr - n_xc:len(refs) - n_scr]
        scr = refs[len(refs) - n_scr]
        es = refs[len(refs) - n_scr + 1:len(refs) - n_scr + 1 + n_es]
        i, kk = pl.program_id(0), pl.program_id(1)
        if n_xc:
            begin, finish = comm.make(xc_src, xc_dst, refs[-2], refs[-1])
            pl.when((i == 0) & (kk == 0))(begin)

        @pl.when(kk == 0)
        def _():
            scr[...] = acc_ref[...] if has_acc else jnp.zeros_like(scr)

        scr[...] += _dot(a_ref[...], w_ref[...], NN)

        @pl.when(kk == nk - 1)
        def _():
            if epilogue is None:
                outs[0][...] = scr[...]
            else:
                epilogue(scr[...], outs, i, ni, *ex, *es)

        if n_xc:
            pl.when((i == ni - 1) & (kk == nk - 1))(finish)

    in_specs = [pl.BlockSpec((tb, tk), lambda i, kk: (i, kk)), pl.BlockSpec((tk, n), lambda i, kk: (kk, 0))]
    args = [a, w]
    if has_acc:
        in_specs.append(pl.BlockSpec((tb, n), lambda i, kk: (i, 0)))
        args.append(acc_in)
    in_specs += list(extra_specs) + [ANY] * n_xc
    args += list(extras) + (list(comm.srcs) if comm else [])
    if epilogue is None:
        out_shape = [jax.ShapeDtypeStruct((m, n), F32)]
        out_specs = [pl.BlockSpec((tb, n), lambda i, kk: (i, 0))]
    else:
        out_shape, out_specs = list(extra_out_shapes), list(extra_out_specs)
    scratch = [pltpu.VMEM((tb, n), F32)] + list(extra_scratch)
    if n_xc:
        out_shape += list(comm.out_shapes)
        out_specs += [ANY] * n_xc
        scratch += _comm_sems(comm)
    return pl.pallas_call(
        body, name=name, grid=(ni, nk), in_specs=in_specs, out_specs=out_specs, out_shape=out_shape,
        scratch_shapes=scratch, compiler_params=_cparams(2),
    )(*args)


def _rms_bwd_epilogue(du, outs, i, ni, h_ref, g_ref, dh1_ref, obuf, sems):
    dx_ref, dmeta_ref, dg_ref = outs
    h = h_ref[...]
    r = lax.rsqrt(jnp.mean(h * h, axis=-1, keepdims=True) + EPS)
    xh = h * r
    dxh = du * g_ref[...]
    dh0 = dh1_ref[...] + r * (dxh - xh * jnp.mean(dxh * xh, axis=-1, keepdims=True))

    def put(slot, tile):
        return pltpu.make_async_copy(obuf.at[slot], dx_ref.at[pl.ds(pl.multiple_of(tile * TB - T0, 8), TB)],
                                     sems.at[slot])

    @pl.when(i == 0)
    def _():
        dg_ref[...] = jnp.zeros_like(dg_ref)
        dmeta_ref[...] = dh0[PADF:T0, :]
        obuf[0] = dh0
        first = pltpu.make_async_copy(obuf.at[0, pl.ds(T0, TB - T0)], dx_ref.at[pl.ds(0, TB - T0)], sems.at[0])
        first.start()
        first.wait()

    @pl.when(i >= 1)
    def _():
        slot = i % 2

        @pl.when(i >= 3)
        def _():
            put(slot, i - 2).wait()

        obuf[slot] = dh0
        put(slot, i).start()

    dg_ref[...] += jnp.sum(du * xh, axis=0, keepdims=True)

    @pl.when(i == ni - 1)
    def _():
        for tile in (ni - 2, ni - 1):
            if tile >= 1:
                put(tile % 2, tile).wait()


def _mm_tn(name, a, b, bn, ncols=None, bcol0=0, into=None, col0=0, out_cols=None):
    t, m = a.shape
    n = ncols or b.shape[1]
    j0, bj0 = col0 // bn, bcol0 // bn

    def body(a_ref, b_ref, *rest):
        o_ref = rest[-1]

        @pl.when(pl.program_id(1) == 0)
        def _():
            o_ref[...] = jnp.zeros_like(o_ref)

        o_ref[...] += _dot(a_ref[...], b_ref[...], TN)

    in_specs = [pl.BlockSpec((TK, m), lambda j, kk: (kk, 0)), pl.BlockSpec((TK, bn), lambda j, kk: (kk, bj0 + j))]
    args = [a, b]
    aliases = {}
    if into is not None:
        in_specs.append(ANY)
        args.append(into)
        aliases = {2: 0}
        out_cols = into.shape[1]
    return pl.pallas_call(
        body, name=name, grid=(n // bn, t // TK), in_specs=in_specs,
        out_specs=pl.BlockSpec((m, bn), lambda j, kk: (0, j0 + j)),
        out_shape=jax.ShapeDtypeStruct((m, out_cols or n), F32), input_output_aliases=aliases,
        compiler_params=_cparams(2),
    )(*args)


def _place_merge_cols_call(dwp, dw_m, dw_glr):
    c0 = W_R + W_GP - 128
    tail = IN_PAD - c0
    rows = 256

    def body(m_ref, low, p_ref, o_ref, buf, sem):
        for r in range(0, D_MODEL, rows):
            buf[r:r + rows, :] = jnp.concatenate(
                [low[r:r + rows, :GATE_RANK], m_ref[r:r + rows, :],
                 jnp.zeros((rows, tail - GATE_RANK - W_M), F32)], axis=1)
        put = pltpu.make_async_copy(buf, o_ref.at[:, pl.ds(c0, tail)], sem)
        put.start()
        put.wait()

    return pl.pallas_call(
        body, name="place_merge_cols",
        in_specs=[pl.BlockSpec(memory_space=pltpu.VMEM), pl.BlockSpec(memory_space=pltpu.VMEM), ANY], out_specs=ANY,
        out_shape=jax.ShapeDtypeStruct(dwp.shape, F32), input_output_aliases={2: 0},
        scratch_shapes=[pltpu.VMEM((D_MODEL, tail), F32), pltpu.SemaphoreType.DMA],
        compiler_params=pltpu.CompilerParams(vmem_limit_bytes=VMEM_LIMIT),
    )(dw_m, dw_glr, dwp)


def _ret_fill_decay(lg_ref, dm_scr):
    c = TM
    ii = lax.broadcasted_iota(jnp.int32, (c, c), 0)
    jj = lax.broadcasted_iota(jnp.int32, (c, c), 1)
    rel = (ii - jj).astype(F32)
    for h in range(RET_HEADS):
        dm_scr[h] = jnp.where(rel >= 0, jnp.exp(jnp.maximum(rel, 0.0) * lg_ref[h]), 0.0)


def _ret_consts(lg, dm_ref):
    c = TM
    idx = lax.broadcasted_iota(jnp.int32, (c, 1), 0).astype(F32)
    xi = jnp.exp((idx + 1.0) * lg)
    zeta = jnp.exp((c - 1.0 - idx) * lg)
    gc = jnp.exp(jnp.full((1, 1), c, F32) * lg)
    return dm_ref[...], xi, zeta, gc


def _ret_fwd_call(rqk, rv, rg, gain, lgam):
    tp = rqk.shape[0]
    nc = tp // TM

    def body(lg_ref, qk_ref, v_ref, rg_ref, g_ref, o_ref, a_ref, st_ref, sc_ref, s_scr, dm_scr):
        @pl.when(pl.program_id(0) == 0)
        def _():
            s_scr[...] = jnp.zeros_like(s_scr)
            _ret_fill_decay(lg_ref, dm_scr)

        for h in range(RET_HEADS):
            dm, xi, zeta, gc = _ret_consts(lg_ref[h], dm_scr.at[h])
            q = qk_ref[:, h * RET_QK:(h + 1) * RET_QK]
            k = qk_ref[:, D_MODEL + h * RET_QK:D_MODEL + (h + 1) * RET_QK]
            v = v_ref[:, h * RET_V:(h + 1) * RET_V]
            sb = s_scr[h].astype(BF16)
            st_ref[0, h] = sb
            s = (_dot(q, k, NT) * dm).astype(BF16)
            sc_ref[0, h] = s
            o = _dot(s, v, NN) + xi * _dot(q, sb, NN)
            kz = (k.astype(F32) * zeta).astype(BF16)
            s_scr[h] = gc * s_scr[h] + _dot(kz, v, TN)
            o_ref[:, h * RET_V:(h + 1) * RET_V] = o
            mu = _head_mean(o)
            xc = o - mu
            xh = xc * lax.rsqrt(_head_mean(xc * xc) + EPS)
            a_ref[:, h * RET_V:(h + 1) * RET_V] = (
                xh * g_ref[:, h * RET_V:(h + 1) * RET_V] * _silu(rg_ref[:, h * RET_V:(h + 1) * RET_V])).astype(BF16)

    return pl.pallas_call(
        body, name="ret_fwd", grid=(nc,),
        in_specs=[pl.BlockSpec(memory_space=pltpu.SMEM),
                  pl.BlockSpec((TM, 2 * D_MODEL), lambda n: (n, 0)),
                  pl.BlockSpec((TM, RET_W), lambda n: (n, 0)),
                  pl.BlockSpec((TM, RET_W), lambda n: (n, 0)),
                  pl.BlockSpec((1, RET_W), lambda n: (0, 0))],
        out_specs=[pl.BlockSpec((TM, RET_W), lambda n: (n, 0)),
                   pl.BlockSpec((TM, RET_W), lambda n: (n, 0)),
                   pl.BlockSpec((1, RET_HEADS, RET_QK, RET_V), lambda n: (n, 0, 0, 0)),
                   pl.BlockSpec((1, RET_HEADS, TM, TM), lambda n: (n, 0, 0, 0))],
        out_shape=[jax.ShapeDtypeStruct((tp, RET_W), F32), jax.ShapeDtypeStruct((tp, RET_W), BF16),
                   jax.ShapeDtypeStruct((nc, RET_HEADS, RET_QK, RET_V), BF16),
                   jax.ShapeDtypeStruct((nc, RET_HEADS, TM, TM), BF16)],
        scratch_shapes=[pltpu.VMEM((RET_HEADS, RET_QK, RET_V), F32), pltpu.VMEM((RET_HEADS, TM, TM), F32)],
        compiler_params=_cparams(1),
    )(lgam, rqk, rv, rg, gain)


def _ret_bwd_call(rqk, rv, rg, o_ret, dpr, wbr, states, scores, gain, lgam, cos, sin):
    tp = rqk.shape[0]
    nc = tp // TM
    half = RET_QK // 2

    def body(lg_ref, qk_ref, v_ref, rg_ref, o_ref, dpr_ref, wbr_ref, st_ref, sc_ref, g_ref, cos_ref, sin_ref, dp_ref,
             dg_ref, ds_scr, dm_scr):
        @pl.when(pl.program_id(0) == 0)
        def _():
            ds_scr[...] = jnp.zeros_like(ds_scr)
            dg_ref[...] = jnp.zeros_like(dg_ref)
            _ret_fill_decay(lg_ref, dm_scr)

        cos, sin = cos_ref[...], sin_ref[...]
        for h in range(RET_HEADS):
            hs = slice(h * RET_V, (h + 1) * RET_V)
            dm, xi, zeta, gc = _ret_consts(lg_ref[h], dm_scr.at[h])
            o = o_ref[:, hs]
            mu = _head_mean(o)
            xc = o - mu
            rstd = lax.rsqrt(_head_mean(xc * xc) + EPS)
            xh = xc * rstd
            gain_h = g_ref[:, hs]
            g = rg_ref[:, hs]
            sg = _sigmoid(g)
            silu = g * sg
            dah = _dot(dpr_ref[...], wbr_ref[hs, :], NT)
            dp_ref[:, 4 * D_MODEL + h * RET_V:4 * D_MODEL + (h + 1) * RET_V] = (
                dah * (xh * gain_h) * (sg * (1.0 + g * (1.0 - sg)))).astype(BF16)
            dn = dah * silu
            dg_ref[:, hs] += jnp.sum(dn * xh, axis=0, keepdims=True)
            dxh = dn * gain_h
            do = rstd * (dxh - _head_mean(dxh) - xh * _head_mean(dxh * xh))
            dob = do.astype(BF16)
            q = qk_ref[:, h * RET_QK:(h + 1) * RET_QK]
            k = qk_ref[:, D_MODEL + h * RET_QK:D_MODEL + (h + 1) * RET_QK]
            v = v_ref[:, hs]
            sp = st_ref[0, h]
            ds = ds_scr[h]
            dsb = ds.astype(BF16)
            s = sc_ref[0, h]
            dsc = (_dot(dob, v, NT) * dm).astype(BF16)
            dq = _dot(dsc, k, NN) + xi * _dot(dob, sp, NT)
            dk = _dot(dsc, q, TN) + zeta * _dot(v, dsb, NT)
            kz = (k.astype(F32) * zeta).astype(BF16)
            dv = _dot(s, dob, TN) + _dot(kz, dsb, NN)
            qx = (q.astype(F32) * xi).astype(BF16)
            ds_scr[h] = gc * ds + _dot(qx, dob, TN)
            dp_ref[:, 2 * D_MODEL + h * RET_V:2 * D_MODEL + (h + 1) * RET_V] = dv.astype(BF16)
            dk = dk * (RET_QK ** -0.5)
            for base, t in ((0, dq), (D_MODEL, dk)):
                t1, t2 = t[:, :half], t[:, half:]
                dp_ref[:, base + h * RET_QK:base + h * RET_QK + half] = (t1 * cos + t2 * sin).astype(BF16)
                dp_ref[:, base + h * RET_QK + half:base + (h + 1) * RET_QK] = (t2 * cos - t1 * sin).astype(BF16)

    rev = lambda n: (nc - 1 - n, 0)
    return pl.pallas_call(
        body, name="ret_bwd", grid=(nc,),
        in_specs=[pl.BlockSpec(memory_space=pltpu.SMEM),
                  pl.BlockSpec((TM, 2 * D_MODEL), rev),
                  pl.BlockSpec((TM, RET_W), rev),
                  pl.BlockSpec((TM, RET_W), rev),
                  pl.BlockSpec((TM, RET_W), rev),
                  pl.BlockSpec((TM, D_MODEL), rev),
                  pl.BlockSpec((RET_W, D_MODEL), lambda n: (0, 0)),
                  pl.BlockSpec((1, RET_HEADS, RET_QK, RET_V), lambda n: (nc - 1 - n, 0, 0, 0)),
                  pl.BlockSpec((1, RET_HEADS, TM, TM), lambda n: (nc - 1 - n, 0, 0, 0)),
                  pl.BlockSpec((1, RET_W), lambda n: (0, 0)),
                  pl.BlockSpec((TM, half), rev),
                  pl.BlockSpec((TM, half), rev)],
        out_specs=[pl.BlockSpec((TM, W_R), rev), pl.BlockSpec((1, RET_W), lambda n: (0, 0))],
        out_shape=[jax.ShapeDtypeStruct((tp, W_R), BF16), jax.ShapeDtypeStruct((1, RET_W), F32)],
        scratch_shapes=[pltpu.VMEM((RET_HEADS, RET_QK, RET_V), F32), pltpu.VMEM((RET_HEADS, TM, TM), F32)],
        compiler_params=_cparams(1),
    )(lgam, rqk, rv, rg, o_ret, dpr, wbr, states, scores, gain, cos, sin)


GLA_LEVELS = tuple(GC >> (s + 1) for s in range(int(math.log2(GC // GLA_SUB))))
NLEV = len(GLA_LEVELS)


def _gla_tril():
    return np.tril(np.ones((GC, GC), np.float32))


def _gla_masks():
    ii = lax.broadcasted_iota(jnp.int32, (GC, GC), 0)
    jj = lax.broadcasted_iota(jnp.int32, (GC, GC), 1)
    masks = []
    for m in GLA_LEVELS:
        sh = int(math.log2(2 * m))
        masks.append(((ii >> sh) == (jj >> sh)) & ((ii & m) != 0) & ((jj & m) == 0))
    sh = int(math.log2(GLA_SUB))
    md = ((ii >> sh) == (jj >> sh)) & (jj <= ii)
    row = lax.broadcasted_iota(jnp.int32, (GC, 1), 0)
    second = [(row & m) != 0 for m in GLA_LEVELS]
    return masks, md, second


def _gla_gate_call(u, w_g, wg, bg, pmat):
    tp = u.shape[0]
    gb = _proj_rows(tp)
    assert gb % GC == 0

    def body(u_ref, w_ref, wg_ref, bg_ref, p_ref, glr_ref, z_ref, b_ref):
        glr = _dot(u_ref[...], w_ref[...], NT)
        glr_ref[...] = glr
        z = _dot(glr.astype(BF16), wg_ref[...], NN) + bg_ref[...]
        z_ref[...] = z
        la = (jnp.minimum(z, 0.0) - jnp.log1p(jnp.exp(-jnp.abs(z)))) * (1.0 / GATE_TAU)
        for r in range(0, gb, GC):
            b_ref[r:r + GC, :] = _exact_pm(p_ref[...], la[r:r + GC, :])

    tile = pl.BlockSpec((gb, GLA_KW), lambda i: (i, 0))
    return pl.pallas_call(
        body, name="gla_gate", grid=(tp // gb,),
        in_specs=[pl.BlockSpec((gb, D_MODEL), lambda i: (i, 0)),
                  pl.BlockSpec((128, D_MODEL), lambda i: ((W_GP - 128) // 128, 0)),
                  pl.BlockSpec((128, GLA_KW), lambda i: (0, 0)),
                  pl.BlockSpec((1, GLA_KW), lambda i: (0, 0)), pl.BlockSpec((GC, GC), lambda i: (0, 0))],
        out_specs=[pl.BlockSpec((gb, 128), lambda i: (i, 0)), tile, tile],
        out_shape=[jax.ShapeDtypeStruct((tp, 128), F32), jax.ShapeDtypeStruct((tp, GLA_KW), F32),
                   jax.ShapeDtypeStruct((tp, GLA_KW), F32)],
        compiler_params=_cparams(1),
    )(u, w_g, wg, bg, pmat)


def _gla_gate_bwd_call(db, z, glr, u, wg, pmat_t, d_g):
    tp = db.shape[0]
    gb = _proj_rows(tp)
    assert gb % GC == 0 and (W_GP - 128) % 128 == 0

    def body(db_ref, z_ref, glr_ref, u_ref, wg_ref, pt_ref, dgin_ref, dg_ref, dwg_ref, dbg_ref, dwl_ref):
        i = pl.program_id(0)

        @pl.when(i == 0)
        def _():
            dwg_ref[...] = jnp.zeros_like(dwg_ref)
            dbg_ref[...] = jnp.zeros_like(dbg_ref)
            dwl_ref[...] = jnp.zeros_like(dwl_ref)

        dla = jnp.concatenate([_exact_pm(pt_ref[...], db_ref[r:r + GC, :]) for r in range(0, gb, GC)], axis=0)
        row = i * gb + lax.broadcasted_iota(jnp.int32, (gb, 1), 0)
        dz = jnp.where(row >= PADF, dla * (1.0 / GATE_TAU) * _sigmoid(-z_ref[...]), 0.0)
        dzb = dz.astype(BF16)
        dglr = _dot(dzb, wg_ref[...], NT).astype(BF16)
        dg_ref[...] = dglr
        dwg_ref[...] += _dot(glr_ref[...].astype(BF16), dzb, TN)
        dbg_ref[...] += jnp.sum(dz, axis=0, keepdims=True)
        dwl_ref[...] += _dot(u_ref[...], dglr, TN)

    tile = pl.BlockSpec((gb, GLA_KW), lambda i: (i, 0))
    const = lambda i: (0, 0)
    return pl.pallas_call(
        body, name="gla_gate_bwd", grid=(tp // gb,),
        in_specs=[tile, tile, pl.BlockSpec((gb, 128), lambda i: (i, 0)), pl.BlockSpec((gb, D_MODEL), lambda i: (i, 0)),
                  pl.BlockSpec((128, GLA_KW), const), pl.BlockSpec((GC, GC), const), ANY],
        out_specs=[pl.BlockSpec((gb, 128), lambda i: (i, (W_GP - 128) // 128)), pl.BlockSpec((128, GLA_KW), const),
                   pl.BlockSpec((1, GLA_KW), const), pl.BlockSpec((D_MODEL, 128), const)],
        out_shape=[jax.ShapeDtypeStruct(d_g.shape, BF16), jax.ShapeDtypeStruct((128, GLA_KW), F32),
                   jax.ShapeDtypeStruct((1, GLA_KW), F32), jax.ShapeDtypeStruct((D_MODEL, 128), F32)],
        input_output_aliases={6: 0}, compiler_params=_cparams(1),
    )(db, z, glr, u, wg, pmat_t, d_g)


def _gla_row_steps(b_ref, cs, rows, size):
    parts = [jnp.zeros((size, GLA_K), F32) if r is None else jnp.broadcast_to(b_ref[r:r + 1, cs], (size, GLA_K))
             for r in rows]
    return parts[0] if len(parts) == 1 else jnp.concatenate(parts, axis=0)


def _gla_factors(b_ref, h, second):
    cs = slice(h * GLA_K, (h + 1) * GLA_K)
    b = b_ref[:, cs]
    fq, fk = [], []
    for l, m in enumerate(GLA_LEVELS):
        d = b - _gla_row_steps(b_ref, cs, [s + m - 1 for s in range(0, GC, 2 * m)], 2 * m)
        f = jnp.exp(jnp.where(second[l], d, -d))
        fq.append(jnp.where(second[l], f, 0.0))
        fk.append(jnp.where(second[l], 0.0, f))
    dd = b - _gla_row_steps(b_ref, cs, [None] + [s - 1 for s in range(GLA_SUB, GC, GLA_SUB)], GLA_SUB)
    ed = jnp.exp(dd)
    edi = jnp.exp(-dd)
    eb = jnp.exp(b)
    bl = b_ref[GC - 1:GC, cs]
    ee = jnp.exp(bl - b)
    ebl = jnp.exp(bl)
    return fq, fk, ed, edi, eb, ee, ebl


def _gla_scaled(q, k, fq, fk, ed, edi):
    qt = [(q * f).astype(BF16) for f in fq]
    kt = [(k * f).astype(BF16) for f in fk]
    return qt, kt, (q * ed).astype(BF16), (k * edi).astype(BF16)


def _gla_scores(qt, kt, qd, kd, masks, md):
    a = jnp.where(md, _dot(qd, kd, NT), 0.0)
    for l in range(NLEV):
        a = a + jnp.where(masks[l], _dot(qt[l], kt[l], NT), 0.0)
    return a.astype(BF16)


def _gla_fwd_call(gqk, gv, b, gg, gain, comm=None):
    tp = gqk.shape[0]
    nc = tp // GC
    ns = nc // GS
    n_xc = len(comm.srcs) if comm else 0

    def body(qk_ref, v_ref, b_ref, gg_ref, g_ref, *rest):
        xc_src = rest[:n_xc]
        o_ref, a_ref, st_ref, am_ref = rest[n_xc:n_xc + 4]
        xc_dst = rest[n_xc + 4:2 * n_xc + 4]
        s_scr = rest[2 * n_xc + 4]
        n = pl.program_id(0)
        if n_xc:
            begin, finish = comm.make(xc_src, xc_dst, rest[-2], rest[-1])
            pl.when(n == 0)(begin)
            pl.when(n == ns - 1)(finish)

        @pl.when(n == 0)
        def _():
            s_scr[...] = jnp.zeros_like(s_scr)

        masks, md, second = _gla_masks()
        for cc in range(GS):
            rows = pl.ds(cc * GC, GC)
            qk_c, v_c, b_c, gg_c, o_c, a_c = (r.at[rows] for r in (qk_ref, v_ref, b_ref, gg_ref, o_ref, a_ref))
            for h in range(GLA_HEADS):
                q = qk_c[:, h * GLA_K:(h + 1) * GLA_K]
                k = qk_c[:, GLA_KW + h * GLA_K:GLA_KW + (h + 1) * GLA_K]
                vs = slice(h * GLA_V, (h + 1) * GLA_V)
                v = v_c[:, vs]
                fq, fk, ed, edi, eb, ee, ebl = _gla_factors(b_c, h, second)
                a = _gla_scores(*_gla_scaled(q, k, fq, fk, ed, edi), masks, md)
                am_ref[cc, h] = a
                sb = s_scr[h].astype(BF16)
                st_ref[cc, h] = sb
                o = _dot(a, v, NN) + _dot((q * eb).astype(BF16), sb, NT)
                s_scr[h] = s_scr[h] * ebl + _dot(v, (k * ee).astype(BF16), TN)
                o_c[:, vs] = o
                xh = o * lax.rsqrt(_head_mean(o * o) + EPS)
                a_c[:, vs] = (xh * g_ref[:, vs] * _silu(gg_c[:, vs])).astype(BF16)

    return pl.pallas_call(
        body, name="gla_fwd", grid=(ns,),
        in_specs=[pl.BlockSpec((GS * GC, 2 * GLA_KW), lambda n: (n, 0)),
                  pl.BlockSpec((GS * GC, GLA_W), lambda n: (n, 0)),
                  pl.BlockSpec((GS * GC, GLA_KW), lambda n: (n, 0)),
                  pl.BlockSpec((GS * GC, GLA_W), lambda n: (n, 0)),
                  pl.BlockSpec((1, GLA_W), lambda n: (0, 0))] + [ANY] * n_xc,
        out_specs=[pl.BlockSpec((GS * GC, GLA_W), lambda n: (n, 0)),
                   pl.BlockSpec((GS * GC, GLA_W), lambda n: (n, 0)),
                   pl.BlockSpec((GS, GLA_HEADS, GLA_V, GLA_K), lambda n: (n, 0, 0, 0)),
                   pl.BlockSpec((GS, GLA_HEADS, GC, GC), lambda n: (n, 0, 0, 0))] + [ANY] * n_xc,
        out_shape=[jax.ShapeDtypeStruct((tp, GLA_W), F32), jax.ShapeDtypeStruct((tp, GLA_W), BF16),
                   jax.ShapeDtypeStruct((nc, GLA_HEADS, GLA_V, GLA_K), BF16),
                   jax.ShapeDtypeStruct((nc, GLA_HEADS, GC, GC), BF16)] + (list(comm.out_shapes) if comm else []),
        scratch_shapes=[pltpu.VMEM((GLA_HEADS, GLA_V, GLA_K), F32)] + (_comm_sems(comm) if comm else []),
        compiler_params=_cparams(1),
    )(gqk, gv, b, gg, gain, *(comm.srcs if comm else ()))


def _gla_bwd_call(gqk, gv, b, gg, o_gla, da, states, scores, gain, comm=None):
    tp = gqk.shape[0]
    nc = tp // GC
    ns = nc // GS
    o_gv, o_gg = 2 * GLA_KW, 2 * GLA_KW + GLA_W
    n_xc = len(comm.srcs) if comm else 0

    def body(qk_all, v_all, b_all, gg_all, o_all, da_all, st_ref, am_ref, g_ref, *rest):
        xc_src = rest[:n_xc]
        dp_all, db_all, dg_ref = rest[n_xc:n_xc + 3]
        xc_dst = rest[n_xc + 3:2 * n_xc + 3]
        ds_scr = rest[2 * n_xc + 3]
        n = pl.program_id(0)
        if n_xc:
            begin, finish = comm.make(xc_src, xc_dst, rest[-2], rest[-1])
            pl.when(n == 0)(begin)
            pl.when(n == ns - 1)(finish)

        @pl.when(n == 0)
        def _():
            ds_scr[...] = jnp.zeros_like(ds_scr)
            dg_ref[...] = jnp.zeros_like(dg_ref)

        masks, md, second = _gla_masks()
        for cc, h in [(cc, h) for cc in reversed(range(GS)) for h in range(GLA_HEADS)]:
            rows = pl.ds(cc * GC, GC)
            qk_ref, v_ref, b_scr, gg_ref, o_ref, da_ref, dp_ref, db_scr = (
                r.at[rows] for r in (qk_all, v_all, b_all, gg_all, o_all, da_all, dp_all, db_all))
            cs = slice(h * GLA_K, (h + 1) * GLA_K)
            vs = slice(h * GLA_V, (h + 1) * GLA_V)
            o = o_ref[:, vs]
            rstd = lax.rsqrt(_head_mean(o * o) + EPS)
            xh = o * rstd
            gain_h = g_ref[:, vs]
            g = gg_ref[:, vs]
            sg = _sigmoid(g)
            dah = da_ref[:, vs]
            dp_ref[:, o_gg + h * GLA_V:o_gg + (h + 1) * GLA_V] = (
                dah * (xh * gain_h) * (sg * (1.0 + g * (1.0 - sg)))).astype(BF16)
            dn = dah * (g * sg)
            dg_ref[:, vs] += jnp.sum(dn * xh, axis=0, keepdims=True)
            dxh = dn * gain_h
            do = rstd * (dxh - xh * _head_mean(dxh * xh))
            dob = do.astype(BF16)
            q = qk_ref[:, cs]
            k = qk_ref[:, GLA_KW + h * GLA_K:GLA_KW + (h + 1) * GLA_K]
            v = v_ref[:, vs]
            fq, fk, ed, edi, eb, ee, ebl = _gla_factors(b_scr, h, second)
            qt, kt, qd, kd = _gla_scaled(q, k, fq, fk, ed, edi)
            sp = st_ref[cc, h]
            ds = ds_scr[h]
            dsb = ds.astype(BF16)
            q_in = q * eb
            k_end = k * ee
            da_s = _dot(dob, v, NT)
            dv = _dot(am_ref[cc, h], dob, TN) + _dot(k_end.astype(BF16), dsb, NT)
            dq_in = _dot(dob, sp, NN)
            dk_end = _dot(v, dsb, NN)
            dbl = jnp.sum(sp.astype(F32) * ds, axis=0, keepdims=True) * ebl
            ds_scr[h] = ds * ebl + _dot(dob, q_in.astype(BF16), TN)
            dq = dq_in * eb
            dk = dk_end * ee
            de_end = dk_end * k_end
            db = dq_in * q_in - de_end
            placed = [(GC - 1, jnp.sum(de_end, axis=0, keepdims=True) + dbl)]
            for l, m in enumerate(GLA_LEVELS):
                dal = jnp.where(masks[l], da_s, 0.0).astype(BF16)
                dqt = _dot(dal, kt[l], NN)
                dkt = _dot(dal, qt[l], TN)
                dq = dq + dqt * fq[l]
                dk = dk + dkt * fk[l]
                gl = dqt * (q * fq[l]) - dkt * (k * fk[l])
                db = db + gl
                placed += [(s + m - 1, -jnp.sum(gl[s:s + 2 * m], axis=0, keepdims=True)) for s in range(0, GC, 2 * m)]
            dad = jnp.where(md, da_s, 0.0).astype(BF16)
            dqd = _dot(dad, kd, NN)
            dkd = _dot(dad, qd, TN)
            dq = dq + dqd * ed
            dk = dk + dkd * edi
            gd = dqd * (q * ed) - dkd * (k * edi)
            db = db + gd
            placed += [(s - 1, -jnp.sum(gd[s:s + GLA_SUB], axis=0, keepdims=True)) for s in range(GLA_SUB, GC, GLA_SUB)]
            db_scr[:, cs] = db
            for r, val in placed:
                db_scr[r:r + 1, cs] += val
            dp_ref[:, cs] = (dq * (GLA_K ** -0.5)).astype(BF16)
            dp_ref[:, GLA_KW + h * GLA_K:GLA_KW + (h + 1) * GLA_K] = dk.astype(BF16)
            dp_ref[:, o_gv + h * GLA_V:o_gv + (h + 1) * GLA_V] = dv.astype(BF16)

    rev = lambda n: (ns - 1 - n, 0)
    const = lambda n: (0, 0)
    xc_shapes, xc_sems = (list(comm.out_shapes), _comm_sems(comm)) if n_xc else ([], [])
    return pl.pallas_call(
        body, name="gla_bwd", grid=(ns,),
        in_specs=[pl.BlockSpec((GS * GC, 2 * GLA_KW), rev),
                  pl.BlockSpec((GS * GC, GLA_W), rev),
                  pl.BlockSpec((GS * GC, GLA_KW), rev),
                  pl.BlockSpec((GS * GC, GLA_W), rev),
                  pl.BlockSpec((GS * GC, GLA_W), rev),
                  pl.BlockSpec((GS * GC, GLA_W), rev),
                  pl.BlockSpec((GS, GLA_HEADS, GLA_V, GLA_K), lambda n: (ns - 1 - n, 0, 0, 0)),
                  pl.BlockSpec((GS, GLA_HEADS, GC, GC), lambda n: (ns - 1 - n, 0, 0, 0)),
                  pl.BlockSpec((1, GLA_W), const)] + [ANY] * n_xc,
        out_specs=[pl.BlockSpec((GS * GC, W_GP), rev), pl.BlockSpec((GS * GC, GLA_KW), rev),
                   pl.BlockSpec((1, GLA_W), const)] + [ANY] * n_xc,
        out_shape=[jax.ShapeDtypeStruct((tp, W_GP), BF16), jax.ShapeDtypeStruct((tp, GLA_KW), F32),
                   jax.ShapeDtypeStruct((1, GLA_W), F32)] + xc_shapes,
        scratch_shapes=[pltpu.VMEM((GLA_HEADS, GLA_V, GLA_K), F32)] + xc_sems,
        compiler_params=_cparams(1),
    )(gqk, gv, b, gg, o_gla, da, states, scores, gain, *(comm.srcs if comm else ()))


def _mid_call(a_ret, a_gla, mg, h0, tgt, wbr, wbg, wout, gf):
    tp = h0.shape[0]
    nt = tp // TM

    def body(ar_ref, ag_ref, mg_ref, h_ref, t_ref, wbr_ref, wbg_ref, wo_ref, gf_ref,
             dh1_ref, dag_ref, dm_ref, mb_ref, dh1b_ref, dprb_ref, dpgb_ref, loss_ref, dgf_ref):
        i = pl.program_id(0)

        @pl.when(i == 0)
        def _():
            loss_ref[...] = jnp.zeros_like(loss_ref)
            dgf_ref[...] = jnp.zeros_like(dgf_ref)

        ar, ag = ar_ref[...], ag_ref[...]
        pr = _dot(ar, wbr_ref[...], NN)
        pg = _dot(ag, wbg_ref[...], NN)
        sr = _sigmoid(mg_ref[:, :D_MODEL])
        sg = _sigmoid(mg_ref[:, D_MODEL:])
        merged = (sr * pr + sg * pg).astype(BF16)
        mb_ref[...] = merged
        h1 = h_ref[...] + _dot(merged, wo_ref[...], NN)
        r1 = lax.rsqrt(jnp.mean(h1 * h1, axis=-1, keepdims=True) + EPS)
        xh = h1 * r1
        gfv = gf_ref[...]
        live = jnp.where(i > 0, 1.0, 0.0).astype(F32)
        err = (xh * gfv - t_ref[...]) * live
        loss_ref[...] += jnp.full(loss_ref.shape, 0.5 / D_MODEL, F32) * jnp.sum(err * err)
        dy = err * (1.0 / D_MODEL)
        dgf_ref[...] += jnp.sum(dy * xh, axis=0, keepdims=True)
        dxh = dy * gfv
        dh1 = r1 * (dxh - xh * jnp.mean(dxh * xh, axis=-1, keepdims=True))
        dh1_ref[...] = dh1
        dh1b = dh1.astype(BF16)
        dh1b_ref[...] = dh1b
        dmerged = _dot(dh1b, wo_ref[...], NT)
        dm_ref[:, :D_MODEL] = (dmerged * pr * sr * (1.0 - sr)).astype(BF16)
        dm_ref[:, D_MODEL:] = (dmerged * pg * sg * (1.0 - sg)).astype(BF16)
        dpr = (dmerged * sr).astype(BF16)
        dpg = (dmerged * sg).astype(BF16)
        dprb_ref[...] = dpr
        dpgb_ref[...] = dpg
        dag_ref[...] = _dot(dpg, wbg_ref[...], NT)

    tile = lambda w: pl.BlockSpec((TM, w), lambda i: (i, 0))
    const = lambda r, w: pl.BlockSpec((r, w), lambda i: (0, 0))
    return pl.pallas_call(
        body, name="merge_out_loss", grid=(nt,),
        in_specs=[tile(RET_W), tile(GLA_W), tile(W_M), tile(D_MODEL),
                  pl.BlockSpec((TM, D_MODEL), lambda i: (jnp.maximum(i - 1, 0), 0)),
                  const(RET_W, D_MODEL), const(GLA_W, D_MODEL), const(D_MODEL, D_MODEL), const(1, D_MODEL)],
        out_specs=[tile(D_MODEL), tile(GLA_W), tile(W_M), tile(D_MODEL), tile(D_MODEL), tile(D_MODEL),
                   tile(D_MODEL), const(1, 128), const(1, D_MODEL)],
        out_shape=[jax.ShapeDtypeStruct((tp, D_MODEL), F32), jax.ShapeDtypeStruct((tp, GLA_W), F32),
                   jax.ShapeDtypeStruct((tp, W_M), BF16),
                   jax.ShapeDtypeStruct((tp, D_MODEL), BF16), jax.ShapeDtypeStruct((tp, D_MODEL), BF16),
                   jax.ShapeDtypeStruct((tp, D_MODEL), BF16), jax.ShapeDtypeStruct((tp, D_MODEL), BF16),
                   jax.ShapeDtypeStruct((1, 128), F32), jax.ShapeDtypeStruct((1, D_MODEL), F32)],
        compiler_params=_cparams(1),
    )(a_ret, a_gla, mg, h0, tgt, wbr, wbg, wout, gf)


def _device_step(x2d, tgt2d, meta, norm_gain, w_in_part, w_gate_up, b_gate, ret_gain, gla_gain, branch_parts,
                 final_gain, ck):
    seq = x2d.shape[0]
    tp = T0 + seq
    head = jnp.concatenate([jnp.zeros((PADF, D_MODEL), F32), meta], axis=0)
    wg_pad = jnp.pad(w_gate_up, ((0, 128 - GATE_RANK), (0, 0))).astype(BF16)

    half = RET_QK // 2
    cos, sin = (jnp.asarray(t) for t in _rope_tables(tp))
    lgam = jnp.log1p(-(2.0 ** (-5.0 - jnp.arange(RET_HEADS, dtype=F32))))
    pmat = jnp.asarray(_gla_tril(), BF16)
    pmat_t = jnp.asarray(_gla_tril().T.copy(), BF16)

    h0, u, g_in = _rms_call(x2d, head, norm_gain, _gather_plan([w_in_part], relay=(True,)))
    sw, hc = w_in_part.shape
    w_in_t = g_in.reshape(4, 2, sw, hc).transpose(0, 2, 1, 3).reshape(4 * sw, 2 * hc)
    w_r = w_in_t
    w_g = jnp.pad(w_in_t[W_R:W_R + W_G], ((0, W_GP - W_G), (0, 0)))
    w_m = w_in_t[W_R + W_G:]
    tab = pl.BlockSpec((_proj_rows(tp), half), lambda j, i: (i, 0))
    rqk = _mm_nn("proj_rqk", u, w_r, BF16, D_MODEL, 0, 2 * D_MODEL, _rope_epilogue, (cos, sin), (tab, tab))
    rv = _mm_nn("proj_rv", u, w_r, BF16, RET_W, 2 * D_MODEL, RET_W)
    rg = _mm_nn("proj_rg", u, w_r, F32, RET_W, 4 * D_MODEL, RET_W)
    gqk = _mm_nn("proj_gqk", u, w_g, F32, 2 * GLA_KW, 0, 2 * GLA_KW, _gqk_epilogue)
    gv = _mm_nn("proj_gv", u, w_g, BF16, GLA_W, 2 * GLA_KW, GLA_W)
    gg = _mm_nn("proj_gg", u, w_g, F32, GLA_W, 2 * GLA_KW + GLA_W, GLA_W)
    mg = _mm_nn("proj_mg", u, w_m, F32, W_M, 0, W_M)

    o_ret, a_ret, st_ret, sc_ret = _ret_fwd_call(rqk, rv, rg, ret_gain, lgam)
    glr, z_gate, b_dec = _gla_gate_call(u, w_g, wg_pad, b_gate, pmat)
    o_gla, a_gla, st_gla, sc_gla, g_br, g_bg, g_out = _gla_fwd_call(gqk, gv, b_dec, gg, gla_gain,
                                                                    comm=_spread_plan(branch_parts))
    wbr = g_br.reshape(RET_W, D_MODEL)
    wbg = g_bg.reshape(GLA_W, D_MODEL)
    wout = g_out.reshape(D_MODEL, D_MODEL)

    gf = final_gain.reshape(1, D_MODEL)
    (dh1, da_gla, dm, merged_b, dh1_b, dpr_b, dpg_b, loss, dgf) = _mid_call(
        a_ret, a_gla, mg, h0, tgt2d, wbr, wbg, wout, gf)

    names_b = ("w_branch_ret", "w_branch_gla", "w_out")
    g2_b = [_mm_tn("dw_br", a_ret, dpr_b, D_MODEL).reshape(4, 2, RET_W // 8, D_MODEL).transpose(1, 0, 2, 3),
            _mm_tn("dw_bg", a_gla, dpg_b, D_MODEL).reshape(4, 2, GLA_W // 8, D_MODEL).transpose(1, 0, 2, 3),
            _mm_tn("dw_out", merged_b, dh1_b, D_MODEL).reshape(4, 2, D_MODEL // 8, D_MODEL).transpose(1, 0, 2, 3)]
    sib_b = _swap_halves_call("swap_halves_branch", g2_b)
    sum_b = [_add_half_call("add_half_" + nm, g, b, ck) for nm, g, b in zip(names_b, g2_b, sib_b)]
    d_g, db_dec, dgla_gain, *chips_b = _gla_bwd_call(gqk, gv, b_dec, gg, o_gla, da_gla, st_gla, sc_gla, gla_gain,
                                                     comm=_exchange_plan(sum_b))
    d_g, dwg, dbg, dw_glr = _gla_gate_bwd_call(db_dec, z_gate, glr, u, wg_pad, pmat_t, d_g)
    mine = [_add_chips_call("add_chips_" + nm, g, b, p, ck) for nm, g, b, p in zip(names_b, g2_b, sib_b, chips_b)]

    d_r, dret_gain = _ret_bwd_call(rqk, rv, rg, o_ret, dpr_b, wbr, st_ret, sc_ret, ret_gain, lgam, cos, sin)

    dwp = _mm_tn("dw_r", u, d_r, 3 * D_MODEL, out_cols=IN_PAD)
    dwp = _mm_tn("dw_g", u, d_g, 3 * D_MODEL, ncols=W_GP - 128, into=dwp, col0=W_R)
    g2_in = _place_merge_cols_call(dwp, _mm_tn("dw_m", u, dm, 2 * D_MODEL), dw_glr).reshape(2, D_MODEL // 2, IN_PAD)

    du, sib_in = _mm_nt_acc("du_g", d_g, w_g, W_GP, comm=_swap_plan([g2_in]), tb=_proj_rows(tp))
    sum_in = _add_rows_call("add_half_w_in", g2_in, sib_in, ck)
    du, chips_in = _mm_nt_acc("du_r", d_r, w_r, 2 * D_MODEL, acc_in=du, comm=_exchange_window_plan(sum_in),
                              tb=_proj_rows(tp))
    tile = pl.BlockSpec((TB, D_MODEL), lambda i, kk: (i, 0))
    row = pl.BlockSpec((1, D_MODEL), lambda i, kk: (0, 0))
    dx, dmeta, dnorm_gain = _mm_nt_acc(
        "du_m", dm, w_m, W_M, acc_in=du, epilogue=_rms_bwd_epilogue, extras=(h0, norm_gain, dh1),
        extra_specs=(tile, row, tile),
        extra_out_shapes=(jax.ShapeDtypeStruct((seq, D_MODEL), F32), jax.ShapeDtypeStruct((N_META, D_MODEL), F32),
                          jax.ShapeDtypeStruct((1, D_MODEL), F32)),
        extra_out_specs=(ANY, pl.BlockSpec((N_META, D_MODEL), lambda i, kk: (0, 0)), row),
        extra_scratch=(pltpu.VMEM((2, TB, D_MODEL), F32), pltpu.SemaphoreType.DMA((2,))))
    small = dict(norm_gain=dnorm_gain, b_gate=dbg, ret_norm_gain=dret_gain, gla_norm_gain=dgla_gain,
                 final_norm_gain=dgf, w_gate_up=dwg[:GATE_RANK], meta_tokens=dmeta, loss=loss[0, 0])
    rows = -(-sum(sz for _, sz in SMALL) // 128 // 8) * 8
    mine_in, g_small = _add_window_call("add_chips_w_in", g2_in, sib_in, chips_in, ck,
                                        _gather_plan([_pack_rows([small[nm] for nm, _ in SMALL], rows)]))
    full = _join_halves_call("join_halves", [mine_in] + mine)

    return dict(dx=dx, small=g_small, w_in=full[0], w_branch_ret=full[1], w_branch_gla=full[2], w_out=full[3])


MESH = pl.DeviceIdType.MESH
ANY = pl.BlockSpec(memory_space=pl.ANY)


def _place():
    return lax.axis_index("x"), lax.axis_index("y"), lax.axis_index("c")


def _gather8_call(name, parts):
    comm = _gather_plan(parts)
    n = len(parts)

    def body(*refs):
        begin, finish = comm.make(refs[:n], refs[n:2 * n], refs[-2], refs[-1])
        begin()
        finish()

    return pl.pallas_call(
        body, name=name, out_shape=list(comm.out_shapes), in_specs=[ANY] * n, out_specs=[ANY] * n,
        scratch_shapes=_comm_sems(comm),
    )(*parts)


def _swap_halves_call(name, gs):
    n = len(gs)

    def body(*refs):
        g_refs, b_refs = refs[:n], refs[n:2 * n]
        send_sems, recv_sems = refs[2 * n:]
        x, y, c = _place()
        copies = [pltpu.make_async_remote_copy(
            src_ref=g_refs[t].at[1 - c], dst_ref=b_refs[t], send_sem=send_sems.at[t], recv_sem=recv_sems.at[t],
            device_id=(x, y, 1 - c), device_id_type=MESH) for t in range(n)]
        for cp in copies:
            cp.start()
        for cp in copies:
            cp.wait()

    return pl.pallas_call(
        body, name=name,
        out_shape=[jax.ShapeDtypeStruct(g.shape[1:], g.dtype) for g in gs],
        in_specs=[ANY] * n, out_specs=[ANY] * n,
        scratch_shapes=[pltpu.SemaphoreType.DMA((n,)), pltpu.SemaphoreType.DMA((n,))],
    )(*gs)


def _join_halves_call(name, ts):
    n = len(ts)

    def body(*refs):
        o_refs = refs[n:2 * n]
        send_sems, recv_sems = refs[2 * n:]
        x, y, c = _place()
        copies = [pltpu.make_async_remote_copy(
            src_ref=o_refs[t].at[c], dst_ref=o_refs[t].at[c], send_sem=send_sems.at[t], recv_sem=recv_sems.at[t],
            device_id=(x, y, 1 - c), device_id_type=MESH) for t in range(n)]
        for cp in copies:
            cp.start()
        for t in range(n):
            copies[t].wait_send()
            pltpu.make_async_remote_copy(
                src_ref=o_refs[t].at[c], dst_ref=o_refs[t].at[1 - c], send_sem=send_sems.at[t],
                recv_sem=recv_sems.at[t], device_id=(x, y, 1 - c), device_id_type=MESH).wait_recv()

    return pl.pallas_call(
        body, name=name,
        out_shape=[jax.ShapeDtypeStruct(t.shape, t.dtype) for t in ts],
        in_specs=[ANY] * n, out_specs=[ANY] * n, input_output_aliases={t: t for t in range(n)},
        scratch_shapes=[pltpu.SemaphoreType.DMA((n,)), pltpu.SemaphoreType.DMA((n,))],
    )(*ts)


def _row_block(rows, cols, budget):
    best = 8
    for rb in range(8, rows + 1, 8):
        if rows % rb == 0 and rb * cols * 4 <= budget:
            best = rb
    return best


def _add_half_call(name, g, b, ck):
    _, _, r, cc = g.shape
    rb = _row_block(r, cc, 2 * 1024 * 1024)

    def body(ck_ref, g_ref, b_ref, o_ref):
        o_ref[...] = (g_ref[...] + b_ref[...]).astype(BF16)

    return pl.pallas_call(
        body, name=name,
        grid_spec=pltpu.PrefetchScalarGridSpec(
            num_scalar_prefetch=1, grid=(4, r // rb),
            in_specs=[pl.BlockSpec((None, None, rb, cc), lambda k, i, ck_ref: (ck_ref[0], k, i, 0)),
                      pl.BlockSpec((None, rb, cc), lambda k, i, ck_ref: (k, i, 0))],
            out_specs=pl.BlockSpec((None, rb, cc), lambda k, i, ck_ref: (k, i, 0))),
        out_shape=jax.ShapeDtypeStruct(b.shape, BF16),
        compiler_params=_cparams(2),
    )(ck, g, b)


def _add_rows_call(name, g, b, ck):
    _, r, cc = g.shape
    rb = _row_block(r, cc, 2 * 1024 * 1024)

    def body(ck_ref, g_ref, b_ref, o_ref):
        o_ref[...] = (g_ref[...] + b_ref[...]).astype(BF16)

    return pl.pallas_call(
        body, name=name,
        grid_spec=pltpu.PrefetchScalarGridSpec(
            num_scalar_prefetch=1, grid=(r // rb,),
            in_specs=[pl.BlockSpec((None, rb, cc), lambda i, ck_ref: (ck_ref[0], i, 0)),
                      pl.BlockSpec((rb, cc), lambda i, ck_ref: (i, 0))],
            out_specs=pl.BlockSpec((rb, cc), lambda i, ck_ref: (i, 0))),
        out_shape=jax.ShapeDtypeStruct((r, cc), BF16),
        compiler_params=_cparams(1),
    )(ck, g, b)


def _add_window_call(name, g, b, p, ck, comm):
    _, r, _ = g.shape
    nb, step = WIN_W // 128, WIN_STEP // 128
    n_xc = len(comm.srcs)

    def body(ck_ref, g_ref, b_ref, p0_ref, p1_ref, p2_ref, *rest):
        o_ref = rest[n_xc]
        i = pl.program_id(0)
        begin, finish = comm.make(rest[:n_xc], rest[n_xc + 1:2 * n_xc + 1], rest[-2], rest[-1])
        pl.when(i == 0)(begin)
        own = g_ref[...] + b_ref[...]
        o_ref[...] = ((own + p0_ref[...].astype(F32)) + p1_ref[...].astype(F32)) + p2_ref[...].astype(F32)
        pl.when(i == nb - 1)(finish)

    def peer(j):
        return pl.BlockSpec((None, r, 128), lambda i, ck_ref: (j, 0, i))

    return pl.pallas_call(
        body, name=name,
        grid_spec=pltpu.PrefetchScalarGridSpec(
            num_scalar_prefetch=1, grid=(nb,),
            in_specs=[pl.BlockSpec((None, r, 128), lambda i, ck_ref: (ck_ref[0], 0, step * ck_ref[1] + i)),
                      pl.BlockSpec((r, 128), lambda i, ck_ref: (0, step * ck_ref[1] + i)),
                      peer(0), peer(1), peer(2)] + [ANY] * n_xc,
            out_specs=[pl.BlockSpec((None, r, 128), lambda i, ck_ref: (ck_ref[0], 0, i))] + [ANY] * n_xc,
            scratch_shapes=_comm_sems(comm)),
        out_shape=[jax.ShapeDtypeStruct((2, r, WIN_W), F32)] + list(comm.out_shapes),
        compiler_params=_cparams(1),
    )(ck, g, b, p, p, p, *comm.srcs)


def _add_chips_call(name, g, b, p, ck):
    _, _, r, cc = g.shape
    rb = _row_block(r, cc, 2 * 1024 * 1024)

    def body(ck_ref, g_ref, b_ref, p0_ref, p1_ref, p2_ref, o_ref):
        own = g_ref[...] + b_ref[...]
        o_ref[...] = ((own + p0_ref[...].astype(F32)) + p1_ref[...].astype(F32)) + p2_ref[...].astype(F32)

    def peer(j):
        return pl.BlockSpec((None, rb, cc), lambda i, ck_ref: (j, i, 0))

    return pl.pallas_call(
        body, name=name,
        grid_spec=pltpu.PrefetchScalarGridSpec(
            num_scalar_prefetch=1, grid=(r // rb,),
            in_specs=[pl.BlockSpec((None, None, rb, cc), lambda i, ck_ref: (ck_ref[0], ck_ref[1], i, 0)),
                      pl.BlockSpec((None, rb, cc), lambda i, ck_ref: (ck_ref[1], i, 0)),
                      peer(0), peer(1), peer(2)],
            out_specs=pl.BlockSpec((None, rb, cc), lambda i, ck_ref: (ck_ref[0], i, 0))),
        out_shape=jax.ShapeDtypeStruct((2, r, cc), F32),
        compiler_params=_cparams(1),
    )(ck, g, b, p, p, p)


def _sum8_call(name, g):
    def body(g_ref, o_ref):
        acc = g_ref[0]
        for d in range(1, 8):
            acc = acc + g_ref[d]
        o_ref[...] = acc

    return pl.pallas_call(body, name=name, out_shape=jax.ShapeDtypeStruct(g.shape[1:], F32))(g)


def _adamw_call(name, w, g, m, v):
    r, cc = w.shape
    if r % 8 == 0 or r * cc * 4 <= 1024 * 1024:
        rb = _row_block(r, cc, 1024 * 1024) if r % 8 == 0 else r
        grid, spec = (r // rb,), pl.BlockSpec((rb, cc), lambda i: (i, 0))
    else:
        grid, spec = (cc // 128,), pl.BlockSpec((r, 128), lambda i: (0, i))

    def body(w_ref, g_ref, m_ref, v_ref, d_ref, m2_ref, v2_ref):
        _adamw_update(g_ref[...], w_ref, m_ref, v_ref, d_ref, m2_ref, v2_ref)

    return pl.pallas_call(
        body, name=name, grid=grid, in_specs=[spec] * 4, out_specs=[spec] * 3,
        out_shape=[jax.ShapeDtypeStruct((r, cc), F32)] * 3, compiler_params=_cparams(1),
    )(w, g, m, v)


def _adamw_update(gv, w_ref, m_ref, v_ref, d_ref, m2_ref, v2_ref):
    m2 = ADAM_B1 * m_ref[...] + (1.0 - ADAM_B1) * gv
    v2 = ADAM_B2 * v_ref[...] + (1.0 - ADAM_B2) * (gv * gv)
    m_hat = m2 / (1.0 - ADAM_B1 ** ADAM_STEP)
    v_hat = v2 / (1.0 - ADAM_B2 ** ADAM_STEP)
    d_ref[...] = -ADAM_LR * (m_hat / (jnp.sqrt(v_hat) + ADAM_EPS) + ADAM_WD * w_ref[...])
    m2_ref[...] = m2
    v2_ref[...] = v2


def _adamw_window_call(name, w_t, f, m_t, v_t, lane0):
    s, r = w_t.shape
    wl = f.shape[1]

    def body(l0_ref, w_ref, f_ref, m_ref, v_ref, g_ref, d_ref, m2_ref, v2_ref):
        gv = pltpu.roll(f_ref[...], lax.rem(wl - l0_ref[0], wl), axis=1).T[:s]
        g_ref[...] = gv
        _adamw_update(gv, w_ref, m_ref, v_ref, d_ref, m2_ref, v2_ref)

    spec = pl.BlockSpec((s, 128), lambda i, l0: (0, i))
    return pl.pallas_call(
        body, name=name,
        grid_spec=pltpu.PrefetchScalarGridSpec(
            num_scalar_prefetch=1, grid=(r // 128,),
            in_specs=[spec, pl.BlockSpec((128, wl), lambda i, l0: (i, 0)), spec, spec], out_specs=[spec] * 4),
        out_shape=[jax.ShapeDtypeStruct((s, r), F32)] * 4, compiler_params=_cparams(1),
    )(lane0, w_t, f, m_t, v_t)


SMALL = (("norm_gain", D_MODEL), ("b_gate", GLA_KW), ("ret_norm_gain", RET_W), ("gla_norm_gain", GLA_W),
         ("final_norm_gain", D_MODEL), ("w_gate_up", GATE_RANK * GLA_KW), ("meta_tokens", N_META * D_MODEL),
         ("loss", 1))


def _pack_rows(vecs, rows):
    flat = jnp.concatenate([v.reshape(-1) for v in vecs])
    return jnp.pad(flat, (0, rows * 128 - flat.shape[0])).reshape(rows, 128)


def kernel(x, meta_tokens, norm_gain, w_in, w_gate_up, b_gate, ret_norm_gain, gla_norm_gain, w_branch_ret, w_branch_gla, w_out, final_norm_gain, loss_target, m_meta_tokens, m_norm_gain, m_w_in, m_w_gate_up, m_b_gate, m_ret_norm_gain, m_gla_norm_gain, m_w_branch_ret, m_w_branch_gla, m_w_out, m_final_norm_gain, v_meta_tokens, v_norm_gain, v_w_in, v_w_gate_up, v_b_gate, v_ret_norm_gain, v_gla_norm_gain, v_w_branch_ret, v_w_branch_gla, v_w_out, v_final_norm_gain):
    xi, yi, ci = _place()
    kme = 2 * xi + yi
    ck = jnp.stack([ci, kme]).astype(jnp.int32)
    sw_in = w_in.shape[2]

    def my_half(a, dtype):
        r, cc = a.shape
        return lax.dynamic_index_in_dim(a.reshape(2, r // 2, cc), ci, 0, keepdims=False).astype(dtype)

    g_meta, g_wg = _gather8_call("gather_small_weights", [my_half(meta_tokens, F32), my_half(w_gate_up[0], F32)])
    branch_parts = [my_half(w_branch_ret[0], BF16), my_half(w_branch_gla[0], BF16), my_half(w_out[0], BF16)]
    meta = g_meta.reshape(4, 2, N_META // 2, D_MODEL // 4).transpose(1, 2, 0, 3).reshape(N_META, D_MODEL)
    wg_full = g_wg.reshape(4, 2, GATE_RANK // 2, GLA_KW // 4).transpose(1, 2, 0, 3).reshape(GATE_RANK, GLA_KW)

    w_in_part = lax.dynamic_slice_in_dim(w_in[0].T, ci * (D_MODEL // 2), D_MODEL // 2, axis=1).astype(BF16)
    loc = _device_step(x[0], loss_target[0], meta, norm_gain, w_in_part, wg_full, b_gate, ret_norm_gain,
                       gla_norm_gain,
                       branch_parts, final_norm_gain, ck)
    names = ("w_in", "w_branch_ret", "w_branch_gla", "w_out")
    full = [loc[nm] for nm in names]
    big_w = dict(w_in=w_in[0], w_branch_ret=w_branch_ret[0], w_branch_gla=w_branch_gla[0], w_out=w_out[0])
    big_m = dict(w_in=m_w_in[0], w_branch_ret=m_w_branch_ret[0], w_branch_gla=m_w_branch_gla[0], w_out=m_w_out[0])
    big_v = dict(w_in=v_w_in[0], w_branch_ret=v_w_branch_ret[0], w_branch_gla=v_w_branch_gla[0], w_out=v_w_out[0])
    grads, deltas, new_m, new_v = {}, {}, {}, {}
    for nm, f in zip(names, full):
        shape = big_w[nm].shape
        if nm == "w_in":
            lane0 = ((sw_in - WIN_STEP) * kme).astype(jnp.int32).reshape(1)
            m_t, v_t = lax.optimization_barrier(big_m[nm].T), lax.optimization_barrier(big_v[nm].T)
            g, d, m2, v2 = (a.T for a in _adamw_window_call(
                "adamw_" + nm, big_w[nm].T, f.reshape(shape[0], WIN_W), m_t, v_t, lane0))
        else:
            g = f.reshape(shape)
            d, m2, v2 = _adamw_call("adamw_" + nm, big_w[nm], g, big_m[nm], big_v[nm])
        grads[nm], deltas[nm], new_m[nm], new_v[nm] = (a.reshape((1,) + shape) for a in (g, d, m2, v2))

    tot = _sum8_call("sum_small_grads", loc["small"]).reshape(-1)
    off = 0
    sg = {}
    for nm, sz in SMALL:
        sg[nm] = tot[off:off + sz]
        off += sz
    loss = sg.pop("loss")[0]
    sg["w_gate_up"] = lax.dynamic_slice_in_dim(sg["w_gate_up"].reshape(GATE_RANK, GLA_KW), kme * (GLA_KW // 4),
                                               GLA_KW // 4, axis=1)
    sg["meta_tokens"] = lax.dynamic_slice_in_dim(sg["meta_tokens"].reshape(N_META, D_MODEL), kme * (D_MODEL // 4),
                                                 D_MODEL // 4, axis=1)
    small_w = dict(norm_gain=norm_gain, b_gate=b_gate, ret_norm_gain=ret_norm_gain, gla_norm_gain=gla_norm_gain,
                   final_norm_gain=final_norm_gain, w_gate_up=w_gate_up, meta_tokens=meta_tokens)
    small_m = dict(norm_gain=m_norm_gain, b_gate=m_b_gate, ret_norm_gain=m_ret_norm_gain,
                   gla_norm_gain=m_gla_norm_gain, final_norm_gain=m_final_norm_gain, w_gate_up=m_w_gate_up,
                   meta_tokens=m_meta_tokens)
    small_v = dict(norm_gain=v_norm_gain, b_gate=v_b_gate, ret_norm_gain=v_ret_norm_gain,
                   gla_norm_gain=v_gla_norm_gain, final_norm_gain=v_final_norm_gain, w_gate_up=v_w_gate_up,
                   meta_tokens=v_meta_tokens)
    for nm in small_w:
        shape = small_w[nm].shape
        as2d = lambda a: a.reshape((-1, shape[-1]))
        grads[nm] = sg[nm].reshape(shape)
        deltas[nm], new_m[nm], new_v[nm] = (a.reshape(shape) for a in _adamw_call(
            "adamw_" + nm, as2d(small_w[nm]), as2d(sg[nm]), as2d(small_m[nm]), as2d(small_v[nm])))

    out_order = ("meta_tokens", "norm_gain", "w_in", "w_gate_up", "b_gate", "ret_norm_gain", "gla_norm_gain",
                 "w_branch_ret", "w_branch_gla", "w_out", "final_norm_gain")
    dx = loc["dx"].reshape(x.shape)
    return (loss, dx, *[grads[nm] for nm in out_order], *[deltas[nm] for nm in out_order],
            *[new_m[nm] for nm in out_order], *[new_v[nm] for nm in out_order])
```

```python
import math
from typing import Callable, NamedTuple

import numpy as np
import jax
import jax.numpy as jnp
from jax import lax
from jax.experimental import pallas as pl
from jax.experimental.pallas import tpu as pltpu

F32 = jnp.float32
BF16 = jnp.bfloat16

D_MODEL = 1024
N_META = 16
EPS = 1e-6
ROPE_BASE = 10000.0
RET_HEADS, RET_QK, RET_V = 4, 256, 512
RET_W = RET_HEADS * RET_V
GLA_HEADS, GLA_K, GLA_V = 4, 128, 256
GLA_W = GLA_HEADS * GLA_V
GLA_KW = GLA_HEADS * GLA_K
GATE_RANK = 16
GATE_TAU = 16.0
GLA_SUB = 16

TM = 256
T0 = TM
PADF = T0 - N_META
GC = 128
GS = 3
TB = 768
TK = 768

W_R = 6144
W_G = 3088
W_GP = 3200
W_M = 2048
IN_COLS = W_R + W_G + W_M
WIN_STEP = (IN_COLS // 4) // 128 * 128
WIN_W = -(-(3 * (IN_COLS // 4 - WIN_STEP) + IN_COLS // 4) // 128) * 128
IN_PAD = 3 * WIN_STEP + WIN_W

ADAM_LR, ADAM_B1, ADAM_B2, ADAM_EPS, ADAM_WD, ADAM_STEP = 0.001, 0.9, 0.999, 1e-08, 0.01, 10

VMEM_LIMIT = 56 * 1024 * 1024

NN = ((1,), (0,))
NT = ((1,), (1,))
TN = ((0,), (0,))


def _dot(a, b, dims):
    return lax.dot_general(a, b, (dims, ((), ())), preferred_element_type=F32)


def _cparams(n_axes):
    return pltpu.CompilerParams(dimension_semantics=("arbitrary",) * n_axes, vmem_limit_bytes=VMEM_LIMIT)


def _sigmoid(x):
    return 0.5 * jnp.tanh(0.5 * x) + 0.5


def _silu(x):
    h = 0.5 * x
    return h + h * jnp.tanh(h)


def _head_mean(x):
    return jnp.mean(x, axis=-1, keepdims=True)


def _split3(x):
    hi = x.astype(BF16)
    r1 = x - hi.astype(F32)
    mid = r1.astype(BF16)
    lo = (r1 - mid.astype(F32)).astype(BF16)
    return hi, mid, lo


def _exact_pm(p, x):
    hi, mid, lo = _split3(x)
    return _dot(p, hi, NN) + _dot(p, mid, NN) + _dot(p, lo, NN)


def _rms_call(x2d, head, gain, comm):
    tp = T0 + x2d.shape[0]
    nt = tp // TM
    n_xc = len(comm.srcs)

    def body(x_ref, hd_ref, g_ref, *rest):
        xc_src = rest[:n_xc]
        h_ref, u_ref = rest[n_xc:n_xc + 2]
        xc_dst = rest[n_xc + 2:2 * n_xc + 2]
        i = pl.program_id(0)
        begin, finish = comm.make(xc_src, xc_dst, rest[-2], rest[-1])
        pl.when(i == 0)(begin)
        h = jnp.where(i == 0, hd_ref[...], x_ref[...])
        h_ref[...] = h
        r = lax.rsqrt(jnp.mean(h * h, axis=-1, keepdims=True) + EPS)
        u_ref[...] = (h * r * g_ref[...]).astype(BF16)
        pl.when(i == nt - 1)(finish)

    tile = pl.BlockSpec((TM, D_MODEL), lambda i: (i, 0))
    return pl.pallas_call(
        body, name="rms_in", grid=(nt,),
        in_specs=[pl.BlockSpec((TM, D_MODEL), lambda i: (jnp.maximum(i - 1, 0), 0)),
                  pl.BlockSpec((T0, D_MODEL), lambda i: (0, 0)), pl.BlockSpec((1, D_MODEL), lambda i: (0, 0))]
        + [ANY] * n_xc,
        out_specs=[tile, tile] + [ANY] * n_xc,
        out_shape=[jax.ShapeDtypeStruct((tp, D_MODEL), F32), jax.ShapeDtypeStruct((tp, D_MODEL), BF16)]
        + list(comm.out_shapes),
        scratch_shapes=_comm_sems(comm), compiler_params=_cparams(1),
    )(x2d, head, gain, *comm.srcs)


PROJ_ROWS_MAX = 1408


def _proj_rows(m):
    return max(r for r in range(16, PROJ_ROWS_MAX + 1, 16) if m % r == 0)


def _mm_nn(name, a, bt, out_dtype, tn, col0, ncols, epilogue=None, extras=(), extra_specs=()):
    m, k = a.shape
    nj, j0 = ncols // tn, col0 // tn
    tb = _proj_rows(m)

    def body(a_ref, b_ref, *rest):
        *ex, o_ref = rest
        acc = _dot(a_ref[...], b_ref[...], NT)
        if epilogue is None:
            o_ref[...] = acc.astype(out_dtype)
        else:
            epilogue(acc, o_ref, *ex)

    return pl.pallas_call(
        body, name=name, grid=(nj, m // tb),
        in_specs=[pl.BlockSpec((tb, k), lambda j, i: (i, 0)), pl.BlockSpec((tn, k), lambda j, i: (j0 + j, 0))]
        + list(extra_specs),
        out_specs=pl.BlockSpec((tb, tn), lambda j, i: (i, j)),
        out_shape=jax.ShapeDtypeStruct((m, ncols), out_dtype),
        compiler_params=_cparams(2),
    )(a, bt, *extras)


def _rope_tables(tp):
    half = RET_QK // 2
    pos = np.arange(tp, dtype=np.float32) - np.float32(PADF)
    inv = (ROPE_BASE ** (-np.arange(half, dtype=np.float64) / half)).astype(np.float32)
    ang = (pos[:, None] * inv[None, :]).astype(np.float64)
    return np.cos(ang).astype(np.float32), np.sin(ang).astype(np.float32)


def _rope_epilogue(acc, o_ref, cos_ref, sin_ref):
    scale = jnp.where(pl.program_id(0) == 1, RET_QK ** -0.5, 1.0).astype(F32)
    cos, sin = cos_ref[...], sin_ref[...]
    half = RET_QK // 2
    for h in range(RET_HEADS):
        t1 = acc[:, h * RET_QK:h * RET_QK + half]
        t2 = acc[:, h * RET_QK + half:(h + 1) * RET_QK]
        o_ref[:, h * RET_QK:h * RET_QK + half] = ((t1 * cos - t2 * sin) * scale).astype(BF16)
        o_ref[:, h * RET_QK + half:(h + 1) * RET_QK] = ((t2 * cos + t1 * sin) * scale).astype(BF16)


def _gqk_epilogue(acc, o_ref):
    o_ref[:, :GLA_KW] = acc[:, :GLA_KW] * (GLA_K ** -0.5)
    o_ref[:, GLA_KW:] = acc[:, GLA_KW:]


class _Comm(NamedTuple):
    srcs: tuple
    out_shapes: tuple
    n_sems: int
    make: Callable


def _comm_sems(comm):
    return [pltpu.SemaphoreType.DMA((comm.n_sems,)), pltpu.SemaphoreType.DMA((comm.n_sems,))]


def _start_wait(copies):
    def begin():
        for cp in copies:
            cp.start()

    def finish():
        for cp in copies:
            cp.wait()

    return begin, finish


def _other_chips(x, y):
    return [(1 - x, y), (x, 1 - y), (1 - x, 1 - y)]


def _gather_plan(parts, relay=()):
    n = len(parts)
    relay = tuple(relay) + (False,) * (n - len(relay))

    def make(x_refs, out_refs, send_sems, recv_sems):
        x, y, c = _place()
        me, sibling = (x, y, c), (x, y, 1 - c)
        xn, yn, dg = (1 - x, y), (x, 1 - y), (1 - x, 1 - y)

        def slot(t, px, py, pc, half=None):
            ref = out_refs[t].at[4 * px + 2 * py + pc]
            if half is None:
                return ref
            cols = ref.shape[-1] // 2
            return ref.at[:, pl.ds(half * cols, cols)]

        def copy(t, k, dst, to, src=None):
            return pltpu.make_async_remote_copy(
                src_ref=dst if src is None else src, dst_ref=dst, send_sem=send_sems.at[8 * t + k],
                recv_sem=recv_sems.at[8 * t + k], device_id=to, device_id_type=MESH)

        mine = [pltpu.make_async_copy(x_refs[t], slot(t, *me), send_sems.at[8 * n + t]) for t in range(n)]
        sent = []
        for t in range(n):
            sent.append(copy(t, 0, slot(t, *me), sibling, src=x_refs[t]))
            sent.append(copy(t, 1, slot(t, *me), (*xn, c), src=x_refs[t]))
            sent.append(copy(t, 2, slot(t, *me), (*yn, c), src=x_refs[t]))
            if not relay[t]:
                sent.append(copy(t, 3, slot(t, *me), (*dg, c), src=x_refs[t]))

        def begin():
            for cp in mine + sent:
                cp.start()

        def finish():
            later = []

            def start(cp):
                cp.start()
                later.append(cp)

            for t in range(n):
                copy(t, 2, slot(t, *yn, c), me).wait_recv()
                if relay[t]:
                    start(copy(t, 3, slot(t, *yn, c, half=0), (*xn, c)))
                start(copy(t, 6, slot(t, *yn, c), sibling))
            for t in range(n):
                copy(t, 1, slot(t, *xn, c), me).wait_recv()
                if relay[t]:
                    start(copy(t, 4, slot(t, *xn, c, half=1), (*yn, c)))
                start(copy(t, 5, slot(t, *xn, c), sibling))
            for t in range(n):
                if relay[t]:
                    copy(t, 3, slot(t, *dg, c, half=0), me).wait_recv()
                    copy(t, 4, slot(t, *dg, c, half=1), me).wait_recv()
                else:
                    copy(t, 3, slot(t, *dg, c), me).wait_recv()
                start(copy(t, 7, slot(t, *dg, c), sibling))
            for t in range(n):
                copy(t, 0, slot(t, *sibling), me).wait_recv()
                copy(t, 5, slot(t, *xn, 1 - c), me).wait_recv()
                copy(t, 6, slot(t, *yn, 1 - c), me).wait_recv()
                copy(t, 7, slot(t, *dg, 1 - c), me).wait_recv()
            for cp in sent + later:
                cp.wait_send()
            for cp in mine:
                cp.wait()

        return begin, finish

    return _Comm(tuple(parts), tuple(jax.ShapeDtypeStruct((8,) + p.shape, p.dtype) for p in parts), 9 * n, make)


def _exchange_plan(ss):
    def make(s_refs, b_refs, send_sems, recv_sems):
        x, y, c = _place()
        return _start_wait([pltpu.make_async_remote_copy(
            src_ref=s_refs[t].at[2 * chip[0] + chip[1]], dst_ref=b_refs[t].at[j], send_sem=send_sems.at[3 * t + j],
            recv_sem=recv_sems.at[3 * t + j], device_id=(*chip, c), device_id_type=MESH)
            for t in range(len(s_refs)) for j, chip in enumerate(_other_chips(x, y))])

    return _Comm(tuple(ss), tuple(jax.ShapeDtypeStruct((3,) + s.shape[1:], s.dtype) for s in ss), 3 * len(ss), make)


def _exchange_window_plan(s):
    def make(s_refs, b_refs, send_sems, recv_sems):
        x, y, c = _place()
        return _start_wait([pltpu.make_async_remote_copy(
            src_ref=s_refs[0].at[:, pl.ds(pl.multiple_of((2 * chip[0] + chip[1]) * WIN_STEP, 128), WIN_W)],
            dst_ref=b_refs[0].at[j], send_sem=send_sems.at[j], recv_sem=recv_sems.at[j], device_id=(*chip, c),
            device_id_type=MESH) for j, chip in enumerate(_other_chips(x, y))])

    return _Comm((s,), (jax.ShapeDtypeStruct((3, s.shape[0], WIN_W), s.dtype),), 3, make)


def _swap_plan(gs):
    def make(g_refs, b_refs, send_sems, recv_sems):
        x, y, c = _place()
        return _start_wait([pltpu.make_async_remote_copy(
            src_ref=g_refs[t].at[1 - c], dst_ref=b_refs[t], send_sem=send_sems.at[t], recv_sem=recv_sems.at[t],
            device_id=(x, y, 1 - c), device_id_type=MESH) for t in range(len(g_refs))])

    return _Comm(tuple(gs), tuple(jax.ShapeDtypeStruct(g.shape[1:], g.dtype) for g in gs), len(gs), make)


def _spread_plan(parts):
    def make(p_refs, o_refs, send_sems, recv_sems):
        x, y, c = _place()
        copies = []
        for t in range(len(p_refs)):
            mine = o_refs[t].at[4 * x + 2 * y + c]
            copies.append(pltpu.make_async_copy(p_refs[t], mine, send_sems.at[7 * len(p_refs) + t]))
            for r in range(1, 8):
                peer = (1 - x if r & 4 else x, 1 - y if r & 2 else y, 1 - c if r & 1 else c)
                copies.append(pltpu.make_async_remote_copy(
                    src_ref=p_refs[t], dst_ref=mine, send_sem=send_sems.at[7 * t + r - 1],
                    recv_sem=recv_sems.at[7 * t + r - 1], device_id=peer, device_id_type=MESH))
        return _start_wait(copies)

    return _Comm(tuple(parts), tuple(jax.ShapeDtypeStruct((8,) + p.shape, p.dtype) for p in parts), 8 * len(parts),
                 make)


def _mm_nt_acc(name, a, w, tk, acc_in=None, epilogue=None, extras=(), extra_specs=(), extra_out_shapes=(),
               extra_out_specs=(), extra_scratch=(), comm=None, tb=TB):
    m, k = a.shape
    n = w.shape[1]
    nk, ni = k // tk, m // tb
    has_acc = acc_in is not None
    n_xc = len(comm.srcs) if comm else 0
    n_es = len(extra_scratch)

    def body(*refs):
        a_ref, w_ref = refs[0], refs[1]
        pos = 2
        acc_ref = None
        if has_acc:
            acc_ref = refs[pos]
            pos += 1
        ex = refs[pos:pos + len(extras)]
        pos += len(extras)
        xc_src = refs[pos:pos + n_xc]
        pos += n_xc
        n_scr = 1 + n_es + (2 if n_xc else 0)
        outs = refs[pos:len(refs) - n_scr - n_xc]
        xc_dst = refs[len(refs) - n_scr - n_xc:len(refs) - n_scr]
        scr = refs[len(refs) - n_scr]
        es = refs[len(refs) - n_scr + 1:len(refs) - n_scr + 1 + n_es]
        i, kk = pl.program_id(0), pl.program_id(1)
        if n_xc:
            begin, finish = comm.make(xc_src, xc_dst, refs[-2], refs[-1])
            pl.when((i == 0) & (kk == 0))(begin)

        @pl.when(kk == 0)
        def _():
            scr[...] = acc_ref[...] if has_acc else jnp.zeros_like(scr)

        scr[...] += _dot(a_ref[...], w_ref[...], NN)

        @pl.when(kk == nk - 1)
        def _():
            if epilogue is None:
                outs[0][...] = scr[...]
            else:
                epilogue(scr[...], outs, i, ni, *ex, *es)

        if n_xc:
            pl.when((i == ni - 1) & (kk == nk - 1))(finish)

    in_specs = [pl.BlockSpec((tb, tk), lambda i, kk: (i, kk)), pl.BlockSpec((tk, n), lambda i, kk: (kk, 0))]
    args = [a, w]
    if has_acc:
        in_specs.append(pl.BlockSpec((tb, n), lambda i, kk: (i, 0)))
        args.append(acc_in)
    in_specs += list(extra_specs) + [ANY] * n_xc
    args += list(extras) + (list(comm.srcs) if comm else [])
    if epilogue is None:
        out_shape = [jax.ShapeDtypeStruct((m, n), F32)]
        out_specs = [pl.BlockSpec((tb, n), lambda i, kk: (i, 0))]
    else:
        out_shape, out_specs = list(extra_out_shapes), list(extra_out_specs)
    scratch = [pltpu.VMEM((tb, n), F32)] + list(extra_scratch)
    if n_xc:
        out_shape += list(comm.out_shapes)
        out_specs += [ANY] * n_xc
        scratch += _comm_sems(comm)
    return pl.pallas_call(
        body, name=name, grid=(ni, nk), in_specs=in_specs, out_specs=out_specs, out_shape=out_shape,
        scratch_shapes=scratch, compiler_params=_cparams(2),
    )(*args)


def _rms_bwd_epilogue(du, outs, i, ni, h_ref, g_ref, dh1_ref, obuf, sems):
    dx_ref, dmeta_ref, dg_ref = outs
    h = h_ref[...]
    r = lax.rsqrt(jnp.mean(h * h, axis=-1, keepdims=True) + EPS)
    xh = h * r
    dxh = du * g_ref[...]
    dh0 = dh1_ref[...] + r * (dxh - xh * jnp.mean(dxh * xh, axis=-1, keepdims=True))

    def put(slot, tile):
        return pltpu.make_async_copy(obuf.at[slot], dx_ref.at[pl.ds(pl.multiple_of(tile * TB - T0, 8), TB)],
                                     sems.at[slot])

    @pl.when(i == 0)
    def _():
        dg_ref[...] = jnp.zeros_like(dg_ref)
        dmeta_ref[...] = dh0[PADF:T0, :]
        obuf[0] = dh0
        first = pltpu.make_async_copy(obuf.at[0, pl.ds(T0, TB - T0)], dx_ref.at[pl.ds(0, TB - T0)], sems.at[0])
        first.start()
        first.wait()

    @pl.when(i >= 1)
    def _():
        slot = i % 2

        @pl.when(i >= 3)
        def _():
            put(slot, i - 2).wait()

        obuf[slot] = dh0
        put(slot, i).start()

    dg_ref[...] += jnp.sum(du * xh, axis=0, keepdims=True)

    @pl.when(i == ni - 1)
    def _():
        for tile in (ni - 2, ni - 1):
            if tile >= 1:
                put(tile % 2, tile).wait()


def _mm_tn(name, a, b, bn, ncols=None, bcol0=0, into=None, col0=0, out_cols=None):
    t, m = a.shape
    n = ncols or b.shape[1]
    j0, bj0 = col0 // bn, bcol0 // bn

    def body(a_ref, b_ref, *rest):
        o_ref = rest[-1]

        @pl.when(pl.program_id(1) == 0)
        def _():
            o_ref[...] = jnp.zeros_like(o_ref)

        o_ref[...] += _dot(a_ref[...], b_ref[...], TN)

    in_specs = [pl.BlockSpec((TK, m), lambda j, kk: (kk, 0)), pl.BlockSpec((TK, bn), lambda j, kk: (kk, bj0 + j))]
    args = [a, b]
    aliases = {}
    if into is not None:
        in_specs.append(ANY)
        args.append(into)
        aliases = {2: 0}
        out_cols = into.shape[1]
    return pl.pallas_call(
        body, name=name, grid=(n // bn, t // TK), in_specs=in_specs,
        out_specs=pl.BlockSpec((m, bn), lambda j, kk: (0, j0 + j)),
        out_shape=jax.ShapeDtypeStruct((m, out_cols or n), F32), input_output_aliases=aliases,
        compiler_params=_cparams(2),
    )(*args)


def _assemble_call(g_in):
    nblk, s, h = g_in.shape
    rows, slots = (nblk // 2) * s, 4

    def body(g_ref, o_ref, ibuf, obuf, isem, osem):
        def fetch(b):
            return pltpu.make_async_copy(g_ref.at[b], ibuf.at[b % slots], isem.at[b % slots])

        o32 = obuf.bitcast(jnp.uint32)
        for b in range(slots):
            fetch(b).start()
        done, puts = 0, []
        for b in range(nblk):
            k, c = divmod(b, 2)
            fetch(b).wait()
            o32[pl.ds(k * s // 2, s // 2), pl.ds(c * h, h)] = ibuf.at[b % slots].bitcast(jnp.uint32)[...]
            if b + slots < nblk:
                fetch(b + slots).start()
            if c == 1:
                hi = rows if b == nblk - 1 else (k + 1) * s // 16 * 16
                puts.append(pltpu.make_async_copy(obuf.at[pl.ds(done, hi - done)], o_ref.at[pl.ds(done, hi - done)],
                                                  osem.at[k]))
                puts[-1].start()
                done = hi
        for cp in puts:
            cp.wait()

    return pl.pallas_call(
        body, name="assemble_w_in", in_specs=[pl.BlockSpec(memory_space=pl.ANY)],
        out_specs=pl.BlockSpec(memory_space=pl.ANY), out_shape=jax.ShapeDtypeStruct((rows, 2 * h), g_in.dtype),
        scratch_shapes=[pltpu.VMEM((slots, s, h), g_in.dtype), pltpu.VMEM((rows, 2 * h), g_in.dtype),
                        pltpu.SemaphoreType.DMA((slots,)), pltpu.SemaphoreType.DMA((nblk // 2,))],
        compiler_params=pltpu.CompilerParams(vmem_limit_bytes=VMEM_LIMIT),
    )(g_in)


def _place_merge_cols_call(dwp, dw_m, dw_glr):
    c0 = W_R + W_GP - 128
    tail = IN_PAD - c0
    rows = 256

    def body(m_ref, low, p_ref, o_ref, buf, sem):
        for r in range(0, D_MODEL, rows):
            buf[r:r + rows, :] = jnp.concatenate(
                [low[r:r + rows, :GATE_RANK], m_ref[r:r + rows, :],
                 jnp.zeros((rows, tail - GATE_RANK - W_M), F32)], axis=1)
        put = pltpu.make_async_copy(buf, o_ref.at[:, pl.ds(c0, tail)], sem)
        put.start()
        put.wait()

    return pl.pallas_call(
        body, name="place_merge_cols",
        in_specs=[pl.BlockSpec(memory_space=pltpu.VMEM), pl.BlockSpec(memory_space=pltpu.VMEM), ANY], out_specs=ANY,
        out_shape=jax.ShapeDtypeStruct(dwp.shape, F32), input_output_aliases={2: 0},
        scratch_shapes=[pltpu.VMEM((D_MODEL, tail), F32), pltpu.SemaphoreType.DMA],
        compiler_params=pltpu.CompilerParams(vmem_limit_bytes=VMEM_LIMIT),
    )(dw_m, dw_glr, dwp)


def _ret_fill_decay(lg_ref, dm_scr):
    c = TM
    ii = lax.broadcasted_iota(jnp.int32, (c, c), 0)
    jj = lax.broadcasted_iota(jnp.int32, (c, c), 1)
    rel = (ii - jj).astype(F32)
    for h in range(RET_HEADS):
        dm_scr[h] = jnp.where(rel >= 0, jnp.exp(jnp.maximum(rel, 0.0) * lg_ref[h]), 0.0)


def _ret_consts(lg, dm_ref):
    c = TM
    idx = lax.broadcasted_iota(jnp.int32, (c, 1), 0).astype(F32)
    xi = jnp.exp((idx + 1.0) * lg)
    zeta = jnp.exp((c - 1.0 - idx) * lg)
    gc = jnp.exp(jnp.full((1, 1), c, F32) * lg)
    return dm_ref[...], xi, zeta, gc


def _ret_fwd_call(rqk, rv, rg, gain, lgam):
    tp = rqk.shape[0]
    nc = tp // TM

    def body(lg_ref, qk_ref, v_ref, rg_ref, g_ref, o_ref, a_ref, st_ref, sc_ref, s_scr, dm_scr):
        @pl.when(pl.program_id(0) == 0)
        def _():
            s_scr[...] = jnp.zeros_like(s_scr)
            _ret_fill_decay(lg_ref, dm_scr)

        for h in range(RET_HEADS):
            dm, xi, zeta, gc = _ret_consts(lg_ref[h], dm_scr.at[h])
            q = qk_ref[:, h * RET_QK:(h + 1) * RET_QK]
            k = qk_ref[:, D_MODEL + h * RET_QK:D_MODEL + (h + 1) * RET_QK]
            v = v_ref[:, h * RET_V:(h + 1) * RET_V]
            sb = s_scr[h].astype(BF16)
            st_ref[0, h] = sb
            s = (_dot(q, k, NT) * dm).astype(BF16)
            sc_ref[0, h] = s
            o = _dot(s, v, NN) + xi * _dot(q, sb, NN)
            kz = (k.astype(F32) * zeta).astype(BF16)
            s_scr[h] = gc * s_scr[h] + _dot(kz, v, TN)
            o_ref[:, h * RET_V:(h + 1) * RET_V] = o
            mu = _head_mean(o)
            xc = o - mu
            xh = xc * lax.rsqrt(_head_mean(xc * xc) + EPS)
            a_ref[:, h * RET_V:(h + 1) * RET_V] = (
                xh * g_ref[:, h * RET_V:(h + 1) * RET_V] * _silu(rg_ref[:, h * RET_V:(h + 1) * RET_V])).astype(BF16)

    return pl.pallas_call(
        body, name="ret_fwd", grid=(nc,),
        in_specs=[pl.BlockSpec(memory_space=pltpu.SMEM),
                  pl.BlockSpec((TM, 2 * D_MODEL), lambda n: (n, 0)),
                  pl.BlockSpec((TM, RET_W), lambda n: (n, 0)),
                  pl.BlockSpec((TM, RET_W), lambda n: (n, 0)),
                  pl.BlockSpec((1, RET_W), lambda n: (0, 0))],
        out_specs=[pl.BlockSpec((TM, RET_W), lambda n: (n, 0)),
                   pl.BlockSpec((TM, RET_W), lambda n: (n, 0)),
                   pl.BlockSpec((1, RET_HEADS, RET_QK, RET_V), lambda n: (n, 0, 0, 0)),
                   pl.BlockSpec((1, RET_HEADS, TM, TM), lambda n: (n, 0, 0, 0))],
        out_shape=[jax.ShapeDtypeStruct((tp, RET_W), F32), jax.ShapeDtypeStruct((tp, RET_W), BF16),
                   jax.ShapeDtypeStruct((nc, RET_HEADS, RET_QK, RET_V), BF16),
                   jax.ShapeDtypeStruct((nc, RET_HEADS, TM, TM), BF16)],
        scratch_shapes=[pltpu.VMEM((RET_HEADS, RET_QK, RET_V), F32), pltpu.VMEM((RET_HEADS, TM, TM), F32)],
        compiler_params=_cparams(1),
    )(lgam, rqk, rv, rg, gain)


def _ret_bwd_call(rqk, rv, rg, o_ret, dpr, wbr, states, scores, gain, lgam, cos, sin):
    tp = rqk.shape[0]
    nc = tp // TM
    half = RET_QK // 2

    def body(lg_ref, qk_ref, v_ref, rg_ref, o_ref, dpr_ref, wbr_ref, st_ref, sc_ref, g_ref, cos_ref, sin_ref, dp_ref,
             dg_ref, ds_scr, dm_scr):
        @pl.when(pl.program_id(0) == 0)
        def _():
            ds_scr[...] = jnp.zeros_like(ds_scr)
            dg_ref[...] = jnp.zeros_like(dg_ref)
            _ret_fill_decay(lg_ref, dm_scr)

        cos, sin = cos_ref[...], sin_ref[...]
        for h in range(RET_HEADS):
            hs = slice(h * RET_V, (h + 1) * RET_V)
            dm, xi, zeta, gc = _ret_consts(lg_ref[h], dm_scr.at[h])
            o = o_ref[:, hs]
            mu = _head_mean(o)
            xc = o - mu
            rstd = lax.rsqrt(_head_mean(xc * xc) + EPS)
            xh = xc * rstd
            gain_h = g_ref[:, hs]
            g = rg_ref[:, hs]
            sg = _sigmoid(g)
            silu = g * sg
            dah = _dot(dpr_ref[...], wbr_ref[hs, :], NT)
            dp_ref[:, 4 * D_MODEL + h * RET_V:4 * D_MODEL + (h + 1) * RET_V] = (
                dah * (xh * gain_h) * (sg * (1.0 + g * (1.0 - sg)))).astype(BF16)
            dn = dah * silu
            dg_ref[:, hs] += jnp.sum(dn * xh, axis=0, keepdims=True)
            dxh = dn * gain_h
            do = rstd * (dxh - _head_mean(dxh) - xh * _head_mean(dxh * xh))
            dob = do.astype(BF16)
            q = qk_ref[:, h * RET_QK:(h + 1) * RET_QK]
            k = qk_ref[:, D_MODEL + h * RET_QK:D_MODEL + (h + 1) * RET_QK]
            v = v_ref[:, hs]
            sp = st_ref[0, h]
            ds = ds_scr[h]
            dsb = ds.astype(BF16)
            s = sc_ref[0, h]
            dsc = (_dot(dob, v, NT) * dm).astype(BF16)
            dq = _dot(dsc, k, NN) + xi * _dot(dob, sp, NT)
            dk = _dot(dsc, q, TN) + zeta * _dot(v, dsb, NT)
            kz = (k.astype(F32) * zeta).astype(BF16)
            dv = _dot(s, dob, TN) + _dot(kz, dsb, NN)
            qx = (q.astype(F32) * xi).astype(BF16)
            ds_scr[h] = gc * ds + _dot(qx, dob, TN)
            dp_ref[:, 2 * D_MODEL + h * RET_V:2 * D_MODEL + (h + 1) * RET_V] = dv.astype(BF16)
            dk = dk * (RET_QK ** -0.5)
            for base, t in ((0, dq), (D_MODEL, dk)):
                t1, t2 = t[:, :half], t[:, half:]
                dp_ref[:, base + h * RET_QK:base + h * RET_QK + half] = (t1 * cos + t2 * sin).astype(BF16)
                dp_ref[:, base + h * RET_QK + half:base + (h + 1) * RET_QK] = (t2 * cos - t1 * sin).astype(BF16)

    rev = lambda n: (nc - 1 - n, 0)
    return pl.pallas_call(
        body, name="ret_bwd", grid=(nc,),
        in_specs=[pl.BlockSpec(memory_space=pltpu.SMEM),
                  pl.BlockSpec((TM, 2 * D_MODEL), rev),
                  pl.BlockSpec((TM, RET_W), rev),
                  pl.BlockSpec((TM, RET_W), rev),
                  pl.BlockSpec((TM, RET_W), rev),
                  pl.BlockSpec((TM, D_MODEL), rev),
                  pl.BlockSpec((RET_W, D_MODEL), lambda n: (0, 0)),
                  pl.BlockSpec((1, RET_HEADS, RET_QK, RET_V), lambda n: (nc - 1 - n, 0, 0, 0)),
                  pl.BlockSpec((1, RET_HEADS, TM, TM), lambda n: (nc - 1 - n, 0, 0, 0)),
                  pl.BlockSpec((1, RET_W), lambda n: (0, 0)),
                  pl.BlockSpec((TM, half), rev),
                  pl.BlockSpec((TM, half), rev)],
        out_specs=[pl.BlockSpec((TM, W_R), rev), pl.BlockSpec((1, RET_W), lambda n: (0, 0))],
        out_shape=[jax.ShapeDtypeStruct((tp, W_R), BF16), jax.ShapeDtypeStruct((1, RET_W), F32)],
        scratch_shapes=[pltpu.VMEM((RET_HEADS, RET_QK, RET_V), F32), pltpu.VMEM((RET_HEADS, TM, TM), F32)],
        compiler_params=_cparams(1),
    )(lgam, rqk, rv, rg, o_ret, dpr, wbr, states, scores, gain, cos, sin)


GLA_LEVELS = tuple(GC >> (s + 1) for s in range(int(math.log2(GC // GLA_SUB))))
NLEV = len(GLA_LEVELS)


def _gla_tril():
    return np.tril(np.ones((GC, GC), np.float32))


def _gla_masks():
    ii = lax.broadcasted_iota(jnp.int32, (GC, GC), 0)
    jj = lax.broadcasted_iota(jnp.int32, (GC, GC), 1)
    masks = []
    for m in GLA_LEVELS:
        sh = int(math.log2(2 * m))
        masks.append(((ii >> sh) == (jj >> sh)) & ((ii & m) != 0) & ((jj & m) == 0))
    sh = int(math.log2(GLA_SUB))
    md = ((ii >> sh) == (jj >> sh)) & (jj <= ii)
    row = lax.broadcasted_iota(jnp.int32, (GC, 1), 0)
    second = [(row & m) != 0 for m in GLA_LEVELS]
    return masks, md, second


def _gla_gate_call(u, w_g, wg, bg, pmat):
    tp = u.shape[0]
    gb = _proj_rows(tp)
    assert gb % GC == 0

    def body(u_ref, w_ref, wg_ref, bg_ref, p_ref, glr_ref, z_ref, b_ref):
        glr = _dot(u_ref[...], w_ref[...], NT)
        glr_ref[...] = glr
        z = _dot(glr.astype(BF16), wg_ref[...], NN) + bg_ref[...]
        z_ref[...] = z
        la = (jnp.minimum(z, 0.0) - jnp.log1p(jnp.exp(-jnp.abs(z)))) * (1.0 / GATE_TAU)
        for r in range(0, gb, GC):
            b_ref[r:r + GC, :] = _exact_pm(p_ref[...], la[r:r + GC, :])

    tile = pl.BlockSpec((gb, GLA_KW), lambda i: (i, 0))
    return pl.pallas_call(
        body, name="gla_gate", grid=(tp // gb,),
        in_specs=[pl.BlockSpec((gb, D_MODEL), lambda i: (i, 0)),
                  pl.BlockSpec((128, D_MODEL), lambda i: ((W_GP - 128) // 128, 0)),
                  pl.BlockSpec((128, GLA_KW), lambda i: (0, 0)),
                  pl.BlockSpec((1, GLA_KW), lambda i: (0, 0)), pl.BlockSpec((GC, GC), lambda i: (0, 0))],
        out_specs=[pl.BlockSpec((gb, 128), lambda i: (i, 0)), tile, tile],
        out_shape=[jax.ShapeDtypeStruct((tp, 128), F32), jax.ShapeDtypeStruct((tp, GLA_KW), F32),
                   jax.ShapeDtypeStruct((tp, GLA_KW), F32)],
        compiler_params=_cparams(1),
    )(u, w_g, wg, bg, pmat)


def _gla_gate_bwd_call(db, z, glr, u, wg, pmat_t, d_g):
    tp = db.shape[0]
    gb = _proj_rows(tp)
    assert gb % GC == 0 and (W_GP - 128) % 128 == 0

    def body(db_ref, z_ref, glr_ref, u_ref, wg_ref, pt_ref, dgin_ref, dg_ref, dwg_ref, dbg_ref, dwl_ref):
        i = pl.program_id(0)

        @pl.when(i == 0)
        def _():
            dwg_ref[...] = jnp.zeros_like(dwg_ref)
            dbg_ref[...] = jnp.zeros_like(dbg_ref)
            dwl_ref[...] = jnp.zeros_like(dwl_ref)

        dla = jnp.concatenate([_exact_pm(pt_ref[...], db_ref[r:r + GC, :]) for r in range(0, gb, GC)], axis=0)
        row = i * gb + lax.broadcasted_iota(jnp.int32, (gb, 1), 0)
        dz = jnp.where(row >= PADF, dla * (1.0 / GATE_TAU) * _sigmoid(-z_ref[...]), 0.0)
        dzb = dz.astype(BF16)
        dglr = _dot(dzb, wg_ref[...], NT).astype(BF16)
        dg_ref[...] = dglr
        dwg_ref[...] += _dot(glr_ref[...].astype(BF16), dzb, TN)
        dbg_ref[...] += jnp.sum(dz, axis=0, keepdims=True)
        dwl_ref[...] += _dot(u_ref[...], dglr, TN)

    tile = pl.BlockSpec((gb, GLA_KW), lambda i: (i, 0))
    const = lambda i: (0, 0)
    return pl.pallas_call(
        body, name="gla_gate_bwd", grid=(tp // gb,),
        in_specs=[tile, tile, pl.BlockSpec((gb, 128), lambda i: (i, 0)), pl.BlockSpec((gb, D_MODEL), lambda i: (i, 0)),
                  pl.BlockSpec((128, GLA_KW), const), pl.BlockSpec((GC, GC), const), ANY],
        out_specs=[pl.BlockSpec((gb, 128), lambda i: (i, (W_GP - 128) // 128)), pl.BlockSpec((128, GLA_KW), const),
                   pl.BlockSpec((1, GLA_KW), const), pl.BlockSpec((D_MODEL, 128), const)],
        out_shape=[jax.ShapeDtypeStruct(d_g.shape, BF16), jax.ShapeDtypeStruct((128, GLA_KW), F32),
                   jax.ShapeDtypeStruct((1, GLA_KW), F32), jax.ShapeDtypeStruct((D_MODEL, 128), F32)],
        input_output_aliases={6: 0}, compiler_params=_cparams(1),
    )(db, z, glr, u, wg, pmat_t, d_g)


def _gla_row_steps(b_ref, cs, rows, size):
    parts = [jnp.zeros((size, GLA_K), F32) if r is None else jnp.broadcast_to(b_ref[r:r + 1, cs], (size, GLA_K))
             for r in rows]
    return parts[0] if len(parts) == 1 else jnp.concatenate(parts, axis=0)


def _gla_factors(b_ref, h, second):
    cs = slice(h * GLA_K, (h + 1) * GLA_K)
    b = b_ref[:, cs]
    fq, fk = [], []
    for l, m in enumerate(GLA_LEVELS):
        d = b - _gla_row_steps(b_ref, cs, [s + m - 1 for s in range(0, GC, 2 * m)], 2 * m)
        f = jnp.exp(jnp.where(second[l], d, -d))
        fq.append(jnp.where(second[l], f, 0.0))
        fk.append(jnp.where(second[l], 0.0, f))
    dd = b - _gla_row_steps(b_ref, cs, [None] + [s - 1 for s in range(GLA_SUB, GC, GLA_SUB)], GLA_SUB)
    ed = jnp.exp(dd)
    edi = jnp.exp(-dd)
    eb = jnp.exp(b)
    bl = b_ref[GC - 1:GC, cs]
    ee = jnp.exp(bl - b)
    ebl = jnp.exp(bl)
    return fq, fk, ed, edi, eb, ee, ebl


def _gla_scaled(q, k, fq, fk, ed, edi):
    qt = [(q * f).astype(BF16) for f in fq]
    kt = [(k * f).astype(BF16) for f in fk]
    return qt, kt, (q * ed).astype(BF16), (k * edi).astype(BF16)


def _gla_scores(qt, kt, qd, kd, masks, md):
    a = jnp.where(md, _dot(qd, kd, NT), 0.0)
    for l in range(NLEV):
        a = a + jnp.where(masks[l], _dot(qt[l], kt[l], NT), 0.0)
    return a.astype(BF16)


def _gla_fwd_call(gqk, gv, b, gg, gain, comm=None):
    tp = gqk.shape[0]
    nc = tp // GC
    ns = nc // GS
    n_xc = len(comm.srcs) if comm else 0

    def body(qk_ref, v_ref, b_ref, gg_ref, g_ref, *rest):
        xc_src = rest[:n_xc]
        o_ref, a_ref, st_ref, am_ref = rest[n_xc:n_xc + 4]
        xc_dst = rest[n_xc + 4:2 * n_xc + 4]
        s_scr = rest[2 * n_xc + 4]
        n = pl.program_id(0)
        if n_xc:
            begin, finish = comm.make(xc_src, xc_dst, rest[-2], rest[-1])
            pl.when(n == 0)(begin)
            pl.when(n == ns - 1)(finish)

        @pl.when(n == 0)
        def _():
            s_scr[...] = jnp.zeros_like(s_scr)

        masks, md, second = _gla_masks()
        for cc in range(GS):
            rows = pl.ds(cc * GC, GC)
            qk_c, v_c, b_c, gg_c, o_c, a_c = (r.at[rows] for r in (qk_ref, v_ref, b_ref, gg_ref, o_ref, a_ref))
            for h in range(GLA_HEADS):
                q = qk_c[:, h * GLA_K:(h + 1) * GLA_K]
                k = qk_c[:, GLA_KW + h * GLA_K:GLA_KW + (h + 1) * GLA_K]
                vs = slice(h * GLA_V, (h + 1) * GLA_V)
                v = v_c[:, vs]
                fq, fk, ed, edi, eb, ee, ebl = _gla_factors(b_c, h, second)
                a = _gla_scores(*_gla_scaled(q, k, fq, fk, ed, edi), masks, md)
                am_ref[cc, h] = a
                sb = s_scr[h].astype(BF16)
                st_ref[cc, h] = sb
                o = _dot(a, v, NN) + _dot((q * eb).astype(BF16), sb, NT)
                s_scr[h] = s_scr[h] * ebl + _dot(v, (k * ee).astype(BF16), TN)
                o_c[:, vs] = o
                xh = o * lax.rsqrt(_head_mean(o * o) + EPS)
                a_c[:, vs] = (xh * g_ref[:, vs] * _silu(gg_c[:, vs])).astype(BF16)

    return pl.pallas_call(
        body, name="gla_fwd", grid=(ns,),
        in_specs=[pl.BlockSpec((GS * GC, 2 * GLA_KW), lambda n: (n, 0)),
                  pl.BlockSpec((GS * GC, GLA_W), lambda n: (n, 0)),
                  pl.BlockSpec((GS * GC, GLA_KW), lambda n: (n, 0)),
                  pl.BlockSpec((GS * GC, GLA_W), lambda n: (n, 0)),
                  pl.BlockSpec((1, GLA_W), lambda n: (0, 0))] + [ANY] * n_xc,
        out_specs=[pl.BlockSpec((GS * GC, GLA_W), lambda n: (n, 0)),
                   pl.BlockSpec((GS * GC, GLA_W), lambda n: (n, 0)),
                   pl.BlockSpec((GS, GLA_HEADS, GLA_V, GLA_K), lambda n: (n, 0, 0, 0)),
                   pl.BlockSpec((GS, GLA_HEADS, GC, GC), lambda n: (n, 0, 0, 0))] + [ANY] * n_xc,
        out_shape=[jax.ShapeDtypeStruct((tp, GLA_W), F32), jax.ShapeDtypeStruct((tp, GLA_W), BF16),
                   jax.ShapeDtypeStruct((nc, GLA_HEADS, GLA_V, GLA_K), BF16),
                   jax.ShapeDtypeStruct((nc, GLA_HEADS, GC, GC), BF16)] + (list(comm.out_shapes) if comm else []),
        scratch_shapes=[pltpu.VMEM((GLA_HEADS, GLA_V, GLA_K), F32)] + (_comm_sems(comm) if comm else []),
        compiler_params=_cparams(1),
    )(gqk, gv, b, gg, gain, *(comm.srcs if comm else ()))


def _gla_bwd_call(gqk, gv, b, gg, o_gla, da, states, scores, gain, comm=None):
    tp = gqk.shape[0]
    nc = tp // GC
    ns = nc // GS
    o_gv, o_gg = 2 * GLA_KW, 2 * GLA_KW + GLA_W
    n_xc = len(comm.srcs) if comm else 0

    def body(qk_all, v_all, b_all, gg_all, o_all, da_all, st_ref, am_ref, g_ref, *rest):
        xc_src = rest[:n_xc]
        dp_all, db_all, dg_ref = rest[n_xc:n_xc + 3]
        xc_dst = rest[n_xc + 3:2 * n_xc + 3]
        ds_scr = rest[2 * n_xc + 3]
        n = pl.program_id(0)
        if n_xc:
            begin, finish = comm.make(xc_src, xc_dst, rest[-2], rest[-1])
            pl.when(n == 0)(begin)
            pl.when(n == ns - 1)(finish)

        @pl.when(n == 0)
        def _():
            ds_scr[...] = jnp.zeros_like(ds_scr)
            dg_ref[...] = jnp.zeros_like(dg_ref)

        masks, md, second = _gla_masks()
        for cc, h in [(cc, h) for cc in reversed(range(GS)) for h in range(GLA_HEADS)]:
            rows = pl.ds(cc * GC, GC)
            qk_ref, v_ref, b_scr, gg_ref, o_ref, da_ref, dp_ref, db_scr = (
                r.at[rows] for r in (qk_all, v_all, b_all, gg_all, o_all, da_all, dp_all, db_all))
            cs = slice(h * GLA_K, (h + 1) * GLA_K)
            vs = slice(h * GLA_V, (h + 1) * GLA_V)
            o = o_ref[:, vs]
            rstd = lax.rsqrt(_head_mean(o * o) + EPS)
            xh = o * rstd
            gain_h = g_ref[:, vs]
            g = gg_ref[:, vs]
            sg = _sigmoid(g)
            dah = da_ref[:, vs]
            dp_ref[:, o_gg + h * GLA_V:o_gg + (h + 1) * GLA_V] = (
                dah * (xh * gain_h) * (sg * (1.0 + g * (1.0 - sg)))).astype(BF16)
            dn = dah * (g * sg)
            dg_ref[:, vs] += jnp.sum(dn * xh, axis=0, keepdims=True)
            dxh = dn * gain_h
            do = rstd * (dxh - xh * _head_mean(dxh * xh))
            dob = do.astype(BF16)
            q = qk_ref[:, cs]
            k = qk_ref[:, GLA_KW + h * GLA_K:GLA_KW + (h + 1) * GLA_K]
            v = v_ref[:, vs]
            fq, fk, ed, edi, eb, ee, ebl = _gla_factors(b_scr, h, second)
            qt, kt, qd, kd = _gla_scaled(q, k, fq, fk, ed, edi)
            sp = st_ref[cc, h]
            ds = ds_scr[h]
            dsb = ds.astype(BF16)
            q_in = q * eb
            k_end = k * ee
            da_s = _dot(dob, v, NT)
            dv = _dot(am_ref[cc, h], dob, TN) + _dot(k_end.astype(BF16), dsb, NT)
            dq_in = _dot(dob, sp, NN)
            dk_end = _dot(v, dsb, NN)
            dbl = jnp.sum(sp.astype(F32) * ds, axis=0, keepdims=True) * ebl
            ds_scr[h] = ds * ebl + _dot(dob, q_in.astype(BF16), TN)
            dq = dq_in * eb
            dk = dk_end * ee
            de_end = dk_end * k_end
            db = dq_in * q_in - de_end
            placed = [(GC - 1, jnp.sum(de_end, axis=0, keepdims=True) + dbl)]
            for l, m in enumerate(GLA_LEVELS):
                dal = jnp.where(masks[l], da_s, 0.0).astype(BF16)
                dqt = _dot(dal, kt[l], NN)
                dkt = _dot(dal, qt[l], TN)
                dq = dq + dqt * fq[l]
                dk = dk + dkt * fk[l]
                gl = dqt * (q * fq[l]) - dkt * (k * fk[l])
                db = db + gl
                placed += [(s + m - 1, -jnp.sum(gl[s:s + 2 * m], axis=0, keepdims=True)) for s in range(0, GC, 2 * m)]
            dad = jnp.where(md, da_s, 0.0).astype(BF16)
            dqd = _dot(dad, kd, NN)
            dkd = _dot(dad, qd, TN)
            dq = dq + dqd * ed
            dk = dk + dkd * edi
            gd = dqd * (q * ed) - dkd * (k * edi)
            db = db + gd
            placed += [(s - 1, -jnp.sum(gd[s:s + GLA_SUB], axis=0, keepdims=True)) for s in range(GLA_SUB, GC, GLA_SUB)]
            db_scr[:, cs] = db
            for r, val in placed:
                db_scr[r:r + 1, cs] += val
            dp_ref[:, cs] = (dq * (GLA_K ** -0.5)).astype(BF16)
            dp_ref[:, GLA_KW + h * GLA_K:GLA_KW + (h + 1) * GLA_K] = dk.astype(BF16)
            dp_ref[:, o_gv + h * GLA_V:o_gv + (h + 1) * GLA_V] = dv.astype(BF16)

    rev = lambda n: (ns - 1 - n, 0)
    const = lambda n: (0, 0)
    xc_shapes, xc_sems = (list(comm.out_shapes), _comm_sems(comm)) if n_xc else ([], [])
    return pl.pallas_call(
        body, name="gla_bwd", grid=(ns,),
        in_specs=[pl.BlockSpec((GS * GC, 2 * GLA_KW), rev),
                  pl.BlockSpec((GS * GC, GLA_W), rev),
                  pl.BlockSpec((GS * GC, GLA_KW), rev),
                  pl.BlockSpec((GS * GC, GLA_W), rev),
                  pl.BlockSpec((GS * GC, GLA_W), rev),
                  pl.BlockSpec((GS * GC, GLA_W), rev),
                  pl.BlockSpec((GS, GLA_HEADS, GLA_V, GLA_K), lambda n: (ns - 1 - n, 0, 0, 0)),
                  pl.BlockSpec((GS, GLA_HEADS, GC, GC), lambda n: (ns - 1 - n, 0, 0, 0)),
                  pl.BlockSpec((1, GLA_W), const)] + [ANY] * n_xc,
        out_specs=[pl.BlockSpec((GS * GC, W_GP), rev), pl.BlockSpec((GS * GC, GLA_KW), rev),
                   pl.BlockSpec((1, GLA_W), const)] + [ANY] * n_xc,
        out_shape=[jax.ShapeDtypeStruct((tp, W_GP), BF16), jax.ShapeDtypeStruct((tp, GLA_KW), F32),
                   jax.ShapeDtypeStruct((1, GLA_W), F32)] + xc_shapes,
        scratch_shapes=[pltpu.VMEM((GLA_HEADS, GLA_V, GLA_K), F32)] + xc_sems,
        compiler_params=_cparams(1),
    )(gqk, gv, b, gg, o_gla, da, states, scores, gain, *(comm.srcs if comm else ()))


def _mid_call(a_ret, a_gla, mg, h0, tgt, wbr, wbg, wout, gf):
    tp = h0.shape[0]
    nt = tp // TM

    def body(ar_ref, ag_ref, mg_ref, h_ref, t_ref, wbr_ref, wbg_ref, wo_ref, gf_ref,
             dh1_ref, dag_ref, dm_ref, mb_ref, dh1b_ref, dprb_ref, dpgb_ref, loss_ref, dgf_ref):
        i = pl.program_id(0)

        @pl.when(i == 0)
        def _():
            loss_ref[...] = jnp.zeros_like(loss_ref)
            dgf_ref[...] = jnp.zeros_like(dgf_ref)

        ar, ag = ar_ref[...], ag_ref[...]
        pr = _dot(ar, wbr_ref[...], NN)
        pg = _dot(ag, wbg_ref[...], NN)
        sr = _sigmoid(mg_ref[:, :D_MODEL])
        sg = _sigmoid(mg_ref[:, D_MODEL:])
        merged = (sr * pr + sg * pg).astype(BF16)
        mb_ref[...] = merged
        h1 = h_ref[...] + _dot(merged, wo_ref[...], NN)
        r1 = lax.rsqrt(jnp.mean(h1 * h1, axis=-1, keepdims=True) + EPS)
        xh = h1 * r1
        gfv = gf_ref[...]
        live = jnp.where(i > 0, 1.0, 0.0).astype(F32)
        err = (xh * gfv - t_ref[...]) * live
        loss_ref[...] += jnp.full(loss_ref.shape, 0.5 / D_MODEL, F32) * jnp.sum(err * err)
        dy = err * (1.0 / D_MODEL)
        dgf_ref[...] += jnp.sum(dy * xh, axis=0, keepdims=True)
        dxh = dy * gfv
        dh1 = r1 * (dxh - xh * jnp.mean(dxh * xh, axis=-1, keepdims=True))
        dh1_ref[...] = dh1
        dh1b = dh1.astype(BF16)
        dh1b_ref[...] = dh1b
        dmerged = _dot(dh1b, wo_ref[...], NT)
        dm_ref[:, :D_MODEL] = (dmerged * pr * sr * (1.0 - sr)).astype(BF16)
        dm_ref[:, D_MODEL:] = (dmerged * pg * sg * (1.0 - sg)).astype(BF16)
        dpr = (dmerged * sr).astype(BF16)
        dpg = (dmerged * sg).astype(BF16)
        dprb_ref[...] = dpr
        dpgb_ref[...] = dpg
        dag_ref[...] = _dot(dpg, wbg_ref[...], NT)

    tile = lambda w: pl.BlockSpec((TM, w), lambda i: (i, 0))
    const = lambda r, w: pl.BlockSpec((r, w), lambda i: (0, 0))
    return pl.pallas_call(
        body, name="merge_out_loss", grid=(nt,),
        in_specs=[tile(RET_W), tile(GLA_W), tile(W_M), tile(D_MODEL),
                  pl.BlockSpec((TM, D_MODEL), lambda i: (jnp.maximum(i - 1, 0), 0)),
                  const(RET_W, D_MODEL), const(GLA_W, D_MODEL), const(D_MODEL, D_MODEL), const(1, D_MODEL)],
        out_specs=[tile(D_MODEL), tile(GLA_W), tile(W_M), tile(D_MODEL), tile(D_MODEL), tile(D_MODEL),
                   tile(D_MODEL), const(1, 128), const(1, D_MODEL)],
        out_shape=[jax.ShapeDtypeStruct((tp, D_MODEL), F32), jax.ShapeDtypeStruct((tp, GLA_W), F32),
                   jax.ShapeDtypeStruct((tp, W_M), BF16),
                   jax.ShapeDtypeStruct((tp, D_MODEL), BF16), jax.ShapeDtypeStruct((tp, D_MODEL), BF16),
                   jax.ShapeDtypeStruct((tp, D_MODEL), BF16), jax.ShapeDtypeStruct((tp, D_MODEL), BF16),
                   jax.ShapeDtypeStruct((1, 128), F32), jax.ShapeDtypeStruct((1, D_MODEL), F32)],
        compiler_params=_cparams(1),
    )(a_ret, a_gla, mg, h0, tgt, wbr, wbg, wout, gf)


def _device_step(x2d, tgt2d, meta, norm_gain, w_in_part, w_gate_up, b_gate, ret_gain, gla_gain, branch_parts,
                 final_gain, ck):
    seq = x2d.shape[0]
    tp = T0 + seq
    head = jnp.concatenate([jnp.zeros((PADF, D_MODEL), F32), meta], axis=0)
    wg_pad = jnp.pad(w_gate_up, ((0, 128 - GATE_RANK), (0, 0))).astype(BF16)

    half = RET_QK // 2
    cos, sin = (jnp.asarray(t) for t in _rope_tables(tp))
    lgam = jnp.log1p(-(2.0 ** (-5.0 - jnp.arange(RET_HEADS, dtype=F32))))
    pmat = jnp.asarray(_gla_tril(), BF16)
    pmat_t = jnp.asarray(_gla_tril().T.copy(), BF16)

    h0, u, g_in = _rms_call(x2d, head, norm_gain, _gather_plan([w_in_part], relay=(True,)))
    sw, hc = w_in_part.shape
    w_in_t = _assemble_call(g_in.reshape(8, sw, hc))
    w_r = w_in_t
    w_g = jnp.pad(w_in_t[W_R:W_R + W_G], ((0, W_GP - W_G), (0, 0)))
    w_m = w_in_t[W_R + W_G:]
    tab = pl.BlockSpec((_proj_rows(tp), half), lambda j, i: (i, 0))
    rqk = _mm_nn("proj_rqk", u, w_r, BF16, D_MODEL, 0, 2 * D_MODEL, _rope_epilogue, (cos, sin), (tab, tab))
    rv = _mm_nn("proj_rv", u, w_r, BF16, RET_W, 2 * D_MODEL, RET_W)
    rg = _mm_nn("proj_rg", u, w_r, F32, RET_W, 4 * D_MODEL, RET_W)
    gqk = _mm_nn("proj_gqk", u, w_g, F32, 2 * GLA_KW, 0, 2 * GLA_KW, _gqk_epilogue)
    gv = _mm_nn("proj_gv", u, w_g, BF16, GLA_W, 2 * GLA_KW, GLA_W)
    gg = _mm_nn("proj_gg", u, w_g, F32, GLA_W, 2 * GLA_KW + GLA_W, GLA_W)
    mg = _mm_nn("proj_mg", u, w_m, F32, W_M, 0, W_M)

    o_ret, a_ret, st_ret, sc_ret = _ret_fwd_call(rqk, rv, rg, ret_gain, lgam)
    glr, z_gate, b_dec = _gla_gate_call(u, w_g, wg_pad, b_gate, pmat)
    o_gla, a_gla, st_gla, sc_gla, g_br, g_bg, g_out = _gla_fwd_call(gqk, gv, b_dec, gg, gla_gain,
                                                                    comm=_spread_plan(branch_parts))
    wbr = g_br.reshape(RET_W, D_MODEL)
    wbg = g_bg.reshape(GLA_W, D_MODEL)
    wout = g_out.reshape(D_MODEL, D_MODEL)

    gf = final_gain.reshape(1, D_MODEL)
    (dh1, da_gla, dm, merged_b, dh1_b, dpr_b, dpg_b, loss, dgf) = _mid_call(
        a_ret, a_gla, mg, h0, tgt2d, wbr, wbg, wout, gf)

    names_b = ("w_branch_ret", "w_branch_gla", "w_out")
    g2_b = [_mm_tn("dw_br", a_ret, dpr_b, D_MODEL).reshape(4, 2, RET_W // 8, D_MODEL).transpose(1, 0, 2, 3),
            _mm_tn("dw_bg", a_gla, dpg_b, D_MODEL).reshape(4, 2, GLA_W // 8, D_MODEL).transpose(1, 0, 2, 3),
            _mm_tn("dw_out", merged_b, dh1_b, D_MODEL).reshape(4, 2, D_MODEL // 8, D_MODEL).transpose(1, 0, 2, 3)]
    sib_b = _swap_halves_call("swap_halves_branch", g2_b)
    sum_b = [_add_half_call("add_half_" + nm, g, b, ck) for nm, g, b in zip(names_b, g2_b, sib_b)]
    d_g, db_dec, dgla_gain, *chips_b = _gla_bwd_call(gqk, gv, b_dec, gg, o_gla, da_gla, st_gla, sc_gla, gla_gain,
                                                     comm=_exchange_plan(sum_b))
    d_g, dwg, dbg, dw_glr = _gla_gate_bwd_call(db_dec, z_gate, glr, u, wg_pad, pmat_t, d_g)
    mine = [_add_chips_call("add_chips_" + nm, g, b, p, ck) for nm, g, b, p in zip(names_b, g2_b, sib_b, chips_b)]

    d_r, dret_gain = _ret_bwd_call(rqk, rv, rg, o_ret, dpr_b, wbr, st_ret, sc_ret, ret_gain, lgam, cos, sin)

    dwp = _mm_tn("dw_r", u, d_r, 3 * D_MODEL, out_cols=IN_PAD)
    dwp = _mm_tn("dw_g", u, d_g, 3 * D_MODEL, ncols=W_GP - 128, into=dwp, col0=W_R)
    g2_in = _place_merge_cols_call(dwp, _mm_tn("dw_m", u, dm, 2 * D_MODEL), dw_glr).reshape(2, D_MODEL // 2, IN_PAD)

    du, sib_in = _mm_nt_acc("du_g", d_g, w_g, W_GP, comm=_swap_plan([g2_in]), tb=_proj_rows(tp))
    sum_in = _add_rows_call("add_half_w_in", g2_in, sib_in, ck)
    du, chips_in = _mm_nt_acc("du_r", d_r, w_r, 2 * D_MODEL, acc_in=du, comm=_exchange_window_plan(sum_in),
                              tb=_proj_rows(tp))
    tile = pl.BlockSpec((TB, D_MODEL), lambda i, kk: (i, 0))
    row = pl.BlockSpec((1, D_MODEL), lambda i, kk: (0, 0))
    dx, dmeta, dnorm_gain = _mm_nt_acc(
        "du_m", dm, w_m, W_M, acc_in=du, epilogue=_rms_bwd_epilogue, extras=(h0, norm_gain, dh1),
        extra_specs=(tile, row, tile),
        extra_out_shapes=(jax.ShapeDtypeStruct((seq, D_MODEL), F32), jax.ShapeDtypeStruct((N_META, D_MODEL), F32),
                          jax.ShapeDtypeStruct((1, D_MODEL), F32)),
        extra_out_specs=(ANY, pl.BlockSpec((N_META, D_MODEL), lambda i, kk: (0, 0)), row),
        extra_scratch=(pltpu.VMEM((2, TB, D_MODEL), F32), pltpu.SemaphoreType.DMA((2,))))
    small = dict(norm_gain=dnorm_gain, b_gate=dbg, ret_norm_gain=dret_gain, gla_norm_gain=dgla_gain,
                 final_norm_gain=dgf, w_gate_up=dwg[:GATE_RANK], meta_tokens=dmeta, loss=loss[0, 0])
    rows = -(-sum(sz for _, sz in SMALL) // 128 // 8) * 8
    mine_in, g_small = _add_window_call("add_chips_w_in", g2_in, sib_in, chips_in, ck,
                                        _gather_plan([_pack_rows([small[nm] for nm, _ in SMALL], rows)]))
    full = _join_halves_call("join_halves", [mine_in] + mine)

    return dict(dx=dx, small=g_small, w_in=full[0], w_branch_ret=full[1], w_branch_gla=full[2], w_out=full[3])


MESH = pl.DeviceIdType.MESH
ANY = pl.BlockSpec(memory_space=pl.ANY)


def _place():
    return lax.axis_index("x"), lax.axis_index("y"), lax.axis_index("c")


def _gather8_call(name, parts):
    comm = _gather_plan(parts)
    n = len(parts)

    def body(*refs):
        begin, finish = comm.make(refs[:n], refs[n:2 * n], refs[-2], refs[-1])
        begin()
        finish()

    return pl.pallas_call(
        body, name=name, out_shape=list(comm.out_shapes), in_specs=[ANY] * n, out_specs=[ANY] * n,
        scratch_shapes=_comm_sems(comm),
    )(*parts)


def _swap_halves_call(name, gs):
    n = len(gs)

    def body(*refs):
        g_refs, b_refs = refs[:n], refs[n:2 * n]
        send_sems, recv_sems = refs[2 * n:]
        x, y, c = _place()
        copies = [pltpu.make_async_remote_copy(
            src_ref=g_refs[t].at[1 - c], dst_ref=b_refs[t], send_sem=send_sems.at[t], recv_sem=recv_sems.at[t],
            device_id=(x, y, 1 - c), device_id_type=MESH) for t in range(n)]
        for cp in copies:
            cp.start()
        for cp in copies:
            cp.wait()

    return pl.pallas_call(
        body, name=name,
        out_shape=[jax.ShapeDtypeStruct(g.shape[1:], g.dtype) for g in gs],
        in_specs=[ANY] * n, out_specs=[ANY] * n,
        scratch_shapes=[pltpu.SemaphoreType.DMA((n,)), pltpu.SemaphoreType.DMA((n,))],
    )(*gs)


def _join_halves_call(name, ts):
    n = len(ts)

    def body(*refs):
        o_refs = refs[n:2 * n]
        send_sems, recv_sems = refs[2 * n:]
        x, y, c = _place()
        copies = [pltpu.make_async_remote_copy(
            src_ref=o_refs[t].at[c], dst_ref=o_refs[t].at[c], send_sem=send_sems.at[t], recv_sem=recv_sems.at[t],
            device_id=(x, y, 1 - c), device_id_type=MESH) for t in range(n)]
        for cp in copies:
            cp.start()
        for t in range(n):
            copies[t].wait_send()
            pltpu.make_async_remote_copy(
                src_ref=o_refs[t].at[c], dst_ref=o_refs[t].at[1 - c], send_sem=send_sems.at[t],
                recv_sem=recv_sems.at[t], device_id=(x, y, 1 - c), device_id_type=MESH).wait_recv()

    return pl.pallas_call(
        body, name=name,
        out_shape=[jax.ShapeDtypeStruct(t.shape, t.dtype) for t in ts],
        in_specs=[ANY] * n, out_specs=[ANY] * n, input_output_aliases={t: t for t in range(n)},
        scratch_shapes=[pltpu.SemaphoreType.DMA((n,)), pltpu.SemaphoreType.DMA((n,))],
    )(*ts)


def _row_block(rows, cols, budget):
    best = 8
    for rb in range(8, rows + 1, 8):
        if rows % rb == 0 and rb * cols * 4 <= budget:
            best = rb
    return best


def _add_half_call(name, g, b, ck):
    _, _, r, cc = g.shape
    rb = _row_block(r, cc, 2 * 1024 * 1024)

    def body(ck_ref, g_ref, b_ref, o_ref):
        o_ref[...] = (g_ref[...] + b_ref[...]).astype(BF16)

    return pl.pallas_call(
        body, name=name,
        grid_spec=pltpu.PrefetchScalarGridSpec(
            num_scalar_prefetch=1, grid=(4, r // rb),
            in_specs=[pl.BlockSpec((None, None, rb, cc), lambda k, i, ck_ref: (ck_ref[0], k, i, 0)),
                      pl.BlockSpec((None, rb, cc), lambda k, i, ck_ref: (k, i, 0))],
            out_specs=pl.BlockSpec((None, rb, cc), lambda k, i, ck_ref: (k, i, 0))),
        out_shape=jax.ShapeDtypeStruct(b.shape, BF16),
        compiler_params=_cparams(2),
    )(ck, g, b)


def _add_rows_call(name, g, b, ck):
    _, r, cc = g.shape
    rb = _row_block(r, cc, 2 * 1024 * 1024)

    def body(ck_ref, g_ref, b_ref, o_ref):
        o_ref[...] = (g_ref[...] + b_ref[...]).astype(BF16)

    return pl.pallas_call(
        body, name=name,
        grid_spec=pltpu.PrefetchScalarGridSpec(
            num_scalar_prefetch=1, grid=(r // rb,),
            in_specs=[pl.BlockSpec((None, rb, cc), lambda i, ck_ref: (ck_ref[0], i, 0)),
                      pl.BlockSpec((rb, cc), lambda i, ck_ref: (i, 0))],
            out_specs=pl.BlockSpec((rb, cc), lambda i, ck_ref: (i, 0))),
        out_shape=jax.ShapeDtypeStruct((r, cc), BF16),
        compiler_params=_cparams(1),
    )(ck, g, b)


def _add_window_call(name, g, b, p, ck, comm):
    _, r, _ = g.shape
    nb, step = WIN_W // 128, WIN_STEP // 128
    n_xc = len(comm.srcs)

    def body(ck_ref, g_ref, b_ref, p0_ref, p1_ref, p2_ref, *rest):
        o_ref = rest[n_xc]
        i = pl.program_id(0)
        begin, finish = comm.make(rest[:n_xc], rest[n_xc + 1:2 * n_xc + 1], rest[-2], rest[-1])
        pl.when(i == 0)(begin)
        own = g_ref[...] + b_ref[...]
        o_ref[...] = ((own + p0_ref[...].astype(F32)) + p1_ref[...].astype(F32)) + p2_ref[...].astype(F32)
        pl.when(i == nb - 1)(finish)

    def peer(j):
        return pl.BlockSpec((None, r, 128), lambda i, ck_ref: (j, 0, i))

    return pl.pallas_call(
        body, name=name,
        grid_spec=pltpu.PrefetchScalarGridSpec(
            num_scalar_prefetch=1, grid=(nb,),
            in_specs=[pl.BlockSpec((None, r, 128), lambda i, ck_ref: (ck_ref[0], 0, step * ck_ref[1] + i)),
                      pl.BlockSpec((r, 128), lambda i, ck_ref: (0, step * ck_ref[1] + i)),
                      peer(0), peer(1), peer(2)] + [ANY] * n_xc,
            out_specs=[pl.BlockSpec((None, r, 128), lambda i, ck_ref: (ck_ref[0], 0, i))] + [ANY] * n_xc,
            scratch_shapes=_comm_sems(comm)),
        out_shape=[jax.ShapeDtypeStruct((2, r, WIN_W), F32)] + list(comm.out_shapes),
        compiler_params=_cparams(1),
    )(ck, g, b, p, p, p, *comm.srcs)


def _add_chips_call(name, g, b, p, ck):
    _, _, r, cc = g.shape
    rb = _row_block(r, cc, 2 * 1024 * 1024)

    def body(ck_ref, g_ref, b_ref, p0_ref, p1_ref, p2_ref, o_ref):
        own = g_ref[...] + b_ref[...]
        o_ref[...] = ((own + p0_ref[...].astype(F32)) + p1_ref[...].astype(F32)) + p2_ref[...].astype(F32)

    def peer(j):
        return pl.BlockSpec((None, rb, cc), lambda i, ck_ref: (j, i, 0))

    return pl.pallas_call(
        body, name=name,
        grid_spec=pltpu.PrefetchScalarGridSpec(
            num_scalar_prefetch=1, grid=(r // rb,),
            in_specs=[pl.BlockSpec((None, None, rb, cc), lambda i, ck_ref: (ck_ref[0], ck_ref[1], i, 0)),
                      pl.BlockSpec((None, rb, cc), lambda i, ck_ref: (ck_ref[1], i, 0)),
                      peer(0), peer(1), peer(2)],
            out_specs=pl.BlockSpec((None, rb, cc), lambda i, ck_ref: (ck_ref[0], i, 0))),
        out_shape=jax.ShapeDtypeStruct((2, r, cc), F32),
        compiler_params=_cparams(1),
    )(ck, g, b, p, p, p)


def _sum8_call(name, g):
    def body(g_ref, o_ref):
        acc = g_ref[0]
        for d in range(1, 8):
            acc = acc + g_ref[d]
        o_ref[...] = acc

    return pl.pallas_call(body, name=name, out_shape=jax.ShapeDtypeStruct(g.shape[1:], F32))(g)


def _adamw_call(name, w, g, m, v):
    r, cc = w.shape
    if r % 8 == 0 or r * cc * 4 <= 1024 * 1024:
        rb = _row_block(r, cc, 1024 * 1024) if r % 8 == 0 else r
        grid, spec = (r // rb,), pl.BlockSpec((rb, cc), lambda i: (i, 0))
    else:
        grid, spec = (cc // 128,), pl.BlockSpec((r, 128), lambda i: (0, i))

    def body(w_ref, g_ref, m_ref, v_ref, d_ref, m2_ref, v2_ref):
        _adamw_update(g_ref[...], w_ref, m_ref, v_ref, d_ref, m2_ref, v2_ref)

    return pl.pallas_call(
        body, name=name, grid=grid, in_specs=[spec] * 4, out_specs=[spec] * 3,
        out_shape=[jax.ShapeDtypeStruct((r, cc), F32)] * 3, compiler_params=_cparams(1),
    )(w, g, m, v)


def _adamw_update(gv, w_ref, m_ref, v_ref, d_ref, m2_ref, v2_ref):
    m2 = ADAM_B1 * m_ref[...] + (1.0 - ADAM_B1) * gv
    v2 = ADAM_B2 * v_ref[...] + (1.0 - ADAM_B2) * (gv * gv)
    m_hat = m2 / (1.0 - ADAM_B1 ** ADAM_STEP)
    v_hat = v2 / (1.0 - ADAM_B2 ** ADAM_STEP)
    d_ref[...] = -ADAM_LR * (m_hat / (jnp.sqrt(v_hat) + ADAM_EPS) + ADAM_WD * w_ref[...])
    m2_ref[...] = m2
    v2_ref[...] = v2


def _adamw_window_call(name, w_t, f, m_t, v_t, lane0):
    s, r = w_t.shape
    wl = f.shape[1]

    def body(l0_ref, w_ref, f_ref, m_ref, v_ref, g_ref, d_ref, m2_ref, v2_ref):
        gv = pltpu.roll(f_ref[...], lax.rem(wl - l0_ref[0], wl), axis=1).T[:s]
        g_ref[...] = gv
        _adamw_update(gv, w_ref, m_ref, v_ref, d_ref, m2_ref, v2_ref)

    spec = pl.BlockSpec((s, 128), lambda i, l0: (0, i))
    return pl.pallas_call(
        body, name=name,
        grid_spec=pltpu.PrefetchScalarGridSpec(
            num_scalar_prefetch=1, grid=(r // 128,),
            in_specs=[spec, pl.BlockSpec((128, wl), lambda i, l0: (i, 0)), spec, spec], out_specs=[spec] * 4),
        out_shape=[jax.ShapeDtypeStruct((s, r), F32)] * 4, compiler_params=_cparams(1),
    )(lane0, w_t, f, m_t, v_t)


SMALL = (("norm_gain", D_MODEL), ("b_gate", GLA_KW), ("ret_norm_gain", RET_W), ("gla_norm_gain", GLA_W),
         ("final_norm_gain", D_MODEL), ("w_gate_up", GATE_RANK * GLA_KW), ("meta_tokens", N_META * D_MODEL),
         ("loss", 1))


def _pack_rows(vecs, rows):
    flat = jnp.concatenate([v.reshape(-1) for v in vecs])
    return jnp.pad(flat, (0, rows * 128 - flat.shape[0])).reshape(rows, 128)


def kernel(x, meta_tokens, norm_gain, w_in, w_gate_up, b_gate, ret_norm_gain, gla_norm_gain, w_branch_ret, w_branch_gla, w_out, final_norm_gain, loss_target, m_meta_tokens, m_norm_gain, m_w_in, m_w_gate_up, m_b_gate, m_ret_norm_gain, m_gla_norm_gain, m_w_branch_ret, m_w_branch_gla, m_w_out, m_final_norm_gain, v_meta_tokens, v_norm_gain, v_w_in, v_w_gate_up, v_b_gate, v_ret_norm_gain, v_gla_norm_gain, v_w_branch_ret, v_w_branch_gla, v_w_out, v_final_norm_gain):
    xi, yi, ci = _place()
    kme = 2 * xi + yi
    ck = jnp.stack([ci, kme]).astype(jnp.int32)
    sw_in = w_in.shape[2]

    def my_half(a, dtype):
        r, cc = a.shape
        return lax.dynamic_index_in_dim(a.reshape(2, r // 2, cc), ci, 0, keepdims=False).astype(dtype)

    g_meta, g_wg = _gather8_call("gather_small_weights", [my_half(meta_tokens, F32), my_half(w_gate_up[0], F32)])
    branch_parts = [my_half(w_branch_ret[0], BF16), my_half(w_branch_gla[0], BF16), my_half(w_out[0], BF16)]
    meta = g_meta.reshape(4, 2, N_META // 2, D_MODEL // 4).transpose(1, 2, 0, 3).reshape(N_META, D_MODEL)
    wg_full = g_wg.reshape(4, 2, GATE_RANK // 2, GLA_KW // 4).transpose(1, 2, 0, 3).reshape(GATE_RANK, GLA_KW)

    w_in_part = lax.dynamic_slice_in_dim(w_in[0].T, ci * (D_MODEL // 2), D_MODEL // 2, axis=1).astype(BF16)
    loc = _device_step(x[0], loss_target[0], meta, norm_gain, w_in_part, wg_full, b_gate, ret_norm_gain,
                       gla_norm_gain,
                       branch_parts, final_norm_gain, ck)
    names = ("w_in", "w_branch_ret", "w_branch_gla", "w_out")
    full = [loc[nm] for nm in names]
    big_w = dict(w_in=w_in[0], w_branch_ret=w_branch_ret[0], w_branch_gla=w_branch_gla[0], w_out=w_out[0])
    big_m = dict(w_in=m_w_in[0], w_branch_ret=m_w_branch_ret[0], w_branch_gla=m_w_branch_gla[0], w_out=m_w_out[0])
    big_v = dict(w_in=v_w_in[0], w_branch_ret=v_w_branch_ret[0], w_branch_gla=v_w_branch_gla[0], w_out=v_w_out[0])
    grads, deltas, new_m, new_v = {}, {}, {}, {}
    for nm, f in zip(names, full):
        shape = big_w[nm].shape
        if nm == "w_in":
            lane0 = ((sw_in - WIN_STEP) * kme).astype(jnp.int32).reshape(1)
            m_t, v_t = lax.optimization_barrier(big_m[nm].T), lax.optimization_barrier(big_v[nm].T)
            g, d, m2, v2 = (a.T for a in _adamw_window_call(
                "adamw_" + nm, big_w[nm].T, f.reshape(shape[0], WIN_W), m_t, v_t, lane0))
        else:
            g = f.reshape(shape)
            d, m2, v2 = _adamw_call("adamw_" + nm, big_w[nm], g, big_m[nm], big_v[nm])
        grads[nm], deltas[nm], new_m[nm], new_v[nm] = (a.reshape((1,) + shape) for a in (g, d, m2, v2))

    tot = _sum8_call("sum_small_grads", loc["small"]).reshape(-1)
    off = 0
    sg = {}
    for nm, sz in SMALL:
        sg[nm] = tot[off:off + sz]
        off += sz
    loss = sg.pop("loss")[0]
    sg["w_gate_up"] = lax.dynamic_slice_in_dim(sg["w_gate_up"].reshape(GATE_RANK, GLA_KW), kme * (GLA_KW // 4),
                                               GLA_KW // 4, axis=1)
    sg["meta_tokens"] = lax.dynamic_slice_in_dim(sg["meta_tokens"].reshape(N_META, D_MODEL), kme * (D_MODEL // 4),
                                                 D_MODEL // 4, axis=1)
    small_w = dict(norm_gain=norm_gain, b_gate=b_gate, ret_norm_gain=ret_norm_gain, gla_norm_gain=gla_norm_gain,
                   final_norm_gain=final_norm_gain, w_gate_up=w_gate_up, meta_tokens=meta_tokens)
    small_m = dict(norm_gain=m_norm_gain, b_gate=m_b_gate, ret_norm_gain=m_ret_norm_gain,
                   gla_norm_gain=m_gla_norm_gain, final_norm_gain=m_final_norm_gain, w_gate_up=m_w_gate_up,
                   meta_tokens=m_meta_tokens)
    small_v = dict(norm_gain=v_norm_gain, b_gate=v_b_gate, ret_norm_gain=v_ret_norm_gain,
                   gla_norm_gain=v_gla_norm_gain, final_norm_gain=v_final_norm_gain, w_gate_up=v_w_gate_up,
                   meta_tokens=v_meta_tokens)
    for nm in small_w:
        shape = small_w[nm].shape
        as2d = lambda a: a.reshape((-1, shape[-1]))
        grads[nm] = sg[nm].reshape(shape)
        deltas[nm], new_m[nm], new_v[nm] = (a.reshape(shape) for a in _adamw_call(
            "adamw_" + nm, as2d(small_w[nm]), as2d(sg[nm]), as2d(small_m[nm]), as2d(small_v[nm])))

    out_order = ("meta_tokens", "norm_gain", "w_in", "w_gate_up", "b_gate", "ret_norm_gain", "gla_norm_gain",
                 "w_branch_ret", "w_branch_gla", "w_out", "final_norm_gain")
    dx = loc["dx"].reshape(x.shape)
    return (loss, dx, *[grads[nm] for nm in out_order], *[deltas[nm] for nm in out_order],
            *[new_m[nm] for nm in out_order], *[new_v[nm] for nm in out_order])
```

```python
import math
from typing import Callable, NamedTuple

import numpy as np
import jax
import jax.numpy as jnp
from jax import lax
from jax.experimental import pallas as pl
from jax.experimental.pallas import tpu as pltpu

F32 = jnp.float32
BF16 = jnp.bfloat16

D_MODEL = 1024
N_META = 16
EPS = 1e-6
ROPE_BASE = 10000.0
RET_HEADS, RET_QK, RET_V = 4, 256, 512
RET_W = RET_HEADS * RET_V
GLA_HEADS, GLA_K, GLA_V = 4, 128, 256
GLA_W = GLA_HEADS * GLA_V
GLA_KW = GLA_HEADS * GLA_K
GATE_RANK = 16
GATE_TAU = 16.0
GLA_SUB = 16

TM = 256
T0 = TM
PADF = T0 - N_META
GC = 128
GS = 3
TB = 768
TK = 768

W_R = 6144
W_G = 3088
W_GP = 3200
W_M = 2048
IN_COLS = W_R + W_G + W_M
WIN_STEP = (IN_COLS // 4) // 128 * 128
WIN_W = -(-(3 * (IN_COLS // 4 - WIN_STEP) + IN_COLS // 4) // 128) * 128
IN_PAD = 3 * WIN_STEP + WIN_W

ADAM_LR, ADAM_B1, ADAM_B2, ADAM_EPS, ADAM_WD, ADAM_STEP = 0.001, 0.9, 0.999, 1e-08, 0.01, 10

VMEM_LIMIT = 56 * 1024 * 1024

NN = ((1,), (0,))
NT = ((1,), (1,))
TN = ((0,), (0,))


def _dot(a, b, dims):
    return lax.dot_general(a, b, (dims, ((), ())), preferred_element_type=F32)


def _cparams(n_axes):
    return pltpu.CompilerParams(dimension_semantics=("arbitrary",) * n_axes, vmem_limit_bytes=VMEM_LIMIT)


def _sigmoid(x):
    return 0.5 * jnp.tanh(0.5 * x) + 0.5


def _silu(x):
    h = 0.5 * x
    return h + h * jnp.tanh(h)


def _head_mean(x):
    return jnp.mean(x, axis=-1, keepdims=True)


def _split3(x):
    hi = x.astype(BF16)
    r1 = x - hi.astype(F32)
    mid = r1.astype(BF16)
    lo = (r1 - mid.astype(F32)).astype(BF16)
    return hi, mid, lo


def _exact_pm(p, x):
    hi, mid, lo = _split3(x)
    return _dot(p, hi, NN) + _dot(p, mid, NN) + _dot(p, lo, NN)


def _rms_call(x2d, head, gain, comm):
    tp = T0 + x2d.shape[0]
    nt = tp // TM
    n_xc = len(comm.srcs)

    def body(x_ref, hd_ref, g_ref, *rest):
        xc_src = rest[:n_xc]
        h_ref, u_ref = rest[n_xc:n_xc + 2]
        xc_dst = rest[n_xc + 2:2 * n_xc + 2]
        i = pl.program_id(0)
        begin, finish = comm.make(xc_src, xc_dst, rest[-2], rest[-1])
        pl.when(i == 0)(begin)
        h = jnp.where(i == 0, hd_ref[...], x_ref[...])
        h_ref[...] = h
        r = lax.rsqrt(jnp.mean(h * h, axis=-1, keepdims=True) + EPS)
        u_ref[...] = (h * r * g_ref[...]).astype(BF16)
        pl.when(i == nt - 1)(finish)

    tile = pl.BlockSpec((TM, D_MODEL), lambda i: (i, 0))
    return pl.pallas_call(
        body, name="rms_in", grid=(nt,),
        in_specs=[pl.BlockSpec((TM, D_MODEL), lambda i: (jnp.maximum(i - 1, 0), 0)),
                  pl.BlockSpec((T0, D_MODEL), lambda i: (0, 0)), pl.BlockSpec((1, D_MODEL), lambda i: (0, 0))]
        + [ANY] * n_xc,
        out_specs=[tile, tile] + [ANY] * n_xc,
        out_shape=[jax.ShapeDtypeStruct((tp, D_MODEL), F32), jax.ShapeDtypeStruct((tp, D_MODEL), BF16)]
        + list(comm.out_shapes),
        scratch_shapes=_comm_sems(comm), compiler_params=_cparams(1),
    )(x2d, head, gain, *comm.srcs)


PROJ_ROWS_MAX = 1408


def _proj_rows(m):
    return max(r for r in range(16, PROJ_ROWS_MAX + 1, 16) if m % r == 0)


def _mm_nn(name, a, bt, out_dtype, tn, col0, ncols, epilogue=None, extras=(), extra_specs=()):
    m, k = a.shape
    nj, j0 = ncols // tn, col0 // tn
    tb = _proj_rows(m)

    def body(a_ref, b_ref, *rest):
        *ex, o_ref = rest
        acc = _dot(a_ref[...], b_ref[...], NT)
        if epilogue is None:
            o_ref[...] = acc.astype(out_dtype)
        else:
            epilogue(acc, o_ref, *ex)

    return pl.pallas_call(
        body, name=name, grid=(nj, m // tb),
        in_specs=[pl.BlockSpec((tb, k), lambda j, i: (i, 0)), pl.BlockSpec((tn, k), lambda j, i: (j0 + j, 0))]
        + list(extra_specs),
        out_specs=pl.BlockSpec((tb, tn), lambda j, i: (i, j)),
        out_shape=jax.ShapeDtypeStruct((m, ncols), out_dtype),
        compiler_params=_cparams(2),
    )(a, bt, *extras)


def _rope_tables(tp):
    half = RET_QK // 2
    pos = np.arange(tp, dtype=np.float32) - np.float32(PADF)
    inv = (ROPE_BASE ** (-np.arange(half, dtype=np.float64) / half)).astype(np.float32)
    ang = (pos[:, None] * inv[None, :]).astype(np.float64)
    return np.cos(ang).astype(np.float32), np.sin(ang).astype(np.float32)


def _rope_epilogue(acc, o_ref, cos_ref, sin_ref):
    scale = jnp.where(pl.program_id(0) == 1, RET_QK ** -0.5, 1.0).astype(F32)
    cos, sin = cos_ref[...], sin_ref[...]
    half = RET_QK // 2
    for h in range(RET_HEADS):
        t1 = acc[:, h * RET_QK:h * RET_QK + half]
        t2 = acc[:, h * RET_QK + half:(h + 1) * RET_QK]
        o_ref[:, h * RET_QK:h * RET_QK + half] = ((t1 * cos - t2 * sin) * scale).astype(BF16)
        o_ref[:, h * RET_QK + half:(h + 1) * RET_QK] = ((t2 * cos + t1 * sin) * scale).astype(BF16)


def _gqk_epilogue(acc, o_ref):
    o_ref[:, :GLA_KW] = acc[:, :GLA_KW] * (GLA_K ** -0.5)
    o_ref[:, GLA_KW:] = acc[:, GLA_KW:]


class _Comm(NamedTuple):
    srcs: tuple
    out_shapes: tuple
    n_sems: int
    make: Callable


def _comm_sems(comm):
    return [pltpu.SemaphoreType.DMA((comm.n_sems,)), pltpu.SemaphoreType.DMA((comm.n_sems,))]


def _start_wait(copies):
    def begin():
        for cp in copies:
            cp.start()

    def finish():
        for cp in copies:
            cp.wait()

    return begin, finish


def _other_chips(x, y):
    return [(1 - x, y), (x, 1 - y), (1 - x, 1 - y)]


def _gather_plan(parts, relay=()):
    n = len(parts)
    relay = tuple(relay) + (False,) * (n - len(relay))

    def make(x_refs, out_refs, send_sems, recv_sems):
        x, y, c = _place()
        me, sibling = (x, y, c), (x, y, 1 - c)
        xn, yn, dg = (1 - x, y), (x, 1 - y), (1 - x, 1 - y)

        def slot(t, px, py, pc, half=None):
            ref = out_refs[t].at[4 * px + 2 * py + pc]
            if half is None:
                return ref
            cols = ref.shape[-1] // 2
            return ref.at[:, pl.ds(half * cols, cols)]

        def copy(t, k, dst, to, src=None):
            return pltpu.make_async_remote_copy(
                src_ref=dst if src is None else src, dst_ref=dst, send_sem=send_sems.at[8 * t + k],
                recv_sem=recv_sems.at[8 * t + k], device_id=to, device_id_type=MESH)

        mine = [pltpu.make_async_copy(x_refs[t], slot(t, *me), send_sems.at[8 * n + t]) for t in range(n)]
        sent = []
        for t in range(n):
            sent.append(copy(t, 0, slot(t, *me), sibling, src=x_refs[t]))
            sent.append(copy(t, 1, slot(t, *me), (*xn, c), src=x_refs[t]))
            sent.append(copy(t, 2, slot(t, *me), (*yn, c), src=x_refs[t]))
            if not relay[t]:
                sent.append(copy(t, 3, slot(t, *me), (*dg, c), src=x_refs[t]))

        def begin():
            for cp in mine + sent:
                cp.start()

        def finish():
            later = []

            def start(cp):
                cp.start()
                later.append(cp)

            for t in range(n):
                copy(t, 2, slot(t, *yn, c), me).wait_recv()
                if relay[t]:
                    start(copy(t, 3, slot(t, *yn, c, half=0), (*xn, c)))
                start(copy(t, 6, slot(t, *yn, c), sibling))
            for t in range(n):
                copy(t, 1, slot(t, *xn, c), me).wait_recv()
                if relay[t]:
                    start(copy(t, 4, slot(t, *xn, c, half=1), (*yn, c)))
                start(copy(t, 5, slot(t, *xn, c), sibling))
            for t in range(n):
                if relay[t]:
                    copy(t, 3, slot(t, *dg, c, half=0), me).wait_recv()
                    copy(t, 4, slot(t, *dg, c, half=1), me).wait_recv()
                else:
                    copy(t, 3, slot(t, *dg, c), me).wait_recv()
                start(copy(t, 7, slot(t, *dg, c), sibling))
            for t in range(n):
                copy(t, 0, slot(t, *sibling), me).wait_recv()
                copy(t, 5, slot(t, *xn, 1 - c), me).wait_recv()
                copy(t, 6, slot(t, *yn, 1 - c), me).wait_recv()
                copy(t, 7, slot(t, *dg, 1 - c), me).wait_recv()
            for cp in sent + later:
                cp.wait_send()
            for cp in mine:
                cp.wait()

        return begin, finish

    return _Comm(tuple(parts), tuple(jax.ShapeDtypeStruct((8,) + p.shape, p.dtype) for p in parts), 9 * n, make)


def _exchange_plan(ss):
    def make(s_refs, b_refs, send_sems, recv_sems):
        x, y, c = _place()
        return _start_wait([pltpu.make_async_remote_copy(
            src_ref=s_refs[t].at[2 * chip[0] + chip[1]], dst_ref=b_refs[t].at[j], send_sem=send_sems.at[3 * t + j],
            recv_sem=recv_sems.at[3 * t + j], device_id=(*chip, c), device_id_type=MESH)
            for t in range(len(s_refs)) for j, chip in enumerate(_other_chips(x, y))])

    return _Comm(tuple(ss), tuple(jax.ShapeDtypeStruct((3,) + s.shape[1:], s.dtype) for s in ss), 3 * len(ss), make)


def _exchange_window_plan(s):
    def make(s_refs, b_refs, send_sems, recv_sems):
        x, y, c = _place()
        return _start_wait([pltpu.make_async_remote_copy(
            src_ref=s_refs[0].at[:, pl.ds(pl.multiple_of((2 * chip[0] + chip[1]) * WIN_STEP, 128), WIN_W)],
            dst_ref=b_refs[0].at[j], send_sem=send_sems.at[j], recv_sem=recv_sems.at[j], device_id=(*chip, c),
            device_id_type=MESH) for j, chip in enumerate(_other_chips(x, y))])

    return _Comm((s,), (jax.ShapeDtypeStruct((3, s.shape[0], WIN_W), s.dtype),), 3, make)


def _swap_plan(gs):
    def make(g_refs, b_refs, send_sems, recv_sems):
        x, y, c = _place()
        return _start_wait([pltpu.make_async_remote_copy(
            src_ref=g_refs[t].at[1 - c], dst_ref=b_refs[t], send_sem=send_sems.at[t], recv_sem=recv_sems.at[t],
            device_id=(x, y, 1 - c), device_id_type=MESH) for t in range(len(g_refs))])

    return _Comm(tuple(gs), tuple(jax.ShapeDtypeStruct(g.shape[1:], g.dtype) for g in gs), len(gs), make)


def _spread_plan(parts):
    def make(p_refs, o_refs, send_sems, recv_sems):
        x, y, c = _place()
        copies = []
        for t in range(len(p_refs)):
            mine = o_refs[t].at[4 * x + 2 * y + c]
            copies.append(pltpu.make_async_copy(p_refs[t], mine, send_sems.at[7 * len(p_refs) + t]))
            for r in range(1, 8):
                peer = (1 - x if r & 4 else x, 1 - y if r & 2 else y, 1 - c if r & 1 else c)
                copies.append(pltpu.make_async_remote_copy(
                    src_ref=p_refs[t], dst_ref=mine, send_sem=send_sems.at[7 * t + r - 1],
                    recv_sem=recv_sems.at[7 * t + r - 1], device_id=peer, device_id_type=MESH))
        return _start_wait(copies)

    return _Comm(tuple(parts), tuple(jax.ShapeDtypeStruct((8,) + p.shape, p.dtype) for p in parts), 8 * len(parts),
                 make)


def _mm_nt_acc(name, a, w, tk, acc_in=None, epilogue=None, extras=(), extra_specs=(), extra_out_shapes=(),
               extra_out_specs=(), extra_scratch=(), comm=None, tb=TB):
    m, k = a.shape
    n = w.shape[1]
    nk, ni = k // tk, m // tb
    has_acc = acc_in is not None
    n_xc = len(comm.srcs) if comm else 0
    n_es = len(extra_scratch)

    def body(*refs):
        a_ref, w_ref = refs[0], refs[1]
        pos = 2
        acc_ref = None
        if has_acc:
            acc_ref = refs[pos]
            pos += 1
        ex = refs[pos:pos + len(extras)]
        pos += len(extras)
        xc_src = refs[pos:pos + n_xc]
        pos += n_xc
        n_scr = 1 + n_es + (2 if n_xc else 0)
        outs = refs[pos:len(refs) - n_scr - n_xc]
        xc_dst = refs[len(refs) - n_scr - n_xc:len(refs) - n_scr]
        scr = refs[len(refs) - n_scr]
        es = refs[len(refs) - n_scr + 1:len(refs) - n_scr + 1 + n_es]
        i, kk = pl.program_id(0), pl.program_id(1)
        if n_xc:
            begin, finish = comm.make(xc_src, xc_dst, refs[-2], refs[-1])
            pl.when((i == 0) & (kk == 0))(begin)

        @pl.when(kk == 0)
        def _():
            scr[...] = acc_ref[...] if has_acc else jnp.zeros_like(scr)

        scr[...] += _dot(a_ref[...], w_ref[...], NN)

        @pl.when(kk == nk - 1)
        def _():
            if epilogue is None:
                outs[0][...] = scr[...]
            else:
                epilogue(scr[...], outs, i, ni, *ex, *es)

        if n_xc:
            pl.when((i == ni - 1) & (kk == nk - 1))(finish)

    in_specs = [pl.BlockSpec((tb, tk), lambda i, kk: (i, kk)), pl.BlockSpec((tk, n), lambda i, kk: (kk, 0))]
    args = [a, w]
    if has_acc:
        in_specs.append(pl.BlockSpec((tb, n), lambda i, kk: (i, 0)))
        args.append(acc_in)
    in_specs += list(extra_specs) + [ANY] * n_xc
    args += list(extras) + (list(comm.srcs) if comm else [])
    if epilogue is None:
        out_shape = [jax.ShapeDtypeStruct((m, n), F32)]
        out_specs = [pl.BlockSpec((tb, n), lambda i, kk: (i, 0))]
    else:
        out_shape, out_specs = list(extra_out_shapes), list(extra_out_specs)
    scratch = [pltpu.VMEM((tb, n), F32)] + list(extra_scratch)
    if n_xc:
        out_shape += list(comm.out_shapes)
        out_specs += [ANY] * n_xc
        scratch += _comm_sems(comm)
    return pl.pallas_call(
        body, name=name, grid=(ni, nk), in_specs=in_specs, out_specs=out_specs, out_shape=out_shape,
        scratch_shapes=scratch, compiler_params=_cparams(2),
    )(*args)


def _rms_bwd_epilogue(du, outs, i, ni, h_ref, g_ref, dh1_ref, obuf, sems):
    dx_ref, dmeta_ref, dg_ref = outs
    h = h_ref[...]
    r = lax.rsqrt(jnp.mean(h * h, axis=-1, keepdims=True) + EPS)
    xh = h * r
    dxh = du * g_ref[...]
    dh0 = dh1_ref[...] + r * (dxh - xh * jnp.mean(dxh * xh, axis=-1, keepdims=True))

    def put(slot, tile):
        return pltpu.make_async_copy(obuf.at[slot], dx_ref.at[pl.ds(pl.multiple_of(tile * TB - T0, 8), TB)],
                                     sems.at[slot])

    @pl.when(i == 0)
    def _():
        dg_ref[...] = jnp.zeros_like(dg_ref)
        dmeta_ref[...] = dh0[PADF:T0, :]
        obuf[0] = dh0
        first = pltpu.make_async_copy(obuf.at[0, pl.ds(T0, TB - T0)], dx_ref.at[pl.ds(0, TB - T0)], sems.at[0])
        first.start()
        first.wait()

    @pl.when(i >= 1)
    def _():
        slot = i % 2

        @pl.when(i >= 3)
        def _():
            put(slot, i - 2).wait()

        obuf[slot] = dh0
        put(slot, i).start()

    dg_ref[...] += jnp.sum(du * xh, axis=0, keepdims=True)

    @pl.when(i == ni - 1)
    def _():
        for tile in (ni - 2, ni - 1):
            if tile >= 1:
                put(tile % 2, tile).wait()


def _mm_tn(name, a, b, bn, ncols=None, bcol0=0, into=None, col0=0, out_cols=None):
    t, m = a.shape
    n = ncols or b.shape[1]
    j0, bj0 = col0 // bn, bcol0 // bn

    def body(a_ref, b_ref, *rest):
        o_ref = rest[-1]

        @pl.when(pl.program_id(1) == 0)
        def _():
            o_ref[...] = jnp.zeros_like(o_ref)

        o_ref[...] += _dot(a_ref[...], b_ref[...], TN)

    in_specs = [pl.BlockSpec((TK, m), lambda j, kk: (kk, 0)), pl.BlockSpec((TK, bn), lambda j, kk: (kk, bj0 + j))]
    args = [a, b]
    aliases = {}
    if into is not None:
        in_specs.append(ANY)
        args.append(into)
        aliases = {2: 0}
        out_cols = into.shape[1]
    return pl.pallas_call(
        body, name=name, grid=(n // bn, t // TK), in_specs=in_specs,
        out_specs=pl.BlockSpec((m, bn), lambda j, kk: (0, j0 + j)),
        out_shape=jax.ShapeDtypeStruct((m, out_cols or n), F32), input_output_aliases=aliases,
        compiler_params=_cparams(2),
    )(*args)


def _assemble_call(g_in, groups):
    nblk, s, h = g_in.shape
    rows, slots = (nblk // 2) * s, 4
    ends = [(k + 1) * s // 16 * 16 for k in range(nblk // 2 - 1)] + [rows]
    pieces = [[(gi, max(a, lo) - lo, max(a, lo), min(b, hi) - max(a, lo)) for gi, (lo, hi, _) in enumerate(groups)
               if max(a, lo) < min(b, hi)] for a, b in zip([0] + ends[:-1], ends)]
    pads = [(gi, hi - lo, padded - (hi - lo)) for gi, (lo, hi, padded) in enumerate(groups) if padded > hi - lo]
    zrows = max([n for _, _, n in pads], default=16)
    n_out = sum(len(p) for p in pieces) + len(pads)

    def body(g_ref, *refs):
        o_refs, (ibuf, obuf, zbuf, isem, osem) = refs[:len(groups)], refs[len(groups):]

        def fetch(b):
            return pltpu.make_async_copy(g_ref.at[b], ibuf.at[b % slots], isem.at[b % slots])

        o32 = obuf.bitcast(jnp.uint32)
        for b in range(slots):
            fetch(b).start()
        zbuf[...] = jnp.zeros(zbuf.shape, zbuf.dtype)
        puts = [pltpu.make_async_copy(zbuf.at[pl.ds(0, n)], o_refs[gi].at[pl.ds(at, n)], osem.at[j])
                for j, (gi, at, n) in enumerate(pads)]
        for cp in puts:
            cp.start()
        for b in range(nblk):
            k, c = divmod(b, 2)
            fetch(b).wait()
            o32[pl.ds(k * s // 2, s // 2), pl.ds(c * h, h)] = ibuf.at[b % slots].bitcast(jnp.uint32)[...]
            if b + slots < nblk:
                fetch(b + slots).start()
            if c == 1:
                for gi, at, src, n in pieces[k]:
                    puts.append(pltpu.make_async_copy(obuf.at[pl.ds(src, n)], o_refs[gi].at[pl.ds(at, n)],
                                                      osem.at[len(puts)]))
                    puts[-1].start()
        for cp in puts:
            cp.wait()

    return pl.pallas_call(
        body, name="assemble_w_in", in_specs=[pl.BlockSpec(memory_space=pl.ANY)],
        out_specs=[pl.BlockSpec(memory_space=pl.ANY)] * len(groups),
        out_shape=[jax.ShapeDtypeStruct((padded, 2 * h), g_in.dtype) for _, _, padded in groups],
        scratch_shapes=[pltpu.VMEM((slots, s, h), g_in.dtype), pltpu.VMEM((rows, 2 * h), g_in.dtype),
                        pltpu.VMEM((zrows, 2 * h), g_in.dtype),
                        pltpu.SemaphoreType.DMA((slots,)), pltpu.SemaphoreType.DMA((n_out,))],
        compiler_params=pltpu.CompilerParams(vmem_limit_bytes=VMEM_LIMIT),
    )(g_in)


def _place_merge_cols_call(dwp, dw_m, dw_glr):
    c0 = W_R + W_GP - 128
    tail = IN_PAD - c0
    rows = 256

    def body(m_ref, low, p_ref, o_ref, buf, sem):
        for r in range(0, D_MODEL, rows):
            buf[r:r + rows, :] = jnp.concatenate(
                [low[r:r + rows, :GATE_RANK], m_ref[r:r + rows, :],
                 jnp.zeros((rows, tail - GATE_RANK - W_M), F32)], axis=1)
        put = pltpu.make_async_copy(buf, o_ref.at[:, pl.ds(c0, tail)], sem)
        put.start()
        put.wait()

    return pl.pallas_call(
        body, name="place_merge_cols",
        in_specs=[pl.BlockSpec(memory_space=pltpu.VMEM), pl.BlockSpec(memory_space=pltpu.VMEM), ANY], out_specs=ANY,
        out_shape=jax.ShapeDtypeStruct(dwp.shape, F32), input_output_aliases={2: 0},
        scratch_shapes=[pltpu.VMEM((D_MODEL, tail), F32), pltpu.SemaphoreType.DMA],
        compiler_params=pltpu.CompilerParams(vmem_limit_bytes=VMEM_LIMIT),
    )(dw_m, dw_glr, dwp)


def _ret_fill_decay(lg_ref, dm_scr):
    c = TM
    ii = lax.broadcasted_iota(jnp.int32, (c, c), 0)
    jj = lax.broadcasted_iota(jnp.int32, (c, c), 1)
    rel = (ii - jj).astype(F32)
    for h in range(RET_HEADS):
        dm_scr[h] = jnp.where(rel >= 0, jnp.exp(jnp.maximum(rel, 0.0) * lg_ref[h]), 0.0)


def _ret_consts(lg, dm_ref):
    c = TM
    idx = lax.broadcasted_iota(jnp.int32, (c, 1), 0).astype(F32)
    xi = jnp.exp((idx + 1.0) * lg)
    zeta = jnp.exp((c - 1.0 - idx) * lg)
    gc = jnp.exp(jnp.full((1, 1), c, F32) * lg)
    return dm_ref[...], xi, zeta, gc


def _ret_fwd_call(rqk, rv, rg, gain, lgam):
    tp = rqk.shape[0]
    nc = tp // TM

    def body(lg_ref, qk_ref, v_ref, rg_ref, g_ref, o_ref, a_ref, st_ref, sc_ref, s_scr, dm_scr):
        @pl.when(pl.program_id(0) == 0)
        def _():
            s_scr[...] = jnp.zeros_like(s_scr)
            _ret_fill_decay(lg_ref, dm_scr)

        for h in range(RET_HEADS):
            dm, xi, zeta, gc = _ret_consts(lg_ref[h], dm_scr.at[h])
            q = qk_ref[:, h * RET_QK:(h + 1) * RET_QK]
            k = qk_ref[:, D_MODEL + h * RET_QK:D_MODEL + (h + 1) * RET_QK]
            v = v_ref[:, h * RET_V:(h + 1) * RET_V]
            sb = s_scr[h].astype(BF16)
            st_ref[0, h] = sb
            s = (_dot(q, k, NT) * dm).astype(BF16)
            sc_ref[0, h] = s
            o = _dot(s, v, NN) + xi * _dot(q, sb, NN)
            kz = (k.astype(F32) * zeta).astype(BF16)
            s_scr[h] = gc * s_scr[h] + _dot(kz, v, TN)
            o_ref[:, h * RET_V:(h + 1) * RET_V] = o
            mu = _head_mean(o)
            xc = o - mu
            xh = xc * lax.rsqrt(_head_mean(xc * xc) + EPS)
            a_ref[:, h * RET_V:(h + 1) * RET_V] = (
                xh * g_ref[:, h * RET_V:(h + 1) * RET_V] * _silu(rg_ref[:, h * RET_V:(h + 1) * RET_V])).astype(BF16)

    return pl.pallas_call(
        body, name="ret_fwd", grid=(nc,),
        in_specs=[pl.BlockSpec(memory_space=pltpu.SMEM),
                  pl.BlockSpec((TM, 2 * D_MODEL), lambda n: (n, 0)),
                  pl.BlockSpec((TM, RET_W), lambda n: (n, 0)),
                  pl.BlockSpec((TM, RET_W), lambda n: (n, 0)),
                  pl.BlockSpec((1, RET_W), lambda n: (0, 0))],
        out_specs=[pl.BlockSpec((TM, RET_W), lambda n: (n, 0)),
                   pl.BlockSpec((TM, RET_W), lambda n: (n, 0)),
                   pl.BlockSpec((1, RET_HEADS, RET_QK, RET_V), lambda n: (n, 0, 0, 0)),
                   pl.BlockSpec((1, RET_HEADS, TM, TM), lambda n: (n, 0, 0, 0))],
        out_shape=[jax.ShapeDtypeStruct((tp, RET_W), F32), jax.ShapeDtypeStruct((tp, RET_W), BF16),
                   jax.ShapeDtypeStruct((nc, RET_HEADS, RET_QK, RET_V), BF16),
                   jax.ShapeDtypeStruct((nc, RET_HEADS, TM, TM), BF16)],
        scratch_shapes=[pltpu.VMEM((RET_HEADS, RET_QK, RET_V), F32), pltpu.VMEM((RET_HEADS, TM, TM), F32)],
        compiler_params=_cparams(1),
    )(lgam, rqk, rv, rg, gain)


def _ret_bwd_call(rqk, rv, rg, o_ret, dpr, wbr, states, scores, gain, lgam, cos, sin):
    tp = rqk.shape[0]
    nc = tp // TM
    half = RET_QK // 2

    def body(lg_ref, qk_ref, v_ref, rg_ref, o_ref, dpr_ref, wbr_ref, st_ref, sc_ref, g_ref, cos_ref, sin_ref, dp_ref,
             dg_ref, ds_scr, dm_scr):
        @pl.when(pl.program_id(0) == 0)
        def _():
            ds_scr[...] = jnp.zeros_like(ds_scr)
            dg_ref[...] = jnp.zeros_like(dg_ref)
            _ret_fill_decay(lg_ref, dm_scr)

        cos, sin = cos_ref[...], sin_ref[...]
        for h in range(RET_HEADS):
            hs = slice(h * RET_V, (h + 1) * RET_V)
            dm, xi, zeta, gc = _ret_consts(lg_ref[h], dm_scr.at[h])
            o = o_ref[:, hs]
            mu = _head_mean(o)
            xc = o - mu
            rstd = lax.rsqrt(_head_mean(xc * xc) + EPS)
            xh = xc * rstd
            gain_h = g_ref[:, hs]
            g = rg_ref[:, hs]
            sg = _sigmoid(g)
            silu = g * sg
            dah = _dot(dpr_ref[...], wbr_ref[hs, :], NT)
            dp_ref[:, 4 * D_MODEL + h * RET_V:4 * D_MODEL + (h + 1) * RET_V] = (
                dah * (xh * gain_h) * (sg * (1.0 + g * (1.0 - sg)))).astype(BF16)
            dn = dah * silu
            dg_ref[:, hs] += jnp.sum(dn * xh, axis=0, keepdims=True)
            dxh = dn * gain_h
            do = rstd * (dxh - _head_mean(dxh) - xh * _head_mean(dxh * xh))
            dob = do.astype(BF16)
            q = qk_ref[:, h * RET_QK:(h + 1) * RET_QK]
            k = qk_ref[:, D_MODEL + h * RET_QK:D_MODEL + (h + 1) * RET_QK]
            v = v_ref[:, hs]
            sp = st_ref[0, h]
            ds = ds_scr[h]
            dsb = ds.astype(BF16)
            s = sc_ref[0, h]
            dsc = (_dot(dob, v, NT) * dm).astype(BF16)
            dq = _dot(dsc, k, NN) + xi * _dot(dob, sp, NT)
            dk = _dot(dsc, q, TN) + zeta * _dot(v, dsb, NT)
            kz = (k.astype(F32) * zeta).astype(BF16)
            dv = _dot(s, dob, TN) + _dot(kz, dsb, NN)
            qx = (q.astype(F32) * xi).astype(BF16)
            ds_scr[h] = gc * ds + _dot(qx, dob, TN)
            dp_ref[:, 2 * D_MODEL + h * RET_V:2 * D_MODEL + (h + 1) * RET_V] = dv.astype(BF16)
            dk = dk * (RET_QK ** -0.5)
            for base, t in ((0, dq), (D_MODEL, dk)):
                t1, t2 = t[:, :half], t[:, half:]
                dp_ref[:, base + h * RET_QK:base + h * RET_QK + half] = (t1 * cos + t2 * sin).astype(BF16)
                dp_ref[:, base + h * RET_QK + half:base + (h + 1) * RET_QK] = (t2 * cos - t1 * sin).astype(BF16)

    rev = lambda n: (nc - 1 - n, 0)
    return pl.pallas_call(
        body, name="ret_bwd", grid=(nc,),
        in_specs=[pl.BlockSpec(memory_space=pltpu.SMEM),
                  pl.BlockSpec((TM, 2 * D_MODEL), rev),
                  pl.BlockSpec((TM, RET_W), rev),
                  pl.BlockSpec((TM, RET_W), rev),
                  pl.BlockSpec((TM, RET_W), rev),
                  pl.BlockSpec((TM, D_MODEL), rev),
                  pl.BlockSpec((RET_W, D_MODEL), lambda n: (0, 0)),
                  pl.BlockSpec((1, RET_HEADS, RET_QK, RET_V), lambda n: (nc - 1 - n, 0, 0, 0)),
                  pl.BlockSpec((1, RET_HEADS, TM, TM), lambda n: (nc - 1 - n, 0, 0, 0)),
                  pl.BlockSpec((1, RET_W), lambda n: (0, 0)),
                  pl.BlockSpec((TM, half), rev),
                  pl.BlockSpec((TM, half), rev)],
        out_specs=[pl.BlockSpec((TM, W_R), rev), pl.BlockSpec((1, RET_W), lambda n: (0, 0))],
        out_shape=[jax.ShapeDtypeStruct((tp, W_R), BF16), jax.ShapeDtypeStruct((1, RET_W), F32)],
        scratch_shapes=[pltpu.VMEM((RET_HEADS, RET_QK, RET_V), F32), pltpu.VMEM((RET_HEADS, TM, TM), F32)],
        compiler_params=_cparams(1),
    )(lgam, rqk, rv, rg, o_ret, dpr, wbr, states, scores, gain, cos, sin)


GLA_LEVELS = tuple(GC >> (s + 1) for s in range(int(math.log2(GC // GLA_SUB))))
NLEV = len(GLA_LEVELS)


def _gla_tril():
    return np.tril(np.ones((GC, GC), np.float32))


def _gla_masks():
    ii = lax.broadcasted_iota(jnp.int32, (GC, GC), 0)
    jj = lax.broadcasted_iota(jnp.int32, (GC, GC), 1)
    masks = []
    for m in GLA_LEVELS:
        sh = int(math.log2(2 * m))
        masks.append(((ii >> sh) == (jj >> sh)) & ((ii & m) != 0) & ((jj & m) == 0))
    sh = int(math.log2(GLA_SUB))
    md = ((ii >> sh) == (jj >> sh)) & (jj <= ii)
    row = lax.broadcasted_iota(jnp.int32, (GC, 1), 0)
    second = [(row & m) != 0 for m in GLA_LEVELS]
    return masks, md, second


def _gla_gate_call(u, w_g, wg, bg, pmat):
    tp = u.shape[0]
    gb = _proj_rows(tp)
    assert gb % GC == 0

    def body(u_ref, w_ref, wg_ref, bg_ref, p_ref, glr_ref, z_ref, b_ref):
        glr = _dot(u_ref[...], w_ref[...], NT)
        glr_ref[...] = glr
        z = _dot(glr.astype(BF16), wg_ref[...], NN) + bg_ref[...]
        z_ref[...] = z
        la = (jnp.minimum(z, 0.0) - jnp.log1p(jnp.exp(-jnp.abs(z)))) * (1.0 / GATE_TAU)
        for r in range(0, gb, GC):
            b_ref[r:r + GC, :] = _exact_pm(p_ref[...], la[r:r + GC, :])

    tile = pl.BlockSpec((gb, GLA_KW), lambda i: (i, 0))
    return pl.pallas_call(
        body, name="gla_gate", grid=(tp // gb,),
        in_specs=[pl.BlockSpec((gb, D_MODEL), lambda i: (i, 0)),
                  pl.BlockSpec((128, D_MODEL), lambda i: ((W_GP - 128) // 128, 0)),
                  pl.BlockSpec((128, GLA_KW), lambda i: (0, 0)),
                  pl.BlockSpec((1, GLA_KW), lambda i: (0, 0)), pl.BlockSpec((GC, GC), lambda i: (0, 0))],
        out_specs=[pl.BlockSpec((gb, 128), lambda i: (i, 0)), tile, tile],
        out_shape=[jax.ShapeDtypeStruct((tp, 128), F32), jax.ShapeDtypeStruct((tp, GLA_KW), F32),
                   jax.ShapeDtypeStruct((tp, GLA_KW), F32)],
        compiler_params=_cparams(1),
    )(u, w_g, wg, bg, pmat)


def _gla_gate_bwd_call(db, z, glr, u, wg, pmat_t, d_g):
    tp = db.shape[0]
    gb = _proj_rows(tp)
    assert gb % GC == 0 and (W_GP - 128) % 128 == 0

    def body(db_ref, z_ref, glr_ref, u_ref, wg_ref, pt_ref, dgin_ref, dg_ref, dwg_ref, dbg_ref, dwl_ref):
        i = pl.program_id(0)

        @pl.when(i == 0)
        def _():
            dwg_ref[...] = jnp.zeros_like(dwg_ref)
            dbg_ref[...] = jnp.zeros_like(dbg_ref)
            dwl_ref[...] = jnp.zeros_like(dwl_ref)

        dla = jnp.concatenate([_exact_pm(pt_ref[...], db_ref[r:r + GC, :]) for r in range(0, gb, GC)], axis=0)
        row = i * gb + lax.broadcasted_iota(jnp.int32, (gb, 1), 0)
        dz = jnp.where(row >= PADF, dla * (1.0 / GATE_TAU) * _sigmoid(-z_ref[...]), 0.0)
        dzb = dz.astype(BF16)
        dglr = _dot(dzb, wg_ref[...], NT).astype(BF16)
        dg_ref[...] = dglr
        dwg_ref[...] += _dot(glr_ref[...].astype(BF16), dzb, TN)
        dbg_ref[...] += jnp.sum(dz, axis=0, keepdims=True)
        dwl_ref[...] += _dot(u_ref[...], dglr, TN)

    tile = pl.BlockSpec((gb, GLA_KW), lambda i: (i, 0))
    const = lambda i: (0, 0)
    return pl.pallas_call(
        body, name="gla_gate_bwd", grid=(tp // gb,),
        in_specs=[tile, tile, pl.BlockSpec((gb, 128), lambda i: (i, 0)), pl.BlockSpec((gb, D_MODEL), lambda i: (i, 0)),
                  pl.BlockSpec((128, GLA_KW), const), pl.BlockSpec((GC, GC), const), ANY],
        out_specs=[pl.BlockSpec((gb, 128), lambda i: (i, (W_GP - 128) // 128)), pl.BlockSpec((128, GLA_KW), const),
                   pl.BlockSpec((1, GLA_KW), const), pl.BlockSpec((D_MODEL, 128), const)],
        out_shape=[jax.ShapeDtypeStruct(d_g.shape, BF16), jax.ShapeDtypeStruct((128, GLA_KW), F32),
                   jax.ShapeDtypeStruct((1, GLA_KW), F32), jax.ShapeDtypeStruct((D_MODEL, 128), F32)],
        input_output_aliases={6: 0}, compiler_params=_cparams(1),
    )(db, z, glr, u, wg, pmat_t, d_g)


def _gla_row_steps(b_ref, cs, rows, size):
    parts = [jnp.zeros((size, GLA_K), F32) if r is None else jnp.broadcast_to(b_ref[r:r + 1, cs], (size, GLA_K))
             for r in rows]
    return parts[0] if len(parts) == 1 else jnp.concatenate(parts, axis=0)


def _gla_factors(b_ref, h, second):
    cs = slice(h * GLA_K, (h + 1) * GLA_K)
    b = b_ref[:, cs]
    fq, fk = [], []
    for l, m in enumerate(GLA_LEVELS):
        d = b - _gla_row_steps(b_ref, cs, [s + m - 1 for s in range(0, GC, 2 * m)], 2 * m)
        f = jnp.exp(jnp.where(second[l], d, -d))
        fq.append(jnp.where(second[l], f, 0.0))
        fk.append(jnp.where(second[l], 0.0, f))
    dd = b - _gla_row_steps(b_ref, cs, [None] + [s - 1 for s in range(GLA_SUB, GC, GLA_SUB)], GLA_SUB)
    ed = jnp.exp(dd)
    edi = jnp.exp(-dd)
    eb = jnp.exp(b)
    bl = b_ref[GC - 1:GC, cs]
    ee = jnp.exp(bl - b)
    ebl = jnp.exp(bl)
    return fq, fk, ed, edi, eb, ee, ebl


def _gla_scaled(q, k, fq, fk, ed, edi):
    qt = [(q * f).astype(BF16) for f in fq]
    kt = [(k * f).astype(BF16) for f in fk]
    return qt, kt, (q * ed).astype(BF16), (k * edi).astype(BF16)


def _gla_scores(qt, kt, qd, kd, masks, md):
    a = jnp.where(md, _dot(qd, kd, NT), 0.0)
    for l in range(NLEV):
        a = a + jnp.where(masks[l], _dot(qt[l], kt[l], NT), 0.0)
    return a.astype(BF16)


def _gla_fwd_call(gqk, gv, b, gg, gain, comm=None):
    tp = gqk.shape[0]
    nc = tp // GC
    ns = nc // GS
    n_xc = len(comm.srcs) if comm else 0

    def body(qk_ref, v_ref, b_ref, gg_ref, g_ref, *rest):
        xc_src = rest[:n_xc]
        o_ref, a_ref, st_ref, am_ref = rest[n_xc:n_xc + 4]
        xc_dst = rest[n_xc + 4:2 * n_xc + 4]
        s_scr = rest[2 * n_xc + 4]
        n = pl.program_id(0)
        if n_xc:
            begin, finish = comm.make(xc_src, xc_dst, rest[-2], rest[-1])
            pl.when(n == 0)(begin)
            pl.when(n == ns - 1)(finish)

        @pl.when(n == 0)
        def _():
            s_scr[...] = jnp.zeros_like(s_scr)

        masks, md, second = _gla_masks()
        for cc in range(GS):
            rows = pl.ds(cc * GC, GC)
            qk_c, v_c, b_c, gg_c, o_c, a_c = (r.at[rows] for r in (qk_ref, v_ref, b_ref, gg_ref, o_ref, a_ref))
            for h in range(GLA_HEADS):
                q = qk_c[:, h * GLA_K:(h + 1) * GLA_K]
                k = qk_c[:, GLA_KW + h * GLA_K:GLA_KW + (h + 1) * GLA_K]
                vs = slice(h * GLA_V, (h + 1) * GLA_V)
                v = v_c[:, vs]
                fq, fk, ed, edi, eb, ee, ebl = _gla_factors(b_c, h, second)
                a = _gla_scores(*_gla_scaled(q, k, fq, fk, ed, edi), masks, md)
                am_ref[cc, h] = a
                sb = s_scr[h].astype(BF16)
                st_ref[cc, h] = sb
                o = _dot(a, v, NN) + _dot((q * eb).astype(BF16), sb, NT)
                s_scr[h] = s_scr[h] * ebl + _dot(v, (k * ee).astype(BF16), TN)
                o_c[:, vs] = o
                xh = o * lax.rsqrt(_head_mean(o * o) + EPS)
                a_c[:, vs] = (xh * g_ref[:, vs] * _silu(gg_c[:, vs])).astype(BF16)

    return pl.pallas_call(
        body, name="gla_fwd", grid=(ns,),
        in_specs=[pl.BlockSpec((GS * GC, 2 * GLA_KW), lambda n: (n, 0)),
                  pl.BlockSpec((GS * GC, GLA_W), lambda n: (n, 0)),
                  pl.BlockSpec((GS * GC, GLA_KW), lambda n: (n, 0)),
                  pl.BlockSpec((GS * GC, GLA_W), lambda n: (n, 0)),
                  pl.BlockSpec((1, GLA_W), lambda n: (0, 0))] + [ANY] * n_xc,
        out_specs=[pl.BlockSpec((GS * GC, GLA_W), lambda n: (n, 0)),
                   pl.BlockSpec((GS * GC, GLA_W), lambda n: (n, 0)),
                   pl.BlockSpec((GS, GLA_HEADS, GLA_V, GLA_K), lambda n: (n, 0, 0, 0)),
                   pl.BlockSpec((GS, GLA_HEADS, GC, GC), lambda n: (n, 0, 0, 0))] + [ANY] * n_xc,
        out_shape=[jax.ShapeDtypeStruct((tp, GLA_W), F32), jax.ShapeDtypeStruct((tp, GLA_W), BF16),
                   jax.ShapeDtypeStruct((nc, GLA_HEADS, GLA_V, GLA_K), BF16),
                   jax.ShapeDtypeStruct((nc, GLA_HEADS, GC, GC), BF16)] + (list(comm.out_shapes) if comm else []),
        scratch_shapes=[pltpu.VMEM((GLA_HEADS, GLA_V, GLA_K), F32)] + (_comm_sems(comm) if comm else []),
        compiler_params=_cparams(1),
    )(gqk, gv, b, gg, gain, *(comm.srcs if comm else ()))


def _gla_bwd_call(gqk, gv, b, gg, o_gla, da, states, scores, gain, comm=None):
    tp = gqk.shape[0]
    nc = tp // GC
    ns = nc // GS
    o_gv, o_gg = 2 * GLA_KW, 2 * GLA_KW + GLA_W
    n_xc = len(comm.srcs) if comm else 0

    def body(qk_all, v_all, b_all, gg_all, o_all, da_all, st_ref, am_ref, g_ref, *rest):
        xc_src = rest[:n_xc]
        dp_all, db_all, dg_ref = rest[n_xc:n_xc + 3]
        xc_dst = rest[n_xc + 3:2 * n_xc + 3]
        ds_scr = rest[2 * n_xc + 3]
        n = pl.program_id(0)
        if n_xc:
            begin, finish = comm.make(xc_src, xc_dst, rest[-2], rest[-1])
            pl.when(n == 0)(begin)
            pl.when(n == ns - 1)(finish)

        @pl.when(n == 0)
        def _():
            ds_scr[...] = jnp.zeros_like(ds_scr)
            dg_ref[...] = jnp.zeros_like(dg_ref)

        masks, md, second = _gla_masks()
        for cc, h in [(cc, h) for cc in reversed(range(GS)) for h in range(GLA_HEADS)]:
            rows = pl.ds(cc * GC, GC)
            qk_ref, v_ref, b_scr, gg_ref, o_ref, da_ref, dp_ref, db_scr = (
                r.at[rows] for r in (qk_all, v_all, b_all, gg_all, o_all, da_all, dp_all, db_all))
            cs = slice(h * GLA_K, (h + 1) * GLA_K)
            vs = slice(h * GLA_V, (h + 1) * GLA_V)
            o = o_ref[:, vs]
            rstd = lax.rsqrt(_head_mean(o * o) + EPS)
            xh = o * rstd
            gain_h = g_ref[:, vs]
            g = gg_ref[:, vs]
            sg = _sigmoid(g)
            dah = da_ref[:, vs]
            dp_ref[:, o_gg + h * GLA_V:o_gg + (h + 1) * GLA_V] = (
                dah * (xh * gain_h) * (sg * (1.0 + g * (1.0 - sg)))).astype(BF16)
            dn = dah * (g * sg)
            dg_ref[:, vs] += jnp.sum(dn * xh, axis=0, keepdims=True)
            dxh = dn * gain_h
            do = rstd * (dxh - xh * _head_mean(dxh * xh))
            dob = do.astype(BF16)
            q = qk_ref[:, cs]
            k = qk_ref[:, GLA_KW + h * GLA_K:GLA_KW + (h + 1) * GLA_K]
            v = v_ref[:, vs]
            fq, fk, ed, edi, eb, ee, ebl = _gla_factors(b_scr, h, second)
            qt, kt, qd, kd = _gla_scaled(q, k, fq, fk, ed, edi)
            sp = st_ref[cc, h]
            ds = ds_scr[h]
            dsb = ds.astype(BF16)
            q_in = q * eb
            k_end = k * ee
            da_s = _dot(dob, v, NT)
            dv = _dot(am_ref[cc, h], dob, TN) + _dot(k_end.astype(BF16), dsb, NT)
            dq_in = _dot(dob, sp, NN)
            dk_end = _dot(v, dsb, NN)
            dbl = jnp.sum(sp.astype(F32) * ds, axis=0, keepdims=True) * ebl
            ds_scr[h] = ds * ebl + _dot(dob, q_in.astype(BF16), TN)
            dq = dq_in * eb
            dk = dk_end * ee
            de_end = dk_end * k_end
            db = dq_in * q_in - de_end
            placed = [(GC - 1, jnp.sum(de_end, axis=0, keepdims=True) + dbl)]
            for l, m in enumerate(GLA_LEVELS):
                dal = jnp.where(masks[l], da_s, 0.0).astype(BF16)
                dqt = _dot(dal, kt[l], NN)
                dkt = _dot(dal, qt[l], TN)
                dq = dq + dqt * fq[l]
                dk = dk + dkt * fk[l]
                gl = dqt * (q * fq[l]) - dkt * (k * fk[l])
                db = db + gl
                placed += [(s + m - 1, -jnp.sum(gl[s:s + 2 * m], axis=0, keepdims=True)) for s in range(0, GC, 2 * m)]
            dad = jnp.where(md, da_s, 0.0).astype(BF16)
            dqd = _dot(dad, kd, NN)
            dkd = _dot(dad, qd, TN)
            dq = dq + dqd * ed
            dk = dk + dkd * edi
            gd = dqd * (q * ed) - dkd * (k * edi)
            db = db + gd
            placed += [(s - 1, -jnp.sum(gd[s:s + GLA_SUB], axis=0, keepdims=True)) for s in range(GLA_SUB, GC, GLA_SUB)]
            db_scr[:, cs] = db
            for r, val in placed:
                db_scr[r:r + 1, cs] += val
            dp_ref[:, cs] = (dq * (GLA_K ** -0.5)).astype(BF16)
            dp_ref[:, GLA_KW + h * GLA_K:GLA_KW + (h + 1) * GLA_K] = dk.astype(BF16)
            dp_ref[:, o_gv + h * GLA_V:o_gv + (h + 1) * GLA_V] = dv.astype(BF16)

    rev = lambda n: (ns - 1 - n, 0)
    const = lambda n: (0, 0)
    xc_shapes, xc_sems = (list(comm.out_shapes), _comm_sems(comm)) if n_xc else ([], [])
    return pl.pallas_call(
        body, name="gla_bwd", grid=(ns,),
        in_specs=[pl.BlockSpec((GS * GC, 2 * GLA_KW), rev),
                  pl.BlockSpec((GS * GC, GLA_W), rev),
                  pl.BlockSpec((GS * GC, GLA_KW), rev),
                  pl.BlockSpec((GS * GC, GLA_W), rev),
                  pl.BlockSpec((GS * GC, GLA_W), rev),
                  pl.BlockSpec((GS * GC, GLA_W), rev),
                  pl.BlockSpec((GS, GLA_HEADS, GLA_V, GLA_K), lambda n: (ns - 1 - n, 0, 0, 0)),
                  pl.BlockSpec((GS, GLA_HEADS, GC, GC), lambda n: (ns - 1 - n, 0, 0, 0)),
                  pl.BlockSpec((1, GLA_W), const)] + [ANY] * n_xc,
        out_specs=[pl.BlockSpec((GS * GC, W_GP), rev), pl.BlockSpec((GS * GC, GLA_KW), rev),
                   pl.BlockSpec((1, GLA_W), const)] + [ANY] * n_xc,
        out_shape=[jax.ShapeDtypeStruct((tp, W_GP), BF16), jax.ShapeDtypeStruct((tp, GLA_KW), F32),
                   jax.ShapeDtypeStruct((1, GLA_W), F32)] + xc_shapes,
        scratch_shapes=[pltpu.VMEM((GLA_HEADS, GLA_V, GLA_K), F32)] + xc_sems,
        compiler_params=_cparams(1),
    )(gqk, gv, b, gg, o_gla, da, states, scores, gain, *(comm.srcs if comm else ()))


def _mid_call(a_ret, a_gla, mg, h0, tgt, wbr, wbg, wout, gf):
    tp = h0.shape[0]
    nt = tp // TM

    def body(ar_ref, ag_ref, mg_ref, h_ref, t_ref, wbr_ref, wbg_ref, wo_ref, gf_ref,
             dh1_ref, dag_ref, dm_ref, mb_ref, dh1b_ref, dprb_ref, dpgb_ref, loss_ref, dgf_ref):
        i = pl.program_id(0)

        @pl.when(i == 0)
        def _():
            loss_ref[...] = jnp.zeros_like(loss_ref)
            dgf_ref[...] = jnp.zeros_like(dgf_ref)

        ar, ag = ar_ref[...], ag_ref[...]
        pr = _dot(ar, wbr_ref[...], NN)
        pg = _dot(ag, wbg_ref[...], NN)
        sr = _sigmoid(mg_ref[:, :D_MODEL])
        sg = _sigmoid(mg_ref[:, D_MODEL:])
        merged = (sr * pr + sg * pg).astype(BF16)
        mb_ref[...] = merged
        h1 = h_ref[...] + _dot(merged, wo_ref[...], NN)
        r1 = lax.rsqrt(jnp.mean(h1 * h1, axis=-1, keepdims=True) + EPS)
        xh = h1 * r1
        gfv = gf_ref[...]
        live = jnp.where(i > 0, 1.0, 0.0).astype(F32)
        err = (xh * gfv - t_ref[...]) * live
        loss_ref[...] += jnp.full(loss_ref.shape, 0.5 / D_MODEL, F32) * jnp.sum(err * err)
        dy = err * (1.0 / D_MODEL)
        dgf_ref[...] += jnp.sum(dy * xh, axis=0, keepdims=True)
        dxh = dy * gfv
        dh1 = r1 * (dxh - xh * jnp.mean(dxh * xh, axis=-1, keepdims=True))
        dh1_ref[...] = dh1
        dh1b = dh1.astype(BF16)
        dh1b_ref[...] = dh1b
        dmerged = _dot(dh1b, wo_ref[...], NT)
        dm_ref[:, :D_MODEL] = (dmerged * pr * sr * (1.0 - sr)).astype(BF16)
        dm_ref[:, D_MODEL:] = (dmerged * pg * sg * (1.0 - sg)).astype(BF16)
        dpr = (dmerged * sr).astype(BF16)
        dpg = (dmerged * sg).astype(BF16)
        dprb_ref[...] = dpr
        dpgb_ref[...] = dpg
        dag_ref[...] = _dot(dpg, wbg_ref[...], NT)

    tile = lambda w: pl.BlockSpec((TM, w), lambda i: (i, 0))
    const = lambda r, w: pl.BlockSpec((r, w), lambda i: (0, 0))
    return pl.pallas_call(
        body, name="merge_out_loss", grid=(nt,),
        in_specs=[tile(RET_W), tile(GLA_W), tile(W_M), tile(D_MODEL),
                  pl.BlockSpec((TM, D_MODEL), lambda i: (jnp.maximum(i - 1, 0), 0)),
                  const(RET_W, D_MODEL), const(GLA_W, D_MODEL), const(D_MODEL, D_MODEL), const(1, D_MODEL)],
        out_specs=[tile(D_MODEL), tile(GLA_W), tile(W_M), tile(D_MODEL), tile(D_MODEL), tile(D_MODEL),
                   tile(D_MODEL), const(1, 128), const(1, D_MODEL)],
        out_shape=[jax.ShapeDtypeStruct((tp, D_MODEL), F32), jax.ShapeDtypeStruct((tp, GLA_W), F32),
                   jax.ShapeDtypeStruct((tp, W_M), BF16),
                   jax.ShapeDtypeStruct((tp, D_MODEL), BF16), jax.ShapeDtypeStruct((tp, D_MODEL), BF16),
                   jax.ShapeDtypeStruct((tp, D_MODEL), BF16), jax.ShapeDtypeStruct((tp, D_MODEL), BF16),
                   jax.ShapeDtypeStruct((1, 128), F32), jax.ShapeDtypeStruct((1, D_MODEL), F32)],
        compiler_params=_cparams(1),
    )(a_ret, a_gla, mg, h0, tgt, wbr, wbg, wout, gf)


def _device_step(x2d, tgt2d, meta, norm_gain, w_in_part, w_gate_up, b_gate, ret_gain, gla_gain, branch_parts,
                 final_gain, ck):
    seq = x2d.shape[0]
    tp = T0 + seq
    head = jnp.concatenate([jnp.zeros((PADF, D_MODEL), F32), meta], axis=0)
    wg_pad = jnp.pad(w_gate_up, ((0, 128 - GATE_RANK), (0, 0))).astype(BF16)

    half = RET_QK // 2
    cos, sin = (jnp.asarray(t) for t in _rope_tables(tp))
    lgam = jnp.log1p(-(2.0 ** (-5.0 - jnp.arange(RET_HEADS, dtype=F32))))
    pmat = jnp.asarray(_gla_tril(), BF16)
    pmat_t = jnp.asarray(_gla_tril().T.copy(), BF16)

    h0, u, g_in = _rms_call(x2d, head, norm_gain, _gather_plan([w_in_part], relay=(True,)))
    sw, hc = w_in_part.shape
    w_r, w_g, w_m = _assemble_call(g_in.reshape(8, sw, hc), (
        (0, W_R, W_R), (W_R, W_R + W_G, W_GP), (W_R + W_G, IN_COLS, IN_COLS - W_R - W_G)))
    tab = pl.BlockSpec((_proj_rows(tp), half), lambda j, i: (i, 0))
    rqk = _mm_nn("proj_rqk", u, w_r, BF16, D_MODEL, 0, 2 * D_MODEL, _rope_epilogue, (cos, sin), (tab, tab))
    rv = _mm_nn("proj_rv", u, w_r, BF16, RET_W, 2 * D_MODEL, RET_W)
    rg = _mm_nn("proj_rg", u, w_r, F32, RET_W, 4 * D_MODEL, RET_W)
    gqk = _mm_nn("proj_gqk", u, w_g, F32, 2 * GLA_KW, 0, 2 * GLA_KW, _gqk_epilogue)
    gv = _mm_nn("proj_gv", u, w_g, BF16, GLA_W, 2 * GLA_KW, GLA_W)
    gg = _mm_nn("proj_gg", u, w_g, F32, GLA_W, 2 * GLA_KW + GLA_W, GLA_W)
    mg = _mm_nn("proj_mg", u, w_m, F32, W_M, 0, W_M)

    o_ret, a_ret, st_ret, sc_ret = _ret_fwd_call(rqk, rv, rg, ret_gain, lgam)
    glr, z_gate, b_dec = _gla_gate_call(u, w_g, wg_pad, b_gate, pmat)
    o_gla, a_gla, st_gla, sc_gla, g_br, g_bg, g_out = _gla_fwd_call(gqk, gv, b_dec, gg, gla_gain,
                                                                    comm=_spread_plan(branch_parts))
    wbr = g_br.reshape(RET_W, D_MODEL)
    wbg = g_bg.reshape(GLA_W, D_MODEL)
    wout = g_out.reshape(D_MODEL, D_MODEL)

    gf = final_gain.reshape(1, D_MODEL)
    (dh1, da_gla, dm, merged_b, dh1_b, dpr_b, dpg_b, loss, dgf) = _mid_call(
        a_ret, a_gla, mg, h0, tgt2d, wbr, wbg, wout, gf)

    names_b = ("w_branch_ret", "w_branch_gla", "w_out")
    g2_b = [_mm_tn("dw_br", a_ret, dpr_b, D_MODEL).reshape(4, 2, RET_W // 8, D_MODEL).transpose(1, 0, 2, 3),
            _mm_tn("dw_bg", a_gla, dpg_b, D_MODEL).reshape(4, 2, GLA_W // 8, D_MODEL).transpose(1, 0, 2, 3),
            _mm_tn("dw_out", merged_b, dh1_b, D_MODEL).reshape(4, 2, D_MODEL // 8, D_MODEL).transpose(1, 0, 2, 3)]
    sib_b = _swap_halves_call("swap_halves_branch", g2_b)
    sum_b = [_add_half_call("add_half_" + nm, g, b, ck) for nm, g, b in zip(names_b, g2_b, sib_b)]
    d_g, db_dec, dgla_gain, *chips_b = _gla_bwd_call(gqk, gv, b_dec, gg, o_gla, da_gla, st_gla, sc_gla, gla_gain,
                                                     comm=_exchange_plan(sum_b))
    d_g, dwg, dbg, dw_glr = _gla_gate_bwd_call(db_dec, z_gate, glr, u, wg_pad, pmat_t, d_g)
    mine = [_add_chips_call("add_chips_" + nm, g, b, p, ck) for nm, g, b, p in zip(names_b, g2_b, sib_b, chips_b)]

    d_r, dret_gain = _ret_bwd_call(rqk, rv, rg, o_ret, dpr_b, wbr, st_ret, sc_ret, ret_gain, lgam, cos, sin)

    dwp = _mm_tn("dw_r", u, d_r, 3 * D_MODEL, out_cols=IN_PAD)
    dwp = _mm_tn("dw_g", u, d_g, 3 * D_MODEL, ncols=W_GP - 128, into=dwp, col0=W_R)
    g2_in = _place_merge_cols_call(dwp, _mm_tn("dw_m", u, dm, 2 * D_MODEL), dw_glr).reshape(2, D_MODEL // 2, IN_PAD)

    du, sib_in = _mm_nt_acc("du_g", d_g, w_g, W_GP, comm=_swap_plan([g2_in]), tb=_proj_rows(tp))
    sum_in = _add_rows_call("add_half_w_in", g2_in, sib_in, ck)
    du, chips_in = _mm_nt_acc("du_r", d_r, w_r, 2 * D_MODEL, acc_in=du, comm=_exchange_window_plan(sum_in),
                              tb=_proj_rows(tp))
    tile = pl.BlockSpec((TB, D_MODEL), lambda i, kk: (i, 0))
    row = pl.BlockSpec((1, D_MODEL), lambda i, kk: (0, 0))
    dx, dmeta, dnorm_gain = _mm_nt_acc(
        "du_m", dm, w_m, W_M, acc_in=du, epilogue=_rms_bwd_epilogue, extras=(h0, norm_gain, dh1),
        extra_specs=(tile, row, tile),
        extra_out_shapes=(jax.ShapeDtypeStruct((seq, D_MODEL), F32), jax.ShapeDtypeStruct((N_META, D_MODEL), F32),
                          jax.ShapeDtypeStruct((1, D_MODEL), F32)),
        extra_out_specs=(ANY, pl.BlockSpec((N_META, D_MODEL), lambda i, kk: (0, 0)), row),
        extra_scratch=(pltpu.VMEM((2, TB, D_MODEL), F32), pltpu.SemaphoreType.DMA((2,))))
    small = dict(norm_gain=dnorm_gain, b_gate=dbg, ret_norm_gain=dret_gain, gla_norm_gain=dgla_gain,
                 final_norm_gain=dgf, w_gate_up=dwg[:GATE_RANK], meta_tokens=dmeta, loss=loss[0, 0])
    rows = -(-sum(sz for _, sz in SMALL) // 128 // 8) * 8
    mine_in, g_small = _add_window_call("add_chips_w_in", g2_in, sib_in, chips_in, ck,
                                        _gather_plan([_pack_rows([small[nm] for nm, _ in SMALL], rows)]))
    full = _join_halves_call("join_halves", [mine_in] + mine)

    return dict(dx=dx, small=g_small, w_in=full[0], w_branch_ret=full[1], w_branch_gla=full[2], w_out=full[3])


MESH = pl.DeviceIdType.MESH
ANY = pl.BlockSpec(memory_space=pl.ANY)


def _place():
    return lax.axis_index("x"), lax.axis_index("y"), lax.axis_index("c")


def _gather8_call(name, parts):
    comm = _gather_plan(parts)
    n = len(parts)

    def body(*refs):
        begin, finish = comm.make(refs[:n], refs[n:2 * n], refs[-2], refs[-1])
        begin()
        finish()

    return pl.pallas_call(
        body, name=name, out_shape=list(comm.out_shapes), in_specs=[ANY] * n, out_specs=[ANY] * n,
        scratch_shapes=_comm_sems(comm),
    )(*parts)


def _swap_halves_call(name, gs):
    n = len(gs)

    def body(*refs):
        g_refs, b_refs = refs[:n], refs[n:2 * n]
        send_sems, recv_sems = refs[2 * n:]
        x, y, c = _place()
        copies = [pltpu.make_async_remote_copy(
            src_ref=g_refs[t].at[1 - c], dst_ref=b_refs[t], send_sem=send_sems.at[t], recv_sem=recv_sems.at[t],
            device_id=(x, y, 1 - c), device_id_type=MESH) for t in range(n)]
        for cp in copies:
            cp.start()
        for cp in copies:
            cp.wait()

    return pl.pallas_call(
        body, name=name,
        out_shape=[jax.ShapeDtypeStruct(g.shape[1:], g.dtype) for g in gs],
        in_specs=[ANY] * n, out_specs=[ANY] * n,
        scratch_shapes=[pltpu.SemaphoreType.DMA((n,)), pltpu.SemaphoreType.DMA((n,))],
    )(*gs)


def _join_halves_call(name, ts):
    n = len(ts)

    def body(*refs):
        o_refs = refs[n:2 * n]
        send_sems, recv_sems = refs[2 * n:]
        x, y, c = _place()
        copies = [pltpu.make_async_remote_copy(
            src_ref=o_refs[t].at[c], dst_ref=o_refs[t].at[c], send_sem=send_sems.at[t], recv_sem=recv_sems.at[t],
            device_id=(x, y, 1 - c), device_id_type=MESH) for t in range(n)]
        for cp in copies:
            cp.start()
        for t in range(n):
            copies[t].wait_send()
            pltpu.make_async_remote_copy(
                src_ref=o_refs[t].at[c], dst_ref=o_refs[t].at[1 - c], send_sem=send_sems.at[t],
                recv_sem=recv_sems.at[t], device_id=(x, y, 1 - c), device_id_type=MESH).wait_recv()

    return pl.pallas_call(
        body, name=name,
        out_shape=[jax.ShapeDtypeStruct(t.shape, t.dtype) for t in ts],
        in_specs=[ANY] * n, out_specs=[ANY] * n, input_output_aliases={t: t for t in range(n)},
        scratch_shapes=[pltpu.SemaphoreType.DMA((n,)), pltpu.SemaphoreType.DMA((n,))],
    )(*ts)


def _row_block(rows, cols, budget):
    best = 8
    for rb in range(8, rows + 1, 8):
        if rows % rb == 0 and rb * cols * 4 <= budget:
            best = rb
    return best


def _add_half_call(name, g, b, ck):
    _, _, r, cc = g.shape
    rb = _row_block(r, cc, 2 * 1024 * 1024)

    def body(ck_ref, g_ref, b_ref, o_ref):
        o_ref[...] = (g_ref[...] + b_ref[...]).astype(BF16)

    return pl.pallas_call(
        body, name=name,
        grid_spec=pltpu.PrefetchScalarGridSpec(
            num_scalar_prefetch=1, grid=(4, r // rb),
            in_specs=[pl.BlockSpec((None, None, rb, cc), lambda k, i, ck_ref: (ck_ref[0], k, i, 0)),
                      pl.BlockSpec((None, rb, cc), lambda k, i, ck_ref: (k, i, 0))],
            out_specs=pl.BlockSpec((None, rb, cc), lambda k, i, ck_ref: (k, i, 0))),
        out_shape=jax.ShapeDtypeStruct(b.shape, BF16),
        compiler_params=_cparams(2),
    )(ck, g, b)


def _add_rows_call(name, g, b, ck):
    _, r, cc = g.shape
    rb = _row_block(r, cc, 2 * 1024 * 1024)

    def body(ck_ref, g_ref, b_ref, o_ref):
        o_ref[...] = (g_ref[...] + b_ref[...]).astype(BF16)

    return pl.pallas_call(
        body, name=name,
        grid_spec=pltpu.PrefetchScalarGridSpec(
            num_scalar_prefetch=1, grid=(r // rb,),
            in_specs=[pl.BlockSpec((None, rb, cc), lambda i, ck_ref: (ck_ref[0], i, 0)),
                      pl.BlockSpec((rb, cc), lambda i, ck_ref: (i, 0))],
            out_specs=pl.BlockSpec((rb, cc), lambda i, ck_ref: (i, 0))),
        out_shape=jax.ShapeDtypeStruct((r, cc), BF16),
        compiler_params=_cparams(1),
    )(ck, g, b)


def _add_window_call(name, g, b, p, ck, comm):
    _, r, _ = g.shape
    nb, step = WIN_W // 128, WIN_STEP // 128
    n_xc = len(comm.srcs)

    def body(ck_ref, g_ref, b_ref, p0_ref, p1_ref, p2_ref, *rest):
        o_ref = rest[n_xc]
        i = pl.program_id(0)
        begin, finish = comm.make(rest[:n_xc], rest[n_xc + 1:2 * n_xc + 1], rest[-2], rest[-1])
        pl.when(i == 0)(begin)
        own = g_ref[...] + b_ref[...]
        o_ref[...] = ((own + p0_ref[...].astype(F32)) + p1_ref[...].astype(F32)) + p2_ref[...].astype(F32)
        pl.when(i == nb - 1)(finish)

    def peer(j):
        return pl.BlockSpec((None, r, 128), lambda i, ck_ref: (j, 0, i))

    return pl.pallas_call(
        body, name=name,
        grid_spec=pltpu.PrefetchScalarGridSpec(
            num_scalar_prefetch=1, grid=(nb,),
            in_specs=[pl.BlockSpec((None, r, 128), lambda i, ck_ref: (ck_ref[0], 0, step * ck_ref[1] + i)),
                      pl.BlockSpec((r, 128), lambda i, ck_ref: (0, step * ck_ref[1] + i)),
                      peer(0), peer(1), peer(2)] + [ANY] * n_xc,
            out_specs=[pl.BlockSpec((None, r, 128), lambda i, ck_ref: (ck_ref[0], 0, i))] + [ANY] * n_xc,
            scratch_shapes=_comm_sems(comm)),
        out_shape=[jax.ShapeDtypeStruct((2, r, WIN_W), F32)] + list(comm.out_shapes),
        compiler_params=_cparams(1),
    )(ck, g, b, p, p, p, *comm.srcs)


def _add_chips_call(name, g, b, p, ck):
    _, _, r, cc = g.shape
    rb = _row_block(r, cc, 2 * 1024 * 1024)

    def body(ck_ref, g_ref, b_ref, p0_ref, p1_ref, p2_ref, o_ref):
        own = g_ref[...] + b_ref[...]
        o_ref[...] = ((own + p0_ref[...].astype(F32)) + p1_ref[...].astype(F32)) + p2_ref[...].astype(F32)

    def peer(j):
        return pl.BlockSpec((None, rb, cc), lambda i, ck_ref: (j, i, 0))

    return pl.pallas_call(
        body, name=name,
        grid_spec=pltpu.PrefetchScalarGridSpec(
            num_scalar_prefetch=1, grid=(r // rb,),
            in_specs=[pl.BlockSpec((None, None, rb, cc), lambda i, ck_ref: (ck_ref[0], ck_ref[1], i, 0)),
                      pl.BlockSpec((None, rb, cc), lambda i, ck_ref: (ck_ref[1], i, 0)),
                      peer(0), peer(1), peer(2)],
            out_specs=pl.BlockSpec((None, rb, cc), lambda i, ck_ref: (ck_ref[0], i, 0))),
        out_shape=jax.ShapeDtypeStruct((2, r, cc), F32),
        compiler_params=_cparams(1),
    )(ck, g, b, p, p, p)


def _sum8_call(name, g):
    def body(g_ref, o_ref):
        acc = g_ref[0]
        for d in range(1, 8):
            acc = acc + g_ref[d]
        o_ref[...] = acc

    return pl.pallas_call(body, name=name, out_shape=jax.ShapeDtypeStruct(g.shape[1:], F32))(g)


def _adamw_call(name, w, g, m, v):
    r, cc = w.shape
    if r % 8 == 0 or r * cc * 4 <= 1024 * 1024:
        rb = _row_block(r, cc, 1024 * 1024) if r % 8 == 0 else r
        grid, spec = (r // rb,), pl.BlockSpec((rb, cc), lambda i: (i, 0))
    else:
        grid, spec = (cc // 128,), pl.BlockSpec((r, 128), lambda i: (0, i))

    def body(w_ref, g_ref, m_ref, v_ref, d_ref, m2_ref, v2_ref):
        _adamw_update(g_ref[...], w_ref, m_ref, v_ref, d_ref, m2_ref, v2_ref)

    return pl.pallas_call(
        body, name=name, grid=grid, in_specs=[spec] * 4, out_specs=[spec] * 3,
        out_shape=[jax.ShapeDtypeStruct((r, cc), F32)] * 3, compiler_params=_cparams(1),
    )(w, g, m, v)


def _adamw_update(gv, w_ref, m_ref, v_ref, d_ref, m2_ref, v2_ref):
    m2 = ADAM_B1 * m_ref[...] + (1.0 - ADAM_B1) * gv
    v2 = ADAM_B2 * v_ref[...] + (1.0 - ADAM_B2) * (gv * gv)
    m_hat = m2 / (1.0 - ADAM_B1 ** ADAM_STEP)
    v_hat = v2 / (1.0 - ADAM_B2 ** ADAM_STEP)
    d_ref[...] = -ADAM_LR * (m_hat / (jnp.sqrt(v_hat) + ADAM_EPS) + ADAM_WD * w_ref[...])
    m2_ref[...] = m2
    v2_ref[...] = v2


def _adamw_window_call(name, w_t, f, m_t, v_t, lane0):
    s, r = w_t.shape
    wl = f.shape[1]

    def body(l0_ref, w_ref, f_ref, m_ref, v_ref, g_ref, d_ref, m2_ref, v2_ref):
        gv = pltpu.roll(f_ref[...], lax.rem(wl - l0_ref[0], wl), axis=1).T[:s]
        g_ref[...] = gv
        _adamw_update(gv, w_ref, m_ref, v_ref, d_ref, m2_ref, v2_ref)

    spec = pl.BlockSpec((s, 128), lambda i, l0: (0, i))
    return pl.pallas_call(
        body, name=name,
        grid_spec=pltpu.PrefetchScalarGridSpec(
            num_scalar_prefetch=1, grid=(r // 128,),
            in_specs=[spec, pl.BlockSpec((128, wl), lambda i, l0: (i, 0)), spec, spec], out_specs=[spec] * 4),
        out_shape=[jax.ShapeDtypeStruct((s, r), F32)] * 4, compiler_params=_cparams(1),
    )(lane0, w_t, f, m_t, v_t)


SMALL = (("norm_gain", D_MODEL), ("b_gate", GLA_KW), ("ret_norm_gain", RET_W), ("gla_norm_gain", GLA_W),
         ("final_norm_gain", D_MODEL), ("w_gate_up", GATE_RANK * GLA_KW), ("meta_tokens", N_META * D_MODEL),
         ("loss", 1))


def _pack_rows(vecs, rows):
    flat = jnp.concatenate([v.reshape(-1) for v in vecs])
    return jnp.pad(flat, (0, rows * 128 - flat.shape[0])).reshape(rows, 128)


def kernel(x, meta_tokens, norm_gain, w_in, w_gate_up, b_gate, ret_norm_gain, gla_norm_gain, w_branch_ret, w_branch_gla, w_out, final_norm_gain, loss_target, m_meta_tokens, m_norm_gain, m_w_in, m_w_gate_up, m_b_gate, m_ret_norm_gain, m_gla_norm_gain, m_w_branch_ret, m_w_branch_gla, m_w_out, m_final_norm_gain, v_meta_tokens, v_norm_gain, v_w_in, v_w_gate_up, v_b_gate, v_ret_norm_gain, v_gla_norm_gain, v_w_branch_ret, v_w_branch_gla, v_w_out, v_final_norm_gain):
    xi, yi, ci = _place()
    kme = 2 * xi + yi
    ck = jnp.stack([ci, kme]).astype(jnp.int32)
    sw_in = w_in.shape[2]

    def my_half(a, dtype):
        r, cc = a.shape
        return lax.dynamic_index_in_dim(a.reshape(2, r // 2, cc), ci, 0, keepdims=False).astype(dtype)

    g_meta, g_wg = _gather8_call("gather_small_weights", [my_half(meta_tokens, F32), my_half(w_gate_up[0], F32)])
    branch_parts = [my_half(w_branch_ret[0], BF16), my_half(w_branch_gla[0], BF16), my_half(w_out[0], BF16)]
    meta = g_meta.reshape(4, 2, N_META // 2, D_MODEL // 4).transpose(1, 2, 0, 3).reshape(N_META, D_MODEL)
    wg_full = g_wg.reshape(4, 2, GATE_RANK // 2, GLA_KW // 4).transpose(1, 2, 0, 3).reshape(GATE_RANK, GLA_KW)

    w_in_part = lax.dynamic_slice_in_dim(w_in[0].T, ci * (D_MODEL // 2), D_MODEL // 2, axis=1).astype(BF16)
    loc = _device_step(x[0], loss_target[0], meta, norm_gain, w_in_part, wg_full, b_gate, ret_norm_gain,
                       gla_norm_gain,
                       branch_parts, final_norm_gain, ck)
    names = ("w_in", "w_branch_ret", "w_branch_gla", "w_out")
    full = [loc[nm] for nm in names]
    big_w = dict(w_in=w_in[0], w_branch_ret=w_branch_ret[0], w_branch_gla=w_branch_gla[0], w_out=w_out[0])
    big_m = dict(w_in=m_w_in[0], w_branch_ret=m_w_branch_ret[0], w_branch_gla=m_w_branch_gla[0], w_out=m_w_out[0])
    big_v = dict(w_in=v_w_in[0], w_branch_ret=v_w_branch_ret[0], w_branch_gla=v_w_branch_gla[0], w_out=v_w_out[0])
    grads, deltas, new_m, new_v = {}, {}, {}, {}
    for nm, f in zip(names, full):
        shape = big_w[nm].shape
        if nm == "w_in":
            lane0 = ((sw_in - WIN_STEP) * kme).astype(jnp.int32).reshape(1)
            m_t, v_t = lax.optimization_barrier(big_m[nm].T), lax.optimization_barrier(big_v[nm].T)
            g, d, m2, v2 = (a.T for a in _adamw_window_call(
                "adamw_" + nm, big_w[nm].T, f.reshape(shape[0], WIN_W), m_t, v_t, lane0))
        else:
            g = f.reshape(shape)
            d, m2, v2 = _adamw_call("adamw_" + nm, big_w[nm], g, big_m[nm], big_v[nm])
        grads[nm], deltas[nm], new_m[nm], new_v[nm] = (a.reshape((1,) + shape) for a in (g, d, m2, v2))

    tot = _sum8_call("sum_small_grads", loc["small"]).reshape(-1)
    off = 0
    sg = {}
    for nm, sz in SMALL:
        sg[nm] = tot[off:off + sz]
        off += sz
    loss = sg.pop("loss")[0]
    sg["w_gate_up"] = lax.dynamic_slice_in_dim(sg["w_gate_up"].reshape(GATE_RANK, GLA_KW), kme * (GLA_KW // 4),
                                               GLA_KW // 4, axis=1)
    sg["meta_tokens"] = lax.dynamic_slice_in_dim(sg["meta_tokens"].reshape(N_META, D_MODEL), kme * (D_MODEL // 4),
                                                 D_MODEL // 4, axis=1)
    small_w = dict(norm_gain=norm_gain, b_gate=b_gate, ret_norm_gain=ret_norm_gain, gla_norm_gain=gla_norm_gain,
                   final_norm_gain=final_norm_gain, w_gate_up=w_gate_up, meta_tokens=meta_tokens)
    small_m = dict(norm_gain=m_norm_gain, b_gate=m_b_gate, ret_norm_gain=m_ret_norm_gain,
                   gla_norm_gain=m_gla_norm_gain, final_norm_gain=m_final_norm_gain, w_gate_up=m_w_gate_up,
                   meta_tokens=m_meta_tokens)
    small_v = dict(norm_gain=v_norm_gain, b_gate=v_b_gate, ret_norm_gain=v_ret_norm_gain,
                   gla_norm_gain=v_gla_norm_gain, final_norm_gain=v_final_norm_gain, w_gate_up=v_w_gate_up,
                   meta_tokens=v_meta_tokens)
    for nm in small_w:
        shape = small_w[nm].shape
        as2d = lambda a: a.reshape((-1, shape[-1]))
        grads[nm] = sg[nm].reshape(shape)
        deltas[nm], new_m[nm], new_v[nm] = (a.reshape(shape) for a in _adamw_call(
            "adamw_" + nm, as2d(small_w[nm]), as2d(sg[nm]), as2d(small_m[nm]), as2d(small_v[nm])))

    out_order = ("meta_tokens", "norm_gain", "w_in", "w_gate_up", "b_gate", "ret_norm_gain", "gla_norm_gain",
                 "w_branch_ret", "w_branch_gla", "w_out", "final_norm_gain")
    dx = loc["dx"].reshape(x.shape)
    return (loss, dx, *[grads[nm] for nm in out_order], *[deltas[nm] for nm in out_order],
            *[new_m[nm] for nm in out_order], *[new_v[nm] for nm in out_order])
```
